```python
import math
import jax, jax.numpy as jnp
from jax import lax
import numpy as np


D_MODEL = 1024
BATCH = 8
SEQ = 4096
DEPTH = 1

D_INNER = 2 * D_MODEL
ATTN_HEADS = 16
ATTN_HEAD_DIM = 64
D_ATTN = ATTN_HEADS * ATTN_HEAD_DIM
DILATED_PATTERNS = ((128, 1), (512, 4), (2048, 16))
ATTN_BLOCK = 128
D_SSM = D_INNER - D_ATTN
SSM_HEAD_DIM = 64
SSM_HEADS = D_SSM // SSM_HEAD_DIM
SSM_GROUPS = 2
D_STATE = 128
CONV_K = 4
CHUNK = 128
D_CONV = D_SSM + 2 * SSM_GROUPS * D_STATE
D_IN_PROJ = 4 * D_ATTN + D_SSM + D_CONV + SSM_HEADS
NORM_EPS = 1e-6
DT_MIN = 1e-3
DT_MAX = 1e-1

kernel_name = 'hymba_dilated_attn_mamba2_hybrid'


def rms_norm(x, w):
    xf = x.astype(jnp.float32)
    y = xf * lax.rsqrt(jnp.mean(xf * xf, axis=-1, keepdims=True) + NORM_EPS)
    return (y * w.astype(jnp.float32)).astype(x.dtype)


def dilated_window_attention(q, k, v, window, dilation):
    b, s, h, dh = q.shape
    n = s // dilation
    nb = -(-n // ATTN_BLOCK)
    n_pad = nb * ATTN_BLOCK
    span = window // dilation

    def to_sub(t):
        t = t.reshape(b, n, dilation, h, dh).transpose(0, 2, 3, 1, 4)
        t = jnp.pad(t, ((0, 0), (0, 0), (0, 0), (0, n_pad - n), (0, 0)))
        return t.reshape(b, dilation, h, nb, ATTN_BLOCK, dh)

    def with_prev(t):
        prev = jnp.pad(t, ((0, 0), (0, 0), (0, 0), (1, 0), (0, 0), (0, 0)))[:, :, :, :-1]
        return jnp.concatenate([prev, t], axis=4)

    qb = to_sub(q)
    kw = with_prev(to_sub(k))
    vw = with_prev(to_sub(v))
    scores = jnp.einsum('bdhnqc,bdhnkc->bdhnqk', qb, kw, preferred_element_type=jnp.float32)

    qi = jnp.arange(ATTN_BLOCK)[:, None]
    kj = jnp.arange(2 * ATTN_BLOCK)[None, :]
    dist = ATTN_BLOCK + qi - kj
    key_idx = jnp.arange(nb)[:, None, None] * ATTN_BLOCK - ATTN_BLOCK + kj[None]
    valid = (dist >= 0) & (dist <= span) & (key_idx >= 0)
    scores = jnp.where(valid, scores, -jnp.inf)

    m = jnp.max(scores, axis=-1, keepdims=True)
    p = jnp.exp(scores - m)
    l = jnp.sum(p, axis=-1, keepdims=True)
    o = jnp.einsum('bdhnqk,bdhnkc->bdhnqc', p, vw.astype(jnp.float32)) / l
    lse = (m + jnp.log(l))[..., 0]

    o = o.reshape(b, dilation, h, n_pad, dh)[:, :, :, :n].transpose(0, 3, 1, 2, 4).reshape(b, s, h, dh)
    lse = lse.reshape(b, dilation, h, n_pad)[:, :, :, :n].transpose(0, 3, 1, 2).reshape(b, s, h)
    return o, lse


def mixture_of_dilations(q, k, v):
    outs, lses = [], []
    for window, dilation in DILATED_PATTERNS:
        o, lse = dilated_window_attention(q, k, v, window, dilation)
        outs.append(o)
        lses.append(lse)
    weights = jax.nn.softmax(jnp.stack(lses), axis=0)
    return jnp.einsum('pbsh,pbshd->bshd', weights, jnp.stack(outs))


def causal_depthwise_conv(x, w, bias):
    c = x.shape[-1]
    y = lax.conv_general_dilated(
        x, w[:, None, :].astype(x.dtype), window_strides=(1,), padding=[(CONV_K - 1, 0)],
        dimension_numbers=('NWC', 'WIO', 'NWC'), feature_group_count=c)
    return y + bias.astype(x.dtype)


def ssd_chunked(x, dt, a, b_mat, c_mat):
    bsz, s, h, p = x.shape
    g = SSM_GROUPS
    e = h // g
    n = b_mat.shape[-1]
    nc = s // CHUNK
    xdt = (x.astype(jnp.float32) * dt[..., None]).reshape(bsz, nc, CHUNK, g, e, p)
    a_dt = (dt * a).reshape(bsz, nc, CHUNK, g, e).transpose(0, 3, 4, 1, 2)
    bc = b_mat.astype(jnp.float32).reshape(bsz, nc, CHUNK, g, n)
    cc = c_mat.astype(jnp.float32).reshape(bsz, nc, CHUNK, g, n)
    a_cs = jnp.cumsum(a_dt, axis=-1)

    causal = jnp.tril(jnp.ones((CHUNK, CHUNK), dtype=bool))
    seg = a_cs[..., :, None] - a_cs[..., None, :]
    decay = jnp.exp(jnp.where(causal, seg, -jnp.inf))
    cb = jnp.einsum('bclgn,bcsgn->bgcls', cc, bc)
    y_diag = jnp.einsum('bgecls,bcsgep->bclgep', cb[:, :, None] * decay, xdt)

    decay_states = jnp.exp(a_cs[..., -1:] - a_cs)
    states = jnp.einsum('bclgn,bgecl,bclgep->bcgepn', bc, decay_states, xdt)

    chunk_decay = jnp.exp(a_cs[..., -1])

    def step(h_prev, inp):
        st, dec = inp
        return h_prev * dec[..., None, None] + st, h_prev

    init = jnp.zeros((bsz, g, e, p, n), jnp.float32)
    _, prev = lax.scan(step, init, (states.transpose(1, 0, 2, 3, 4, 5), chunk_decay.transpose(3, 0, 1, 2)))
    prev = prev.transpose(1, 0, 2, 3, 4, 5)

    y_off = jnp.einsum('bclgn,bcgepn,bgecl->bclgep', cc, prev, jnp.exp(a_cs))
    return (y_diag + y_off).reshape(bsz, s, h, p)


def gated_group_rms_norm(y, z, w):
    yz = y.astype(jnp.float32) * jax.nn.silu(z.astype(jnp.float32))
    shp = yz.shape
    yz = yz.reshape(shp[:-1] + (SSM_GROUPS, shp[-1] // SSM_GROUPS))
    yz = yz * lax.rsqrt(jnp.mean(yz * yz, axis=-1, keepdims=True) + NORM_EPS)
    return yz.reshape(shp) * w.astype(jnp.float32)


def hybrid_layer(hid, norm_pre_w, w_in, conv_w, conv_b, dt_bias, a_log, d_skip, ssm_norm_w, w_out, norm_post_w):
    bsz, s, _ = hid.shape
    u = rms_norm(hid, norm_pre_w)
    proj = jnp.einsum('bsd,de->bse', u, w_in.astype(u.dtype))
    sizes = [D_ATTN, D_ATTN, D_ATTN, D_ATTN, D_SSM, D_CONV]
    q, k, v, g_attn, z, xbc, dt_raw = jnp.split(proj, np.cumsum(sizes).tolist(), axis=-1)

    q = q.reshape(bsz, s, ATTN_HEADS, ATTN_HEAD_DIM) * (ATTN_HEAD_DIM ** -0.5)
    k = k.reshape(bsz, s, ATTN_HEADS, ATTN_HEAD_DIM)
    v = v.reshape(bsz, s, ATTN_HEADS, ATTN_HEAD_DIM)
    attn = mixture_of_dilations(q, k, v).reshape(bsz, s, D_ATTN)
    attn = attn * jax.nn.silu(g_attn.astype(jnp.float32))

    xbc = jax.nn.silu(causal_depthwise_conv(xbc, conv_w, conv_b))
    xs, b_mat, c_mat = jnp.split(xbc, [D_SSM, D_SSM + SSM_GROUPS * D_STATE], axis=-1)
    xs = xs.reshape(bsz, s, SSM_HEADS, SSM_HEAD_DIM)
    b_mat = b_mat.reshape(bsz, s, SSM_GROUPS, D_STATE)
    c_mat = c_mat.reshape(bsz, s, SSM_GROUPS, D_STATE)
    dt = jax.nn.softplus(dt_raw.astype(jnp.float32) + dt_bias.astype(jnp.float32))
    a = -jnp.exp(a_log.astype(jnp.float32))
    y = ssd_chunked(xs, dt, a, b_mat, c_mat) + d_skip.astype(jnp.float32)[:, None] * xs.astype(jnp.float32)
    y = gated_group_rms_norm(y.reshape(bsz, s, D_SSM), z, ssm_norm_w)

    mix = jnp.concatenate([attn, y], axis=-1).astype(hid.dtype)
    out = jnp.einsum('bse,ed->bsd', mix, w_out.astype(mix.dtype))
    return hid + rms_norm(out, norm_post_w)


def _fwd_setup_inputs(seed: int = 0) -> dict:
    key = jax.random.key(seed)
    ks = jax.random.split(key, 12)
    f32 = jnp.float32
    x = jax.random.normal(ks[0], (BATCH, SEQ, D_MODEL), f32)
    norm_pre_w = 1.0 + 0.1 * jax.random.normal(ks[1], (DEPTH, D_MODEL), f32)
    w_in = jax.random.normal(ks[2], (DEPTH, D_MODEL, D_IN_PROJ), f32) * D_MODEL ** -0.5
    conv_w = jax.random.normal(ks[3], (DEPTH, CONV_K, D_CONV), f32) * CONV_K ** -0.5
    conv_b = 0.02 * jax.random.normal(ks[4], (DEPTH, D_CONV), f32)
    dt0 = jnp.exp(jax.random.uniform(ks[5], (DEPTH, SSM_HEADS), f32, math.log(DT_MIN), math.log(DT_MAX)))
    dt_bias = dt0 + jnp.log(-jnp.expm1(-dt0))
    a_log = jnp.log(jax.random.uniform(ks[6], (DEPTH, SSM_HEADS), f32, 1.0, 16.0))
    d_skip = 1.0 + 0.1 * jax.random.normal(ks[7], (DEPTH, SSM_HEADS), f32)
    ssm_norm_w = 1.0 + 0.1 * jax.random.normal(ks[8], (DEPTH, D_SSM), f32)
    w_out = jax.random.normal(ks[9], (DEPTH, D_INNER, D_MODEL), f32) * D_INNER ** -0.5
    norm_post_w = 1.0 + 0.1 * jax.random.normal(ks[10], (DEPTH, D_MODEL), f32)
    return {'x': x, 'norm_pre_w': norm_pre_w, 'w_in': w_in, 'conv_w': conv_w, 'conv_b': conv_b,
            'dt_bias': dt_bias, 'a_log': a_log, 'd_skip': d_skip, 'ssm_norm_w': ssm_norm_w,
            'w_out': w_out, 'norm_post_w': norm_post_w}


def _fwd_reference(x, norm_pre_w, w_in, conv_w, conv_b, dt_bias, a_log, d_skip, ssm_norm_w, w_out, norm_post_w):
    hid = x
    for layer in range(DEPTH):
        hid = hybrid_layer(hid, norm_pre_w[layer], w_in[layer], conv_w[layer], conv_b[layer],
                           dt_bias[layer], a_log[layer], d_skip[layer], ssm_norm_w[layer],
                           w_out[layer], norm_post_w[layer])
    return hid


import jax as _jax
import jax.numpy as _jnp

TWIN_FORMAT = 'train_step'
FWD_PARAMS = ['x', 'norm_pre_w', 'w_in', 'conv_w', 'conv_b', 'dt_bias', 'a_log', 'd_skip', 'ssm_norm_w', 'w_out', 'norm_post_w']
TWIN_WEIGHTS = ['norm_pre_w', 'w_in', 'conv_w', 'conv_b', 'dt_bias', 'a_log', 'd_skip', 'ssm_norm_w', 'w_out', 'norm_post_w']
TWIN_DIFF_INPUT = 'x'
TWIN_INPUTS = ['x', 'norm_pre_w', 'w_in', 'conv_w', 'conv_b', 'dt_bias', 'a_log', 'd_skip', 'ssm_norm_w', 'w_out', 'norm_post_w', 'loss_target', 'm_norm_pre_w', 'm_w_in', 'm_conv_w', 'm_conv_b', 'm_dt_bias', 'm_a_log', 'm_d_skip', 'm_ssm_norm_w', 'm_w_out', 'm_norm_post_w', 'v_norm_pre_w', 'v_w_in', 'v_conv_w', 'v_conv_b', 'v_dt_bias', 'v_a_log', 'v_d_skip', 'v_ssm_norm_w', 'v_w_out', 'v_norm_post_w']
TWIN_OUTPUTS = ['loss', 'grad_x', 'grad_norm_pre_w', 'grad_w_in', 'grad_conv_w', 'grad_conv_b', 'grad_dt_bias', 'grad_a_log', 'grad_d_skip', 'grad_ssm_norm_w', 'grad_w_out', 'grad_norm_post_w', 'delta_norm_pre_w', 'delta_w_in', 'delta_conv_w', 'delta_conv_b', 'delta_dt_bias', 'delta_a_log', 'delta_d_skip', 'delta_ssm_norm_w', 'delta_w_out', 'delta_norm_post_w', 'new_m_norm_pre_w', 'new_m_w_in', 'new_m_conv_w', 'new_m_conv_b', 'new_m_dt_bias', 'new_m_a_log', 'new_m_d_skip', 'new_m_ssm_norm_w', 'new_m_w_out', 'new_m_norm_post_w', 'new_v_norm_pre_w', 'new_v_w_in', 'new_v_conv_w', 'new_v_conv_b', 'new_v_dt_bias', 'new_v_a_log', 'new_v_d_skip', 'new_v_ssm_norm_w', 'new_v_w_out', 'new_v_norm_post_w']
TWIN_LEAF_KINDS = {'loss': 'loss', 'grad_x': 'grad_x', 'grad_norm_pre_w': 'grad_w', 'grad_w_in': 'grad_w', 'grad_conv_w': 'grad_w', 'grad_conv_b': 'grad_w', 'grad_dt_bias': 'grad_w', 'grad_a_log': 'grad_w', 'grad_d_skip': 'grad_w', 'grad_ssm_norm_w': 'grad_w', 'grad_w_out': 'grad_w', 'grad_norm_post_w': 'grad_w', 'delta_norm_pre_w': 'delta_w', 'delta_w_in': 'delta_w', 'delta_conv_w': 'delta_w', 'delta_conv_b': 'delta_w', 'delta_dt_bias': 'delta_w', 'delta_a_log': 'delta_w', 'delta_d_skip': 'delta_w', 'delta_ssm_norm_w': 'delta_w', 'delta_w_out': 'delta_w', 'delta_norm_post_w': 'delta_w', 'new_m_norm_pre_w': 'new_m', 'new_m_w_in': 'new_m', 'new_m_conv_w': 'new_m', 'new_m_conv_b': 'new_m', 'new_m_dt_bias': 'new_m', 'new_m_a_log': 'new_m', 'new_m_d_skip': 'new_m', 'new_m_ssm_norm_w': 'new_m', 'new_m_w_out': 'new_m', 'new_m_norm_post_w': 'new_m', 'new_v_norm_pre_w': 'new_v', 'new_v_w_in': 'new_v', 'new_v_conv_w': 'new_v', 'new_v_conv_b': 'new_v', 'new_v_dt_bias': 'new_v', 'new_v_a_log': 'new_v', 'new_v_d_skip': 'new_v', 'new_v_ssm_norm_w': 'new_v', 'new_v_w_out': 'new_v', 'new_v_norm_post_w': 'new_v'}


def _forward(args):
    return _fwd_reference(*[args[k] for k in FWD_PARAMS])


def _output_shape():
    out = _jax.eval_shape(lambda: _forward(_fwd_setup_inputs(0)))
    return out.shape, out.dtype

N_MICROBATCH = 1
ADAM_LR = 0.001
ADAM_B1 = 0.9
ADAM_B2 = 0.999
ADAM_EPS = 1e-08
ADAM_WD = 0.01
ADAM_STEP = 10
PER_EXAMPLE_BATCH_AXIS = {'x': 0, 'loss_target': 0}
SHARED_INPUTS = []
_WEIGHT_DTYPES = {'norm_pre_w': _jnp.float32, 'w_in': _jnp.float32, 'conv_w': _jnp.float32, 'conv_b': _jnp.float32, 'dt_bias': _jnp.float32, 'a_log': _jnp.float32, 'd_skip': _jnp.float32, 'ssm_norm_w': _jnp.float32, 'w_out': _jnp.float32, 'norm_post_w': _jnp.float32}
MOMENT_SCALE = {'norm_pre_w': 5.015503e-01, 'w_in': 1.795040e-01, 'conv_w': 2.833967e-01, 'conv_b': 6.123499e-01, 'dt_bias': 6.942603e-01, 'a_log': 1.711131e+00, 'd_skip': 2.019790e+00, 'ssm_norm_w': 4.164788e-01, 'w_out': 4.073794e-01, 'norm_post_w': 3.228916e+01}


def _to_microbatches(a, axis):
    t = _jnp.moveaxis(a, axis, 0)
    t = t.reshape((N_MICROBATCH, t.shape[0] // N_MICROBATCH) + t.shape[1:])
    return _jnp.moveaxis(t, 1, axis + 1)


def setup_inputs(seed: int = 0) -> dict:
    inp = _fwd_setup_inputs(seed)
    key = _jax.random.fold_in(_jax.random.key(seed), 7919)
    shape, _ = _output_shape()
    out = dict(inp)
    out["loss_target"] = _jax.random.normal(_jax.random.fold_in(key, 0), shape, _jnp.float32)
    for i, name in enumerate(TWIN_WEIGHTS):
        w = inp[name].astype(_jnp.float32)
        if MOMENT_SCALE is None:
            s = _jnp.sqrt(_jnp.mean(_jnp.square(w)) + 1e-30)
        else:
            s = MOMENT_SCALE[name]
        km, kv = _jax.random.split(_jax.random.fold_in(key, i + 1))
        out[name] = w
        out["m_" + name] = s * _jax.random.normal(km, w.shape, _jnp.float32)
        out["v_" + name] = (s * s) * _jax.random.uniform(kv, w.shape, _jnp.float32, 0.5, 1.5)
    if N_MICROBATCH > 1:
        for name, axis in PER_EXAMPLE_BATCH_AXIS.items():
            out[name] = _to_microbatches(out[name], axis)
    return {'x': out['x'], 'norm_pre_w': out['norm_pre_w'], 'w_in': out['w_in'], 'conv_w': out['conv_w'], 'conv_b': out['conv_b'], 'dt_bias': out['dt_bias'], 'a_log': out['a_log'], 'd_skip': out['d_skip'], 'ssm_norm_w': out['ssm_norm_w'], 'w_out': out['w_out'], 'norm_post_w': out['norm_post_w'], 'loss_target': out['loss_target'], 'm_norm_pre_w': out['m_norm_pre_w'], 'm_w_in': out['m_w_in'], 'm_conv_w': out['m_conv_w'], 'm_conv_b': out['m_conv_b'], 'm_dt_bias': out['m_dt_bias'], 'm_a_log': out['m_a_log'], 'm_d_skip': out['m_d_skip'], 'm_ssm_norm_w': out['m_ssm_norm_w'], 'm_w_out': out['m_w_out'], 'm_norm_post_w': out['m_norm_post_w'], 'v_norm_pre_w': out['v_norm_pre_w'], 'v_w_in': out['v_w_in'], 'v_conv_w': out['v_conv_w'], 'v_conv_b': out['v_conv_b'], 'v_dt_bias': out['v_dt_bias'], 'v_a_log': out['v_a_log'], 'v_d_skip': out['v_d_skip'], 'v_ssm_norm_w': out['v_ssm_norm_w'], 'v_w_out': out['v_w_out'], 'v_norm_post_w': out['v_norm_post_w']}


def _loss(weights, diff, rest, loss_target):
    with _jax.named_scope("forward"):
        args = {**rest, TWIN_DIFF_INPUT: diff, **{k: w.astype(_WEIGHT_DTYPES[k]) for k, w in weights.items()}}
        y = _forward(args)
    with _jax.named_scope("loss_head"):
        err = _jnp.square(y.astype(_jnp.float32) - loss_target)
        return 0.5 * _jnp.sum(_jnp.mean(err, axis=-1)) if err.ndim else 0.5 * err


def _adamw(w, g, m, v):
    m = ADAM_B1 * m + (1.0 - ADAM_B1) * g
    v = ADAM_B2 * v + (1.0 - ADAM_B2) * _jnp.square(g)
    m_hat = m / (1.0 - ADAM_B1 ** ADAM_STEP)
    v_hat = v / (1.0 - ADAM_B2 ** ADAM_STEP)
    delta = -ADAM_LR * (m_hat / (_jnp.sqrt(v_hat) + ADAM_EPS) + ADAM_WD * w)
    return delta, m, v


def reference(x, norm_pre_w, w_in, conv_w, conv_b, dt_bias, a_log, d_skip, ssm_norm_w, w_out, norm_post_w, loss_target, m_norm_pre_w, m_w_in, m_conv_w, m_conv_b, m_dt_bias, m_a_log, m_d_skip, m_ssm_norm_w, m_w_out, m_norm_post_w, v_norm_pre_w, v_w_in, v_conv_w, v_conv_b, v_dt_bias, v_a_log, v_d_skip, v_ssm_norm_w, v_w_out, v_norm_post_w):
    given = dict(x=x, norm_pre_w=norm_pre_w, w_in=w_in, conv_w=conv_w, conv_b=conv_b, dt_bias=dt_bias, a_log=a_log, d_skip=d_skip, ssm_norm_w=ssm_norm_w, w_out=w_out, norm_post_w=norm_post_w, loss_target=loss_target, m_norm_pre_w=m_norm_pre_w, m_w_in=m_w_in, m_conv_w=m_conv_w, m_conv_b=m_conv_b, m_dt_bias=m_dt_bias, m_a_log=m_a_log, m_d_skip=m_d_skip, m_ssm_norm_w=m_ssm_norm_w, m_w_out=m_w_out, m_norm_post_w=m_norm_post_w, v_norm_pre_w=v_norm_pre_w, v_w_in=v_w_in, v_conv_w=v_conv_w, v_conv_b=v_conv_b, v_dt_bias=v_dt_bias, v_a_log=v_a_log, v_d_skip=v_d_skip, v_ssm_norm_w=v_ssm_norm_w, v_w_out=v_w_out, v_norm_post_w=v_norm_post_w)
    weights = {n: given[n] for n in TWIN_WEIGHTS}
    shared = {n: given[n] for n in SHARED_INPUTS}
    per_example = {n: given[n] for n in ['x']}
    grad_fn = _jax.value_and_grad(_loss, argnums=(0, 1))

    def one_microbatch(ex, loss_target):
        ex = dict(ex)
        diff = ex.pop(TWIN_DIFF_INPUT)
        return grad_fn(weights, diff, {**shared, **ex}, loss_target)

    if N_MICROBATCH == 1:
        loss, (grad_w, grad_x) = one_microbatch(per_example, given["loss_target"])
    else:
        def body(carry, xs):
            loss_sum, grad_sum = carry
            l_k, (gw_k, gx_k) = one_microbatch(xs[0], xs[1])
            with _jax.named_scope("update"):
                return (loss_sum + l_k, _jax.tree.map(_jnp.add, grad_sum, gw_k)), gx_k

        init = (_jnp.zeros((), _jnp.float32), _jax.tree.map(_jnp.zeros_like, weights))
        (loss, grad_w), grad_x = _jax.lax.scan(body, init, (per_example, given["loss_target"]))
    with _jax.named_scope("update"):
        delta_w, new_m, new_v = {}, {}, {}
        for n in TWIN_WEIGHTS:
            delta_w[n], new_m[n], new_v[n] = _adamw(weights[n], grad_w[n], given["m_" + n], given["v_" + n])
    return (loss, grad_x, *[grad_w[n] for n in TWIN_WEIGHTS], *[delta_w[n] for n in TWIN_WEIGHTS],
            *[new_m[n] for n in TWIN_WEIGHTS], *[new_v[n] for n in TWIN_WEIGHTS])
```

```python
import functools
import math

import jax
import jax.numpy as jnp
import numpy as np
from jax import lax
from jax.experimental import pallas as pl
from jax.experimental.pallas import tpu as pltpu

f32, bf16 = jnp.float32, jnp.bfloat16
SDS = jax.ShapeDtypeStruct
HIGHEST = lax.Precision.HIGHEST
MESH = pl.DeviceIdType.MESH

N_DEV = 8
D_MODEL = 1024
D_ATTN = 1024
D_SSM = 1024
HEAD_DIM = 64
N_PAIRS = 8
D_STATE = 128
N_GROUPS = 2
D_CONV = D_SSM + 2 * N_GROUPS * D_STATE
D_IN_PROJ = 4 * D_ATTN + D_SSM + D_CONV + 16
NP = 7168
CHUNK = 128
BLK = 128
DILATIONS = (1, 4, 16)
EPS = 1e-6
LANES = 128
COL_Z, COL_XS, COL_BC, COL_DT = 4096, 5120, 6144, 6656

ADAM_LR, ADAM_B1, ADAM_B2, ADAM_EPS, ADAM_WD, ADAM_STEP = 0.001, 0.9, 0.999, 1e-08, 0.01, 10

PACK_ROWS, PACK_W = 16, 1536


def _nt(a, b):
    return lax.dot_general(a, b, (((1,), (1,)), ((), ())), preferred_element_type=f32)


def _tn(a, b):
    return lax.dot_general(a, b, (((0,), (0,)), ((), ())), preferred_element_type=f32)


def _nn(a, b):
    return jnp.dot(a, b, preferred_element_type=f32)


def _nn_hi(a, b):
    return jnp.dot(a, b, precision=HIGHEST, preferred_element_type=f32)


def _sigmoid(x):
    return 1.0 / (1.0 + jnp.exp(-x))


def _softplus(x):
    return jnp.maximum(x, 0.0) + jnp.log1p(jnp.exp(-jnp.abs(x)))


def _iota(shape, dim):
    return lax.broadcasted_iota(jnp.int32, shape, dim)


def _my_pos():
    return lax.axis_index("x"), lax.axis_index("y"), lax.axis_index("c")


def _all_gather(arrs):
    n = len(arrs)

    def body(*refs):
        ins, outs = refs[:n], refs[n:2 * n]
        send_sems, recv_sems, local_sems = refs[2 * n:]
        x, y, c = _my_pos()
        me, sibling = (x, y, c), (x, y, 1 - c)
        chips = [(1 - x, y), (x, 1 - y), (1 - x, 1 - y)]

        def slot(a, px, py, pc):
            return outs[a].at[4 * px + 2 * py + pc]

        def copy(a, k, block, to, src=None):
            return pltpu.make_async_remote_copy(
                src_ref=slot(a, *block) if src is None else src, dst_ref=slot(a, *block),
                send_sem=send_sems.at[7 * a + k], recv_sem=recv_sems.at[7 * a + k],
                device_id=to, device_id_type=MESH)

        mine = [pltpu.make_async_copy(ins[a], slot(a, *me), local_sems.at[a]) for a in range(n)]
        for cp in mine:
            cp.start()
        first = []
        for a in range(n):
            first.append(copy(a, 0, me, sibling, src=ins[a]))
            first += [copy(a, 1 + j, me, (*chip, c), src=ins[a]) for j, chip in enumerate(chips)]
        for cp in first:
            cp.start()
        passed = []
        for j, chip in enumerate(chips):
            for a in range(n):
                copy(a, 1 + j, (*chip, c), me).wait_recv()
                cp = copy(a, 4 + j, (*chip, c), sibling)
                cp.start()
                passed.append(cp)
        for a in range(n):
            copy(a, 0, sibling, me).wait_recv()
            for j, chip in enumerate(chips):
                copy(a, 4 + j, (*chip, 1 - c), me).wait_recv()
        for cp in first + passed:
            cp.wait_send()
        for cp in mine:
            cp.wait()

    anyspec = pl.BlockSpec(memory_space=pl.ANY)
    return pl.pallas_call(
        body, name="weights_all_gather",
        out_shape=[SDS((N_DEV,) + a.shape, a.dtype) for a in arrs],
        in_specs=[anyspec] * n, out_specs=[anyspec] * n,
        scratch_shapes=[pltpu.SemaphoreType.DMA((7 * n,)), pltpu.SemaphoreType.DMA((7 * n,)),
                        pltpu.SemaphoreType.DMA((n,))],
    )(*arrs)


def _exchange(scatter, gather):
    arrs = list(scatter) + list(gather)
    n, ns = len(arrs), len(scatter)

    def body(*refs):
        ins, outs = refs[:n], refs[n:2 * n]
        send_sems, recv_sems, local_sems = refs[2 * n:]
        x, y, c = _my_pos()
        me = 4 * x + 2 * y + c

        def src_of(a, j):
            return ins[a].at[j] if a < ns else ins[a]

        mine = [pltpu.make_async_copy(src_of(a, me), outs[a].at[me], local_sems.at[a]) for a in range(n)]
        for cp in mine:
            cp.start()
        sends = []
        for k in range(1, N_DEV):
            to = (me + k) % N_DEV
            for a in range(n):
                cp = pltpu.make_async_remote_copy(
                    src_ref=src_of(a, to), dst_ref=outs[a].at[me],
                    send_sem=send_sems.at[7 * a + k - 1], recv_sem=recv_sems.at[7 * a + k - 1],
                    device_id=(to // 4, (to // 2) % 2, to % 2), device_id_type=MESH)
                cp.start()
                sends.append(cp)
        for k in range(1, N_DEV):
            frm = (me + N_DEV - k) % N_DEV
            for a in range(n):
                pltpu.make_async_remote_copy(
                    src_ref=src_of(a, frm), dst_ref=outs[a].at[frm],
                    send_sem=send_sems.at[7 * a + k - 1], recv_sem=recv_sems.at[7 * a + k - 1],
                    device_id=(x, y, c), device_id_type=MESH).wait_recv()
        for cp in sends:
            cp.wait_send()
        for cp in mine:
            cp.wait()

    anyspec = pl.BlockSpec(memory_space=pl.ANY)
    out_shape = [SDS(a.shape, a.dtype) for a in scatter] + [SDS((N_DEV,) + a.shape, a.dtype) for a in gather]
    return pl.pallas_call(
        body, name="grad_exchange", out_shape=out_shape,
        in_specs=[anyspec] * n, out_specs=[anyspec] * n,
        scratch_shapes=[pltpu.SemaphoreType.DMA((7 * n,)), pltpu.SemaphoreType.DMA((7 * n,)),
                        pltpu.SemaphoreType.DMA((n,))],
    )(*arrs)


def _prenorm_inproj(x, nw, wp):
    s, d = x.shape
    npad = wp.shape[1]
    tm, tn = 1024, 512

    def body(x_ref, nw_ref, w_ref, proj_ref, u_ref):
        @pl.when(pl.program_id(1) == 0)
        def _():
            xv = x_ref[...]
            r = lax.rsqrt(jnp.mean(xv * xv, axis=-1, keepdims=True) + EPS)
            u_ref[...] = (xv * r * nw_ref[...]).astype(bf16)
        proj_ref[...] = _nn(u_ref[...], w_ref[...])

    return pl.pallas_call(
        body, name="prenorm_inproj", grid=(s // tm, npad // tn),
        in_specs=[pl.BlockSpec((tm, d), lambda i, j: (i, 0)), pl.BlockSpec((1, d), lambda i, j: (0, 0)),
                  pl.BlockSpec((d, tn), lambda i, j: (0, j))],
        out_specs=[pl.BlockSpec((tm, tn), lambda i, j: (i, j)), pl.BlockSpec((tm, d), lambda i, j: (i, 0))],
        out_shape=[SDS((s, npad), f32), SDS((s, d), bf16)],
        compiler_params=pltpu.CompilerParams(dimension_semantics=("parallel", "arbitrary")),
    )(x, nw, wp)


def _attn_masks():
    lane = _iota((BLK, LANES), 1)
    sub = _iota((BLK, LANES), 0)
    head0 = lane < HEAD_DIM
    cur_ok = sub >= lane
    prev_ok = lane >= sub
    return head0, cur_ok, prev_ok


def _attn_fwd(proj):
    s = proj.shape[0]
    n_it = s // BLK

    def body(q_ref, k_ref, v_ref, g_ref, o_ref, l_ref, mix_ref, op0, op1, op2, lp0, lp1, lp2):
        op_refs, lp_refs = (op0, op1, op2), (lp0, lp1, lp2)
        head0, cur_ok, prev_ok0 = _attn_masks()
        for p, d in enumerate(DILATIONS):
            nb = s // (BLK * d)

            def it(i, carry, d=d, nb=nb, p=p):
                r, blk = i // nb, i % nb
                st = blk * (BLK * d) + r
                stp = jnp.maximum(blk - 1, 0) * (BLK * d) + r
                rows = pl.ds(st, BLK, stride=d)
                rows_p = pl.ds(stp, BLK, stride=d)
                qs = q_ref[rows, :] * 0.125
                kc, kp = k_ref[rows, :].astype(bf16), k_ref[rows_p, :].astype(bf16)
                vc, vp = v_ref[rows, :].astype(bf16), v_ref[rows_p, :].astype(bf16)
                prev_ok = jnp.logical_and(prev_ok0, blk > 0)
                outs, lses = [], []
                for h in range(2):
                    mh = head0 if h == 0 else jnp.logical_not(head0)
                    qh = jnp.where(mh, qs, 0.0).astype(bf16)
                    sc = jnp.where(cur_ok, _nt(qh, kc), -jnp.inf)
                    sp = jnp.where(prev_ok, _nt(qh, kp), -jnp.inf)
                    m = jnp.maximum(jnp.max(sc, axis=1, keepdims=True), jnp.max(sp, axis=1, keepdims=True))
                    pc, pp = jnp.exp(sc - m), jnp.exp(sp - m)
                    l = jnp.sum(pc, axis=1, keepdims=True) + jnp.sum(pp, axis=1, keepdims=True)
                    o = _nn(pc.astype(bf16), vc) + _nn(pp.astype(bf16), vp)
                    outs.append(o / l)
                    lses.append(m + jnp.log(l))
                op_refs[p][rows, :] = jnp.where(head0, outs[0], outs[1])
                lp_refs[p][rows, :] = jnp.where(head0, lses[0], lses[1])
                return carry

            lax.fori_loop(0, n_it, it, 0)

        def merge(i, carry):
            rows = pl.ds(pl.multiple_of(i * 256, 256), 256)
            l0, l1, l2 = lp0[rows, :], lp1[rows, :], lp2[rows, :]
            m = jnp.maximum(jnp.maximum(l0, l1), l2)
            e0, e1, e2 = jnp.exp(l0 - m), jnp.exp(l1 - m), jnp.exp(l2 - m)
            z = e0 + e1 + e2
            o = (e0 * op0[rows, :] + e1 * op1[rows, :] + e2 * op2[rows, :]) / z
            o_ref[rows, :] = o
            l_ref[rows, :] = m + jnp.log(z)
            g = g_ref[rows, :]
            mix_ref[rows, :] = (o * (g * _sigmoid(g))).astype(bf16)
            return carry

        lax.fori_loop(0, s // 256, merge, 0)

    col = lambda base: pl.BlockSpec((s, LANES), lambda h: (0, base + h))
    return pl.pallas_call(
        body, name="attn_fwd", grid=(N_PAIRS,),
        in_specs=[col(0), col(8), col(16), col(24)],
        out_specs=[col(0), col(0), col(0)],
        out_shape=[SDS((s, D_ATTN), f32), SDS((s, D_ATTN), f32), SDS((s, D_ATTN), bf16)],
        scratch_shapes=[pltpu.VMEM((s, LANES), f32)] * 6,
        compiler_params=pltpu.CompilerParams(dimension_semantics=("parallel",)),
    )(proj, proj, proj, proj)


def _expand_mat():
    row = _iota((LANES, 2 * D_SSM), 0)
    colv = _iota((LANES, 2 * D_SSM), 1)
    head = 2 * ((colv % D_SSM) // LANES) + colv // D_SSM
    return (row == head).astype(f32)


def _ssd_common(xs_ref, bc_ref, xs_tail, bc_tail, dt_ref, cw_ref, cb_ref, dtb_ref, alog16_ref, xpad, first):
    keep = jnp.where(first, 0.0, 1.0)
    xpad[0:8, 0:D_SSM] = xs_tail[...] * keep
    xpad[0:8, D_SSM:D_CONV] = bc_tail[...] * keep
    xpad[8:8 + CHUNK, 0:D_SSM] = xs_ref[...]
    xpad[8:8 + CHUNK, D_SSM:D_CONV] = bc_ref[...]
    cv = cb_ref[...] + cw_ref[0:1, :] * xpad[pl.ds(5, CHUNK), :]
    for j in range(1, 4):
        cv = cv + cw_ref[j:j + 1, :] * xpad[pl.ds(5 + j, CHUNK), :]
    sig = _sigmoid(cv)
    xbc = cv * sig

    pre = dt_ref[...] + dtb_ref[...]
    dt16 = _softplus(pre)
    a16 = -jnp.exp(alog16_ref[...])
    sub, lane = _iota((CHUNK, CHUNK), 0), _iota((CHUNK, CHUNK), 1)
    tri = (sub >= lane).astype(f32)
    al16 = _nn_hi(tri, dt16 * a16)
    al_t = al16.T
    emat = _expand_mat()
    dt_x = _nn_hi(dt16, emat)
    al_x = _nn_hi(al16, emat)
    lane_w = _iota((CHUNK, D_SSM), 1)
    even = (lane_w % LANES) < HEAD_DIM
    dt_f = jnp.where(even, dt_x[:, :D_SSM], dt_x[:, D_SSM:])
    al_f = jnp.where(even, al_x[:, :D_SSM], al_x[:, D_SSM:])
    return cv, sig, xbc, pre, dt_f, al_f, al_x, al_t


def _decay_mat(al_x, al_t, pair, h):
    sub, lane = _iota((CHUNK, CHUNK), 0), _iota((CHUNK, CHUNK), 1)
    col = al_x[:, h * D_SSM + pair * LANES: h * D_SSM + (pair + 1) * LANES]
    row = al_t[2 * pair + h: 2 * pair + h + 1, :]
    return jnp.exp(jnp.where(sub >= lane, col - row, -jnp.inf))


def _ssd_in_specs(order):
    blk = lambda w, cb: pl.BlockSpec((CHUNK, w), lambda i: (order(i), cb))
    tail = lambda w, cb: pl.BlockSpec((8, w), lambda i: (jnp.maximum(16 * order(i) - 1, 0), cb))
    return [blk(D_SSM, COL_XS // D_SSM), blk(512, COL_BC // 512), tail(D_SSM, COL_XS // D_SSM),
            tail(512, COL_BC // 512), blk(LANES, COL_DT // LANES), blk(D_SSM, COL_Z // D_SSM)]


def _full(shape):
    return pl.BlockSpec(shape, lambda i: (0,) * len(shape))


def _ssd_fwd(proj, conv_w, conv_b, dtb16, alog16, alog_f, d_f, nw):
    s = proj.shape[0]
    nc = s // CHUNK

    def body(xs_ref, bc_ref, xs_tail, bc_tail, dt_ref, z_ref, cw_ref, cb_ref, dtb_ref, alog16_ref, alogf_ref,
             df_ref, nw_ref, mix_ref, y_ref, st_ref, h_scr, xpad, y_scr):
        c = pl.program_id(0)

        @pl.when(c == 0)
        def _():
            h_scr[...] = jnp.zeros_like(h_scr)

        _, _, xbc, _, dt_f, al_f, al_x, al_t = _ssd_common(
            xs_ref, bc_ref, xs_tail, bc_tail, dt_ref, cw_ref, cb_ref, dtb_ref, alog16_ref, xpad, c == 0)
        head0 = _iota((CHUNK, LANES), 1) < HEAD_DIM
        st_ref[...] = h_scr[...]
        for g in range(N_GROUPS):
            bm = xbc[:, D_SSM + g * D_STATE: D_SSM + (g + 1) * D_STATE].astype(bf16)
            cm = xbc[:, D_SSM + (N_GROUPS + g) * D_STATE: D_SSM + (N_GROUPS + g + 1) * D_STATE].astype(bf16)
            gmat = _nt(cm, bm)
            for pair in range(4 * g, 4 * g + 4):
                sl = slice(pair * LANES, (pair + 1) * LANES)
                xp, dtp, alp = xbc[:, sl], dt_f[:, sl], al_f[:, sl]
                xdt = xp * dtp
                xdt16 = xdt.astype(bf16)
                al_last = alp[CHUNK - 1:CHUNK, :]
                hp = h_scr[:, sl]
                y_off = jnp.exp(alp) * _nn(cm, hp.astype(bf16))
                yd = [_nn((gmat * _decay_mat(al_x, al_t, pair, h)).astype(bf16), xdt16) for h in range(2)]
                y_scr[:, sl] = jnp.where(head0, yd[0], yd[1]) + y_off + df_ref[:, sl] * xp
                st = _tn(bm, (jnp.exp(al_last - alp) * xdt).astype(bf16))
                h_scr[:, sl] = jnp.exp(al_last) * hp + st
        y = y_scr[...]
        y_ref[...] = y
        z = z_ref[...]
        yz = y * (z * _sigmoid(z))
        gw = D_SSM // N_GROUPS
        for g in range(N_GROUPS):
            part = yz[:, g * gw:(g + 1) * gw]
            r = lax.rsqrt(jnp.mean(part * part, axis=-1, keepdims=True) + EPS)
            mix_ref[:, g * gw:(g + 1) * gw] = (part * r * nw_ref[:, g * gw:(g + 1) * gw]).astype(bf16)

    order = lambda i: i
    row = lambda w: pl.BlockSpec((CHUNK, w), lambda i: (i, 0))
    return pl.pallas_call(
        body, name="ssd_fwd", grid=(nc,),
        in_specs=_ssd_in_specs(order) + [_full((4, D_CONV)), _full((1, D_CONV)), _full((1, LANES)), _full((1, LANES)),
                                         _full((1, D_SSM)), _full((1, D_SSM)), _full((1, D_SSM))],
        out_specs=[row(D_SSM), row(D_SSM), pl.BlockSpec((None, D_STATE, D_SSM), lambda i: (i, 0, 0))],
        out_shape=[SDS((s, D_SSM), bf16), SDS((s, D_SSM), f32), SDS((nc, D_STATE, D_SSM), f32)],
        scratch_shapes=[pltpu.VMEM((D_STATE, D_SSM), f32), pltpu.VMEM((8 + CHUNK, D_CONV), f32),
                        pltpu.VMEM((CHUNK, D_SSM), f32)],
        compiler_params=pltpu.CompilerParams(dimension_semantics=("arbitrary",)),
    )(proj, proj, proj, proj, proj, proj, conv_w, conv_b, dtb16, alog16, alog_f, d_f, nw)


def _outproj_loss(mix_a, mix_s, wo, x, tgt, npw):
    s, d = x.shape
    tm = 512

    def body(ma_ref, ms_ref, wo_ref, x_ref, t_ref, npw_ref, dmix_ref, dout_ref, dres_ref, acc_ref):
        @pl.when(pl.program_id(0) == 0)
        def _():
            acc_ref[...] = jnp.zeros_like(acc_ref)

        out = _nn(ma_ref[...], wo_ref[0:D_ATTN, :]) + _nn(ms_ref[...], wo_ref[D_ATTN:, :])
        r = lax.rsqrt(jnp.mean(out * out, axis=-1, keepdims=True) + EPS)
        on = out * r
        diff = x_ref[...] + on * npw_ref[...] - t_ref[...]
        dres = diff * (1.0 / d)
        dres_ref[...] = dres
        acc_ref[0:1, :] += jnp.sum(diff * diff, axis=0, keepdims=True)
        acc_ref[1:2, :] += jnp.sum(dres * on, axis=0, keepdims=True)
        dn = dres * npw_ref[...]
        dout = (r * (dn - on * jnp.mean(dn * on, axis=-1, keepdims=True))).astype(bf16)
        dout_ref[...] = dout
        dmix_ref[...] = _nt(dout, wo_ref[...])

    row = lambda w: pl.BlockSpec((tm, w), lambda i: (i, 0))
    return pl.pallas_call(
        body, name="outproj_loss", grid=(s // tm,),
        in_specs=[row(D_ATTN), row(D_SSM), _full((D_ATTN + D_SSM, d)), row(d), row(d), _full((1, d))],
        out_specs=[row(D_ATTN + D_SSM), row(d), row(d), _full((8, d))],
        out_shape=[SDS((s, D_ATTN + D_SSM), f32), SDS((s, d), bf16), SDS((s, d), f32), SDS((8, d), f32)],
        compiler_params=pltpu.CompilerParams(dimension_semantics=("arbitrary",)),
    )(mix_a, mix_s, wo, x, tgt, npw)


def _attn_bwd(proj, o, lb, dmix):
    s = proj.shape[0]
    n_it = s // BLK

    def body(q_ref, k_ref, v_ref, g_ref, o_ref, l_ref, dm_ref, dq_ref, dk_ref, dv_ref, dg_ref,
             dq_acc, dk_acc, dv_acc, do_scr, dl_scr):
        head0, cur_ok, prev_ok0 = _attn_masks()
        head1 = jnp.logical_not(head0)

        def pro(i, carry):
            rows = pl.ds(pl.multiple_of(i * 256, 256), 256)
            g = g_ref[rows, :]
            sg = _sigmoid(g)
            dmx = dm_ref[rows, :]
            ov = o_ref[rows, :]
            dg_ref[rows, :] = (dmx * ov * (sg * (1.0 + g * (1.0 - sg)))).astype(bf16)
            do = dmx * (g * sg)
            do_scr[rows, :] = do
            prod = do * ov
            m0 = _iota(prod.shape, 1) < HEAD_DIM
            d0 = jnp.sum(jnp.where(m0, prod, 0.0), axis=1, keepdims=True)
            d1 = jnp.sum(jnp.where(m0, 0.0, prod), axis=1, keepdims=True)
            dl_scr[rows, :] = jnp.where(m0, d0, d1)
            z = jnp.zeros((256, LANES), f32)
            dq_acc[rows, :] = z
            dk_acc[rows, :] = z
            dv_acc[rows, :] = z
            return carry

        lax.fori_loop(0, s // 256, pro, 0)

        for d in DILATIONS:
            nb = s // (BLK * d)

            def it(i, carry, d=d, nb=nb):
                r, blk = i // nb, i % nb
                st = blk * (BLK * d) + r
                stp = jnp.maximum(blk - 1, 0) * (BLK * d) + r
                rows = pl.ds(st, BLK, stride=d)
                rows_p = pl.ds(stp, BLK, stride=d)
                q = q_ref[rows, :]
                kc, kp = k_ref[rows, :], k_ref[rows_p, :]
                vc, vp = v_ref[rows, :].astype(bf16), v_ref[rows_p, :].astype(bf16)
                kc16, kp16 = kc.astype(bf16), kp.astype(bf16)
                do = do_scr[rows, :]
                lse = l_ref[rows, :]
                dl = dl_scr[rows, :]
                prev_ok = jnp.logical_and(prev_ok0, blk > 0)
                dq = jnp.zeros((BLK, LANES), f32)
                dkc = jnp.zeros((BLK, LANES), f32)
                dkp = jnp.zeros((BLK, LANES), f32)
                dvc = jnp.zeros((BLK, LANES), f32)
                dvp = jnp.zeros((BLK, LANES), f32)
                for h in range(2):
                    mh = head0 if h == 0 else head1
                    lo = h * HEAD_DIM
                    qh = jnp.where(mh, q, 0.0).astype(bf16)
                    qsh = jnp.where(mh, q * 0.125, 0.0).astype(bf16)
                    doh = jnp.where(mh, do, 0.0).astype(bf16)
                    lh = lse[:, lo:lo + 1]
                    dlh = dl[:, lo:lo + 1]
                    pc = jnp.exp(jnp.where(cur_ok, _nt(qsh, kc16), -jnp.inf) - lh)
                    pp = jnp.exp(jnp.where(prev_ok, _nt(qsh, kp16), -jnp.inf) - lh)
                    dsc = (pc * (_nt(doh, vc) - dlh) * 0.125).astype(bf16)
                    dsp = (pp * (_nt(doh, vp) - dlh) * 0.125).astype(bf16)
                    dvc = dvc + _tn(pc.astype(bf16), doh)
                    dvp = dvp + _tn(pp.astype(bf16), doh)
                    dq = dq + _nn(dsc, jnp.where(mh, kc, 0.0).astype(bf16)) + _nn(dsp, jnp.where(mh, kp, 0.0).astype(bf16))
                    dkc = dkc + _tn(dsc, qh)
                    dkp = dkp + _tn(dsp, qh)
                dq_acc[rows, :] += dq
                dk_acc[rows, :] += dkc
                dv_acc[rows, :] += dvc
                dk_acc[rows_p, :] += dkp
                dv_acc[rows_p, :] += dvp
                return carry

            lax.fori_loop(0, n_it, it, 0)

        def epi(i, carry):
            rows = pl.ds(pl.multiple_of(i * 256, 256), 256)
            dq_ref[rows, :] = dq_acc[rows, :].astype(bf16)
            dk_ref[rows, :] = dk_acc[rows, :].astype(bf16)
            dv_ref[rows, :] = dv_acc[rows, :].astype(bf16)
            return carry

        lax.fori_loop(0, s // 256, epi, 0)

    col = lambda base: pl.BlockSpec((s, LANES), lambda h: (0, base + h))
    outs = pl.pallas_call(
        body, name="attn_bwd", grid=(N_PAIRS,),
        in_specs=[col(0), col(8), col(16), col(24), col(0), col(0), col(0)],
        out_specs=[col(0)] * 4,
        out_shape=[SDS((s, D_ATTN), bf16)] * 4,
        scratch_shapes=[pltpu.VMEM((s, LANES), f32)] * 5,
        compiler_params=pltpu.CompilerParams(dimension_semantics=("parallel",)),
    )(proj, proj, proj, proj, o, lb, dmix)
    return outs


def _ssd_bwd(proj, y, states, dmix, conv_w, conv_b, dtb16, alog16, alog_f, d_f, nw):
    s = proj.shape[0]
    nc = s // CHUNK
    gw = D_SSM // N_GROUPS

    def body(xs_ref, bc_ref, xs_tail, bc_tail, dt_ref, z_ref, y_ref, st_ref, dm_ref, cw_ref, cb_ref, dtb_ref,
             alog16_ref, alogf_ref, df_ref, nw_ref, out_ref, gconv_ref, gvec_ref, gdt_ref,
             dh_scr, head_scr, xpad, dcpad, da_scr, dxdt_scr, dbc_scr):
        i = pl.program_id(0)
        c = nc - 1 - i

        @pl.when(i == 0)
        def _():
            dh_scr[...] = jnp.zeros_like(dh_scr)
            head_scr[...] = jnp.zeros_like(head_scr)
            gconv_ref[...] = jnp.zeros_like(gconv_ref)
            gvec_ref[...] = jnp.zeros_like(gvec_ref)
            gdt_ref[...] = jnp.zeros_like(gdt_ref)

        cv, sig, xbc, pre, dt_f, al_f, al_x, al_t = _ssd_common(
            xs_ref, bc_ref, xs_tail, bc_tail, dt_ref, cw_ref, cb_ref, dtb_ref, alog16_ref, xpad, c == 0)
        head0 = _iota((CHUNK, LANES), 1) < HEAD_DIM
        sub = _iota((CHUNK, LANES), 0)
        last_row = sub == CHUNK - 1

        yv, z, dmx = y_ref[...], z_ref[...], dm_ref[...]
        sz = _sigmoid(z)
        silu = z * sz
        yz = yv * silu
        dyz_parts = []
        for g in range(N_GROUPS):
            gs = slice(g * gw, (g + 1) * gw)
            part = yz[:, gs]
            r = lax.rsqrt(jnp.mean(part * part, axis=-1, keepdims=True) + EPS)
            nh = part * r
            gvec_ref[0:1, gs] += jnp.sum(dmx[:, gs] * nh, axis=0, keepdims=True)
            dn = dmx[:, gs] * nw_ref[:, gs]
            dyz_parts.append(r * (dn - nh * jnp.mean(dn * nh, axis=-1, keepdims=True)))
        dyz = jnp.concatenate(dyz_parts, axis=1)
        dy = dyz * silu
        out_ref[:, 0:D_SSM] = (dyz * yv * (sz * (1.0 + z * (1.0 - sz)))).astype(bf16)

        x_all = xbc[:, 0:D_SSM]
        gvec_ref[2:3, :] += jnp.sum(dy * x_all, axis=0, keepdims=True)

        for g in range(N_GROUPS):
            bm = xbc[:, D_SSM + g * D_STATE: D_SSM + (g + 1) * D_STATE].astype(bf16)
            cm = xbc[:, D_SSM + (N_GROUPS + g) * D_STATE: D_SSM + (N_GROUPS + g + 1) * D_STATE].astype(bf16)
            gmat = _nt(cm, bm)
            dgm = jnp.zeros((CHUNK, CHUNK), f32)
            db = jnp.zeros((CHUNK, D_STATE), f32)
            dc = jnp.zeros((CHUNK, D_STATE), f32)
            for pair in range(4 * g, 4 * g + 4):
                sl = slice(pair * LANES, (pair + 1) * LANES)
                xp, dtp, alp, dyp = x_all[:, sl], dt_f[:, sl], al_f[:, sl], dy[:, sl]
                xdt = xp * dtp
                xdt16 = xdt.astype(bf16)
                al_last = alp[CHUNK - 1:CHUNK, :]
                e_l = jnp.exp(alp)
                wf = jnp.exp(al_last - alp)
                e_last = jnp.exp(al_last)
                hp = st_ref[:, sl]
                hp16 = hp.astype(bf16)
                dhn = dh_scr[:, sl]
                dhn16 = dhn.astype(bf16)
                y_off = e_l * _nn(cm, hp16)
                dch16 = (dyp * e_l).astype(bf16)
                dc = dc + _nt(dch16, hp16)
                dh_out = _tn(cm, dch16)
                dal = dyp * y_off
                xw16 = (wf * xdt).astype(bf16)
                db = db + _nt(xw16, dhn16)
                dxw = _nn(bm, dhn16)
                dxdt = dxw * wf
                dwf = dxw * xdt * wf
                dal = dal - dwf
                dal_last = jnp.sum(dwf, axis=0, keepdims=True) + jnp.sum(dhn * hp, axis=0, keepdims=True) * e_last
                dh_scr[:, sl] = e_last * dhn + dh_out
                for h in range(2):
                    mh = head0 if h == 0 else jnp.logical_not(head0)
                    dyh16 = jnp.where(mh, dyp, 0.0).astype(bf16)
                    lmat = _decay_mat(al_x, al_t, pair, h)
                    mm = gmat * lmat
                    dmm = _nt(dyh16, xdt16)
                    dxdt = dxdt + _tn(mm.astype(bf16), dyh16)
                    n16 = (dmm * mm).astype(bf16)
                    jh = jnp.where(mh, 1.0 / HEAD_DIM, 0.0).astype(bf16)
                    dal = dal + _nn(n16, jh) - _tn(n16, jh)
                    dgm = dgm + dmm * lmat
                da_scr[:, sl] = dal + jnp.where(last_row, dal_last, 0.0)
                dxdt_scr[:, sl] = dxdt
            dgm16 = dgm.astype(bf16)
            dbc_scr[:, g * D_STATE:(g + 1) * D_STATE] = db + _tn(dgm16, cm)
            dbc_scr[:, (N_GROUPS + g) * D_STATE:(N_GROUPS + g + 1) * D_STATE] = dc + _nn(dgm16, bm)

        sub_c, lane_c = _iota((CHUNK, CHUNK), 0), _iota((CHUNK, CHUNK), 1)
        tri_t = (lane_c >= sub_c).astype(f32)
        dadt = _nn_hi(tri_t, da_scr[...])
        a_f = -jnp.exp(alogf_ref[...])
        dxdt_all = dxdt_scr[...]
        ddt_f = dxdt_all * x_all + a_f * dadt
        gvec_ref[1:2, :] += jnp.sum(dt_f * dadt, axis=0, keepdims=True) * a_f
        dx = df_ref[...] * dy + dxdt_all * dt_f
        row_h = _iota((D_SSM, LANES), 0) // HEAD_DIM
        fold = (row_h == _iota((D_SSM, LANES), 1)).astype(f32)
        ddt_raw = _nn_hi(ddt_f, fold) * _sigmoid(pre)
        gdt_ref[0:1, :] += jnp.sum(ddt_raw, axis=0, keepdims=True)
        out_ref[:, D_SSM + D_CONV:D_SSM + D_CONV + LANES] = ddt_raw.astype(bf16)
        out_ref[:, D_SSM + D_CONV + LANES:] = jnp.zeros((CHUNK, 3 * LANES), bf16)

        dsil = sig * (1.0 + cv * (1.0 - sig))
        dcv_x = dx * dsil[:, 0:D_SSM]
        dcv_bc = dbc_scr[...] * dsil[:, D_SSM:]
        dcpad[0:CHUNK, 0:D_SSM] = dcv_x
        dcpad[0:CHUNK, D_SSM:] = dcv_bc
        dcpad[CHUNK:, :] = head_scr[...]
        dcv = dcpad[0:CHUNK, :]
        gconv_ref[4:5, :] += jnp.sum(dcv, axis=0, keepdims=True)
        draw = jnp.zeros((CHUNK, D_CONV), f32)
        for j in range(4):
            gconv_ref[j:j + 1, :] += jnp.sum(dcv * xpad[pl.ds(5 + j, CHUNK), :], axis=0, keepdims=True)
            draw = draw + cw_ref[j:j + 1, :] * dcpad[pl.ds(3 - j, CHUNK), :]
        head_scr[...] = dcpad[0:8, :]
        out_ref[:, D_SSM:D_SSM + D_CONV] = draw.astype(bf16)

    order = lambda i: nc - 1 - i
    row = lambda w, cb=0: pl.BlockSpec((CHUNK, w), lambda i: (nc - 1 - i, cb))
    return pl.pallas_call(
        body, name="ssd_bwd", grid=(nc,),
        in_specs=_ssd_in_specs(order) + [row(D_SSM), pl.BlockSpec((None, D_STATE, D_SSM), lambda i: (nc - 1 - i, 0, 0)),
                                         row(D_SSM, 1), _full((4, D_CONV)), _full((1, D_CONV)), _full((1, LANES)),
                                         _full((1, LANES)), _full((1, D_SSM)), _full((1, D_SSM)), _full((1, D_SSM))],
        out_specs=[row(3072), _full((8, D_CONV)), _full((8, D_SSM)), _full((8, LANES))],
        out_shape=[SDS((s, 3072), bf16), SDS((8, D_CONV), f32), SDS((8, D_SSM), f32), SDS((8, LANES), f32)],
        scratch_shapes=[pltpu.VMEM((D_STATE, D_SSM), f32), pltpu.VMEM((8, D_CONV), f32),
                        pltpu.VMEM((8 + CHUNK, D_CONV), f32), pltpu.VMEM((8 + CHUNK, D_CONV), f32),
                        pltpu.VMEM((CHUNK, D_SSM), f32), pltpu.VMEM((CHUNK, D_SSM), f32),
                        pltpu.VMEM((CHUNK, 2 * N_GROUPS * D_STATE), f32)],
        compiler_params=pltpu.CompilerParams(dimension_semantics=("arbitrary",)),
    )(proj, proj, proj, proj, proj, proj, y, states, dmix, conv_w, conv_b, dtb16, alog16, alog_f, d_f, nw)


def _inproj_bwd(dqkvg, dzxd, wp, x, nw, dres):
    s, d = x.shape
    tm, tk = 1024, 1024
    ka = dqkvg.shape[1] // tk
    nk = ka + dzxd.shape[1] // tk

    def body(da_ref, db_ref, w_ref, x_ref, nw_ref, dres_ref, gx_ref, gnw_ref, acc):
        i, k = pl.program_id(0), pl.program_id(1)

        @pl.when(jnp.logical_and(i == 0, k == 0))
        def _():
            gnw_ref[...] = jnp.zeros_like(gnw_ref)

        @pl.when(k == 0)
        def _():
            acc[...] = jnp.zeros_like(acc)

        @pl.when(k < ka)
        def _():
            acc[...] += _nt(da_ref[...], w_ref[...])

        @pl.when(k >= ka)
        def _():
            acc[...] += _nt(db_ref[...], w_ref[...])

        @pl.when(k == nk - 1)
        def _():
            xv = x_ref[...]
            r = lax.rsqrt(jnp.mean(xv * xv, axis=-1, keepdims=True) + EPS)
            xn = xv * r
            du = acc[...]
            gnw_ref[0:1, :] += jnp.sum(du * xn, axis=0, keepdims=True)
            dn = du * nw_ref[...]
            gx_ref[...] = dres_ref[...] + r * (dn - xn * jnp.mean(dn * xn, axis=-1, keepdims=True))

    return pl.pallas_call(
        body, name="inproj_bwd", grid=(s // tm, nk),
        in_specs=[pl.BlockSpec((tm, tk), lambda i, k: (i, jnp.minimum(k, ka - 1))),
                  pl.BlockSpec((tm, tk), lambda i, k: (i, jnp.maximum(k - ka, 0))),
                  pl.BlockSpec((d, tk), lambda i, k: (0, k)),
                  pl.BlockSpec((tm, d), lambda i, k: (i, 0)), pl.BlockSpec((1, d), lambda i, k: (0, 0)),
                  pl.BlockSpec((tm, d), lambda i, k: (i, 0))],
        out_specs=[pl.BlockSpec((tm, d), lambda i, k: (i, 0)), pl.BlockSpec((8, d), lambda i, k: (0, 0))],
        out_shape=[SDS((s, d), f32), SDS((8, d), f32)],
        scratch_shapes=[pltpu.VMEM((tm, d), f32)],
        compiler_params=pltpu.CompilerParams(dimension_semantics=("arbitrary", "arbitrary")),
    )(dqkvg, dzxd, wp, x, nw, dres)


def _matmul_tn(a, b, name):
    s, m = a.shape
    n = b.shape[1]
    tn, tk = 1024, 512
    nk = s // tk

    def body(a_ref, b_ref, o_ref):
        @pl.when(pl.program_id(1) == 0)
        def _():
            o_ref[...] = jnp.zeros_like(o_ref)
        o_ref[...] += _tn(a_ref[...], b_ref[...])

    return pl.pallas_call(
        body, name=name, grid=(n // tn, nk),
        in_specs=[pl.BlockSpec((tk, m), lambda j, k: (k, 0)), pl.BlockSpec((tk, tn), lambda j, k: (k, j))],
        out_specs=pl.BlockSpec((m, tn), lambda j, k: (0, j)),
        out_shape=SDS((m, n), f32),
        compiler_params=pltpu.CompilerParams(dimension_semantics=("parallel", "arbitrary")),
    )(a, b)


def _adamw(w, g, m, v):
    m = ADAM_B1 * m + (1.0 - ADAM_B1) * g
    v = ADAM_B2 * v + (1.0 - ADAM_B2) * (g * g)
    m_hat = m / (1.0 - ADAM_B1 ** ADAM_STEP)
    v_hat = v / (1.0 - ADAM_B2 ** ADAM_STEP)
    delta = -ADAM_LR * (m_hat / (jnp.sqrt(v_hat) + ADAM_EPS) + ADAM_WD * w)
    return delta, m, v


def _sum_adamw(parts, w, m, v, name):
    r, c = w.shape
    tr = 128

    def body(p_ref, w_ref, m_ref, v_ref, g_ref, d_ref, nm_ref, nv_ref):
        g = p_ref[0]
        for j in range(1, N_DEV):
            g = g + p_ref[j]
        g_ref[...] = g
        d_ref[...], nm_ref[...], nv_ref[...] = _adamw(w_ref[...], g, m_ref[...], v_ref[...])

    blk = pl.BlockSpec((tr, c), lambda i: (i, 0))
    return pl.pallas_call(
        body, name=name, grid=(r // tr,),
        in_specs=[pl.BlockSpec((N_DEV, tr, c), lambda i: (0, i, 0)), blk, blk, blk],
        out_specs=[blk] * 4, out_shape=[SDS((r, c), f32)] * 4,
        compiler_params=pltpu.CompilerParams(dimension_semantics=("parallel",)),
    )(parts, w, m, v)


def _sum_small(parts):
    def body(p_ref, o_ref):
        t = p_ref[0]
        for j in range(1, N_DEV):
            t = t + p_ref[j]
        o_ref[...] = t
        row_h = _iota((D_SSM, LANES), 0) // HEAD_DIM
        fold = (row_h == _iota((D_SSM, LANES), 1)).astype(f32)
        lower = t[8:16, 0:LANES]
        folded = _nn_hi(t[8:16, 0:D_SSM], fold)
        loss = jnp.sum(t[11:12, 0:D_MODEL], axis=1, keepdims=True) * (0.5 / D_MODEL)
        row = _iota((8, LANES), 0)
        o_ref[8:16, 0:LANES] = jnp.where(row < 2, folded, jnp.where(row == 4, loss, lower))

    return pl.pallas_call(body, name="sum_small", out_shape=SDS((PACK_ROWS, PACK_W), f32),
                          in_specs=[pl.BlockSpec(memory_space=pltpu.VMEM)],
                          out_specs=pl.BlockSpec(memory_space=pltpu.VMEM))(parts)


def _adamw_small(w, g, m, v):
    def body(w_ref, g_ref, m_ref, v_ref, d_ref, nm_ref, nv_ref):
        d_ref[...], nm_ref[...], nv_ref[...] = _adamw(w_ref[...], g_ref[...], m_ref[...], v_ref[...])

    vm = pl.BlockSpec(memory_space=pltpu.VMEM)
    return pl.pallas_call(body, name="adamw_small", out_shape=[SDS(w.shape, f32)] * 3,
                          in_specs=[vm] * 4, out_specs=[vm] * 3)(w, g, m, v)


def _pad_lanes(v, width):
    return jnp.pad(v, ((0, 0), (0, width - v.shape[1])))


def _local_step(x, tgt, norm_pre_w, wp, conv_w, conv_b, dt_bias, a_log, d_skip, ssm_norm_w, wo, norm_post_w):
    dtb16 = _pad_lanes(dt_bias, LANES)
    alog16 = _pad_lanes(a_log, LANES)
    alog_f = jnp.repeat(a_log, HEAD_DIM, axis=1)
    d_f = jnp.repeat(d_skip, HEAD_DIM, axis=1)

    proj, u = _prenorm_inproj(x, norm_pre_w, wp)
    o, lb, mix_a = _attn_fwd(proj)
    mix_s, y, states = _ssd_fwd(proj, conv_w, conv_b, dtb16, alog16, alog_f, d_f, ssm_norm_w)
    dmix, dout, dres, acc_post = _outproj_loss(mix_a, mix_s, wo, x, tgt, norm_post_w)
    dq, dk, dv, dg = _attn_bwd(proj, o, lb, dmix)
    dqkvg = jnp.concatenate([dq, dk, dv, dg], axis=1)
    dzxd, g_conv, g_vec, g_dt = _ssd_bwd(proj, y, states, dmix, conv_w, conv_b, dtb16, alog16, alog_f, d_f, ssm_norm_w)
    grad_x, g_pre = _inproj_bwd(dqkvg, dzxd, wp, x, norm_pre_w, dres)
    dw_in = jnp.concatenate([_matmul_tn(u, dqkvg, "dw_in_a"), _matmul_tn(u, dzxd, "dw_in_b")], axis=1)
    dw_out = jnp.concatenate([_matmul_tn(mix_a, dout, "dw_out_a"), _matmul_tn(mix_s, dout, "dw_out_b")], axis=0)

    rows = [g_conv[0:5], _pad_lanes(g_pre[0:1], PACK_W), _pad_lanes(g_vec[0:1], PACK_W),
            _pad_lanes(acc_post[1:2], PACK_W), _pad_lanes(g_vec[1:3], PACK_W), _pad_lanes(g_dt[0:1], PACK_W),
            _pad_lanes(acc_post[0:1], PACK_W), jnp.zeros((4, PACK_W), f32)]
    return grad_x, dw_in, dw_out, jnp.concatenate(rows, axis=0)


def kernel(x, norm_pre_w, w_in, conv_w, conv_b, dt_bias, a_log, d_skip, ssm_norm_w, w_out, norm_post_w, loss_target, m_norm_pre_w, m_w_in, m_conv_w, m_conv_b, m_dt_bias, m_a_log, m_d_skip, m_ssm_norm_w, m_w_out, m_norm_post_w, v_norm_pre_w, v_w_in, v_conv_w, v_conv_b, v_dt_bias, v_a_log, v_d_skip, v_ssm_norm_w, v_w_out, v_norm_post_w):
    shard_in = w_in.shape[2]
    shard_cv = conv_w.shape[2]
    me = 4 * lax.axis_index("x") + 2 * lax.axis_index("y") + lax.axis_index("c")

    g_in, g_out, g_cw = _all_gather([w_in[0].astype(bf16), w_out[0].astype(bf16), conv_w[0]])
    wp = jnp.pad(g_in.transpose(1, 0, 2).reshape(D_MODEL, N_DEV * shard_in), ((0, 0), (0, NP - N_DEV * shard_in)))
    wo = g_out.reshape(N_DEV * w_out.shape[1], D_MODEL)
    cw = g_cw.transpose(1, 0, 2).reshape(4, D_CONV)

    grad_x, dw_in, dw_out, pack = _local_step(
        x[0], loss_target[0], norm_pre_w, wp, cw, conv_b, dt_bias, a_log, d_skip, ssm_norm_w, wo, norm_post_w)

    send_in = dw_in[:, :N_DEV * shard_in].reshape(D_MODEL, N_DEV, shard_in).transpose(1, 0, 2)
    send_out = dw_out.reshape(N_DEV, w_out.shape[1], D_MODEL)
    parts_in, parts_out, parts_small = _exchange([send_in, send_out], [pack])

    g_w_in, d_w_in, nm_w_in, nv_w_in = _sum_adamw(parts_in, w_in[0], m_w_in[0], v_w_in[0], "sum_adamw_w_in")
    g_w_out, d_w_out, nm_w_out, nv_w_out = _sum_adamw(parts_out, w_out[0], m_w_out[0], v_w_out[0], "sum_adamw_w_out")
    tot = _sum_small(parts_small)

    g_cw_all = tot[0:4]
    small_g = {
        "conv_w": lax.dynamic_slice(g_cw_all, (0, me * shard_cv), (4, shard_cv)),
        "conv_b": tot[4:5], "norm_pre_w": tot[5:6, :D_MODEL], "ssm_norm_w": tot[6:7, :D_SSM],
        "norm_post_w": tot[7:8, :D_MODEL], "a_log": tot[8:9, :16], "d_skip": tot[9:10, :16], "dt_bias": tot[10:11, :16],
    }
    loss = tot[12, 0]
    small_w = {"conv_w": (conv_w[0], m_conv_w[0], v_conv_w[0]), "conv_b": (conv_b, m_conv_b, v_conv_b),
               "norm_pre_w": (norm_pre_w, m_norm_pre_w, v_norm_pre_w), "ssm_norm_w": (ssm_norm_w, m_ssm_norm_w, v_ssm_norm_w),
               "norm_post_w": (norm_post_w, m_norm_post_w, v_norm_post_w), "a_log": (a_log, m_a_log, v_a_log),
               "d_skip": (d_skip, m_d_skip, v_d_skip), "dt_bias": (dt_bias, m_dt_bias, v_dt_bias)}
    names = list(small_w)
    sizes = [small_g[k].size for k in names]
    tot_size = sum(sizes)
    pad_to = -(-tot_size // 1024) * 1024

    def flat(arrs):
        v = jnp.concatenate([a.reshape(-1) for a in arrs])
        return jnp.pad(v, (0, pad_to - tot_size)).reshape(pad_to // LANES, LANES)

    fw = flat([small_w[k][0] for k in names])
    fg = flat([small_g[k] for k in names])
    fm = flat([small_w[k][1] for k in names])
    fv = jnp.pad(jnp.concatenate([small_w[k][2].reshape(-1) for k in names]), (0, pad_to - tot_size),
                 constant_values=1.0).reshape(pad_to // LANES, LANES)
    fd, fnm, fnv = _adamw_small(fw, fg, fm, fv)

    def unflat(f):
        out, off = {}, 0
        v = f.reshape(-1)
        for k, n in zip(names, sizes):
            out[k] = v[off:off + n].reshape(small_g[k].shape)
            off += n
        return out

    sd, snm, snv = unflat(fd), unflat(fnm), unflat(fnv)
    lead = lambda a: a[None]
    order = ["norm_pre_w", "w_in", "conv_w", "conv_b", "dt_bias", "a_log", "d_skip", "ssm_norm_w", "w_out", "norm_post_w"]
    grads = dict(small_g, w_in=g_w_in, w_out=g_w_out)
    deltas = dict(sd, w_in=d_w_in, w_out=d_w_out)
    new_m = dict(snm, w_in=nm_w_in, w_out=nm_w_out)
    new_v = dict(snv, w_in=nv_w_in, w_out=nv_w_out)

    def shaped(dct, k):
        a = dct[k]
        return lead(a) if k in ("w_in", "w_out", "conv_w") else a

    return (loss, grad_x[None], *[shaped(grads, k) for k in order], *[shaped(deltas, k) for k in order],
            *[shaped(new_m, k) for k in order], *[shaped(new_v, k) for k in order])
```

```python
import functools
import math

import jax
import jax.numpy as jnp
import numpy as np
from jax import lax
from jax.experimental import pallas as pl
from jax.experimental.pallas import tpu as pltpu

f32, bf16 = jnp.float32, jnp.bfloat16
SDS = jax.ShapeDtypeStruct
HIGHEST = lax.Precision.HIGHEST
MESH = pl.DeviceIdType.MESH

N_DEV = 8
D_MODEL = 1024
D_ATTN = 1024
D_SSM = 1024
HEAD_DIM = 64
N_PAIRS = 8
D_STATE = 128
N_GROUPS = 2
D_CONV = D_SSM + 2 * N_GROUPS * D_STATE
D_IN_PROJ = 4 * D_ATTN + D_SSM + D_CONV + 16
NP = 7168
CHUNK = 128
BLK = 128
DILATIONS = (1, 4, 16)
EPS = 1e-6
LANES = 128
COL_Z, COL_XS, COL_BC, COL_DT = 4096, 5120, 6144, 6656

ADAM_LR, ADAM_B1, ADAM_B2, ADAM_EPS, ADAM_WD, ADAM_STEP = 0.001, 0.9, 0.999, 1e-08, 0.01, 10

PACK_ROWS, PACK_W = 16, 1536


def _nt(a, b):
    return lax.dot_general(a, b, (((1,), (1,)), ((), ())), preferred_element_type=f32)


def _tn(a, b):
    return lax.dot_general(a, b, (((0,), (0,)), ((), ())), preferred_element_type=f32)


def _nn(a, b):
    return jnp.dot(a, b, preferred_element_type=f32)


def _nn_hi(a, b):
    return jnp.dot(a, b, precision=HIGHEST, preferred_element_type=f32)


def _sigmoid(x):
    return 1.0 / (1.0 + jnp.exp(-x))


def _softplus(x):
    return jnp.maximum(x, 0.0) + jnp.log1p(jnp.exp(-jnp.abs(x)))


def _iota(shape, dim):
    return lax.broadcasted_iota(jnp.int32, shape, dim)


def _my_pos():
    return lax.axis_index("x"), lax.axis_index("y"), lax.axis_index("c")


def _all_gather(arrs):
    n = len(arrs)

    def body(*refs):
        ins, outs = refs[:n], refs[n:2 * n]
        send_sems, recv_sems, local_sems = refs[2 * n:]
        x, y, c = _my_pos()
        me, sibling = (x, y, c), (x, y, 1 - c)
        chips = [(1 - x, y), (x, 1 - y), (1 - x, 1 - y)]

        def slot(a, px, py, pc):
            return outs[a].at[4 * px + 2 * py + pc]

        def copy(a, k, block, to, src=None):
            return pltpu.make_async_remote_copy(
                src_ref=slot(a, *block) if src is None else src, dst_ref=slot(a, *block),
                send_sem=send_sems.at[7 * a + k], recv_sem=recv_sems.at[7 * a + k],
                device_id=to, device_id_type=MESH)

        mine = [pltpu.make_async_copy(ins[a], slot(a, *me), local_sems.at[a]) for a in range(n)]
        for cp in mine:
            cp.start()
        first = []
        for a in range(n):
            first.append(copy(a, 0, me, sibling, src=ins[a]))
            first += [copy(a, 1 + j, me, (*chip, c), src=ins[a]) for j, chip in enumerate(chips)]
        for cp in first:
            cp.start()
        passed = []
        for j, chip in enumerate(chips):
            for a in range(n):
                copy(a, 1 + j, (*chip, c), me).wait_recv()
                cp = copy(a, 4 + j, (*chip, c), sibling)
                cp.start()
                passed.append(cp)
        for a in range(n):
            copy(a, 0, sibling, me).wait_recv()
            for j, chip in enumerate(chips):
                copy(a, 4 + j, (*chip, 1 - c), me).wait_recv()
        for cp in first + passed:
            cp.wait_send()
        for cp in mine:
            cp.wait()

    anyspec = pl.BlockSpec(memory_space=pl.ANY)
    return pl.pallas_call(
        body, name="weights_all_gather",
        out_shape=[SDS((N_DEV,) + a.shape, a.dtype) for a in arrs],
        in_specs=[anyspec] * n, out_specs=[anyspec] * n,
        scratch_shapes=[pltpu.SemaphoreType.DMA((7 * n,)), pltpu.SemaphoreType.DMA((7 * n,)),
                        pltpu.SemaphoreType.DMA((n,))],
    )(*arrs)


def _sibling_swap(bigs, small):
    nb = len(bigs)

    def body(*refs):
        ins, small_in = refs[:nb], refs[nb]
        outs, small_out = refs[nb + 1:2 * nb + 1], refs[2 * nb + 1]
        send_sems, recv_sems, local_sem = refs[2 * nb + 2:]
        x, y, c = _my_pos()
        me = 4 * x + 2 * y + c
        mine = pltpu.make_async_copy(small_in, small_out.at[me], local_sem)
        mine.start()
        sends = []
        for a in range(nb):
            cp = pltpu.make_async_remote_copy(
                src_ref=ins[a].at[1 - c], dst_ref=outs[a], send_sem=send_sems.at[a], recv_sem=recv_sems.at[a],
                device_id=(x, y, 1 - c), device_id_type=MESH)
            cp.start()
            sends.append(cp)
        for k in range(1, N_DEV):
            to = (me + k) % N_DEV
            cp = pltpu.make_async_remote_copy(
                src_ref=small_in, dst_ref=small_out.at[me],
                send_sem=send_sems.at[nb + k - 1], recv_sem=recv_sems.at[nb + k - 1],
                device_id=(to // 4, (to // 2) % 2, to % 2), device_id_type=MESH)
            cp.start()
            sends.append(cp)
        for a in range(nb):
            pltpu.make_async_remote_copy(
                src_ref=ins[a].at[c], dst_ref=outs[a], send_sem=send_sems.at[a], recv_sem=recv_sems.at[a],
                device_id=(x, y, c), device_id_type=MESH).wait_recv()
        for k in range(1, N_DEV):
            frm = (me + N_DEV - k) % N_DEV
            pltpu.make_async_remote_copy(
                src_ref=small_in, dst_ref=small_out.at[frm],
                send_sem=send_sems.at[nb + k - 1], recv_sem=recv_sems.at[nb + k - 1],
                device_id=(x, y, c), device_id_type=MESH).wait_recv()
        for cp in sends:
            cp.wait_send()
        mine.wait()

    anyspec = pl.BlockSpec(memory_space=pl.ANY)
    out_shape = [SDS(a.shape[1:], a.dtype) for a in bigs] + [SDS((N_DEV,) + small.shape, small.dtype)]
    return pl.pallas_call(
        body, name="grad_sibling_swap", out_shape=out_shape,
        in_specs=[anyspec] * (nb + 1), out_specs=[anyspec] * (nb + 1),
        scratch_shapes=[pltpu.SemaphoreType.DMA((nb + 7,)), pltpu.SemaphoreType.DMA((nb + 7,)),
                        pltpu.SemaphoreType.DMA(())],
    )(*bigs, small)


def _chip_sum(mine, got, name):
    _, nq, r, cdim = mine.shape
    tr = 128

    def body(m_ref, g_ref, s16_ref, own_ref):
        q = pl.program_id(1)
        c = lax.axis_index("c")
        my_q = 2 * lax.axis_index("x") + lax.axis_index("y")
        tot = m_ref[c] + g_ref[...].astype(f32)
        s16_ref[...] = tot.astype(bf16)

        @pl.when(q == my_q)
        def _():
            own_ref[...] = tot

    return pl.pallas_call(
        body, name=name, grid=(r // tr, nq),
        in_specs=[pl.BlockSpec((2, None, tr, cdim), lambda i, q: (0, q, i, 0)),
                  pl.BlockSpec((None, tr, cdim), lambda i, q: (q, i, 0))],
        out_specs=[pl.BlockSpec((None, tr, cdim), lambda i, q: (q, i, 0)), pl.BlockSpec((tr, cdim), lambda i, q: (i, 0))],
        out_shape=[SDS((nq, r, cdim), bf16), SDS((r, cdim), f32)],
        compiler_params=pltpu.CompilerParams(dimension_semantics=("parallel", "arbitrary")),
    )(mine, got)


def _chip_exchange(bigs):
    nb = len(bigs)

    def body(*refs):
        ins, outs = refs[:nb], refs[nb:2 * nb]
        send_sems, recv_sems, local_sems = refs[2 * nb:]
        x, y, c = _my_pos()
        my_q = 2 * x + y
        mine = [pltpu.make_async_copy(ins[a].at[my_q], outs[a].at[my_q], local_sems.at[a]) for a in range(nb)]
        for cp in mine:
            cp.start()
        sends = []
        for k in range(1, 4):
            to = (my_q + k) % 4
            for a in range(nb):
                cp = pltpu.make_async_remote_copy(
                    src_ref=ins[a].at[to], dst_ref=outs[a].at[my_q],
                    send_sem=send_sems.at[3 * a + k - 1], recv_sem=recv_sems.at[3 * a + k - 1],
                    device_id=(to // 2, to % 2, c), device_id_type=MESH)
                cp.start()
                sends.append(cp)
        for k in range(1, 4):
            frm = (my_q + 4 - k) % 4
            for a in range(nb):
                pltpu.make_async_remote_copy(
                    src_ref=ins[a].at[frm], dst_ref=outs[a].at[frm],
                    send_sem=send_sems.at[3 * a + k - 1], recv_sem=recv_sems.at[3 * a + k - 1],
                    device_id=(x, y, c), device_id_type=MESH).wait_recv()
        for cp in sends:
            cp.wait_send()
        for cp in mine:
            cp.wait()

    anyspec = pl.BlockSpec(memory_space=pl.ANY)
    return pl.pallas_call(
        body, name="grad_chip_exchange", out_shape=[SDS(a.shape, a.dtype) for a in bigs],
        in_specs=[anyspec] * nb, out_specs=[anyspec] * nb,
        scratch_shapes=[pltpu.SemaphoreType.DMA((3 * nb,)), pltpu.SemaphoreType.DMA((3 * nb,)),
                        pltpu.SemaphoreType.DMA((nb,))],
    )(*bigs)


def _prenorm_inproj(x, nw, wp):
    s, d = x.shape
    npad = wp.shape[1]
    tm, tn = 1024, 512

    def body(x_ref, nw_ref, w_ref, proj_ref, u_ref):
        @pl.when(pl.program_id(1) == 0)
        def _():
            xv = x_ref[...]
            r = lax.rsqrt(jnp.mean(xv * xv, axis=-1, keepdims=True) + EPS)
            u_ref[...] = (xv * r * nw_ref[...]).astype(bf16)
        proj_ref[...] = _nn(u_ref[...], w_ref[...])

    return pl.pallas_call(
        body, name="prenorm_inproj", grid=(s // tm, npad // tn),
        in_specs=[pl.BlockSpec((tm, d), lambda i, j: (i, 0)), pl.BlockSpec((1, d), lambda i, j: (0, 0)),
                  pl.BlockSpec((d, tn), lambda i, j: (0, j))],
        out_specs=[pl.BlockSpec((tm, tn), lambda i, j: (i, j)), pl.BlockSpec((tm, d), lambda i, j: (i, 0))],
        out_shape=[SDS((s, npad), f32), SDS((s, d), bf16)],
        compiler_params=pltpu.CompilerParams(dimension_semantics=("parallel", "arbitrary")),
    )(x, nw, wp)


def _attn_masks():
    lane = _iota((BLK, LANES), 1)
    sub = _iota((BLK, LANES), 0)
    head0 = lane < HEAD_DIM
    cur_ok = sub >= lane
    prev_ok = lane >= sub
    return head0, cur_ok, prev_ok


def _attn_fwd(proj):
    s = proj.shape[0]
    n_it = s // BLK

    def body(q_ref, k_ref, v_ref, g_ref, o_ref, l_ref, mix_ref, op0, op1, op2, lp0, lp1, lp2):
        op_refs, lp_refs = (op0, op1, op2), (lp0, lp1, lp2)
        head0, cur_ok, prev_ok0 = _attn_masks()
        for p, d in enumerate(DILATIONS):
            nb = s // (BLK * d)

            def it(i, carry, d=d, nb=nb, p=p):
                r, blk = i // nb, i % nb
                st = blk * (BLK * d) + r
                stp = jnp.maximum(blk - 1, 0) * (BLK * d) + r
                rows = pl.ds(st, BLK, stride=d)
                rows_p = pl.ds(stp, BLK, stride=d)
                qs = q_ref[rows, :] * 0.125
                kc, kp = k_ref[rows, :].astype(bf16), k_ref[rows_p, :].astype(bf16)
                vc, vp = v_ref[rows, :].astype(bf16), v_ref[rows_p, :].astype(bf16)
                prev_ok = jnp.logical_and(prev_ok0, blk > 0)
                outs, lses = [], []
                for h in range(2):
                    mh = head0 if h == 0 else jnp.logical_not(head0)
                    qh = jnp.where(mh, qs, 0.0).astype(bf16)
                    sc = jnp.where(cur_ok, _nt(qh, kc), -jnp.inf)
                    sp = jnp.where(prev_ok, _nt(qh, kp), -jnp.inf)
                    m = jnp.maximum(jnp.max(sc, axis=1, keepdims=True), jnp.max(sp, axis=1, keepdims=True))
                    pc, pp = jnp.exp(sc - m), jnp.exp(sp - m)
                    l = jnp.sum(pc, axis=1, keepdims=True) + jnp.sum(pp, axis=1, keepdims=True)
                    o = _nn(pc.astype(bf16), vc) + _nn(pp.astype(bf16), vp)
                    outs.append(o / l)
                    lses.append(m + jnp.log(l))
                op_refs[p][rows, :] = jnp.where(head0, outs[0], outs[1])
                lp_refs[p][rows, :] = jnp.where(head0, lses[0], lses[1])
                return carry

            lax.fori_loop(0, n_it, it, 0, unroll=2)

        def merge(i, carry):
            rows = pl.ds(pl.multiple_of(i * 256, 256), 256)
            l0, l1, l2 = lp0[rows, :], lp1[rows, :], lp2[rows, :]
            m = jnp.maximum(jnp.maximum(l0, l1), l2)
            e0, e1, e2 = jnp.exp(l0 - m), jnp.exp(l1 - m), jnp.exp(l2 - m)
            z = e0 + e1 + e2
            o = (e0 * op0[rows, :] + e1 * op1[rows, :] + e2 * op2[rows, :]) / z
            o_ref[rows, :] = o
            l_ref[rows, :] = m + jnp.log(z)
            g = g_ref[rows, :]
            mix_ref[rows, :] = (o * (g * _sigmoid(g))).astype(bf16)
            return carry

        lax.fori_loop(0, s // 256, merge, 0)

    col = lambda base: pl.BlockSpec((s, LANES), lambda h: (0, base + h))
    return pl.pallas_call(
        body, name="attn_fwd", grid=(N_PAIRS,),
        in_specs=[col(0), col(8), col(16), col(24)],
        out_specs=[col(0), col(0), col(0)],
        out_shape=[SDS((s, D_ATTN), f32), SDS((s, D_ATTN), f32), SDS((s, D_ATTN), bf16)],
        scratch_shapes=[pltpu.VMEM((s, LANES), f32)] * 6,
        compiler_params=pltpu.CompilerParams(dimension_semantics=("parallel",)),
    )(proj, proj, proj, proj)


def _expand_mat():
    row = _iota((LANES, 2 * D_SSM), 0)
    colv = _iota((LANES, 2 * D_SSM), 1)
    head = 2 * ((colv % D_SSM) // LANES) + colv // D_SSM
    return (row == head).astype(f32)


def _ssd_common(xs_ref, bc_ref, xs_tail, bc_tail, dt_ref, cw_ref, cb_ref, dtb_ref, alog16_ref, xpad, first):
    keep = jnp.where(first, 0.0, 1.0)
    xpad[0:8, 0:D_SSM] = xs_tail[...] * keep
    xpad[0:8, D_SSM:D_CONV] = bc_tail[...] * keep
    xpad[8:8 + CHUNK, 0:D_SSM] = xs_ref[...]
    xpad[8:8 + CHUNK, D_SSM:D_CONV] = bc_ref[...]
    cv = cb_ref[...] + cw_ref[0:1, :] * xpad[pl.ds(5, CHUNK), :]
    for j in range(1, 4):
        cv = cv + cw_ref[j:j + 1, :] * xpad[pl.ds(5 + j, CHUNK), :]
    sig = _sigmoid(cv)
    xbc = cv * sig

    pre = dt_ref[...] + dtb_ref[...]
    dt16 = _softplus(pre)
    a16 = -jnp.exp(alog16_ref[...])
    sub, lane = _iota((CHUNK, CHUNK), 0), _iota((CHUNK, CHUNK), 1)
    tri = (sub >= lane).astype(f32)
    al16 = _nn_hi(tri, dt16 * a16)
    al_t = al16.T
    emat = _expand_mat()
    dt_x = _nn_hi(dt16, emat)
    al_x = _nn_hi(al16, emat)
    lane_w = _iota((CHUNK, D_SSM), 1)
    even = (lane_w % LANES) < HEAD_DIM
    dt_f = jnp.where(even, dt_x[:, :D_SSM], dt_x[:, D_SSM:])
    al_f = jnp.where(even, al_x[:, :D_SSM], al_x[:, D_SSM:])
    return cv, sig, xbc, pre, dt_f, al_f, al_x, al_t


def _decay_mat(al_x, al_t, pair, h):
    sub, lane = _iota((CHUNK, CHUNK), 0), _iota((CHUNK, CHUNK), 1)
    col = al_x[:, h * D_SSM + pair * LANES: h * D_SSM + (pair + 1) * LANES]
    row = al_t[2 * pair + h: 2 * pair + h + 1, :]
    return jnp.exp(jnp.where(sub >= lane, col - row, -jnp.inf))


def _ssd_in_specs(order):
    blk = lambda w, cb: pl.BlockSpec((CHUNK, w), lambda i: (order(i), cb))
    tail = lambda w, cb: pl.BlockSpec((8, w), lambda i: (jnp.maximum(16 * order(i) - 1, 0), cb))
    return [blk(D_SSM, COL_XS // D_SSM), blk(512, COL_BC // 512), tail(D_SSM, COL_XS // D_SSM),
            tail(512, COL_BC // 512), blk(LANES, COL_DT // LANES), blk(D_SSM, COL_Z // D_SSM)]


def _full(shape):
    return pl.BlockSpec(shape, lambda i: (0,) * len(shape))


def _ssd_fwd(proj, conv_w, conv_b, dtb16, alog16, alog_f, d_f, nw):
    s = proj.shape[0]
    nc = s // CHUNK

    def body(xs_ref, bc_ref, xs_tail, bc_tail, dt_ref, z_ref, cw_ref, cb_ref, dtb_ref, alog16_ref, alogf_ref,
             df_ref, nw_ref, mix_ref, y_ref, st_ref, h_scr, xpad, y_scr):
        c = pl.program_id(0)

        @pl.when(c == 0)
        def _():
            h_scr[...] = jnp.zeros_like(h_scr)

        _, _, xbc, _, dt_f, al_f, al_x, al_t = _ssd_common(
            xs_ref, bc_ref, xs_tail, bc_tail, dt_ref, cw_ref, cb_ref, dtb_ref, alog16_ref, xpad, c == 0)
        head0 = _iota((CHUNK, LANES), 1) < HEAD_DIM
        st_ref[...] = h_scr[...]
        for g in range(N_GROUPS):
            bm = xbc[:, D_SSM + g * D_STATE: D_SSM + (g + 1) * D_STATE].astype(bf16)
            cm = xbc[:, D_SSM + (N_GROUPS + g) * D_STATE: D_SSM + (N_GROUPS + g + 1) * D_STATE].astype(bf16)
            gmat = _nt(cm, bm)
            for pair in range(4 * g, 4 * g + 4):
                sl = slice(pair * LANES, (pair + 1) * LANES)
                xp, dtp, alp = xbc[:, sl], dt_f[:, sl], al_f[:, sl]
                xdt = xp * dtp
                xdt16 = xdt.astype(bf16)
                al_last = alp[CHUNK - 1:CHUNK, :]
                hp = h_scr[:, sl]
                y_off = jnp.exp(alp) * _nn(cm, hp.astype(bf16))
                yd = [_nn((gmat * _decay_mat(al_x, al_t, pair, h)).astype(bf16), xdt16) for h in range(2)]
                y_scr[:, sl] = jnp.where(head0, yd[0], yd[1]) + y_off + df_ref[:, sl] * xp
                st = _tn(bm, (jnp.exp(al_last - alp) * xdt).astype(bf16))
                h_scr[:, sl] = jnp.exp(al_last) * hp + st
        y = y_scr[...]
        y_ref[...] = y
        z = z_ref[...]
        yz = y * (z * _sigmoid(z))
        gw = D_SSM // N_GROUPS
        for g in range(N_GROUPS):
            part = yz[:, g * gw:(g + 1) * gw]
            r = lax.rsqrt(jnp.mean(part * part, axis=-1, keepdims=True) + EPS)
            mix_ref[:, g * gw:(g + 1) * gw] = (part * r * nw_ref[:, g * gw:(g + 1) * gw]).astype(bf16)

    order = lambda i: i
    row = lambda w: pl.BlockSpec((CHUNK, w), lambda i: (i, 0))
    return pl.pallas_call(
        body, name="ssd_fwd", grid=(nc,),
        in_specs=_ssd_in_specs(order) + [_full((4, D_CONV)), _full((1, D_CONV)), _full((1, LANES)), _full((1, LANES)),
                                         _full((1, D_SSM)), _full((1, D_SSM)), _full((1, D_SSM))],
        out_specs=[row(D_SSM), row(D_SSM), pl.BlockSpec((None, D_STATE, D_SSM), lambda i: (i, 0, 0))],
        out_shape=[SDS((s, D_SSM), bf16), SDS((s, D_SSM), f32), SDS((nc, D_STATE, D_SSM), f32)],
        scratch_shapes=[pltpu.VMEM((D_STATE, D_SSM), f32), pltpu.VMEM((8 + CHUNK, D_CONV), f32),
                        pltpu.VMEM((CHUNK, D_SSM), f32)],
        compiler_params=pltpu.CompilerParams(dimension_semantics=("arbitrary",)),
    )(proj, proj, proj, proj, proj, proj, conv_w, conv_b, dtb16, alog16, alog_f, d_f, nw)


def _outproj_loss(mix_a, mix_s, wo, x, tgt, npw):
    s, d = x.shape
    tm = 512

    def body(ma_ref, ms_ref, wo_ref, x_ref, t_ref, npw_ref, dmix_ref, dout_ref, dres_ref, acc_ref):
        @pl.when(pl.program_id(0) == 0)
        def _():
            acc_ref[...] = jnp.zeros_like(acc_ref)

        out = _nn(ma_ref[...], wo_ref[0:D_ATTN, :]) + _nn(ms_ref[...], wo_ref[D_ATTN:, :])
        r = lax.rsqrt(jnp.mean(out * out, axis=-1, keepdims=True) + EPS)
        on = out * r
        diff = x_ref[...] + on * npw_ref[...] - t_ref[...]
        dres = diff * (1.0 / d)
        dres_ref[...] = dres
        acc_ref[0:1, :] += jnp.sum(diff * diff, axis=0, keepdims=True)
        acc_ref[1:2, :] += jnp.sum(dres * on, axis=0, keepdims=True)
        dn = dres * npw_ref[...]
        dout = (r * (dn - on * jnp.mean(dn * on, axis=-1, keepdims=True))).astype(bf16)
        dout_ref[...] = dout
        dmix_ref[...] = _nt(dout, wo_ref[...])

    row = lambda w: pl.BlockSpec((tm, w), lambda i: (i, 0))
    return pl.pallas_call(
        body, name="outproj_loss", grid=(s // tm,),
        in_specs=[row(D_ATTN), row(D_SSM), _full((D_ATTN + D_SSM, d)), row(d), row(d), _full((1, d))],
        out_specs=[row(D_ATTN + D_SSM), row(d), row(d), _full((8, d))],
        out_shape=[SDS((s, D_ATTN + D_SSM), f32), SDS((s, d), bf16), SDS((s, d), f32), SDS((8, d), f32)],
        compiler_params=pltpu.CompilerParams(dimension_semantics=("arbitrary",)),
    )(mix_a, mix_s, wo, x, tgt, npw)


def _attn_bwd(proj, o, lb, dmix):
    s = proj.shape[0]
    n_it = s // BLK

    def body(q_ref, k_ref, v_ref, g_ref, o_ref, l_ref, dm_ref, dq_ref, dk_ref, dv_ref, dg_ref,
             dq_acc, dk_acc, dv_acc, do_scr, dl_scr):
        head0, cur_ok, prev_ok0 = _attn_masks()
        head1 = jnp.logical_not(head0)

        def pro(i, carry):
            rows = pl.ds(pl.multiple_of(i * 256, 256), 256)
            g = g_ref[rows, :]
            sg = _sigmoid(g)
            dmx = dm_ref[rows, :]
            ov = o_ref[rows, :]
            dg_ref[rows, :] = (dmx * ov * (sg * (1.0 + g * (1.0 - sg)))).astype(bf16)
            do = dmx * (g * sg)
            do_scr[rows, :] = do
            prod = do * ov
            m0 = _iota(prod.shape, 1) < HEAD_DIM
            d0 = jnp.sum(jnp.where(m0, prod, 0.0), axis=1, keepdims=True)
            d1 = jnp.sum(jnp.where(m0, 0.0, prod), axis=1, keepdims=True)
            dl_scr[rows, :] = jnp.where(m0, d0, d1)
            z = jnp.zeros((256, LANES), f32)
            dq_acc[rows, :] = z
            dk_acc[rows, :] = z
            dv_acc[rows, :] = z
            return carry

        lax.fori_loop(0, s // 256, pro, 0)

        for d in DILATIONS:
            nb = s // (BLK * d)

            def it(i, carry, d=d, nb=nb):
                r, blk = i // nb, i % nb
                st = blk * (BLK * d) + r
                stp = jnp.maximum(blk - 1, 0) * (BLK * d) + r
                rows = pl.ds(st, BLK, stride=d)
                rows_p = pl.ds(stp, BLK, stride=d)
                q = q_ref[rows, :]
                kc, kp = k_ref[rows, :], k_ref[rows_p, :]
                vc, vp = v_ref[rows, :].astype(bf16), v_ref[rows_p, :].astype(bf16)
                kc16, kp16 = kc.astype(bf16), kp.astype(bf16)
                do = do_scr[rows, :]
                lse = l_ref[rows, :]
                dl = dl_scr[rows, :]
                prev_ok = jnp.logical_and(prev_ok0, blk > 0)
                dq = jnp.zeros((BLK, LANES), f32)
                dkc = jnp.zeros((BLK, LANES), f32)
                dkp = jnp.zeros((BLK, LANES), f32)
                dvc = jnp.zeros((BLK, LANES), f32)
                dvp = jnp.zeros((BLK, LANES), f32)
                for h in range(2):
                    mh = head0 if h == 0 else head1
                    lo = h * HEAD_DIM
                    qh = jnp.where(mh, q, 0.0).astype(bf16)
                    qsh = jnp.where(mh, q * 0.125, 0.0).astype(bf16)
                    doh = jnp.where(mh, do, 0.0).astype(bf16)
                    lh = lse[:, lo:lo + 1]
                    dlh = dl[:, lo:lo + 1]
                    pc = jnp.exp(jnp.where(cur_ok, _nt(qsh, kc16), -jnp.inf) - lh)
                    pp = jnp.exp(jnp.where(prev_ok, _nt(qsh, kp16), -jnp.inf) - lh)
                    dsc = (pc * (_nt(doh, vc) - dlh) * 0.125).astype(bf16)
                    dsp = (pp * (_nt(doh, vp) - dlh) * 0.125).astype(bf16)
                    dvc = dvc + _tn(pc.astype(bf16), doh)
                    dvp = dvp + _tn(pp.astype(bf16), doh)
                    dq = dq + _nn(dsc, jnp.where(mh, kc, 0.0).astype(bf16)) + _nn(dsp, jnp.where(mh, kp, 0.0).astype(bf16))
                    dkc = dkc + _tn(dsc, qh)
                    dkp = dkp + _tn(dsp, qh)
                dq_acc[rows, :] += dq
                dk_acc[rows, :] += dkc
                dv_acc[rows, :] += dvc
                dk_acc[rows_p, :] += dkp
                dv_acc[rows_p, :] += dvp
                return carry

            lax.fori_loop(0, n_it, it, 0, unroll=2)

        def epi(i, carry):
            rows = pl.ds(pl.multiple_of(i * 256, 256), 256)
            dq_ref[rows, :] = dq_acc[rows, :].astype(bf16)
            dk_ref[rows, :] = dk_acc[rows, :].astype(bf16)
            dv_ref[rows, :] = dv_acc[rows, :].astype(bf16)
            return carry

        lax.fori_loop(0, s // 256, epi, 0)

    col = lambda base: pl.BlockSpec((s, LANES), lambda h: (0, base + h))
    outs = pl.pallas_call(
        body, name="attn_bwd", grid=(N_PAIRS,),
        in_specs=[col(0), col(8), col(16), col(24), col(0), col(0), col(0)],
        out_specs=[col(0)] * 4,
        out_shape=[SDS((s, D_ATTN), bf16)] * 4,
        scratch_shapes=[pltpu.VMEM((s, LANES), f32)] * 5,
        compiler_params=pltpu.CompilerParams(dimension_semantics=("parallel",)),
    )(proj, proj, proj, proj, o, lb, dmix)
    return outs


def _ssd_bwd(proj, y, states, dmix, conv_w, conv_b, dtb16, alog16, alog_f, d_f, nw):
    s = proj.shape[0]
    nc = s // CHUNK
    gw = D_SSM // N_GROUPS

    def body(xs_ref, bc_ref, xs_tail, bc_tail, dt_ref, z_ref, y_ref, st_ref, dm_ref, cw_ref, cb_ref, dtb_ref,
             alog16_ref, alogf_ref, df_ref, nw_ref, out_ref, gconv_ref, gvec_ref, gdt_ref,
             dh_scr, head_scr, xpad, dcpad, da_scr, dxdt_scr, dbc_scr):
        i = pl.program_id(0)
        c = nc - 1 - i

        @pl.when(i == 0)
        def _():
            dh_scr[...] = jnp.zeros_like(dh_scr)
            head_scr[...] = jnp.zeros_like(head_scr)
            gconv_ref[...] = jnp.zeros_like(gconv_ref)
            gvec_ref[...] = jnp.zeros_like(gvec_ref)
            gdt_ref[...] = jnp.zeros_like(gdt_ref)

        cv, sig, xbc, pre, dt_f, al_f, al_x, al_t = _ssd_common(
            xs_ref, bc_ref, xs_tail, bc_tail, dt_ref, cw_ref, cb_ref, dtb_ref, alog16_ref, xpad, c == 0)
        head0 = _iota((CHUNK, LANES), 1) < HEAD_DIM
        sub = _iota((CHUNK, LANES), 0)
        last_row = sub == CHUNK - 1

        yv, z, dmx = y_ref[...], z_ref[...], dm_ref[...]
        sz = _sigmoid(z)
        silu = z * sz
        yz = yv * silu
        dyz_parts = []
        for g in range(N_GROUPS):
            gs = slice(g * gw, (g + 1) * gw)
            part = yz[:, gs]
            r = lax.rsqrt(jnp.mean(part * part, axis=-1, keepdims=True) + EPS)
            nh = part * r
            gvec_ref[0:1, gs] += jnp.sum(dmx[:, gs] * nh, axis=0, keepdims=True)
            dn = dmx[:, gs] * nw_ref[:, gs]
            dyz_parts.append(r * (dn - nh * jnp.mean(dn * nh, axis=-1, keepdims=True)))
        dyz = jnp.concatenate(dyz_parts, axis=1)
        dy = dyz * silu
        out_ref[:, 0:D_SSM] = (dyz * yv * (sz * (1.0 + z * (1.0 - sz)))).astype(bf16)

        x_all = xbc[:, 0:D_SSM]
        gvec_ref[2:3, :] += jnp.sum(dy * x_all, axis=0, keepdims=True)

        for g in range(N_GROUPS):
            bm = xbc[:, D_SSM + g * D_STATE: D_SSM + (g + 1) * D_STATE].astype(bf16)
            cm = xbc[:, D_SSM + (N_GROUPS + g) * D_STATE: D_SSM + (N_GROUPS + g + 1) * D_STATE].astype(bf16)
            gmat = _nt(cm, bm)
            dgm = jnp.zeros((CHUNK, CHUNK), f32)
            db = jnp.zeros((CHUNK, D_STATE), f32)
            dc = jnp.zeros((CHUNK, D_STATE), f32)
            for pair in range(4 * g, 4 * g + 4):
                sl = slice(pair * LANES, (pair + 1) * LANES)
                xp, dtp, alp, dyp = x_all[:, sl], dt_f[:, sl], al_f[:, sl], dy[:, sl]
                xdt = xp * dtp
                xdt16 = xdt.astype(bf16)
                al_last = alp[CHUNK - 1:CHUNK, :]
                e_l = jnp.exp(alp)
                wf = jnp.exp(al_last - alp)
                e_last = jnp.exp(al_last)
                hp = st_ref[:, sl]
                hp16 = hp.astype(bf16)
                dhn = dh_scr[:, sl]
                dhn16 = dhn.astype(bf16)
                y_off = e_l * _nn(cm, hp16)
                dch16 = (dyp * e_l).astype(bf16)
                dc = dc + _nt(dch16, hp16)
                dh_out = _tn(cm, dch16)
                dal = dyp * y_off
                xw16 = (wf * xdt).astype(bf16)
                db = db + _nt(xw16, dhn16)
                dxw = _nn(bm, dhn16)
                dxdt = dxw * wf
                dwf = dxw * xdt * wf
                dal = dal - dwf
                dal_last = jnp.sum(dwf, axis=0, keepdims=True) + jnp.sum(dhn * hp, axis=0, keepdims=True) * e_last
                dh_scr[:, sl] = e_last * dhn + dh_out
                for h in range(2):
                    mh = head0 if h == 0 else jnp.logical_not(head0)
                    dyh16 = jnp.where(mh, dyp, 0.0).astype(bf16)
                    lmat = _decay_mat(al_x, al_t, pair, h)
                    mm = gmat * lmat
                    dmm = _nt(dyh16, xdt16)
                    dxdt = dxdt + _tn(mm.astype(bf16), dyh16)
                    n16 = (dmm * mm).astype(bf16)
                    jh = jnp.where(mh, 1.0 / HEAD_DIM, 0.0).astype(bf16)
                    dal = dal + _nn(n16, jh) - _tn(n16, jh)
                    dgm = dgm + dmm * lmat
                da_scr[:, sl] = dal + jnp.where(last_row, dal_last, 0.0)
                dxdt_scr[:, sl] = dxdt
            dgm16 = dgm.astype(bf16)
            dbc_scr[:, g * D_STATE:(g + 1) * D_STATE] = db + _tn(dgm16, cm)
            dbc_scr[:, (N_GROUPS + g) * D_STATE:(N_GROUPS + g + 1) * D_STATE] = dc + _nn(dgm16, bm)

        sub_c, lane_c = _iota((CHUNK, CHUNK), 0), _iota((CHUNK, CHUNK), 1)
        tri_t = (lane_c >= sub_c).astype(f32)
        dadt = _nn_hi(tri_t, da_scr[...])
        a_f = -jnp.exp(alogf_ref[...])
        dxdt_all = dxdt_scr[...]
        ddt_f = dxdt_all * x_all + a_f * dadt
        gvec_ref[1:2, :] += jnp.sum(dt_f * dadt, axis=0, keepdims=True) * a_f
        dx = df_ref[...] * dy + dxdt_all * dt_f
        row_h = _iota((D_SSM, LANES), 0) // HEAD_DIM
        fold = (row_h == _iota((D_SSM, LANES), 1)).astype(f32)
        ddt_raw = _nn_hi(ddt_f, fold) * _sigmoid(pre)
        gdt_ref[0:1, :] += jnp.sum(ddt_raw, axis=0, keepdims=True)
        out_ref[:, D_SSM + D_CONV:D_SSM + D_CONV + LANES] = ddt_raw.astype(bf16)
        out_ref[:, D_SSM + D_CONV + LANES:] = jnp.zeros((CHUNK, 3 * LANES), bf16)

        dsil = sig * (1.0 + cv * (1.0 - sig))
        dcv_x = dx * dsil[:, 0:D_SSM]
        dcv_bc = dbc_scr[...] * dsil[:, D_SSM:]
        dcpad[0:CHUNK, 0:D_SSM] = dcv_x
        dcpad[0:CHUNK, D_SSM:] = dcv_bc
        dcpad[CHUNK:, :] = head_scr[...]
        dcv = dcpad[0:CHUNK, :]
        gconv_ref[4:5, :] += jnp.sum(dcv, axis=0, keepdims=True)
        draw = jnp.zeros((CHUNK, D_CONV), f32)
        for j in range(4):
            gconv_ref[j:j + 1, :] += jnp.sum(dcv * xpad[pl.ds(5 + j, CHUNK), :], axis=0, keepdims=True)
            draw = draw + cw_ref[j:j + 1, :] * dcpad[pl.ds(3 - j, CHUNK), :]
        head_scr[...] = dcpad[0:8, :]
        out_ref[:, D_SSM:D_SSM + D_CONV] = draw.astype(bf16)

    order = lambda i: nc - 1 - i
    row = lambda w, cb=0: pl.BlockSpec((CHUNK, w), lambda i: (nc - 1 - i, cb))
    return pl.pallas_call(
        body, name="ssd_bwd", grid=(nc,),
        in_specs=_ssd_in_specs(order) + [row(D_SSM), pl.BlockSpec((None, D_STATE, D_SSM), lambda i: (nc - 1 - i, 0, 0)),
                                         row(D_SSM, 1), _full((4, D_CONV)), _full((1, D_CONV)), _full((1, LANES)),
                                         _full((1, LANES)), _full((1, D_SSM)), _full((1, D_SSM)), _full((1, D_SSM))],
        out_specs=[row(3072), _full((8, D_CONV)), _full((8, D_SSM)), _full((8, LANES))],
        out_shape=[SDS((s, 3072), bf16), SDS((8, D_CONV), f32), SDS((8, D_SSM), f32), SDS((8, LANES), f32)],
        scratch_shapes=[pltpu.VMEM((D_STATE, D_SSM), f32), pltpu.VMEM((8, D_CONV), f32),
                        pltpu.VMEM((8 + CHUNK, D_CONV), f32), pltpu.VMEM((8 + CHUNK, D_CONV), f32),
                        pltpu.VMEM((CHUNK, D_SSM), f32), pltpu.VMEM((CHUNK, D_SSM), f32),
                        pltpu.VMEM((CHUNK, 2 * N_GROUPS * D_STATE), f32)],
        compiler_params=pltpu.CompilerParams(dimension_semantics=("arbitrary",)),
    )(proj, proj, proj, proj, proj, proj, y, states, dmix, conv_w, conv_b, dtb16, alog16, alog_f, d_f, nw)


def _inproj_bwd(dqkvg, dzxd, wp, x, nw, dres):
    s, d = x.shape
    tm, tk = 1024, 1024
    ka = dqkvg.shape[1] // tk
    nk = ka + dzxd.shape[1] // tk

    def body(da_ref, db_ref, w_ref, x_ref, nw_ref, dres_ref, gx_ref, gnw_ref, acc):
        i, k = pl.program_id(0), pl.program_id(1)

        @pl.when(jnp.logical_and(i == 0, k == 0))
        def _():
            gnw_ref[...] = jnp.zeros_like(gnw_ref)

        @pl.when(k == 0)
        def _():
            acc[...] = jnp.zeros_like(acc)

        @pl.when(k < ka)
        def _():
            acc[...] += _nt(da_ref[...], w_ref[...])

        @pl.when(k >= ka)
        def _():
            acc[...] += _nt(db_ref[...], w_ref[...])

        @pl.when(k == nk - 1)
        def _():
            xv = x_ref[...]
            r = lax.rsqrt(jnp.mean(xv * xv, axis=-1, keepdims=True) + EPS)
            xn = xv * r
            du = acc[...]
            gnw_ref[0:1, :] += jnp.sum(du * xn, axis=0, keepdims=True)
            dn = du * nw_ref[...]
            gx_ref[...] = dres_ref[...] + r * (dn - xn * jnp.mean(dn * xn, axis=-1, keepdims=True))

    return pl.pallas_call(
        body, name="inproj_bwd", grid=(s // tm, nk),
        in_specs=[pl.BlockSpec((tm, tk), lambda i, k: (i, jnp.minimum(k, ka - 1))),
                  pl.BlockSpec((tm, tk), lambda i, k: (i, jnp.maximum(k - ka, 0))),
                  pl.BlockSpec((d, tk), lambda i, k: (0, k)),
                  pl.BlockSpec((tm, d), lambda i, k: (i, 0)), pl.BlockSpec((1, d), lambda i, k: (0, 0)),
                  pl.BlockSpec((tm, d), lambda i, k: (i, 0))],
        out_specs=[pl.BlockSpec((tm, d), lambda i, k: (i, 0)), pl.BlockSpec((8, d), lambda i, k: (0, 0))],
        out_shape=[SDS((s, d), f32), SDS((8, d), f32)],
        scratch_shapes=[pltpu.VMEM((tm, d), f32)],
        compiler_params=pltpu.CompilerParams(dimension_semantics=("arbitrary", "arbitrary")),
    )(dqkvg, dzxd, wp, x, nw, dres)


def _matmul_tn(a, b, name):
    s, m = a.shape
    n = b.shape[1]
    tn, tk = 1024, 512
    nk = s // tk

    def body(a_ref, b_ref, o_ref):
        @pl.when(pl.program_id(1) == 0)
        def _():
            o_ref[...] = jnp.zeros_like(o_ref)
        o_ref[...] += _tn(a_ref[...], b_ref[...])

    return pl.pallas_call(
        body, name=name, grid=(n // tn, nk),
        in_specs=[pl.BlockSpec((tk, m), lambda j, k: (k, 0)), pl.BlockSpec((tk, tn), lambda j, k: (k, j))],
        out_specs=pl.BlockSpec((m, tn), lambda j, k: (0, j)),
        out_shape=SDS((m, n), f32),
        compiler_params=pltpu.CompilerParams(dimension_semantics=("parallel", "arbitrary")),
    )(a, b)


def _adamw(w, g, m, v):
    m = ADAM_B1 * m + (1.0 - ADAM_B1) * g
    v = ADAM_B2 * v + (1.0 - ADAM_B2) * (g * g)
    m_hat = m / (1.0 - ADAM_B1 ** ADAM_STEP)
    v_hat = v / (1.0 - ADAM_B2 ** ADAM_STEP)
    delta = -ADAM_LR * (m_hat / (jnp.sqrt(v_hat) + ADAM_EPS) + ADAM_WD * w)
    return delta, m, v


def _sum_adamw(own, parts, w, m, v, name):
    r, c = w.shape
    tr = 128

    def body(o_ref, p_ref, w_ref, m_ref, v_ref, g_ref, d_ref, nm_ref, nv_ref):
        my_q = 2 * lax.axis_index("x") + lax.axis_index("y")
        own_v = o_ref[...]
        g = jnp.where(my_q == 0, own_v, p_ref[0].astype(f32))
        for q in range(1, 4):
            g = g + jnp.where(my_q == q, own_v, p_ref[q].astype(f32))
        g_ref[...] = g
        d_ref[...], nm_ref[...], nv_ref[...] = _adamw(w_ref[...], g, m_ref[...], v_ref[...])

    blk = pl.BlockSpec((tr, c), lambda i: (i, 0))
    return pl.pallas_call(
        body, name=name, grid=(r // tr,),
        in_specs=[blk, pl.BlockSpec((4, tr, c), lambda i: (0, i, 0)), blk, blk, blk],
        out_specs=[blk] * 4, out_shape=[SDS((r, c), f32)] * 4,
        compiler_params=pltpu.CompilerParams(dimension_semantics=("parallel",)),
    )(own, parts, w, m, v)


def _sum_small(parts):
    def body(p_ref, o_ref):
        t = p_ref[0]
        for j in range(1, N_DEV):
            t = t + p_ref[j]
        o_ref[...] = t
        row_h = _iota((D_SSM, LANES), 0) // HEAD_DIM
        fold = (row_h == _iota((D_SSM, LANES), 1)).astype(f32)
        lower = t[8:16, 0:LANES]
        folded = _nn_hi(t[8:16, 0:D_SSM], fold)
        loss = jnp.sum(t[11:12, 0:D_MODEL], axis=1, keepdims=True) * (0.5 / D_MODEL)
        row = _iota((8, LANES), 0)
        o_ref[8:16, 0:LANES] = jnp.where(row < 2, folded, jnp.where(row == 4, loss, lower))

    return pl.pallas_call(body, name="sum_small", out_shape=SDS((PACK_ROWS, PACK_W), f32),
                          in_specs=[pl.BlockSpec(memory_space=pltpu.VMEM)],
                          out_specs=pl.BlockSpec(memory_space=pltpu.VMEM))(parts)


def _adamw_small(w, g, m, v):
    def body(w_ref, g_ref, m_ref, v_ref, d_ref, nm_ref, nv_ref):
        d_ref[...], nm_ref[...], nv_ref[...] = _adamw(w_ref[...], g_ref[...], m_ref[...], v_ref[...])

    vm = pl.BlockSpec(memory_space=pltpu.VMEM)
    return pl.pallas_call(body, name="adamw_small", out_shape=[SDS(w.shape, f32)] * 3,
                          in_specs=[vm] * 4, out_specs=[vm] * 3)(w, g, m, v)


def _pad_lanes(v, width):
    return jnp.pad(v, ((0, 0), (0, width - v.shape[1])))


def _local_step(x, tgt, norm_pre_w, wp, conv_w, conv_b, dt_bias, a_log, d_skip, ssm_norm_w, wo, norm_post_w):
    dtb16 = _pad_lanes(dt_bias, LANES)
    alog16 = _pad_lanes(a_log, LANES)
    alog_f = jnp.repeat(a_log, HEAD_DIM, axis=1)
    d_f = jnp.repeat(d_skip, HEAD_DIM, axis=1)

    proj, u = _prenorm_inproj(x, norm_pre_w, wp)
    o, lb, mix_a = _attn_fwd(proj)
    mix_s, y, states = _ssd_fwd(proj, conv_w, conv_b, dtb16, alog16, alog_f, d_f, ssm_norm_w)
    dmix, dout, dres, acc_post = _outproj_loss(mix_a, mix_s, wo, x, tgt, norm_post_w)
    dq, dk, dv, dg = _attn_bwd(proj, o, lb, dmix)
    dqkvg = jnp.concatenate([dq, dk, dv, dg], axis=1)
    dzxd, g_conv, g_vec, g_dt = _ssd_bwd(proj, y, states, dmix, conv_w, conv_b, dtb16, alog16, alog_f, d_f, ssm_norm_w)
    grad_x, g_pre = _inproj_bwd(dqkvg, dzxd, wp, x, norm_pre_w, dres)
    dw_in = jnp.concatenate([_matmul_tn(u, dqkvg, "dw_in_a"), _matmul_tn(u, dzxd, "dw_in_b")], axis=1)
    dw_out = jnp.concatenate([_matmul_tn(mix_a, dout, "dw_out_a"), _matmul_tn(mix_s, dout, "dw_out_b")], axis=0)

    rows = [g_conv[0:5], _pad_lanes(g_pre[0:1], PACK_W), _pad_lanes(g_vec[0:1], PACK_W),
            _pad_lanes(acc_post[1:2], PACK_W), _pad_lanes(g_vec[1:3], PACK_W), _pad_lanes(g_dt[0:1], PACK_W),
            _pad_lanes(acc_post[0:1], PACK_W), jnp.zeros((4, PACK_W), f32)]
    return grad_x, dw_in, dw_out, jnp.concatenate(rows, axis=0)


def kernel(x, norm_pre_w, w_in, conv_w, conv_b, dt_bias, a_log, d_skip, ssm_norm_w, w_out, norm_post_w, loss_target, m_norm_pre_w, m_w_in, m_conv_w, m_conv_b, m_dt_bias, m_a_log, m_d_skip, m_ssm_norm_w, m_w_out, m_norm_post_w, v_norm_pre_w, v_w_in, v_conv_w, v_conv_b, v_dt_bias, v_a_log, v_d_skip, v_ssm_norm_w, v_w_out, v_norm_post_w):
    shard_in = w_in.shape[2]
    shard_cv = conv_w.shape[2]
    me = 4 * lax.axis_index("x") + 2 * lax.axis_index("y") + lax.axis_index("c")

    g_in, g_out, g_cw = _all_gather([w_in[0].astype(bf16), w_out[0].astype(bf16), conv_w[0]])
    wp = jnp.pad(g_in.transpose(1, 0, 2).reshape(D_MODEL, N_DEV * shard_in), ((0, 0), (0, NP - N_DEV * shard_in)))
    wo = g_out.reshape(N_DEV * w_out.shape[1], D_MODEL)
    cw = g_cw.transpose(1, 0, 2).reshape(4, D_CONV)

    grad_x, dw_in, dw_out, pack = _local_step(
        x[0], loss_target[0], norm_pre_w, wp, cw, conv_b, dt_bias, a_log, d_skip, ssm_norm_w, wo, norm_post_w)

    send_in = dw_in[:, :N_DEV * shard_in].reshape(D_MODEL, 4, 2, shard_in).transpose(2, 1, 0, 3)
    send_out = dw_out.reshape(4, 2, w_out.shape[1], D_MODEL).transpose(1, 0, 2, 3)
    got_in, got_out, parts_small = _sibling_swap([send_in.astype(bf16), send_out.astype(bf16)], pack)
    chip_in, own_in = _chip_sum(send_in, got_in, "chip_sum_w_in")
    chip_out, own_out = _chip_sum(send_out, got_out, "chip_sum_w_out")
    parts_in, parts_out = _chip_exchange([chip_in, chip_out])

    g_w_in, d_w_in, nm_w_in, nv_w_in = _sum_adamw(own_in, parts_in, w_in[0], m_w_in[0], v_w_in[0], "sum_adamw_w_in")
    g_w_out, d_w_out, nm_w_out, nv_w_out = _sum_adamw(own_out, parts_out, w_out[0], m_w_out[0], v_w_out[0], "sum_adamw_w_out")
    tot = _sum_small(parts_small)

    g_cw_all = tot[0:4]
    small_g = {
        "conv_w": lax.dynamic_slice(g_cw_all, (0, me * shard_cv), (4, shard_cv)),
        "conv_b": tot[4:5], "norm_pre_w": tot[5:6, :D_MODEL], "ssm_norm_w": tot[6:7, :D_SSM],
        "norm_post_w": tot[7:8, :D_MODEL], "a_log": tot[8:9, :16], "d_skip": tot[9:10, :16], "dt_bias": tot[10:11, :16],
    }
    loss = tot[12, 0]
    small_w = {"conv_w": (conv_w[0], m_conv_w[0], v_conv_w[0]), "conv_b": (conv_b, m_conv_b, v_conv_b),
               "norm_pre_w": (norm_pre_w, m_norm_pre_w, v_norm_pre_w), "ssm_norm_w": (ssm_norm_w, m_ssm_norm_w, v_ssm_norm_w),
               "norm_post_w": (norm_post_w, m_norm_post_w, v_norm_post_w), "a_log": (a_log, m_a_log, v_a_log),
               "d_skip": (d_skip, m_d_skip, v_d_skip), "dt_bias": (dt_bias, m_dt_bias, v_dt_bias)}
    names = list(small_w)
    sizes = [small_g[k].size for k in names]
    tot_size = sum(sizes)
    pad_to = -(-tot_size // 1024) * 1024

    def flat(arrs):
        v = jnp.concatenate([a.reshape(-1) for a in arrs])
        return jnp.pad(v, (0, pad_to - tot_size)).reshape(pad_to // LANES, LANES)

    fw = flat([small_w[k][0] for k in names])
    fg = flat([small_g[k] for k in names])
    fm = flat([small_w[k][1] for k in names])
    fv = jnp.pad(jnp.concatenate([small_w[k][2].reshape(-1) for k in names]), (0, pad_to - tot_size),
                 constant_values=1.0).reshape(pad_to // LANES, LANES)
    fd, fnm, fnv = _adamw_small(fw, fg, fm, fv)

    def unflat(f):
        out, off = {}, 0
        v = f.reshape(-1)
        for k, n in zip(names, sizes):
            out[k] = v[off:off + n].reshape(small_g[k].shape)
            off += n
        return out

    sd, snm, snv = unflat(fd), unflat(fnm), unflat(fnv)
    lead = lambda a: a[None]
    order = ["norm_pre_w", "w_in", "conv_w", "conv_b", "dt_bias", "a_log", "d_skip", "ssm_norm_w", "w_out", "norm_post_w"]
    grads = dict(small_g, w_in=g_w_in, w_out=g_w_out)
    deltas = dict(sd, w_in=d_w_in, w_out=d_w_out)
    new_m = dict(snm, w_in=nm_w_in, w_out=nm_w_out)
    new_v = dict(snv, w_in=nv_w_in, w_out=nv_w_out)

    def shaped(dct, k):
        a = dct[k]
        return lead(a) if k in ("w_in", "w_out", "conv_w") else a

    return (loss, grad_x[None], *[shaped(grads, k) for k in order], *[shaped(deltas, k) for k in order],
            *[shaped(new_m, k) for k in order], *[shaped(new_v, k) for k in order])
```

```python
import functools
import math

import jax
import jax.numpy as jnp
import numpy as np
from jax import lax
from jax.experimental import pallas as pl
from jax.experimental.pallas import tpu as pltpu

f32, bf16 = jnp.float32, jnp.bfloat16
SDS = jax.ShapeDtypeStruct
HIGHEST = lax.Precision.HIGHEST
MESH = pl.DeviceIdType.MESH

N_DEV = 8
D_MODEL = 1024
D_ATTN = 1024
D_SSM = 1024
HEAD_DIM = 64
N_PAIRS = 8
D_STATE = 128
N_GROUPS = 2
D_CONV = D_SSM + 2 * N_GROUPS * D_STATE
D_IN_PROJ = 4 * D_ATTN + D_SSM + D_CONV + 16
NP = 7168
CHUNK = 128
BLK = 128
DILATIONS = (1, 4, 16)
EPS = 1e-6
LANES = 128
COL_Z, COL_XS, COL_BC, COL_DT = 4096, 5120, 6144, 6656

ADAM_LR, ADAM_B1, ADAM_B2, ADAM_EPS, ADAM_WD, ADAM_STEP = 0.001, 0.9, 0.999, 1e-08, 0.01, 10

PACK_ROWS, PACK_W = 16, 1536


def _nt(a, b):
    return lax.dot_general(a, b, (((1,), (1,)), ((), ())), preferred_element_type=f32)


def _tn(a, b):
    return lax.dot_general(a, b, (((0,), (0,)), ((), ())), preferred_element_type=f32)


def _nn(a, b):
    return jnp.dot(a, b, preferred_element_type=f32)


def _nn_hi(a, b):
    return jnp.dot(a, b, precision=HIGHEST, preferred_element_type=f32)


def _sigmoid(x):
    return 1.0 / (1.0 + jnp.exp(-x))


def _softplus(x):
    return jnp.maximum(x, 0.0) + jnp.log1p(jnp.exp(-jnp.abs(x)))


def _iota(shape, dim):
    return lax.broadcasted_iota(jnp.int32, shape, dim)


def _my_pos():
    return lax.axis_index("x"), lax.axis_index("y"), lax.axis_index("c")


def _all_gather(arrs):
    n = len(arrs)

    def body(*refs):
        ins, outs = refs[:n], refs[n:2 * n]
        send_sems, recv_sems, local_sems = refs[2 * n:]
        x, y, c = _my_pos()
        me, sibling = (x, y, c), (x, y, 1 - c)
        chips = [(1 - x, y), (x, 1 - y), (1 - x, 1 - y)]

        def slot(a, px, py, pc):
            return outs[a].at[4 * px + 2 * py + pc]

        def copy(a, k, block, to, src=None):
            return pltpu.make_async_remote_copy(
                src_ref=slot(a, *block) if src is None else src, dst_ref=slot(a, *block),
                send_sem=send_sems.at[7 * a + k], recv_sem=recv_sems.at[7 * a + k],
                device_id=to, device_id_type=MESH)

        mine = [pltpu.make_async_copy(ins[a], slot(a, *me), local_sems.at[a]) for a in range(n)]
        for cp in mine:
            cp.start()
        first = []
        for a in range(n):
            first.append(copy(a, 0, me, sibling, src=ins[a]))
            first += [copy(a, 1 + j, me, (*chip, c), src=ins[a]) for j, chip in enumerate(chips)]
        for cp in first:
            cp.start()
        passed = []
        for j, chip in enumerate(chips):
            for a in range(n):
                copy(a, 1 + j, (*chip, c), me).wait_recv()
                cp = copy(a, 4 + j, (*chip, c), sibling)
                cp.start()
                passed.append(cp)
        for a in range(n):
            copy(a, 0, sibling, me).wait_recv()
            for j, chip in enumerate(chips):
                copy(a, 4 + j, (*chip, 1 - c), me).wait_recv()
        for cp in first + passed:
            cp.wait_send()
        for cp in mine:
            cp.wait()

    anyspec = pl.BlockSpec(memory_space=pl.ANY)
    return pl.pallas_call(
        body, name="weights_all_gather",
        out_shape=[SDS((N_DEV,) + a.shape, a.dtype) for a in arrs],
        in_specs=[anyspec] * n, out_specs=[anyspec] * n,
        scratch_shapes=[pltpu.SemaphoreType.DMA((7 * n,)), pltpu.SemaphoreType.DMA((7 * n,)),
                        pltpu.SemaphoreType.DMA((n,))],
    )(*arrs)


def _sibling_swap(bigs, small):
    nb = len(bigs)

    def body(*refs):
        ins, small_in = refs[:nb], refs[nb]
        outs, small_out = refs[nb + 1:2 * nb + 1], refs[2 * nb + 1]
        send_sems, recv_sems, local_sem = refs[2 * nb + 2:]
        x, y, c = _my_pos()
        me = 4 * x + 2 * y + c
        mine = pltpu.make_async_copy(small_in, small_out.at[me], local_sem)
        mine.start()
        sends = []
        for a in range(nb):
            cp = pltpu.make_async_remote_copy(
                src_ref=ins[a].at[1 - c], dst_ref=outs[a], send_sem=send_sems.at[a], recv_sem=recv_sems.at[a],
                device_id=(x, y, 1 - c), device_id_type=MESH)
            cp.start()
            sends.append(cp)
        for k in range(1, N_DEV):
            to = (me + k) % N_DEV
            cp = pltpu.make_async_remote_copy(
                src_ref=small_in, dst_ref=small_out.at[me],
                send_sem=send_sems.at[nb + k - 1], recv_sem=recv_sems.at[nb + k - 1],
                device_id=(to // 4, (to // 2) % 2, to % 2), device_id_type=MESH)
            cp.start()
            sends.append(cp)
        for a in range(nb):
            pltpu.make_async_remote_copy(
                src_ref=ins[a].at[c], dst_ref=outs[a], send_sem=send_sems.at[a], recv_sem=recv_sems.at[a],
                device_id=(x, y, c), device_id_type=MESH).wait_recv()
        for k in range(1, N_DEV):
            frm = (me + N_DEV - k) % N_DEV
            pltpu.make_async_remote_copy(
                src_ref=small_in, dst_ref=small_out.at[frm],
                send_sem=send_sems.at[nb + k - 1], recv_sem=recv_sems.at[nb + k - 1],
                device_id=(x, y, c), device_id_type=MESH).wait_recv()
        for cp in sends:
            cp.wait_send()
        mine.wait()

    anyspec = pl.BlockSpec(memory_space=pl.ANY)
    out_shape = [SDS(a.shape[1:], a.dtype) for a in bigs] + [SDS((N_DEV,) + small.shape, small.dtype)]
    return pl.pallas_call(
        body, name="grad_sibling_swap", out_shape=out_shape,
        in_specs=[anyspec] * (nb + 1), out_specs=[anyspec] * (nb + 1),
        scratch_shapes=[pltpu.SemaphoreType.DMA((nb + 7,)), pltpu.SemaphoreType.DMA((nb + 7,)),
                        pltpu.SemaphoreType.DMA(())],
    )(*bigs, small)


def _chip_sum(mine, got, name):
    _, nq, r, cdim = mine.shape
    tr = 128

    def body(m_ref, g_ref, s16_ref, own_ref):
        q = pl.program_id(1)
        c = lax.axis_index("c")
        my_q = 2 * lax.axis_index("x") + lax.axis_index("y")
        tot = m_ref[c] + g_ref[...].astype(f32)
        s16_ref[...] = tot.astype(bf16)

        @pl.when(q == my_q)
        def _():
            own_ref[...] = tot

    return pl.pallas_call(
        body, name=name, grid=(r // tr, nq),
        in_specs=[pl.BlockSpec((2, None, tr, cdim), lambda i, q: (0, q, i, 0)),
                  pl.BlockSpec((None, tr, cdim), lambda i, q: (q, i, 0))],
        out_specs=[pl.BlockSpec((None, tr, cdim), lambda i, q: (q, i, 0)), pl.BlockSpec((tr, cdim), lambda i, q: (i, 0))],
        out_shape=[SDS((nq, r, cdim), bf16), SDS((r, cdim), f32)],
        compiler_params=pltpu.CompilerParams(dimension_semantics=("parallel", "arbitrary")),
    )(mine, got)


def _chip_exchange(bigs):
    nb = len(bigs)

    def body(*refs):
        ins, outs = refs[:nb], refs[nb:2 * nb]
        send_sems, recv_sems, local_sems = refs[2 * nb:]
        x, y, c = _my_pos()
        my_q = 2 * x + y
        mine = [pltpu.make_async_copy(ins[a].at[my_q], outs[a].at[my_q], local_sems.at[a]) for a in range(nb)]
        for cp in mine:
            cp.start()
        sends = []
        for k in range(1, 4):
            to = (my_q + k) % 4
            for a in range(nb):
                cp = pltpu.make_async_remote_copy(
                    src_ref=ins[a].at[to], dst_ref=outs[a].at[my_q],
                    send_sem=send_sems.at[3 * a + k - 1], recv_sem=recv_sems.at[3 * a + k - 1],
                    device_id=(to // 2, to % 2, c), device_id_type=MESH)
                cp.start()
                sends.append(cp)
        for k in range(1, 4):
            frm = (my_q + 4 - k) % 4
            for a in range(nb):
                pltpu.make_async_remote_copy(
                    src_ref=ins[a].at[frm], dst_ref=outs[a].at[frm],
                    send_sem=send_sems.at[3 * a + k - 1], recv_sem=recv_sems.at[3 * a + k - 1],
                    device_id=(x, y, c), device_id_type=MESH).wait_recv()
        for cp in sends:
            cp.wait_send()
        for cp in mine:
            cp.wait()

    anyspec = pl.BlockSpec(memory_space=pl.ANY)
    return pl.pallas_call(
        body, name="grad_chip_exchange", out_shape=[SDS(a.shape, a.dtype) for a in bigs],
        in_specs=[anyspec] * nb, out_specs=[anyspec] * nb,
        scratch_shapes=[pltpu.SemaphoreType.DMA((3 * nb,)), pltpu.SemaphoreType.DMA((3 * nb,)),
                        pltpu.SemaphoreType.DMA((nb,))],
    )(*bigs)


def _prenorm_inproj(x, nw, wp):
    s, d = x.shape
    npad = wp.shape[1]
    tm, tn = 1024, 512

    def body(x_ref, nw_ref, w_ref, proj_ref, u_ref):
        @pl.when(pl.program_id(1) == 0)
        def _():
            xv = x_ref[...]
            r = lax.rsqrt(jnp.mean(xv * xv, axis=-1, keepdims=True) + EPS)
            u_ref[...] = (xv * r * nw_ref[...]).astype(bf16)
        proj_ref[...] = _nn(u_ref[...], w_ref[...])

    return pl.pallas_call(
        body, name="prenorm_inproj", grid=(s // tm, npad // tn),
        in_specs=[pl.BlockSpec((tm, d), lambda i, j: (i, 0)), pl.BlockSpec((1, d), lambda i, j: (0, 0)),
                  pl.BlockSpec((d, tn), lambda i, j: (0, j))],
        out_specs=[pl.BlockSpec((tm, tn), lambda i, j: (i, j)), pl.BlockSpec((tm, d), lambda i, j: (i, 0))],
        out_shape=[SDS((s, npad), f32), SDS((s, d), bf16)],
        compiler_params=pltpu.CompilerParams(dimension_semantics=("parallel", "arbitrary")),
    )(x, nw, wp)


def _attn_consts():
    head0 = _iota((BLK, LANES), 1) < HEAD_DIM
    tri2 = (_iota((BLK, 2 * LANES), 1) % LANES) <= _iota((BLK, 2 * LANES), 0)
    ones2 = ((_iota((LANES, 2 * LANES), 0) < HEAD_DIM) == (_iota((LANES, 2 * LANES), 1) < LANES)).astype(bf16)
    rmat = ((_iota((2 * LANES, LANES), 0) < LANES) == (_iota((2 * LANES, LANES), 1) < HEAD_DIM)).astype(bf16)
    bones = ((_iota((LANES, LANES), 0) < HEAD_DIM) == (_iota((LANES, LANES), 1) < HEAD_DIM)).astype(bf16)
    return head0, tri2, ones2, rmat, bones


def _stack_heads(x16, head0):
    zero = jnp.zeros_like(x16)
    return jnp.concatenate([jnp.where(head0, x16, zero), jnp.where(head0, zero, x16)], axis=0)


def _split_dot(x, w16):
    hi = x.astype(bf16)
    lo = (x - hi.astype(f32)).astype(bf16)
    return _nn(hi, w16) + _nn(lo, w16)


def _attn_fwd(proj):
    s = proj.shape[0]
    n_it = s // BLK

    def body(q_ref, k_ref, v_ref, g_ref, o_ref, l_ref, mix_ref, op0, op1, op2, lp0, lp1, lp2):
        op_refs, lp_refs = (op0, op1, op2), (lp0, lp1, lp2)
        head0, tri2, ones2, rmat, _ = _attn_consts()
        for p, d in enumerate(DILATIONS):
            nb = s // (BLK * d)

            def it(i, carry, d=d, nb=nb, p=p):
                r, blk = i // nb, i % nb
                st = blk * (BLK * d) + r
                stp = jnp.maximum(blk - 1, 0) * (BLK * d) + r
                rows = pl.ds(st, BLK, stride=d)
                rows_p = pl.ds(stp, BLK, stride=d)
                has_prev = blk > 0
                qs = q_ref[rows, :] * 0.125
                kc, kp = k_ref[rows, :], k_ref[rows_p, :]
                vc, vp = v_ref[rows, :], v_ref[rows_p, :]
                qs16 = qs.astype(bf16)
                sc = _nt(qs16, _stack_heads(kc.astype(bf16), head0))
                sp = _nt(qs16, _stack_heads(kp.astype(bf16), head0))
                sc = jnp.where(tri2, sc, jnp.where(has_prev, sp, -jnp.inf))
                sd2 = jnp.where(has_prev, _split_dot(qs * kp, ones2), -jnp.inf)
                m0 = jnp.max(sc[:, :LANES], axis=1, keepdims=True)
                m1 = jnp.max(sc[:, LANES:], axis=1, keepdims=True)
                m2 = jnp.concatenate([jnp.broadcast_to(m0, (BLK, LANES)), jnp.broadcast_to(m1, (BLK, LANES))], axis=1)
                m2 = jnp.maximum(m2, sd2)
                pt16 = jnp.exp(sc - m2).astype(bf16)
                m_pair = jnp.where(head0, m2[:, :LANES], m2[:, LANES:])
                pd = jnp.exp(jnp.where(head0, sd2[:, :LANES], sd2[:, LANES:]) - m_pair)
                zero = jnp.zeros_like(pt16)
                o = (_nn(jnp.where(tri2, pt16, zero), _stack_heads(vc.astype(bf16), head0))
                     + _nn(jnp.where(tri2, zero, pt16), _stack_heads(vp.astype(bf16), head0)) + pd * vp)
                l = _nn(pt16, rmat) + pd
                op_refs[p][rows, :] = o / l
                lp_refs[p][rows, :] = m_pair + jnp.log(l)
                return carry

            lax.fori_loop(0, n_it, it, 0, unroll=2)

        def merge(i, carry):
            rows = pl.ds(pl.multiple_of(i * 256, 256), 256)
            l0, l1, l2 = lp0[rows, :], lp1[rows, :], lp2[rows, :]
            m = jnp.maximum(jnp.maximum(l0, l1), l2)
            e0, e1, e2 = jnp.exp(l0 - m), jnp.exp(l1 - m), jnp.exp(l2 - m)
            z = e0 + e1 + e2
            o = (e0 * op0[rows, :] + e1 * op1[rows, :] + e2 * op2[rows, :]) / z
            o_ref[rows, :] = o
            l_ref[rows, :] = m + jnp.log(z)
            g = g_ref[rows, :]
            mix_ref[rows, :] = (o * (g * _sigmoid(g))).astype(bf16)
            return carry

        lax.fori_loop(0, s // 256, merge, 0)

    col = lambda base: pl.BlockSpec((s, LANES), lambda h: (0, base + h))
    return pl.pallas_call(
        body, name="attn_fwd", grid=(N_PAIRS,),
        in_specs=[col(0), col(8), col(16), col(24)],
        out_specs=[col(0), col(0), col(0)],
        out_shape=[SDS((s, D_ATTN), f32), SDS((s, D_ATTN), f32), SDS((s, D_ATTN), bf16)],
        scratch_shapes=[pltpu.VMEM((s, LANES), f32)] * 6,
        compiler_params=pltpu.CompilerParams(dimension_semantics=("parallel",)),
    )(proj, proj, proj, proj)


def _expand_mat():
    row = _iota((LANES, 2 * D_SSM), 0)
    colv = _iota((LANES, 2 * D_SSM), 1)
    head = 2 * ((colv % D_SSM) // LANES) + colv // D_SSM
    return (row == head).astype(f32)


def _ssd_common(xs_ref, bc_ref, xs_tail, bc_tail, dt_ref, cw_ref, cb_ref, dtb_ref, alog16_ref, xpad, first):
    keep = jnp.where(first, 0.0, 1.0)
    xpad[0:8, 0:D_SSM] = xs_tail[...] * keep
    xpad[0:8, D_SSM:D_CONV] = bc_tail[...] * keep
    xpad[8:8 + CHUNK, 0:D_SSM] = xs_ref[...]
    xpad[8:8 + CHUNK, D_SSM:D_CONV] = bc_ref[...]
    cv = cb_ref[...] + cw_ref[0:1, :] * xpad[pl.ds(5, CHUNK), :]
    for j in range(1, 4):
        cv = cv + cw_ref[j:j + 1, :] * xpad[pl.ds(5 + j, CHUNK), :]
    sig = _sigmoid(cv)
    xbc = cv * sig

    pre = dt_ref[...] + dtb_ref[...]
    dt16 = _softplus(pre)
    a16 = -jnp.exp(alog16_ref[...])
    sub, lane = _iota((CHUNK, CHUNK), 0), _iota((CHUNK, CHUNK), 1)
    tri = (sub >= lane).astype(f32)
    al16 = _nn_hi(tri, dt16 * a16)
    al_t = al16.T
    emat = _expand_mat()
    dt_x = _nn_hi(dt16, emat)
    al_x = _nn_hi(al16, emat)
    lane_w = _iota((CHUNK, D_SSM), 1)
    even = (lane_w % LANES) < HEAD_DIM
    dt_f = jnp.where(even, dt_x[:, :D_SSM], dt_x[:, D_SSM:])
    al_f = jnp.where(even, al_x[:, :D_SSM], al_x[:, D_SSM:])
    return cv, sig, xbc, pre, dt_f, al_f, al_x, al_t


def _decay_mat(al_x, al_t, pair, h):
    sub, lane = _iota((CHUNK, CHUNK), 0), _iota((CHUNK, CHUNK), 1)
    col = al_x[:, h * D_SSM + pair * LANES: h * D_SSM + (pair + 1) * LANES]
    row = al_t[2 * pair + h: 2 * pair + h + 1, :]
    return jnp.exp(jnp.where(sub >= lane, col - row, -jnp.inf))


def _ssd_in_specs(order):
    blk = lambda w, cb: pl.BlockSpec((CHUNK, w), lambda i: (order(i), cb))
    tail = lambda w, cb: pl.BlockSpec((8, w), lambda i: (jnp.maximum(16 * order(i) - 1, 0), cb))
    return [blk(D_SSM, COL_XS // D_SSM), blk(512, COL_BC // 512), tail(D_SSM, COL_XS // D_SSM),
            tail(512, COL_BC // 512), blk(LANES, COL_DT // LANES), blk(D_SSM, COL_Z // D_SSM)]


def _full(shape):
    return pl.BlockSpec(shape, lambda i: (0,) * len(shape))


def _ssd_fwd(proj, conv_w, conv_b, dtb16, alog16, alog_f, d_f, nw):
    s = proj.shape[0]
    nc = s // CHUNK

    def body(xs_ref, bc_ref, xs_tail, bc_tail, dt_ref, z_ref, cw_ref, cb_ref, dtb_ref, alog16_ref, alogf_ref,
             df_ref, nw_ref, mix_ref, y_ref, st_ref, h_scr, xpad, y_scr):
        c = pl.program_id(0)

        @pl.when(c == 0)
        def _():
            h_scr[...] = jnp.zeros_like(h_scr)

        _, _, xbc, _, dt_f, al_f, al_x, al_t = _ssd_common(
            xs_ref, bc_ref, xs_tail, bc_tail, dt_ref, cw_ref, cb_ref, dtb_ref, alog16_ref, xpad, c == 0)
        head0 = _iota((CHUNK, LANES), 1) < HEAD_DIM
        st_ref[...] = h_scr[...]
        for g in range(N_GROUPS):
            bm = xbc[:, D_SSM + g * D_STATE: D_SSM + (g + 1) * D_STATE].astype(bf16)
            cm = xbc[:, D_SSM + (N_GROUPS + g) * D_STATE: D_SSM + (N_GROUPS + g + 1) * D_STATE].astype(bf16)
            gmat = _nt(cm, bm)
            for pair in range(4 * g, 4 * g + 4):
                sl = slice(pair * LANES, (pair + 1) * LANES)
                xp, dtp, alp = xbc[:, sl], dt_f[:, sl], al_f[:, sl]
                xdt = xp * dtp
                xdt16 = xdt.astype(bf16)
                al_last = alp[CHUNK - 1:CHUNK, :]
                hp = h_scr[:, sl]
                y_off = jnp.exp(alp) * _nn(cm, hp.astype(bf16))
                yd = [_nn((gmat * _decay_mat(al_x, al_t, pair, h)).astype(bf16), xdt16) for h in range(2)]
                y_scr[:, sl] = jnp.where(head0, yd[0], yd[1]) + y_off + df_ref[:, sl] * xp
                st = _tn(bm, (jnp.exp(al_last - alp) * xdt).astype(bf16))
                h_scr[:, sl] = jnp.exp(al_last) * hp + st
        y = y_scr[...]
        y_ref[...] = y
        z = z_ref[...]
        yz = y * (z * _sigmoid(z))
        gw = D_SSM // N_GROUPS
        for g in range(N_GROUPS):
            part = yz[:, g * gw:(g + 1) * gw]
            r = lax.rsqrt(jnp.mean(part * part, axis=-1, keepdims=True) + EPS)
            mix_ref[:, g * gw:(g + 1) * gw] = (part * r * nw_ref[:, g * gw:(g + 1) * gw]).astype(bf16)

    order = lambda i: i
    row = lambda w: pl.BlockSpec((CHUNK, w), lambda i: (i, 0))
    return pl.pallas_call(
        body, name="ssd_fwd", grid=(nc,),
        in_specs=_ssd_in_specs(order) + [_full((4, D_CONV)), _full((1, D_CONV)), _full((1, LANES)), _full((1, LANES)),
                                         _full((1, D_SSM)), _full((1, D_SSM)), _full((1, D_SSM))],
        out_specs=[row(D_SSM), row(D_SSM), pl.BlockSpec((None, D_STATE, D_SSM), lambda i: (i, 0, 0))],
        out_shape=[SDS((s, D_SSM), bf16), SDS((s, D_SSM), f32), SDS((nc, D_STATE, D_SSM), f32)],
        scratch_shapes=[pltpu.VMEM((D_STATE, D_SSM), f32), pltpu.VMEM((8 + CHUNK, D_CONV), f32),
                        pltpu.VMEM((CHUNK, D_SSM), f32)],
        compiler_params=pltpu.CompilerParams(dimension_semantics=("arbitrary",)),
    )(proj, proj, proj, proj, proj, proj, conv_w, conv_b, dtb16, alog16, alog_f, d_f, nw)


def _outproj_loss(mix_a, mix_s, wo, x, tgt, npw):
    s, d = x.shape
    tm = 512

    def body(ma_ref, ms_ref, wo_ref, x_ref, t_ref, npw_ref, dmix_ref, dout_ref, dres_ref, acc_ref):
        @pl.when(pl.program_id(0) == 0)
        def _():
            acc_ref[...] = jnp.zeros_like(acc_ref)

        out = _nn(ma_ref[...], wo_ref[0:D_ATTN, :]) + _nn(ms_ref[...], wo_ref[D_ATTN:, :])
        r = lax.rsqrt(jnp.mean(out * out, axis=-1, keepdims=True) + EPS)
        on = out * r
        diff = x_ref[...] + on * npw_ref[...] - t_ref[...]
        dres = diff * (1.0 / d)
        dres_ref[...] = dres
        acc_ref[0:1, :] += jnp.sum(diff * diff, axis=0, keepdims=True)
        acc_ref[1:2, :] += jnp.sum(dres * on, axis=0, keepdims=True)
        dn = dres * npw_ref[...]
        dout = (r * (dn - on * jnp.mean(dn * on, axis=-1, keepdims=True))).astype(bf16)
        dout_ref[...] = dout
        dmix_ref[...] = _nt(dout, wo_ref[...])

    row = lambda w: pl.BlockSpec((tm, w), lambda i: (i, 0))
    return pl.pallas_call(
        body, name="outproj_loss", grid=(s // tm,),
        in_specs=[row(D_ATTN), row(D_SSM), _full((D_ATTN + D_SSM, d)), row(d), row(d), _full((1, d))],
        out_specs=[row(D_ATTN + D_SSM), row(d), row(d), _full((8, d))],
        out_shape=[SDS((s, D_ATTN + D_SSM), f32), SDS((s, d), bf16), SDS((s, d), f32), SDS((8, d), f32)],
        compiler_params=pltpu.CompilerParams(dimension_semantics=("arbitrary",)),
    )(mix_a, mix_s, wo, x, tgt, npw)


def _attn_bwd(proj, o, lb, dmix):
    s = proj.shape[0]
    n_it = s // BLK

    def body(q_ref, k_ref, v_ref, g_ref, o_ref, l_ref, dm_ref, dq_ref, dk_ref, dv_ref, dg_ref,
             dq_acc, dk_acc, dv_acc, do_scr, dl_scr):
        head0, tri2, _, _, bones = _attn_consts()

        def pro(i, carry):
            rows = pl.ds(pl.multiple_of(i * 256, 256), 256)
            g = g_ref[rows, :]
            sg = _sigmoid(g)
            dmx = dm_ref[rows, :]
            ov = o_ref[rows, :]
            dg_ref[rows, :] = (dmx * ov * (sg * (1.0 + g * (1.0 - sg)))).astype(bf16)
            do = dmx * (g * sg)
            do_scr[rows, :] = do
            dl_scr[rows, :] = _split_dot(do * ov, bones)
            z = jnp.zeros((256, LANES), f32)
            dq_acc[rows, :] = z
            dk_acc[rows, :] = z
            dv_acc[rows, :] = z
            return carry

        lax.fori_loop(0, s // 256, pro, 0)

        def per_head(t):
            return jnp.concatenate([t[:, :LANES], t[:, LANES:]], axis=0)

        def both_heads(t):
            tr = pltpu.roll(t, HEAD_DIM, 1)
            return jnp.concatenate([jnp.where(head0, t, tr), jnp.where(head0, tr, t)], axis=1)

        for d in DILATIONS:
            nb = s // (BLK * d)

            def it(i, carry, d=d, nb=nb):
                r, blk = i // nb, i % nb
                st = blk * (BLK * d) + r
                stp = jnp.maximum(blk - 1, 0) * (BLK * d) + r
                rows = pl.ds(st, BLK, stride=d)
                rows_p = pl.ds(stp, BLK, stride=d)
                has_prev = blk > 0
                q = q_ref[rows, :]
                kc, kp = k_ref[rows, :], k_ref[rows_p, :]
                vc, vp = v_ref[rows, :], v_ref[rows_p, :]
                do = do_scr[rows, :]
                lse = l_ref[rows, :]
                dl = dl_scr[rows, :]
                qs = q * 0.125
                qs16, q16, do16 = qs.astype(bf16), q.astype(bf16), do.astype(bf16)
                kst_c, kst_p = _stack_heads(kc.astype(bf16), head0), _stack_heads(kp.astype(bf16), head0)
                vst_c, vst_p = _stack_heads(vc.astype(bf16), head0), _stack_heads(vp.astype(bf16), head0)
                sc = jnp.where(tri2, _nt(qs16, kst_c), jnp.where(has_prev, _nt(qs16, kst_p), -jnp.inf))
                pt = jnp.exp(sc - both_heads(lse))
                dp = jnp.where(tri2, _nt(do16, vst_c), _nt(do16, vst_p))
                ds16 = (pt * (dp - both_heads(dl)) * 0.125).astype(bf16)
                pt16 = pt.astype(bf16)
                zero = jnp.zeros_like(pt16)
                dsc, dsp = jnp.where(tri2, ds16, zero), jnp.where(tri2, zero, ds16)
                pc, pp = jnp.where(tri2, pt16, zero), jnp.where(tri2, zero, pt16)
                pd = jnp.where(has_prev, jnp.exp(_split_dot(qs * kp, bones) - lse), 0.0)
                dsd = pd * (_split_dot(do * vp, bones) - dl) * 0.125
                qst, dost = _stack_heads(q16, head0), _stack_heads(do16, head0)
                dq_acc[rows, :] += _nn(dsc, kst_c) + _nn(dsp, kst_p) + dsd * kp
                dk_acc[rows, :] += _tn(per_head(dsc), qst)
                dv_acc[rows, :] += _tn(per_head(pc), dost)
                dk_acc[rows_p, :] += _tn(per_head(dsp), qst) + dsd * q
                dv_acc[rows_p, :] += _tn(per_head(pp), dost) + pd * do
                return carry

            lax.fori_loop(0, n_it, it, 0, unroll=2)

        def epi(i, carry):
            rows = pl.ds(pl.multiple_of(i * 256, 256), 256)
            dq_ref[rows, :] = dq_acc[rows, :].astype(bf16)
            dk_ref[rows, :] = dk_acc[rows, :].astype(bf16)
            dv_ref[rows, :] = dv_acc[rows, :].astype(bf16)
            return carry

        lax.fori_loop(0, s // 256, epi, 0)

    col = lambda base: pl.BlockSpec((s, LANES), lambda h: (0, base + h))
    outs = pl.pallas_call(
        body, name="attn_bwd", grid=(N_PAIRS,),
        in_specs=[col(0), col(8), col(16), col(24), col(0), col(0), col(0)],
        out_specs=[col(0)] * 4,
        out_shape=[SDS((s, D_ATTN), bf16)] * 4,
        scratch_shapes=[pltpu.VMEM((s, LANES), f32)] * 5,
        compiler_params=pltpu.CompilerParams(dimension_semantics=("parallel",)),
    )(proj, proj, proj, proj, o, lb, dmix)
    return outs


def _ssd_bwd(proj, y, states, dmix, conv_w, conv_b, dtb16, alog16, alog_f, d_f, nw):
    s = proj.shape[0]
    nc = s // CHUNK
    gw = D_SSM // N_GROUPS

    def body(xs_ref, bc_ref, xs_tail, bc_tail, dt_ref, z_ref, y_ref, st_ref, dm_ref, cw_ref, cb_ref, dtb_ref,
             alog16_ref, alogf_ref, df_ref, nw_ref, out_ref, gconv_ref, gvec_ref, gdt_ref,
             dh_scr, head_scr, xpad, dcpad, da_scr, dxdt_scr, dbc_scr):
        i = pl.program_id(0)
        c = nc - 1 - i

        @pl.when(i == 0)
        def _():
            dh_scr[...] = jnp.zeros_like(dh_scr)
            head_scr[...] = jnp.zeros_like(head_scr)
            gconv_ref[...] = jnp.zeros_like(gconv_ref)
            gvec_ref[...] = jnp.zeros_like(gvec_ref)
            gdt_ref[...] = jnp.zeros_like(gdt_ref)

        cv, sig, xbc, pre, dt_f, al_f, al_x, al_t = _ssd_common(
            xs_ref, bc_ref, xs_tail, bc_tail, dt_ref, cw_ref, cb_ref, dtb_ref, alog16_ref, xpad, c == 0)
        head0 = _iota((CHUNK, LANES), 1) < HEAD_DIM
        sub = _iota((CHUNK, LANES), 0)
        last_row = sub == CHUNK - 1

        yv, z, dmx = y_ref[...], z_ref[...], dm_ref[...]
        sz = _sigmoid(z)
        silu = z * sz
        yz = yv * silu
        dyz_parts = []
        for g in range(N_GROUPS):
            gs = slice(g * gw, (g + 1) * gw)
            part = yz[:, gs]
            r = lax.rsqrt(jnp.mean(part * part, axis=-1, keepdims=True) + EPS)
            nh = part * r
            gvec_ref[0:1, gs] += jnp.sum(dmx[:, gs] * nh, axis=0, keepdims=True)
            dn = dmx[:, gs] * nw_ref[:, gs]
            dyz_parts.append(r * (dn - nh * jnp.mean(dn * nh, axis=-1, keepdims=True)))
        dyz = jnp.concatenate(dyz_parts, axis=1)
        dy = dyz * silu
        out_ref[:, 0:D_SSM] = (dyz * yv * (sz * (1.0 + z * (1.0 - sz)))).astype(bf16)

        x_all = xbc[:, 0:D_SSM]
        gvec_ref[2:3, :] += jnp.sum(dy * x_all, axis=0, keepdims=True)

        for g in range(N_GROUPS):
            bm = xbc[:, D_SSM + g * D_STATE: D_SSM + (g + 1) * D_STATE].astype(bf16)
            cm = xbc[:, D_SSM + (N_GROUPS + g) * D_STATE: D_SSM + (N_GROUPS + g + 1) * D_STATE].astype(bf16)
            gmat = _nt(cm, bm)
            dgm = jnp.zeros((CHUNK, CHUNK), f32)
            db = jnp.zeros((CHUNK, D_STATE), f32)
            dc = jnp.zeros((CHUNK, D_STATE), f32)
            for pair in range(4 * g, 4 * g + 4):
                sl = slice(pair * LANES, (pair + 1) * LANES)
                xp, dtp, alp, dyp = x_all[:, sl], dt_f[:, sl], al_f[:, sl], dy[:, sl]
                xdt = xp * dtp
                xdt16 = xdt.astype(bf16)
                al_last = alp[CHUNK - 1:CHUNK, :]
                e_l = jnp.exp(alp)
                wf = jnp.exp(al_last - alp)
                e_last = jnp.exp(al_last)
                hp = st_ref[:, sl]
                hp16 = hp.astype(bf16)
                dhn = dh_scr[:, sl]
                dhn16 = dhn.astype(bf16)
                y_off = e_l * _nn(cm, hp16)
                dch16 = (dyp * e_l).astype(bf16)
                dc = dc + _nt(dch16, hp16)
                dh_out = _tn(cm, dch16)
                dal = dyp * y_off
                xw16 = (wf * xdt).astype(bf16)
                db = db + _nt(xw16, dhn16)
                dxw = _nn(bm, dhn16)
                dxdt = dxw * wf
                dwf = dxw * xdt * wf
                dal = dal - dwf
                dal_last = jnp.sum(dwf, axis=0, keepdims=True) + jnp.sum(dhn * hp, axis=0, keepdims=True) * e_last
                dh_scr[:, sl] = e_last * dhn + dh_out
                for h in range(2):
                    mh = head0 if h == 0 else jnp.logical_not(head0)
                    dyh16 = jnp.where(mh, dyp, 0.0).astype(bf16)
                    lmat = _decay_mat(al_x, al_t, pair, h)
                    mm = gmat * lmat
                    dmm = _nt(dyh16, xdt16)
                    dxdt = dxdt + _tn(mm.astype(bf16), dyh16)
                    n16 = (dmm * mm).astype(bf16)
                    jh = jnp.where(mh, 1.0 / HEAD_DIM, 0.0).astype(bf16)
                    dal = dal + _nn(n16, jh) - _tn(n16, jh)
                    dgm = dgm + dmm * lmat
                da_scr[:, sl] = dal + jnp.where(last_row, dal_last, 0.0)
                dxdt_scr[:, sl] = dxdt
            dgm16 = dgm.astype(bf16)
            dbc_scr[:, g * D_STATE:(g + 1) * D_STATE] = db + _tn(dgm16, cm)
            dbc_scr[:, (N_GROUPS + g) * D_STATE:(N_GROUPS + g + 1) * D_STATE] = dc + _nn(dgm16, bm)

        sub_c, lane_c = _iota((CHUNK, CHUNK), 0), _iota((CHUNK, CHUNK), 1)
        tri_t = (lane_c >= sub_c).astype(f32)
        dadt = _nn_hi(tri_t, da_scr[...])
        a_f = -jnp.exp(alogf_ref[...])
        dxdt_all = dxdt_scr[...]
        ddt_f = dxdt_all * x_all + a_f * dadt
        gvec_ref[1:2, :] += jnp.sum(dt_f * dadt, axis=0, keepdims=True) * a_f
        dx = df_ref[...] * dy + dxdt_all * dt_f
        row_h = _iota((D_SSM, LANES), 0) // HEAD_DIM
        fold = (row_h == _iota((D_SSM, LANES), 1)).astype(f32)
        ddt_raw = _nn_hi(ddt_f, fold) * _sigmoid(pre)
        gdt_ref[0:1, :] += jnp.sum(ddt_raw, axis=0, keepdims=True)
        out_ref[:, D_SSM + D_CONV:D_SSM + D_CONV + LANES] = ddt_raw.astype(bf16)
        out_ref[:, D_SSM + D_CONV + LANES:] = jnp.zeros((CHUNK, 3 * LANES), bf16)

        dsil = sig * (1.0 + cv * (1.0 - sig))
        dcv_x = dx * dsil[:, 0:D_SSM]
        dcv_bc = dbc_scr[...] * dsil[:, D_SSM:]
        dcpad[0:CHUNK, 0:D_SSM] = dcv_x
        dcpad[0:CHUNK, D_SSM:] = dcv_bc
        dcpad[CHUNK:, :] = head_scr[...]
        dcv = dcpad[0:CHUNK, :]
        gconv_ref[4:5, :] += jnp.sum(dcv, axis=0, keepdims=True)
        draw = jnp.zeros((CHUNK, D_CONV), f32)
        for j in range(4):
            gconv_ref[j:j + 1, :] += jnp.sum(dcv * xpad[pl.ds(5 + j, CHUNK), :], axis=0, keepdims=True)
            draw = draw + cw_ref[j:j + 1, :] * dcpad[pl.ds(3 - j, CHUNK), :]
        head_scr[...] = dcpad[0:8, :]
        out_ref[:, D_SSM:D_SSM + D_CONV] = draw.astype(bf16)

    order = lambda i: nc - 1 - i
    row = lambda w, cb=0: pl.BlockSpec((CHUNK, w), lambda i: (nc - 1 - i, cb))
    return pl.pallas_call(
        body, name="ssd_bwd", grid=(nc,),
        in_specs=_ssd_in_specs(order) + [row(D_SSM), pl.BlockSpec((None, D_STATE, D_SSM), lambda i: (nc - 1 - i, 0, 0)),
                                         row(D_SSM, 1), _full((4, D_CONV)), _full((1, D_CONV)), _full((1, LANES)),
                                         _full((1, LANES)), _full((1, D_SSM)), _full((1, D_SSM)), _full((1, D_SSM))],
        out_specs=[row(3072), _full((8, D_CONV)), _full((8, D_SSM)), _full((8, LANES))],
        out_shape=[SDS((s, 3072), bf16), SDS((8, D_CONV), f32), SDS((8, D_SSM), f32), SDS((8, LANES), f32)],
        scratch_shapes=[pltpu.VMEM((D_STATE, D_SSM), f32), pltpu.VMEM((8, D_CONV), f32),
                        pltpu.VMEM((8 + CHUNK, D_CONV), f32), pltpu.VMEM((8 + CHUNK, D_CONV), f32),
                        pltpu.VMEM((CHUNK, D_SSM), f32), pltpu.VMEM((CHUNK, D_SSM), f32),
                        pltpu.VMEM((CHUNK, 2 * N_GROUPS * D_STATE), f32)],
        compiler_params=pltpu.CompilerParams(dimension_semantics=("arbitrary",)),
    )(proj, proj, proj, proj, proj, proj, y, states, dmix, conv_w, conv_b, dtb16, alog16, alog_f, d_f, nw)


def _col_blocks(parts, tile):
    counts = [p.shape[1] // tile for p in parts]
    offs = [sum(counts[:t]) for t in range(len(parts))]
    return offs, counts, sum(counts)


def _inproj_bwd(dparts, wp, x, nw, dres):
    s, d = x.shape
    tm, tk = 1024, 1024
    offs, counts, nk = _col_blocks(dparts, tk)
    npart = len(dparts)

    def body(*refs):
        dp_refs = refs[:npart]
        w_ref, x_ref, nw_ref, dres_ref, gx_ref, gnw_ref, acc = refs[npart:]
        i, k = pl.program_id(0), pl.program_id(1)

        @pl.when(jnp.logical_and(i == 0, k == 0))
        def _():
            gnw_ref[...] = jnp.zeros_like(gnw_ref)

        @pl.when(k == 0)
        def _():
            acc[...] = jnp.zeros_like(acc)

        for t in range(npart):
            @pl.when(jnp.logical_and(k >= offs[t], k < offs[t] + counts[t]))
            def _(t=t):
                acc[...] += _nt(dp_refs[t][...], w_ref[...])

        @pl.when(k == nk - 1)
        def _():
            xv = x_ref[...]
            r = lax.rsqrt(jnp.mean(xv * xv, axis=-1, keepdims=True) + EPS)
            xn = xv * r
            du = acc[...]
            gnw_ref[0:1, :] += jnp.sum(du * xn, axis=0, keepdims=True)
            dn = du * nw_ref[...]
            gx_ref[...] = dres_ref[...] + r * (dn - xn * jnp.mean(dn * xn, axis=-1, keepdims=True))

    def piece(t):
        return pl.BlockSpec((tm, tk), lambda i, k: (i, jnp.clip(k - offs[t], 0, counts[t] - 1)))

    return pl.pallas_call(
        body, name="inproj_bwd", grid=(s // tm, nk),
        in_specs=[piece(t) for t in range(npart)] + [
            pl.BlockSpec((d, tk), lambda i, k: (0, k)),
            pl.BlockSpec((tm, d), lambda i, k: (i, 0)), pl.BlockSpec((1, d), lambda i, k: (0, 0)),
            pl.BlockSpec((tm, d), lambda i, k: (i, 0))],
        out_specs=[pl.BlockSpec((tm, d), lambda i, k: (i, 0)), pl.BlockSpec((8, d), lambda i, k: (0, 0))],
        out_shape=[SDS((s, d), f32), SDS((8, d), f32)],
        scratch_shapes=[pltpu.VMEM((tm, d), f32)],
        compiler_params=pltpu.CompilerParams(dimension_semantics=("arbitrary", "arbitrary")),
    )(*dparts, wp, x, nw, dres)


def _matmul_tn(a_parts, b_parts, name):
    tile, tk = 1024, 512
    s = a_parts[0].shape[0]
    nk = s // tk
    na, nb = len(a_parts), len(b_parts)
    offs, counts, nj = _col_blocks(b_parts, tile)

    def body(*refs):
        a_refs, b_refs, o_ref = refs[:na], refs[na:na + nb], refs[na + nb]
        i, j = pl.program_id(0), pl.program_id(1)

        @pl.when(pl.program_id(2) == 0)
        def _():
            o_ref[...] = jnp.zeros_like(o_ref)

        for ta in range(na):
            for tb in range(nb):
                @pl.when(jnp.logical_and(i == ta, jnp.logical_and(j >= offs[tb], j < offs[tb] + counts[tb])))
                def _(ta=ta, tb=tb):
                    o_ref[...] += _tn(a_refs[ta][...], b_refs[tb][...])

    def a_spec(t):
        return pl.BlockSpec((tk, tile), lambda i, j, k: (jnp.where(i == t, k, 0), 0))

    def b_spec(t):
        def index(i, j, k):
            mine = jnp.logical_and(j >= offs[t], j < offs[t] + counts[t])
            return jnp.where(mine, k, 0), jnp.clip(j - offs[t], 0, counts[t] - 1)
        return pl.BlockSpec((tk, tile), index)

    return pl.pallas_call(
        body, name=name, grid=(na, nj, nk),
        in_specs=[a_spec(t) for t in range(na)] + [b_spec(t) for t in range(nb)],
        out_specs=pl.BlockSpec((tile, tile), lambda i, j, k: (i, j)),
        out_shape=SDS((na * tile, nj * tile), f32),
        compiler_params=pltpu.CompilerParams(dimension_semantics=("parallel", "parallel", "arbitrary")),
    )(*a_parts, *b_parts)


def _adamw(w, g, m, v):
    m = ADAM_B1 * m + (1.0 - ADAM_B1) * g
    v = ADAM_B2 * v + (1.0 - ADAM_B2) * (g * g)
    m_hat = m / (1.0 - ADAM_B1 ** ADAM_STEP)
    v_hat = v / (1.0 - ADAM_B2 ** ADAM_STEP)
    delta = -ADAM_LR * (m_hat / (jnp.sqrt(v_hat) + ADAM_EPS) + ADAM_WD * w)
    return delta, m, v


def _sum_adamw(own, parts, w, m, v, name):
    r, c = w.shape
    tr = 128

    def body(o_ref, p_ref, w_ref, m_ref, v_ref, g_ref, d_ref, nm_ref, nv_ref):
        my_q = 2 * lax.axis_index("x") + lax.axis_index("y")
        own_v = o_ref[...]
        g = jnp.where(my_q == 0, own_v, p_ref[0].astype(f32))
        for q in range(1, 4):
            g = g + jnp.where(my_q == q, own_v, p_ref[q].astype(f32))
        g_ref[...] = g
        d_ref[...], nm_ref[...], nv_ref[...] = _adamw(w_ref[...], g, m_ref[...], v_ref[...])

    blk = pl.BlockSpec((tr, c), lambda i: (i, 0))
    return pl.pallas_call(
        body, name=name, grid=(r // tr,),
        in_specs=[blk, pl.BlockSpec((4, tr, c), lambda i: (0, i, 0)), blk, blk, blk],
        out_specs=[blk] * 4, out_shape=[SDS((r, c), f32)] * 4,
        compiler_params=pltpu.CompilerParams(dimension_semantics=("parallel",)),
    )(own, parts, w, m, v)


def _sum_small(parts):
    def body(p_ref, o_ref):
        t = p_ref[0]
        for j in range(1, N_DEV):
            t = t + p_ref[j]
        o_ref[...] = t
        row_h = _iota((D_SSM, LANES), 0) // HEAD_DIM
        fold = (row_h == _iota((D_SSM, LANES), 1)).astype(f32)
        lower = t[8:16, 0:LANES]
        folded = _nn_hi(t[8:16, 0:D_SSM], fold)
        loss = jnp.sum(t[11:12, 0:D_MODEL], axis=1, keepdims=True) * (0.5 / D_MODEL)
        row = _iota((8, LANES), 0)
        o_ref[8:16, 0:LANES] = jnp.where(row < 2, folded, jnp.where(row == 4, loss, lower))

    return pl.pallas_call(body, name="sum_small", out_shape=SDS((PACK_ROWS, PACK_W), f32),
                          in_specs=[pl.BlockSpec(memory_space=pltpu.VMEM)],
                          out_specs=pl.BlockSpec(memory_space=pltpu.VMEM))(parts)


def _adamw_small(w, g, m, v):
    def body(w_ref, g_ref, m_ref, v_ref, d_ref, nm_ref, nv_ref):
        d_ref[...], nm_ref[...], nv_ref[...] = _adamw(w_ref[...], g_ref[...], m_ref[...], v_ref[...])

    vm = pl.BlockSpec(memory_space=pltpu.VMEM)
    return pl.pallas_call(body, name="adamw_small", out_shape=[SDS(w.shape, f32)] * 3,
                          in_specs=[vm] * 4, out_specs=[vm] * 3)(w, g, m, v)


def _pad_lanes(v, width):
    return jnp.pad(v, ((0, 0), (0, width - v.shape[1])))


def _local_step(x, tgt, norm_pre_w, wp, conv_w, conv_b, dt_bias, a_log, d_skip, ssm_norm_w, wo, norm_post_w):
    dtb16 = _pad_lanes(dt_bias, LANES)
    alog16 = _pad_lanes(a_log, LANES)
    alog_f = jnp.repeat(a_log, HEAD_DIM, axis=1)
    d_f = jnp.repeat(d_skip, HEAD_DIM, axis=1)

    proj, u = _prenorm_inproj(x, norm_pre_w, wp)
    o, lb, mix_a = _attn_fwd(proj)
    mix_s, y, states = _ssd_fwd(proj, conv_w, conv_b, dtb16, alog16, alog_f, d_f, ssm_norm_w)
    dmix, dout, dres, acc_post = _outproj_loss(mix_a, mix_s, wo, x, tgt, norm_post_w)
    dq, dk, dv, dg = _attn_bwd(proj, o, lb, dmix)
    dzxd, g_conv, g_vec, g_dt = _ssd_bwd(proj, y, states, dmix, conv_w, conv_b, dtb16, alog16, alog_f, d_f, ssm_norm_w)
    dparts = [dq, dk, dv, dg, dzxd]
    dw_out = _matmul_tn([mix_a, mix_s], [dout], "dw_out")
    dw_in = _matmul_tn([u], dparts, "dw_in")
    grad_x, g_pre = _inproj_bwd(dparts, wp, x, norm_pre_w, dres)

    rows = [g_conv[0:5], _pad_lanes(g_pre[0:1], PACK_W), _pad_lanes(g_vec[0:1], PACK_W),
            _pad_lanes(acc_post[1:2], PACK_W), _pad_lanes(g_vec[1:3], PACK_W), _pad_lanes(g_dt[0:1], PACK_W),
            _pad_lanes(acc_post[0:1], PACK_W), jnp.zeros((4, PACK_W), f32)]
    return grad_x, dw_in, dw_out, jnp.concatenate(rows, axis=0)


def kernel(x, norm_pre_w, w_in, conv_w, conv_b, dt_bias, a_log, d_skip, ssm_norm_w, w_out, norm_post_w, loss_target, m_norm_pre_w, m_w_in, m_conv_w, m_conv_b, m_dt_bias, m_a_log, m_d_skip, m_ssm_norm_w, m_w_out, m_norm_post_w, v_norm_pre_w, v_w_in, v_conv_w, v_conv_b, v_dt_bias, v_a_log, v_d_skip, v_ssm_norm_w, v_w_out, v_norm_post_w):
    shard_in = w_in.shape[2]
    shard_cv = conv_w.shape[2]
    me = 4 * lax.axis_index("x") + 2 * lax.axis_index("y") + lax.axis_index("c")

    g_in, g_out, g_cw = _all_gather([w_in[0].astype(bf16), w_out[0].astype(bf16), conv_w[0]])
    wp = jnp.pad(g_in.transpose(1, 0, 2).reshape(D_MODEL, N_DEV * shard_in), ((0, 0), (0, NP - N_DEV * shard_in)))
    wo = g_out.reshape(N_DEV * w_out.shape[1], D_MODEL)
    cw = g_cw.transpose(1, 0, 2).reshape(4, D_CONV)

    grad_x, dw_in, dw_out, pack = _local_step(
        x[0], loss_target[0], norm_pre_w, wp, cw, conv_b, dt_bias, a_log, d_skip, ssm_norm_w, wo, norm_post_w)

    send_in = dw_in[:, :N_DEV * shard_in].reshape(D_MODEL, 4, 2, shard_in).transpose(2, 1, 0, 3)
    send_out = dw_out.reshape(4, 2, w_out.shape[1], D_MODEL).transpose(1, 0, 2, 3)
    got_in, got_out, parts_small = _sibling_swap([send_in, send_out], pack)
    chip_in, own_in = _chip_sum(send_in, got_in, "chip_sum_w_in")
    chip_out, own_out = _chip_sum(send_out, got_out, "chip_sum_w_out")
    parts_in, parts_out = _chip_exchange([chip_in, chip_out])

    g_w_in, d_w_in, nm_w_in, nv_w_in = _sum_adamw(own_in, parts_in, w_in[0], m_w_in[0], v_w_in[0], "sum_adamw_w_in")
    g_w_out, d_w_out, nm_w_out, nv_w_out = _sum_adamw(own_out, parts_out, w_out[0], m_w_out[0], v_w_out[0], "sum_adamw_w_out")
    tot = _sum_small(parts_small)

    g_cw_all = tot[0:4]
    small_g = {
        "conv_w": lax.dynamic_slice(g_cw_all, (0, me * shard_cv), (4, shard_cv)),
        "conv_b": tot[4:5], "norm_pre_w": tot[5:6, :D_MODEL], "ssm_norm_w": tot[6:7, :D_SSM],
        "norm_post_w": tot[7:8, :D_MODEL], "a_log": tot[8:9, :16], "d_skip": tot[9:10, :16], "dt_bias": tot[10:11, :16],
    }
    loss = tot[12, 0]
    small_w = {"conv_w": (conv_w[0], m_conv_w[0], v_conv_w[0]), "conv_b": (conv_b, m_conv_b, v_conv_b),
               "norm_pre_w": (norm_pre_w, m_norm_pre_w, v_norm_pre_w), "ssm_norm_w": (ssm_norm_w, m_ssm_norm_w, v_ssm_norm_w),
               "norm_post_w": (norm_post_w, m_norm_post_w, v_norm_post_w), "a_log": (a_log, m_a_log, v_a_log),
               "d_skip": (d_skip, m_d_skip, v_d_skip), "dt_bias": (dt_bias, m_dt_bias, v_dt_bias)}
    names = list(small_w)
    sizes = [small_g[k].size for k in names]
    tot_size = sum(sizes)
    pad_to = -(-tot_size // 1024) * 1024

    def flat(arrs):
        v = jnp.concatenate([a.reshape(-1) for a in arrs])
        return jnp.pad(v, (0, pad_to - tot_size)).reshape(pad_to // LANES, LANES)

    fw = flat([small_w[k][0] for k in names])
    fg = flat([small_g[k] for k in names])
    fm = flat([small_w[k][1] for k in names])
    fv = jnp.pad(jnp.concatenate([small_w[k][2].reshape(-1) for k in names]), (0, pad_to - tot_size),
                 constant_values=1.0).reshape(pad_to // LANES, LANES)
    fd, fnm, fnv = _adamw_small(fw, fg, fm, fv)

    def unflat(f):
        out, off = {}, 0
        v = f.reshape(-1)
        for k, n in zip(names, sizes):
            out[k] = v[off:off + n].reshape(small_g[k].shape)
            off += n
        return out

    sd, snm, snv = unflat(fd), unflat(fnm), unflat(fnv)
    lead = lambda a: a[None]
    order = ["norm_pre_w", "w_in", "conv_w", "conv_b", "dt_bias", "a_log", "d_skip", "ssm_norm_w", "w_out", "norm_post_w"]
    grads = dict(small_g, w_in=g_w_in, w_out=g_w_out)
    deltas = dict(sd, w_in=d_w_in, w_out=d_w_out)
    new_m = dict(snm, w_in=nm_w_in, w_out=nm_w_out)
    new_v = dict(snv, w_in=nv_w_in, w_out=nv_w_out)

    def shaped(dct, k):
        a = dct[k]
        return lead(a) if k in ("w_in", "w_out", "conv_w") else a

    return (loss, grad_x[None], *[shaped(grads, k) for k in order], *[shaped(deltas, k) for k in order],
            *[shaped(new_m, k) for k in order], *[shaped(new_v, k) for k in order])
```

```python
import functools
import math

import jax
import jax.numpy as jnp
import numpy as np
from jax import lax
from jax.experimental import pallas as pl
from jax.experimental.pallas import tpu as pltpu

f32, bf16 = jnp.float32, jnp.bfloat16
SDS = jax.ShapeDtypeStruct
HIGHEST = lax.Precision.HIGHEST
MESH = pl.DeviceIdType.MESH

N_DEV = 8
D_MODEL = 1024
D_ATTN = 1024
D_SSM = 1024
HEAD_DIM = 64
N_PAIRS = 8
D_STATE = 128
N_GROUPS = 2
D_CONV = D_SSM + 2 * N_GROUPS * D_STATE
D_IN_PROJ = 4 * D_ATTN + D_SSM + D_CONV + 16
NP = 7168
CHUNK = 128
BLK = 128
DILATIONS = (1, 4, 16)
EPS = 1e-6
LANES = 128
COL_Z, COL_XS, COL_BC, COL_DT = 4096, 5120, 6144, 6656

ADAM_LR, ADAM_B1, ADAM_B2, ADAM_EPS, ADAM_WD, ADAM_STEP = 0.001, 0.9, 0.999, 1e-08, 0.01, 10

PACK_ROWS, PACK_W = 16, 1536


def _nt(a, b):
    return lax.dot_general(a, b, (((1,), (1,)), ((), ())), preferred_element_type=f32)


def _tn(a, b):
    return lax.dot_general(a, b, (((0,), (0,)), ((), ())), preferred_element_type=f32)


def _nn(a, b):
    return jnp.dot(a, b, preferred_element_type=f32)


def _nn_hi(a, b):
    return jnp.dot(a, b, precision=HIGHEST, preferred_element_type=f32)


def _sigmoid(x):
    return 1.0 / (1.0 + jnp.exp(-x))


def _softplus(x):
    return jnp.maximum(x, 0.0) + jnp.log1p(jnp.exp(-jnp.abs(x)))


def _iota(shape, dim):
    return lax.broadcasted_iota(jnp.int32, shape, dim)


def _my_pos():
    return lax.axis_index("x"), lax.axis_index("y"), lax.axis_index("c")


def _all_gather(arrs):
    n = len(arrs)

    def body(*refs):
        ins, outs = refs[:n], refs[n:2 * n]
        send_sems, recv_sems, local_sems = refs[2 * n:]
        x, y, c = _my_pos()
        me, sibling = (x, y, c), (x, y, 1 - c)
        chips = [(1 - x, y), (x, 1 - y), (1 - x, 1 - y)]

        def slot(a, px, py, pc):
            return outs[a].at[4 * px + 2 * py + pc]

        def copy(a, k, block, to, src=None):
            return pltpu.make_async_remote_copy(
                src_ref=slot(a, *block) if src is None else src, dst_ref=slot(a, *block),
                send_sem=send_sems.at[7 * a + k], recv_sem=recv_sems.at[7 * a + k],
                device_id=to, device_id_type=MESH)

        mine = [pltpu.make_async_copy(ins[a], slot(a, *me), local_sems.at[a]) for a in range(n)]
        for cp in mine:
            cp.start()
        first = []
        for a in range(n):
            first.append(copy(a, 0, me, sibling, src=ins[a]))
            first += [copy(a, 1 + j, me, (*chip, c), src=ins[a]) for j, chip in enumerate(chips)]
        for cp in first:
            cp.start()
        passed = []
        for j, chip in enumerate(chips):
            for a in range(n):
                copy(a, 1 + j, (*chip, c), me).wait_recv()
                cp = copy(a, 4 + j, (*chip, c), sibling)
                cp.start()
                passed.append(cp)
        for a in range(n):
            copy(a, 0, sibling, me).wait_recv()
            for j, chip in enumerate(chips):
                copy(a, 4 + j, (*chip, 1 - c), me).wait_recv()
        for cp in first + passed:
            cp.wait_send()
        for cp in mine:
            cp.wait()

    anyspec = pl.BlockSpec(memory_space=pl.ANY)
    return pl.pallas_call(
        body, name="weights_all_gather",
        out_shape=[SDS((N_DEV,) + a.shape, a.dtype) for a in arrs],
        in_specs=[anyspec] * n, out_specs=[anyspec] * n,
        scratch_shapes=[pltpu.SemaphoreType.DMA((7 * n,)), pltpu.SemaphoreType.DMA((7 * n,)),
                        pltpu.SemaphoreType.DMA((n,))],
    )(*arrs)


def _sibling_swap(bigs, small):
    nb = len(bigs)

    def body(*refs):
        ins, small_in = refs[:nb], refs[nb]
        outs, small_out = refs[nb + 1:2 * nb + 1], refs[2 * nb + 1]
        send_sems, recv_sems, local_sem = refs[2 * nb + 2:]
        x, y, c = _my_pos()
        me = 4 * x + 2 * y + c
        mine = pltpu.make_async_copy(small_in, small_out.at[me], local_sem)
        mine.start()
        sends = []
        for a in range(nb):
            cp = pltpu.make_async_remote_copy(
                src_ref=ins[a].at[1 - c], dst_ref=outs[a], send_sem=send_sems.at[a], recv_sem=recv_sems.at[a],
                device_id=(x, y, 1 - c), device_id_type=MESH)
            cp.start()
            sends.append(cp)
        for k in range(1, N_DEV):
            to = (me + k) % N_DEV
            cp = pltpu.make_async_remote_copy(
                src_ref=small_in, dst_ref=small_out.at[me],
                send_sem=send_sems.at[nb + k - 1], recv_sem=recv_sems.at[nb + k - 1],
                device_id=(to // 4, (to // 2) % 2, to % 2), device_id_type=MESH)
            cp.start()
            sends.append(cp)
        for a in range(nb):
            pltpu.make_async_remote_copy(
                src_ref=ins[a].at[c], dst_ref=outs[a], send_sem=send_sems.at[a], recv_sem=recv_sems.at[a],
                device_id=(x, y, c), device_id_type=MESH).wait_recv()
        for k in range(1, N_DEV):
            frm = (me + N_DEV - k) % N_DEV
            pltpu.make_async_remote_copy(
                src_ref=small_in, dst_ref=small_out.at[frm],
                send_sem=send_sems.at[nb + k - 1], recv_sem=recv_sems.at[nb + k - 1],
                device_id=(x, y, c), device_id_type=MESH).wait_recv()
        for cp in sends:
            cp.wait_send()
        mine.wait()

    anyspec = pl.BlockSpec(memory_space=pl.ANY)
    out_shape = [SDS(a.shape[1:], a.dtype) for a in bigs] + [SDS((N_DEV,) + small.shape, small.dtype)]
    return pl.pallas_call(
        body, name="grad_sibling_swap", out_shape=out_shape,
        in_specs=[anyspec] * (nb + 1), out_specs=[anyspec] * (nb + 1),
        scratch_shapes=[pltpu.SemaphoreType.DMA((nb + 7,)), pltpu.SemaphoreType.DMA((nb + 7,)),
                        pltpu.SemaphoreType.DMA(())],
    )(*bigs, small)


def _chip_sum(mine, got, name):
    _, nq, r, cdim = mine.shape
    tc = 256

    def body(m_ref, g_ref, s16_ref, own_ref):
        q = pl.program_id(1)
        c = lax.axis_index("c")
        my_q = 2 * lax.axis_index("x") + lax.axis_index("y")
        tot = m_ref[c] + g_ref[...]
        s16_ref[...] = tot.astype(bf16)

        @pl.when(q == my_q)
        def _():
            own_ref[...] = tot

    return pl.pallas_call(
        body, name=name, grid=(cdim // tc, nq),
        in_specs=[pl.BlockSpec((2, None, r, tc), lambda i, q: (0, q, 0, i)),
                  pl.BlockSpec((None, r, tc), lambda i, q: (q, 0, i))],
        out_specs=[pl.BlockSpec((None, r, tc), lambda i, q: (q, 0, i)), pl.BlockSpec((r, tc), lambda i, q: (0, i))],
        out_shape=[SDS((nq, r, cdim), bf16), SDS((r, cdim), f32)],
        compiler_params=pltpu.CompilerParams(dimension_semantics=("parallel", "arbitrary")),
    )(mine, got)


def _chip_exchange(bigs):
    nb = len(bigs)

    def body(*refs):
        ins, outs = refs[:nb], refs[nb:2 * nb]
        send_sems, recv_sems, local_sems = refs[2 * nb:]
        x, y, c = _my_pos()
        my_q = 2 * x + y
        mine = [pltpu.make_async_copy(ins[a].at[my_q], outs[a].at[my_q], local_sems.at[a]) for a in range(nb)]
        for cp in mine:
            cp.start()
        sends = []
        for k in range(1, 4):
            to = (my_q + k) % 4
            for a in range(nb):
                cp = pltpu.make_async_remote_copy(
                    src_ref=ins[a].at[to], dst_ref=outs[a].at[my_q],
                    send_sem=send_sems.at[3 * a + k - 1], recv_sem=recv_sems.at[3 * a + k - 1],
                    device_id=(to // 2, to % 2, c), device_id_type=MESH)
                cp.start()
                sends.append(cp)
        for k in range(1, 4):
            frm = (my_q + 4 - k) % 4
            for a in range(nb):
                pltpu.make_async_remote_copy(
                    src_ref=ins[a].at[frm], dst_ref=outs[a].at[frm],
                    send_sem=send_sems.at[3 * a + k - 1], recv_sem=recv_sems.at[3 * a + k - 1],
                    device_id=(x, y, c), device_id_type=MESH).wait_recv()
        for cp in sends:
            cp.wait_send()
        for cp in mine:
            cp.wait()

    anyspec = pl.BlockSpec(memory_space=pl.ANY)
    return pl.pallas_call(
        body, name="grad_chip_exchange", out_shape=[SDS(a.shape, a.dtype) for a in bigs],
        in_specs=[anyspec] * nb, out_specs=[anyspec] * nb,
        scratch_shapes=[pltpu.SemaphoreType.DMA((3 * nb,)), pltpu.SemaphoreType.DMA((3 * nb,)),
                        pltpu.SemaphoreType.DMA((nb,))],
    )(*bigs)


def _prenorm_inproj(x, nw, wt):
    s, d = x.shape
    npad = wt.shape[0]
    tm, tn = 1024, 512

    def body(x_ref, nw_ref, w_ref, proj_ref, u_ref):
        @pl.when(pl.program_id(1) == 0)
        def _():
            xv = x_ref[...]
            r = lax.rsqrt(jnp.mean(xv * xv, axis=-1, keepdims=True) + EPS)
            u_ref[...] = (xv * r * nw_ref[...]).astype(bf16)
        proj_ref[...] = _nt(u_ref[...], w_ref[...])

    return pl.pallas_call(
        body, name="prenorm_inproj", grid=(s // tm, npad // tn),
        in_specs=[pl.BlockSpec((tm, d), lambda i, j: (i, 0)), pl.BlockSpec((1, d), lambda i, j: (0, 0)),
                  pl.BlockSpec((tn, d), lambda i, j: (j, 0))],
        out_specs=[pl.BlockSpec((tm, tn), lambda i, j: (i, j)), pl.BlockSpec((tm, d), lambda i, j: (i, 0))],
        out_shape=[SDS((s, npad), f32), SDS((s, d), bf16)],
        compiler_params=pltpu.CompilerParams(dimension_semantics=("parallel", "arbitrary")),
    )(x, nw, wt)


def _attn_consts():
    head0 = _iota((BLK, LANES), 1) < HEAD_DIM
    tri2 = (_iota((BLK, 2 * LANES), 1) % LANES) <= _iota((BLK, 2 * LANES), 0)
    ones2 = ((_iota((LANES, 2 * LANES), 0) < HEAD_DIM) == (_iota((LANES, 2 * LANES), 1) < LANES)).astype(bf16)
    rmat = ((_iota((2 * LANES, LANES), 0) < LANES) == (_iota((2 * LANES, LANES), 1) < HEAD_DIM)).astype(bf16)
    bones = ((_iota((LANES, LANES), 0) < HEAD_DIM) == (_iota((LANES, LANES), 1) < HEAD_DIM)).astype(bf16)
    return head0, tri2, ones2, rmat, bones


def _stack_heads(x16, head0):
    zero = jnp.zeros_like(x16)
    return jnp.concatenate([jnp.where(head0, x16, zero), jnp.where(head0, zero, x16)], axis=0)


def _split_dot(x, w16):
    hi = x.astype(bf16)
    lo = (x - hi.astype(f32)).astype(bf16)
    return _nn(hi, w16) + _nn(lo, w16)


def _attn_fwd(proj):
    s = proj.shape[0]
    n_it = s // BLK

    def body(q_ref, k_ref, v_ref, g_ref, o_ref, l_ref, mix_ref, op0, op1, op2, lp0, lp1, lp2):
        op_refs, lp_refs = (op0, op1, op2), (lp0, lp1, lp2)
        head0, tri2, ones2, rmat, _ = _attn_consts()
        for p, d in enumerate(DILATIONS):
            nb = s // (BLK * d)

            def it(i, carry, d=d, nb=nb, p=p):
                r, blk = i // nb, i % nb
                st = blk * (BLK * d) + r
                stp = jnp.maximum(blk - 1, 0) * (BLK * d) + r
                rows = pl.ds(st, BLK, stride=d)
                rows_p = pl.ds(stp, BLK, stride=d)
                has_prev = blk > 0
                qs = q_ref[rows, :] * 0.125
                kc, kp = k_ref[rows, :], k_ref[rows_p, :]
                vc, vp = v_ref[rows, :], v_ref[rows_p, :]
                qs16 = qs.astype(bf16)
                sc = _nt(qs16, _stack_heads(kc.astype(bf16), head0))
                sp = _nt(qs16, _stack_heads(kp.astype(bf16), head0))
                sc = jnp.where(tri2, sc, jnp.where(has_prev, sp, -jnp.inf))
                sd2 = jnp.where(has_prev, _split_dot(qs * kp, ones2), -jnp.inf)
                m0 = jnp.max(sc[:, :LANES], axis=1, keepdims=True)
                m1 = jnp.max(sc[:, LANES:], axis=1, keepdims=True)
                m2 = jnp.concatenate([jnp.broadcast_to(m0, (BLK, LANES)), jnp.broadcast_to(m1, (BLK, LANES))], axis=1)
                m2 = jnp.maximum(m2, sd2)
                pt16 = jnp.exp(sc - m2).astype(bf16)
                m_pair = jnp.where(head0, m2[:, :LANES], m2[:, LANES:])
                pd = jnp.exp(jnp.where(head0, sd2[:, :LANES], sd2[:, LANES:]) - m_pair)
                zero = jnp.zeros_like(pt16)
                o = (_nn(jnp.where(tri2, pt16, zero), _stack_heads(vc.astype(bf16), head0))
                     + _nn(jnp.where(tri2, zero, pt16), _stack_heads(vp.astype(bf16), head0)) + pd * vp)
                l = _nn(pt16, rmat) + pd
                op_refs[p][rows, :] = o / l
                lp_refs[p][rows, :] = m_pair + jnp.log(l)
                return carry

            lax.fori_loop(0, n_it, it, 0, unroll=2)

        def merge(i, carry):
            rows = pl.ds(pl.multiple_of(i * 256, 256), 256)
            l0, l1, l2 = lp0[rows, :], lp1[rows, :], lp2[rows, :]
            m = jnp.maximum(jnp.maximum(l0, l1), l2)
            e0, e1, e2 = jnp.exp(l0 - m), jnp.exp(l1 - m), jnp.exp(l2 - m)
            z = e0 + e1 + e2
            o = (e0 * op0[rows, :] + e1 * op1[rows, :] + e2 * op2[rows, :]) / z
            o_ref[rows, :] = o
            l_ref[rows, :] = m + jnp.log(z)
            g = g_ref[rows, :]
            mix_ref[rows, :] = (o * (g * _sigmoid(g))).astype(bf16)
            return carry

        lax.fori_loop(0, s // 256, merge, 0)

    col = lambda base: pl.BlockSpec((s, LANES), lambda h: (0, base + h))
    return pl.pallas_call(
        body, name="attn_fwd", grid=(N_PAIRS,),
        in_specs=[col(0), col(8), col(16), col(24)],
        out_specs=[col(0), col(0), col(0)],
        out_shape=[SDS((s, D_ATTN), f32), SDS((s, D_ATTN), f32), SDS((s, D_ATTN), bf16)],
        scratch_shapes=[pltpu.VMEM((s, LANES), f32)] * 6,
        compiler_params=pltpu.CompilerParams(dimension_semantics=("parallel",)),
    )(proj, proj, proj, proj)


def _expand_mat():
    row = _iota((LANES, 2 * D_SSM), 0)
    colv = _iota((LANES, 2 * D_SSM), 1)
    head = 2 * ((colv % D_SSM) // LANES) + colv // D_SSM
    return (row == head).astype(f32)


def _ssd_common(xs_ref, bc_ref, xs_tail, bc_tail, dt_ref, cw_ref, cb_ref, dtb_ref, alog16_ref, xpad, first):
    keep = jnp.where(first, 0.0, 1.0)
    xpad[0:8, 0:D_SSM] = xs_tail[...] * keep
    xpad[0:8, D_SSM:D_CONV] = bc_tail[...] * keep
    xpad[8:8 + CHUNK, 0:D_SSM] = xs_ref[...]
    xpad[8:8 + CHUNK, D_SSM:D_CONV] = bc_ref[...]
    cv = cb_ref[...] + cw_ref[0:1, :] * xpad[pl.ds(5, CHUNK), :]
    for j in range(1, 4):
        cv = cv + cw_ref[j:j + 1, :] * xpad[pl.ds(5 + j, CHUNK), :]
    sig = _sigmoid(cv)
    xbc = cv * sig

    pre = dt_ref[...] + dtb_ref[...]
    dt16 = _softplus(pre)
    a16 = -jnp.exp(alog16_ref[...])
    sub, lane = _iota((CHUNK, CHUNK), 0), _iota((CHUNK, CHUNK), 1)
    tri = (sub >= lane).astype(f32)
    al16 = _nn_hi(tri, dt16 * a16)
    al_t = al16.T
    emat = _expand_mat()
    dt_x = _nn_hi(dt16, emat)
    al_x = _nn_hi(al16, emat)
    lane_w = _iota((CHUNK, D_SSM), 1)
    even = (lane_w % LANES) < HEAD_DIM
    dt_f = jnp.where(even, dt_x[:, :D_SSM], dt_x[:, D_SSM:])
    al_f = jnp.where(even, al_x[:, :D_SSM], al_x[:, D_SSM:])
    return cv, sig, xbc, pre, dt_f, al_f, al_x, al_t


def _decay_mat(al_x, al_t, pair, h):
    sub, lane = _iota((CHUNK, CHUNK), 0), _iota((CHUNK, CHUNK), 1)
    col = al_x[:, h * D_SSM + pair * LANES: h * D_SSM + (pair + 1) * LANES]
    row = al_t[2 * pair + h: 2 * pair + h + 1, :]
    return jnp.exp(jnp.where(sub >= lane, col - row, -jnp.inf))


def _ssd_in_specs(order):
    blk = lambda w, cb: pl.BlockSpec((CHUNK, w), lambda i: (order(i), cb))
    tail = lambda w, cb: pl.BlockSpec((8, w), lambda i: (jnp.maximum(16 * order(i) - 1, 0), cb))
    return [blk(D_SSM, COL_XS // D_SSM), blk(512, COL_BC // 512), tail(D_SSM, COL_XS // D_SSM),
            tail(512, COL_BC // 512), blk(LANES, COL_DT // LANES), blk(D_SSM, COL_Z // D_SSM)]


def _full(shape):
    return pl.BlockSpec(shape, lambda i: (0,) * len(shape))


def _ssd_fwd(proj, conv_w, conv_b, dtb16, alog16, alog_f, d_f, nw):
    s = proj.shape[0]
    nc = s // CHUNK

    def body(xs_ref, bc_ref, xs_tail, bc_tail, dt_ref, z_ref, cw_ref, cb_ref, dtb_ref, alog16_ref, alogf_ref,
             df_ref, nw_ref, mix_ref, y_ref, st_ref, h_scr, xpad, y_scr):
        c = pl.program_id(0)

        @pl.when(c == 0)
        def _():
            h_scr[...] = jnp.zeros_like(h_scr)

        _, _, xbc, _, dt_f, al_f, al_x, al_t = _ssd_common(
            xs_ref, bc_ref, xs_tail, bc_tail, dt_ref, cw_ref, cb_ref, dtb_ref, alog16_ref, xpad, c == 0)
        head0 = _iota((CHUNK, LANES), 1) < HEAD_DIM
        st_ref[...] = h_scr[...]
        for g in range(N_GROUPS):
            bm = xbc[:, D_SSM + g * D_STATE: D_SSM + (g + 1) * D_STATE].astype(bf16)
            cm = xbc[:, D_SSM + (N_GROUPS + g) * D_STATE: D_SSM + (N_GROUPS + g + 1) * D_STATE].astype(bf16)
            gmat = _nt(cm, bm)
            for pair in range(4 * g, 4 * g + 4):
                sl = slice(pair * LANES, (pair + 1) * LANES)
                xp, dtp, alp = xbc[:, sl], dt_f[:, sl], al_f[:, sl]
                xdt = xp * dtp
                xdt16 = xdt.astype(bf16)
                al_last = alp[CHUNK - 1:CHUNK, :]
                hp = h_scr[:, sl]
                y_off = jnp.exp(alp) * _nn(cm, hp.astype(bf16))
                yd = [_nn((gmat * _decay_mat(al_x, al_t, pair, h)).astype(bf16), xdt16) for h in range(2)]
                y_scr[:, sl] = jnp.where(head0, yd[0], yd[1]) + y_off + df_ref[:, sl] * xp
                st = _tn(bm, (jnp.exp(al_last - alp) * xdt).astype(bf16))
                h_scr[:, sl] = jnp.exp(al_last) * hp + st
        y = y_scr[...]
        y_ref[...] = y
        z = z_ref[...]
        yz = y * (z * _sigmoid(z))
        gw = D_SSM // N_GROUPS
        for g in range(N_GROUPS):
            part = yz[:, g * gw:(g + 1) * gw]
            r = lax.rsqrt(jnp.mean(part * part, axis=-1, keepdims=True) + EPS)
            mix_ref[:, g * gw:(g + 1) * gw] = (part * r * nw_ref[:, g * gw:(g + 1) * gw]).astype(bf16)

    order = lambda i: i
    row = lambda w: pl.BlockSpec((CHUNK, w), lambda i: (i, 0))
    return pl.pallas_call(
        body, name="ssd_fwd", grid=(nc,),
        in_specs=_ssd_in_specs(order) + [_full((4, D_CONV)), _full((1, D_CONV)), _full((1, LANES)), _full((1, LANES)),
                                         _full((1, D_SSM)), _full((1, D_SSM)), _full((1, D_SSM))],
        out_specs=[row(D_SSM), row(D_SSM), pl.BlockSpec((None, D_STATE, D_SSM), lambda i: (i, 0, 0))],
        out_shape=[SDS((s, D_SSM), bf16), SDS((s, D_SSM), f32), SDS((nc, D_STATE, D_SSM), f32)],
        scratch_shapes=[pltpu.VMEM((D_STATE, D_SSM), f32), pltpu.VMEM((8 + CHUNK, D_CONV), f32),
                        pltpu.VMEM((CHUNK, D_SSM), f32)],
        compiler_params=pltpu.CompilerParams(dimension_semantics=("arbitrary",)),
    )(proj, proj, proj, proj, proj, proj, conv_w, conv_b, dtb16, alog16, alog_f, d_f, nw)


def _outproj_loss(mix_a, mix_s, wo, x, tgt, npw):
    s, d = x.shape
    tm = 512

    def body(ma_ref, ms_ref, wo_ref, x_ref, t_ref, npw_ref, dmix_ref, dout_ref, dres_ref, acc_ref):
        @pl.when(pl.program_id(0) == 0)
        def _():
            acc_ref[...] = jnp.zeros_like(acc_ref)

        out = _nn(ma_ref[...], wo_ref[0:D_ATTN, :]) + _nn(ms_ref[...], wo_ref[D_ATTN:, :])
        r = lax.rsqrt(jnp.mean(out * out, axis=-1, keepdims=True) + EPS)
        on = out * r
        diff = x_ref[...] + on * npw_ref[...] - t_ref[...]
        dres = diff * (1.0 / d)
        dres_ref[...] = dres
        acc_ref[0:1, :] += jnp.sum(diff * diff, axis=0, keepdims=True)
        acc_ref[1:2, :] += jnp.sum(dres * on, axis=0, keepdims=True)
        dn = dres * npw_ref[...]
        dout = (r * (dn - on * jnp.mean(dn * on, axis=-1, keepdims=True))).astype(bf16)
        dout_ref[...] = dout
        dmix_ref[...] = _nt(dout, wo_ref[...])

    row = lambda w: pl.BlockSpec((tm, w), lambda i: (i, 0))
    return pl.pallas_call(
        body, name="outproj_loss", grid=(s // tm,),
        in_specs=[row(D_ATTN), row(D_SSM), _full((D_ATTN + D_SSM, d)), row(d), row(d), _full((1, d))],
        out_specs=[row(D_ATTN + D_SSM), row(d), row(d), _full((8, d))],
        out_shape=[SDS((s, D_ATTN + D_SSM), f32), SDS((s, d), bf16), SDS((s, d), f32), SDS((8, d), f32)],
        compiler_params=pltpu.CompilerParams(dimension_semantics=("arbitrary",)),
    )(mix_a, mix_s, wo, x, tgt, npw)


def _attn_bwd(proj, o, lb, dmix):
    s = proj.shape[0]
    n_it = s // BLK

    def body(q_ref, k_ref, v_ref, g_ref, o_ref, l_ref, dm_ref, dq_ref, dk_ref, dv_ref, dg_ref,
             dq_acc, dk_acc, dv_acc, do_scr, dl_scr):
        head0, tri2, _, _, bones = _attn_consts()

        def pro(i, carry):
            rows = pl.ds(pl.multiple_of(i * 256, 256), 256)
            g = g_ref[rows, :]
            sg = _sigmoid(g)
            dmx = dm_ref[rows, :]
            ov = o_ref[rows, :]
            dg_ref[rows, :] = (dmx * ov * (sg * (1.0 + g * (1.0 - sg)))).astype(bf16)
            do = dmx * (g * sg)
            do_scr[rows, :] = do
            dl_scr[rows, :] = _split_dot(do * ov, bones)
            z = jnp.zeros((256, LANES), f32)
            dq_acc[rows, :] = z
            dk_acc[rows, :] = z
            dv_acc[rows, :] = z
            return carry

        lax.fori_loop(0, s // 256, pro, 0)

        def per_head(t):
            return jnp.concatenate([t[:, :LANES], t[:, LANES:]], axis=0)

        def both_heads(t):
            tr = pltpu.roll(t, HEAD_DIM, 1)
            return jnp.concatenate([jnp.where(head0, t, tr), jnp.where(head0, tr, t)], axis=1)

        for d in DILATIONS:
            nb = s // (BLK * d)

            def it(i, carry, d=d, nb=nb):
                r, blk = i // nb, i % nb
                st = blk * (BLK * d) + r
                stp = jnp.maximum(blk - 1, 0) * (BLK * d) + r
                rows = pl.ds(st, BLK, stride=d)
                rows_p = pl.ds(stp, BLK, stride=d)
                has_prev = blk > 0
                q = q_ref[rows, :]
                kc, kp = k_ref[rows, :], k_ref[rows_p, :]
                vc, vp = v_ref[rows, :], v_ref[rows_p, :]
                do = do_scr[rows, :]
                lse = l_ref[rows, :]
                dl = dl_scr[rows, :]
                qs = q * 0.125
                qs16, q16, do16 = qs.astype(bf16), q.astype(bf16), do.astype(bf16)
                kst_c, kst_p = _stack_heads(kc.astype(bf16), head0), _stack_heads(kp.astype(bf16), head0)
                vst_c, vst_p = _stack_heads(vc.astype(bf16), head0), _stack_heads(vp.astype(bf16), head0)
                sc = jnp.where(tri2, _nt(qs16, kst_c), jnp.where(has_prev, _nt(qs16, kst_p), -jnp.inf))
                pt = jnp.exp(sc - both_heads(lse))
                dp = jnp.where(tri2, _nt(do16, vst_c), _nt(do16, vst_p))
                ds16 = (pt * (dp - both_heads(dl)) * 0.125).astype(bf16)
                pt16 = pt.astype(bf16)
                zero = jnp.zeros_like(pt16)
                dsc, dsp = jnp.where(tri2, ds16, zero), jnp.where(tri2, zero, ds16)
                pc, pp = jnp.where(tri2, pt16, zero), jnp.where(tri2, zero, pt16)
                pd = jnp.where(has_prev, jnp.exp(_split_dot(qs * kp, bones) - lse), 0.0)
                dsd = pd * (_split_dot(do * vp, bones) - dl) * 0.125
                qst, dost = _stack_heads(q16, head0), _stack_heads(do16, head0)
                dq_acc[rows, :] += _nn(dsc, kst_c) + _nn(dsp, kst_p) + dsd * kp
                dk_acc[rows, :] += _tn(per_head(dsc), qst)
                dv_acc[rows, :] += _tn(per_head(pc), dost)
                dk_acc[rows_p, :] += _tn(per_head(dsp), qst) + dsd * q
                dv_acc[rows_p, :] += _tn(per_head(pp), dost) + pd * do
                return carry

            lax.fori_loop(0, n_it, it, 0, unroll=2)

        def epi(i, carry):
            rows = pl.ds(pl.multiple_of(i * 256, 256), 256)
            dq_ref[rows, :] = dq_acc[rows, :].astype(bf16)
            dk_ref[rows, :] = dk_acc[rows, :].astype(bf16)
            dv_ref[rows, :] = dv_acc[rows, :].astype(bf16)
            return carry

        lax.fori_loop(0, s // 256, epi, 0)

    col = lambda base: pl.BlockSpec((s, LANES), lambda h: (0, base + h))
    outs = pl.pallas_call(
        body, name="attn_bwd", grid=(N_PAIRS,),
        in_specs=[col(0), col(8), col(16), col(24), col(0), col(0), col(0)],
        out_specs=[col(0)] * 4,
        out_shape=[SDS((s, D_ATTN), bf16)] * 4,
        scratch_shapes=[pltpu.VMEM((s, LANES), f32)] * 5,
        compiler_params=pltpu.CompilerParams(dimension_semantics=("parallel",)),
    )(proj, proj, proj, proj, o, lb, dmix)
    return outs


def _ssd_bwd(proj, y, states, dmix, conv_w, conv_b, dtb16, alog16, alog_f, d_f, nw):
    s = proj.shape[0]
    nc = s // CHUNK
    gw = D_SSM // N_GROUPS

    def body(xs_ref, bc_ref, xs_tail, bc_tail, dt_ref, z_ref, y_ref, st_ref, dm_ref, cw_ref, cb_ref, dtb_ref,
             alog16_ref, alogf_ref, df_ref, nw_ref, out_ref, gconv_ref, gvec_ref, gdt_ref,
             dh_scr, head_scr, xpad, dcpad, da_scr, dxdt_scr, dbc_scr):
        i = pl.program_id(0)
        c = nc - 1 - i

        @pl.when(i == 0)
        def _():
            dh_scr[...] = jnp.zeros_like(dh_scr)
            head_scr[...] = jnp.zeros_like(head_scr)
            gconv_ref[...] = jnp.zeros_like(gconv_ref)
            gvec_ref[...] = jnp.zeros_like(gvec_ref)
            gdt_ref[...] = jnp.zeros_like(gdt_ref)

        cv, sig, xbc, pre, dt_f, al_f, al_x, al_t = _ssd_common(
            xs_ref, bc_ref, xs_tail, bc_tail, dt_ref, cw_ref, cb_ref, dtb_ref, alog16_ref, xpad, c == 0)
        head0 = _iota((CHUNK, LANES), 1) < HEAD_DIM
        sub = _iota((CHUNK, LANES), 0)
        last_row = sub == CHUNK - 1

        yv, z, dmx = y_ref[...], z_ref[...], dm_ref[...]
        sz = _sigmoid(z)
        silu = z * sz
        yz = yv * silu
        dyz_parts = []
        for g in range(N_GROUPS):
            gs = slice(g * gw, (g + 1) * gw)
            part = yz[:, gs]
            r = lax.rsqrt(jnp.mean(part * part, axis=-1, keepdims=True) + EPS)
            nh = part * r
            gvec_ref[0:1, gs] += jnp.sum(dmx[:, gs] * nh, axis=0, keepdims=True)
            dn = dmx[:, gs] * nw_ref[:, gs]
            dyz_parts.append(r * (dn - nh * jnp.mean(dn * nh, axis=-1, keepdims=True)))
        dyz = jnp.concatenate(dyz_parts, axis=1)
        dy = dyz * silu
        out_ref[:, 0:D_SSM] = (dyz * yv * (sz * (1.0 + z * (1.0 - sz)))).astype(bf16)

        x_all = xbc[:, 0:D_SSM]
        gvec_ref[2:3, :] += jnp.sum(dy * x_all, axis=0, keepdims=True)

        for g in range(N_GROUPS):
            bm = xbc[:, D_SSM + g * D_STATE: D_SSM + (g + 1) * D_STATE].astype(bf16)
            cm = xbc[:, D_SSM + (N_GROUPS + g) * D_STATE: D_SSM + (N_GROUPS + g + 1) * D_STATE].astype(bf16)
            gmat = _nt(cm, bm)
            dgm = jnp.zeros((CHUNK, CHUNK), f32)
            db = jnp.zeros((CHUNK, D_STATE), f32)
            dc = jnp.zeros((CHUNK, D_STATE), f32)
            for pair in range(4 * g, 4 * g + 4):
                sl = slice(pair * LANES, (pair + 1) * LANES)
                xp, dtp, alp, dyp = x_all[:, sl], dt_f[:, sl], al_f[:, sl], dy[:, sl]
                xdt = xp * dtp
                xdt16 = xdt.astype(bf16)
                al_last = alp[CHUNK - 1:CHUNK, :]
                e_l = jnp.exp(alp)
                wf = jnp.exp(al_last - alp)
                e_last = jnp.exp(al_last)
                hp = st_ref[:, sl]
                hp16 = hp.astype(bf16)
                dhn = dh_scr[:, sl]
                dhn16 = dhn.astype(bf16)
                y_off = e_l * _nn(cm, hp16)
                dch16 = (dyp * e_l).astype(bf16)
                dc = dc + _nt(dch16, hp16)
                dh_out = _tn(cm, dch16)
                dal = dyp * y_off
                xw16 = (wf * xdt).astype(bf16)
                db = db + _nt(xw16, dhn16)
                dxw = _nn(bm, dhn16)
                dxdt = dxw * wf
                dwf = dxw * xdt * wf
                dal = dal - dwf
                dal_last = jnp.sum(dwf, axis=0, keepdims=True) + jnp.sum(dhn * hp, axis=0, keepdims=True) * e_last
                dh_scr[:, sl] = e_last * dhn + dh_out
                for h in range(2):
                    mh = head0 if h == 0 else jnp.logical_not(head0)
                    dyh16 = jnp.where(mh, dyp, 0.0).astype(bf16)
                    lmat = _decay_mat(al_x, al_t, pair, h)
                    mm = gmat * lmat
                    dmm = _nt(dyh16, xdt16)
                    dxdt = dxdt + _tn(mm.astype(bf16), dyh16)
                    n16 = (dmm * mm).astype(bf16)
                    jh = jnp.where(mh, 1.0 / HEAD_DIM, 0.0).astype(bf16)
                    dal = dal + _nn(n16, jh) - _tn(n16, jh)
                    dgm = dgm + dmm * lmat
                da_scr[:, sl] = dal + jnp.where(last_row, dal_last, 0.0)
                dxdt_scr[:, sl] = dxdt
            dgm16 = dgm.astype(bf16)
            dbc_scr[:, g * D_STATE:(g + 1) * D_STATE] = db + _tn(dgm16, cm)
            dbc_scr[:, (N_GROUPS + g) * D_STATE:(N_GROUPS + g + 1) * D_STATE] = dc + _nn(dgm16, bm)

        sub_c, lane_c = _iota((CHUNK, CHUNK), 0), _iota((CHUNK, CHUNK), 1)
        tri_t = (lane_c >= sub_c).astype(f32)
        dadt = _nn_hi(tri_t, da_scr[...])
        a_f = -jnp.exp(alogf_ref[...])
        dxdt_all = dxdt_scr[...]
        ddt_f = dxdt_all * x_all + a_f * dadt
        gvec_ref[1:2, :] += jnp.sum(dt_f * dadt, axis=0, keepdims=True) * a_f
        dx = df_ref[...] * dy + dxdt_all * dt_f
        row_h = _iota((D_SSM, LANES), 0) // HEAD_DIM
        fold = (row_h == _iota((D_SSM, LANES), 1)).astype(f32)
        ddt_raw = _nn_hi(ddt_f, fold) * _sigmoid(pre)
        gdt_ref[0:1, :] += jnp.sum(ddt_raw, axis=0, keepdims=True)
        out_ref[:, D_SSM + D_CONV:D_SSM + D_CONV + LANES] = ddt_raw.astype(bf16)
        out_ref[:, D_SSM + D_CONV + LANES:] = jnp.zeros((CHUNK, 3 * LANES), bf16)

        dsil = sig * (1.0 + cv * (1.0 - sig))
        dcv_x = dx * dsil[:, 0:D_SSM]
        dcv_bc = dbc_scr[...] * dsil[:, D_SSM:]
        dcpad[0:CHUNK, 0:D_SSM] = dcv_x
        dcpad[0:CHUNK, D_SSM:] = dcv_bc
        dcpad[CHUNK:, :] = head_scr[...]
        dcv = dcpad[0:CHUNK, :]
        gconv_ref[4:5, :] += jnp.sum(dcv, axis=0, keepdims=True)
        draw = jnp.zeros((CHUNK, D_CONV), f32)
        for j in range(4):
            gconv_ref[j:j + 1, :] += jnp.sum(dcv * xpad[pl.ds(5 + j, CHUNK), :], axis=0, keepdims=True)
            draw = draw + cw_ref[j:j + 1, :] * dcpad[pl.ds(3 - j, CHUNK), :]
        head_scr[...] = dcpad[0:8, :]
        out_ref[:, D_SSM:D_SSM + D_CONV] = draw.astype(bf16)

    order = lambda i: nc - 1 - i
    row = lambda w, cb=0: pl.BlockSpec((CHUNK, w), lambda i: (nc - 1 - i, cb))
    return pl.pallas_call(
        body, name="ssd_bwd", grid=(nc,),
        in_specs=_ssd_in_specs(order) + [row(D_SSM), pl.BlockSpec((None, D_STATE, D_SSM), lambda i: (nc - 1 - i, 0, 0)),
                                         row(D_SSM, 1), _full((4, D_CONV)), _full((1, D_CONV)), _full((1, LANES)),
                                         _full((1, LANES)), _full((1, D_SSM)), _full((1, D_SSM)), _full((1, D_SSM))],
        out_specs=[row(3072), _full((8, D_CONV)), _full((8, D_SSM)), _full((8, LANES))],
        out_shape=[SDS((s, 3072), bf16), SDS((8, D_CONV), f32), SDS((8, D_SSM), f32), SDS((8, LANES), f32)],
        scratch_shapes=[pltpu.VMEM((D_STATE, D_SSM), f32), pltpu.VMEM((8, D_CONV), f32),
                        pltpu.VMEM((8 + CHUNK, D_CONV), f32), pltpu.VMEM((8 + CHUNK, D_CONV), f32),
                        pltpu.VMEM((CHUNK, D_SSM), f32), pltpu.VMEM((CHUNK, D_SSM), f32),
                        pltpu.VMEM((CHUNK, 2 * N_GROUPS * D_STATE), f32)],
        compiler_params=pltpu.CompilerParams(dimension_semantics=("arbitrary",)),
    )(proj, proj, proj, proj, proj, proj, y, states, dmix, conv_w, conv_b, dtb16, alog16, alog_f, d_f, nw)


def _col_blocks(parts, tile):
    counts = [p.shape[1] // tile for p in parts]
    offs = [sum(counts[:t]) for t in range(len(parts))]
    return offs, counts, sum(counts)


def _inproj_bwd(dparts, wt, x, nw, dres):
    s, d = x.shape
    tm, tk = 1024, 1024
    offs, counts, nk = _col_blocks(dparts, tk)
    npart = len(dparts)

    def body(*refs):
        dp_refs = refs[:npart]
        w_ref, x_ref, nw_ref, dres_ref, gx_ref, gnw_ref, acc = refs[npart:]
        i, k = pl.program_id(0), pl.program_id(1)

        @pl.when(jnp.logical_and(i == 0, k == 0))
        def _():
            gnw_ref[...] = jnp.zeros_like(gnw_ref)

        @pl.when(k == 0)
        def _():
            acc[...] = jnp.zeros_like(acc)

        for t in range(npart):
            @pl.when(jnp.logical_and(k >= offs[t], k < offs[t] + counts[t]))
            def _(t=t):
                acc[...] += _nn(dp_refs[t][...], w_ref[...])

        @pl.when(k == nk - 1)
        def _():
            xv = x_ref[...]
            r = lax.rsqrt(jnp.mean(xv * xv, axis=-1, keepdims=True) + EPS)
            xn = xv * r
            du = acc[...]
            gnw_ref[0:1, :] += jnp.sum(du * xn, axis=0, keepdims=True)
            dn = du * nw_ref[...]
            gx_ref[...] = dres_ref[...] + r * (dn - xn * jnp.mean(dn * xn, axis=-1, keepdims=True))

    def piece(t):
        return pl.BlockSpec((tm, tk), lambda i, k: (i, jnp.clip(k - offs[t], 0, counts[t] - 1)))

    return pl.pallas_call(
        body, name="inproj_bwd", grid=(s // tm, nk),
        in_specs=[piece(t) for t in range(npart)] + [
            pl.BlockSpec((tk, d), lambda i, k: (k, 0)),
            pl.BlockSpec((tm, d), lambda i, k: (i, 0)), pl.BlockSpec((1, d), lambda i, k: (0, 0)),
            pl.BlockSpec((tm, d), lambda i, k: (i, 0))],
        out_specs=[pl.BlockSpec((tm, d), lambda i, k: (i, 0)), pl.BlockSpec((8, d), lambda i, k: (0, 0))],
        out_shape=[SDS((s, d), f32), SDS((8, d), f32)],
        scratch_shapes=[pltpu.VMEM((tm, d), f32)],
        compiler_params=pltpu.CompilerParams(dimension_semantics=("arbitrary", "arbitrary")),
    )(*dparts, wt, x, nw, dres)


def _matmul_tn(a_parts, b_parts, name):
    tile, tk = 1024, 512
    s = a_parts[0].shape[0]
    nk = s // tk
    na, nb = len(a_parts), len(b_parts)
    offs_a, counts_a, ni = _col_blocks(a_parts, tile)
    offs_b, counts_b, nj = _col_blocks(b_parts, tile)

    def body(*refs):
        a_refs, b_refs, o_ref = refs[:na], refs[na:na + nb], refs[na + nb]
        i, j = pl.program_id(0), pl.program_id(1)

        @pl.when(pl.program_id(2) == 0)
        def _():
            o_ref[...] = jnp.zeros_like(o_ref)

        for ta in range(na):
            for tb in range(nb):
                in_a = jnp.logical_and(i >= offs_a[ta], i < offs_a[ta] + counts_a[ta])
                in_b = jnp.logical_and(j >= offs_b[tb], j < offs_b[tb] + counts_b[tb])

                @pl.when(jnp.logical_and(in_a, in_b))
                def _(ta=ta, tb=tb):
                    o_ref[...] += _tn(a_refs[ta][...], b_refs[tb][...])

    def spec(offs, counts, t, axis):
        def index(i, j, k):
            pos = (i, j)[axis]
            mine = jnp.logical_and(pos >= offs[t], pos < offs[t] + counts[t])
            return jnp.where(mine, k, 0), jnp.clip(pos - offs[t], 0, counts[t] - 1)
        return pl.BlockSpec((tk, tile), index)

    return pl.pallas_call(
        body, name=name, grid=(ni, nj, nk),
        in_specs=[spec(offs_a, counts_a, t, 0) for t in range(na)] + [spec(offs_b, counts_b, t, 1) for t in range(nb)],
        out_specs=pl.BlockSpec((tile, tile), lambda i, j, k: (i, j)),
        out_shape=SDS((ni * tile, nj * tile), f32),
        compiler_params=pltpu.CompilerParams(dimension_semantics=("parallel", "parallel", "arbitrary")),
    )(*a_parts, *b_parts)


def _adamw(w, g, m, v):
    m = ADAM_B1 * m + (1.0 - ADAM_B1) * g
    v = ADAM_B2 * v + (1.0 - ADAM_B2) * (g * g)
    m_hat = m / (1.0 - ADAM_B1 ** ADAM_STEP)
    v_hat = v / (1.0 - ADAM_B2 ** ADAM_STEP)
    delta = -ADAM_LR * (m_hat / (jnp.sqrt(v_hat) + ADAM_EPS) + ADAM_WD * w)
    return delta, m, v


def _sum_adamw(own, parts, w, m, v, name):
    r, c = w.shape
    tc = 256

    def body(o_ref, p_ref, w_ref, m_ref, v_ref, g_ref, d_ref, nm_ref, nv_ref):
        my_q = 2 * lax.axis_index("x") + lax.axis_index("y")
        own_v = o_ref[...]
        g = jnp.where(my_q == 0, own_v, p_ref[0].astype(f32))
        for q in range(1, 4):
            g = g + jnp.where(my_q == q, own_v, p_ref[q].astype(f32))
        g_ref[...] = g
        d_ref[...], nm_ref[...], nv_ref[...] = _adamw(w_ref[...], g, m_ref[...], v_ref[...])

    blk = pl.BlockSpec((r, tc), lambda i: (0, i))
    return pl.pallas_call(
        body, name=name, grid=(c // tc,),
        in_specs=[blk, pl.BlockSpec((4, r, tc), lambda i: (0, 0, i)), blk, blk, blk],
        out_specs=[blk] * 4, out_shape=[SDS((r, c), f32)] * 4,
        compiler_params=pltpu.CompilerParams(dimension_semantics=("parallel",)),
    )(own, parts, w, m, v)


def _sum_small(parts):
    def body(p_ref, o_ref):
        t = p_ref[0]
        for j in range(1, N_DEV):
            t = t + p_ref[j]
        o_ref[...] = t
        row_h = _iota((D_SSM, LANES), 0) // HEAD_DIM
        fold = (row_h == _iota((D_SSM, LANES), 1)).astype(f32)
        lower = t[8:16, 0:LANES]
        folded = _nn_hi(t[8:16, 0:D_SSM], fold)
        loss = jnp.sum(t[11:12, 0:D_MODEL], axis=1, keepdims=True) * (0.5 / D_MODEL)
        row = _iota((8, LANES), 0)
        o_ref[8:16, 0:LANES] = jnp.where(row < 2, folded, jnp.where(row == 4, loss, lower))

    return pl.pallas_call(body, name="sum_small", out_shape=SDS((PACK_ROWS, PACK_W), f32),
                          in_specs=[pl.BlockSpec(memory_space=pltpu.VMEM)],
                          out_specs=pl.BlockSpec(memory_space=pltpu.VMEM))(parts)


def _adamw_small(w, g, m, v):
    def body(w_ref, g_ref, m_ref, v_ref, d_ref, nm_ref, nv_ref):
        d_ref[...], nm_ref[...], nv_ref[...] = _adamw(w_ref[...], g_ref[...], m_ref[...], v_ref[...])

    vm = pl.BlockSpec(memory_space=pltpu.VMEM)
    return pl.pallas_call(body, name="adamw_small", out_shape=[SDS(w.shape, f32)] * 3,
                          in_specs=[vm] * 4, out_specs=[vm] * 3)(w, g, m, v)


def _pad_lanes(v, width):
    return jnp.pad(v, ((0, 0), (0, width - v.shape[1])))


def _local_step(x, tgt, norm_pre_w, wt, conv_w, conv_b, dt_bias, a_log, d_skip, ssm_norm_w, wo, norm_post_w):
    dtb16 = _pad_lanes(dt_bias, LANES)
    alog16 = _pad_lanes(a_log, LANES)
    alog_f = jnp.repeat(a_log, HEAD_DIM, axis=1)
    d_f = jnp.repeat(d_skip, HEAD_DIM, axis=1)

    proj, u = _prenorm_inproj(x, norm_pre_w, wt)
    o, lb, mix_a = _attn_fwd(proj)
    mix_s, y, states = _ssd_fwd(proj, conv_w, conv_b, dtb16, alog16, alog_f, d_f, ssm_norm_w)
    dmix, dout, dres, acc_post = _outproj_loss(mix_a, mix_s, wo, x, tgt, norm_post_w)
    dq, dk, dv, dg = _attn_bwd(proj, o, lb, dmix)
    dzxd, g_conv, g_vec, g_dt = _ssd_bwd(proj, y, states, dmix, conv_w, conv_b, dtb16, alog16, alog_f, d_f, ssm_norm_w)
    dparts = [dq, dk, dv, dg, dzxd]
    dw_out = _matmul_tn([mix_a, mix_s], [dout], "dw_out")
    dw_in = _matmul_tn(dparts, [u], "dw_in")
    grad_x, g_pre = _inproj_bwd(dparts, wt, x, norm_pre_w, dres)

    rows = [g_conv[0:5], _pad_lanes(g_pre[0:1], PACK_W), _pad_lanes(g_vec[0:1], PACK_W),
            _pad_lanes(acc_post[1:2], PACK_W), _pad_lanes(g_vec[1:3], PACK_W), _pad_lanes(g_dt[0:1], PACK_W),
            _pad_lanes(acc_post[0:1], PACK_W), jnp.zeros((4, PACK_W), f32)]
    return grad_x, dw_in, dw_out, jnp.concatenate(rows, axis=0)


def kernel(x, norm_pre_w, w_in, conv_w, conv_b, dt_bias, a_log, d_skip, ssm_norm_w, w_out, norm_post_w, loss_target, m_norm_pre_w, m_w_in, m_conv_w, m_conv_b, m_dt_bias, m_a_log, m_d_skip, m_ssm_norm_w, m_w_out, m_norm_post_w, v_norm_pre_w, v_w_in, v_conv_w, v_conv_b, v_dt_bias, v_a_log, v_d_skip, v_ssm_norm_w, v_w_out, v_norm_post_w):
    shard_in = w_in.shape[2]
    shard_cv = conv_w.shape[2]
    me = 4 * lax.axis_index("x") + 2 * lax.axis_index("y") + lax.axis_index("c")

    g_in, g_out, g_cw = _all_gather([w_in[0].T.astype(bf16), w_out[0].astype(bf16), conv_w[0]])
    wt = jnp.pad(g_in.reshape(N_DEV * shard_in, D_MODEL), ((0, NP - N_DEV * shard_in), (0, 0)))
    wo = g_out.reshape(N_DEV * w_out.shape[1], D_MODEL)
    cw = g_cw.transpose(1, 0, 2).reshape(4, D_CONV)

    grad_x, dw_in, dw_out, pack = _local_step(
        x[0], loss_target[0], norm_pre_w, wt, cw, conv_b, dt_bias, a_log, d_skip, ssm_norm_w, wo, norm_post_w)

    send_in = dw_in[:N_DEV * shard_in].reshape(4, 2, shard_in, D_MODEL).transpose(1, 0, 2, 3)
    send_out = dw_out.reshape(4, 2, w_out.shape[1], D_MODEL).transpose(1, 0, 2, 3)
    got_in, got_out, parts_small = _sibling_swap([send_in, send_out], pack)
    chip_in, own_in = _chip_sum(send_in, got_in, "chip_sum_w_in")
    chip_out, own_out = _chip_sum(send_out, got_out, "chip_sum_w_out")
    parts_in, parts_out = _chip_exchange([chip_in, chip_out])

    g_w_in, d_w_in, nm_w_in, nv_w_in = (a.T for a in _sum_adamw(
        own_in, parts_in, w_in[0].T, m_w_in[0].T, v_w_in[0].T, "sum_adamw_w_in"))
    g_w_out, d_w_out, nm_w_out, nv_w_out = _sum_adamw(own_out, parts_out, w_out[0], m_w_out[0], v_w_out[0], "sum_adamw_w_out")
    tot = _sum_small(parts_small)

    g_cw_all = tot[0:4]
    small_g = {
        "conv_w": lax.dynamic_slice(g_cw_all, (0, me * shard_cv), (4, shard_cv)),
        "conv_b": tot[4:5], "norm_pre_w": tot[5:6, :D_MODEL], "ssm_norm_w": tot[6:7, :D_SSM],
        "norm_post_w": tot[7:8, :D_MODEL], "a_log": tot[8:9, :16], "d_skip": tot[9:10, :16], "dt_bias": tot[10:11, :16],
    }
    loss = tot[12, 0]
    small_w = {"conv_w": (conv_w[0], m_conv_w[0], v_conv_w[0]), "conv_b": (conv_b, m_conv_b, v_conv_b),
               "norm_pre_w": (norm_pre_w, m_norm_pre_w, v_norm_pre_w), "ssm_norm_w": (ssm_norm_w, m_ssm_norm_w, v_ssm_norm_w),
               "norm_post_w": (norm_post_w, m_norm_post_w, v_norm_post_w), "a_log": (a_log, m_a_log, v_a_log),
               "d_skip": (d_skip, m_d_skip, v_d_skip), "dt_bias": (dt_bias, m_dt_bias, v_dt_bias)}
    names = list(small_w)
    sizes = [small_g[k].size for k in names]
    tot_size = sum(sizes)
    pad_to = -(-tot_size // 1024) * 1024

    def flat(arrs):
        v = jnp.concatenate([a.reshape(-1) for a in arrs])
        return jnp.pad(v, (0, pad_to - tot_size)).reshape(pad_to // LANES, LANES)

    fw = flat([small_w[k][0] for k in names])
    fg = flat([small_g[k] for k in names])
    fm = flat([small_w[k][1] for k in names])
    fv = jnp.pad(jnp.concatenate([small_w[k][2].reshape(-1) for k in names]), (0, pad_to - tot_size),
                 constant_values=1.0).reshape(pad_to // LANES, LANES)
    fd, fnm, fnv = _adamw_small(fw, fg, fm, fv)

    def unflat(f):
        out, off = {}, 0
        v = f.reshape(-1)
        for k, n in zip(names, sizes):
            out[k] = v[off:off + n].reshape(small_g[k].shape)
            off += n
        return out

    sd, snm, snv = unflat(fd), unflat(fnm), unflat(fnv)
    lead = lambda a: a[None]
    order = ["norm_pre_w", "w_in", "conv_w", "conv_b", "dt_bias", "a_log", "d_skip", "ssm_norm_w", "w_out", "norm_post_w"]
    grads = dict(small_g, w_in=g_w_in, w_out=g_w_out)
    deltas = dict(sd, w_in=d_w_in, w_out=d_w_out)
    new_m = dict(snm, w_in=nm_w_in, w_out=nm_w_out)
    new_v = dict(snv, w_in=nv_w_in, w_out=nv_w_out)

    def shaped(dct, k):
        a = dct[k]
        return lead(a) if k in ("w_in", "w_out", "conv_w") else a

    return (loss, grad_x[None], *[shaped(grads, k) for k in order], *[shaped(deltas, k) for k in order],
            *[shaped(new_m, k) for k in order], *[shaped(new_v, k) for k in order])
```

```python
import functools
import math

import jax
import jax.numpy as jnp
import numpy as np
from jax import lax
from jax.experimental import pallas as pl
from jax.experimental.pallas import tpu as pltpu

f32, bf16 = jnp.float32, jnp.bfloat16
SDS = jax.ShapeDtypeStruct
HIGHEST = lax.Precision.HIGHEST
MESH = pl.DeviceIdType.MESH

N_DEV = 8
D_MODEL = 1024
D_ATTN = 1024
D_SSM = 1024
HEAD_DIM = 64
N_PAIRS = 8
D_STATE = 128
N_GROUPS = 2
D_CONV = D_SSM + 2 * N_GROUPS * D_STATE
D_IN_PROJ = 4 * D_ATTN + D_SSM + D_CONV + 16
NP = 7168
CHUNK = 128
BLK = 128
DILATIONS = (1, 4, 16)
EPS = 1e-6
LANES = 128
COL_Z, COL_XS, COL_BC, COL_DT = 4096, 5120, 6144, 6656

ADAM_LR, ADAM_B1, ADAM_B2, ADAM_EPS, ADAM_WD, ADAM_STEP = 0.001, 0.9, 0.999, 1e-08, 0.01, 10

PACK_ROWS, PACK_W = 16, 1536


def _nt(a, b):
    return lax.dot_general(a, b, (((1,), (1,)), ((), ())), preferred_element_type=f32)


def _tn(a, b):
    return lax.dot_general(a, b, (((0,), (0,)), ((), ())), preferred_element_type=f32)


def _nn(a, b):
    return jnp.dot(a, b, preferred_element_type=f32)


def _nn_hi(a, b):
    return jnp.dot(a, b, precision=HIGHEST, preferred_element_type=f32)


def _sigmoid(x):
    return 1.0 / (1.0 + jnp.exp(-x))


def _softplus(x):
    return jnp.maximum(x, 0.0) + jnp.log1p(jnp.exp(-jnp.abs(x)))


def _iota(shape, dim):
    return lax.broadcasted_iota(jnp.int32, shape, dim)


def _my_pos():
    return lax.axis_index("x"), lax.axis_index("y"), lax.axis_index("c")


def _all_gather(arrs):
    n = len(arrs)

    def body(*refs):
        ins, outs = refs[:n], refs[n:2 * n]
        send_sems, recv_sems, local_sems = refs[2 * n:]
        x, y, c = _my_pos()
        me, sibling = (x, y, c), (x, y, 1 - c)
        chips = [(1 - x, y), (x, 1 - y), (1 - x, 1 - y)]

        def slot(a, px, py, pc):
            return outs[a].at[4 * px + 2 * py + pc]

        def copy(a, k, block, to, src=None):
            return pltpu.make_async_remote_copy(
                src_ref=slot(a, *block) if src is None else src, dst_ref=slot(a, *block),
                send_sem=send_sems.at[7 * a + k], recv_sem=recv_sems.at[7 * a + k],
                device_id=to, device_id_type=MESH)

        mine = [pltpu.make_async_copy(ins[a], slot(a, *me), local_sems.at[a]) for a in range(n)]
        for cp in mine:
            cp.start()
        first = []
        for a in range(n):
            first.append(copy(a, 0, me, sibling, src=ins[a]))
            first += [copy(a, 1 + j, me, (*chip, c), src=ins[a]) for j, chip in enumerate(chips)]
        for cp in first:
            cp.start()
        passed = []
        for j, chip in enumerate(chips):
            for a in range(n):
                copy(a, 1 + j, (*chip, c), me).wait_recv()
                cp = copy(a, 4 + j, (*chip, c), sibling)
                cp.start()
                passed.append(cp)
        for a in range(n):
            copy(a, 0, sibling, me).wait_recv()
            for j, chip in enumerate(chips):
                copy(a, 4 + j, (*chip, 1 - c), me).wait_recv()
        for cp in first + passed:
            cp.wait_send()
        for cp in mine:
            cp.wait()

    anyspec = pl.BlockSpec(memory_space=pl.ANY)
    return pl.pallas_call(
        body, name="weights_all_gather",
        out_shape=[SDS((N_DEV,) + a.shape, a.dtype) for a in arrs],
        in_specs=[anyspec] * n, out_specs=[anyspec] * n,
        scratch_shapes=[pltpu.SemaphoreType.DMA((7 * n,)), pltpu.SemaphoreType.DMA((7 * n,)),
                        pltpu.SemaphoreType.DMA((n,))],
    )(*arrs)


def _sibling_swap(bigs):
    nb = len(bigs)

    def body(*refs):
        ins, outs = refs[:nb], refs[nb:2 * nb]
        send_sems, recv_sems = refs[2 * nb:]
        x, y, c = _my_pos()
        sends = []
        for a in range(nb):
            cp = pltpu.make_async_remote_copy(
                src_ref=ins[a].at[1 - c], dst_ref=outs[a], send_sem=send_sems.at[a], recv_sem=recv_sems.at[a],
                device_id=(x, y, 1 - c), device_id_type=MESH)
            cp.start()
            sends.append(cp)
        for a in range(nb):
            pltpu.make_async_remote_copy(
                src_ref=ins[a].at[c], dst_ref=outs[a], send_sem=send_sems.at[a], recv_sem=recv_sems.at[a],
                device_id=(x, y, c), device_id_type=MESH).wait_recv()
        for cp in sends:
            cp.wait_send()

    anyspec = pl.BlockSpec(memory_space=pl.ANY)
    return pl.pallas_call(
        body, name="grad_sibling_swap", out_shape=[SDS(a.shape[1:], a.dtype) for a in bigs],
        in_specs=[anyspec] * nb, out_specs=[anyspec] * nb,
        scratch_shapes=[pltpu.SemaphoreType.DMA((nb,)), pltpu.SemaphoreType.DMA((nb,))],
    )(*bigs)


def _gather_small(small):
    def body(small_in, small_out, send_sems, recv_sems, local_sem):
        x, y, c = _my_pos()
        me = 4 * x + 2 * y + c
        mine = pltpu.make_async_copy(small_in, small_out.at[me], local_sem)
        mine.start()
        sends = []
        for k in range(1, N_DEV):
            to = (me + k) % N_DEV
            cp = pltpu.make_async_remote_copy(
                src_ref=small_in, dst_ref=small_out.at[me], send_sem=send_sems.at[k - 1], recv_sem=recv_sems.at[k - 1],
                device_id=(to // 4, (to // 2) % 2, to % 2), device_id_type=MESH)
            cp.start()
            sends.append(cp)
        for k in range(1, N_DEV):
            frm = (me + N_DEV - k) % N_DEV
            pltpu.make_async_remote_copy(
                src_ref=small_in, dst_ref=small_out.at[frm], send_sem=send_sems.at[k - 1], recv_sem=recv_sems.at[k - 1],
                device_id=(x, y, c), device_id_type=MESH).wait_recv()
        for cp in sends:
            cp.wait_send()
        mine.wait()

    anyspec = pl.BlockSpec(memory_space=pl.ANY)
    return pl.pallas_call(
        body, name="small_grads_gather", out_shape=SDS((N_DEV,) + small.shape, small.dtype),
        in_specs=[anyspec], out_specs=anyspec,
        scratch_shapes=[pltpu.SemaphoreType.DMA((7,)), pltpu.SemaphoreType.DMA((7,)), pltpu.SemaphoreType.DMA(())],
    )(small)


def _chip_sum(mine, got, name):
    _, nq, r, cdim = mine.shape
    tc = 256

    def body(m_ref, g_ref, s16_ref, own_ref):
        q = pl.program_id(1)
        c = lax.axis_index("c")
        my_q = 2 * lax.axis_index("x") + lax.axis_index("y")
        tot = m_ref[c] + g_ref[...]
        s16_ref[...] = tot.astype(bf16)

        @pl.when(q == my_q)
        def _():
            own_ref[...] = tot

    return pl.pallas_call(
        body, name=name, grid=(cdim // tc, nq),
        in_specs=[pl.BlockSpec((2, None, r, tc), lambda i, q: (0, q, 0, i)),
                  pl.BlockSpec((None, r, tc), lambda i, q: (q, 0, i))],
        out_specs=[pl.BlockSpec((None, r, tc), lambda i, q: (q, 0, i)), pl.BlockSpec((r, tc), lambda i, q: (0, i))],
        out_shape=[SDS((nq, r, cdim), bf16), SDS((r, cdim), f32)],
        compiler_params=pltpu.CompilerParams(dimension_semantics=("parallel", "arbitrary")),
    )(mine, got)


def _chip_exchange_copies(ins, outs, send_sems, recv_sems, local_sems):
    nb = len(ins)
    x, y, c = _my_pos()
    my_q = 2 * x + y
    mine = [pltpu.make_async_copy(ins[a].at[my_q], outs[a].at[my_q], local_sems.at[a]) for a in range(nb)]
    sends, recvs = [], []
    for k in range(1, 4):
        to, frm = (my_q + k) % 4, (my_q + 4 - k) % 4
        for a in range(nb):
            sems = dict(send_sem=send_sems.at[3 * a + k - 1], recv_sem=recv_sems.at[3 * a + k - 1], device_id_type=MESH)
            sends.append(pltpu.make_async_remote_copy(
                src_ref=ins[a].at[to], dst_ref=outs[a].at[my_q], device_id=(to // 2, to % 2, c), **sems))
            recvs.append(pltpu.make_async_remote_copy(
                src_ref=ins[a].at[frm], dst_ref=outs[a].at[frm], device_id=(x, y, c), **sems))
    return mine, sends, recvs


def _chip_exchange_scratch(nb):
    return [pltpu.SemaphoreType.DMA((3 * nb,)), pltpu.SemaphoreType.DMA((3 * nb,)), pltpu.SemaphoreType.DMA((nb,))]


def _prenorm_inproj(x, nw, wt):
    s, d = x.shape
    npad = wt.shape[0]
    tm, tn = 1024, 512

    def body(x_ref, nw_ref, w_ref, proj_ref, u_ref):
        @pl.when(pl.program_id(1) == 0)
        def _():
            xv = x_ref[...]
            r = lax.rsqrt(jnp.mean(xv * xv, axis=-1, keepdims=True) + EPS)
            u_ref[...] = (xv * r * nw_ref[...]).astype(bf16)
        proj_ref[...] = _nt(u_ref[...], w_ref[...])

    return pl.pallas_call(
        body, name="prenorm_inproj", grid=(s // tm, npad // tn),
        in_specs=[pl.BlockSpec((tm, d), lambda i, j: (i, 0)), pl.BlockSpec((1, d), lambda i, j: (0, 0)),
                  pl.BlockSpec((tn, d), lambda i, j: (j, 0))],
        out_specs=[pl.BlockSpec((tm, tn), lambda i, j: (i, j)), pl.BlockSpec((tm, d), lambda i, j: (i, 0))],
        out_shape=[SDS((s, npad), f32), SDS((s, d), bf16)],
        compiler_params=pltpu.CompilerParams(dimension_semantics=("parallel", "arbitrary")),
    )(x, nw, wt)


def _attn_consts():
    head0 = _iota((BLK, LANES), 1) < HEAD_DIM
    tri2 = (_iota((BLK, 2 * LANES), 1) % LANES) <= _iota((BLK, 2 * LANES), 0)
    ones2 = ((_iota((LANES, 2 * LANES), 0) < HEAD_DIM) == (_iota((LANES, 2 * LANES), 1) < LANES)).astype(bf16)
    rmat = ((_iota((2 * LANES, LANES), 0) < LANES) == (_iota((2 * LANES, LANES), 1) < HEAD_DIM)).astype(bf16)
    bones = ((_iota((LANES, LANES), 0) < HEAD_DIM) == (_iota((LANES, LANES), 1) < HEAD_DIM)).astype(bf16)
    return head0, tri2, ones2, rmat, bones


def _stack_heads(x16, head0):
    zero = jnp.zeros_like(x16)
    return jnp.concatenate([jnp.where(head0, x16, zero), jnp.where(head0, zero, x16)], axis=0)


def _split_dot(x, w16):
    hi = x.astype(bf16)
    lo = (x - hi.astype(f32)).astype(bf16)
    return _nn(hi, w16) + _nn(lo, w16)


def _bf16_terms(x, terms):
    out = []
    for _ in range(terms):
        t = x.astype(bf16)
        out.append(t)
        x = x - t.astype(f32)
    return out


def _dot_01(x, w16, terms):
    return sum(_nn(t, w16) for t in _bf16_terms(x, terms))


def _dot_01_left(w16, x, terms):
    return sum(_nn(w16, t) for t in _bf16_terms(x, terms))


def _attn_fwd(proj):
    s = proj.shape[0]
    n_it = s // BLK

    def body(q_ref, k_ref, v_ref, g_ref, o_ref, l_ref, mix_ref, op0, op1, op2, lp0, lp1, lp2):
        op_refs, lp_refs = (op0, op1, op2), (lp0, lp1, lp2)
        head0, tri2, ones2, rmat, _ = _attn_consts()
        for p, d in enumerate(DILATIONS):
            nb = s // (BLK * d)

            def it(i, carry, d=d, nb=nb, p=p):
                r, blk = i // nb, i % nb
                st = blk * (BLK * d) + r
                stp = jnp.maximum(blk - 1, 0) * (BLK * d) + r
                rows = pl.ds(st, BLK, stride=d)
                rows_p = pl.ds(stp, BLK, stride=d)
                has_prev = blk > 0
                qs = q_ref[rows, :] * 0.125
                kc, kp = k_ref[rows, :], k_ref[rows_p, :]
                vc, vp = v_ref[rows, :], v_ref[rows_p, :]
                qs16 = qs.astype(bf16)
                sc = _nt(qs16, _stack_heads(kc.astype(bf16), head0))
                sp = _nt(qs16, _stack_heads(kp.astype(bf16), head0))
                sc = jnp.where(tri2, sc, jnp.where(has_prev, sp, -jnp.inf))
                sd2 = jnp.where(has_prev, _split_dot(qs * kp, ones2), -jnp.inf)
                m0 = jnp.max(sc[:, :LANES], axis=1, keepdims=True)
                m1 = jnp.max(sc[:, LANES:], axis=1, keepdims=True)
                m2 = jnp.concatenate([jnp.broadcast_to(m0, (BLK, LANES)), jnp.broadcast_to(m1, (BLK, LANES))], axis=1)
                m2 = jnp.maximum(m2, sd2)
                pt16 = jnp.exp(sc - m2).astype(bf16)
                m_pair = jnp.where(head0, m2[:, :LANES], m2[:, LANES:])
                pd = jnp.exp(jnp.where(head0, sd2[:, :LANES], sd2[:, LANES:]) - m_pair)
                zero = jnp.zeros_like(pt16)
                o = (_nn(jnp.where(tri2, pt16, zero), _stack_heads(vc.astype(bf16), head0))
                     + _nn(jnp.where(tri2, zero, pt16), _stack_heads(vp.astype(bf16), head0)) + pd * vp)
                l = _nn(pt16, rmat) + pd
                op_refs[p][rows, :] = o / l
                lp_refs[p][rows, :] = m_pair + jnp.log(l)
                return carry

            lax.fori_loop(0, n_it, it, 0, unroll=2)

        def merge(i, carry):
            rows = pl.ds(pl.multiple_of(i * 256, 256), 256)
            l0, l1, l2 = lp0[rows, :], lp1[rows, :], lp2[rows, :]
            m = jnp.maximum(jnp.maximum(l0, l1), l2)
            e0, e1, e2 = jnp.exp(l0 - m), jnp.exp(l1 - m), jnp.exp(l2 - m)
            z = e0 + e1 + e2
            o = (e0 * op0[rows, :] + e1 * op1[rows, :] + e2 * op2[rows, :]) / z
            o_ref[rows, :] = o
            l_ref[rows, :] = m + jnp.log(z)
            g = g_ref[rows, :]
            mix_ref[rows, :] = (o * (g * _sigmoid(g))).astype(bf16)
            return carry

        lax.fori_loop(0, s // 256, merge, 0)

    col = lambda base: pl.BlockSpec((s, LANES), lambda h: (0, base + h))
    return pl.pallas_call(
        body, name="attn_fwd", grid=(N_PAIRS,),
        in_specs=[col(0), col(8), col(16), col(24)],
        out_specs=[col(0), col(0), col(0)],
        out_shape=[SDS((s, D_ATTN), f32), SDS((s, D_ATTN), f32), SDS((s, D_ATTN), bf16)],
        scratch_shapes=[pltpu.VMEM((s, LANES), f32)] * 6,
        compiler_params=pltpu.CompilerParams(dimension_semantics=("parallel",)),
    )(proj, proj, proj, proj)


def _expand_mat():
    colv = np.arange(2 * D_SSM)
    head = 2 * ((colv % D_SSM) // LANES) + colv // D_SSM
    return jnp.asarray(np.arange(LANES)[:, None] == head[None, :], dtype=bf16)


def _fold_mat():
    return jnp.asarray((np.arange(D_SSM) // HEAD_DIM)[:, None] == np.arange(LANES)[None, :], dtype=bf16)


def _ssd_common(xs_ref, bc_ref, xs_tail, bc_tail, dt_ref, cw_ref, cb_ref, dtb_ref, alog16_ref, emat_ref, xpad, first):
    keep = jnp.where(first, 0.0, 1.0)
    xpad[0:8, 0:D_SSM] = xs_tail[...] * keep
    xpad[0:8, D_SSM:D_CONV] = bc_tail[...] * keep
    xpad[8:8 + CHUNK, 0:D_SSM] = xs_ref[...]
    xpad[8:8 + CHUNK, D_SSM:D_CONV] = bc_ref[...]
    cv = cb_ref[...] + cw_ref[0:1, :] * xpad[pl.ds(5, CHUNK), :]
    for j in range(1, 4):
        cv = cv + cw_ref[j:j + 1, :] * xpad[pl.ds(5 + j, CHUNK), :]
    sig = _sigmoid(cv)
    xbc = cv * sig

    pre = dt_ref[...] + dtb_ref[...]
    dt16 = _softplus(pre)
    a16 = -jnp.exp(alog16_ref[...])
    sub, lane = _iota((CHUNK, CHUNK), 0), _iota((CHUNK, CHUNK), 1)
    tri = (sub >= lane).astype(f32)
    al16 = _nn_hi(tri, dt16 * a16)
    al_t = al16.T
    emat = emat_ref[...]
    dt_x = _dot_01(dt16, emat, 3)
    al_x = _dot_01(al16, emat, 3)
    lane_w = _iota((CHUNK, D_SSM), 1)
    even = (lane_w % LANES) < HEAD_DIM
    dt_f = jnp.where(even, dt_x[:, :D_SSM], dt_x[:, D_SSM:])
    al_f = jnp.where(even, al_x[:, :D_SSM], al_x[:, D_SSM:])
    return cv, sig, xbc, pre, dt_f, al_f, al_x, al_t


def _decay_mat(al_x, al_t, pair, h):
    sub, lane = _iota((CHUNK, CHUNK), 0), _iota((CHUNK, CHUNK), 1)
    col = al_x[:, h * D_SSM + pair * LANES: h * D_SSM + (pair + 1) * LANES]
    row = al_t[2 * pair + h: 2 * pair + h + 1, :]
    return jnp.exp(jnp.where(sub >= lane, col - row, -jnp.inf))


def _ssd_in_specs(order):
    blk = lambda w, cb: pl.BlockSpec((CHUNK, w), lambda i: (order(i), cb))
    tail = lambda w, cb: pl.BlockSpec((8, w), lambda i: (jnp.maximum(16 * order(i) - 1, 0), cb))
    return [blk(D_SSM, COL_XS // D_SSM), blk(512, COL_BC // 512), tail(D_SSM, COL_XS // D_SSM),
            tail(512, COL_BC // 512), blk(LANES, COL_DT // LANES), blk(D_SSM, COL_Z // D_SSM)]


def _full(shape):
    return pl.BlockSpec(shape, lambda i: (0,) * len(shape))


def _ssd_fwd(proj, conv_w, conv_b, dtb16, alog16, alog_f, d_f, nw):
    s = proj.shape[0]
    nc = s // CHUNK

    def body(xs_ref, bc_ref, xs_tail, bc_tail, dt_ref, z_ref, cw_ref, cb_ref, dtb_ref, alog16_ref, alogf_ref,
             df_ref, nw_ref, emat_ref, mix_ref, y_ref, st_ref, h_scr, xpad, y_scr):
        c = pl.program_id(0)

        @pl.when(c == 0)
        def _():
            h_scr[...] = jnp.zeros_like(h_scr)

        _, _, xbc, _, dt_f, al_f, al_x, al_t = _ssd_common(
            xs_ref, bc_ref, xs_tail, bc_tail, dt_ref, cw_ref, cb_ref, dtb_ref, alog16_ref, emat_ref, xpad, c == 0)
        head0 = _iota((CHUNK, LANES), 1) < HEAD_DIM
        st_ref[...] = h_scr[...]
        for g in range(N_GROUPS):
            bm = xbc[:, D_SSM + g * D_STATE: D_SSM + (g + 1) * D_STATE].astype(bf16)
            cm = xbc[:, D_SSM + (N_GROUPS + g) * D_STATE: D_SSM + (N_GROUPS + g + 1) * D_STATE].astype(bf16)
            gmat = _nt(cm, bm)
            for pair in range(4 * g, 4 * g + 4):
                sl = slice(pair * LANES, (pair + 1) * LANES)
                xp, dtp, alp = xbc[:, sl], dt_f[:, sl], al_f[:, sl]
                xdt = xp * dtp
                xdt16 = xdt.astype(bf16)
                al_last = alp[CHUNK - 1:CHUNK, :]
                hp = h_scr[:, sl]
                y_off = jnp.exp(alp) * _nn(cm, hp.astype(bf16))
                yd = [_nn((gmat * _decay_mat(al_x, al_t, pair, h)).astype(bf16), xdt16) for h in range(2)]
                y_scr[:, sl] = jnp.where(head0, yd[0], yd[1]) + y_off + df_ref[:, sl] * xp
                st = _tn(bm, (jnp.exp(al_last - alp) * xdt).astype(bf16))
                h_scr[:, sl] = jnp.exp(al_last) * hp + st
        y = y_scr[...]
        y_ref[...] = y
        z = z_ref[...]
        yz = y * (z * _sigmoid(z))
        gw = D_SSM // N_GROUPS
        for g in range(N_GROUPS):
            part = yz[:, g * gw:(g + 1) * gw]
            r = lax.rsqrt(jnp.mean(part * part, axis=-1, keepdims=True) + EPS)
            mix_ref[:, g * gw:(g + 1) * gw] = (part * r * nw_ref[:, g * gw:(g + 1) * gw]).astype(bf16)

    order = lambda i: i
    row = lambda w: pl.BlockSpec((CHUNK, w), lambda i: (i, 0))
    return pl.pallas_call(
        body, name="ssd_fwd", grid=(nc,),
        in_specs=_ssd_in_specs(order) + [_full((4, D_CONV)), _full((1, D_CONV)), _full((1, LANES)), _full((1, LANES)),
                                         _full((1, D_SSM)), _full((1, D_SSM)), _full((1, D_SSM)),
                                         _full((LANES, 2 * D_SSM))],
        out_specs=[row(D_SSM), row(D_SSM), pl.BlockSpec((None, D_STATE, D_SSM), lambda i: (i, 0, 0))],
        out_shape=[SDS((s, D_SSM), bf16), SDS((s, D_SSM), f32), SDS((nc, D_STATE, D_SSM), f32)],
        scratch_shapes=[pltpu.VMEM((D_STATE, D_SSM), f32), pltpu.VMEM((8 + CHUNK, D_CONV), f32),
                        pltpu.VMEM((CHUNK, D_SSM), f32)],
        compiler_params=pltpu.CompilerParams(dimension_semantics=("arbitrary",)),
    )(proj, proj, proj, proj, proj, proj, conv_w, conv_b, dtb16, alog16, alog_f, d_f, nw, _expand_mat())


def _outproj_loss(mix_a, mix_s, wo, x, tgt, npw):
    s, d = x.shape
    tm = 512

    def body(ma_ref, ms_ref, wo_ref, x_ref, t_ref, npw_ref, dmix_ref, dout_ref, dres_ref, acc_ref):
        @pl.when(pl.program_id(0) == 0)
        def _():
            acc_ref[...] = jnp.zeros_like(acc_ref)

        out = _nn(ma_ref[...], wo_ref[0:D_ATTN, :]) + _nn(ms_ref[...], wo_ref[D_ATTN:, :])
        r = lax.rsqrt(jnp.mean(out * out, axis=-1, keepdims=True) + EPS)
        on = out * r
        diff = x_ref[...] + on * npw_ref[...] - t_ref[...]
        dres = diff * (1.0 / d)
        dres_ref[...] = dres
        acc_ref[0:1, :] += jnp.sum(diff * diff, axis=0, keepdims=True)
        acc_ref[1:2, :] += jnp.sum(dres * on, axis=0, keepdims=True)
        dn = dres * npw_ref[...]
        dout = (r * (dn - on * jnp.mean(dn * on, axis=-1, keepdims=True))).astype(bf16)
        dout_ref[...] = dout
        dmix_ref[...] = _nt(dout, wo_ref[...])

    row = lambda w: pl.BlockSpec((tm, w), lambda i: (i, 0))
    return pl.pallas_call(
        body, name="outproj_loss", grid=(s // tm,),
        in_specs=[row(D_ATTN), row(D_SSM), _full((D_ATTN + D_SSM, d)), row(d), row(d), _full((1, d))],
        out_specs=[row(D_ATTN + D_SSM), row(d), row(d), _full((8, d))],
        out_shape=[SDS((s, D_ATTN + D_SSM), f32), SDS((s, d), bf16), SDS((s, d), f32), SDS((8, d), f32)],
        compiler_params=pltpu.CompilerParams(dimension_semantics=("arbitrary",)),
    )(mix_a, mix_s, wo, x, tgt, npw)


def _attn_bwd(proj, o, lb, dmix):
    s = proj.shape[0]
    n_it = s // BLK

    def body(q_ref, k_ref, v_ref, g_ref, o_ref, l_ref, dm_ref, dq_ref, dk_ref, dv_ref, dg_ref,
             dq_acc, dk_acc, dv_acc, do_scr, dl_scr):
        head0, tri2, _, _, bones = _attn_consts()

        def pro(i, carry):
            rows = pl.ds(pl.multiple_of(i * 256, 256), 256)
            g = g_ref[rows, :]
            sg = _sigmoid(g)
            dmx = dm_ref[rows, :]
            ov = o_ref[rows, :]
            dg_ref[rows, :] = (dmx * ov * (sg * (1.0 + g * (1.0 - sg)))).astype(bf16)
            do = dmx * (g * sg)
            do_scr[rows, :] = do
            dl_scr[rows, :] = _split_dot(do * ov, bones)
            z = jnp.zeros((256, LANES), f32)
            dq_acc[rows, :] = z
            dk_acc[rows, :] = z
            dv_acc[rows, :] = z
            return carry

        lax.fori_loop(0, s // 256, pro, 0)

        def per_head(t):
            return jnp.concatenate([t[:, :LANES], t[:, LANES:]], axis=0)

        def both_heads(t):
            tr = pltpu.roll(t, HEAD_DIM, 1)
            return jnp.concatenate([jnp.where(head0, t, tr), jnp.where(head0, tr, t)], axis=1)

        for d in DILATIONS:
            nb = s // (BLK * d)

            def it(i, carry, d=d, nb=nb):
                r, blk = i // nb, i % nb
                st = blk * (BLK * d) + r
                stp = jnp.maximum(blk - 1, 0) * (BLK * d) + r
                rows = pl.ds(st, BLK, stride=d)
                rows_p = pl.ds(stp, BLK, stride=d)
                has_prev = blk > 0
                q = q_ref[rows, :]
                kc, kp = k_ref[rows, :], k_ref[rows_p, :]
                vc, vp = v_ref[rows, :], v_ref[rows_p, :]
                do = do_scr[rows, :]
                lse = l_ref[rows, :]
                dl = dl_scr[rows, :]
                qs = q * 0.125
                qs16, q16, do16 = qs.astype(bf16), q.astype(bf16), do.astype(bf16)
                kst_c, kst_p = _stack_heads(kc.astype(bf16), head0), _stack_heads(kp.astype(bf16), head0)
                vst_c, vst_p = _stack_heads(vc.astype(bf16), head0), _stack_heads(vp.astype(bf16), head0)
                sc = jnp.where(tri2, _nt(qs16, kst_c), jnp.where(has_prev, _nt(qs16, kst_p), -jnp.inf))
                pt = jnp.exp(sc - both_heads(lse))
                dp = jnp.where(tri2, _nt(do16, vst_c), _nt(do16, vst_p))
                ds16 = (pt * (dp - both_heads(dl)) * 0.125).astype(bf16)
                pt16 = pt.astype(bf16)
                zero = jnp.zeros_like(pt16)
                dsc, dsp = jnp.where(tri2, ds16, zero), jnp.where(tri2, zero, ds16)
                pc, pp = jnp.where(tri2, pt16, zero), jnp.where(tri2, zero, pt16)
                pd = jnp.where(has_prev, jnp.exp(_split_dot(qs * kp, bones) - lse), 0.0)
                dsd = pd * (_split_dot(do * vp, bones) - dl) * 0.125
                qst, dost = _stack_heads(q16, head0), _stack_heads(do16, head0)
                dq_acc[rows, :] += _nn(dsc, kst_c) + _nn(dsp, kst_p) + dsd * kp
                dk_acc[rows, :] += _tn(per_head(dsc), qst)
                dv_acc[rows, :] += _tn(per_head(pc), dost)
                dk_acc[rows_p, :] += _tn(per_head(dsp), qst) + dsd * q
                dv_acc[rows_p, :] += _tn(per_head(pp), dost) + pd * do
                return carry

            lax.fori_loop(0, n_it, it, 0, unroll=2)

        def epi(i, carry):
            rows = pl.ds(pl.multiple_of(i * 256, 256), 256)
            dq_ref[rows, :] = dq_acc[rows, :].astype(bf16)
            dk_ref[rows, :] = dk_acc[rows, :].astype(bf16)
            dv_ref[rows, :] = dv_acc[rows, :].astype(bf16)
            return carry

        lax.fori_loop(0, s // 256, epi, 0)

    col = lambda base: pl.BlockSpec((s, LANES), lambda h: (0, base + h))
    outs = pl.pallas_call(
        body, name="attn_bwd", grid=(N_PAIRS,),
        in_specs=[col(0), col(8), col(16), col(24), col(0), col(0), col(0)],
        out_specs=[col(0)] * 4,
        out_shape=[SDS((s, D_ATTN), bf16)] * 4,
        scratch_shapes=[pltpu.VMEM((s, LANES), f32)] * 5,
        compiler_params=pltpu.CompilerParams(dimension_semantics=("parallel",)),
    )(proj, proj, proj, proj, o, lb, dmix)
    return outs


def _ssd_bwd(proj, y, states, dmix, conv_w, conv_b, dtb16, alog16, alog_f, d_f, nw):
    s = proj.shape[0]
    nc = s // CHUNK
    gw = D_SSM // N_GROUPS

    def body(xs_ref, bc_ref, xs_tail, bc_tail, dt_ref, z_ref, y_ref, st_ref, dm_ref, cw_ref, cb_ref, dtb_ref,
             alog16_ref, alogf_ref, df_ref, nw_ref, emat_ref, fold_ref, out_ref, gconv_ref, gvec_ref, gdt_ref,
             dh_scr, head_scr, xpad, dcpad, da_scr, dxdt_scr, dbc_scr):
        i = pl.program_id(0)
        c = nc - 1 - i

        @pl.when(i == 0)
        def _():
            dh_scr[...] = jnp.zeros_like(dh_scr)
            head_scr[...] = jnp.zeros_like(head_scr)
            gconv_ref[...] = jnp.zeros_like(gconv_ref)
            gvec_ref[...] = jnp.zeros_like(gvec_ref)
            gdt_ref[...] = jnp.zeros_like(gdt_ref)

        cv, sig, xbc, pre, dt_f, al_f, al_x, al_t = _ssd_common(
            xs_ref, bc_ref, xs_tail, bc_tail, dt_ref, cw_ref, cb_ref, dtb_ref, alog16_ref, emat_ref, xpad, c == 0)
        head0 = _iota((CHUNK, LANES), 1) < HEAD_DIM
        sub = _iota((CHUNK, LANES), 0)
        last_row = sub == CHUNK - 1

        yv, z, dmx = y_ref[...], z_ref[...], dm_ref[...]
        sz = _sigmoid(z)
        silu = z * sz
        yz = yv * silu
        dyz_parts = []
        for g in range(N_GROUPS):
            gs = slice(g * gw, (g + 1) * gw)
            part = yz[:, gs]
            r = lax.rsqrt(jnp.mean(part * part, axis=-1, keepdims=True) + EPS)
            nh = part * r
            gvec_ref[0:1, gs] += jnp.sum(dmx[:, gs] * nh, axis=0, keepdims=True)
            dn = dmx[:, gs] * nw_ref[:, gs]
            dyz_parts.append(r * (dn - nh * jnp.mean(dn * nh, axis=-1, keepdims=True)))
        dyz = jnp.concatenate(dyz_parts, axis=1)
        dy = dyz * silu
        out_ref[:, 0:D_SSM] = (dyz * yv * (sz * (1.0 + z * (1.0 - sz)))).astype(bf16)

        x_all = xbc[:, 0:D_SSM]
        gvec_ref[2:3, :] += jnp.sum(dy * x_all, axis=0, keepdims=True)

        for g in range(N_GROUPS):
            bm = xbc[:, D_SSM + g * D_STATE: D_SSM + (g + 1) * D_STATE].astype(bf16)
            cm = xbc[:, D_SSM + (N_GROUPS + g) * D_STATE: D_SSM + (N_GROUPS + g + 1) * D_STATE].astype(bf16)
            gmat = _nt(cm, bm)
            dgm = jnp.zeros((CHUNK, CHUNK), f32)
            db = jnp.zeros((CHUNK, D_STATE), f32)
            dc = jnp.zeros((CHUNK, D_STATE), f32)
            for pair in range(4 * g, 4 * g + 4):
                sl = slice(pair * LANES, (pair + 1) * LANES)
                xp, dtp, alp, dyp = x_all[:, sl], dt_f[:, sl], al_f[:, sl], dy[:, sl]
                xdt = xp * dtp
                xdt16 = xdt.astype(bf16)
                al_last = alp[CHUNK - 1:CHUNK, :]
                e_l = jnp.exp(alp)
                wf = jnp.exp(al_last - alp)
                e_last = jnp.exp(al_last)
                hp = st_ref[:, sl]
                hp16 = hp.astype(bf16)
                dhn = dh_scr[:, sl]
                dhn16 = dhn.astype(bf16)
                y_off = e_l * _nn(cm, hp16)
                dch16 = (dyp * e_l).astype(bf16)
                dc = dc + _nt(dch16, hp16)
                dh_out = _tn(cm, dch16)
                dal = dyp * y_off
                xw16 = (wf * xdt).astype(bf16)
                db = db + _nt(xw16, dhn16)
                dxw = _nn(bm, dhn16)
                dxdt = dxw * wf
                dwf = dxw * xdt * wf
                dal = dal - dwf
                dal_last = jnp.sum(dwf, axis=0, keepdims=True) + jnp.sum(dhn * hp, axis=0, keepdims=True) * e_last
                dh_scr[:, sl] = e_last * dhn + dh_out
                for h in range(2):
                    mh = head0 if h == 0 else jnp.logical_not(head0)
                    dyh16 = jnp.where(mh, dyp, 0.0).astype(bf16)
                    lmat = _decay_mat(al_x, al_t, pair, h)
                    mm = gmat * lmat
                    dmm = _nt(dyh16, xdt16)
                    dxdt = dxdt + _tn(mm.astype(bf16), dyh16)
                    n16 = (dmm * mm).astype(bf16)
                    jh = jnp.where(mh, 1.0 / HEAD_DIM, 0.0).astype(bf16)
                    dal = dal + _nn(n16, jh) - _tn(n16, jh)
                    dgm = dgm + dmm * lmat
                da_scr[:, sl] = dal + jnp.where(last_row, dal_last, 0.0)
                dxdt_scr[:, sl] = dxdt
            dgm16 = dgm.astype(bf16)
            dbc_scr[:, g * D_STATE:(g + 1) * D_STATE] = db + _tn(dgm16, cm)
            dbc_scr[:, (N_GROUPS + g) * D_STATE:(N_GROUPS + g + 1) * D_STATE] = dc + _nn(dgm16, bm)

        sub_c, lane_c = _iota((CHUNK, CHUNK), 0), _iota((CHUNK, CHUNK), 1)
        tri_t = (lane_c >= sub_c).astype(bf16)
        dadt = _dot_01_left(tri_t, da_scr[...], 2)
        a_f = -jnp.exp(alogf_ref[...])
        dxdt_all = dxdt_scr[...]
        ddt_f = dxdt_all * x_all + a_f * dadt
        gvec_ref[1:2, :] += jnp.sum(dt_f * dadt, axis=0, keepdims=True) * a_f
        dx = df_ref[...] * dy + dxdt_all * dt_f
        ddt_raw = _dot_01(ddt_f, fold_ref[...], 2) * _sigmoid(pre)
        gdt_ref[0:1, :] += jnp.sum(ddt_raw, axis=0, keepdims=True)
        out_ref[:, D_SSM + D_CONV:D_SSM + D_CONV + LANES] = ddt_raw.astype(bf16)
        out_ref[:, D_SSM + D_CONV + LANES:] = jnp.zeros((CHUNK, 3 * LANES), bf16)

        dsil = sig * (1.0 + cv * (1.0 - sig))
        dcv_x = dx * dsil[:, 0:D_SSM]
        dcv_bc = dbc_scr[...] * dsil[:, D_SSM:]
        dcpad[0:CHUNK, 0:D_SSM] = dcv_x
        dcpad[0:CHUNK, D_SSM:] = dcv_bc
        dcpad[CHUNK:, :] = head_scr[...]
        dcv = dcpad[0:CHUNK, :]
        gconv_ref[4:5, :] += jnp.sum(dcv, axis=0, keepdims=True)
        draw = jnp.zeros((CHUNK, D_CONV), f32)
        for j in range(4):
            gconv_ref[j:j + 1, :] += jnp.sum(dcv * xpad[pl.ds(5 + j, CHUNK), :], axis=0, keepdims=True)
            draw = draw + cw_ref[j:j + 1, :] * dcpad[pl.ds(3 - j, CHUNK), :]
        head_scr[...] = dcpad[0:8, :]
        out_ref[:, D_SSM:D_SSM + D_CONV] = draw.astype(bf16)

    order = lambda i: nc - 1 - i
    row = lambda w, cb=0: pl.BlockSpec((CHUNK, w), lambda i: (nc - 1 - i, cb))
    return pl.pallas_call(
        body, name="ssd_bwd", grid=(nc,),
        in_specs=_ssd_in_specs(order) + [row(D_SSM), pl.BlockSpec((None, D_STATE, D_SSM), lambda i: (nc - 1 - i, 0, 0)),
                                         row(D_SSM, 1), _full((4, D_CONV)), _full((1, D_CONV)), _full((1, LANES)),
                                         _full((1, LANES)), _full((1, D_SSM)), _full((1, D_SSM)), _full((1, D_SSM)),
                                         _full((LANES, 2 * D_SSM)), _full((D_SSM, LANES))],
        out_specs=[row(3072), _full((8, D_CONV)), _full((8, D_SSM)), _full((8, LANES))],
        out_shape=[SDS((s, 3072), bf16), SDS((8, D_CONV), f32), SDS((8, D_SSM), f32), SDS((8, LANES), f32)],
        scratch_shapes=[pltpu.VMEM((D_STATE, D_SSM), f32), pltpu.VMEM((8, D_CONV), f32),
                        pltpu.VMEM((8 + CHUNK, D_CONV), f32), pltpu.VMEM((8 + CHUNK, D_CONV), f32),
                        pltpu.VMEM((CHUNK, D_SSM), f32), pltpu.VMEM((CHUNK, D_SSM), f32),
                        pltpu.VMEM((CHUNK, 2 * N_GROUPS * D_STATE), f32)],
        compiler_params=pltpu.CompilerParams(dimension_semantics=("arbitrary",)),
    )(proj, proj, proj, proj, proj, proj, y, states, dmix, conv_w, conv_b, dtb16, alog16, alog_f, d_f, nw,
      _expand_mat(), _fold_mat())


def _col_blocks(parts, tile):
    counts = [p.shape[1] // tile for p in parts]
    offs = [sum(counts[:t]) for t in range(len(parts))]
    return offs, counts, sum(counts)


def _inproj_bwd(dparts, wt, x, nw, dres, chip_sums):
    s, d = x.shape
    tm, tk = 1024, 1024
    offs, counts, nk = _col_blocks(dparts, tk)
    npart, nx = len(dparts), len(chip_sums)
    ni = s // tm

    def body(*refs):
        dp_refs = refs[:npart]
        w_ref, x_ref, nw_ref, dres_ref = refs[npart:npart + 4]
        cs_in = refs[npart + 4:npart + 4 + nx]
        gx_ref, gnw_ref = refs[npart + 4 + nx:npart + 6 + nx]
        cs_out = refs[npart + 6 + nx:npart + 6 + 2 * nx]
        acc, send_sems, recv_sems, local_sems = refs[npart + 6 + 2 * nx:]
        i, k = pl.program_id(0), pl.program_id(1)

        @pl.when(jnp.logical_and(i == 0, k == 0))
        def _():
            gnw_ref[...] = jnp.zeros_like(gnw_ref)
            if nx:
                mine, sends, _ = _chip_exchange_copies(cs_in, cs_out, send_sems, recv_sems, local_sems)
                for cp in mine + sends:
                    cp.start()

        @pl.when(jnp.logical_and(i == ni - 1, k == nk - 1))
        def _():
            if nx:
                mine, sends, recvs = _chip_exchange_copies(cs_in, cs_out, send_sems, recv_sems, local_sems)
                for cp in recvs:
                    cp.wait_recv()
                for cp in sends:
                    cp.wait_send()
                for cp in mine:
                    cp.wait()

        @pl.when(k == 0)
        def _():
            acc[...] = jnp.zeros_like(acc)

        for t in range(npart):
            @pl.when(jnp.logical_and(k >= offs[t], k < offs[t] + counts[t]))
            def _(t=t):
                acc[...] += _nn(dp_refs[t][...], w_ref[...])

        @pl.when(k == nk - 1)
        def _():
            xv = x_ref[...]
            r = lax.rsqrt(jnp.mean(xv * xv, axis=-1, keepdims=True) + EPS)
            xn = xv * r
            du = acc[...]
            gnw_ref[0:1, :] += jnp.sum(du * xn, axis=0, keepdims=True)
            dn = du * nw_ref[...]
            gx_ref[...] = dres_ref[...] + r * (dn - xn * jnp.mean(dn * xn, axis=-1, keepdims=True))

    def piece(t):
        return pl.BlockSpec((tm, tk), lambda i, k: (i, jnp.clip(k - offs[t], 0, counts[t] - 1)))

    anyspec = pl.BlockSpec(memory_space=pl.ANY)
    outs = pl.pallas_call(
        body, name="inproj_bwd", grid=(ni, nk),
        in_specs=[piece(t) for t in range(npart)] + [
            pl.BlockSpec((tk, d), lambda i, k: (k, 0)),
            pl.BlockSpec((tm, d), lambda i, k: (i, 0)), pl.BlockSpec((1, d), lambda i, k: (0, 0)),
            pl.BlockSpec((tm, d), lambda i, k: (i, 0))] + [anyspec] * nx,
        out_specs=[pl.BlockSpec((tm, d), lambda i, k: (i, 0)), pl.BlockSpec((8, d), lambda i, k: (0, 0))] + [anyspec] * nx,
        out_shape=[SDS((s, d), f32), SDS((8, d), f32)] + [SDS(a.shape, a.dtype) for a in chip_sums],
        scratch_shapes=[pltpu.VMEM((tm, d), f32)] + _chip_exchange_scratch(max(nx, 1)),
        compiler_params=pltpu.CompilerParams(dimension_semantics=("arbitrary", "arbitrary")),
    )(*dparts, wt, x, nw, dres, *chip_sums)
    return outs[0], outs[1], outs[2:]


def _matmul_tn(a_parts, b_parts, name):
    tile, tk = 1024, 512
    s = a_parts[0].shape[0]
    nk = s // tk
    na, nb = len(a_parts), len(b_parts)
    offs_a, counts_a, ni = _col_blocks(a_parts, tile)
    offs_b, counts_b, nj = _col_blocks(b_parts, tile)

    def body(*refs):
        a_refs, b_refs, o_ref = refs[:na], refs[na:na + nb], refs[na + nb]
        i, j = pl.program_id(0), pl.program_id(1)

        @pl.when(pl.program_id(2) == 0)
        def _():
            o_ref[...] = jnp.zeros_like(o_ref)

        for ta in range(na):
            for tb in range(nb):
                in_a = jnp.logical_and(i >= offs_a[ta], i < offs_a[ta] + counts_a[ta])
                in_b = jnp.logical_and(j >= offs_b[tb], j < offs_b[tb] + counts_b[tb])

                @pl.when(jnp.logical_and(in_a, in_b))
                def _(ta=ta, tb=tb):
                    o_ref[...] += _tn(a_refs[ta][...], b_refs[tb][...])

    def spec(offs, counts, t, axis):
        def index(i, j, k):
            pos = (i, j)[axis]
            mine = jnp.logical_and(pos >= offs[t], pos < offs[t] + counts[t])
            return jnp.where(mine, k, 0), jnp.clip(pos - offs[t], 0, counts[t] - 1)
        return pl.BlockSpec((tk, tile), index)

    return pl.pallas_call(
        body, name=name, grid=(ni, nj, nk),
        in_specs=[spec(offs_a, counts_a, t, 0) for t in range(na)] + [spec(offs_b, counts_b, t, 1) for t in range(nb)],
        out_specs=pl.BlockSpec((tile, tile), lambda i, j, k: (i, j)),
        out_shape=SDS((ni * tile, nj * tile), f32),
        compiler_params=pltpu.CompilerParams(dimension_semantics=("parallel", "parallel", "arbitrary")),
    )(*a_parts, *b_parts)


def _adamw(w, g, m, v):
    m = ADAM_B1 * m + (1.0 - ADAM_B1) * g
    v = ADAM_B2 * v + (1.0 - ADAM_B2) * (g * g)
    m_hat = m / (1.0 - ADAM_B1 ** ADAM_STEP)
    v_hat = v / (1.0 - ADAM_B2 ** ADAM_STEP)
    delta = -ADAM_LR * (m_hat / (jnp.sqrt(v_hat) + ADAM_EPS) + ADAM_WD * w)
    return delta, m, v


def _sum_adamw(own, parts, w, m, v, name):
    r, c = w.shape
    tc = 256

    def body(o_ref, p_ref, w_ref, m_ref, v_ref, g_ref, d_ref, nm_ref, nv_ref):
        my_q = 2 * lax.axis_index("x") + lax.axis_index("y")
        own_v = o_ref[...]
        g = jnp.where(my_q == 0, own_v, p_ref[0].astype(f32))
        for q in range(1, 4):
            g = g + jnp.where(my_q == q, own_v, p_ref[q].astype(f32))
        g_ref[...] = g
        d_ref[...], nm_ref[...], nv_ref[...] = _adamw(w_ref[...], g, m_ref[...], v_ref[...])

    blk = pl.BlockSpec((r, tc), lambda i: (0, i))
    return pl.pallas_call(
        body, name=name, grid=(c // tc,),
        in_specs=[blk, pl.BlockSpec((4, r, tc), lambda i: (0, 0, i)), blk, blk, blk],
        out_specs=[blk] * 4, out_shape=[SDS((r, c), f32)] * 4,
        compiler_params=pltpu.CompilerParams(dimension_semantics=("parallel",)),
    )(own, parts, w, m, v)


def _sum_small(parts):
    def body(p_ref, o_ref):
        t = p_ref[0]
        for j in range(1, N_DEV):
            t = t + p_ref[j]
        o_ref[...] = t
        row_h = _iota((D_SSM, LANES), 0) // HEAD_DIM
        fold = (row_h == _iota((D_SSM, LANES), 1)).astype(f32)
        lower = t[8:16, 0:LANES]
        folded = _nn_hi(t[8:16, 0:D_SSM], fold)
        loss = jnp.sum(t[11:12, 0:D_MODEL], axis=1, keepdims=True) * (0.5 / D_MODEL)
        row = _iota((8, LANES), 0)
        o_ref[8:16, 0:LANES] = jnp.where(row < 2, folded, jnp.where(row == 4, loss, lower))

    return pl.pallas_call(body, name="sum_small", out_shape=SDS((PACK_ROWS, PACK_W), f32),
                          in_specs=[pl.BlockSpec(memory_space=pltpu.VMEM)],
                          out_specs=pl.BlockSpec(memory_space=pltpu.VMEM))(parts)


def _adamw_small(w, g, m, v):
    def body(w_ref, g_ref, m_ref, v_ref, d_ref, nm_ref, nv_ref):
        d_ref[...], nm_ref[...], nv_ref[...] = _adamw(w_ref[...], g_ref[...], m_ref[...], v_ref[...])

    vm = pl.BlockSpec(memory_space=pltpu.VMEM)
    return pl.pallas_call(body, name="adamw_small", out_shape=[SDS(w.shape, f32)] * 3,
                          in_specs=[vm] * 4, out_specs=[vm] * 3)(w, g, m, v)


def _pad_lanes(v, width):
    return jnp.pad(v, ((0, 0), (0, width - v.shape[1])))


def _local_step(x, tgt, norm_pre_w, wt, conv_w, conv_b, dt_bias, a_log, d_skip, ssm_norm_w, wo, norm_post_w,
                reduce_in_chip):
    dtb16 = _pad_lanes(dt_bias, LANES)
    alog16 = _pad_lanes(a_log, LANES)
    alog_f = jnp.repeat(a_log, HEAD_DIM, axis=1)
    d_f = jnp.repeat(d_skip, HEAD_DIM, axis=1)

    proj, u = _prenorm_inproj(x, norm_pre_w, wt)
    o, lb, mix_a = _attn_fwd(proj)
    mix_s, y, states = _ssd_fwd(proj, conv_w, conv_b, dtb16, alog16, alog_f, d_f, ssm_norm_w)
    dmix, dout, dres, acc_post = _outproj_loss(mix_a, mix_s, wo, x, tgt, norm_post_w)
    dq, dk, dv, dg = _attn_bwd(proj, o, lb, dmix)
    dzxd, g_conv, g_vec, g_dt = _ssd_bwd(proj, y, states, dmix, conv_w, conv_b, dtb16, alog16, alog_f, d_f, ssm_norm_w)
    dparts = [dq, dk, dv, dg, dzxd]
    dw_out = _matmul_tn([mix_a, mix_s], [dout], "dw_out")
    dw_in = _matmul_tn(dparts, [u], "dw_in")
    chip_sums, carry = reduce_in_chip(dw_in, dw_out)
    grad_x, g_pre, exchanged = _inproj_bwd(dparts, wt, x, norm_pre_w, dres, chip_sums)

    rows = [g_conv[0:5], _pad_lanes(g_pre[0:1], PACK_W), _pad_lanes(g_vec[0:1], PACK_W),
            _pad_lanes(acc_post[1:2], PACK_W), _pad_lanes(g_vec[1:3], PACK_W), _pad_lanes(g_dt[0:1], PACK_W),
            _pad_lanes(acc_post[0:1], PACK_W), jnp.zeros((4, PACK_W), f32)]
    return grad_x, carry, exchanged, jnp.concatenate(rows, axis=0)


def kernel(x, norm_pre_w, w_in, conv_w, conv_b, dt_bias, a_log, d_skip, ssm_norm_w, w_out, norm_post_w, loss_target, m_norm_pre_w, m_w_in, m_conv_w, m_conv_b, m_dt_bias, m_a_log, m_d_skip, m_ssm_norm_w, m_w_out, m_norm_post_w, v_norm_pre_w, v_w_in, v_conv_w, v_conv_b, v_dt_bias, v_a_log, v_d_skip, v_ssm_norm_w, v_w_out, v_norm_post_w):
    shard_in = w_in.shape[2]
    shard_cv = conv_w.shape[2]
    me = 4 * lax.axis_index("x") + 2 * lax.axis_index("y") + lax.axis_index("c")

    g_in, g_out, g_cw = _all_gather([w_in[0].T.astype(bf16), w_out[0].astype(bf16), conv_w[0]])
    wt = jnp.pad(g_in.reshape(N_DEV * shard_in, D_MODEL), ((0, NP - N_DEV * shard_in), (0, 0)))
    wo = g_out.reshape(N_DEV * w_out.shape[1], D_MODEL)
    cw = g_cw.transpose(1, 0, 2).reshape(4, D_CONV)

    def reduce_in_chip(dw_in, dw_out):
        send_in = dw_in[:N_DEV * shard_in].reshape(4, 2, shard_in, D_MODEL).transpose(1, 0, 2, 3)
        send_out = dw_out.reshape(4, 2, w_out.shape[1], D_MODEL).transpose(1, 0, 2, 3)
        got_in, got_out = _sibling_swap([send_in, send_out])
        chip_in, own_in = _chip_sum(send_in, got_in, "chip_sum_w_in")
        chip_out, own_out = _chip_sum(send_out, got_out, "chip_sum_w_out")
        return [chip_in, chip_out], (own_in, own_out)

    grad_x, (own_in, own_out), (parts_in, parts_out), pack = _local_step(
        x[0], loss_target[0], norm_pre_w, wt, cw, conv_b, dt_bias, a_log, d_skip, ssm_norm_w, wo, norm_post_w,
        reduce_in_chip)
    parts_small = _gather_small(pack)

    g_w_in, d_w_in, nm_w_in, nv_w_in = (a.T for a in _sum_adamw(
        own_in, parts_in, w_in[0].T, m_w_in[0].T, v_w_in[0].T, "sum_adamw_w_in"))
    g_w_out, d_w_out, nm_w_out, nv_w_out = _sum_adamw(own_out, parts_out, w_out[0], m_w_out[0], v_w_out[0], "sum_adamw_w_out")
    tot = _sum_small(parts_small)

    g_cw_all = tot[0:4]
    small_g = {
        "conv_w": lax.dynamic_slice(g_cw_all, (0, me * shard_cv), (4, shard_cv)),
        "conv_b": tot[4:5], "norm_pre_w": tot[5:6, :D_MODEL], "ssm_norm_w": tot[6:7, :D_SSM],
        "norm_post_w": tot[7:8, :D_MODEL], "a_log": tot[8:9, :16], "d_skip": tot[9:10, :16], "dt_bias": tot[10:11, :16],
    }
    loss = tot[12, 0]
    small_w = {"conv_w": (conv_w[0], m_conv_w[0], v_conv_w[0]), "conv_b": (conv_b, m_conv_b, v_conv_b),
               "norm_pre_w": (norm_pre_w, m_norm_pre_w, v_norm_pre_w), "ssm_norm_w": (ssm_norm_w, m_ssm_norm_w, v_ssm_norm_w),
               "norm_post_w": (norm_post_w, m_norm_post_w, v_norm_post_w), "a_log": (a_log, m_a_log, v_a_log),
               "d_skip": (d_skip, m_d_skip, v_d_skip), "dt_bias": (dt_bias, m_dt_bias, v_dt_bias)}
    names = list(small_w)
    sizes = [small_g[k].size for k in names]
    tot_size = sum(sizes)
    pad_to = -(-tot_size // 1024) * 1024

    def flat(arrs):
        v = jnp.concatenate([a.reshape(-1) for a in arrs])
        return jnp.pad(v, (0, pad_to - tot_size)).reshape(pad_to // LANES, LANES)

    fw = flat([small_w[k][0] for k in names])
    fg = flat([small_g[k] for k in names])
    fm = flat([small_w[k][1] for k in names])
    fv = jnp.pad(jnp.concatenate([small_w[k][2].reshape(-1) for k in names]), (0, pad_to - tot_size),
                 constant_values=1.0).reshape(pad_to // LANES, LANES)
    fd, fnm, fnv = _adamw_small(fw, fg, fm, fv)

    def unflat(f):
        out, off = {}, 0
        v = f.reshape(-1)
        for k, n in zip(names, sizes):
            out[k] = v[off:off + n].reshape(small_g[k].shape)
            off += n
        return out

    sd, snm, snv = unflat(fd), unflat(fnm), unflat(fnv)
    lead = lambda a: a[None]
    order = ["norm_pre_w", "w_in", "conv_w", "conv_b", "dt_bias", "a_log", "d_skip", "ssm_norm_w", "w_out", "norm_post_w"]
    grads = dict(small_g, w_in=g_w_in, w_out=g_w_out)
    deltas = dict(sd, w_in=d_w_in, w_out=d_w_out)
    new_m = dict(snm, w_in=nm_w_in, w_out=nm_w_out)
    new_v = dict(snv, w_in=nv_w_in, w_out=nv_w_out)

    def shaped(dct, k):
        a = dct[k]
        return lead(a) if k in ("w_in", "w_out", "conv_w") else a

    return (loss, grad_x[None], *[shaped(grads, k) for k in order], *[shaped(deltas, k) for k in order],
            *[shaped(new_m, k) for k in order], *[shaped(new_v, k) for k in order])
```

```python
import functools
import math

import jax
import jax.numpy as jnp
import numpy as np
from jax import lax
from jax.experimental import pallas as pl
from jax.experimental.pallas import tpu as pltpu

f32, bf16 = jnp.float32, jnp.bfloat16
SDS = jax.ShapeDtypeStruct
HIGHEST = lax.Precision.HIGHEST
MESH = pl.DeviceIdType.MESH

N_DEV = 8
D_MODEL = 1024
D_ATTN = 1024
D_SSM = 1024
HEAD_DIM = 64
N_PAIRS = 8
D_STATE = 128
N_GROUPS = 2
D_CONV = D_SSM + 2 * N_GROUPS * D_STATE
D_IN_PROJ = 4 * D_ATTN + D_SSM + D_CONV + 16
NP = 7168
CHUNK = 128
BLK = 128
DILATIONS = (1, 4, 16)
EPS = 1e-6
LANES = 128
COL_Z, COL_XS, COL_BC, COL_DT = 4096, 5120, 6144, 6656

ADAM_LR, ADAM_B1, ADAM_B2, ADAM_EPS, ADAM_WD, ADAM_STEP = 0.001, 0.9, 0.999, 1e-08, 0.01, 10

PACK_ROWS, PACK_W = 16, 1536


def _nt(a, b):
    return lax.dot_general(a, b, (((1,), (1,)), ((), ())), preferred_element_type=f32)


def _tn(a, b):
    return lax.dot_general(a, b, (((0,), (0,)), ((), ())), preferred_element_type=f32)


def _nn(a, b):
    return jnp.dot(a, b, preferred_element_type=f32)


def _nn_hi(a, b):
    return jnp.dot(a, b, precision=HIGHEST, preferred_element_type=f32)


def _sigmoid(x):
    return 1.0 / (1.0 + jnp.exp(-x))


def _softplus(x):
    return jnp.maximum(x, 0.0) + jnp.log1p(jnp.exp(-jnp.abs(x)))


def _iota(shape, dim):
    return lax.broadcasted_iota(jnp.int32, shape, dim)


def _my_pos():
    return lax.axis_index("x"), lax.axis_index("y"), lax.axis_index("c")


def _all_gather(arrs):
    n = len(arrs)

    def body(*refs):
        ins, outs = refs[:n], refs[n:2 * n]
        send_sems, recv_sems, local_sems = refs[2 * n:]
        x, y, c = _my_pos()
        me, sibling = (x, y, c), (x, y, 1 - c)
        chips = [(1 - x, y), (x, 1 - y), (1 - x, 1 - y)]

        def slot(a, px, py, pc):
            return outs[a].at[4 * px + 2 * py + pc]

        def copy(a, k, block, to, src=None):
            return pltpu.make_async_remote_copy(
                src_ref=slot(a, *block) if src is None else src, dst_ref=slot(a, *block),
                send_sem=send_sems.at[7 * a + k], recv_sem=recv_sems.at[7 * a + k],
                device_id=to, device_id_type=MESH)

        mine = [pltpu.make_async_copy(ins[a], slot(a, *me), local_sems.at[a]) for a in range(n)]
        for cp in mine:
            cp.start()
        first = []
        for a in range(n):
            first.append(copy(a, 0, me, sibling, src=ins[a]))
            first += [copy(a, 1 + j, me, (*chip, c), src=ins[a]) for j, chip in enumerate(chips)]
        for cp in first:
            cp.start()
        passed = []
        for j, chip in enumerate(chips):
            for a in range(n):
                copy(a, 1 + j, (*chip, c), me).wait_recv()
                cp = copy(a, 4 + j, (*chip, c), sibling)
                cp.start()
                passed.append(cp)
        for a in range(n):
            copy(a, 0, sibling, me).wait_recv()
            for j, chip in enumerate(chips):
                copy(a, 4 + j, (*chip, 1 - c), me).wait_recv()
        for cp in first + passed:
            cp.wait_send()
        for cp in mine:
            cp.wait()

    anyspec = pl.BlockSpec(memory_space=pl.ANY)
    return pl.pallas_call(
        body, name="weights_all_gather",
        out_shape=[SDS((N_DEV,) + a.shape, a.dtype) for a in arrs],
        in_specs=[anyspec] * n, out_specs=[anyspec] * n,
        scratch_shapes=[pltpu.SemaphoreType.DMA((7 * n,)), pltpu.SemaphoreType.DMA((7 * n,)),
                        pltpu.SemaphoreType.DMA((n,))],
    )(*arrs)


def _sibling_swap(bigs):
    nb = len(bigs)

    def body(*refs):
        ins, outs = refs[:nb], refs[nb:2 * nb]
        send_sems, recv_sems = refs[2 * nb:]
        x, y, c = _my_pos()
        sends = []
        for a in range(nb):
            cp = pltpu.make_async_remote_copy(
                src_ref=ins[a].at[1 - c], dst_ref=outs[a], send_sem=send_sems.at[a], recv_sem=recv_sems.at[a],
                device_id=(x, y, 1 - c), device_id_type=MESH)
            cp.start()
            sends.append(cp)
        for a in range(nb):
            pltpu.make_async_remote_copy(
                src_ref=ins[a].at[c], dst_ref=outs[a], send_sem=send_sems.at[a], recv_sem=recv_sems.at[a],
                device_id=(x, y, c), device_id_type=MESH).wait_recv()
        for cp in sends:
            cp.wait_send()

    anyspec = pl.BlockSpec(memory_space=pl.ANY)
    return pl.pallas_call(
        body, name="grad_sibling_swap", out_shape=[SDS(a.shape[1:], a.dtype) for a in bigs],
        in_specs=[anyspec] * nb, out_specs=[anyspec] * nb,
        scratch_shapes=[pltpu.SemaphoreType.DMA((nb,)), pltpu.SemaphoreType.DMA((nb,))],
    )(*bigs)


def _gather_small(small):
    def body(small_in, small_out, send_sems, recv_sems, local_sem):
        x, y, c = _my_pos()
        me = 4 * x + 2 * y + c
        mine = pltpu.make_async_copy(small_in, small_out.at[me], local_sem)
        mine.start()
        sends = []
        for k in range(1, N_DEV):
            to = (me + k) % N_DEV
            cp = pltpu.make_async_remote_copy(
                src_ref=small_in, dst_ref=small_out.at[me], send_sem=send_sems.at[k - 1], recv_sem=recv_sems.at[k - 1],
                device_id=(to // 4, (to // 2) % 2, to % 2), device_id_type=MESH)
            cp.start()
            sends.append(cp)
        for k in range(1, N_DEV):
            frm = (me + N_DEV - k) % N_DEV
            pltpu.make_async_remote_copy(
                src_ref=small_in, dst_ref=small_out.at[frm], send_sem=send_sems.at[k - 1], recv_sem=recv_sems.at[k - 1],
                device_id=(x, y, c), device_id_type=MESH).wait_recv()
        for cp in sends:
            cp.wait_send()
        mine.wait()

    anyspec = pl.BlockSpec(memory_space=pl.ANY)
    return pl.pallas_call(
        body, name="small_grads_gather", out_shape=SDS((N_DEV,) + small.shape, small.dtype),
        in_specs=[anyspec], out_specs=anyspec,
        scratch_shapes=[pltpu.SemaphoreType.DMA((7,)), pltpu.SemaphoreType.DMA((7,)), pltpu.SemaphoreType.DMA(())],
    )(small)


def _chip_sum(mine, got, name):
    _, nq, r, cdim = mine.shape
    tc = 256

    def body(m_ref, g_ref, s16_ref, own_ref):
        q = pl.program_id(1)
        c = lax.axis_index("c")
        my_q = 2 * lax.axis_index("x") + lax.axis_index("y")
        tot = m_ref[c] + g_ref[...]
        s16_ref[...] = tot.astype(bf16)

        @pl.when(q == my_q)
        def _():
            own_ref[...] = tot

    return pl.pallas_call(
        body, name=name, grid=(cdim // tc, nq),
        in_specs=[pl.BlockSpec((2, None, r, tc), lambda i, q: (0, q, 0, i)),
                  pl.BlockSpec((None, r, tc), lambda i, q: (q, 0, i))],
        out_specs=[pl.BlockSpec((None, r, tc), lambda i, q: (q, 0, i)), pl.BlockSpec((r, tc), lambda i, q: (0, i))],
        out_shape=[SDS((nq, r, cdim), bf16), SDS((r, cdim), f32)],
        compiler_params=pltpu.CompilerParams(dimension_semantics=("parallel", "arbitrary")),
    )(mine, got)


def _chip_exchange_copies(ins, outs, send_sems, recv_sems, local_sems):
    nb = len(ins)
    x, y, c = _my_pos()
    my_q = 2 * x + y
    mine = [pltpu.make_async_copy(ins[a].at[my_q], outs[a].at[my_q], local_sems.at[a]) for a in range(nb)]
    sends, recvs = [], []
    for k in range(1, 4):
        to, frm = (my_q + k) % 4, (my_q + 4 - k) % 4
        for a in range(nb):
            sems = dict(send_sem=send_sems.at[3 * a + k - 1], recv_sem=recv_sems.at[3 * a + k - 1], device_id_type=MESH)
            sends.append(pltpu.make_async_remote_copy(
                src_ref=ins[a].at[to], dst_ref=outs[a].at[my_q], device_id=(to // 2, to % 2, c), **sems))
            recvs.append(pltpu.make_async_remote_copy(
                src_ref=ins[a].at[frm], dst_ref=outs[a].at[frm], device_id=(x, y, c), **sems))
    return mine, sends, recvs


def _chip_exchange_scratch(nb):
    return [pltpu.SemaphoreType.DMA((3 * nb,)), pltpu.SemaphoreType.DMA((3 * nb,)), pltpu.SemaphoreType.DMA((nb,))]


def _prenorm_inproj(x, nw, wt):
    s, d = x.shape
    npad = wt.shape[0]
    tm, tn = 1024, 512

    def body(x_ref, nw_ref, w_ref, proj_ref, u_ref):
        @pl.when(pl.program_id(1) == 0)
        def _():
            xv = x_ref[...]
            r = lax.rsqrt(jnp.mean(xv * xv, axis=-1, keepdims=True) + EPS)
            u_ref[...] = (xv * r * nw_ref[...]).astype(bf16)
        proj_ref[...] = _nt(u_ref[...], w_ref[...])

    return pl.pallas_call(
        body, name="prenorm_inproj", grid=(s // tm, npad // tn),
        in_specs=[pl.BlockSpec((tm, d), lambda i, j: (i, 0)), pl.BlockSpec((1, d), lambda i, j: (0, 0)),
                  pl.BlockSpec((tn, d), lambda i, j: (j, 0))],
        out_specs=[pl.BlockSpec((tm, tn), lambda i, j: (i, j)), pl.BlockSpec((tm, d), lambda i, j: (i, 0))],
        out_shape=[SDS((s, npad), f32), SDS((s, d), bf16)],
        compiler_params=pltpu.CompilerParams(dimension_semantics=("parallel", "arbitrary")),
    )(x, nw, wt)


def _attn_consts():
    head0 = _iota((BLK, LANES), 1) < HEAD_DIM
    tri2 = (_iota((BLK, 2 * LANES), 1) % LANES) <= _iota((BLK, 2 * LANES), 0)
    ones2 = ((_iota((LANES, 2 * LANES), 0) < HEAD_DIM) == (_iota((LANES, 2 * LANES), 1) < LANES)).astype(bf16)
    rmat = ((_iota((2 * LANES, LANES), 0) < LANES) == (_iota((2 * LANES, LANES), 1) < HEAD_DIM)).astype(bf16)
    bones = ((_iota((LANES, LANES), 0) < HEAD_DIM) == (_iota((LANES, LANES), 1) < HEAD_DIM)).astype(bf16)
    return head0, tri2, ones2, rmat, bones


def _stack_heads(x16, head0):
    zero = jnp.zeros_like(x16)
    return jnp.concatenate([jnp.where(head0, x16, zero), jnp.where(head0, zero, x16)], axis=0)


def _split_dot(x, w16):
    hi = x.astype(bf16)
    lo = (x - hi.astype(f32)).astype(bf16)
    return _nn(hi, w16) + _nn(lo, w16)


def _bf16_terms(x, terms):
    out = []
    for _ in range(terms):
        t = x.astype(bf16)
        out.append(t)
        x = x - t.astype(f32)
    return out


def _dot_01(x, w16, terms):
    return sum(_nn(t, w16) for t in _bf16_terms(x, terms))


def _dot_01_left(w16, x, terms):
    return sum(_nn(w16, t) for t in _bf16_terms(x, terms))


def _attn_fwd(proj):
    s = proj.shape[0]
    n_it = s // BLK

    def body(q_ref, k_ref, v_ref, g_ref, o_ref, l_ref, mix_ref, op0, op1, op2, lp0, lp1, lp2,
             s_a, s_b, sd_a, sd_b, p_a, p_b, m_a, m_b, pd_a, pd_b):
        op_refs, lp_refs = (op0, op1, op2), (lp0, lp1, lp2)
        head0, tri2, ones2, rmat, _ = _attn_consts()
        score_bufs, prob_bufs = ((s_a, sd_a), (s_b, sd_b)), ((p_a, m_a, pd_a), (p_b, m_b, pd_b))

        def block_rows(i, d, nb):
            r, blk = i // nb, i % nb
            st = blk * (BLK * d) + r
            stp = jnp.maximum(blk - 1, 0) * (BLK * d) + r
            return pl.ds(st, BLK, stride=d), pl.ds(stp, BLK, stride=d), blk > 0

        def scores(i, d, nb, bufs):
            rows, rows_p, has_prev = block_rows(i, d, nb)
            s_buf, sd_buf = bufs
            qs = q_ref[rows, :] * 0.125
            kc, kp = k_ref[rows, :], k_ref[rows_p, :]
            qs16 = qs.astype(bf16)
            sc = _nt(qs16, _stack_heads(kc.astype(bf16), head0))
            sp = _nt(qs16, _stack_heads(kp.astype(bf16), head0))
            s_buf[...] = jnp.where(tri2, sc, jnp.where(has_prev, sp, -jnp.inf))
            sd_buf[...] = jnp.where(has_prev, _split_dot(qs * kp, ones2), -jnp.inf)

        def softmax(bufs_in, bufs_out):
            s_buf, sd_buf = bufs_in
            p_buf, m_buf, pd_buf = bufs_out
            sc, sd2 = s_buf[...], sd_buf[...]
            m0 = jnp.max(sc[:, :LANES], axis=1, keepdims=True)
            m1 = jnp.max(sc[:, LANES:], axis=1, keepdims=True)
            m2 = jnp.concatenate([jnp.broadcast_to(m0, (BLK, LANES)), jnp.broadcast_to(m1, (BLK, LANES))], axis=1)
            m2 = jnp.maximum(m2, sd2)
            p_buf[...] = jnp.exp(sc - m2).astype(bf16)
            m_pair = jnp.where(head0, m2[:, :LANES], m2[:, LANES:])
            m_buf[...] = m_pair
            pd_buf[...] = jnp.exp(jnp.where(head0, sd2[:, :LANES], sd2[:, LANES:]) - m_pair)

        def output(i, d, nb, p, bufs):
            rows, rows_p, _ = block_rows(i, d, nb)
            p_buf, m_buf, pd_buf = bufs
            vc, vp = v_ref[rows, :], v_ref[rows_p, :]
            pt16, pd = p_buf[...], pd_buf[...]
            zero = jnp.zeros_like(pt16)
            o = (_nn(jnp.where(tri2, pt16, zero), _stack_heads(vc.astype(bf16), head0))
                 + _nn(jnp.where(tri2, zero, pt16), _stack_heads(vp.astype(bf16), head0)) + pd * vp)
            l = _nn(pt16, rmat) + pd
            op_refs[p][rows, :] = o / l
            lp_refs[p][rows, :] = m_buf[...] + jnp.log(l)

        for p, d in enumerate(DILATIONS):
            nb = s // (BLK * d)
            scores(0, d, nb, score_bufs[0])
            scores(1, d, nb, score_bufs[1])
            softmax(score_bufs[0], prob_bufs[0])

            def steps(j, carry, d=d, nb=nb, p=p):
                for par in range(2):
                    t = 2 * j + 2 + par
                    scores(t, d, nb, score_bufs[par])
                    output(t - 2, d, nb, p, prob_bufs[par])
                    softmax(score_bufs[1 - par], prob_bufs[1 - par])
                return carry

            lax.fori_loop(0, (n_it - 2) // 2, steps, 0)
            output(n_it - 2, d, nb, p, prob_bufs[0])
            softmax(score_bufs[1], prob_bufs[1])
            output(n_it - 1, d, nb, p, prob_bufs[1])

        def merge(i, carry):
            rows = pl.ds(pl.multiple_of(i * 256, 256), 256)
            l0, l1, l2 = lp0[rows, :], lp1[rows, :], lp2[rows, :]
            m = jnp.maximum(jnp.maximum(l0, l1), l2)
            e0, e1, e2 = jnp.exp(l0 - m), jnp.exp(l1 - m), jnp.exp(l2 - m)
            z = e0 + e1 + e2
            o = (e0 * op0[rows, :] + e1 * op1[rows, :] + e2 * op2[rows, :]) / z
            o_ref[rows, :] = o
            l_ref[rows, :] = m + jnp.log(z)
            g = g_ref[rows, :]
            mix_ref[rows, :] = (o * (g * _sigmoid(g))).astype(bf16)
            return carry

        lax.fori_loop(0, s // 256, merge, 0)

    col = lambda base: pl.BlockSpec((s, LANES), lambda h: (0, base + h))
    return pl.pallas_call(
        body, name="attn_fwd", grid=(N_PAIRS,),
        in_specs=[col(0), col(8), col(16), col(24)],
        out_specs=[col(0), col(0), col(0)],
        out_shape=[SDS((s, D_ATTN), f32), SDS((s, D_ATTN), f32), SDS((s, D_ATTN), bf16)],
        scratch_shapes=[pltpu.VMEM((s, LANES), f32)] * 6 + [pltpu.VMEM((BLK, 2 * LANES), f32)] * 4
        + [pltpu.VMEM((BLK, 2 * LANES), bf16)] * 2 + [pltpu.VMEM((BLK, LANES), f32)] * 4,
        compiler_params=pltpu.CompilerParams(dimension_semantics=("parallel",)),
    )(proj, proj, proj, proj)


def _expand_mat():
    colv = np.arange(2 * D_SSM)
    head = 2 * ((colv % D_SSM) // LANES) + colv // D_SSM
    return jnp.asarray(np.arange(LANES)[:, None] == head[None, :], dtype=bf16)


def _fold_mat():
    return jnp.asarray((np.arange(D_SSM) // HEAD_DIM)[:, None] == np.arange(LANES)[None, :], dtype=bf16)


def _ssd_common(xs_ref, bc_ref, xs_tail, bc_tail, dt_ref, cw_ref, cb_ref, dtb_ref, alog16_ref, emat_ref, xpad, first):
    keep = jnp.where(first, 0.0, 1.0)
    xpad[0:8, 0:D_SSM] = xs_tail[...] * keep
    xpad[0:8, D_SSM:D_CONV] = bc_tail[...] * keep
    xpad[8:8 + CHUNK, 0:D_SSM] = xs_ref[...]
    xpad[8:8 + CHUNK, D_SSM:D_CONV] = bc_ref[...]
    cv = cb_ref[...] + cw_ref[0:1, :] * xpad[pl.ds(5, CHUNK), :]
    for j in range(1, 4):
        cv = cv + cw_ref[j:j + 1, :] * xpad[pl.ds(5 + j, CHUNK), :]
    sig = _sigmoid(cv)
    xbc = cv * sig

    pre = dt_ref[...] + dtb_ref[...]
    dt16 = _softplus(pre)
    a16 = -jnp.exp(alog16_ref[...])
    sub, lane = _iota((CHUNK, CHUNK), 0), _iota((CHUNK, CHUNK), 1)
    tri = (sub >= lane).astype(f32)
    al16 = _nn_hi(tri, dt16 * a16)
    al_t = al16.T
    emat = emat_ref[...]
    dt_x = _dot_01(dt16, emat, 3)
    al_x = _dot_01(al16, emat, 3)
    lane_w = _iota((CHUNK, D_SSM), 1)
    even = (lane_w % LANES) < HEAD_DIM
    dt_f = jnp.where(even, dt_x[:, :D_SSM], dt_x[:, D_SSM:])
    al_f = jnp.where(even, al_x[:, :D_SSM], al_x[:, D_SSM:])
    return cv, sig, xbc, pre, dt_f, al_f, al_x, al_t


def _decay_mat(al_x, al_t, pair, h):
    sub, lane = _iota((CHUNK, CHUNK), 0), _iota((CHUNK, CHUNK), 1)
    col = al_x[:, h * D_SSM + pair * LANES: h * D_SSM + (pair + 1) * LANES]
    row = al_t[2 * pair + h: 2 * pair + h + 1, :]
    return jnp.exp(jnp.where(sub >= lane, col - row, -jnp.inf))


def _ssd_in_specs(order):
    blk = lambda w, cb: pl.BlockSpec((CHUNK, w), lambda i: (order(i), cb))
    tail = lambda w, cb: pl.BlockSpec((8, w), lambda i: (jnp.maximum(16 * order(i) - 1, 0), cb))
    return [blk(D_SSM, COL_XS // D_SSM), blk(512, COL_BC // 512), tail(D_SSM, COL_XS // D_SSM),
            tail(512, COL_BC // 512), blk(LANES, COL_DT // LANES), blk(D_SSM, COL_Z // D_SSM)]


def _full(shape):
    return pl.BlockSpec(shape, lambda i: (0,) * len(shape))


def _ssd_fwd(proj, conv_w, conv_b, dtb16, alog16, alog_f, d_f, nw):
    s = proj.shape[0]
    nc = s // CHUNK

    def body(xs_ref, bc_ref, xs_tail, bc_tail, dt_ref, z_ref, cw_ref, cb_ref, dtb_ref, alog16_ref, alogf_ref,
             df_ref, nw_ref, emat_ref, mix_ref, y_ref, st_ref, h_scr, xpad, y_scr):
        c = pl.program_id(0)

        @pl.when(c == 0)
        def _():
            h_scr[...] = jnp.zeros_like(h_scr)

        _, _, xbc, _, dt_f, al_f, al_x, al_t = _ssd_common(
            xs_ref, bc_ref, xs_tail, bc_tail, dt_ref, cw_ref, cb_ref, dtb_ref, alog16_ref, emat_ref, xpad, c == 0)
        head0 = _iota((CHUNK, LANES), 1) < HEAD_DIM
        st_ref[...] = h_scr[...]
        for g in range(N_GROUPS):
            bm = xbc[:, D_SSM + g * D_STATE: D_SSM + (g + 1) * D_STATE].astype(bf16)
            cm = xbc[:, D_SSM + (N_GROUPS + g) * D_STATE: D_SSM + (N_GROUPS + g + 1) * D_STATE].astype(bf16)
            gmat = _nt(cm, bm)
            for pair in range(4 * g, 4 * g + 4):
                sl = slice(pair * LANES, (pair + 1) * LANES)
                xp, dtp, alp = xbc[:, sl], dt_f[:, sl], al_f[:, sl]
                xdt = xp * dtp
                xdt16 = xdt.astype(bf16)
                al_last = alp[CHUNK - 1:CHUNK, :]
                hp = h_scr[:, sl]
                y_off = jnp.exp(alp) * _nn(cm, hp.astype(bf16))
                yd = [_nn((gmat * _decay_mat(al_x, al_t, pair, h)).astype(bf16), xdt16) for h in range(2)]
                y_scr[:, sl] = jnp.where(head0, yd[0], yd[1]) + y_off + df_ref[:, sl] * xp
                st = _tn(bm, (jnp.exp(al_last - alp) * xdt).astype(bf16))
                h_scr[:, sl] = jnp.exp(al_last) * hp + st
        y = y_scr[...]
        y_ref[...] = y
        z = z_ref[...]
        yz = y * (z * _sigmoid(z))
        gw = D_SSM // N_GROUPS
        for g in range(N_GROUPS):
            part = yz[:, g * gw:(g + 1) * gw]
            r = lax.rsqrt(jnp.mean(part * part, axis=-1, keepdims=True) + EPS)
            mix_ref[:, g * gw:(g + 1) * gw] = (part * r * nw_ref[:, g * gw:(g + 1) * gw]).astype(bf16)

    order = lambda i: i
    row = lambda w: pl.BlockSpec((CHUNK, w), lambda i: (i, 0))
    return pl.pallas_call(
        body, name="ssd_fwd", grid=(nc,),
        in_specs=_ssd_in_specs(order) + [_full((4, D_CONV)), _full((1, D_CONV)), _full((1, LANES)), _full((1, LANES)),
                                         _full((1, D_SSM)), _full((1, D_SSM)), _full((1, D_SSM)),
                                         _full((LANES, 2 * D_SSM))],
        out_specs=[row(D_SSM), row(D_SSM), pl.BlockSpec((None, D_STATE, D_SSM), lambda i: (i, 0, 0))],
        out_shape=[SDS((s, D_SSM), bf16), SDS((s, D_SSM), f32), SDS((nc, D_STATE, D_SSM), f32)],
        scratch_shapes=[pltpu.VMEM((D_STATE, D_SSM), f32), pltpu.VMEM((8 + CHUNK, D_CONV), f32),
                        pltpu.VMEM((CHUNK, D_SSM), f32)],
        compiler_params=pltpu.CompilerParams(dimension_semantics=("arbitrary",)),
    )(proj, proj, proj, proj, proj, proj, conv_w, conv_b, dtb16, alog16, alog_f, d_f, nw, _expand_mat())


def _outproj_loss(mix_a, mix_s, wo, x, tgt, npw):
    s, d = x.shape
    tm = 512

    def body(ma_ref, ms_ref, wo_ref, x_ref, t_ref, npw_ref, dmix_ref, dout_ref, dres_ref, acc_ref):
        @pl.when(pl.program_id(0) == 0)
        def _():
            acc_ref[...] = jnp.zeros_like(acc_ref)

        out = _nn(ma_ref[...], wo_ref[0:D_ATTN, :]) + _nn(ms_ref[...], wo_ref[D_ATTN:, :])
        r = lax.rsqrt(jnp.mean(out * out, axis=-1, keepdims=True) + EPS)
        on = out * r
        diff = x_ref[...] + on * npw_ref[...] - t_ref[...]
        dres = diff * (1.0 / d)
        dres_ref[...] = dres
        acc_ref[0:1, :] += jnp.sum(diff * diff, axis=0, keepdims=True)
        acc_ref[1:2, :] += jnp.sum(dres * on, axis=0, keepdims=True)
        dn = dres * npw_ref[...]
        dout = (r * (dn - on * jnp.mean(dn * on, axis=-1, keepdims=True))).astype(bf16)
        dout_ref[...] = dout
        dmix_ref[...] = _nt(dout, wo_ref[...])

    row = lambda w: pl.BlockSpec((tm, w), lambda i: (i, 0))
    return pl.pallas_call(
        body, name="outproj_loss", grid=(s // tm,),
        in_specs=[row(D_ATTN), row(D_SSM), _full((D_ATTN + D_SSM, d)), row(d), row(d), _full((1, d))],
        out_specs=[row(D_ATTN + D_SSM), row(d), row(d), _full((8, d))],
        out_shape=[SDS((s, D_ATTN + D_SSM), f32), SDS((s, d), bf16), SDS((s, d), f32), SDS((8, d), f32)],
        compiler_params=pltpu.CompilerParams(dimension_semantics=("arbitrary",)),
    )(mix_a, mix_s, wo, x, tgt, npw)


def _attn_bwd(proj, o, lb, dmix):
    s = proj.shape[0]
    n_it = s // BLK

    def body(q_ref, k_ref, v_ref, g_ref, o_ref, l_ref, dm_ref, dq_ref, dk_ref, dv_ref, dg_ref,
             dq_acc, dk_acc, dv_acc, do_scr, dl_scr, *bufs):
        head0, tri2, _, _, bones = _attn_consts()

        def pro(i, carry):
            rows = pl.ds(pl.multiple_of(i * 256, 256), 256)
            g = g_ref[rows, :]
            sg = _sigmoid(g)
            dmx = dm_ref[rows, :]
            ov = o_ref[rows, :]
            dg_ref[rows, :] = (dmx * ov * (sg * (1.0 + g * (1.0 - sg)))).astype(bf16)
            do = dmx * (g * sg)
            do_scr[rows, :] = do
            dl_scr[rows, :] = _split_dot(do * ov, bones)
            z = jnp.zeros((256, LANES), f32)
            dq_acc[rows, :] = z
            dk_acc[rows, :] = z
            dv_acc[rows, :] = z
            return carry

        lax.fori_loop(0, s // 256, pro, 0)

        def per_head(t):
            return jnp.concatenate([t[:, :LANES], t[:, LANES:]], axis=0)

        def both_heads(t):
            tr = pltpu.roll(t, HEAD_DIM, 1)
            return jnp.concatenate([jnp.where(head0, t, tr), jnp.where(head0, tr, t)], axis=1)

        mm_bufs = ((bufs[0], bufs[1], bufs[2], bufs[3]), (bufs[4], bufs[5], bufs[6], bufs[7]))
        ds_bufs = ((bufs[8], bufs[9], bufs[10], bufs[11]), (bufs[12], bufs[13], bufs[14], bufs[15]))
        op_bufs = ((bufs[16], bufs[17], bufs[18], bufs[19]), (bufs[20], bufs[21], bufs[22], bufs[23]))

        def block_rows(i, d, nb):
            r, blk = i // nb, i % nb
            st = blk * (BLK * d) + r
            stp = jnp.maximum(blk - 1, 0) * (BLK * d) + r
            return pl.ds(st, BLK, stride=d), pl.ds(stp, BLK, stride=d), blk > 0

        def products(i, d, nb, out, ops):
            rows, rows_p, has_prev = block_rows(i, d, nb)
            s_buf, dp_buf, sd_buf, dpd_buf = out
            kc_buf, kp_buf, q_buf, do_buf = ops
            q = q_ref[rows, :]
            qs = q * 0.125
            kc, kp = k_ref[rows, :], k_ref[rows_p, :]
            vc, vp = v_ref[rows, :], v_ref[rows_p, :]
            do = do_scr[rows, :]
            qs16, do16 = qs.astype(bf16), do.astype(bf16)
            kst_c, kst_p = _stack_heads(kc.astype(bf16), head0), _stack_heads(kp.astype(bf16), head0)
            vst_c, vst_p = _stack_heads(vc.astype(bf16), head0), _stack_heads(vp.astype(bf16), head0)
            kc_buf[...] = kst_c
            kp_buf[...] = kst_p
            q_buf[...] = q.astype(bf16)
            do_buf[...] = do16
            s_buf[...] = jnp.where(tri2, _nt(qs16, kst_c), jnp.where(has_prev, _nt(qs16, kst_p), -jnp.inf))
            dp_buf[...] = jnp.where(tri2, _nt(do16, vst_c), _nt(do16, vst_p))
            sd_buf[...] = _split_dot(qs * kp, bones)
            dpd_buf[...] = _split_dot(do * vp, bones)

        def softmax_grad(i, d, nb, inp, out):
            rows, _, has_prev = block_rows(i, d, nb)
            s_buf, dp_buf, sd_buf, dpd_buf = inp
            p_buf, ds_buf, pd_buf, dsd_buf = out
            lse = l_ref[rows, :]
            dl = dl_scr[rows, :]
            pt = jnp.exp(s_buf[...] - both_heads(lse))
            ds_buf[...] = (pt * (dp_buf[...] - both_heads(dl)) * 0.125).astype(bf16)
            p_buf[...] = pt.astype(bf16)
            pd = jnp.where(has_prev, jnp.exp(sd_buf[...] - lse), 0.0)
            pd_buf[...] = pd
            dsd_buf[...] = pd * (dpd_buf[...] - dl) * 0.125

        def accumulate(i, d, nb, inp, ops):
            rows, rows_p, _ = block_rows(i, d, nb)
            p_buf, ds_buf, pd_buf, dsd_buf = inp
            kc_buf, kp_buf, q_buf, do_buf = ops
            pt16, ds16, pd, dsd = p_buf[...], ds_buf[...], pd_buf[...], dsd_buf[...]
            zero = jnp.zeros_like(pt16)
            dsc, dsp = jnp.where(tri2, ds16, zero), jnp.where(tri2, zero, ds16)
            pc, pp = jnp.where(tri2, pt16, zero), jnp.where(tri2, zero, pt16)
            kst_c, kst_p, q16, do16 = kc_buf[...], kp_buf[...], q_buf[...], do_buf[...]
            kp16 = kst_p[:BLK] + kst_p[BLK:]
            qst, dost = _stack_heads(q16, head0), _stack_heads(do16, head0)
            dq_acc[rows, :] += _nn(dsc, kst_c) + _nn(dsp, kst_p) + dsd * kp16.astype(f32)
            dk_acc[rows, :] += _tn(per_head(dsc), qst)
            dv_acc[rows, :] += _tn(per_head(pc), dost)
            dk_acc[rows_p, :] += _tn(per_head(dsp), qst) + dsd * q16.astype(f32)
            dv_acc[rows_p, :] += _tn(per_head(pp), dost) + pd * do16.astype(f32)

        for d in DILATIONS:
            nb = s // (BLK * d)
            products(0, d, nb, mm_bufs[0], op_bufs[0])
            products(1, d, nb, mm_bufs[1], op_bufs[1])
            softmax_grad(0, d, nb, mm_bufs[0], ds_bufs[0])

            def steps(j, carry, d=d, nb=nb):
                for par in range(2):
                    t = 2 * j + 2 + par
                    accumulate(t - 2, d, nb, ds_bufs[par], op_bufs[par])
                    products(t, d, nb, mm_bufs[par], op_bufs[par])
                    softmax_grad(t - 1, d, nb, mm_bufs[1 - par], ds_bufs[1 - par])
                return carry

            lax.fori_loop(0, (n_it - 2) // 2, steps, 0)
            accumulate(n_it - 2, d, nb, ds_bufs[0], op_bufs[0])
            softmax_grad(n_it - 1, d, nb, mm_bufs[1], ds_bufs[1])
            accumulate(n_it - 1, d, nb, ds_bufs[1], op_bufs[1])

        def epi(i, carry):
            rows = pl.ds(pl.multiple_of(i * 256, 256), 256)
            dq_ref[rows, :] = dq_acc[rows, :].astype(bf16)
            dk_ref[rows, :] = dk_acc[rows, :].astype(bf16)
            dv_ref[rows, :] = dv_acc[rows, :].astype(bf16)
            return carry

        lax.fori_loop(0, s // 256, epi, 0)

    col = lambda base: pl.BlockSpec((s, LANES), lambda h: (0, base + h))
    outs = pl.pallas_call(
        body, name="attn_bwd", grid=(N_PAIRS,),
        in_specs=[col(0), col(8), col(16), col(24), col(0), col(0), col(0)],
        out_specs=[col(0)] * 4,
        out_shape=[SDS((s, D_ATTN), bf16)] * 4,
        scratch_shapes=[pltpu.VMEM((s, LANES), f32)] * 5
        + [pltpu.VMEM((BLK, 2 * LANES), f32)] * 2 + [pltpu.VMEM((BLK, LANES), f32)] * 2
        + [pltpu.VMEM((BLK, 2 * LANES), f32)] * 2 + [pltpu.VMEM((BLK, LANES), f32)] * 2
        + [pltpu.VMEM((BLK, 2 * LANES), bf16)] * 2 + [pltpu.VMEM((BLK, LANES), f32)] * 2
        + [pltpu.VMEM((BLK, 2 * LANES), bf16)] * 2 + [pltpu.VMEM((BLK, LANES), f32)] * 2
        + [pltpu.VMEM((2 * BLK, LANES), bf16)] * 2 + [pltpu.VMEM((BLK, LANES), bf16)] * 2
        + [pltpu.VMEM((2 * BLK, LANES), bf16)] * 2 + [pltpu.VMEM((BLK, LANES), bf16)] * 2,
        compiler_params=pltpu.CompilerParams(dimension_semantics=("parallel",)),
    )(proj, proj, proj, proj, o, lb, dmix)
    return outs


def _ssd_bwd(proj, y, states, dmix, conv_w, conv_b, dtb16, alog16, alog_f, d_f, nw):
    s = proj.shape[0]
    nc = s // CHUNK
    gw = D_SSM // N_GROUPS

    def body(xs_ref, bc_ref, xs_tail, bc_tail, dt_ref, z_ref, y_ref, st_ref, dm_ref, cw_ref, cb_ref, dtb_ref,
             alog16_ref, alogf_ref, df_ref, nw_ref, emat_ref, fold_ref, out_ref, gconv_ref, gvec_ref, gdt_ref,
             dh_scr, head_scr, xpad, dcpad, da_scr, dxdt_scr, dbc_scr):
        i = pl.program_id(0)
        c = nc - 1 - i

        @pl.when(i == 0)
        def _():
            dh_scr[...] = jnp.zeros_like(dh_scr)
            head_scr[...] = jnp.zeros_like(head_scr)
            gconv_ref[...] = jnp.zeros_like(gconv_ref)
            gvec_ref[...] = jnp.zeros_like(gvec_ref)
            gdt_ref[...] = jnp.zeros_like(gdt_ref)

        cv, sig, xbc, pre, dt_f, al_f, al_x, al_t = _ssd_common(
            xs_ref, bc_ref, xs_tail, bc_tail, dt_ref, cw_ref, cb_ref, dtb_ref, alog16_ref, emat_ref, xpad, c == 0)
        head0 = _iota((CHUNK, LANES), 1) < HEAD_DIM
        sub = _iota((CHUNK, LANES), 0)
        last_row = sub == CHUNK - 1

        yv, z, dmx = y_ref[...], z_ref[...], dm_ref[...]
        sz = _sigmoid(z)
        silu = z * sz
        yz = yv * silu
        dyz_parts = []
        for g in range(N_GROUPS):
            gs = slice(g * gw, (g + 1) * gw)
            part = yz[:, gs]
            r = lax.rsqrt(jnp.mean(part * part, axis=-1, keepdims=True) + EPS)
            nh = part * r
            gvec_ref[0:1, gs] += jnp.sum(dmx[:, gs] * nh, axis=0, keepdims=True)
            dn = dmx[:, gs] * nw_ref[:, gs]
            dyz_parts.append(r * (dn - nh * jnp.mean(dn * nh, axis=-1, keepdims=True)))
        dyz = jnp.concatenate(dyz_parts, axis=1)
        dy = dyz * silu
        out_ref[:, 0:D_SSM] = (dyz * yv * (sz * (1.0 + z * (1.0 - sz)))).astype(bf16)

        x_all = xbc[:, 0:D_SSM]
        gvec_ref[2:3, :] += jnp.sum(dy * x_all, axis=0, keepdims=True)

        for g in range(N_GROUPS):
            bm = xbc[:, D_SSM + g * D_STATE: D_SSM + (g + 1) * D_STATE].astype(bf16)
            cm = xbc[:, D_SSM + (N_GROUPS + g) * D_STATE: D_SSM + (N_GROUPS + g + 1) * D_STATE].astype(bf16)
            gmat = _nt(cm, bm)
            dgm = jnp.zeros((CHUNK, CHUNK), f32)
            db = jnp.zeros((CHUNK, D_STATE), f32)
            dc = jnp.zeros((CHUNK, D_STATE), f32)
            for pair in range(4 * g, 4 * g + 4):
                sl = slice(pair * LANES, (pair + 1) * LANES)
                xp, dtp, alp, dyp = x_all[:, sl], dt_f[:, sl], al_f[:, sl], dy[:, sl]
                xdt = xp * dtp
                xdt16 = xdt.astype(bf16)
                al_last = alp[CHUNK - 1:CHUNK, :]
                e_l = jnp.exp(alp)
                wf = jnp.exp(al_last - alp)
                e_last = jnp.exp(al_last)
                hp = st_ref[:, sl]
                hp16 = hp.astype(bf16)
                dhn = dh_scr[:, sl]
                dhn16 = dhn.astype(bf16)
                y_off = e_l * _nn(cm, hp16)
                dch16 = (dyp * e_l).astype(bf16)
                dc = dc + _nt(dch16, hp16)
                dh_out = _tn(cm, dch16)
                dal = dyp * y_off
                xw16 = (wf * xdt).astype(bf16)
                db = db + _nt(xw16, dhn16)
                dxw = _nn(bm, dhn16)
                dxdt = dxw * wf
                dwf = dxw * xdt * wf
                dal = dal - dwf
                dal_last = jnp.sum(dwf, axis=0, keepdims=True) + jnp.sum(dhn * hp, axis=0, keepdims=True) * e_last
                dh_scr[:, sl] = e_last * dhn + dh_out
                for h in range(2):
                    mh = head0 if h == 0 else jnp.logical_not(head0)
                    dyh16 = jnp.where(mh, dyp, 0.0).astype(bf16)
                    lmat = _decay_mat(al_x, al_t, pair, h)
                    mm = gmat * lmat
                    dmm = _nt(dyh16, xdt16)
                    dxdt = dxdt + _tn(mm.astype(bf16), dyh16)
                    n16 = (dmm * mm).astype(bf16)
                    jh = jnp.where(mh, 1.0 / HEAD_DIM, 0.0).astype(bf16)
                    dal = dal + _nn(n16, jh) - _tn(n16, jh)
                    dgm = dgm + dmm * lmat
                da_scr[:, sl] = dal + jnp.where(last_row, dal_last, 0.0)
                dxdt_scr[:, sl] = dxdt
            dgm16 = dgm.astype(bf16)
            dbc_scr[:, g * D_STATE:(g + 1) * D_STATE] = db + _tn(dgm16, cm)
            dbc_scr[:, (N_GROUPS + g) * D_STATE:(N_GROUPS + g + 1) * D_STATE] = dc + _nn(dgm16, bm)

        sub_c, lane_c = _iota((CHUNK, CHUNK), 0), _iota((CHUNK, CHUNK), 1)
        tri_t = (lane_c >= sub_c).astype(bf16)
        dadt = _dot_01_left(tri_t, da_scr[...], 2)
        a_f = -jnp.exp(alogf_ref[...])
        dxdt_all = dxdt_scr[...]
        ddt_f = dxdt_all * x_all + a_f * dadt
        gvec_ref[1:2, :] += jnp.sum(dt_f * dadt, axis=0, keepdims=True) * a_f
        dx = df_ref[...] * dy + dxdt_all * dt_f
        ddt_raw = _dot_01(ddt_f, fold_ref[...], 2) * _sigmoid(pre)
        gdt_ref[0:1, :] += jnp.sum(ddt_raw, axis=0, keepdims=True)
        out_ref[:, D_SSM + D_CONV:D_SSM + D_CONV + LANES] = ddt_raw.astype(bf16)
        out_ref[:, D_SSM + D_CONV + LANES:] = jnp.zeros((CHUNK, 3 * LANES), bf16)

        dsil = sig * (1.0 + cv * (1.0 - sig))
        dcv_x = dx * dsil[:, 0:D_SSM]
        dcv_bc = dbc_scr[...] * dsil[:, D_SSM:]
        dcpad[0:CHUNK, 0:D_SSM] = dcv_x
        dcpad[0:CHUNK, D_SSM:] = dcv_bc
        dcpad[CHUNK:, :] = head_scr[...]
        dcv = dcpad[0:CHUNK, :]
        gconv_ref[4:5, :] += jnp.sum(dcv, axis=0, keepdims=True)
        draw = jnp.zeros((CHUNK, D_CONV), f32)
        for j in range(4):
            gconv_ref[j:j + 1, :] += jnp.sum(dcv * xpad[pl.ds(5 + j, CHUNK), :], axis=0, keepdims=True)
            draw = draw + cw_ref[j:j + 1, :] * dcpad[pl.ds(3 - j, CHUNK), :]
        head_scr[...] = dcpad[0:8, :]
        out_ref[:, D_SSM:D_SSM + D_CONV] = draw.astype(bf16)

    order = lambda i: nc - 1 - i
    row = lambda w, cb=0: pl.BlockSpec((CHUNK, w), lambda i: (nc - 1 - i, cb))
    return pl.pallas_call(
        body, name="ssd_bwd", grid=(nc,),
        in_specs=_ssd_in_specs(order) + [row(D_SSM), pl.BlockSpec((None, D_STATE, D_SSM), lambda i: (nc - 1 - i, 0, 0)),
                                         row(D_SSM, 1), _full((4, D_CONV)), _full((1, D_CONV)), _full((1, LANES)),
                                         _full((1, LANES)), _full((1, D_SSM)), _full((1, D_SSM)), _full((1, D_SSM)),
                                         _full((LANES, 2 * D_SSM)), _full((D_SSM, LANES))],
        out_specs=[row(3072), _full((8, D_CONV)), _full((8, D_SSM)), _full((8, LANES))],
        out_shape=[SDS((s, 3072), bf16), SDS((8, D_CONV), f32), SDS((8, D_SSM), f32), SDS((8, LANES), f32)],
        scratch_shapes=[pltpu.VMEM((D_STATE, D_SSM), f32), pltpu.VMEM((8, D_CONV), f32),
                        pltpu.VMEM((8 + CHUNK, D_CONV), f32), pltpu.VMEM((8 + CHUNK, D_CONV), f32),
                        pltpu.VMEM((CHUNK, D_SSM), f32), pltpu.VMEM((CHUNK, D_SSM), f32),
                        pltpu.VMEM((CHUNK, 2 * N_GROUPS * D_STATE), f32)],
        compiler_params=pltpu.CompilerParams(dimension_semantics=("arbitrary",)),
    )(proj, proj, proj, proj, proj, proj, y, states, dmix, conv_w, conv_b, dtb16, alog16, alog_f, d_f, nw,
      _expand_mat(), _fold_mat())


def _col_blocks(parts, tile):
    counts = [p.shape[1] // tile for p in parts]
    offs = [sum(counts[:t]) for t in range(len(parts))]
    return offs, counts, sum(counts)


def _inproj_bwd(dparts, wt, x, nw, dres, chip_sums):
    s, d = x.shape
    tm, tk = 1024, 1024
    offs, counts, nk = _col_blocks(dparts, tk)
    npart, nx = len(dparts), len(chip_sums)
    ni = s // tm

    def body(*refs):
        dp_refs = refs[:npart]
        w_ref, x_ref, nw_ref, dres_ref = refs[npart:npart + 4]
        cs_in = refs[npart + 4:npart + 4 + nx]
        gx_ref, gnw_ref = refs[npart + 4 + nx:npart + 6 + nx]
        cs_out = refs[npart + 6 + nx:npart + 6 + 2 * nx]
        acc, send_sems, recv_sems, local_sems = refs[npart + 6 + 2 * nx:]
        i, k = pl.program_id(0), pl.program_id(1)

        @pl.when(jnp.logical_and(i == 0, k == 0))
        def _():
            gnw_ref[...] = jnp.zeros_like(gnw_ref)
            if nx:
                mine, sends, _ = _chip_exchange_copies(cs_in, cs_out, send_sems, recv_sems, local_sems)
                for cp in mine + sends:
                    cp.start()

        @pl.when(jnp.logical_and(i == ni - 1, k == nk - 1))
        def _():
            if nx:
                mine, sends, recvs = _chip_exchange_copies(cs_in, cs_out, send_sems, recv_sems, local_sems)
                for cp in recvs:
                    cp.wait_recv()
                for cp in sends:
                    cp.wait_send()
                for cp in mine:
                    cp.wait()

        @pl.when(k == 0)
        def _():
            acc[...] = jnp.zeros_like(acc)

        for t in range(npart):
            @pl.when(jnp.logical_and(k >= offs[t], k < offs[t] + counts[t]))
            def _(t=t):
                acc[...] += _nn(dp_refs[t][...], w_ref[...])

        @pl.when(k == nk - 1)
        def _():
            xv = x_ref[...]
            r = lax.rsqrt(jnp.mean(xv * xv, axis=-1, keepdims=True) + EPS)
            xn = xv * r
            du = acc[...]
            gnw_ref[0:1, :] += jnp.sum(du * xn, axis=0, keepdims=True)
            dn = du * nw_ref[...]
            gx_ref[...] = dres_ref[...] + r * (dn - xn * jnp.mean(dn * xn, axis=-1, keepdims=True))

    def piece(t):
        return pl.BlockSpec((tm, tk), lambda i, k: (i, jnp.clip(k - offs[t], 0, counts[t] - 1)))

    anyspec = pl.BlockSpec(memory_space=pl.ANY)
    outs = pl.pallas_call(
        body, name="inproj_bwd", grid=(ni, nk),
        in_specs=[piece(t) for t in range(npart)] + [
            pl.BlockSpec((tk, d), lambda i, k: (k, 0)),
            pl.BlockSpec((tm, d), lambda i, k: (i, 0)), pl.BlockSpec((1, d), lambda i, k: (0, 0)),
            pl.BlockSpec((tm, d), lambda i, k: (i, 0))] + [anyspec] * nx,
        out_specs=[pl.BlockSpec((tm, d), lambda i, k: (i, 0)), pl.BlockSpec((8, d), lambda i, k: (0, 0))] + [anyspec] * nx,
        out_shape=[SDS((s, d), f32), SDS((8, d), f32)] + [SDS(a.shape, a.dtype) for a in chip_sums],
        scratch_shapes=[pltpu.VMEM((tm, d), f32)] + _chip_exchange_scratch(max(nx, 1)),
        compiler_params=pltpu.CompilerParams(dimension_semantics=("arbitrary", "arbitrary")),
    )(*dparts, wt, x, nw, dres, *chip_sums)
    return outs[0], outs[1], outs[2:]


def _matmul_tn(a_parts, b_parts, name):
    tile, tk = 1024, 512
    s = a_parts[0].shape[0]
    nk = s // tk
    na, nb = len(a_parts), len(b_parts)
    offs_a, counts_a, ni = _col_blocks(a_parts, tile)
    offs_b, counts_b, nj = _col_blocks(b_parts, tile)

    def body(*refs):
        a_refs, b_refs, o_ref = refs[:na], refs[na:na + nb], refs[na + nb]
        i, j = pl.program_id(0), pl.program_id(1)

        @pl.when(pl.program_id(2) == 0)
        def _():
            o_ref[...] = jnp.zeros_like(o_ref)

        for ta in range(na):
            for tb in range(nb):
                in_a = jnp.logical_and(i >= offs_a[ta], i < offs_a[ta] + counts_a[ta])
                in_b = jnp.logical_and(j >= offs_b[tb], j < offs_b[tb] + counts_b[tb])

                @pl.when(jnp.logical_and(in_a, in_b))
                def _(ta=ta, tb=tb):
                    o_ref[...] += _tn(a_refs[ta][...], b_refs[tb][...])

    def spec(offs, counts, t, axis):
        def index(i, j, k):
            pos = (i, j)[axis]
            mine = jnp.logical_and(pos >= offs[t], pos < offs[t] + counts[t])
            return jnp.where(mine, k, 0), jnp.clip(pos - offs[t], 0, counts[t] - 1)
        return pl.BlockSpec((tk, tile), index)

    return pl.pallas_call(
        body, name=name, grid=(ni, nj, nk),
        in_specs=[spec(offs_a, counts_a, t, 0) for t in range(na)] + [spec(offs_b, counts_b, t, 1) for t in range(nb)],
        out_specs=pl.BlockSpec((tile, tile), lambda i, j, k: (i, j)),
        out_shape=SDS((ni * tile, nj * tile), f32),
        compiler_params=pltpu.CompilerParams(dimension_semantics=("parallel", "parallel", "arbitrary")),
    )(*a_parts, *b_parts)


def _adamw(w, g, m, v):
    m = ADAM_B1 * m + (1.0 - ADAM_B1) * g
    v = ADAM_B2 * v + (1.0 - ADAM_B2) * (g * g)
    m_hat = m / (1.0 - ADAM_B1 ** ADAM_STEP)
    v_hat = v / (1.0 - ADAM_B2 ** ADAM_STEP)
    delta = -ADAM_LR * (m_hat / (jnp.sqrt(v_hat) + ADAM_EPS) + ADAM_WD * w)
    return delta, m, v


def _sum_adamw(own, parts, w, m, v, name):
    r, c = w.shape
    tc = 256

    def body(o_ref, p_ref, w_ref, m_ref, v_ref, g_ref, d_ref, nm_ref, nv_ref):
        my_q = 2 * lax.axis_index("x") + lax.axis_index("y")
        own_v = o_ref[...]
        g = jnp.where(my_q == 0, own_v, p_ref[0].astype(f32))
        for q in range(1, 4):
            g = g + jnp.where(my_q == q, own_v, p_ref[q].astype(f32))
        g_ref[...] = g
        d_ref[...], nm_ref[...], nv_ref[...] = _adamw(w_ref[...], g, m_ref[...], v_ref[...])

    blk = pl.BlockSpec((r, tc), lambda i: (0, i))
    return pl.pallas_call(
        body, name=name, grid=(c // tc,),
        in_specs=[blk, pl.BlockSpec((4, r, tc), lambda i: (0, 0, i)), blk, blk, blk],
        out_specs=[blk] * 4, out_shape=[SDS((r, c), f32)] * 4,
        compiler_params=pltpu.CompilerParams(dimension_semantics=("parallel",)),
    )(own, parts, w, m, v)


def _sum_small(parts):
    def body(p_ref, o_ref):
        t = p_ref[0]
        for j in range(1, N_DEV):
            t = t + p_ref[j]
        o_ref[...] = t
        row_h = _iota((D_SSM, LANES), 0) // HEAD_DIM
        fold = (row_h == _iota((D_SSM, LANES), 1)).astype(f32)
        lower = t[8:16, 0:LANES]
        folded = _nn_hi(t[8:16, 0:D_SSM], fold)
        loss = jnp.sum(t[11:12, 0:D_MODEL], axis=1, keepdims=True) * (0.5 / D_MODEL)
        row = _iota((8, LANES), 0)
        o_ref[8:16, 0:LANES] = jnp.where(row < 2, folded, jnp.where(row == 4, loss, lower))

    return pl.pallas_call(body, name="sum_small", out_shape=SDS((PACK_ROWS, PACK_W), f32),
                          in_specs=[pl.BlockSpec(memory_space=pltpu.VMEM)],
                          out_specs=pl.BlockSpec(memory_space=pltpu.VMEM))(parts)


def _adamw_small(w, g, m, v):
    def body(w_ref, g_ref, m_ref, v_ref, d_ref, nm_ref, nv_ref):
        d_ref[...], nm_ref[...], nv_ref[...] = _adamw(w_ref[...], g_ref[...], m_ref[...], v_ref[...])

    vm = pl.BlockSpec(memory_space=pltpu.VMEM)
    return pl.pallas_call(body, name="adamw_small", out_shape=[SDS(w.shape, f32)] * 3,
                          in_specs=[vm] * 4, out_specs=[vm] * 3)(w, g, m, v)


def _pad_lanes(v, width):
    return jnp.pad(v, ((0, 0), (0, width - v.shape[1])))


def _local_step(x, tgt, norm_pre_w, wt, conv_w, conv_b, dt_bias, a_log, d_skip, ssm_norm_w, wo, norm_post_w,
                reduce_in_chip):
    dtb16 = _pad_lanes(dt_bias, LANES)
    alog16 = _pad_lanes(a_log, LANES)
    alog_f = jnp.repeat(a_log, HEAD_DIM, axis=1)
    d_f = jnp.repeat(d_skip, HEAD_DIM, axis=1)

    proj, u = _prenorm_inproj(x, norm_pre_w, wt)
    o, lb, mix_a = _attn_fwd(proj)
    mix_s, y, states = _ssd_fwd(proj, conv_w, conv_b, dtb16, alog16, alog_f, d_f, ssm_norm_w)
    dmix, dout, dres, acc_post = _outproj_loss(mix_a, mix_s, wo, x, tgt, norm_post_w)
    dq, dk, dv, dg = _attn_bwd(proj, o, lb, dmix)
    dzxd, g_conv, g_vec, g_dt = _ssd_bwd(proj, y, states, dmix, conv_w, conv_b, dtb16, alog16, alog_f, d_f, ssm_norm_w)
    dparts = [dq, dk, dv, dg, dzxd]
    dw_out = _matmul_tn([mix_a, mix_s], [dout], "dw_out")
    dw_in = _matmul_tn(dparts, [u], "dw_in")
    chip_sums, carry = reduce_in_chip(dw_in, dw_out)
    grad_x, g_pre, exchanged = _inproj_bwd(dparts, wt, x, norm_pre_w, dres, chip_sums)

    rows = [g_conv[0:5], _pad_lanes(g_pre[0:1], PACK_W), _pad_lanes(g_vec[0:1], PACK_W),
            _pad_lanes(acc_post[1:2], PACK_W), _pad_lanes(g_vec[1:3], PACK_W), _pad_lanes(g_dt[0:1], PACK_W),
            _pad_lanes(acc_post[0:1], PACK_W), jnp.zeros((4, PACK_W), f32)]
    return grad_x, carry, exchanged, jnp.concatenate(rows, axis=0)


def kernel(x, norm_pre_w, w_in, conv_w, conv_b, dt_bias, a_log, d_skip, ssm_norm_w, w_out, norm_post_w, loss_target, m_norm_pre_w, m_w_in, m_conv_w, m_conv_b, m_dt_bias, m_a_log, m_d_skip, m_ssm_norm_w, m_w_out, m_norm_post_w, v_norm_pre_w, v_w_in, v_conv_w, v_conv_b, v_dt_bias, v_a_log, v_d_skip, v_ssm_norm_w, v_w_out, v_norm_post_w):
    shard_in = w_in.shape[2]
    shard_cv = conv_w.shape[2]
    me = 4 * lax.axis_index("x") + 2 * lax.axis_index("y") + lax.axis_index("c")

    g_in, g_out, g_cw = _all_gather([w_in[0].T.astype(bf16), w_out[0].astype(bf16), conv_w[0]])
    wt = jnp.pad(g_in.reshape(N_DEV * shard_in, D_MODEL), ((0, NP - N_DEV * shard_in), (0, 0)))
    wo = g_out.reshape(N_DEV * w_out.shape[1], D_MODEL)
    cw = g_cw.transpose(1, 0, 2).reshape(4, D_CONV)

    def reduce_in_chip(dw_in, dw_out):
        send_in = dw_in[:N_DEV * shard_in].reshape(4, 2, shard_in, D_MODEL).transpose(1, 0, 2, 3)
        send_out = dw_out.reshape(4, 2, w_out.shape[1], D_MODEL).transpose(1, 0, 2, 3)
        got_in, got_out = _sibling_swap([send_in, send_out])
        chip_in, own_in = _chip_sum(send_in, got_in, "chip_sum_w_in")
        chip_out, own_out = _chip_sum(send_out, got_out, "chip_sum_w_out")
        return [chip_in, chip_out], (own_in, own_out)

    grad_x, (own_in, own_out), (parts_in, parts_out), pack = _local_step(
        x[0], loss_target[0], norm_pre_w, wt, cw, conv_b, dt_bias, a_log, d_skip, ssm_norm_w, wo, norm_post_w,
        reduce_in_chip)
    parts_small = _gather_small(pack)

    g_w_in, d_w_in, nm_w_in, nv_w_in = (a.T for a in _sum_adamw(
        own_in, parts_in, w_in[0].T, m_w_in[0].T, v_w_in[0].T, "sum_adamw_w_in"))
    g_w_out, d_w_out, nm_w_out, nv_w_out = _sum_adamw(own_out, parts_out, w_out[0], m_w_out[0], v_w_out[0], "sum_adamw_w_out")
    tot = _sum_small(parts_small)

    g_cw_all = tot[0:4]
    small_g = {
        "conv_w": lax.dynamic_slice(g_cw_all, (0, me * shard_cv), (4, shard_cv)),
        "conv_b": tot[4:5], "norm_pre_w": tot[5:6, :D_MODEL], "ssm_norm_w": tot[6:7, :D_SSM],
        "norm_post_w": tot[7:8, :D_MODEL], "a_log": tot[8:9, :16], "d_skip": tot[9:10, :16], "dt_bias": tot[10:11, :16],
    }
    loss = tot[12, 0]
    small_w = {"conv_w": (conv_w[0], m_conv_w[0], v_conv_w[0]), "conv_b": (conv_b, m_conv_b, v_conv_b),
               "norm_pre_w": (norm_pre_w, m_norm_pre_w, v_norm_pre_w), "ssm_norm_w": (ssm_norm_w, m_ssm_norm_w, v_ssm_norm_w),
               "norm_post_w": (norm_post_w, m_norm_post_w, v_norm_post_w), "a_log": (a_log, m_a_log, v_a_log),
               "d_skip": (d_skip, m_d_skip, v_d_skip), "dt_bias": (dt_bias, m_dt_bias, v_dt_bias)}
    names = list(small_w)
    sizes = [small_g[k].size for k in names]
    tot_size = sum(sizes)
    pad_to = -(-tot_size // 1024) * 1024

    def flat(arrs):
        v = jnp.concatenate([a.reshape(-1) for a in arrs])
        return jnp.pad(v, (0, pad_to - tot_size)).reshape(pad_to // LANES, LANES)

    fw = flat([small_w[k][0] for k in names])
    fg = flat([small_g[k] for k in names])
    fm = flat([small_w[k][1] for k in names])
    fv = jnp.pad(jnp.concatenate([small_w[k][2].reshape(-1) for k in names]), (0, pad_to - tot_size),
                 constant_values=1.0).reshape(pad_to // LANES, LANES)
    fd, fnm, fnv = _adamw_small(fw, fg, fm, fv)

    def unflat(f):
        out, off = {}, 0
        v = f.reshape(-1)
        for k, n in zip(names, sizes):
            out[k] = v[off:off + n].reshape(small_g[k].shape)
            off += n
        return out

    sd, snm, snv = unflat(fd), unflat(fnm), unflat(fnv)
    lead = lambda a: a[None]
    order = ["norm_pre_w", "w_in", "conv_w", "conv_b", "dt_bias", "a_log", "d_skip", "ssm_norm_w", "w_out", "norm_post_w"]
    grads = dict(small_g, w_in=g_w_in, w_out=g_w_out)
    deltas = dict(sd, w_in=d_w_in, w_out=d_w_out)
    new_m = dict(snm, w_in=nm_w_in, w_out=nm_w_out)
    new_v = dict(snv, w_in=nv_w_in, w_out=nv_w_out)

    def shaped(dct, k):
        a = dct[k]
        return lead(a) if k in ("w_in", "w_out", "conv_w") else a

    return (loss, grad_x[None], *[shaped(grads, k) for k in order], *[shaped(deltas, k) for k in order],
            *[shaped(new_m, k) for k in order], *[shaped(new_v, k) for k in order])
```

```python
import functools
import math

import jax
import jax.numpy as jnp
import numpy as np
from jax import lax
from jax.experimental import pallas as pl
from jax.experimental.pallas import tpu as pltpu

f32, bf16 = jnp.float32, jnp.bfloat16
SDS = jax.ShapeDtypeStruct
HIGHEST = lax.Precision.HIGHEST
MESH = pl.DeviceIdType.MESH

N_DEV = 8
D_MODEL = 1024
D_ATTN = 1024
D_SSM = 1024
HEAD_DIM = 64
N_PAIRS = 8
D_STATE = 128
N_GROUPS = 2
D_CONV = D_SSM + 2 * N_GROUPS * D_STATE
D_IN_PROJ = 4 * D_ATTN + D_SSM + D_CONV + 16
NP = 7168
CHUNK = 128
BLK = 128
DILATIONS = (1, 4, 16)
EPS = 1e-6
LANES = 128
COL_Z, COL_XS, COL_BC, COL_DT = 4096, 5120, 6144, 6656

ADAM_LR, ADAM_B1, ADAM_B2, ADAM_EPS, ADAM_WD, ADAM_STEP = 0.001, 0.9, 0.999, 1e-08, 0.01, 10

PACK_ROWS, PACK_W = 16, 1536


def _nt(a, b):
    return lax.dot_general(a, b, (((1,), (1,)), ((), ())), preferred_element_type=f32)


def _tn(a, b):
    return lax.dot_general(a, b, (((0,), (0,)), ((), ())), preferred_element_type=f32)


def _nn(a, b):
    return jnp.dot(a, b, preferred_element_type=f32)


def _nn_hi(a, b):
    return jnp.dot(a, b, precision=HIGHEST, preferred_element_type=f32)


def _sigmoid(x):
    return 1.0 / (1.0 + jnp.exp(-x))


def _softplus(x):
    return jnp.maximum(x, 0.0) + jnp.log1p(jnp.exp(-jnp.abs(x)))


def _iota(shape, dim):
    return lax.broadcasted_iota(jnp.int32, shape, dim)


def _my_pos():
    return lax.axis_index("x"), lax.axis_index("y"), lax.axis_index("c")


def _all_gather(arrs):
    n = len(arrs)

    def body(*refs):
        ins, outs = refs[:n], refs[n:2 * n]
        send_sems, recv_sems, local_sems = refs[2 * n:]
        x, y, c = _my_pos()
        me, sibling = (x, y, c), (x, y, 1 - c)
        chips = [(1 - x, y), (x, 1 - y), (1 - x, 1 - y)]

        def slot(a, px, py, pc):
            return outs[a].at[4 * px + 2 * py + pc]

        def copy(a, k, block, to, src=None):
            return pltpu.make_async_remote_copy(
                src_ref=slot(a, *block) if src is None else src, dst_ref=slot(a, *block),
                send_sem=send_sems.at[7 * a + k], recv_sem=recv_sems.at[7 * a + k],
                device_id=to, device_id_type=MESH)

        mine = [pltpu.make_async_copy(ins[a], slot(a, *me), local_sems.at[a]) for a in range(n)]
        for cp in mine:
            cp.start()
        first = []
        for a in range(n):
            first.append(copy(a, 0, me, sibling, src=ins[a]))
            first += [copy(a, 1 + j, me, (*chip, c), src=ins[a]) for j, chip in enumerate(chips)]
        for cp in first:
            cp.start()
        passed = []
        for j, chip in enumerate(chips):
            for a in range(n):
                copy(a, 1 + j, (*chip, c), me).wait_recv()
                cp = copy(a, 4 + j, (*chip, c), sibling)
                cp.start()
                passed.append(cp)
        for a in range(n):
            copy(a, 0, sibling, me).wait_recv()
            for j, chip in enumerate(chips):
                copy(a, 4 + j, (*chip, 1 - c), me).wait_recv()
        for cp in first + passed:
            cp.wait_send()
        for cp in mine:
            cp.wait()

    anyspec = pl.BlockSpec(memory_space=pl.ANY)
    return pl.pallas_call(
        body, name="weights_all_gather",
        out_shape=[SDS((N_DEV,) + a.shape, a.dtype) for a in arrs],
        in_specs=[anyspec] * n, out_specs=[anyspec] * n,
        scratch_shapes=[pltpu.SemaphoreType.DMA((7 * n,)), pltpu.SemaphoreType.DMA((7 * n,)),
                        pltpu.SemaphoreType.DMA((n,))],
    )(*arrs)


def _sibling_swap(bigs):
    nb = len(bigs)

    def body(*refs):
        ins, outs = refs[:nb], refs[nb:2 * nb]
        send_sems, recv_sems = refs[2 * nb:]
        x, y, c = _my_pos()
        sends = []
        for a in range(nb):
            cp = pltpu.make_async_remote_copy(
                src_ref=ins[a].at[1 - c], dst_ref=outs[a], send_sem=send_sems.at[a], recv_sem=recv_sems.at[a],
                device_id=(x, y, 1 - c), device_id_type=MESH)
            cp.start()
            sends.append(cp)
        for a in range(nb):
            pltpu.make_async_remote_copy(
                src_ref=ins[a].at[c], dst_ref=outs[a], send_sem=send_sems.at[a], recv_sem=recv_sems.at[a],
                device_id=(x, y, c), device_id_type=MESH).wait_recv()
        for cp in sends:
            cp.wait_send()

    anyspec = pl.BlockSpec(memory_space=pl.ANY)
    return pl.pallas_call(
        body, name="grad_sibling_swap", out_shape=[SDS(a.shape[1:], a.dtype) for a in bigs],
        in_specs=[anyspec] * nb, out_specs=[anyspec] * nb,
        scratch_shapes=[pltpu.SemaphoreType.DMA((nb,)), pltpu.SemaphoreType.DMA((nb,))],
    )(*bigs)


def _gather_small(small):
    def body(small_in, small_out, send_sems, recv_sems, local_sem):
        x, y, c = _my_pos()
        me = 4 * x + 2 * y + c
        mine = pltpu.make_async_copy(small_in, small_out.at[me], local_sem)
        mine.start()
        sends = []
        for k in range(1, N_DEV):
            to = (me + k) % N_DEV
            cp = pltpu.make_async_remote_copy(
                src_ref=small_in, dst_ref=small_out.at[me], send_sem=send_sems.at[k - 1], recv_sem=recv_sems.at[k - 1],
                device_id=(to // 4, (to // 2) % 2, to % 2), device_id_type=MESH)
            cp.start()
            sends.append(cp)
        for k in range(1, N_DEV):
            frm = (me + N_DEV - k) % N_DEV
            pltpu.make_async_remote_copy(
                src_ref=small_in, dst_ref=small_out.at[frm], send_sem=send_sems.at[k - 1], recv_sem=recv_sems.at[k - 1],
                device_id=(x, y, c), device_id_type=MESH).wait_recv()
        for cp in sends:
            cp.wait_send()
        mine.wait()

    anyspec = pl.BlockSpec(memory_space=pl.ANY)
    return pl.pallas_call(
        body, name="small_grads_gather", out_shape=SDS((N_DEV,) + small.shape, small.dtype),
        in_specs=[anyspec], out_specs=anyspec,
        scratch_shapes=[pltpu.SemaphoreType.DMA((7,)), pltpu.SemaphoreType.DMA((7,)), pltpu.SemaphoreType.DMA(())],
    )(small)


def _chip_sum(mine, got, name):
    _, nq, r, cdim = mine.shape
    tc = 256

    def body(m_ref, g_ref, s16_ref, own_ref):
        q = pl.program_id(1)
        c = lax.axis_index("c")
        my_q = 2 * lax.axis_index("x") + lax.axis_index("y")
        tot = m_ref[c] + g_ref[...]
        s16_ref[...] = tot.astype(bf16)

        @pl.when(q == my_q)
        def _():
            own_ref[...] = tot

    return pl.pallas_call(
        body, name=name, grid=(cdim // tc, nq),
        in_specs=[pl.BlockSpec((2, None, r, tc), lambda i, q: (0, q, 0, i)),
                  pl.BlockSpec((None, r, tc), lambda i, q: (q, 0, i))],
        out_specs=[pl.BlockSpec((None, r, tc), lambda i, q: (q, 0, i)), pl.BlockSpec((r, tc), lambda i, q: (0, i))],
        out_shape=[SDS((nq, r, cdim), bf16), SDS((r, cdim), f32)],
        compiler_params=pltpu.CompilerParams(dimension_semantics=("parallel", "arbitrary")),
    )(mine, got)


def _chip_exchange_copies(ins, outs, send_sems, recv_sems, local_sems):
    nb = len(ins)
    x, y, c = _my_pos()
    my_q = 2 * x + y
    mine = [pltpu.make_async_copy(ins[a].at[my_q], outs[a].at[my_q], local_sems.at[a]) for a in range(nb)]
    sends, recvs = [], []
    for k in range(1, 4):
        to, frm = (my_q + k) % 4, (my_q + 4 - k) % 4
        for a in range(nb):
            sems = dict(send_sem=send_sems.at[3 * a + k - 1], recv_sem=recv_sems.at[3 * a + k - 1], device_id_type=MESH)
            sends.append(pltpu.make_async_remote_copy(
                src_ref=ins[a].at[to], dst_ref=outs[a].at[my_q], device_id=(to // 2, to % 2, c), **sems))
            recvs.append(pltpu.make_async_remote_copy(
                src_ref=ins[a].at[frm], dst_ref=outs[a].at[frm], device_id=(x, y, c), **sems))
    return mine, sends, recvs


def _chip_exchange_scratch(nb):
    return [pltpu.SemaphoreType.DMA((3 * nb,)), pltpu.SemaphoreType.DMA((3 * nb,)), pltpu.SemaphoreType.DMA((nb,))]


def _prenorm_inproj(x, nw, wt):
    s, d = x.shape
    npad = wt.shape[0]
    tm, tn = 1024, 1024

    def body(x_ref, nw_ref, w_ref, proj_ref, u_ref):
        @pl.when(pl.program_id(1) == 0)
        def _():
            xv = x_ref[...]
            r = lax.rsqrt(jnp.mean(xv * xv, axis=-1, keepdims=True) + EPS)
            u_ref[...] = (xv * r * nw_ref[...]).astype(bf16)
        proj_ref[...] = _nt(u_ref[...], w_ref[...])

    return pl.pallas_call(
        body, name="prenorm_inproj", grid=(s // tm, npad // tn),
        in_specs=[pl.BlockSpec((tm, d), lambda i, j: (i, 0)), pl.BlockSpec((1, d), lambda i, j: (0, 0)),
                  pl.BlockSpec((tn, d), lambda i, j: (j, 0))],
        out_specs=[pl.BlockSpec((tm, tn), lambda i, j: (i, j)), pl.BlockSpec((tm, d), lambda i, j: (i, 0))],
        out_shape=[SDS((s, npad), f32), SDS((s, d), bf16)],
        compiler_params=pltpu.CompilerParams(dimension_semantics=("parallel", "arbitrary")),
    )(x, nw, wt)


def _attn_consts():
    head0 = _iota((BLK, LANES), 1) < HEAD_DIM
    tri2 = (_iota((BLK, 2 * LANES), 1) % LANES) <= _iota((BLK, 2 * LANES), 0)
    ones2 = ((_iota((LANES, 2 * LANES), 0) < HEAD_DIM) == (_iota((LANES, 2 * LANES), 1) < LANES)).astype(bf16)
    rmat = ((_iota((2 * LANES, LANES), 0) < LANES) == (_iota((2 * LANES, LANES), 1) < HEAD_DIM)).astype(bf16)
    bones = ((_iota((LANES, LANES), 0) < HEAD_DIM) == (_iota((LANES, LANES), 1) < HEAD_DIM)).astype(bf16)
    return head0, tri2, ones2, rmat, bones


def _stack_heads(x16, head0):
    zero = jnp.zeros_like(x16)
    return jnp.concatenate([jnp.where(head0, x16, zero), jnp.where(head0, zero, x16)], axis=0)


def _split_dot(x, w16):
    hi = x.astype(bf16)
    lo = (x - hi.astype(f32)).astype(bf16)
    return _nn(hi, w16) + _nn(lo, w16)


def _bf16_terms(x, terms):
    out = []
    for _ in range(terms):
        t = x.astype(bf16)
        out.append(t)
        x = x - t.astype(f32)
    return out


def _dot_01(x, w16, terms):
    return sum(_nn(t, w16) for t in _bf16_terms(x, terms))


def _dot_01_left(w16, x, terms):
    return sum(_nn(w16, t) for t in _bf16_terms(x, terms))


def _attn_fwd(proj):
    s = proj.shape[0]
    n_it = s // BLK

    def body(q_ref, k_ref, v_ref, g_ref, o_ref, l_ref, mix_ref, op0, op1, op2, lp0, lp1, lp2,
             s_a, s_b, sd_a, sd_b, p_a, p_b, m_a, m_b, pd_a, pd_b):
        op_refs, lp_refs = (op0, op1, op2), (lp0, lp1, lp2)
        head0, tri2, ones2, rmat, _ = _attn_consts()
        score_bufs, prob_bufs = ((s_a, sd_a), (s_b, sd_b)), ((p_a, m_a, pd_a), (p_b, m_b, pd_b))

        def block_rows(i, d, nb):
            r, blk = i // nb, i % nb
            st = blk * (BLK * d) + r
            stp = jnp.maximum(blk - 1, 0) * (BLK * d) + r
            return pl.ds(st, BLK, stride=d), pl.ds(stp, BLK, stride=d), blk > 0

        def scores(i, d, nb, bufs):
            rows, rows_p, has_prev = block_rows(i, d, nb)
            s_buf, sd_buf = bufs
            qs = q_ref[rows, :] * 0.125
            kc, kp = k_ref[rows, :], k_ref[rows_p, :]
            qs16 = qs.astype(bf16)
            sc = _nt(qs16, _stack_heads(kc.astype(bf16), head0))
            sp = _nt(qs16, _stack_heads(kp.astype(bf16), head0))
            s_buf[...] = jnp.where(tri2, sc, jnp.where(has_prev, sp, -jnp.inf))
            sd_buf[...] = jnp.where(has_prev, _split_dot(qs * kp, ones2), -jnp.inf)

        def softmax(bufs_in, bufs_out):
            s_buf, sd_buf = bufs_in
            p_buf, m_buf, pd_buf = bufs_out
            sc, sd2 = s_buf[...], sd_buf[...]
            m0 = jnp.max(sc[:, :LANES], axis=1, keepdims=True)
            m1 = jnp.max(sc[:, LANES:], axis=1, keepdims=True)
            m2 = jnp.concatenate([jnp.broadcast_to(m0, (BLK, LANES)), jnp.broadcast_to(m1, (BLK, LANES))], axis=1)
            m2 = jnp.maximum(m2, sd2)
            p_buf[...] = jnp.exp(sc - m2).astype(bf16)
            m_pair = jnp.where(head0, m2[:, :LANES], m2[:, LANES:])
            m_buf[...] = m_pair
            pd_buf[...] = jnp.exp(jnp.where(head0, sd2[:, :LANES], sd2[:, LANES:]) - m_pair)

        def output(i, d, nb, p, bufs):
            rows, rows_p, _ = block_rows(i, d, nb)
            p_buf, m_buf, pd_buf = bufs
            vc, vp = v_ref[rows, :], v_ref[rows_p, :]
            pt16, pd = p_buf[...], pd_buf[...]
            zero = jnp.zeros_like(pt16)
            o = (_nn(jnp.where(tri2, pt16, zero), _stack_heads(vc.astype(bf16), head0))
                 + _nn(jnp.where(tri2, zero, pt16), _stack_heads(vp.astype(bf16), head0)) + pd * vp)
            l = _nn(pt16, rmat) + pd
            op_refs[p][rows, :] = o / l
            lp_refs[p][rows, :] = m_buf[...] + jnp.log(l)

        for p, d in enumerate(DILATIONS):
            nb = s // (BLK * d)
            scores(0, d, nb, score_bufs[0])
            scores(1, d, nb, score_bufs[1])
            softmax(score_bufs[0], prob_bufs[0])

            def steps(j, carry, d=d, nb=nb, p=p):
                for par in range(2):
                    t = 2 * j + 2 + par
                    scores(t, d, nb, score_bufs[par])
                    output(t - 2, d, nb, p, prob_bufs[par])
                    softmax(score_bufs[1 - par], prob_bufs[1 - par])
                return carry

            lax.fori_loop(0, (n_it - 2) // 2, steps, 0)
            output(n_it - 2, d, nb, p, prob_bufs[0])
            softmax(score_bufs[1], prob_bufs[1])
            output(n_it - 1, d, nb, p, prob_bufs[1])

        def merge(i, carry):
            rows = pl.ds(pl.multiple_of(i * 256, 256), 256)
            l0, l1, l2 = lp0[rows, :], lp1[rows, :], lp2[rows, :]
            m = jnp.maximum(jnp.maximum(l0, l1), l2)
            e0, e1, e2 = jnp.exp(l0 - m), jnp.exp(l1 - m), jnp.exp(l2 - m)
            z = e0 + e1 + e2
            o = (e0 * op0[rows, :] + e1 * op1[rows, :] + e2 * op2[rows, :]) / z
            o_ref[rows, :] = o
            l_ref[rows, :] = m + jnp.log(z)
            g = g_ref[rows, :]
            mix_ref[rows, :] = (o * (g * _sigmoid(g))).astype(bf16)
            return carry

        lax.fori_loop(0, s // 256, merge, 0)

    col = lambda base: pl.BlockSpec((s, LANES), lambda h: (0, base + h))
    return pl.pallas_call(
        body, name="attn_fwd", grid=(N_PAIRS,),
        in_specs=[col(0), col(8), col(16), col(24)],
        out_specs=[col(0), col(0), col(0)],
        out_shape=[SDS((s, D_ATTN), f32), SDS((s, D_ATTN), f32), SDS((s, D_ATTN), bf16)],
        scratch_shapes=[pltpu.VMEM((s, LANES), f32)] * 6 + [pltpu.VMEM((BLK, 2 * LANES), f32)] * 4
        + [pltpu.VMEM((BLK, 2 * LANES), bf16)] * 2 + [pltpu.VMEM((BLK, LANES), f32)] * 4,
        compiler_params=pltpu.CompilerParams(dimension_semantics=("parallel",)),
    )(proj, proj, proj, proj)


def _expand_mat():
    colv = np.arange(2 * D_SSM)
    head = 2 * ((colv % D_SSM) // LANES) + colv // D_SSM
    return jnp.asarray(np.arange(LANES)[:, None] == head[None, :], dtype=bf16)


def _fold_mat():
    return jnp.asarray((np.arange(D_SSM) // HEAD_DIM)[:, None] == np.arange(LANES)[None, :], dtype=bf16)


def _ssd_common(xs_ref, bc_ref, xs_tail, bc_tail, dt_ref, cw_ref, cb_ref, dtb_ref, alog16_ref, emat_ref, xpad, first):
    keep = jnp.where(first, 0.0, 1.0)
    xpad[0:8, 0:D_SSM] = xs_tail[...] * keep
    xpad[0:8, D_SSM:D_CONV] = bc_tail[...] * keep
    xpad[8:8 + CHUNK, 0:D_SSM] = xs_ref[...]
    xpad[8:8 + CHUNK, D_SSM:D_CONV] = bc_ref[...]
    xp = xpad[...]
    taps = [pltpu.roll(xp, 3 - j, 0)[8:8 + CHUNK] for j in range(3)] + [xp[8:8 + CHUNK]]
    cv = cb_ref[...] + cw_ref[0:1, :] * taps[0]
    for j in range(1, 4):
        cv = cv + cw_ref[j:j + 1, :] * taps[j]
    sig = _sigmoid(cv)
    xbc = cv * sig

    pre = dt_ref[...] + dtb_ref[...]
    dt16 = _softplus(pre)
    a16 = -jnp.exp(alog16_ref[...])
    sub, lane = _iota((CHUNK, CHUNK), 0), _iota((CHUNK, CHUNK), 1)
    tri = (sub >= lane).astype(f32)
    al16 = _nn_hi(tri, dt16 * a16)
    al_t = al16.T
    emat = emat_ref[...]
    dt_x = _dot_01(dt16, emat, 3)
    al_x = _dot_01(al16, emat, 3)
    lane_w = _iota((CHUNK, D_SSM), 1)
    even = (lane_w % LANES) < HEAD_DIM
    dt_f = jnp.where(even, dt_x[:, :D_SSM], dt_x[:, D_SSM:])
    al_f = jnp.where(even, al_x[:, :D_SSM], al_x[:, D_SSM:])
    return cv, sig, xbc, pre, dt_f, al_f, al_x, al_t, taps


def _decay_mat(al_x, al_t, pair, h):
    sub, lane = _iota((CHUNK, CHUNK), 0), _iota((CHUNK, CHUNK), 1)
    col = al_x[:, h * D_SSM + pair * LANES: h * D_SSM + (pair + 1) * LANES]
    row = al_t[2 * pair + h: 2 * pair + h + 1, :]
    return jnp.exp(jnp.where(sub >= lane, col - row, -jnp.inf))


def _ssd_in_specs(order):
    blk = lambda w, cb: pl.BlockSpec((CHUNK, w), lambda i: (order(i), cb))
    tail = lambda w, cb: pl.BlockSpec((8, w), lambda i: (jnp.maximum(16 * order(i) - 1, 0), cb))
    return [blk(D_SSM, COL_XS // D_SSM), blk(512, COL_BC // 512), tail(D_SSM, COL_XS // D_SSM),
            tail(512, COL_BC // 512), blk(LANES, COL_DT // LANES), blk(D_SSM, COL_Z // D_SSM)]


def _full(shape):
    return pl.BlockSpec(shape, lambda i: (0,) * len(shape))


def _ssd_fwd(proj, conv_w, conv_b, dtb16, alog16, alog_f, d_f, nw):
    s = proj.shape[0]
    nc = s // CHUNK

    def body(xs_ref, bc_ref, xs_tail, bc_tail, dt_ref, z_ref, cw_ref, cb_ref, dtb_ref, alog16_ref, alogf_ref,
             df_ref, nw_ref, emat_ref, mix_ref, y_ref, st_ref, h_scr, xpad, y_scr):
        c = pl.program_id(0)

        @pl.when(c == 0)
        def _():
            h_scr[...] = jnp.zeros_like(h_scr)

        _, _, xbc, _, dt_f, al_f, al_x, al_t, _ = _ssd_common(
            xs_ref, bc_ref, xs_tail, bc_tail, dt_ref, cw_ref, cb_ref, dtb_ref, alog16_ref, emat_ref, xpad, c == 0)
        head0 = _iota((CHUNK, LANES), 1) < HEAD_DIM
        st_ref[...] = h_scr[...]
        for g in range(N_GROUPS):
            bm = xbc[:, D_SSM + g * D_STATE: D_SSM + (g + 1) * D_STATE].astype(bf16)
            cm = xbc[:, D_SSM + (N_GROUPS + g) * D_STATE: D_SSM + (N_GROUPS + g + 1) * D_STATE].astype(bf16)
            gmat = _nt(cm, bm)
            for pair in range(4 * g, 4 * g + 4):
                sl = slice(pair * LANES, (pair + 1) * LANES)
                xp, dtp, alp = xbc[:, sl], dt_f[:, sl], al_f[:, sl]
                xdt = xp * dtp
                xdt16 = xdt.astype(bf16)
                al_last = alp[CHUNK - 1:CHUNK, :]
                hp = h_scr[:, sl]
                y_off = jnp.exp(alp) * _nn(cm, hp.astype(bf16))
                yd = [_nn((gmat * _decay_mat(al_x, al_t, pair, h)).astype(bf16), xdt16) for h in range(2)]
                y_scr[:, sl] = jnp.where(head0, yd[0], yd[1]) + y_off + df_ref[:, sl] * xp
                st = _tn(bm, (jnp.exp(al_last - alp) * xdt).astype(bf16))
                h_scr[:, sl] = jnp.exp(al_last) * hp + st
        y = y_scr[...]
        y_ref[...] = y
        z = z_ref[...]
        yz = y * (z * _sigmoid(z))
        gw = D_SSM // N_GROUPS
        for g in range(N_GROUPS):
            part = yz[:, g * gw:(g + 1) * gw]
            r = lax.rsqrt(jnp.mean(part * part, axis=-1, keepdims=True) + EPS)
            mix_ref[:, g * gw:(g + 1) * gw] = (part * r * nw_ref[:, g * gw:(g + 1) * gw]).astype(bf16)

    order = lambda i: i
    row = lambda w: pl.BlockSpec((CHUNK, w), lambda i: (i, 0))
    return pl.pallas_call(
        body, name="ssd_fwd", grid=(nc,),
        in_specs=_ssd_in_specs(order) + [_full((4, D_CONV)), _full((1, D_CONV)), _full((1, LANES)), _full((1, LANES)),
                                         _full((1, D_SSM)), _full((1, D_SSM)), _full((1, D_SSM)),
                                         _full((LANES, 2 * D_SSM))],
        out_specs=[row(D_SSM), row(D_SSM), pl.BlockSpec((None, D_STATE, D_SSM), lambda i: (i, 0, 0))],
        out_shape=[SDS((s, D_SSM), bf16), SDS((s, D_SSM), f32), SDS((nc, D_STATE, D_SSM), f32)],
        scratch_shapes=[pltpu.VMEM((D_STATE, D_SSM), f32), pltpu.VMEM((8 + CHUNK, D_CONV), f32),
                        pltpu.VMEM((CHUNK, D_SSM), f32)],
        compiler_params=pltpu.CompilerParams(dimension_semantics=("arbitrary",)),
    )(proj, proj, proj, proj, proj, proj, conv_w, conv_b, dtb16, alog16, alog_f, d_f, nw, _expand_mat())


def _outproj_loss(mix_a, mix_s, wo, x, tgt, npw):
    s, d = x.shape
    tm = 512

    def body(ma_ref, ms_ref, wo_ref, x_ref, t_ref, npw_ref, dmix_ref, dout_ref, dres_ref, acc_ref):
        @pl.when(pl.program_id(0) == 0)
        def _():
            acc_ref[...] = jnp.zeros_like(acc_ref)

        out = _nn(ma_ref[...], wo_ref[0:D_ATTN, :]) + _nn(ms_ref[...], wo_ref[D_ATTN:, :])
        r = lax.rsqrt(jnp.mean(out * out, axis=-1, keepdims=True) + EPS)
        on = out * r
        diff = x_ref[...] + on * npw_ref[...] - t_ref[...]
        dres = diff * (1.0 / d)
        dres_ref[...] = dres
        acc_ref[0:1, :] += jnp.sum(diff * diff, axis=0, keepdims=True)
        acc_ref[1:2, :] += jnp.sum(dres * on, axis=0, keepdims=True)
        dn = dres * npw_ref[...]
        dout = (r * (dn - on * jnp.mean(dn * on, axis=-1, keepdims=True))).astype(bf16)
        dout_ref[...] = dout
        dmix_ref[...] = _nt(dout, wo_ref[...])

    row = lambda w: pl.BlockSpec((tm, w), lambda i: (i, 0))
    return pl.pallas_call(
        body, name="outproj_loss", grid=(s // tm,),
        in_specs=[row(D_ATTN), row(D_SSM), _full((D_ATTN + D_SSM, d)), row(d), row(d), _full((1, d))],
        out_specs=[row(D_ATTN + D_SSM), row(d), row(d), _full((8, d))],
        out_shape=[SDS((s, D_ATTN + D_SSM), f32), SDS((s, d), bf16), SDS((s, d), f32), SDS((8, d), f32)],
        compiler_params=pltpu.CompilerParams(dimension_semantics=("arbitrary",)),
    )(mix_a, mix_s, wo, x, tgt, npw)


def _attn_bwd(proj, o, lb, dmix):
    s = proj.shape[0]
    n_it = s // BLK

    def body(q_ref, k_ref, v_ref, g_ref, o_ref, l_ref, dm_ref, dq_ref, dk_ref, dv_ref, dg_ref,
             dq_acc, dk_acc, dv_acc, do_scr, dl_scr, *bufs):
        head0, tri2, _, _, bones = _attn_consts()

        def pro(i, carry):
            rows = pl.ds(pl.multiple_of(i * 256, 256), 256)
            g = g_ref[rows, :]
            sg = _sigmoid(g)
            dmx = dm_ref[rows, :]
            ov = o_ref[rows, :]
            dg_ref[rows, :] = (dmx * ov * (sg * (1.0 + g * (1.0 - sg)))).astype(bf16)
            do = dmx * (g * sg)
            do_scr[rows, :] = do
            dl_scr[rows, :] = _split_dot(do * ov, bones)
            z = jnp.zeros((256, LANES), f32)
            dq_acc[rows, :] = z
            dk_acc[rows, :] = z
            dv_acc[rows, :] = z
            return carry

        lax.fori_loop(0, s // 256, pro, 0)

        def per_head(t):
            return jnp.concatenate([t[:, :LANES], t[:, LANES:]], axis=0)

        def both_heads(t):
            tr = pltpu.roll(t, HEAD_DIM, 1)
            return jnp.concatenate([jnp.where(head0, t, tr), jnp.where(head0, tr, t)], axis=1)

        mm_bufs = ((bufs[0], bufs[1], bufs[2], bufs[3]), (bufs[4], bufs[5], bufs[6], bufs[7]))
        ds_bufs = ((bufs[8], bufs[9], bufs[10], bufs[11]), (bufs[12], bufs[13], bufs[14], bufs[15]))
        op_bufs = ((bufs[16], bufs[17], bufs[18], bufs[19]), (bufs[20], bufs[21], bufs[22], bufs[23]))

        def block_rows(i, d, nb):
            r, blk = i // nb, i % nb
            st = blk * (BLK * d) + r
            stp = jnp.maximum(blk - 1, 0) * (BLK * d) + r
            return pl.ds(st, BLK, stride=d), pl.ds(stp, BLK, stride=d), blk > 0

        def products(i, d, nb, out, ops):
            rows, rows_p, has_prev = block_rows(i, d, nb)
            s_buf, dp_buf, sd_buf, dpd_buf = out
            kc_buf, kp_buf, q_buf, do_buf = ops
            q = q_ref[rows, :]
            qs = q * 0.125
            kc, kp = k_ref[rows, :], k_ref[rows_p, :]
            vc, vp = v_ref[rows, :], v_ref[rows_p, :]
            do = do_scr[rows, :]
            qs16, do16 = qs.astype(bf16), do.astype(bf16)
            kst_c, kst_p = _stack_heads(kc.astype(bf16), head0), _stack_heads(kp.astype(bf16), head0)
            vst_c, vst_p = _stack_heads(vc.astype(bf16), head0), _stack_heads(vp.astype(bf16), head0)
            kc_buf[...] = kst_c
            kp_buf[...] = kst_p
            q_buf[...] = q.astype(bf16)
            do_buf[...] = do16
            s_buf[...] = jnp.where(tri2, _nt(qs16, kst_c), jnp.where(has_prev, _nt(qs16, kst_p), -jnp.inf))
            dp_buf[...] = jnp.where(tri2, _nt(do16, vst_c), _nt(do16, vst_p))
            sd_buf[...] = _split_dot(qs * kp, bones)
            dpd_buf[...] = _split_dot(do * vp, bones)

        def softmax_grad(i, d, nb, inp, out):
            rows, _, has_prev = block_rows(i, d, nb)
            s_buf, dp_buf, sd_buf, dpd_buf = inp
            p_buf, ds_buf, pd_buf, dsd_buf = out
            lse = l_ref[rows, :]
            dl = dl_scr[rows, :]
            pt = jnp.exp(s_buf[...] - both_heads(lse))
            ds_buf[...] = (pt * (dp_buf[...] - both_heads(dl)) * 0.125).astype(bf16)
            p_buf[...] = pt.astype(bf16)
            pd = jnp.where(has_prev, jnp.exp(sd_buf[...] - lse), 0.0)
            pd_buf[...] = pd
            dsd_buf[...] = pd * (dpd_buf[...] - dl) * 0.125

        def accumulate(i, d, nb, inp, ops):
            rows, rows_p, _ = block_rows(i, d, nb)
            p_buf, ds_buf, pd_buf, dsd_buf = inp
            kc_buf, kp_buf, q_buf, do_buf = ops
            pt16, ds16, pd, dsd = p_buf[...], ds_buf[...], pd_buf[...], dsd_buf[...]
            zero = jnp.zeros_like(pt16)
            dsc, dsp = jnp.where(tri2, ds16, zero), jnp.where(tri2, zero, ds16)
            pc, pp = jnp.where(tri2, pt16, zero), jnp.where(tri2, zero, pt16)
            kst_c, kst_p, q16, do16 = kc_buf[...], kp_buf[...], q_buf[...], do_buf[...]
            kp16 = kst_p[:BLK] + kst_p[BLK:]
            qst, dost = _stack_heads(q16, head0), _stack_heads(do16, head0)
            dq_acc[rows, :] += _nn(dsc, kst_c) + _nn(dsp, kst_p) + dsd * kp16.astype(f32)
            dk_acc[rows, :] += _tn(per_head(dsc), qst)
            dv_acc[rows, :] += _tn(per_head(pc), dost)
            dk_acc[rows_p, :] += _tn(per_head(dsp), qst) + dsd * q16.astype(f32)
            dv_acc[rows_p, :] += _tn(per_head(pp), dost) + pd * do16.astype(f32)

        for d in DILATIONS:
            nb = s // (BLK * d)
            products(0, d, nb, mm_bufs[0], op_bufs[0])
            products(1, d, nb, mm_bufs[1], op_bufs[1])
            softmax_grad(0, d, nb, mm_bufs[0], ds_bufs[0])

            def steps(j, carry, d=d, nb=nb):
                for par in range(2):
                    t = 2 * j + 2 + par
                    accumulate(t - 2, d, nb, ds_bufs[par], op_bufs[par])
                    products(t, d, nb, mm_bufs[par], op_bufs[par])
                    softmax_grad(t - 1, d, nb, mm_bufs[1 - par], ds_bufs[1 - par])
                return carry

            lax.fori_loop(0, (n_it - 2) // 2, steps, 0)
            accumulate(n_it - 2, d, nb, ds_bufs[0], op_bufs[0])
            softmax_grad(n_it - 1, d, nb, mm_bufs[1], ds_bufs[1])
            accumulate(n_it - 1, d, nb, ds_bufs[1], op_bufs[1])

        def epi(i, carry):
            rows = pl.ds(pl.multiple_of(i * 256, 256), 256)
            dq_ref[rows, :] = dq_acc[rows, :].astype(bf16)
            dk_ref[rows, :] = dk_acc[rows, :].astype(bf16)
            dv_ref[rows, :] = dv_acc[rows, :].astype(bf16)
            return carry

        lax.fori_loop(0, s // 256, epi, 0)

    col = lambda base: pl.BlockSpec((s, LANES), lambda h: (0, base + h))
    outs = pl.pallas_call(
        body, name="attn_bwd", grid=(N_PAIRS,),
        in_specs=[col(0), col(8), col(16), col(24), col(0), col(0), col(0)],
        out_specs=[col(0)] * 4,
        out_shape=[SDS((s, D_ATTN), bf16)] * 4,
        scratch_shapes=[pltpu.VMEM((s, LANES), f32)] * 5
        + [pltpu.VMEM((BLK, 2 * LANES), f32)] * 2 + [pltpu.VMEM((BLK, LANES), f32)] * 2
        + [pltpu.VMEM((BLK, 2 * LANES), f32)] * 2 + [pltpu.VMEM((BLK, LANES), f32)] * 2
        + [pltpu.VMEM((BLK, 2 * LANES), bf16)] * 2 + [pltpu.VMEM((BLK, LANES), f32)] * 2
        + [pltpu.VMEM((BLK, 2 * LANES), bf16)] * 2 + [pltpu.VMEM((BLK, LANES), f32)] * 2
        + [pltpu.VMEM((2 * BLK, LANES), bf16)] * 2 + [pltpu.VMEM((BLK, LANES), bf16)] * 2
        + [pltpu.VMEM((2 * BLK, LANES), bf16)] * 2 + [pltpu.VMEM((BLK, LANES), bf16)] * 2,
        compiler_params=pltpu.CompilerParams(dimension_semantics=("parallel",)),
    )(proj, proj, proj, proj, o, lb, dmix)
    return outs


def _ssd_bwd(proj, y, states, dmix, conv_w, conv_b, dtb16, alog16, alog_f, d_f, nw):
    s = proj.shape[0]
    nc = s // CHUNK
    gw = D_SSM // N_GROUPS

    def body(xs_ref, bc_ref, xs_tail, bc_tail, dt_ref, z_ref, y_ref, st_ref, dm_ref, cw_ref, cb_ref, dtb_ref,
             alog16_ref, alogf_ref, df_ref, nw_ref, emat_ref, fold_ref, out_ref, gconv_ref, gvec_ref, gdt_ref,
             dh_scr, head_scr, xpad, dcpad, da_scr, dxdt_scr, dbc_scr):
        i = pl.program_id(0)
        c = nc - 1 - i

        @pl.when(i == 0)
        def _():
            dh_scr[...] = jnp.zeros_like(dh_scr)
            head_scr[...] = jnp.zeros_like(head_scr)
            gconv_ref[...] = jnp.zeros_like(gconv_ref)
            gvec_ref[...] = jnp.zeros_like(gvec_ref)
            gdt_ref[...] = jnp.zeros_like(gdt_ref)

        cv, sig, xbc, pre, dt_f, al_f, al_x, al_t, taps = _ssd_common(
            xs_ref, bc_ref, xs_tail, bc_tail, dt_ref, cw_ref, cb_ref, dtb_ref, alog16_ref, emat_ref, xpad, c == 0)
        head0 = _iota((CHUNK, LANES), 1) < HEAD_DIM
        sub = _iota((CHUNK, LANES), 0)
        last_row = sub == CHUNK - 1

        yv, z, dmx = y_ref[...], z_ref[...], dm_ref[...]
        sz = _sigmoid(z)
        silu = z * sz
        yz = yv * silu
        dyz_parts = []
        for g in range(N_GROUPS):
            gs = slice(g * gw, (g + 1) * gw)
            part = yz[:, gs]
            r = lax.rsqrt(jnp.mean(part * part, axis=-1, keepdims=True) + EPS)
            nh = part * r
            gvec_ref[0:1, gs] += jnp.sum(dmx[:, gs] * nh, axis=0, keepdims=True)
            dn = dmx[:, gs] * nw_ref[:, gs]
            dyz_parts.append(r * (dn - nh * jnp.mean(dn * nh, axis=-1, keepdims=True)))
        dyz = jnp.concatenate(dyz_parts, axis=1)
        dy = dyz * silu
        out_ref[:, 0:D_SSM] = (dyz * yv * (sz * (1.0 + z * (1.0 - sz)))).astype(bf16)

        x_all = xbc[:, 0:D_SSM]
        gvec_ref[2:3, :] += jnp.sum(dy * x_all, axis=0, keepdims=True)

        for g in range(N_GROUPS):
            bm = xbc[:, D_SSM + g * D_STATE: D_SSM + (g + 1) * D_STATE].astype(bf16)
            cm = xbc[:, D_SSM + (N_GROUPS + g) * D_STATE: D_SSM + (N_GROUPS + g + 1) * D_STATE].astype(bf16)
            gmat = _nt(cm, bm)
            dgm = jnp.zeros((CHUNK, CHUNK), f32)
            db = jnp.zeros((CHUNK, D_STATE), f32)
            dc = jnp.zeros((CHUNK, D_STATE), f32)
            for pair in range(4 * g, 4 * g + 4):
                sl = slice(pair * LANES, (pair + 1) * LANES)
                xp, dtp, alp, dyp = x_all[:, sl], dt_f[:, sl], al_f[:, sl], dy[:, sl]
                xdt = xp * dtp
                xdt16 = xdt.astype(bf16)
                al_last = alp[CHUNK - 1:CHUNK, :]
                e_l = jnp.exp(alp)
                wf = jnp.exp(al_last - alp)
                e_last = jnp.exp(al_last)
                hp = st_ref[:, sl]
                hp16 = hp.astype(bf16)
                dhn = dh_scr[:, sl]
                dhn16 = dhn.astype(bf16)
                y_off = e_l * _nn(cm, hp16)
                dch16 = (dyp * e_l).astype(bf16)
                dc = dc + _nt(dch16, hp16)
                dh_out = _tn(cm, dch16)
                dal = dyp * y_off
                xw16 = (wf * xdt).astype(bf16)
                db = db + _nt(xw16, dhn16)
                dxw = _nn(bm, dhn16)
                dxdt = dxw * wf
                dwf = dxw * xdt * wf
                dal = dal - dwf
                dal_last = jnp.sum(dwf, axis=0, keepdims=True) + jnp.sum(dhn * hp, axis=0, keepdims=True) * e_last
                dh_scr[:, sl] = e_last * dhn + dh_out
                for h in range(2):
                    mh = head0 if h == 0 else jnp.logical_not(head0)
                    dyh16 = jnp.where(mh, dyp, 0.0).astype(bf16)
                    lmat = _decay_mat(al_x, al_t, pair, h)
                    mm = gmat * lmat
                    dmm = _nt(dyh16, xdt16)
                    dxdt = dxdt + _tn(mm.astype(bf16), dyh16)
                    n16 = (dmm * mm).astype(bf16)
                    jh = jnp.where(mh, 1.0 / HEAD_DIM, 0.0).astype(bf16)
                    dal = dal + _nn(n16, jh) - _tn(n16, jh)
                    dgm = dgm + dmm * lmat
                da_scr[:, sl] = dal + jnp.where(last_row, dal_last, 0.0)
                dxdt_scr[:, sl] = dxdt
            dgm16 = dgm.astype(bf16)
            dbc_scr[:, g * D_STATE:(g + 1) * D_STATE] = db + _tn(dgm16, cm)
            dbc_scr[:, (N_GROUPS + g) * D_STATE:(N_GROUPS + g + 1) * D_STATE] = dc + _nn(dgm16, bm)

        sub_c, lane_c = _iota((CHUNK, CHUNK), 0), _iota((CHUNK, CHUNK), 1)
        tri_t = (lane_c >= sub_c).astype(bf16)
        dadt = _dot_01_left(tri_t, da_scr[...], 2)
        a_f = -jnp.exp(alogf_ref[...])
        dxdt_all = dxdt_scr[...]
        ddt_f = dxdt_all * x_all + a_f * dadt
        gvec_ref[1:2, :] += jnp.sum(dt_f * dadt, axis=0, keepdims=True) * a_f
        dx = df_ref[...] * dy + dxdt_all * dt_f
        ddt_raw = _dot_01(ddt_f, fold_ref[...], 2) * _sigmoid(pre)
        gdt_ref[0:1, :] += jnp.sum(ddt_raw, axis=0, keepdims=True)
        out_ref[:, D_SSM + D_CONV:D_SSM + D_CONV + LANES] = ddt_raw.astype(bf16)
        out_ref[:, D_SSM + D_CONV + LANES:] = jnp.zeros((CHUNK, 3 * LANES), bf16)

        dsil = sig * (1.0 + cv * (1.0 - sig))
        dcv_x = dx * dsil[:, 0:D_SSM]
        dcv_bc = dbc_scr[...] * dsil[:, D_SSM:]
        dcpad[0:CHUNK, 0:D_SSM] = dcv_x
        dcpad[0:CHUNK, D_SSM:] = dcv_bc
        dcpad[CHUNK:, :] = head_scr[...]
        dcp = dcpad[...]
        dcv = dcp[0:CHUNK]
        gconv_ref[4:5, :] += jnp.sum(dcv, axis=0, keepdims=True)
        draw = cw_ref[3:4, :] * dcv
        for j in range(4):
            gconv_ref[j:j + 1, :] += jnp.sum(dcv * taps[j], axis=0, keepdims=True)
        for j in range(3):
            draw = draw + cw_ref[j:j + 1, :] * pltpu.roll(dcp, CHUNK + 8 - (3 - j), 0)[0:CHUNK]
        head_scr[...] = dcv[0:8]
        out_ref[:, D_SSM:D_SSM + D_CONV] = draw.astype(bf16)

    order = lambda i: nc - 1 - i
    row = lambda w, cb=0: pl.BlockSpec((CHUNK, w), lambda i: (nc - 1 - i, cb))
    return pl.pallas_call(
        body, name="ssd_bwd", grid=(nc,),
        in_specs=_ssd_in_specs(order) + [row(D_SSM), pl.BlockSpec((None, D_STATE, D_SSM), lambda i: (nc - 1 - i, 0, 0)),
                                         row(D_SSM, 1), _full((4, D_CONV)), _full((1, D_CONV)), _full((1, LANES)),
                                         _full((1, LANES)), _full((1, D_SSM)), _full((1, D_SSM)), _full((1, D_SSM)),
                                         _full((LANES, 2 * D_SSM)), _full((D_SSM, LANES))],
        out_specs=[row(3072), _full((8, D_CONV)), _full((8, D_SSM)), _full((8, LANES))],
        out_shape=[SDS((s, 3072), bf16), SDS((8, D_CONV), f32), SDS((8, D_SSM), f32), SDS((8, LANES), f32)],
        scratch_shapes=[pltpu.VMEM((D_STATE, D_SSM), f32), pltpu.VMEM((8, D_CONV), f32),
                        pltpu.VMEM((8 + CHUNK, D_CONV), f32), pltpu.VMEM((8 + CHUNK, D_CONV), f32),
                        pltpu.VMEM((CHUNK, D_SSM), f32), pltpu.VMEM((CHUNK, D_SSM), f32),
                        pltpu.VMEM((CHUNK, 2 * N_GROUPS * D_STATE), f32)],
        compiler_params=pltpu.CompilerParams(dimension_semantics=("arbitrary",)),
    )(proj, proj, proj, proj, proj, proj, y, states, dmix, conv_w, conv_b, dtb16, alog16, alog_f, d_f, nw,
      _expand_mat(), _fold_mat())


def _col_blocks(parts, tile):
    counts = [p.shape[1] // tile for p in parts]
    offs = [sum(counts[:t]) for t in range(len(parts))]
    return offs, counts, sum(counts)


def _inproj_bwd(dparts, wt, x, nw, dres, chip_sums):
    s, d = x.shape
    tm, tk = 1024, 1024
    offs, counts, nk = _col_blocks(dparts, tk)
    npart, nx = len(dparts), len(chip_sums)
    ni = s // tm

    def body(*refs):
        dp_refs = refs[:npart]
        w_ref, x_ref, nw_ref, dres_ref = refs[npart:npart + 4]
        cs_in = refs[npart + 4:npart + 4 + nx]
        gx_ref, gnw_ref = refs[npart + 4 + nx:npart + 6 + nx]
        cs_out = refs[npart + 6 + nx:npart + 6 + 2 * nx]
        acc, send_sems, recv_sems, local_sems = refs[npart + 6 + 2 * nx:]
        i, k = pl.program_id(0), pl.program_id(1)

        @pl.when(jnp.logical_and(i == 0, k == 0))
        def _():
            gnw_ref[...] = jnp.zeros_like(gnw_ref)
            if nx:
                mine, sends, _ = _chip_exchange_copies(cs_in, cs_out, send_sems, recv_sems, local_sems)
                for cp in mine + sends:
                    cp.start()

        @pl.when(jnp.logical_and(i == ni - 1, k == nk - 1))
        def _():
            if nx:
                mine, sends, recvs = _chip_exchange_copies(cs_in, cs_out, send_sems, recv_sems, local_sems)
                for cp in recvs:
                    cp.wait_recv()
                for cp in sends:
                    cp.wait_send()
                for cp in mine:
                    cp.wait()

        @pl.when(k == 0)
        def _():
            acc[...] = jnp.zeros_like(acc)

        for t in range(npart):
            @pl.when(jnp.logical_and(k >= offs[t], k < offs[t] + counts[t]))
            def _(t=t):
                acc[...] += _nn(dp_refs[t][...], w_ref[...])

        @pl.when(k == nk - 1)
        def _():
            xv = x_ref[...]
            r = lax.rsqrt(jnp.mean(xv * xv, axis=-1, keepdims=True) + EPS)
            xn = xv * r
            du = acc[...]
            gnw_ref[0:1, :] += jnp.sum(du * xn, axis=0, keepdims=True)
            dn = du * nw_ref[...]
            gx_ref[...] = dres_ref[...] + r * (dn - xn * jnp.mean(dn * xn, axis=-1, keepdims=True))

    def piece(t):
        return pl.BlockSpec((tm, tk), lambda i, k: (i, jnp.clip(k - offs[t], 0, counts[t] - 1)))

    anyspec = pl.BlockSpec(memory_space=pl.ANY)
    outs = pl.pallas_call(
        body, name="inproj_bwd", grid=(ni, nk),
        in_specs=[piece(t) for t in range(npart)] + [
            pl.BlockSpec((tk, d), lambda i, k: (k, 0)),
            pl.BlockSpec((tm, d), lambda i, k: (i, 0)), pl.BlockSpec((1, d), lambda i, k: (0, 0)),
            pl.BlockSpec((tm, d), lambda i, k: (i, 0))] + [anyspec] * nx,
        out_specs=[pl.BlockSpec((tm, d), lambda i, k: (i, 0)), pl.BlockSpec((8, d), lambda i, k: (0, 0))] + [anyspec] * nx,
        out_shape=[SDS((s, d), f32), SDS((8, d), f32)] + [SDS(a.shape, a.dtype) for a in chip_sums],
        scratch_shapes=[pltpu.VMEM((tm, d), f32)] + _chip_exchange_scratch(max(nx, 1)),
        compiler_params=pltpu.CompilerParams(dimension_semantics=("arbitrary", "arbitrary")),
    )(*dparts, wt, x, nw, dres, *chip_sums)
    return outs[0], outs[1], outs[2:]


def _matmul_tn(a_parts, b_parts, name):
    tile, tk = 1024, 1024
    s = a_parts[0].shape[0]
    nk = s // tk
    na, nb = len(a_parts), len(b_parts)
    offs_a, counts_a, ni = _col_blocks(a_parts, tile)
    offs_b, counts_b, nj = _col_blocks(b_parts, tile)

    def body(*refs):
        a_refs, b_refs, o_ref = refs[:na], refs[na:na + nb], refs[na + nb]
        i, j = pl.program_id(0), pl.program_id(1)

        @pl.when(pl.program_id(2) == 0)
        def _():
            o_ref[...] = jnp.zeros_like(o_ref)

        for ta in range(na):
            for tb in range(nb):
                in_a = jnp.logical_and(i >= offs_a[ta], i < offs_a[ta] + counts_a[ta])
                in_b = jnp.logical_and(j >= offs_b[tb], j < offs_b[tb] + counts_b[tb])

                @pl.when(jnp.logical_and(in_a, in_b))
                def _(ta=ta, tb=tb):
                    o_ref[...] += _tn(a_refs[ta][...], b_refs[tb][...])

    def spec(offs, counts, t, axis):
        def index(i, j, k):
            pos = (i, j)[axis]
            mine = jnp.logical_and(pos >= offs[t], pos < offs[t] + counts[t])
            return jnp.where(mine, k, 0), jnp.clip(pos - offs[t], 0, counts[t] - 1)
        return pl.BlockSpec((tk, tile), index)

    return pl.pallas_call(
        body, name=name, grid=(ni, nj, nk),
        in_specs=[spec(offs_a, counts_a, t, 0) for t in range(na)] + [spec(offs_b, counts_b, t, 1) for t in range(nb)],
        out_specs=pl.BlockSpec((tile, tile), lambda i, j, k: (i, j)),
        out_shape=SDS((ni * tile, nj * tile), f32),
        compiler_params=pltpu.CompilerParams(dimension_semantics=("parallel", "parallel", "arbitrary")),
    )(*a_parts, *b_parts)


def _adamw(w, g, m, v):
    m = ADAM_B1 * m + (1.0 - ADAM_B1) * g
    v = ADAM_B2 * v + (1.0 - ADAM_B2) * (g * g)
    m_hat = m / (1.0 - ADAM_B1 ** ADAM_STEP)
    v_hat = v / (1.0 - ADAM_B2 ** ADAM_STEP)
    delta = -ADAM_LR * (m_hat / (jnp.sqrt(v_hat) + ADAM_EPS) + ADAM_WD * w)
    return delta, m, v


def _sum_adamw(own, parts, w, m, v, name):
    r, c = w.shape
    tc = 256

    def body(o_ref, p_ref, w_ref, m_ref, v_ref, g_ref, d_ref, nm_ref, nv_ref):
        my_q = 2 * lax.axis_index("x") + lax.axis_index("y")
        own_v = o_ref[...]
        g = jnp.where(my_q == 0, own_v, p_ref[0].astype(f32))
        for q in range(1, 4):
            g = g + jnp.where(my_q == q, own_v, p_ref[q].astype(f32))
        g_ref[...] = g
        d_ref[...], nm_ref[...], nv_ref[...] = _adamw(w_ref[...], g, m_ref[...], v_ref[...])

    blk = pl.BlockSpec((r, tc), lambda i: (0, i))
    return pl.pallas_call(
        body, name=name, grid=(c // tc,),
        in_specs=[blk, pl.BlockSpec((4, r, tc), lambda i: (0, 0, i)), blk, blk, blk],
        out_specs=[blk] * 4, out_shape=[SDS((r, c), f32)] * 4,
        compiler_params=pltpu.CompilerParams(dimension_semantics=("parallel",)),
    )(own, parts, w, m, v)


def _sum_small(parts):
    def body(p_ref, o_ref):
        t = p_ref[0]
        for j in range(1, N_DEV):
            t = t + p_ref[j]
        o_ref[...] = t
        row_h = _iota((D_SSM, LANES), 0) // HEAD_DIM
        fold = (row_h == _iota((D_SSM, LANES), 1)).astype(f32)
        lower = t[8:16, 0:LANES]
        folded = _nn_hi(t[8:16, 0:D_SSM], fold)
        loss = jnp.sum(t[11:12, 0:D_MODEL], axis=1, keepdims=True) * (0.5 / D_MODEL)
        row = _iota((8, LANES), 0)
        o_ref[8:16, 0:LANES] = jnp.where(row < 2, folded, jnp.where(row == 4, loss, lower))

    return pl.pallas_call(body, name="sum_small", out_shape=SDS((PACK_ROWS, PACK_W), f32),
                          in_specs=[pl.BlockSpec(memory_space=pltpu.VMEM)],
                          out_specs=pl.BlockSpec(memory_space=pltpu.VMEM))(parts)


def _adamw_small(w, g, m, v):
    def body(w_ref, g_ref, m_ref, v_ref, d_ref, nm_ref, nv_ref):
        d_ref[...], nm_ref[...], nv_ref[...] = _adamw(w_ref[...], g_ref[...], m_ref[...], v_ref[...])

    vm = pl.BlockSpec(memory_space=pltpu.VMEM)
    return pl.pallas_call(body, name="adamw_small", out_shape=[SDS(w.shape, f32)] * 3,
                          in_specs=[vm] * 4, out_specs=[vm] * 3)(w, g, m, v)


def _pad_lanes(v, width):
    return jnp.pad(v, ((0, 0), (0, width - v.shape[1])))


def _local_step(x, tgt, norm_pre_w, wt, conv_w, conv_b, dt_bias, a_log, d_skip, ssm_norm_w, wo, norm_post_w,
                reduce_in_chip):
    dtb16 = _pad_lanes(dt_bias, LANES)
    alog16 = _pad_lanes(a_log, LANES)
    alog_f = jnp.repeat(a_log, HEAD_DIM, axis=1)
    d_f = jnp.repeat(d_skip, HEAD_DIM, axis=1)

    proj, u = _prenorm_inproj(x, norm_pre_w, wt)
    o, lb, mix_a = _attn_fwd(proj)
    mix_s, y, states = _ssd_fwd(proj, conv_w, conv_b, dtb16, alog16, alog_f, d_f, ssm_norm_w)
    dmix, dout, dres, acc_post = _outproj_loss(mix_a, mix_s, wo, x, tgt, norm_post_w)
    dq, dk, dv, dg = _attn_bwd(proj, o, lb, dmix)
    dzxd, g_conv, g_vec, g_dt = _ssd_bwd(proj, y, states, dmix, conv_w, conv_b, dtb16, alog16, alog_f, d_f, ssm_norm_w)
    dparts = [dq, dk, dv, dg, dzxd]
    dw_out = _matmul_tn([mix_a, mix_s], [dout], "dw_out")
    dw_in = _matmul_tn(dparts, [u], "dw_in")
    chip_sums, carry = reduce_in_chip(dw_in, dw_out)
    grad_x, g_pre, exchanged = _inproj_bwd(dparts, wt, x, norm_pre_w, dres, chip_sums)

    rows = [g_conv[0:5], _pad_lanes(g_pre[0:1], PACK_W), _pad_lanes(g_vec[0:1], PACK_W),
            _pad_lanes(acc_post[1:2], PACK_W), _pad_lanes(g_vec[1:3], PACK_W), _pad_lanes(g_dt[0:1], PACK_W),
            _pad_lanes(acc_post[0:1], PACK_W), jnp.zeros((4, PACK_W), f32)]
    return grad_x, carry, exchanged, jnp.concatenate(rows, axis=0)


def kernel(x, norm_pre_w, w_in, conv_w, conv_b, dt_bias, a_log, d_skip, ssm_norm_w, w_out, norm_post_w, loss_target, m_norm_pre_w, m_w_in, m_conv_w, m_conv_b, m_dt_bias, m_a_log, m_d_skip, m_ssm_norm_w, m_w_out, m_norm_post_w, v_norm_pre_w, v_w_in, v_conv_w, v_conv_b, v_dt_bias, v_a_log, v_d_skip, v_ssm_norm_w, v_w_out, v_norm_post_w):
    shard_in = w_in.shape[2]
    shard_cv = conv_w.shape[2]
    me = 4 * lax.axis_index("x") + 2 * lax.axis_index("y") + lax.axis_index("c")

    g_in, g_out, g_cw = _all_gather([w_in[0].T.astype(bf16), w_out[0].astype(bf16), conv_w[0]])
    wt = jnp.pad(g_in.reshape(N_DEV * shard_in, D_MODEL), ((0, NP - N_DEV * shard_in), (0, 0)))
    wo = g_out.reshape(N_DEV * w_out.shape[1], D_MODEL)
    cw = g_cw.transpose(1, 0, 2).reshape(4, D_CONV)

    def reduce_in_chip(dw_in, dw_out):
        send_in = dw_in[:N_DEV * shard_in].reshape(4, 2, shard_in, D_MODEL).transpose(1, 0, 2, 3)
        send_out = dw_out.reshape(4, 2, w_out.shape[1], D_MODEL).transpose(1, 0, 2, 3)
        got_in, got_out = _sibling_swap([send_in, send_out])
        chip_in, own_in = _chip_sum(send_in, got_in, "chip_sum_w_in")
        chip_out, own_out = _chip_sum(send_out, got_out, "chip_sum_w_out")
        return [chip_in, chip_out], (own_in, own_out)

    grad_x, (own_in, own_out), (parts_in, parts_out), pack = _local_step(
        x[0], loss_target[0], norm_pre_w, wt, cw, conv_b, dt_bias, a_log, d_skip, ssm_norm_w, wo, norm_post_w,
        reduce_in_chip)
    parts_small = _gather_small(pack)

    g_w_in, d_w_in, nm_w_in, nv_w_in = (a.T for a in _sum_adamw(
        own_in, parts_in, w_in[0].T, m_w_in[0].T, v_w_in[0].T, "sum_adamw_w_in"))
    g_w_out, d_w_out, nm_w_out, nv_w_out = _sum_adamw(own_out, parts_out, w_out[0], m_w_out[0], v_w_out[0], "sum_adamw_w_out")
    tot = _sum_small(parts_small)

    g_cw_all = tot[0:4]
    small_g = {
        "conv_w": lax.dynamic_slice(g_cw_all, (0, me * shard_cv), (4, shard_cv)),
        "conv_b": tot[4:5], "norm_pre_w": tot[5:6, :D_MODEL], "ssm_norm_w": tot[6:7, :D_SSM],
        "norm_post_w": tot[7:8, :D_MODEL], "a_log": tot[8:9, :16], "d_skip": tot[9:10, :16], "dt_bias": tot[10:11, :16],
    }
    loss = tot[12, 0]
    small_w = {"conv_w": (conv_w[0], m_conv_w[0], v_conv_w[0]), "conv_b": (conv_b, m_conv_b, v_conv_b),
               "norm_pre_w": (norm_pre_w, m_norm_pre_w, v_norm_pre_w), "ssm_norm_w": (ssm_norm_w, m_ssm_norm_w, v_ssm_norm_w),
               "norm_post_w": (norm_post_w, m_norm_post_w, v_norm_post_w), "a_log": (a_log, m_a_log, v_a_log),
               "d_skip": (d_skip, m_d_skip, v_d_skip), "dt_bias": (dt_bias, m_dt_bias, v_dt_bias)}
    names = list(small_w)
    sizes = [small_g[k].size for k in names]
    tot_size = sum(sizes)
    pad_to = -(-tot_size // 1024) * 1024

    def flat(arrs):
        v = jnp.concatenate([a.reshape(-1) for a in arrs])
        return jnp.pad(v, (0, pad_to - tot_size)).reshape(pad_to // LANES, LANES)

    fw = flat([small_w[k][0] for k in names])
    fg = flat([small_g[k] for k in names])
    fm = flat([small_w[k][1] for k in names])
    fv = jnp.pad(jnp.concatenate([small_w[k][2].reshape(-1) for k in names]), (0, pad_to - tot_size),
                 constant_values=1.0).reshape(pad_to // LANES, LANES)
    fd, fnm, fnv = _adamw_small(fw, fg, fm, fv)

    def unflat(f):
        out, off = {}, 0
        v = f.reshape(-1)
        for k, n in zip(names, sizes):
            out[k] = v[off:off + n].reshape(small_g[k].shape)
            off += n
        return out

    sd, snm, snv = unflat(fd), unflat(fnm), unflat(fnv)
    lead = lambda a: a[None]
    order = ["norm_pre_w", "w_in", "conv_w", "conv_b", "dt_bias", "a_log", "d_skip", "ssm_norm_w", "w_out", "norm_post_w"]
    grads = dict(small_g, w_in=g_w_in, w_out=g_w_out)
    deltas = dict(sd, w_in=d_w_in, w_out=d_w_out)
    new_m = dict(snm, w_in=nm_w_in, w_out=nm_w_out)
    new_v = dict(snv, w_in=nv_w_in, w_out=nv_w_out)

    def shaped(dct, k):
        a = dct[k]
        return lead(a) if k in ("w_in", "w_out", "conv_w") else a

    return (loss, grad_x[None], *[shaped(grads, k) for k in order], *[shaped(deltas, k) for k in order],
            *[shaped(new_m, k) for k in order], *[shaped(new_v, k) for k in order])
```

```python
import functools
import math

import jax
import jax.numpy as jnp
import numpy as np
from jax import lax
from jax.experimental import pallas as pl
from jax.experimental.pallas import tpu as pltpu

f32, bf16 = jnp.float32, jnp.bfloat16
SDS = jax.ShapeDtypeStruct
HIGHEST = lax.Precision.HIGHEST
MESH = pl.DeviceIdType.MESH

N_DEV = 8
D_MODEL = 1024
D_ATTN = 1024
D_SSM = 1024
HEAD_DIM = 64
N_PAIRS = 8
D_STATE = 128
N_GROUPS = 2
D_CONV = D_SSM + 2 * N_GROUPS * D_STATE
D_IN_PROJ = 4 * D_ATTN + D_SSM + D_CONV + 16
NP = 7168
CHUNK = 128
BLK = 128
DILATIONS = (1, 4, 16)
EPS = 1e-6
LANES = 128
COL_Z, COL_XS, COL_BC, COL_DT = 4096, 5120, 6144, 6656

ADAM_LR, ADAM_B1, ADAM_B2, ADAM_EPS, ADAM_WD, ADAM_STEP = 0.001, 0.9, 0.999, 1e-08, 0.01, 10

PACK_ROWS, PACK_W = 16, 1536


def _nt(a, b):
    return lax.dot_general(a, b, (((1,), (1,)), ((), ())), preferred_element_type=f32)


def _tn(a, b):
    return lax.dot_general(a, b, (((0,), (0,)), ((), ())), preferred_element_type=f32)


def _nn(a, b):
    return jnp.dot(a, b, preferred_element_type=f32)


def _nn_hi(a, b):
    return jnp.dot(a, b, precision=HIGHEST, preferred_element_type=f32)


def _sigmoid(x):
    return 1.0 / (1.0 + jnp.exp(-x))


def _softplus(x):
    return jnp.maximum(x, 0.0) + jnp.log1p(jnp.exp(-jnp.abs(x)))


def _iota(shape, dim):
    return lax.broadcasted_iota(jnp.int32, shape, dim)


def _my_pos():
    return lax.axis_index("x"), lax.axis_index("y"), lax.axis_index("c")


def _all_gather(arrs):
    n = len(arrs)
    ns = 9

    def body(*refs):
        ins, outs = refs[:n], refs[n:2 * n]
        send_sems, recv_sems, local_sems = refs[2 * n:]
        x, y, c = _my_pos()
        me, sibling = (x, y, c), (x, y, 1 - c)
        xn, yn, diag = (1 - x, y), (x, 1 - y), (1 - x, 1 - y)

        def slot(a, px, py, pc):
            return outs[a].at[4 * px + 2 * py + pc]

        def part(a, ref, h):
            width = arrs[a].shape[-1]
            if width % (2 * LANES):
                return ref if h == 1 else None
            return ref.at[:, pl.ds(h * (width // 2), width // 2)]

        def copy(a, k, block, to, src=None, h=None):
            src_ref = slot(a, *block) if src is None else src
            dst_ref = slot(a, *block)
            if h is not None:
                src_ref, dst_ref = part(a, src_ref, h), part(a, dst_ref, h)
                if src_ref is None:
                    return None
            return pltpu.make_async_remote_copy(
                src_ref=src_ref, dst_ref=dst_ref, send_sem=send_sems.at[ns * a + k], recv_sem=recv_sems.at[ns * a + k],
                device_id=to, device_id_type=MESH)

        mine = [pltpu.make_async_copy(ins[a], slot(a, *me), local_sems.at[a]) for a in range(n)]
        for cp in mine:
            cp.start()
        sends = []
        for a in range(n):
            sends += [copy(a, 0, me, sibling, src=ins[a]), copy(a, 1, me, (*xn, c), src=ins[a]),
                      copy(a, 2, me, (*yn, c), src=ins[a])]
        for cp in sends:
            cp.start()

        def start(cp):
            if cp is not None:
                cp.start()
                sends.append(cp)

        for a in range(n):
            copy(a, 1, (*xn, c), me).wait_recv()
            start(copy(a, 4, (*xn, c), sibling))
            start(copy(a, 7, (*xn, c), (*yn, c), h=1))
        for a in range(n):
            copy(a, 2, (*yn, c), me).wait_recv()
            start(copy(a, 5, (*yn, c), sibling))
            start(copy(a, 8, (*yn, c), (*xn, c), h=0))
        for a in range(n):
            for k, h in ((8, 0), (7, 1)):
                cp = copy(a, k, (*diag, c), me, h=h)
                if cp is not None:
                    cp.wait_recv()
            start(copy(a, 6, (*diag, c), sibling))
        for a in range(n):
            copy(a, 0, sibling, me).wait_recv()
            for j, chip in enumerate((xn, yn, diag)):
                copy(a, 4 + j, (*chip, 1 - c), me).wait_recv()
        for cp in sends:
            cp.wait_send()
        for cp in mine:
            cp.wait()

    anyspec = pl.BlockSpec(memory_space=pl.ANY)
    return pl.pallas_call(
        body, name="weights_all_gather",
        out_shape=[SDS((N_DEV,) + a.shape, a.dtype) for a in arrs],
        in_specs=[anyspec] * n, out_specs=[anyspec] * n,
        scratch_shapes=[pltpu.SemaphoreType.DMA((ns * n,)), pltpu.SemaphoreType.DMA((ns * n,)),
                        pltpu.SemaphoreType.DMA((n,))],
    )(*arrs)


def _sibling_swap(bigs):
    nb = len(bigs)

    def body(*refs):
        ins, outs = refs[:nb], refs[nb:2 * nb]
        send_sems, recv_sems = refs[2 * nb:]
        x, y, c = _my_pos()
        sends = []
        for a in range(nb):
            cp = pltpu.make_async_remote_copy(
                src_ref=ins[a].at[1 - c], dst_ref=outs[a], send_sem=send_sems.at[a], recv_sem=recv_sems.at[a],
                device_id=(x, y, 1 - c), device_id_type=MESH)
            cp.start()
            sends.append(cp)
        for a in range(nb):
            pltpu.make_async_remote_copy(
                src_ref=ins[a].at[c], dst_ref=outs[a], send_sem=send_sems.at[a], recv_sem=recv_sems.at[a],
                device_id=(x, y, c), device_id_type=MESH).wait_recv()
        for cp in sends:
            cp.wait_send()

    anyspec = pl.BlockSpec(memory_space=pl.ANY)
    return pl.pallas_call(
        body, name="grad_sibling_swap", out_shape=[SDS(a.shape[1:], a.dtype) for a in bigs],
        in_specs=[anyspec] * nb, out_specs=[anyspec] * nb,
        scratch_shapes=[pltpu.SemaphoreType.DMA((nb,)), pltpu.SemaphoreType.DMA((nb,))],
    )(*bigs)


def _gather_small(small):
    def body(small_in, small_out, send_sems, recv_sems, local_sem):
        x, y, c = _my_pos()
        me = 4 * x + 2 * y + c
        mine = pltpu.make_async_copy(small_in, small_out.at[me], local_sem)
        mine.start()
        sends = []
        for k in range(1, N_DEV):
            to = (me + k) % N_DEV
            cp = pltpu.make_async_remote_copy(
                src_ref=small_in, dst_ref=small_out.at[me], send_sem=send_sems.at[k - 1], recv_sem=recv_sems.at[k - 1],
                device_id=(to // 4, (to // 2) % 2, to % 2), device_id_type=MESH)
            cp.start()
            sends.append(cp)
        for k in range(1, N_DEV):
            frm = (me + N_DEV - k) % N_DEV
            pltpu.make_async_remote_copy(
                src_ref=small_in, dst_ref=small_out.at[frm], send_sem=send_sems.at[k - 1], recv_sem=recv_sems.at[k - 1],
                device_id=(x, y, c), device_id_type=MESH).wait_recv()
        for cp in sends:
            cp.wait_send()
        mine.wait()

    anyspec = pl.BlockSpec(memory_space=pl.ANY)
    return pl.pallas_call(
        body, name="small_grads_gather", out_shape=SDS((N_DEV,) + small.shape, small.dtype),
        in_specs=[anyspec], out_specs=anyspec,
        scratch_shapes=[pltpu.SemaphoreType.DMA((7,)), pltpu.SemaphoreType.DMA((7,)), pltpu.SemaphoreType.DMA(())],
    )(small)


def _chip_sum(mine, got, name):
    _, nq, r, cdim = mine.shape
    tc = 256

    def body(m_ref, g_ref, s16_ref, own_ref):
        q = pl.program_id(1)
        c = lax.axis_index("c")
        my_q = 2 * lax.axis_index("x") + lax.axis_index("y")
        tot = m_ref[c] + g_ref[...]
        s16_ref[...] = tot.astype(bf16)

        @pl.when(q == my_q)
        def _():
            own_ref[...] = tot

    return pl.pallas_call(
        body, name=name, grid=(cdim // tc, nq),
        in_specs=[pl.BlockSpec((2, None, r, tc), lambda i, q: (0, q, 0, i)),
                  pl.BlockSpec((None, r, tc), lambda i, q: (q, 0, i))],
        out_specs=[pl.BlockSpec((None, r, tc), lambda i, q: (q, 0, i)), pl.BlockSpec((r, tc), lambda i, q: (0, i))],
        out_shape=[SDS((nq, r, cdim), bf16), SDS((r, cdim), f32)],
        compiler_params=pltpu.CompilerParams(dimension_semantics=("parallel", "arbitrary")),
    )(mine, got)


def _chip_exchange_copies(ins, outs, send_sems, recv_sems, local_sems):
    nb = len(ins)
    x, y, c = _my_pos()
    my_q = 2 * x + y
    mine = [pltpu.make_async_copy(ins[a].at[my_q], outs[a].at[my_q], local_sems.at[a]) for a in range(nb)]
    sends, recvs = [], []
    for k in range(1, 4):
        to, frm = (my_q + k) % 4, (my_q + 4 - k) % 4
        for a in range(nb):
            sems = dict(send_sem=send_sems.at[3 * a + k - 1], recv_sem=recv_sems.at[3 * a + k - 1], device_id_type=MESH)
            sends.append(pltpu.make_async_remote_copy(
                src_ref=ins[a].at[to], dst_ref=outs[a].at[my_q], device_id=(to // 2, to % 2, c), **sems))
            recvs.append(pltpu.make_async_remote_copy(
                src_ref=ins[a].at[frm], dst_ref=outs[a].at[frm], device_id=(x, y, c), **sems))
    return mine, sends, recvs


def _chip_exchange_scratch(nb):
    return [pltpu.SemaphoreType.DMA((3 * nb,)), pltpu.SemaphoreType.DMA((3 * nb,)), pltpu.SemaphoreType.DMA((nb,))]


def _prenorm_inproj(x, nw, wt):
    s, d = x.shape
    npad = wt.shape[0]
    tm, tn = 1024, 1024

    def body(x_ref, nw_ref, w_ref, proj_ref, u_ref):
        @pl.when(pl.program_id(1) == 0)
        def _():
            xv = x_ref[...]
            r = lax.rsqrt(jnp.mean(xv * xv, axis=-1, keepdims=True) + EPS)
            u_ref[...] = (xv * r * nw_ref[...]).astype(bf16)
        proj_ref[...] = _nt(u_ref[...], w_ref[...])

    return pl.pallas_call(
        body, name="prenorm_inproj", grid=(s // tm, npad // tn),
        in_specs=[pl.BlockSpec((tm, d), lambda i, j: (i, 0)), pl.BlockSpec((1, d), lambda i, j: (0, 0)),
                  pl.BlockSpec((tn, d), lambda i, j: (j, 0))],
        out_specs=[pl.BlockSpec((tm, tn), lambda i, j: (i, j)), pl.BlockSpec((tm, d), lambda i, j: (i, 0))],
        out_shape=[SDS((s, npad), f32), SDS((s, d), bf16)],
        compiler_params=pltpu.CompilerParams(dimension_semantics=("parallel", "arbitrary")),
    )(x, nw, wt)


def _attn_consts():
    head0 = _iota((BLK, LANES), 1) < HEAD_DIM
    tri2 = (_iota((BLK, 2 * LANES), 1) % LANES) <= _iota((BLK, 2 * LANES), 0)
    ones2 = ((_iota((LANES, 2 * LANES), 0) < HEAD_DIM) == (_iota((LANES, 2 * LANES), 1) < LANES)).astype(bf16)
    rmat = ((_iota((2 * LANES, LANES), 0) < LANES) == (_iota((2 * LANES, LANES), 1) < HEAD_DIM)).astype(bf16)
    bones = ((_iota((LANES, LANES), 0) < HEAD_DIM) == (_iota((LANES, LANES), 1) < HEAD_DIM)).astype(bf16)
    return head0, tri2, ones2, rmat, bones


def _stack_heads(x16, head0):
    zero = jnp.zeros_like(x16)
    return jnp.concatenate([jnp.where(head0, x16, zero), jnp.where(head0, zero, x16)], axis=0)


def _split_dot(x, w16):
    hi = x.astype(bf16)
    lo = (x - hi.astype(f32)).astype(bf16)
    return _nn(hi, w16) + _nn(lo, w16)


def _bf16_terms(x, terms):
    out = []
    for _ in range(terms):
        t = x.astype(bf16)
        out.append(t)
        x = x - t.astype(f32)
    return out


def _dot_01(x, w16, terms):
    return sum(_nn(t, w16) for t in _bf16_terms(x, terms))


def _dot_01_left(w16, x, terms):
    return sum(_nn(w16, t) for t in _bf16_terms(x, terms))


def _attn_fwd(proj):
    s = proj.shape[0]
    n_it = s // BLK

    def body(q_ref, k_ref, v_ref, g_ref, o_ref, l_ref, mix_ref, op0, op1, op2, lp0, lp1, lp2,
             s_a, s_b, sd_a, sd_b, p_a, p_b, m_a, m_b, pd_a, pd_b):
        op_refs, lp_refs = (op0, op1, op2), (lp0, lp1, lp2)
        head0, tri2, ones2, rmat, _ = _attn_consts()
        score_bufs, prob_bufs = ((s_a, sd_a), (s_b, sd_b)), ((p_a, m_a, pd_a), (p_b, m_b, pd_b))

        def block_rows(i, d, nb):
            r, blk = i // nb, i % nb
            st = blk * (BLK * d) + r
            stp = jnp.maximum(blk - 1, 0) * (BLK * d) + r
            return pl.ds(st, BLK, stride=d), pl.ds(stp, BLK, stride=d), blk > 0

        def scores(i, d, nb, bufs):
            rows, rows_p, has_prev = block_rows(i, d, nb)
            s_buf, sd_buf = bufs
            qs = q_ref[rows, :] * 0.125
            kc, kp = k_ref[rows, :], k_ref[rows_p, :]
            qs16 = qs.astype(bf16)
            sc = _nt(qs16, _stack_heads(kc.astype(bf16), head0))
            sp = _nt(qs16, _stack_heads(kp.astype(bf16), head0))
            s_buf[...] = jnp.where(tri2, sc, jnp.where(has_prev, sp, -jnp.inf))
            sd_buf[...] = jnp.where(has_prev, _split_dot(qs * kp, ones2), -jnp.inf)

        def softmax(bufs_in, bufs_out):
            s_buf, sd_buf = bufs_in
            p_buf, m_buf, pd_buf = bufs_out
            sc, sd2 = s_buf[...], sd_buf[...]
            m0 = jnp.max(sc[:, :LANES], axis=1, keepdims=True)
            m1 = jnp.max(sc[:, LANES:], axis=1, keepdims=True)
            m2 = jnp.concatenate([jnp.broadcast_to(m0, (BLK, LANES)), jnp.broadcast_to(m1, (BLK, LANES))], axis=1)
            m2 = jnp.maximum(m2, sd2)
            p_buf[...] = jnp.exp(sc - m2).astype(bf16)
            m_pair = jnp.where(head0, m2[:, :LANES], m2[:, LANES:])
            m_buf[...] = m_pair
            pd_buf[...] = jnp.exp(jnp.where(head0, sd2[:, :LANES], sd2[:, LANES:]) - m_pair)

        def output(i, d, nb, p, bufs):
            rows, rows_p, _ = block_rows(i, d, nb)
            p_buf, m_buf, pd_buf = bufs
            vc, vp = v_ref[rows, :], v_ref[rows_p, :]
            pt16, pd = p_buf[...], pd_buf[...]
            zero = jnp.zeros_like(pt16)
            o = (_nn(jnp.where(tri2, pt16, zero), _stack_heads(vc.astype(bf16), head0))
                 + _nn(jnp.where(tri2, zero, pt16), _stack_heads(vp.astype(bf16), head0)) + pd * vp)
            l = _nn(pt16, rmat) + pd
            op_refs[p][rows, :] = o / l
            lp_refs[p][rows, :] = m_buf[...] + jnp.log(l)

        for p, d in enumerate(DILATIONS):
            nb = s // (BLK * d)
            scores(0, d, nb, score_bufs[0])
            scores(1, d, nb, score_bufs[1])
            softmax(score_bufs[0], prob_bufs[0])

            def steps(j, carry, d=d, nb=nb, p=p):
                for par in range(2):
                    t = 2 * j + 2 + par
                    scores(t, d, nb, score_bufs[par])
                    output(t - 2, d, nb, p, prob_bufs[par])
                    softmax(score_bufs[1 - par], prob_bufs[1 - par])
                return carry

            lax.fori_loop(0, (n_it - 2) // 2, steps, 0)
            output(n_it - 2, d, nb, p, prob_bufs[0])
            softmax(score_bufs[1], prob_bufs[1])
            output(n_it - 1, d, nb, p, prob_bufs[1])

        def merge(i, carry):
            rows = pl.ds(pl.multiple_of(i * 256, 256), 256)
            l0, l1, l2 = lp0[rows, :], lp1[rows, :], lp2[rows, :]
            m = jnp.maximum(jnp.maximum(l0, l1), l2)
            e0, e1, e2 = jnp.exp(l0 - m), jnp.exp(l1 - m), jnp.exp(l2 - m)
            z = e0 + e1 + e2
            o = (e0 * op0[rows, :] + e1 * op1[rows, :] + e2 * op2[rows, :]) / z
            o_ref[rows, :] = o
            l_ref[rows, :] = m + jnp.log(z)
            g = g_ref[rows, :]
            mix_ref[rows, :] = (o * (g * _sigmoid(g))).astype(bf16)
            return carry

        lax.fori_loop(0, s // 256, merge, 0)

    col = lambda base: pl.BlockSpec((s, LANES), lambda h: (0, base + h))
    return pl.pallas_call(
        body, name="attn_fwd", grid=(N_PAIRS,),
        in_specs=[col(0), col(8), col(16), col(24)],
        out_specs=[col(0), col(0), col(0)],
        out_shape=[SDS((s, D_ATTN), f32), SDS((s, D_ATTN), f32), SDS((s, D_ATTN), bf16)],
        scratch_shapes=[pltpu.VMEM((s, LANES), f32)] * 6 + [pltpu.VMEM((BLK, 2 * LANES), f32)] * 4
        + [pltpu.VMEM((BLK, 2 * LANES), bf16)] * 2 + [pltpu.VMEM((BLK, LANES), f32)] * 4,
        compiler_params=pltpu.CompilerParams(dimension_semantics=("parallel",)),
    )(proj, proj, proj, proj)


def _expand_mat():
    colv = np.arange(2 * D_SSM)
    head = 2 * ((colv % D_SSM) // LANES) + colv // D_SSM
    return jnp.asarray(np.arange(LANES)[:, None] == head[None, :], dtype=bf16)


def _fold_mat():
    return jnp.asarray((np.arange(D_SSM) // HEAD_DIM)[:, None] == np.arange(LANES)[None, :], dtype=bf16)


def _ssd_common(xs_ref, bc_ref, xs_tail, bc_tail, dt_ref, cw_ref, cb_ref, dtb_ref, alog16_ref, emat_ref, xpad, first):
    keep = jnp.where(first, 0.0, 1.0)
    xpad[0:8, 0:D_SSM] = xs_tail[...] * keep
    xpad[0:8, D_SSM:D_CONV] = bc_tail[...] * keep
    xpad[8:8 + CHUNK, 0:D_SSM] = xs_ref[...]
    xpad[8:8 + CHUNK, D_SSM:D_CONV] = bc_ref[...]
    xp = xpad[...]
    taps = [pltpu.roll(xp, 3 - j, 0)[8:8 + CHUNK] for j in range(3)] + [xp[8:8 + CHUNK]]
    cv = cb_ref[...] + cw_ref[0:1, :] * taps[0]
    for j in range(1, 4):
        cv = cv + cw_ref[j:j + 1, :] * taps[j]
    sig = _sigmoid(cv)
    xbc = cv * sig

    pre = dt_ref[...] + dtb_ref[...]
    dt16 = _softplus(pre)
    a16 = -jnp.exp(alog16_ref[...])
    sub, lane = _iota((CHUNK, CHUNK), 0), _iota((CHUNK, CHUNK), 1)
    tri = (sub >= lane).astype(f32)
    al16 = _nn_hi(tri, dt16 * a16)
    al_t = al16.T
    emat = emat_ref[...]
    dt_x = _dot_01(dt16, emat, 3)
    al_x = _dot_01(al16, emat, 3)
    lane_w = _iota((CHUNK, D_SSM), 1)
    even = (lane_w % LANES) < HEAD_DIM
    dt_f = jnp.where(even, dt_x[:, :D_SSM], dt_x[:, D_SSM:])
    al_f = jnp.where(even, al_x[:, :D_SSM], al_x[:, D_SSM:])
    return cv, sig, xbc, pre, dt_f, al_f, al_x, al_t, taps


def _decay_mat(al_x, al_t, pair, h):
    sub, lane = _iota((CHUNK, CHUNK), 0), _iota((CHUNK, CHUNK), 1)
    col = al_x[:, h * D_SSM + pair * LANES: h * D_SSM + (pair + 1) * LANES]
    row = al_t[2 * pair + h: 2 * pair + h + 1, :]
    return jnp.exp(jnp.where(sub >= lane, col - row, -jnp.inf))


def _ssd_in_specs(order):
    blk = lambda w, cb: pl.BlockSpec((CHUNK, w), lambda i: (order(i), cb))
    tail = lambda w, cb: pl.BlockSpec((8, w), lambda i: (jnp.maximum(16 * order(i) - 1, 0), cb))
    return [blk(D_SSM, COL_XS // D_SSM), blk(512, COL_BC // 512), tail(D_SSM, COL_XS // D_SSM),
            tail(512, COL_BC // 512), blk(LANES, COL_DT // LANES), blk(D_SSM, COL_Z // D_SSM)]


def _full(shape):
    return pl.BlockSpec(shape, lambda i: (0,) * len(shape))


def _ssd_fwd(proj, conv_w, conv_b, dtb16, alog16, alog_f, d_f, nw):
    s = proj.shape[0]
    nc = s // CHUNK

    def body(xs_ref, bc_ref, xs_tail, bc_tail, dt_ref, z_ref, cw_ref, cb_ref, dtb_ref, alog16_ref, alogf_ref,
             df_ref, nw_ref, emat_ref, mix_ref, y_ref, st_ref, h_scr, xpad, y_scr):
        c = pl.program_id(0)

        @pl.when(c == 0)
        def _():
            h_scr[...] = jnp.zeros_like(h_scr)

        _, _, xbc, _, dt_f, al_f, al_x, al_t, _ = _ssd_common(
            xs_ref, bc_ref, xs_tail, bc_tail, dt_ref, cw_ref, cb_ref, dtb_ref, alog16_ref, emat_ref, xpad, c == 0)
        head0 = _iota((CHUNK, LANES), 1) < HEAD_DIM
        st_ref[...] = h_scr[...]
        for g in range(N_GROUPS):
            bm = xbc[:, D_SSM + g * D_STATE: D_SSM + (g + 1) * D_STATE].astype(bf16)
            cm = xbc[:, D_SSM + (N_GROUPS + g) * D_STATE: D_SSM + (N_GROUPS + g + 1) * D_STATE].astype(bf16)
            gmat = _nt(cm, bm)
            for pair in range(4 * g, 4 * g + 4):
                sl = slice(pair * LANES, (pair + 1) * LANES)
                xp, dtp, alp = xbc[:, sl], dt_f[:, sl], al_f[:, sl]
                xdt = xp * dtp
                xdt16 = xdt.astype(bf16)
                al_last = alp[CHUNK - 1:CHUNK, :]
                hp = h_scr[:, sl]
                y_off = jnp.exp(alp) * _nn(cm, hp.astype(bf16))
                yd = [_nn((gmat * _decay_mat(al_x, al_t, pair, h)).astype(bf16), xdt16) for h in range(2)]
                y_scr[:, sl] = jnp.where(head0, yd[0], yd[1]) + y_off + df_ref[:, sl] * xp
                st = _tn(bm, (jnp.exp(al_last - alp) * xdt).astype(bf16))
                h_scr[:, sl] = jnp.exp(al_last) * hp + st
        y = y_scr[...]
        y_ref[...] = y
        z = z_ref[...]
        yz = y * (z * _sigmoid(z))
        gw = D_SSM // N_GROUPS
        for g in range(N_GROUPS):
            part = yz[:, g * gw:(g + 1) * gw]
            r = lax.rsqrt(jnp.mean(part * part, axis=-1, keepdims=True) + EPS)
            mix_ref[:, g * gw:(g + 1) * gw] = (part * r * nw_ref[:, g * gw:(g + 1) * gw]).astype(bf16)

    order = lambda i: i
    row = lambda w: pl.BlockSpec((CHUNK, w), lambda i: (i, 0))
    return pl.pallas_call(
        body, name="ssd_fwd", grid=(nc,),
        in_specs=_ssd_in_specs(order) + [_full((4, D_CONV)), _full((1, D_CONV)), _full((1, LANES)), _full((1, LANES)),
                                         _full((1, D_SSM)), _full((1, D_SSM)), _full((1, D_SSM)),
                                         _full((LANES, 2 * D_SSM))],
        out_specs=[row(D_SSM), row(D_SSM), pl.BlockSpec((None, D_STATE, D_SSM), lambda i: (i, 0, 0))],
        out_shape=[SDS((s, D_SSM), bf16), SDS((s, D_SSM), f32), SDS((nc, D_STATE, D_SSM), f32)],
        scratch_shapes=[pltpu.VMEM((D_STATE, D_SSM), f32), pltpu.VMEM((8 + CHUNK, D_CONV), f32),
                        pltpu.VMEM((CHUNK, D_SSM), f32)],
        compiler_params=pltpu.CompilerParams(dimension_semantics=("arbitrary",)),
    )(proj, proj, proj, proj, proj, proj, conv_w, conv_b, dtb16, alog16, alog_f, d_f, nw, _expand_mat())


def _outproj_loss(mix_a, mix_s, wo, x, tgt, npw):
    s, d = x.shape
    tm = 512

    def body(ma_ref, ms_ref, wo_ref, x_ref, t_ref, npw_ref, dmix_ref, dout_ref, dres_ref, acc_ref):
        @pl.when(pl.program_id(0) == 0)
        def _():
            acc_ref[...] = jnp.zeros_like(acc_ref)

        out = _nn(ma_ref[...], wo_ref[0:D_ATTN, :]) + _nn(ms_ref[...], wo_ref[D_ATTN:, :])
        r = lax.rsqrt(jnp.mean(out * out, axis=-1, keepdims=True) + EPS)
        on = out * r
        diff = x_ref[...] + on * npw_ref[...] - t_ref[...]
        dres = diff * (1.0 / d)
        dres_ref[...] = dres
        acc_ref[0:1, :] += jnp.sum(diff * diff, axis=0, keepdims=True)
        acc_ref[1:2, :] += jnp.sum(dres * on, axis=0, keepdims=True)
        dn = dres * npw_ref[...]
        dout = (r * (dn - on * jnp.mean(dn * on, axis=-1, keepdims=True))).astype(bf16)
        dout_ref[...] = dout
        dmix_ref[...] = _nt(dout, wo_ref[...])

    row = lambda w: pl.BlockSpec((tm, w), lambda i: (i, 0))
    return pl.pallas_call(
        body, name="outproj_loss", grid=(s // tm,),
        in_specs=[row(D_ATTN), row(D_SSM), _full((D_ATTN + D_SSM, d)), row(d), row(d), _full((1, d))],
        out_specs=[row(D_ATTN + D_SSM), row(d), row(d), _full((8, d))],
        out_shape=[SDS((s, D_ATTN + D_SSM), f32), SDS((s, d), bf16), SDS((s, d), f32), SDS((8, d), f32)],
        compiler_params=pltpu.CompilerParams(dimension_semantics=("arbitrary",)),
    )(mix_a, mix_s, wo, x, tgt, npw)


def _attn_bwd(proj, o, lb, dmix):
    s = proj.shape[0]
    n_it = s // BLK

    def body(q_ref, k_ref, v_ref, g_ref, o_ref, l_ref, dm_ref, dq_ref, dk_ref, dv_ref, dg_ref,
             dq_acc, dk_acc, dv_acc, do_scr, dl_scr, *bufs):
        head0, tri2, _, _, bones = _attn_consts()

        def pro(i, carry):
            rows = pl.ds(pl.multiple_of(i * 256, 256), 256)
            g = g_ref[rows, :]
            sg = _sigmoid(g)
            dmx = dm_ref[rows, :]
            ov = o_ref[rows, :]
            dg_ref[rows, :] = (dmx * ov * (sg * (1.0 + g * (1.0 - sg)))).astype(bf16)
            do = dmx * (g * sg)
            do_scr[rows, :] = do
            dl_scr[rows, :] = _split_dot(do * ov, bones)
            z = jnp.zeros((256, LANES), f32)
            dq_acc[rows, :] = z
            dk_acc[rows, :] = z
            dv_acc[rows, :] = z
            return carry

        lax.fori_loop(0, s // 256, pro, 0)

        def per_head(t):
            return jnp.concatenate([t[:, :LANES], t[:, LANES:]], axis=0)

        def both_heads(t):
            tr = pltpu.roll(t, HEAD_DIM, 1)
            return jnp.concatenate([jnp.where(head0, t, tr), jnp.where(head0, tr, t)], axis=1)

        mm_bufs = ((bufs[0], bufs[1], bufs[2], bufs[3]), (bufs[4], bufs[5], bufs[6], bufs[7]))
        ds_bufs = ((bufs[8], bufs[9], bufs[10], bufs[11]), (bufs[12], bufs[13], bufs[14], bufs[15]))
        op_bufs = ((bufs[16], bufs[17], bufs[18], bufs[19]), (bufs[20], bufs[21], bufs[22], bufs[23]))

        def block_rows(i, d, nb):
            r, blk = i // nb, i % nb
            st = blk * (BLK * d) + r
            stp = jnp.maximum(blk - 1, 0) * (BLK * d) + r
            return pl.ds(st, BLK, stride=d), pl.ds(stp, BLK, stride=d), blk > 0

        def products(i, d, nb, out, ops):
            rows, rows_p, has_prev = block_rows(i, d, nb)
            s_buf, dp_buf, sd_buf, dpd_buf = out
            kc_buf, kp_buf, q_buf, do_buf = ops
            q = q_ref[rows, :]
            qs = q * 0.125
            kc, kp = k_ref[rows, :], k_ref[rows_p, :]
            vc, vp = v_ref[rows, :], v_ref[rows_p, :]
            do = do_scr[rows, :]
            qs16, do16 = qs.astype(bf16), do.astype(bf16)
            kst_c, kst_p = _stack_heads(kc.astype(bf16), head0), _stack_heads(kp.astype(bf16), head0)
            vst_c, vst_p = _stack_heads(vc.astype(bf16), head0), _stack_heads(vp.astype(bf16), head0)
            kc_buf[...] = kst_c
            kp_buf[...] = kst_p
            q_buf[...] = q.astype(bf16)
            do_buf[...] = do16
            s_buf[...] = jnp.where(tri2, _nt(qs16, kst_c), jnp.where(has_prev, _nt(qs16, kst_p), -jnp.inf))
            dp_buf[...] = jnp.where(tri2, _nt(do16, vst_c), _nt(do16, vst_p))
            sd_buf[...] = _split_dot(qs * kp, bones)
            dpd_buf[...] = _split_dot(do * vp, bones)

        def softmax_grad(i, d, nb, inp, out):
            rows, _, has_prev = block_rows(i, d, nb)
            s_buf, dp_buf, sd_buf, dpd_buf = inp
            p_buf, ds_buf, pd_buf, dsd_buf = out
            lse = l_ref[rows, :]
            dl = dl_scr[rows, :]
            pt = jnp.exp(s_buf[...] - both_heads(lse))
            ds_buf[...] = (pt * (dp_buf[...] - both_heads(dl)) * 0.125).astype(bf16)
            p_buf[...] = pt.astype(bf16)
            pd = jnp.where(has_prev, jnp.exp(sd_buf[...] - lse), 0.0)
            pd_buf[...] = pd
            dsd_buf[...] = pd * (dpd_buf[...] - dl) * 0.125

        def accumulate(i, d, nb, inp, ops):
            rows, rows_p, _ = block_rows(i, d, nb)
            p_buf, ds_buf, pd_buf, dsd_buf = inp
            kc_buf, kp_buf, q_buf, do_buf = ops
            pt16, ds16, pd, dsd = p_buf[...], ds_buf[...], pd_buf[...], dsd_buf[...]
            zero = jnp.zeros_like(pt16)
            dsc, dsp = jnp.where(tri2, ds16, zero), jnp.where(tri2, zero, ds16)
            pc, pp = jnp.where(tri2, pt16, zero), jnp.where(tri2, zero, pt16)
            kst_c, kst_p, q16, do16 = kc_buf[...], kp_buf[...], q_buf[...], do_buf[...]
            kp16 = kst_p[:BLK] + kst_p[BLK:]
            qst, dost = _stack_heads(q16, head0), _stack_heads(do16, head0)
            dq_acc[rows, :] += _nn(dsc, kst_c) + _nn(dsp, kst_p) + dsd * kp16.astype(f32)
            dk_acc[rows, :] += _tn(per_head(dsc), qst)
            dv_acc[rows, :] += _tn(per_head(pc), dost)
            dk_acc[rows_p, :] += _tn(per_head(dsp), qst) + dsd * q16.astype(f32)
            dv_acc[rows_p, :] += _tn(per_head(pp), dost) + pd * do16.astype(f32)

        for d in DILATIONS:
            nb = s // (BLK * d)
            products(0, d, nb, mm_bufs[0], op_bufs[0])
            products(1, d, nb, mm_bufs[1], op_bufs[1])
            softmax_grad(0, d, nb, mm_bufs[0], ds_bufs[0])

            def steps(j, carry, d=d, nb=nb):
                for par in range(2):
                    t = 2 * j + 2 + par
                    accumulate(t - 2, d, nb, ds_bufs[par], op_bufs[par])
                    products(t, d, nb, mm_bufs[par], op_bufs[par])
                    softmax_grad(t - 1, d, nb, mm_bufs[1 - par], ds_bufs[1 - par])
                return carry

            lax.fori_loop(0, (n_it - 2) // 2, steps, 0)
            accumulate(n_it - 2, d, nb, ds_bufs[0], op_bufs[0])
            softmax_grad(n_it - 1, d, nb, mm_bufs[1], ds_bufs[1])
            accumulate(n_it - 1, d, nb, ds_bufs[1], op_bufs[1])

        def epi(i, carry):
            rows = pl.ds(pl.multiple_of(i * 256, 256), 256)
            dq_ref[rows, :] = dq_acc[rows, :].astype(bf16)
            dk_ref[rows, :] = dk_acc[rows, :].astype(bf16)
            dv_ref[rows, :] = dv_acc[rows, :].astype(bf16)
            return carry

        lax.fori_loop(0, s // 256, epi, 0)

    col = lambda base: pl.BlockSpec((s, LANES), lambda h: (0, base + h))
    outs = pl.pallas_call(
        body, name="attn_bwd", grid=(N_PAIRS,),
        in_specs=[col(0), col(8), col(16), col(24), col(0), col(0), col(0)],
        out_specs=[col(0)] * 4,
        out_shape=[SDS((s, D_ATTN), bf16)] * 4,
        scratch_shapes=[pltpu.VMEM((s, LANES), f32)] * 5
        + [pltpu.VMEM((BLK, 2 * LANES), f32)] * 2 + [pltpu.VMEM((BLK, LANES), f32)] * 2
        + [pltpu.VMEM((BLK, 2 * LANES), f32)] * 2 + [pltpu.VMEM((BLK, LANES), f32)] * 2
        + [pltpu.VMEM((BLK, 2 * LANES), bf16)] * 2 + [pltpu.VMEM((BLK, LANES), f32)] * 2
        + [pltpu.VMEM((BLK, 2 * LANES), bf16)] * 2 + [pltpu.VMEM((BLK, LANES), f32)] * 2
        + [pltpu.VMEM((2 * BLK, LANES), bf16)] * 2 + [pltpu.VMEM((BLK, LANES), bf16)] * 2
        + [pltpu.VMEM((2 * BLK, LANES), bf16)] * 2 + [pltpu.VMEM((BLK, LANES), bf16)] * 2,
        compiler_params=pltpu.CompilerParams(dimension_semantics=("parallel",)),
    )(proj, proj, proj, proj, o, lb, dmix)
    return outs


def _ssd_bwd(proj, y, states, dmix, conv_w, conv_b, dtb16, alog16, alog_f, d_f, nw):
    s = proj.shape[0]
    nc = s // CHUNK
    gw = D_SSM // N_GROUPS

    def body(xs_ref, bc_ref, xs_tail, bc_tail, dt_ref, z_ref, y_ref, st_ref, dm_ref, cw_ref, cb_ref, dtb_ref,
             alog16_ref, alogf_ref, df_ref, nw_ref, emat_ref, fold_ref, out_ref, gconv_ref, gvec_ref, gdt_ref,
             dh_scr, head_scr, xpad, dcpad, da_scr, dxdt_scr, dbc_scr):
        i = pl.program_id(0)
        c = nc - 1 - i

        @pl.when(i == 0)
        def _():
            dh_scr[...] = jnp.zeros_like(dh_scr)
            head_scr[...] = jnp.zeros_like(head_scr)
            gconv_ref[...] = jnp.zeros_like(gconv_ref)
            gvec_ref[...] = jnp.zeros_like(gvec_ref)
            gdt_ref[...] = jnp.zeros_like(gdt_ref)

        cv, sig, xbc, pre, dt_f, al_f, al_x, al_t, taps = _ssd_common(
            xs_ref, bc_ref, xs_tail, bc_tail, dt_ref, cw_ref, cb_ref, dtb_ref, alog16_ref, emat_ref, xpad, c == 0)
        head0 = _iota((CHUNK, LANES), 1) < HEAD_DIM
        sub = _iota((CHUNK, LANES), 0)
        last_row = sub == CHUNK - 1

        yv, z, dmx = y_ref[...], z_ref[...], dm_ref[...]
        sz = _sigmoid(z)
        silu = z * sz
        yz = yv * silu
        dyz_parts = []
        for g in range(N_GROUPS):
            gs = slice(g * gw, (g + 1) * gw)
            part = yz[:, gs]
            r = lax.rsqrt(jnp.mean(part * part, axis=-1, keepdims=True) + EPS)
            nh = part * r
            gvec_ref[0:1, gs] += jnp.sum(dmx[:, gs] * nh, axis=0, keepdims=True)
            dn = dmx[:, gs] * nw_ref[:, gs]
            dyz_parts.append(r * (dn - nh * jnp.mean(dn * nh, axis=-1, keepdims=True)))
        dyz = jnp.concatenate(dyz_parts, axis=1)
        dy = dyz * silu
        out_ref[:, 0:D_SSM] = (dyz * yv * (sz * (1.0 + z * (1.0 - sz)))).astype(bf16)

        x_all = xbc[:, 0:D_SSM]
        gvec_ref[2:3, :] += jnp.sum(dy * x_all, axis=0, keepdims=True)

        for g in range(N_GROUPS):
            bm = xbc[:, D_SSM + g * D_STATE: D_SSM + (g + 1) * D_STATE].astype(bf16)
            cm = xbc[:, D_SSM + (N_GROUPS + g) * D_STATE: D_SSM + (N_GROUPS + g + 1) * D_STATE].astype(bf16)
            gmat = _nt(cm, bm)
            dgm = jnp.zeros((CHUNK, CHUNK), f32)
            db = jnp.zeros((CHUNK, D_STATE), f32)
            dc = jnp.zeros((CHUNK, D_STATE), f32)
            for pair in range(4 * g, 4 * g + 4):
                sl = slice(pair * LANES, (pair + 1) * LANES)
                xp, dtp, alp, dyp = x_all[:, sl], dt_f[:, sl], al_f[:, sl], dy[:, sl]
                xdt = xp * dtp
                xdt16 = xdt.astype(bf16)
                al_last = alp[CHUNK - 1:CHUNK, :]
                e_l = jnp.exp(alp)
                wf = jnp.exp(al_last - alp)
                e_last = jnp.exp(al_last)
                hp = st_ref[:, sl]
                hp16 = hp.astype(bf16)
                dhn = dh_scr[:, sl]
                dhn16 = dhn.astype(bf16)
                y_off = e_l * _nn(cm, hp16)
                dch16 = (dyp * e_l).astype(bf16)
                dc = dc + _nt(dch16, hp16)
                dh_out = _tn(cm, dch16)
                dal = dyp * y_off
                xw16 = (wf * xdt).astype(bf16)
                db = db + _nt(xw16, dhn16)
                dxw = _nn(bm, dhn16)
                dxdt = dxw * wf
                dwf = dxw * xdt * wf
                dal = dal - dwf
                dal_last = jnp.sum(dwf, axis=0, keepdims=True) + jnp.sum(dhn * hp, axis=0, keepdims=True) * e_last
                dh_scr[:, sl] = e_last * dhn + dh_out
                for h in range(2):
                    mh = head0 if h == 0 else jnp.logical_not(head0)
                    dyh16 = jnp.where(mh, dyp, 0.0).astype(bf16)
                    lmat = _decay_mat(al_x, al_t, pair, h)
                    mm = gmat * lmat
                    dmm = _nt(dyh16, xdt16)
                    dxdt = dxdt + _tn(mm.astype(bf16), dyh16)
                    n16 = (dmm * mm).astype(bf16)
                    jh = jnp.where(mh, 1.0 / HEAD_DIM, 0.0).astype(bf16)
                    dal = dal + _nn(n16, jh) - _tn(n16, jh)
                    dgm = dgm + dmm * lmat
                da_scr[:, sl] = dal + jnp.where(last_row, dal_last, 0.0)
                dxdt_scr[:, sl] = dxdt
            dgm16 = dgm.astype(bf16)
            dbc_scr[:, g * D_STATE:(g + 1) * D_STATE] = db + _tn(dgm16, cm)
            dbc_scr[:, (N_GROUPS + g) * D_STATE:(N_GROUPS + g + 1) * D_STATE] = dc + _nn(dgm16, bm)

        sub_c, lane_c = _iota((CHUNK, CHUNK), 0), _iota((CHUNK, CHUNK), 1)
        tri_t = (lane_c >= sub_c).astype(bf16)
        dadt = _dot_01_left(tri_t, da_scr[...], 2)
        a_f = -jnp.exp(alogf_ref[...])
        dxdt_all = dxdt_scr[...]
        ddt_f = dxdt_all * x_all + a_f * dadt
        gvec_ref[1:2, :] += jnp.sum(dt_f * dadt, axis=0, keepdims=True) * a_f
        dx = df_ref[...] * dy + dxdt_all * dt_f
        ddt_raw = _dot_01(ddt_f, fold_ref[...], 2) * _sigmoid(pre)
        gdt_ref[0:1, :] += jnp.sum(ddt_raw, axis=0, keepdims=True)
        out_ref[:, D_SSM + D_CONV:D_SSM + D_CONV + LANES] = ddt_raw.astype(bf16)
        out_ref[:, D_SSM + D_CONV + LANES:] = jnp.zeros((CHUNK, 3 * LANES), bf16)

        dsil = sig * (1.0 + cv * (1.0 - sig))
        dcv_x = dx * dsil[:, 0:D_SSM]
        dcv_bc = dbc_scr[...] * dsil[:, D_SSM:]
        dcpad[0:CHUNK, 0:D_SSM] = dcv_x
        dcpad[0:CHUNK, D_SSM:] = dcv_bc
        dcpad[CHUNK:, :] = head_scr[...]
        dcp = dcpad[...]
        dcv = dcp[0:CHUNK]
        gconv_ref[4:5, :] += jnp.sum(dcv, axis=0, keepdims=True)
        draw = cw_ref[3:4, :] * dcv
        for j in range(4):
            gconv_ref[j:j + 1, :] += jnp.sum(dcv * taps[j], axis=0, keepdims=True)
        for j in range(3):
            draw = draw + cw_ref[j:j + 1, :] * pltpu.roll(dcp, CHUNK + 8 - (3 - j), 0)[0:CHUNK]
        head_scr[...] = dcv[0:8]
        out_ref[:, D_SSM:D_SSM + D_CONV] = draw.astype(bf16)

    order = lambda i: nc - 1 - i
    row = lambda w, cb=0: pl.BlockSpec((CHUNK, w), lambda i: (nc - 1 - i, cb))
    return pl.pallas_call(
        body, name="ssd_bwd", grid=(nc,),
        in_specs=_ssd_in_specs(order) + [row(D_SSM), pl.BlockSpec((None, D_STATE, D_SSM), lambda i: (nc - 1 - i, 0, 0)),
                                         row(D_SSM, 1), _full((4, D_CONV)), _full((1, D_CONV)), _full((1, LANES)),
                                         _full((1, LANES)), _full((1, D_SSM)), _full((1, D_SSM)), _full((1, D_SSM)),
                                         _full((LANES, 2 * D_SSM)), _full((D_SSM, LANES))],
        out_specs=[row(3072), _full((8, D_CONV)), _full((8, D_SSM)), _full((8, LANES))],
        out_shape=[SDS((s, 3072), bf16), SDS((8, D_CONV), f32), SDS((8, D_SSM), f32), SDS((8, LANES), f32)],
        scratch_shapes=[pltpu.VMEM((D_STATE, D_SSM), f32), pltpu.VMEM((8, D_CONV), f32),
                        pltpu.VMEM((8 + CHUNK, D_CONV), f32), pltpu.VMEM((8 + CHUNK, D_CONV), f32),
                        pltpu.VMEM((CHUNK, D_SSM), f32), pltpu.VMEM((CHUNK, D_SSM), f32),
                        pltpu.VMEM((CHUNK, 2 * N_GROUPS * D_STATE), f32)],
        compiler_params=pltpu.CompilerParams(dimension_semantics=("arbitrary",)),
    )(proj, proj, proj, proj, proj, proj, y, states, dmix, conv_w, conv_b, dtb16, alog16, alog_f, d_f, nw,
      _expand_mat(), _fold_mat())


def _col_blocks(parts, tile):
    counts = [p.shape[1] // tile for p in parts]
    offs = [sum(counts[:t]) for t in range(len(parts))]
    return offs, counts, sum(counts)


def _inproj_bwd(dparts, wt, x, nw, dres, chip_sums):
    s, d = x.shape
    tm, tk = 1024, 1024
    offs, counts, nk = _col_blocks(dparts, tk)
    npart, nx = len(dparts), len(chip_sums)
    ni = s // tm

    def body(*refs):
        dp_refs = refs[:npart]
        w_ref, x_ref, nw_ref, dres_ref = refs[npart:npart + 4]
        cs_in = refs[npart + 4:npart + 4 + nx]
        gx_ref, gnw_ref = refs[npart + 4 + nx:npart + 6 + nx]
        cs_out = refs[npart + 6 + nx:npart + 6 + 2 * nx]
        acc, send_sems, recv_sems, local_sems = refs[npart + 6 + 2 * nx:]
        i, k = pl.program_id(0), pl.program_id(1)

        @pl.when(jnp.logical_and(i == 0, k == 0))
        def _():
            gnw_ref[...] = jnp.zeros_like(gnw_ref)
            if nx:
                mine, sends, _ = _chip_exchange_copies(cs_in, cs_out, send_sems, recv_sems, local_sems)
                for cp in mine + sends:
                    cp.start()

        @pl.when(jnp.logical_and(i == ni - 1, k == nk - 1))
        def _():
            if nx:
                mine, sends, recvs = _chip_exchange_copies(cs_in, cs_out, send_sems, recv_sems, local_sems)
                for cp in recvs:
                    cp.wait_recv()
                for cp in sends:
                    cp.wait_send()
                for cp in mine:
                    cp.wait()

        @pl.when(k == 0)
        def _():
            acc[...] = jnp.zeros_like(acc)

        for t in range(npart):
            @pl.when(jnp.logical_and(k >= offs[t], k < offs[t] + counts[t]))
            def _(t=t):
                acc[...] += _nn(dp_refs[t][...], w_ref[...])

        @pl.when(k == nk - 1)
        def _():
            xv = x_ref[...]
            r = lax.rsqrt(jnp.mean(xv * xv, axis=-1, keepdims=True) + EPS)
            xn = xv * r
            du = acc[...]
            gnw_ref[0:1, :] += jnp.sum(du * xn, axis=0, keepdims=True)
            dn = du * nw_ref[...]
            gx_ref[...] = dres_ref[...] + r * (dn - xn * jnp.mean(dn * xn, axis=-1, keepdims=True))

    def piece(t):
        return pl.BlockSpec((tm, tk), lambda i, k: (i, jnp.clip(k - offs[t], 0, counts[t] - 1)))

    anyspec = pl.BlockSpec(memory_space=pl.ANY)
    outs = pl.pallas_call(
        body, name="inproj_bwd", grid=(ni, nk),
        in_specs=[piece(t) for t in range(npart)] + [
            pl.BlockSpec((tk, d), lambda i, k: (k, 0)),
            pl.BlockSpec((tm, d), lambda i, k: (i, 0)), pl.BlockSpec((1, d), lambda i, k: (0, 0)),
            pl.BlockSpec((tm, d), lambda i, k: (i, 0))] + [anyspec] * nx,
        out_specs=[pl.BlockSpec((tm, d), lambda i, k: (i, 0)), pl.BlockSpec((8, d), lambda i, k: (0, 0))] + [anyspec] * nx,
        out_shape=[SDS((s, d), f32), SDS((8, d), f32)] + [SDS(a.shape, a.dtype) for a in chip_sums],
        scratch_shapes=[pltpu.VMEM((tm, d), f32)] + _chip_exchange_scratch(max(nx, 1)),
        compiler_params=pltpu.CompilerParams(dimension_semantics=("arbitrary", "arbitrary")),
    )(*dparts, wt, x, nw, dres, *chip_sums)
    return outs[0], outs[1], outs[2:]


def _matmul_tn(a_parts, b_parts, name):
    tile, tk = 1024, 1024
    s = a_parts[0].shape[0]
    nk = s // tk
    na, nb = len(a_parts), len(b_parts)
    offs_a, counts_a, ni = _col_blocks(a_parts, tile)
    offs_b, counts_b, nj = _col_blocks(b_parts, tile)

    def body(*refs):
        a_refs, b_refs, o_ref = refs[:na], refs[na:na + nb], refs[na + nb]
        i, j = pl.program_id(0), pl.program_id(1)

        @pl.when(pl.program_id(2) == 0)
        def _():
            o_ref[...] = jnp.zeros_like(o_ref)

        for ta in range(na):
            for tb in range(nb):
                in_a = jnp.logical_and(i >= offs_a[ta], i < offs_a[ta] + counts_a[ta])
                in_b = jnp.logical_and(j >= offs_b[tb], j < offs_b[tb] + counts_b[tb])

                @pl.when(jnp.logical_and(in_a, in_b))
                def _(ta=ta, tb=tb):
                    o_ref[...] += _tn(a_refs[ta][...], b_refs[tb][...])

    def spec(offs, counts, t, axis):
        def index(i, j, k):
            pos = (i, j)[axis]
            mine = jnp.logical_and(pos >= offs[t], pos < offs[t] + counts[t])
            return jnp.where(mine, k, 0), jnp.clip(pos - offs[t], 0, counts[t] - 1)
        return pl.BlockSpec((tk, tile), index)

    return pl.pallas_call(
        body, name=name, grid=(ni, nj, nk),
        in_specs=[spec(offs_a, counts_a, t, 0) for t in range(na)] + [spec(offs_b, counts_b, t, 1) for t in range(nb)],
        out_specs=pl.BlockSpec((tile, tile), lambda i, j, k: (i, j)),
        out_shape=SDS((ni * tile, nj * tile), f32),
        compiler_params=pltpu.CompilerParams(dimension_semantics=("parallel", "parallel", "arbitrary")),
    )(*a_parts, *b_parts)


def _adamw(w, g, m, v):
    m = ADAM_B1 * m + (1.0 - ADAM_B1) * g
    v = ADAM_B2 * v + (1.0 - ADAM_B2) * (g * g)
    m_hat = m / (1.0 - ADAM_B1 ** ADAM_STEP)
    v_hat = v / (1.0 - ADAM_B2 ** ADAM_STEP)
    delta = -ADAM_LR * (m_hat / (jnp.sqrt(v_hat) + ADAM_EPS) + ADAM_WD * w)
    return delta, m, v


def _sum_adamw(own, parts, w, m, v, name):
    r, c = w.shape
    tc = 256

    def body(o_ref, p_ref, w_ref, m_ref, v_ref, g_ref, d_ref, nm_ref, nv_ref):
        my_q = 2 * lax.axis_index("x") + lax.axis_index("y")
        own_v = o_ref[...]
        g = jnp.where(my_q == 0, own_v, p_ref[0].astype(f32))
        for q in range(1, 4):
            g = g + jnp.where(my_q == q, own_v, p_ref[q].astype(f32))
        g_ref[...] = g
        d_ref[...], nm_ref[...], nv_ref[...] = _adamw(w_ref[...], g, m_ref[...], v_ref[...])

    blk = pl.BlockSpec((r, tc), lambda i: (0, i))
    return pl.pallas_call(
        body, name=name, grid=(c // tc,),
        in_specs=[blk, pl.BlockSpec((4, r, tc), lambda i: (0, 0, i)), blk, blk, blk],
        out_specs=[blk] * 4, out_shape=[SDS((r, c), f32)] * 4,
        compiler_params=pltpu.CompilerParams(dimension_semantics=("parallel",)),
    )(own, parts, w, m, v)


def _sum_small(parts):
    def body(p_ref, o_ref):
        t = p_ref[0]
        for j in range(1, N_DEV):
            t = t + p_ref[j]
        o_ref[...] = t
        row_h = _iota((D_SSM, LANES), 0) // HEAD_DIM
        fold = (row_h == _iota((D_SSM, LANES), 1)).astype(f32)
        lower = t[8:16, 0:LANES]
        folded = _nn_hi(t[8:16, 0:D_SSM], fold)
        loss = jnp.sum(t[11:12, 0:D_MODEL], axis=1, keepdims=True) * (0.5 / D_MODEL)
        row = _iota((8, LANES), 0)
        o_ref[8:16, 0:LANES] = jnp.where(row < 2, folded, jnp.where(row == 4, loss, lower))

    return pl.pallas_call(body, name="sum_small", out_shape=SDS((PACK_ROWS, PACK_W), f32),
                          in_specs=[pl.BlockSpec(memory_space=pltpu.VMEM)],
                          out_specs=pl.BlockSpec(memory_space=pltpu.VMEM))(parts)


def _adamw_small(w, g, m, v):
    def body(w_ref, g_ref, m_ref, v_ref, d_ref, nm_ref, nv_ref):
        d_ref[...], nm_ref[...], nv_ref[...] = _adamw(w_ref[...], g_ref[...], m_ref[...], v_ref[...])

    vm = pl.BlockSpec(memory_space=pltpu.VMEM)
    return pl.pallas_call(body, name="adamw_small", out_shape=[SDS(w.shape, f32)] * 3,
                          in_specs=[vm] * 4, out_specs=[vm] * 3)(w, g, m, v)


def _pad_lanes(v, width):
    return jnp.pad(v, ((0, 0), (0, width - v.shape[1])))


def _local_step(x, tgt, norm_pre_w, wt, conv_w, conv_b, dt_bias, a_log, d_skip, ssm_norm_w, wo, norm_post_w,
                reduce_in_chip):
    dtb16 = _pad_lanes(dt_bias, LANES)
    alog16 = _pad_lanes(a_log, LANES)
    alog_f = jnp.repeat(a_log, HEAD_DIM, axis=1)
    d_f = jnp.repeat(d_skip, HEAD_DIM, axis=1)

    proj, u = _prenorm_inproj(x, norm_pre_w, wt)
    o, lb, mix_a = _attn_fwd(proj)
    mix_s, y, states = _ssd_fwd(proj, conv_w, conv_b, dtb16, alog16, alog_f, d_f, ssm_norm_w)
    dmix, dout, dres, acc_post = _outproj_loss(mix_a, mix_s, wo, x, tgt, norm_post_w)
    dq, dk, dv, dg = _attn_bwd(proj, o, lb, dmix)
    dzxd, g_conv, g_vec, g_dt = _ssd_bwd(proj, y, states, dmix, conv_w, conv_b, dtb16, alog16, alog_f, d_f, ssm_norm_w)
    dparts = [dq, dk, dv, dg, dzxd]
    dw_out = _matmul_tn([mix_a, mix_s], [dout], "dw_out")
    dw_in = _matmul_tn(dparts, [u], "dw_in")
    chip_sums, carry = reduce_in_chip(dw_in, dw_out)
    grad_x, g_pre, exchanged = _inproj_bwd(dparts, wt, x, norm_pre_w, dres, chip_sums)

    rows = [g_conv[0:5], _pad_lanes(g_pre[0:1], PACK_W), _pad_lanes(g_vec[0:1], PACK_W),
            _pad_lanes(acc_post[1:2], PACK_W), _pad_lanes(g_vec[1:3], PACK_W), _pad_lanes(g_dt[0:1], PACK_W),
            _pad_lanes(acc_post[0:1], PACK_W), jnp.zeros((4, PACK_W), f32)]
    return grad_x, carry, exchanged, jnp.concatenate(rows, axis=0)


def kernel(x, norm_pre_w, w_in, conv_w, conv_b, dt_bias, a_log, d_skip, ssm_norm_w, w_out, norm_post_w, loss_target, m_norm_pre_w, m_w_in, m_conv_w, m_conv_b, m_dt_bias, m_a_log, m_d_skip, m_ssm_norm_w, m_w_out, m_norm_post_w, v_norm_pre_w, v_w_in, v_conv_w, v_conv_b, v_dt_bias, v_a_log, v_d_skip, v_ssm_norm_w, v_w_out, v_norm_post_w):
    shard_in = w_in.shape[2]
    shard_cv = conv_w.shape[2]
    me = 4 * lax.axis_index("x") + 2 * lax.axis_index("y") + lax.axis_index("c")

    g_in, g_out, g_cw = _all_gather([w_in[0].T.astype(bf16), w_out[0].astype(bf16), conv_w[0]])
    wt = jnp.pad(g_in.reshape(N_DEV * shard_in, D_MODEL), ((0, NP - N_DEV * shard_in), (0, 0)))
    wo = g_out.reshape(N_DEV * w_out.shape[1], D_MODEL)
    cw = g_cw.transpose(1, 0, 2).reshape(4, D_CONV)

    def reduce_in_chip(dw_in, dw_out):
        send_in = dw_in[:N_DEV * shard_in].reshape(4, 2, shard_in, D_MODEL).transpose(1, 0, 2, 3)
        send_out = dw_out.reshape(4, 2, w_out.shape[1], D_MODEL).transpose(1, 0, 2, 3)
        got_in, got_out = _sibling_swap([send_in, send_out])
        chip_in, own_in = _chip_sum(send_in, got_in, "chip_sum_w_in")
        chip_out, own_out = _chip_sum(send_out, got_out, "chip_sum_w_out")
        return [chip_in, chip_out], (own_in, own_out)

    grad_x, (own_in, own_out), (parts_in, parts_out), pack = _local_step(
        x[0], loss_target[0], norm_pre_w, wt, cw, conv_b, dt_bias, a_log, d_skip, ssm_norm_w, wo, norm_post_w,
        reduce_in_chip)
    parts_small = _gather_small(pack)

    g_w_in, d_w_in, nm_w_in, nv_w_in = (a.T for a in _sum_adamw(
        own_in, parts_in, w_in[0].T, m_w_in[0].T, v_w_in[0].T, "sum_adamw_w_in"))
    g_w_out, d_w_out, nm_w_out, nv_w_out = _sum_adamw(own_out, parts_out, w_out[0], m_w_out[0], v_w_out[0], "sum_adamw_w_out")
    tot = _sum_small(parts_small)

    g_cw_all = tot[0:4]
    small_g = {
        "conv_w": lax.dynamic_slice(g_cw_all, (0, me * shard_cv), (4, shard_cv)),
        "conv_b": tot[4:5], "norm_pre_w": tot[5:6, :D_MODEL], "ssm_norm_w": tot[6:7, :D_SSM],
        "norm_post_w": tot[7:8, :D_MODEL], "a_log": tot[8:9, :16], "d_skip": tot[9:10, :16], "dt_bias": tot[10:11, :16],
    }
    loss = tot[12, 0]
    small_w = {"conv_w": (conv_w[0], m_conv_w[0], v_conv_w[0]), "conv_b": (conv_b, m_conv_b, v_conv_b),
               "norm_pre_w": (norm_pre_w, m_norm_pre_w, v_norm_pre_w), "ssm_norm_w": (ssm_norm_w, m_ssm_norm_w, v_ssm_norm_w),
               "norm_post_w": (norm_post_w, m_norm_post_w, v_norm_post_w), "a_log": (a_log, m_a_log, v_a_log),
               "d_skip": (d_skip, m_d_skip, v_d_skip), "dt_bias": (dt_bias, m_dt_bias, v_dt_bias)}
    names = list(small_w)
    sizes = [small_g[k].size for k in names]
    tot_size = sum(sizes)
    pad_to = -(-tot_size // 1024) * 1024

    def flat(arrs):
        v = jnp.concatenate([a.reshape(-1) for a in arrs])
        return jnp.pad(v, (0, pad_to - tot_size)).reshape(pad_to // LANES, LANES)

    fw = flat([small_w[k][0] for k in names])
    fg = flat([small_g[k] for k in names])
    fm = flat([small_w[k][1] for k in names])
    fv = jnp.pad(jnp.concatenate([small_w[k][2].reshape(-1) for k in names]), (0, pad_to - tot_size),
                 constant_values=1.0).reshape(pad_to // LANES, LANES)
    fd, fnm, fnv = _adamw_small(fw, fg, fm, fv)

    def unflat(f):
        out, off = {}, 0
        v = f.reshape(-1)
        for k, n in zip(names, sizes):
            out[k] = v[off:off + n].reshape(small_g[k].shape)
            off += n
        return out

    sd, snm, snv = unflat(fd), unflat(fnm), unflat(fnv)
    lead = lambda a: a[None]
    order = ["norm_pre_w", "w_in", "conv_w", "conv_b", "dt_bias", "a_log", "d_skip", "ssm_norm_w", "w_out", "norm_post_w"]
    grads = dict(small_g, w_in=g_w_in, w_out=g_w_out)
    deltas = dict(sd, w_in=d_w_in, w_out=d_w_out)
    new_m = dict(snm, w_in=nm_w_in, w_out=nm_w_out)
    new_v = dict(snv, w_in=nv_w_in, w_out=nv_w_out)

    def shaped(dct, k):
        a = dct[k]
        return lead(a) if k in ("w_in", "w_out", "conv_w") else a

    return (loss, grad_x[None], *[shaped(grads, k) for k in order], *[shaped(deltas, k) for k in order],
            *[shaped(new_m, k) for k in order], *[shaped(new_v, k) for k in order])
```

```python
import functools
import math

import jax
import jax.numpy as jnp
import numpy as np
from jax import lax
from jax.experimental import pallas as pl
from jax.experimental.pallas import tpu as pltpu

f32, bf16 = jnp.float32, jnp.bfloat16
SDS = jax.ShapeDtypeStruct
HIGHEST = lax.Precision.HIGHEST
MESH = pl.DeviceIdType.MESH

N_DEV = 8
D_MODEL = 1024
D_ATTN = 1024
D_SSM = 1024
HEAD_DIM = 64
N_PAIRS = 8
D_STATE = 128
N_GROUPS = 2
D_CONV = D_SSM + 2 * N_GROUPS * D_STATE
D_IN_PROJ = 4 * D_ATTN + D_SSM + D_CONV + 16
NP = 7168
CHUNK = 128
BLK = 128
DILATIONS = (1, 4, 16)
EPS = 1e-6
LANES = 128
COL_Z, COL_XS, COL_BC, COL_DT = 4096, 5120, 6144, 6656

ADAM_LR, ADAM_B1, ADAM_B2, ADAM_EPS, ADAM_WD, ADAM_STEP = 0.001, 0.9, 0.999, 1e-08, 0.01, 10

PACK_ROWS, PACK_W = 16, 1536


def _nt(a, b):
    return lax.dot_general(a, b, (((1,), (1,)), ((), ())), preferred_element_type=f32)


def _tn(a, b):
    return lax.dot_general(a, b, (((0,), (0,)), ((), ())), preferred_element_type=f32)


def _nn(a, b):
    return jnp.dot(a, b, preferred_element_type=f32)


def _nn_hi(a, b):
    return jnp.dot(a, b, precision=HIGHEST, preferred_element_type=f32)


def _sigmoid(x):
    return 1.0 / (1.0 + jnp.exp(-x))


def _softplus(x):
    return jnp.maximum(x, 0.0) + jnp.log1p(jnp.exp(-jnp.abs(x)))


def _iota(shape, dim):
    return lax.broadcasted_iota(jnp.int32, shape, dim)


def _my_pos():
    return lax.axis_index("x"), lax.axis_index("y"), lax.axis_index("c")


def _all_gather(arrs):
    n = len(arrs)
    ns = 9

    def body(*refs):
        ins, outs = refs[:n], refs[n:2 * n]
        send_sems, recv_sems, local_sems = refs[2 * n:]
        x, y, c = _my_pos()
        me, sibling = (x, y, c), (x, y, 1 - c)
        xn, yn, diag = (1 - x, y), (x, 1 - y), (1 - x, 1 - y)

        def slot(a, px, py, pc):
            return outs[a].at[4 * px + 2 * py + pc]

        def part(a, ref, h):
            width = arrs[a].shape[-1]
            if width % (2 * LANES):
                return ref if h == 1 else None
            return ref.at[:, pl.ds(h * (width // 2), width // 2)]

        def copy(a, k, block, to, src=None, h=None):
            src_ref = slot(a, *block) if src is None else src
            dst_ref = slot(a, *block)
            if h is not None:
                src_ref, dst_ref = part(a, src_ref, h), part(a, dst_ref, h)
                if src_ref is None:
                    return None
            return pltpu.make_async_remote_copy(
                src_ref=src_ref, dst_ref=dst_ref, send_sem=send_sems.at[ns * a + k], recv_sem=recv_sems.at[ns * a + k],
                device_id=to, device_id_type=MESH)

        mine = [pltpu.make_async_copy(ins[a], slot(a, *me), local_sems.at[a]) for a in range(n)]
        for cp in mine:
            cp.start()
        sends = []
        for a in range(n):
            sends += [copy(a, 0, me, sibling, src=ins[a]), copy(a, 1, me, (*xn, c), src=ins[a]),
                      copy(a, 2, me, (*yn, c), src=ins[a])]
        for cp in sends:
            cp.start()

        def start(cp):
            if cp is not None:
                cp.start()
                sends.append(cp)

        for a in range(n):
            copy(a, 1, (*xn, c), me).wait_recv()
            start(copy(a, 4, (*xn, c), sibling))
            start(copy(a, 7, (*xn, c), (*yn, c), h=1))
        for a in range(n):
            copy(a, 2, (*yn, c), me).wait_recv()
            start(copy(a, 5, (*yn, c), sibling))
            start(copy(a, 8, (*yn, c), (*xn, c), h=0))
        for a in range(n):
            for k, h in ((8, 0), (7, 1)):
                cp = copy(a, k, (*diag, c), me, h=h)
                if cp is not None:
                    cp.wait_recv()
            start(copy(a, 6, (*diag, c), sibling))
        for a in range(n):
            copy(a, 0, sibling, me).wait_recv()
            for j, chip in enumerate((xn, yn, diag)):
                copy(a, 4 + j, (*chip, 1 - c), me).wait_recv()
        for cp in sends:
            cp.wait_send()
        for cp in mine:
            cp.wait()

    anyspec = pl.BlockSpec(memory_space=pl.ANY)
    return pl.pallas_call(
        body, name="weights_all_gather",
        out_shape=[SDS((N_DEV,) + a.shape, a.dtype) for a in arrs],
        in_specs=[anyspec] * n, out_specs=[anyspec] * n,
        scratch_shapes=[pltpu.SemaphoreType.DMA((ns * n,)), pltpu.SemaphoreType.DMA((ns * n,)),
                        pltpu.SemaphoreType.DMA((n,))],
    )(*arrs)


def _dw_in_swap(a_parts, u, dw_out):
    tile, tk = 1024, 1024
    s = u.shape[0]
    nk = s // tk
    na = len(a_parts)
    offs, counts, ni = _col_blocks(a_parts, tile)

    def body(*refs):
        a_refs, u_ref, dwo_ref = refs[:na], refs[na], refs[na + 1]
        dw_ref, got_ref, goto_ref = refs[na + 2:na + 5]
        acc, local_sems, send_sems, recv_sem, o_send, o_recv = refs[na + 5:]
        i, k = pl.program_id(0), pl.program_id(1)
        x, y, c = _my_pos()
        par = i % 2

        def tile_copies(t, p):
            rows = pl.ds(pl.multiple_of(t * tile, tile), tile)
            loc = pltpu.make_async_copy(acc.at[p], dw_ref.at[rows], local_sems.at[p])
            rem = pltpu.make_async_remote_copy(
                src_ref=acc.at[p], dst_ref=got_ref.at[rows], send_sem=send_sems.at[p], recv_sem=recv_sem,
                device_id=(x, y, 1 - c), device_id_type=MESH)
            return loc, rem

        out_copy = pltpu.make_async_remote_copy(
            src_ref=dwo_ref, dst_ref=goto_ref, send_sem=o_send, recv_sem=o_recv,
            device_id=(x, y, 1 - c), device_id_type=MESH)

        @pl.when(jnp.logical_and(i == 0, k == 0))
        def _():
            out_copy.start()

        @pl.when(k == 0)
        def _():
            @pl.when(i >= 2)
            def _():
                loc, rem = tile_copies(i - 2, par)
                loc.wait()
                rem.wait_send()
            acc[par] = jnp.zeros((tile, tile), f32)

        for t in range(na):
            @pl.when(jnp.logical_and(i >= offs[t], i < offs[t] + counts[t]))
            def _(t=t):
                acc[par] += _tn(a_refs[t][...], u_ref[...])

        @pl.when(k == nk - 1)
        def _():
            loc, rem = tile_copies(i, par)
            loc.start()
            rem.start()

        @pl.when(jnp.logical_and(i == ni - 1, k == nk - 1))
        def _():
            for t in (ni - 2, ni - 1):
                loc, rem = tile_copies(t, t % 2)
                loc.wait()
                rem.wait_send()
            pltpu.make_async_remote_copy(src_ref=dw_ref, dst_ref=got_ref, send_sem=send_sems.at[0], recv_sem=recv_sem,
                                         device_id=(x, y, c), device_id_type=MESH).wait_recv()
            out_copy.wait_send()
            out_copy.wait_recv()

    def a_spec(t):
        def index(i, k):
            mine = jnp.logical_and(i >= offs[t], i < offs[t] + counts[t])
            return jnp.where(mine, k, 0), jnp.clip(i - offs[t], 0, counts[t] - 1)
        return pl.BlockSpec((tk, tile), index)

    anyspec = pl.BlockSpec(memory_space=pl.ANY)
    return pl.pallas_call(
        body, name="dw_in_swap", grid=(ni, nk),
        in_specs=[a_spec(t) for t in range(na)] + [pl.BlockSpec((tk, tile), lambda i, k: (k, 0)), anyspec],
        out_specs=[anyspec] * 3,
        out_shape=[SDS((ni * tile, tile), f32), SDS((ni * tile, tile), f32), SDS(dw_out.shape, dw_out.dtype)],
        scratch_shapes=[pltpu.VMEM((2, tile, tile), f32), pltpu.SemaphoreType.DMA((2,)), pltpu.SemaphoreType.DMA((2,)),
                        pltpu.SemaphoreType.DMA(()), pltpu.SemaphoreType.DMA(()), pltpu.SemaphoreType.DMA(())],
        compiler_params=pltpu.CompilerParams(dimension_semantics=("arbitrary", "arbitrary")),
    )(*a_parts, u, dw_out)


def _gather_small(small):
    def body(small_in, small_out, send_sems, recv_sems, local_sem):
        x, y, c = _my_pos()
        me = 4 * x + 2 * y + c
        mine = pltpu.make_async_copy(small_in, small_out.at[me], local_sem)
        mine.start()
        sends = []
        for k in range(1, N_DEV):
            to = (me + k) % N_DEV
            cp = pltpu.make_async_remote_copy(
                src_ref=small_in, dst_ref=small_out.at[me], send_sem=send_sems.at[k - 1], recv_sem=recv_sems.at[k - 1],
                device_id=(to // 4, (to // 2) % 2, to % 2), device_id_type=MESH)
            cp.start()
            sends.append(cp)
        for k in range(1, N_DEV):
            frm = (me + N_DEV - k) % N_DEV
            pltpu.make_async_remote_copy(
                src_ref=small_in, dst_ref=small_out.at[frm], send_sem=send_sems.at[k - 1], recv_sem=recv_sems.at[k - 1],
                device_id=(x, y, c), device_id_type=MESH).wait_recv()
        for cp in sends:
            cp.wait_send()
        mine.wait()

    anyspec = pl.BlockSpec(memory_space=pl.ANY)
    return pl.pallas_call(
        body, name="small_grads_gather", out_shape=SDS((N_DEV,) + small.shape, small.dtype),
        in_specs=[anyspec], out_specs=anyspec,
        scratch_shapes=[pltpu.SemaphoreType.DMA((7,)), pltpu.SemaphoreType.DMA((7,)), pltpu.SemaphoreType.DMA(())],
    )(small)


def _chip_sum(mine, got, name):
    r, cdim = mine.shape
    tr = 1024

    def body(m_ref, g_ref, s16_ref):
        s16_ref[...] = (m_ref[...] + g_ref[...]).astype(bf16)

    blk = pl.BlockSpec((tr, cdim), lambda i: (i, 0))
    return pl.pallas_call(
        body, name=name, grid=(r // tr,), in_specs=[blk, blk], out_specs=blk, out_shape=SDS((r, cdim), bf16),
        compiler_params=pltpu.CompilerParams(dimension_semantics=("parallel",)),
    )(mine, got)


def _chip_exchange_copies(ins, outs, send_sems, recv_sems, local_sems):
    nb = len(ins)
    x, y, c = _my_pos()
    my_q = 2 * x + y
    mine = [pltpu.make_async_copy(ins[a].at[my_q], outs[a].at[my_q], local_sems.at[a]) for a in range(nb)]
    sends, recvs = [], []
    for k in range(1, 4):
        to, frm = (my_q + k) % 4, (my_q + 4 - k) % 4
        for a in range(nb):
            sems = dict(send_sem=send_sems.at[3 * a + k - 1], recv_sem=recv_sems.at[3 * a + k - 1], device_id_type=MESH)
            sends.append(pltpu.make_async_remote_copy(
                src_ref=ins[a].at[to], dst_ref=outs[a].at[my_q], device_id=(to // 2, to % 2, c), **sems))
            recvs.append(pltpu.make_async_remote_copy(
                src_ref=ins[a].at[frm], dst_ref=outs[a].at[frm], device_id=(x, y, c), **sems))
    return mine, sends, recvs


def _chip_exchange_scratch(nb):
    return [pltpu.SemaphoreType.DMA((3 * nb,)), pltpu.SemaphoreType.DMA((3 * nb,)), pltpu.SemaphoreType.DMA((nb,))]


def _prenorm_inproj(x, nw, wt):
    s, d = x.shape
    npad = wt.shape[0]
    tm, tn = 1024, 1024

    def body(x_ref, nw_ref, w_ref, proj_ref, u_ref):
        @pl.when(pl.program_id(1) == 0)
        def _():
            xv = x_ref[...]
            r = lax.rsqrt(jnp.mean(xv * xv, axis=-1, keepdims=True) + EPS)
            u_ref[...] = (xv * r * nw_ref[...]).astype(bf16)
        proj_ref[...] = _nt(u_ref[...], w_ref[...])

    return pl.pallas_call(
        body, name="prenorm_inproj", grid=(s // tm, npad // tn),
        in_specs=[pl.BlockSpec((tm, d), lambda i, j: (i, 0)), pl.BlockSpec((1, d), lambda i, j: (0, 0)),
                  pl.BlockSpec((tn, d), lambda i, j: (j, 0))],
        out_specs=[pl.BlockSpec((tm, tn), lambda i, j: (i, j)), pl.BlockSpec((tm, d), lambda i, j: (i, 0))],
        out_shape=[SDS((s, npad), f32), SDS((s, d), bf16)],
        compiler_params=pltpu.CompilerParams(dimension_semantics=("parallel", "arbitrary")),
    )(x, nw, wt)


def _attn_consts():
    head0 = _iota((BLK, LANES), 1) < HEAD_DIM
    tri2 = (_iota((BLK, 2 * LANES), 1) % LANES) <= _iota((BLK, 2 * LANES), 0)
    ones2 = ((_iota((LANES, 2 * LANES), 0) < HEAD_DIM) == (_iota((LANES, 2 * LANES), 1) < LANES)).astype(bf16)
    rmat = ((_iota((2 * LANES, LANES), 0) < LANES) == (_iota((2 * LANES, LANES), 1) < HEAD_DIM)).astype(bf16)
    bones = ((_iota((LANES, LANES), 0) < HEAD_DIM) == (_iota((LANES, LANES), 1) < HEAD_DIM)).astype(bf16)
    return head0, tri2, ones2, rmat, bones


def _stack_heads(x16, head0):
    zero = jnp.zeros_like(x16)
    return jnp.concatenate([jnp.where(head0, x16, zero), jnp.where(head0, zero, x16)], axis=0)


def _split_dot(x, w16):
    hi = x.astype(bf16)
    lo = (x - hi.astype(f32)).astype(bf16)
    return _nn(hi, w16) + _nn(lo, w16)


def _bf16_terms(x, terms):
    out = []
    for _ in range(terms):
        t = x.astype(bf16)
        out.append(t)
        x = x - t.astype(f32)
    return out


def _dot_01(x, w16, terms):
    return sum(_nn(t, w16) for t in _bf16_terms(x, terms))


def _dot_01_left(w16, x, terms):
    return sum(_nn(w16, t) for t in _bf16_terms(x, terms))


def _attn_fwd(proj):
    s = proj.shape[0]
    n_it = s // BLK

    def body(q_ref, k_ref, v_ref, g_ref, o_ref, l_ref, mix_ref, op0, op1, op2, lp0, lp1, lp2,
             s_a, s_b, sd_a, sd_b, p_a, p_b, m_a, m_b, pd_a, pd_b):
        op_refs, lp_refs = (op0, op1, op2), (lp0, lp1, lp2)
        head0, tri2, ones2, rmat, _ = _attn_consts()
        score_bufs, prob_bufs = ((s_a, sd_a), (s_b, sd_b)), ((p_a, m_a, pd_a), (p_b, m_b, pd_b))

        def block_rows(i, d, nb):
            r, blk = i // nb, i % nb
            st = blk * (BLK * d) + r
            stp = jnp.maximum(blk - 1, 0) * (BLK * d) + r
            return pl.ds(st, BLK, stride=d), pl.ds(stp, BLK, stride=d), blk > 0

        def scores(i, d, nb, bufs):
            rows, rows_p, has_prev = block_rows(i, d, nb)
            s_buf, sd_buf = bufs
            qs = q_ref[rows, :] * 0.125
            kc, kp = k_ref[rows, :], k_ref[rows_p, :]
            qs16 = qs.astype(bf16)
            sc = _nt(qs16, _stack_heads(kc.astype(bf16), head0))
            sp = _nt(qs16, _stack_heads(kp.astype(bf16), head0))
            s_buf[...] = jnp.where(tri2, sc, jnp.where(has_prev, sp, -jnp.inf))
            sd_buf[...] = jnp.where(has_prev, _split_dot(qs * kp, ones2), -jnp.inf)

        def softmax(bufs_in, bufs_out):
            s_buf, sd_buf = bufs_in
            p_buf, m_buf, pd_buf = bufs_out
            sc, sd2 = s_buf[...], sd_buf[...]
            m0 = jnp.max(sc[:, :LANES], axis=1, keepdims=True)
            m1 = jnp.max(sc[:, LANES:], axis=1, keepdims=True)
            m2 = jnp.concatenate([jnp.broadcast_to(m0, (BLK, LANES)), jnp.broadcast_to(m1, (BLK, LANES))], axis=1)
            m2 = jnp.maximum(m2, sd2)
            p_buf[...] = jnp.exp(sc - m2).astype(bf16)
            m_pair = jnp.where(head0, m2[:, :LANES], m2[:, LANES:])
            m_buf[...] = m_pair
            pd_buf[...] = jnp.exp(jnp.where(head0, sd2[:, :LANES], sd2[:, LANES:]) - m_pair)

        def output(i, d, nb, p, bufs):
            rows, rows_p, _ = block_rows(i, d, nb)
            p_buf, m_buf, pd_buf = bufs
            vc, vp = v_ref[rows, :], v_ref[rows_p, :]
            pt16, pd = p_buf[...], pd_buf[...]
            zero = jnp.zeros_like(pt16)
            o = (_nn(jnp.where(tri2, pt16, zero), _stack_heads(vc.astype(bf16), head0))
                 + _nn(jnp.where(tri2, zero, pt16), _stack_heads(vp.astype(bf16), head0)) + pd * vp)
            l = _nn(pt16, rmat) + pd
            op_refs[p][rows, :] = o / l
            lp_refs[p][rows, :] = m_buf[...] + jnp.log(l)

        for p, d in enumerate(DILATIONS):
            nb = s // (BLK * d)
            scores(0, d, nb, score_bufs[0])
            scores(1, d, nb, score_bufs[1])
            softmax(score_bufs[0], prob_bufs[0])

            def steps(j, carry, d=d, nb=nb, p=p):
                for par in range(2):
                    t = 2 * j + 2 + par
                    scores(t, d, nb, score_bufs[par])
                    output(t - 2, d, nb, p, prob_bufs[par])
                    softmax(score_bufs[1 - par], prob_bufs[1 - par])
                return carry

            lax.fori_loop(0, (n_it - 2) // 2, steps, 0)
            output(n_it - 2, d, nb, p, prob_bufs[0])
            softmax(score_bufs[1], prob_bufs[1])
            output(n_it - 1, d, nb, p, prob_bufs[1])

        def merge(i, carry):
            rows = pl.ds(pl.multiple_of(i * 256, 256), 256)
            l0, l1, l2 = lp0[rows, :], lp1[rows, :], lp2[rows, :]
            m = jnp.maximum(jnp.maximum(l0, l1), l2)
            e0, e1, e2 = jnp.exp(l0 - m), jnp.exp(l1 - m), jnp.exp(l2 - m)
            z = e0 + e1 + e2
            o = (e0 * op0[rows, :] + e1 * op1[rows, :] + e2 * op2[rows, :]) / z
            o_ref[rows, :] = o
            l_ref[rows, :] = m + jnp.log(z)
            g = g_ref[rows, :]
            mix_ref[rows, :] = (o * (g * _sigmoid(g))).astype(bf16)
            return carry

        lax.fori_loop(0, s // 256, merge, 0)

    col = lambda base: pl.BlockSpec((s, LANES), lambda h: (0, base + h))
    return pl.pallas_call(
        body, name="attn_fwd", grid=(N_PAIRS,),
        in_specs=[col(0), col(8), col(16), col(24)],
        out_specs=[col(0), col(0), col(0)],
        out_shape=[SDS((s, D_ATTN), f32), SDS((s, D_ATTN), f32), SDS((s, D_ATTN), bf16)],
        scratch_shapes=[pltpu.VMEM((s, LANES), f32)] * 6 + [pltpu.VMEM((BLK, 2 * LANES), f32)] * 4
        + [pltpu.VMEM((BLK, 2 * LANES), bf16)] * 2 + [pltpu.VMEM((BLK, LANES), f32)] * 4,
        compiler_params=pltpu.CompilerParams(dimension_semantics=("parallel",)),
    )(proj, proj, proj, proj)


def _expand_mat():
    colv = np.arange(2 * D_SSM)
    head = 2 * ((colv % D_SSM) // LANES) + colv // D_SSM
    return jnp.asarray(np.arange(LANES)[:, None] == head[None, :], dtype=bf16)


def _fold_mat():
    return jnp.asarray((np.arange(D_SSM) // HEAD_DIM)[:, None] == np.arange(LANES)[None, :], dtype=bf16)


def _ssd_common(xs_ref, bc_ref, xs_tail, bc_tail, dt_ref, cw_ref, cb_ref, dtb_ref, alog16_ref, emat_ref, xpad, first):
    keep = jnp.where(first, 0.0, 1.0)
    xpad[0:8, 0:D_SSM] = xs_tail[...] * keep
    xpad[0:8, D_SSM:D_CONV] = bc_tail[...] * keep
    xpad[8:8 + CHUNK, 0:D_SSM] = xs_ref[...]
    xpad[8:8 + CHUNK, D_SSM:D_CONV] = bc_ref[...]
    xp = xpad[...]
    taps = [pltpu.roll(xp, 3 - j, 0)[8:8 + CHUNK] for j in range(3)] + [xp[8:8 + CHUNK]]
    cv = cb_ref[...] + cw_ref[0:1, :] * taps[0]
    for j in range(1, 4):
        cv = cv + cw_ref[j:j + 1, :] * taps[j]
    sig = _sigmoid(cv)
    xbc = cv * sig

    pre = dt_ref[...] + dtb_ref[...]
    dt16 = _softplus(pre)
    a16 = -jnp.exp(alog16_ref[...])
    sub, lane = _iota((CHUNK, CHUNK), 0), _iota((CHUNK, CHUNK), 1)
    tri = (sub >= lane).astype(f32)
    al16 = _nn_hi(tri, dt16 * a16)
    al_t = al16.T
    emat = emat_ref[...]
    dt_x = _dot_01(dt16, emat, 3)
    al_x = _dot_01(al16, emat, 3)
    lane_w = _iota((CHUNK, D_SSM), 1)
    even = (lane_w % LANES) < HEAD_DIM
    dt_f = jnp.where(even, dt_x[:, :D_SSM], dt_x[:, D_SSM:])
    al_f = jnp.where(even, al_x[:, :D_SSM], al_x[:, D_SSM:])
    return cv, sig, xbc, pre, dt_f, al_f, al_x, al_t, taps


def _decay_mat(al_x, al_t, pair, h):
    sub, lane = _iota((CHUNK, CHUNK), 0), _iota((CHUNK, CHUNK), 1)
    col = al_x[:, h * D_SSM + pair * LANES: h * D_SSM + (pair + 1) * LANES]
    row = al_t[2 * pair + h: 2 * pair + h + 1, :]
    return jnp.exp(jnp.where(sub >= lane, col - row, -jnp.inf))


def _ssd_in_specs(order):
    blk = lambda w, cb: pl.BlockSpec((CHUNK, w), lambda i: (order(i), cb))
    tail = lambda w, cb: pl.BlockSpec((8, w), lambda i: (jnp.maximum(16 * order(i) - 1, 0), cb))
    return [blk(D_SSM, COL_XS // D_SSM), blk(512, COL_BC // 512), tail(D_SSM, COL_XS // D_SSM),
            tail(512, COL_BC // 512), blk(LANES, COL_DT // LANES), blk(D_SSM, COL_Z // D_SSM)]


def _full(shape):
    return pl.BlockSpec(shape, lambda i: (0,) * len(shape))


def _ssd_fwd(proj, conv_w, conv_b, dtb16, alog16, alog_f, d_f, nw):
    s = proj.shape[0]
    nc = s // CHUNK

    def body(xs_ref, bc_ref, xs_tail, bc_tail, dt_ref, z_ref, cw_ref, cb_ref, dtb_ref, alog16_ref, alogf_ref,
             df_ref, nw_ref, emat_ref, mix_ref, y_ref, st_ref, h_scr, xpad, y_scr):
        c = pl.program_id(0)

        @pl.when(c == 0)
        def _():
            h_scr[...] = jnp.zeros_like(h_scr)

        _, _, xbc, _, dt_f, al_f, al_x, al_t, _ = _ssd_common(
            xs_ref, bc_ref, xs_tail, bc_tail, dt_ref, cw_ref, cb_ref, dtb_ref, alog16_ref, emat_ref, xpad, c == 0)
        head0 = _iota((CHUNK, LANES), 1) < HEAD_DIM
        st_ref[...] = h_scr[...]
        for g in range(N_GROUPS):
            bm = xbc[:, D_SSM + g * D_STATE: D_SSM + (g + 1) * D_STATE].astype(bf16)
            cm = xbc[:, D_SSM + (N_GROUPS + g) * D_STATE: D_SSM + (N_GROUPS + g + 1) * D_STATE].astype(bf16)
            gmat = _nt(cm, bm)
            for pair in range(4 * g, 4 * g + 4):
                sl = slice(pair * LANES, (pair + 1) * LANES)
                xp, dtp, alp = xbc[:, sl], dt_f[:, sl], al_f[:, sl]
                xdt = xp * dtp
                xdt16 = xdt.astype(bf16)
                al_last = alp[CHUNK - 1:CHUNK, :]
                hp = h_scr[:, sl]
                y_off = jnp.exp(alp) * _nn(cm, hp.astype(bf16))
                yd = [_nn((gmat * _decay_mat(al_x, al_t, pair, h)).astype(bf16), xdt16) for h in range(2)]
                y_scr[:, sl] = jnp.where(head0, yd[0], yd[1]) + y_off + df_ref[:, sl] * xp
                st = _tn(bm, (jnp.exp(al_last - alp) * xdt).astype(bf16))
                h_scr[:, sl] = jnp.exp(al_last) * hp + st
        y = y_scr[...]
        y_ref[...] = y
        z = z_ref[...]
        yz = y * (z * _sigmoid(z))
        gw = D_SSM // N_GROUPS
        for g in range(N_GROUPS):
            part = yz[:, g * gw:(g + 1) * gw]
            r = lax.rsqrt(jnp.mean(part * part, axis=-1, keepdims=True) + EPS)
            mix_ref[:, g * gw:(g + 1) * gw] = (part * r * nw_ref[:, g * gw:(g + 1) * gw]).astype(bf16)

    order = lambda i: i
    row = lambda w: pl.BlockSpec((CHUNK, w), lambda i: (i, 0))
    return pl.pallas_call(
        body, name="ssd_fwd", grid=(nc,),
        in_specs=_ssd_in_specs(order) + [_full((4, D_CONV)), _full((1, D_CONV)), _full((1, LANES)), _full((1, LANES)),
                                         _full((1, D_SSM)), _full((1, D_SSM)), _full((1, D_SSM)),
                                         _full((LANES, 2 * D_SSM))],
        out_specs=[row(D_SSM), row(D_SSM), pl.BlockSpec((None, D_STATE, D_SSM), lambda i: (i, 0, 0))],
        out_shape=[SDS((s, D_SSM), bf16), SDS((s, D_SSM), f32), SDS((nc, D_STATE, D_SSM), f32)],
        scratch_shapes=[pltpu.VMEM((D_STATE, D_SSM), f32), pltpu.VMEM((8 + CHUNK, D_CONV), f32),
                        pltpu.VMEM((CHUNK, D_SSM), f32)],
        compiler_params=pltpu.CompilerParams(dimension_semantics=("arbitrary",)),
    )(proj, proj, proj, proj, proj, proj, conv_w, conv_b, dtb16, alog16, alog_f, d_f, nw, _expand_mat())


def _outproj_loss(mix_a, mix_s, wo, x, tgt, npw):
    s, d = x.shape
    tm = 512

    def body(ma_ref, ms_ref, wo_ref, x_ref, t_ref, npw_ref, dmix_ref, dout_ref, dres_ref, acc_ref):
        @pl.when(pl.program_id(0) == 0)
        def _():
            acc_ref[...] = jnp.zeros_like(acc_ref)

        out = _nn(ma_ref[...], wo_ref[0:D_ATTN, :]) + _nn(ms_ref[...], wo_ref[D_ATTN:, :])
        r = lax.rsqrt(jnp.mean(out * out, axis=-1, keepdims=True) + EPS)
        on = out * r
        diff = x_ref[...] + on * npw_ref[...] - t_ref[...]
        dres = diff * (1.0 / d)
        dres_ref[...] = dres
        acc_ref[0:1, :] += jnp.sum(diff * diff, axis=0, keepdims=True)
        acc_ref[1:2, :] += jnp.sum(dres * on, axis=0, keepdims=True)
        dn = dres * npw_ref[...]
        dout = (r * (dn - on * jnp.mean(dn * on, axis=-1, keepdims=True))).astype(bf16)
        dout_ref[...] = dout
        dmix_ref[...] = _nt(dout, wo_ref[...])

    row = lambda w: pl.BlockSpec((tm, w), lambda i: (i, 0))
    return pl.pallas_call(
        body, name="outproj_loss", grid=(s // tm,),
        in_specs=[row(D_ATTN), row(D_SSM), _full((D_ATTN + D_SSM, d)), row(d), row(d), _full((1, d))],
        out_specs=[row(D_ATTN + D_SSM), row(d), row(d), _full((8, d))],
        out_shape=[SDS((s, D_ATTN + D_SSM), f32), SDS((s, d), bf16), SDS((s, d), f32), SDS((8, d), f32)],
        compiler_params=pltpu.CompilerParams(dimension_semantics=("arbitrary",)),
    )(mix_a, mix_s, wo, x, tgt, npw)


def _attn_bwd(proj, o, lb, dmix):
    s = proj.shape[0]
    n_it = s // BLK

    def body(q_ref, k_ref, v_ref, g_ref, o_ref, l_ref, dm_ref, dq_ref, dk_ref, dv_ref, dg_ref,
             dq_acc, dk_acc, dv_acc, do_scr, dl_scr, *bufs):
        head0, tri2, _, _, bones = _attn_consts()

        def pro(i, carry):
            rows = pl.ds(pl.multiple_of(i * 256, 256), 256)
            g = g_ref[rows, :]
            sg = _sigmoid(g)
            dmx = dm_ref[rows, :]
            ov = o_ref[rows, :]
            dg_ref[rows, :] = (dmx * ov * (sg * (1.0 + g * (1.0 - sg)))).astype(bf16)
            do = dmx * (g * sg)
            do_scr[rows, :] = do
            dl_scr[rows, :] = _split_dot(do * ov, bones)
            z = jnp.zeros((256, LANES), f32)
            dq_acc[rows, :] = z
            dk_acc[rows, :] = z
            dv_acc[rows, :] = z
            return carry

        lax.fori_loop(0, s // 256, pro, 0)

        def per_head(t):
            return jnp.concatenate([t[:, :LANES], t[:, LANES:]], axis=0)

        def both_heads(t):
            tr = pltpu.roll(t, HEAD_DIM, 1)
            return jnp.concatenate([jnp.where(head0, t, tr), jnp.where(head0, tr, t)], axis=1)

        mm_bufs = ((bufs[0], bufs[1], bufs[2], bufs[3]), (bufs[4], bufs[5], bufs[6], bufs[7]))
        ds_bufs = ((bufs[8], bufs[9], bufs[10], bufs[11]), (bufs[12], bufs[13], bufs[14], bufs[15]))
        op_bufs = ((bufs[16], bufs[17], bufs[18], bufs[19]), (bufs[20], bufs[21], bufs[22], bufs[23]))

        def block_rows(i, d, nb):
            r, blk = i // nb, i % nb
            st = blk * (BLK * d) + r
            stp = jnp.maximum(blk - 1, 0) * (BLK * d) + r
            return pl.ds(st, BLK, stride=d), pl.ds(stp, BLK, stride=d), blk > 0

        def products(i, d, nb, out, ops):
            rows, rows_p, has_prev = block_rows(i, d, nb)
            s_buf, dp_buf, sd_buf, dpd_buf = out
            kc_buf, kp_buf, q_buf, do_buf = ops
            q = q_ref[rows, :]
            qs = q * 0.125
            kc, kp = k_ref[rows, :], k_ref[rows_p, :]
            vc, vp = v_ref[rows, :], v_ref[rows_p, :]
            do = do_scr[rows, :]
            qs16, do16 = qs.astype(bf16), do.astype(bf16)
            kst_c, kst_p = _stack_heads(kc.astype(bf16), head0), _stack_heads(kp.astype(bf16), head0)
            vst_c, vst_p = _stack_heads(vc.astype(bf16), head0), _stack_heads(vp.astype(bf16), head0)
            kc_buf[...] = kst_c
            kp_buf[...] = kst_p
            q_buf[...] = q.astype(bf16)
            do_buf[...] = do16
            s_buf[...] = jnp.where(tri2, _nt(qs16, kst_c), jnp.where(has_prev, _nt(qs16, kst_p), -jnp.inf))
            dp_buf[...] = jnp.where(tri2, _nt(do16, vst_c), _nt(do16, vst_p))
            sd_buf[...] = _split_dot(qs * kp, bones)
            dpd_buf[...] = _split_dot(do * vp, bones)

        def softmax_grad(i, d, nb, inp, out):
            rows, _, has_prev = block_rows(i, d, nb)
            s_buf, dp_buf, sd_buf, dpd_buf = inp
            p_buf, ds_buf, pd_buf, dsd_buf = out
            lse = l_ref[rows, :]
            dl = dl_scr[rows, :]
            pt = jnp.exp(s_buf[...] - both_heads(lse))
            ds_buf[...] = (pt * (dp_buf[...] - both_heads(dl)) * 0.125).astype(bf16)
            p_buf[...] = pt.astype(bf16)
            pd = jnp.where(has_prev, jnp.exp(sd_buf[...] - lse), 0.0)
            pd_buf[...] = pd
            dsd_buf[...] = pd * (dpd_buf[...] - dl) * 0.125

        def accumulate(i, d, nb, inp, ops):
            rows, rows_p, _ = block_rows(i, d, nb)
            p_buf, ds_buf, pd_buf, dsd_buf = inp
            kc_buf, kp_buf, q_buf, do_buf = ops
            pt16, ds16, pd, dsd = p_buf[...], ds_buf[...], pd_buf[...], dsd_buf[...]
            zero = jnp.zeros_like(pt16)
            dsc, dsp = jnp.where(tri2, ds16, zero), jnp.where(tri2, zero, ds16)
            pc, pp = jnp.where(tri2, pt16, zero), jnp.where(tri2, zero, pt16)
            kst_c, kst_p, q16, do16 = kc_buf[...], kp_buf[...], q_buf[...], do_buf[...]
            kp16 = kst_p[:BLK] + kst_p[BLK:]
            qst, dost = _stack_heads(q16, head0), _stack_heads(do16, head0)
            dq_acc[rows, :] += _nn(dsc, kst_c) + _nn(dsp, kst_p) + dsd * kp16.astype(f32)
            dk_acc[rows, :] += _tn(per_head(dsc), qst)
            dv_acc[rows, :] += _tn(per_head(pc), dost)
            dk_acc[rows_p, :] += _tn(per_head(dsp), qst) + dsd * q16.astype(f32)
            dv_acc[rows_p, :] += _tn(per_head(pp), dost) + pd * do16.astype(f32)

        for d in DILATIONS:
            nb = s // (BLK * d)
            products(0, d, nb, mm_bufs[0], op_bufs[0])
            products(1, d, nb, mm_bufs[1], op_bufs[1])
            softmax_grad(0, d, nb, mm_bufs[0], ds_bufs[0])

            def steps(j, carry, d=d, nb=nb):
                for par in range(2):
                    t = 2 * j + 2 + par
                    accumulate(t - 2, d, nb, ds_bufs[par], op_bufs[par])
                    products(t, d, nb, mm_bufs[par], op_bufs[par])
                    softmax_grad(t - 1, d, nb, mm_bufs[1 - par], ds_bufs[1 - par])
                return carry

            lax.fori_loop(0, (n_it - 2) // 2, steps, 0)
            accumulate(n_it - 2, d, nb, ds_bufs[0], op_bufs[0])
            softmax_grad(n_it - 1, d, nb, mm_bufs[1], ds_bufs[1])
            accumulate(n_it - 1, d, nb, ds_bufs[1], op_bufs[1])

        def epi(i, carry):
            rows = pl.ds(pl.multiple_of(i * 256, 256), 256)
            dq_ref[rows, :] = dq_acc[rows, :].astype(bf16)
            dk_ref[rows, :] = dk_acc[rows, :].astype(bf16)
            dv_ref[rows, :] = dv_acc[rows, :].astype(bf16)
            return carry

        lax.fori_loop(0, s // 256, epi, 0)

    col = lambda base: pl.BlockSpec((s, LANES), lambda h: (0, base + h))
    outs = pl.pallas_call(
        body, name="attn_bwd", grid=(N_PAIRS,),
        in_specs=[col(0), col(8), col(16), col(24), col(0), col(0), col(0)],
        out_specs=[col(0)] * 4,
        out_shape=[SDS((s, D_ATTN), bf16)] * 4,
        scratch_shapes=[pltpu.VMEM((s, LANES), f32)] * 5
        + [pltpu.VMEM((BLK, 2 * LANES), f32)] * 2 + [pltpu.VMEM((BLK, LANES), f32)] * 2
        + [pltpu.VMEM((BLK, 2 * LANES), f32)] * 2 + [pltpu.VMEM((BLK, LANES), f32)] * 2
        + [pltpu.VMEM((BLK, 2 * LANES), bf16)] * 2 + [pltpu.VMEM((BLK, LANES), f32)] * 2
        + [pltpu.VMEM((BLK, 2 * LANES), bf16)] * 2 + [pltpu.VMEM((BLK, LANES), f32)] * 2
        + [pltpu.VMEM((2 * BLK, LANES), bf16)] * 2 + [pltpu.VMEM((BLK, LANES), bf16)] * 2
        + [pltpu.VMEM((2 * BLK, LANES), bf16)] * 2 + [pltpu.VMEM((BLK, LANES), bf16)] * 2,
        compiler_params=pltpu.CompilerParams(dimension_semantics=("parallel",)),
    )(proj, proj, proj, proj, o, lb, dmix)
    return outs


def _ssd_bwd(proj, y, states, dmix, conv_w, conv_b, dtb16, alog16, alog_f, d_f, nw):
    s = proj.shape[0]
    nc = s // CHUNK
    gw = D_SSM // N_GROUPS

    def body(xs_ref, bc_ref, xs_tail, bc_tail, dt_ref, z_ref, y_ref, st_ref, dm_ref, cw_ref, cb_ref, dtb_ref,
             alog16_ref, alogf_ref, df_ref, nw_ref, emat_ref, fold_ref, out_ref, gconv_ref, gvec_ref, gdt_ref,
             dh_scr, head_scr, xpad, dcpad, da_scr, dxdt_scr, dbc_scr):
        i = pl.program_id(0)
        c = nc - 1 - i

        @pl.when(i == 0)
        def _():
            dh_scr[...] = jnp.zeros_like(dh_scr)
            head_scr[...] = jnp.zeros_like(head_scr)
            gconv_ref[...] = jnp.zeros_like(gconv_ref)
            gvec_ref[...] = jnp.zeros_like(gvec_ref)
            gdt_ref[...] = jnp.zeros_like(gdt_ref)

        cv, sig, xbc, pre, dt_f, al_f, al_x, al_t, taps = _ssd_common(
            xs_ref, bc_ref, xs_tail, bc_tail, dt_ref, cw_ref, cb_ref, dtb_ref, alog16_ref, emat_ref, xpad, c == 0)
        head0 = _iota((CHUNK, LANES), 1) < HEAD_DIM
        sub = _iota((CHUNK, LANES), 0)
        last_row = sub == CHUNK - 1

        yv, z, dmx = y_ref[...], z_ref[...], dm_ref[...]
        sz = _sigmoid(z)
        silu = z * sz
        yz = yv * silu
        dyz_parts = []
        for g in range(N_GROUPS):
            gs = slice(g * gw, (g + 1) * gw)
            part = yz[:, gs]
            r = lax.rsqrt(jnp.mean(part * part, axis=-1, keepdims=True) + EPS)
            nh = part * r
            gvec_ref[0:1, gs] += jnp.sum(dmx[:, gs] * nh, axis=0, keepdims=True)
            dn = dmx[:, gs] * nw_ref[:, gs]
            dyz_parts.append(r * (dn - nh * jnp.mean(dn * nh, axis=-1, keepdims=True)))
        dyz = jnp.concatenate(dyz_parts, axis=1)
        dy = dyz * silu
        out_ref[:, 0:D_SSM] = (dyz * yv * (sz * (1.0 + z * (1.0 - sz)))).astype(bf16)

        x_all = xbc[:, 0:D_SSM]
        gvec_ref[2:3, :] += jnp.sum(dy * x_all, axis=0, keepdims=True)

        for g in range(N_GROUPS):
            bm = xbc[:, D_SSM + g * D_STATE: D_SSM + (g + 1) * D_STATE].astype(bf16)
            cm = xbc[:, D_SSM + (N_GROUPS + g) * D_STATE: D_SSM + (N_GROUPS + g + 1) * D_STATE].astype(bf16)
            gmat = _nt(cm, bm)
            dgm = jnp.zeros((CHUNK, CHUNK), f32)
            db = jnp.zeros((CHUNK, D_STATE), f32)
            dc = jnp.zeros((CHUNK, D_STATE), f32)
            for pair in range(4 * g, 4 * g + 4):
                sl = slice(pair * LANES, (pair + 1) * LANES)
                xp, dtp, alp, dyp = x_all[:, sl], dt_f[:, sl], al_f[:, sl], dy[:, sl]
                xdt = xp * dtp
                xdt16 = xdt.astype(bf16)
                al_last = alp[CHUNK - 1:CHUNK, :]
                e_l = jnp.exp(alp)
                wf = jnp.exp(al_last - alp)
                e_last = jnp.exp(al_last)
                hp = st_ref[:, sl]
                hp16 = hp.astype(bf16)
                dhn = dh_scr[:, sl]
                dhn16 = dhn.astype(bf16)
                y_off = e_l * _nn(cm, hp16)
                dch16 = (dyp * e_l).astype(bf16)
                dc = dc + _nt(dch16, hp16)
                dh_out = _tn(cm, dch16)
                dal = dyp * y_off
                xw16 = (wf * xdt).astype(bf16)
                db = db + _nt(xw16, dhn16)
                dxw = _nn(bm, dhn16)
                dxdt = dxw * wf
                dwf = dxw * xdt * wf
                dal = dal - dwf
                dal_last = jnp.sum(dwf, axis=0, keepdims=True) + jnp.sum(dhn * hp, axis=0, keepdims=True) * e_last
                dh_scr[:, sl] = e_last * dhn + dh_out
                for h in range(2):
                    mh = head0 if h == 0 else jnp.logical_not(head0)
                    dyh16 = jnp.where(mh, dyp, 0.0).astype(bf16)
                    lmat = _decay_mat(al_x, al_t, pair, h)
                    mm = gmat * lmat
                    dmm = _nt(dyh16, xdt16)
                    dxdt = dxdt + _tn(mm.astype(bf16), dyh16)
                    n16 = (dmm * mm).astype(bf16)
                    jh = jnp.where(mh, 1.0 / HEAD_DIM, 0.0).astype(bf16)
                    dal = dal + _nn(n16, jh) - _tn(n16, jh)
                    dgm = dgm + dmm * lmat
                da_scr[:, sl] = dal + jnp.where(last_row, dal_last, 0.0)
                dxdt_scr[:, sl] = dxdt
            dgm16 = dgm.astype(bf16)
            dbc_scr[:, g * D_STATE:(g + 1) * D_STATE] = db + _tn(dgm16, cm)
            dbc_scr[:, (N_GROUPS + g) * D_STATE:(N_GROUPS + g + 1) * D_STATE] = dc + _nn(dgm16, bm)

        sub_c, lane_c = _iota((CHUNK, CHUNK), 0), _iota((CHUNK, CHUNK), 1)
        tri_t = (lane_c >= sub_c).astype(bf16)
        dadt = _dot_01_left(tri_t, da_scr[...], 2)
        a_f = -jnp.exp(alogf_ref[...])
        dxdt_all = dxdt_scr[...]
        ddt_f = dxdt_all * x_all + a_f * dadt
        gvec_ref[1:2, :] += jnp.sum(dt_f * dadt, axis=0, keepdims=True) * a_f
        dx = df_ref[...] * dy + dxdt_all * dt_f
        ddt_raw = _dot_01(ddt_f, fold_ref[...], 2) * _sigmoid(pre)
        gdt_ref[0:1, :] += jnp.sum(ddt_raw, axis=0, keepdims=True)
        out_ref[:, D_SSM + D_CONV:D_SSM + D_CONV + LANES] = ddt_raw.astype(bf16)
        out_ref[:, D_SSM + D_CONV + LANES:] = jnp.zeros((CHUNK, 3 * LANES), bf16)

        dsil = sig * (1.0 + cv * (1.0 - sig))
        dcv_x = dx * dsil[:, 0:D_SSM]
        dcv_bc = dbc_scr[...] * dsil[:, D_SSM:]
        dcpad[0:CHUNK, 0:D_SSM] = dcv_x
        dcpad[0:CHUNK, D_SSM:] = dcv_bc
        dcpad[CHUNK:, :] = head_scr[...]
        dcp = dcpad[...]
        dcv = dcp[0:CHUNK]
        gconv_ref[4:5, :] += jnp.sum(dcv, axis=0, keepdims=True)
        draw = cw_ref[3:4, :] * dcv
        for j in range(4):
            gconv_ref[j:j + 1, :] += jnp.sum(dcv * taps[j], axis=0, keepdims=True)
        for j in range(3):
            draw = draw + cw_ref[j:j + 1, :] * pltpu.roll(dcp, CHUNK + 8 - (3 - j), 0)[0:CHUNK]
        head_scr[...] = dcv[0:8]
        out_ref[:, D_SSM:D_SSM + D_CONV] = draw.astype(bf16)

    order = lambda i: nc - 1 - i
    row = lambda w, cb=0: pl.BlockSpec((CHUNK, w), lambda i: (nc - 1 - i, cb))
    return pl.pallas_call(
        body, name="ssd_bwd", grid=(nc,),
        in_specs=_ssd_in_specs(order) + [row(D_SSM), pl.BlockSpec((None, D_STATE, D_SSM), lambda i: (nc - 1 - i, 0, 0)),
                                         row(D_SSM, 1), _full((4, D_CONV)), _full((1, D_CONV)), _full((1, LANES)),
                                         _full((1, LANES)), _full((1, D_SSM)), _full((1, D_SSM)), _full((1, D_SSM)),
                                         _full((LANES, 2 * D_SSM)), _full((D_SSM, LANES))],
        out_specs=[row(3072), _full((8, D_CONV)), _full((8, D_SSM)), _full((8, LANES))],
        out_shape=[SDS((s, 3072), bf16), SDS((8, D_CONV), f32), SDS((8, D_SSM), f32), SDS((8, LANES), f32)],
        scratch_shapes=[pltpu.VMEM((D_STATE, D_SSM), f32), pltpu.VMEM((8, D_CONV), f32),
                        pltpu.VMEM((8 + CHUNK, D_CONV), f32), pltpu.VMEM((8 + CHUNK, D_CONV), f32),
                        pltpu.VMEM((CHUNK, D_SSM), f32), pltpu.VMEM((CHUNK, D_SSM), f32),
                        pltpu.VMEM((CHUNK, 2 * N_GROUPS * D_STATE), f32)],
        compiler_params=pltpu.CompilerParams(dimension_semantics=("arbitrary",)),
    )(proj, proj, proj, proj, proj, proj, y, states, dmix, conv_w, conv_b, dtb16, alog16, alog_f, d_f, nw,
      _expand_mat(), _fold_mat())


def _col_blocks(parts, tile):
    counts = [p.shape[1] // tile for p in parts]
    offs = [sum(counts[:t]) for t in range(len(parts))]
    return offs, counts, sum(counts)


def _inproj_bwd(dparts, wt, x, nw, dres, chip_sums):
    s, d = x.shape
    tm, tk = 1024, 1024
    offs, counts, nk = _col_blocks(dparts, tk)
    npart, nx = len(dparts), len(chip_sums)
    ni = s // tm

    def body(*refs):
        dp_refs = refs[:npart]
        w_ref, x_ref, nw_ref, dres_ref = refs[npart:npart + 4]
        cs_in = refs[npart + 4:npart + 4 + nx]
        gx_ref, gnw_ref = refs[npart + 4 + nx:npart + 6 + nx]
        cs_out = refs[npart + 6 + nx:npart + 6 + 2 * nx]
        acc, send_sems, recv_sems, local_sems = refs[npart + 6 + 2 * nx:]
        i, k = pl.program_id(0), pl.program_id(1)

        @pl.when(jnp.logical_and(i == 0, k == 0))
        def _():
            gnw_ref[...] = jnp.zeros_like(gnw_ref)
            if nx:
                mine, sends, _ = _chip_exchange_copies(cs_in, cs_out, send_sems, recv_sems, local_sems)
                for cp in mine + sends:
                    cp.start()

        @pl.when(jnp.logical_and(i == ni - 1, k == nk - 1))
        def _():
            if nx:
                mine, sends, recvs = _chip_exchange_copies(cs_in, cs_out, send_sems, recv_sems, local_sems)
                for cp in recvs:
                    cp.wait_recv()
                for cp in sends:
                    cp.wait_send()
                for cp in mine:
                    cp.wait()

        @pl.when(k == 0)
        def _():
            acc[...] = jnp.zeros_like(acc)

        for t in range(npart):
            @pl.when(jnp.logical_and(k >= offs[t], k < offs[t] + counts[t]))
            def _(t=t):
                acc[...] += _nn(dp_refs[t][...], w_ref[...])

        @pl.when(k == nk - 1)
        def _():
            xv = x_ref[...]
            r = lax.rsqrt(jnp.mean(xv * xv, axis=-1, keepdims=True) + EPS)
            xn = xv * r
            du = acc[...]
            gnw_ref[0:1, :] += jnp.sum(du * xn, axis=0, keepdims=True)
            dn = du * nw_ref[...]
            gx_ref[...] = dres_ref[...] + r * (dn - xn * jnp.mean(dn * xn, axis=-1, keepdims=True))

    def piece(t):
        return pl.BlockSpec((tm, tk), lambda i, k: (i, jnp.clip(k - offs[t], 0, counts[t] - 1)))

    anyspec = pl.BlockSpec(memory_space=pl.ANY)
    outs = pl.pallas_call(
        body, name="inproj_bwd", grid=(ni, nk),
        in_specs=[piece(t) for t in range(npart)] + [
            pl.BlockSpec((tk, d), lambda i, k: (k, 0)),
            pl.BlockSpec((tm, d), lambda i, k: (i, 0)), pl.BlockSpec((1, d), lambda i, k: (0, 0)),
            pl.BlockSpec((tm, d), lambda i, k: (i, 0))] + [anyspec] * nx,
        out_specs=[pl.BlockSpec((tm, d), lambda i, k: (i, 0)), pl.BlockSpec((8, d), lambda i, k: (0, 0))] + [anyspec] * nx,
        out_shape=[SDS((s, d), f32), SDS((8, d), f32)] + [SDS(a.shape, a.dtype) for a in chip_sums],
        scratch_shapes=[pltpu.VMEM((tm, d), f32)] + _chip_exchange_scratch(max(nx, 1)),
        compiler_params=pltpu.CompilerParams(dimension_semantics=("arbitrary", "arbitrary")),
    )(*dparts, wt, x, nw, dres, *chip_sums)
    return outs[0], outs[1], outs[2:]


def _matmul_tn(a_parts, b_parts, name):
    tile, tk = 1024, 1024
    s = a_parts[0].shape[0]
    nk = s // tk
    na, nb = len(a_parts), len(b_parts)
    offs_a, counts_a, ni = _col_blocks(a_parts, tile)
    offs_b, counts_b, nj = _col_blocks(b_parts, tile)

    def body(*refs):
        a_refs, b_refs, o_ref = refs[:na], refs[na:na + nb], refs[na + nb]
        i, j = pl.program_id(0), pl.program_id(1)

        @pl.when(pl.program_id(2) == 0)
        def _():
            o_ref[...] = jnp.zeros_like(o_ref)

        for ta in range(na):
            for tb in range(nb):
                in_a = jnp.logical_and(i >= offs_a[ta], i < offs_a[ta] + counts_a[ta])
                in_b = jnp.logical_and(j >= offs_b[tb], j < offs_b[tb] + counts_b[tb])

                @pl.when(jnp.logical_and(in_a, in_b))
                def _(ta=ta, tb=tb):
                    o_ref[...] += _tn(a_refs[ta][...], b_refs[tb][...])

    def spec(offs, counts, t, axis):
        def index(i, j, k):
            pos = (i, j)[axis]
            mine = jnp.logical_and(pos >= offs[t], pos < offs[t] + counts[t])
            return jnp.where(mine, k, 0), jnp.clip(pos - offs[t], 0, counts[t] - 1)
        return pl.BlockSpec((tk, tile), index)

    return pl.pallas_call(
        body, name=name, grid=(ni, nj, nk),
        in_specs=[spec(offs_a, counts_a, t, 0) for t in range(na)] + [spec(offs_b, counts_b, t, 1) for t in range(nb)],
        out_specs=pl.BlockSpec((tile, tile), lambda i, j, k: (i, j)),
        out_shape=SDS((ni * tile, nj * tile), f32),
        compiler_params=pltpu.CompilerParams(dimension_semantics=("parallel", "parallel", "arbitrary")),
    )(*a_parts, *b_parts)


def _adamw(w, g, m, v):
    m = ADAM_B1 * m + (1.0 - ADAM_B1) * g
    v = ADAM_B2 * v + (1.0 - ADAM_B2) * (g * g)
    m_hat = m / (1.0 - ADAM_B1 ** ADAM_STEP)
    v_hat = v / (1.0 - ADAM_B2 ** ADAM_STEP)
    delta = -ADAM_LR * (m_hat / (jnp.sqrt(v_hat) + ADAM_EPS) + ADAM_WD * w)
    return delta, m, v


def _sum_adamw(parts, w, m, v, name):
    r, c = w.shape
    tc = 256

    def body(p_ref, w_ref, m_ref, v_ref, g_ref, d_ref, nm_ref, nv_ref):
        g = p_ref[0].astype(f32)
        for q in range(1, 4):
            g = g + p_ref[q].astype(f32)
        g_ref[...] = g
        d_ref[...], nm_ref[...], nv_ref[...] = _adamw(w_ref[...], g, m_ref[...], v_ref[...])

    blk = pl.BlockSpec((r, tc), lambda i: (0, i))
    return pl.pallas_call(
        body, name=name, grid=(c // tc,),
        in_specs=[pl.BlockSpec((4, r, tc), lambda i: (0, 0, i)), blk, blk, blk],
        out_specs=[blk] * 4, out_shape=[SDS((r, c), f32)] * 4,
        compiler_params=pltpu.CompilerParams(dimension_semantics=("parallel",)),
    )(parts, w, m, v)


def _sum_small(parts):
    def body(p_ref, o_ref):
        t = p_ref[0]
        for j in range(1, N_DEV):
            t = t + p_ref[j]
        o_ref[...] = t
        row_h = _iota((D_SSM, LANES), 0) // HEAD_DIM
        fold = (row_h == _iota((D_SSM, LANES), 1)).astype(f32)
        lower = t[8:16, 0:LANES]
        folded = _nn_hi(t[8:16, 0:D_SSM], fold)
        loss = jnp.sum(t[11:12, 0:D_MODEL], axis=1, keepdims=True) * (0.5 / D_MODEL)
        row = _iota((8, LANES), 0)
        o_ref[8:16, 0:LANES] = jnp.where(row < 2, folded, jnp.where(row == 4, loss, lower))

    return pl.pallas_call(body, name="sum_small", out_shape=SDS((PACK_ROWS, PACK_W), f32),
                          in_specs=[pl.BlockSpec(memory_space=pltpu.VMEM)],
                          out_specs=pl.BlockSpec(memory_space=pltpu.VMEM))(parts)


def _adamw_small(w, g, m, v):
    def body(w_ref, g_ref, m_ref, v_ref, d_ref, nm_ref, nv_ref):
        d_ref[...], nm_ref[...], nv_ref[...] = _adamw(w_ref[...], g_ref[...], m_ref[...], v_ref[...])

    vm = pl.BlockSpec(memory_space=pltpu.VMEM)
    return pl.pallas_call(body, name="adamw_small", out_shape=[SDS(w.shape, f32)] * 3,
                          in_specs=[vm] * 4, out_specs=[vm] * 3)(w, g, m, v)


def _pad_lanes(v, width):
    return jnp.pad(v, ((0, 0), (0, width - v.shape[1])))


def _local_step(x, tgt, norm_pre_w, wt, conv_w, conv_b, dt_bias, a_log, d_skip, ssm_norm_w, wo, norm_post_w,
                weight_grads):
    dtb16 = _pad_lanes(dt_bias, LANES)
    alog16 = _pad_lanes(a_log, LANES)
    alog_f = jnp.repeat(a_log, HEAD_DIM, axis=1)
    d_f = jnp.repeat(d_skip, HEAD_DIM, axis=1)

    proj, u = _prenorm_inproj(x, norm_pre_w, wt)
    o, lb, mix_a = _attn_fwd(proj)
    mix_s, y, states = _ssd_fwd(proj, conv_w, conv_b, dtb16, alog16, alog_f, d_f, ssm_norm_w)
    dmix, dout, dres, acc_post = _outproj_loss(mix_a, mix_s, wo, x, tgt, norm_post_w)
    dq, dk, dv, dg = _attn_bwd(proj, o, lb, dmix)
    dzxd, g_conv, g_vec, g_dt = _ssd_bwd(proj, y, states, dmix, conv_w, conv_b, dtb16, alog16, alog_f, d_f, ssm_norm_w)
    dparts = [dq, dk, dv, dg, dzxd]
    dw_out = _matmul_tn([mix_a, mix_s], [dout], "dw_out")
    chip_sums, carry = weight_grads(dparts, u, dw_out)
    grad_x, g_pre, exchanged = _inproj_bwd(dparts, wt, x, norm_pre_w, dres, chip_sums)

    rows = [g_conv[0:5], _pad_lanes(g_pre[0:1], PACK_W), _pad_lanes(g_vec[0:1], PACK_W),
            _pad_lanes(acc_post[1:2], PACK_W), _pad_lanes(g_vec[1:3], PACK_W), _pad_lanes(g_dt[0:1], PACK_W),
            _pad_lanes(acc_post[0:1], PACK_W), jnp.zeros((4, PACK_W), f32)]
    return grad_x, carry, exchanged, jnp.concatenate(rows, axis=0)


def kernel(x, norm_pre_w, w_in, conv_w, conv_b, dt_bias, a_log, d_skip, ssm_norm_w, w_out, norm_post_w, loss_target, m_norm_pre_w, m_w_in, m_conv_w, m_conv_b, m_dt_bias, m_a_log, m_d_skip, m_ssm_norm_w, m_w_out, m_norm_post_w, v_norm_pre_w, v_w_in, v_conv_w, v_conv_b, v_dt_bias, v_a_log, v_d_skip, v_ssm_norm_w, v_w_out, v_norm_post_w):
    shard_in = w_in.shape[2]
    shard_cv = conv_w.shape[2]
    me = 4 * lax.axis_index("x") + 2 * lax.axis_index("y") + lax.axis_index("c")

    g_in, g_out, g_cw = _all_gather([w_in[0].T.astype(bf16), w_out[0].astype(bf16), conv_w[0]])
    wt = jnp.pad(g_in.reshape(N_DEV * shard_in, D_MODEL), ((0, NP - N_DEV * shard_in), (0, 0)))
    wo = g_out.reshape(N_DEV * w_out.shape[1], D_MODEL)
    cw = g_cw.transpose(1, 0, 2).reshape(4, D_CONV)

    def weight_grads(dparts, u, dw_out):
        dw_in, got_in, got_out = _dw_in_swap(dparts, u, dw_out)
        sum_in = _chip_sum(dw_in, got_in, "chip_sum_w_in")[:N_DEV * shard_in].reshape(4, 2, shard_in, D_MODEL)
        sum_out = _chip_sum(dw_out, got_out, "chip_sum_w_out").reshape(4, 2, w_out.shape[1], D_MODEL)
        core = lax.axis_index("c")
        return [lax.dynamic_index_in_dim(a, core, axis=1, keepdims=False) for a in (sum_in, sum_out)], ()

    grad_x, _, (parts_in, parts_out), pack = _local_step(
        x[0], loss_target[0], norm_pre_w, wt, cw, conv_b, dt_bias, a_log, d_skip, ssm_norm_w, wo, norm_post_w,
        weight_grads)
    parts_small = _gather_small(pack)

    g_w_in, d_w_in, nm_w_in, nv_w_in = (a.T for a in _sum_adamw(
        parts_in, w_in[0].T, m_w_in[0].T, v_w_in[0].T, "sum_adamw_w_in"))
    g_w_out, d_w_out, nm_w_out, nv_w_out = _sum_adamw(parts_out, w_out[0], m_w_out[0], v_w_out[0], "sum_adamw_w_out")
    tot = _sum_small(parts_small)

    g_cw_all = tot[0:4]
    small_g = {
        "conv_w": lax.dynamic_slice(g_cw_all, (0, me * shard_cv), (4, shard_cv)),
        "conv_b": tot[4:5], "norm_pre_w": tot[5:6, :D_MODEL], "ssm_norm_w": tot[6:7, :D_SSM],
        "norm_post_w": tot[7:8, :D_MODEL], "a_log": tot[8:9, :16], "d_skip": tot[9:10, :16], "dt_bias": tot[10:11, :16],
    }
    loss = tot[12, 0]
    small_w = {"conv_w": (conv_w[0], m_conv_w[0], v_conv_w[0]), "conv_b": (conv_b, m_conv_b, v_conv_b),
               "norm_pre_w": (norm_pre_w, m_norm_pre_w, v_norm_pre_w), "ssm_norm_w": (ssm_norm_w, m_ssm_norm_w, v_ssm_norm_w),
               "norm_post_w": (norm_post_w, m_norm_post_w, v_norm_post_w), "a_log": (a_log, m_a_log, v_a_log),
               "d_skip": (d_skip, m_d_skip, v_d_skip), "dt_bias": (dt_bias, m_dt_bias, v_dt_bias)}
    names = list(small_w)
    sizes = [small_g[k].size for k in names]
    tot_size = sum(sizes)
    pad_to = -(-tot_size // 1024) * 1024

    def flat(arrs):
        v = jnp.concatenate([a.reshape(-1) for a in arrs])
        return jnp.pad(v, (0, pad_to - tot_size)).reshape(pad_to // LANES, LANES)

    fw = flat([small_w[k][0] for k in names])
    fg = flat([small_g[k] for k in names])
    fm = flat([small_w[k][1] for k in names])
    fv = jnp.pad(jnp.concatenate([small_w[k][2].reshape(-1) for k in names]), (0, pad_to - tot_size),
                 constant_values=1.0).reshape(pad_to // LANES, LANES)
    fd, fnm, fnv = _adamw_small(fw, fg, fm, fv)

    def unflat(f):
        out, off = {}, 0
        v = f.reshape(-1)
        for k, n in zip(names, sizes):
            out[k] = v[off:off + n].reshape(small_g[k].shape)
            off += n
        return out

    sd, snm, snv = unflat(fd), unflat(fnm), unflat(fnv)
    lead = lambda a: a[None]
    order = ["norm_pre_w", "w_in", "conv_w", "conv_b", "dt_bias", "a_log", "d_skip", "ssm_norm_w", "w_out", "norm_post_w"]
    grads = dict(small_g, w_in=g_w_in, w_out=g_w_out)
    deltas = dict(sd, w_in=d_w_in, w_out=d_w_out)
    new_m = dict(snm, w_in=nm_w_in, w_out=nm_w_out)
    new_v = dict(snv, w_in=nv_w_in, w_out=nv_w_out)

    def shaped(dct, k):
        a = dct[k]
        return lead(a) if k in ("w_in", "w_out", "conv_w") else a

    return (loss, grad_x[None], *[shaped(grads, k) for k in order], *[shaped(deltas, k) for k in order],
            *[shaped(new_m, k) for k in order], *[shaped(new_v, k) for k in order])
```

```python
import functools
import math

import jax
import jax.numpy as jnp
import numpy as np
from jax import lax
from jax.experimental import pallas as pl
from jax.experimental.pallas import tpu as pltpu

f32, bf16 = jnp.float32, jnp.bfloat16
SDS = jax.ShapeDtypeStruct
HIGHEST = lax.Precision.HIGHEST
MESH = pl.DeviceIdType.MESH

N_DEV = 8
D_MODEL = 1024
D_ATTN = 1024
D_SSM = 1024
HEAD_DIM = 64
N_PAIRS = 8
D_STATE = 128
N_GROUPS = 2
D_CONV = D_SSM + 2 * N_GROUPS * D_STATE
D_IN_PROJ = 4 * D_ATTN + D_SSM + D_CONV + 16
NP = 7168
CHUNK = 128
BLK = 128
DILATIONS = (1, 4, 16)
EPS = 1e-6
LANES = 128
COL_Z, COL_XS, COL_BC, COL_DT = 4096, 5120, 6144, 6656

ADAM_LR, ADAM_B1, ADAM_B2, ADAM_EPS, ADAM_WD, ADAM_STEP = 0.001, 0.9, 0.999, 1e-08, 0.01, 10

PACK_ROWS, PACK_W = 16, 1536


def _nt(a, b):
    return lax.dot_general(a, b, (((1,), (1,)), ((), ())), preferred_element_type=f32)


def _tn(a, b):
    return lax.dot_general(a, b, (((0,), (0,)), ((), ())), preferred_element_type=f32)


def _nn(a, b):
    return jnp.dot(a, b, preferred_element_type=f32)


def _nn_hi(a, b):
    return jnp.dot(a, b, precision=HIGHEST, preferred_element_type=f32)


def _sigmoid(x):
    return 1.0 / (1.0 + jnp.exp(-x))


def _softplus(x):
    return jnp.maximum(x, 0.0) + jnp.log1p(jnp.exp(-jnp.abs(x)))


def _iota(shape, dim):
    return lax.broadcasted_iota(jnp.int32, shape, dim)


def _my_pos():
    return lax.axis_index("x"), lax.axis_index("y"), lax.axis_index("c")


def _all_gather(arrs):
    n = len(arrs)
    ns = 9

    def body(*refs):
        ins, outs = refs[:n], refs[n:2 * n]
        send_sems, recv_sems, local_sems = refs[2 * n:]
        x, y, c = _my_pos()
        me, sibling = (x, y, c), (x, y, 1 - c)
        xn, yn, diag = (1 - x, y), (x, 1 - y), (1 - x, 1 - y)

        def slot(a, px, py, pc):
            return outs[a].at[4 * px + 2 * py + pc]

        def part(a, ref, h):
            width = arrs[a].shape[-1]
            if width % (2 * LANES):
                return ref if h == 1 else None
            return ref.at[:, pl.ds(h * (width // 2), width // 2)]

        def copy(a, k, block, to, src=None, h=None):
            src_ref = slot(a, *block) if src is None else src
            dst_ref = slot(a, *block)
            if h is not None:
                src_ref, dst_ref = part(a, src_ref, h), part(a, dst_ref, h)
                if src_ref is None:
                    return None
            return pltpu.make_async_remote_copy(
                src_ref=src_ref, dst_ref=dst_ref, send_sem=send_sems.at[ns * a + k], recv_sem=recv_sems.at[ns * a + k],
                device_id=to, device_id_type=MESH)

        mine = [pltpu.make_async_copy(ins[a], slot(a, *me), local_sems.at[a]) for a in range(n)]
        for cp in mine:
            cp.start()
        sends = []
        for a in range(n):
            sends += [copy(a, 0, me, sibling, src=ins[a]), copy(a, 1, me, (*xn, c), src=ins[a]),
                      copy(a, 2, me, (*yn, c), src=ins[a])]
        for cp in sends:
            cp.start()

        def start(cp):
            if cp is not None:
                cp.start()
                sends.append(cp)

        for a in range(n):
            copy(a, 1, (*xn, c), me).wait_recv()
            start(copy(a, 4, (*xn, c), sibling))
            start(copy(a, 7, (*xn, c), (*yn, c), h=1))
        for a in range(n):
            copy(a, 2, (*yn, c), me).wait_recv()
            start(copy(a, 5, (*yn, c), sibling))
            start(copy(a, 8, (*yn, c), (*xn, c), h=0))
        for a in range(n):
            for k, h in ((8, 0), (7, 1)):
                cp = copy(a, k, (*diag, c), me, h=h)
                if cp is not None:
                    cp.wait_recv()
            start(copy(a, 6, (*diag, c), sibling))
        for a in range(n):
            copy(a, 0, sibling, me).wait_recv()
            for j, chip in enumerate((xn, yn, diag)):
                copy(a, 4 + j, (*chip, 1 - c), me).wait_recv()
        for cp in sends:
            cp.wait_send()
        for cp in mine:
            cp.wait()

    anyspec = pl.BlockSpec(memory_space=pl.ANY)
    return pl.pallas_call(
        body, name="weights_all_gather",
        out_shape=[SDS((N_DEV,) + a.shape, a.dtype) for a in arrs],
        in_specs=[anyspec] * n, out_specs=[anyspec] * n,
        scratch_shapes=[pltpu.SemaphoreType.DMA((ns * n,)), pltpu.SemaphoreType.DMA((ns * n,)),
                        pltpu.SemaphoreType.DMA((n,))],
    )(*arrs)


def _dw_in_swap(a_parts, u, dw_out):
    tile, tk = 1024, 1024
    s = u.shape[0]
    nk = s // tk
    na = len(a_parts)
    offs, counts, ni = _col_blocks(a_parts, tile)

    def body(*refs):
        a_refs, u_ref, dwo_ref = refs[:na], refs[na], refs[na + 1]
        dw_ref, got_ref, goto_ref = refs[na + 2:na + 5]
        acc, local_sems, send_sems, recv_sem, o_send, o_recv = refs[na + 5:]
        i, k = pl.program_id(0), pl.program_id(1)
        x, y, c = _my_pos()
        par = i % 2

        def tile_copies(t, p):
            rows = pl.ds(pl.multiple_of(t * tile, tile), tile)
            loc = pltpu.make_async_copy(acc.at[p], dw_ref.at[rows], local_sems.at[p])
            rem = pltpu.make_async_remote_copy(
                src_ref=acc.at[p], dst_ref=got_ref.at[rows], send_sem=send_sems.at[p], recv_sem=recv_sem,
                device_id=(x, y, 1 - c), device_id_type=MESH)
            return loc, rem

        out_copy = pltpu.make_async_remote_copy(
            src_ref=dwo_ref, dst_ref=goto_ref, send_sem=o_send, recv_sem=o_recv,
            device_id=(x, y, 1 - c), device_id_type=MESH)

        @pl.when(jnp.logical_and(i == 0, k == 0))
        def _():
            out_copy.start()

        @pl.when(k == 0)
        def _():
            @pl.when(i >= 2)
            def _():
                loc, rem = tile_copies(i - 2, par)
                loc.wait()
                rem.wait_send()
            acc[par] = jnp.zeros((tile, tile), f32)

        for t in range(na):
            @pl.when(jnp.logical_and(i >= offs[t], i < offs[t] + counts[t]))
            def _(t=t):
                acc[par] += _tn(a_refs[t][...], u_ref[...])

        @pl.when(k == nk - 1)
        def _():
            loc, rem = tile_copies(i, par)
            loc.start()
            rem.start()

        @pl.when(jnp.logical_and(i == ni - 1, k == nk - 1))
        def _():
            for t in (ni - 2, ni - 1):
                loc, rem = tile_copies(t, t % 2)
                loc.wait()
                rem.wait_send()
            pltpu.make_async_remote_copy(src_ref=dw_ref, dst_ref=got_ref, send_sem=send_sems.at[0], recv_sem=recv_sem,
                                         device_id=(x, y, c), device_id_type=MESH).wait_recv()
            out_copy.wait_send()
            out_copy.wait_recv()

    def a_spec(t):
        def index(i, k):
            mine = jnp.logical_and(i >= offs[t], i < offs[t] + counts[t])
            return jnp.where(mine, k, 0), jnp.clip(i - offs[t], 0, counts[t] - 1)
        return pl.BlockSpec((tk, tile), index)

    anyspec = pl.BlockSpec(memory_space=pl.ANY)
    return pl.pallas_call(
        body, name="dw_in_swap", grid=(ni, nk),
        in_specs=[a_spec(t) for t in range(na)] + [pl.BlockSpec((tk, tile), lambda i, k: (k, 0)), anyspec],
        out_specs=[anyspec] * 3,
        out_shape=[SDS((ni * tile, tile), f32), SDS((ni * tile, tile), f32), SDS(dw_out.shape, dw_out.dtype)],
        scratch_shapes=[pltpu.VMEM((2, tile, tile), f32), pltpu.SemaphoreType.DMA((2,)), pltpu.SemaphoreType.DMA((2,)),
                        pltpu.SemaphoreType.DMA(()), pltpu.SemaphoreType.DMA(()), pltpu.SemaphoreType.DMA(())],
        compiler_params=pltpu.CompilerParams(dimension_semantics=("arbitrary", "arbitrary")),
    )(*a_parts, u, dw_out)


def _gather_small(small):
    def body(small_in, small_out, send_sems, recv_sems, local_sem):
        x, y, c = _my_pos()
        me = 4 * x + 2 * y + c
        mine = pltpu.make_async_copy(small_in, small_out.at[me], local_sem)
        mine.start()
        sends = []
        for k in range(1, N_DEV):
            to = (me + k) % N_DEV
            cp = pltpu.make_async_remote_copy(
                src_ref=small_in, dst_ref=small_out.at[me], send_sem=send_sems.at[k - 1], recv_sem=recv_sems.at[k - 1],
                device_id=(to // 4, (to // 2) % 2, to % 2), device_id_type=MESH)
            cp.start()
            sends.append(cp)
        for k in range(1, N_DEV):
            frm = (me + N_DEV - k) % N_DEV
            pltpu.make_async_remote_copy(
                src_ref=small_in, dst_ref=small_out.at[frm], send_sem=send_sems.at[k - 1], recv_sem=recv_sems.at[k - 1],
                device_id=(x, y, c), device_id_type=MESH).wait_recv()
        for cp in sends:
            cp.wait_send()
        mine.wait()

    anyspec = pl.BlockSpec(memory_space=pl.ANY)
    return pl.pallas_call(
        body, name="small_grads_gather", out_shape=SDS((N_DEV,) + small.shape, small.dtype),
        in_specs=[anyspec], out_specs=anyspec,
        scratch_shapes=[pltpu.SemaphoreType.DMA((7,)), pltpu.SemaphoreType.DMA((7,)), pltpu.SemaphoreType.DMA(())],
    )(small)


def _chip_sum(mine, got, name):
    r, cdim = mine.shape
    tr = 1024

    def body(m_ref, g_ref, s16_ref):
        s16_ref[...] = (m_ref[...] + g_ref[...]).astype(bf16)

    blk = pl.BlockSpec((tr, cdim), lambda i: (i, 0))
    return pl.pallas_call(
        body, name=name, grid=(r // tr,), in_specs=[blk, blk], out_specs=blk, out_shape=SDS((r, cdim), bf16),
        compiler_params=pltpu.CompilerParams(dimension_semantics=("parallel",)),
    )(mine, got)


def _chip_exchange_copies(ins, outs, send_sems, recv_sems, local_sems):
    nb = len(ins)
    x, y, c = _my_pos()
    my_q = 2 * x + y
    mine = [pltpu.make_async_copy(ins[a].at[my_q], outs[a].at[my_q], local_sems.at[a]) for a in range(nb)]
    sends, recvs = [], []
    for k in range(1, 4):
        to, frm = (my_q + k) % 4, (my_q + 4 - k) % 4
        for a in range(nb):
            sems = dict(send_sem=send_sems.at[3 * a + k - 1], recv_sem=recv_sems.at[3 * a + k - 1], device_id_type=MESH)
            sends.append(pltpu.make_async_remote_copy(
                src_ref=ins[a].at[to], dst_ref=outs[a].at[my_q], device_id=(to // 2, to % 2, c), **sems))
            recvs.append(pltpu.make_async_remote_copy(
                src_ref=ins[a].at[frm], dst_ref=outs[a].at[frm], device_id=(x, y, c), **sems))
    return mine, sends, recvs


def _chip_exchange_scratch(nb):
    return [pltpu.SemaphoreType.DMA((3 * nb,)), pltpu.SemaphoreType.DMA((3 * nb,)), pltpu.SemaphoreType.DMA((nb,))]


def _prenorm_inproj(x, nw, wt):
    s, d = x.shape
    npad = wt.shape[0]
    tm, tn = 1024, 1024

    def body(x_ref, nw_ref, w_ref, proj_ref, u_ref):
        @pl.when(pl.program_id(1) == 0)
        def _():
            xv = x_ref[...]
            r = lax.rsqrt(jnp.mean(xv * xv, axis=-1, keepdims=True) + EPS)
            u_ref[...] = (xv * r * nw_ref[...]).astype(bf16)
        proj_ref[...] = _nt(u_ref[...], w_ref[...])

    return pl.pallas_call(
        body, name="prenorm_inproj", grid=(s // tm, npad // tn),
        in_specs=[pl.BlockSpec((tm, d), lambda i, j: (i, 0)), pl.BlockSpec((1, d), lambda i, j: (0, 0)),
                  pl.BlockSpec((tn, d), lambda i, j: (j, 0))],
        out_specs=[pl.BlockSpec((tm, tn), lambda i, j: (i, j)), pl.BlockSpec((tm, d), lambda i, j: (i, 0))],
        out_shape=[SDS((s, npad), f32), SDS((s, d), bf16)],
        compiler_params=pltpu.CompilerParams(dimension_semantics=("parallel", "arbitrary")),
    )(x, nw, wt)


def _attn_consts():
    head0 = _iota((BLK, LANES), 1) < HEAD_DIM
    tri2 = (_iota((BLK, 2 * LANES), 1) % LANES) <= _iota((BLK, 2 * LANES), 0)
    ones2 = ((_iota((LANES, 2 * LANES), 0) < HEAD_DIM) == (_iota((LANES, 2 * LANES), 1) < LANES)).astype(bf16)
    rmat = ((_iota((2 * LANES, LANES), 0) < LANES) == (_iota((2 * LANES, LANES), 1) < HEAD_DIM)).astype(bf16)
    bones = ((_iota((LANES, LANES), 0) < HEAD_DIM) == (_iota((LANES, LANES), 1) < HEAD_DIM)).astype(bf16)
    return head0, tri2, ones2, rmat, bones


def _stack_heads(x16, head0):
    zero = jnp.zeros_like(x16)
    return jnp.concatenate([jnp.where(head0, x16, zero), jnp.where(head0, zero, x16)], axis=0)


def _split_dot(x, w16):
    hi = x.astype(bf16)
    lo = (x - hi.astype(f32)).astype(bf16)
    return _nn(hi, w16) + _nn(lo, w16)


def _bf16_terms(x, terms):
    out = []
    for _ in range(terms):
        t = x.astype(bf16)
        out.append(t)
        x = x - t.astype(f32)
    return out


def _dot_01(x, w16, terms):
    return sum(_nn(t, w16) for t in _bf16_terms(x, terms))


def _dot_01_left(w16, x, terms):
    return sum(_nn(w16, t) for t in _bf16_terms(x, terms))


def _attn_fwd(proj):
    s = proj.shape[0]
    n_it = s // BLK

    def body(q_ref, k_ref, v_ref, g_ref, o_ref, l_ref, mix_ref, op0, op1, op2, lp0, lp1, lp2,
             s_a, s_b, sd_a, sd_b, p_a, p_b, m_a, m_b, pd_a, pd_b, k_a, k_b, v_a, v_b):
        op_refs, lp_refs = (op0, op1, op2), (lp0, lp1, lp2)
        head0, tri2, ones2, rmat, _ = _attn_consts()
        score_bufs, prob_bufs = ((s_a, sd_a), (s_b, sd_b)), ((p_a, m_a, pd_a), (p_b, m_b, pd_b))
        k_bufs, v_bufs = (k_a, k_b), (v_a, v_b)
        for buf in k_bufs + v_bufs:
            buf[...] = jnp.zeros_like(buf)

        def block_rows(i, d, nb):
            r, blk = i // nb, i % nb
            return pl.ds(blk * (BLK * d) + r, BLK, stride=d), blk > 0

        def unstack(st16):
            return st16[:BLK] + st16[BLK:]

        def scores(i, par, d, nb):
            rows, has_prev = block_rows(i, d, nb)
            s_buf, sd_buf = score_bufs[par]
            qs = q_ref[rows, :] * 0.125
            qs16 = qs.astype(bf16)
            kst_c = _stack_heads(k_ref[rows, :].astype(bf16), head0)
            kst_p = k_bufs[1 - par][...]
            k_bufs[par][...] = kst_c
            sc = _nt(qs16, kst_c)
            sp = _nt(qs16, kst_p)
            s_buf[...] = jnp.where(tri2, sc, jnp.where(has_prev, sp, -jnp.inf))
            sd = _split_dot(qs * unstack(kst_p).astype(f32), ones2)
            sd_buf[...] = jnp.where(has_prev, sd, -jnp.inf)

        def softmax(bufs_in, bufs_out):
            s_buf, sd_buf = bufs_in
            p_buf, m_buf, pd_buf = bufs_out
            sc, sd2 = s_buf[...], sd_buf[...]
            m0 = jnp.max(sc[:, :LANES], axis=1, keepdims=True)
            m1 = jnp.max(sc[:, LANES:], axis=1, keepdims=True)
            m2 = jnp.concatenate([jnp.broadcast_to(m0, (BLK, LANES)), jnp.broadcast_to(m1, (BLK, LANES))], axis=1)
            m2 = jnp.maximum(m2, sd2)
            p_buf[...] = jnp.exp(sc - m2).astype(bf16)
            m_pair = jnp.where(head0, m2[:, :LANES], m2[:, LANES:])
            m_buf[...] = m_pair
            pd_buf[...] = jnp.exp(jnp.where(head0, sd2[:, :LANES], sd2[:, LANES:]) - m_pair)

        def output(i, par, d, nb, p):
            rows, _ = block_rows(i, d, nb)
            p_buf, m_buf, pd_buf = prob_bufs[par]
            vst_c = _stack_heads(v_ref[rows, :].astype(bf16), head0)
            vst_p = v_bufs[1 - par][...]
            v_bufs[par][...] = vst_c
            pt16, pd = p_buf[...], pd_buf[...]
            zero = jnp.zeros_like(pt16)
            o = (_nn(jnp.where(tri2, pt16, zero), vst_c) + _nn(jnp.where(tri2, zero, pt16), vst_p)
                 + pd * unstack(vst_p).astype(f32))
            l = _nn(pt16, rmat) + pd
            op_refs[p][rows, :] = o / l
            lp_refs[p][rows, :] = m_buf[...] + jnp.log(l)

        for p, d in enumerate(DILATIONS):
            nb = s // (BLK * d)
            scores(0, 0, d, nb)
            scores(1, 1, d, nb)
            softmax(score_bufs[0], prob_bufs[0])

            def steps(j, carry, d=d, nb=nb, p=p):
                for par in range(2):
                    t = 2 * j + 2 + par
                    scores(t, par, d, nb)
                    output(t - 2, par, d, nb, p)
                    softmax(score_bufs[1 - par], prob_bufs[1 - par])
                return carry

            lax.fori_loop(0, (n_it - 2) // 2, steps, 0)
            output(n_it - 2, 0, d, nb, p)
            softmax(score_bufs[1], prob_bufs[1])
            output(n_it - 1, 1, d, nb, p)

        def merge(i, carry):
            rows = pl.ds(pl.multiple_of(i * 256, 256), 256)
            l0, l1, l2 = lp0[rows, :], lp1[rows, :], lp2[rows, :]
            m = jnp.maximum(jnp.maximum(l0, l1), l2)
            e0, e1, e2 = jnp.exp(l0 - m), jnp.exp(l1 - m), jnp.exp(l2 - m)
            z = e0 + e1 + e2
            o = (e0 * op0[rows, :] + e1 * op1[rows, :] + e2 * op2[rows, :]) / z
            o_ref[rows, :] = o
            l_ref[rows, :] = m + jnp.log(z)
            g = g_ref[rows, :]
            mix_ref[rows, :] = (o * (g * _sigmoid(g))).astype(bf16)
            return carry

        lax.fori_loop(0, s // 256, merge, 0)

    col = lambda base: pl.BlockSpec((s, LANES), lambda h: (0, base + h))
    return pl.pallas_call(
        body, name="attn_fwd", grid=(N_PAIRS,),
        in_specs=[col(0), col(8), col(16), col(24)],
        out_specs=[col(0), col(0), col(0)],
        out_shape=[SDS((s, D_ATTN), f32), SDS((s, D_ATTN), f32), SDS((s, D_ATTN), bf16)],
        scratch_shapes=[pltpu.VMEM((s, LANES), f32)] * 6 + [pltpu.VMEM((BLK, 2 * LANES), f32)] * 4
        + [pltpu.VMEM((BLK, 2 * LANES), bf16)] * 2 + [pltpu.VMEM((BLK, LANES), f32)] * 4
        + [pltpu.VMEM((2 * BLK, LANES), bf16)] * 4,
        compiler_params=pltpu.CompilerParams(dimension_semantics=("parallel",)),
    )(proj, proj, proj, proj)


def _expand_mat():
    colv = np.arange(2 * D_SSM)
    head = 2 * ((colv % D_SSM) // LANES) + colv // D_SSM
    return jnp.asarray(np.arange(LANES)[:, None] == head[None, :], dtype=bf16)


def _fold_mat():
    return jnp.asarray((np.arange(D_SSM) // HEAD_DIM)[:, None] == np.arange(LANES)[None, :], dtype=bf16)


def _ssd_common(xs_ref, bc_ref, xs_tail, bc_tail, dt_ref, cw_ref, cb_ref, dtb_ref, alog16_ref, emat_ref, xpad, first):
    keep = jnp.where(first, 0.0, 1.0)
    xpad[0:8, 0:D_SSM] = xs_tail[...] * keep
    xpad[0:8, D_SSM:D_CONV] = bc_tail[...] * keep
    xpad[8:8 + CHUNK, 0:D_SSM] = xs_ref[...]
    xpad[8:8 + CHUNK, D_SSM:D_CONV] = bc_ref[...]
    xp = xpad[...]
    taps = [pltpu.roll(xp, 3 - j, 0)[8:8 + CHUNK] for j in range(3)] + [xp[8:8 + CHUNK]]
    cv = cb_ref[...] + cw_ref[0:1, :] * taps[0]
    for j in range(1, 4):
        cv = cv + cw_ref[j:j + 1, :] * taps[j]
    sig = _sigmoid(cv)
    xbc = cv * sig

    pre = dt_ref[...] + dtb_ref[...]
    dt16 = _softplus(pre)
    a16 = -jnp.exp(alog16_ref[...])
    sub, lane = _iota((CHUNK, CHUNK), 0), _iota((CHUNK, CHUNK), 1)
    tri = (sub >= lane).astype(f32)
    al16 = _nn_hi(tri, dt16 * a16)
    al_t = al16.T
    emat = emat_ref[...]
    dt_x = _dot_01(dt16, emat, 3)
    al_x = _dot_01(al16, emat, 3)
    lane_w = _iota((CHUNK, D_SSM), 1)
    even = (lane_w % LANES) < HEAD_DIM
    dt_f = jnp.where(even, dt_x[:, :D_SSM], dt_x[:, D_SSM:])
    al_f = jnp.where(even, al_x[:, :D_SSM], al_x[:, D_SSM:])
    return cv, sig, xbc, pre, dt_f, al_f, al_x, al_t, taps


def _decay_mat(al_x, al_t, pair, h):
    sub, lane = _iota((CHUNK, CHUNK), 0), _iota((CHUNK, CHUNK), 1)
    col = al_x[:, h * D_SSM + pair * LANES: h * D_SSM + (pair + 1) * LANES]
    row = al_t[2 * pair + h: 2 * pair + h + 1, :]
    return jnp.exp(jnp.where(sub >= lane, col - row, -jnp.inf))


def _ssd_in_specs(order):
    blk = lambda w, cb: pl.BlockSpec((CHUNK, w), lambda i: (order(i), cb))
    tail = lambda w, cb: pl.BlockSpec((8, w), lambda i: (jnp.maximum(16 * order(i) - 1, 0), cb))
    return [blk(D_SSM, COL_XS // D_SSM), blk(512, COL_BC // 512), tail(D_SSM, COL_XS // D_SSM),
            tail(512, COL_BC // 512), blk(LANES, COL_DT // LANES), blk(D_SSM, COL_Z // D_SSM)]


def _full(shape):
    return pl.BlockSpec(shape, lambda i: (0,) * len(shape))


def _ssd_fwd(proj, conv_w, conv_b, dtb16, alog16, alog_f, d_f, nw):
    s = proj.shape[0]
    nc = s // CHUNK

    def body(xs_ref, bc_ref, xs_tail, bc_tail, dt_ref, z_ref, cw_ref, cb_ref, dtb_ref, alog16_ref, alogf_ref,
             df_ref, nw_ref, emat_ref, mix_ref, y_ref, st_ref, h_scr, xpad, y_scr):
        c = pl.program_id(0)

        @pl.when(c == 0)
        def _():
            h_scr[...] = jnp.zeros_like(h_scr)

        _, _, xbc, _, dt_f, al_f, al_x, al_t, _ = _ssd_common(
            xs_ref, bc_ref, xs_tail, bc_tail, dt_ref, cw_ref, cb_ref, dtb_ref, alog16_ref, emat_ref, xpad, c == 0)
        head0 = _iota((CHUNK, LANES), 1) < HEAD_DIM
        st_ref[...] = h_scr[...]
        for g in range(N_GROUPS):
            bm = xbc[:, D_SSM + g * D_STATE: D_SSM + (g + 1) * D_STATE].astype(bf16)
            cm = xbc[:, D_SSM + (N_GROUPS + g) * D_STATE: D_SSM + (N_GROUPS + g + 1) * D_STATE].astype(bf16)
            gmat = _nt(cm, bm)
            for pair in range(4 * g, 4 * g + 4):
                sl = slice(pair * LANES, (pair + 1) * LANES)
                xp, dtp, alp = xbc[:, sl], dt_f[:, sl], al_f[:, sl]
                xdt = xp * dtp
                xdt16 = xdt.astype(bf16)
                al_last = alp[CHUNK - 1:CHUNK, :]
                hp = h_scr[:, sl]
                y_off = jnp.exp(alp) * _nn(cm, hp.astype(bf16))
                yd = [_nn((gmat * _decay_mat(al_x, al_t, pair, h)).astype(bf16), xdt16) for h in range(2)]
                y_scr[:, sl] = jnp.where(head0, yd[0], yd[1]) + y_off + df_ref[:, sl] * xp
                st = _tn(bm, (jnp.exp(al_last - alp) * xdt).astype(bf16))
                h_scr[:, sl] = jnp.exp(al_last) * hp + st
        y = y_scr[...]
        y_ref[...] = y
        z = z_ref[...]
        yz = y * (z * _sigmoid(z))
        gw = D_SSM // N_GROUPS
        for g in range(N_GROUPS):
            part = yz[:, g * gw:(g + 1) * gw]
            r = lax.rsqrt(jnp.mean(part * part, axis=-1, keepdims=True) + EPS)
            mix_ref[:, g * gw:(g + 1) * gw] = (part * r * nw_ref[:, g * gw:(g + 1) * gw]).astype(bf16)

    order = lambda i: i
    row = lambda w: pl.BlockSpec((CHUNK, w), lambda i: (i, 0))
    return pl.pallas_call(
        body, name="ssd_fwd", grid=(nc,),
        in_specs=_ssd_in_specs(order) + [_full((4, D_CONV)), _full((1, D_CONV)), _full((1, LANES)), _full((1, LANES)),
                                         _full((1, D_SSM)), _full((1, D_SSM)), _full((1, D_SSM)),
                                         _full((LANES, 2 * D_SSM))],
        out_specs=[row(D_SSM), row(D_SSM), pl.BlockSpec((None, D_STATE, D_SSM), lambda i: (i, 0, 0))],
        out_shape=[SDS((s, D_SSM), bf16), SDS((s, D_SSM), f32), SDS((nc, D_STATE, D_SSM), f32)],
        scratch_shapes=[pltpu.VMEM((D_STATE, D_SSM), f32), pltpu.VMEM((8 + CHUNK, D_CONV), f32),
                        pltpu.VMEM((CHUNK, D_SSM), f32)],
        compiler_params=pltpu.CompilerParams(dimension_semantics=("arbitrary",)),
    )(proj, proj, proj, proj, proj, proj, conv_w, conv_b, dtb16, alog16, alog_f, d_f, nw, _expand_mat())


def _outproj_loss(mix_a, mix_s, wo, x, tgt, npw):
    s, d = x.shape
    tm = 512

    def body(ma_ref, ms_ref, wo_ref, x_ref, t_ref, npw_ref, dmix_ref, dout_ref, dres_ref, acc_ref):
        @pl.when(pl.program_id(0) == 0)
        def _():
            acc_ref[...] = jnp.zeros_like(acc_ref)

        out = _nn(ma_ref[...], wo_ref[0:D_ATTN, :]) + _nn(ms_ref[...], wo_ref[D_ATTN:, :])
        r = lax.rsqrt(jnp.mean(out * out, axis=-1, keepdims=True) + EPS)
        on = out * r
        diff = x_ref[...] + on * npw_ref[...] - t_ref[...]
        dres = diff * (1.0 / d)
        dres_ref[...] = dres
        acc_ref[0:1, :] += jnp.sum(diff * diff, axis=0, keepdims=True)
        acc_ref[1:2, :] += jnp.sum(dres * on, axis=0, keepdims=True)
        dn = dres * npw_ref[...]
        dout = (r * (dn - on * jnp.mean(dn * on, axis=-1, keepdims=True))).astype(bf16)
        dout_ref[...] = dout
        dmix_ref[...] = _nt(dout, wo_ref[...])

    row = lambda w: pl.BlockSpec((tm, w), lambda i: (i, 0))
    return pl.pallas_call(
        body, name="outproj_loss", grid=(s // tm,),
        in_specs=[row(D_ATTN), row(D_SSM), _full((D_ATTN + D_SSM, d)), row(d), row(d), _full((1, d))],
        out_specs=[row(D_ATTN + D_SSM), row(d), row(d), _full((8, d))],
        out_shape=[SDS((s, D_ATTN + D_SSM), f32), SDS((s, d), bf16), SDS((s, d), f32), SDS((8, d), f32)],
        compiler_params=pltpu.CompilerParams(dimension_semantics=("arbitrary",)),
    )(mix_a, mix_s, wo, x, tgt, npw)


def _attn_bwd(proj, o, lb, dmix):
    s = proj.shape[0]
    n_it = s // BLK

    def body(q_ref, k_ref, v_ref, g_ref, o_ref, l_ref, dm_ref, dq_ref, dk_ref, dv_ref, dg_ref,
             dq_acc, dk_acc, dv_acc, do_scr, dl_scr, *bufs):
        head0, tri2, _, _, bones = _attn_consts()

        def pro(i, carry):
            rows = pl.ds(pl.multiple_of(i * 256, 256), 256)
            g = g_ref[rows, :]
            sg = _sigmoid(g)
            dmx = dm_ref[rows, :]
            ov = o_ref[rows, :]
            dg_ref[rows, :] = (dmx * ov * (sg * (1.0 + g * (1.0 - sg)))).astype(bf16)
            do = dmx * (g * sg)
            do_scr[rows, :] = do
            dl_scr[rows, :] = _split_dot(do * ov, bones)
            z = jnp.zeros((256, LANES), f32)
            dq_acc[rows, :] = z
            dk_acc[rows, :] = z
            dv_acc[rows, :] = z
            return carry

        lax.fori_loop(0, s // 256, pro, 0)

        def per_head(t):
            return jnp.concatenate([t[:, :LANES], t[:, LANES:]], axis=0)

        def both_heads(t):
            tr = pltpu.roll(t, HEAD_DIM, 1)
            return jnp.concatenate([jnp.where(head0, t, tr), jnp.where(head0, tr, t)], axis=1)

        mm_bufs = ((bufs[0], bufs[1], bufs[2], bufs[3]), (bufs[4], bufs[5], bufs[6], bufs[7]))
        ds_bufs = ((bufs[8], bufs[9], bufs[10], bufs[11]), (bufs[12], bufs[13], bufs[14], bufs[15]))
        op_bufs = ((bufs[16], bufs[17], bufs[18], bufs[19]), (bufs[20], bufs[21], bufs[22], bufs[23]))
        vc_bufs, carry_k, carry_v = (bufs[24], bufs[25]), bufs[26], bufs[27]
        for buf in (op_bufs[0][0], op_bufs[1][0]) + vc_bufs:
            buf[...] = jnp.zeros_like(buf)

        def block_rows(i, d, nb):
            r, blk = i // nb, i % nb
            return pl.ds(blk * (BLK * d) + r, BLK, stride=d), blk > 0

        def unstack(st16):
            return st16[:BLK] + st16[BLK:]

        def products(i, par, d, nb):
            rows, has_prev = block_rows(i, d, nb)
            s_buf, dp_buf, sd_buf, dpd_buf = mm_bufs[par]
            kc_buf, kp_buf, q_buf, do_buf = op_bufs[par]
            q = q_ref[rows, :]
            qs = q * 0.125
            do = do_scr[rows, :]
            qs16, do16 = qs.astype(bf16), do.astype(bf16)
            kst_c = _stack_heads(k_ref[rows, :].astype(bf16), head0)
            vst_c = _stack_heads(v_ref[rows, :].astype(bf16), head0)
            kst_p, vst_p = op_bufs[1 - par][0][...], vc_bufs[1 - par][...]
            kc_buf[...] = kst_c
            kp_buf[...] = kst_p
            vc_bufs[par][...] = vst_c
            q_buf[...] = q.astype(bf16)
            do_buf[...] = do16
            s_buf[...] = jnp.where(tri2, _nt(qs16, kst_c), jnp.where(has_prev, _nt(qs16, kst_p), -jnp.inf))
            dp_buf[...] = jnp.where(tri2, _nt(do16, vst_c), jnp.where(has_prev, _nt(do16, vst_p), 0.0))
            sd_buf[...] = _split_dot(qs * unstack(kst_p).astype(f32), bones)
            dpd_buf[...] = jnp.where(has_prev, _split_dot(do * unstack(vst_p).astype(f32), bones), 0.0)

        def softmax_grad(i, par, d, nb):
            rows, has_prev = block_rows(i, d, nb)
            s_buf, dp_buf, sd_buf, dpd_buf = mm_bufs[par]
            p_buf, ds_buf, pd_buf, dsd_buf = ds_bufs[par]
            lse = l_ref[rows, :]
            dl = dl_scr[rows, :]
            pt = jnp.exp(s_buf[...] - both_heads(lse))
            ds_buf[...] = (pt * (dp_buf[...] - both_heads(dl)) * 0.125).astype(bf16)
            p_buf[...] = pt.astype(bf16)
            pd = jnp.where(has_prev, jnp.exp(sd_buf[...] - lse), 0.0)
            pd_buf[...] = pd
            dsd_buf[...] = pd * (dpd_buf[...] - dl) * 0.125

        def accumulate(i, par, d, nb):
            rows, _ = block_rows(i, d, nb)
            before, _ = block_rows(jnp.maximum(i - 1, 0), d, nb)
            p_buf, ds_buf, pd_buf, dsd_buf = ds_bufs[par]
            kc_buf, kp_buf, q_buf, do_buf = op_bufs[par]
            pt16, ds16, pd, dsd = p_buf[...], ds_buf[...], pd_buf[...], dsd_buf[...]
            zero = jnp.zeros_like(pt16)
            dsc, dsp = jnp.where(tri2, ds16, zero), jnp.where(tri2, zero, ds16)
            pc, pp = jnp.where(tri2, pt16, zero), jnp.where(tri2, zero, pt16)
            kst_c, kst_p, q16, do16 = kc_buf[...], kp_buf[...], q_buf[...], do_buf[...]
            qst, dost = _stack_heads(q16, head0), _stack_heads(do16, head0)
            dq_acc[rows, :] += _nn(dsc, kst_c) + _nn(dsp, kst_p) + dsd * unstack(kst_p).astype(f32)
            dk_acc[before, :] += carry_k[...] + _tn(per_head(dsp), qst) + dsd * q16.astype(f32)
            dv_acc[before, :] += carry_v[...] + _tn(per_head(pp), dost) + pd * do16.astype(f32)
            carry_k[...] = _tn(per_head(dsc), qst)
            carry_v[...] = _tn(per_head(pc), dost)

        for d in DILATIONS:
            nb = s // (BLK * d)
            carry_k[...] = jnp.zeros_like(carry_k)
            carry_v[...] = jnp.zeros_like(carry_v)
            products(0, 0, d, nb)
            products(1, 1, d, nb)
            softmax_grad(0, 0, d, nb)

            def steps(j, carry, d=d, nb=nb):
                for par in range(2):
                    t = 2 * j + 2 + par
                    accumulate(t - 2, par, d, nb)
                    products(t, par, d, nb)
                    softmax_grad(t - 1, 1 - par, d, nb)
                return carry

            lax.fori_loop(0, (n_it - 2) // 2, steps, 0)
            accumulate(n_it - 2, 0, d, nb)
            softmax_grad(n_it - 1, 1, d, nb)
            accumulate(n_it - 1, 1, d, nb)
            last, _ = block_rows(n_it - 1, d, nb)
            dk_acc[last, :] += carry_k[...]
            dv_acc[last, :] += carry_v[...]

        def epi(i, carry):
            rows = pl.ds(pl.multiple_of(i * 256, 256), 256)
            dq_ref[rows, :] = dq_acc[rows, :].astype(bf16)
            dk_ref[rows, :] = dk_acc[rows, :].astype(bf16)
            dv_ref[rows, :] = dv_acc[rows, :].astype(bf16)
            return carry

        lax.fori_loop(0, s // 256, epi, 0)

    col = lambda base: pl.BlockSpec((s, LANES), lambda h: (0, base + h))
    outs = pl.pallas_call(
        body, name="attn_bwd", grid=(N_PAIRS,),
        in_specs=[col(0), col(8), col(16), col(24), col(0), col(0), col(0)],
        out_specs=[col(0)] * 4,
        out_shape=[SDS((s, D_ATTN), bf16)] * 4,
        scratch_shapes=[pltpu.VMEM((s, LANES), f32)] * 5
        + [pltpu.VMEM((BLK, 2 * LANES), f32)] * 2 + [pltpu.VMEM((BLK, LANES), f32)] * 2
        + [pltpu.VMEM((BLK, 2 * LANES), f32)] * 2 + [pltpu.VMEM((BLK, LANES), f32)] * 2
        + [pltpu.VMEM((BLK, 2 * LANES), bf16)] * 2 + [pltpu.VMEM((BLK, LANES), f32)] * 2
        + [pltpu.VMEM((BLK, 2 * LANES), bf16)] * 2 + [pltpu.VMEM((BLK, LANES), f32)] * 2
        + [pltpu.VMEM((2 * BLK, LANES), bf16)] * 2 + [pltpu.VMEM((BLK, LANES), bf16)] * 2
        + [pltpu.VMEM((2 * BLK, LANES), bf16)] * 2 + [pltpu.VMEM((BLK, LANES), bf16)] * 2
        + [pltpu.VMEM((2 * BLK, LANES), bf16)] * 2 + [pltpu.VMEM((BLK, LANES), f32)] * 2,
        compiler_params=pltpu.CompilerParams(dimension_semantics=("parallel",)),
    )(proj, proj, proj, proj, o, lb, dmix)
    return outs


def _ssd_bwd(proj, y, states, dmix, conv_w, conv_b, dtb16, alog16, alog_f, d_f, nw):
    s = proj.shape[0]
    nc = s // CHUNK
    gw = D_SSM // N_GROUPS

    def body(xs_ref, bc_ref, xs_tail, bc_tail, dt_ref, z_ref, y_ref, st_ref, dm_ref, cw_ref, cb_ref, dtb_ref,
             alog16_ref, alogf_ref, df_ref, nw_ref, emat_ref, fold_ref, out_ref, gconv_ref, gvec_ref, gdt_ref,
             dh_scr, head_scr, xpad, dcpad, da_scr, dxdt_scr, dbc_scr):
        i = pl.program_id(0)
        c = nc - 1 - i

        @pl.when(i == 0)
        def _():
            dh_scr[...] = jnp.zeros_like(dh_scr)
            head_scr[...] = jnp.zeros_like(head_scr)
            gconv_ref[...] = jnp.zeros_like(gconv_ref)
            gvec_ref[...] = jnp.zeros_like(gvec_ref)
            gdt_ref[...] = jnp.zeros_like(gdt_ref)

        cv, sig, xbc, pre, dt_f, al_f, al_x, al_t, taps = _ssd_common(
            xs_ref, bc_ref, xs_tail, bc_tail, dt_ref, cw_ref, cb_ref, dtb_ref, alog16_ref, emat_ref, xpad, c == 0)
        head0 = _iota((CHUNK, LANES), 1) < HEAD_DIM
        sub = _iota((CHUNK, LANES), 0)
        last_row = sub == CHUNK - 1

        yv, z, dmx = y_ref[...], z_ref[...], dm_ref[...]
        sz = _sigmoid(z)
        silu = z * sz
        yz = yv * silu
        dyz_parts = []
        for g in range(N_GROUPS):
            gs = slice(g * gw, (g + 1) * gw)
            part = yz[:, gs]
            r = lax.rsqrt(jnp.mean(part * part, axis=-1, keepdims=True) + EPS)
            nh = part * r
            gvec_ref[0:1, gs] += jnp.sum(dmx[:, gs] * nh, axis=0, keepdims=True)
            dn = dmx[:, gs] * nw_ref[:, gs]
            dyz_parts.append(r * (dn - nh * jnp.mean(dn * nh, axis=-1, keepdims=True)))
        dyz = jnp.concatenate(dyz_parts, axis=1)
        dy = dyz * silu
        out_ref[:, 0:D_SSM] = (dyz * yv * (sz * (1.0 + z * (1.0 - sz)))).astype(bf16)

        x_all = xbc[:, 0:D_SSM]
        gvec_ref[2:3, :] += jnp.sum(dy * x_all, axis=0, keepdims=True)

        for g in range(N_GROUPS):
            bm = xbc[:, D_SSM + g * D_STATE: D_SSM + (g + 1) * D_STATE].astype(bf16)
            cm = xbc[:, D_SSM + (N_GROUPS + g) * D_STATE: D_SSM + (N_GROUPS + g + 1) * D_STATE].astype(bf16)
            gmat = _nt(cm, bm)
            dgm = jnp.zeros((CHUNK, CHUNK), f32)
            db = jnp.zeros((CHUNK, D_STATE), f32)
            dc = jnp.zeros((CHUNK, D_STATE), f32)
            for pair in range(4 * g, 4 * g + 4):
                sl = slice(pair * LANES, (pair + 1) * LANES)
                xp, dtp, alp, dyp = x_all[:, sl], dt_f[:, sl], al_f[:, sl], dy[:, sl]
                xdt = xp * dtp
                xdt16 = xdt.astype(bf16)
                al_last = alp[CHUNK - 1:CHUNK, :]
                e_l = jnp.exp(alp)
                wf = jnp.exp(al_last - alp)
                e_last = jnp.exp(al_last)
                hp = st_ref[:, sl]
                hp16 = hp.astype(bf16)
                dhn = dh_scr[:, sl]
                dhn16 = dhn.astype(bf16)
                y_off = e_l * _nn(cm, hp16)
                dch16 = (dyp * e_l).astype(bf16)
                dc = dc + _nt(dch16, hp16)
                dh_out = _tn(cm, dch16)
                dal = dyp * y_off
                xw16 = (wf * xdt).astype(bf16)
                db = db + _nt(xw16, dhn16)
                dxw = _nn(bm, dhn16)
                dxdt = dxw * wf
                dwf = dxw * xdt * wf
                dal = dal - dwf
                dal_last = jnp.sum(dwf, axis=0, keepdims=True) + jnp.sum(dhn * hp, axis=0, keepdims=True) * e_last
                dh_scr[:, sl] = e_last * dhn + dh_out
                for h in range(2):
                    mh = head0 if h == 0 else jnp.logical_not(head0)
                    dyh16 = jnp.where(mh, dyp, 0.0).astype(bf16)
                    lmat = _decay_mat(al_x, al_t, pair, h)
                    mm = gmat * lmat
                    dmm = _nt(dyh16, xdt16)
                    dxdt = dxdt + _tn(mm.astype(bf16), dyh16)
                    n16 = (dmm * mm).astype(bf16)
                    jh = jnp.where(mh, 1.0 / HEAD_DIM, 0.0).astype(bf16)
                    dal = dal + _nn(n16, jh) - _tn(n16, jh)
                    dgm = dgm + dmm * lmat
                da_scr[:, sl] = dal + jnp.where(last_row, dal_last, 0.0)
                dxdt_scr[:, sl] = dxdt
            dgm16 = dgm.astype(bf16)
            dbc_scr[:, g * D_STATE:(g + 1) * D_STATE] = db + _tn(dgm16, cm)
            dbc_scr[:, (N_GROUPS + g) * D_STATE:(N_GROUPS + g + 1) * D_STATE] = dc + _nn(dgm16, bm)

        sub_c, lane_c = _iota((CHUNK, CHUNK), 0), _iota((CHUNK, CHUNK), 1)
        tri_t = (lane_c >= sub_c).astype(bf16)
        dadt = _dot_01_left(tri_t, da_scr[...], 2)
        a_f = -jnp.exp(alogf_ref[...])
        dxdt_all = dxdt_scr[...]
        ddt_f = dxdt_all * x_all + a_f * dadt
        gvec_ref[1:2, :] += jnp.sum(dt_f * dadt, axis=0, keepdims=True) * a_f
        dx = df_ref[...] * dy + dxdt_all * dt_f
        ddt_raw = _dot_01(ddt_f, fold_ref[...], 2) * _sigmoid(pre)
        gdt_ref[0:1, :] += jnp.sum(ddt_raw, axis=0, keepdims=True)
        out_ref[:, D_SSM + D_CONV:D_SSM + D_CONV + LANES] = ddt_raw.astype(bf16)
        out_ref[:, D_SSM + D_CONV + LANES:] = jnp.zeros((CHUNK, 3 * LANES), bf16)

        dsil = sig * (1.0 + cv * (1.0 - sig))
        dcv_x = dx * dsil[:, 0:D_SSM]
        dcv_bc = dbc_scr[...] * dsil[:, D_SSM:]
        dcpad[0:CHUNK, 0:D_SSM] = dcv_x
        dcpad[0:CHUNK, D_SSM:] = dcv_bc
        dcpad[CHUNK:, :] = head_scr[...]
        dcp = dcpad[...]
        dcv = dcp[0:CHUNK]
        gconv_ref[4:5, :] += jnp.sum(dcv, axis=0, keepdims=True)
        draw = cw_ref[3:4, :] * dcv
        for j in range(4):
            gconv_ref[j:j + 1, :] += jnp.sum(dcv * taps[j], axis=0, keepdims=True)
        for j in range(3):
            draw = draw + cw_ref[j:j + 1, :] * pltpu.roll(dcp, CHUNK + 8 - (3 - j), 0)[0:CHUNK]
        head_scr[...] = dcv[0:8]
        out_ref[:, D_SSM:D_SSM + D_CONV] = draw.astype(bf16)

    order = lambda i: nc - 1 - i
    row = lambda w, cb=0: pl.BlockSpec((CHUNK, w), lambda i: (nc - 1 - i, cb))
    return pl.pallas_call(
        body, name="ssd_bwd", grid=(nc,),
        in_specs=_ssd_in_specs(order) + [row(D_SSM), pl.BlockSpec((None, D_STATE, D_SSM), lambda i: (nc - 1 - i, 0, 0)),
                                         row(D_SSM, 1), _full((4, D_CONV)), _full((1, D_CONV)), _full((1, LANES)),
                                         _full((1, LANES)), _full((1, D_SSM)), _full((1, D_SSM)), _full((1, D_SSM)),
                                         _full((LANES, 2 * D_SSM)), _full((D_SSM, LANES))],
        out_specs=[row(3072), _full((8, D_CONV)), _full((8, D_SSM)), _full((8, LANES))],
        out_shape=[SDS((s, 3072), bf16), SDS((8, D_CONV), f32), SDS((8, D_SSM), f32), SDS((8, LANES), f32)],
        scratch_shapes=[pltpu.VMEM((D_STATE, D_SSM), f32), pltpu.VMEM((8, D_CONV), f32),
                        pltpu.VMEM((8 + CHUNK, D_CONV), f32), pltpu.VMEM((8 + CHUNK, D_CONV), f32),
                        pltpu.VMEM((CHUNK, D_SSM), f32), pltpu.VMEM((CHUNK, D_SSM), f32),
                        pltpu.VMEM((CHUNK, 2 * N_GROUPS * D_STATE), f32)],
        compiler_params=pltpu.CompilerParams(dimension_semantics=("arbitrary",)),
    )(proj, proj, proj, proj, proj, proj, y, states, dmix, conv_w, conv_b, dtb16, alog16, alog_f, d_f, nw,
      _expand_mat(), _fold_mat())


def _col_blocks(parts, tile):
    counts = [p.shape[1] // tile for p in parts]
    offs = [sum(counts[:t]) for t in range(len(parts))]
    return offs, counts, sum(counts)


def _inproj_bwd(dparts, wt, x, nw, dres, chip_sums):
    s, d = x.shape
    tm, tk = 1024, 1024
    offs, counts, nk = _col_blocks(dparts, tk)
    npart, nx = len(dparts), len(chip_sums)
    ni = s // tm

    def body(*refs):
        dp_refs = refs[:npart]
        w_ref, x_ref, nw_ref, dres_ref = refs[npart:npart + 4]
        cs_in = refs[npart + 4:npart + 4 + nx]
        gx_ref, gnw_ref = refs[npart + 4 + nx:npart + 6 + nx]
        cs_out = refs[npart + 6 + nx:npart + 6 + 2 * nx]
        acc, send_sems, recv_sems, local_sems = refs[npart + 6 + 2 * nx:]
        i, k = pl.program_id(0), pl.program_id(1)

        @pl.when(jnp.logical_and(i == 0, k == 0))
        def _():
            gnw_ref[...] = jnp.zeros_like(gnw_ref)
            if nx:
                mine, sends, _ = _chip_exchange_copies(cs_in, cs_out, send_sems, recv_sems, local_sems)
                for cp in mine + sends:
                    cp.start()

        @pl.when(jnp.logical_and(i == ni - 1, k == nk - 1))
        def _():
            if nx:
                mine, sends, recvs = _chip_exchange_copies(cs_in, cs_out, send_sems, recv_sems, local_sems)
                for cp in recvs:
                    cp.wait_recv()
                for cp in sends:
                    cp.wait_send()
                for cp in mine:
                    cp.wait()

        @pl.when(k == 0)
        def _():
            acc[...] = jnp.zeros_like(acc)

        for t in range(npart):
            @pl.when(jnp.logical_and(k >= offs[t], k < offs[t] + counts[t]))
            def _(t=t):
                acc[...] += _nn(dp_refs[t][...], w_ref[...])

        @pl.when(k == nk - 1)
        def _():
            xv = x_ref[...]
            r = lax.rsqrt(jnp.mean(xv * xv, axis=-1, keepdims=True) + EPS)
            xn = xv * r
            du = acc[...]
            gnw_ref[0:1, :] += jnp.sum(du * xn, axis=0, keepdims=True)
            dn = du * nw_ref[...]
            gx_ref[...] = dres_ref[...] + r * (dn - xn * jnp.mean(dn * xn, axis=-1, keepdims=True))

    def piece(t):
        return pl.BlockSpec((tm, tk), lambda i, k: (i, jnp.clip(k - offs[t], 0, counts[t] - 1)))

    anyspec = pl.BlockSpec(memory_space=pl.ANY)
    outs = pl.pallas_call(
        body, name="inproj_bwd", grid=(ni, nk),
        in_specs=[piece(t) for t in range(npart)] + [
            pl.BlockSpec((tk, d), lambda i, k: (k, 0)),
            pl.BlockSpec((tm, d), lambda i, k: (i, 0)), pl.BlockSpec((1, d), lambda i, k: (0, 0)),
            pl.BlockSpec((tm, d), lambda i, k: (i, 0))] + [anyspec] * nx,
        out_specs=[pl.BlockSpec((tm, d), lambda i, k: (i, 0)), pl.BlockSpec((8, d), lambda i, k: (0, 0))] + [anyspec] * nx,
        out_shape=[SDS((s, d), f32), SDS((8, d), f32)] + [SDS(a.shape, a.dtype) for a in chip_sums],
        scratch_shapes=[pltpu.VMEM((tm, d), f32)] + _chip_exchange_scratch(max(nx, 1)),
        compiler_params=pltpu.CompilerParams(dimension_semantics=("arbitrary", "arbitrary")),
    )(*dparts, wt, x, nw, dres, *chip_sums)
    return outs[0], outs[1], outs[2:]


def _matmul_tn(a_parts, b_parts, name):
    tile, tk = 1024, 1024
    s = a_parts[0].shape[0]
    nk = s // tk
    na, nb = len(a_parts), len(b_parts)
    offs_a, counts_a, ni = _col_blocks(a_parts, tile)
    offs_b, counts_b, nj = _col_blocks(b_parts, tile)

    def body(*refs):
        a_refs, b_refs, o_ref = refs[:na], refs[na:na + nb], refs[na + nb]
        i, j = pl.program_id(0), pl.program_id(1)

        @pl.when(pl.program_id(2) == 0)
        def _():
            o_ref[...] = jnp.zeros_like(o_ref)

        for ta in range(na):
            for tb in range(nb):
                in_a = jnp.logical_and(i >= offs_a[ta], i < offs_a[ta] + counts_a[ta])
                in_b = jnp.logical_and(j >= offs_b[tb], j < offs_b[tb] + counts_b[tb])

                @pl.when(jnp.logical_and(in_a, in_b))
                def _(ta=ta, tb=tb):
                    o_ref[...] += _tn(a_refs[ta][...], b_refs[tb][...])

    def spec(offs, counts, t, axis):
        def index(i, j, k):
            pos = (i, j)[axis]
            mine = jnp.logical_and(pos >= offs[t], pos < offs[t] + counts[t])
            return jnp.where(mine, k, 0), jnp.clip(pos - offs[t], 0, counts[t] - 1)
        return pl.BlockSpec((tk, tile), index)

    return pl.pallas_call(
        body, name=name, grid=(ni, nj, nk),
        in_specs=[spec(offs_a, counts_a, t, 0) for t in range(na)] + [spec(offs_b, counts_b, t, 1) for t in range(nb)],
        out_specs=pl.BlockSpec((tile, tile), lambda i, j, k: (i, j)),
        out_shape=SDS((ni * tile, nj * tile), f32),
        compiler_params=pltpu.CompilerParams(dimension_semantics=("parallel", "parallel", "arbitrary")),
    )(*a_parts, *b_parts)


def _adamw(w, g, m, v):
    m = ADAM_B1 * m + (1.0 - ADAM_B1) * g
    v = ADAM_B2 * v + (1.0 - ADAM_B2) * (g * g)
    m_hat = m / (1.0 - ADAM_B1 ** ADAM_STEP)
    v_hat = v / (1.0 - ADAM_B2 ** ADAM_STEP)
    delta = -ADAM_LR * (m_hat / (jnp.sqrt(v_hat) + ADAM_EPS) + ADAM_WD * w)
    return delta, m, v


def _sum_adamw(parts, w, m, v, name):
    r, c = w.shape
    tc = 256

    def body(p_ref, w_ref, m_ref, v_ref, g_ref, d_ref, nm_ref, nv_ref):
        g = p_ref[0].astype(f32)
        for q in range(1, 4):
            g = g + p_ref[q].astype(f32)
        g_ref[...] = g
        d_ref[...], nm_ref[...], nv_ref[...] = _adamw(w_ref[...], g, m_ref[...], v_ref[...])

    blk = pl.BlockSpec((r, tc), lambda i: (0, i))
    return pl.pallas_call(
        body, name=name, grid=(c // tc,),
        in_specs=[pl.BlockSpec((4, r, tc), lambda i: (0, 0, i)), blk, blk, blk],
        out_specs=[blk] * 4, out_shape=[SDS((r, c), f32)] * 4,
        compiler_params=pltpu.CompilerParams(dimension_semantics=("parallel",)),
    )(parts, w, m, v)


def _sum_small(parts):
    def body(p_ref, o_ref):
        t = p_ref[0]
        for j in range(1, N_DEV):
            t = t + p_ref[j]
        o_ref[...] = t
        row_h = _iota((D_SSM, LANES), 0) // HEAD_DIM
        fold = (row_h == _iota((D_SSM, LANES), 1)).astype(f32)
        lower = t[8:16, 0:LANES]
        folded = _nn_hi(t[8:16, 0:D_SSM], fold)
        loss = jnp.sum(t[11:12, 0:D_MODEL], axis=1, keepdims=True) * (0.5 / D_MODEL)
        row = _iota((8, LANES), 0)
        o_ref[8:16, 0:LANES] = jnp.where(row < 2, folded, jnp.where(row == 4, loss, lower))

    return pl.pallas_call(body, name="sum_small", out_shape=SDS((PACK_ROWS, PACK_W), f32),
                          in_specs=[pl.BlockSpec(memory_space=pltpu.VMEM)],
                          out_specs=pl.BlockSpec(memory_space=pltpu.VMEM))(parts)


def _adamw_small(w, g, m, v):
    def body(w_ref, g_ref, m_ref, v_ref, d_ref, nm_ref, nv_ref):
        d_ref[...], nm_ref[...], nv_ref[...] = _adamw(w_ref[...], g_ref[...], m_ref[...], v_ref[...])

    vm = pl.BlockSpec(memory_space=pltpu.VMEM)
    return pl.pallas_call(body, name="adamw_small", out_shape=[SDS(w.shape, f32)] * 3,
                          in_specs=[vm] * 4, out_specs=[vm] * 3)(w, g, m, v)


def _pad_lanes(v, width):
    return jnp.pad(v, ((0, 0), (0, width - v.shape[1])))


def _local_step(x, tgt, norm_pre_w, wt, conv_w, conv_b, dt_bias, a_log, d_skip, ssm_norm_w, wo, norm_post_w,
                weight_grads):
    dtb16 = _pad_lanes(dt_bias, LANES)
    alog16 = _pad_lanes(a_log, LANES)
    alog_f = jnp.repeat(a_log, HEAD_DIM, axis=1)
    d_f = jnp.repeat(d_skip, HEAD_DIM, axis=1)

    proj, u = _prenorm_inproj(x, norm_pre_w, wt)
    o, lb, mix_a = _attn_fwd(proj)
    mix_s, y, states = _ssd_fwd(proj, conv_w, conv_b, dtb16, alog16, alog_f, d_f, ssm_norm_w)
    dmix, dout, dres, acc_post = _outproj_loss(mix_a, mix_s, wo, x, tgt, norm_post_w)
    dq, dk, dv, dg = _attn_bwd(proj, o, lb, dmix)
    dzxd, g_conv, g_vec, g_dt = _ssd_bwd(proj, y, states, dmix, conv_w, conv_b, dtb16, alog16, alog_f, d_f, ssm_norm_w)
    dparts = [dq, dk, dv, dg, dzxd]
    dw_out = _matmul_tn([mix_a, mix_s], [dout], "dw_out")
    chip_sums, carry = weight_grads(dparts, u, dw_out)
    grad_x, g_pre, exchanged = _inproj_bwd(dparts, wt, x, norm_pre_w, dres, chip_sums)

    rows = [g_conv[0:5], _pad_lanes(g_pre[0:1], PACK_W), _pad_lanes(g_vec[0:1], PACK_W),
            _pad_lanes(acc_post[1:2], PACK_W), _pad_lanes(g_vec[1:3], PACK_W), _pad_lanes(g_dt[0:1], PACK_W),
            _pad_lanes(acc_post[0:1], PACK_W), jnp.zeros((4, PACK_W), f32)]
    return grad_x, carry, exchanged, jnp.concatenate(rows, axis=0)


def kernel(x, norm_pre_w, w_in, conv_w, conv_b, dt_bias, a_log, d_skip, ssm_norm_w, w_out, norm_post_w, loss_target, m_norm_pre_w, m_w_in, m_conv_w, m_conv_b, m_dt_bias, m_a_log, m_d_skip, m_ssm_norm_w, m_w_out, m_norm_post_w, v_norm_pre_w, v_w_in, v_conv_w, v_conv_b, v_dt_bias, v_a_log, v_d_skip, v_ssm_norm_w, v_w_out, v_norm_post_w):
    shard_in = w_in.shape[2]
    shard_cv = conv_w.shape[2]
    me = 4 * lax.axis_index("x") + 2 * lax.axis_index("y") + lax.axis_index("c")

    g_in, g_out, g_cw = _all_gather([w_in[0].T.astype(bf16), w_out[0].astype(bf16), conv_w[0]])
    wt = jnp.pad(g_in.reshape(N_DEV * shard_in, D_MODEL), ((0, NP - N_DEV * shard_in), (0, 0)))
    wo = g_out.reshape(N_DEV * w_out.shape[1], D_MODEL)
    cw = g_cw.transpose(1, 0, 2).reshape(4, D_CONV)

    def weight_grads(dparts, u, dw_out):
        dw_in, got_in, got_out = _dw_in_swap(dparts, u, dw_out)
        sum_in = _chip_sum(dw_in, got_in, "chip_sum_w_in")[:N_DEV * shard_in].reshape(4, 2, shard_in, D_MODEL)
        sum_out = _chip_sum(dw_out, got_out, "chip_sum_w_out").reshape(4, 2, w_out.shape[1], D_MODEL)
        core = lax.axis_index("c")
        return [lax.dynamic_index_in_dim(a, core, axis=1, keepdims=False) for a in (sum_in, sum_out)], ()

    grad_x, _, (parts_in, parts_out), pack = _local_step(
        x[0], loss_target[0], norm_pre_w, wt, cw, conv_b, dt_bias, a_log, d_skip, ssm_norm_w, wo, norm_post_w,
        weight_grads)
    parts_small = _gather_small(pack)

    g_w_in, d_w_in, nm_w_in, nv_w_in = (a.T for a in _sum_adamw(
        parts_in, w_in[0].T, m_w_in[0].T, v_w_in[0].T, "sum_adamw_w_in"))
    g_w_out, d_w_out, nm_w_out, nv_w_out = _sum_adamw(parts_out, w_out[0], m_w_out[0], v_w_out[0], "sum_adamw_w_out")
    tot = _sum_small(parts_small)

    g_cw_all = tot[0:4]
    small_g = {
        "conv_w": lax.dynamic_slice(g_cw_all, (0, me * shard_cv), (4, shard_cv)),
        "conv_b": tot[4:5], "norm_pre_w": tot[5:6, :D_MODEL], "ssm_norm_w": tot[6:7, :D_SSM],
        "norm_post_w": tot[7:8, :D_MODEL], "a_log": tot[8:9, :16], "d_skip": tot[9:10, :16], "dt_bias": tot[10:11, :16],
    }
    loss = tot[12, 0]
    small_w = {"conv_w": (conv_w[0], m_conv_w[0], v_conv_w[0]), "conv_b": (conv_b, m_conv_b, v_conv_b),
               "norm_pre_w": (norm_pre_w, m_norm_pre_w, v_norm_pre_w), "ssm_norm_w": (ssm_norm_w, m_ssm_norm_w, v_ssm_norm_w),
               "norm_post_w": (norm_post_w, m_norm_post_w, v_norm_post_w), "a_log": (a_log, m_a_log, v_a_log),
               "d_skip": (d_skip, m_d_skip, v_d_skip), "dt_bias": (dt_bias, m_dt_bias, v_dt_bias)}
    names = list(small_w)
    sizes = [small_g[k].size for k in names]
    tot_size = sum(sizes)
    pad_to = -(-tot_size // 1024) * 1024

    def flat(arrs):
        v = jnp.concatenate([a.reshape(-1) for a in arrs])
        return jnp.pad(v, (0, pad_to - tot_size)).reshape(pad_to // LANES, LANES)

    fw = flat([small_w[k][0] for k in names])
    fg = flat([small_g[k] for k in names])
    fm = flat([small_w[k][1] for k in names])
    fv = jnp.pad(jnp.concatenate([small_w[k][2].reshape(-1) for k in names]), (0, pad_to - tot_size),
                 constant_values=1.0).reshape(pad_to // LANES, LANES)
    fd, fnm, fnv = _adamw_small(fw, fg, fm, fv)

    def unflat(f):
        out, off = {}, 0
        v = f.reshape(-1)
        for k, n in zip(names, sizes):
            out[k] = v[off:off + n].reshape(small_g[k].shape)
            off += n
        return out

    sd, snm, snv = unflat(fd), unflat(fnm), unflat(fnv)
    lead = lambda a: a[None]
    order = ["norm_pre_w", "w_in", "conv_w", "conv_b", "dt_bias", "a_log", "d_skip", "ssm_norm_w", "w_out", "norm_post_w"]
    grads = dict(small_g, w_in=g_w_in, w_out=g_w_out)
    deltas = dict(sd, w_in=d_w_in, w_out=d_w_out)
    new_m = dict(snm, w_in=nm_w_in, w_out=nm_w_out)
    new_v = dict(snv, w_in=nv_w_in, w_out=nv_w_out)

    def shaped(dct, k):
        a = dct[k]
        return lead(a) if k in ("w_in", "w_out", "conv_w") else a

    return (loss, grad_x[None], *[shaped(grads, k) for k in order], *[shaped(deltas, k) for k in order],
            *[shaped(new_m, k) for k in order], *[shaped(new_v, k) for k in order])
```

```python
import functools
import math

import jax
import jax.numpy as jnp
import numpy as np
from jax import lax
from jax.experimental import pallas as pl
from jax.experimental.pallas import tpu as pltpu

f32, bf16 = jnp.float32, jnp.bfloat16
SDS = jax.ShapeDtypeStruct
HIGHEST = lax.Precision.HIGHEST
MESH = pl.DeviceIdType.MESH

N_DEV = 8
D_MODEL = 1024
D_ATTN = 1024
D_SSM = 1024
HEAD_DIM = 64
N_PAIRS = 8
D_STATE = 128
N_GROUPS = 2
D_CONV = D_SSM + 2 * N_GROUPS * D_STATE
D_IN_PROJ = 4 * D_ATTN + D_SSM + D_CONV + 16
NP = 7168
CHUNK = 128
BLK = 128
DILATIONS = (1, 4, 16)
EPS = 1e-6
LANES = 128
COL_Z, COL_XS, COL_BC, COL_DT = 4096, 5120, 6144, 6656

ADAM_LR, ADAM_B1, ADAM_B2, ADAM_EPS, ADAM_WD, ADAM_STEP = 0.001, 0.9, 0.999, 1e-08, 0.01, 10

PACK_ROWS, PACK_W = 16, 1536


def _nt(a, b):
    return lax.dot_general(a, b, (((1,), (1,)), ((), ())), preferred_element_type=f32)


def _tn(a, b):
    return lax.dot_general(a, b, (((0,), (0,)), ((), ())), preferred_element_type=f32)


def _nn(a, b):
    return jnp.dot(a, b, preferred_element_type=f32)


def _nn_hi(a, b):
    return jnp.dot(a, b, precision=HIGHEST, preferred_element_type=f32)


def _sigmoid(x):
    return 1.0 / (1.0 + jnp.exp(-x))


def _softplus(x):
    return jnp.maximum(x, 0.0) + jnp.log1p(jnp.exp(-jnp.abs(x)))


def _iota(shape, dim):
    return lax.broadcasted_iota(jnp.int32, shape, dim)


def _my_pos():
    return lax.axis_index("x"), lax.axis_index("y"), lax.axis_index("c")


def _all_gather(arrs):
    n = len(arrs)
    ns = 9

    def body(*refs):
        ins, outs = refs[:n], refs[n:2 * n]
        send_sems, recv_sems, local_sems = refs[2 * n:]
        x, y, c = _my_pos()
        me, sibling = (x, y, c), (x, y, 1 - c)
        xn, yn, diag = (1 - x, y), (x, 1 - y), (1 - x, 1 - y)

        def slot(a, px, py, pc):
            return outs[a].at[4 * px + 2 * py + pc]

        def part(a, ref, h):
            width = arrs[a].shape[-1]
            if width % (2 * LANES):
                return ref if h == 1 else None
            return ref.at[:, pl.ds(h * (width // 2), width // 2)]

        def copy(a, k, block, to, src=None, h=None):
            src_ref = slot(a, *block) if src is None else src
            dst_ref = slot(a, *block)
            if h is not None:
                src_ref, dst_ref = part(a, src_ref, h), part(a, dst_ref, h)
                if src_ref is None:
                    return None
            return pltpu.make_async_remote_copy(
                src_ref=src_ref, dst_ref=dst_ref, send_sem=send_sems.at[ns * a + k], recv_sem=recv_sems.at[ns * a + k],
                device_id=to, device_id_type=MESH)

        mine = [pltpu.make_async_copy(ins[a], slot(a, *me), local_sems.at[a]) for a in range(n)]
        for cp in mine:
            cp.start()
        sends = []
        for a in range(n):
            sends += [copy(a, 0, me, sibling, src=ins[a]), copy(a, 1, me, (*xn, c), src=ins[a]),
                      copy(a, 2, me, (*yn, c), src=ins[a])]
        for cp in sends:
            cp.start()

        def start(cp):
            if cp is not None:
                cp.start()
                sends.append(cp)

        for a in range(n):
            copy(a, 1, (*xn, c), me).wait_recv()
            start(copy(a, 4, (*xn, c), sibling))
            start(copy(a, 7, (*xn, c), (*yn, c), h=1))
        for a in range(n):
            copy(a, 2, (*yn, c), me).wait_recv()
            start(copy(a, 5, (*yn, c), sibling))
            start(copy(a, 8, (*yn, c), (*xn, c), h=0))
        for a in range(n):
            for k, h in ((8, 0), (7, 1)):
                cp = copy(a, k, (*diag, c), me, h=h)
                if cp is not None:
                    cp.wait_recv()
            start(copy(a, 6, (*diag, c), sibling))
        for a in range(n):
            copy(a, 0, sibling, me).wait_recv()
            for j, chip in enumerate((xn, yn, diag)):
                copy(a, 4 + j, (*chip, 1 - c), me).wait_recv()
        for cp in sends:
            cp.wait_send()
        for cp in mine:
            cp.wait()

    anyspec = pl.BlockSpec(memory_space=pl.ANY)
    return pl.pallas_call(
        body, name="weights_all_gather",
        out_shape=[SDS((N_DEV,) + a.shape, a.dtype) for a in arrs],
        in_specs=[anyspec] * n, out_specs=[anyspec] * n,
        scratch_shapes=[pltpu.SemaphoreType.DMA((ns * n,)), pltpu.SemaphoreType.DMA((ns * n,)),
                        pltpu.SemaphoreType.DMA((n,))],
    )(*arrs)


def _dw_in_swap(a_parts, u, dw_out):
    tile, tk = 1024, 1024
    s = u.shape[0]
    nk = s // tk
    na = len(a_parts)
    offs, counts, ni = _col_blocks(a_parts, tile)

    def body(*refs):
        a_refs, u_ref, dwo_ref = refs[:na], refs[na], refs[na + 1]
        dw_ref, got_ref, goto_ref = refs[na + 2:na + 5]
        acc, local_sems, send_sems, recv_sem, o_send, o_recv = refs[na + 5:]
        i, k = pl.program_id(0), pl.program_id(1)
        x, y, c = _my_pos()
        par = i % 2

        def tile_copies(t, p):
            rows = pl.ds(pl.multiple_of(t * tile, tile), tile)
            loc = pltpu.make_async_copy(acc.at[p], dw_ref.at[rows], local_sems.at[p])
            rem = pltpu.make_async_remote_copy(
                src_ref=acc.at[p], dst_ref=got_ref.at[rows], send_sem=send_sems.at[p], recv_sem=recv_sem,
                device_id=(x, y, 1 - c), device_id_type=MESH)
            return loc, rem

        out_copy = pltpu.make_async_remote_copy(
            src_ref=dwo_ref, dst_ref=goto_ref, send_sem=o_send, recv_sem=o_recv,
            device_id=(x, y, 1 - c), device_id_type=MESH)

        @pl.when(jnp.logical_and(i == 0, k == 0))
        def _():
            out_copy.start()

        @pl.when(k == 0)
        def _():
            @pl.when(i >= 2)
            def _():
                loc, rem = tile_copies(i - 2, par)
                loc.wait()
                rem.wait_send()
            acc[par] = jnp.zeros((tile, tile), f32)

        for t in range(na):
            @pl.when(jnp.logical_and(i >= offs[t], i < offs[t] + counts[t]))
            def _(t=t):
                acc[par] += _tn(a_refs[t][...], u_ref[...])

        @pl.when(k == nk - 1)
        def _():
            loc, rem = tile_copies(i, par)
            loc.start()
            rem.start()

        @pl.when(jnp.logical_and(i == ni - 1, k == nk - 1))
        def _():
            for t in (ni - 2, ni - 1):
                loc, rem = tile_copies(t, t % 2)
                loc.wait()
                rem.wait_send()
            pltpu.make_async_remote_copy(src_ref=dw_ref, dst_ref=got_ref, send_sem=send_sems.at[0], recv_sem=recv_sem,
                                         device_id=(x, y, c), device_id_type=MESH).wait_recv()
            out_copy.wait_send()
            out_copy.wait_recv()

    def a_spec(t):
        def index(i, k):
            mine = jnp.logical_and(i >= offs[t], i < offs[t] + counts[t])
            return jnp.where(mine, k, 0), jnp.clip(i - offs[t], 0, counts[t] - 1)
        return pl.BlockSpec((tk, tile), index)

    anyspec = pl.BlockSpec(memory_space=pl.ANY)
    return pl.pallas_call(
        body, name="dw_in_swap", grid=(ni, nk),
        in_specs=[a_spec(t) for t in range(na)] + [pl.BlockSpec((tk, tile), lambda i, k: (k, 0)), anyspec],
        out_specs=[anyspec] * 3,
        out_shape=[SDS((ni * tile, tile), f32), SDS((ni * tile, tile), f32), SDS(dw_out.shape, dw_out.dtype)],
        scratch_shapes=[pltpu.VMEM((2, tile, tile), f32), pltpu.SemaphoreType.DMA((2,)), pltpu.SemaphoreType.DMA((2,)),
                        pltpu.SemaphoreType.DMA(()), pltpu.SemaphoreType.DMA(()), pltpu.SemaphoreType.DMA(())],
        compiler_params=pltpu.CompilerParams(dimension_semantics=("arbitrary", "arbitrary")),
    )(*a_parts, u, dw_out)


def _gather_small(small):
    def body(small_in, small_out, send_sems, recv_sems, local_sem):
        x, y, c = _my_pos()
        me = 4 * x + 2 * y + c
        mine = pltpu.make_async_copy(small_in, small_out.at[me], local_sem)
        mine.start()
        sends = []
        for k in range(1, N_DEV):
            to = (me + k) % N_DEV
            cp = pltpu.make_async_remote_copy(
                src_ref=small_in, dst_ref=small_out.at[me], send_sem=send_sems.at[k - 1], recv_sem=recv_sems.at[k - 1],
                device_id=(to // 4, (to // 2) % 2, to % 2), device_id_type=MESH)
            cp.start()
            sends.append(cp)
        for k in range(1, N_DEV):
            frm = (me + N_DEV - k) % N_DEV
            pltpu.make_async_remote_copy(
                src_ref=small_in, dst_ref=small_out.at[frm], send_sem=send_sems.at[k - 1], recv_sem=recv_sems.at[k - 1],
                device_id=(x, y, c), device_id_type=MESH).wait_recv()
        for cp in sends:
            cp.wait_send()
        mine.wait()

    anyspec = pl.BlockSpec(memory_space=pl.ANY)
    return pl.pallas_call(
        body, name="small_grads_gather", out_shape=SDS((N_DEV,) + small.shape, small.dtype),
        in_specs=[anyspec], out_specs=anyspec,
        scratch_shapes=[pltpu.SemaphoreType.DMA((7,)), pltpu.SemaphoreType.DMA((7,)), pltpu.SemaphoreType.DMA(())],
    )(small)


def _chip_sum(mine, got, rows, name):
    r, cdim = mine.shape
    tc = LANES

    def body(m_ref, g_ref, s16_ref):
        c = lax.axis_index("c")
        for q in range(4):
            blk = pl.ds(rows * (2 * q + c), rows)
            s16_ref[q] = (m_ref[blk, :] + g_ref[blk, :]).astype(bf16)

    col = pl.BlockSpec((r, tc), lambda i: (0, i))
    return pl.pallas_call(
        body, name=name, grid=(cdim // tc,), in_specs=[col, col],
        out_specs=pl.BlockSpec((4, rows, tc), lambda i: (0, 0, i)), out_shape=SDS((4, rows, cdim), bf16),
        compiler_params=pltpu.CompilerParams(dimension_semantics=("parallel",)),
    )(mine, got)


def _assemble_wt(shards):
    nd, rows, cdim = shards.shape
    tc = 256

    def body(g_ref, o_ref):
        for j in range(nd):
            o_ref[pl.ds(rows * j, rows), :] = g_ref[j]
        o_ref[pl.ds(nd * rows, NP - nd * rows), :] = jnp.zeros((NP - nd * rows, tc), shards.dtype)

    return pl.pallas_call(
        body, name="assemble_w_in", grid=(cdim // tc,),
        in_specs=[pl.BlockSpec((nd, rows, tc), lambda i: (0, 0, i))],
        out_specs=pl.BlockSpec((NP, tc), lambda i: (0, i)), out_shape=SDS((NP, cdim), shards.dtype),
        compiler_params=pltpu.CompilerParams(dimension_semantics=("parallel",)),
    )(shards)


def _chip_exchange_copies(ins, outs, send_sems, recv_sems, local_sems):
    nb = len(ins)
    x, y, c = _my_pos()
    my_q = 2 * x + y
    mine = [pltpu.make_async_copy(ins[a].at[my_q], outs[a].at[my_q], local_sems.at[a]) for a in range(nb)]
    sends, recvs = [], []
    for k in range(1, 4):
        to, frm = (my_q + k) % 4, (my_q + 4 - k) % 4
        for a in range(nb):
            sems = dict(send_sem=send_sems.at[3 * a + k - 1], recv_sem=recv_sems.at[3 * a + k - 1], device_id_type=MESH)
            sends.append(pltpu.make_async_remote_copy(
                src_ref=ins[a].at[to], dst_ref=outs[a].at[my_q], device_id=(to // 2, to % 2, c), **sems))
            recvs.append(pltpu.make_async_remote_copy(
                src_ref=ins[a].at[frm], dst_ref=outs[a].at[frm], device_id=(x, y, c), **sems))
    return mine, sends, recvs


def _chip_exchange_scratch(nb):
    return [pltpu.SemaphoreType.DMA((3 * nb,)), pltpu.SemaphoreType.DMA((3 * nb,)), pltpu.SemaphoreType.DMA((nb,))]


def _prenorm_inproj(x, nw, wt):
    s, d = x.shape
    npad = wt.shape[0]
    tm, tn = 1024, 1024

    def body(x_ref, nw_ref, w_ref, proj_ref, u_ref):
        @pl.when(pl.program_id(1) == 0)
        def _():
            xv = x_ref[...]
            r = lax.rsqrt(jnp.mean(xv * xv, axis=-1, keepdims=True) + EPS)
            u_ref[...] = (xv * r * nw_ref[...]).astype(bf16)
        proj_ref[...] = _nt(u_ref[...], w_ref[...])

    return pl.pallas_call(
        body, name="prenorm_inproj", grid=(s // tm, npad // tn),
        in_specs=[pl.BlockSpec((tm, d), lambda i, j: (i, 0)), pl.BlockSpec((1, d), lambda i, j: (0, 0)),
                  pl.BlockSpec((tn, d), lambda i, j: (j, 0))],
        out_specs=[pl.BlockSpec((tm, tn), lambda i, j: (i, j)), pl.BlockSpec((tm, d), lambda i, j: (i, 0))],
        out_shape=[SDS((s, npad), f32), SDS((s, d), bf16)],
        compiler_params=pltpu.CompilerParams(dimension_semantics=("parallel", "arbitrary")),
    )(x, nw, wt)


def _attn_consts():
    head0 = _iota((BLK, LANES), 1) < HEAD_DIM
    tri2 = (_iota((BLK, 2 * LANES), 1) % LANES) <= _iota((BLK, 2 * LANES), 0)
    ones2 = ((_iota((LANES, 2 * LANES), 0) < HEAD_DIM) == (_iota((LANES, 2 * LANES), 1) < LANES)).astype(bf16)
    rmat = ((_iota((2 * LANES, LANES), 0) < LANES) == (_iota((2 * LANES, LANES), 1) < HEAD_DIM)).astype(bf16)
    bones = ((_iota((LANES, LANES), 0) < HEAD_DIM) == (_iota((LANES, LANES), 1) < HEAD_DIM)).astype(bf16)
    return head0, tri2, ones2, rmat, bones


def _stack_heads(x16, head0):
    zero = jnp.zeros_like(x16)
    return jnp.concatenate([jnp.where(head0, x16, zero), jnp.where(head0, zero, x16)], axis=0)


def _split_dot(x, w16):
    hi = x.astype(bf16)
    lo = (x - hi.astype(f32)).astype(bf16)
    return _nn(hi, w16) + _nn(lo, w16)


def _bf16_terms(x, terms):
    out = []
    for _ in range(terms):
        t = x.astype(bf16)
        out.append(t)
        x = x - t.astype(f32)
    return out


def _dot_01(x, w16, terms):
    return sum(_nn(t, w16) for t in _bf16_terms(x, terms))


def _dot_01_left(w16, x, terms):
    return sum(_nn(w16, t) for t in _bf16_terms(x, terms))


def _attn_fwd(proj):
    s = proj.shape[0]
    n_it = s // BLK

    def body(q_ref, k_ref, v_ref, g_ref, o_ref, l_ref, mix_ref, op0, op1, op2, lp0, lp1, lp2,
             s_a, s_b, sd_a, sd_b, p_a, p_b, m_a, m_b, pd_a, pd_b, k_a, k_b, v_a, v_b):
        op_refs, lp_refs = (op0, op1, op2), (lp0, lp1, lp2)
        head0, tri2, ones2, rmat, _ = _attn_consts()
        score_bufs, prob_bufs = ((s_a, sd_a), (s_b, sd_b)), ((p_a, m_a, pd_a), (p_b, m_b, pd_b))
        k_bufs, v_bufs = (k_a, k_b), (v_a, v_b)
        for buf in k_bufs + v_bufs:
            buf[...] = jnp.zeros_like(buf)

        def block_rows(i, d, nb):
            r, blk = i // nb, i % nb
            return pl.ds(blk * (BLK * d) + r, BLK, stride=d), blk > 0

        def unstack(st16):
            return st16[:BLK] + st16[BLK:]

        def scores(i, par, d, nb):
            rows, has_prev = block_rows(i, d, nb)
            s_buf, sd_buf = score_bufs[par]
            qs = q_ref[rows, :] * 0.125
            qs16 = qs.astype(bf16)
            kst_c = _stack_heads(k_ref[rows, :].astype(bf16), head0)
            kst_p = k_bufs[1 - par][...]
            k_bufs[par][...] = kst_c
            sc = _nt(qs16, kst_c)
            sp = _nt(qs16, kst_p)
            s_buf[...] = jnp.where(tri2, sc, jnp.where(has_prev, sp, -jnp.inf))
            sd = _split_dot(qs * unstack(kst_p).astype(f32), ones2)
            sd_buf[...] = jnp.where(has_prev, sd, -jnp.inf)

        def softmax(bufs_in, bufs_out):
            s_buf, sd_buf = bufs_in
            p_buf, m_buf, pd_buf = bufs_out
            sc, sd2 = s_buf[...], sd_buf[...]
            m0 = jnp.max(sc[:, :LANES], axis=1, keepdims=True)
            m1 = jnp.max(sc[:, LANES:], axis=1, keepdims=True)
            m2 = jnp.concatenate([jnp.broadcast_to(m0, (BLK, LANES)), jnp.broadcast_to(m1, (BLK, LANES))], axis=1)
            m2 = jnp.maximum(m2, sd2)
            p_buf[...] = jnp.exp(sc - m2).astype(bf16)
            m_pair = jnp.where(head0, m2[:, :LANES], m2[:, LANES:])
            m_buf[...] = m_pair
            pd_buf[...] = jnp.exp(jnp.where(head0, sd2[:, :LANES], sd2[:, LANES:]) - m_pair)

        def output(i, par, d, nb, p):
            rows, _ = block_rows(i, d, nb)
            p_buf, m_buf, pd_buf = prob_bufs[par]
            vst_c = _stack_heads(v_ref[rows, :].astype(bf16), head0)
            vst_p = v_bufs[1 - par][...]
            v_bufs[par][...] = vst_c
            pt16, pd = p_buf[...], pd_buf[...]
            zero = jnp.zeros_like(pt16)
            o = (_nn(jnp.where(tri2, pt16, zero), vst_c) + _nn(jnp.where(tri2, zero, pt16), vst_p)
                 + pd * unstack(vst_p).astype(f32))
            l = _nn(pt16, rmat) + pd
            op_refs[p][rows, :] = o / l
            lp_refs[p][rows, :] = m_buf[...] + jnp.log(l)

        for p, d in enumerate(DILATIONS):
            nb = s // (BLK * d)
            scores(0, 0, d, nb)
            scores(1, 1, d, nb)
            softmax(score_bufs[0], prob_bufs[0])

            def steps(j, carry, d=d, nb=nb, p=p):
                for par in range(2):
                    t = 2 * j + 2 + par
                    scores(t, par, d, nb)
                    output(t - 2, par, d, nb, p)
                    softmax(score_bufs[1 - par], prob_bufs[1 - par])
                return carry

            lax.fori_loop(0, (n_it - 2) // 2, steps, 0)
            output(n_it - 2, 0, d, nb, p)
            softmax(score_bufs[1], prob_bufs[1])
            output(n_it - 1, 1, d, nb, p)

        def merge(i, carry):
            rows = pl.ds(pl.multiple_of(i * 256, 256), 256)
            l0, l1, l2 = lp0[rows, :], lp1[rows, :], lp2[rows, :]
            m = jnp.maximum(jnp.maximum(l0, l1), l2)
            e0, e1, e2 = jnp.exp(l0 - m), jnp.exp(l1 - m), jnp.exp(l2 - m)
            z = e0 + e1 + e2
            o = (e0 * op0[rows, :] + e1 * op1[rows, :] + e2 * op2[rows, :]) / z
            o_ref[rows, :] = o
            l_ref[rows, :] = m + jnp.log(z)
            g = g_ref[rows, :]
            mix_ref[rows, :] = (o * (g * _sigmoid(g))).astype(bf16)
            return carry

        lax.fori_loop(0, s // 256, merge, 0)

    col = lambda base: pl.BlockSpec((s, LANES), lambda h: (0, base + h))
    return pl.pallas_call(
        body, name="attn_fwd", grid=(N_PAIRS,),
        in_specs=[col(0), col(8), col(16), col(24)],
        out_specs=[col(0), col(0), col(0)],
        out_shape=[SDS((s, D_ATTN), f32), SDS((s, D_ATTN), f32), SDS((s, D_ATTN), bf16)],
        scratch_shapes=[pltpu.VMEM((s, LANES), f32)] * 6 + [pltpu.VMEM((BLK, 2 * LANES), f32)] * 4
        + [pltpu.VMEM((BLK, 2 * LANES), bf16)] * 2 + [pltpu.VMEM((BLK, LANES), f32)] * 4
        + [pltpu.VMEM((2 * BLK, LANES), bf16)] * 4,
        compiler_params=pltpu.CompilerParams(dimension_semantics=("parallel",)),
    )(proj, proj, proj, proj)


def _expand_mat():
    colv = np.arange(2 * D_SSM)
    head = 2 * ((colv % D_SSM) // LANES) + colv // D_SSM
    return jnp.asarray(np.arange(LANES)[:, None] == head[None, :], dtype=bf16)


def _fold_mat():
    return jnp.asarray((np.arange(D_SSM) // HEAD_DIM)[:, None] == np.arange(LANES)[None, :], dtype=bf16)


def _ssd_common(xs_ref, bc_ref, xs_tail, bc_tail, dt_ref, cw_ref, cb_ref, dtb_ref, alog16_ref, emat_ref, xpad, first):
    keep = jnp.where(first, 0.0, 1.0)
    xpad[0:8, 0:D_SSM] = xs_tail[...] * keep
    xpad[0:8, D_SSM:D_CONV] = bc_tail[...] * keep
    xpad[8:8 + CHUNK, 0:D_SSM] = xs_ref[...]
    xpad[8:8 + CHUNK, D_SSM:D_CONV] = bc_ref[...]
    xp = xpad[...]
    taps = [pltpu.roll(xp, 3 - j, 0)[8:8 + CHUNK] for j in range(3)] + [xp[8:8 + CHUNK]]
    cv = cb_ref[...] + cw_ref[0:1, :] * taps[0]
    for j in range(1, 4):
        cv = cv + cw_ref[j:j + 1, :] * taps[j]
    sig = _sigmoid(cv)
    xbc = cv * sig

    pre = dt_ref[...] + dtb_ref[...]
    dt16 = _softplus(pre)
    a16 = -jnp.exp(alog16_ref[...])
    sub, lane = _iota((CHUNK, CHUNK), 0), _iota((CHUNK, CHUNK), 1)
    tri = (sub >= lane).astype(f32)
    al16 = _nn_hi(tri, dt16 * a16)
    al_t = al16.T
    emat = emat_ref[...]
    dt_x = _dot_01(dt16, emat, 3)
    al_x = _dot_01(al16, emat, 3)
    lane_w = _iota((CHUNK, D_SSM), 1)
    even = (lane_w % LANES) < HEAD_DIM
    dt_f = jnp.where(even, dt_x[:, :D_SSM], dt_x[:, D_SSM:])
    al_f = jnp.where(even, al_x[:, :D_SSM], al_x[:, D_SSM:])
    return cv, sig, xbc, pre, dt_f, al_f, al_x, al_t, taps


def _decay_mat(al_x, al_t, pair, h):
    sub, lane = _iota((CHUNK, CHUNK), 0), _iota((CHUNK, CHUNK), 1)
    col = al_x[:, h * D_SSM + pair * LANES: h * D_SSM + (pair + 1) * LANES]
    row = al_t[2 * pair + h: 2 * pair + h + 1, :]
    return jnp.exp(jnp.where(sub >= lane, col - row, -jnp.inf))


def _ssd_in_specs(order):
    blk = lambda w, cb: pl.BlockSpec((CHUNK, w), lambda i: (order(i), cb))
    tail = lambda w, cb: pl.BlockSpec((8, w), lambda i: (jnp.maximum(16 * order(i) - 1, 0), cb))
    return [blk(D_SSM, COL_XS // D_SSM), blk(512, COL_BC // 512), tail(D_SSM, COL_XS // D_SSM),
            tail(512, COL_BC // 512), blk(LANES, COL_DT // LANES), blk(D_SSM, COL_Z // D_SSM)]


def _full(shape):
    return pl.BlockSpec(shape, lambda i: (0,) * len(shape))


def _ssd_fwd(proj, conv_w, conv_b, dtb16, alog16, alog_f, d_f, nw):
    s = proj.shape[0]
    nc = s // CHUNK

    def body(xs_ref, bc_ref, xs_tail, bc_tail, dt_ref, z_ref, cw_ref, cb_ref, dtb_ref, alog16_ref, alogf_ref,
             df_ref, nw_ref, emat_ref, mix_ref, y_ref, st_ref, h_scr, xpad, y_scr):
        c = pl.program_id(0)

        @pl.when(c == 0)
        def _():
            h_scr[...] = jnp.zeros_like(h_scr)

        _, _, xbc, _, dt_f, al_f, al_x, al_t, _ = _ssd_common(
            xs_ref, bc_ref, xs_tail, bc_tail, dt_ref, cw_ref, cb_ref, dtb_ref, alog16_ref, emat_ref, xpad, c == 0)
        head0 = _iota((CHUNK, LANES), 1) < HEAD_DIM
        st_ref[...] = h_scr[...]
        for g in range(N_GROUPS):
            bm = xbc[:, D_SSM + g * D_STATE: D_SSM + (g + 1) * D_STATE].astype(bf16)
            cm = xbc[:, D_SSM + (N_GROUPS + g) * D_STATE: D_SSM + (N_GROUPS + g + 1) * D_STATE].astype(bf16)
            gmat = _nt(cm, bm)
            for pair in range(4 * g, 4 * g + 4):
                sl = slice(pair * LANES, (pair + 1) * LANES)
                xp, dtp, alp = xbc[:, sl], dt_f[:, sl], al_f[:, sl]
                xdt = xp * dtp
                xdt16 = xdt.astype(bf16)
                al_last = alp[CHUNK - 1:CHUNK, :]
                hp = h_scr[:, sl]
                y_off = jnp.exp(alp) * _nn(cm, hp.astype(bf16))
                yd = [_nn((gmat * _decay_mat(al_x, al_t, pair, h)).astype(bf16), xdt16) for h in range(2)]
                y_scr[:, sl] = jnp.where(head0, yd[0], yd[1]) + y_off + df_ref[:, sl] * xp
                st = _tn(bm, (jnp.exp(al_last - alp) * xdt).astype(bf16))
                h_scr[:, sl] = jnp.exp(al_last) * hp + st
        y = y_scr[...]
        y_ref[...] = y
        z = z_ref[...]
        yz = y * (z * _sigmoid(z))
        gw = D_SSM // N_GROUPS
        for g in range(N_GROUPS):
            part = yz[:, g * gw:(g + 1) * gw]
            r = lax.rsqrt(jnp.mean(part * part, axis=-1, keepdims=True) + EPS)
            mix_ref[:, g * gw:(g + 1) * gw] = (part * r * nw_ref[:, g * gw:(g + 1) * gw]).astype(bf16)

    order = lambda i: i
    row = lambda w: pl.BlockSpec((CHUNK, w), lambda i: (i, 0))
    return pl.pallas_call(
        body, name="ssd_fwd", grid=(nc,),
        in_specs=_ssd_in_specs(order) + [_full((4, D_CONV)), _full((1, D_CONV)), _full((1, LANES)), _full((1, LANES)),
                                         _full((1, D_SSM)), _full((1, D_SSM)), _full((1, D_SSM)),
                                         _full((LANES, 2 * D_SSM))],
        out_specs=[row(D_SSM), row(D_SSM), pl.BlockSpec((None, D_STATE, D_SSM), lambda i: (i, 0, 0))],
        out_shape=[SDS((s, D_SSM), bf16), SDS((s, D_SSM), f32), SDS((nc, D_STATE, D_SSM), f32)],
        scratch_shapes=[pltpu.VMEM((D_STATE, D_SSM), f32), pltpu.VMEM((8 + CHUNK, D_CONV), f32),
                        pltpu.VMEM((CHUNK, D_SSM), f32)],
        compiler_params=pltpu.CompilerParams(dimension_semantics=("arbitrary",)),
    )(proj, proj, proj, proj, proj, proj, conv_w, conv_b, dtb16, alog16, alog_f, d_f, nw, _expand_mat())


def _outproj_loss(mix_a, mix_s, wo, x, tgt, npw):
    s, d = x.shape
    tm = 512

    def body(ma_ref, ms_ref, wo_ref, x_ref, t_ref, npw_ref, dmix_ref, dout_ref, dres_ref, acc_ref):
        @pl.when(pl.program_id(0) == 0)
        def _():
            acc_ref[...] = jnp.zeros_like(acc_ref)

        out = _nn(ma_ref[...], wo_ref[0:D_ATTN, :]) + _nn(ms_ref[...], wo_ref[D_ATTN:, :])
        r = lax.rsqrt(jnp.mean(out * out, axis=-1, keepdims=True) + EPS)
        on = out * r
        diff = x_ref[...] + on * npw_ref[...] - t_ref[...]
        dres = diff * (1.0 / d)
        dres_ref[...] = dres
        acc_ref[0:1, :] += jnp.sum(diff * diff, axis=0, keepdims=True)
        acc_ref[1:2, :] += jnp.sum(dres * on, axis=0, keepdims=True)
        dn = dres * npw_ref[...]
        dout = (r * (dn - on * jnp.mean(dn * on, axis=-1, keepdims=True))).astype(bf16)
        dout_ref[...] = dout
        dmix_ref[...] = _nt(dout, wo_ref[...])

    row = lambda w: pl.BlockSpec((tm, w), lambda i: (i, 0))
    return pl.pallas_call(
        body, name="outproj_loss", grid=(s // tm,),
        in_specs=[row(D_ATTN), row(D_SSM), _full((D_ATTN + D_SSM, d)), row(d), row(d), _full((1, d))],
        out_specs=[row(D_ATTN + D_SSM), row(d), row(d), _full((8, d))],
        out_shape=[SDS((s, D_ATTN + D_SSM), f32), SDS((s, d), bf16), SDS((s, d), f32), SDS((8, d), f32)],
        compiler_params=pltpu.CompilerParams(dimension_semantics=("arbitrary",)),
    )(mix_a, mix_s, wo, x, tgt, npw)


def _attn_bwd(proj, o, lb, dmix):
    s = proj.shape[0]
    n_it = s // BLK

    def body(q_ref, k_ref, v_ref, g_ref, o_ref, l_ref, dm_ref, dq_ref, dk_ref, dv_ref, dg_ref,
             dq_acc, dk_acc, dv_acc, do_scr, dl_scr, *bufs):
        head0, tri2, _, _, bones = _attn_consts()

        def pro(i, carry):
            rows = pl.ds(pl.multiple_of(i * 256, 256), 256)
            g = g_ref[rows, :]
            sg = _sigmoid(g)
            dmx = dm_ref[rows, :]
            ov = o_ref[rows, :]
            dg_ref[rows, :] = (dmx * ov * (sg * (1.0 + g * (1.0 - sg)))).astype(bf16)
            do = dmx * (g * sg)
            do_scr[rows, :] = do
            dl_scr[rows, :] = _split_dot(do * ov, bones)
            z = jnp.zeros((256, LANES), f32)
            dq_acc[rows, :] = z
            dk_acc[rows, :] = z
            dv_acc[rows, :] = z
            return carry

        lax.fori_loop(0, s // 256, pro, 0)

        def per_head(t):
            return jnp.concatenate([t[:, :LANES], t[:, LANES:]], axis=0)

        def both_heads(t):
            tr = pltpu.roll(t, HEAD_DIM, 1)
            return jnp.concatenate([jnp.where(head0, t, tr), jnp.where(head0, tr, t)], axis=1)

        mm_bufs = ((bufs[0], bufs[1], bufs[2], bufs[3]), (bufs[4], bufs[5], bufs[6], bufs[7]))
        ds_bufs = ((bufs[8], bufs[9], bufs[10], bufs[11]), (bufs[12], bufs[13], bufs[14], bufs[15]))
        op_bufs = ((bufs[16], bufs[17], bufs[18], bufs[19]), (bufs[20], bufs[21], bufs[22], bufs[23]))
        vc_bufs, carry_k, carry_v = (bufs[24], bufs[25]), bufs[26], bufs[27]
        for buf in (op_bufs[0][0], op_bufs[1][0]) + vc_bufs:
            buf[...] = jnp.zeros_like(buf)

        def block_rows(i, d, nb):
            r, blk = i // nb, i % nb
            return pl.ds(blk * (BLK * d) + r, BLK, stride=d), blk > 0

        def unstack(st16):
            return st16[:BLK] + st16[BLK:]

        def products(i, par, d, nb):
            rows, has_prev = block_rows(i, d, nb)
            s_buf, dp_buf, sd_buf, dpd_buf = mm_bufs[par]
            kc_buf, kp_buf, q_buf, do_buf = op_bufs[par]
            q = q_ref[rows, :]
            qs = q * 0.125
            do = do_scr[rows, :]
            qs16, do16 = qs.astype(bf16), do.astype(bf16)
            kst_c = _stack_heads(k_ref[rows, :].astype(bf16), head0)
            vst_c = _stack_heads(v_ref[rows, :].astype(bf16), head0)
            kst_p, vst_p = op_bufs[1 - par][0][...], vc_bufs[1 - par][...]
            kc_buf[...] = kst_c
            kp_buf[...] = kst_p
            vc_bufs[par][...] = vst_c
            q_buf[...] = q.astype(bf16)
            do_buf[...] = do16
            s_buf[...] = jnp.where(tri2, _nt(qs16, kst_c), jnp.where(has_prev, _nt(qs16, kst_p), -jnp.inf))
            dp_buf[...] = jnp.where(tri2, _nt(do16, vst_c), jnp.where(has_prev, _nt(do16, vst_p), 0.0))
            sd_buf[...] = _split_dot(qs * unstack(kst_p).astype(f32), bones)
            dpd_buf[...] = jnp.where(has_prev, _split_dot(do * unstack(vst_p).astype(f32), bones), 0.0)

        def softmax_grad(i, par, d, nb):
            rows, has_prev = block_rows(i, d, nb)
            s_buf, dp_buf, sd_buf, dpd_buf = mm_bufs[par]
            p_buf, ds_buf, pd_buf, dsd_buf = ds_bufs[par]
            lse = l_ref[rows, :]
            dl = dl_scr[rows, :]
            pt = jnp.exp(s_buf[...] - both_heads(lse))
            ds_buf[...] = (pt * (dp_buf[...] - both_heads(dl)) * 0.125).astype(bf16)
            p_buf[...] = pt.astype(bf16)
            pd = jnp.where(has_prev, jnp.exp(sd_buf[...] - lse), 0.0)
            pd_buf[...] = pd
            dsd_buf[...] = pd * (dpd_buf[...] - dl) * 0.125

        def accumulate(i, par, d, nb):
            rows, _ = block_rows(i, d, nb)
            before, _ = block_rows(jnp.maximum(i - 1, 0), d, nb)
            p_buf, ds_buf, pd_buf, dsd_buf = ds_bufs[par]
            kc_buf, kp_buf, q_buf, do_buf = op_bufs[par]
            pt16, ds16, pd, dsd = p_buf[...], ds_buf[...], pd_buf[...], dsd_buf[...]
            zero = jnp.zeros_like(pt16)
            dsc, dsp = jnp.where(tri2, ds16, zero), jnp.where(tri2, zero, ds16)
            pc, pp = jnp.where(tri2, pt16, zero), jnp.where(tri2, zero, pt16)
            kst_c, kst_p, q16, do16 = kc_buf[...], kp_buf[...], q_buf[...], do_buf[...]
            qst, dost = _stack_heads(q16, head0), _stack_heads(do16, head0)
            dq_acc[rows, :] += _nn(dsc, kst_c) + _nn(dsp, kst_p) + dsd * unstack(kst_p).astype(f32)
            dk_acc[before, :] += carry_k[...] + _tn(per_head(dsp), qst) + dsd * q16.astype(f32)
            dv_acc[before, :] += carry_v[...] + _tn(per_head(pp), dost) + pd * do16.astype(f32)
            carry_k[...] = _tn(per_head(dsc), qst)
            carry_v[...] = _tn(per_head(pc), dost)

        for d in DILATIONS:
            nb = s // (BLK * d)
            carry_k[...] = jnp.zeros_like(carry_k)
            carry_v[...] = jnp.zeros_like(carry_v)
            products(0, 0, d, nb)
            products(1, 1, d, nb)
            softmax_grad(0, 0, d, nb)

            def steps(j, carry, d=d, nb=nb):
                for par in range(2):
                    t = 2 * j + 2 + par
                    accumulate(t - 2, par, d, nb)
                    products(t, par, d, nb)
                    softmax_grad(t - 1, 1 - par, d, nb)
                return carry

            lax.fori_loop(0, (n_it - 2) // 2, steps, 0)
            accumulate(n_it - 2, 0, d, nb)
            softmax_grad(n_it - 1, 1, d, nb)
            accumulate(n_it - 1, 1, d, nb)
            last, _ = block_rows(n_it - 1, d, nb)
            dk_acc[last, :] += carry_k[...]
            dv_acc[last, :] += carry_v[...]

        def epi(i, carry):
            rows = pl.ds(pl.multiple_of(i * 256, 256), 256)
            dq_ref[rows, :] = dq_acc[rows, :].astype(bf16)
            dk_ref[rows, :] = dk_acc[rows, :].astype(bf16)
            dv_ref[rows, :] = dv_acc[rows, :].astype(bf16)
            return carry

        lax.fori_loop(0, s // 256, epi, 0)

    col = lambda base: pl.BlockSpec((s, LANES), lambda h: (0, base + h))
    outs = pl.pallas_call(
        body, name="attn_bwd", grid=(N_PAIRS,),
        in_specs=[col(0), col(8), col(16), col(24), col(0), col(0), col(0)],
        out_specs=[col(0)] * 4,
        out_shape=[SDS((s, D_ATTN), bf16)] * 4,
        scratch_shapes=[pltpu.VMEM((s, LANES), f32)] * 5
        + [pltpu.VMEM((BLK, 2 * LANES), f32)] * 2 + [pltpu.VMEM((BLK, LANES), f32)] * 2
        + [pltpu.VMEM((BLK, 2 * LANES), f32)] * 2 + [pltpu.VMEM((BLK, LANES), f32)] * 2
        + [pltpu.VMEM((BLK, 2 * LANES), bf16)] * 2 + [pltpu.VMEM((BLK, LANES), f32)] * 2
        + [pltpu.VMEM((BLK, 2 * LANES), bf16)] * 2 + [pltpu.VMEM((BLK, LANES), f32)] * 2
        + [pltpu.VMEM((2 * BLK, LANES), bf16)] * 2 + [pltpu.VMEM((BLK, LANES), bf16)] * 2
        + [pltpu.VMEM((2 * BLK, LANES), bf16)] * 2 + [pltpu.VMEM((BLK, LANES), bf16)] * 2
        + [pltpu.VMEM((2 * BLK, LANES), bf16)] * 2 + [pltpu.VMEM((BLK, LANES), f32)] * 2,
        compiler_params=pltpu.CompilerParams(dimension_semantics=("parallel",)),
    )(proj, proj, proj, proj, o, lb, dmix)
    return outs


def _ssd_bwd(proj, y, states, dmix, conv_w, conv_b, dtb16, alog16, alog_f, d_f, nw):
    s = proj.shape[0]
    nc = s // CHUNK
    gw = D_SSM // N_GROUPS

    def body(xs_ref, bc_ref, xs_tail, bc_tail, dt_ref, z_ref, y_ref, st_ref, dm_ref, cw_ref, cb_ref, dtb_ref,
             alog16_ref, alogf_ref, df_ref, nw_ref, emat_ref, fold_ref, out_ref, gconv_ref, gvec_ref, gdt_ref,
             dh_scr, head_scr, xpad, dcpad, da_scr, dxdt_scr, dbc_scr):
        i = pl.program_id(0)
        c = nc - 1 - i

        @pl.when(i == 0)
        def _():
            dh_scr[...] = jnp.zeros_like(dh_scr)
            head_scr[...] = jnp.zeros_like(head_scr)
            gconv_ref[...] = jnp.zeros_like(gconv_ref)
            gvec_ref[...] = jnp.zeros_like(gvec_ref)
            gdt_ref[...] = jnp.zeros_like(gdt_ref)

        cv, sig, xbc, pre, dt_f, al_f, al_x, al_t, taps = _ssd_common(
            xs_ref, bc_ref, xs_tail, bc_tail, dt_ref, cw_ref, cb_ref, dtb_ref, alog16_ref, emat_ref, xpad, c == 0)
        head0 = _iota((CHUNK, LANES), 1) < HEAD_DIM
        sub = _iota((CHUNK, LANES), 0)
        last_row = sub == CHUNK - 1

        yv, z, dmx = y_ref[...], z_ref[...], dm_ref[...]
        sz = _sigmoid(z)
        silu = z * sz
        yz = yv * silu
        dyz_parts = []
        for g in range(N_GROUPS):
            gs = slice(g * gw, (g + 1) * gw)
            part = yz[:, gs]
            r = lax.rsqrt(jnp.mean(part * part, axis=-1, keepdims=True) + EPS)
            nh = part * r
            gvec_ref[0:1, gs] += jnp.sum(dmx[:, gs] * nh, axis=0, keepdims=True)
            dn = dmx[:, gs] * nw_ref[:, gs]
            dyz_parts.append(r * (dn - nh * jnp.mean(dn * nh, axis=-1, keepdims=True)))
        dyz = jnp.concatenate(dyz_parts, axis=1)
        dy = dyz * silu
        out_ref[:, 0:D_SSM] = (dyz * yv * (sz * (1.0 + z * (1.0 - sz)))).astype(bf16)

        x_all = xbc[:, 0:D_SSM]
        gvec_ref[2:3, :] += jnp.sum(dy * x_all, axis=0, keepdims=True)

        for g in range(N_GROUPS):
            bm = xbc[:, D_SSM + g * D_STATE: D_SSM + (g + 1) * D_STATE].astype(bf16)
            cm = xbc[:, D_SSM + (N_GROUPS + g) * D_STATE: D_SSM + (N_GROUPS + g + 1) * D_STATE].astype(bf16)
            gmat = _nt(cm, bm)
            dgm = jnp.zeros((CHUNK, CHUNK), f32)
            db = jnp.zeros((CHUNK, D_STATE), f32)
            dc = jnp.zeros((CHUNK, D_STATE), f32)
            for pair in range(4 * g, 4 * g + 4):
                sl = slice(pair * LANES, (pair + 1) * LANES)
                xp, dtp, alp, dyp = x_all[:, sl], dt_f[:, sl], al_f[:, sl], dy[:, sl]
                xdt = xp * dtp
                xdt16 = xdt.astype(bf16)
                al_last = alp[CHUNK - 1:CHUNK, :]
                e_l = jnp.exp(alp)
                wf = jnp.exp(al_last - alp)
                e_last = jnp.exp(al_last)
                hp = st_ref[:, sl]
                hp16 = hp.astype(bf16)
                dhn = dh_scr[:, sl]
                dhn16 = dhn.astype(bf16)
                y_off = e_l * _nn(cm, hp16)
                dch16 = (dyp * e_l).astype(bf16)
                dc = dc + _nt(dch16, hp16)
                dh_out = _tn(cm, dch16)
                dal = dyp * y_off
                xw16 = (wf * xdt).astype(bf16)
                db = db + _nt(xw16, dhn16)
                dxw = _nn(bm, dhn16)
                dxdt = dxw * wf
                dwf = dxw * xdt * wf
                dal = dal - dwf
                dal_last = jnp.sum(dwf, axis=0, keepdims=True) + jnp.sum(dhn * hp, axis=0, keepdims=True) * e_last
                dh_scr[:, sl] = e_last * dhn + dh_out
                for h in range(2):
                    mh = head0 if h == 0 else jnp.logical_not(head0)
                    dyh16 = jnp.where(mh, dyp, 0.0).astype(bf16)
                    lmat = _decay_mat(al_x, al_t, pair, h)
                    mm = gmat * lmat
                    dmm = _nt(dyh16, xdt16)
                    dxdt = dxdt + _tn(mm.astype(bf16), dyh16)
                    n16 = (dmm * mm).astype(bf16)
                    jh = jnp.where(mh, 1.0 / HEAD_DIM, 0.0).astype(bf16)
                    dal = dal + _nn(n16, jh) - _tn(n16, jh)
                    dgm = dgm + dmm * lmat
                da_scr[:, sl] = dal + jnp.where(last_row, dal_last, 0.0)
                dxdt_scr[:, sl] = dxdt
            dgm16 = dgm.astype(bf16)
            dbc_scr[:, g * D_STATE:(g + 1) * D_STATE] = db + _tn(dgm16, cm)
            dbc_scr[:, (N_GROUPS + g) * D_STATE:(N_GROUPS + g + 1) * D_STATE] = dc + _nn(dgm16, bm)

        sub_c, lane_c = _iota((CHUNK, CHUNK), 0), _iota((CHUNK, CHUNK), 1)
        tri_t = (lane_c >= sub_c).astype(bf16)
        dadt = _dot_01_left(tri_t, da_scr[...], 2)
        a_f = -jnp.exp(alogf_ref[...])
        dxdt_all = dxdt_scr[...]
        ddt_f = dxdt_all * x_all + a_f * dadt
        gvec_ref[1:2, :] += jnp.sum(dt_f * dadt, axis=0, keepdims=True) * a_f
        dx = df_ref[...] * dy + dxdt_all * dt_f
        ddt_raw = _dot_01(ddt_f, fold_ref[...], 2) * _sigmoid(pre)
        gdt_ref[0:1, :] += jnp.sum(ddt_raw, axis=0, keepdims=True)
        out_ref[:, D_SSM + D_CONV:D_SSM + D_CONV + LANES] = ddt_raw.astype(bf16)
        out_ref[:, D_SSM + D_CONV + LANES:] = jnp.zeros((CHUNK, 3 * LANES), bf16)

        dsil = sig * (1.0 + cv * (1.0 - sig))
        dcv_x = dx * dsil[:, 0:D_SSM]
        dcv_bc = dbc_scr[...] * dsil[:, D_SSM:]
        dcpad[0:CHUNK, 0:D_SSM] = dcv_x
        dcpad[0:CHUNK, D_SSM:] = dcv_bc
        dcpad[CHUNK:, :] = head_scr[...]
        dcp = dcpad[...]
        dcv = dcp[0:CHUNK]
        gconv_ref[4:5, :] += jnp.sum(dcv, axis=0, keepdims=True)
        draw = cw_ref[3:4, :] * dcv
        for j in range(4):
            gconv_ref[j:j + 1, :] += jnp.sum(dcv * taps[j], axis=0, keepdims=True)
        for j in range(3):
            draw = draw + cw_ref[j:j + 1, :] * pltpu.roll(dcp, CHUNK + 8 - (3 - j), 0)[0:CHUNK]
        head_scr[...] = dcv[0:8]
        out_ref[:, D_SSM:D_SSM + D_CONV] = draw.astype(bf16)

    order = lambda i: nc - 1 - i
    row = lambda w, cb=0: pl.BlockSpec((CHUNK, w), lambda i: (nc - 1 - i, cb))
    return pl.pallas_call(
        body, name="ssd_bwd", grid=(nc,),
        in_specs=_ssd_in_specs(order) + [row(D_SSM), pl.BlockSpec((None, D_STATE, D_SSM), lambda i: (nc - 1 - i, 0, 0)),
                                         row(D_SSM, 1), _full((4, D_CONV)), _full((1, D_CONV)), _full((1, LANES)),
                                         _full((1, LANES)), _full((1, D_SSM)), _full((1, D_SSM)), _full((1, D_SSM)),
                                         _full((LANES, 2 * D_SSM)), _full((D_SSM, LANES))],
        out_specs=[row(3072), _full((8, D_CONV)), _full((8, D_SSM)), _full((8, LANES))],
        out_shape=[SDS((s, 3072), bf16), SDS((8, D_CONV), f32), SDS((8, D_SSM), f32), SDS((8, LANES), f32)],
        scratch_shapes=[pltpu.VMEM((D_STATE, D_SSM), f32), pltpu.VMEM((8, D_CONV), f32),
                        pltpu.VMEM((8 + CHUNK, D_CONV), f32), pltpu.VMEM((8 + CHUNK, D_CONV), f32),
                        pltpu.VMEM((CHUNK, D_SSM), f32), pltpu.VMEM((CHUNK, D_SSM), f32),
                        pltpu.VMEM((CHUNK, 2 * N_GROUPS * D_STATE), f32)],
        compiler_params=pltpu.CompilerParams(dimension_semantics=("arbitrary",)),
    )(proj, proj, proj, proj, proj, proj, y, states, dmix, conv_w, conv_b, dtb16, alog16, alog_f, d_f, nw,
      _expand_mat(), _fold_mat())


def _col_blocks(parts, tile):
    counts = [p.shape[1] // tile for p in parts]
    offs = [sum(counts[:t]) for t in range(len(parts))]
    return offs, counts, sum(counts)


def _inproj_bwd(dparts, wt, x, nw, dres, chip_sums):
    s, d = x.shape
    tm, tk = 1024, 1024
    offs, counts, nk = _col_blocks(dparts, tk)
    npart, nx = len(dparts), len(chip_sums)
    ni = s // tm

    def body(*refs):
        dp_refs = refs[:npart]
        w_ref, x_ref, nw_ref, dres_ref = refs[npart:npart + 4]
        cs_in = refs[npart + 4:npart + 4 + nx]
        gx_ref, gnw_ref = refs[npart + 4 + nx:npart + 6 + nx]
        cs_out = refs[npart + 6 + nx:npart + 6 + 2 * nx]
        acc, send_sems, recv_sems, local_sems = refs[npart + 6 + 2 * nx:]
        i, k = pl.program_id(0), pl.program_id(1)

        @pl.when(jnp.logical_and(i == 0, k == 0))
        def _():
            gnw_ref[...] = jnp.zeros_like(gnw_ref)
            if nx:
                mine, sends, _ = _chip_exchange_copies(cs_in, cs_out, send_sems, recv_sems, local_sems)
                for cp in mine + sends:
                    cp.start()

        @pl.when(jnp.logical_and(i == ni - 1, k == nk - 1))
        def _():
            if nx:
                mine, sends, recvs = _chip_exchange_copies(cs_in, cs_out, send_sems, recv_sems, local_sems)
                for cp in recvs:
                    cp.wait_recv()
                for cp in sends:
                    cp.wait_send()
                for cp in mine:
                    cp.wait()

        @pl.when(k == 0)
        def _():
            acc[...] = jnp.zeros_like(acc)

        for t in range(npart):
            @pl.when(jnp.logical_and(k >= offs[t], k < offs[t] + counts[t]))
            def _(t=t):
                acc[...] += _nn(dp_refs[t][...], w_ref[...])

        @pl.when(k == nk - 1)
        def _():
            xv = x_ref[...]
            r = lax.rsqrt(jnp.mean(xv * xv, axis=-1, keepdims=True) + EPS)
            xn = xv * r
            du = acc[...]
            gnw_ref[0:1, :] += jnp.sum(du * xn, axis=0, keepdims=True)
            dn = du * nw_ref[...]
            gx_ref[...] = dres_ref[...] + r * (dn - xn * jnp.mean(dn * xn, axis=-1, keepdims=True))

    def piece(t):
        return pl.BlockSpec((tm, tk), lambda i, k: (i, jnp.clip(k - offs[t], 0, counts[t] - 1)))

    anyspec = pl.BlockSpec(memory_space=pl.ANY)
    outs = pl.pallas_call(
        body, name="inproj_bwd", grid=(ni, nk),
        in_specs=[piece(t) for t in range(npart)] + [
            pl.BlockSpec((tk, d), lambda i, k: (k, 0)),
            pl.BlockSpec((tm, d), lambda i, k: (i, 0)), pl.BlockSpec((1, d), lambda i, k: (0, 0)),
            pl.BlockSpec((tm, d), lambda i, k: (i, 0))] + [anyspec] * nx,
        out_specs=[pl.BlockSpec((tm, d), lambda i, k: (i, 0)), pl.BlockSpec((8, d), lambda i, k: (0, 0))] + [anyspec] * nx,
        out_shape=[SDS((s, d), f32), SDS((8, d), f32)] + [SDS(a.shape, a.dtype) for a in chip_sums],
        scratch_shapes=[pltpu.VMEM((tm, d), f32)] + _chip_exchange_scratch(max(nx, 1)),
        compiler_params=pltpu.CompilerParams(dimension_semantics=("arbitrary", "arbitrary")),
    )(*dparts, wt, x, nw, dres, *chip_sums)
    return outs[0], outs[1], outs[2:]


def _matmul_tn(a_parts, b_parts, name):
    tile, tk = 1024, 1024
    s = a_parts[0].shape[0]
    nk = s // tk
    na, nb = len(a_parts), len(b_parts)
    offs_a, counts_a, ni = _col_blocks(a_parts, tile)
    offs_b, counts_b, nj = _col_blocks(b_parts, tile)

    def body(*refs):
        a_refs, b_refs, o_ref = refs[:na], refs[na:na + nb], refs[na + nb]
        i, j = pl.program_id(0), pl.program_id(1)

        @pl.when(pl.program_id(2) == 0)
        def _():
            o_ref[...] = jnp.zeros_like(o_ref)

        for ta in range(na):
            for tb in range(nb):
                in_a = jnp.logical_and(i >= offs_a[ta], i < offs_a[ta] + counts_a[ta])
                in_b = jnp.logical_and(j >= offs_b[tb], j < offs_b[tb] + counts_b[tb])

                @pl.when(jnp.logical_and(in_a, in_b))
                def _(ta=ta, tb=tb):
                    o_ref[...] += _tn(a_refs[ta][...], b_refs[tb][...])

    def spec(offs, counts, t, axis):
        def index(i, j, k):
            pos = (i, j)[axis]
            mine = jnp.logical_and(pos >= offs[t], pos < offs[t] + counts[t])
            return jnp.where(mine, k, 0), jnp.clip(pos - offs[t], 0, counts[t] - 1)
        return pl.BlockSpec((tk, tile), index)

    return pl.pallas_call(
        body, name=name, grid=(ni, nj, nk),
        in_specs=[spec(offs_a, counts_a, t, 0) for t in range(na)] + [spec(offs_b, counts_b, t, 1) for t in range(nb)],
        out_specs=pl.BlockSpec((tile, tile), lambda i, j, k: (i, j)),
        out_shape=SDS((ni * tile, nj * tile), f32),
        compiler_params=pltpu.CompilerParams(dimension_semantics=("parallel", "parallel", "arbitrary")),
    )(*a_parts, *b_parts)


def _adamw(w, g, m, v):
    m = ADAM_B1 * m + (1.0 - ADAM_B1) * g
    v = ADAM_B2 * v + (1.0 - ADAM_B2) * (g * g)
    m_hat = m / (1.0 - ADAM_B1 ** ADAM_STEP)
    v_hat = v / (1.0 - ADAM_B2 ** ADAM_STEP)
    delta = -ADAM_LR * (m_hat / (jnp.sqrt(v_hat) + ADAM_EPS) + ADAM_WD * w)
    return delta, m, v


def _sum_adamw(parts, w, m, v, name):
    r, c = w.shape
    tc = 256

    def body(p_ref, w_ref, m_ref, v_ref, g_ref, d_ref, nm_ref, nv_ref):
        g = p_ref[0].astype(f32)
        for q in range(1, 4):
            g = g + p_ref[q].astype(f32)
        g_ref[...] = g
        d_ref[...], nm_ref[...], nv_ref[...] = _adamw(w_ref[...], g, m_ref[...], v_ref[...])

    blk = pl.BlockSpec((r, tc), lambda i: (0, i))
    return pl.pallas_call(
        body, name=name, grid=(c // tc,),
        in_specs=[pl.BlockSpec((4, r, tc), lambda i: (0, 0, i)), blk, blk, blk],
        out_specs=[blk] * 4, out_shape=[SDS((r, c), f32)] * 4,
        compiler_params=pltpu.CompilerParams(dimension_semantics=("parallel",)),
    )(parts, w, m, v)


def _sum_small(parts):
    def body(p_ref, o_ref):
        t = p_ref[0]
        for j in range(1, N_DEV):
            t = t + p_ref[j]
        o_ref[...] = t
        row_h = _iota((D_SSM, LANES), 0) // HEAD_DIM
        fold = (row_h == _iota((D_SSM, LANES), 1)).astype(f32)
        lower = t[8:16, 0:LANES]
        folded = _nn_hi(t[8:16, 0:D_SSM], fold)
        loss = jnp.sum(t[11:12, 0:D_MODEL], axis=1, keepdims=True) * (0.5 / D_MODEL)
        row = _iota((8, LANES), 0)
        o_ref[8:16, 0:LANES] = jnp.where(row < 2, folded, jnp.where(row == 4, loss, lower))

    return pl.pallas_call(body, name="sum_small", out_shape=SDS((PACK_ROWS, PACK_W), f32),
                          in_specs=[pl.BlockSpec(memory_space=pltpu.VMEM)],
                          out_specs=pl.BlockSpec(memory_space=pltpu.VMEM))(parts)


def _adamw_small(w, g, m, v):
    def body(w_ref, g_ref, m_ref, v_ref, d_ref, nm_ref, nv_ref):
        d_ref[...], nm_ref[...], nv_ref[...] = _adamw(w_ref[...], g_ref[...], m_ref[...], v_ref[...])

    vm = pl.BlockSpec(memory_space=pltpu.VMEM)
    return pl.pallas_call(body, name="adamw_small", out_shape=[SDS(w.shape, f32)] * 3,
                          in_specs=[vm] * 4, out_specs=[vm] * 3)(w, g, m, v)


def _pad_lanes(v, width):
    return jnp.pad(v, ((0, 0), (0, width - v.shape[1])))


def _local_step(x, tgt, norm_pre_w, wt, conv_w, conv_b, dt_bias, a_log, d_skip, ssm_norm_w, wo, norm_post_w,
                weight_grads):
    dtb16 = _pad_lanes(dt_bias, LANES)
    alog16 = _pad_lanes(a_log, LANES)
    alog_f = jnp.repeat(a_log, HEAD_DIM, axis=1)
    d_f = jnp.repeat(d_skip, HEAD_DIM, axis=1)

    proj, u = _prenorm_inproj(x, norm_pre_w, wt)
    o, lb, mix_a = _attn_fwd(proj)
    mix_s, y, states = _ssd_fwd(proj, conv_w, conv_b, dtb16, alog16, alog_f, d_f, ssm_norm_w)
    dmix, dout, dres, acc_post = _outproj_loss(mix_a, mix_s, wo, x, tgt, norm_post_w)
    dq, dk, dv, dg = _attn_bwd(proj, o, lb, dmix)
    dzxd, g_conv, g_vec, g_dt = _ssd_bwd(proj, y, states, dmix, conv_w, conv_b, dtb16, alog16, alog_f, d_f, ssm_norm_w)
    dparts = [dq, dk, dv, dg, dzxd]
    dw_out = _matmul_tn([mix_a, mix_s], [dout], "dw_out")
    chip_sums, carry = weight_grads(dparts, u, dw_out)
    grad_x, g_pre, exchanged = _inproj_bwd(dparts, wt, x, norm_pre_w, dres, chip_sums)

    rows = [g_conv[0:5], _pad_lanes(g_pre[0:1], PACK_W), _pad_lanes(g_vec[0:1], PACK_W),
            _pad_lanes(acc_post[1:2], PACK_W), _pad_lanes(g_vec[1:3], PACK_W), _pad_lanes(g_dt[0:1], PACK_W),
            _pad_lanes(acc_post[0:1], PACK_W), jnp.zeros((4, PACK_W), f32)]
    return grad_x, carry, exchanged, jnp.concatenate(rows, axis=0)


def kernel(x, norm_pre_w, w_in, conv_w, conv_b, dt_bias, a_log, d_skip, ssm_norm_w, w_out, norm_post_w, loss_target, m_norm_pre_w, m_w_in, m_conv_w, m_conv_b, m_dt_bias, m_a_log, m_d_skip, m_ssm_norm_w, m_w_out, m_norm_post_w, v_norm_pre_w, v_w_in, v_conv_w, v_conv_b, v_dt_bias, v_a_log, v_d_skip, v_ssm_norm_w, v_w_out, v_norm_post_w):
    shard_in = w_in.shape[2]
    shard_cv = conv_w.shape[2]
    me = 4 * lax.axis_index("x") + 2 * lax.axis_index("y") + lax.axis_index("c")

    g_in, g_out, g_cw = _all_gather([w_in[0].T.astype(bf16), w_out[0].astype(bf16), conv_w[0]])
    wt = _assemble_wt(g_in)
    wo = g_out.reshape(N_DEV * w_out.shape[1], D_MODEL)
    cw = g_cw.transpose(1, 0, 2).reshape(4, D_CONV)

    def weight_grads(dparts, u, dw_out):
        dw_in, got_in, got_out = _dw_in_swap(dparts, u, dw_out)
        return [_chip_sum(dw_in, got_in, shard_in, "chip_sum_w_in"),
                _chip_sum(dw_out, got_out, w_out.shape[1], "chip_sum_w_out")], ()

    grad_x, _, (parts_in, parts_out), pack = _local_step(
        x[0], loss_target[0], norm_pre_w, wt, cw, conv_b, dt_bias, a_log, d_skip, ssm_norm_w, wo, norm_post_w,
        weight_grads)
    parts_small = _gather_small(pack)

    g_w_in, d_w_in, nm_w_in, nv_w_in = (a.T for a in _sum_adamw(
        parts_in, w_in[0].T, m_w_in[0].T, v_w_in[0].T, "sum_adamw_w_in"))
    g_w_out, d_w_out, nm_w_out, nv_w_out = _sum_adamw(parts_out, w_out[0], m_w_out[0], v_w_out[0], "sum_adamw_w_out")
    tot = _sum_small(parts_small)

    g_cw_all = tot[0:4]
    small_g = {
        "conv_w": lax.dynamic_slice(g_cw_all, (0, me * shard_cv), (4, shard_cv)),
        "conv_b": tot[4:5], "norm_pre_w": tot[5:6, :D_MODEL], "ssm_norm_w": tot[6:7, :D_SSM],
        "norm_post_w": tot[7:8, :D_MODEL], "a_log": tot[8:9, :16], "d_skip": tot[9:10, :16], "dt_bias": tot[10:11, :16],
    }
    loss = tot[12, 0]
    small_w = {"conv_w": (conv_w[0], m_conv_w[0], v_conv_w[0]), "conv_b": (conv_b, m_conv_b, v_conv_b),
               "norm_pre_w": (norm_pre_w, m_norm_pre_w, v_norm_pre_w), "ssm_norm_w": (ssm_norm_w, m_ssm_norm_w, v_ssm_norm_w),
               "norm_post_w": (norm_post_w, m_norm_post_w, v_norm_post_w), "a_log": (a_log, m_a_log, v_a_log),
               "d_skip": (d_skip, m_d_skip, v_d_skip), "dt_bias": (dt_bias, m_dt_bias, v_dt_bias)}
    names = list(small_w)
    sizes = [small_g[k].size for k in names]
    tot_size = sum(sizes)
    pad_to = -(-tot_size // 1024) * 1024

    def flat(arrs):
        v = jnp.concatenate([a.reshape(-1) for a in arrs])
        return jnp.pad(v, (0, pad_to - tot_size)).reshape(pad_to // LANES, LANES)

    fw = flat([small_w[k][0] for k in names])
    fg = flat([small_g[k] for k in names])
    fm = flat([small_w[k][1] for k in names])
    fv = jnp.pad(jnp.concatenate([small_w[k][2].reshape(-1) for k in names]), (0, pad_to - tot_size),
                 constant_values=1.0).reshape(pad_to // LANES, LANES)
    fd, fnm, fnv = _adamw_small(fw, fg, fm, fv)

    def unflat(f):
        out, off = {}, 0
        v = f.reshape(-1)
        for k, n in zip(names, sizes):
            out[k] = v[off:off + n].reshape(small_g[k].shape)
            off += n
        return out

    sd, snm, snv = unflat(fd), unflat(fnm), unflat(fnv)
    lead = lambda a: a[None]
    order = ["norm_pre_w", "w_in", "conv_w", "conv_b", "dt_bias", "a_log", "d_skip", "ssm_norm_w", "w_out", "norm_post_w"]
    grads = dict(small_g, w_in=g_w_in, w_out=g_w_out)
    deltas = dict(sd, w_in=d_w_in, w_out=d_w_out)
    new_m = dict(snm, w_in=nm_w_in, w_out=nm_w_out)
    new_v = dict(snv, w_in=nv_w_in, w_out=nv_w_out)

    def shaped(dct, k):
        a = dct[k]
        return lead(a) if k in ("w_in", "w_out", "conv_w") else a

    return (loss, grad_x[None], *[shaped(grads, k) for k in order], *[shaped(deltas, k) for k in order],
            *[shaped(new_m, k) for k in order], *[shaped(new_v, k) for k in order])
```

```python
import functools
import math

import jax
import jax.numpy as jnp
import numpy as np
from jax import lax
from jax.experimental import pallas as pl
from jax.experimental.pallas import tpu as pltpu

f32, bf16 = jnp.float32, jnp.bfloat16
SDS = jax.ShapeDtypeStruct
HIGHEST = lax.Precision.HIGHEST
MESH = pl.DeviceIdType.MESH

N_DEV = 8
D_MODEL = 1024
D_ATTN = 1024
D_SSM = 1024
HEAD_DIM = 64
N_PAIRS = 8
D_STATE = 128
N_GROUPS = 2
D_CONV = D_SSM + 2 * N_GROUPS * D_STATE
D_IN_PROJ = 4 * D_ATTN + D_SSM + D_CONV + 16
NP = 7168
CHUNK = 128
BLK = 128
DILATIONS = (1, 4, 16)
EPS = 1e-6
LANES = 128
COL_Z, COL_XS, COL_BC, COL_DT = 4096, 5120, 6144, 6656

ADAM_LR, ADAM_B1, ADAM_B2, ADAM_EPS, ADAM_WD, ADAM_STEP = 0.001, 0.9, 0.999, 1e-08, 0.01, 10

PACK_ROWS, PACK_W = 16, 1536


def _nt(a, b):
    return lax.dot_general(a, b, (((1,), (1,)), ((), ())), preferred_element_type=f32)


def _tn(a, b):
    return lax.dot_general(a, b, (((0,), (0,)), ((), ())), preferred_element_type=f32)


def _nn(a, b):
    return jnp.dot(a, b, preferred_element_type=f32)


def _nn_hi(a, b):
    return jnp.dot(a, b, precision=HIGHEST, preferred_element_type=f32)


def _sigmoid(x):
    return 1.0 / (1.0 + jnp.exp(-x))


def _softplus(x):
    return jnp.maximum(x, 0.0) + jnp.log1p(jnp.exp(-jnp.abs(x)))


def _iota(shape, dim):
    return lax.broadcasted_iota(jnp.int32, shape, dim)


def _my_pos():
    return lax.axis_index("x"), lax.axis_index("y"), lax.axis_index("c")


def _all_gather(arrs):
    n = len(arrs)
    ns = 9

    def body(*refs):
        ins, outs = refs[:n], refs[n:2 * n]
        send_sems, recv_sems, local_sems = refs[2 * n:]
        x, y, c = _my_pos()
        me, sibling = (x, y, c), (x, y, 1 - c)
        xn, yn, diag = (1 - x, y), (x, 1 - y), (1 - x, 1 - y)

        def slot(a, px, py, pc):
            return outs[a].at[4 * px + 2 * py + pc]

        def part(a, ref, h):
            width = arrs[a].shape[-1]
            if width % (2 * LANES):
                return ref if h == 1 else None
            return ref.at[:, pl.ds(h * (width // 2), width // 2)]

        def copy(a, k, block, to, src=None, h=None):
            src_ref = slot(a, *block) if src is None else src
            dst_ref = slot(a, *block)
            if h is not None:
                src_ref, dst_ref = part(a, src_ref, h), part(a, dst_ref, h)
                if src_ref is None:
                    return None
            return pltpu.make_async_remote_copy(
                src_ref=src_ref, dst_ref=dst_ref, send_sem=send_sems.at[ns * a + k], recv_sem=recv_sems.at[ns * a + k],
                device_id=to, device_id_type=MESH)

        mine = [pltpu.make_async_copy(ins[a], slot(a, *me), local_sems.at[a]) for a in range(n)]
        for cp in mine:
            cp.start()
        sends = []
        for a in range(n):
            sends += [copy(a, 0, me, sibling, src=ins[a]), copy(a, 1, me, (*xn, c), src=ins[a]),
                      copy(a, 2, me, (*yn, c), src=ins[a])]
        for cp in sends:
            cp.start()

        def start(cp):
            if cp is not None:
                cp.start()
                sends.append(cp)

        for a in range(n):
            copy(a, 1, (*xn, c), me).wait_recv()
            start(copy(a, 4, (*xn, c), sibling))
            start(copy(a, 7, (*xn, c), (*yn, c), h=1))
        for a in range(n):
            copy(a, 2, (*yn, c), me).wait_recv()
            start(copy(a, 5, (*yn, c), sibling))
            start(copy(a, 8, (*yn, c), (*xn, c), h=0))
        for a in range(n):
            for k, h in ((8, 0), (7, 1)):
                cp = copy(a, k, (*diag, c), me, h=h)
                if cp is not None:
                    cp.wait_recv()
            start(copy(a, 6, (*diag, c), sibling))
        for a in range(n):
            copy(a, 0, sibling, me).wait_recv()
            for j, chip in enumerate((xn, yn, diag)):
                copy(a, 4 + j, (*chip, 1 - c), me).wait_recv()
        for cp in sends:
            cp.wait_send()
        for cp in mine:
            cp.wait()

    anyspec = pl.BlockSpec(memory_space=pl.ANY)
    return pl.pallas_call(
        body, name="weights_all_gather",
        out_shape=[SDS((N_DEV,) + a.shape, a.dtype) for a in arrs],
        in_specs=[anyspec] * n, out_specs=[anyspec] * n,
        scratch_shapes=[pltpu.SemaphoreType.DMA((ns * n,)), pltpu.SemaphoreType.DMA((ns * n,)),
                        pltpu.SemaphoreType.DMA((n,))],
    )(*arrs)


def _dw_in_swap(a_parts, u, dw_out):
    tile, tk = 1024, 1024
    s = u.shape[0]
    nk = s // tk
    na = len(a_parts)
    offs, counts, ni = _col_blocks(a_parts, tile)

    def body(*refs):
        a_refs, u_ref, dwo_ref = refs[:na], refs[na], refs[na + 1]
        dw_ref, got_ref, goto_ref = refs[na + 2:na + 5]
        acc, local_sems, send_sems, recv_sem, o_send, o_recv = refs[na + 5:]
        i, k = pl.program_id(0), pl.program_id(1)
        x, y, c = _my_pos()
        par = i % 2

        def tile_copies(t, p):
            rows = pl.ds(pl.multiple_of(t * tile, tile), tile)
            loc = pltpu.make_async_copy(acc.at[p], dw_ref.at[rows], local_sems.at[p])
            rem = pltpu.make_async_remote_copy(
                src_ref=acc.at[p], dst_ref=got_ref.at[rows], send_sem=send_sems.at[p], recv_sem=recv_sem,
                device_id=(x, y, 1 - c), device_id_type=MESH)
            return loc, rem

        out_copy = pltpu.make_async_remote_copy(
            src_ref=dwo_ref, dst_ref=goto_ref, send_sem=o_send, recv_sem=o_recv,
            device_id=(x, y, 1 - c), device_id_type=MESH)

        @pl.when(jnp.logical_and(i == 0, k == 0))
        def _():
            out_copy.start()

        @pl.when(k == 0)
        def _():
            @pl.when(i >= 2)
            def _():
                loc, rem = tile_copies(i - 2, par)
                loc.wait()
                rem.wait_send()
            acc[par] = jnp.zeros((tile, tile), f32)

        for t in range(na):
            @pl.when(jnp.logical_and(i >= offs[t], i < offs[t] + counts[t]))
            def _(t=t):
                acc[par] += _tn(a_refs[t][...], u_ref[...])

        @pl.when(k == nk - 1)
        def _():
            loc, rem = tile_copies(i, par)
            loc.start()
            rem.start()

        @pl.when(jnp.logical_and(i == ni - 1, k == nk - 1))
        def _():
            for t in (ni - 2, ni - 1):
                loc, rem = tile_copies(t, t % 2)
                loc.wait()
                rem.wait_send()
            pltpu.make_async_remote_copy(src_ref=dw_ref, dst_ref=got_ref, send_sem=send_sems.at[0], recv_sem=recv_sem,
                                         device_id=(x, y, c), device_id_type=MESH).wait_recv()
            out_copy.wait_send()
            out_copy.wait_recv()

    def a_spec(t):
        def index(i, k):
            mine = jnp.logical_and(i >= offs[t], i < offs[t] + counts[t])
            return jnp.where(mine, k, 0), jnp.clip(i - offs[t], 0, counts[t] - 1)
        return pl.BlockSpec((tk, tile), index)

    anyspec = pl.BlockSpec(memory_space=pl.ANY)
    return pl.pallas_call(
        body, name="dw_in_swap", grid=(ni, nk),
        in_specs=[a_spec(t) for t in range(na)] + [pl.BlockSpec((tk, tile), lambda i, k: (k, 0)), anyspec],
        out_specs=[anyspec] * 3,
        out_shape=[SDS((ni * tile, tile), f32), SDS((ni * tile, tile), f32), SDS(dw_out.shape, dw_out.dtype)],
        scratch_shapes=[pltpu.VMEM((2, tile, tile), f32), pltpu.SemaphoreType.DMA((2,)), pltpu.SemaphoreType.DMA((2,)),
                        pltpu.SemaphoreType.DMA(()), pltpu.SemaphoreType.DMA(()), pltpu.SemaphoreType.DMA(())],
        compiler_params=pltpu.CompilerParams(dimension_semantics=("arbitrary", "arbitrary")),
    )(*a_parts, u, dw_out)


def _gather_small(small):
    def body(small_in, small_out, send_sems, recv_sems, local_sem):
        x, y, c = _my_pos()
        me = 4 * x + 2 * y + c
        mine = pltpu.make_async_copy(small_in, small_out.at[me], local_sem)
        mine.start()
        sends = []
        for k in range(1, N_DEV):
            to = (me + k) % N_DEV
            cp = pltpu.make_async_remote_copy(
                src_ref=small_in, dst_ref=small_out.at[me], send_sem=send_sems.at[k - 1], recv_sem=recv_sems.at[k - 1],
                device_id=(to // 4, (to // 2) % 2, to % 2), device_id_type=MESH)
            cp.start()
            sends.append(cp)
        for k in range(1, N_DEV):
            frm = (me + N_DEV - k) % N_DEV
            pltpu.make_async_remote_copy(
                src_ref=small_in, dst_ref=small_out.at[frm], send_sem=send_sems.at[k - 1], recv_sem=recv_sems.at[k - 1],
                device_id=(x, y, c), device_id_type=MESH).wait_recv()
        for cp in sends:
            cp.wait_send()
        mine.wait()

    anyspec = pl.BlockSpec(memory_space=pl.ANY)
    return pl.pallas_call(
        body, name="small_grads_gather", out_shape=SDS((N_DEV,) + small.shape, small.dtype),
        in_specs=[anyspec], out_specs=anyspec,
        scratch_shapes=[pltpu.SemaphoreType.DMA((7,)), pltpu.SemaphoreType.DMA((7,)), pltpu.SemaphoreType.DMA(())],
    )(small)


def _chip_sum(mine, got, rows, name):
    r, cdim = mine.shape
    tc = LANES

    def body(m_ref, g_ref, s16_ref):
        c = lax.axis_index("c")
        for q in range(4):
            blk = pl.ds(rows * (2 * q + c), rows)
            s16_ref[q] = (m_ref[blk, :] + g_ref[blk, :]).astype(bf16)

    col = pl.BlockSpec((r, tc), lambda i: (0, i))
    return pl.pallas_call(
        body, name=name, grid=(cdim // tc,), in_specs=[col, col],
        out_specs=pl.BlockSpec((4, rows, tc), lambda i: (0, 0, i)), out_shape=SDS((4, rows, cdim), bf16),
        compiler_params=pltpu.CompilerParams(dimension_semantics=("parallel",)),
    )(mine, got)


def _assemble_wt(shards):
    nd, rows, cdim = shards.shape
    tc = 256

    def body(g_ref, o_ref):
        for j in range(nd):
            o_ref[pl.ds(rows * j, rows), :] = g_ref[j]
        o_ref[pl.ds(nd * rows, NP - nd * rows), :] = jnp.zeros((NP - nd * rows, tc), shards.dtype)

    return pl.pallas_call(
        body, name="assemble_w_in", grid=(cdim // tc,),
        in_specs=[pl.BlockSpec((nd, rows, tc), lambda i: (0, 0, i))],
        out_specs=pl.BlockSpec((NP, tc), lambda i: (0, i)), out_shape=SDS((NP, cdim), shards.dtype),
        compiler_params=pltpu.CompilerParams(dimension_semantics=("parallel",)),
    )(shards)


def _chip_exchange_copies(ins, outs, send_sems, recv_sems, local_sems):
    nb = len(ins)
    x, y, c = _my_pos()
    my_q = 2 * x + y
    mine = [pltpu.make_async_copy(ins[a].at[my_q], outs[a].at[my_q], local_sems.at[a]) for a in range(nb)]
    sends, recvs = [], []
    for k in range(1, 4):
        to, frm = (my_q + k) % 4, (my_q + 4 - k) % 4
        for a in range(nb):
            sems = dict(send_sem=send_sems.at[3 * a + k - 1], recv_sem=recv_sems.at[3 * a + k - 1], device_id_type=MESH)
            sends.append(pltpu.make_async_remote_copy(
                src_ref=ins[a].at[to], dst_ref=outs[a].at[my_q], device_id=(to // 2, to % 2, c), **sems))
            recvs.append(pltpu.make_async_remote_copy(
                src_ref=ins[a].at[frm], dst_ref=outs[a].at[frm], device_id=(x, y, c), **sems))
    return mine, sends, recvs


def _chip_exchange_scratch(nb):
    return [pltpu.SemaphoreType.DMA((3 * nb,)), pltpu.SemaphoreType.DMA((3 * nb,)), pltpu.SemaphoreType.DMA((nb,))]


def _prenorm_inproj(x, nw, wt):
    s, d = x.shape
    npad = wt.shape[0]
    tm, tn = 1024, 1024

    def body(x_ref, nw_ref, w_ref, proj_ref, u_ref):
        @pl.when(pl.program_id(1) == 0)
        def _():
            xv = x_ref[...]
            r = lax.rsqrt(jnp.mean(xv * xv, axis=-1, keepdims=True) + EPS)
            u_ref[...] = (xv * r * nw_ref[...]).astype(bf16)
        proj_ref[...] = _nt(u_ref[...], w_ref[...])

    return pl.pallas_call(
        body, name="prenorm_inproj", grid=(s // tm, npad // tn),
        in_specs=[pl.BlockSpec((tm, d), lambda i, j: (i, 0)), pl.BlockSpec((1, d), lambda i, j: (0, 0)),
                  pl.BlockSpec((tn, d), lambda i, j: (j, 0))],
        out_specs=[pl.BlockSpec((tm, tn), lambda i, j: (i, j)), pl.BlockSpec((tm, d), lambda i, j: (i, 0))],
        out_shape=[SDS((s, npad), f32), SDS((s, d), bf16)],
        compiler_params=pltpu.CompilerParams(dimension_semantics=("parallel", "arbitrary")),
    )(x, nw, wt)


def _attn_consts():
    head0 = _iota((BLK, LANES), 1) < HEAD_DIM
    tri2 = (_iota((BLK, 2 * LANES), 1) % LANES) <= _iota((BLK, 2 * LANES), 0)
    ones2 = ((_iota((LANES, 2 * LANES), 0) < HEAD_DIM) == (_iota((LANES, 2 * LANES), 1) < LANES)).astype(bf16)
    rmat = ((_iota((2 * LANES, LANES), 0) < LANES) == (_iota((2 * LANES, LANES), 1) < HEAD_DIM)).astype(bf16)
    bones = ((_iota((LANES, LANES), 0) < HEAD_DIM) == (_iota((LANES, LANES), 1) < HEAD_DIM)).astype(bf16)
    return head0, tri2, ones2, rmat, bones


def _stack_heads(x16, head0):
    zero = jnp.zeros_like(x16)
    return jnp.concatenate([jnp.where(head0, x16, zero), jnp.where(head0, zero, x16)], axis=0)


def _bf16_terms(x, terms):
    out = []
    for _ in range(terms):
        t = x.astype(bf16)
        out.append(t)
        x = x - t.astype(f32)
    return out


def _dot_01(x, w16, terms):
    return _nn(jnp.concatenate(_bf16_terms(x, terms), axis=1), jnp.concatenate([w16] * terms, axis=0))


def _split_dot(x, w16):
    return _dot_01(x, w16, 2)


def _split_dot_sum(x, w16):
    hi, lo = _bf16_terms(x, 2)
    return _nn(hi, w16) + _nn(lo, w16)


def _dot_01_left(w16, x, terms):
    return _nn(jnp.concatenate([w16] * terms, axis=1), jnp.concatenate(_bf16_terms(x, terms), axis=0))


def _attn_fwd(proj):
    s = proj.shape[0]
    n_it = s // BLK

    def body(q_ref, k_ref, v_ref, g_ref, o_ref, l_ref, mix_ref, op0, op1, op2, lp0, lp1, lp2,
             s_a, s_b, sd_a, sd_b, p_a, p_b, m_a, m_b, pd_a, pd_b, k_a, k_b, v_a, v_b):
        op_refs, lp_refs = (op0, op1, op2), (lp0, lp1, lp2)
        head0, tri2, ones2, rmat, _ = _attn_consts()
        score_bufs, prob_bufs = ((s_a, sd_a), (s_b, sd_b)), ((p_a, m_a, pd_a), (p_b, m_b, pd_b))
        k_bufs, v_bufs = (k_a, k_b), (v_a, v_b)
        for buf in k_bufs + v_bufs:
            buf[...] = jnp.zeros_like(buf)

        def block_rows(i, d, nb):
            r, blk = i // nb, i % nb
            return pl.ds(blk * (BLK * d) + r, BLK, stride=d), blk > 0

        def unstack(st16):
            return st16[:BLK] + st16[BLK:]

        def scores(i, par, d, nb):
            rows, has_prev = block_rows(i, d, nb)
            s_buf, sd_buf = score_bufs[par]
            qs = q_ref[rows, :] * 0.125
            qs16 = qs.astype(bf16)
            kst_c = _stack_heads(k_ref[rows, :].astype(bf16), head0)
            kst_p = k_bufs[1 - par][...]
            k_bufs[par][...] = kst_c
            sc = _nt(qs16, kst_c)
            sp = _nt(qs16, kst_p)
            s_buf[...] = jnp.where(tri2, sc, jnp.where(has_prev, sp, -jnp.inf))
            sd = _split_dot(qs * unstack(kst_p).astype(f32), ones2)
            sd_buf[...] = jnp.where(has_prev, sd, -jnp.inf)

        def softmax(bufs_in, bufs_out):
            s_buf, sd_buf = bufs_in
            p_buf, m_buf, pd_buf = bufs_out
            sc, sd2 = s_buf[...], sd_buf[...]
            m0 = jnp.max(sc[:, :LANES], axis=1, keepdims=True)
            m1 = jnp.max(sc[:, LANES:], axis=1, keepdims=True)
            m2 = jnp.concatenate([jnp.broadcast_to(m0, (BLK, LANES)), jnp.broadcast_to(m1, (BLK, LANES))], axis=1)
            m2 = jnp.maximum(m2, sd2)
            p_buf[...] = jnp.exp(sc - m2).astype(bf16)
            m_pair = jnp.where(head0, m2[:, :LANES], m2[:, LANES:])
            m_buf[...] = m_pair
            pd_buf[...] = jnp.exp(jnp.where(head0, sd2[:, :LANES], sd2[:, LANES:]) - m_pair)

        def output(i, par, d, nb, p):
            rows, _ = block_rows(i, d, nb)
            p_buf, m_buf, pd_buf = prob_bufs[par]
            vst_c = _stack_heads(v_ref[rows, :].astype(bf16), head0)
            vst_p = v_bufs[1 - par][...]
            v_bufs[par][...] = vst_c
            pt16, pd = p_buf[...], pd_buf[...]
            zero = jnp.zeros_like(pt16)
            o = (_nn(jnp.where(tri2, pt16, zero), vst_c) + _nn(jnp.where(tri2, zero, pt16), vst_p)
                 + pd * unstack(vst_p).astype(f32))
            l = _nn(pt16, rmat) + pd
            op_refs[p][rows, :] = o / l
            lp_refs[p][rows, :] = m_buf[...] + jnp.log(l)

        for p, d in enumerate(DILATIONS):
            nb = s // (BLK * d)
            scores(0, 0, d, nb)
            scores(1, 1, d, nb)
            softmax(score_bufs[0], prob_bufs[0])

            def steps(j, carry, d=d, nb=nb, p=p):
                for par in range(2):
                    t = 2 * j + 2 + par
                    scores(t, par, d, nb)
                    output(t - 2, par, d, nb, p)
                    softmax(score_bufs[1 - par], prob_bufs[1 - par])
                return carry

            lax.fori_loop(0, (n_it - 2) // 2, steps, 0)
            output(n_it - 2, 0, d, nb, p)
            softmax(score_bufs[1], prob_bufs[1])
            output(n_it - 1, 1, d, nb, p)

        def merge(i, carry):
            rows = pl.ds(pl.multiple_of(i * 256, 256), 256)
            l0, l1, l2 = lp0[rows, :], lp1[rows, :], lp2[rows, :]
            m = jnp.maximum(jnp.maximum(l0, l1), l2)
            e0, e1, e2 = jnp.exp(l0 - m), jnp.exp(l1 - m), jnp.exp(l2 - m)
            z = e0 + e1 + e2
            o = (e0 * op0[rows, :] + e1 * op1[rows, :] + e2 * op2[rows, :]) / z
            o_ref[rows, :] = o
            l_ref[rows, :] = m + jnp.log(z)
            g = g_ref[rows, :]
            mix_ref[rows, :] = (o * (g * _sigmoid(g))).astype(bf16)
            return carry

        lax.fori_loop(0, s // 256, merge, 0)

    col = lambda base: pl.BlockSpec((s, LANES), lambda h: (0, base + h))
    return pl.pallas_call(
        body, name="attn_fwd", grid=(N_PAIRS,),
        in_specs=[col(0), col(8), col(16), col(24)],
        out_specs=[col(0), col(0), col(0)],
        out_shape=[SDS((s, D_ATTN), f32), SDS((s, D_ATTN), f32), SDS((s, D_ATTN), bf16)],
        scratch_shapes=[pltpu.VMEM((s, LANES), f32)] * 6 + [pltpu.VMEM((BLK, 2 * LANES), f32)] * 4
        + [pltpu.VMEM((BLK, 2 * LANES), bf16)] * 2 + [pltpu.VMEM((BLK, LANES), f32)] * 4
        + [pltpu.VMEM((2 * BLK, LANES), bf16)] * 4,
        compiler_params=pltpu.CompilerParams(dimension_semantics=("parallel",)),
    )(proj, proj, proj, proj)


def _expand_mat():
    colv = np.arange(2 * D_SSM)
    head = 2 * ((colv % D_SSM) // LANES) + colv // D_SSM
    return jnp.asarray(np.arange(LANES)[:, None] == head[None, :], dtype=bf16)


def _fold_mat():
    return jnp.asarray((np.arange(D_SSM) // HEAD_DIM)[:, None] == np.arange(LANES)[None, :], dtype=bf16)


def _ssd_common(xs_ref, bc_ref, xs_tail, bc_tail, dt_ref, cw_ref, cb_ref, dtb_ref, alog16_ref, emat_ref, xpad, first):
    keep = jnp.where(first, 0.0, 1.0)
    xpad[0:8, 0:D_SSM] = xs_tail[...] * keep
    xpad[0:8, D_SSM:D_CONV] = bc_tail[...] * keep
    xpad[8:8 + CHUNK, 0:D_SSM] = xs_ref[...]
    xpad[8:8 + CHUNK, D_SSM:D_CONV] = bc_ref[...]
    xp = xpad[...]
    taps = [pltpu.roll(xp, 3 - j, 0)[8:8 + CHUNK] for j in range(3)] + [xp[8:8 + CHUNK]]
    cv = cb_ref[...] + cw_ref[0:1, :] * taps[0]
    for j in range(1, 4):
        cv = cv + cw_ref[j:j + 1, :] * taps[j]
    sig = _sigmoid(cv)
    xbc = cv * sig

    pre = dt_ref[...] + dtb_ref[...]
    dt16 = _softplus(pre)
    a16 = -jnp.exp(alog16_ref[...])
    sub, lane = _iota((CHUNK, CHUNK), 0), _iota((CHUNK, CHUNK), 1)
    tri = (sub >= lane).astype(f32)
    al16 = _nn_hi(tri, dt16 * a16)
    al_t = al16.T
    emat = emat_ref[...]
    dt_x = _dot_01(dt16, emat, 3)
    al_x = _dot_01(al16, emat, 3)
    lane_w = _iota((CHUNK, D_SSM), 1)
    even = (lane_w % LANES) < HEAD_DIM
    dt_f = jnp.where(even, dt_x[:, :D_SSM], dt_x[:, D_SSM:])
    al_f = jnp.where(even, al_x[:, :D_SSM], al_x[:, D_SSM:])
    return cv, sig, xbc, pre, dt_f, al_f, al_x, al_t, taps


def _decay_mat(al_x, al_t, pair, h):
    sub, lane = _iota((CHUNK, CHUNK), 0), _iota((CHUNK, CHUNK), 1)
    col = al_x[:, h * D_SSM + pair * LANES: h * D_SSM + (pair + 1) * LANES]
    row = al_t[2 * pair + h: 2 * pair + h + 1, :]
    return jnp.exp(jnp.where(sub >= lane, col - row, -jnp.inf))


def _ssd_in_specs(order):
    blk = lambda w, cb: pl.BlockSpec((CHUNK, w), lambda i: (order(i), cb))
    tail = lambda w, cb: pl.BlockSpec((8, w), lambda i: (jnp.maximum(16 * order(i) - 1, 0), cb))
    return [blk(D_SSM, COL_XS // D_SSM), blk(512, COL_BC // 512), tail(D_SSM, COL_XS // D_SSM),
            tail(512, COL_BC // 512), blk(LANES, COL_DT // LANES), blk(D_SSM, COL_Z // D_SSM)]


def _full(shape):
    return pl.BlockSpec(shape, lambda i: (0,) * len(shape))


def _ssd_fwd(proj, conv_w, conv_b, dtb16, alog16, alog_f, d_f, nw):
    s = proj.shape[0]
    nc = s // CHUNK

    def body(xs_ref, bc_ref, xs_tail, bc_tail, dt_ref, z_ref, cw_ref, cb_ref, dtb_ref, alog16_ref, alogf_ref,
             df_ref, nw_ref, emat_ref, mix_ref, y_ref, st_ref, h_scr, xpad, y_scr):
        c = pl.program_id(0)

        @pl.when(c == 0)
        def _():
            h_scr[...] = jnp.zeros_like(h_scr)

        _, _, xbc, _, dt_f, al_f, al_x, al_t, _ = _ssd_common(
            xs_ref, bc_ref, xs_tail, bc_tail, dt_ref, cw_ref, cb_ref, dtb_ref, alog16_ref, emat_ref, xpad, c == 0)
        head0 = _iota((CHUNK, LANES), 1) < HEAD_DIM
        st_ref[...] = h_scr[...]
        for g in range(N_GROUPS):
            bm = xbc[:, D_SSM + g * D_STATE: D_SSM + (g + 1) * D_STATE].astype(bf16)
            cm = xbc[:, D_SSM + (N_GROUPS + g) * D_STATE: D_SSM + (N_GROUPS + g + 1) * D_STATE].astype(bf16)
            gmat = _nt(cm, bm)
            for pair in range(4 * g, 4 * g + 4):
                sl = slice(pair * LANES, (pair + 1) * LANES)
                xp, dtp, alp = xbc[:, sl], dt_f[:, sl], al_f[:, sl]
                xdt = xp * dtp
                xdt16 = xdt.astype(bf16)
                al_last = alp[CHUNK - 1:CHUNK, :]
                hp = h_scr[:, sl]
                y_off = jnp.exp(alp) * _nn(cm, hp.astype(bf16))
                yd = [_nn((gmat * _decay_mat(al_x, al_t, pair, h)).astype(bf16), xdt16) for h in range(2)]
                y_scr[:, sl] = jnp.where(head0, yd[0], yd[1]) + y_off + df_ref[:, sl] * xp
                st = _tn(bm, (jnp.exp(al_last - alp) * xdt).astype(bf16))
                h_scr[:, sl] = jnp.exp(al_last) * hp + st
        y = y_scr[...]
        y_ref[...] = y
        z = z_ref[...]
        yz = y * (z * _sigmoid(z))
        gw = D_SSM // N_GROUPS
        for g in range(N_GROUPS):
            part = yz[:, g * gw:(g + 1) * gw]
            r = lax.rsqrt(jnp.mean(part * part, axis=-1, keepdims=True) + EPS)
            mix_ref[:, g * gw:(g + 1) * gw] = (part * r * nw_ref[:, g * gw:(g + 1) * gw]).astype(bf16)

    order = lambda i: i
    row = lambda w: pl.BlockSpec((CHUNK, w), lambda i: (i, 0))
    return pl.pallas_call(
        body, name="ssd_fwd", grid=(nc,),
        in_specs=_ssd_in_specs(order) + [_full((4, D_CONV)), _full((1, D_CONV)), _full((1, LANES)), _full((1, LANES)),
                                         _full((1, D_SSM)), _full((1, D_SSM)), _full((1, D_SSM)),
                                         _full((LANES, 2 * D_SSM))],
        out_specs=[row(D_SSM), row(D_SSM), pl.BlockSpec((None, D_STATE, D_SSM), lambda i: (i, 0, 0))],
        out_shape=[SDS((s, D_SSM), bf16), SDS((s, D_SSM), f32), SDS((nc, D_STATE, D_SSM), f32)],
        scratch_shapes=[pltpu.VMEM((D_STATE, D_SSM), f32), pltpu.VMEM((8 + CHUNK, D_CONV), f32),
                        pltpu.VMEM((CHUNK, D_SSM), f32)],
        compiler_params=pltpu.CompilerParams(dimension_semantics=("arbitrary",)),
    )(proj, proj, proj, proj, proj, proj, conv_w, conv_b, dtb16, alog16, alog_f, d_f, nw, _expand_mat())


def _outproj_loss(mix_a, mix_s, wo, x, tgt, npw):
    s, d = x.shape
    tm = 512

    def body(ma_ref, ms_ref, wo_ref, x_ref, t_ref, npw_ref, dmix_ref, dout_ref, dres_ref, acc_ref):
        @pl.when(pl.program_id(0) == 0)
        def _():
            acc_ref[...] = jnp.zeros_like(acc_ref)

        out = _nn(ma_ref[...], wo_ref[0:D_ATTN, :]) + _nn(ms_ref[...], wo_ref[D_ATTN:, :])
        r = lax.rsqrt(jnp.mean(out * out, axis=-1, keepdims=True) + EPS)
        on = out * r
        diff = x_ref[...] + on * npw_ref[...] - t_ref[...]
        dres = diff * (1.0 / d)
        dres_ref[...] = dres
        acc_ref[0:1, :] += jnp.sum(diff * diff, axis=0, keepdims=True)
        acc_ref[1:2, :] += jnp.sum(dres * on, axis=0, keepdims=True)
        dn = dres * npw_ref[...]
        dout = (r * (dn - on * jnp.mean(dn * on, axis=-1, keepdims=True))).astype(bf16)
        dout_ref[...] = dout
        dmix_ref[...] = _nt(dout, wo_ref[...])

    row = lambda w: pl.BlockSpec((tm, w), lambda i: (i, 0))
    return pl.pallas_call(
        body, name="outproj_loss", grid=(s // tm,),
        in_specs=[row(D_ATTN), row(D_SSM), _full((D_ATTN + D_SSM, d)), row(d), row(d), _full((1, d))],
        out_specs=[row(D_ATTN + D_SSM), row(d), row(d), _full((8, d))],
        out_shape=[SDS((s, D_ATTN + D_SSM), f32), SDS((s, d), bf16), SDS((s, d), f32), SDS((8, d), f32)],
        compiler_params=pltpu.CompilerParams(dimension_semantics=("arbitrary",)),
    )(mix_a, mix_s, wo, x, tgt, npw)


def _attn_bwd(proj, o, lb, dmix):
    s = proj.shape[0]
    n_it = s // BLK

    def body(q_ref, k_ref, v_ref, g_ref, o_ref, l_ref, dm_ref, dq_ref, dk_ref, dv_ref, dg_ref,
             dq_acc, dk_acc, dv_acc, do_scr, dl_scr, *bufs):
        head0, tri2, _, _, bones = _attn_consts()

        def pro(i, carry):
            rows = pl.ds(pl.multiple_of(i * 256, 256), 256)
            g = g_ref[rows, :]
            sg = _sigmoid(g)
            dmx = dm_ref[rows, :]
            ov = o_ref[rows, :]
            dg_ref[rows, :] = (dmx * ov * (sg * (1.0 + g * (1.0 - sg)))).astype(bf16)
            do = dmx * (g * sg)
            do_scr[rows, :] = do
            dl_scr[rows, :] = _split_dot_sum(do * ov, bones)
            z = jnp.zeros((256, LANES), f32)
            dq_acc[rows, :] = z
            dk_acc[rows, :] = z
            dv_acc[rows, :] = z
            return carry

        lax.fori_loop(0, s // 256, pro, 0)

        def per_head(t):
            return jnp.concatenate([t[:, :LANES], t[:, LANES:]], axis=0)

        def both_heads(t):
            tr = pltpu.roll(t, HEAD_DIM, 1)
            return jnp.concatenate([jnp.where(head0, t, tr), jnp.where(head0, tr, t)], axis=1)

        mm_bufs = ((bufs[0], bufs[1], bufs[2], bufs[3]), (bufs[4], bufs[5], bufs[6], bufs[7]))
        ds_bufs = ((bufs[8], bufs[9], bufs[10], bufs[11]), (bufs[12], bufs[13], bufs[14], bufs[15]))
        op_bufs = ((bufs[16], bufs[17], bufs[18], bufs[19]), (bufs[20], bufs[21], bufs[22], bufs[23]))
        vc_bufs, carry_k, carry_v = (bufs[24], bufs[25]), bufs[26], bufs[27]
        for buf in (op_bufs[0][0], op_bufs[1][0]) + vc_bufs:
            buf[...] = jnp.zeros_like(buf)

        def block_rows(i, d, nb):
            r, blk = i // nb, i % nb
            return pl.ds(blk * (BLK * d) + r, BLK, stride=d), blk > 0

        def unstack(st16):
            return st16[:BLK] + st16[BLK:]

        def products(i, par, d, nb):
            rows, has_prev = block_rows(i, d, nb)
            s_buf, dp_buf, sd_buf, dpd_buf = mm_bufs[par]
            kc_buf, kp_buf, q_buf, do_buf = op_bufs[par]
            q = q_ref[rows, :]
            qs = q * 0.125
            do = do_scr[rows, :]
            qs16, do16 = qs.astype(bf16), do.astype(bf16)
            kst_c = _stack_heads(k_ref[rows, :].astype(bf16), head0)
            vst_c = _stack_heads(v_ref[rows, :].astype(bf16), head0)
            kst_p, vst_p = op_bufs[1 - par][0][...], vc_bufs[1 - par][...]
            kc_buf[...] = kst_c
            kp_buf[...] = kst_p
            vc_bufs[par][...] = vst_c
            q_buf[...] = q.astype(bf16)
            do_buf[...] = do16
            s_buf[...] = jnp.where(tri2, _nt(qs16, kst_c), jnp.where(has_prev, _nt(qs16, kst_p), -jnp.inf))
            dp_buf[...] = jnp.where(tri2, _nt(do16, vst_c), jnp.where(has_prev, _nt(do16, vst_p), 0.0))
            sd_buf[...] = _split_dot_sum(qs * unstack(kst_p).astype(f32), bones)
            dpd_buf[...] = jnp.where(has_prev, _split_dot_sum(do * unstack(vst_p).astype(f32), bones), 0.0)

        def softmax_grad(i, par, d, nb):
            rows, has_prev = block_rows(i, d, nb)
            s_buf, dp_buf, sd_buf, dpd_buf = mm_bufs[par]
            p_buf, ds_buf, pd_buf, dsd_buf = ds_bufs[par]
            lse = l_ref[rows, :]
            dl = dl_scr[rows, :]
            pt = jnp.exp(s_buf[...] - both_heads(lse))
            ds_buf[...] = (pt * (dp_buf[...] - both_heads(dl)) * 0.125).astype(bf16)
            p_buf[...] = pt.astype(bf16)
            pd = jnp.where(has_prev, jnp.exp(sd_buf[...] - lse), 0.0)
            pd_buf[...] = pd
            dsd_buf[...] = pd * (dpd_buf[...] - dl) * 0.125

        def accumulate(i, par, d, nb):
            rows, _ = block_rows(i, d, nb)
            before, _ = block_rows(jnp.maximum(i - 1, 0), d, nb)
            p_buf, ds_buf, pd_buf, dsd_buf = ds_bufs[par]
            kc_buf, kp_buf, q_buf, do_buf = op_bufs[par]
            pt16, ds16, pd, dsd = p_buf[...], ds_buf[...], pd_buf[...], dsd_buf[...]
            zero = jnp.zeros_like(pt16)
            dsc, dsp = jnp.where(tri2, ds16, zero), jnp.where(tri2, zero, ds16)
            pc, pp = jnp.where(tri2, pt16, zero), jnp.where(tri2, zero, pt16)
            kst_c, kst_p, q16, do16 = kc_buf[...], kp_buf[...], q_buf[...], do_buf[...]
            qst, dost = _stack_heads(q16, head0), _stack_heads(do16, head0)
            dq_acc[rows, :] += _nn(dsc, kst_c) + _nn(dsp, kst_p) + dsd * unstack(kst_p).astype(f32)
            dk2 = _tn(jnp.concatenate([per_head(dsc), per_head(dsp)], axis=1), qst)
            dv2 = _tn(jnp.concatenate([per_head(pc), per_head(pp)], axis=1), dost)
            dk_acc[before, :] += carry_k[...] + dk2[BLK:] + dsd * q16.astype(f32)
            dv_acc[before, :] += carry_v[...] + dv2[BLK:] + pd * do16.astype(f32)
            carry_k[...] = dk2[:BLK]
            carry_v[...] = dv2[:BLK]

        for d in DILATIONS:
            nb = s // (BLK * d)
            carry_k[...] = jnp.zeros_like(carry_k)
            carry_v[...] = jnp.zeros_like(carry_v)
            products(0, 0, d, nb)
            products(1, 1, d, nb)
            softmax_grad(0, 0, d, nb)

            def steps(j, carry, d=d, nb=nb):
                for par in range(2):
                    t = 2 * j + 2 + par
                    accumulate(t - 2, par, d, nb)
                    products(t, par, d, nb)
                    softmax_grad(t - 1, 1 - par, d, nb)
                return carry

            lax.fori_loop(0, (n_it - 2) // 2, steps, 0)
            accumulate(n_it - 2, 0, d, nb)
            softmax_grad(n_it - 1, 1, d, nb)
            accumulate(n_it - 1, 1, d, nb)
            last, _ = block_rows(n_it - 1, d, nb)
            dk_acc[last, :] += carry_k[...]
            dv_acc[last, :] += carry_v[...]

        def epi(i, carry):
            rows = pl.ds(pl.multiple_of(i * 256, 256), 256)
            dq_ref[rows, :] = dq_acc[rows, :].astype(bf16)
            dk_ref[rows, :] = dk_acc[rows, :].astype(bf16)
            dv_ref[rows, :] = dv_acc[rows, :].astype(bf16)
            return carry

        lax.fori_loop(0, s // 256, epi, 0)

    col = lambda base: pl.BlockSpec((s, LANES), lambda h: (0, base + h))
    outs = pl.pallas_call(
        body, name="attn_bwd", grid=(N_PAIRS,),
        in_specs=[col(0), col(8), col(16), col(24), col(0), col(0), col(0)],
        out_specs=[col(0)] * 4,
        out_shape=[SDS((s, D_ATTN), bf16)] * 4,
        scratch_shapes=[pltpu.VMEM((s, LANES), f32)] * 5
        + [pltpu.VMEM((BLK, 2 * LANES), f32)] * 2 + [pltpu.VMEM((BLK, LANES), f32)] * 2
        + [pltpu.VMEM((BLK, 2 * LANES), f32)] * 2 + [pltpu.VMEM((BLK, LANES), f32)] * 2
        + [pltpu.VMEM((BLK, 2 * LANES), bf16)] * 2 + [pltpu.VMEM((BLK, LANES), f32)] * 2
        + [pltpu.VMEM((BLK, 2 * LANES), bf16)] * 2 + [pltpu.VMEM((BLK, LANES), f32)] * 2
        + [pltpu.VMEM((2 * BLK, LANES), bf16)] * 2 + [pltpu.VMEM((BLK, LANES), bf16)] * 2
        + [pltpu.VMEM((2 * BLK, LANES), bf16)] * 2 + [pltpu.VMEM((BLK, LANES), bf16)] * 2
        + [pltpu.VMEM((2 * BLK, LANES), bf16)] * 2 + [pltpu.VMEM((BLK, LANES), f32)] * 2,
        compiler_params=pltpu.CompilerParams(dimension_semantics=("parallel",)),
    )(proj, proj, proj, proj, o, lb, dmix)
    return outs


def _ssd_bwd(proj, y, states, dmix, conv_w, conv_b, dtb16, alog16, alog_f, d_f, nw):
    s = proj.shape[0]
    nc = s // CHUNK
    gw = D_SSM // N_GROUPS

    def body(xs_ref, bc_ref, xs_tail, bc_tail, dt_ref, z_ref, y_ref, st_ref, dm_ref, cw_ref, cb_ref, dtb_ref,
             alog16_ref, alogf_ref, df_ref, nw_ref, emat_ref, fold_ref, out_ref, gconv_ref, gvec_ref, gdt_ref,
             dh_scr, head_scr, xpad, dcpad, da_scr, dxdt_scr, dbc_scr):
        i = pl.program_id(0)
        c = nc - 1 - i

        @pl.when(i == 0)
        def _():
            dh_scr[...] = jnp.zeros_like(dh_scr)
            head_scr[...] = jnp.zeros_like(head_scr)
            gconv_ref[...] = jnp.zeros_like(gconv_ref)
            gvec_ref[...] = jnp.zeros_like(gvec_ref)
            gdt_ref[...] = jnp.zeros_like(gdt_ref)

        cv, sig, xbc, pre, dt_f, al_f, al_x, al_t, taps = _ssd_common(
            xs_ref, bc_ref, xs_tail, bc_tail, dt_ref, cw_ref, cb_ref, dtb_ref, alog16_ref, emat_ref, xpad, c == 0)
        head0 = _iota((CHUNK, LANES), 1) < HEAD_DIM
        sub = _iota((CHUNK, LANES), 0)
        last_row = sub == CHUNK - 1

        yv, z, dmx = y_ref[...], z_ref[...], dm_ref[...]
        sz = _sigmoid(z)
        silu = z * sz
        yz = yv * silu
        dyz_parts = []
        for g in range(N_GROUPS):
            gs = slice(g * gw, (g + 1) * gw)
            part = yz[:, gs]
            r = lax.rsqrt(jnp.mean(part * part, axis=-1, keepdims=True) + EPS)
            nh = part * r
            gvec_ref[0:1, gs] += jnp.sum(dmx[:, gs] * nh, axis=0, keepdims=True)
            dn = dmx[:, gs] * nw_ref[:, gs]
            dyz_parts.append(r * (dn - nh * jnp.mean(dn * nh, axis=-1, keepdims=True)))
        dyz = jnp.concatenate(dyz_parts, axis=1)
        dy = dyz * silu
        out_ref[:, 0:D_SSM] = (dyz * yv * (sz * (1.0 + z * (1.0 - sz)))).astype(bf16)

        x_all = xbc[:, 0:D_SSM]
        gvec_ref[2:3, :] += jnp.sum(dy * x_all, axis=0, keepdims=True)

        for g in range(N_GROUPS):
            bm = xbc[:, D_SSM + g * D_STATE: D_SSM + (g + 1) * D_STATE].astype(bf16)
            cm = xbc[:, D_SSM + (N_GROUPS + g) * D_STATE: D_SSM + (N_GROUPS + g + 1) * D_STATE].astype(bf16)
            gmat = _nt(cm, bm)
            dgm = jnp.zeros((CHUNK, CHUNK), f32)
            db = jnp.zeros((CHUNK, D_STATE), f32)
            dc = jnp.zeros((CHUNK, D_STATE), f32)
            for pair in range(4 * g, 4 * g + 4):
                sl = slice(pair * LANES, (pair + 1) * LANES)
                xp, dtp, alp, dyp = x_all[:, sl], dt_f[:, sl], al_f[:, sl], dy[:, sl]
                xdt = xp * dtp
                xdt16 = xdt.astype(bf16)
                al_last = alp[CHUNK - 1:CHUNK, :]
                e_l = jnp.exp(alp)
                wf = jnp.exp(al_last - alp)
                e_last = jnp.exp(al_last)
                hp = st_ref[:, sl]
                hp16 = hp.astype(bf16)
                dhn = dh_scr[:, sl]
                dhn16 = dhn.astype(bf16)
                y_off = e_l * _nn(cm, hp16)
                dch16 = (dyp * e_l).astype(bf16)
                dc = dc + _nt(dch16, hp16)
                dh_out = _tn(cm, dch16)
                dal = dyp * y_off
                xw16 = (wf * xdt).astype(bf16)
                db = db + _nt(xw16, dhn16)
                dxw = _nn(bm, dhn16)
                dxdt = dxw * wf
                dwf = dxw * xdt * wf
                dal = dal - dwf
                dal_last = jnp.sum(dwf, axis=0, keepdims=True) + jnp.sum(dhn * hp, axis=0, keepdims=True) * e_last
                dh_scr[:, sl] = e_last * dhn + dh_out
                for h in range(2):
                    mh = head0 if h == 0 else jnp.logical_not(head0)
                    dyh16 = jnp.where(mh, dyp, 0.0).astype(bf16)
                    lmat = _decay_mat(al_x, al_t, pair, h)
                    mm = gmat * lmat
                    dmm = _nt(dyh16, xdt16)
                    dxdt = dxdt + _tn(mm.astype(bf16), dyh16)
                    n16 = (dmm * mm).astype(bf16)
                    jh = jnp.where(mh, 1.0 / HEAD_DIM, 0.0).astype(bf16)
                    dal = dal + _nn(n16, jh) - _tn(n16, jh)
                    dgm = dgm + dmm * lmat
                da_scr[:, sl] = dal + jnp.where(last_row, dal_last, 0.0)
                dxdt_scr[:, sl] = dxdt
            dgm16 = dgm.astype(bf16)
            dbc_scr[:, g * D_STATE:(g + 1) * D_STATE] = db + _tn(dgm16, cm)
            dbc_scr[:, (N_GROUPS + g) * D_STATE:(N_GROUPS + g + 1) * D_STATE] = dc + _nn(dgm16, bm)

        sub_c, lane_c = _iota((CHUNK, CHUNK), 0), _iota((CHUNK, CHUNK), 1)
        tri_t = (lane_c >= sub_c).astype(bf16)
        dadt = _dot_01_left(tri_t, da_scr[...], 2)
        a_f = -jnp.exp(alogf_ref[...])
        dxdt_all = dxdt_scr[...]
        ddt_f = dxdt_all * x_all + a_f * dadt
        gvec_ref[1:2, :] += jnp.sum(dt_f * dadt, axis=0, keepdims=True) * a_f
        dx = df_ref[...] * dy + dxdt_all * dt_f
        ddt_raw = _dot_01(ddt_f, fold_ref[...], 2) * _sigmoid(pre)
        gdt_ref[0:1, :] += jnp.sum(ddt_raw, axis=0, keepdims=True)
        out_ref[:, D_SSM + D_CONV:D_SSM + D_CONV + LANES] = ddt_raw.astype(bf16)
        out_ref[:, D_SSM + D_CONV + LANES:] = jnp.zeros((CHUNK, 3 * LANES), bf16)

        dsil = sig * (1.0 + cv * (1.0 - sig))
        dcv_x = dx * dsil[:, 0:D_SSM]
        dcv_bc = dbc_scr[...] * dsil[:, D_SSM:]
        dcpad[0:CHUNK, 0:D_SSM] = dcv_x
        dcpad[0:CHUNK, D_SSM:] = dcv_bc
        dcpad[CHUNK:, :] = head_scr[...]
        dcp = dcpad[...]
        dcv = dcp[0:CHUNK]
        gconv_ref[4:5, :] += jnp.sum(dcv, axis=0, keepdims=True)
        draw = cw_ref[3:4, :] * dcv
        for j in range(4):
            gconv_ref[j:j + 1, :] += jnp.sum(dcv * taps[j], axis=0, keepdims=True)
        for j in range(3):
            draw = draw + cw_ref[j:j + 1, :] * pltpu.roll(dcp, CHUNK + 8 - (3 - j), 0)[0:CHUNK]
        head_scr[...] = dcv[0:8]
        out_ref[:, D_SSM:D_SSM + D_CONV] = draw.astype(bf16)

    order = lambda i: nc - 1 - i
    row = lambda w, cb=0: pl.BlockSpec((CHUNK, w), lambda i: (nc - 1 - i, cb))
    return pl.pallas_call(
        body, name="ssd_bwd", grid=(nc,),
        in_specs=_ssd_in_specs(order) + [row(D_SSM), pl.BlockSpec((None, D_STATE, D_SSM), lambda i: (nc - 1 - i, 0, 0)),
                                         row(D_SSM, 1), _full((4, D_CONV)), _full((1, D_CONV)), _full((1, LANES)),
                                         _full((1, LANES)), _full((1, D_SSM)), _full((1, D_SSM)), _full((1, D_SSM)),
                                         _full((LANES, 2 * D_SSM)), _full((D_SSM, LANES))],
        out_specs=[row(3072), _full((8, D_CONV)), _full((8, D_SSM)), _full((8, LANES))],
        out_shape=[SDS((s, 3072), bf16), SDS((8, D_CONV), f32), SDS((8, D_SSM), f32), SDS((8, LANES), f32)],
        scratch_shapes=[pltpu.VMEM((D_STATE, D_SSM), f32), pltpu.VMEM((8, D_CONV), f32),
                        pltpu.VMEM((8 + CHUNK, D_CONV), f32), pltpu.VMEM((8 + CHUNK, D_CONV), f32),
                        pltpu.VMEM((CHUNK, D_SSM), f32), pltpu.VMEM((CHUNK, D_SSM), f32),
                        pltpu.VMEM((CHUNK, 2 * N_GROUPS * D_STATE), f32)],
        compiler_params=pltpu.CompilerParams(dimension_semantics=("arbitrary",)),
    )(proj, proj, proj, proj, proj, proj, y, states, dmix, conv_w, conv_b, dtb16, alog16, alog_f, d_f, nw,
      _expand_mat(), _fold_mat())


def _col_blocks(parts, tile):
    counts = [p.shape[1] // tile for p in parts]
    offs = [sum(counts[:t]) for t in range(len(parts))]
    return offs, counts, sum(counts)


def _inproj_bwd(dparts, wt, x, nw, dres, chip_sums):
    s, d = x.shape
    tm, tk = 1024, 1024
    offs, counts, nk = _col_blocks(dparts, tk)
    npart, nx = len(dparts), len(chip_sums)
    ni = s // tm

    def body(*refs):
        dp_refs = refs[:npart]
        w_ref, x_ref, nw_ref, dres_ref = refs[npart:npart + 4]
        cs_in = refs[npart + 4:npart + 4 + nx]
        gx_ref, gnw_ref = refs[npart + 4 + nx:npart + 6 + nx]
        cs_out = refs[npart + 6 + nx:npart + 6 + 2 * nx]
        acc, send_sems, recv_sems, local_sems = refs[npart + 6 + 2 * nx:]
        i, k = pl.program_id(0), pl.program_id(1)

        @pl.when(jnp.logical_and(i == 0, k == 0))
        def _():
            gnw_ref[...] = jnp.zeros_like(gnw_ref)
            if nx:
                mine, sends, _ = _chip_exchange_copies(cs_in, cs_out, send_sems, recv_sems, local_sems)
                for cp in mine + sends:
                    cp.start()

        @pl.when(jnp.logical_and(i == ni - 1, k == nk - 1))
        def _():
            if nx:
                mine, sends, recvs = _chip_exchange_copies(cs_in, cs_out, send_sems, recv_sems, local_sems)
                for cp in recvs:
                    cp.wait_recv()
                for cp in sends:
                    cp.wait_send()
                for cp in mine:
                    cp.wait()

        @pl.when(k == 0)
        def _():
            acc[...] = jnp.zeros_like(acc)

        for t in range(npart):
            @pl.when(jnp.logical_and(k >= offs[t], k < offs[t] + counts[t]))
            def _(t=t):
                acc[...] += _nn(dp_refs[t][...], w_ref[...])

        @pl.when(k == nk - 1)
        def _():
            xv = x_ref[...]
            r = lax.rsqrt(jnp.mean(xv * xv, axis=-1, keepdims=True) + EPS)
            xn = xv * r
            du = acc[...]
            gnw_ref[0:1, :] += jnp.sum(du * xn, axis=0, keepdims=True)
            dn = du * nw_ref[...]
            gx_ref[...] = dres_ref[...] + r * (dn - xn * jnp.mean(dn * xn, axis=-1, keepdims=True))

    def piece(t):
        return pl.BlockSpec((tm, tk), lambda i, k: (i, jnp.clip(k - offs[t], 0, counts[t] - 1)))

    anyspec = pl.BlockSpec(memory_space=pl.ANY)
    outs = pl.pallas_call(
        body, name="inproj_bwd", grid=(ni, nk),
        in_specs=[piece(t) for t in range(npart)] + [
            pl.BlockSpec((tk, d), lambda i, k: (k, 0)),
            pl.BlockSpec((tm, d), lambda i, k: (i, 0)), pl.BlockSpec((1, d), lambda i, k: (0, 0)),
            pl.BlockSpec((tm, d), lambda i, k: (i, 0))] + [anyspec] * nx,
        out_specs=[pl.BlockSpec((tm, d), lambda i, k: (i, 0)), pl.BlockSpec((8, d), lambda i, k: (0, 0))] + [anyspec] * nx,
        out_shape=[SDS((s, d), f32), SDS((8, d), f32)] + [SDS(a.shape, a.dtype) for a in chip_sums],
        scratch_shapes=[pltpu.VMEM((tm, d), f32)] + _chip_exchange_scratch(max(nx, 1)),
        compiler_params=pltpu.CompilerParams(dimension_semantics=("arbitrary", "arbitrary")),
    )(*dparts, wt, x, nw, dres, *chip_sums)
    return outs[0], outs[1], outs[2:]


def _matmul_tn(a_parts, b_parts, name):
    tile, tk = 1024, 1024
    s = a_parts[0].shape[0]
    nk = s // tk
    na, nb = len(a_parts), len(b_parts)
    offs_a, counts_a, ni = _col_blocks(a_parts, tile)
    offs_b, counts_b, nj = _col_blocks(b_parts, tile)

    def body(*refs):
        a_refs, b_refs, o_ref = refs[:na], refs[na:na + nb], refs[na + nb]
        i, j = pl.program_id(0), pl.program_id(1)

        @pl.when(pl.program_id(2) == 0)
        def _():
            o_ref[...] = jnp.zeros_like(o_ref)

        for ta in range(na):
            for tb in range(nb):
                in_a = jnp.logical_and(i >= offs_a[ta], i < offs_a[ta] + counts_a[ta])
                in_b = jnp.logical_and(j >= offs_b[tb], j < offs_b[tb] + counts_b[tb])

                @pl.when(jnp.logical_and(in_a, in_b))
                def _(ta=ta, tb=tb):
                    o_ref[...] += _tn(a_refs[ta][...], b_refs[tb][...])

    def spec(offs, counts, t, axis):
        def index(i, j, k):
            pos = (i, j)[axis]
            mine = jnp.logical_and(pos >= offs[t], pos < offs[t] + counts[t])
            return jnp.where(mine, k, 0), jnp.clip(pos - offs[t], 0, counts[t] - 1)
        return pl.BlockSpec((tk, tile), index)

    return pl.pallas_call(
        body, name=name, grid=(ni, nj, nk),
        in_specs=[spec(offs_a, counts_a, t, 0) for t in range(na)] + [spec(offs_b, counts_b, t, 1) for t in range(nb)],
        out_specs=pl.BlockSpec((tile, tile), lambda i, j, k: (i, j)),
        out_shape=SDS((ni * tile, nj * tile), f32),
        compiler_params=pltpu.CompilerParams(dimension_semantics=("parallel", "parallel", "arbitrary")),
    )(*a_parts, *b_parts)


def _adamw(w, g, m, v):
    m = ADAM_B1 * m + (1.0 - ADAM_B1) * g
    v = ADAM_B2 * v + (1.0 - ADAM_B2) * (g * g)
    m_hat = m / (1.0 - ADAM_B1 ** ADAM_STEP)
    v_hat = v / (1.0 - ADAM_B2 ** ADAM_STEP)
    delta = -ADAM_LR * (m_hat / (jnp.sqrt(v_hat) + ADAM_EPS) + ADAM_WD * w)
    return delta, m, v


def _sum_adamw(parts, w, m, v, name):
    r, c = w.shape
    tc = 256

    def body(p_ref, w_ref, m_ref, v_ref, g_ref, d_ref, nm_ref, nv_ref):
        g = p_ref[0].astype(f32)
        for q in range(1, 4):
            g = g + p_ref[q].astype(f32)
        g_ref[...] = g
        d_ref[...], nm_ref[...], nv_ref[...] = _adamw(w_ref[...], g, m_ref[...], v_ref[...])

    blk = pl.BlockSpec((r, tc), lambda i: (0, i))
    return pl.pallas_call(
        body, name=name, grid=(c // tc,),
        in_specs=[pl.BlockSpec((4, r, tc), lambda i: (0, 0, i)), blk, blk, blk],
        out_specs=[blk] * 4, out_shape=[SDS((r, c), f32)] * 4,
        compiler_params=pltpu.CompilerParams(dimension_semantics=("parallel",)),
    )(parts, w, m, v)


def _sum_small(parts):
    def body(p_ref, o_ref):
        t = p_ref[0]
        for j in range(1, N_DEV):
            t = t + p_ref[j]
        o_ref[...] = t
        row_h = _iota((D_SSM, LANES), 0) // HEAD_DIM
        fold = (row_h == _iota((D_SSM, LANES), 1)).astype(f32)
        lower = t[8:16, 0:LANES]
        folded = _nn_hi(t[8:16, 0:D_SSM], fold)
        loss = jnp.sum(t[11:12, 0:D_MODEL], axis=1, keepdims=True) * (0.5 / D_MODEL)
        row = _iota((8, LANES), 0)
        o_ref[8:16, 0:LANES] = jnp.where(row < 2, folded, jnp.where(row == 4, loss, lower))

    return pl.pallas_call(body, name="sum_small", out_shape=SDS((PACK_ROWS, PACK_W), f32),
                          in_specs=[pl.BlockSpec(memory_space=pltpu.VMEM)],
                          out_specs=pl.BlockSpec(memory_space=pltpu.VMEM))(parts)


def _adamw_small(w, g, m, v):
    def body(w_ref, g_ref, m_ref, v_ref, d_ref, nm_ref, nv_ref):
        d_ref[...], nm_ref[...], nv_ref[...] = _adamw(w_ref[...], g_ref[...], m_ref[...], v_ref[...])

    vm = pl.BlockSpec(memory_space=pltpu.VMEM)
    return pl.pallas_call(body, name="adamw_small", out_shape=[SDS(w.shape, f32)] * 3,
                          in_specs=[vm] * 4, out_specs=[vm] * 3)(w, g, m, v)


def _pad_lanes(v, width):
    return jnp.pad(v, ((0, 0), (0, width - v.shape[1])))


def _local_step(x, tgt, norm_pre_w, wt, conv_w, conv_b, dt_bias, a_log, d_skip, ssm_norm_w, wo, norm_post_w,
                weight_grads):
    dtb16 = _pad_lanes(dt_bias, LANES)
    alog16 = _pad_lanes(a_log, LANES)
    alog_f = jnp.repeat(a_log, HEAD_DIM, axis=1)
    d_f = jnp.repeat(d_skip, HEAD_DIM, axis=1)

    proj, u = _prenorm_inproj(x, norm_pre_w, wt)
    o, lb, mix_a = _attn_fwd(proj)
    mix_s, y, states = _ssd_fwd(proj, conv_w, conv_b, dtb16, alog16, alog_f, d_f, ssm_norm_w)
    dmix, dout, dres, acc_post = _outproj_loss(mix_a, mix_s, wo, x, tgt, norm_post_w)
    dq, dk, dv, dg = _attn_bwd(proj, o, lb, dmix)
    dzxd, g_conv, g_vec, g_dt = _ssd_bwd(proj, y, states, dmix, conv_w, conv_b, dtb16, alog16, alog_f, d_f, ssm_norm_w)
    dparts = [dq, dk, dv, dg, dzxd]
    dw_out = _matmul_tn([mix_a, mix_s], [dout], "dw_out")
    chip_sums, carry = weight_grads(dparts, u, dw_out)
    grad_x, g_pre, exchanged = _inproj_bwd(dparts, wt, x, norm_pre_w, dres, chip_sums)

    rows = [g_conv[0:5], _pad_lanes(g_pre[0:1], PACK_W), _pad_lanes(g_vec[0:1], PACK_W),
            _pad_lanes(acc_post[1:2], PACK_W), _pad_lanes(g_vec[1:3], PACK_W), _pad_lanes(g_dt[0:1], PACK_W),
            _pad_lanes(acc_post[0:1], PACK_W), jnp.zeros((4, PACK_W), f32)]
    return grad_x, carry, exchanged, jnp.concatenate(rows, axis=0)


def kernel(x, norm_pre_w, w_in, conv_w, conv_b, dt_bias, a_log, d_skip, ssm_norm_w, w_out, norm_post_w, loss_target, m_norm_pre_w, m_w_in, m_conv_w, m_conv_b, m_dt_bias, m_a_log, m_d_skip, m_ssm_norm_w, m_w_out, m_norm_post_w, v_norm_pre_w, v_w_in, v_conv_w, v_conv_b, v_dt_bias, v_a_log, v_d_skip, v_ssm_norm_w, v_w_out, v_norm_post_w):
    shard_in = w_in.shape[2]
    shard_cv = conv_w.shape[2]
    me = 4 * lax.axis_index("x") + 2 * lax.axis_index("y") + lax.axis_index("c")

    g_in, g_out, g_cw = _all_gather([w_in[0].T.astype(bf16), w_out[0].astype(bf16), conv_w[0]])
    wt = _assemble_wt(g_in)
    wo = g_out.reshape(N_DEV * w_out.shape[1], D_MODEL)
    cw = g_cw.transpose(1, 0, 2).reshape(4, D_CONV)

    def weight_grads(dparts, u, dw_out):
        dw_in, got_in, got_out = _dw_in_swap(dparts, u, dw_out)
        return [_chip_sum(dw_in, got_in, shard_in, "chip_sum_w_in"),
                _chip_sum(dw_out, got_out, w_out.shape[1], "chip_sum_w_out")], ()

    grad_x, _, (parts_in, parts_out), pack = _local_step(
        x[0], loss_target[0], norm_pre_w, wt, cw, conv_b, dt_bias, a_log, d_skip, ssm_norm_w, wo, norm_post_w,
        weight_grads)
    parts_small = _gather_small(pack)

    g_w_in, d_w_in, nm_w_in, nv_w_in = (a.T for a in _sum_adamw(
        parts_in, w_in[0].T, m_w_in[0].T, v_w_in[0].T, "sum_adamw_w_in"))
    g_w_out, d_w_out, nm_w_out, nv_w_out = _sum_adamw(parts_out, w_out[0], m_w_out[0], v_w_out[0], "sum_adamw_w_out")
    tot = _sum_small(parts_small)

    g_cw_all = tot[0:4]
    small_g = {
        "conv_w": lax.dynamic_slice(g_cw_all, (0, me * shard_cv), (4, shard_cv)),
        "conv_b": tot[4:5], "norm_pre_w": tot[5:6, :D_MODEL], "ssm_norm_w": tot[6:7, :D_SSM],
        "norm_post_w": tot[7:8, :D_MODEL], "a_log": tot[8:9, :16], "d_skip": tot[9:10, :16], "dt_bias": tot[10:11, :16],
    }
    loss = tot[12, 0]
    small_w = {"conv_w": (conv_w[0], m_conv_w[0], v_conv_w[0]), "conv_b": (conv_b, m_conv_b, v_conv_b),
               "norm_pre_w": (norm_pre_w, m_norm_pre_w, v_norm_pre_w), "ssm_norm_w": (ssm_norm_w, m_ssm_norm_w, v_ssm_norm_w),
               "norm_post_w": (norm_post_w, m_norm_post_w, v_norm_post_w), "a_log": (a_log, m_a_log, v_a_log),
               "d_skip": (d_skip, m_d_skip, v_d_skip), "dt_bias": (dt_bias, m_dt_bias, v_dt_bias)}
    names = list(small_w)
    sizes = [small_g[k].size for k in names]
    tot_size = sum(sizes)
    pad_to = -(-tot_size // 1024) * 1024

    def flat(arrs):
        v = jnp.concatenate([a.reshape(-1) for a in arrs])
        return jnp.pad(v, (0, pad_to - tot_size)).reshape(pad_to // LANES, LANES)

    fw = flat([small_w[k][0] for k in names])
    fg = flat([small_g[k] for k in names])
    fm = flat([small_w[k][1] for k in names])
    fv = jnp.pad(jnp.concatenate([small_w[k][2].reshape(-1) for k in names]), (0, pad_to - tot_size),
                 constant_values=1.0).reshape(pad_to // LANES, LANES)
    fd, fnm, fnv = _adamw_small(fw, fg, fm, fv)

    def unflat(f):
        out, off = {}, 0
        v = f.reshape(-1)
        for k, n in zip(names, sizes):
            out[k] = v[off:off + n].reshape(small_g[k].shape)
            off += n
        return out

    sd, snm, snv = unflat(fd), unflat(fnm), unflat(fnv)
    lead = lambda a: a[None]
    order = ["norm_pre_w", "w_in", "conv_w", "conv_b", "dt_bias", "a_log", "d_skip", "ssm_norm_w", "w_out", "norm_post_w"]
    grads = dict(small_g, w_in=g_w_in, w_out=g_w_out)
    deltas = dict(sd, w_in=d_w_in, w_out=d_w_out)
    new_m = dict(snm, w_in=nm_w_in, w_out=nm_w_out)
    new_v = dict(snv, w_in=nv_w_in, w_out=nv_w_out)

    def shaped(dct, k):
        a = dct[k]
        return lead(a) if k in ("w_in", "w_out", "conv_w") else a

    return (loss, grad_x[None], *[shaped(grads, k) for k in order], *[shaped(deltas, k) for k in order],
            *[shaped(new_m, k) for k in order], *[shaped(new_v, k) for k in order])
```

```python
import functools
import math

import jax
import jax.numpy as jnp
import numpy as np
from jax import lax
from jax.experimental import pallas as pl
from jax.experimental.pallas import tpu as pltpu

f32, bf16 = jnp.float32, jnp.bfloat16
SDS = jax.ShapeDtypeStruct
HIGHEST = lax.Precision.HIGHEST
MESH = pl.DeviceIdType.MESH

N_DEV = 8
D_MODEL = 1024
D_ATTN = 1024
D_SSM = 1024
HEAD_DIM = 64
N_PAIRS = 8
D_STATE = 128
N_GROUPS = 2
D_CONV = D_SSM + 2 * N_GROUPS * D_STATE
D_IN_PROJ = 4 * D_ATTN + D_SSM + D_CONV + 16
NP = 7168
CHUNK = 128
BLK = 128
DILATIONS = (1, 4, 16)
EPS = 1e-6
LANES = 128
COL_Z, COL_XS, COL_BC, COL_DT = 4096, 5120, 6144, 6656

ADAM_LR, ADAM_B1, ADAM_B2, ADAM_EPS, ADAM_WD, ADAM_STEP = 0.001, 0.9, 0.999, 1e-08, 0.01, 10

PACK_ROWS, PACK_W = 16, 1536


def _nt(a, b):
    return lax.dot_general(a, b, (((1,), (1,)), ((), ())), preferred_element_type=f32)


def _tn(a, b):
    return lax.dot_general(a, b, (((0,), (0,)), ((), ())), preferred_element_type=f32)


def _nn(a, b):
    return jnp.dot(a, b, preferred_element_type=f32)


def _nn_hi(a, b):
    return jnp.dot(a, b, precision=HIGHEST, preferred_element_type=f32)


def _sigmoid(x):
    return 1.0 / (1.0 + jnp.exp(-x))


def _softplus(x):
    return jnp.maximum(x, 0.0) + jnp.log1p(jnp.exp(-jnp.abs(x)))


def _iota(shape, dim):
    return lax.broadcasted_iota(jnp.int32, shape, dim)


def _my_pos():
    return lax.axis_index("x"), lax.axis_index("y"), lax.axis_index("c")


GATHER_SEMS = 9


def _gather_phases(ins, outs, send_sems, recv_sems, local_sems):
    n, ns = len(ins), GATHER_SEMS
    x, y, c = _my_pos()
    me, sibling = (x, y, c), (x, y, 1 - c)
    xn, yn, diag = (1 - x, y), (x, 1 - y), (1 - x, 1 - y)

    def slot(a, px, py, pc):
        return outs[a].at[4 * px + 2 * py + pc]

    def part(a, ref, h):
        width = ins[a].shape[-1]
        if width % (2 * LANES):
            return ref if h == 1 else None
        return ref.at[:, pl.ds(h * (width // 2), width // 2)]

    def copy(a, k, block, to, src=None, h=None):
        src_ref = slot(a, *block) if src is None else src
        dst_ref = slot(a, *block)
        if h is not None:
            src_ref, dst_ref = part(a, src_ref, h), part(a, dst_ref, h)
            if src_ref is None:
                return None
        return pltpu.make_async_remote_copy(
            src_ref=src_ref, dst_ref=dst_ref, send_sem=send_sems.at[ns * a + k], recv_sem=recv_sems.at[ns * a + k],
            device_id=to, device_id_type=MESH)

    def mine():
        return [pltpu.make_async_copy(ins[a], slot(a, *me), local_sems.at[a]) for a in range(n)]

    def own_sends(a):
        return [copy(a, 0, me, sibling, src=ins[a]), copy(a, 1, me, (*xn, c), src=ins[a]),
                copy(a, 2, me, (*yn, c), src=ins[a])]

    def neighbour_relays(a):
        return [copy(a, 4, (*xn, c), sibling), copy(a, 7, (*xn, c), (*yn, c), h=1),
                copy(a, 5, (*yn, c), sibling), copy(a, 8, (*yn, c), (*xn, c), h=0)]

    def diagonal_halves(a):
        return [copy(a, k, (*diag, c), me, h=h) for k, h in ((8, 0), (7, 1))]

    def start_all(cps):
        for cp in cps:
            if cp is not None:
                cp.start()

    def phase0():
        start_all(mine())
        for a in range(n):
            start_all(own_sends(a))

    def phase1():
        for a in range(n):
            copy(a, 1, (*xn, c), me).wait_recv()
            copy(a, 2, (*yn, c), me).wait_recv()
            start_all(neighbour_relays(a))

    def phase2():
        for a in range(n):
            for cp in diagonal_halves(a):
                if cp is not None:
                    cp.wait_recv()
            copy(a, 6, (*diag, c), sibling).start()

    def finish():
        for a in range(n):
            copy(a, 0, sibling, me).wait_recv()
            for j, chip in enumerate((xn, yn, diag)):
                copy(a, 4 + j, (*chip, 1 - c), me).wait_recv()
        for a in range(n):
            for cp in own_sends(a) + neighbour_relays(a) + [copy(a, 6, (*diag, c), sibling)]:
                if cp is not None:
                    cp.wait_send()
        for cp in mine():
            cp.wait()

    return phase0, phase1, phase2, finish


def _gather_scratch(n):
    return [pltpu.SemaphoreType.DMA((GATHER_SEMS * n,)), pltpu.SemaphoreType.DMA((GATHER_SEMS * n,)),
            pltpu.SemaphoreType.DMA((n,))]


def _all_gather(arrs):
    n = len(arrs)

    def body(*refs):
        for phase in _gather_phases(refs[:n], refs[n:2 * n], *refs[2 * n:]):
            phase()

    anyspec = pl.BlockSpec(memory_space=pl.ANY)
    return pl.pallas_call(
        body, name="weights_all_gather",
        out_shape=[SDS((N_DEV,) + a.shape, a.dtype) for a in arrs],
        in_specs=[anyspec] * n, out_specs=[anyspec] * n, scratch_shapes=_gather_scratch(n),
    )(*arrs)


def _dw_in_swap(a_parts, u, dw_out):
    tile, tk = 1024, 1024
    s = u.shape[0]
    nk = s // tk
    na = len(a_parts)
    offs, counts, ni = _col_blocks(a_parts, tile)

    def body(*refs):
        a_refs, u_ref, dwo_ref = refs[:na], refs[na], refs[na + 1]
        dw_ref, got_ref, goto_ref = refs[na + 2:na + 5]
        acc, local_sems, send_sems, recv_sem, o_send, o_recv = refs[na + 5:]
        i, k = pl.program_id(0), pl.program_id(1)
        x, y, c = _my_pos()
        par = i % 2

        def tile_copies(t, p):
            rows = pl.ds(pl.multiple_of(t * tile, tile), tile)
            loc = pltpu.make_async_copy(acc.at[p], dw_ref.at[rows], local_sems.at[p])
            rem = pltpu.make_async_remote_copy(
                src_ref=acc.at[p], dst_ref=got_ref.at[rows], send_sem=send_sems.at[p], recv_sem=recv_sem,
                device_id=(x, y, 1 - c), device_id_type=MESH)
            return loc, rem

        out_copy = pltpu.make_async_remote_copy(
            src_ref=dwo_ref, dst_ref=goto_ref, send_sem=o_send, recv_sem=o_recv,
            device_id=(x, y, 1 - c), device_id_type=MESH)

        @pl.when(jnp.logical_and(i == 0, k == 0))
        def _():
            out_copy.start()

        @pl.when(k == 0)
        def _():
            @pl.when(i >= 2)
            def _():
                loc, rem = tile_copies(i - 2, par)
                loc.wait()
                rem.wait_send()
            acc[par] = jnp.zeros((tile, tile), f32)

        for t in range(na):
            @pl.when(jnp.logical_and(i >= offs[t], i < offs[t] + counts[t]))
            def _(t=t):
                acc[par] += _tn(a_refs[t][...], u_ref[...])

        @pl.when(k == nk - 1)
        def _():
            loc, rem = tile_copies(i, par)
            loc.start()
            rem.start()

        @pl.when(jnp.logical_and(i == ni - 1, k == nk - 1))
        def _():
            for t in (ni - 2, ni - 1):
                loc, rem = tile_copies(t, t % 2)
                loc.wait()
                rem.wait_send()
            pltpu.make_async_remote_copy(src_ref=dw_ref, dst_ref=got_ref, send_sem=send_sems.at[0], recv_sem=recv_sem,
                                         device_id=(x, y, c), device_id_type=MESH).wait_recv()
            out_copy.wait_send()
            out_copy.wait_recv()

    def a_spec(t):
        def index(i, k):
            mine = jnp.logical_and(i >= offs[t], i < offs[t] + counts[t])
            return jnp.where(mine, k, 0), jnp.clip(i - offs[t], 0, counts[t] - 1)
        return pl.BlockSpec((tk, tile), index)

    anyspec = pl.BlockSpec(memory_space=pl.ANY)
    return pl.pallas_call(
        body, name="dw_in_swap", grid=(ni, nk),
        in_specs=[a_spec(t) for t in range(na)] + [pl.BlockSpec((tk, tile), lambda i, k: (k, 0)), anyspec],
        out_specs=[anyspec] * 3,
        out_shape=[SDS((ni * tile, tile), f32), SDS((ni * tile, tile), f32), SDS(dw_out.shape, dw_out.dtype)],
        scratch_shapes=[pltpu.VMEM((2, tile, tile), f32), pltpu.SemaphoreType.DMA((2,)), pltpu.SemaphoreType.DMA((2,)),
                        pltpu.SemaphoreType.DMA(()), pltpu.SemaphoreType.DMA(()), pltpu.SemaphoreType.DMA(())],
        compiler_params=pltpu.CompilerParams(dimension_semantics=("arbitrary", "arbitrary")),
    )(*a_parts, u, dw_out)


def _chip_sum(mine, got, rows, name):
    r, cdim = mine.shape
    tc = LANES

    def body(m_ref, g_ref, s16_ref):
        c = lax.axis_index("c")
        for q in range(4):
            blk = pl.ds(rows * (2 * q + c), rows)
            s16_ref[q] = (m_ref[blk, :] + g_ref[blk, :]).astype(bf16)

    col = pl.BlockSpec((r, tc), lambda i: (0, i))
    return pl.pallas_call(
        body, name=name, grid=(cdim // tc,), in_specs=[col, col],
        out_specs=pl.BlockSpec((4, rows, tc), lambda i: (0, 0, i)), out_shape=SDS((4, rows, cdim), bf16),
        compiler_params=pltpu.CompilerParams(dimension_semantics=("parallel",)),
    )(mine, got)


def _assemble_wt(shards):
    nd, rows, cdim = shards.shape
    tc = 256

    def body(g_ref, o_ref):
        for j in range(nd):
            o_ref[pl.ds(rows * j, rows), :] = g_ref[j]
        o_ref[pl.ds(nd * rows, NP - nd * rows), :] = jnp.zeros((NP - nd * rows, tc), shards.dtype)

    return pl.pallas_call(
        body, name="assemble_w_in", grid=(cdim // tc,),
        in_specs=[pl.BlockSpec((nd, rows, tc), lambda i: (0, 0, i))],
        out_specs=pl.BlockSpec((NP, tc), lambda i: (0, i)), out_shape=SDS((NP, cdim), shards.dtype),
        compiler_params=pltpu.CompilerParams(dimension_semantics=("parallel",)),
    )(shards)


def _chip_exchange_copies(ins, outs, send_sems, recv_sems, local_sems):
    nb = len(ins)
    x, y, c = _my_pos()
    my_q = 2 * x + y
    mine = [pltpu.make_async_copy(ins[a].at[my_q], outs[a].at[my_q], local_sems.at[a]) for a in range(nb)]
    sends, recvs = [], []
    for k in range(1, 4):
        to, frm = (my_q + k) % 4, (my_q + 4 - k) % 4
        for a in range(nb):
            sems = dict(send_sem=send_sems.at[3 * a + k - 1], recv_sem=recv_sems.at[3 * a + k - 1], device_id_type=MESH)
            sends.append(pltpu.make_async_remote_copy(
                src_ref=ins[a].at[to], dst_ref=outs[a].at[my_q], device_id=(to // 2, to % 2, c), **sems))
            recvs.append(pltpu.make_async_remote_copy(
                src_ref=ins[a].at[frm], dst_ref=outs[a].at[frm], device_id=(x, y, c), **sems))
    return mine, sends, recvs


def _chip_exchange_scratch(nb):
    return [pltpu.SemaphoreType.DMA((3 * nb,)), pltpu.SemaphoreType.DMA((3 * nb,)), pltpu.SemaphoreType.DMA((nb,))]


def _prenorm_inproj(x, nw, wt, gather=()):
    s, d = x.shape
    npad = wt.shape[0]
    tm, tn = 1024, 1024
    ng = len(gather)
    ni, nj = s // tm, npad // tn

    def body(x_ref, nw_ref, w_ref, *refs):
        g_in, (proj_ref, u_ref), g_out, sems = refs[:ng], refs[ng:ng + 2], refs[ng + 2:2 * ng + 2], refs[2 * ng + 2:]
        i, j = pl.program_id(0), pl.program_id(1)
        if ng:
            phases = _gather_phases(g_in, g_out, *sems)
            for step, phase in enumerate(phases[:3]):
                @pl.when(jnp.logical_and(i == step, j == 0))
                def _(phase=phase):
                    phase()

        @pl.when(j == 0)
        def _():
            xv = x_ref[...]
            r = lax.rsqrt(jnp.mean(xv * xv, axis=-1, keepdims=True) + EPS)
            u_ref[...] = (xv * r * nw_ref[...]).astype(bf16)
        proj_ref[...] = _nt(u_ref[...], w_ref[...])

        if ng:
            @pl.when(jnp.logical_and(i == ni - 1, j == nj - 1))
            def _():
                phases[3]()

    anyspec = pl.BlockSpec(memory_space=pl.ANY)
    outs = pl.pallas_call(
        body, name="prenorm_inproj", grid=(ni, nj),
        in_specs=[pl.BlockSpec((tm, d), lambda i, j: (i, 0)), pl.BlockSpec((1, d), lambda i, j: (0, 0)),
                  pl.BlockSpec((tn, d), lambda i, j: (j, 0))] + [anyspec] * ng,
        out_specs=[pl.BlockSpec((tm, tn), lambda i, j: (i, j)), pl.BlockSpec((tm, d), lambda i, j: (i, 0))]
        + [anyspec] * ng,
        out_shape=[SDS((s, npad), f32), SDS((s, d), bf16)] + [SDS((N_DEV,) + a.shape, a.dtype) for a in gather],
        scratch_shapes=_gather_scratch(ng) if ng else [],
        compiler_params=pltpu.CompilerParams(dimension_semantics=("arbitrary", "arbitrary")),
    )(x, nw, wt, *gather)
    return outs[0], outs[1], outs[2:]


def _attn_consts():
    head0 = _iota((BLK, LANES), 1) < HEAD_DIM
    tri2 = (_iota((BLK, 2 * LANES), 1) % LANES) <= _iota((BLK, 2 * LANES), 0)
    ones2 = ((_iota((LANES, 2 * LANES), 0) < HEAD_DIM) == (_iota((LANES, 2 * LANES), 1) < LANES)).astype(bf16)
    rmat = ((_iota((2 * LANES, LANES), 0) < LANES) == (_iota((2 * LANES, LANES), 1) < HEAD_DIM)).astype(bf16)
    bones = ((_iota((LANES, LANES), 0) < HEAD_DIM) == (_iota((LANES, LANES), 1) < HEAD_DIM)).astype(bf16)
    return head0, tri2, ones2, rmat, bones


def _stack_heads(x16, head0):
    zero = jnp.zeros_like(x16)
    return jnp.concatenate([jnp.where(head0, x16, zero), jnp.where(head0, zero, x16)], axis=0)


def _bf16_terms(x, terms):
    out = []
    for _ in range(terms):
        t = x.astype(bf16)
        out.append(t)
        x = x - t.astype(f32)
    return out


def _dot_01(x, w16, terms):
    return _nn(jnp.concatenate(_bf16_terms(x, terms), axis=1), jnp.concatenate([w16] * terms, axis=0))


def _split_dot(x, w16):
    return _dot_01(x, w16, 2)


def _split_dot_sum(x, w16):
    hi, lo = _bf16_terms(x, 2)
    return _nn(hi, w16) + _nn(lo, w16)


def _dot_01_left(w16, x, terms):
    return _nn(jnp.concatenate([w16] * terms, axis=1), jnp.concatenate(_bf16_terms(x, terms), axis=0))


def _attn_fwd(proj):
    s = proj.shape[0]
    n_it = s // BLK

    def body(q_ref, k_ref, v_ref, g_ref, o_ref, l_ref, mix_ref, op0, op1, op2, lp0, lp1, lp2,
             s_a, s_b, sd_a, sd_b, p_a, p_b, m_a, m_b, pd_a, pd_b, k_a, k_b, v_a, v_b):
        op_refs, lp_refs = (op0, op1, op2), (lp0, lp1, lp2)
        head0, tri2, ones2, rmat, _ = _attn_consts()
        score_bufs, prob_bufs = ((s_a, sd_a), (s_b, sd_b)), ((p_a, m_a, pd_a), (p_b, m_b, pd_b))
        k_bufs, v_bufs = (k_a, k_b), (v_a, v_b)
        for buf in k_bufs + v_bufs:
            buf[...] = jnp.zeros_like(buf)

        def block_rows(i, d, nb):
            r, blk = i // nb, i % nb
            return pl.ds(blk * (BLK * d) + r, BLK, stride=d), blk > 0

        def unstack(st16):
            return st16[:BLK] + st16[BLK:]

        def scores(i, par, d, nb):
            rows, has_prev = block_rows(i, d, nb)
            s_buf, sd_buf = score_bufs[par]
            qs = q_ref[rows, :] * 0.125
            qs16 = qs.astype(bf16)
            kst_c = _stack_heads(k_ref[rows, :].astype(bf16), head0)
            kst_p = k_bufs[1 - par][...]
            k_bufs[par][...] = kst_c
            sc = _nt(qs16, kst_c)
            sp = _nt(qs16, kst_p)
            s_buf[...] = jnp.where(tri2, sc, jnp.where(has_prev, sp, -jnp.inf))
            sd = _split_dot(qs * unstack(kst_p).astype(f32), ones2)
            sd_buf[...] = jnp.where(has_prev, sd, -jnp.inf)

        def softmax(bufs_in, bufs_out):
            s_buf, sd_buf = bufs_in
            p_buf, m_buf, pd_buf = bufs_out
            sc, sd2 = s_buf[...], sd_buf[...]
            m0 = jnp.max(sc[:, :LANES], axis=1, keepdims=True)
            m1 = jnp.max(sc[:, LANES:], axis=1, keepdims=True)
            m2 = jnp.concatenate([jnp.broadcast_to(m0, (BLK, LANES)), jnp.broadcast_to(m1, (BLK, LANES))], axis=1)
            m2 = jnp.maximum(m2, sd2)
            p_buf[...] = jnp.exp(sc - m2).astype(bf16)
            m_pair = jnp.where(head0, m2[:, :LANES], m2[:, LANES:])
            m_buf[...] = m_pair
            pd_buf[...] = jnp.exp(jnp.where(head0, sd2[:, :LANES], sd2[:, LANES:]) - m_pair)

        def output(i, par, d, nb, p):
            rows, _ = block_rows(i, d, nb)
            p_buf, m_buf, pd_buf = prob_bufs[par]
            vst_c = _stack_heads(v_ref[rows, :].astype(bf16), head0)
            vst_p = v_bufs[1 - par][...]
            v_bufs[par][...] = vst_c
            pt16, pd = p_buf[...], pd_buf[...]
            zero = jnp.zeros_like(pt16)
            o = (_nn(jnp.where(tri2, pt16, zero), vst_c) + _nn(jnp.where(tri2, zero, pt16), vst_p)
                 + pd * unstack(vst_p).astype(f32))
            l = _nn(pt16, rmat) + pd
            op_refs[p][rows, :] = o / l
            lp_refs[p][rows, :] = m_buf[...] + jnp.log(l)

        for p, d in enumerate(DILATIONS):
            nb = s // (BLK * d)
            scores(0, 0, d, nb)
            scores(1, 1, d, nb)
            softmax(score_bufs[0], prob_bufs[0])

            def steps(j, carry, d=d, nb=nb, p=p):
                for par in range(2):
                    t = 2 * j + 2 + par
                    scores(t, par, d, nb)
                    output(t - 2, par, d, nb, p)
                    softmax(score_bufs[1 - par], prob_bufs[1 - par])
                return carry

            lax.fori_loop(0, (n_it - 2) // 2, steps, 0)
            output(n_it - 2, 0, d, nb, p)
            softmax(score_bufs[1], prob_bufs[1])
            output(n_it - 1, 1, d, nb, p)

        def merge(i, carry):
            rows = pl.ds(pl.multiple_of(i * 256, 256), 256)
            l0, l1, l2 = lp0[rows, :], lp1[rows, :], lp2[rows, :]
            m = jnp.maximum(jnp.maximum(l0, l1), l2)
            e0, e1, e2 = jnp.exp(l0 - m), jnp.exp(l1 - m), jnp.exp(l2 - m)
            z = e0 + e1 + e2
            o = (e0 * op0[rows, :] + e1 * op1[rows, :] + e2 * op2[rows, :]) / z
            o_ref[rows, :] = o
            l_ref[rows, :] = m + jnp.log(z)
            g = g_ref[rows, :]
            mix_ref[rows, :] = (o * (g * _sigmoid(g))).astype(bf16)
            return carry

        lax.fori_loop(0, s // 256, merge, 0)

    col = lambda base: pl.BlockSpec((s, LANES), lambda h: (0, base + h))
    return pl.pallas_call(
        body, name="attn_fwd", grid=(N_PAIRS,),
        in_specs=[col(0), col(8), col(16), col(24)],
        out_specs=[col(0), col(0), col(0)],
        out_shape=[SDS((s, D_ATTN), f32), SDS((s, D_ATTN), f32), SDS((s, D_ATTN), bf16)],
        scratch_shapes=[pltpu.VMEM((s, LANES), f32)] * 6 + [pltpu.VMEM((BLK, 2 * LANES), f32)] * 4
        + [pltpu.VMEM((BLK, 2 * LANES), bf16)] * 2 + [pltpu.VMEM((BLK, LANES), f32)] * 4
        + [pltpu.VMEM((2 * BLK, LANES), bf16)] * 4,
        compiler_params=pltpu.CompilerParams(dimension_semantics=("parallel",)),
    )(proj, proj, proj, proj)


def _expand_mat():
    colv = np.arange(2 * D_SSM)
    head = 2 * ((colv % D_SSM) // LANES) + colv // D_SSM
    return jnp.asarray(np.arange(LANES)[:, None] == head[None, :], dtype=bf16)


def _fold_mat():
    return jnp.asarray((np.arange(D_SSM) // HEAD_DIM)[:, None] == np.arange(LANES)[None, :], dtype=bf16)


def _ssd_common(xs_ref, bc_ref, xs_tail, bc_tail, dt_ref, cw_ref, cb_ref, dtb_ref, alog16_ref, emat_ref, xpad, first):
    keep = jnp.where(first, 0.0, 1.0)
    xpad[0:8, 0:D_SSM] = xs_tail[...] * keep
    xpad[0:8, D_SSM:D_CONV] = bc_tail[...] * keep
    xpad[8:8 + CHUNK, 0:D_SSM] = xs_ref[...]
    xpad[8:8 + CHUNK, D_SSM:D_CONV] = bc_ref[...]
    xp = xpad[...]
    taps = [pltpu.roll(xp, 3 - j, 0)[8:8 + CHUNK] for j in range(3)] + [xp[8:8 + CHUNK]]
    cv = cb_ref[...] + cw_ref[0:1, :] * taps[0]
    for j in range(1, 4):
        cv = cv + cw_ref[j:j + 1, :] * taps[j]
    sig = _sigmoid(cv)
    xbc = cv * sig

    pre = dt_ref[...] + dtb_ref[...]
    dt16 = _softplus(pre)
    a16 = -jnp.exp(alog16_ref[...])
    sub, lane = _iota((CHUNK, CHUNK), 0), _iota((CHUNK, CHUNK), 1)
    tri = (sub >= lane).astype(f32)
    al16 = _nn_hi(tri, dt16 * a16)
    al_t = al16.T
    emat = emat_ref[...]
    dt_x = _dot_01(dt16, emat, 3)
    al_x = _dot_01(al16, emat, 3)
    lane_w = _iota((CHUNK, D_SSM), 1)
    even = (lane_w % LANES) < HEAD_DIM
    dt_f = jnp.where(even, dt_x[:, :D_SSM], dt_x[:, D_SSM:])
    al_f = jnp.where(even, al_x[:, :D_SSM], al_x[:, D_SSM:])
    return cv, sig, xbc, pre, dt_f, al_f, al_x, al_t, taps


def _decay_mat(al_x, al_t, pair, h):
    sub, lane = _iota((CHUNK, CHUNK), 0), _iota((CHUNK, CHUNK), 1)
    col = al_x[:, h * D_SSM + pair * LANES: h * D_SSM + (pair + 1) * LANES]
    row = al_t[2 * pair + h: 2 * pair + h + 1, :]
    return jnp.exp(jnp.where(sub >= lane, col - row, -jnp.inf))


def _ssd_in_specs(order):
    blk = lambda w, cb: pl.BlockSpec((CHUNK, w), lambda i: (order(i), cb))
    tail = lambda w, cb: pl.BlockSpec((8, w), lambda i: (jnp.maximum(16 * order(i) - 1, 0), cb))
    return [blk(D_SSM, COL_XS // D_SSM), blk(512, COL_BC // 512), tail(D_SSM, COL_XS // D_SSM),
            tail(512, COL_BC // 512), blk(LANES, COL_DT // LANES), blk(D_SSM, COL_Z // D_SSM)]


def _full(shape):
    return pl.BlockSpec(shape, lambda i: (0,) * len(shape))


def _ssd_fwd(proj, conv_w, conv_b, dtb16, alog16, alog_f, d_f, nw):
    s = proj.shape[0]
    nc = s // CHUNK

    def body(xs_ref, bc_ref, xs_tail, bc_tail, dt_ref, z_ref, cw_ref, cb_ref, dtb_ref, alog16_ref, alogf_ref,
             df_ref, nw_ref, emat_ref, mix_ref, y_ref, st_ref, h_scr, xpad, y_scr):
        c = pl.program_id(0)

        @pl.when(c == 0)
        def _():
            h_scr[...] = jnp.zeros_like(h_scr)

        _, _, xbc, _, dt_f, al_f, al_x, al_t, _ = _ssd_common(
            xs_ref, bc_ref, xs_tail, bc_tail, dt_ref, cw_ref, cb_ref, dtb_ref, alog16_ref, emat_ref, xpad, c == 0)
        head0 = _iota((CHUNK, LANES), 1) < HEAD_DIM
        st_ref[...] = h_scr[...]
        for g in range(N_GROUPS):
            bm = xbc[:, D_SSM + g * D_STATE: D_SSM + (g + 1) * D_STATE].astype(bf16)
            cm = xbc[:, D_SSM + (N_GROUPS + g) * D_STATE: D_SSM + (N_GROUPS + g + 1) * D_STATE].astype(bf16)
            gmat = _nt(cm, bm)
            for pair in range(4 * g, 4 * g + 4):
                sl = slice(pair * LANES, (pair + 1) * LANES)
                xp, dtp, alp = xbc[:, sl], dt_f[:, sl], al_f[:, sl]
                xdt = xp * dtp
                xdt16 = xdt.astype(bf16)
                al_last = alp[CHUNK - 1:CHUNK, :]
                hp = h_scr[:, sl]
                y_off = jnp.exp(alp) * _nn(cm, hp.astype(bf16))
                yd = [_nn((gmat * _decay_mat(al_x, al_t, pair, h)).astype(bf16), xdt16) for h in range(2)]
                y_scr[:, sl] = jnp.where(head0, yd[0], yd[1]) + y_off + df_ref[:, sl] * xp
                st = _tn(bm, (jnp.exp(al_last - alp) * xdt).astype(bf16))
                h_scr[:, sl] = jnp.exp(al_last) * hp + st
        y = y_scr[...]
        y_ref[...] = y
        z = z_ref[...]
        yz = y * (z * _sigmoid(z))
        gw = D_SSM // N_GROUPS
        for g in range(N_GROUPS):
            part = yz[:, g * gw:(g + 1) * gw]
            r = lax.rsqrt(jnp.mean(part * part, axis=-1, keepdims=True) + EPS)
            mix_ref[:, g * gw:(g + 1) * gw] = (part * r * nw_ref[:, g * gw:(g + 1) * gw]).astype(bf16)

    order = lambda i: i
    row = lambda w: pl.BlockSpec((CHUNK, w), lambda i: (i, 0))
    return pl.pallas_call(
        body, name="ssd_fwd", grid=(nc,),
        in_specs=_ssd_in_specs(order) + [_full((4, D_CONV)), _full((1, D_CONV)), _full((1, LANES)), _full((1, LANES)),
                                         _full((1, D_SSM)), _full((1, D_SSM)), _full((1, D_SSM)),
                                         _full((LANES, 2 * D_SSM))],
        out_specs=[row(D_SSM), row(D_SSM), pl.BlockSpec((None, D_STATE, D_SSM), lambda i: (i, 0, 0))],
        out_shape=[SDS((s, D_SSM), bf16), SDS((s, D_SSM), f32), SDS((nc, D_STATE, D_SSM), f32)],
        scratch_shapes=[pltpu.VMEM((D_STATE, D_SSM), f32), pltpu.VMEM((8 + CHUNK, D_CONV), f32),
                        pltpu.VMEM((CHUNK, D_SSM), f32)],
        compiler_params=pltpu.CompilerParams(dimension_semantics=("arbitrary",)),
    )(proj, proj, proj, proj, proj, proj, conv_w, conv_b, dtb16, alog16, alog_f, d_f, nw, _expand_mat())


def _outproj_loss(mix_a, mix_s, wo, x, tgt, npw):
    s, d = x.shape
    tm = 512

    def body(ma_ref, ms_ref, wo_ref, x_ref, t_ref, npw_ref, dmix_ref, dout_ref, dres_ref, acc_ref):
        @pl.when(pl.program_id(0) == 0)
        def _():
            acc_ref[...] = jnp.zeros_like(acc_ref)

        out = _nn(ma_ref[...], wo_ref[0:D_ATTN, :]) + _nn(ms_ref[...], wo_ref[D_ATTN:, :])
        r = lax.rsqrt(jnp.mean(out * out, axis=-1, keepdims=True) + EPS)
        on = out * r
        diff = x_ref[...] + on * npw_ref[...] - t_ref[...]
        dres = diff * (1.0 / d)
        dres_ref[...] = dres
        acc_ref[0:1, :] += jnp.sum(diff * diff, axis=0, keepdims=True)
        acc_ref[1:2, :] += jnp.sum(dres * on, axis=0, keepdims=True)
        dn = dres * npw_ref[...]
        dout = (r * (dn - on * jnp.mean(dn * on, axis=-1, keepdims=True))).astype(bf16)
        dout_ref[...] = dout
        dmix_ref[...] = _nt(dout, wo_ref[...])

    row = lambda w: pl.BlockSpec((tm, w), lambda i: (i, 0))
    return pl.pallas_call(
        body, name="outproj_loss", grid=(s // tm,),
        in_specs=[row(D_ATTN), row(D_SSM), _full((D_ATTN + D_SSM, d)), row(d), row(d), _full((1, d))],
        out_specs=[row(D_ATTN + D_SSM), row(d), row(d), _full((8, d))],
        out_shape=[SDS((s, D_ATTN + D_SSM), f32), SDS((s, d), bf16), SDS((s, d), f32), SDS((8, d), f32)],
        compiler_params=pltpu.CompilerParams(dimension_semantics=("arbitrary",)),
    )(mix_a, mix_s, wo, x, tgt, npw)


def _attn_bwd(proj, o, lb, dmix):
    s = proj.shape[0]
    n_it = s // BLK

    def body(q_ref, k_ref, v_ref, g_ref, o_ref, l_ref, dm_ref, dq_ref, dk_ref, dv_ref, dg_ref,
             dq_acc, dk_acc, dv_acc, do_scr, dl_scr, *bufs):
        head0, tri2, _, _, bones = _attn_consts()

        def pro(i, carry):
            rows = pl.ds(pl.multiple_of(i * 256, 256), 256)
            g = g_ref[rows, :]
            sg = _sigmoid(g)
            dmx = dm_ref[rows, :]
            ov = o_ref[rows, :]
            dg_ref[rows, :] = (dmx * ov * (sg * (1.0 + g * (1.0 - sg)))).astype(bf16)
            do = dmx * (g * sg)
            do_scr[rows, :] = do
            dl_scr[rows, :] = _split_dot_sum(do * ov, bones)
            z = jnp.zeros((256, LANES), f32)
            dq_acc[rows, :] = z
            dk_acc[rows, :] = z
            dv_acc[rows, :] = z
            return carry

        lax.fori_loop(0, s // 256, pro, 0)

        def per_head(t):
            return jnp.concatenate([t[:, :LANES], t[:, LANES:]], axis=0)

        def both_heads(t):
            tr = pltpu.roll(t, HEAD_DIM, 1)
            return jnp.concatenate([jnp.where(head0, t, tr), jnp.where(head0, tr, t)], axis=1)

        mm_bufs = ((bufs[0], bufs[1], bufs[2], bufs[3]), (bufs[4], bufs[5], bufs[6], bufs[7]))
        ds_bufs = ((bufs[8], bufs[9], bufs[10], bufs[11]), (bufs[12], bufs[13], bufs[14], bufs[15]))
        op_bufs = ((bufs[16], bufs[17], bufs[18], bufs[19]), (bufs[20], bufs[21], bufs[22], bufs[23]))
        vc_bufs, carry_k, carry_v = (bufs[24], bufs[25]), bufs[26], bufs[27]
        for buf in (op_bufs[0][0], op_bufs[1][0]) + vc_bufs:
            buf[...] = jnp.zeros_like(buf)

        def block_rows(i, d, nb):
            r, blk = i // nb, i % nb
            return pl.ds(blk * (BLK * d) + r, BLK, stride=d), blk > 0

        def unstack(st16):
            return st16[:BLK] + st16[BLK:]

        def products(i, par, d, nb):
            rows, has_prev = block_rows(i, d, nb)
            s_buf, dp_buf, sd_buf, dpd_buf = mm_bufs[par]
            kc_buf, kp_buf, q_buf, do_buf = op_bufs[par]
            q = q_ref[rows, :]
            qs = q * 0.125
            do = do_scr[rows, :]
            qs16, do16 = qs.astype(bf16), do.astype(bf16)
            kst_c = _stack_heads(k_ref[rows, :].astype(bf16), head0)
            vst_c = _stack_heads(v_ref[rows, :].astype(bf16), head0)
            kst_p, vst_p = op_bufs[1 - par][0][...], vc_bufs[1 - par][...]
            kc_buf[...] = kst_c
            kp_buf[...] = kst_p
            vc_bufs[par][...] = vst_c
            q_buf[...] = q.astype(bf16)
            do_buf[...] = do16
            s_buf[...] = jnp.where(tri2, _nt(qs16, kst_c), jnp.where(has_prev, _nt(qs16, kst_p), -jnp.inf))
            dp_buf[...] = jnp.where(tri2, _nt(do16, vst_c), jnp.where(has_prev, _nt(do16, vst_p), 0.0))
            sd_buf[...] = _split_dot_sum(qs * unstack(kst_p).astype(f32), bones)
            dpd_buf[...] = jnp.where(has_prev, _split_dot_sum(do * unstack(vst_p).astype(f32), bones), 0.0)

        def softmax_grad(i, par, d, nb):
            rows, has_prev = block_rows(i, d, nb)
            s_buf, dp_buf, sd_buf, dpd_buf = mm_bufs[par]
            p_buf, ds_buf, pd_buf, dsd_buf = ds_bufs[par]
            lse = l_ref[rows, :]
            dl = dl_scr[rows, :]
            pt = jnp.exp(s_buf[...] - both_heads(lse))
            ds_buf[...] = (pt * (dp_buf[...] - both_heads(dl)) * 0.125).astype(bf16)
            p_buf[...] = pt.astype(bf16)
            pd = jnp.where(has_prev, jnp.exp(sd_buf[...] - lse), 0.0)
            pd_buf[...] = pd
            dsd_buf[...] = pd * (dpd_buf[...] - dl) * 0.125

        def accumulate(i, par, d, nb):
            rows, _ = block_rows(i, d, nb)
            before, _ = block_rows(jnp.maximum(i - 1, 0), d, nb)
            p_buf, ds_buf, pd_buf, dsd_buf = ds_bufs[par]
            kc_buf, kp_buf, q_buf, do_buf = op_bufs[par]
            pt16, ds16, pd, dsd = p_buf[...], ds_buf[...], pd_buf[...], dsd_buf[...]
            zero = jnp.zeros_like(pt16)
            dsc, dsp = jnp.where(tri2, ds16, zero), jnp.where(tri2, zero, ds16)
            pc, pp = jnp.where(tri2, pt16, zero), jnp.where(tri2, zero, pt16)
            kst_c, kst_p, q16, do16 = kc_buf[...], kp_buf[...], q_buf[...], do_buf[...]
            qst, dost = _stack_heads(q16, head0), _stack_heads(do16, head0)
            dq_acc[rows, :] += _nn(dsc, kst_c) + _nn(dsp, kst_p) + dsd * unstack(kst_p).astype(f32)
            dk2 = _tn(jnp.concatenate([per_head(dsc), per_head(dsp)], axis=1), qst)
            dv2 = _tn(jnp.concatenate([per_head(pc), per_head(pp)], axis=1), dost)
            dk_acc[before, :] += carry_k[...] + dk2[BLK:] + dsd * q16.astype(f32)
            dv_acc[before, :] += carry_v[...] + dv2[BLK:] + pd * do16.astype(f32)
            carry_k[...] = dk2[:BLK]
            carry_v[...] = dv2[:BLK]

        for d in DILATIONS:
            nb = s // (BLK * d)
            carry_k[...] = jnp.zeros_like(carry_k)
            carry_v[...] = jnp.zeros_like(carry_v)
            products(0, 0, d, nb)
            products(1, 1, d, nb)
            softmax_grad(0, 0, d, nb)

            def steps(j, carry, d=d, nb=nb):
                for par in range(2):
                    t = 2 * j + 2 + par
                    accumulate(t - 2, par, d, nb)
                    products(t, par, d, nb)
                    softmax_grad(t - 1, 1 - par, d, nb)
                return carry

            lax.fori_loop(0, (n_it - 2) // 2, steps, 0)
            accumulate(n_it - 2, 0, d, nb)
            softmax_grad(n_it - 1, 1, d, nb)
            accumulate(n_it - 1, 1, d, nb)
            last, _ = block_rows(n_it - 1, d, nb)
            dk_acc[last, :] += carry_k[...]
            dv_acc[last, :] += carry_v[...]

        def epi(i, carry):
            rows = pl.ds(pl.multiple_of(i * 256, 256), 256)
            dq_ref[rows, :] = dq_acc[rows, :].astype(bf16)
            dk_ref[rows, :] = dk_acc[rows, :].astype(bf16)
            dv_ref[rows, :] = dv_acc[rows, :].astype(bf16)
            return carry

        lax.fori_loop(0, s // 256, epi, 0)

    col = lambda base: pl.BlockSpec((s, LANES), lambda h: (0, base + h))
    outs = pl.pallas_call(
        body, name="attn_bwd", grid=(N_PAIRS,),
        in_specs=[col(0), col(8), col(16), col(24), col(0), col(0), col(0)],
        out_specs=[col(0)] * 4,
        out_shape=[SDS((s, D_ATTN), bf16)] * 4,
        scratch_shapes=[pltpu.VMEM((s, LANES), f32)] * 5
        + [pltpu.VMEM((BLK, 2 * LANES), f32)] * 2 + [pltpu.VMEM((BLK, LANES), f32)] * 2
        + [pltpu.VMEM((BLK, 2 * LANES), f32)] * 2 + [pltpu.VMEM((BLK, LANES), f32)] * 2
        + [pltpu.VMEM((BLK, 2 * LANES), bf16)] * 2 + [pltpu.VMEM((BLK, LANES), f32)] * 2
        + [pltpu.VMEM((BLK, 2 * LANES), bf16)] * 2 + [pltpu.VMEM((BLK, LANES), f32)] * 2
        + [pltpu.VMEM((2 * BLK, LANES), bf16)] * 2 + [pltpu.VMEM((BLK, LANES), bf16)] * 2
        + [pltpu.VMEM((2 * BLK, LANES), bf16)] * 2 + [pltpu.VMEM((BLK, LANES), bf16)] * 2
        + [pltpu.VMEM((2 * BLK, LANES), bf16)] * 2 + [pltpu.VMEM((BLK, LANES), f32)] * 2,
        compiler_params=pltpu.CompilerParams(dimension_semantics=("parallel",)),
    )(proj, proj, proj, proj, o, lb, dmix)
    return outs


def _ssd_bwd(proj, y, states, dmix, conv_w, conv_b, dtb16, alog16, alog_f, d_f, nw):
    s = proj.shape[0]
    nc = s // CHUNK
    gw = D_SSM // N_GROUPS

    def body(xs_ref, bc_ref, xs_tail, bc_tail, dt_ref, z_ref, y_ref, st_ref, dm_ref, cw_ref, cb_ref, dtb_ref,
             alog16_ref, alogf_ref, df_ref, nw_ref, emat_ref, fold_ref, out_ref, gconv_ref, gvec_ref, gdt_ref,
             dh_scr, head_scr, xpad, dcpad, da_scr, dxdt_scr, dbc_scr):
        i = pl.program_id(0)
        c = nc - 1 - i

        @pl.when(i == 0)
        def _():
            dh_scr[...] = jnp.zeros_like(dh_scr)
            head_scr[...] = jnp.zeros_like(head_scr)
            gconv_ref[...] = jnp.zeros_like(gconv_ref)
            gvec_ref[...] = jnp.zeros_like(gvec_ref)
            gdt_ref[...] = jnp.zeros_like(gdt_ref)

        cv, sig, xbc, pre, dt_f, al_f, al_x, al_t, taps = _ssd_common(
            xs_ref, bc_ref, xs_tail, bc_tail, dt_ref, cw_ref, cb_ref, dtb_ref, alog16_ref, emat_ref, xpad, c == 0)
        head0 = _iota((CHUNK, LANES), 1) < HEAD_DIM
        sub = _iota((CHUNK, LANES), 0)
        last_row = sub == CHUNK - 1

        yv, z, dmx = y_ref[...], z_ref[...], dm_ref[...]
        sz = _sigmoid(z)
        silu = z * sz
        yz = yv * silu
        dyz_parts = []
        for g in range(N_GROUPS):
            gs = slice(g * gw, (g + 1) * gw)
            part = yz[:, gs]
            r = lax.rsqrt(jnp.mean(part * part, axis=-1, keepdims=True) + EPS)
            nh = part * r
            gvec_ref[0:1, gs] += jnp.sum(dmx[:, gs] * nh, axis=0, keepdims=True)
            dn = dmx[:, gs] * nw_ref[:, gs]
            dyz_parts.append(r * (dn - nh * jnp.mean(dn * nh, axis=-1, keepdims=True)))
        dyz = jnp.concatenate(dyz_parts, axis=1)
        dy = dyz * silu
        out_ref[:, 0:D_SSM] = (dyz * yv * (sz * (1.0 + z * (1.0 - sz)))).astype(bf16)

        x_all = xbc[:, 0:D_SSM]
        gvec_ref[2:3, :] += jnp.sum(dy * x_all, axis=0, keepdims=True)

        for g in range(N_GROUPS):
            bm = xbc[:, D_SSM + g * D_STATE: D_SSM + (g + 1) * D_STATE].astype(bf16)
            cm = xbc[:, D_SSM + (N_GROUPS + g) * D_STATE: D_SSM + (N_GROUPS + g + 1) * D_STATE].astype(bf16)
            gmat = _nt(cm, bm)
            dgm = jnp.zeros((CHUNK, CHUNK), f32)
            db = jnp.zeros((CHUNK, D_STATE), f32)
            dc = jnp.zeros((CHUNK, D_STATE), f32)
            for pair in range(4 * g, 4 * g + 4):
                sl = slice(pair * LANES, (pair + 1) * LANES)
                xp, dtp, alp, dyp = x_all[:, sl], dt_f[:, sl], al_f[:, sl], dy[:, sl]
                xdt = xp * dtp
                xdt16 = xdt.astype(bf16)
                al_last = alp[CHUNK - 1:CHUNK, :]
                e_l = jnp.exp(alp)
                wf = jnp.exp(al_last - alp)
                e_last = jnp.exp(al_last)
                hp = st_ref[:, sl]
                hp16 = hp.astype(bf16)
                dhn = dh_scr[:, sl]
                dhn16 = dhn.astype(bf16)
                y_off = e_l * _nn(cm, hp16)
                dch16 = (dyp * e_l).astype(bf16)
                dc = dc + _nt(dch16, hp16)
                dh_out = _tn(cm, dch16)
                dal = dyp * y_off
                xw16 = (wf * xdt).astype(bf16)
                db = db + _nt(xw16, dhn16)
                dxw = _nn(bm, dhn16)
                dxdt = dxw * wf
                dwf = dxw * xdt * wf
                dal = dal - dwf
                dal_last = jnp.sum(dwf, axis=0, keepdims=True) + jnp.sum(dhn * hp, axis=0, keepdims=True) * e_last
                dh_scr[:, sl] = e_last * dhn + dh_out
                for h in range(2):
                    mh = head0 if h == 0 else jnp.logical_not(head0)
                    dyh16 = jnp.where(mh, dyp, 0.0).astype(bf16)
                    lmat = _decay_mat(al_x, al_t, pair, h)
                    mm = gmat * lmat
                    dmm = _nt(dyh16, xdt16)
                    dxdt = dxdt + _tn(mm.astype(bf16), dyh16)
                    n16 = (dmm * mm).astype(bf16)
                    jh = jnp.where(mh, 1.0 / HEAD_DIM, 0.0).astype(bf16)
                    dal = dal + _nn(n16, jh) - _tn(n16, jh)
                    dgm = dgm + dmm * lmat
                da_scr[:, sl] = dal + jnp.where(last_row, dal_last, 0.0)
                dxdt_scr[:, sl] = dxdt
            dgm16 = dgm.astype(bf16)
            dbc_scr[:, g * D_STATE:(g + 1) * D_STATE] = db + _tn(dgm16, cm)
            dbc_scr[:, (N_GROUPS + g) * D_STATE:(N_GROUPS + g + 1) * D_STATE] = dc + _nn(dgm16, bm)

        sub_c, lane_c = _iota((CHUNK, CHUNK), 0), _iota((CHUNK, CHUNK), 1)
        tri_t = (lane_c >= sub_c).astype(bf16)
        dadt = _dot_01_left(tri_t, da_scr[...], 2)
        a_f = -jnp.exp(alogf_ref[...])
        dxdt_all = dxdt_scr[...]
        ddt_f = dxdt_all * x_all + a_f * dadt
        gvec_ref[1:2, :] += jnp.sum(dt_f * dadt, axis=0, keepdims=True) * a_f
        dx = df_ref[...] * dy + dxdt_all * dt_f
        ddt_raw = _dot_01(ddt_f, fold_ref[...], 2) * _sigmoid(pre)
        gdt_ref[0:1, :] += jnp.sum(ddt_raw, axis=0, keepdims=True)
        out_ref[:, D_SSM + D_CONV:D_SSM + D_CONV + LANES] = ddt_raw.astype(bf16)
        out_ref[:, D_SSM + D_CONV + LANES:] = jnp.zeros((CHUNK, 3 * LANES), bf16)

        dsil = sig * (1.0 + cv * (1.0 - sig))
        dcv_x = dx * dsil[:, 0:D_SSM]
        dcv_bc = dbc_scr[...] * dsil[:, D_SSM:]
        dcpad[0:CHUNK, 0:D_SSM] = dcv_x
        dcpad[0:CHUNK, D_SSM:] = dcv_bc
        dcpad[CHUNK:, :] = head_scr[...]
        dcp = dcpad[...]
        dcv = dcp[0:CHUNK]
        gconv_ref[4:5, :] += jnp.sum(dcv, axis=0, keepdims=True)
        draw = cw_ref[3:4, :] * dcv
        for j in range(4):
            gconv_ref[j:j + 1, :] += jnp.sum(dcv * taps[j], axis=0, keepdims=True)
        for j in range(3):
            draw = draw + cw_ref[j:j + 1, :] * pltpu.roll(dcp, CHUNK + 8 - (3 - j), 0)[0:CHUNK]
        head_scr[...] = dcv[0:8]
        out_ref[:, D_SSM:D_SSM + D_CONV] = draw.astype(bf16)

    order = lambda i: nc - 1 - i
    row = lambda w, cb=0: pl.BlockSpec((CHUNK, w), lambda i: (nc - 1 - i, cb))
    return pl.pallas_call(
        body, name="ssd_bwd", grid=(nc,),
        in_specs=_ssd_in_specs(order) + [row(D_SSM), pl.BlockSpec((None, D_STATE, D_SSM), lambda i: (nc - 1 - i, 0, 0)),
                                         row(D_SSM, 1), _full((4, D_CONV)), _full((1, D_CONV)), _full((1, LANES)),
                                         _full((1, LANES)), _full((1, D_SSM)), _full((1, D_SSM)), _full((1, D_SSM)),
                                         _full((LANES, 2 * D_SSM)), _full((D_SSM, LANES))],
        out_specs=[row(3072), _full((8, D_CONV)), _full((8, D_SSM)), _full((8, LANES))],
        out_shape=[SDS((s, 3072), bf16), SDS((8, D_CONV), f32), SDS((8, D_SSM), f32), SDS((8, LANES), f32)],
        scratch_shapes=[pltpu.VMEM((D_STATE, D_SSM), f32), pltpu.VMEM((8, D_CONV), f32),
                        pltpu.VMEM((8 + CHUNK, D_CONV), f32), pltpu.VMEM((8 + CHUNK, D_CONV), f32),
                        pltpu.VMEM((CHUNK, D_SSM), f32), pltpu.VMEM((CHUNK, D_SSM), f32),
                        pltpu.VMEM((CHUNK, 2 * N_GROUPS * D_STATE), f32)],
        compiler_params=pltpu.CompilerParams(dimension_semantics=("arbitrary",)),
    )(proj, proj, proj, proj, proj, proj, y, states, dmix, conv_w, conv_b, dtb16, alog16, alog_f, d_f, nw,
      _expand_mat(), _fold_mat())


def _col_blocks(parts, tile):
    counts = [p.shape[1] // tile for p in parts]
    offs = [sum(counts[:t]) for t in range(len(parts))]
    return offs, counts, sum(counts)


def _bcast_copies(src_ref, out_ref, send_sems, recv_sems, local_sem):
    x, y, c = _my_pos()
    me = 4 * x + 2 * y + c
    mine = pltpu.make_async_copy(src_ref, out_ref.at[me], local_sem)
    sends, recvs = [], []
    for k in range(1, N_DEV):
        to, frm = (me + k) % N_DEV, (me + N_DEV - k) % N_DEV
        sems = dict(send_sem=send_sems.at[k - 1], recv_sem=recv_sems.at[k - 1], device_id_type=MESH)
        sends.append(pltpu.make_async_remote_copy(
            src_ref=src_ref, dst_ref=out_ref.at[me], device_id=(to // 4, (to // 2) % 2, to % 2), **sems))
        recvs.append(pltpu.make_async_remote_copy(
            src_ref=src_ref, dst_ref=out_ref.at[frm], device_id=(x, y, c), **sems))
    return mine, sends, recvs


def _bcast_scratch():
    return [pltpu.SemaphoreType.DMA((N_DEV - 1,)), pltpu.SemaphoreType.DMA((N_DEV - 1,)), pltpu.SemaphoreType.DMA(())]


def _inproj_bwd(dparts, wt, x, nw, dres, chip_sums=(), pack=None):
    s, d = x.shape
    tm, tk = 1024, 1024
    offs, counts, nk = _col_blocks(dparts, tk)
    npart, nx = len(dparts), len(chip_sums)
    npk = 0 if pack is None else 1
    ni = s // tm

    def body(*refs):
        dp_refs = refs[:npart]
        w_ref, x_ref, nw_ref, dres_ref = refs[npart:npart + 4]
        pos = npart + 4
        cs_in, pos = refs[pos:pos + nx], pos + nx
        pack_in, pos = refs[pos:pos + npk], pos + npk
        (gx_ref, gnw_ref), pos = refs[pos:pos + 2], pos + 2
        cs_out, pos = refs[pos:pos + nx], pos + nx
        pack_out, pos = refs[pos:pos + 2 * npk], pos + 2 * npk
        acc, pos = refs[pos], pos + 1
        cs_sems, pos = refs[pos:pos + 3 * min(nx, 1)], pos + 3 * min(nx, 1)
        pk_refs = refs[pos:]
        i, k = pl.program_id(0), pl.program_id(1)

        def exchange():
            return _chip_exchange_copies(cs_in, cs_out, *cs_sems)

        def pack_copies():
            return _bcast_copies(pack_in[0], pack_out[0], *pk_refs[1:4])

        def gnw_copies():
            return _bcast_copies(pk_refs[0], pack_out[1], *pk_refs[4:7])

        @pl.when(jnp.logical_and(i == 0, k == 0))
        def _():
            gnw_ref[...] = jnp.zeros_like(gnw_ref)
            if nx:
                mine, sends, _ = exchange()
                for cp in mine + sends:
                    cp.start()
            if npk:
                mine, sends, _ = pack_copies()
                for cp in [mine] + sends:
                    cp.start()

        @pl.when(k == 0)
        def _():
            acc[...] = jnp.zeros_like(acc)

        for t in range(npart):
            @pl.when(jnp.logical_and(k >= offs[t], k < offs[t] + counts[t]))
            def _(t=t):
                acc[...] += _nn(dp_refs[t][...], w_ref[...])

        @pl.when(k == nk - 1)
        def _():
            xv = x_ref[...]
            r = lax.rsqrt(jnp.mean(xv * xv, axis=-1, keepdims=True) + EPS)
            xn = xv * r
            du = acc[...]
            gnw_ref[0:1, :] += jnp.sum(du * xn, axis=0, keepdims=True)
            dn = du * nw_ref[...]
            gx_ref[...] = dres_ref[...] + r * (dn - xn * jnp.mean(dn * xn, axis=-1, keepdims=True))

        @pl.when(jnp.logical_and(i == ni - 1, k == nk - 1))
        def _():
            if npk:
                pk_refs[0][...] = gnw_ref[...]
                mine, sends, _ = gnw_copies()
                for cp in [mine] + sends:
                    cp.start()
            if nx:
                mine, sends, recvs = exchange()
                for cp in recvs:
                    cp.wait_recv()
                for cp in sends:
                    cp.wait_send()
                for cp in mine:
                    cp.wait()
            if npk:
                for copies in (pack_copies(), gnw_copies()):
                    mine, sends, recvs = copies
                    for cp in recvs:
                        cp.wait_recv()
                    for cp in sends:
                        cp.wait_send()
                    mine.wait()

    def piece(t):
        return pl.BlockSpec((tm, tk), lambda i, k: (i, jnp.clip(k - offs[t], 0, counts[t] - 1)))

    anyspec = pl.BlockSpec(memory_space=pl.ANY)
    packs = [] if pack is None else [pack]
    pack_shapes = [] if pack is None else [SDS((N_DEV,) + pack.shape, f32), SDS((N_DEV, 8, d), f32)]
    scratch = [pltpu.VMEM((tm, d), f32)] + (_chip_exchange_scratch(nx) if nx else [])
    if npk:
        scratch += [pltpu.VMEM((8, d), f32)] + _bcast_scratch() + _bcast_scratch()
    outs = pl.pallas_call(
        body, name="inproj_bwd", grid=(ni, nk),
        in_specs=[piece(t) for t in range(npart)] + [
            pl.BlockSpec((tk, d), lambda i, k: (k, 0)),
            pl.BlockSpec((tm, d), lambda i, k: (i, 0)), pl.BlockSpec((1, d), lambda i, k: (0, 0)),
            pl.BlockSpec((tm, d), lambda i, k: (i, 0))] + [anyspec] * (nx + npk),
        out_specs=[pl.BlockSpec((tm, d), lambda i, k: (i, 0)), pl.BlockSpec((8, d), lambda i, k: (0, 0))]
        + [anyspec] * (nx + 2 * npk),
        out_shape=[SDS((s, d), f32), SDS((8, d), f32)] + [SDS(a.shape, a.dtype) for a in chip_sums] + pack_shapes,
        scratch_shapes=scratch,
        compiler_params=pltpu.CompilerParams(dimension_semantics=("arbitrary", "arbitrary")),
    )(*dparts, wt, x, nw, dres, *chip_sums, *packs)
    return outs[0], outs[1], outs[2:2 + nx], outs[2 + nx:]


def _matmul_tn(a_parts, b_parts, name):
    tile, tk = 1024, 1024
    s = a_parts[0].shape[0]
    nk = s // tk
    na, nb = len(a_parts), len(b_parts)
    offs_a, counts_a, ni = _col_blocks(a_parts, tile)
    offs_b, counts_b, nj = _col_blocks(b_parts, tile)

    def body(*refs):
        a_refs, b_refs, o_ref = refs[:na], refs[na:na + nb], refs[na + nb]
        i, j = pl.program_id(0), pl.program_id(1)

        @pl.when(pl.program_id(2) == 0)
        def _():
            o_ref[...] = jnp.zeros_like(o_ref)

        for ta in range(na):
            for tb in range(nb):
                in_a = jnp.logical_and(i >= offs_a[ta], i < offs_a[ta] + counts_a[ta])
                in_b = jnp.logical_and(j >= offs_b[tb], j < offs_b[tb] + counts_b[tb])

                @pl.when(jnp.logical_and(in_a, in_b))
                def _(ta=ta, tb=tb):
                    o_ref[...] += _tn(a_refs[ta][...], b_refs[tb][...])

    def spec(offs, counts, t, axis):
        def index(i, j, k):
            pos = (i, j)[axis]
            mine = jnp.logical_and(pos >= offs[t], pos < offs[t] + counts[t])
            return jnp.where(mine, k, 0), jnp.clip(pos - offs[t], 0, counts[t] - 1)
        return pl.BlockSpec((tk, tile), index)

    return pl.pallas_call(
        body, name=name, grid=(ni, nj, nk),
        in_specs=[spec(offs_a, counts_a, t, 0) for t in range(na)] + [spec(offs_b, counts_b, t, 1) for t in range(nb)],
        out_specs=pl.BlockSpec((tile, tile), lambda i, j, k: (i, j)),
        out_shape=SDS((ni * tile, nj * tile), f32),
        compiler_params=pltpu.CompilerParams(dimension_semantics=("parallel", "parallel", "arbitrary")),
    )(*a_parts, *b_parts)


def _adamw(w, g, m, v):
    m = ADAM_B1 * m + (1.0 - ADAM_B1) * g
    v = ADAM_B2 * v + (1.0 - ADAM_B2) * (g * g)
    m_hat = m / (1.0 - ADAM_B1 ** ADAM_STEP)
    v_hat = v / (1.0 - ADAM_B2 ** ADAM_STEP)
    delta = -ADAM_LR * (m_hat / (jnp.sqrt(v_hat) + ADAM_EPS) + ADAM_WD * w)
    return delta, m, v


def _sum_adamw(parts, w, m, v, name):
    r, c = w.shape
    tc = 256

    def body(p_ref, w_ref, m_ref, v_ref, g_ref, d_ref, nm_ref, nv_ref):
        g = p_ref[0].astype(f32)
        for q in range(1, 4):
            g = g + p_ref[q].astype(f32)
        g_ref[...] = g
        d_ref[...], nm_ref[...], nv_ref[...] = _adamw(w_ref[...], g, m_ref[...], v_ref[...])

    blk = pl.BlockSpec((r, tc), lambda i: (0, i))
    return pl.pallas_call(
        body, name=name, grid=(c // tc,),
        in_specs=[pl.BlockSpec((4, r, tc), lambda i: (0, 0, i)), blk, blk, blk],
        out_specs=[blk] * 4, out_shape=[SDS((r, c), f32)] * 4,
        compiler_params=pltpu.CompilerParams(dimension_semantics=("parallel",)),
    )(parts, w, m, v)


def _sum_small(parts, pre_blocks):
    def body(p_ref, b_ref, o_ref):
        t = p_ref[0]
        pre = b_ref[0]
        for j in range(1, N_DEV):
            t = t + p_ref[j]
            pre = pre + b_ref[j]
        o_ref[...] = t
        o_ref[5:6, 0:D_MODEL] = pre[0:1, :]
        row_h = _iota((D_SSM, LANES), 0) // HEAD_DIM
        fold = (row_h == _iota((D_SSM, LANES), 1)).astype(f32)
        lower = t[8:16, 0:LANES]
        folded = _nn_hi(t[8:16, 0:D_SSM], fold)
        loss = jnp.sum(t[11:12, 0:D_MODEL], axis=1, keepdims=True) * (0.5 / D_MODEL)
        row = _iota((8, LANES), 0)
        o_ref[8:16, 0:LANES] = jnp.where(row < 2, folded, jnp.where(row == 4, loss, lower))

    return pl.pallas_call(body, name="sum_small", out_shape=SDS((PACK_ROWS, PACK_W), f32),
                          in_specs=[pl.BlockSpec(memory_space=pltpu.VMEM)] * 2,
                          out_specs=pl.BlockSpec(memory_space=pltpu.VMEM))(parts, pre_blocks)


def _adamw_small(w, g, m, v):
    def body(w_ref, g_ref, m_ref, v_ref, d_ref, nm_ref, nv_ref):
        d_ref[...], nm_ref[...], nv_ref[...] = _adamw(w_ref[...], g_ref[...], m_ref[...], v_ref[...])

    vm = pl.BlockSpec(memory_space=pltpu.VMEM)
    return pl.pallas_call(body, name="adamw_small", out_shape=[SDS(w.shape, f32)] * 3,
                          in_specs=[vm] * 4, out_specs=[vm] * 3)(w, g, m, v)


def _pad_lanes(v, width):
    return jnp.pad(v, ((0, 0), (0, width - v.shape[1])))


def _local_step(x, tgt, norm_pre_w, wt, conv_w, conv_b, dt_bias, a_log, d_skip, ssm_norm_w, wo, norm_post_w,
                weight_grads, sharded):
    dtb16 = _pad_lanes(dt_bias, LANES)
    alog16 = _pad_lanes(a_log, LANES)
    alog_f = jnp.repeat(a_log, HEAD_DIM, axis=1)
    d_f = jnp.repeat(d_skip, HEAD_DIM, axis=1)

    if sharded:
        proj, u, (g_out, g_cw) = _prenorm_inproj(x, norm_pre_w, wt, gather=(wo, conv_w))
        wo = g_out.reshape(N_DEV * wo.shape[0], D_MODEL)
        conv_w = g_cw.transpose(1, 0, 2).reshape(4, D_CONV)
    else:
        proj, u, _ = _prenorm_inproj(x, norm_pre_w, wt)
    o, lb, mix_a = _attn_fwd(proj)
    mix_s, y, states = _ssd_fwd(proj, conv_w, conv_b, dtb16, alog16, alog_f, d_f, ssm_norm_w)
    dmix, dout, dres, acc_post = _outproj_loss(mix_a, mix_s, wo, x, tgt, norm_post_w)
    dq, dk, dv, dg = _attn_bwd(proj, o, lb, dmix)
    dzxd, g_conv, g_vec, g_dt = _ssd_bwd(proj, y, states, dmix, conv_w, conv_b, dtb16, alog16, alog_f, d_f, ssm_norm_w)
    dparts = [dq, dk, dv, dg, dzxd]
    dw_out = _matmul_tn([mix_a, mix_s], [dout], "dw_out")
    chip_sums, carry = weight_grads(dparts, u, dw_out)

    def pack(g_pre_row):
        return jnp.concatenate(
            [g_conv[0:5], g_pre_row, _pad_lanes(g_vec[0:1], PACK_W), _pad_lanes(acc_post[1:2], PACK_W),
             _pad_lanes(g_vec[1:3], PACK_W), _pad_lanes(g_dt[0:1], PACK_W), _pad_lanes(acc_post[0:1], PACK_W),
             jnp.zeros((4, PACK_W), f32)], axis=0)

    if sharded:
        grad_x, _, exchanged, small = _inproj_bwd(dparts, wt, x, norm_pre_w, dres, chip_sums,
                                                  pack(jnp.zeros((1, PACK_W), f32)))
    else:
        grad_x, g_pre, exchanged, _ = _inproj_bwd(dparts, wt, x, norm_pre_w, dres, chip_sums)
        small = pack(_pad_lanes(g_pre[0:1], PACK_W))
    return grad_x, carry, exchanged, small


def kernel(x, norm_pre_w, w_in, conv_w, conv_b, dt_bias, a_log, d_skip, ssm_norm_w, w_out, norm_post_w, loss_target, m_norm_pre_w, m_w_in, m_conv_w, m_conv_b, m_dt_bias, m_a_log, m_d_skip, m_ssm_norm_w, m_w_out, m_norm_post_w, v_norm_pre_w, v_w_in, v_conv_w, v_conv_b, v_dt_bias, v_a_log, v_d_skip, v_ssm_norm_w, v_w_out, v_norm_post_w):
    shard_in = w_in.shape[2]
    shard_cv = conv_w.shape[2]
    me = 4 * lax.axis_index("x") + 2 * lax.axis_index("y") + lax.axis_index("c")

    g_in, = _all_gather([w_in[0].T.astype(bf16)])
    wt = _assemble_wt(g_in)

    def weight_grads(dparts, u, dw_out):
        dw_in, got_in, got_out = _dw_in_swap(dparts, u, dw_out)
        return [_chip_sum(dw_in, got_in, shard_in, "chip_sum_w_in"),
                _chip_sum(dw_out, got_out, w_out.shape[1], "chip_sum_w_out")], ()

    grad_x, _, (parts_in, parts_out), (parts_small, pre_blocks) = _local_step(
        x[0], loss_target[0], norm_pre_w, wt, conv_w[0], conv_b, dt_bias, a_log, d_skip, ssm_norm_w,
        w_out[0].astype(bf16), norm_post_w, weight_grads, sharded=True)

    g_w_in, d_w_in, nm_w_in, nv_w_in = (a.T for a in _sum_adamw(
        parts_in, w_in[0].T, m_w_in[0].T, v_w_in[0].T, "sum_adamw_w_in"))
    g_w_out, d_w_out, nm_w_out, nv_w_out = _sum_adamw(parts_out, w_out[0], m_w_out[0], v_w_out[0], "sum_adamw_w_out")
    tot = _sum_small(parts_small, pre_blocks)

    g_cw_all = tot[0:4]
    small_g = {
        "conv_w": lax.dynamic_slice(g_cw_all, (0, me * shard_cv), (4, shard_cv)),
        "conv_b": tot[4:5], "norm_pre_w": tot[5:6, :D_MODEL], "ssm_norm_w": tot[6:7, :D_SSM],
        "norm_post_w": tot[7:8, :D_MODEL], "a_log": tot[8:9, :16], "d_skip": tot[9:10, :16], "dt_bias": tot[10:11, :16],
    }
    loss = tot[12, 0]
    small_w = {"conv_w": (conv_w[0], m_conv_w[0], v_conv_w[0]), "conv_b": (conv_b, m_conv_b, v_conv_b),
               "norm_pre_w": (norm_pre_w, m_norm_pre_w, v_norm_pre_w), "ssm_norm_w": (ssm_norm_w, m_ssm_norm_w, v_ssm_norm_w),
               "norm_post_w": (norm_post_w, m_norm_post_w, v_norm_post_w), "a_log": (a_log, m_a_log, v_a_log),
               "d_skip": (d_skip, m_d_skip, v_d_skip), "dt_bias": (dt_bias, m_dt_bias, v_dt_bias)}
    names = list(small_w)
    sizes = [small_g[k].size for k in names]
    tot_size = sum(sizes)
    pad_to = -(-tot_size // 1024) * 1024

    def flat(arrs):
        v = jnp.concatenate([a.reshape(-1) for a in arrs])
        return jnp.pad(v, (0, pad_to - tot_size)).reshape(pad_to // LANES, LANES)

    fw = flat([small_w[k][0] for k in names])
    fg = flat([small_g[k] for k in names])
    fm = flat([small_w[k][1] for k in names])
    fv = jnp.pad(jnp.concatenate([small_w[k][2].reshape(-1) for k in names]), (0, pad_to - tot_size),
                 constant_values=1.0).reshape(pad_to // LANES, LANES)
    fd, fnm, fnv = _adamw_small(fw, fg, fm, fv)

    def unflat(f):
        out, off = {}, 0
        v = f.reshape(-1)
        for k, n in zip(names, sizes):
            out[k] = v[off:off + n].reshape(small_g[k].shape)
            off += n
        return out

    sd, snm, snv = unflat(fd), unflat(fnm), unflat(fnv)
    lead = lambda a: a[None]
    order = ["norm_pre_w", "w_in", "conv_w", "conv_b", "dt_bias", "a_log", "d_skip", "ssm_norm_w", "w_out", "norm_post_w"]
    grads = dict(small_g, w_in=g_w_in, w_out=g_w_out)
    deltas = dict(sd, w_in=d_w_in, w_out=d_w_out)
    new_m = dict(snm, w_in=nm_w_in, w_out=nm_w_out)
    new_v = dict(snv, w_in=nv_w_in, w_out=nv_w_out)

    def shaped(dct, k):
        a = dct[k]
        return lead(a) if k in ("w_in", "w_out", "conv_w") else a

    return (loss, grad_x[None], *[shaped(grads, k) for k in order], *[shaped(deltas, k) for k in order],
            *[shaped(new_m, k) for k in order], *[shaped(new_v, k) for k in order])
```

```python
import functools
import math

import jax
import jax.numpy as jnp
import numpy as np
from jax import lax
from jax.experimental import pallas as pl
from jax.experimental.pallas import tpu as pltpu

f32, bf16 = jnp.float32, jnp.bfloat16
SDS = jax.ShapeDtypeStruct
HIGHEST = lax.Precision.HIGHEST
MESH = pl.DeviceIdType.MESH

N_DEV = 8
D_MODEL = 1024
D_ATTN = 1024
D_SSM = 1024
HEAD_DIM = 64
N_PAIRS = 8
D_STATE = 128
N_GROUPS = 2
D_CONV = D_SSM + 2 * N_GROUPS * D_STATE
D_IN_PROJ = 4 * D_ATTN + D_SSM + D_CONV + 16
NP = 7168
CHUNK = 128
BLK = 128
DILATIONS = (1, 4, 16)
EPS = 1e-6
LANES = 128
COL_Z, COL_XS, COL_BC, COL_DT = 4096, 5120, 6144, 6656

ADAM_LR, ADAM_B1, ADAM_B2, ADAM_EPS, ADAM_WD, ADAM_STEP = 0.001, 0.9, 0.999, 1e-08, 0.01, 10

PACK_ROWS, PACK_W = 16, 1536


def _nt(a, b):
    return lax.dot_general(a, b, (((1,), (1,)), ((), ())), preferred_element_type=f32)


def _tn(a, b):
    return lax.dot_general(a, b, (((0,), (0,)), ((), ())), preferred_element_type=f32)


def _nn(a, b):
    return jnp.dot(a, b, preferred_element_type=f32)


def _nn_hi(a, b):
    return jnp.dot(a, b, precision=HIGHEST, preferred_element_type=f32)


def _sigmoid(x):
    return 1.0 / (1.0 + jnp.exp(-x))


def _softplus(x):
    return jnp.maximum(x, 0.0) + jnp.log1p(jnp.exp(-jnp.abs(x)))


def _iota(shape, dim):
    return lax.broadcasted_iota(jnp.int32, shape, dim)


def _my_pos():
    return lax.axis_index("x"), lax.axis_index("y"), lax.axis_index("c")


GATHER_SEMS = 9


def _gather_phases(ins, outs, send_sems, recv_sems, local_sems):
    n, ns = len(ins), GATHER_SEMS
    x, y, c = _my_pos()
    me, sibling = (x, y, c), (x, y, 1 - c)
    xn, yn, diag = (1 - x, y), (x, 1 - y), (1 - x, 1 - y)

    def slot(a, px, py, pc):
        return outs[a].at[4 * px + 2 * py + pc]

    def part(a, ref, h):
        width = ins[a].shape[-1]
        if width % (2 * LANES):
            return ref if h == 1 else None
        return ref.at[:, pl.ds(h * (width // 2), width // 2)]

    def copy(a, k, block, to, src=None, h=None):
        src_ref = slot(a, *block) if src is None else src
        dst_ref = slot(a, *block)
        if h is not None:
            src_ref, dst_ref = part(a, src_ref, h), part(a, dst_ref, h)
            if src_ref is None:
                return None
        return pltpu.make_async_remote_copy(
            src_ref=src_ref, dst_ref=dst_ref, send_sem=send_sems.at[ns * a + k], recv_sem=recv_sems.at[ns * a + k],
            device_id=to, device_id_type=MESH)

    def mine():
        return [pltpu.make_async_copy(ins[a], slot(a, *me), local_sems.at[a]) for a in range(n)]

    def own_sends(a):
        return [copy(a, 0, me, sibling, src=ins[a]), copy(a, 1, me, (*xn, c), src=ins[a]),
                copy(a, 2, me, (*yn, c), src=ins[a])]

    def neighbour_relays(a):
        return [copy(a, 4, (*xn, c), sibling), copy(a, 7, (*xn, c), (*yn, c), h=1),
                copy(a, 5, (*yn, c), sibling), copy(a, 8, (*yn, c), (*xn, c), h=0)]

    def diagonal_halves(a):
        return [copy(a, k, (*diag, c), me, h=h) for k, h in ((8, 0), (7, 1))]

    def start_all(cps):
        for cp in cps:
            if cp is not None:
                cp.start()

    def phase0():
        start_all(mine())
        for a in range(n):
            start_all(own_sends(a))

    def phase1():
        for a in range(n):
            copy(a, 1, (*xn, c), me).wait_recv()
            copy(a, 2, (*yn, c), me).wait_recv()
            start_all(neighbour_relays(a))

    def phase2():
        for a in range(n):
            for cp in diagonal_halves(a):
                if cp is not None:
                    cp.wait_recv()
            copy(a, 6, (*diag, c), sibling).start()

    def finish():
        for a in range(n):
            copy(a, 0, sibling, me).wait_recv()
            for j, chip in enumerate((xn, yn, diag)):
                copy(a, 4 + j, (*chip, 1 - c), me).wait_recv()
        for a in range(n):
            for cp in own_sends(a) + neighbour_relays(a) + [copy(a, 6, (*diag, c), sibling)]:
                if cp is not None:
                    cp.wait_send()
        for cp in mine():
            cp.wait()

    return phase0, phase1, phase2, finish


def _gather_scratch(n):
    return [pltpu.SemaphoreType.DMA((GATHER_SEMS * n,)), pltpu.SemaphoreType.DMA((GATHER_SEMS * n,)),
            pltpu.SemaphoreType.DMA((n,))]


def _all_gather(arrs):
    n = len(arrs)

    def body(*refs):
        for phase in _gather_phases(refs[:n], refs[n:2 * n], *refs[2 * n:]):
            phase()

    anyspec = pl.BlockSpec(memory_space=pl.ANY)
    return pl.pallas_call(
        body, name="weights_all_gather",
        out_shape=[SDS((N_DEV,) + a.shape, a.dtype) for a in arrs],
        in_specs=[anyspec] * n, out_specs=[anyspec] * n, scratch_shapes=_gather_scratch(n),
    )(*arrs)


def _dw_in_swap(a_parts, u, dw_out):
    tile, tk = 1024, 1024
    s = u.shape[0]
    nk = s // tk
    na = len(a_parts)
    offs, counts, ni = _col_blocks(a_parts, tile)

    def body(*refs):
        a_refs, u_ref, dwo_ref = refs[:na], refs[na], refs[na + 1]
        dw_ref, got_ref, goto_ref = refs[na + 2:na + 5]
        acc, local_sems, send_sems, recv_sem, o_send, o_recv = refs[na + 5:]
        i, k = pl.program_id(0), pl.program_id(1)
        x, y, c = _my_pos()
        par = i % 2

        def tile_copies(t, p):
            rows = pl.ds(pl.multiple_of(t * tile, tile), tile)
            loc = pltpu.make_async_copy(acc.at[p], dw_ref.at[rows], local_sems.at[p])
            rem = pltpu.make_async_remote_copy(
                src_ref=acc.at[p], dst_ref=got_ref.at[rows], send_sem=send_sems.at[p], recv_sem=recv_sem,
                device_id=(x, y, 1 - c), device_id_type=MESH)
            return loc, rem

        out_copy = pltpu.make_async_remote_copy(
            src_ref=dwo_ref, dst_ref=goto_ref, send_sem=o_send, recv_sem=o_recv,
            device_id=(x, y, 1 - c), device_id_type=MESH)

        @pl.when(jnp.logical_and(i == 0, k == 0))
        def _():
            out_copy.start()

        @pl.when(k == 0)
        def _():
            @pl.when(i >= 2)
            def _():
                loc, rem = tile_copies(i - 2, par)
                loc.wait()
                rem.wait_send()
            acc[par] = jnp.zeros((tile, tile), f32)

        for t in range(na):
            @pl.when(jnp.logical_and(i >= offs[t], i < offs[t] + counts[t]))
            def _(t=t):
                acc[par] += _tn(a_refs[t][...], u_ref[...])

        @pl.when(k == nk - 1)
        def _():
            loc, rem = tile_copies(i, par)
            loc.start()
            rem.start()

        @pl.when(jnp.logical_and(i == ni - 1, k == nk - 1))
        def _():
            for t in (ni - 2, ni - 1):
                loc, rem = tile_copies(t, t % 2)
                loc.wait()
                rem.wait_send()
            pltpu.make_async_remote_copy(src_ref=dw_ref, dst_ref=got_ref, send_sem=send_sems.at[0], recv_sem=recv_sem,
                                         device_id=(x, y, c), device_id_type=MESH).wait_recv()
            out_copy.wait_send()
            out_copy.wait_recv()

    def a_spec(t):
        def index(i, k):
            mine = jnp.logical_and(i >= offs[t], i < offs[t] + counts[t])
            return jnp.where(mine, k, 0), jnp.clip(i - offs[t], 0, counts[t] - 1)
        return pl.BlockSpec((tk, tile), index)

    anyspec = pl.BlockSpec(memory_space=pl.ANY)
    return pl.pallas_call(
        body, name="dw_in_swap", grid=(ni, nk),
        in_specs=[a_spec(t) for t in range(na)] + [pl.BlockSpec((tk, tile), lambda i, k: (k, 0)), anyspec],
        out_specs=[anyspec] * 3,
        out_shape=[SDS((ni * tile, tile), f32), SDS((ni * tile, tile), f32), SDS(dw_out.shape, dw_out.dtype)],
        scratch_shapes=[pltpu.VMEM((2, tile, tile), f32), pltpu.SemaphoreType.DMA((2,)), pltpu.SemaphoreType.DMA((2,)),
                        pltpu.SemaphoreType.DMA(()), pltpu.SemaphoreType.DMA(()), pltpu.SemaphoreType.DMA(())],
        compiler_params=pltpu.CompilerParams(dimension_semantics=("arbitrary", "arbitrary")),
    )(*a_parts, u, dw_out)


def _chip_sum(mine, got, rows, name):
    r, cdim = mine.shape
    tc = LANES

    def body(m_ref, g_ref, s16_ref):
        c = lax.axis_index("c")
        for q in range(4):
            blk = pl.ds(rows * (2 * q + c), rows)
            s16_ref[q] = (m_ref[blk, :] + g_ref[blk, :]).astype(bf16)

    col = pl.BlockSpec((r, tc), lambda i: (0, i))
    return pl.pallas_call(
        body, name=name, grid=(cdim // tc,), in_specs=[col, col],
        out_specs=pl.BlockSpec((4, rows, tc), lambda i: (0, 0, i)), out_shape=SDS((4, rows, cdim), bf16),
        compiler_params=pltpu.CompilerParams(dimension_semantics=("parallel",)),
    )(mine, got)


def _assemble_wt(shards):
    nd, rows, cdim = shards.shape
    tc = 256

    def body(g_ref, o_ref):
        for j in range(nd):
            o_ref[pl.ds(rows * j, rows), :] = g_ref[j]
        o_ref[pl.ds(nd * rows, NP - nd * rows), :] = jnp.zeros((NP - nd * rows, tc), shards.dtype)

    return pl.pallas_call(
        body, name="assemble_w_in", grid=(cdim // tc,),
        in_specs=[pl.BlockSpec((nd, rows, tc), lambda i: (0, 0, i))],
        out_specs=pl.BlockSpec((NP, tc), lambda i: (0, i)), out_shape=SDS((NP, cdim), shards.dtype),
        compiler_params=pltpu.CompilerParams(dimension_semantics=("parallel",)),
    )(shards)


def _chip_exchange_copies(ins, outs, send_sems, recv_sems, local_sems):
    nb = len(ins)
    x, y, c = _my_pos()
    my_q = 2 * x + y
    mine = [pltpu.make_async_copy(ins[a].at[my_q], outs[a].at[my_q], local_sems.at[a]) for a in range(nb)]
    sends, recvs = [], []
    for k in range(1, 4):
        to, frm = (my_q + k) % 4, (my_q + 4 - k) % 4
        for a in range(nb):
            sems = dict(send_sem=send_sems.at[3 * a + k - 1], recv_sem=recv_sems.at[3 * a + k - 1], device_id_type=MESH)
            sends.append(pltpu.make_async_remote_copy(
                src_ref=ins[a].at[to], dst_ref=outs[a].at[my_q], device_id=(to // 2, to % 2, c), **sems))
            recvs.append(pltpu.make_async_remote_copy(
                src_ref=ins[a].at[frm], dst_ref=outs[a].at[frm], device_id=(x, y, c), **sems))
    return mine, sends, recvs


def _chip_exchange_scratch(nb):
    return [pltpu.SemaphoreType.DMA((3 * nb,)), pltpu.SemaphoreType.DMA((3 * nb,)), pltpu.SemaphoreType.DMA((nb,))]


def _prenorm_inproj(x, nw, wt, gather=()):
    s, d = x.shape
    npad = wt.shape[0]
    tm, tn = 1024, 1024
    ng = len(gather)
    ni, nj = s // tm, npad // tn

    def body(x_ref, nw_ref, w_ref, *refs):
        g_in, (proj_ref, u_ref), g_out, sems = refs[:ng], refs[ng:ng + 2], refs[ng + 2:2 * ng + 2], refs[2 * ng + 2:]
        i, j = pl.program_id(0), pl.program_id(1)
        if ng:
            phases = _gather_phases(g_in, g_out, *sems)
            for step, phase in enumerate(phases[:3]):
                @pl.when(jnp.logical_and(i == step, j == 0))
                def _(phase=phase):
                    phase()

        @pl.when(j == 0)
        def _():
            xv = x_ref[...]
            r = lax.rsqrt(jnp.mean(xv * xv, axis=-1, keepdims=True) + EPS)
            u_ref[...] = (xv * r * nw_ref[...]).astype(bf16)
        proj_ref[...] = _nt(u_ref[...], w_ref[...])

        if ng:
            @pl.when(jnp.logical_and(i == ni - 1, j == nj - 1))
            def _():
                phases[3]()

    anyspec = pl.BlockSpec(memory_space=pl.ANY)
    outs = pl.pallas_call(
        body, name="prenorm_inproj", grid=(ni, nj),
        in_specs=[pl.BlockSpec((tm, d), lambda i, j: (i, 0)), pl.BlockSpec((1, d), lambda i, j: (0, 0)),
                  pl.BlockSpec((tn, d), lambda i, j: (j, 0))] + [anyspec] * ng,
        out_specs=[pl.BlockSpec((tm, tn), lambda i, j: (i, j)), pl.BlockSpec((tm, d), lambda i, j: (i, 0))]
        + [anyspec] * ng,
        out_shape=[SDS((s, npad), f32), SDS((s, d), bf16)] + [SDS((N_DEV,) + a.shape, a.dtype) for a in gather],
        scratch_shapes=_gather_scratch(ng) if ng else [],
        compiler_params=pltpu.CompilerParams(dimension_semantics=("arbitrary", "arbitrary")),
    )(x, nw, wt, *gather)
    return outs[0], outs[1], outs[2:]


def _attn_consts():
    head0 = _iota((BLK, LANES), 1) < HEAD_DIM
    tri2 = (_iota((BLK, 2 * LANES), 1) % LANES) <= _iota((BLK, 2 * LANES), 0)
    ones2 = ((_iota((LANES, 2 * LANES), 0) < HEAD_DIM) == (_iota((LANES, 2 * LANES), 1) < LANES)).astype(bf16)
    rmat = ((_iota((2 * LANES, LANES), 0) < LANES) == (_iota((2 * LANES, LANES), 1) < HEAD_DIM)).astype(bf16)
    bones = ((_iota((LANES, LANES), 0) < HEAD_DIM) == (_iota((LANES, LANES), 1) < HEAD_DIM)).astype(bf16)
    return head0, tri2, ones2, rmat, bones


def _stack_heads(x16, head0):
    zero = jnp.zeros_like(x16)
    return jnp.concatenate([jnp.where(head0, x16, zero), jnp.where(head0, zero, x16)], axis=0)


def _bf16_terms(x, terms):
    out = []
    for _ in range(terms):
        t = x.astype(bf16)
        out.append(t)
        x = x - t.astype(f32)
    return out


def _dot_01(x, w16, terms):
    return _nn(jnp.concatenate(_bf16_terms(x, terms), axis=1), jnp.concatenate([w16] * terms, axis=0))


def _split_dot(x, w16):
    return _dot_01(x, w16, 2)


def _split_dot_sum(x, w16):
    hi, lo = _bf16_terms(x, 2)
    return _nn(hi, w16) + _nn(lo, w16)


def _dot_01_left(w16, x, terms):
    return _nn(jnp.concatenate([w16] * terms, axis=1), jnp.concatenate(_bf16_terms(x, terms), axis=0))


def _attn_fwd(proj):
    s = proj.shape[0]
    n_it = s // BLK

    def body(q_ref, k_ref, v_ref, g_ref, o_ref, l_ref, mix_ref, op0, op1, op2, lp0, lp1, lp2,
             s_a, s_b, sd_a, sd_b, p_a, p_b, m_a, m_b, pd_a, pd_b, k_a, k_b, v_a, v_b):
        op_refs, lp_refs = (op0, op1, op2), (lp0, lp1, lp2)
        head0, tri2, ones2, rmat, _ = _attn_consts()
        score_bufs, prob_bufs = ((s_a, sd_a), (s_b, sd_b)), ((p_a, m_a, pd_a), (p_b, m_b, pd_b))
        k_bufs, v_bufs = (k_a, k_b), (v_a, v_b)
        for buf in k_bufs + v_bufs:
            buf[...] = jnp.zeros_like(buf)

        def block_rows(i, d, nb):
            r, blk = i // nb, i % nb
            return pl.ds(blk * (BLK * d) + r, BLK, stride=d), blk > 0

        def unstack(st16):
            return st16[:BLK] + st16[BLK:]

        def scores(i, par, d, nb):
            rows, has_prev = block_rows(i, d, nb)
            s_buf, sd_buf = score_bufs[par]
            qs = q_ref[rows, :] * 0.125
            qs16 = qs.astype(bf16)
            kst_c = _stack_heads(k_ref[rows, :].astype(bf16), head0)
            kst_p = k_bufs[1 - par][...]
            k_bufs[par][...] = kst_c
            sc = _nt(qs16, kst_c)
            sp = _nt(qs16, kst_p)
            s_buf[...] = jnp.where(tri2, sc, jnp.where(has_prev, sp, -jnp.inf))
            sd = _split_dot(qs * unstack(kst_p).astype(f32), ones2)
            sd_buf[...] = jnp.where(has_prev, sd, -jnp.inf)

        def softmax(bufs_in, bufs_out):
            s_buf, sd_buf = bufs_in
            p_buf, m_buf, pd_buf = bufs_out
            sc, sd2 = s_buf[...], sd_buf[...]
            m0 = jnp.max(sc[:, :LANES], axis=1, keepdims=True)
            m1 = jnp.max(sc[:, LANES:], axis=1, keepdims=True)
            m2 = jnp.concatenate([jnp.broadcast_to(m0, (BLK, LANES)), jnp.broadcast_to(m1, (BLK, LANES))], axis=1)
            m2 = jnp.maximum(m2, sd2)
            p_buf[...] = jnp.exp(sc - m2).astype(bf16)
            m_pair = jnp.where(head0, m2[:, :LANES], m2[:, LANES:])
            m_buf[...] = m_pair
            pd_buf[...] = jnp.exp(jnp.where(head0, sd2[:, :LANES], sd2[:, LANES:]) - m_pair)

        def output(i, par, d, nb, p):
            rows, _ = block_rows(i, d, nb)
            p_buf, m_buf, pd_buf = prob_bufs[par]
            vst_c = _stack_heads(v_ref[rows, :].astype(bf16), head0)
            vst_p = v_bufs[1 - par][...]
            v_bufs[par][...] = vst_c
            pt16, pd = p_buf[...], pd_buf[...]
            zero = jnp.zeros_like(pt16)
            o = (_nn(jnp.where(tri2, pt16, zero), vst_c) + _nn(jnp.where(tri2, zero, pt16), vst_p)
                 + pd * unstack(vst_p).astype(f32))
            l = _nn(pt16, rmat) + pd
            op_refs[p][rows, :] = o / l
            lp_refs[p][rows, :] = m_buf[...] + jnp.log(l)

        for p, d in enumerate(DILATIONS):
            nb = s // (BLK * d)
            scores(0, 0, d, nb)
            scores(1, 1, d, nb)
            softmax(score_bufs[0], prob_bufs[0])

            def steps(j, carry, d=d, nb=nb, p=p):
                for par in range(2):
                    t = 2 * j + 2 + par
                    scores(t, par, d, nb)
                    output(t - 2, par, d, nb, p)
                    softmax(score_bufs[1 - par], prob_bufs[1 - par])
                return carry

            lax.fori_loop(0, (n_it - 2) // 2, steps, 0)
            output(n_it - 2, 0, d, nb, p)
            softmax(score_bufs[1], prob_bufs[1])
            output(n_it - 1, 1, d, nb, p)

        def merge(i, carry):
            rows = pl.ds(pl.multiple_of(i * 256, 256), 256)
            l0, l1, l2 = lp0[rows, :], lp1[rows, :], lp2[rows, :]
            m = jnp.maximum(jnp.maximum(l0, l1), l2)
            e0, e1, e2 = jnp.exp(l0 - m), jnp.exp(l1 - m), jnp.exp(l2 - m)
            z = e0 + e1 + e2
            o = (e0 * op0[rows, :] + e1 * op1[rows, :] + e2 * op2[rows, :]) / z
            o_ref[rows, :] = o
            l_ref[rows, :] = m + jnp.log(z)
            g = g_ref[rows, :]
            mix_ref[rows, :] = (o * (g * _sigmoid(g))).astype(bf16)
            return carry

        lax.fori_loop(0, s // 256, merge, 0)

    col = lambda base: pl.BlockSpec((s, LANES), lambda h: (0, base + h))
    return pl.pallas_call(
        body, name="attn_fwd", grid=(N_PAIRS,),
        in_specs=[col(0), col(8), col(16), col(24)],
        out_specs=[col(0), col(0), col(0)],
        out_shape=[SDS((s, D_ATTN), f32), SDS((s, D_ATTN), f32), SDS((s, D_ATTN), bf16)],
        scratch_shapes=[pltpu.VMEM((s, LANES), f32)] * 6 + [pltpu.VMEM((BLK, 2 * LANES), f32)] * 4
        + [pltpu.VMEM((BLK, 2 * LANES), bf16)] * 2 + [pltpu.VMEM((BLK, LANES), f32)] * 4
        + [pltpu.VMEM((2 * BLK, LANES), bf16)] * 4,
        compiler_params=pltpu.CompilerParams(dimension_semantics=("parallel",)),
    )(proj, proj, proj, proj)


def _expand_mat():
    colv = _iota((LANES, 2 * D_SSM), 1)
    head = 2 * ((colv % D_SSM) // LANES) + colv // D_SSM
    return (_iota((LANES, 2 * D_SSM), 0) == head).astype(bf16)


def _fold_mat():
    return (_iota((D_SSM, LANES), 0) // HEAD_DIM == _iota((D_SSM, LANES), 1)).astype(bf16)


def _ssd_common(xs_ref, bc_ref, xs_tail, bc_tail, dt_ref, cw_ref, cb_ref, dtb_ref, alog16_ref, emat_ref, xpad, first):
    keep = jnp.where(first, 0.0, 1.0)
    xpad[0:8, 0:D_SSM] = xs_tail[...] * keep
    xpad[0:8, D_SSM:D_CONV] = bc_tail[...] * keep
    xpad[8:8 + CHUNK, 0:D_SSM] = xs_ref[...]
    xpad[8:8 + CHUNK, D_SSM:D_CONV] = bc_ref[...]
    xp = xpad[...]
    taps = [pltpu.roll(xp, 3 - j, 0)[8:8 + CHUNK] for j in range(3)] + [xp[8:8 + CHUNK]]
    cv = cb_ref[...] + cw_ref[0:1, :] * taps[0]
    for j in range(1, 4):
        cv = cv + cw_ref[j:j + 1, :] * taps[j]
    sig = _sigmoid(cv)
    xbc = cv * sig

    pre = dt_ref[...] + dtb_ref[...]
    dt16 = _softplus(pre)
    a16 = -jnp.exp(alog16_ref[...])
    sub, lane = _iota((CHUNK, CHUNK), 0), _iota((CHUNK, CHUNK), 1)
    tri = (sub >= lane).astype(f32)
    al16 = _nn_hi(tri, dt16 * a16)
    al_t = al16.T
    emat = emat_ref[...]
    dt_x = _dot_01(dt16, emat, 3)
    al_x = _dot_01(al16, emat, 3)
    lane_w = _iota((CHUNK, D_SSM), 1)
    even = (lane_w % LANES) < HEAD_DIM
    dt_f = jnp.where(even, dt_x[:, :D_SSM], dt_x[:, D_SSM:])
    al_f = jnp.where(even, al_x[:, :D_SSM], al_x[:, D_SSM:])
    return cv, sig, xbc, pre, dt_f, al_f, al_x, al_t, taps


def _decay_mat(al_x, al_t, pair, h):
    sub, lane = _iota((CHUNK, CHUNK), 0), _iota((CHUNK, CHUNK), 1)
    col = al_x[:, h * D_SSM + pair * LANES: h * D_SSM + (pair + 1) * LANES]
    row = al_t[2 * pair + h: 2 * pair + h + 1, :]
    return jnp.exp(jnp.where(sub >= lane, col - row, -jnp.inf))


def _ssd_in_specs(order):
    blk = lambda w, cb: pl.BlockSpec((CHUNK, w), lambda i: (order(i), cb))
    tail = lambda w, cb: pl.BlockSpec((8, w), lambda i: (jnp.maximum(16 * order(i) - 1, 0), cb))
    return [blk(D_SSM, COL_XS // D_SSM), blk(512, COL_BC // 512), tail(D_SSM, COL_XS // D_SSM),
            tail(512, COL_BC // 512), blk(LANES, COL_DT // LANES), blk(D_SSM, COL_Z // D_SSM)]


def _full(shape):
    return pl.BlockSpec(shape, lambda i: (0,) * len(shape))


def _ssd_fwd(proj, conv_w, conv_b, dtb16, alog16, alog_f, d_f, nw):
    s = proj.shape[0]
    nc = s // CHUNK

    def body(xs_ref, bc_ref, xs_tail, bc_tail, dt_ref, z_ref, cw_ref, cb_ref, dtb_ref, alog16_ref, alogf_ref,
             df_ref, nw_ref, mix_ref, y_ref, st_ref, h_scr, xpad, y_scr, emat_ref):
        c = pl.program_id(0)

        @pl.when(c == 0)
        def _():
            h_scr[...] = jnp.zeros_like(h_scr)
            emat_ref[...] = _expand_mat()

        _, _, xbc, _, dt_f, al_f, al_x, al_t, _ = _ssd_common(
            xs_ref, bc_ref, xs_tail, bc_tail, dt_ref, cw_ref, cb_ref, dtb_ref, alog16_ref, emat_ref, xpad, c == 0)
        head0 = _iota((CHUNK, LANES), 1) < HEAD_DIM
        st_ref[...] = h_scr[...]
        for g in range(N_GROUPS):
            bm = xbc[:, D_SSM + g * D_STATE: D_SSM + (g + 1) * D_STATE].astype(bf16)
            cm = xbc[:, D_SSM + (N_GROUPS + g) * D_STATE: D_SSM + (N_GROUPS + g + 1) * D_STATE].astype(bf16)
            gmat = _nt(cm, bm)
            for pair in range(4 * g, 4 * g + 4):
                sl = slice(pair * LANES, (pair + 1) * LANES)
                xp, dtp, alp = xbc[:, sl], dt_f[:, sl], al_f[:, sl]
                xdt = xp * dtp
                xdt16 = xdt.astype(bf16)
                al_last = alp[CHUNK - 1:CHUNK, :]
                hp = h_scr[:, sl]
                y_off = jnp.exp(alp) * _nn(cm, hp.astype(bf16))
                yd = [_nn((gmat * _decay_mat(al_x, al_t, pair, h)).astype(bf16), xdt16) for h in range(2)]
                y_scr[:, sl] = jnp.where(head0, yd[0], yd[1]) + y_off + df_ref[:, sl] * xp
                st = _tn(bm, (jnp.exp(al_last - alp) * xdt).astype(bf16))
                h_scr[:, sl] = jnp.exp(al_last) * hp + st
        y = y_scr[...]
        y_ref[...] = y
        z = z_ref[...]
        yz = y * (z * _sigmoid(z))
        gw = D_SSM // N_GROUPS
        for g in range(N_GROUPS):
            part = yz[:, g * gw:(g + 1) * gw]
            r = lax.rsqrt(jnp.mean(part * part, axis=-1, keepdims=True) + EPS)
            mix_ref[:, g * gw:(g + 1) * gw] = (part * r * nw_ref[:, g * gw:(g + 1) * gw]).astype(bf16)

    order = lambda i: i
    row = lambda w: pl.BlockSpec((CHUNK, w), lambda i: (i, 0))
    return pl.pallas_call(
        body, name="ssd_fwd", grid=(nc,),
        in_specs=_ssd_in_specs(order) + [_full((4, D_CONV)), _full((1, D_CONV)), _full((1, LANES)), _full((1, LANES)),
                                         _full((1, D_SSM)), _full((1, D_SSM)), _full((1, D_SSM))],
        out_specs=[row(D_SSM), row(D_SSM), pl.BlockSpec((None, D_STATE, D_SSM), lambda i: (i, 0, 0))],
        out_shape=[SDS((s, D_SSM), bf16), SDS((s, D_SSM), f32), SDS((nc, D_STATE, D_SSM), f32)],
        scratch_shapes=[pltpu.VMEM((D_STATE, D_SSM), f32), pltpu.VMEM((8 + CHUNK, D_CONV), f32),
                        pltpu.VMEM((CHUNK, D_SSM), f32), pltpu.VMEM((LANES, 2 * D_SSM), bf16)],
        compiler_params=pltpu.CompilerParams(dimension_semantics=("arbitrary",)),
    )(proj, proj, proj, proj, proj, proj, conv_w, conv_b, dtb16, alog16, alog_f, d_f, nw)


def _outproj_loss(mix_a, mix_s, wo, x, tgt, npw):
    s, d = x.shape
    tm = 512

    def body(ma_ref, ms_ref, wo_ref, x_ref, t_ref, npw_ref, dmix_ref, dout_ref, dres_ref, acc_ref):
        @pl.when(pl.program_id(0) == 0)
        def _():
            acc_ref[...] = jnp.zeros_like(acc_ref)

        out = _nn(ma_ref[...], wo_ref[0:D_ATTN, :]) + _nn(ms_ref[...], wo_ref[D_ATTN:, :])
        r = lax.rsqrt(jnp.mean(out * out, axis=-1, keepdims=True) + EPS)
        on = out * r
        diff = x_ref[...] + on * npw_ref[...] - t_ref[...]
        dres = diff * (1.0 / d)
        dres_ref[...] = dres
        acc_ref[0:1, :] += jnp.sum(diff * diff, axis=0, keepdims=True)
        acc_ref[1:2, :] += jnp.sum(dres * on, axis=0, keepdims=True)
        dn = dres * npw_ref[...]
        dout = (r * (dn - on * jnp.mean(dn * on, axis=-1, keepdims=True))).astype(bf16)
        dout_ref[...] = dout
        dmix_ref[...] = _nt(dout, wo_ref[...])

    row = lambda w: pl.BlockSpec((tm, w), lambda i: (i, 0))
    return pl.pallas_call(
        body, name="outproj_loss", grid=(s // tm,),
        in_specs=[row(D_ATTN), row(D_SSM), _full((D_ATTN + D_SSM, d)), row(d), row(d), _full((1, d))],
        out_specs=[row(D_ATTN + D_SSM), row(d), row(d), _full((8, d))],
        out_shape=[SDS((s, D_ATTN + D_SSM), f32), SDS((s, d), bf16), SDS((s, d), f32), SDS((8, d), f32)],
        compiler_params=pltpu.CompilerParams(dimension_semantics=("arbitrary",)),
    )(mix_a, mix_s, wo, x, tgt, npw)


def _attn_bwd(proj, o, lb, dmix):
    s = proj.shape[0]
    n_it = s // BLK

    def body(q_ref, k_ref, v_ref, g_ref, o_ref, l_ref, dm_ref, dq_ref, dk_ref, dv_ref, dg_ref,
             dq_acc, dk_acc, dv_acc, do_scr, dl_scr, *bufs):
        head0, tri2, _, _, bones = _attn_consts()

        def pro(i, carry):
            rows = pl.ds(pl.multiple_of(i * 256, 256), 256)
            g = g_ref[rows, :]
            sg = _sigmoid(g)
            dmx = dm_ref[rows, :]
            ov = o_ref[rows, :]
            dg_ref[rows, :] = (dmx * ov * (sg * (1.0 + g * (1.0 - sg)))).astype(bf16)
            do = dmx * (g * sg)
            do_scr[rows, :] = do
            dl_scr[rows, :] = _split_dot_sum(do * ov, bones)
            z = jnp.zeros((256, LANES), f32)
            dq_acc[rows, :] = z
            dk_acc[rows, :] = z
            dv_acc[rows, :] = z
            return carry

        lax.fori_loop(0, s // 256, pro, 0)

        def per_head(t):
            return jnp.concatenate([t[:, :LANES], t[:, LANES:]], axis=0)

        def both_heads(t):
            tr = pltpu.roll(t, HEAD_DIM, 1)
            return jnp.concatenate([jnp.where(head0, t, tr), jnp.where(head0, tr, t)], axis=1)

        mm_bufs = ((bufs[0], bufs[1], bufs[2], bufs[3]), (bufs[4], bufs[5], bufs[6], bufs[7]))
        ds_bufs = ((bufs[8], bufs[9], bufs[10], bufs[11]), (bufs[12], bufs[13], bufs[14], bufs[15]))
        op_bufs = ((bufs[16], bufs[17], bufs[18], bufs[19]), (bufs[20], bufs[21], bufs[22], bufs[23]))
        vc_bufs, carry_k, carry_v = (bufs[24], bufs[25]), bufs[26], bufs[27]
        for buf in (op_bufs[0][0], op_bufs[1][0]) + vc_bufs:
            buf[...] = jnp.zeros_like(buf)

        def block_rows(i, d, nb):
            r, blk = i // nb, i % nb
            return pl.ds(blk * (BLK * d) + r, BLK, stride=d), blk > 0

        def unstack(st16):
            return st16[:BLK] + st16[BLK:]

        def products(i, par, d, nb):
            rows, has_prev = block_rows(i, d, nb)
            s_buf, dp_buf, sd_buf, dpd_buf = mm_bufs[par]
            kc_buf, kp_buf, q_buf, do_buf = op_bufs[par]
            q = q_ref[rows, :]
            qs = q * 0.125
            do = do_scr[rows, :]
            qs16, do16 = qs.astype(bf16), do.astype(bf16)
            kst_c = _stack_heads(k_ref[rows, :].astype(bf16), head0)
            vst_c = _stack_heads(v_ref[rows, :].astype(bf16), head0)
            kst_p, vst_p = op_bufs[1 - par][0][...], vc_bufs[1 - par][...]
            kc_buf[...] = kst_c
            kp_buf[...] = kst_p
            vc_bufs[par][...] = vst_c
            q_buf[...] = q.astype(bf16)
            do_buf[...] = do16
            s_buf[...] = jnp.where(tri2, _nt(qs16, kst_c), jnp.where(has_prev, _nt(qs16, kst_p), -jnp.inf))
            dp_buf[...] = jnp.where(tri2, _nt(do16, vst_c), jnp.where(has_prev, _nt(do16, vst_p), 0.0))
            sd_buf[...] = _split_dot_sum(qs * unstack(kst_p).astype(f32), bones)
            dpd_buf[...] = jnp.where(has_prev, _split_dot_sum(do * unstack(vst_p).astype(f32), bones), 0.0)

        def softmax_grad(i, par, d, nb):
            rows, has_prev = block_rows(i, d, nb)
            s_buf, dp_buf, sd_buf, dpd_buf = mm_bufs[par]
            p_buf, ds_buf, pd_buf, dsd_buf = ds_bufs[par]
            lse = l_ref[rows, :]
            dl = dl_scr[rows, :]
            pt = jnp.exp(s_buf[...] - both_heads(lse))
            ds_buf[...] = (pt * (dp_buf[...] - both_heads(dl)) * 0.125).astype(bf16)
            p_buf[...] = pt.astype(bf16)
            pd = jnp.where(has_prev, jnp.exp(sd_buf[...] - lse), 0.0)
            pd_buf[...] = pd
            dsd_buf[...] = pd * (dpd_buf[...] - dl) * 0.125

        def accumulate(i, par, d, nb):
            rows, _ = block_rows(i, d, nb)
            before, _ = block_rows(jnp.maximum(i - 1, 0), d, nb)
            p_buf, ds_buf, pd_buf, dsd_buf = ds_bufs[par]
            kc_buf, kp_buf, q_buf, do_buf = op_bufs[par]
            pt16, ds16, pd, dsd = p_buf[...], ds_buf[...], pd_buf[...], dsd_buf[...]
            zero = jnp.zeros_like(pt16)
            dsc, dsp = jnp.where(tri2, ds16, zero), jnp.where(tri2, zero, ds16)
            pc, pp = jnp.where(tri2, pt16, zero), jnp.where(tri2, zero, pt16)
            kst_c, kst_p, q16, do16 = kc_buf[...], kp_buf[...], q_buf[...], do_buf[...]
            qst, dost = _stack_heads(q16, head0), _stack_heads(do16, head0)
            dq_acc[rows, :] += _nn(dsc, kst_c) + _nn(dsp, kst_p) + dsd * unstack(kst_p).astype(f32)
            dk2 = _tn(jnp.concatenate([per_head(dsc), per_head(dsp)], axis=1), qst)
            dv2 = _tn(jnp.concatenate([per_head(pc), per_head(pp)], axis=1), dost)
            dk_acc[before, :] += carry_k[...] + dk2[BLK:] + dsd * q16.astype(f32)
            dv_acc[before, :] += carry_v[...] + dv2[BLK:] + pd * do16.astype(f32)
            carry_k[...] = dk2[:BLK]
            carry_v[...] = dv2[:BLK]

        for d in DILATIONS:
            nb = s // (BLK * d)
            carry_k[...] = jnp.zeros_like(carry_k)
            carry_v[...] = jnp.zeros_like(carry_v)
            products(0, 0, d, nb)
            products(1, 1, d, nb)
            softmax_grad(0, 0, d, nb)

            def steps(j, carry, d=d, nb=nb):
                for par in range(2):
                    t = 2 * j + 2 + par
                    accumulate(t - 2, par, d, nb)
                    products(t, par, d, nb)
                    softmax_grad(t - 1, 1 - par, d, nb)
                return carry

            lax.fori_loop(0, (n_it - 2) // 2, steps, 0)
            accumulate(n_it - 2, 0, d, nb)
            softmax_grad(n_it - 1, 1, d, nb)
            accumulate(n_it - 1, 1, d, nb)
            last, _ = block_rows(n_it - 1, d, nb)
            dk_acc[last, :] += carry_k[...]
            dv_acc[last, :] += carry_v[...]

        def epi(i, carry):
            rows = pl.ds(pl.multiple_of(i * 256, 256), 256)
            dq_ref[rows, :] = dq_acc[rows, :].astype(bf16)
            dk_ref[rows, :] = dk_acc[rows, :].astype(bf16)
            dv_ref[rows, :] = dv_acc[rows, :].astype(bf16)
            return carry

        lax.fori_loop(0, s // 256, epi, 0)

    col = lambda base: pl.BlockSpec((s, LANES), lambda h: (0, base + h))
    outs = pl.pallas_call(
        body, name="attn_bwd", grid=(N_PAIRS,),
        in_specs=[col(0), col(8), col(16), col(24), col(0), col(0), col(0)],
        out_specs=[col(0)] * 4,
        out_shape=[SDS((s, D_ATTN), bf16)] * 4,
        scratch_shapes=[pltpu.VMEM((s, LANES), f32)] * 5
        + [pltpu.VMEM((BLK, 2 * LANES), f32)] * 2 + [pltpu.VMEM((BLK, LANES), f32)] * 2
        + [pltpu.VMEM((BLK, 2 * LANES), f32)] * 2 + [pltpu.VMEM((BLK, LANES), f32)] * 2
        + [pltpu.VMEM((BLK, 2 * LANES), bf16)] * 2 + [pltpu.VMEM((BLK, LANES), f32)] * 2
        + [pltpu.VMEM((BLK, 2 * LANES), bf16)] * 2 + [pltpu.VMEM((BLK, LANES), f32)] * 2
        + [pltpu.VMEM((2 * BLK, LANES), bf16)] * 2 + [pltpu.VMEM((BLK, LANES), bf16)] * 2
        + [pltpu.VMEM((2 * BLK, LANES), bf16)] * 2 + [pltpu.VMEM((BLK, LANES), bf16)] * 2
        + [pltpu.VMEM((2 * BLK, LANES), bf16)] * 2 + [pltpu.VMEM((BLK, LANES), f32)] * 2,
        compiler_params=pltpu.CompilerParams(dimension_semantics=("parallel",)),
    )(proj, proj, proj, proj, o, lb, dmix)
    return outs


def _ssd_bwd(proj, y, states, dmix, conv_w, conv_b, dtb16, alog16, alog_f, d_f, nw):
    s = proj.shape[0]
    nc = s // CHUNK
    gw = D_SSM // N_GROUPS

    def body(xs_ref, bc_ref, xs_tail, bc_tail, dt_ref, z_ref, y_ref, st_ref, dm_ref, cw_ref, cb_ref, dtb_ref,
             alog16_ref, alogf_ref, df_ref, nw_ref, out_ref, gconv_ref, gvec_ref, gdt_ref,
             dh_scr, head_scr, xpad, dcpad, da_scr, dxdt_scr, dbc_scr, emat_ref, fold_ref):
        i = pl.program_id(0)
        c = nc - 1 - i

        @pl.when(i == 0)
        def _():
            emat_ref[...] = _expand_mat()
            fold_ref[...] = _fold_mat()
            dh_scr[...] = jnp.zeros_like(dh_scr)
            head_scr[...] = jnp.zeros_like(head_scr)
            gconv_ref[...] = jnp.zeros_like(gconv_ref)
            gvec_ref[...] = jnp.zeros_like(gvec_ref)
            gdt_ref[...] = jnp.zeros_like(gdt_ref)

        cv, sig, xbc, pre, dt_f, al_f, al_x, al_t, taps = _ssd_common(
            xs_ref, bc_ref, xs_tail, bc_tail, dt_ref, cw_ref, cb_ref, dtb_ref, alog16_ref, emat_ref, xpad, c == 0)
        head0 = _iota((CHUNK, LANES), 1) < HEAD_DIM
        sub = _iota((CHUNK, LANES), 0)
        last_row = sub == CHUNK - 1

        yv, z, dmx = y_ref[...], z_ref[...], dm_ref[...]
        sz = _sigmoid(z)
        silu = z * sz
        yz = yv * silu
        dyz_parts = []
        for g in range(N_GROUPS):
            gs = slice(g * gw, (g + 1) * gw)
            part = yz[:, gs]
            r = lax.rsqrt(jnp.mean(part * part, axis=-1, keepdims=True) + EPS)
            nh = part * r
            gvec_ref[0:1, gs] += jnp.sum(dmx[:, gs] * nh, axis=0, keepdims=True)
            dn = dmx[:, gs] * nw_ref[:, gs]
            dyz_parts.append(r * (dn - nh * jnp.mean(dn * nh, axis=-1, keepdims=True)))
        dyz = jnp.concatenate(dyz_parts, axis=1)
        dy = dyz * silu
        out_ref[:, 0:D_SSM] = (dyz * yv * (sz * (1.0 + z * (1.0 - sz)))).astype(bf16)

        x_all = xbc[:, 0:D_SSM]
        gvec_ref[2:3, :] += jnp.sum(dy * x_all, axis=0, keepdims=True)

        for g in range(N_GROUPS):
            bm = xbc[:, D_SSM + g * D_STATE: D_SSM + (g + 1) * D_STATE].astype(bf16)
            cm = xbc[:, D_SSM + (N_GROUPS + g) * D_STATE: D_SSM + (N_GROUPS + g + 1) * D_STATE].astype(bf16)
            gmat = _nt(cm, bm)
            dgm = jnp.zeros((CHUNK, CHUNK), f32)
            db = jnp.zeros((CHUNK, D_STATE), f32)
            dc = jnp.zeros((CHUNK, D_STATE), f32)
            for pair in range(4 * g, 4 * g + 4):
                sl = slice(pair * LANES, (pair + 1) * LANES)
                xp, dtp, alp, dyp = x_all[:, sl], dt_f[:, sl], al_f[:, sl], dy[:, sl]
                xdt = xp * dtp
                xdt16 = xdt.astype(bf16)
                al_last = alp[CHUNK - 1:CHUNK, :]
                e_l = jnp.exp(alp)
                wf = jnp.exp(al_last - alp)
                e_last = jnp.exp(al_last)
                hp = st_ref[:, sl]
                hp16 = hp.astype(bf16)
                dhn = dh_scr[:, sl]
                dhn16 = dhn.astype(bf16)
                y_off = e_l * _nn(cm, hp16)
                dch16 = (dyp * e_l).astype(bf16)
                dc = dc + _nt(dch16, hp16)
                dh_out = _tn(cm, dch16)
                dal = dyp * y_off
                xw16 = (wf * xdt).astype(bf16)
                db = db + _nt(xw16, dhn16)
                dxw = _nn(bm, dhn16)
                dxdt = dxw * wf
                dwf = dxw * xdt * wf
                dal = dal - dwf
                dal_last = jnp.sum(dwf, axis=0, keepdims=True) + jnp.sum(dhn * hp, axis=0, keepdims=True) * e_last
                dh_scr[:, sl] = e_last * dhn + dh_out
                for h in range(2):
                    mh = head0 if h == 0 else jnp.logical_not(head0)
                    dyh16 = jnp.where(mh, dyp, 0.0).astype(bf16)
                    lmat = _decay_mat(al_x, al_t, pair, h)
                    mm = gmat * lmat
                    dmm = _nt(dyh16, xdt16)
                    dxdt = dxdt + _tn(mm.astype(bf16), dyh16)
                    n16 = (dmm * mm).astype(bf16)
                    jh = jnp.where(mh, 1.0 / HEAD_DIM, 0.0).astype(bf16)
                    dal = dal + _nn(n16, jh) - _tn(n16, jh)
                    dgm = dgm + dmm * lmat
                da_scr[:, sl] = dal + jnp.where(last_row, dal_last, 0.0)
                dxdt_scr[:, sl] = dxdt
            dgm16 = dgm.astype(bf16)
            dbc_scr[:, g * D_STATE:(g + 1) * D_STATE] = db + _tn(dgm16, cm)
            dbc_scr[:, (N_GROUPS + g) * D_STATE:(N_GROUPS + g + 1) * D_STATE] = dc + _nn(dgm16, bm)

        sub_c, lane_c = _iota((CHUNK, CHUNK), 0), _iota((CHUNK, CHUNK), 1)
        tri_t = (lane_c >= sub_c).astype(bf16)
        dadt = _dot_01_left(tri_t, da_scr[...], 2)
        a_f = -jnp.exp(alogf_ref[...])
        dxdt_all = dxdt_scr[...]
        ddt_f = dxdt_all * x_all + a_f * dadt
        gvec_ref[1:2, :] += jnp.sum(dt_f * dadt, axis=0, keepdims=True) * a_f
        dx = df_ref[...] * dy + dxdt_all * dt_f
        ddt_raw = _dot_01(ddt_f, fold_ref[...], 2) * _sigmoid(pre)
        gdt_ref[0:1, :] += jnp.sum(ddt_raw, axis=0, keepdims=True)
        out_ref[:, D_SSM + D_CONV:D_SSM + D_CONV + LANES] = ddt_raw.astype(bf16)
        out_ref[:, D_SSM + D_CONV + LANES:] = jnp.zeros((CHUNK, 3 * LANES), bf16)

        dsil = sig * (1.0 + cv * (1.0 - sig))
        dcv_x = dx * dsil[:, 0:D_SSM]
        dcv_bc = dbc_scr[...] * dsil[:, D_SSM:]
        dcpad[0:CHUNK, 0:D_SSM] = dcv_x
        dcpad[0:CHUNK, D_SSM:] = dcv_bc
        dcpad[CHUNK:, :] = head_scr[...]
        dcp = dcpad[...]
        dcv = dcp[0:CHUNK]
        gconv_ref[4:5, :] += jnp.sum(dcv, axis=0, keepdims=True)
        draw = cw_ref[3:4, :] * dcv
        for j in range(4):
            gconv_ref[j:j + 1, :] += jnp.sum(dcv * taps[j], axis=0, keepdims=True)
        for j in range(3):
            draw = draw + cw_ref[j:j + 1, :] * pltpu.roll(dcp, CHUNK + 8 - (3 - j), 0)[0:CHUNK]
        head_scr[...] = dcv[0:8]
        out_ref[:, D_SSM:D_SSM + D_CONV] = draw.astype(bf16)

    order = lambda i: nc - 1 - i
    row = lambda w, cb=0: pl.BlockSpec((CHUNK, w), lambda i: (nc - 1 - i, cb))
    return pl.pallas_call(
        body, name="ssd_bwd", grid=(nc,),
        in_specs=_ssd_in_specs(order) + [row(D_SSM), pl.BlockSpec((None, D_STATE, D_SSM), lambda i: (nc - 1 - i, 0, 0)),
                                         row(D_SSM, 1), _full((4, D_CONV)), _full((1, D_CONV)), _full((1, LANES)),
                                         _full((1, LANES)), _full((1, D_SSM)), _full((1, D_SSM)), _full((1, D_SSM))],
        out_specs=[row(3072), _full((8, D_CONV)), _full((8, D_SSM)), _full((8, LANES))],
        out_shape=[SDS((s, 3072), bf16), SDS((8, D_CONV), f32), SDS((8, D_SSM), f32), SDS((8, LANES), f32)],
        scratch_shapes=[pltpu.VMEM((D_STATE, D_SSM), f32), pltpu.VMEM((8, D_CONV), f32),
                        pltpu.VMEM((8 + CHUNK, D_CONV), f32), pltpu.VMEM((8 + CHUNK, D_CONV), f32),
                        pltpu.VMEM((CHUNK, D_SSM), f32), pltpu.VMEM((CHUNK, D_SSM), f32),
                        pltpu.VMEM((CHUNK, 2 * N_GROUPS * D_STATE), f32),
                        pltpu.VMEM((LANES, 2 * D_SSM), bf16), pltpu.VMEM((D_SSM, LANES), bf16)],
        compiler_params=pltpu.CompilerParams(dimension_semantics=("arbitrary",)),
    )(proj, proj, proj, proj, proj, proj, y, states, dmix, conv_w, conv_b, dtb16, alog16, alog_f, d_f, nw)


def _col_blocks(parts, tile):
    counts = [p.shape[1] // tile for p in parts]
    offs = [sum(counts[:t]) for t in range(len(parts))]
    return offs, counts, sum(counts)


def _bcast_copies(src_ref, out_ref, send_sems, recv_sems, local_sem):
    x, y, c = _my_pos()
    me = 4 * x + 2 * y + c
    mine = pltpu.make_async_copy(src_ref, out_ref.at[me], local_sem)
    sends, recvs = [], []
    for k in range(1, N_DEV):
        to, frm = (me + k) % N_DEV, (me + N_DEV - k) % N_DEV
        sems = dict(send_sem=send_sems.at[k - 1], recv_sem=recv_sems.at[k - 1], device_id_type=MESH)
        sends.append(pltpu.make_async_remote_copy(
            src_ref=src_ref, dst_ref=out_ref.at[me], device_id=(to // 4, (to // 2) % 2, to % 2), **sems))
        recvs.append(pltpu.make_async_remote_copy(
            src_ref=src_ref, dst_ref=out_ref.at[frm], device_id=(x, y, c), **sems))
    return mine, sends, recvs


def _bcast_scratch():
    return [pltpu.SemaphoreType.DMA((N_DEV - 1,)), pltpu.SemaphoreType.DMA((N_DEV - 1,)), pltpu.SemaphoreType.DMA(())]


def _inproj_bwd(dparts, wt, x, nw, dres, chip_sums=(), pack=None):
    s, d = x.shape
    tm, tk = 1024, 1024
    offs, counts, nk = _col_blocks(dparts, tk)
    npart, nx = len(dparts), len(chip_sums)
    npk = 0 if pack is None else 1
    ni = s // tm

    def body(*refs):
        dp_refs = refs[:npart]
        w_ref, x_ref, nw_ref, dres_ref = refs[npart:npart + 4]
        pos = npart + 4
        cs_in, pos = refs[pos:pos + nx], pos + nx
        pack_in, pos = refs[pos:pos + npk], pos + npk
        (gx_ref, gnw_ref), pos = refs[pos:pos + 2], pos + 2
        cs_out, pos = refs[pos:pos + nx], pos + nx
        pack_out, pos = refs[pos:pos + 2 * npk], pos + 2 * npk
        acc, pos = refs[pos], pos + 1
        cs_sems, pos = refs[pos:pos + 3 * min(nx, 1)], pos + 3 * min(nx, 1)
        pk_refs = refs[pos:]
        i, k = pl.program_id(0), pl.program_id(1)

        def exchange():
            return _chip_exchange_copies(cs_in, cs_out, *cs_sems)

        def pack_copies():
            return _bcast_copies(pack_in[0], pack_out[0], *pk_refs[1:4])

        def gnw_copies():
            return _bcast_copies(pk_refs[0], pack_out[1], *pk_refs[4:7])

        @pl.when(jnp.logical_and(i == 0, k == 0))
        def _():
            gnw_ref[...] = jnp.zeros_like(gnw_ref)
            if nx:
                mine, sends, _ = exchange()
                for cp in mine + sends:
                    cp.start()
            if npk:
                mine, sends, _ = pack_copies()
                for cp in [mine] + sends:
                    cp.start()

        @pl.when(k == 0)
        def _():
            acc[...] = jnp.zeros_like(acc)

        for t in range(npart):
            @pl.when(jnp.logical_and(k >= offs[t], k < offs[t] + counts[t]))
            def _(t=t):
                acc[...] += _nn(dp_refs[t][...], w_ref[...])

        @pl.when(k == nk - 1)
        def _():
            xv = x_ref[...]
            r = lax.rsqrt(jnp.mean(xv * xv, axis=-1, keepdims=True) + EPS)
            xn = xv * r
            du = acc[...]
            gnw_ref[0:1, :] += jnp.sum(du * xn, axis=0, keepdims=True)
            dn = du * nw_ref[...]
            gx_ref[...] = dres_ref[...] + r * (dn - xn * jnp.mean(dn * xn, axis=-1, keepdims=True))

        @pl.when(jnp.logical_and(i == ni - 1, k == nk - 1))
        def _():
            if npk:
                pk_refs[0][...] = gnw_ref[...]
                mine, sends, _ = gnw_copies()
                for cp in [mine] + sends:
                    cp.start()
            if nx:
                mine, sends, recvs = exchange()
                for cp in recvs:
                    cp.wait_recv()
                for cp in sends:
                    cp.wait_send()
                for cp in mine:
                    cp.wait()
            if npk:
                for copies in (pack_copies(), gnw_copies()):
                    mine, sends, recvs = copies
                    for cp in recvs:
                        cp.wait_recv()
                    for cp in sends:
                        cp.wait_send()
                    mine.wait()

    def piece(t):
        return pl.BlockSpec((tm, tk), lambda i, k: (i, jnp.clip(k - offs[t], 0, counts[t] - 1)))

    anyspec = pl.BlockSpec(memory_space=pl.ANY)
    packs = [] if pack is None else [pack]
    pack_shapes = [] if pack is None else [SDS((N_DEV,) + pack.shape, f32), SDS((N_DEV, 8, d), f32)]
    scratch = [pltpu.VMEM((tm, d), f32)] + (_chip_exchange_scratch(nx) if nx else [])
    if npk:
        scratch += [pltpu.VMEM((8, d), f32)] + _bcast_scratch() + _bcast_scratch()
    outs = pl.pallas_call(
        body, name="inproj_bwd", grid=(ni, nk),
        in_specs=[piece(t) for t in range(npart)] + [
            pl.BlockSpec((tk, d), lambda i, k: (k, 0)),
            pl.BlockSpec((tm, d), lambda i, k: (i, 0)), pl.BlockSpec((1, d), lambda i, k: (0, 0)),
            pl.BlockSpec((tm, d), lambda i, k: (i, 0))] + [anyspec] * (nx + npk),
        out_specs=[pl.BlockSpec((tm, d), lambda i, k: (i, 0)), pl.BlockSpec((8, d), lambda i, k: (0, 0))]
        + [anyspec] * (nx + 2 * npk),
        out_shape=[SDS((s, d), f32), SDS((8, d), f32)] + [SDS(a.shape, a.dtype) for a in chip_sums] + pack_shapes,
        scratch_shapes=scratch,
        compiler_params=pltpu.CompilerParams(dimension_semantics=("arbitrary", "arbitrary")),
    )(*dparts, wt, x, nw, dres, *chip_sums, *packs)
    return outs[0], outs[1], outs[2:2 + nx], outs[2 + nx:]


def _matmul_tn(a_parts, b_parts, name):
    tile, tk = 1024, 1024
    s = a_parts[0].shape[0]
    nk = s // tk
    na, nb = len(a_parts), len(b_parts)
    offs_a, counts_a, ni = _col_blocks(a_parts, tile)
    offs_b, counts_b, nj = _col_blocks(b_parts, tile)

    def body(*refs):
        a_refs, b_refs, o_ref = refs[:na], refs[na:na + nb], refs[na + nb]
        i, j = pl.program_id(0), pl.program_id(1)

        @pl.when(pl.program_id(2) == 0)
        def _():
            o_ref[...] = jnp.zeros_like(o_ref)

        for ta in range(na):
            for tb in range(nb):
                in_a = jnp.logical_and(i >= offs_a[ta], i < offs_a[ta] + counts_a[ta])
                in_b = jnp.logical_and(j >= offs_b[tb], j < offs_b[tb] + counts_b[tb])

                @pl.when(jnp.logical_and(in_a, in_b))
                def _(ta=ta, tb=tb):
                    o_ref[...] += _tn(a_refs[ta][...], b_refs[tb][...])

    def spec(offs, counts, t, axis):
        def index(i, j, k):
            pos = (i, j)[axis]
            mine = jnp.logical_and(pos >= offs[t], pos < offs[t] + counts[t])
            return jnp.where(mine, k, 0), jnp.clip(pos - offs[t], 0, counts[t] - 1)
        return pl.BlockSpec((tk, tile), index)

    return pl.pallas_call(
        body, name=name, grid=(ni, nj, nk),
        in_specs=[spec(offs_a, counts_a, t, 0) for t in range(na)] + [spec(offs_b, counts_b, t, 1) for t in range(nb)],
        out_specs=pl.BlockSpec((tile, tile), lambda i, j, k: (i, j)),
        out_shape=SDS((ni * tile, nj * tile), f32),
        compiler_params=pltpu.CompilerParams(dimension_semantics=("parallel", "parallel", "arbitrary")),
    )(*a_parts, *b_parts)


def _adamw(w, g, m, v):
    m = ADAM_B1 * m + (1.0 - ADAM_B1) * g
    v = ADAM_B2 * v + (1.0 - ADAM_B2) * (g * g)
    m_hat = m / (1.0 - ADAM_B1 ** ADAM_STEP)
    v_hat = v / (1.0 - ADAM_B2 ** ADAM_STEP)
    delta = -ADAM_LR * (m_hat / (jnp.sqrt(v_hat) + ADAM_EPS) + ADAM_WD * w)
    return delta, m, v


def _sum_adamw(parts, w, m, v, name):
    r, c = w.shape
    tc = 256

    def body(p_ref, w_ref, m_ref, v_ref, g_ref, d_ref, nm_ref, nv_ref):
        g = p_ref[0].astype(f32)
        for q in range(1, 4):
            g = g + p_ref[q].astype(f32)
        g_ref[...] = g
        d_ref[...], nm_ref[...], nv_ref[...] = _adamw(w_ref[...], g, m_ref[...], v_ref[...])

    blk = pl.BlockSpec((r, tc), lambda i: (0, i))
    return pl.pallas_call(
        body, name=name, grid=(c // tc,),
        in_specs=[pl.BlockSpec((4, r, tc), lambda i: (0, 0, i)), blk, blk, blk],
        out_specs=[blk] * 4, out_shape=[SDS((r, c), f32)] * 4,
        compiler_params=pltpu.CompilerParams(dimension_semantics=("parallel",)),
    )(parts, w, m, v)


def _sum_small(parts, pre_blocks):
    def body(p_ref, b_ref, o_ref):
        t = p_ref[0]
        pre = b_ref[0]
        for j in range(1, N_DEV):
            t = t + p_ref[j]
            pre = pre + b_ref[j]
        o_ref[...] = t
        o_ref[5:6, 0:D_MODEL] = pre[0:1, :]
        row_h = _iota((D_SSM, LANES), 0) // HEAD_DIM
        fold = (row_h == _iota((D_SSM, LANES), 1)).astype(f32)
        lower = t[8:16, 0:LANES]
        folded = _nn_hi(t[8:16, 0:D_SSM], fold)
        loss = jnp.sum(t[11:12, 0:D_MODEL], axis=1, keepdims=True) * (0.5 / D_MODEL)
        row = _iota((8, LANES), 0)
        o_ref[8:16, 0:LANES] = jnp.where(row < 2, folded, jnp.where(row == 4, loss, lower))

    return pl.pallas_call(body, name="sum_small", out_shape=SDS((PACK_ROWS, PACK_W), f32),
                          in_specs=[pl.BlockSpec(memory_space=pltpu.VMEM)] * 2,
                          out_specs=pl.BlockSpec(memory_space=pltpu.VMEM))(parts, pre_blocks)


def _adamw_small(w, g, m, v):
    def body(w_ref, g_ref, m_ref, v_ref, d_ref, nm_ref, nv_ref):
        d_ref[...], nm_ref[...], nv_ref[...] = _adamw(w_ref[...], g_ref[...], m_ref[...], v_ref[...])

    vm = pl.BlockSpec(memory_space=pltpu.VMEM)
    return pl.pallas_call(body, name="adamw_small", out_shape=[SDS(w.shape, f32)] * 3,
                          in_specs=[vm] * 4, out_specs=[vm] * 3)(w, g, m, v)


def _pad_lanes(v, width):
    return jnp.pad(v, ((0, 0), (0, width - v.shape[1])))


def _local_step(x, tgt, norm_pre_w, wt, conv_w, conv_b, dt_bias, a_log, d_skip, ssm_norm_w, wo, norm_post_w,
                weight_grads, sharded):
    dtb16 = _pad_lanes(dt_bias, LANES)
    alog16 = _pad_lanes(a_log, LANES)
    alog_f = jnp.repeat(a_log, HEAD_DIM, axis=1)
    d_f = jnp.repeat(d_skip, HEAD_DIM, axis=1)

    if sharded:
        proj, u, (g_out, g_cw) = _prenorm_inproj(x, norm_pre_w, wt, gather=(wo, conv_w))
        wo = g_out.reshape(N_DEV * wo.shape[0], D_MODEL)
        conv_w = g_cw.transpose(1, 0, 2).reshape(4, D_CONV)
    else:
        proj, u, _ = _prenorm_inproj(x, norm_pre_w, wt)
    o, lb, mix_a = _attn_fwd(proj)
    mix_s, y, states = _ssd_fwd(proj, conv_w, conv_b, dtb16, alog16, alog_f, d_f, ssm_norm_w)
    dmix, dout, dres, acc_post = _outproj_loss(mix_a, mix_s, wo, x, tgt, norm_post_w)
    dq, dk, dv, dg = _attn_bwd(proj, o, lb, dmix)
    dzxd, g_conv, g_vec, g_dt = _ssd_bwd(proj, y, states, dmix, conv_w, conv_b, dtb16, alog16, alog_f, d_f, ssm_norm_w)
    dparts = [dq, dk, dv, dg, dzxd]
    dw_out = _matmul_tn([mix_a, mix_s], [dout], "dw_out")
    chip_sums, carry = weight_grads(dparts, u, dw_out)

    def pack(g_pre_row):
        return jnp.concatenate(
            [g_conv[0:5], g_pre_row, _pad_lanes(g_vec[0:1], PACK_W), _pad_lanes(acc_post[1:2], PACK_W),
             _pad_lanes(g_vec[1:3], PACK_W), _pad_lanes(g_dt[0:1], PACK_W), _pad_lanes(acc_post[0:1], PACK_W),
             jnp.zeros((4, PACK_W), f32)], axis=0)

    if sharded:
        grad_x, _, exchanged, small = _inproj_bwd(dparts, wt, x, norm_pre_w, dres, chip_sums,
                                                  pack(jnp.zeros((1, PACK_W), f32)))
    else:
        grad_x, g_pre, exchanged, _ = _inproj_bwd(dparts, wt, x, norm_pre_w, dres, chip_sums)
        small = pack(_pad_lanes(g_pre[0:1], PACK_W))
    return grad_x, carry, exchanged, small


def kernel(x, norm_pre_w, w_in, conv_w, conv_b, dt_bias, a_log, d_skip, ssm_norm_w, w_out, norm_post_w, loss_target, m_norm_pre_w, m_w_in, m_conv_w, m_conv_b, m_dt_bias, m_a_log, m_d_skip, m_ssm_norm_w, m_w_out, m_norm_post_w, v_norm_pre_w, v_w_in, v_conv_w, v_conv_b, v_dt_bias, v_a_log, v_d_skip, v_ssm_norm_w, v_w_out, v_norm_post_w):
    shard_in = w_in.shape[2]
    shard_cv = conv_w.shape[2]
    me = 4 * lax.axis_index("x") + 2 * lax.axis_index("y") + lax.axis_index("c")

    g_in, = _all_gather([w_in[0].T.astype(bf16)])
    wt = _assemble_wt(g_in)

    def weight_grads(dparts, u, dw_out):
        dw_in, got_in, got_out = _dw_in_swap(dparts, u, dw_out)
        return [_chip_sum(dw_in, got_in, shard_in, "chip_sum_w_in"),
                _chip_sum(dw_out, got_out, w_out.shape[1], "chip_sum_w_out")], ()

    grad_x, _, (parts_in, parts_out), (parts_small, pre_blocks) = _local_step(
        x[0], loss_target[0], norm_pre_w, wt, conv_w[0], conv_b, dt_bias, a_log, d_skip, ssm_norm_w,
        w_out[0].astype(bf16), norm_post_w, weight_grads, sharded=True)

    g_w_in, d_w_in, nm_w_in, nv_w_in = (a.T for a in _sum_adamw(
        parts_in, w_in[0].T, m_w_in[0].T, v_w_in[0].T, "sum_adamw_w_in"))
    g_w_out, d_w_out, nm_w_out, nv_w_out = _sum_adamw(parts_out, w_out[0], m_w_out[0], v_w_out[0], "sum_adamw_w_out")
    tot = _sum_small(parts_small, pre_blocks)

    g_cw_all = tot[0:4]
    small_g = {
        "conv_w": lax.dynamic_slice(g_cw_all, (0, me * shard_cv), (4, shard_cv)),
        "conv_b": tot[4:5], "norm_pre_w": tot[5:6, :D_MODEL], "ssm_norm_w": tot[6:7, :D_SSM],
        "norm_post_w": tot[7:8, :D_MODEL], "a_log": tot[8:9, :16], "d_skip": tot[9:10, :16], "dt_bias": tot[10:11, :16],
    }
    loss = tot[12, 0]
    small_w = {"conv_w": (conv_w[0], m_conv_w[0], v_conv_w[0]), "conv_b": (conv_b, m_conv_b, v_conv_b),
               "norm_pre_w": (norm_pre_w, m_norm_pre_w, v_norm_pre_w), "ssm_norm_w": (ssm_norm_w, m_ssm_norm_w, v_ssm_norm_w),
               "norm_post_w": (norm_post_w, m_norm_post_w, v_norm_post_w), "a_log": (a_log, m_a_log, v_a_log),
               "d_skip": (d_skip, m_d_skip, v_d_skip), "dt_bias": (dt_bias, m_dt_bias, v_dt_bias)}
    names = list(small_w)
    sizes = [small_g[k].size for k in names]
    tot_size = sum(sizes)
    pad_to = -(-tot_size // 1024) * 1024

    def flat(arrs):
        v = jnp.concatenate([a.reshape(-1) for a in arrs])
        return jnp.pad(v, (0, pad_to - tot_size)).reshape(pad_to // LANES, LANES)

    fw = flat([small_w[k][0] for k in names])
    fg = flat([small_g[k] for k in names])
    fm = flat([small_w[k][1] for k in names])
    fv = jnp.pad(jnp.concatenate([small_w[k][2].reshape(-1) for k in names]), (0, pad_to - tot_size),
                 constant_values=1.0).reshape(pad_to // LANES, LANES)
    fd, fnm, fnv = _adamw_small(fw, fg, fm, fv)

    def unflat(f):
        out, off = {}, 0
        v = f.reshape(-1)
        for k, n in zip(names, sizes):
            out[k] = v[off:off + n].reshape(small_g[k].shape)
            off += n
        return out

    sd, snm, snv = unflat(fd), unflat(fnm), unflat(fnv)
    lead = lambda a: a[None]
    order = ["norm_pre_w", "w_in", "conv_w", "conv_b", "dt_bias", "a_log", "d_skip", "ssm_norm_w", "w_out", "norm_post_w"]
    grads = dict(small_g, w_in=g_w_in, w_out=g_w_out)
    deltas = dict(sd, w_in=d_w_in, w_out=d_w_out)
    new_m = dict(snm, w_in=nm_w_in, w_out=nm_w_out)
    new_v = dict(snv, w_in=nv_w_in, w_out=nv_w_out)

    def shaped(dct, k):
        a = dct[k]
        return lead(a) if k in ("w_in", "w_out", "conv_w") else a

    return (loss, grad_x[None], *[shaped(grads, k) for k in order], *[shaped(deltas, k) for k in order],
            *[shaped(new_m, k) for k in order], *[shaped(new_v, k) for k in order])
```

```python
import functools
import math

import jax
import jax.numpy as jnp
import numpy as np
from jax import lax
from jax.experimental import pallas as pl
from jax.experimental.pallas import tpu as pltpu

f32, bf16 = jnp.float32, jnp.bfloat16
SDS = jax.ShapeDtypeStruct
HIGHEST = lax.Precision.HIGHEST
MESH = pl.DeviceIdType.MESH

N_DEV = 8
D_MODEL = 1024
D_ATTN = 1024
D_SSM = 1024
HEAD_DIM = 64
N_PAIRS = 8
D_STATE = 128
N_GROUPS = 2
D_CONV = D_SSM + 2 * N_GROUPS * D_STATE
D_IN_PROJ = 4 * D_ATTN + D_SSM + D_CONV + 16
NP = 7168
CHUNK = 128
BLK = 128
DILATIONS = (1, 4, 16)
EPS = 1e-6
LANES = 128
COL_Z, COL_XS, COL_BC, COL_DT = 4096, 5120, 6144, 6656

ADAM_LR, ADAM_B1, ADAM_B2, ADAM_EPS, ADAM_WD, ADAM_STEP = 0.001, 0.9, 0.999, 1e-08, 0.01, 10

PACK_ROWS, PACK_W = 16, 1536


def _nt(a, b):
    return lax.dot_general(a, b, (((1,), (1,)), ((), ())), preferred_element_type=f32)


def _tn(a, b):
    return lax.dot_general(a, b, (((0,), (0,)), ((), ())), preferred_element_type=f32)


def _nn(a, b):
    return jnp.dot(a, b, preferred_element_type=f32)


def _nn_hi(a, b):
    return jnp.dot(a, b, precision=HIGHEST, preferred_element_type=f32)


def _sigmoid(x):
    return 1.0 / (1.0 + jnp.exp(-x))


def _softplus(x):
    return jnp.maximum(x, 0.0) + jnp.log1p(jnp.exp(-jnp.abs(x)))


def _iota(shape, dim):
    return lax.broadcasted_iota(jnp.int32, shape, dim)


def _my_pos():
    return lax.axis_index("x"), lax.axis_index("y"), lax.axis_index("c")


GATHER_SEMS = 9


def _gather_phases(ins, outs, send_sems, recv_sems, local_sems):
    n, ns = len(ins), GATHER_SEMS
    x, y, c = _my_pos()
    me, sibling = (x, y, c), (x, y, 1 - c)
    xn, yn, diag = (1 - x, y), (x, 1 - y), (1 - x, 1 - y)

    def slot(a, px, py, pc):
        return outs[a].at[4 * px + 2 * py + pc]

    def part(a, ref, h):
        width = ins[a].shape[-1]
        if width % (2 * LANES):
            return ref if h == 1 else None
        return ref.at[:, pl.ds(h * (width // 2), width // 2)]

    def copy(a, k, block, to, src=None, h=None):
        src_ref = slot(a, *block) if src is None else src
        dst_ref = slot(a, *block)
        if h is not None:
            src_ref, dst_ref = part(a, src_ref, h), part(a, dst_ref, h)
            if src_ref is None:
                return None
        return pltpu.make_async_remote_copy(
            src_ref=src_ref, dst_ref=dst_ref, send_sem=send_sems.at[ns * a + k], recv_sem=recv_sems.at[ns * a + k],
            device_id=to, device_id_type=MESH)

    def mine():
        return [pltpu.make_async_copy(ins[a], slot(a, *me), local_sems.at[a]) for a in range(n)]

    def own_sends(a):
        return [copy(a, 0, me, sibling, src=ins[a]), copy(a, 1, me, (*xn, c), src=ins[a]),
                copy(a, 2, me, (*yn, c), src=ins[a])]

    def neighbour_relays(a):
        return [copy(a, 4, (*xn, c), sibling), copy(a, 7, (*xn, c), (*yn, c), h=1),
                copy(a, 5, (*yn, c), sibling), copy(a, 8, (*yn, c), (*xn, c), h=0)]

    def diagonal_halves(a):
        return [copy(a, k, (*diag, c), me, h=h) for k, h in ((8, 0), (7, 1))]

    def start_all(cps):
        for cp in cps:
            if cp is not None:
                cp.start()

    def phase0():
        start_all(mine())
        for a in range(n):
            start_all(own_sends(a))

    def phase1():
        for a in range(n):
            copy(a, 1, (*xn, c), me).wait_recv()
            copy(a, 2, (*yn, c), me).wait_recv()
            start_all(neighbour_relays(a))

    def phase2():
        for a in range(n):
            for cp in diagonal_halves(a):
                if cp is not None:
                    cp.wait_recv()
            copy(a, 6, (*diag, c), sibling).start()

    def finish():
        for a in range(n):
            copy(a, 0, sibling, me).wait_recv()
            for j, chip in enumerate((xn, yn, diag)):
                copy(a, 4 + j, (*chip, 1 - c), me).wait_recv()
        for a in range(n):
            for cp in own_sends(a) + neighbour_relays(a) + [copy(a, 6, (*diag, c), sibling)]:
                if cp is not None:
                    cp.wait_send()
        for cp in mine():
            cp.wait()

    return phase0, phase1, phase2, finish


def _gather_scratch(n):
    return [pltpu.SemaphoreType.DMA((GATHER_SEMS * n,)), pltpu.SemaphoreType.DMA((GATHER_SEMS * n,)),
            pltpu.SemaphoreType.DMA((n,))]


def _all_gather(arrs):
    n = len(arrs)

    def body(*refs):
        for phase in _gather_phases(refs[:n], refs[n:2 * n], *refs[2 * n:]):
            phase()

    anyspec = pl.BlockSpec(memory_space=pl.ANY)
    return pl.pallas_call(
        body, name="weights_all_gather",
        out_shape=[SDS((N_DEV,) + a.shape, a.dtype) for a in arrs],
        in_specs=[anyspec] * n, out_specs=[anyspec] * n, scratch_shapes=_gather_scratch(n),
    )(*arrs)


def _dw_in_swap(a_parts, u):
    tile, tk = 1024, 1024
    s = u.shape[0]
    nk = s // tk
    na = len(a_parts)
    offs, counts, ni = _col_blocks(a_parts, tile)

    def body(*refs):
        a_refs, u_ref = refs[:na], refs[na]
        dw_ref, got_ref = refs[na + 1:na + 3]
        acc, local_sems, send_sems, recv_sem = refs[na + 3:]
        i, k = pl.program_id(0), pl.program_id(1)
        x, y, c = _my_pos()
        par = i % 2

        def tile_copies(t, p):
            rows = pl.ds(pl.multiple_of(t * tile, tile), tile)
            loc = pltpu.make_async_copy(acc.at[p], dw_ref.at[rows], local_sems.at[p])
            rem = pltpu.make_async_remote_copy(
                src_ref=acc.at[p], dst_ref=got_ref.at[rows], send_sem=send_sems.at[p], recv_sem=recv_sem,
                device_id=(x, y, 1 - c), device_id_type=MESH)
            return loc, rem

        @pl.when(k == 0)
        def _():
            @pl.when(i >= 2)
            def _():
                loc, rem = tile_copies(i - 2, par)
                loc.wait()
                rem.wait_send()
            acc[par] = jnp.zeros((tile, tile), f32)

        for t in range(na):
            @pl.when(jnp.logical_and(i >= offs[t], i < offs[t] + counts[t]))
            def _(t=t):
                acc[par] += _tn(a_refs[t][...], u_ref[...])

        @pl.when(k == nk - 1)
        def _():
            loc, rem = tile_copies(i, par)
            loc.start()
            rem.start()

        @pl.when(jnp.logical_and(i == ni - 1, k == nk - 1))
        def _():
            for t in (ni - 2, ni - 1):
                loc, rem = tile_copies(t, t % 2)
                loc.wait()
                rem.wait_send()
            pltpu.make_async_remote_copy(src_ref=dw_ref, dst_ref=got_ref, send_sem=send_sems.at[0], recv_sem=recv_sem,
                                         device_id=(x, y, c), device_id_type=MESH).wait_recv()

    def a_spec(t):
        def index(i, k):
            mine = jnp.logical_and(i >= offs[t], i < offs[t] + counts[t])
            return jnp.where(mine, k, 0), jnp.clip(i - offs[t], 0, counts[t] - 1)
        return pl.BlockSpec((tk, tile), index)

    anyspec = pl.BlockSpec(memory_space=pl.ANY)
    return pl.pallas_call(
        body, name="dw_in_swap", grid=(ni, nk),
        in_specs=[a_spec(t) for t in range(na)] + [pl.BlockSpec((tk, tile), lambda i, k: (k, 0))],
        out_specs=[anyspec] * 2,
        out_shape=[SDS((ni * tile, tile), f32), SDS((ni * tile, tile), f32)],
        scratch_shapes=[pltpu.VMEM((2, tile, tile), f32), pltpu.SemaphoreType.DMA((2,)), pltpu.SemaphoreType.DMA((2,)),
                        pltpu.SemaphoreType.DMA(())],
        compiler_params=pltpu.CompilerParams(dimension_semantics=("arbitrary", "arbitrary")),
    )(*a_parts, u)


def _chip_sum(mine, got, rows, name):
    r, cdim = mine.shape
    tc = LANES

    def body(m_ref, g_ref, s16_ref):
        c = lax.axis_index("c")
        for q in range(4):
            blk = pl.ds(rows * (2 * q + c), rows)
            s16_ref[q] = (m_ref[blk, :] + g_ref[blk, :]).astype(bf16)

    col = pl.BlockSpec((r, tc), lambda i: (0, i))
    return pl.pallas_call(
        body, name=name, grid=(cdim // tc,), in_specs=[col, col],
        out_specs=pl.BlockSpec((4, rows, tc), lambda i: (0, 0, i)), out_shape=SDS((4, rows, cdim), bf16),
        compiler_params=pltpu.CompilerParams(dimension_semantics=("parallel",)),
    )(mine, got)


def _assemble_wt(shards):
    nd, rows, cdim = shards.shape
    tc = 256

    def body(g_ref, o_ref):
        for j in range(nd):
            o_ref[pl.ds(rows * j, rows), :] = g_ref[j]
        o_ref[pl.ds(nd * rows, NP - nd * rows), :] = jnp.zeros((NP - nd * rows, tc), shards.dtype)

    return pl.pallas_call(
        body, name="assemble_w_in", grid=(cdim // tc,),
        in_specs=[pl.BlockSpec((nd, rows, tc), lambda i: (0, 0, i))],
        out_specs=pl.BlockSpec((NP, tc), lambda i: (0, i)), out_shape=SDS((NP, cdim), shards.dtype),
        compiler_params=pltpu.CompilerParams(dimension_semantics=("parallel",)),
    )(shards)


def _chip_exchange_copies(ins, outs, send_sems, recv_sems, local_sems):
    nb = len(ins)
    x, y, c = _my_pos()
    my_q = 2 * x + y
    mine = [pltpu.make_async_copy(ins[a].at[my_q], outs[a].at[my_q], local_sems.at[a]) for a in range(nb)]
    sends, recvs = [], []
    for k in range(1, 4):
        to, frm = (my_q + k) % 4, (my_q + 4 - k) % 4
        for a in range(nb):
            sems = dict(send_sem=send_sems.at[3 * a + k - 1], recv_sem=recv_sems.at[3 * a + k - 1], device_id_type=MESH)
            sends.append(pltpu.make_async_remote_copy(
                src_ref=ins[a].at[to], dst_ref=outs[a].at[my_q], device_id=(to // 2, to % 2, c), **sems))
            recvs.append(pltpu.make_async_remote_copy(
                src_ref=ins[a].at[frm], dst_ref=outs[a].at[frm], device_id=(x, y, c), **sems))
    return mine, sends, recvs


def _chip_exchange_scratch(nb):
    return [pltpu.SemaphoreType.DMA((3 * nb,)), pltpu.SemaphoreType.DMA((3 * nb,)), pltpu.SemaphoreType.DMA((nb,))]


def _prenorm_inproj(x, nw, wt, gather=()):
    s, d = x.shape
    npad = wt.shape[0]
    tm, tn = 1024, 1024
    ng = len(gather)
    ni, nj = s // tm, npad // tn

    def body(x_ref, nw_ref, w_ref, *refs):
        g_in, (proj_ref, u_ref), g_out, sems = refs[:ng], refs[ng:ng + 2], refs[ng + 2:2 * ng + 2], refs[2 * ng + 2:]
        i, j = pl.program_id(0), pl.program_id(1)
        if ng:
            phases = _gather_phases(g_in, g_out, *sems)
            for step, phase in enumerate(phases[:3]):
                @pl.when(jnp.logical_and(i == step, j == 0))
                def _(phase=phase):
                    phase()

        @pl.when(j == 0)
        def _():
            xv = x_ref[...]
            r = lax.rsqrt(jnp.mean(xv * xv, axis=-1, keepdims=True) + EPS)
            u_ref[...] = (xv * r * nw_ref[...]).astype(bf16)
        proj_ref[...] = _nt(u_ref[...], w_ref[...])

        if ng:
            @pl.when(jnp.logical_and(i == ni - 1, j == nj - 1))
            def _():
                phases[3]()

    anyspec = pl.BlockSpec(memory_space=pl.ANY)
    outs = pl.pallas_call(
        body, name="prenorm_inproj", grid=(ni, nj),
        in_specs=[pl.BlockSpec((tm, d), lambda i, j: (i, 0)), pl.BlockSpec((1, d), lambda i, j: (0, 0)),
                  pl.BlockSpec((tn, d), lambda i, j: (j, 0))] + [anyspec] * ng,
        out_specs=[pl.BlockSpec((tm, tn), lambda i, j: (i, j)), pl.BlockSpec((tm, d), lambda i, j: (i, 0))]
        + [anyspec] * ng,
        out_shape=[SDS((s, npad), f32), SDS((s, d), bf16)] + [SDS((N_DEV,) + a.shape, a.dtype) for a in gather],
        scratch_shapes=_gather_scratch(ng) if ng else [],
        compiler_params=pltpu.CompilerParams(dimension_semantics=("arbitrary", "arbitrary")),
    )(x, nw, wt, *gather)
    return outs[0], outs[1], outs[2:]


def _attn_consts():
    head0 = _iota((BLK, LANES), 1) < HEAD_DIM
    tri2 = (_iota((BLK, 2 * LANES), 1) % LANES) <= _iota((BLK, 2 * LANES), 0)
    ones2 = ((_iota((LANES, 2 * LANES), 0) < HEAD_DIM) == (_iota((LANES, 2 * LANES), 1) < LANES)).astype(bf16)
    rmat = ((_iota((2 * LANES, LANES), 0) < LANES) == (_iota((2 * LANES, LANES), 1) < HEAD_DIM)).astype(bf16)
    bones = ((_iota((LANES, LANES), 0) < HEAD_DIM) == (_iota((LANES, LANES), 1) < HEAD_DIM)).astype(bf16)
    return head0, tri2, ones2, rmat, bones


def _stack_heads(x16, head0):
    zero = jnp.zeros_like(x16)
    return jnp.concatenate([jnp.where(head0, x16, zero), jnp.where(head0, zero, x16)], axis=0)


def _bf16_terms(x, terms):
    out = []
    for _ in range(terms):
        t = x.astype(bf16)
        out.append(t)
        x = x - t.astype(f32)
    return out


def _dot_01(x, w16, terms):
    return _nn(jnp.concatenate(_bf16_terms(x, terms), axis=1), jnp.concatenate([w16] * terms, axis=0))


def _split_dot(x, w16):
    return _dot_01(x, w16, 2)


def _split_dot_sum(x, w16):
    hi, lo = _bf16_terms(x, 2)
    return _nn(hi, w16) + _nn(lo, w16)


def _dot_01_left(w16, x, terms):
    return _nn(jnp.concatenate([w16] * terms, axis=1), jnp.concatenate(_bf16_terms(x, terms), axis=0))


def _attn_fwd(proj):
    s = proj.shape[0]
    n_it = s // BLK

    def body(q_ref, k_ref, v_ref, g_ref, o_ref, l_ref, mix_ref, op0, op1, op2, lp0, lp1, lp2,
             s_a, s_b, sd_a, sd_b, p_a, p_b, m_a, m_b, pd_a, pd_b, k_a, k_b, v_a, v_b):
        op_refs, lp_refs = (op0, op1, op2), (lp0, lp1, lp2)
        head0, tri2, ones2, rmat, _ = _attn_consts()
        score_bufs, prob_bufs = ((s_a, sd_a), (s_b, sd_b)), ((p_a, m_a, pd_a), (p_b, m_b, pd_b))
        k_bufs, v_bufs = (k_a, k_b), (v_a, v_b)
        for buf in k_bufs + v_bufs:
            buf[...] = jnp.zeros_like(buf)

        def block_rows(i, d, nb):
            r, blk = i // nb, i % nb
            return pl.ds(blk * (BLK * d) + r, BLK, stride=d), blk > 0

        def unstack(st16):
            return st16[:BLK] + st16[BLK:]

        def scores(i, par, d, nb):
            rows, has_prev = block_rows(i, d, nb)
            s_buf, sd_buf = score_bufs[par]
            qs = q_ref[rows, :] * 0.125
            qs16 = qs.astype(bf16)
            kst_c = _stack_heads(k_ref[rows, :].astype(bf16), head0)
            kst_p = k_bufs[1 - par][...]
            k_bufs[par][...] = kst_c
            sc = _nt(qs16, kst_c)
            sp = _nt(qs16, kst_p)
            s_buf[...] = jnp.where(tri2, sc, jnp.where(has_prev, sp, -jnp.inf))
            sd = _split_dot(qs * unstack(kst_p).astype(f32), ones2)
            sd_buf[...] = jnp.where(has_prev, sd, -jnp.inf)

        def softmax(bufs_in, bufs_out):
            s_buf, sd_buf = bufs_in
            p_buf, m_buf, pd_buf = bufs_out
            sc, sd2 = s_buf[...], sd_buf[...]
            m0 = jnp.max(sc[:, :LANES], axis=1, keepdims=True)
            m1 = jnp.max(sc[:, LANES:], axis=1, keepdims=True)
            m2 = jnp.concatenate([jnp.broadcast_to(m0, (BLK, LANES)), jnp.broadcast_to(m1, (BLK, LANES))], axis=1)
            m2 = jnp.maximum(m2, sd2)
            p_buf[...] = jnp.exp(sc - m2).astype(bf16)
            m_pair = jnp.where(head0, m2[:, :LANES], m2[:, LANES:])
            m_buf[...] = m_pair
            pd_buf[...] = jnp.exp(jnp.where(head0, sd2[:, :LANES], sd2[:, LANES:]) - m_pair)

        def output(i, par, d, nb, p):
            rows, _ = block_rows(i, d, nb)
            p_buf, m_buf, pd_buf = prob_bufs[par]
            vst_c = _stack_heads(v_ref[rows, :].astype(bf16), head0)
            vst_p = v_bufs[1 - par][...]
            v_bufs[par][...] = vst_c
            pt16, pd = p_buf[...], pd_buf[...]
            zero = jnp.zeros_like(pt16)
            o = (_nn(jnp.where(tri2, pt16, zero), vst_c) + _nn(jnp.where(tri2, zero, pt16), vst_p)
                 + pd * unstack(vst_p).astype(f32))
            l = _nn(pt16, rmat) + pd
            op_refs[p][rows, :] = o / l
            lp_refs[p][rows, :] = m_buf[...] + jnp.log(l)

        for p, d in enumerate(DILATIONS):
            nb = s // (BLK * d)
            scores(0, 0, d, nb)
            scores(1, 1, d, nb)
            softmax(score_bufs[0], prob_bufs[0])

            def steps(j, carry, d=d, nb=nb, p=p):
                for par in range(2):
                    t = 2 * j + 2 + par
                    scores(t, par, d, nb)
                    output(t - 2, par, d, nb, p)
                    softmax(score_bufs[1 - par], prob_bufs[1 - par])
                return carry

            lax.fori_loop(0, (n_it - 2) // 2, steps, 0)
            output(n_it - 2, 0, d, nb, p)
            softmax(score_bufs[1], prob_bufs[1])
            output(n_it - 1, 1, d, nb, p)

        def merge(i, carry):
            rows = pl.ds(pl.multiple_of(i * 256, 256), 256)
            l0, l1, l2 = lp0[rows, :], lp1[rows, :], lp2[rows, :]
            m = jnp.maximum(jnp.maximum(l0, l1), l2)
            e0, e1, e2 = jnp.exp(l0 - m), jnp.exp(l1 - m), jnp.exp(l2 - m)
            z = e0 + e1 + e2
            o = (e0 * op0[rows, :] + e1 * op1[rows, :] + e2 * op2[rows, :]) / z
            o_ref[rows, :] = o
            l_ref[rows, :] = m + jnp.log(z)
            g = g_ref[rows, :]
            mix_ref[rows, :] = (o * (g * _sigmoid(g))).astype(bf16)
            return carry

        lax.fori_loop(0, s // 256, merge, 0)

    col = lambda base: pl.BlockSpec((s, LANES), lambda h: (0, base + h))
    return pl.pallas_call(
        body, name="attn_fwd", grid=(N_PAIRS,),
        in_specs=[col(0), col(8), col(16), col(24)],
        out_specs=[col(0), col(0), col(0)],
        out_shape=[SDS((s, D_ATTN), f32), SDS((s, D_ATTN), f32), SDS((s, D_ATTN), bf16)],
        scratch_shapes=[pltpu.VMEM((s, LANES), f32)] * 6 + [pltpu.VMEM((BLK, 2 * LANES), f32)] * 4
        + [pltpu.VMEM((BLK, 2 * LANES), bf16)] * 2 + [pltpu.VMEM((BLK, LANES), f32)] * 4
        + [pltpu.VMEM((2 * BLK, LANES), bf16)] * 4,
        compiler_params=pltpu.CompilerParams(dimension_semantics=("parallel",)),
    )(proj, proj, proj, proj)


def _expand_mat():
    colv = _iota((LANES, 2 * D_SSM), 1)
    head = 2 * ((colv % D_SSM) // LANES) + colv // D_SSM
    return (_iota((LANES, 2 * D_SSM), 0) == head).astype(bf16)


def _fold_mat():
    return (_iota((D_SSM, LANES), 0) // HEAD_DIM == _iota((D_SSM, LANES), 1)).astype(bf16)


def _ssd_common(xs_ref, bc_ref, xs_tail, bc_tail, dt_ref, cw_ref, cb_ref, dtb_ref, alog16_ref, emat_ref, xpad, first):
    keep = jnp.where(first, 0.0, 1.0)
    xpad[0:8, 0:D_SSM] = xs_tail[...] * keep
    xpad[0:8, D_SSM:D_CONV] = bc_tail[...] * keep
    xpad[8:8 + CHUNK, 0:D_SSM] = xs_ref[...]
    xpad[8:8 + CHUNK, D_SSM:D_CONV] = bc_ref[...]
    xp = xpad[...]
    taps = [pltpu.roll(xp, 3 - j, 0)[8:8 + CHUNK] for j in range(3)] + [xp[8:8 + CHUNK]]
    cv = cb_ref[...] + cw_ref[0:1, :] * taps[0]
    for j in range(1, 4):
        cv = cv + cw_ref[j:j + 1, :] * taps[j]
    sig = _sigmoid(cv)
    xbc = cv * sig

    pre = dt_ref[...] + dtb_ref[...]
    dt16 = _softplus(pre)
    a16 = -jnp.exp(alog16_ref[...])
    sub, lane = _iota((CHUNK, CHUNK), 0), _iota((CHUNK, CHUNK), 1)
    tri = (sub >= lane).astype(f32)
    al16 = _nn_hi(tri, dt16 * a16)
    al_t = al16.T
    emat = emat_ref[...]
    dt_x = _dot_01(dt16, emat, 3)
    al_x = _dot_01(al16, emat, 3)
    lane_w = _iota((CHUNK, D_SSM), 1)
    even = (lane_w % LANES) < HEAD_DIM
    dt_f = jnp.where(even, dt_x[:, :D_SSM], dt_x[:, D_SSM:])
    al_f = jnp.where(even, al_x[:, :D_SSM], al_x[:, D_SSM:])
    return cv, sig, xbc, pre, dt_f, al_f, al_x, al_t, taps


def _decay_mat(al_x, al_t, pair, h):
    sub, lane = _iota((CHUNK, CHUNK), 0), _iota((CHUNK, CHUNK), 1)
    col = al_x[:, h * D_SSM + pair * LANES: h * D_SSM + (pair + 1) * LANES]
    row = al_t[2 * pair + h: 2 * pair + h + 1, :]
    return jnp.exp(jnp.where(sub >= lane, col - row, -jnp.inf))


def _ssd_in_specs(order):
    blk = lambda w, cb: pl.BlockSpec((CHUNK, w), lambda i: (order(i), cb))
    tail = lambda w, cb: pl.BlockSpec((8, w), lambda i: (jnp.maximum(16 * order(i) - 1, 0), cb))
    return [blk(D_SSM, COL_XS // D_SSM), blk(512, COL_BC // 512), tail(D_SSM, COL_XS // D_SSM),
            tail(512, COL_BC // 512), blk(LANES, COL_DT // LANES), blk(D_SSM, COL_Z // D_SSM)]


def _full(shape):
    return pl.BlockSpec(shape, lambda i: (0,) * len(shape))


def _ssd_fwd(proj, conv_w, conv_b, dtb16, alog16, alog_f, d_f, nw):
    s = proj.shape[0]
    nc = s // CHUNK

    def body(xs_ref, bc_ref, xs_tail, bc_tail, dt_ref, z_ref, cw_ref, cb_ref, dtb_ref, alog16_ref, alogf_ref,
             df_ref, nw_ref, mix_ref, y_ref, st_ref, h_scr, xpad, y_scr, emat_ref):
        c = pl.program_id(0)

        @pl.when(c == 0)
        def _():
            h_scr[...] = jnp.zeros_like(h_scr)
            emat_ref[...] = _expand_mat()

        _, _, xbc, _, dt_f, al_f, al_x, al_t, _ = _ssd_common(
            xs_ref, bc_ref, xs_tail, bc_tail, dt_ref, cw_ref, cb_ref, dtb_ref, alog16_ref, emat_ref, xpad, c == 0)
        head0 = _iota((CHUNK, LANES), 1) < HEAD_DIM
        st_ref[...] = h_scr[...]
        for g in range(N_GROUPS):
            bm = xbc[:, D_SSM + g * D_STATE: D_SSM + (g + 1) * D_STATE].astype(bf16)
            cm = xbc[:, D_SSM + (N_GROUPS + g) * D_STATE: D_SSM + (N_GROUPS + g + 1) * D_STATE].astype(bf16)
            gmat = _nt(cm, bm)
            for pair in range(4 * g, 4 * g + 4):
                sl = slice(pair * LANES, (pair + 1) * LANES)
                xp, dtp, alp = xbc[:, sl], dt_f[:, sl], al_f[:, sl]
                xdt = xp * dtp
                xdt16 = xdt.astype(bf16)
                al_last = alp[CHUNK - 1:CHUNK, :]
                hp = h_scr[:, sl]
                y_off = jnp.exp(alp) * _nn(cm, hp.astype(bf16))
                yd = [_nn((gmat * _decay_mat(al_x, al_t, pair, h)).astype(bf16), xdt16) for h in range(2)]
                y_scr[:, sl] = jnp.where(head0, yd[0], yd[1]) + y_off + df_ref[:, sl] * xp
                st = _tn(bm, (jnp.exp(al_last - alp) * xdt).astype(bf16))
                h_scr[:, sl] = jnp.exp(al_last) * hp + st
        y = y_scr[...]
        y_ref[...] = y
        z = z_ref[...]
        yz = y * (z * _sigmoid(z))
        gw = D_SSM // N_GROUPS
        for g in range(N_GROUPS):
            part = yz[:, g * gw:(g + 1) * gw]
            r = lax.rsqrt(jnp.mean(part * part, axis=-1, keepdims=True) + EPS)
            mix_ref[:, g * gw:(g + 1) * gw] = (part * r * nw_ref[:, g * gw:(g + 1) * gw]).astype(bf16)

    order = lambda i: i
    row = lambda w: pl.BlockSpec((CHUNK, w), lambda i: (i, 0))
    return pl.pallas_call(
        body, name="ssd_fwd", grid=(nc,),
        in_specs=_ssd_in_specs(order) + [_full((4, D_CONV)), _full((1, D_CONV)), _full((1, LANES)), _full((1, LANES)),
                                         _full((1, D_SSM)), _full((1, D_SSM)), _full((1, D_SSM))],
        out_specs=[row(D_SSM), row(D_SSM), pl.BlockSpec((None, D_STATE, D_SSM), lambda i: (i, 0, 0))],
        out_shape=[SDS((s, D_SSM), bf16), SDS((s, D_SSM), f32), SDS((nc, D_STATE, D_SSM), f32)],
        scratch_shapes=[pltpu.VMEM((D_STATE, D_SSM), f32), pltpu.VMEM((8 + CHUNK, D_CONV), f32),
                        pltpu.VMEM((CHUNK, D_SSM), f32), pltpu.VMEM((LANES, 2 * D_SSM), bf16)],
        compiler_params=pltpu.CompilerParams(dimension_semantics=("arbitrary",)),
    )(proj, proj, proj, proj, proj, proj, conv_w, conv_b, dtb16, alog16, alog_f, d_f, nw)


def _outproj_loss(mix_a, mix_s, wo, x, tgt, npw):
    s, d = x.shape
    tm = 512

    def body(ma_ref, ms_ref, wo_ref, x_ref, t_ref, npw_ref, dmix_ref, dout_ref, dres_ref, acc_ref):
        @pl.when(pl.program_id(0) == 0)
        def _():
            acc_ref[...] = jnp.zeros_like(acc_ref)

        out = _nn(ma_ref[...], wo_ref[0:D_ATTN, :]) + _nn(ms_ref[...], wo_ref[D_ATTN:, :])
        r = lax.rsqrt(jnp.mean(out * out, axis=-1, keepdims=True) + EPS)
        on = out * r
        diff = x_ref[...] + on * npw_ref[...] - t_ref[...]
        dres = diff * (1.0 / d)
        dres_ref[...] = dres
        acc_ref[0:1, :] += jnp.sum(diff * diff, axis=0, keepdims=True)
        acc_ref[1:2, :] += jnp.sum(dres * on, axis=0, keepdims=True)
        dn = dres * npw_ref[...]
        dout = (r * (dn - on * jnp.mean(dn * on, axis=-1, keepdims=True))).astype(bf16)
        dout_ref[...] = dout
        dmix_ref[...] = _nt(dout, wo_ref[...])

    row = lambda w: pl.BlockSpec((tm, w), lambda i: (i, 0))
    return pl.pallas_call(
        body, name="outproj_loss", grid=(s // tm,),
        in_specs=[row(D_ATTN), row(D_SSM), _full((D_ATTN + D_SSM, d)), row(d), row(d), _full((1, d))],
        out_specs=[row(D_ATTN + D_SSM), row(d), row(d), _full((8, d))],
        out_shape=[SDS((s, D_ATTN + D_SSM), f32), SDS((s, d), bf16), SDS((s, d), f32), SDS((8, d), f32)],
        compiler_params=pltpu.CompilerParams(dimension_semantics=("arbitrary",)),
    )(mix_a, mix_s, wo, x, tgt, npw)


def _attn_bwd(proj, o, lb, dmix, swap=None):
    s = proj.shape[0]
    n_it = s // BLK

    nsw = 0 if swap is None else 1

    def body(*refs):
        q_ref, k_ref, v_ref, g_ref, o_ref, l_ref, dm_ref = refs[:7]
        swap_in = refs[7:7 + nsw]
        dq_ref, dk_ref, dv_ref, dg_ref = refs[7 + nsw:11 + nsw]
        swap_out = refs[11 + nsw:11 + 2 * nsw]
        dq_acc, dk_acc, dv_acc, do_scr, dl_scr = refs[11 + 2 * nsw:16 + 2 * nsw]
        bufs = refs[16 + 2 * nsw:44 + 2 * nsw]
        swap_sems = refs[44 + 2 * nsw:]
        head0, tri2, _, _, bones = _attn_consts()

        if nsw:
            x, y, c = _my_pos()
            swap_copy = pltpu.make_async_remote_copy(
                src_ref=swap_in[0], dst_ref=swap_out[0], send_sem=swap_sems[0], recv_sem=swap_sems[1],
                device_id=(x, y, 1 - c), device_id_type=MESH)

            @pl.when(pl.program_id(0) == 0)
            def _():
                swap_copy.start()

        def pro(i, carry):
            rows = pl.ds(pl.multiple_of(i * 256, 256), 256)
            g = g_ref[rows, :]
            sg = _sigmoid(g)
            dmx = dm_ref[rows, :]
            ov = o_ref[rows, :]
            dg_ref[rows, :] = (dmx * ov * (sg * (1.0 + g * (1.0 - sg)))).astype(bf16)
            do = dmx * (g * sg)
            do_scr[rows, :] = do
            dl_scr[rows, :] = _split_dot_sum(do * ov, bones)
            z = jnp.zeros((256, LANES), f32)
            dq_acc[rows, :] = z
            dk_acc[rows, :] = z
            dv_acc[rows, :] = z
            return carry

        lax.fori_loop(0, s // 256, pro, 0)

        def per_head(t):
            return jnp.concatenate([t[:, :LANES], t[:, LANES:]], axis=0)

        def both_heads(t):
            tr = pltpu.roll(t, HEAD_DIM, 1)
            return jnp.concatenate([jnp.where(head0, t, tr), jnp.where(head0, tr, t)], axis=1)

        mm_bufs = ((bufs[0], bufs[1], bufs[2], bufs[3]), (bufs[4], bufs[5], bufs[6], bufs[7]))
        ds_bufs = ((bufs[8], bufs[9], bufs[10], bufs[11]), (bufs[12], bufs[13], bufs[14], bufs[15]))
        op_bufs = ((bufs[16], bufs[17], bufs[18], bufs[19]), (bufs[20], bufs[21], bufs[22], bufs[23]))
        vc_bufs, carry_k, carry_v = (bufs[24], bufs[25]), bufs[26], bufs[27]
        for buf in (op_bufs[0][0], op_bufs[1][0]) + vc_bufs:
            buf[...] = jnp.zeros_like(buf)

        def block_rows(i, d, nb):
            r, blk = i // nb, i % nb
            return pl.ds(blk * (BLK * d) + r, BLK, stride=d), blk > 0

        def unstack(st16):
            return st16[:BLK] + st16[BLK:]

        def products(i, par, d, nb):
            rows, has_prev = block_rows(i, d, nb)
            s_buf, dp_buf, sd_buf, dpd_buf = mm_bufs[par]
            kc_buf, kp_buf, q_buf, do_buf = op_bufs[par]
            q = q_ref[rows, :]
            qs = q * 0.125
            do = do_scr[rows, :]
            qs16, do16 = qs.astype(bf16), do.astype(bf16)
            kst_c = _stack_heads(k_ref[rows, :].astype(bf16), head0)
            vst_c = _stack_heads(v_ref[rows, :].astype(bf16), head0)
            kst_p, vst_p = op_bufs[1 - par][0][...], vc_bufs[1 - par][...]
            kc_buf[...] = kst_c
            kp_buf[...] = kst_p
            vc_bufs[par][...] = vst_c
            q_buf[...] = q.astype(bf16)
            do_buf[...] = do16
            s_buf[...] = jnp.where(tri2, _nt(qs16, kst_c), jnp.where(has_prev, _nt(qs16, kst_p), -jnp.inf))
            dp_buf[...] = jnp.where(tri2, _nt(do16, vst_c), jnp.where(has_prev, _nt(do16, vst_p), 0.0))
            sd_buf[...] = _split_dot_sum(qs * unstack(kst_p).astype(f32), bones)
            dpd_buf[...] = jnp.where(has_prev, _split_dot_sum(do * unstack(vst_p).astype(f32), bones), 0.0)

        def softmax_grad(i, par, d, nb):
            rows, has_prev = block_rows(i, d, nb)
            s_buf, dp_buf, sd_buf, dpd_buf = mm_bufs[par]
            p_buf, ds_buf, pd_buf, dsd_buf = ds_bufs[par]
            lse = l_ref[rows, :]
            dl = dl_scr[rows, :]
            pt = jnp.exp(s_buf[...] - both_heads(lse))
            ds_buf[...] = (pt * (dp_buf[...] - both_heads(dl)) * 0.125).astype(bf16)
            p_buf[...] = pt.astype(bf16)
            pd = jnp.where(has_prev, jnp.exp(sd_buf[...] - lse), 0.0)
            pd_buf[...] = pd
            dsd_buf[...] = pd * (dpd_buf[...] - dl) * 0.125

        def accumulate(i, par, d, nb):
            rows, _ = block_rows(i, d, nb)
            before, _ = block_rows(jnp.maximum(i - 1, 0), d, nb)
            p_buf, ds_buf, pd_buf, dsd_buf = ds_bufs[par]
            kc_buf, kp_buf, q_buf, do_buf = op_bufs[par]
            pt16, ds16, pd, dsd = p_buf[...], ds_buf[...], pd_buf[...], dsd_buf[...]
            zero = jnp.zeros_like(pt16)
            dsc, dsp = jnp.where(tri2, ds16, zero), jnp.where(tri2, zero, ds16)
            pc, pp = jnp.where(tri2, pt16, zero), jnp.where(tri2, zero, pt16)
            kst_c, kst_p, q16, do16 = kc_buf[...], kp_buf[...], q_buf[...], do_buf[...]
            qst, dost = _stack_heads(q16, head0), _stack_heads(do16, head0)
            dq_acc[rows, :] += _nn(dsc, kst_c) + _nn(dsp, kst_p) + dsd * unstack(kst_p).astype(f32)
            dk2 = _tn(jnp.concatenate([per_head(dsc), per_head(dsp)], axis=1), qst)
            dv2 = _tn(jnp.concatenate([per_head(pc), per_head(pp)], axis=1), dost)
            dk_acc[before, :] += carry_k[...] + dk2[BLK:] + dsd * q16.astype(f32)
            dv_acc[before, :] += carry_v[...] + dv2[BLK:] + pd * do16.astype(f32)
            carry_k[...] = dk2[:BLK]
            carry_v[...] = dv2[:BLK]

        for d in DILATIONS:
            nb = s // (BLK * d)
            carry_k[...] = jnp.zeros_like(carry_k)
            carry_v[...] = jnp.zeros_like(carry_v)
            products(0, 0, d, nb)
            products(1, 1, d, nb)
            softmax_grad(0, 0, d, nb)

            def steps(j, carry, d=d, nb=nb):
                for par in range(2):
                    t = 2 * j + 2 + par
                    accumulate(t - 2, par, d, nb)
                    products(t, par, d, nb)
                    softmax_grad(t - 1, 1 - par, d, nb)
                return carry

            lax.fori_loop(0, (n_it - 2) // 2, steps, 0)
            accumulate(n_it - 2, 0, d, nb)
            softmax_grad(n_it - 1, 1, d, nb)
            accumulate(n_it - 1, 1, d, nb)
            last, _ = block_rows(n_it - 1, d, nb)
            dk_acc[last, :] += carry_k[...]
            dv_acc[last, :] += carry_v[...]

        def epi(i, carry):
            rows = pl.ds(pl.multiple_of(i * 256, 256), 256)
            dq_ref[rows, :] = dq_acc[rows, :].astype(bf16)
            dk_ref[rows, :] = dk_acc[rows, :].astype(bf16)
            dv_ref[rows, :] = dv_acc[rows, :].astype(bf16)
            return carry

        lax.fori_loop(0, s // 256, epi, 0)

        if nsw:
            @pl.when(pl.program_id(0) == N_PAIRS - 1)
            def _():
                swap_copy.wait_send()
                swap_copy.wait_recv()

    col = lambda base: pl.BlockSpec((s, LANES), lambda h: (0, base + h))
    anyspec = pl.BlockSpec(memory_space=pl.ANY)
    swaps = [] if swap is None else [swap]
    outs = pl.pallas_call(
        body, name="attn_bwd", grid=(N_PAIRS,),
        in_specs=[col(0), col(8), col(16), col(24), col(0), col(0), col(0)] + [anyspec] * nsw,
        out_specs=[col(0)] * 4 + [anyspec] * nsw,
        out_shape=[SDS((s, D_ATTN), bf16)] * 4 + [SDS(a.shape, a.dtype) for a in swaps],
        scratch_shapes=[pltpu.VMEM((s, LANES), f32)] * 5
        + [pltpu.VMEM((BLK, 2 * LANES), f32)] * 2 + [pltpu.VMEM((BLK, LANES), f32)] * 2
        + [pltpu.VMEM((BLK, 2 * LANES), f32)] * 2 + [pltpu.VMEM((BLK, LANES), f32)] * 2
        + [pltpu.VMEM((BLK, 2 * LANES), bf16)] * 2 + [pltpu.VMEM((BLK, LANES), f32)] * 2
        + [pltpu.VMEM((BLK, 2 * LANES), bf16)] * 2 + [pltpu.VMEM((BLK, LANES), f32)] * 2
        + [pltpu.VMEM((2 * BLK, LANES), bf16)] * 2 + [pltpu.VMEM((BLK, LANES), bf16)] * 2
        + [pltpu.VMEM((2 * BLK, LANES), bf16)] * 2 + [pltpu.VMEM((BLK, LANES), bf16)] * 2
        + [pltpu.VMEM((2 * BLK, LANES), bf16)] * 2 + [pltpu.VMEM((BLK, LANES), f32)] * 2
        + [pltpu.SemaphoreType.DMA(())] * (2 * nsw),
        compiler_params=pltpu.CompilerParams(dimension_semantics=("arbitrary",)),
    )(proj, proj, proj, proj, o, lb, dmix, *swaps)
    return outs


def _ssd_bwd(proj, y, states, dmix, conv_w, conv_b, dtb16, alog16, alog_f, d_f, nw, chip_sums=()):
    s = proj.shape[0]
    nc = s // CHUNK
    gw = D_SSM // N_GROUPS
    nx = len(chip_sums)

    def body(*refs):
        (xs_ref, bc_ref, xs_tail, bc_tail, dt_ref, z_ref, y_ref, st_ref, dm_ref, cw_ref, cb_ref, dtb_ref,
         alog16_ref, alogf_ref, df_ref, nw_ref) = refs[:16]
        cs_in = refs[16:16 + nx]
        out_ref, gconv_ref, gvec_ref, gdt_ref = refs[16 + nx:20 + nx]
        cs_out = refs[20 + nx:20 + 2 * nx]
        (dh_scr, head_scr, xpad, dcpad, da_scr, dxdt_scr, dbc_scr, emat_ref, fold_ref) = refs[20 + 2 * nx:29 + 2 * nx]
        cs_sems = refs[29 + 2 * nx:]
        i = pl.program_id(0)
        c = nc - 1 - i

        if nx:
            @pl.when(i == 0)
            def _():
                mine, sends, _ = _chip_exchange_copies(cs_in, cs_out, *cs_sems)
                for cp in mine + sends:
                    cp.start()

            @pl.when(i == nc - 1)
            def _():
                mine, sends, recvs = _chip_exchange_copies(cs_in, cs_out, *cs_sems)
                for cp in recvs:
                    cp.wait_recv()
                for cp in sends:
                    cp.wait_send()
                for cp in mine:
                    cp.wait()

        @pl.when(i == 0)
        def _():
            emat_ref[...] = _expand_mat()
            fold_ref[...] = _fold_mat()
            dh_scr[...] = jnp.zeros_like(dh_scr)
            head_scr[...] = jnp.zeros_like(head_scr)
            gconv_ref[...] = jnp.zeros_like(gconv_ref)
            gvec_ref[...] = jnp.zeros_like(gvec_ref)
            gdt_ref[...] = jnp.zeros_like(gdt_ref)

        cv, sig, xbc, pre, dt_f, al_f, al_x, al_t, taps = _ssd_common(
            xs_ref, bc_ref, xs_tail, bc_tail, dt_ref, cw_ref, cb_ref, dtb_ref, alog16_ref, emat_ref, xpad, c == 0)
        head0 = _iota((CHUNK, LANES), 1) < HEAD_DIM
        sub = _iota((CHUNK, LANES), 0)
        last_row = sub == CHUNK - 1

        yv, z, dmx = y_ref[...], z_ref[...], dm_ref[...]
        sz = _sigmoid(z)
        silu = z * sz
        yz = yv * silu
        dyz_parts = []
        for g in range(N_GROUPS):
            gs = slice(g * gw, (g + 1) * gw)
            part = yz[:, gs]
            r = lax.rsqrt(jnp.mean(part * part, axis=-1, keepdims=True) + EPS)
            nh = part * r
            gvec_ref[0:1, gs] += jnp.sum(dmx[:, gs] * nh, axis=0, keepdims=True)
            dn = dmx[:, gs] * nw_ref[:, gs]
            dyz_parts.append(r * (dn - nh * jnp.mean(dn * nh, axis=-1, keepdims=True)))
        dyz = jnp.concatenate(dyz_parts, axis=1)
        dy = dyz * silu
        out_ref[:, 0:D_SSM] = (dyz * yv * (sz * (1.0 + z * (1.0 - sz)))).astype(bf16)

        x_all = xbc[:, 0:D_SSM]
        gvec_ref[2:3, :] += jnp.sum(dy * x_all, axis=0, keepdims=True)

        for g in range(N_GROUPS):
            bm = xbc[:, D_SSM + g * D_STATE: D_SSM + (g + 1) * D_STATE].astype(bf16)
            cm = xbc[:, D_SSM + (N_GROUPS + g) * D_STATE: D_SSM + (N_GROUPS + g + 1) * D_STATE].astype(bf16)
            gmat = _nt(cm, bm)
            dgm = jnp.zeros((CHUNK, CHUNK), f32)
            db = jnp.zeros((CHUNK, D_STATE), f32)
            dc = jnp.zeros((CHUNK, D_STATE), f32)
            for pair in range(4 * g, 4 * g + 4):
                sl = slice(pair * LANES, (pair + 1) * LANES)
                xp, dtp, alp, dyp = x_all[:, sl], dt_f[:, sl], al_f[:, sl], dy[:, sl]
                xdt = xp * dtp
                xdt16 = xdt.astype(bf16)
                al_last = alp[CHUNK - 1:CHUNK, :]
                e_l = jnp.exp(alp)
                wf = jnp.exp(al_last - alp)
                e_last = jnp.exp(al_last)
                hp = st_ref[:, sl]
                hp16 = hp.astype(bf16)
                dhn = dh_scr[:, sl]
                dhn16 = dhn.astype(bf16)
                y_off = e_l * _nn(cm, hp16)
                dch16 = (dyp * e_l).astype(bf16)
                dc = dc + _nt(dch16, hp16)
                dh_out = _tn(cm, dch16)
                dal = dyp * y_off
                xw16 = (wf * xdt).astype(bf16)
                db = db + _nt(xw16, dhn16)
                dxw = _nn(bm, dhn16)
                dxdt = dxw * wf
                dwf = dxw * xdt * wf
                dal = dal - dwf
                dal_last = jnp.sum(dwf, axis=0, keepdims=True) + jnp.sum(dhn * hp, axis=0, keepdims=True) * e_last
                dh_scr[:, sl] = e_last * dhn + dh_out
                for h in range(2):
                    mh = head0 if h == 0 else jnp.logical_not(head0)
                    dyh16 = jnp.where(mh, dyp, 0.0).astype(bf16)
                    lmat = _decay_mat(al_x, al_t, pair, h)
                    mm = gmat * lmat
                    dmm = _nt(dyh16, xdt16)
                    dxdt = dxdt + _tn(mm.astype(bf16), dyh16)
                    n16 = (dmm * mm).astype(bf16)
                    jh = jnp.where(mh, 1.0 / HEAD_DIM, 0.0).astype(bf16)
                    dal = dal + _nn(n16, jh) - _tn(n16, jh)
                    dgm = dgm + dmm * lmat
                da_scr[:, sl] = dal + jnp.where(last_row, dal_last, 0.0)
                dxdt_scr[:, sl] = dxdt
            dgm16 = dgm.astype(bf16)
            dbc_scr[:, g * D_STATE:(g + 1) * D_STATE] = db + _tn(dgm16, cm)
            dbc_scr[:, (N_GROUPS + g) * D_STATE:(N_GROUPS + g + 1) * D_STATE] = dc + _nn(dgm16, bm)

        sub_c, lane_c = _iota((CHUNK, CHUNK), 0), _iota((CHUNK, CHUNK), 1)
        tri_t = (lane_c >= sub_c).astype(bf16)
        dadt = _dot_01_left(tri_t, da_scr[...], 2)
        a_f = -jnp.exp(alogf_ref[...])
        dxdt_all = dxdt_scr[...]
        ddt_f = dxdt_all * x_all + a_f * dadt
        gvec_ref[1:2, :] += jnp.sum(dt_f * dadt, axis=0, keepdims=True) * a_f
        dx = df_ref[...] * dy + dxdt_all * dt_f
        ddt_raw = _dot_01(ddt_f, fold_ref[...], 2) * _sigmoid(pre)
        gdt_ref[0:1, :] += jnp.sum(ddt_raw, axis=0, keepdims=True)
        out_ref[:, D_SSM + D_CONV:D_SSM + D_CONV + LANES] = ddt_raw.astype(bf16)
        out_ref[:, D_SSM + D_CONV + LANES:] = jnp.zeros((CHUNK, 3 * LANES), bf16)

        dsil = sig * (1.0 + cv * (1.0 - sig))
        dcv_x = dx * dsil[:, 0:D_SSM]
        dcv_bc = dbc_scr[...] * dsil[:, D_SSM:]
        dcpad[0:CHUNK, 0:D_SSM] = dcv_x
        dcpad[0:CHUNK, D_SSM:] = dcv_bc
        dcpad[CHUNK:, :] = head_scr[...]
        dcp = dcpad[...]
        dcv = dcp[0:CHUNK]
        gconv_ref[4:5, :] += jnp.sum(dcv, axis=0, keepdims=True)
        draw = cw_ref[3:4, :] * dcv
        for j in range(4):
            gconv_ref[j:j + 1, :] += jnp.sum(dcv * taps[j], axis=0, keepdims=True)
        for j in range(3):
            draw = draw + cw_ref[j:j + 1, :] * pltpu.roll(dcp, CHUNK + 8 - (3 - j), 0)[0:CHUNK]
        head_scr[...] = dcv[0:8]
        out_ref[:, D_SSM:D_SSM + D_CONV] = draw.astype(bf16)

    order = lambda i: nc - 1 - i
    row = lambda w, cb=0: pl.BlockSpec((CHUNK, w), lambda i: (nc - 1 - i, cb))
    anyspec = pl.BlockSpec(memory_space=pl.ANY)
    outs = pl.pallas_call(
        body, name="ssd_bwd", grid=(nc,),
        in_specs=_ssd_in_specs(order) + [row(D_SSM), pl.BlockSpec((None, D_STATE, D_SSM), lambda i: (nc - 1 - i, 0, 0)),
                                         row(D_SSM, 1), _full((4, D_CONV)), _full((1, D_CONV)), _full((1, LANES)),
                                         _full((1, LANES)), _full((1, D_SSM)), _full((1, D_SSM)), _full((1, D_SSM))]
        + [anyspec] * nx,
        out_specs=[row(3072), _full((8, D_CONV)), _full((8, D_SSM)), _full((8, LANES))] + [anyspec] * nx,
        out_shape=[SDS((s, 3072), bf16), SDS((8, D_CONV), f32), SDS((8, D_SSM), f32), SDS((8, LANES), f32)]
        + [SDS(a.shape, a.dtype) for a in chip_sums],
        scratch_shapes=[pltpu.VMEM((D_STATE, D_SSM), f32), pltpu.VMEM((8, D_CONV), f32),
                        pltpu.VMEM((8 + CHUNK, D_CONV), f32), pltpu.VMEM((8 + CHUNK, D_CONV), f32),
                        pltpu.VMEM((CHUNK, D_SSM), f32), pltpu.VMEM((CHUNK, D_SSM), f32),
                        pltpu.VMEM((CHUNK, 2 * N_GROUPS * D_STATE), f32),
                        pltpu.VMEM((LANES, 2 * D_SSM), bf16), pltpu.VMEM((D_SSM, LANES), bf16)]
        + (_chip_exchange_scratch(nx) if nx else []),
        compiler_params=pltpu.CompilerParams(dimension_semantics=("arbitrary",)),
    )(proj, proj, proj, proj, proj, proj, y, states, dmix, conv_w, conv_b, dtb16, alog16, alog_f, d_f, nw, *chip_sums)
    return outs[0], outs[1], outs[2], outs[3], outs[4:]


def _col_blocks(parts, tile):
    counts = [p.shape[1] // tile for p in parts]
    offs = [sum(counts[:t]) for t in range(len(parts))]
    return offs, counts, sum(counts)


def _bcast_copies(src_ref, out_ref, send_sems, recv_sems, local_sem):
    x, y, c = _my_pos()
    me = 4 * x + 2 * y + c
    mine = pltpu.make_async_copy(src_ref, out_ref.at[me], local_sem)
    sends, recvs = [], []
    for k in range(1, N_DEV):
        to, frm = (me + k) % N_DEV, (me + N_DEV - k) % N_DEV
        sems = dict(send_sem=send_sems.at[k - 1], recv_sem=recv_sems.at[k - 1], device_id_type=MESH)
        sends.append(pltpu.make_async_remote_copy(
            src_ref=src_ref, dst_ref=out_ref.at[me], device_id=(to // 4, (to // 2) % 2, to % 2), **sems))
        recvs.append(pltpu.make_async_remote_copy(
            src_ref=src_ref, dst_ref=out_ref.at[frm], device_id=(x, y, c), **sems))
    return mine, sends, recvs


def _bcast_scratch():
    return [pltpu.SemaphoreType.DMA((N_DEV - 1,)), pltpu.SemaphoreType.DMA((N_DEV - 1,)), pltpu.SemaphoreType.DMA(())]


def _inproj_bwd(dparts, wt, x, nw, dres, chip_sums=(), pack=None):
    s, d = x.shape
    tm, tk = 1024, 1024
    offs, counts, nk = _col_blocks(dparts, tk)
    npart, nx = len(dparts), len(chip_sums)
    npk = 0 if pack is None else 1
    ni = s // tm

    def body(*refs):
        dp_refs = refs[:npart]
        w_ref, x_ref, nw_ref, dres_ref = refs[npart:npart + 4]
        pos = npart + 4
        cs_in, pos = refs[pos:pos + nx], pos + nx
        pack_in, pos = refs[pos:pos + npk], pos + npk
        (gx_ref, gnw_ref), pos = refs[pos:pos + 2], pos + 2
        cs_out, pos = refs[pos:pos + nx], pos + nx
        pack_out, pos = refs[pos:pos + 2 * npk], pos + 2 * npk
        acc, pos = refs[pos], pos + 1
        cs_sems, pos = refs[pos:pos + 3 * min(nx, 1)], pos + 3 * min(nx, 1)
        pk_refs = refs[pos:]
        i, k = pl.program_id(0), pl.program_id(1)

        def exchange():
            return _chip_exchange_copies(cs_in, cs_out, *cs_sems)

        def pack_copies():
            return _bcast_copies(pack_in[0], pack_out[0], *pk_refs[1:4])

        def gnw_copies():
            return _bcast_copies(pk_refs[0], pack_out[1], *pk_refs[4:7])

        @pl.when(jnp.logical_and(i == 0, k == 0))
        def _():
            gnw_ref[...] = jnp.zeros_like(gnw_ref)
            if nx:
                mine, sends, _ = exchange()
                for cp in mine + sends:
                    cp.start()
            if npk:
                mine, sends, _ = pack_copies()
                for cp in [mine] + sends:
                    cp.start()

        @pl.when(k == 0)
        def _():
            acc[...] = jnp.zeros_like(acc)

        for t in range(npart):
            @pl.when(jnp.logical_and(k >= offs[t], k < offs[t] + counts[t]))
            def _(t=t):
                acc[...] += _nn(dp_refs[t][...], w_ref[...])

        @pl.when(k == nk - 1)
        def _():
            xv = x_ref[...]
            r = lax.rsqrt(jnp.mean(xv * xv, axis=-1, keepdims=True) + EPS)
            xn = xv * r
            du = acc[...]
            gnw_ref[0:1, :] += jnp.sum(du * xn, axis=0, keepdims=True)
            dn = du * nw_ref[...]
            gx_ref[...] = dres_ref[...] + r * (dn - xn * jnp.mean(dn * xn, axis=-1, keepdims=True))

        @pl.when(jnp.logical_and(i == ni - 1, k == nk - 1))
        def _():
            if npk:
                pk_refs[0][...] = gnw_ref[...]
                mine, sends, _ = gnw_copies()
                for cp in [mine] + sends:
                    cp.start()
            if nx:
                mine, sends, recvs = exchange()
                for cp in recvs:
                    cp.wait_recv()
                for cp in sends:
                    cp.wait_send()
                for cp in mine:
                    cp.wait()
            if npk:
                for copies in (pack_copies(), gnw_copies()):
                    mine, sends, recvs = copies
                    for cp in recvs:
                        cp.wait_recv()
                    for cp in sends:
                        cp.wait_send()
                    mine.wait()

    def piece(t):
        return pl.BlockSpec((tm, tk), lambda i, k: (i, jnp.clip(k - offs[t], 0, counts[t] - 1)))

    anyspec = pl.BlockSpec(memory_space=pl.ANY)
    packs = [] if pack is None else [pack]
    pack_shapes = [] if pack is None else [SDS((N_DEV,) + pack.shape, f32), SDS((N_DEV, 8, d), f32)]
    scratch = [pltpu.VMEM((tm, d), f32)] + (_chip_exchange_scratch(nx) if nx else [])
    if npk:
        scratch += [pltpu.VMEM((8, d), f32)] + _bcast_scratch() + _bcast_scratch()
    outs = pl.pallas_call(
        body, name="inproj_bwd", grid=(ni, nk),
        in_specs=[piece(t) for t in range(npart)] + [
            pl.BlockSpec((tk, d), lambda i, k: (k, 0)),
            pl.BlockSpec((tm, d), lambda i, k: (i, 0)), pl.BlockSpec((1, d), lambda i, k: (0, 0)),
            pl.BlockSpec((tm, d), lambda i, k: (i, 0))] + [anyspec] * (nx + npk),
        out_specs=[pl.BlockSpec((tm, d), lambda i, k: (i, 0)), pl.BlockSpec((8, d), lambda i, k: (0, 0))]
        + [anyspec] * (nx + 2 * npk),
        out_shape=[SDS((s, d), f32), SDS((8, d), f32)] + [SDS(a.shape, a.dtype) for a in chip_sums] + pack_shapes,
        scratch_shapes=scratch,
        compiler_params=pltpu.CompilerParams(dimension_semantics=("arbitrary", "arbitrary")),
    )(*dparts, wt, x, nw, dres, *chip_sums, *packs)
    return outs[0], outs[1], outs[2:2 + nx], outs[2 + nx:]


def _matmul_tn(a_parts, b_parts, name):
    tile, tk = 1024, 1024
    s = a_parts[0].shape[0]
    nk = s // tk
    na, nb = len(a_parts), len(b_parts)
    offs_a, counts_a, ni = _col_blocks(a_parts, tile)
    offs_b, counts_b, nj = _col_blocks(b_parts, tile)

    def body(*refs):
        a_refs, b_refs, o_ref = refs[:na], refs[na:na + nb], refs[na + nb]
        i, j = pl.program_id(0), pl.program_id(1)

        @pl.when(pl.program_id(2) == 0)
        def _():
            o_ref[...] = jnp.zeros_like(o_ref)

        for ta in range(na):
            for tb in range(nb):
                in_a = jnp.logical_and(i >= offs_a[ta], i < offs_a[ta] + counts_a[ta])
                in_b = jnp.logical_and(j >= offs_b[tb], j < offs_b[tb] + counts_b[tb])

                @pl.when(jnp.logical_and(in_a, in_b))
                def _(ta=ta, tb=tb):
                    o_ref[...] += _tn(a_refs[ta][...], b_refs[tb][...])

    def spec(offs, counts, t, axis):
        def index(i, j, k):
            pos = (i, j)[axis]
            mine = jnp.logical_and(pos >= offs[t], pos < offs[t] + counts[t])
            return jnp.where(mine, k, 0), jnp.clip(pos - offs[t], 0, counts[t] - 1)
        return pl.BlockSpec((tk, tile), index)

    return pl.pallas_call(
        body, name=name, grid=(ni, nj, nk),
        in_specs=[spec(offs_a, counts_a, t, 0) for t in range(na)] + [spec(offs_b, counts_b, t, 1) for t in range(nb)],
        out_specs=pl.BlockSpec((tile, tile), lambda i, j, k: (i, j)),
        out_shape=SDS((ni * tile, nj * tile), f32),
        compiler_params=pltpu.CompilerParams(dimension_semantics=("parallel", "parallel", "arbitrary")),
    )(*a_parts, *b_parts)


def _adamw(w, g, m, v):
    m = ADAM_B1 * m + (1.0 - ADAM_B1) * g
    v = ADAM_B2 * v + (1.0 - ADAM_B2) * (g * g)
    m_hat = m / (1.0 - ADAM_B1 ** ADAM_STEP)
    v_hat = v / (1.0 - ADAM_B2 ** ADAM_STEP)
    delta = -ADAM_LR * (m_hat / (jnp.sqrt(v_hat) + ADAM_EPS) + ADAM_WD * w)
    return delta, m, v


def _sum_adamw(parts, w, m, v, name):
    r, c = w.shape
    tc = 256

    def body(p_ref, w_ref, m_ref, v_ref, g_ref, d_ref, nm_ref, nv_ref):
        g = p_ref[0].astype(f32)
        for q in range(1, 4):
            g = g + p_ref[q].astype(f32)
        g_ref[...] = g
        d_ref[...], nm_ref[...], nv_ref[...] = _adamw(w_ref[...], g, m_ref[...], v_ref[...])

    blk = pl.BlockSpec((r, tc), lambda i: (0, i))
    return pl.pallas_call(
        body, name=name, grid=(c // tc,),
        in_specs=[pl.BlockSpec((4, r, tc), lambda i: (0, 0, i)), blk, blk, blk],
        out_specs=[blk] * 4, out_shape=[SDS((r, c), f32)] * 4,
        compiler_params=pltpu.CompilerParams(dimension_semantics=("parallel",)),
    )(parts, w, m, v)


def _sum_small(parts, pre_blocks):
    def body(p_ref, b_ref, o_ref):
        t = p_ref[0]
        pre = b_ref[0]
        for j in range(1, N_DEV):
            t = t + p_ref[j]
            pre = pre + b_ref[j]
        o_ref[...] = t
        o_ref[5:6, 0:D_MODEL] = pre[0:1, :]
        row_h = _iota((D_SSM, LANES), 0) // HEAD_DIM
        fold = (row_h == _iota((D_SSM, LANES), 1)).astype(f32)
        lower = t[8:16, 0:LANES]
        folded = _nn_hi(t[8:16, 0:D_SSM], fold)
        loss = jnp.sum(t[11:12, 0:D_MODEL], axis=1, keepdims=True) * (0.5 / D_MODEL)
        row = _iota((8, LANES), 0)
        o_ref[8:16, 0:LANES] = jnp.where(row < 2, folded, jnp.where(row == 4, loss, lower))

    return pl.pallas_call(body, name="sum_small", out_shape=SDS((PACK_ROWS, PACK_W), f32),
                          in_specs=[pl.BlockSpec(memory_space=pltpu.VMEM)] * 2,
                          out_specs=pl.BlockSpec(memory_space=pltpu.VMEM))(parts, pre_blocks)


def _adamw_small(w, g, m, v):
    def body(w_ref, g_ref, m_ref, v_ref, d_ref, nm_ref, nv_ref):
        d_ref[...], nm_ref[...], nv_ref[...] = _adamw(w_ref[...], g_ref[...], m_ref[...], v_ref[...])

    vm = pl.BlockSpec(memory_space=pltpu.VMEM)
    return pl.pallas_call(body, name="adamw_small", out_shape=[SDS(w.shape, f32)] * 3,
                          in_specs=[vm] * 4, out_specs=[vm] * 3)(w, g, m, v)


def _pad_lanes(v, width):
    return jnp.pad(v, ((0, 0), (0, width - v.shape[1])))


def _local_step(x, tgt, norm_pre_w, wt, conv_w, conv_b, dt_bias, a_log, d_skip, ssm_norm_w, wo, norm_post_w, sharded):
    dtb16 = _pad_lanes(dt_bias, LANES)
    alog16 = _pad_lanes(a_log, LANES)
    alog_f = jnp.repeat(a_log, HEAD_DIM, axis=1)
    d_f = jnp.repeat(d_skip, HEAD_DIM, axis=1)

    shard_out = wo.shape[0]
    if sharded:
        proj, u, (g_out, g_cw) = _prenorm_inproj(x, norm_pre_w, wt, gather=(wo, conv_w))
        wo = g_out.reshape(N_DEV * shard_out, D_MODEL)
        conv_w = g_cw.transpose(1, 0, 2).reshape(4, D_CONV)
    else:
        proj, u, _ = _prenorm_inproj(x, norm_pre_w, wt)
    o, lb, mix_a = _attn_fwd(proj)
    mix_s, y, states = _ssd_fwd(proj, conv_w, conv_b, dtb16, alog16, alog_f, d_f, ssm_norm_w)
    dmix, dout, dres, acc_post = _outproj_loss(mix_a, mix_s, wo, x, tgt, norm_post_w)
    dw_out = _matmul_tn([mix_a, mix_s], [dout], "dw_out")
    ssd_args = (proj, y, states, dmix, conv_w, conv_b, dtb16, alog16, alog_f, d_f, ssm_norm_w)
    if sharded:
        dq, dk, dv, dg, got_out = _attn_bwd(proj, o, lb, dmix, swap=dw_out)
        chip_out = _chip_sum(dw_out, got_out, shard_out, "chip_sum_w_out")
        dzxd, g_conv, g_vec, g_dt, (parts_out,) = _ssd_bwd(*ssd_args, chip_sums=[chip_out])
    else:
        dq, dk, dv, dg = _attn_bwd(proj, o, lb, dmix)
        dzxd, g_conv, g_vec, g_dt, _ = _ssd_bwd(*ssd_args)
    dparts = [dq, dk, dv, dg, dzxd]

    def pack(g_pre_row):
        return jnp.concatenate(
            [g_conv[0:5], g_pre_row, _pad_lanes(g_vec[0:1], PACK_W), _pad_lanes(acc_post[1:2], PACK_W),
             _pad_lanes(g_vec[1:3], PACK_W), _pad_lanes(g_dt[0:1], PACK_W), _pad_lanes(acc_post[0:1], PACK_W),
             jnp.zeros((4, PACK_W), f32)], axis=0)

    if sharded:
        dw_in, got_in = _dw_in_swap(dparts, u)
        chip_in = _chip_sum(dw_in, got_in, D_IN_PROJ // N_DEV, "chip_sum_w_in")
        grad_x, _, (parts_in,), small = _inproj_bwd(dparts, wt, x, norm_pre_w, dres, [chip_in],
                                                    pack(jnp.zeros((1, PACK_W), f32)))
        return grad_x, (parts_in, parts_out), small
    dw_in = _matmul_tn(dparts, [u], "dw_in")
    grad_x, g_pre, _, _ = _inproj_bwd(dparts, wt, x, norm_pre_w, dres)
    return grad_x, (dw_in, dw_out), pack(_pad_lanes(g_pre[0:1], PACK_W))


def kernel(x, norm_pre_w, w_in, conv_w, conv_b, dt_bias, a_log, d_skip, ssm_norm_w, w_out, norm_post_w, loss_target, m_norm_pre_w, m_w_in, m_conv_w, m_conv_b, m_dt_bias, m_a_log, m_d_skip, m_ssm_norm_w, m_w_out, m_norm_post_w, v_norm_pre_w, v_w_in, v_conv_w, v_conv_b, v_dt_bias, v_a_log, v_d_skip, v_ssm_norm_w, v_w_out, v_norm_post_w):
    shard_in = w_in.shape[2]
    shard_cv = conv_w.shape[2]
    me = 4 * lax.axis_index("x") + 2 * lax.axis_index("y") + lax.axis_index("c")

    g_in, = _all_gather([w_in[0].T.astype(bf16)])
    wt = _assemble_wt(g_in)

    grad_x, (parts_in, parts_out), (parts_small, pre_blocks) = _local_step(
        x[0], loss_target[0], norm_pre_w, wt, conv_w[0], conv_b, dt_bias, a_log, d_skip, ssm_norm_w,
        w_out[0].astype(bf16), norm_post_w, sharded=True)

    g_w_in, d_w_in, nm_w_in, nv_w_in = (a.T for a in _sum_adamw(
        parts_in, w_in[0].T, m_w_in[0].T, v_w_in[0].T, "sum_adamw_w_in"))
    g_w_out, d_w_out, nm_w_out, nv_w_out = _sum_adamw(parts_out, w_out[0], m_w_out[0], v_w_out[0], "sum_adamw_w_out")
    tot = _sum_small(parts_small, pre_blocks)

    g_cw_all = tot[0:4]
    small_g = {
        "conv_w": lax.dynamic_slice(g_cw_all, (0, me * shard_cv), (4, shard_cv)),
        "conv_b": tot[4:5], "norm_pre_w": tot[5:6, :D_MODEL], "ssm_norm_w": tot[6:7, :D_SSM],
        "norm_post_w": tot[7:8, :D_MODEL], "a_log": tot[8:9, :16], "d_skip": tot[9:10, :16], "dt_bias": tot[10:11, :16],
    }
    loss = tot[12, 0]
    small_w = {"conv_w": (conv_w[0], m_conv_w[0], v_conv_w[0]), "conv_b": (conv_b, m_conv_b, v_conv_b),
               "norm_pre_w": (norm_pre_w, m_norm_pre_w, v_norm_pre_w), "ssm_norm_w": (ssm_norm_w, m_ssm_norm_w, v_ssm_norm_w),
               "norm_post_w": (norm_post_w, m_norm_post_w, v_norm_post_w), "a_log": (a_log, m_a_log, v_a_log),
               "d_skip": (d_skip, m_d_skip, v_d_skip), "dt_bias": (dt_bias, m_dt_bias, v_dt_bias)}
    names = list(small_w)
    sizes = [small_g[k].size for k in names]
    tot_size = sum(sizes)
    pad_to = -(-tot_size // 1024) * 1024

    def flat(arrs):
        v = jnp.concatenate([a.reshape(-1) for a in arrs])
        return jnp.pad(v, (0, pad_to - tot_size)).reshape(pad_to // LANES, LANES)

    fw = flat([small_w[k][0] for k in names])
    fg = flat([small_g[k] for k in names])
    fm = flat([small_w[k][1] for k in names])
    fv = jnp.pad(jnp.concatenate([small_w[k][2].reshape(-1) for k in names]), (0, pad_to - tot_size),
                 constant_values=1.0).reshape(pad_to // LANES, LANES)
    fd, fnm, fnv = _adamw_small(fw, fg, fm, fv)

    def unflat(f):
        out, off = {}, 0
        v = f.reshape(-1)
        for k, n in zip(names, sizes):
            out[k] = v[off:off + n].reshape(small_g[k].shape)
            off += n
        return out

    sd, snm, snv = unflat(fd), unflat(fnm), unflat(fnv)
    lead = lambda a: a[None]
    order = ["norm_pre_w", "w_in", "conv_w", "conv_b", "dt_bias", "a_log", "d_skip", "ssm_norm_w", "w_out", "norm_post_w"]
    grads = dict(small_g, w_in=g_w_in, w_out=g_w_out)
    deltas = dict(sd, w_in=d_w_in, w_out=d_w_out)
    new_m = dict(snm, w_in=nm_w_in, w_out=nm_w_out)
    new_v = dict(snv, w_in=nv_w_in, w_out=nv_w_out)

    def shaped(dct, k):
        a = dct[k]
        return lead(a) if k in ("w_in", "w_out", "conv_w") else a

    return (loss, grad_x[None], *[shaped(grads, k) for k in order], *[shaped(deltas, k) for k in order],
            *[shaped(new_m, k) for k in order], *[shaped(new_v, k) for k in order])
```

```python
import functools
import math

import jax
import jax.numpy as jnp
import numpy as np
from jax import lax
from jax.experimental import pallas as pl
from jax.experimental.pallas import tpu as pltpu

f32, bf16 = jnp.float32, jnp.bfloat16
SDS = jax.ShapeDtypeStruct
HIGHEST = lax.Precision.HIGHEST
MESH = pl.DeviceIdType.MESH

N_DEV = 8
D_MODEL = 1024
D_ATTN = 1024
D_SSM = 1024
HEAD_DIM = 64
N_PAIRS = 8
D_STATE = 128
N_GROUPS = 2
D_CONV = D_SSM + 2 * N_GROUPS * D_STATE
D_IN_PROJ = 4 * D_ATTN + D_SSM + D_CONV + 16
NP = 7168
CHUNK = 128
BLK = 128
DILATIONS = (1, 4, 16)
EPS = 1e-6
LANES = 128
COL_Z, COL_XS, COL_BC, COL_DT = 4096, 5120, 6144, 6656

ADAM_LR, ADAM_B1, ADAM_B2, ADAM_EPS, ADAM_WD, ADAM_STEP = 0.001, 0.9, 0.999, 1e-08, 0.01, 10

PACK_ROWS, PACK_W = 16, 1536


def _nt(a, b):
    return lax.dot_general(a, b, (((1,), (1,)), ((), ())), preferred_element_type=f32)


def _tn(a, b):
    return lax.dot_general(a, b, (((0,), (0,)), ((), ())), preferred_element_type=f32)


def _nn(a, b):
    return jnp.dot(a, b, preferred_element_type=f32)


def _nn_hi(a, b):
    return jnp.dot(a, b, precision=HIGHEST, preferred_element_type=f32)


def _sigmoid(x):
    return 1.0 / (1.0 + jnp.exp(-x))


def _softplus(x):
    return jnp.maximum(x, 0.0) + jnp.log1p(jnp.exp(-jnp.abs(x)))


def _iota(shape, dim):
    return lax.broadcasted_iota(jnp.int32, shape, dim)


def _my_pos():
    return lax.axis_index("x"), lax.axis_index("y"), lax.axis_index("c")


GATHER_SEMS = 9


def _gather_phases(ins, outs, send_sems, recv_sems, local_sems):
    n, ns = len(ins), GATHER_SEMS
    x, y, c = _my_pos()
    me, sibling = (x, y, c), (x, y, 1 - c)
    xn, yn, diag = (1 - x, y), (x, 1 - y), (1 - x, 1 - y)

    def slot(a, px, py, pc):
        return outs[a].at[4 * px + 2 * py + pc]

    def part(a, ref, h):
        width = ins[a].shape[-1]
        if width % (2 * LANES):
            return ref if h == 1 else None
        return ref.at[:, pl.ds(h * (width // 2), width // 2)]

    def copy(a, k, block, to, src=None, h=None):
        src_ref = slot(a, *block) if src is None else src
        dst_ref = slot(a, *block)
        if h is not None:
            src_ref, dst_ref = part(a, src_ref, h), part(a, dst_ref, h)
            if src_ref is None:
                return None
        return pltpu.make_async_remote_copy(
            src_ref=src_ref, dst_ref=dst_ref, send_sem=send_sems.at[ns * a + k], recv_sem=recv_sems.at[ns * a + k],
            device_id=to, device_id_type=MESH)

    def mine():
        return [pltpu.make_async_copy(ins[a], slot(a, *me), local_sems.at[a]) for a in range(n)]

    def own_sends(a):
        return [copy(a, 0, me, sibling, src=ins[a]), copy(a, 1, me, (*xn, c), src=ins[a]),
                copy(a, 2, me, (*yn, c), src=ins[a])]

    def neighbour_relays(a):
        return [copy(a, 4, (*xn, c), sibling), copy(a, 7, (*xn, c), (*yn, c), h=1),
                copy(a, 5, (*yn, c), sibling), copy(a, 8, (*yn, c), (*xn, c), h=0)]

    def diagonal_halves(a):
        return [copy(a, k, (*diag, c), me, h=h) for k, h in ((8, 0), (7, 1))]

    def start_all(cps):
        for cp in cps:
            if cp is not None:
                cp.start()

    def phase0():
        start_all(mine())
        for a in range(n):
            start_all(own_sends(a))

    def phase1():
        for a in range(n):
            copy(a, 1, (*xn, c), me).wait_recv()
            copy(a, 2, (*yn, c), me).wait_recv()
            start_all(neighbour_relays(a))

    def phase2():
        for a in range(n):
            for cp in diagonal_halves(a):
                if cp is not None:
                    cp.wait_recv()
            copy(a, 6, (*diag, c), sibling).start()

    def finish():
        for a in range(n):
            copy(a, 0, sibling, me).wait_recv()
            for j, chip in enumerate((xn, yn, diag)):
                copy(a, 4 + j, (*chip, 1 - c), me).wait_recv()
        for a in range(n):
            for cp in own_sends(a) + neighbour_relays(a) + [copy(a, 6, (*diag, c), sibling)]:
                if cp is not None:
                    cp.wait_send()
        for cp in mine():
            cp.wait()

    return phase0, phase1, phase2, finish


def _gather_scratch(n):
    return [pltpu.SemaphoreType.DMA((GATHER_SEMS * n,)), pltpu.SemaphoreType.DMA((GATHER_SEMS * n,)),
            pltpu.SemaphoreType.DMA((n,))]


def _all_gather(arrs):
    n = len(arrs)

    def body(*refs):
        for phase in _gather_phases(refs[:n], refs[n:2 * n], *refs[2 * n:]):
            phase()

    anyspec = pl.BlockSpec(memory_space=pl.ANY)
    return pl.pallas_call(
        body, name="weights_all_gather",
        out_shape=[SDS((N_DEV,) + a.shape, a.dtype) for a in arrs],
        in_specs=[anyspec] * n, out_specs=[anyspec] * n, scratch_shapes=_gather_scratch(n),
    )(*arrs)


def _dw_in_swap(a_parts, u):
    tile, tk = 1024, 1024
    s = u.shape[0]
    nk = s // tk
    na = len(a_parts)
    offs, counts, ni = _col_blocks(a_parts, tile)

    def body(*refs):
        a_refs, u_ref = refs[:na], refs[na]
        dw_ref, got_ref = refs[na + 1:na + 3]
        acc, local_sems, send_sems, recv_sem = refs[na + 3:]
        i, k = pl.program_id(0), pl.program_id(1)
        x, y, c = _my_pos()
        par = i % 2

        def tile_copies(t, p):
            rows = pl.ds(pl.multiple_of(t * tile, tile), tile)
            loc = pltpu.make_async_copy(acc.at[p], dw_ref.at[rows], local_sems.at[p])
            rem = pltpu.make_async_remote_copy(
                src_ref=acc.at[p], dst_ref=got_ref.at[rows], send_sem=send_sems.at[p], recv_sem=recv_sem,
                device_id=(x, y, 1 - c), device_id_type=MESH)
            return loc, rem

        @pl.when(k == 0)
        def _():
            @pl.when(i >= 2)
            def _():
                loc, rem = tile_copies(i - 2, par)
                loc.wait()
                rem.wait_send()

        for t in range(na):
            @pl.when(jnp.logical_and(i >= offs[t], i < offs[t] + counts[t]))
            def _(t=t):
                prod = _tn(a_refs[t][...], u_ref[...])

                @pl.when(k == 0)
                def _():
                    acc[par] = prod

                @pl.when(k > 0)
                def _():
                    acc[par] += prod

        @pl.when(k == nk - 1)
        def _():
            loc, rem = tile_copies(i, par)
            loc.start()
            rem.start()

        @pl.when(jnp.logical_and(i == ni - 1, k == nk - 1))
        def _():
            for t in (ni - 2, ni - 1):
                loc, rem = tile_copies(t, t % 2)
                loc.wait()
                rem.wait_send()
            pltpu.make_async_remote_copy(src_ref=dw_ref, dst_ref=got_ref, send_sem=send_sems.at[0], recv_sem=recv_sem,
                                         device_id=(x, y, c), device_id_type=MESH).wait_recv()

    def a_spec(t):
        def index(i, k):
            mine = jnp.logical_and(i >= offs[t], i < offs[t] + counts[t])
            return jnp.where(mine, k, 0), jnp.clip(i - offs[t], 0, counts[t] - 1)
        return pl.BlockSpec((tk, tile), index)

    anyspec = pl.BlockSpec(memory_space=pl.ANY)
    return pl.pallas_call(
        body, name="dw_in_swap", grid=(ni, nk),
        in_specs=[a_spec(t) for t in range(na)] + [pl.BlockSpec((tk, tile), lambda i, k: (k, 0))],
        out_specs=[anyspec] * 2,
        out_shape=[SDS((ni * tile, tile), f32), SDS((ni * tile, tile), f32)],
        scratch_shapes=[pltpu.VMEM((2, tile, tile), f32), pltpu.SemaphoreType.DMA((2,)), pltpu.SemaphoreType.DMA((2,)),
                        pltpu.SemaphoreType.DMA(())],
        compiler_params=pltpu.CompilerParams(dimension_semantics=("arbitrary", "arbitrary")),
    )(*a_parts, u)


def _chip_sum(mine, got, rows, name):
    r, cdim = mine.shape
    tc = LANES

    def body(m_ref, g_ref, s16_ref):
        c = lax.axis_index("c")
        for q in range(4):
            blk = pl.ds(rows * (2 * q + c), rows)
            s16_ref[q] = (m_ref[blk, :] + g_ref[blk, :]).astype(bf16)

    col = pl.BlockSpec((r, tc), lambda i: (0, i))
    return pl.pallas_call(
        body, name=name, grid=(cdim // tc,), in_specs=[col, col],
        out_specs=pl.BlockSpec((4, rows, tc), lambda i: (0, 0, i)), out_shape=SDS((4, rows, cdim), bf16),
        compiler_params=pltpu.CompilerParams(dimension_semantics=("parallel",)),
    )(mine, got)


def _assemble_wt(shards):
    nd, rows, cdim = shards.shape
    tc = 256

    def body(g_ref, o_ref):
        for j in range(nd):
            o_ref[pl.ds(rows * j, rows), :] = g_ref[j]
        o_ref[pl.ds(nd * rows, NP - nd * rows), :] = jnp.zeros((NP - nd * rows, tc), shards.dtype)

    return pl.pallas_call(
        body, name="assemble_w_in", grid=(cdim // tc,),
        in_specs=[pl.BlockSpec((nd, rows, tc), lambda i: (0, 0, i))],
        out_specs=pl.BlockSpec((NP, tc), lambda i: (0, i)), out_shape=SDS((NP, cdim), shards.dtype),
        compiler_params=pltpu.CompilerParams(dimension_semantics=("parallel",)),
    )(shards)


def _chip_exchange_copies(ins, outs, send_sems, recv_sems, local_sems):
    nb = len(ins)
    x, y, c = _my_pos()
    my_q = 2 * x + y
    mine = [pltpu.make_async_copy(ins[a].at[my_q], outs[a].at[my_q], local_sems.at[a]) for a in range(nb)]
    sends, recvs = [], []
    for k in range(1, 4):
        to, frm = (my_q + k) % 4, (my_q + 4 - k) % 4
        for a in range(nb):
            sems = dict(send_sem=send_sems.at[3 * a + k - 1], recv_sem=recv_sems.at[3 * a + k - 1], device_id_type=MESH)
            sends.append(pltpu.make_async_remote_copy(
                src_ref=ins[a].at[to], dst_ref=outs[a].at[my_q], device_id=(to // 2, to % 2, c), **sems))
            recvs.append(pltpu.make_async_remote_copy(
                src_ref=ins[a].at[frm], dst_ref=outs[a].at[frm], device_id=(x, y, c), **sems))
    return mine, sends, recvs


def _chip_exchange_scratch(nb):
    return [pltpu.SemaphoreType.DMA((3 * nb,)), pltpu.SemaphoreType.DMA((3 * nb,)), pltpu.SemaphoreType.DMA((nb,))]


def _prenorm_inproj(x, nw, wt, gather=()):
    s, d = x.shape
    npad = wt.shape[0]
    tm, tn = 1024, 1024
    ng = len(gather)
    ni, nj = s // tm, npad // tn

    def body(x_ref, nw_ref, w_ref, *refs):
        g_in, (proj_ref, u_ref), g_out, sems = refs[:ng], refs[ng:ng + 2], refs[ng + 2:2 * ng + 2], refs[2 * ng + 2:]
        i, j = pl.program_id(0), pl.program_id(1)
        if ng:
            phases = _gather_phases(g_in, g_out, *sems)
            for step, phase in enumerate(phases[:3]):
                @pl.when(jnp.logical_and(i == step, j == 0))
                def _(phase=phase):
                    phase()

        @pl.when(j == 0)
        def _():
            xv = x_ref[...]
            r = lax.rsqrt(jnp.mean(xv * xv, axis=-1, keepdims=True) + EPS)
            u_ref[...] = (xv * r * nw_ref[...]).astype(bf16)
        proj_ref[...] = _nt(u_ref[...], w_ref[...])

        if ng:
            @pl.when(jnp.logical_and(i == ni - 1, j == nj - 1))
            def _():
                phases[3]()

    anyspec = pl.BlockSpec(memory_space=pl.ANY)
    outs = pl.pallas_call(
        body, name="prenorm_inproj", grid=(ni, nj),
        in_specs=[pl.BlockSpec((tm, d), lambda i, j: (i, 0)), pl.BlockSpec((1, d), lambda i, j: (0, 0)),
                  pl.BlockSpec((tn, d), lambda i, j: (j, 0))] + [anyspec] * ng,
        out_specs=[pl.BlockSpec((tm, tn), lambda i, j: (i, j)), pl.BlockSpec((tm, d), lambda i, j: (i, 0))]
        + [anyspec] * ng,
        out_shape=[SDS((s, npad), f32), SDS((s, d), bf16)] + [SDS((N_DEV,) + a.shape, a.dtype) for a in gather],
        scratch_shapes=_gather_scratch(ng) if ng else [],
        compiler_params=pltpu.CompilerParams(dimension_semantics=("arbitrary", "arbitrary")),
    )(x, nw, wt, *gather)
    return outs[0], outs[1], outs[2:]


def _attn_consts():
    head0 = _iota((BLK, LANES), 1) < HEAD_DIM
    tri2 = (_iota((BLK, 2 * LANES), 1) % LANES) <= _iota((BLK, 2 * LANES), 0)
    ones2 = ((_iota((LANES, 2 * LANES), 0) < HEAD_DIM) == (_iota((LANES, 2 * LANES), 1) < LANES)).astype(bf16)
    rmat = ((_iota((2 * LANES, LANES), 0) < LANES) == (_iota((2 * LANES, LANES), 1) < HEAD_DIM)).astype(bf16)
    bones = ((_iota((LANES, LANES), 0) < HEAD_DIM) == (_iota((LANES, LANES), 1) < HEAD_DIM)).astype(bf16)
    return head0, tri2, ones2, rmat, bones


def _stack_heads(x16, head0):
    zero = jnp.zeros_like(x16)
    return jnp.concatenate([jnp.where(head0, x16, zero), jnp.where(head0, zero, x16)], axis=0)


def _bf16_terms(x, terms):
    out = []
    for _ in range(terms):
        t = x.astype(bf16)
        out.append(t)
        x = x - t.astype(f32)
    return out


def _dot_01(x, w16, terms):
    return _nn(jnp.concatenate(_bf16_terms(x, terms), axis=1), jnp.concatenate([w16] * terms, axis=0))


def _split_dot(x, w16):
    return _dot_01(x, w16, 2)


def _split_dot_sum(x, w16):
    hi, lo = _bf16_terms(x, 2)
    return _nn(hi, w16) + _nn(lo, w16)


def _dot_01_left(w16, x, terms):
    return _nn(jnp.concatenate([w16] * terms, axis=1), jnp.concatenate(_bf16_terms(x, terms), axis=0))


def _attn_fwd(proj):
    s = proj.shape[0]
    n_it = s // BLK

    def body(q_ref, k_ref, v_ref, g_ref, o_ref, l_ref, mix_ref, op0, op1, op2, lp0, lp1, lp2,
             s_a, s_b, sd_a, sd_b, p_a, p_b, m_a, m_b, pd_a, pd_b, k_a, k_b, v_a, v_b):
        op_refs, lp_refs = (op0, op1, op2), (lp0, lp1, lp2)
        head0, tri2, ones2, rmat, _ = _attn_consts()
        score_bufs, prob_bufs = ((s_a, sd_a), (s_b, sd_b)), ((p_a, m_a, pd_a), (p_b, m_b, pd_b))
        k_bufs, v_bufs = (k_a, k_b), (v_a, v_b)
        for buf in k_bufs + v_bufs:
            buf[...] = jnp.zeros_like(buf)

        def block_rows(i, d, nb):
            r, blk = i // nb, i % nb
            return pl.ds(blk * (BLK * d) + r, BLK, stride=d), blk > 0

        def unstack(st16):
            return st16[:BLK] + st16[BLK:]

        def scores(i, par, d, nb):
            rows, has_prev = block_rows(i, d, nb)
            s_buf, sd_buf = score_bufs[par]
            qs = q_ref[rows, :] * 0.125
            qs16 = qs.astype(bf16)
            kst_c = _stack_heads(k_ref[rows, :].astype(bf16), head0)
            kst_p = k_bufs[1 - par][...]
            k_bufs[par][...] = kst_c
            sc = _nt(qs16, kst_c)
            sp = _nt(qs16, kst_p)
            s_buf[...] = jnp.where(tri2, sc, jnp.where(has_prev, sp, -jnp.inf))
            sd = _nn((qs * unstack(kst_p).astype(f32)).astype(bf16), ones2)
            sd_buf[...] = jnp.where(has_prev, sd, -jnp.inf)

        def softmax(bufs_in, bufs_out):
            s_buf, sd_buf = bufs_in
            p_buf, m_buf, pd_buf = bufs_out
            sc, sd2 = s_buf[...], sd_buf[...]
            m0 = jnp.max(sc[:, :LANES], axis=1, keepdims=True)
            m1 = jnp.max(sc[:, LANES:], axis=1, keepdims=True)
            m2 = jnp.concatenate([jnp.broadcast_to(m0, (BLK, LANES)), jnp.broadcast_to(m1, (BLK, LANES))], axis=1)
            m2 = jnp.maximum(m2, sd2)
            p_buf[...] = jnp.exp(sc - m2).astype(bf16)
            m_pair = jnp.where(head0, m2[:, :LANES], m2[:, LANES:])
            m_buf[...] = m_pair
            pd_buf[...] = jnp.exp(jnp.where(head0, sd2[:, :LANES], sd2[:, LANES:]) - m_pair)

        def output(i, par, d, nb, p):
            rows, _ = block_rows(i, d, nb)
            p_buf, m_buf, pd_buf = prob_bufs[par]
            vst_c = _stack_heads(v_ref[rows, :].astype(bf16), head0)
            vst_p = v_bufs[1 - par][...]
            v_bufs[par][...] = vst_c
            pt16, pd = p_buf[...], pd_buf[...]
            zero = jnp.zeros_like(pt16)
            o = (_nn(jnp.where(tri2, pt16, zero), vst_c) + _nn(jnp.where(tri2, zero, pt16), vst_p)
                 + pd * unstack(vst_p).astype(f32))
            l = _nn(pt16, rmat) + pd
            op_refs[p][rows, :] = o / l
            lp_refs[p][rows, :] = m_buf[...] + jnp.log(l)

        for p, d in enumerate(DILATIONS):
            nb = s // (BLK * d)
            scores(0, 0, d, nb)
            scores(1, 1, d, nb)
            softmax(score_bufs[0], prob_bufs[0])

            def steps(j, carry, d=d, nb=nb, p=p):
                for par in range(2):
                    t = 2 * j + 2 + par
                    scores(t, par, d, nb)
                    output(t - 2, par, d, nb, p)
                    softmax(score_bufs[1 - par], prob_bufs[1 - par])
                return carry

            lax.fori_loop(0, (n_it - 2) // 2, steps, 0)
            output(n_it - 2, 0, d, nb, p)
            softmax(score_bufs[1], prob_bufs[1])
            output(n_it - 1, 1, d, nb, p)

        def merge(i, carry):
            rows = pl.ds(pl.multiple_of(i * 256, 256), 256)
            l0, l1, l2 = lp0[rows, :], lp1[rows, :], lp2[rows, :]
            m = jnp.maximum(jnp.maximum(l0, l1), l2)
            e0, e1, e2 = jnp.exp(l0 - m), jnp.exp(l1 - m), jnp.exp(l2 - m)
            z = e0 + e1 + e2
            o = (e0 * op0[rows, :] + e1 * op1[rows, :] + e2 * op2[rows, :]) / z
            o_ref[rows, :] = o
            l_ref[rows, :] = m + jnp.log(z)
            g = g_ref[rows, :]
            mix_ref[rows, :] = (o * (g * _sigmoid(g))).astype(bf16)
            return carry

        lax.fori_loop(0, s // 256, merge, 0)

    col = lambda base: pl.BlockSpec((s, LANES), lambda h: (0, base + h))
    return pl.pallas_call(
        body, name="attn_fwd", grid=(N_PAIRS,),
        in_specs=[col(0), col(8), col(16), col(24)],
        out_specs=[col(0), col(0), col(0)],
        out_shape=[SDS((s, D_ATTN), f32), SDS((s, D_ATTN), f32), SDS((s, D_ATTN), bf16)],
        scratch_shapes=[pltpu.VMEM((s, LANES), f32)] * 6 + [pltpu.VMEM((BLK, 2 * LANES), f32)] * 4
        + [pltpu.VMEM((BLK, 2 * LANES), bf16)] * 2 + [pltpu.VMEM((BLK, LANES), f32)] * 4
        + [pltpu.VMEM((2 * BLK, LANES), bf16)] * 4,
        compiler_params=pltpu.CompilerParams(dimension_semantics=("parallel",)),
    )(proj, proj, proj, proj)


def _expand_mat():
    colv = _iota((LANES, 2 * D_SSM), 1)
    head = 2 * ((colv % D_SSM) // LANES) + colv // D_SSM
    return (_iota((LANES, 2 * D_SSM), 0) == head).astype(bf16)


def _fold_mat():
    return (_iota((D_SSM, LANES), 0) // HEAD_DIM == _iota((D_SSM, LANES), 1)).astype(bf16)


def _ssd_common(xs_ref, bc_ref, xs_tail, bc_tail, dt_ref, cw_ref, cb_ref, dtb_ref, alog16_ref, emat_ref, xpad, first):
    keep = jnp.where(first, 0.0, 1.0)
    xpad[0:8, 0:D_SSM] = xs_tail[...] * keep
    xpad[0:8, D_SSM:D_CONV] = bc_tail[...] * keep
    xpad[8:8 + CHUNK, 0:D_SSM] = xs_ref[...]
    xpad[8:8 + CHUNK, D_SSM:D_CONV] = bc_ref[...]
    xp = xpad[...]
    taps = [pltpu.roll(xp, 3 - j, 0)[8:8 + CHUNK] for j in range(3)] + [xp[8:8 + CHUNK]]
    cv = cb_ref[...] + cw_ref[0:1, :] * taps[0]
    for j in range(1, 4):
        cv = cv + cw_ref[j:j + 1, :] * taps[j]
    sig = _sigmoid(cv)
    xbc = cv * sig

    pre = dt_ref[...] + dtb_ref[...]
    dt16 = _softplus(pre)
    a16 = -jnp.exp(alog16_ref[...])
    sub, lane = _iota((CHUNK, CHUNK), 0), _iota((CHUNK, CHUNK), 1)
    tri = (sub >= lane).astype(f32)
    al16 = _nn_hi(tri, dt16 * a16)
    al_t = al16.T
    emat = emat_ref[...]
    dt_x = _dot_01(dt16, emat, 3)
    al_x = _dot_01(al16, emat, 3)
    lane_w = _iota((CHUNK, D_SSM), 1)
    even = (lane_w % LANES) < HEAD_DIM
    dt_f = jnp.where(even, dt_x[:, :D_SSM], dt_x[:, D_SSM:])
    al_f = jnp.where(even, al_x[:, :D_SSM], al_x[:, D_SSM:])
    return cv, sig, xbc, pre, dt_f, al_f, al_x, al_t, taps


def _decay_mat(al_x, al_t, pair, h):
    sub, lane = _iota((CHUNK, CHUNK), 0), _iota((CHUNK, CHUNK), 1)
    col = al_x[:, h * D_SSM + pair * LANES: h * D_SSM + (pair + 1) * LANES]
    row = al_t[2 * pair + h: 2 * pair + h + 1, :]
    return jnp.exp(jnp.where(sub >= lane, col - row, -jnp.inf))


def _ssd_in_specs(order):
    blk = lambda w, cb: pl.BlockSpec((CHUNK, w), lambda i: (order(i), cb))
    tail = lambda w, cb: pl.BlockSpec((8, w), lambda i: (jnp.maximum(16 * order(i) - 1, 0), cb))
    return [blk(D_SSM, COL_XS // D_SSM), blk(512, COL_BC // 512), tail(D_SSM, COL_XS // D_SSM),
            tail(512, COL_BC // 512), blk(LANES, COL_DT // LANES), blk(D_SSM, COL_Z // D_SSM)]


def _full(shape):
    return pl.BlockSpec(shape, lambda i: (0,) * len(shape))


def _ssd_fwd(proj, conv_w, conv_b, dtb16, alog16, alog_f, d_f, nw):
    s = proj.shape[0]
    nc = s // CHUNK

    def body(xs_ref, bc_ref, xs_tail, bc_tail, dt_ref, z_ref, cw_ref, cb_ref, dtb_ref, alog16_ref, alogf_ref,
             df_ref, nw_ref, mix_ref, y_ref, st_ref, h_scr, xpad, y_scr, emat_ref):
        c = pl.program_id(0)

        @pl.when(c == 0)
        def _():
            h_scr[...] = jnp.zeros_like(h_scr)
            emat_ref[...] = _expand_mat()

        _, _, xbc, _, dt_f, al_f, al_x, al_t, _ = _ssd_common(
            xs_ref, bc_ref, xs_tail, bc_tail, dt_ref, cw_ref, cb_ref, dtb_ref, alog16_ref, emat_ref, xpad, c == 0)
        head0 = _iota((CHUNK, LANES), 1) < HEAD_DIM
        st_ref[...] = h_scr[...]
        for g in range(N_GROUPS):
            bm = xbc[:, D_SSM + g * D_STATE: D_SSM + (g + 1) * D_STATE].astype(bf16)
            cm = xbc[:, D_SSM + (N_GROUPS + g) * D_STATE: D_SSM + (N_GROUPS + g + 1) * D_STATE].astype(bf16)
            gmat = _nt(cm, bm)
            for pair in range(4 * g, 4 * g + 4):
                sl = slice(pair * LANES, (pair + 1) * LANES)
                xp, dtp, alp = xbc[:, sl], dt_f[:, sl], al_f[:, sl]
                xdt = xp * dtp
                xdt16 = xdt.astype(bf16)
                al_last = alp[CHUNK - 1:CHUNK, :]
                hp = h_scr[:, sl]
                y_off = jnp.exp(alp) * _nn(cm, hp.astype(bf16))
                yd = [_nn((gmat * _decay_mat(al_x, al_t, pair, h)).astype(bf16), xdt16) for h in range(2)]
                y_scr[:, sl] = jnp.where(head0, yd[0], yd[1]) + y_off + df_ref[:, sl] * xp
                st = _tn(bm, (jnp.exp(al_last - alp) * xdt).astype(bf16))
                h_scr[:, sl] = jnp.exp(al_last) * hp + st
        y = y_scr[...]
        y_ref[...] = y
        z = z_ref[...]
        yz = y * (z * _sigmoid(z))
        gw = D_SSM // N_GROUPS
        for g in range(N_GROUPS):
            part = yz[:, g * gw:(g + 1) * gw]
            r = lax.rsqrt(jnp.mean(part * part, axis=-1, keepdims=True) + EPS)
            mix_ref[:, g * gw:(g + 1) * gw] = (part * r * nw_ref[:, g * gw:(g + 1) * gw]).astype(bf16)

    order = lambda i: i
    row = lambda w: pl.BlockSpec((CHUNK, w), lambda i: (i, 0))
    return pl.pallas_call(
        body, name="ssd_fwd", grid=(nc,),
        in_specs=_ssd_in_specs(order) + [_full((4, D_CONV)), _full((1, D_CONV)), _full((1, LANES)), _full((1, LANES)),
                                         _full((1, D_SSM)), _full((1, D_SSM)), _full((1, D_SSM))],
        out_specs=[row(D_SSM), row(D_SSM), pl.BlockSpec((None, D_STATE, D_SSM), lambda i: (i, 0, 0))],
        out_shape=[SDS((s, D_SSM), bf16), SDS((s, D_SSM), f32), SDS((nc, D_STATE, D_SSM), f32)],
        scratch_shapes=[pltpu.VMEM((D_STATE, D_SSM), f32), pltpu.VMEM((8 + CHUNK, D_CONV), f32),
                        pltpu.VMEM((CHUNK, D_SSM), f32), pltpu.VMEM((LANES, 2 * D_SSM), bf16)],
        compiler_params=pltpu.CompilerParams(dimension_semantics=("arbitrary",)),
    )(proj, proj, proj, proj, proj, proj, conv_w, conv_b, dtb16, alog16, alog_f, d_f, nw)


def _outproj_loss(mix_a, mix_s, wo, x, tgt, npw):
    s, d = x.shape
    tm = 512

    def body(ma_ref, ms_ref, wo_ref, x_ref, t_ref, npw_ref, dmix_ref, dout_ref, dres_ref, acc_ref):
        @pl.when(pl.program_id(0) == 0)
        def _():
            acc_ref[...] = jnp.zeros_like(acc_ref)

        out = _nn(ma_ref[...], wo_ref[0:D_ATTN, :]) + _nn(ms_ref[...], wo_ref[D_ATTN:, :])
        r = lax.rsqrt(jnp.mean(out * out, axis=-1, keepdims=True) + EPS)
        on = out * r
        diff = x_ref[...] + on * npw_ref[...] - t_ref[...]
        dres = diff * (1.0 / d)
        dres_ref[...] = dres
        acc_ref[0:1, :] += jnp.sum(diff * diff, axis=0, keepdims=True)
        acc_ref[1:2, :] += jnp.sum(dres * on, axis=0, keepdims=True)
        dn = dres * npw_ref[...]
        dout = (r * (dn - on * jnp.mean(dn * on, axis=-1, keepdims=True))).astype(bf16)
        dout_ref[...] = dout
        dmix_ref[...] = _nt(dout, wo_ref[...])

    row = lambda w: pl.BlockSpec((tm, w), lambda i: (i, 0))
    return pl.pallas_call(
        body, name="outproj_loss", grid=(s // tm,),
        in_specs=[row(D_ATTN), row(D_SSM), _full((D_ATTN + D_SSM, d)), row(d), row(d), _full((1, d))],
        out_specs=[row(D_ATTN + D_SSM), row(d), row(d), _full((8, d))],
        out_shape=[SDS((s, D_ATTN + D_SSM), f32), SDS((s, d), bf16), SDS((s, d), f32), SDS((8, d), f32)],
        compiler_params=pltpu.CompilerParams(dimension_semantics=("arbitrary",)),
    )(mix_a, mix_s, wo, x, tgt, npw)


def _attn_bwd(proj, o, lb, dmix, swap=None):
    s = proj.shape[0]
    n_it = s // BLK

    nsw = 0 if swap is None else 1

    def body(*refs):
        q_ref, k_ref, v_ref, g_ref, o_ref, l_ref, dm_ref = refs[:7]
        swap_in = refs[7:7 + nsw]
        dq_ref, dk_ref, dv_ref, dg_ref = refs[7 + nsw:11 + nsw]
        swap_out = refs[11 + nsw:11 + 2 * nsw]
        dq_acc, dk_acc, dv_acc, do_scr, dl_scr = refs[11 + 2 * nsw:16 + 2 * nsw]
        bufs = refs[16 + 2 * nsw:44 + 2 * nsw]
        swap_sems = refs[44 + 2 * nsw:]
        head0, tri2, _, _, bones = _attn_consts()

        if nsw:
            x, y, c = _my_pos()
            swap_copy = pltpu.make_async_remote_copy(
                src_ref=swap_in[0], dst_ref=swap_out[0], send_sem=swap_sems[0], recv_sem=swap_sems[1],
                device_id=(x, y, 1 - c), device_id_type=MESH)

            @pl.when(pl.program_id(0) == 0)
            def _():
                swap_copy.start()

        def pro(i, carry):
            rows = pl.ds(pl.multiple_of(i * 256, 256), 256)
            g = g_ref[rows, :]
            sg = _sigmoid(g)
            dmx = dm_ref[rows, :]
            ov = o_ref[rows, :]
            dg_ref[rows, :] = (dmx * ov * (sg * (1.0 + g * (1.0 - sg)))).astype(bf16)
            do = dmx * (g * sg)
            do_scr[rows, :] = do
            dl_scr[rows, :] = _split_dot_sum(do * ov, bones)
            z = jnp.zeros((256, LANES), f32)
            dq_acc[rows, :] = z
            dk_acc[rows, :] = z
            dv_acc[rows, :] = z
            return carry

        lax.fori_loop(0, s // 256, pro, 0)

        def per_head(t):
            return jnp.concatenate([t[:, :LANES], t[:, LANES:]], axis=0)

        def both_heads(t):
            tr = pltpu.roll(t, HEAD_DIM, 1)
            return jnp.concatenate([jnp.where(head0, t, tr), jnp.where(head0, tr, t)], axis=1)

        mm_bufs = ((bufs[0], bufs[1], bufs[2], bufs[3]), (bufs[4], bufs[5], bufs[6], bufs[7]))
        ds_bufs = ((bufs[8], bufs[9], bufs[10], bufs[11]), (bufs[12], bufs[13], bufs[14], bufs[15]))
        op_bufs = ((bufs[16], bufs[17], bufs[18], bufs[19]), (bufs[20], bufs[21], bufs[22], bufs[23]))
        vc_bufs, carry_k, carry_v = (bufs[24], bufs[25]), bufs[26], bufs[27]
        for buf in (op_bufs[0][0], op_bufs[1][0]) + vc_bufs:
            buf[...] = jnp.zeros_like(buf)

        def block_rows(i, d, nb):
            r, blk = i // nb, i % nb
            return pl.ds(blk * (BLK * d) + r, BLK, stride=d), blk > 0

        def unstack(st16):
            return st16[:BLK] + st16[BLK:]

        def products(i, par, d, nb):
            rows, has_prev = block_rows(i, d, nb)
            s_buf, dp_buf, sd_buf, dpd_buf = mm_bufs[par]
            kc_buf, kp_buf, q_buf, do_buf = op_bufs[par]
            q = q_ref[rows, :]
            qs = q * 0.125
            do = do_scr[rows, :]
            qs16, do16 = qs.astype(bf16), do.astype(bf16)
            kst_c = _stack_heads(k_ref[rows, :].astype(bf16), head0)
            vst_c = _stack_heads(v_ref[rows, :].astype(bf16), head0)
            kst_p, vst_p = op_bufs[1 - par][0][...], vc_bufs[1 - par][...]
            kc_buf[...] = kst_c
            kp_buf[...] = kst_p
            vc_bufs[par][...] = vst_c
            q_buf[...] = q.astype(bf16)
            do_buf[...] = do16
            s_buf[...] = jnp.where(tri2, _nt(qs16, kst_c), jnp.where(has_prev, _nt(qs16, kst_p), -jnp.inf))
            dp_buf[...] = jnp.where(tri2, _nt(do16, vst_c), jnp.where(has_prev, _nt(do16, vst_p), 0.0))
            sd_buf[...] = _nn((qs * unstack(kst_p).astype(f32)).astype(bf16), bones)
            dpd_buf[...] = jnp.where(has_prev, _nn((do * unstack(vst_p).astype(f32)).astype(bf16), bones), 0.0)

        def softmax_grad(i, par, d, nb):
            rows, has_prev = block_rows(i, d, nb)
            s_buf, dp_buf, sd_buf, dpd_buf = mm_bufs[par]
            p_buf, ds_buf, pd_buf, dsd_buf = ds_bufs[par]
            lse = l_ref[rows, :]
            dl = dl_scr[rows, :]
            pt = jnp.exp(s_buf[...] - both_heads(lse))
            ds_buf[...] = (pt * (dp_buf[...] - both_heads(dl)) * 0.125).astype(bf16)
            p_buf[...] = pt.astype(bf16)
            pd = jnp.where(has_prev, jnp.exp(sd_buf[...] - lse), 0.0)
            pd_buf[...] = pd
            dsd_buf[...] = pd * (dpd_buf[...] - dl) * 0.125

        def accumulate(i, par, d, nb):
            rows, _ = block_rows(i, d, nb)
            before, _ = block_rows(jnp.maximum(i - 1, 0), d, nb)
            p_buf, ds_buf, pd_buf, dsd_buf = ds_bufs[par]
            kc_buf, kp_buf, q_buf, do_buf = op_bufs[par]
            pt16, ds16, pd, dsd = p_buf[...], ds_buf[...], pd_buf[...], dsd_buf[...]
            zero = jnp.zeros_like(pt16)
            dsc, dsp = jnp.where(tri2, ds16, zero), jnp.where(tri2, zero, ds16)
            pc, pp = jnp.where(tri2, pt16, zero), jnp.where(tri2, zero, pt16)
            kst_c, kst_p, q16, do16 = kc_buf[...], kp_buf[...], q_buf[...], do_buf[...]
            qst, dost = _stack_heads(q16, head0), _stack_heads(do16, head0)
            dq_acc[rows, :] += _nn(dsc, kst_c) + _nn(dsp, kst_p) + dsd * unstack(kst_p).astype(f32)
            dk2 = _tn(jnp.concatenate([per_head(dsc), per_head(dsp)], axis=1), qst)
            dv2 = _tn(jnp.concatenate([per_head(pc), per_head(pp)], axis=1), dost)
            dk_acc[before, :] += carry_k[...] + dk2[BLK:] + dsd * q16.astype(f32)
            dv_acc[before, :] += carry_v[...] + dv2[BLK:] + pd * do16.astype(f32)
            carry_k[...] = dk2[:BLK]
            carry_v[...] = dv2[:BLK]

        for d in DILATIONS:
            nb = s // (BLK * d)
            carry_k[...] = jnp.zeros_like(carry_k)
            carry_v[...] = jnp.zeros_like(carry_v)
            products(0, 0, d, nb)
            products(1, 1, d, nb)
            softmax_grad(0, 0, d, nb)

            def steps(j, carry, d=d, nb=nb):
                for par in range(2):
                    t = 2 * j + 2 + par
                    accumulate(t - 2, par, d, nb)
                    products(t, par, d, nb)
                    softmax_grad(t - 1, 1 - par, d, nb)
                return carry

            lax.fori_loop(0, (n_it - 2) // 2, steps, 0)
            accumulate(n_it - 2, 0, d, nb)
            softmax_grad(n_it - 1, 1, d, nb)
            accumulate(n_it - 1, 1, d, nb)
            last, _ = block_rows(n_it - 1, d, nb)
            dk_acc[last, :] += carry_k[...]
            dv_acc[last, :] += carry_v[...]

        def epi(i, carry):
            rows = pl.ds(pl.multiple_of(i * 256, 256), 256)
            dq_ref[rows, :] = dq_acc[rows, :].astype(bf16)
            dk_ref[rows, :] = dk_acc[rows, :].astype(bf16)
            dv_ref[rows, :] = dv_acc[rows, :].astype(bf16)
            return carry

        lax.fori_loop(0, s // 256, epi, 0)

        if nsw:
            @pl.when(pl.program_id(0) == N_PAIRS - 1)
            def _():
                swap_copy.wait_send()
                swap_copy.wait_recv()

    col = lambda base: pl.BlockSpec((s, LANES), lambda h: (0, base + h))
    anyspec = pl.BlockSpec(memory_space=pl.ANY)
    swaps = [] if swap is None else [swap]
    outs = pl.pallas_call(
        body, name="attn_bwd", grid=(N_PAIRS,),
        in_specs=[col(0), col(8), col(16), col(24), col(0), col(0), col(0)] + [anyspec] * nsw,
        out_specs=[col(0)] * 4 + [anyspec] * nsw,
        out_shape=[SDS((s, D_ATTN), bf16)] * 4 + [SDS(a.shape, a.dtype) for a in swaps],
        scratch_shapes=[pltpu.VMEM((s, LANES), f32)] * 5
        + [pltpu.VMEM((BLK, 2 * LANES), f32)] * 2 + [pltpu.VMEM((BLK, LANES), f32)] * 2
        + [pltpu.VMEM((BLK, 2 * LANES), f32)] * 2 + [pltpu.VMEM((BLK, LANES), f32)] * 2
        + [pltpu.VMEM((BLK, 2 * LANES), bf16)] * 2 + [pltpu.VMEM((BLK, LANES), f32)] * 2
        + [pltpu.VMEM((BLK, 2 * LANES), bf16)] * 2 + [pltpu.VMEM((BLK, LANES), f32)] * 2
        + [pltpu.VMEM((2 * BLK, LANES), bf16)] * 2 + [pltpu.VMEM((BLK, LANES), bf16)] * 2
        + [pltpu.VMEM((2 * BLK, LANES), bf16)] * 2 + [pltpu.VMEM((BLK, LANES), bf16)] * 2
        + [pltpu.VMEM((2 * BLK, LANES), bf16)] * 2 + [pltpu.VMEM((BLK, LANES), f32)] * 2
        + [pltpu.SemaphoreType.DMA(())] * (2 * nsw),
        compiler_params=pltpu.CompilerParams(dimension_semantics=("arbitrary",)),
    )(proj, proj, proj, proj, o, lb, dmix, *swaps)
    return outs


def _ssd_bwd(proj, y, states, dmix, conv_w, conv_b, dtb16, alog16, alog_f, d_f, nw, chip_sums=()):
    s = proj.shape[0]
    nc = s // CHUNK
    gw = D_SSM // N_GROUPS
    nx = len(chip_sums)

    def body(*refs):
        (xs_ref, bc_ref, xs_tail, bc_tail, dt_ref, z_ref, y_ref, st_ref, dm_ref, cw_ref, cb_ref, dtb_ref,
         alog16_ref, alogf_ref, df_ref, nw_ref) = refs[:16]
        cs_in = refs[16:16 + nx]
        out_ref, gconv_ref, gvec_ref, gdt_ref = refs[16 + nx:20 + nx]
        cs_out = refs[20 + nx:20 + 2 * nx]
        (dh_scr, head_scr, xpad, dcpad, da_scr, dxdt_scr, dbc_scr, emat_ref, fold_ref) = refs[20 + 2 * nx:29 + 2 * nx]
        cs_sems = refs[29 + 2 * nx:]
        i = pl.program_id(0)
        c = nc - 1 - i

        if nx:
            @pl.when(i == 0)
            def _():
                mine, sends, _ = _chip_exchange_copies(cs_in, cs_out, *cs_sems)
                for cp in mine + sends:
                    cp.start()

            @pl.when(i == nc - 1)
            def _():
                mine, sends, recvs = _chip_exchange_copies(cs_in, cs_out, *cs_sems)
                for cp in recvs:
                    cp.wait_recv()
                for cp in sends:
                    cp.wait_send()
                for cp in mine:
                    cp.wait()

        @pl.when(i == 0)
        def _():
            emat_ref[...] = _expand_mat()
            fold_ref[...] = _fold_mat()
            dh_scr[...] = jnp.zeros_like(dh_scr)
            head_scr[...] = jnp.zeros_like(head_scr)
            gconv_ref[...] = jnp.zeros_like(gconv_ref)
            gvec_ref[...] = jnp.zeros_like(gvec_ref)
            gdt_ref[...] = jnp.zeros_like(gdt_ref)

        cv, sig, xbc, pre, dt_f, al_f, al_x, al_t, taps = _ssd_common(
            xs_ref, bc_ref, xs_tail, bc_tail, dt_ref, cw_ref, cb_ref, dtb_ref, alog16_ref, emat_ref, xpad, c == 0)
        head0 = _iota((CHUNK, LANES), 1) < HEAD_DIM
        sub = _iota((CHUNK, LANES), 0)
        last_row = sub == CHUNK - 1

        yv, z, dmx = y_ref[...], z_ref[...], dm_ref[...]
        sz = _sigmoid(z)
        silu = z * sz
        yz = yv * silu
        dyz_parts = []
        for g in range(N_GROUPS):
            gs = slice(g * gw, (g + 1) * gw)
            part = yz[:, gs]
            r = lax.rsqrt(jnp.mean(part * part, axis=-1, keepdims=True) + EPS)
            nh = part * r
            gvec_ref[0:1, gs] += jnp.sum(dmx[:, gs] * nh, axis=0, keepdims=True)
            dn = dmx[:, gs] * nw_ref[:, gs]
            dyz_parts.append(r * (dn - nh * jnp.mean(dn * nh, axis=-1, keepdims=True)))
        dyz = jnp.concatenate(dyz_parts, axis=1)
        dy = dyz * silu
        out_ref[:, 0:D_SSM] = (dyz * yv * (sz * (1.0 + z * (1.0 - sz)))).astype(bf16)

        x_all = xbc[:, 0:D_SSM]
        gvec_ref[2:3, :] += jnp.sum(dy * x_all, axis=0, keepdims=True)

        for g in range(N_GROUPS):
            bm = xbc[:, D_SSM + g * D_STATE: D_SSM + (g + 1) * D_STATE].astype(bf16)
            cm = xbc[:, D_SSM + (N_GROUPS + g) * D_STATE: D_SSM + (N_GROUPS + g + 1) * D_STATE].astype(bf16)
            gmat = _nt(cm, bm)
            dgm = jnp.zeros((CHUNK, CHUNK), f32)
            db = jnp.zeros((CHUNK, D_STATE), f32)
            dc = jnp.zeros((CHUNK, D_STATE), f32)
            for pair in range(4 * g, 4 * g + 4):
                sl = slice(pair * LANES, (pair + 1) * LANES)
                xp, dtp, alp, dyp = x_all[:, sl], dt_f[:, sl], al_f[:, sl], dy[:, sl]
                xdt = xp * dtp
                xdt16 = xdt.astype(bf16)
                al_last = alp[CHUNK - 1:CHUNK, :]
                e_l = jnp.exp(alp)
                wf = jnp.exp(al_last - alp)
                e_last = jnp.exp(al_last)
                hp = st_ref[:, sl]
                hp16 = hp.astype(bf16)
                dhn = dh_scr[:, sl]
                dhn16 = dhn.astype(bf16)
                y_off = e_l * _nn(cm, hp16)
                dch16 = (dyp * e_l).astype(bf16)
                dc = dc + _nt(dch16, hp16)
                dh_out = _tn(cm, dch16)
                dal = dyp * y_off
                xw16 = (wf * xdt).astype(bf16)
                db = db + _nt(xw16, dhn16)
                dxw = _nn(bm, dhn16)
                dxdt = dxw * wf
                dwf = dxw * xdt * wf
                dal = dal - dwf
                dal_last = jnp.sum(dwf, axis=0, keepdims=True) + jnp.sum(dhn * hp, axis=0, keepdims=True) * e_last
                dh_scr[:, sl] = e_last * dhn + dh_out
                for h in range(2):
                    mh = head0 if h == 0 else jnp.logical_not(head0)
                    dyh16 = jnp.where(mh, dyp, 0.0).astype(bf16)
                    lmat = _decay_mat(al_x, al_t, pair, h)
                    mm = gmat * lmat
                    dmm = _nt(dyh16, xdt16)
                    dxdt = dxdt + _tn(mm.astype(bf16), dyh16)
                    n16 = (dmm * mm).astype(bf16)
                    jh = jnp.where(mh, 1.0 / HEAD_DIM, 0.0).astype(bf16)
                    dal = dal + _nn(n16, jh) - _tn(n16, jh)
                    dgm = dgm + dmm * lmat
                da_scr[:, sl] = dal + jnp.where(last_row, dal_last, 0.0)
                dxdt_scr[:, sl] = dxdt
            dgm16 = dgm.astype(bf16)
            dbc_scr[:, g * D_STATE:(g + 1) * D_STATE] = db + _tn(dgm16, cm)
            dbc_scr[:, (N_GROUPS + g) * D_STATE:(N_GROUPS + g + 1) * D_STATE] = dc + _nn(dgm16, bm)

        sub_c, lane_c = _iota((CHUNK, CHUNK), 0), _iota((CHUNK, CHUNK), 1)
        tri_t = (lane_c >= sub_c).astype(bf16)
        dadt = _dot_01_left(tri_t, da_scr[...], 2)
        a_f = -jnp.exp(alogf_ref[...])
        dxdt_all = dxdt_scr[...]
        ddt_f = dxdt_all * x_all + a_f * dadt
        gvec_ref[1:2, :] += jnp.sum(dt_f * dadt, axis=0, keepdims=True) * a_f
        dx = df_ref[...] * dy + dxdt_all * dt_f
        ddt_raw = _dot_01(ddt_f, fold_ref[...], 2) * _sigmoid(pre)
        gdt_ref[0:1, :] += jnp.sum(ddt_raw, axis=0, keepdims=True)
        out_ref[:, D_SSM + D_CONV:D_SSM + D_CONV + LANES] = ddt_raw.astype(bf16)
        out_ref[:, D_SSM + D_CONV + LANES:] = jnp.zeros((CHUNK, 3 * LANES), bf16)

        dsil = sig * (1.0 + cv * (1.0 - sig))
        dcv_x = dx * dsil[:, 0:D_SSM]
        dcv_bc = dbc_scr[...] * dsil[:, D_SSM:]
        dcpad[0:CHUNK, 0:D_SSM] = dcv_x
        dcpad[0:CHUNK, D_SSM:] = dcv_bc
        dcpad[CHUNK:, :] = head_scr[...]
        dcp = dcpad[...]
        dcv = dcp[0:CHUNK]
        gconv_ref[4:5, :] += jnp.sum(dcv, axis=0, keepdims=True)
        draw = cw_ref[3:4, :] * dcv
        for j in range(4):
            gconv_ref[j:j + 1, :] += jnp.sum(dcv * taps[j], axis=0, keepdims=True)
        for j in range(3):
            draw = draw + cw_ref[j:j + 1, :] * pltpu.roll(dcp, CHUNK + 8 - (3 - j), 0)[0:CHUNK]
        head_scr[...] = dcv[0:8]
        out_ref[:, D_SSM:D_SSM + D_CONV] = draw.astype(bf16)

    order = lambda i: nc - 1 - i
    row = lambda w, cb=0: pl.BlockSpec((CHUNK, w), lambda i: (nc - 1 - i, cb))
    anyspec = pl.BlockSpec(memory_space=pl.ANY)
    outs = pl.pallas_call(
        body, name="ssd_bwd", grid=(nc,),
        in_specs=_ssd_in_specs(order) + [row(D_SSM), pl.BlockSpec((None, D_STATE, D_SSM), lambda i: (nc - 1 - i, 0, 0)),
                                         row(D_SSM, 1), _full((4, D_CONV)), _full((1, D_CONV)), _full((1, LANES)),
                                         _full((1, LANES)), _full((1, D_SSM)), _full((1, D_SSM)), _full((1, D_SSM))]
        + [anyspec] * nx,
        out_specs=[row(3072), _full((8, D_CONV)), _full((8, D_SSM)), _full((8, LANES))] + [anyspec] * nx,
        out_shape=[SDS((s, 3072), bf16), SDS((8, D_CONV), f32), SDS((8, D_SSM), f32), SDS((8, LANES), f32)]
        + [SDS(a.shape, a.dtype) for a in chip_sums],
        scratch_shapes=[pltpu.VMEM((D_STATE, D_SSM), f32), pltpu.VMEM((8, D_CONV), f32),
                        pltpu.VMEM((8 + CHUNK, D_CONV), f32), pltpu.VMEM((8 + CHUNK, D_CONV), f32),
                        pltpu.VMEM((CHUNK, D_SSM), f32), pltpu.VMEM((CHUNK, D_SSM), f32),
                        pltpu.VMEM((CHUNK, 2 * N_GROUPS * D_STATE), f32),
                        pltpu.VMEM((LANES, 2 * D_SSM), bf16), pltpu.VMEM((D_SSM, LANES), bf16)]
        + (_chip_exchange_scratch(nx) if nx else []),
        compiler_params=pltpu.CompilerParams(dimension_semantics=("arbitrary",)),
    )(proj, proj, proj, proj, proj, proj, y, states, dmix, conv_w, conv_b, dtb16, alog16, alog_f, d_f, nw, *chip_sums)
    return outs[0], outs[1], outs[2], outs[3], outs[4:]


def _col_blocks(parts, tile):
    counts = [p.shape[1] // tile for p in parts]
    offs = [sum(counts[:t]) for t in range(len(parts))]
    return offs, counts, sum(counts)


def _bcast_copies(src_ref, out_ref, send_sems, recv_sems, local_sem):
    x, y, c = _my_pos()
    me = 4 * x + 2 * y + c
    mine = pltpu.make_async_copy(src_ref, out_ref.at[me], local_sem)
    sends, recvs = [], []
    for k in range(1, N_DEV):
        to, frm = (me + k) % N_DEV, (me + N_DEV - k) % N_DEV
        sems = dict(send_sem=send_sems.at[k - 1], recv_sem=recv_sems.at[k - 1], device_id_type=MESH)
        sends.append(pltpu.make_async_remote_copy(
            src_ref=src_ref, dst_ref=out_ref.at[me], device_id=(to // 4, (to // 2) % 2, to % 2), **sems))
        recvs.append(pltpu.make_async_remote_copy(
            src_ref=src_ref, dst_ref=out_ref.at[frm], device_id=(x, y, c), **sems))
    return mine, sends, recvs


def _bcast_scratch():
    return [pltpu.SemaphoreType.DMA((N_DEV - 1,)), pltpu.SemaphoreType.DMA((N_DEV - 1,)), pltpu.SemaphoreType.DMA(())]


def _inproj_bwd(dparts, wt, x, nw, dres, chip_sums=(), pack=None):
    s, d = x.shape
    tm, tk = 1024, 1024
    offs, counts, nk = _col_blocks(dparts, tk)
    npart, nx = len(dparts), len(chip_sums)
    npk = 0 if pack is None else 1
    ni = s // tm

    def body(*refs):
        dp_refs = refs[:npart]
        w_ref, x_ref, nw_ref, dres_ref = refs[npart:npart + 4]
        pos = npart + 4
        cs_in, pos = refs[pos:pos + nx], pos + nx
        pack_in, pos = refs[pos:pos + npk], pos + npk
        (gx_ref, gnw_ref), pos = refs[pos:pos + 2], pos + 2
        cs_out, pos = refs[pos:pos + nx], pos + nx
        pack_out, pos = refs[pos:pos + 2 * npk], pos + 2 * npk
        acc, pos = refs[pos], pos + 1
        cs_sems, pos = refs[pos:pos + 3 * min(nx, 1)], pos + 3 * min(nx, 1)
        pk_refs = refs[pos:]
        i, k = pl.program_id(0), pl.program_id(1)

        def exchange():
            return _chip_exchange_copies(cs_in, cs_out, *cs_sems)

        def pack_copies():
            return _bcast_copies(pack_in[0], pack_out[0], *pk_refs[1:4])

        def gnw_copies():
            return _bcast_copies(pk_refs[0], pack_out[1], *pk_refs[4:7])

        @pl.when(jnp.logical_and(i == 0, k == 0))
        def _():
            gnw_ref[...] = jnp.zeros_like(gnw_ref)
            if nx:
                mine, sends, _ = exchange()
                for cp in mine + sends:
                    cp.start()
            if npk:
                mine, sends, _ = pack_copies()
                for cp in [mine] + sends:
                    cp.start()

        @pl.when(k == 0)
        def _():
            acc[...] = _nn(dp_refs[0][...], w_ref[...])

        for t in range(npart):
            @pl.when(jnp.logical_and(k >= max(offs[t], 1), k < offs[t] + counts[t]))
            def _(t=t):
                acc[...] += _nn(dp_refs[t][...], w_ref[...])

        @pl.when(k == nk - 1)
        def _():
            xv = x_ref[...]
            r = lax.rsqrt(jnp.mean(xv * xv, axis=-1, keepdims=True) + EPS)
            xn = xv * r
            du = acc[...]
            gnw_ref[0:1, :] += jnp.sum(du * xn, axis=0, keepdims=True)
            dn = du * nw_ref[...]
            gx_ref[...] = dres_ref[...] + r * (dn - xn * jnp.mean(dn * xn, axis=-1, keepdims=True))

        @pl.when(jnp.logical_and(i == ni - 1, k == nk - 1))
        def _():
            if npk:
                pk_refs[0][...] = gnw_ref[...]
                mine, sends, _ = gnw_copies()
                for cp in [mine] + sends:
                    cp.start()
            if nx:
                mine, sends, recvs = exchange()
                for cp in recvs:
                    cp.wait_recv()
                for cp in sends:
                    cp.wait_send()
                for cp in mine:
                    cp.wait()
            if npk:
                for copies in (pack_copies(), gnw_copies()):
                    mine, sends, recvs = copies
                    for cp in recvs:
                        cp.wait_recv()
                    for cp in sends:
                        cp.wait_send()
                    mine.wait()

    def piece(t):
        return pl.BlockSpec((tm, tk), lambda i, k: (i, jnp.clip(k - offs[t], 0, counts[t] - 1)))

    anyspec = pl.BlockSpec(memory_space=pl.ANY)
    packs = [] if pack is None else [pack]
    pack_shapes = [] if pack is None else [SDS((N_DEV,) + pack.shape, f32), SDS((N_DEV, 8, d), f32)]
    scratch = [pltpu.VMEM((tm, d), f32)] + (_chip_exchange_scratch(nx) if nx else [])
    if npk:
        scratch += [pltpu.VMEM((8, d), f32)] + _bcast_scratch() + _bcast_scratch()
    outs = pl.pallas_call(
        body, name="inproj_bwd", grid=(ni, nk),
        in_specs=[piece(t) for t in range(npart)] + [
            pl.BlockSpec((tk, d), lambda i, k: (k, 0)),
            pl.BlockSpec((tm, d), lambda i, k: (i, 0)), pl.BlockSpec((1, d), lambda i, k: (0, 0)),
            pl.BlockSpec((tm, d), lambda i, k: (i, 0))] + [anyspec] * (nx + npk),
        out_specs=[pl.BlockSpec((tm, d), lambda i, k: (i, 0)), pl.BlockSpec((8, d), lambda i, k: (0, 0))]
        + [anyspec] * (nx + 2 * npk),
        out_shape=[SDS((s, d), f32), SDS((8, d), f32)] + [SDS(a.shape, a.dtype) for a in chip_sums] + pack_shapes,
        scratch_shapes=scratch,
        compiler_params=pltpu.CompilerParams(dimension_semantics=("arbitrary", "arbitrary")),
    )(*dparts, wt, x, nw, dres, *chip_sums, *packs)
    return outs[0], outs[1], outs[2:2 + nx], outs[2 + nx:]


def _matmul_tn(a_parts, b_parts, name):
    tile, tk = 1024, 1024
    s = a_parts[0].shape[0]
    nk = s // tk
    na, nb = len(a_parts), len(b_parts)
    offs_a, counts_a, ni = _col_blocks(a_parts, tile)
    offs_b, counts_b, nj = _col_blocks(b_parts, tile)

    def body(*refs):
        a_refs, b_refs, o_ref = refs[:na], refs[na:na + nb], refs[na + nb]
        i, j = pl.program_id(0), pl.program_id(1)

        @pl.when(pl.program_id(2) == 0)
        def _():
            o_ref[...] = jnp.zeros_like(o_ref)

        for ta in range(na):
            for tb in range(nb):
                in_a = jnp.logical_and(i >= offs_a[ta], i < offs_a[ta] + counts_a[ta])
                in_b = jnp.logical_and(j >= offs_b[tb], j < offs_b[tb] + counts_b[tb])

                @pl.when(jnp.logical_and(in_a, in_b))
                def _(ta=ta, tb=tb):
                    o_ref[...] += _tn(a_refs[ta][...], b_refs[tb][...])

    def spec(offs, counts, t, axis):
        def index(i, j, k):
            pos = (i, j)[axis]
            mine = jnp.logical_and(pos >= offs[t], pos < offs[t] + counts[t])
            return jnp.where(mine, k, 0), jnp.clip(pos - offs[t], 0, counts[t] - 1)
        return pl.BlockSpec((tk, tile), index)

    return pl.pallas_call(
        body, name=name, grid=(ni, nj, nk),
        in_specs=[spec(offs_a, counts_a, t, 0) for t in range(na)] + [spec(offs_b, counts_b, t, 1) for t in range(nb)],
        out_specs=pl.BlockSpec((tile, tile), lambda i, j, k: (i, j)),
        out_shape=SDS((ni * tile, nj * tile), f32),
        compiler_params=pltpu.CompilerParams(dimension_semantics=("parallel", "parallel", "arbitrary")),
    )(*a_parts, *b_parts)


def _adamw(w, g, m, v):
    m = ADAM_B1 * m + (1.0 - ADAM_B1) * g
    v = ADAM_B2 * v + (1.0 - ADAM_B2) * (g * g)
    m_hat = m / (1.0 - ADAM_B1 ** ADAM_STEP)
    v_hat = v / (1.0 - ADAM_B2 ** ADAM_STEP)
    delta = -ADAM_LR * (m_hat / (jnp.sqrt(v_hat) + ADAM_EPS) + ADAM_WD * w)
    return delta, m, v


def _sum_adamw(parts, w, m, v, name):
    r, c = w.shape
    tc = 256

    def body(p_ref, w_ref, m_ref, v_ref, g_ref, d_ref, nm_ref, nv_ref):
        g = p_ref[0].astype(f32)
        for q in range(1, 4):
            g = g + p_ref[q].astype(f32)
        g_ref[...] = g
        d_ref[...], nm_ref[...], nv_ref[...] = _adamw(w_ref[...], g, m_ref[...], v_ref[...])

    blk = pl.BlockSpec((r, tc), lambda i: (0, i))
    return pl.pallas_call(
        body, name=name, grid=(c // tc,),
        in_specs=[pl.BlockSpec((4, r, tc), lambda i: (0, 0, i)), blk, blk, blk],
        out_specs=[blk] * 4, out_shape=[SDS((r, c), f32)] * 4,
        compiler_params=pltpu.CompilerParams(dimension_semantics=("parallel",)),
    )(parts, w, m, v)


def _sum_small(parts, pre_blocks):
    def body(p_ref, b_ref, o_ref):
        t = p_ref[0]
        pre = b_ref[0]
        for j in range(1, N_DEV):
            t = t + p_ref[j]
            pre = pre + b_ref[j]
        o_ref[...] = t
        o_ref[5:6, 0:D_MODEL] = pre[0:1, :]
        row_h = _iota((D_SSM, LANES), 0) // HEAD_DIM
        fold = (row_h == _iota((D_SSM, LANES), 1)).astype(f32)
        lower = t[8:16, 0:LANES]
        folded = _nn_hi(t[8:16, 0:D_SSM], fold)
        loss = jnp.sum(t[11:12, 0:D_MODEL], axis=1, keepdims=True) * (0.5 / D_MODEL)
        row = _iota((8, LANES), 0)
        o_ref[8:16, 0:LANES] = jnp.where(row < 2, folded, jnp.where(row == 4, loss, lower))

    return pl.pallas_call(body, name="sum_small", out_shape=SDS((PACK_ROWS, PACK_W), f32),
                          in_specs=[pl.BlockSpec(memory_space=pltpu.VMEM)] * 2,
                          out_specs=pl.BlockSpec(memory_space=pltpu.VMEM))(parts, pre_blocks)


def _adamw_small(w, g, m, v):
    def body(w_ref, g_ref, m_ref, v_ref, d_ref, nm_ref, nv_ref):
        d_ref[...], nm_ref[...], nv_ref[...] = _adamw(w_ref[...], g_ref[...], m_ref[...], v_ref[...])

    vm = pl.BlockSpec(memory_space=pltpu.VMEM)
    return pl.pallas_call(body, name="adamw_small", out_shape=[SDS(w.shape, f32)] * 3,
                          in_specs=[vm] * 4, out_specs=[vm] * 3)(w, g, m, v)


def _pad_lanes(v, width):
    return jnp.pad(v, ((0, 0), (0, width - v.shape[1])))


def _local_step(x, tgt, norm_pre_w, wt, conv_w, conv_b, dt_bias, a_log, d_skip, ssm_norm_w, wo, norm_post_w, sharded):
    dtb16 = _pad_lanes(dt_bias, LANES)
    alog16 = _pad_lanes(a_log, LANES)
    alog_f = jnp.repeat(a_log, HEAD_DIM, axis=1)
    d_f = jnp.repeat(d_skip, HEAD_DIM, axis=1)

    shard_out = wo.shape[0]
    if sharded:
        proj, u, (g_out, g_cw) = _prenorm_inproj(x, norm_pre_w, wt, gather=(wo, conv_w))
        wo = g_out.reshape(N_DEV * shard_out, D_MODEL)
        conv_w = g_cw.transpose(1, 0, 2).reshape(4, D_CONV)
    else:
        proj, u, _ = _prenorm_inproj(x, norm_pre_w, wt)
    o, lb, mix_a = _attn_fwd(proj)
    mix_s, y, states = _ssd_fwd(proj, conv_w, conv_b, dtb16, alog16, alog_f, d_f, ssm_norm_w)
    dmix, dout, dres, acc_post = _outproj_loss(mix_a, mix_s, wo, x, tgt, norm_post_w)
    dw_out = _matmul_tn([mix_a, mix_s], [dout], "dw_out")
    ssd_args = (proj, y, states, dmix, conv_w, conv_b, dtb16, alog16, alog_f, d_f, ssm_norm_w)
    if sharded:
        dq, dk, dv, dg, got_out = _attn_bwd(proj, o, lb, dmix, swap=dw_out)
        chip_out = _chip_sum(dw_out, got_out, shard_out, "chip_sum_w_out")
        dzxd, g_conv, g_vec, g_dt, (parts_out,) = _ssd_bwd(*ssd_args, chip_sums=[chip_out])
    else:
        dq, dk, dv, dg = _attn_bwd(proj, o, lb, dmix)
        dzxd, g_conv, g_vec, g_dt, _ = _ssd_bwd(*ssd_args)
    dparts = [dq, dk, dv, dg, dzxd]

    def pack(g_pre_row):
        return jnp.concatenate(
            [g_conv[0:5], g_pre_row, _pad_lanes(g_vec[0:1], PACK_W), _pad_lanes(acc_post[1:2], PACK_W),
             _pad_lanes(g_vec[1:3], PACK_W), _pad_lanes(g_dt[0:1], PACK_W), _pad_lanes(acc_post[0:1], PACK_W),
             jnp.zeros((4, PACK_W), f32)], axis=0)

    if sharded:
        dw_in, got_in = _dw_in_swap(dparts, u)
        chip_in = _chip_sum(dw_in, got_in, D_IN_PROJ // N_DEV, "chip_sum_w_in")
        grad_x, _, (parts_in,), small = _inproj_bwd(dparts, wt, x, norm_pre_w, dres, [chip_in],
                                                    pack(jnp.zeros((1, PACK_W), f32)))
        return grad_x, (parts_in, parts_out), small
    dw_in = _matmul_tn(dparts, [u], "dw_in")
    grad_x, g_pre, _, _ = _inproj_bwd(dparts, wt, x, norm_pre_w, dres)
    return grad_x, (dw_in, dw_out), pack(_pad_lanes(g_pre[0:1], PACK_W))


def kernel(x, norm_pre_w, w_in, conv_w, conv_b, dt_bias, a_log, d_skip, ssm_norm_w, w_out, norm_post_w, loss_target, m_norm_pre_w, m_w_in, m_conv_w, m_conv_b, m_dt_bias, m_a_log, m_d_skip, m_ssm_norm_w, m_w_out, m_norm_post_w, v_norm_pre_w, v_w_in, v_conv_w, v_conv_b, v_dt_bias, v_a_log, v_d_skip, v_ssm_norm_w, v_w_out, v_norm_post_w):
    shard_in = w_in.shape[2]
    shard_cv = conv_w.shape[2]
    me = 4 * lax.axis_index("x") + 2 * lax.axis_index("y") + lax.axis_index("c")

    g_in, = _all_gather([w_in[0].T.astype(bf16)])
    wt = _assemble_wt(g_in)

    grad_x, (parts_in, parts_out), (parts_small, pre_blocks) = _local_step(
        x[0], loss_target[0], norm_pre_w, wt, conv_w[0], conv_b, dt_bias, a_log, d_skip, ssm_norm_w,
        w_out[0].astype(bf16), norm_post_w, sharded=True)

    g_w_in, d_w_in, nm_w_in, nv_w_in = (a.T for a in _sum_adamw(
        parts_in, w_in[0].T, m_w_in[0].T, v_w_in[0].T, "sum_adamw_w_in"))
    g_w_out, d_w_out, nm_w_out, nv_w_out = _sum_adamw(parts_out, w_out[0], m_w_out[0], v_w_out[0], "sum_adamw_w_out")
    tot = _sum_small(parts_small, pre_blocks)

    g_cw_all = tot[0:4]
    small_g = {
        "conv_w": lax.dynamic_slice(g_cw_all, (0, me * shard_cv), (4, shard_cv)),
        "conv_b": tot[4:5], "norm_pre_w": tot[5:6, :D_MODEL], "ssm_norm_w": tot[6:7, :D_SSM],
        "norm_post_w": tot[7:8, :D_MODEL], "a_log": tot[8:9, :16], "d_skip": tot[9:10, :16], "dt_bias": tot[10:11, :16],
    }
    loss = tot[12, 0]
    small_w = {"conv_w": (conv_w[0], m_conv_w[0], v_conv_w[0]), "conv_b": (conv_b, m_conv_b, v_conv_b),
               "norm_pre_w": (norm_pre_w, m_norm_pre_w, v_norm_pre_w), "ssm_norm_w": (ssm_norm_w, m_ssm_norm_w, v_ssm_norm_w),
               "norm_post_w": (norm_post_w, m_norm_post_w, v_norm_post_w), "a_log": (a_log, m_a_log, v_a_log),
               "d_skip": (d_skip, m_d_skip, v_d_skip), "dt_bias": (dt_bias, m_dt_bias, v_dt_bias)}
    names = list(small_w)
    sizes = [small_g[k].size for k in names]
    tot_size = sum(sizes)
    pad_to = -(-tot_size // 1024) * 1024

    def flat(arrs):
        v = jnp.concatenate([a.reshape(-1) for a in arrs])
        return jnp.pad(v, (0, pad_to - tot_size)).reshape(pad_to // LANES, LANES)

    fw = flat([small_w[k][0] for k in names])
    fg = flat([small_g[k] for k in names])
    fm = flat([small_w[k][1] for k in names])
    fv = jnp.pad(jnp.concatenate([small_w[k][2].reshape(-1) for k in names]), (0, pad_to - tot_size),
                 constant_values=1.0).reshape(pad_to // LANES, LANES)
    fd, fnm, fnv = _adamw_small(fw, fg, fm, fv)

    def unflat(f):
        out, off = {}, 0
        v = f.reshape(-1)
        for k, n in zip(names, sizes):
            out[k] = v[off:off + n].reshape(small_g[k].shape)
            off += n
        return out

    sd, snm, snv = unflat(fd), unflat(fnm), unflat(fnv)
    lead = lambda a: a[None]
    order = ["norm_pre_w", "w_in", "conv_w", "conv_b", "dt_bias", "a_log", "d_skip", "ssm_norm_w", "w_out", "norm_post_w"]
    grads = dict(small_g, w_in=g_w_in, w_out=g_w_out)
    deltas = dict(sd, w_in=d_w_in, w_out=d_w_out)
    new_m = dict(snm, w_in=nm_w_in, w_out=nm_w_out)
    new_v = dict(snv, w_in=nv_w_in, w_out=nv_w_out)

    def shaped(dct, k):
        a = dct[k]
        return lead(a) if k in ("w_in", "w_out", "conv_w") else a

    return (loss, grad_x[None], *[shaped(grads, k) for k in order], *[shaped(deltas, k) for k in order],
            *[shaped(new_m, k) for k in order], *[shaped(new_v, k) for k in order])
```

```python
import jax
import jax.numpy as jnp
from jax import lax
from jax.experimental import pallas as pl
from jax.experimental.pallas import tpu as pltpu

f32, bf16 = jnp.float32, jnp.bfloat16
SDS = jax.ShapeDtypeStruct
HIGHEST = lax.Precision.HIGHEST
MESH = pl.DeviceIdType.MESH

N_DEV = 8
D_MODEL = 1024
D_ATTN = 1024
D_SSM = 1024
HEAD_DIM = 64
N_PAIRS = 8
D_STATE = 128
N_GROUPS = 2
D_CONV = D_SSM + 2 * N_GROUPS * D_STATE
D_IN_PROJ = 4 * D_ATTN + D_SSM + D_CONV + 16
NP = 7168
CHUNK = 128
BLK = 128
DILATIONS = (1, 4, 16)
EPS = 1e-6
LANES = 128
COL_Z, COL_XS, COL_BC, COL_DT = 4096, 5120, 6144, 6656

ADAM_LR, ADAM_B1, ADAM_B2, ADAM_EPS, ADAM_WD, ADAM_STEP = 0.001, 0.9, 0.999, 1e-08, 0.01, 10

PACK_ROWS, PACK_W = 16, 1536


def _nt(a, b):
    return lax.dot_general(a, b, (((1,), (1,)), ((), ())), preferred_element_type=f32)


def _tn(a, b):
    return lax.dot_general(a, b, (((0,), (0,)), ((), ())), preferred_element_type=f32)


def _nn(a, b):
    return jnp.dot(a, b, preferred_element_type=f32)


def _nn_hi(a, b):
    return jnp.dot(a, b, precision=HIGHEST, preferred_element_type=f32)


def _sigmoid(x):
    return 1.0 / (1.0 + jnp.exp(-x))


def _softplus(x):
    return jnp.maximum(x, 0.0) + jnp.log1p(jnp.exp(-jnp.abs(x)))


def _iota(shape, dim):
    return lax.broadcasted_iota(jnp.int32, shape, dim)


def _my_pos():
    return lax.axis_index("x"), lax.axis_index("y"), lax.axis_index("c")


GATHER_SEMS = 9


def _gather_phases(ins, outs, send_sems, recv_sems, local_sems):
    n, ns = len(ins), GATHER_SEMS
    x, y, c = _my_pos()
    me, sibling = (x, y, c), (x, y, 1 - c)
    xn, yn, diag = (1 - x, y), (x, 1 - y), (1 - x, 1 - y)

    def slot(a, px, py, pc):
        return outs[a].at[4 * px + 2 * py + pc]

    def part(a, ref, h):
        width = ins[a].shape[-1]
        if width % (2 * LANES):
            return ref if h == 1 else None
        return ref.at[:, pl.ds(h * (width // 2), width // 2)]

    def copy(a, k, block, to, src=None, h=None):
        src_ref = slot(a, *block) if src is None else src
        dst_ref = slot(a, *block)
        if h is not None:
            src_ref, dst_ref = part(a, src_ref, h), part(a, dst_ref, h)
            if src_ref is None:
                return None
        return pltpu.make_async_remote_copy(
            src_ref=src_ref, dst_ref=dst_ref, send_sem=send_sems.at[ns * a + k], recv_sem=recv_sems.at[ns * a + k],
            device_id=to, device_id_type=MESH)

    def mine():
        return [pltpu.make_async_copy(ins[a], slot(a, *me), local_sems.at[a]) for a in range(n)]

    def own_sends(a):
        return [copy(a, 0, me, sibling, src=ins[a]), copy(a, 1, me, (*xn, c), src=ins[a]),
                copy(a, 2, me, (*yn, c), src=ins[a])]

    def neighbour_relays(a):
        return [copy(a, 4, (*xn, c), sibling), copy(a, 7, (*xn, c), (*yn, c), h=1),
                copy(a, 5, (*yn, c), sibling), copy(a, 8, (*yn, c), (*xn, c), h=0)]

    def diagonal_halves(a):
        return [copy(a, k, (*diag, c), me, h=h) for k, h in ((8, 0), (7, 1))]

    def start_all(cps):
        for cp in cps:
            if cp is not None:
                cp.start()

    def phase0():
        start_all(mine())
        for a in range(n):
            start_all(own_sends(a))

    def phase1():
        for a in range(n):
            copy(a, 1, (*xn, c), me).wait_recv()
            copy(a, 2, (*yn, c), me).wait_recv()
            start_all(neighbour_relays(a))

    def phase2():
        for a in range(n):
            for cp in diagonal_halves(a):
                if cp is not None:
                    cp.wait_recv()
            copy(a, 6, (*diag, c), sibling).start()

    def finish():
        for a in range(n):
            copy(a, 0, sibling, me).wait_recv()
            for j, chip in enumerate((xn, yn, diag)):
                copy(a, 4 + j, (*chip, 1 - c), me).wait_recv()
        for a in range(n):
            for cp in own_sends(a) + neighbour_relays(a) + [copy(a, 6, (*diag, c), sibling)]:
                if cp is not None:
                    cp.wait_send()
        for cp in mine():
            cp.wait()

    return phase0, phase1, phase2, finish


def _gather_scratch(n):
    return [pltpu.SemaphoreType.DMA((GATHER_SEMS * n,)), pltpu.SemaphoreType.DMA((GATHER_SEMS * n,)),
            pltpu.SemaphoreType.DMA((n,))]


def _all_gather(arrs):
    n = len(arrs)

    def body(*refs):
        for phase in _gather_phases(refs[:n], refs[n:2 * n], *refs[2 * n:]):
            phase()

    anyspec = pl.BlockSpec(memory_space=pl.ANY)
    return pl.pallas_call(
        body, name="weights_all_gather",
        out_shape=[SDS((N_DEV,) + a.shape, a.dtype) for a in arrs],
        in_specs=[anyspec] * n, out_specs=[anyspec] * n, scratch_shapes=_gather_scratch(n),
    )(*arrs)


def _dw_in_swap(a_parts, u):
    tile, tk = 1024, 1024
    s = u.shape[0]
    nk = s // tk
    na = len(a_parts)
    offs, counts, ni = _col_blocks(a_parts, tile)

    def body(*refs):
        a_refs, u_ref = refs[:na], refs[na]
        dw_ref, got_ref = refs[na + 1:na + 3]
        acc, stage, local_sems, send_sems, recv_sem = refs[na + 3:]
        i, k = pl.program_id(0), pl.program_id(1)
        x, y, c = _my_pos()
        par = i % 2

        def tile_copies(t, p):
            rows = pl.ds(pl.multiple_of(t * tile, tile), tile)
            loc = pltpu.make_async_copy(stage.at[p], dw_ref.at[rows], local_sems.at[p])
            rem = pltpu.make_async_remote_copy(
                src_ref=stage.at[p], dst_ref=got_ref.at[rows], send_sem=send_sems.at[p], recv_sem=recv_sem,
                device_id=(x, y, 1 - c), device_id_type=MESH)
            return loc, rem

        @pl.when(k == 0)
        def _():
            acc[...] = jnp.zeros((tile, tile), f32)

        for t in range(na):
            @pl.when(jnp.logical_and(i >= offs[t], i < offs[t] + counts[t]))
            def _(t=t):
                acc[...] += _tn(a_refs[t][...], u_ref[...])

        @pl.when(k == nk - 1)
        def _():
            @pl.when(i >= 2)
            def _():
                loc, rem = tile_copies(i - 2, par)
                loc.wait()
                rem.wait_send()
            stage[par] = acc[...]
            loc, rem = tile_copies(i, par)
            loc.start()
            rem.start()

        @pl.when(jnp.logical_and(i == ni - 1, k == nk - 1))
        def _():
            for t in (ni - 2, ni - 1):
                loc, rem = tile_copies(t, t % 2)
                loc.wait()
                rem.wait_send()
            pltpu.make_async_remote_copy(src_ref=dw_ref, dst_ref=got_ref, send_sem=send_sems.at[0], recv_sem=recv_sem,
                                         device_id=(x, y, c), device_id_type=MESH).wait_recv()

    def a_spec(t):
        def index(i, k):
            mine = jnp.logical_and(i >= offs[t], i < offs[t] + counts[t])
            return jnp.where(mine, k, 0), jnp.clip(i - offs[t], 0, counts[t] - 1)
        return pl.BlockSpec((tk, tile), index)

    anyspec = pl.BlockSpec(memory_space=pl.ANY)
    return pl.pallas_call(
        body, name="dw_in_swap", grid=(ni, nk),
        in_specs=[a_spec(t) for t in range(na)] + [pl.BlockSpec((tk, tile), lambda i, k: (k, 0))],
        out_specs=[anyspec] * 2,
        out_shape=[SDS((ni * tile, tile), f32), SDS((ni * tile, tile), f32)],
        scratch_shapes=[pltpu.VMEM((tile, tile), f32), pltpu.VMEM((2, tile, tile), f32), pltpu.SemaphoreType.DMA((2,)),
                        pltpu.SemaphoreType.DMA((2,)), pltpu.SemaphoreType.DMA(())],
        compiler_params=pltpu.CompilerParams(dimension_semantics=("arbitrary", "arbitrary")),
    )(*a_parts, u)


def _chip_sum(mine, got, rows, name):
    r, cdim = mine.shape
    tc = LANES

    def body(m_ref, g_ref, s16_ref):
        c = lax.axis_index("c")
        for q in range(4):
            blk = pl.ds(rows * (2 * q + c), rows)
            s16_ref[q] = (m_ref[blk, :] + g_ref[blk, :]).astype(bf16)

    col = pl.BlockSpec((r, tc), lambda i: (0, i))
    return pl.pallas_call(
        body, name=name, grid=(cdim // tc,), in_specs=[col, col],
        out_specs=pl.BlockSpec((4, rows, tc), lambda i: (0, 0, i)), out_shape=SDS((4, rows, cdim), bf16),
        compiler_params=pltpu.CompilerParams(dimension_semantics=("parallel",)),
    )(mine, got)


def _assemble_wt(shards):
    nd, rows, cdim = shards.shape
    tc = 256

    def body(g_ref, o_ref):
        for j in range(nd):
            o_ref[pl.ds(rows * j, rows), :] = g_ref[j]
        o_ref[pl.ds(nd * rows, NP - nd * rows), :] = jnp.zeros((NP - nd * rows, tc), shards.dtype)

    return pl.pallas_call(
        body, name="assemble_w_in", grid=(cdim // tc,),
        in_specs=[pl.BlockSpec((nd, rows, tc), lambda i: (0, 0, i))],
        out_specs=pl.BlockSpec((NP, tc), lambda i: (0, i)), out_shape=SDS((NP, cdim), shards.dtype),
        compiler_params=pltpu.CompilerParams(dimension_semantics=("parallel",)),
    )(shards)


def _chip_exchange_copies(ins, outs, send_sems, recv_sems, local_sems):
    nb = len(ins)
    x, y, c = _my_pos()
    my_q = 2 * x + y
    mine = [pltpu.make_async_copy(ins[a].at[my_q], outs[a].at[my_q], local_sems.at[a]) for a in range(nb)]
    sends, recvs = [], []
    for k in range(1, 4):
        to, frm = (my_q + k) % 4, (my_q + 4 - k) % 4
        for a in range(nb):
            sems = dict(send_sem=send_sems.at[3 * a + k - 1], recv_sem=recv_sems.at[3 * a + k - 1], device_id_type=MESH)
            sends.append(pltpu.make_async_remote_copy(
                src_ref=ins[a].at[to], dst_ref=outs[a].at[my_q], device_id=(to // 2, to % 2, c), **sems))
            recvs.append(pltpu.make_async_remote_copy(
                src_ref=ins[a].at[frm], dst_ref=outs[a].at[frm], device_id=(x, y, c), **sems))
    return mine, sends, recvs


def _chip_exchange_scratch(nb):
    return [pltpu.SemaphoreType.DMA((3 * nb,)), pltpu.SemaphoreType.DMA((3 * nb,)), pltpu.SemaphoreType.DMA((nb,))]


def _prenorm_inproj(x, nw, wt, gather=()):
    s, d = x.shape
    npad = wt.shape[0]
    tm, tn = 1024, 1024
    ng = len(gather)
    ni, nj = s // tm, npad // tn

    def body(x_ref, nw_ref, w_ref, *refs):
        g_in, (proj_ref, u_ref), g_out, sems = refs[:ng], refs[ng:ng + 2], refs[ng + 2:2 * ng + 2], refs[2 * ng + 2:]
        i, j = pl.program_id(0), pl.program_id(1)
        if ng:
            phases = _gather_phases(g_in, g_out, *sems)
            for step, phase in enumerate(phases[:3]):
                @pl.when(jnp.logical_and(i == step, j == 0))
                def _(phase=phase):
                    phase()

        @pl.when(j == 0)
        def _():
            xv = x_ref[...]
            r = lax.rsqrt(jnp.mean(xv * xv, axis=-1, keepdims=True) + EPS)
            u_ref[...] = (xv * r * nw_ref[...]).astype(bf16)
        proj_ref[...] = _nt(u_ref[...], w_ref[...])

        if ng:
            @pl.when(jnp.logical_and(i == ni - 1, j == nj - 1))
            def _():
                phases[3]()

    anyspec = pl.BlockSpec(memory_space=pl.ANY)
    outs = pl.pallas_call(
        body, name="prenorm_inproj", grid=(ni, nj),
        in_specs=[pl.BlockSpec((tm, d), lambda i, j: (i, 0)), pl.BlockSpec((1, d), lambda i, j: (0, 0)),
                  pl.BlockSpec((tn, d), lambda i, j: (j, 0))] + [anyspec] * ng,
        out_specs=[pl.BlockSpec((tm, tn), lambda i, j: (i, j)), pl.BlockSpec((tm, d), lambda i, j: (i, 0))]
        + [anyspec] * ng,
        out_shape=[SDS((s, npad), f32), SDS((s, d), bf16)] + [SDS((N_DEV,) + a.shape, a.dtype) for a in gather],
        scratch_shapes=_gather_scratch(ng) if ng else [],
        compiler_params=pltpu.CompilerParams(dimension_semantics=("arbitrary", "arbitrary")),
    )(x, nw, wt, *gather)
    return outs[0], outs[1], outs[2:]


def _attn_consts():
    head0 = _iota((BLK, LANES), 1) < HEAD_DIM
    tri2 = (_iota((BLK, 2 * LANES), 1) % LANES) <= _iota((BLK, 2 * LANES), 0)
    ones2 = ((_iota((LANES, 2 * LANES), 0) < HEAD_DIM) == (_iota((LANES, 2 * LANES), 1) < LANES)).astype(bf16)
    rmat = ((_iota((2 * LANES, LANES), 0) < LANES) == (_iota((2 * LANES, LANES), 1) < HEAD_DIM)).astype(bf16)
    bones = ((_iota((LANES, LANES), 0) < HEAD_DIM) == (_iota((LANES, LANES), 1) < HEAD_DIM)).astype(bf16)
    return head0, tri2, ones2, rmat, bones


def _stack_heads(x16, head0):
    zero = jnp.zeros_like(x16)
    return jnp.concatenate([jnp.where(head0, x16, zero), jnp.where(head0, zero, x16)], axis=0)


def _bf16_terms(x, terms):
    out = []
    for _ in range(terms):
        t = x.astype(bf16)
        out.append(t)
        x = x - t.astype(f32)
    return out


def _dot_01(x, w16, terms):
    return _nn(jnp.concatenate(_bf16_terms(x, terms), axis=1), jnp.concatenate([w16] * terms, axis=0))


def _split_dot_sum(x, w16):
    hi, lo = _bf16_terms(x, 2)
    return _nn(hi, w16) + _nn(lo, w16)


def _dot_01_left(w16, x, terms):
    return _nn(jnp.concatenate([w16] * terms, axis=1), jnp.concatenate(_bf16_terms(x, terms), axis=0))


def _attn_fwd(proj):
    s = proj.shape[0]
    n_it = s // BLK

    def body(q_ref, k_ref, v_ref, g_ref, o_ref, l_ref, mix_ref, op0, op1, op2, lp0, lp1, lp2,
             s_a, s_b, sd_a, sd_b, p_a, p_b, m_a, m_b, pd_a, pd_b, k_a, k_b, v_a, v_b):
        op_refs, lp_refs = (op0, op1, op2), (lp0, lp1, lp2)
        head0, tri2, ones2, rmat, _ = _attn_consts()
        score_bufs, prob_bufs = ((s_a, sd_a), (s_b, sd_b)), ((p_a, m_a, pd_a), (p_b, m_b, pd_b))
        k_bufs, v_bufs = (k_a, k_b), (v_a, v_b)
        for buf in k_bufs + v_bufs:
            buf[...] = jnp.zeros_like(buf)

        def block_rows(i, d, nb):
            r, blk = i // nb, i % nb
            return pl.ds(blk * (BLK * d) + r, BLK, stride=d), blk > 0

        def unstack(st16):
            return st16[:BLK] + st16[BLK:]

        def scores(i, par, d, nb):
            rows, has_prev = block_rows(i, d, nb)
            s_buf, sd_buf = score_bufs[par]
            qs = q_ref[rows, :] * 0.125
            qs16 = qs.astype(bf16)
            kst_c = _stack_heads(k_ref[rows, :].astype(bf16), head0)
            kst_p = k_bufs[1 - par][...]
            k_bufs[par][...] = kst_c
            sc = _nt(qs16, kst_c)
            sp = _nt(qs16, kst_p)
            s_buf[...] = jnp.where(tri2, sc, jnp.where(has_prev, sp, -jnp.inf))
            sd = _nn((qs * unstack(kst_p).astype(f32)).astype(bf16), ones2)
            sd_buf[...] = jnp.where(has_prev, sd, -jnp.inf)

        def softmax(bufs_in, bufs_out):
            s_buf, sd_buf = bufs_in
            p_buf, m_buf, pd_buf = bufs_out
            sc, sd2 = s_buf[...], sd_buf[...]
            m0 = jnp.max(sc[:, :LANES], axis=1, keepdims=True)
            m1 = jnp.max(sc[:, LANES:], axis=1, keepdims=True)
            m2 = jnp.concatenate([jnp.broadcast_to(m0, (BLK, LANES)), jnp.broadcast_to(m1, (BLK, LANES))], axis=1)
            m2 = jnp.maximum(m2, sd2)
            p_buf[...] = jnp.exp(sc - m2).astype(bf16)
            m_pair = jnp.where(head0, m2[:, :LANES], m2[:, LANES:])
            m_buf[...] = m_pair
            pd_buf[...] = jnp.exp(jnp.where(head0, sd2[:, :LANES], sd2[:, LANES:]) - m_pair)

        def output(i, par, d, nb, p):
            rows, _ = block_rows(i, d, nb)
            p_buf, m_buf, pd_buf = prob_bufs[par]
            vst_c = _stack_heads(v_ref[rows, :].astype(bf16), head0)
            vst_p = v_bufs[1 - par][...]
            v_bufs[par][...] = vst_c
            pt16, pd = p_buf[...], pd_buf[...]
            zero = jnp.zeros_like(pt16)
            o = (_nn(jnp.where(tri2, pt16, zero), vst_c) + _nn(jnp.where(tri2, zero, pt16), vst_p)
                 + pd * unstack(vst_p).astype(f32))
            l = _nn(pt16, rmat) + pd
            op_refs[p][rows, :] = o / l
            lp_refs[p][rows, :] = m_buf[...] + jnp.log(l)

        for p, d in enumerate(DILATIONS):
            nb = s // (BLK * d)
            scores(0, 0, d, nb)
            scores(1, 1, d, nb)
            softmax(score_bufs[0], prob_bufs[0])

            def steps(j, carry, d=d, nb=nb, p=p):
                for par in range(2):
                    t = 2 * j + 2 + par
                    scores(t, par, d, nb)
                    output(t - 2, par, d, nb, p)
                    softmax(score_bufs[1 - par], prob_bufs[1 - par])
                return carry

            lax.fori_loop(0, (n_it - 2) // 2, steps, 0)
            output(n_it - 2, 0, d, nb, p)
            softmax(score_bufs[1], prob_bufs[1])
            output(n_it - 1, 1, d, nb, p)

        def merge(i, carry):
            rows = pl.ds(pl.multiple_of(i * 256, 256), 256)
            l0, l1, l2 = lp0[rows, :], lp1[rows, :], lp2[rows, :]
            m = jnp.maximum(jnp.maximum(l0, l1), l2)
            e0, e1, e2 = jnp.exp(l0 - m), jnp.exp(l1 - m), jnp.exp(l2 - m)
            z = e0 + e1 + e2
            o = (e0 * op0[rows, :] + e1 * op1[rows, :] + e2 * op2[rows, :]) / z
            o_ref[rows, :] = o
            l_ref[rows, :] = m + jnp.log(z)
            g = g_ref[rows, :]
            mix_ref[rows, :] = (o * (g * _sigmoid(g))).astype(bf16)
            return carry

        lax.fori_loop(0, s // 256, merge, 0)

    col = lambda base: pl.BlockSpec((s, LANES), lambda h: (0, base + h))
    return pl.pallas_call(
        body, name="attn_fwd", grid=(N_PAIRS,),
        in_specs=[col(0), col(8), col(16), col(24)],
        out_specs=[col(0), col(0), col(0)],
        out_shape=[SDS((s, D_ATTN), f32), SDS((s, D_ATTN), f32), SDS((s, D_ATTN), bf16)],
        scratch_shapes=[pltpu.VMEM((s, LANES), f32)] * 6 + [pltpu.VMEM((BLK, 2 * LANES), f32)] * 4
        + [pltpu.VMEM((BLK, 2 * LANES), bf16)] * 2 + [pltpu.VMEM((BLK, LANES), f32)] * 4
        + [pltpu.VMEM((2 * BLK, LANES), bf16)] * 4,
        compiler_params=pltpu.CompilerParams(dimension_semantics=("parallel",)),
    )(proj, proj, proj, proj)


def _expand_mat():
    colv = _iota((LANES, 2 * D_SSM), 1)
    head = 2 * ((colv % D_SSM) // LANES) + colv // D_SSM
    return (_iota((LANES, 2 * D_SSM), 0) == head).astype(bf16)


def _fold_mat():
    return (_iota((D_SSM, LANES), 0) // HEAD_DIM == _iota((D_SSM, LANES), 1)).astype(bf16)


def _ssd_common(xs_ref, bc_ref, xs_tail, bc_tail, dt_ref, cw_ref, cb_ref, dtb_ref, alog16_ref, emat_ref, xpad, first):
    keep = jnp.where(first, 0.0, 1.0)
    xpad[0:8, 0:D_SSM] = xs_tail[...] * keep
    xpad[0:8, D_SSM:D_CONV] = bc_tail[...] * keep
    xpad[8:8 + CHUNK, 0:D_SSM] = xs_ref[...]
    xpad[8:8 + CHUNK, D_SSM:D_CONV] = bc_ref[...]
    xp = xpad[...]
    taps = [pltpu.roll(xp, 3 - j, 0)[8:8 + CHUNK] for j in range(3)] + [xp[8:8 + CHUNK]]
    cv = cb_ref[...] + cw_ref[0:1, :] * taps[0]
    for j in range(1, 4):
        cv = cv + cw_ref[j:j + 1, :] * taps[j]
    sig = _sigmoid(cv)
    xbc = cv * sig

    pre = dt_ref[...] + dtb_ref[...]
    dt16 = _softplus(pre)
    a16 = -jnp.exp(alog16_ref[...])
    sub, lane = _iota((CHUNK, CHUNK), 0), _iota((CHUNK, CHUNK), 1)
    tri = (sub >= lane).astype(f32)
    al16 = _nn_hi(tri, dt16 * a16)
    al_t = al16.T
    emat = emat_ref[...]
    dt_x = _dot_01(dt16, emat, 3)
    al_x = _dot_01(al16, emat, 3)
    lane_w = _iota((CHUNK, D_SSM), 1)
    even = (lane_w % LANES) < HEAD_DIM
    dt_f = jnp.where(even, dt_x[:, :D_SSM], dt_x[:, D_SSM:])
    al_f = jnp.where(even, al_x[:, :D_SSM], al_x[:, D_SSM:])
    return cv, sig, xbc, pre, dt_f, al_f, al_x, al_t, taps


def _decay_mat(al_x, al_t, pair, h):
    sub, lane = _iota((CHUNK, CHUNK), 0), _iota((CHUNK, CHUNK), 1)
    col = al_x[:, h * D_SSM + pair * LANES: h * D_SSM + (pair + 1) * LANES]
    row = al_t[2 * pair + h: 2 * pair + h + 1, :]
    return jnp.exp(jnp.where(sub >= lane, col - row, -jnp.inf))


def _ssd_in_specs(order):
    blk = lambda w, cb: pl.BlockSpec((CHUNK, w), lambda i: (order(i), cb))
    tail = lambda w, cb: pl.BlockSpec((8, w), lambda i: (jnp.maximum(16 * order(i) - 1, 0), cb))
    return [blk(D_SSM, COL_XS // D_SSM), blk(512, COL_BC // 512), tail(D_SSM, COL_XS // D_SSM),
            tail(512, COL_BC // 512), blk(LANES, COL_DT // LANES), blk(D_SSM, COL_Z // D_SSM)]


def _full(shape):
    return pl.BlockSpec(shape, lambda i: (0,) * len(shape))


def _ssd_fwd(proj, conv_w, conv_b, dtb16, alog16, alog_f, d_f, nw):
    s = proj.shape[0]
    nc = s // CHUNK

    def body(xs_ref, bc_ref, xs_tail, bc_tail, dt_ref, z_ref, cw_ref, cb_ref, dtb_ref, alog16_ref, alogf_ref,
             df_ref, nw_ref, mix_ref, y_ref, st_ref, h_scr, xpad, y_scr, emat_ref):
        c = pl.program_id(0)

        @pl.when(c == 0)
        def _():
            h_scr[...] = jnp.zeros_like(h_scr)
            emat_ref[...] = _expand_mat()

        _, _, xbc, _, dt_f, al_f, al_x, al_t, _ = _ssd_common(
            xs_ref, bc_ref, xs_tail, bc_tail, dt_ref, cw_ref, cb_ref, dtb_ref, alog16_ref, emat_ref, xpad, c == 0)
        head0 = _iota((CHUNK, LANES), 1) < HEAD_DIM
        st_ref[...] = h_scr[...]
        for g in range(N_GROUPS):
            bm = xbc[:, D_SSM + g * D_STATE: D_SSM + (g + 1) * D_STATE].astype(bf16)
            cm = xbc[:, D_SSM + (N_GROUPS + g) * D_STATE: D_SSM + (N_GROUPS + g + 1) * D_STATE].astype(bf16)
            gmat = _nt(cm, bm)
            for pair in range(4 * g, 4 * g + 4):
                sl = slice(pair * LANES, (pair + 1) * LANES)
                xp, dtp, alp = xbc[:, sl], dt_f[:, sl], al_f[:, sl]
                xdt = xp * dtp
                xdt16 = xdt.astype(bf16)
                al_last = alp[CHUNK - 1:CHUNK, :]
                hp = h_scr[:, sl]
                y_off = jnp.exp(alp) * _nn(cm, hp.astype(bf16))
                yd = [_nn((gmat * _decay_mat(al_x, al_t, pair, h)).astype(bf16), xdt16) for h in range(2)]
                y_scr[:, sl] = jnp.where(head0, yd[0], yd[1]) + y_off + df_ref[:, sl] * xp
                st = _tn(bm, (jnp.exp(al_last - alp) * xdt).astype(bf16))
                h_scr[:, sl] = jnp.exp(al_last) * hp + st
        y = y_scr[...]
        y_ref[...] = y
        z = z_ref[...]
        yz = y * (z * _sigmoid(z))
        gw = D_SSM // N_GROUPS
        for g in range(N_GROUPS):
            part = yz[:, g * gw:(g + 1) * gw]
            r = lax.rsqrt(jnp.mean(part * part, axis=-1, keepdims=True) + EPS)
            mix_ref[:, g * gw:(g + 1) * gw] = (part * r * nw_ref[:, g * gw:(g + 1) * gw]).astype(bf16)

    order = lambda i: i
    row = lambda w: pl.BlockSpec((CHUNK, w), lambda i: (i, 0))
    return pl.pallas_call(
        body, name="ssd_fwd", grid=(nc,),
        in_specs=_ssd_in_specs(order) + [_full((4, D_CONV)), _full((1, D_CONV)), _full((1, LANES)), _full((1, LANES)),
                                         _full((1, D_SSM)), _full((1, D_SSM)), _full((1, D_SSM))],
        out_specs=[row(D_SSM), row(D_SSM), pl.BlockSpec((None, D_STATE, D_SSM), lambda i: (i, 0, 0))],
        out_shape=[SDS((s, D_SSM), bf16), SDS((s, D_SSM), f32), SDS((nc, D_STATE, D_SSM), f32)],
        scratch_shapes=[pltpu.VMEM((D_STATE, D_SSM), f32), pltpu.VMEM((8 + CHUNK, D_CONV), f32),
                        pltpu.VMEM((CHUNK, D_SSM), f32), pltpu.VMEM((LANES, 2 * D_SSM), bf16)],
        compiler_params=pltpu.CompilerParams(dimension_semantics=("arbitrary",)),
    )(proj, proj, proj, proj, proj, proj, conv_w, conv_b, dtb16, alog16, alog_f, d_f, nw)


def _outproj_loss(mix_a, mix_s, wo, x, tgt, npw):
    s, d = x.shape
    tm = 512

    def body(ma_ref, ms_ref, wo_ref, x_ref, t_ref, npw_ref, dmix_ref, dout_ref, dres_ref, acc_ref):
        @pl.when(pl.program_id(0) == 0)
        def _():
            acc_ref[...] = jnp.zeros_like(acc_ref)

        out = _nn(ma_ref[...], wo_ref[0:D_ATTN, :]) + _nn(ms_ref[...], wo_ref[D_ATTN:, :])
        r = lax.rsqrt(jnp.mean(out * out, axis=-1, keepdims=True) + EPS)
        on = out * r
        diff = x_ref[...] + on * npw_ref[...] - t_ref[...]
        dres = diff * (1.0 / d)
        dres_ref[...] = dres
        acc_ref[0:1, :] += jnp.sum(diff * diff, axis=0, keepdims=True)
        acc_ref[1:2, :] += jnp.sum(dres * on, axis=0, keepdims=True)
        dn = dres * npw_ref[...]
        dout = (r * (dn - on * jnp.mean(dn * on, axis=-1, keepdims=True))).astype(bf16)
        dout_ref[...] = dout
        dmix_ref[...] = _nt(dout, wo_ref[...])

    row = lambda w: pl.BlockSpec((tm, w), lambda i: (i, 0))
    return pl.pallas_call(
        body, name="outproj_loss", grid=(s // tm,),
        in_specs=[row(D_ATTN), row(D_SSM), _full((D_ATTN + D_SSM, d)), row(d), row(d), _full((1, d))],
        out_specs=[row(D_ATTN + D_SSM), row(d), row(d), _full((8, d))],
        out_shape=[SDS((s, D_ATTN + D_SSM), f32), SDS((s, d), bf16), SDS((s, d), f32), SDS((8, d), f32)],
        compiler_params=pltpu.CompilerParams(dimension_semantics=("arbitrary",)),
    )(mix_a, mix_s, wo, x, tgt, npw)


def _attn_bwd(proj, o, lb, dmix, swap=None):
    s = proj.shape[0]
    n_it = s // BLK

    nsw = 0 if swap is None else 1

    def body(*refs):
        q_ref, k_ref, v_ref, g_ref, o_ref, l_ref, dm_ref = refs[:7]
        swap_in = refs[7:7 + nsw]
        dq_ref, dk_ref, dv_ref, dg_ref = refs[7 + nsw:11 + nsw]
        swap_out = refs[11 + nsw:11 + 2 * nsw]
        dq_acc, dk_acc, dv_acc, do_scr, dl_scr = refs[11 + 2 * nsw:16 + 2 * nsw]
        bufs = refs[16 + 2 * nsw:44 + 2 * nsw]
        swap_sems = refs[44 + 2 * nsw:]
        head0, tri2, _, _, bones = _attn_consts()

        if nsw:
            x, y, c = _my_pos()
            swap_copy = pltpu.make_async_remote_copy(
                src_ref=swap_in[0], dst_ref=swap_out[0], send_sem=swap_sems[0], recv_sem=swap_sems[1],
                device_id=(x, y, 1 - c), device_id_type=MESH)

            @pl.when(pl.program_id(0) == 0)
            def _():
                swap_copy.start()

        def pro(i, carry):
            rows = pl.ds(pl.multiple_of(i * 256, 256), 256)
            g = g_ref[rows, :]
            sg = _sigmoid(g)
            dmx = dm_ref[rows, :]
            ov = o_ref[rows, :]
            dg_ref[rows, :] = (dmx * ov * (sg * (1.0 + g * (1.0 - sg)))).astype(bf16)
            do = dmx * (g * sg)
            do_scr[rows, :] = do
            dl_scr[rows, :] = _split_dot_sum(do * ov, bones)
            z = jnp.zeros((256, LANES), f32)
            dq_acc[rows, :] = z
            dk_acc[rows, :] = z
            dv_acc[rows, :] = z
            return carry

        lax.fori_loop(0, s // 256, pro, 0)

        def per_head(t):
            return jnp.concatenate([t[:, :LANES], t[:, LANES:]], axis=0)

        def both_heads(t):
            tr = pltpu.roll(t, HEAD_DIM, 1)
            return jnp.concatenate([jnp.where(head0, t, tr), jnp.where(head0, tr, t)], axis=1)

        mm_bufs = ((bufs[0], bufs[1], bufs[2], bufs[3]), (bufs[4], bufs[5], bufs[6], bufs[7]))
        ds_bufs = ((bufs[8], bufs[9], bufs[10], bufs[11]), (bufs[12], bufs[13], bufs[14], bufs[15]))
        op_bufs = ((bufs[16], bufs[17], bufs[18], bufs[19]), (bufs[20], bufs[21], bufs[22], bufs[23]))
        vc_bufs, carry_k, carry_v = (bufs[24], bufs[25]), bufs[26], bufs[27]
        for buf in (op_bufs[0][0], op_bufs[1][0]) + vc_bufs:
            buf[...] = jnp.zeros_like(buf)

        def block_rows(i, d, nb):
            r, blk = i // nb, i % nb
            return pl.ds(blk * (BLK * d) + r, BLK, stride=d), blk > 0

        def unstack(st16):
            return st16[:BLK] + st16[BLK:]

        def products(i, par, d, nb):
            rows, has_prev = block_rows(i, d, nb)
            s_buf, dp_buf, sd_buf, dpd_buf = mm_bufs[par]
            kc_buf, kp_buf, q_buf, do_buf = op_bufs[par]
            q = q_ref[rows, :]
            qs = q * 0.125
            do = do_scr[rows, :]
            qs16, do16 = qs.astype(bf16), do.astype(bf16)
            kst_c = _stack_heads(k_ref[rows, :].astype(bf16), head0)
            vst_c = _stack_heads(v_ref[rows, :].astype(bf16), head0)
            kst_p, vst_p = op_bufs[1 - par][0][...], vc_bufs[1 - par][...]
            kc_buf[...] = kst_c
            kp_buf[...] = kst_p
            vc_bufs[par][...] = vst_c
            q_buf[...] = q.astype(bf16)
            do_buf[...] = do16
            s_buf[...] = jnp.where(tri2, _nt(qs16, kst_c), jnp.where(has_prev, _nt(qs16, kst_p), -jnp.inf))
            dp_buf[...] = jnp.where(tri2, _nt(do16, vst_c), jnp.where(has_prev, _nt(do16, vst_p), 0.0))
            sd_buf[...] = _nn((qs * unstack(kst_p).astype(f32)).astype(bf16), bones)
            dpd_buf[...] = jnp.where(has_prev, _nn((do * unstack(vst_p).astype(f32)).astype(bf16), bones), 0.0)

        def softmax_grad(i, par, d, nb):
            rows, has_prev = block_rows(i, d, nb)
            s_buf, dp_buf, sd_buf, dpd_buf = mm_bufs[par]
            p_buf, ds_buf, pd_buf, dsd_buf = ds_bufs[par]
            lse = l_ref[rows, :]
            dl = dl_scr[rows, :]
            pt = jnp.exp(s_buf[...] - both_heads(lse))
            ds_buf[...] = (pt * (dp_buf[...] - both_heads(dl)) * 0.125).astype(bf16)
            p_buf[...] = pt.astype(bf16)
            pd = jnp.where(has_prev, jnp.exp(sd_buf[...] - lse), 0.0)
            pd_buf[...] = pd
            dsd_buf[...] = pd * (dpd_buf[...] - dl) * 0.125

        def accumulate(i, par, d, nb):
            rows, _ = block_rows(i, d, nb)
            before, _ = block_rows(jnp.maximum(i - 1, 0), d, nb)
            p_buf, ds_buf, pd_buf, dsd_buf = ds_bufs[par]
            kc_buf, kp_buf, q_buf, do_buf = op_bufs[par]
            pt16, ds16, pd, dsd = p_buf[...], ds_buf[...], pd_buf[...], dsd_buf[...]
            zero = jnp.zeros_like(pt16)
            dsc, dsp = jnp.where(tri2, ds16, zero), jnp.where(tri2, zero, ds16)
            pc, pp = jnp.where(tri2, pt16, zero), jnp.where(tri2, zero, pt16)
            kst_c, kst_p, q16, do16 = kc_buf[...], kp_buf[...], q_buf[...], do_buf[...]
            qst, dost = _stack_heads(q16, head0), _stack_heads(do16, head0)
            dq_acc[rows, :] += _nn(dsc, kst_c) + _nn(dsp, kst_p) + dsd * unstack(kst_p).astype(f32)
            dk2 = _tn(jnp.concatenate([per_head(dsc), per_head(dsp)], axis=1), qst)
            dv2 = _tn(jnp.concatenate([per_head(pc), per_head(pp)], axis=1), dost)
            dk_acc[before, :] += carry_k[...] + dk2[BLK:] + dsd * q16.astype(f32)
            dv_acc[before, :] += carry_v[...] + dv2[BLK:] + pd * do16.astype(f32)
            carry_k[...] = dk2[:BLK]
            carry_v[...] = dv2[:BLK]

        for d in DILATIONS:
            nb = s // (BLK * d)
            carry_k[...] = jnp.zeros_like(carry_k)
            carry_v[...] = jnp.zeros_like(carry_v)
            products(0, 0, d, nb)
            products(1, 1, d, nb)
            softmax_grad(0, 0, d, nb)

            def steps(j, carry, d=d, nb=nb):
                for par in range(2):
                    t = 2 * j + 2 + par
                    accumulate(t - 2, par, d, nb)
                    products(t, par, d, nb)
                    softmax_grad(t - 1, 1 - par, d, nb)
                return carry

            lax.fori_loop(0, (n_it - 2) // 2, steps, 0)
            accumulate(n_it - 2, 0, d, nb)
            softmax_grad(n_it - 1, 1, d, nb)
            accumulate(n_it - 1, 1, d, nb)
            last, _ = block_rows(n_it - 1, d, nb)
            dk_acc[last, :] += carry_k[...]
            dv_acc[last, :] += carry_v[...]

        def epi(i, carry):
            rows = pl.ds(pl.multiple_of(i * 256, 256), 256)
            dq_ref[rows, :] = dq_acc[rows, :].astype(bf16)
            dk_ref[rows, :] = dk_acc[rows, :].astype(bf16)
            dv_ref[rows, :] = dv_acc[rows, :].astype(bf16)
            return carry

        lax.fori_loop(0, s // 256, epi, 0)

        if nsw:
            @pl.when(pl.program_id(0) == N_PAIRS - 1)
            def _():
                swap_copy.wait_send()
                swap_copy.wait_recv()

    col = lambda base: pl.BlockSpec((s, LANES), lambda h: (0, base + h))
    anyspec = pl.BlockSpec(memory_space=pl.ANY)
    swaps = [] if swap is None else [swap]
    outs = pl.pallas_call(
        body, name="attn_bwd", grid=(N_PAIRS,),
        in_specs=[col(0), col(8), col(16), col(24), col(0), col(0), col(0)] + [anyspec] * nsw,
        out_specs=[col(0)] * 4 + [anyspec] * nsw,
        out_shape=[SDS((s, D_ATTN), bf16)] * 4 + [SDS(a.shape, a.dtype) for a in swaps],
        scratch_shapes=[pltpu.VMEM((s, LANES), f32)] * 5
        + [pltpu.VMEM((BLK, 2 * LANES), f32)] * 2 + [pltpu.VMEM((BLK, LANES), f32)] * 2
        + [pltpu.VMEM((BLK, 2 * LANES), f32)] * 2 + [pltpu.VMEM((BLK, LANES), f32)] * 2
        + [pltpu.VMEM((BLK, 2 * LANES), bf16)] * 2 + [pltpu.VMEM((BLK, LANES), f32)] * 2
        + [pltpu.VMEM((BLK, 2 * LANES), bf16)] * 2 + [pltpu.VMEM((BLK, LANES), f32)] * 2
        + [pltpu.VMEM((2 * BLK, LANES), bf16)] * 2 + [pltpu.VMEM((BLK, LANES), bf16)] * 2
        + [pltpu.VMEM((2 * BLK, LANES), bf16)] * 2 + [pltpu.VMEM((BLK, LANES), bf16)] * 2
        + [pltpu.VMEM((2 * BLK, LANES), bf16)] * 2 + [pltpu.VMEM((BLK, LANES), f32)] * 2
        + [pltpu.SemaphoreType.DMA(())] * (2 * nsw),
        compiler_params=pltpu.CompilerParams(dimension_semantics=("arbitrary",)),
    )(proj, proj, proj, proj, o, lb, dmix, *swaps)
    return outs


def _ssd_bwd(proj, y, states, dmix, conv_w, conv_b, dtb16, alog16, alog_f, d_f, nw, chip_sums=()):
    s = proj.shape[0]
    nc = s // CHUNK
    gw = D_SSM // N_GROUPS
    nx = len(chip_sums)

    def body(*refs):
        (xs_ref, bc_ref, xs_tail, bc_tail, dt_ref, z_ref, y_ref, st_ref, dm_ref, cw_ref, cb_ref, dtb_ref,
         alog16_ref, alogf_ref, df_ref, nw_ref) = refs[:16]
        cs_in = refs[16:16 + nx]
        out_ref, gconv_ref, gvec_ref, gdt_ref = refs[16 + nx:20 + nx]
        cs_out = refs[20 + nx:20 + 2 * nx]
        (dh_scr, head_scr, xpad, dcpad, da_scr, dxdt_scr, dbc_scr, emat_ref, fold_ref) = refs[20 + 2 * nx:29 + 2 * nx]
        cs_sems = refs[29 + 2 * nx:]
        i = pl.program_id(0)
        c = nc - 1 - i

        if nx:
            @pl.when(i == 0)
            def _():
                mine, sends, _ = _chip_exchange_copies(cs_in, cs_out, *cs_sems)
                for cp in mine + sends:
                    cp.start()

            @pl.when(i == nc - 1)
            def _():
                mine, sends, recvs = _chip_exchange_copies(cs_in, cs_out, *cs_sems)
                for cp in recvs:
                    cp.wait_recv()
                for cp in sends:
                    cp.wait_send()
                for cp in mine:
                    cp.wait()

        @pl.when(i == 0)
        def _():
            emat_ref[...] = _expand_mat()
            fold_ref[...] = _fold_mat()
            dh_scr[...] = jnp.zeros_like(dh_scr)
            head_scr[...] = jnp.zeros_like(head_scr)
            gconv_ref[...] = jnp.zeros_like(gconv_ref)
            gvec_ref[...] = jnp.zeros_like(gvec_ref)
            gdt_ref[...] = jnp.zeros_like(gdt_ref)

        cv, sig, xbc, pre, dt_f, al_f, al_x, al_t, taps = _ssd_common(
            xs_ref, bc_ref, xs_tail, bc_tail, dt_ref, cw_ref, cb_ref, dtb_ref, alog16_ref, emat_ref, xpad, c == 0)
        head0 = _iota((CHUNK, LANES), 1) < HEAD_DIM
        sub = _iota((CHUNK, LANES), 0)
        last_row = sub == CHUNK - 1

        yv, z, dmx = y_ref[...], z_ref[...], dm_ref[...]
        sz = _sigmoid(z)
        silu = z * sz
        yz = yv * silu
        dyz_parts = []
        for g in range(N_GROUPS):
            gs = slice(g * gw, (g + 1) * gw)
            part = yz[:, gs]
            r = lax.rsqrt(jnp.mean(part * part, axis=-1, keepdims=True) + EPS)
            nh = part * r
            gvec_ref[0:1, gs] += jnp.sum(dmx[:, gs] * nh, axis=0, keepdims=True)
            dn = dmx[:, gs] * nw_ref[:, gs]
            dyz_parts.append(r * (dn - nh * jnp.mean(dn * nh, axis=-1, keepdims=True)))
        dyz = jnp.concatenate(dyz_parts, axis=1)
        dy = dyz * silu
        out_ref[:, 0:D_SSM] = (dyz * yv * (sz * (1.0 + z * (1.0 - sz)))).astype(bf16)

        x_all = xbc[:, 0:D_SSM]
        gvec_ref[2:3, :] += jnp.sum(dy * x_all, axis=0, keepdims=True)

        for g in range(N_GROUPS):
            bm = xbc[:, D_SSM + g * D_STATE: D_SSM + (g + 1) * D_STATE].astype(bf16)
            cm = xbc[:, D_SSM + (N_GROUPS + g) * D_STATE: D_SSM + (N_GROUPS + g + 1) * D_STATE].astype(bf16)
            gmat = _nt(cm, bm)
            dgm = jnp.zeros((CHUNK, CHUNK), f32)
            db = jnp.zeros((CHUNK, D_STATE), f32)
            dc = jnp.zeros((CHUNK, D_STATE), f32)
            for pair in range(4 * g, 4 * g + 4):
                sl = slice(pair * LANES, (pair + 1) * LANES)
                xp, dtp, alp, dyp = x_all[:, sl], dt_f[:, sl], al_f[:, sl], dy[:, sl]
                xdt = xp * dtp
                xdt16 = xdt.astype(bf16)
                al_last = alp[CHUNK - 1:CHUNK, :]
                e_l = jnp.exp(alp)
                wf = jnp.exp(al_last - alp)
                e_last = jnp.exp(al_last)
                hp = st_ref[:, sl]
                hp16 = hp.astype(bf16)
                dhn = dh_scr[:, sl]
                dhn16 = dhn.astype(bf16)
                y_off = e_l * _nn(cm, hp16)
                dch16 = (dyp * e_l).astype(bf16)
                dc = dc + _nt(dch16, hp16)
                dh_out = _tn(cm, dch16)
                dal = dyp * y_off
                xw16 = (wf * xdt).astype(bf16)
                db = db + _nt(xw16, dhn16)
                dxw = _nn(bm, dhn16)
                dxdt = dxw * wf
                dwf = dxw * xdt * wf
                dal = dal - dwf
                dal_last = jnp.sum(dwf, axis=0, keepdims=True) + jnp.sum(dhn * hp, axis=0, keepdims=True) * e_last
                dh_scr[:, sl] = e_last * dhn + dh_out
                for h in range(2):
                    mh = head0 if h == 0 else jnp.logical_not(head0)
                    dyh16 = jnp.where(mh, dyp, 0.0).astype(bf16)
                    lmat = _decay_mat(al_x, al_t, pair, h)
                    mm = gmat * lmat
                    dmm = _nt(dyh16, xdt16)
                    dxdt = dxdt + _tn(mm.astype(bf16), dyh16)
                    n16 = (dmm * mm).astype(bf16)
                    jh = jnp.where(mh, 1.0 / HEAD_DIM, 0.0).astype(bf16)
                    dal = dal + _nn(n16, jh) - _tn(n16, jh)
                    dgm = dgm + dmm * lmat
                da_scr[:, sl] = dal + jnp.where(last_row, dal_last, 0.0)
                dxdt_scr[:, sl] = dxdt
            dgm16 = dgm.astype(bf16)
            dbc_scr[:, g * D_STATE:(g + 1) * D_STATE] = db + _tn(dgm16, cm)
            dbc_scr[:, (N_GROUPS + g) * D_STATE:(N_GROUPS + g + 1) * D_STATE] = dc + _nn(dgm16, bm)

        sub_c, lane_c = _iota((CHUNK, CHUNK), 0), _iota((CHUNK, CHUNK), 1)
        tri_t = (lane_c >= sub_c).astype(bf16)
        dadt = _dot_01_left(tri_t, da_scr[...], 2)
        a_f = -jnp.exp(alogf_ref[...])
        dxdt_all = dxdt_scr[...]
        ddt_f = dxdt_all * x_all + a_f * dadt
        gvec_ref[1:2, :] += jnp.sum(dt_f * dadt, axis=0, keepdims=True) * a_f
        dx = df_ref[...] * dy + dxdt_all * dt_f
        ddt_raw = _dot_01(ddt_f, fold_ref[...], 2) * _sigmoid(pre)
        gdt_ref[0:1, :] += jnp.sum(ddt_raw, axis=0, keepdims=True)
        out_ref[:, D_SSM + D_CONV:D_SSM + D_CONV + LANES] = ddt_raw.astype(bf16)
        out_ref[:, D_SSM + D_CONV + LANES:] = jnp.zeros((CHUNK, 3 * LANES), bf16)

        dsil = sig * (1.0 + cv * (1.0 - sig))
        dcv_x = dx * dsil[:, 0:D_SSM]
        dcv_bc = dbc_scr[...] * dsil[:, D_SSM:]
        dcpad[0:CHUNK, 0:D_SSM] = dcv_x
        dcpad[0:CHUNK, D_SSM:] = dcv_bc
        dcpad[CHUNK:, :] = head_scr[...]
        dcp = dcpad[...]
        dcv = dcp[0:CHUNK]
        gconv_ref[4:5, :] += jnp.sum(dcv, axis=0, keepdims=True)
        draw = cw_ref[3:4, :] * dcv
        for j in range(4):
            gconv_ref[j:j + 1, :] += jnp.sum(dcv * taps[j], axis=0, keepdims=True)
        for j in range(3):
            draw = draw + cw_ref[j:j + 1, :] * pltpu.roll(dcp, CHUNK + 8 - (3 - j), 0)[0:CHUNK]
        head_scr[...] = dcv[0:8]
        out_ref[:, D_SSM:D_SSM + D_CONV] = draw.astype(bf16)

    order = lambda i: nc - 1 - i
    row = lambda w, cb=0: pl.BlockSpec((CHUNK, w), lambda i: (nc - 1 - i, cb))
    anyspec = pl.BlockSpec(memory_space=pl.ANY)
    outs = pl.pallas_call(
        body, name="ssd_bwd", grid=(nc,),
        in_specs=_ssd_in_specs(order) + [row(D_SSM), pl.BlockSpec((None, D_STATE, D_SSM), lambda i: (nc - 1 - i, 0, 0)),
                                         row(D_SSM, 1), _full((4, D_CONV)), _full((1, D_CONV)), _full((1, LANES)),
                                         _full((1, LANES)), _full((1, D_SSM)), _full((1, D_SSM)), _full((1, D_SSM))]
        + [anyspec] * nx,
        out_specs=[row(3072), _full((8, D_CONV)), _full((8, D_SSM)), _full((8, LANES))] + [anyspec] * nx,
        out_shape=[SDS((s, 3072), bf16), SDS((8, D_CONV), f32), SDS((8, D_SSM), f32), SDS((8, LANES), f32)]
        + [SDS(a.shape, a.dtype) for a in chip_sums],
        scratch_shapes=[pltpu.VMEM((D_STATE, D_SSM), f32), pltpu.VMEM((8, D_CONV), f32),
                        pltpu.VMEM((8 + CHUNK, D_CONV), f32), pltpu.VMEM((8 + CHUNK, D_CONV), f32),
                        pltpu.VMEM((CHUNK, D_SSM), f32), pltpu.VMEM((CHUNK, D_SSM), f32),
                        pltpu.VMEM((CHUNK, 2 * N_GROUPS * D_STATE), f32),
                        pltpu.VMEM((LANES, 2 * D_SSM), bf16), pltpu.VMEM((D_SSM, LANES), bf16)]
        + (_chip_exchange_scratch(nx) if nx else []),
        compiler_params=pltpu.CompilerParams(dimension_semantics=("arbitrary",)),
    )(proj, proj, proj, proj, proj, proj, y, states, dmix, conv_w, conv_b, dtb16, alog16, alog_f, d_f, nw, *chip_sums)
    return outs[0], outs[1], outs[2], outs[3], outs[4:]


def _col_blocks(parts, tile):
    counts = [p.shape[1] // tile for p in parts]
    offs = [sum(counts[:t]) for t in range(len(parts))]
    return offs, counts, sum(counts)


def _bcast_copies(src_ref, out_ref, send_sems, recv_sems, local_sem):
    x, y, c = _my_pos()
    me = 4 * x + 2 * y + c
    mine = pltpu.make_async_copy(src_ref, out_ref.at[me], local_sem)
    sends, recvs = [], []
    for k in range(1, N_DEV):
        to, frm = (me + k) % N_DEV, (me + N_DEV - k) % N_DEV
        sems = dict(send_sem=send_sems.at[k - 1], recv_sem=recv_sems.at[k - 1], device_id_type=MESH)
        sends.append(pltpu.make_async_remote_copy(
            src_ref=src_ref, dst_ref=out_ref.at[me], device_id=(to // 4, (to // 2) % 2, to % 2), **sems))
        recvs.append(pltpu.make_async_remote_copy(
            src_ref=src_ref, dst_ref=out_ref.at[frm], device_id=(x, y, c), **sems))
    return mine, sends, recvs


def _bcast_scratch():
    return [pltpu.SemaphoreType.DMA((N_DEV - 1,)), pltpu.SemaphoreType.DMA((N_DEV - 1,)), pltpu.SemaphoreType.DMA(())]


def _inproj_bwd(dparts, wt, x, nw, dres, chip_sums=(), pack=None):
    s, d = x.shape
    tm, tk = 1024, 1024
    offs, counts, nk = _col_blocks(dparts, tk)
    npart, nx = len(dparts), len(chip_sums)
    npk = 0 if pack is None else 1
    ni = s // tm

    def body(*refs):
        dp_refs = refs[:npart]
        w_ref, x_ref, nw_ref, dres_ref = refs[npart:npart + 4]
        pos = npart + 4
        cs_in, pos = refs[pos:pos + nx], pos + nx
        pack_in, pos = refs[pos:pos + npk], pos + npk
        (gx_ref, gnw_ref), pos = refs[pos:pos + 2], pos + 2
        cs_out, pos = refs[pos:pos + nx], pos + nx
        pack_out, pos = refs[pos:pos + 2 * npk], pos + 2 * npk
        acc, pos = refs[pos], pos + 1
        cs_sems, pos = refs[pos:pos + 3 * min(nx, 1)], pos + 3 * min(nx, 1)
        pk_refs = refs[pos:]
        i, k = pl.program_id(0), pl.program_id(1)

        def exchange():
            return _chip_exchange_copies(cs_in, cs_out, *cs_sems)

        def pack_copies():
            return _bcast_copies(pack_in[0], pack_out[0], *pk_refs[1:4])

        def gnw_copies():
            return _bcast_copies(pk_refs[0], pack_out[1], *pk_refs[4:7])

        @pl.when(jnp.logical_and(i == 0, k == 0))
        def _():
            gnw_ref[...] = jnp.zeros_like(gnw_ref)
            if nx:
                mine, sends, _ = exchange()
                for cp in mine + sends:
                    cp.start()
            if npk:
                mine, sends, _ = pack_copies()
                for cp in [mine] + sends:
                    cp.start()

        @pl.when(k == 0)
        def _():
            acc[...] = _nn(dp_refs[0][...], w_ref[...])

        for t in range(npart):
            @pl.when(jnp.logical_and(k >= max(offs[t], 1), k < offs[t] + counts[t]))
            def _(t=t):
                acc[...] += _nn(dp_refs[t][...], w_ref[...])

        @pl.when(k == nk - 1)
        def _():
            xv = x_ref[...]
            r = lax.rsqrt(jnp.mean(xv * xv, axis=-1, keepdims=True) + EPS)
            xn = xv * r
            du = acc[...]
            gnw_ref[0:1, :] += jnp.sum(du * xn, axis=0, keepdims=True)
            dn = du * nw_ref[...]
            gx_ref[...] = dres_ref[...] + r * (dn - xn * jnp.mean(dn * xn, axis=-1, keepdims=True))

        @pl.when(jnp.logical_and(i == ni - 1, k == nk - 1))
        def _():
            if npk:
                pk_refs[0][...] = gnw_ref[...]
                mine, sends, _ = gnw_copies()
                for cp in [mine] + sends:
                    cp.start()
            if nx:
                mine, sends, recvs = exchange()
                for cp in recvs:
                    cp.wait_recv()
                for cp in sends:
                    cp.wait_send()
                for cp in mine:
                    cp.wait()
            if npk:
                for copies in (pack_copies(), gnw_copies()):
                    mine, sends, recvs = copies
                    for cp in recvs:
                        cp.wait_recv()
                    for cp in sends:
                        cp.wait_send()
                    mine.wait()

    def piece(t):
        return pl.BlockSpec((tm, tk), lambda i, k: (i, jnp.clip(k - offs[t], 0, counts[t] - 1)))

    anyspec = pl.BlockSpec(memory_space=pl.ANY)
    packs = [] if pack is None else [pack]
    pack_shapes = [] if pack is None else [SDS((N_DEV,) + pack.shape, f32), SDS((N_DEV, 8, d), f32)]
    scratch = [pltpu.VMEM((tm, d), f32)] + (_chip_exchange_scratch(nx) if nx else [])
    if npk:
        scratch += [pltpu.VMEM((8, d), f32)] + _bcast_scratch() + _bcast_scratch()
    outs = pl.pallas_call(
        body, name="inproj_bwd", grid=(ni, nk),
        in_specs=[piece(t) for t in range(npart)] + [
            pl.BlockSpec((tk, d), lambda i, k: (k, 0)),
            pl.BlockSpec((tm, d), lambda i, k: (i, 0)), pl.BlockSpec((1, d), lambda i, k: (0, 0)),
            pl.BlockSpec((tm, d), lambda i, k: (i, 0))] + [anyspec] * (nx + npk),
        out_specs=[pl.BlockSpec((tm, d), lambda i, k: (i, 0)), pl.BlockSpec((8, d), lambda i, k: (0, 0))]
        + [anyspec] * (nx + 2 * npk),
        out_shape=[SDS((s, d), f32), SDS((8, d), f32)] + [SDS(a.shape, a.dtype) for a in chip_sums] + pack_shapes,
        scratch_shapes=scratch,
        compiler_params=pltpu.CompilerParams(dimension_semantics=("arbitrary", "arbitrary")),
    )(*dparts, wt, x, nw, dres, *chip_sums, *packs)
    return outs[0], outs[1], outs[2:2 + nx], outs[2 + nx:]


def _matmul_tn(a_parts, b_parts, name):
    tile, tk = 1024, 1024
    s = a_parts[0].shape[0]
    nk = s // tk
    na, nb = len(a_parts), len(b_parts)
    offs_a, counts_a, ni = _col_blocks(a_parts, tile)
    offs_b, counts_b, nj = _col_blocks(b_parts, tile)

    def body(*refs):
        a_refs, b_refs, o_ref = refs[:na], refs[na:na + nb], refs[na + nb]
        i, j = pl.program_id(0), pl.program_id(1)

        @pl.when(pl.program_id(2) == 0)
        def _():
            o_ref[...] = jnp.zeros_like(o_ref)

        for ta in range(na):
            for tb in range(nb):
                in_a = jnp.logical_and(i >= offs_a[ta], i < offs_a[ta] + counts_a[ta])
                in_b = jnp.logical_and(j >= offs_b[tb], j < offs_b[tb] + counts_b[tb])

                @pl.when(jnp.logical_and(in_a, in_b))
                def _(ta=ta, tb=tb):
                    o_ref[...] += _tn(a_refs[ta][...], b_refs[tb][...])

    def spec(offs, counts, t, axis):
        def index(i, j, k):
            pos = (i, j)[axis]
            mine = jnp.logical_and(pos >= offs[t], pos < offs[t] + counts[t])
            return jnp.where(mine, k, 0), jnp.clip(pos - offs[t], 0, counts[t] - 1)
        return pl.BlockSpec((tk, tile), index)

    return pl.pallas_call(
        body, name=name, grid=(ni, nj, nk),
        in_specs=[spec(offs_a, counts_a, t, 0) for t in range(na)] + [spec(offs_b, counts_b, t, 1) for t in range(nb)],
        out_specs=pl.BlockSpec((tile, tile), lambda i, j, k: (i, j)),
        out_shape=SDS((ni * tile, nj * tile), f32),
        compiler_params=pltpu.CompilerParams(dimension_semantics=("parallel", "parallel", "arbitrary")),
    )(*a_parts, *b_parts)


def _adamw(w, g, m, v):
    m = ADAM_B1 * m + (1.0 - ADAM_B1) * g
    v = ADAM_B2 * v + (1.0 - ADAM_B2) * (g * g)
    m_hat = m / (1.0 - ADAM_B1 ** ADAM_STEP)
    v_hat = v / (1.0 - ADAM_B2 ** ADAM_STEP)
    delta = -ADAM_LR * (m_hat / (jnp.sqrt(v_hat) + ADAM_EPS) + ADAM_WD * w)
    return delta, m, v


def _sum_adamw(parts, w, m, v, name):
    r, c = w.shape
    tc = 256

    def body(p_ref, w_ref, m_ref, v_ref, g_ref, d_ref, nm_ref, nv_ref):
        g = p_ref[0].astype(f32)
        for q in range(1, 4):
            g = g + p_ref[q].astype(f32)
        g_ref[...] = g
        d_ref[...], nm_ref[...], nv_ref[...] = _adamw(w_ref[...], g, m_ref[...], v_ref[...])

    blk = pl.BlockSpec((r, tc), lambda i: (0, i))
    return pl.pallas_call(
        body, name=name, grid=(c // tc,),
        in_specs=[pl.BlockSpec((4, r, tc), lambda i: (0, 0, i)), blk, blk, blk],
        out_specs=[blk] * 4, out_shape=[SDS((r, c), f32)] * 4,
        compiler_params=pltpu.CompilerParams(dimension_semantics=("parallel",)),
    )(parts, w, m, v)


def _sum_small(parts, pre_blocks):
    def body(p_ref, b_ref, o_ref):
        t = p_ref[0]
        pre = b_ref[0]
        for j in range(1, N_DEV):
            t = t + p_ref[j]
            pre = pre + b_ref[j]
        o_ref[...] = t
        o_ref[5:6, 0:D_MODEL] = pre[0:1, :]
        row_h = _iota((D_SSM, LANES), 0) // HEAD_DIM
        fold = (row_h == _iota((D_SSM, LANES), 1)).astype(f32)
        lower = t[8:16, 0:LANES]
        folded = _nn_hi(t[8:16, 0:D_SSM], fold)
        loss = jnp.sum(t[11:12, 0:D_MODEL], axis=1, keepdims=True) * (0.5 / D_MODEL)
        row = _iota((8, LANES), 0)
        o_ref[8:16, 0:LANES] = jnp.where(row < 2, folded, jnp.where(row == 4, loss, lower))

    return pl.pallas_call(body, name="sum_small", out_shape=SDS((PACK_ROWS, PACK_W), f32),
                          in_specs=[pl.BlockSpec(memory_space=pltpu.VMEM)] * 2,
                          out_specs=pl.BlockSpec(memory_space=pltpu.VMEM))(parts, pre_blocks)


def _adamw_small(w, g, m, v):
    def body(w_ref, g_ref, m_ref, v_ref, d_ref, nm_ref, nv_ref):
        d_ref[...], nm_ref[...], nv_ref[...] = _adamw(w_ref[...], g_ref[...], m_ref[...], v_ref[...])

    vm = pl.BlockSpec(memory_space=pltpu.VMEM)
    return pl.pallas_call(body, name="adamw_small", out_shape=[SDS(w.shape, f32)] * 3,
                          in_specs=[vm] * 4, out_specs=[vm] * 3)(w, g, m, v)


def _pad_lanes(v, width):
    return jnp.pad(v, ((0, 0), (0, width - v.shape[1])))


def _local_step(x, tgt, norm_pre_w, wt, conv_w, conv_b, dt_bias, a_log, d_skip, ssm_norm_w, wo, norm_post_w, sharded):
    dtb16 = _pad_lanes(dt_bias, LANES)
    alog16 = _pad_lanes(a_log, LANES)
    alog_f = jnp.repeat(a_log, HEAD_DIM, axis=1)
    d_f = jnp.repeat(d_skip, HEAD_DIM, axis=1)

    shard_out = wo.shape[0]
    if sharded:
        proj, u, (g_out, g_cw) = _prenorm_inproj(x, norm_pre_w, wt, gather=(wo, conv_w))
        wo = g_out.reshape(N_DEV * shard_out, D_MODEL)
        conv_w = g_cw.transpose(1, 0, 2).reshape(4, D_CONV)
    else:
        proj, u, _ = _prenorm_inproj(x, norm_pre_w, wt)
    o, lb, mix_a = _attn_fwd(proj)
    mix_s, y, states = _ssd_fwd(proj, conv_w, conv_b, dtb16, alog16, alog_f, d_f, ssm_norm_w)
    dmix, dout, dres, acc_post = _outproj_loss(mix_a, mix_s, wo, x, tgt, norm_post_w)
    dw_out = _matmul_tn([mix_a, mix_s], [dout], "dw_out")
    ssd_args = (proj, y, states, dmix, conv_w, conv_b, dtb16, alog16, alog_f, d_f, ssm_norm_w)
    if sharded:
        dq, dk, dv, dg, got_out = _attn_bwd(proj, o, lb, dmix, swap=dw_out)
        chip_out = _chip_sum(dw_out, got_out, shard_out, "chip_sum_w_out")
        dzxd, g_conv, g_vec, g_dt, (parts_out,) = _ssd_bwd(*ssd_args, chip_sums=[chip_out])
    else:
        dq, dk, dv, dg = _attn_bwd(proj, o, lb, dmix)
        dzxd, g_conv, g_vec, g_dt, _ = _ssd_bwd(*ssd_args)
    dparts = [dq, dk, dv, dg, dzxd]

    def pack(g_pre_row):
        return jnp.concatenate(
            [g_conv[0:5], g_pre_row, _pad_lanes(g_vec[0:1], PACK_W), _pad_lanes(acc_post[1:2], PACK_W),
             _pad_lanes(g_vec[1:3], PACK_W), _pad_lanes(g_dt[0:1], PACK_W), _pad_lanes(acc_post[0:1], PACK_W),
             jnp.zeros((4, PACK_W), f32)], axis=0)

    if sharded:
        dw_in, got_in = _dw_in_swap(dparts, u)
        chip_in = _chip_sum(dw_in, got_in, D_IN_PROJ // N_DEV, "chip_sum_w_in")
        grad_x, _, (parts_in,), small = _inproj_bwd(dparts, wt, x, norm_pre_w, dres, [chip_in],
                                                    pack(jnp.zeros((1, PACK_W), f32)))
        return grad_x, (parts_in, parts_out), small
    dw_in = _matmul_tn(dparts, [u], "dw_in")
    grad_x, g_pre, _, _ = _inproj_bwd(dparts, wt, x, norm_pre_w, dres)
    return grad_x, (dw_in, dw_out), pack(_pad_lanes(g_pre[0:1], PACK_W))


def kernel(x, norm_pre_w, w_in, conv_w, conv_b, dt_bias, a_log, d_skip, ssm_norm_w, w_out, norm_post_w, loss_target, m_norm_pre_w, m_w_in, m_conv_w, m_conv_b, m_dt_bias, m_a_log, m_d_skip, m_ssm_norm_w, m_w_out, m_norm_post_w, v_norm_pre_w, v_w_in, v_conv_w, v_conv_b, v_dt_bias, v_a_log, v_d_skip, v_ssm_norm_w, v_w_out, v_norm_post_w):
    shard_in = w_in.shape[2]
    shard_cv = conv_w.shape[2]
    me = 4 * lax.axis_index("x") + 2 * lax.axis_index("y") + lax.axis_index("c")

    g_in, = _all_gather([w_in[0].T.astype(bf16)])
    wt = _assemble_wt(g_in)

    grad_x, (parts_in, parts_out), (parts_small, pre_blocks) = _local_step(
        x[0], loss_target[0], norm_pre_w, wt, conv_w[0], conv_b, dt_bias, a_log, d_skip, ssm_norm_w,
        w_out[0].astype(bf16), norm_post_w, sharded=True)

    g_w_in, d_w_in, nm_w_in, nv_w_in = (a.T for a in _sum_adamw(
        parts_in, w_in[0].T, m_w_in[0].T, v_w_in[0].T, "sum_adamw_w_in"))
    g_w_out, d_w_out, nm_w_out, nv_w_out = _sum_adamw(parts_out, w_out[0], m_w_out[0], v_w_out[0], "sum_adamw_w_out")
    tot = _sum_small(parts_small, pre_blocks)

    g_cw_all = tot[0:4]
    small_g = {
        "conv_w": lax.dynamic_slice(g_cw_all, (0, me * shard_cv), (4, shard_cv)),
        "conv_b": tot[4:5], "norm_pre_w": tot[5:6, :D_MODEL], "ssm_norm_w": tot[6:7, :D_SSM],
        "norm_post_w": tot[7:8, :D_MODEL], "a_log": tot[8:9, :16], "d_skip": tot[9:10, :16], "dt_bias": tot[10:11, :16],
    }
    loss = tot[12, 0]
    small_w = {"conv_w": (conv_w[0], m_conv_w[0], v_conv_w[0]), "conv_b": (conv_b, m_conv_b, v_conv_b),
               "norm_pre_w": (norm_pre_w, m_norm_pre_w, v_norm_pre_w), "ssm_norm_w": (ssm_norm_w, m_ssm_norm_w, v_ssm_norm_w),
               "norm_post_w": (norm_post_w, m_norm_post_w, v_norm_post_w), "a_log": (a_log, m_a_log, v_a_log),
               "d_skip": (d_skip, m_d_skip, v_d_skip), "dt_bias": (dt_bias, m_dt_bias, v_dt_bias)}
    names = list(small_w)
    sizes = [small_g[k].size for k in names]
    tot_size = sum(sizes)
    pad_to = -(-tot_size // 1024) * 1024

    def flat(arrs):
        v = jnp.concatenate([a.reshape(-1) for a in arrs])
        return jnp.pad(v, (0, pad_to - tot_size)).reshape(pad_to // LANES, LANES)

    fw = flat([small_w[k][0] for k in names])
    fg = flat([small_g[k] for k in names])
    fm = flat([small_w[k][1] for k in names])
    fv = jnp.pad(jnp.concatenate([small_w[k][2].reshape(-1) for k in names]), (0, pad_to - tot_size),
                 constant_values=1.0).reshape(pad_to // LANES, LANES)
    fd, fnm, fnv = _adamw_small(fw, fg, fm, fv)

    def unflat(f):
        out, off = {}, 0
        v = f.reshape(-1)
        for k, n in zip(names, sizes):
            out[k] = v[off:off + n].reshape(small_g[k].shape)
            off += n
        return out

    sd, snm, snv = unflat(fd), unflat(fnm), unflat(fnv)
    lead = lambda a: a[None]
    order = ["norm_pre_w", "w_in", "conv_w", "conv_b", "dt_bias", "a_log", "d_skip", "ssm_norm_w", "w_out", "norm_post_w"]
    grads = dict(small_g, w_in=g_w_in, w_out=g_w_out)
    deltas = dict(sd, w_in=d_w_in, w_out=d_w_out)
    new_m = dict(snm, w_in=nm_w_in, w_out=nm_w_out)
    new_v = dict(snv, w_in=nv_w_in, w_out=nv_w_out)

    def shaped(dct, k):
        a = dct[k]
        return lead(a) if k in ("w_in", "w_out", "conv_w") else a

    return (loss, grad_x[None], *[shaped(grads, k) for k in order], *[shaped(deltas, k) for k in order],
            *[shaped(new_m, k) for k in order], *[shaped(new_v, k) for k in order])
```

```python
import jax
import jax.numpy as jnp
from jax import lax
from jax.experimental import pallas as pl
from jax.experimental.pallas import tpu as pltpu

f32, bf16 = jnp.float32, jnp.bfloat16
SDS = jax.ShapeDtypeStruct
HIGHEST = lax.Precision.HIGHEST
MESH = pl.DeviceIdType.MESH

N_DEV = 8
D_MODEL = 1024
D_ATTN = 1024
D_SSM = 1024
HEAD_DIM = 64
N_PAIRS = 8
D_STATE = 128
N_GROUPS = 2
D_CONV = D_SSM + 2 * N_GROUPS * D_STATE
D_IN_PROJ = 4 * D_ATTN + D_SSM + D_CONV + 16
NP = 7168
CHUNK = 128
BLK = 128
DILATIONS = (1, 4, 16)
EPS = 1e-6
LANES = 128
COL_Z, COL_XS, COL_BC, COL_DT = 4096, 5120, 6144, 6656

ADAM_LR, ADAM_B1, ADAM_B2, ADAM_EPS, ADAM_WD, ADAM_STEP = 0.001, 0.9, 0.999, 1e-08, 0.01, 10

PACK_ROWS, PACK_W = 16, 1536


def _nt(a, b):
    return lax.dot_general(a, b, (((1,), (1,)), ((), ())), preferred_element_type=f32)


def _tn(a, b):
    return lax.dot_general(a, b, (((0,), (0,)), ((), ())), preferred_element_type=f32)


def _nn(a, b):
    return jnp.dot(a, b, preferred_element_type=f32)


def _nn_hi(a, b):
    return jnp.dot(a, b, precision=HIGHEST, preferred_element_type=f32)


def _sigmoid(x):
    return 1.0 / (1.0 + jnp.exp(-x))


def _softplus(x):
    return jnp.maximum(x, 0.0) + jnp.log1p(jnp.exp(-jnp.abs(x)))


def _iota(shape, dim):
    return lax.broadcasted_iota(jnp.int32, shape, dim)


def _my_pos():
    return lax.axis_index("x"), lax.axis_index("y"), lax.axis_index("c")


GATHER_SEMS = 9


def _gather_phases(ins, outs, send_sems, recv_sems, local_sems):
    n, ns = len(ins), GATHER_SEMS
    x, y, c = _my_pos()
    me, sibling = (x, y, c), (x, y, 1 - c)
    xn, yn, diag = (1 - x, y), (x, 1 - y), (1 - x, 1 - y)

    def slot(a, px, py, pc):
        return outs[a].at[4 * px + 2 * py + pc]

    def part(a, ref, h):
        width = ins[a].shape[-1]
        if width % (2 * LANES):
            return ref if h == 1 else None
        return ref.at[:, pl.ds(h * (width // 2), width // 2)]

    def copy(a, k, block, to, src=None, h=None):
        src_ref = slot(a, *block) if src is None else src
        dst_ref = slot(a, *block)
        if h is not None:
            src_ref, dst_ref = part(a, src_ref, h), part(a, dst_ref, h)
            if src_ref is None:
                return None
        return pltpu.make_async_remote_copy(
            src_ref=src_ref, dst_ref=dst_ref, send_sem=send_sems.at[ns * a + k], recv_sem=recv_sems.at[ns * a + k],
            device_id=to, device_id_type=MESH)

    def mine():
        return [pltpu.make_async_copy(ins[a], slot(a, *me), local_sems.at[a]) for a in range(n)]

    def own_sends(a):
        return [copy(a, 0, me, sibling, src=ins[a]), copy(a, 1, me, (*xn, c), src=ins[a]),
                copy(a, 2, me, (*yn, c), src=ins[a])]

    def neighbour_relays(a):
        return [copy(a, 4, (*xn, c), sibling), copy(a, 7, (*xn, c), (*yn, c), h=1),
                copy(a, 5, (*yn, c), sibling), copy(a, 8, (*yn, c), (*xn, c), h=0)]

    def diagonal_halves(a):
        return [copy(a, k, (*diag, c), me, h=h) for k, h in ((8, 0), (7, 1))]

    def start_all(cps):
        for cp in cps:
            if cp is not None:
                cp.start()

    def phase0():
        start_all(mine())
        for a in range(n):
            start_all(own_sends(a))

    def phase1():
        for a in range(n):
            copy(a, 1, (*xn, c), me).wait_recv()
            copy(a, 2, (*yn, c), me).wait_recv()
            start_all(neighbour_relays(a))

    def phase2():
        for a in range(n):
            for cp in diagonal_halves(a):
                if cp is not None:
                    cp.wait_recv()
            copy(a, 6, (*diag, c), sibling).start()

    def finish():
        for a in range(n):
            copy(a, 0, sibling, me).wait_recv()
            for j, chip in enumerate((xn, yn, diag)):
                copy(a, 4 + j, (*chip, 1 - c), me).wait_recv()
        for a in range(n):
            for cp in own_sends(a) + neighbour_relays(a) + [copy(a, 6, (*diag, c), sibling)]:
                if cp is not None:
                    cp.wait_send()
        for cp in mine():
            cp.wait()

    return phase0, phase1, phase2, finish


def _gather_scratch(n):
    return [pltpu.SemaphoreType.DMA((GATHER_SEMS * n,)), pltpu.SemaphoreType.DMA((GATHER_SEMS * n,)),
            pltpu.SemaphoreType.DMA((n,))]


def _all_gather(arrs):
    n = len(arrs)

    def body(*refs):
        for phase in _gather_phases(refs[:n], refs[n:2 * n], *refs[2 * n:]):
            phase()

    anyspec = pl.BlockSpec(memory_space=pl.ANY)
    return pl.pallas_call(
        body, name="weights_all_gather",
        out_shape=[SDS((N_DEV,) + a.shape, a.dtype) for a in arrs],
        in_specs=[anyspec] * n, out_specs=[anyspec] * n, scratch_shapes=_gather_scratch(n),
    )(*arrs)


def _dw_in_swap(a_parts, u):
    tile, tk = 1024, 1024
    s = u.shape[0]
    nk = s // tk
    na = len(a_parts)
    offs, counts, ni = _col_blocks(a_parts, tile)

    def body(*refs):
        a_refs, u_ref = refs[:na], refs[na]
        dw_ref, got_ref = refs[na + 1:na + 3]
        acc, stage, local_sems, send_sems, recv_sem = refs[na + 3:]
        i, k = pl.program_id(0), pl.program_id(1)
        x, y, c = _my_pos()
        par = i % 2

        def tile_copies(t, p):
            rows = pl.ds(pl.multiple_of(t * tile, tile), tile)
            loc = pltpu.make_async_copy(stage.at[p], dw_ref.at[rows], local_sems.at[p])
            rem = pltpu.make_async_remote_copy(
                src_ref=stage.at[p], dst_ref=got_ref.at[rows], send_sem=send_sems.at[p], recv_sem=recv_sem,
                device_id=(x, y, 1 - c), device_id_type=MESH)
            return loc, rem

        @pl.when(k == 0)
        def _():
            acc[...] = jnp.zeros((tile, tile), f32)

        for t in range(na):
            @pl.when(jnp.logical_and(i >= offs[t], i < offs[t] + counts[t]))
            def _(t=t):
                acc[...] += _tn(a_refs[t][...], u_ref[...])

        @pl.when(k == nk - 1)
        def _():
            @pl.when(i >= 2)
            def _():
                loc, rem = tile_copies(i - 2, par)
                loc.wait()
                rem.wait_send()
            stage[par] = acc[...]
            loc, rem = tile_copies(i, par)
            loc.start()
            rem.start()

        @pl.when(jnp.logical_and(i == ni - 1, k == nk - 1))
        def _():
            for t in (ni - 2, ni - 1):
                loc, rem = tile_copies(t, t % 2)
                loc.wait()
                rem.wait_send()
            pltpu.make_async_remote_copy(src_ref=dw_ref, dst_ref=got_ref, send_sem=send_sems.at[0], recv_sem=recv_sem,
                                         device_id=(x, y, c), device_id_type=MESH).wait_recv()

    def a_spec(t):
        def index(i, k):
            mine = jnp.logical_and(i >= offs[t], i < offs[t] + counts[t])
            return jnp.where(mine, k, 0), jnp.clip(i - offs[t], 0, counts[t] - 1)
        return pl.BlockSpec((tk, tile), index)

    anyspec = pl.BlockSpec(memory_space=pl.ANY)
    return pl.pallas_call(
        body, name="dw_in_swap", grid=(ni, nk),
        in_specs=[a_spec(t) for t in range(na)] + [pl.BlockSpec((tk, tile), lambda i, k: (k, 0))],
        out_specs=[anyspec] * 2,
        out_shape=[SDS((ni * tile, tile), f32), SDS((ni * tile, tile), f32)],
        scratch_shapes=[pltpu.VMEM((tile, tile), f32), pltpu.VMEM((2, tile, tile), f32), pltpu.SemaphoreType.DMA((2,)),
                        pltpu.SemaphoreType.DMA((2,)), pltpu.SemaphoreType.DMA(())],
        compiler_params=pltpu.CompilerParams(dimension_semantics=("arbitrary", "arbitrary")),
    )(*a_parts, u)


def _chip_sum(mine, got, rows, name):
    r, cdim = mine.shape
    tc = LANES

    def body(m_ref, g_ref, s16_ref):
        c = lax.axis_index("c")
        for q in range(4):
            blk = pl.ds(rows * (2 * q + c), rows)
            s16_ref[q] = (m_ref[blk, :] + g_ref[blk, :]).astype(bf16)

    col = pl.BlockSpec((r, tc), lambda i: (0, i))
    return pl.pallas_call(
        body, name=name, grid=(cdim // tc,), in_specs=[col, col],
        out_specs=pl.BlockSpec((4, rows, tc), lambda i: (0, 0, i)), out_shape=SDS((4, rows, cdim), bf16),
        compiler_params=pltpu.CompilerParams(dimension_semantics=("parallel",)),
    )(mine, got)


def _assemble_wt(shards):
    nd, rows, cdim = shards.shape
    tc = 256

    def body(g_ref, o_ref):
        for j in range(nd):
            o_ref[pl.ds(rows * j, rows), :] = g_ref[j]
        o_ref[pl.ds(nd * rows, NP - nd * rows), :] = jnp.zeros((NP - nd * rows, tc), shards.dtype)

    return pl.pallas_call(
        body, name="assemble_w_in", grid=(cdim // tc,),
        in_specs=[pl.BlockSpec((nd, rows, tc), lambda i: (0, 0, i))],
        out_specs=pl.BlockSpec((NP, tc), lambda i: (0, i)), out_shape=SDS((NP, cdim), shards.dtype),
        compiler_params=pltpu.CompilerParams(dimension_semantics=("parallel",)),
    )(shards)


def _chip_exchange_copies(ins, outs, send_sems, recv_sems, local_sems):
    nb = len(ins)
    x, y, c = _my_pos()
    my_q = 2 * x + y
    mine = [pltpu.make_async_copy(ins[a].at[my_q], outs[a].at[my_q], local_sems.at[a]) for a in range(nb)]
    sends, recvs = [], []
    for k in range(1, 4):
        to, frm = (my_q + k) % 4, (my_q + 4 - k) % 4
        for a in range(nb):
            sems = dict(send_sem=send_sems.at[3 * a + k - 1], recv_sem=recv_sems.at[3 * a + k - 1], device_id_type=MESH)
            sends.append(pltpu.make_async_remote_copy(
                src_ref=ins[a].at[to], dst_ref=outs[a].at[my_q], device_id=(to // 2, to % 2, c), **sems))
            recvs.append(pltpu.make_async_remote_copy(
                src_ref=ins[a].at[frm], dst_ref=outs[a].at[frm], device_id=(x, y, c), **sems))
    return mine, sends, recvs


def _chip_exchange_scratch(nb):
    return [pltpu.SemaphoreType.DMA((3 * nb,)), pltpu.SemaphoreType.DMA((3 * nb,)), pltpu.SemaphoreType.DMA((nb,))]


def _prenorm_inproj(x, nw, wt, gather=()):
    s, d = x.shape
    npad = wt.shape[0]
    tm, tn = 1024, 1024
    ng = len(gather)
    ni, nj = s // tm, npad // tn

    def body(x_ref, nw_ref, w_ref, *refs):
        g_in, (proj_ref, u_ref), g_out, sems = refs[:ng], refs[ng:ng + 2], refs[ng + 2:2 * ng + 2], refs[2 * ng + 2:]
        i, j = pl.program_id(0), pl.program_id(1)
        if ng:
            phases = _gather_phases(g_in, g_out, *sems)
            for step, phase in enumerate(phases[:3]):
                @pl.when(jnp.logical_and(i == step, j == 0))
                def _(phase=phase):
                    phase()

        @pl.when(j == 0)
        def _():
            xv = x_ref[...]
            r = lax.rsqrt(jnp.mean(xv * xv, axis=-1, keepdims=True) + EPS)
            u_ref[...] = (xv * r * nw_ref[...]).astype(bf16)
        proj_ref[...] = _nt(u_ref[...], w_ref[...])

        if ng:
            @pl.when(jnp.logical_and(i == ni - 1, j == nj - 1))
            def _():
                phases[3]()

    anyspec = pl.BlockSpec(memory_space=pl.ANY)
    outs = pl.pallas_call(
        body, name="prenorm_inproj", grid=(ni, nj),
        in_specs=[pl.BlockSpec((tm, d), lambda i, j: (i, 0)), pl.BlockSpec((1, d), lambda i, j: (0, 0)),
                  pl.BlockSpec((tn, d), lambda i, j: (j, 0))] + [anyspec] * ng,
        out_specs=[pl.BlockSpec((tm, tn), lambda i, j: (i, j)), pl.BlockSpec((tm, d), lambda i, j: (i, 0))]
        + [anyspec] * ng,
        out_shape=[SDS((s, npad), f32), SDS((s, d), bf16)] + [SDS((N_DEV,) + a.shape, a.dtype) for a in gather],
        scratch_shapes=_gather_scratch(ng) if ng else [],
        compiler_params=pltpu.CompilerParams(dimension_semantics=("arbitrary", "arbitrary")),
    )(x, nw, wt, *gather)
    return outs[0], outs[1], outs[2:]


def _attn_consts():
    head0 = _iota((BLK, LANES), 1) < HEAD_DIM
    tri2 = (_iota((BLK, 2 * LANES), 1) % LANES) <= _iota((BLK, 2 * LANES), 0)
    ones2 = ((_iota((LANES, 2 * LANES), 0) < HEAD_DIM) == (_iota((LANES, 2 * LANES), 1) < LANES)).astype(bf16)
    rmat = ((_iota((2 * LANES, LANES), 0) < LANES) == (_iota((2 * LANES, LANES), 1) < HEAD_DIM)).astype(bf16)
    bones = ((_iota((LANES, LANES), 0) < HEAD_DIM) == (_iota((LANES, LANES), 1) < HEAD_DIM)).astype(bf16)
    return head0, tri2, ones2, rmat, bones


def _stack_heads(x16, head0):
    zero = jnp.zeros_like(x16)
    return jnp.concatenate([jnp.where(head0, x16, zero), jnp.where(head0, zero, x16)], axis=0)


def _bf16_terms(x, terms):
    out = []
    for _ in range(terms):
        t = x.astype(bf16)
        out.append(t)
        x = x - t.astype(f32)
    return out


def _dot_01(x, w16, terms):
    return _nn(jnp.concatenate(_bf16_terms(x, terms), axis=1), jnp.concatenate([w16] * terms, axis=0))


def _split_dot_sum(x, w16):
    hi, lo = _bf16_terms(x, 2)
    return _nn(hi, w16) + _nn(lo, w16)


def _dot_01_left(w16, x, terms):
    return _nn(jnp.concatenate([w16] * terms, axis=1), jnp.concatenate(_bf16_terms(x, terms), axis=0))


def _attn_fwd(proj):
    s = proj.shape[0]
    n_it = s // BLK

    def body(q_ref, k_ref, v_ref, g_ref, o_ref, l_ref, mix_ref, op0, op1, op2, lp0, lp1, lp2,
             s_a, s_b, sd_a, sd_b, p_a, p_b, m_a, m_b, pd_a, pd_b, k_a, k_b, v_a, v_b):
        op_refs, lp_refs = (op0, op1, op2), (lp0, lp1, lp2)
        head0, tri2, ones2, rmat, _ = _attn_consts()
        score_bufs, prob_bufs = ((s_a, sd_a), (s_b, sd_b)), ((p_a, m_a, pd_a), (p_b, m_b, pd_b))
        k_bufs, v_bufs = (k_a, k_b), (v_a, v_b)
        for buf in k_bufs + v_bufs:
            buf[...] = jnp.zeros_like(buf)

        def block_rows(i, d, nb):
            r, blk = i // nb, i % nb
            return pl.ds(blk * (BLK * d) + r, BLK, stride=d), blk > 0

        def unstack(st16):
            return st16[:BLK] + st16[BLK:]

        def scores(i, par, d, nb):
            rows, has_prev = block_rows(i, d, nb)
            s_buf, sd_buf = score_bufs[par]
            qs = q_ref[rows, :] * 0.125
            qs16 = qs.astype(bf16)
            kst_c = _stack_heads(k_ref[rows, :].astype(bf16), head0)
            kst_p = k_bufs[1 - par][...]
            k_bufs[par][...] = kst_c
            sc = _nt(qs16, kst_c)
            sp = _nt(qs16, kst_p)
            s_buf[...] = jnp.where(tri2, sc, jnp.where(has_prev, sp, -jnp.inf))
            sd = _nn((qs * unstack(kst_p).astype(f32)).astype(bf16), ones2)
            sd_buf[...] = jnp.where(has_prev, sd, -jnp.inf)

        def softmax(bufs_in, bufs_out):
            s_buf, sd_buf = bufs_in
            p_buf, m_buf, pd_buf = bufs_out
            sc, sd2 = s_buf[...], sd_buf[...]
            m0 = jnp.max(sc[:, :LANES], axis=1, keepdims=True)
            m1 = jnp.max(sc[:, LANES:], axis=1, keepdims=True)
            m2 = jnp.concatenate([jnp.broadcast_to(m0, (BLK, LANES)), jnp.broadcast_to(m1, (BLK, LANES))], axis=1)
            m2 = jnp.maximum(m2, sd2)
            p_buf[...] = jnp.exp(sc - m2).astype(bf16)
            m_pair = jnp.where(head0, m2[:, :LANES], m2[:, LANES:])
            m_buf[...] = m_pair
            pd_buf[...] = jnp.exp(jnp.where(head0, sd2[:, :LANES], sd2[:, LANES:]) - m_pair)

        def output(i, par, d, nb, p):
            rows, _ = block_rows(i, d, nb)
            p_buf, m_buf, pd_buf = prob_bufs[par]
            vst_c = _stack_heads(v_ref[rows, :].astype(bf16), head0)
            vst_p = v_bufs[1 - par][...]
            v_bufs[par][...] = vst_c
            pt16, pd = p_buf[...], pd_buf[...]
            zero = jnp.zeros_like(pt16)
            o = (_nn(jnp.where(tri2, pt16, zero), vst_c) + _nn(jnp.where(tri2, zero, pt16), vst_p)
                 + pd * unstack(vst_p).astype(f32))
            l = _nn(pt16, rmat) + pd
            op_refs[p][rows, :] = o / l
            lp_refs[p][rows, :] = m_buf[...] + jnp.log(l)

        for p, d in enumerate(DILATIONS):
            nb = s // (BLK * d)
            scores(0, 0, d, nb)
            scores(1, 1, d, nb)
            softmax(score_bufs[0], prob_bufs[0])

            def steps(j, carry, d=d, nb=nb, p=p):
                for par in range(2):
                    t = 2 * j + 2 + par
                    scores(t, par, d, nb)
                    output(t - 2, par, d, nb, p)
                    softmax(score_bufs[1 - par], prob_bufs[1 - par])
                return carry

            lax.fori_loop(0, (n_it - 2) // 2, steps, 0)
            output(n_it - 2, 0, d, nb, p)
            softmax(score_bufs[1], prob_bufs[1])
            output(n_it - 1, 1, d, nb, p)

        def merge(i, carry):
            rows = pl.ds(pl.multiple_of(i * 256, 256), 256)
            l0, l1, l2 = lp0[rows, :], lp1[rows, :], lp2[rows, :]
            m = jnp.maximum(jnp.maximum(l0, l1), l2)
            e0, e1, e2 = jnp.exp(l0 - m), jnp.exp(l1 - m), jnp.exp(l2 - m)
            z = e0 + e1 + e2
            o = (e0 * op0[rows, :] + e1 * op1[rows, :] + e2 * op2[rows, :]) / z
            o_ref[rows, :] = o
            l_ref[rows, :] = m + jnp.log(z)
            g = g_ref[rows, :]
            mix_ref[rows, :] = (o * (g * _sigmoid(g))).astype(bf16)
            return carry

        lax.fori_loop(0, s // 256, merge, 0)

    col = lambda base: pl.BlockSpec((s, LANES), lambda h: (0, base + h))
    return pl.pallas_call(
        body, name="attn_fwd", grid=(N_PAIRS,),
        in_specs=[col(0), col(8), col(16), col(24)],
        out_specs=[col(0), col(0), col(0)],
        out_shape=[SDS((s, D_ATTN), f32), SDS((s, D_ATTN), f32), SDS((s, D_ATTN), bf16)],
        scratch_shapes=[pltpu.VMEM((s, LANES), f32)] * 6 + [pltpu.VMEM((BLK, 2 * LANES), f32)] * 4
        + [pltpu.VMEM((BLK, 2 * LANES), bf16)] * 2 + [pltpu.VMEM((BLK, LANES), f32)] * 4
        + [pltpu.VMEM((2 * BLK, LANES), bf16)] * 4,
        compiler_params=pltpu.CompilerParams(dimension_semantics=("parallel",)),
    )(proj, proj, proj, proj)


def _expand_mat():
    colv = _iota((LANES, 2 * D_SSM), 1)
    head = 2 * ((colv % D_SSM) // LANES) + colv // D_SSM
    return (_iota((LANES, 2 * D_SSM), 0) == head).astype(bf16)


def _fold_mat():
    return (_iota((D_SSM, LANES), 0) // HEAD_DIM == _iota((D_SSM, LANES), 1)).astype(bf16)


def _conv(xs_ref, bc_ref, xs_tail, bc_tail, cw_ref, cb_ref, xpad, first):
    keep = jnp.where(first, 0.0, 1.0)
    xpad[0:8, 0:D_SSM] = xs_tail[...] * keep
    xpad[0:8, D_SSM:D_CONV] = bc_tail[...] * keep
    xpad[8:8 + CHUNK, 0:D_SSM] = xs_ref[...]
    xpad[8:8 + CHUNK, D_SSM:D_CONV] = bc_ref[...]
    xp = xpad[...]
    cv = cb_ref[...] + cw_ref[3:4, :] * xp[8:8 + CHUNK]
    for j in range(3):
        cv = cv + cw_ref[j:j + 1, :] * pltpu.roll(xp, 3 - j, 0)[8:8 + CHUNK]
    return cv


def _decay_terms(dt_ref, dtb_ref, alog16_ref, emat_ref):
    pre = dt_ref[...] + dtb_ref[...]
    dt16 = _softplus(pre)
    a16 = -jnp.exp(alog16_ref[...])
    sub, lane = _iota((CHUNK, CHUNK), 0), _iota((CHUNK, CHUNK), 1)
    tri = (sub >= lane).astype(f32)
    al16 = _nn_hi(tri, dt16 * a16)
    al_t = al16.T
    emat = emat_ref[...]
    dt_x = _dot_01(dt16, emat, 3)
    al_x = _dot_01(al16, emat, 3)
    lane_w = _iota((CHUNK, D_SSM), 1)
    even = (lane_w % LANES) < HEAD_DIM
    dt_f = jnp.where(even, dt_x[:, :D_SSM], dt_x[:, D_SSM:])
    al_f = jnp.where(even, al_x[:, :D_SSM], al_x[:, D_SSM:])
    return pre, dt_f, al_f, al_x, al_t


def _decay_mat(al_x, al_t, pair, h):
    sub, lane = _iota((CHUNK, CHUNK), 0), _iota((CHUNK, CHUNK), 1)
    col = al_x[:, h * D_SSM + pair * LANES: h * D_SSM + (pair + 1) * LANES]
    row = al_t[2 * pair + h: 2 * pair + h + 1, :]
    return jnp.exp(jnp.where(sub >= lane, col - row, -jnp.inf))


def _ssd_in_specs(order):
    blk = lambda w, cb: pl.BlockSpec((CHUNK, w), lambda i: (order(i), cb))
    tail = lambda w, cb: pl.BlockSpec((8, w), lambda i: (jnp.maximum(16 * order(i) - 1, 0), cb))
    return [blk(D_SSM, COL_XS // D_SSM), blk(512, COL_BC // 512), tail(D_SSM, COL_XS // D_SSM),
            tail(512, COL_BC // 512), blk(LANES, COL_DT // LANES), blk(D_SSM, COL_Z // D_SSM)]


def _full(shape):
    return pl.BlockSpec(shape, lambda i: (0,) * len(shape))


def _ssd_fwd(proj, conv_w, conv_b, dtb16, alog16, alog_f, d_f, nw):
    s = proj.shape[0]
    nc = s // CHUNK

    def body(xs_ref, bc_ref, xs_tail, bc_tail, dt_ref, z_ref, cw_ref, cb_ref, dtb_ref, alog16_ref, alogf_ref,
             df_ref, nw_ref, mix_ref, y_ref, st_ref, cv_ref, h_scr, xpad, y_scr, emat_ref):
        c = pl.program_id(0)

        @pl.when(c == 0)
        def _():
            h_scr[...] = jnp.zeros_like(h_scr)
            emat_ref[...] = _expand_mat()

        cv = _conv(xs_ref, bc_ref, xs_tail, bc_tail, cw_ref, cb_ref, xpad, c == 0)
        cv_ref[...] = cv
        xbc = cv * _sigmoid(cv)
        _, dt_f, al_f, al_x, al_t = _decay_terms(dt_ref, dtb_ref, alog16_ref, emat_ref)
        head0 = _iota((CHUNK, LANES), 1) < HEAD_DIM
        st_ref[...] = h_scr[...]
        for g in range(N_GROUPS):
            bm = xbc[:, D_SSM + g * D_STATE: D_SSM + (g + 1) * D_STATE].astype(bf16)
            cm = xbc[:, D_SSM + (N_GROUPS + g) * D_STATE: D_SSM + (N_GROUPS + g + 1) * D_STATE].astype(bf16)
            gmat = _nt(cm, bm)
            for pair in range(4 * g, 4 * g + 4):
                sl = slice(pair * LANES, (pair + 1) * LANES)
                xp, dtp, alp = xbc[:, sl], dt_f[:, sl], al_f[:, sl]
                xdt = xp * dtp
                xdt16 = xdt.astype(bf16)
                al_last = alp[CHUNK - 1:CHUNK, :]
                hp = h_scr[:, sl]
                y_off = jnp.exp(alp) * _nn(cm, hp.astype(bf16))
                yd = [_nn((gmat * _decay_mat(al_x, al_t, pair, h)).astype(bf16), xdt16) for h in range(2)]
                y_scr[:, sl] = jnp.where(head0, yd[0], yd[1]) + y_off + df_ref[:, sl] * xp
                st = _tn(bm, (jnp.exp(al_last - alp) * xdt).astype(bf16))
                h_scr[:, sl] = jnp.exp(al_last) * hp + st
        y = y_scr[...]
        y_ref[...] = y
        z = z_ref[...]
        yz = y * (z * _sigmoid(z))
        gw = D_SSM // N_GROUPS
        for g in range(N_GROUPS):
            part = yz[:, g * gw:(g + 1) * gw]
            r = lax.rsqrt(jnp.mean(part * part, axis=-1, keepdims=True) + EPS)
            mix_ref[:, g * gw:(g + 1) * gw] = (part * r * nw_ref[:, g * gw:(g + 1) * gw]).astype(bf16)

    order = lambda i: i
    row = lambda w: pl.BlockSpec((CHUNK, w), lambda i: (i, 0))
    return pl.pallas_call(
        body, name="ssd_fwd", grid=(nc,),
        in_specs=_ssd_in_specs(order) + [_full((4, D_CONV)), _full((1, D_CONV)), _full((1, LANES)), _full((1, LANES)),
                                         _full((1, D_SSM)), _full((1, D_SSM)), _full((1, D_SSM))],
        out_specs=[row(D_SSM), row(D_SSM), pl.BlockSpec((None, D_STATE, D_SSM), lambda i: (i, 0, 0)), row(D_CONV)],
        out_shape=[SDS((s, D_SSM), bf16), SDS((s, D_SSM), f32), SDS((nc, D_STATE, D_SSM), f32),
                   SDS((s, D_CONV), f32)],
        scratch_shapes=[pltpu.VMEM((D_STATE, D_SSM), f32), pltpu.VMEM((8 + CHUNK, D_CONV), f32),
                        pltpu.VMEM((CHUNK, D_SSM), f32), pltpu.VMEM((LANES, 2 * D_SSM), bf16)],
        compiler_params=pltpu.CompilerParams(dimension_semantics=("arbitrary",)),
    )(proj, proj, proj, proj, proj, proj, conv_w, conv_b, dtb16, alog16, alog_f, d_f, nw)


def _outproj_loss(mix_a, mix_s, wo, x, tgt, npw):
    s, d = x.shape
    tm = 512

    def body(ma_ref, ms_ref, wo_ref, x_ref, t_ref, npw_ref, dmix_ref, dout_ref, dres_ref, acc_ref):
        @pl.when(pl.program_id(0) == 0)
        def _():
            acc_ref[...] = jnp.zeros_like(acc_ref)

        out = _nn(ma_ref[...], wo_ref[0:D_ATTN, :]) + _nn(ms_ref[...], wo_ref[D_ATTN:, :])
        r = lax.rsqrt(jnp.mean(out * out, axis=-1, keepdims=True) + EPS)
        on = out * r
        diff = x_ref[...] + on * npw_ref[...] - t_ref[...]
        dres = diff * (1.0 / d)
        dres_ref[...] = dres
        acc_ref[0:1, :] += jnp.sum(diff * diff, axis=0, keepdims=True)
        acc_ref[1:2, :] += jnp.sum(dres * on, axis=0, keepdims=True)
        dn = dres * npw_ref[...]
        dout = (r * (dn - on * jnp.mean(dn * on, axis=-1, keepdims=True))).astype(bf16)
        dout_ref[...] = dout
        dmix_ref[...] = _nt(dout, wo_ref[...])

    row = lambda w: pl.BlockSpec((tm, w), lambda i: (i, 0))
    return pl.pallas_call(
        body, name="outproj_loss", grid=(s // tm,),
        in_specs=[row(D_ATTN), row(D_SSM), _full((D_ATTN + D_SSM, d)), row(d), row(d), _full((1, d))],
        out_specs=[row(D_ATTN + D_SSM), row(d), row(d), _full((8, d))],
        out_shape=[SDS((s, D_ATTN + D_SSM), f32), SDS((s, d), bf16), SDS((s, d), f32), SDS((8, d), f32)],
        compiler_params=pltpu.CompilerParams(dimension_semantics=("arbitrary",)),
    )(mix_a, mix_s, wo, x, tgt, npw)


def _attn_bwd(proj, o, lb, dmix, swap=None):
    s = proj.shape[0]
    n_it = s // BLK

    nsw = 0 if swap is None else 1

    def body(*refs):
        q_ref, k_ref, v_ref, g_ref, o_ref, l_ref, dm_ref = refs[:7]
        swap_in = refs[7:7 + nsw]
        dq_ref, dk_ref, dv_ref, dg_ref = refs[7 + nsw:11 + nsw]
        swap_out = refs[11 + nsw:11 + 2 * nsw]
        dq_acc, dk_acc, dv_acc, do_scr, dl_scr = refs[11 + 2 * nsw:16 + 2 * nsw]
        bufs = refs[16 + 2 * nsw:44 + 2 * nsw]
        swap_sems = refs[44 + 2 * nsw:]
        head0, tri2, _, _, bones = _attn_consts()

        if nsw:
            x, y, c = _my_pos()
            swap_copy = pltpu.make_async_remote_copy(
                src_ref=swap_in[0], dst_ref=swap_out[0], send_sem=swap_sems[0], recv_sem=swap_sems[1],
                device_id=(x, y, 1 - c), device_id_type=MESH)

            @pl.when(pl.program_id(0) == 0)
            def _():
                swap_copy.start()

        def pro(i, carry):
            rows = pl.ds(pl.multiple_of(i * 256, 256), 256)
            g = g_ref[rows, :]
            sg = _sigmoid(g)
            dmx = dm_ref[rows, :]
            ov = o_ref[rows, :]
            dg_ref[rows, :] = (dmx * ov * (sg * (1.0 + g * (1.0 - sg)))).astype(bf16)
            do = dmx * (g * sg)
            do_scr[rows, :] = do
            dl_scr[rows, :] = _split_dot_sum(do * ov, bones)
            z = jnp.zeros((256, LANES), f32)
            dq_acc[rows, :] = z
            dk_acc[rows, :] = z
            dv_acc[rows, :] = z
            return carry

        lax.fori_loop(0, s // 256, pro, 0)

        def per_head(t):
            return jnp.concatenate([t[:, :LANES], t[:, LANES:]], axis=0)

        def both_heads(t):
            tr = pltpu.roll(t, HEAD_DIM, 1)
            return jnp.concatenate([jnp.where(head0, t, tr), jnp.where(head0, tr, t)], axis=1)

        mm_bufs = ((bufs[0], bufs[1], bufs[2], bufs[3]), (bufs[4], bufs[5], bufs[6], bufs[7]))
        ds_bufs = ((bufs[8], bufs[9], bufs[10], bufs[11]), (bufs[12], bufs[13], bufs[14], bufs[15]))
        op_bufs = ((bufs[16], bufs[17], bufs[18], bufs[19]), (bufs[20], bufs[21], bufs[22], bufs[23]))
        vc_bufs, carry_k, carry_v = (bufs[24], bufs[25]), bufs[26], bufs[27]
        for buf in (op_bufs[0][0], op_bufs[1][0]) + vc_bufs:
            buf[...] = jnp.zeros_like(buf)

        def block_rows(i, d, nb):
            r, blk = i // nb, i % nb
            return pl.ds(blk * (BLK * d) + r, BLK, stride=d), blk > 0

        def unstack(st16):
            return st16[:BLK] + st16[BLK:]

        def products(i, par, d, nb):
            rows, has_prev = block_rows(i, d, nb)
            s_buf, dp_buf, sd_buf, dpd_buf = mm_bufs[par]
            kc_buf, kp_buf, q_buf, do_buf = op_bufs[par]
            q = q_ref[rows, :]
            qs = q * 0.125
            do = do_scr[rows, :]
            qs16, do16 = qs.astype(bf16), do.astype(bf16)
            kst_c = _stack_heads(k_ref[rows, :].astype(bf16), head0)
            vst_c = _stack_heads(v_ref[rows, :].astype(bf16), head0)
            kst_p, vst_p = op_bufs[1 - par][0][...], vc_bufs[1 - par][...]
            kc_buf[...] = kst_c
            kp_buf[...] = kst_p
            vc_bufs[par][...] = vst_c
            q_buf[...] = q.astype(bf16)
            do_buf[...] = do16
            s_buf[...] = jnp.where(tri2, _nt(qs16, kst_c), jnp.where(has_prev, _nt(qs16, kst_p), -jnp.inf))
            dp_buf[...] = jnp.where(tri2, _nt(do16, vst_c), jnp.where(has_prev, _nt(do16, vst_p), 0.0))
            sd_buf[...] = _nn((qs * unstack(kst_p).astype(f32)).astype(bf16), bones)
            dpd_buf[...] = jnp.where(has_prev, _nn((do * unstack(vst_p).astype(f32)).astype(bf16), bones), 0.0)

        def softmax_grad(i, par, d, nb):
            rows, has_prev = block_rows(i, d, nb)
            s_buf, dp_buf, sd_buf, dpd_buf = mm_bufs[par]
            p_buf, ds_buf, pd_buf, dsd_buf = ds_bufs[par]
            lse = l_ref[rows, :]
            dl = dl_scr[rows, :]
            pt = jnp.exp(s_buf[...] - both_heads(lse))
            ds_buf[...] = (pt * (dp_buf[...] - both_heads(dl)) * 0.125).astype(bf16)
            p_buf[...] = pt.astype(bf16)
            pd = jnp.where(has_prev, jnp.exp(sd_buf[...] - lse), 0.0)
            pd_buf[...] = pd
            dsd_buf[...] = pd * (dpd_buf[...] - dl) * 0.125

        def accumulate(i, par, d, nb):
            rows, _ = block_rows(i, d, nb)
            before, _ = block_rows(jnp.maximum(i - 1, 0), d, nb)
            p_buf, ds_buf, pd_buf, dsd_buf = ds_bufs[par]
            kc_buf, kp_buf, q_buf, do_buf = op_bufs[par]
            pt16, ds16, pd, dsd = p_buf[...], ds_buf[...], pd_buf[...], dsd_buf[...]
            zero = jnp.zeros_like(pt16)
            dsc, dsp = jnp.where(tri2, ds16, zero), jnp.where(tri2, zero, ds16)
            pc, pp = jnp.where(tri2, pt16, zero), jnp.where(tri2, zero, pt16)
            kst_c, kst_p, q16, do16 = kc_buf[...], kp_buf[...], q_buf[...], do_buf[...]
            qst, dost = _stack_heads(q16, head0), _stack_heads(do16, head0)
            dq_acc[rows, :] += _nn(dsc, kst_c) + _nn(dsp, kst_p) + dsd * unstack(kst_p).astype(f32)
            dk2 = _tn(jnp.concatenate([per_head(dsc), per_head(dsp)], axis=1), qst)
            dv2 = _tn(jnp.concatenate([per_head(pc), per_head(pp)], axis=1), dost)
            dk_acc[before, :] += carry_k[...] + dk2[BLK:] + dsd * q16.astype(f32)
            dv_acc[before, :] += carry_v[...] + dv2[BLK:] + pd * do16.astype(f32)
            carry_k[...] = dk2[:BLK]
            carry_v[...] = dv2[:BLK]

        for d in DILATIONS:
            nb = s // (BLK * d)
            carry_k[...] = jnp.zeros_like(carry_k)
            carry_v[...] = jnp.zeros_like(carry_v)
            products(0, 0, d, nb)
            products(1, 1, d, nb)
            softmax_grad(0, 0, d, nb)

            def steps(j, carry, d=d, nb=nb):
                for par in range(2):
                    t = 2 * j + 2 + par
                    accumulate(t - 2, par, d, nb)
                    products(t, par, d, nb)
                    softmax_grad(t - 1, 1 - par, d, nb)
                return carry

            lax.fori_loop(0, (n_it - 2) // 2, steps, 0)
            accumulate(n_it - 2, 0, d, nb)
            softmax_grad(n_it - 1, 1, d, nb)
            accumulate(n_it - 1, 1, d, nb)
            last, _ = block_rows(n_it - 1, d, nb)
            dk_acc[last, :] += carry_k[...]
            dv_acc[last, :] += carry_v[...]

        def epi(i, carry):
            rows = pl.ds(pl.multiple_of(i * 256, 256), 256)
            dq_ref[rows, :] = dq_acc[rows, :].astype(bf16)
            dk_ref[rows, :] = dk_acc[rows, :].astype(bf16)
            dv_ref[rows, :] = dv_acc[rows, :].astype(bf16)
            return carry

        lax.fori_loop(0, s // 256, epi, 0)

        if nsw:
            @pl.when(pl.program_id(0) == N_PAIRS - 1)
            def _():
                swap_copy.wait_send()
                swap_copy.wait_recv()

    col = lambda base: pl.BlockSpec((s, LANES), lambda h: (0, base + h))
    anyspec = pl.BlockSpec(memory_space=pl.ANY)
    swaps = [] if swap is None else [swap]
    outs = pl.pallas_call(
        body, name="attn_bwd", grid=(N_PAIRS,),
        in_specs=[col(0), col(8), col(16), col(24), col(0), col(0), col(0)] + [anyspec] * nsw,
        out_specs=[col(0)] * 4 + [anyspec] * nsw,
        out_shape=[SDS((s, D_ATTN), bf16)] * 4 + [SDS(a.shape, a.dtype) for a in swaps],
        scratch_shapes=[pltpu.VMEM((s, LANES), f32)] * 5
        + [pltpu.VMEM((BLK, 2 * LANES), f32)] * 2 + [pltpu.VMEM((BLK, LANES), f32)] * 2
        + [pltpu.VMEM((BLK, 2 * LANES), f32)] * 2 + [pltpu.VMEM((BLK, LANES), f32)] * 2
        + [pltpu.VMEM((BLK, 2 * LANES), bf16)] * 2 + [pltpu.VMEM((BLK, LANES), f32)] * 2
        + [pltpu.VMEM((BLK, 2 * LANES), bf16)] * 2 + [pltpu.VMEM((BLK, LANES), f32)] * 2
        + [pltpu.VMEM((2 * BLK, LANES), bf16)] * 2 + [pltpu.VMEM((BLK, LANES), bf16)] * 2
        + [pltpu.VMEM((2 * BLK, LANES), bf16)] * 2 + [pltpu.VMEM((BLK, LANES), bf16)] * 2
        + [pltpu.VMEM((2 * BLK, LANES), bf16)] * 2 + [pltpu.VMEM((BLK, LANES), f32)] * 2
        + [pltpu.SemaphoreType.DMA(())] * (2 * nsw),
        compiler_params=pltpu.CompilerParams(dimension_semantics=("arbitrary",)),
    )(proj, proj, proj, proj, o, lb, dmix, *swaps)
    return outs


def _ssd_bwd(proj, y, states, cv, dmix, conv_w, conv_b, dtb16, alog16, alog_f, d_f, nw, chip_sums=()):
    s = proj.shape[0]
    nc = s // CHUNK
    gw = D_SSM // N_GROUPS
    nx = len(chip_sums)

    def body(*refs):
        (xs_ref, bc_ref, _, _, dt_ref, z_ref, y_ref, st_ref, dm_ref, cw_ref, cb_ref, dtb_ref,
         alog16_ref, alogf_ref, df_ref, nw_ref, cv_ref) = refs[:17]
        cs_in = refs[17:17 + nx]
        out_ref, gconv_ref, gvec_ref, gdt_ref = refs[17 + nx:21 + nx]
        cs_out = refs[21 + nx:21 + 2 * nx]
        (dh_scr, head_scr, dcpad, da_scr, dxdt_scr, dbc_scr, emat_ref, fold_ref) = refs[21 + 2 * nx:29 + 2 * nx]
        cs_sems = refs[29 + 2 * nx:]
        i = pl.program_id(0)
        c = nc - 1 - i

        if nx:
            @pl.when(i == 0)
            def _():
                mine, sends, _ = _chip_exchange_copies(cs_in, cs_out, *cs_sems)
                for cp in mine + sends:
                    cp.start()

            @pl.when(i == nc - 1)
            def _():
                mine, sends, recvs = _chip_exchange_copies(cs_in, cs_out, *cs_sems)
                for cp in recvs:
                    cp.wait_recv()
                for cp in sends:
                    cp.wait_send()
                for cp in mine:
                    cp.wait()

        @pl.when(i == 0)
        def _():
            emat_ref[...] = _expand_mat()
            fold_ref[...] = _fold_mat()
            dh_scr[...] = jnp.zeros_like(dh_scr)
            head_scr[...] = jnp.zeros_like(head_scr)
            gconv_ref[...] = jnp.zeros_like(gconv_ref)
            gvec_ref[...] = jnp.zeros_like(gvec_ref)
            gdt_ref[...] = jnp.zeros_like(gdt_ref)

        cv = cv_ref[...]
        sig = _sigmoid(cv)
        xbc = cv * sig
        pre, dt_f, al_f, al_x, al_t = _decay_terms(dt_ref, dtb_ref, alog16_ref, emat_ref)
        head0 = _iota((CHUNK, LANES), 1) < HEAD_DIM
        sub = _iota((CHUNK, LANES), 0)
        last_row = sub == CHUNK - 1

        yv, z, dmx = y_ref[...], z_ref[...], dm_ref[...]
        sz = _sigmoid(z)
        silu = z * sz
        yz = yv * silu
        dyz_parts = []
        for g in range(N_GROUPS):
            gs = slice(g * gw, (g + 1) * gw)
            part = yz[:, gs]
            r = lax.rsqrt(jnp.mean(part * part, axis=-1, keepdims=True) + EPS)
            nh = part * r
            gvec_ref[0:1, gs] += jnp.sum(dmx[:, gs] * nh, axis=0, keepdims=True)
            dn = dmx[:, gs] * nw_ref[:, gs]
            dyz_parts.append(r * (dn - nh * jnp.mean(dn * nh, axis=-1, keepdims=True)))
        dyz = jnp.concatenate(dyz_parts, axis=1)
        dy = dyz * silu
        out_ref[:, 0:D_SSM] = (dyz * yv * (sz * (1.0 + z * (1.0 - sz)))).astype(bf16)

        x_all = xbc[:, 0:D_SSM]
        gvec_ref[2:3, :] += jnp.sum(dy * x_all, axis=0, keepdims=True)

        for g in range(N_GROUPS):
            bm = xbc[:, D_SSM + g * D_STATE: D_SSM + (g + 1) * D_STATE].astype(bf16)
            cm = xbc[:, D_SSM + (N_GROUPS + g) * D_STATE: D_SSM + (N_GROUPS + g + 1) * D_STATE].astype(bf16)
            gmat = _nt(cm, bm)
            dgm = jnp.zeros((CHUNK, CHUNK), f32)
            db = jnp.zeros((CHUNK, D_STATE), f32)
            dc = jnp.zeros((CHUNK, D_STATE), f32)
            for pair in range(4 * g, 4 * g + 4):
                sl = slice(pair * LANES, (pair + 1) * LANES)
                xp, dtp, alp, dyp = x_all[:, sl], dt_f[:, sl], al_f[:, sl], dy[:, sl]
                xdt = xp * dtp
                xdt16 = xdt.astype(bf16)
                al_last = alp[CHUNK - 1:CHUNK, :]
                e_l = jnp.exp(alp)
                wf = jnp.exp(al_last - alp)
                e_last = jnp.exp(al_last)
                hp = st_ref[:, sl]
                hp16 = hp.astype(bf16)
                dhn = dh_scr[:, sl]
                dhn16 = dhn.astype(bf16)
                y_off = e_l * _nn(cm, hp16)
                dch16 = (dyp * e_l).astype(bf16)
                dc = dc + _nt(dch16, hp16)
                dh_out = _tn(cm, dch16)
                dal = dyp * y_off
                xw16 = (wf * xdt).astype(bf16)
                db = db + _nt(xw16, dhn16)
                dxw = _nn(bm, dhn16)
                dxdt = dxw * wf
                dwf = dxw * xdt * wf
                dal = dal - dwf
                dal_last = jnp.sum(dwf, axis=0, keepdims=True) + jnp.sum(dhn * hp, axis=0, keepdims=True) * e_last
                dh_scr[:, sl] = e_last * dhn + dh_out
                for h in range(2):
                    mh = head0 if h == 0 else jnp.logical_not(head0)
                    dyh16 = jnp.where(mh, dyp, 0.0).astype(bf16)
                    lmat = _decay_mat(al_x, al_t, pair, h)
                    mm = gmat * lmat
                    dmm = _nt(dyh16, xdt16)
                    dxdt = dxdt + _tn(mm.astype(bf16), dyh16)
                    n16 = (dmm * mm).astype(bf16)
                    jh = jnp.where(mh, 1.0 / HEAD_DIM, 0.0).astype(bf16)
                    dal = dal + _nn(n16, jh) - _tn(n16, jh)
                    dgm = dgm + dmm * lmat
                da_scr[:, sl] = dal + jnp.where(last_row, dal_last, 0.0)
                dxdt_scr[:, sl] = dxdt
            dgm16 = dgm.astype(bf16)
            dbc_scr[:, g * D_STATE:(g + 1) * D_STATE] = db + _tn(dgm16, cm)
            dbc_scr[:, (N_GROUPS + g) * D_STATE:(N_GROUPS + g + 1) * D_STATE] = dc + _nn(dgm16, bm)

        sub_c, lane_c = _iota((CHUNK, CHUNK), 0), _iota((CHUNK, CHUNK), 1)
        tri_t = (lane_c >= sub_c).astype(bf16)
        dadt = _dot_01_left(tri_t, da_scr[...], 2)
        a_f = -jnp.exp(alogf_ref[...])
        dxdt_all = dxdt_scr[...]
        ddt_f = dxdt_all * x_all + a_f * dadt
        gvec_ref[1:2, :] += jnp.sum(dt_f * dadt, axis=0, keepdims=True) * a_f
        dx = df_ref[...] * dy + dxdt_all * dt_f
        ddt_raw = _dot_01(ddt_f, fold_ref[...], 2) * _sigmoid(pre)
        gdt_ref[0:1, :] += jnp.sum(ddt_raw, axis=0, keepdims=True)
        out_ref[:, D_SSM + D_CONV:D_SSM + D_CONV + LANES] = ddt_raw.astype(bf16)
        out_ref[:, D_SSM + D_CONV + LANES:] = jnp.zeros((CHUNK, 3 * LANES), bf16)

        dsil = sig * (1.0 + cv * (1.0 - sig))
        dcv_x = dx * dsil[:, 0:D_SSM]
        dcv_bc = dbc_scr[...] * dsil[:, D_SSM:]
        dcpad[0:CHUNK, 0:D_SSM] = dcv_x
        dcpad[0:CHUNK, D_SSM:] = dcv_bc
        dcpad[CHUNK:, :] = head_scr[...]
        dcp = dcpad[...]
        dcv = dcp[0:CHUNK]
        gconv_ref[4:5, :] += jnp.sum(dcv, axis=0, keepdims=True)
        x_raw = jnp.concatenate([xs_ref[...], bc_ref[...]], axis=1)
        draw = cw_ref[3:4, :] * dcv
        gconv_ref[3:4, :] += jnp.sum(dcv * x_raw, axis=0, keepdims=True)
        for j in range(3):
            ahead = pltpu.roll(dcp, CHUNK + 8 - (3 - j), 0)[0:CHUNK]
            draw = draw + cw_ref[j:j + 1, :] * ahead
            gconv_ref[j:j + 1, :] += jnp.sum(ahead * x_raw, axis=0, keepdims=True)
        head_scr[...] = dcv[0:8]
        out_ref[:, D_SSM:D_SSM + D_CONV] = draw.astype(bf16)

    order = lambda i: nc - 1 - i
    row = lambda w, cb=0: pl.BlockSpec((CHUNK, w), lambda i: (nc - 1 - i, cb))
    anyspec = pl.BlockSpec(memory_space=pl.ANY)
    outs = pl.pallas_call(
        body, name="ssd_bwd", grid=(nc,),
        in_specs=_ssd_in_specs(order) + [row(D_SSM), pl.BlockSpec((None, D_STATE, D_SSM), lambda i: (nc - 1 - i, 0, 0)),
                                         row(D_SSM, 1), _full((4, D_CONV)), _full((1, D_CONV)), _full((1, LANES)),
                                         _full((1, LANES)), _full((1, D_SSM)), _full((1, D_SSM)), _full((1, D_SSM)),
                                         row(D_CONV)]
        + [anyspec] * nx,
        out_specs=[row(3072), _full((8, D_CONV)), _full((8, D_SSM)), _full((8, LANES))] + [anyspec] * nx,
        out_shape=[SDS((s, 3072), bf16), SDS((8, D_CONV), f32), SDS((8, D_SSM), f32), SDS((8, LANES), f32)]
        + [SDS(a.shape, a.dtype) for a in chip_sums],
        scratch_shapes=[pltpu.VMEM((D_STATE, D_SSM), f32), pltpu.VMEM((8, D_CONV), f32),
                        pltpu.VMEM((8 + CHUNK, D_CONV), f32),
                        pltpu.VMEM((CHUNK, D_SSM), f32), pltpu.VMEM((CHUNK, D_SSM), f32),
                        pltpu.VMEM((CHUNK, 2 * N_GROUPS * D_STATE), f32),
                        pltpu.VMEM((LANES, 2 * D_SSM), bf16), pltpu.VMEM((D_SSM, LANES), bf16)]
        + (_chip_exchange_scratch(nx) if nx else []),
        compiler_params=pltpu.CompilerParams(dimension_semantics=("arbitrary",)),
    )(proj, proj, proj, proj, proj, proj, y, states, dmix, conv_w, conv_b, dtb16, alog16, alog_f, d_f, nw, cv,
      *chip_sums)
    return outs[0], outs[1], outs[2], outs[3], outs[4:]


def _col_blocks(parts, tile):
    counts = [p.shape[1] // tile for p in parts]
    offs = [sum(counts[:t]) for t in range(len(parts))]
    return offs, counts, sum(counts)


def _bcast_copies(src_ref, out_ref, send_sems, recv_sems, local_sem):
    x, y, c = _my_pos()
    me = 4 * x + 2 * y + c
    mine = pltpu.make_async_copy(src_ref, out_ref.at[me], local_sem)
    sends, recvs = [], []
    for k in range(1, N_DEV):
        to, frm = (me + k) % N_DEV, (me + N_DEV - k) % N_DEV
        sems = dict(send_sem=send_sems.at[k - 1], recv_sem=recv_sems.at[k - 1], device_id_type=MESH)
        sends.append(pltpu.make_async_remote_copy(
            src_ref=src_ref, dst_ref=out_ref.at[me], device_id=(to // 4, (to // 2) % 2, to % 2), **sems))
        recvs.append(pltpu.make_async_remote_copy(
            src_ref=src_ref, dst_ref=out_ref.at[frm], device_id=(x, y, c), **sems))
    return mine, sends, recvs


def _bcast_scratch():
    return [pltpu.SemaphoreType.DMA((N_DEV - 1,)), pltpu.SemaphoreType.DMA((N_DEV - 1,)), pltpu.SemaphoreType.DMA(())]


def _inproj_bwd(dparts, wt, x, nw, dres, chip_sums=(), pack=None):
    s, d = x.shape
    tm, tk = 1024, 1024
    offs, counts, nk = _col_blocks(dparts, tk)
    npart, nx = len(dparts), len(chip_sums)
    npk = 0 if pack is None else 1
    ni = s // tm

    def body(*refs):
        dp_refs = refs[:npart]
        w_ref, x_ref, nw_ref, dres_ref = refs[npart:npart + 4]
        pos = npart + 4
        cs_in, pos = refs[pos:pos + nx], pos + nx
        pack_in, pos = refs[pos:pos + npk], pos + npk
        (gx_ref, gnw_ref), pos = refs[pos:pos + 2], pos + 2
        cs_out, pos = refs[pos:pos + nx], pos + nx
        pack_out, pos = refs[pos:pos + 2 * npk], pos + 2 * npk
        acc, pos = refs[pos], pos + 1
        cs_sems, pos = refs[pos:pos + 3 * min(nx, 1)], pos + 3 * min(nx, 1)
        pk_refs = refs[pos:]
        i, k = pl.program_id(0), pl.program_id(1)

        def exchange():
            return _chip_exchange_copies(cs_in, cs_out, *cs_sems)

        def pack_copies():
            return _bcast_copies(pack_in[0], pack_out[0], *pk_refs[1:4])

        def gnw_copies():
            return _bcast_copies(pk_refs[0], pack_out[1], *pk_refs[4:7])

        @pl.when(jnp.logical_and(i == 0, k == 0))
        def _():
            gnw_ref[...] = jnp.zeros_like(gnw_ref)
            if nx:
                mine, sends, _ = exchange()
                for cp in mine + sends:
                    cp.start()
            if npk:
                mine, sends, _ = pack_copies()
                for cp in [mine] + sends:
                    cp.start()

        @pl.when(k == 0)
        def _():
            acc[...] = _nn(dp_refs[0][...], w_ref[...])

        for t in range(npart):
            @pl.when(jnp.logical_and(k >= max(offs[t], 1), k < offs[t] + counts[t]))
            def _(t=t):
                acc[...] += _nn(dp_refs[t][...], w_ref[...])

        @pl.when(k == nk - 1)
        def _():
            xv = x_ref[...]
            r = lax.rsqrt(jnp.mean(xv * xv, axis=-1, keepdims=True) + EPS)
            xn = xv * r
            du = acc[...]
            gnw_ref[0:1, :] += jnp.sum(du * xn, axis=0, keepdims=True)
            dn = du * nw_ref[...]
            gx_ref[...] = dres_ref[...] + r * (dn - xn * jnp.mean(dn * xn, axis=-1, keepdims=True))

        @pl.when(jnp.logical_and(i == ni - 1, k == nk - 1))
        def _():
            if npk:
                pk_refs[0][...] = gnw_ref[...]
                mine, sends, _ = gnw_copies()
                for cp in [mine] + sends:
                    cp.start()
            if nx:
                mine, sends, recvs = exchange()
                for cp in recvs:
                    cp.wait_recv()
                for cp in sends:
                    cp.wait_send()
                for cp in mine:
                    cp.wait()
            if npk:
                for copies in (pack_copies(), gnw_copies()):
                    mine, sends, recvs = copies
                    for cp in recvs:
                        cp.wait_recv()
                    for cp in sends:
                        cp.wait_send()
                    mine.wait()

    def piece(t):
        return pl.BlockSpec((tm, tk), lambda i, k: (i, jnp.clip(k - offs[t], 0, counts[t] - 1)))

    anyspec = pl.BlockSpec(memory_space=pl.ANY)
    packs = [] if pack is None else [pack]
    pack_shapes = [] if pack is None else [SDS((N_DEV,) + pack.shape, f32), SDS((N_DEV, 8, d), f32)]
    scratch = [pltpu.VMEM((tm, d), f32)] + (_chip_exchange_scratch(nx) if nx else [])
    if npk:
        scratch += [pltpu.VMEM((8, d), f32)] + _bcast_scratch() + _bcast_scratch()
    outs = pl.pallas_call(
        body, name="inproj_bwd", grid=(ni, nk),
        in_specs=[piece(t) for t in range(npart)] + [
            pl.BlockSpec((tk, d), lambda i, k: (k, 0)),
            pl.BlockSpec((tm, d), lambda i, k: (i, 0)), pl.BlockSpec((1, d), lambda i, k: (0, 0)),
            pl.BlockSpec((tm, d), lambda i, k: (i, 0))] + [anyspec] * (nx + npk),
        out_specs=[pl.BlockSpec((tm, d), lambda i, k: (i, 0)), pl.BlockSpec((8, d), lambda i, k: (0, 0))]
        + [anyspec] * (nx + 2 * npk),
        out_shape=[SDS((s, d), f32), SDS((8, d), f32)] + [SDS(a.shape, a.dtype) for a in chip_sums] + pack_shapes,
        scratch_shapes=scratch,
        compiler_params=pltpu.CompilerParams(dimension_semantics=("arbitrary", "arbitrary")),
    )(*dparts, wt, x, nw, dres, *chip_sums, *packs)
    return outs[0], outs[1], outs[2:2 + nx], outs[2 + nx:]


def _matmul_tn(a_parts, b_parts, name):
    tile, tk = 1024, 1024
    s = a_parts[0].shape[0]
    nk = s // tk
    na, nb = len(a_parts), len(b_parts)
    offs_a, counts_a, ni = _col_blocks(a_parts, tile)
    offs_b, counts_b, nj = _col_blocks(b_parts, tile)

    def body(*refs):
        a_refs, b_refs, o_ref = refs[:na], refs[na:na + nb], refs[na + nb]
        i, j = pl.program_id(0), pl.program_id(1)

        @pl.when(pl.program_id(2) == 0)
        def _():
            o_ref[...] = jnp.zeros_like(o_ref)

        for ta in range(na):
            for tb in range(nb):
                in_a = jnp.logical_and(i >= offs_a[ta], i < offs_a[ta] + counts_a[ta])
                in_b = jnp.logical_and(j >= offs_b[tb], j < offs_b[tb] + counts_b[tb])

                @pl.when(jnp.logical_and(in_a, in_b))
                def _(ta=ta, tb=tb):
                    o_ref[...] += _tn(a_refs[ta][...], b_refs[tb][...])

    def spec(offs, counts, t, axis):
        def index(i, j, k):
            pos = (i, j)[axis]
            mine = jnp.logical_and(pos >= offs[t], pos < offs[t] + counts[t])
            return jnp.where(mine, k, 0), jnp.clip(pos - offs[t], 0, counts[t] - 1)
        return pl.BlockSpec((tk, tile), index)

    return pl.pallas_call(
        body, name=name, grid=(ni, nj, nk),
        in_specs=[spec(offs_a, counts_a, t, 0) for t in range(na)] + [spec(offs_b, counts_b, t, 1) for t in range(nb)],
        out_specs=pl.BlockSpec((tile, tile), lambda i, j, k: (i, j)),
        out_shape=SDS((ni * tile, nj * tile), f32),
        compiler_params=pltpu.CompilerParams(dimension_semantics=("parallel", "parallel", "arbitrary")),
    )(*a_parts, *b_parts)


def _adamw(w, g, m, v):
    m = ADAM_B1 * m + (1.0 - ADAM_B1) * g
    v = ADAM_B2 * v + (1.0 - ADAM_B2) * (g * g)
    m_hat = m / (1.0 - ADAM_B1 ** ADAM_STEP)
    v_hat = v / (1.0 - ADAM_B2 ** ADAM_STEP)
    delta = -ADAM_LR * (m_hat / (jnp.sqrt(v_hat) + ADAM_EPS) + ADAM_WD * w)
    return delta, m, v


def _sum_adamw(parts, w, m, v, name):
    r, c = w.shape
    tc = 256

    def body(p_ref, w_ref, m_ref, v_ref, g_ref, d_ref, nm_ref, nv_ref):
        g = p_ref[0].astype(f32)
        for q in range(1, 4):
            g = g + p_ref[q].astype(f32)
        g_ref[...] = g
        d_ref[...], nm_ref[...], nv_ref[...] = _adamw(w_ref[...], g, m_ref[...], v_ref[...])

    blk = pl.BlockSpec((r, tc), lambda i: (0, i))
    return pl.pallas_call(
        body, name=name, grid=(c // tc,),
        in_specs=[pl.BlockSpec((4, r, tc), lambda i: (0, 0, i)), blk, blk, blk],
        out_specs=[blk] * 4, out_shape=[SDS((r, c), f32)] * 4,
        compiler_params=pltpu.CompilerParams(dimension_semantics=("parallel",)),
    )(parts, w, m, v)


def _sum_small(parts, pre_blocks):
    def body(p_ref, b_ref, o_ref):
        t = p_ref[0]
        pre = b_ref[0]
        for j in range(1, N_DEV):
            t = t + p_ref[j]
            pre = pre + b_ref[j]
        o_ref[...] = t
        o_ref[5:6, 0:D_MODEL] = pre[0:1, :]
        row_h = _iota((D_SSM, LANES), 0) // HEAD_DIM
        fold = (row_h == _iota((D_SSM, LANES), 1)).astype(f32)
        lower = t[8:16, 0:LANES]
        folded = _nn_hi(t[8:16, 0:D_SSM], fold)
        loss = jnp.sum(t[11:12, 0:D_MODEL], axis=1, keepdims=True) * (0.5 / D_MODEL)
        row = _iota((8, LANES), 0)
        o_ref[8:16, 0:LANES] = jnp.where(row < 2, folded, jnp.where(row == 4, loss, lower))

    return pl.pallas_call(body, name="sum_small", out_shape=SDS((PACK_ROWS, PACK_W), f32),
                          in_specs=[pl.BlockSpec(memory_space=pltpu.VMEM)] * 2,
                          out_specs=pl.BlockSpec(memory_space=pltpu.VMEM))(parts, pre_blocks)


def _adamw_small(w, g, m, v):
    def body(w_ref, g_ref, m_ref, v_ref, d_ref, nm_ref, nv_ref):
        d_ref[...], nm_ref[...], nv_ref[...] = _adamw(w_ref[...], g_ref[...], m_ref[...], v_ref[...])

    vm = pl.BlockSpec(memory_space=pltpu.VMEM)
    return pl.pallas_call(body, name="adamw_small", out_shape=[SDS(w.shape, f32)] * 3,
                          in_specs=[vm] * 4, out_specs=[vm] * 3)(w, g, m, v)


def _pad_lanes(v, width):
    return jnp.pad(v, ((0, 0), (0, width - v.shape[1])))


def _local_step(x, tgt, norm_pre_w, wt, conv_w, conv_b, dt_bias, a_log, d_skip, ssm_norm_w, wo, norm_post_w, sharded):
    dtb16 = _pad_lanes(dt_bias, LANES)
    alog16 = _pad_lanes(a_log, LANES)
    alog_f = jnp.repeat(a_log, HEAD_DIM, axis=1)
    d_f = jnp.repeat(d_skip, HEAD_DIM, axis=1)

    shard_out = wo.shape[0]
    if sharded:
        proj, u, (g_out, g_cw) = _prenorm_inproj(x, norm_pre_w, wt, gather=(wo, conv_w))
        wo = g_out.reshape(N_DEV * shard_out, D_MODEL)
        conv_w = g_cw.transpose(1, 0, 2).reshape(4, D_CONV)
    else:
        proj, u, _ = _prenorm_inproj(x, norm_pre_w, wt)
    o, lb, mix_a = _attn_fwd(proj)
    mix_s, y, states, cv = _ssd_fwd(proj, conv_w, conv_b, dtb16, alog16, alog_f, d_f, ssm_norm_w)
    dmix, dout, dres, acc_post = _outproj_loss(mix_a, mix_s, wo, x, tgt, norm_post_w)
    dw_out = _matmul_tn([mix_a, mix_s], [dout], "dw_out")
    ssd_args = (proj, y, states, cv, dmix, conv_w, conv_b, dtb16, alog16, alog_f, d_f, ssm_norm_w)
    if sharded:
        dq, dk, dv, dg, got_out = _attn_bwd(proj, o, lb, dmix, swap=dw_out)
        chip_out = _chip_sum(dw_out, got_out, shard_out, "chip_sum_w_out")
        dzxd, g_conv, g_vec, g_dt, (parts_out,) = _ssd_bwd(*ssd_args, chip_sums=[chip_out])
    else:
        dq, dk, dv, dg = _attn_bwd(proj, o, lb, dmix)
        dzxd, g_conv, g_vec, g_dt, _ = _ssd_bwd(*ssd_args)
    dparts = [dq, dk, dv, dg, dzxd]

    def pack(g_pre_row):
        return jnp.concatenate(
            [g_conv[0:5], g_pre_row, _pad_lanes(g_vec[0:1], PACK_W), _pad_lanes(acc_post[1:2], PACK_W),
             _pad_lanes(g_vec[1:3], PACK_W), _pad_lanes(g_dt[0:1], PACK_W), _pad_lanes(acc_post[0:1], PACK_W),
             jnp.zeros((4, PACK_W), f32)], axis=0)

    if sharded:
        dw_in, got_in = _dw_in_swap(dparts, u)
        chip_in = _chip_sum(dw_in, got_in, D_IN_PROJ // N_DEV, "chip_sum_w_in")
        grad_x, _, (parts_in,), small = _inproj_bwd(dparts, wt, x, norm_pre_w, dres, [chip_in],
                                                    pack(jnp.zeros((1, PACK_W), f32)))
        return grad_x, (parts_in, parts_out), small
    dw_in = _matmul_tn(dparts, [u], "dw_in")
    grad_x, g_pre, _, _ = _inproj_bwd(dparts, wt, x, norm_pre_w, dres)
    return grad_x, (dw_in, dw_out), pack(_pad_lanes(g_pre[0:1], PACK_W))


def kernel(x, norm_pre_w, w_in, conv_w, conv_b, dt_bias, a_log, d_skip, ssm_norm_w, w_out, norm_post_w, loss_target, m_norm_pre_w, m_w_in, m_conv_w, m_conv_b, m_dt_bias, m_a_log, m_d_skip, m_ssm_norm_w, m_w_out, m_norm_post_w, v_norm_pre_w, v_w_in, v_conv_w, v_conv_b, v_dt_bias, v_a_log, v_d_skip, v_ssm_norm_w, v_w_out, v_norm_post_w):
    shard_in = w_in.shape[2]
    shard_cv = conv_w.shape[2]
    me = 4 * lax.axis_index("x") + 2 * lax.axis_index("y") + lax.axis_index("c")

    g_in, = _all_gather([w_in[0].T.astype(bf16)])
    wt = _assemble_wt(g_in)

    grad_x, (parts_in, parts_out), (parts_small, pre_blocks) = _local_step(
        x[0], loss_target[0], norm_pre_w, wt, conv_w[0], conv_b, dt_bias, a_log, d_skip, ssm_norm_w,
        w_out[0].astype(bf16), norm_post_w, sharded=True)

    g_w_in, d_w_in, nm_w_in, nv_w_in = (a.T for a in _sum_adamw(
        parts_in, w_in[0].T, m_w_in[0].T, v_w_in[0].T, "sum_adamw_w_in"))
    g_w_out, d_w_out, nm_w_out, nv_w_out = _sum_adamw(parts_out, w_out[0], m_w_out[0], v_w_out[0], "sum_adamw_w_out")
    tot = _sum_small(parts_small, pre_blocks)

    g_cw_all = tot[0:4]
    small_g = {
        "conv_w": lax.dynamic_slice(g_cw_all, (0, me * shard_cv), (4, shard_cv)),
        "conv_b": tot[4:5], "norm_pre_w": tot[5:6, :D_MODEL], "ssm_norm_w": tot[6:7, :D_SSM],
        "norm_post_w": tot[7:8, :D_MODEL], "a_log": tot[8:9, :16], "d_skip": tot[9:10, :16], "dt_bias": tot[10:11, :16],
    }
    loss = tot[12, 0]
    small_w = {"conv_w": (conv_w[0], m_conv_w[0], v_conv_w[0]), "conv_b": (conv_b, m_conv_b, v_conv_b),
               "norm_pre_w": (norm_pre_w, m_norm_pre_w, v_norm_pre_w), "ssm_norm_w": (ssm_norm_w, m_ssm_norm_w, v_ssm_norm_w),
               "norm_post_w": (norm_post_w, m_norm_post_w, v_norm_post_w), "a_log": (a_log, m_a_log, v_a_log),
               "d_skip": (d_skip, m_d_skip, v_d_skip), "dt_bias": (dt_bias, m_dt_bias, v_dt_bias)}
    names = list(small_w)
    sizes = [small_g[k].size for k in names]
    tot_size = sum(sizes)
    pad_to = -(-tot_size // 1024) * 1024

    def flat(arrs):
        v = jnp.concatenate([a.reshape(-1) for a in arrs])
        return jnp.pad(v, (0, pad_to - tot_size)).reshape(pad_to // LANES, LANES)

    fw = flat([small_w[k][0] for k in names])
    fg = flat([small_g[k] for k in names])
    fm = flat([small_w[k][1] for k in names])
    fv = jnp.pad(jnp.concatenate([small_w[k][2].reshape(-1) for k in names]), (0, pad_to - tot_size),
                 constant_values=1.0).reshape(pad_to // LANES, LANES)
    fd, fnm, fnv = _adamw_small(fw, fg, fm, fv)

    def unflat(f):
        out, off = {}, 0
        v = f.reshape(-1)
        for k, n in zip(names, sizes):
            out[k] = v[off:off + n].reshape(small_g[k].shape)
            off += n
        return out

    sd, snm, snv = unflat(fd), unflat(fnm), unflat(fnv)
    lead = lambda a: a[None]
    order = ["norm_pre_w", "w_in", "conv_w", "conv_b", "dt_bias", "a_log", "d_skip", "ssm_norm_w", "w_out", "norm_post_w"]
    grads = dict(small_g, w_in=g_w_in, w_out=g_w_out)
    deltas = dict(sd, w_in=d_w_in, w_out=d_w_out)
    new_m = dict(snm, w_in=nm_w_in, w_out=nm_w_out)
    new_v = dict(snv, w_in=nv_w_in, w_out=nv_w_out)

    def shaped(dct, k):
        a = dct[k]
        return lead(a) if k in ("w_in", "w_out", "conv_w") else a

    return (loss, grad_x[None], *[shaped(grads, k) for k in order], *[shaped(deltas, k) for k in order],
            *[shaped(new_m, k) for k in order], *[shaped(new_v, k) for k in order])
```

```python
import jax
import jax.numpy as jnp
from jax import lax
from jax.experimental import pallas as pl
from jax.experimental.pallas import tpu as pltpu

f32, bf16 = jnp.float32, jnp.bfloat16
SDS = jax.ShapeDtypeStruct
HIGHEST = lax.Precision.HIGHEST
MESH = pl.DeviceIdType.MESH

N_DEV = 8
D_MODEL = 1024
D_ATTN = 1024
D_SSM = 1024
HEAD_DIM = 64
N_PAIRS = 8
D_STATE = 128
N_GROUPS = 2
D_CONV = D_SSM + 2 * N_GROUPS * D_STATE
D_IN_PROJ = 4 * D_ATTN + D_SSM + D_CONV + 16
NP = 7168
CHUNK = 128
BLK = 128
DILATIONS = (1, 4, 16)
EPS = 1e-6
LANES = 128
COL_Z, COL_XS, COL_BC, COL_DT = 4096, 5120, 6144, 6656

ADAM_LR, ADAM_B1, ADAM_B2, ADAM_EPS, ADAM_WD, ADAM_STEP = 0.001, 0.9, 0.999, 1e-08, 0.01, 10

PACK_ROWS, PACK_W = 16, 1536


def _nt(a, b):
    return lax.dot_general(a, b, (((1,), (1,)), ((), ())), preferred_element_type=f32)


def _tn(a, b):
    return lax.dot_general(a, b, (((0,), (0,)), ((), ())), preferred_element_type=f32)


def _nn(a, b):
    return jnp.dot(a, b, preferred_element_type=f32)


def _nn_hi(a, b):
    return jnp.dot(a, b, precision=HIGHEST, preferred_element_type=f32)


def _sigmoid(x):
    return 1.0 / (1.0 + jnp.exp(-x))


def _softplus(x):
    return jnp.maximum(x, 0.0) + jnp.log1p(jnp.exp(-jnp.abs(x)))


def _iota(shape, dim):
    return lax.broadcasted_iota(jnp.int32, shape, dim)


def _my_pos():
    return lax.axis_index("x"), lax.axis_index("y"), lax.axis_index("c")


GATHER_SEMS = 9


def _gather_phases(ins, outs, send_sems, recv_sems, local_sems):
    n, ns = len(ins), GATHER_SEMS
    x, y, c = _my_pos()
    me, sibling = (x, y, c), (x, y, 1 - c)
    xn, yn, diag = (1 - x, y), (x, 1 - y), (1 - x, 1 - y)

    def slot(a, px, py, pc):
        return outs[a].at[4 * px + 2 * py + pc]

    def part(a, ref, h):
        width = ins[a].shape[-1]
        if width % (2 * LANES):
            return ref if h == 1 else None
        return ref.at[:, pl.ds(h * (width // 2), width // 2)]

    def copy(a, k, block, to, src=None, h=None):
        src_ref = slot(a, *block) if src is None else src
        dst_ref = slot(a, *block)
        if h is not None:
            src_ref, dst_ref = part(a, src_ref, h), part(a, dst_ref, h)
            if src_ref is None:
                return None
        return pltpu.make_async_remote_copy(
            src_ref=src_ref, dst_ref=dst_ref, send_sem=send_sems.at[ns * a + k], recv_sem=recv_sems.at[ns * a + k],
            device_id=to, device_id_type=MESH)

    def mine():
        return [pltpu.make_async_copy(ins[a], slot(a, *me), local_sems.at[a]) for a in range(n)]

    def own_sends(a):
        return [copy(a, 0, me, sibling, src=ins[a]), copy(a, 1, me, (*xn, c), src=ins[a]),
                copy(a, 2, me, (*yn, c), src=ins[a])]

    def neighbour_relays(a):
        return [copy(a, 4, (*xn, c), sibling), copy(a, 7, (*xn, c), (*yn, c), h=1),
                copy(a, 5, (*yn, c), sibling), copy(a, 8, (*yn, c), (*xn, c), h=0)]

    def diagonal_halves(a):
        return [copy(a, k, (*diag, c), me, h=h) for k, h in ((8, 0), (7, 1))]

    def start_all(cps):
        for cp in cps:
            if cp is not None:
                cp.start()

    def phase0():
        start_all(mine())
        for a in range(n):
            start_all(own_sends(a))

    def phase1():
        for a in range(n):
            copy(a, 1, (*xn, c), me).wait_recv()
            copy(a, 2, (*yn, c), me).wait_recv()
            start_all(neighbour_relays(a))

    def phase2():
        for a in range(n):
            for cp in diagonal_halves(a):
                if cp is not None:
                    cp.wait_recv()
            copy(a, 6, (*diag, c), sibling).start()

    def finish():
        for a in range(n):
            copy(a, 0, sibling, me).wait_recv()
            for j, chip in enumerate((xn, yn, diag)):
                copy(a, 4 + j, (*chip, 1 - c), me).wait_recv()
        for a in range(n):
            for cp in own_sends(a) + neighbour_relays(a) + [copy(a, 6, (*diag, c), sibling)]:
                if cp is not None:
                    cp.wait_send()
        for cp in mine():
            cp.wait()

    return phase0, phase1, phase2, finish


def _gather_scratch(n):
    return [pltpu.SemaphoreType.DMA((GATHER_SEMS * n,)), pltpu.SemaphoreType.DMA((GATHER_SEMS * n,)),
            pltpu.SemaphoreType.DMA((n,))]


def _all_gather(arrs):
    n = len(arrs)

    def body(*refs):
        for phase in _gather_phases(refs[:n], refs[n:2 * n], *refs[2 * n:]):
            phase()

    anyspec = pl.BlockSpec(memory_space=pl.ANY)
    return pl.pallas_call(
        body, name="weights_all_gather",
        out_shape=[SDS((N_DEV,) + a.shape, a.dtype) for a in arrs],
        in_specs=[anyspec] * n, out_specs=[anyspec] * n, scratch_shapes=_gather_scratch(n),
    )(*arrs)


def _dw_in_swap(a_parts, u):
    tile, tk = 1024, 1024
    s = u.shape[0]
    nk = s // tk
    na = len(a_parts)
    offs, counts, ni = _col_blocks(a_parts, tile)

    def body(*refs):
        a_refs, u_ref = refs[:na], refs[na]
        dw_ref, got_ref = refs[na + 1:na + 3]
        acc, stage, local_sems, send_sems, recv_sem = refs[na + 3:]
        i, k = pl.program_id(0), pl.program_id(1)
        x, y, c = _my_pos()
        par = i % 2

        def tile_copies(t, p):
            rows = pl.ds(pl.multiple_of(t * tile, tile), tile)
            loc = pltpu.make_async_copy(stage.at[p], dw_ref.at[rows], local_sems.at[p])
            rem = pltpu.make_async_remote_copy(
                src_ref=stage.at[p], dst_ref=got_ref.at[rows], send_sem=send_sems.at[p], recv_sem=recv_sem,
                device_id=(x, y, 1 - c), device_id_type=MESH)
            return loc, rem

        @pl.when(k == 0)
        def _():
            acc[...] = jnp.zeros((tile, tile), f32)

        for t in range(na):
            @pl.when(jnp.logical_and(i >= offs[t], i < offs[t] + counts[t]))
            def _(t=t):
                acc[...] += _tn(a_refs[t][...], u_ref[pl.ds(pl.multiple_of(k * tk, tk), tk), :])

        @pl.when(k == nk - 1)
        def _():
            @pl.when(i >= 2)
            def _():
                loc, rem = tile_copies(i - 2, par)
                loc.wait()
                rem.wait_send()
            stage[par] = acc[...]
            loc, rem = tile_copies(i, par)
            loc.start()
            rem.start()

        @pl.when(jnp.logical_and(i == ni - 1, k == nk - 1))
        def _():
            for t in (ni - 2, ni - 1):
                loc, rem = tile_copies(t, t % 2)
                loc.wait()
                rem.wait_send()
            pltpu.make_async_remote_copy(src_ref=dw_ref, dst_ref=got_ref, send_sem=send_sems.at[0], recv_sem=recv_sem,
                                         device_id=(x, y, c), device_id_type=MESH).wait_recv()

    def a_spec(t):
        def index(i, k):
            mine = jnp.logical_and(i >= offs[t], i < offs[t] + counts[t])
            return jnp.where(mine, k, 0), jnp.clip(i - offs[t], 0, counts[t] - 1)
        return pl.BlockSpec((tk, tile), index)

    anyspec = pl.BlockSpec(memory_space=pl.ANY)
    return pl.pallas_call(
        body, name="dw_in_swap", grid=(ni, nk),
        in_specs=[a_spec(t) for t in range(na)] + [pl.BlockSpec((s, tile), lambda i, k: (0, 0))],
        out_specs=[anyspec] * 2,
        out_shape=[SDS((ni * tile, tile), f32), SDS((ni * tile, tile), f32)],
        scratch_shapes=[pltpu.VMEM((tile, tile), f32), pltpu.VMEM((2, tile, tile), f32), pltpu.SemaphoreType.DMA((2,)),
                        pltpu.SemaphoreType.DMA((2,)), pltpu.SemaphoreType.DMA(())],
        compiler_params=pltpu.CompilerParams(dimension_semantics=("arbitrary", "arbitrary")),
    )(*a_parts, u)


def _chip_sum(mine, got, rows, name):
    r, cdim = mine.shape
    tc = LANES

    def body(m_ref, g_ref, s16_ref):
        c = lax.axis_index("c")
        for q in range(4):
            blk = pl.ds(rows * (2 * q + c), rows)
            s16_ref[q] = (m_ref[blk, :] + g_ref[blk, :]).astype(bf16)

    col = pl.BlockSpec((r, tc), lambda i: (0, i))
    return pl.pallas_call(
        body, name=name, grid=(cdim // tc,), in_specs=[col, col],
        out_specs=pl.BlockSpec((4, rows, tc), lambda i: (0, 0, i)), out_shape=SDS((4, rows, cdim), bf16),
        compiler_params=pltpu.CompilerParams(dimension_semantics=("parallel",)),
    )(mine, got)


def _assemble_wt(shards):
    nd, rows, cdim = shards.shape
    tc = 256

    def body(g_ref, o_ref):
        for j in range(nd):
            o_ref[pl.ds(rows * j, rows), :] = g_ref[j]
        o_ref[pl.ds(nd * rows, NP - nd * rows), :] = jnp.zeros((NP - nd * rows, tc), shards.dtype)

    return pl.pallas_call(
        body, name="assemble_w_in", grid=(cdim // tc,),
        in_specs=[pl.BlockSpec((nd, rows, tc), lambda i: (0, 0, i))],
        out_specs=pl.BlockSpec((NP, tc), lambda i: (0, i)), out_shape=SDS((NP, cdim), shards.dtype),
        compiler_params=pltpu.CompilerParams(dimension_semantics=("parallel",)),
    )(shards)


def _chip_exchange_copies(ins, outs, send_sems, recv_sems, local_sems):
    nb = len(ins)
    x, y, c = _my_pos()
    my_q = 2 * x + y
    mine = [pltpu.make_async_copy(ins[a].at[my_q], outs[a].at[my_q], local_sems.at[a]) for a in range(nb)]
    sends, recvs = [], []
    for k in range(1, 4):
        to, frm = (my_q + k) % 4, (my_q + 4 - k) % 4
        for a in range(nb):
            sems = dict(send_sem=send_sems.at[3 * a + k - 1], recv_sem=recv_sems.at[3 * a + k - 1], device_id_type=MESH)
            sends.append(pltpu.make_async_remote_copy(
                src_ref=ins[a].at[to], dst_ref=outs[a].at[my_q], device_id=(to // 2, to % 2, c), **sems))
            recvs.append(pltpu.make_async_remote_copy(
                src_ref=ins[a].at[frm], dst_ref=outs[a].at[frm], device_id=(x, y, c), **sems))
    return mine, sends, recvs


def _chip_exchange_scratch(nb):
    return [pltpu.SemaphoreType.DMA((3 * nb,)), pltpu.SemaphoreType.DMA((3 * nb,)), pltpu.SemaphoreType.DMA((nb,))]


def _prenorm_inproj(x, nw, wt, gather=()):
    s, d = x.shape
    npad = wt.shape[0]
    tm, tn = 1024, 1024
    ng = len(gather)
    ni, nj = s // tm, npad // tn

    def body(x_ref, nw_ref, w_ref, *refs):
        g_in, (proj_ref, u_ref), g_out, sems = refs[:ng], refs[ng:ng + 2], refs[ng + 2:2 * ng + 2], refs[2 * ng + 2:]
        i, j = pl.program_id(0), pl.program_id(1)
        if ng:
            phases = _gather_phases(g_in, g_out, *sems)
            for step, phase in enumerate(phases[:3]):
                @pl.when(jnp.logical_and(i == step, j == 0))
                def _(phase=phase):
                    phase()

        @pl.when(j == 0)
        def _():
            xv = x_ref[...]
            r = lax.rsqrt(jnp.mean(xv * xv, axis=-1, keepdims=True) + EPS)
            u_ref[...] = (xv * r * nw_ref[...]).astype(bf16)
        proj_ref[...] = _nt(u_ref[...], w_ref[pl.ds(pl.multiple_of(j * tn, tn), tn), :])

        if ng:
            @pl.when(jnp.logical_and(i == ni - 1, j == nj - 1))
            def _():
                phases[3]()

    anyspec = pl.BlockSpec(memory_space=pl.ANY)
    outs = pl.pallas_call(
        body, name="prenorm_inproj", grid=(ni, nj),
        in_specs=[pl.BlockSpec((tm, d), lambda i, j: (i, 0)), pl.BlockSpec((1, d), lambda i, j: (0, 0)),
                  pl.BlockSpec((npad, d), lambda i, j: (0, 0))] + [anyspec] * ng,
        out_specs=[pl.BlockSpec((tm, tn), lambda i, j: (i, j)), pl.BlockSpec((tm, d), lambda i, j: (i, 0))]
        + [anyspec] * ng,
        out_shape=[SDS((s, npad), f32), SDS((s, d), bf16)] + [SDS((N_DEV,) + a.shape, a.dtype) for a in gather],
        scratch_shapes=_gather_scratch(ng) if ng else [],
        compiler_params=pltpu.CompilerParams(dimension_semantics=("arbitrary", "arbitrary")),
    )(x, nw, wt, *gather)
    return outs[0], outs[1], outs[2:]


def _attn_consts():
    head0 = _iota((BLK, LANES), 1) < HEAD_DIM
    tri2 = (_iota((BLK, 2 * LANES), 1) % LANES) <= _iota((BLK, 2 * LANES), 0)
    ones2 = ((_iota((LANES, 2 * LANES), 0) < HEAD_DIM) == (_iota((LANES, 2 * LANES), 1) < LANES)).astype(bf16)
    rmat = ((_iota((2 * LANES, LANES), 0) < LANES) == (_iota((2 * LANES, LANES), 1) < HEAD_DIM)).astype(bf16)
    bones = ((_iota((LANES, LANES), 0) < HEAD_DIM) == (_iota((LANES, LANES), 1) < HEAD_DIM)).astype(bf16)
    return head0, tri2, ones2, rmat, bones


def _stack_heads(x16, head0):
    zero = jnp.zeros_like(x16)
    return jnp.concatenate([jnp.where(head0, x16, zero), jnp.where(head0, zero, x16)], axis=0)


def _bf16_terms(x, terms):
    out = []
    for _ in range(terms):
        t = x.astype(bf16)
        out.append(t)
        x = x - t.astype(f32)
    return out


def _dot_01(x, w16, terms):
    return _nn(jnp.concatenate(_bf16_terms(x, terms), axis=1), jnp.concatenate([w16] * terms, axis=0))


def _split_dot_sum(x, w16):
    hi, lo = _bf16_terms(x, 2)
    return _nn(hi, w16) + _nn(lo, w16)


def _dot_01_left(w16, x, terms):
    return _nn(jnp.concatenate([w16] * terms, axis=1), jnp.concatenate(_bf16_terms(x, terms), axis=0))


def _attn_fwd(proj):
    s = proj.shape[0]
    n_it = s // BLK

    def body(q_ref, k_ref, v_ref, g_ref, o_ref, l_ref, mix_ref, op0, op1, op2, lp0, lp1, lp2,
             s_a, s_b, sd_a, sd_b, p_a, p_b, m_a, m_b, pd_a, pd_b, k_a, k_b, v_a, v_b):
        op_refs, lp_refs = (op0, op1, op2), (lp0, lp1, lp2)
        head0, tri2, ones2, rmat, _ = _attn_consts()
        score_bufs, prob_bufs = ((s_a, sd_a), (s_b, sd_b)), ((p_a, m_a, pd_a), (p_b, m_b, pd_b))
        k_bufs, v_bufs = (k_a, k_b), (v_a, v_b)
        for buf in k_bufs + v_bufs:
            buf[...] = jnp.zeros_like(buf)

        def block_rows(i, d, nb):
            r, blk = i // nb, i % nb
            return pl.ds(blk * (BLK * d) + r, BLK, stride=d), blk > 0

        def unstack(st16):
            return st16[:BLK] + st16[BLK:]

        def scores(i, par, d, nb):
            rows, has_prev = block_rows(i, d, nb)
            s_buf, sd_buf = score_bufs[par]
            qs = q_ref[rows, :] * 0.125
            qs16 = qs.astype(bf16)
            kst_c = _stack_heads(k_ref[rows, :].astype(bf16), head0)
            kst_p = k_bufs[1 - par][...]
            k_bufs[par][...] = kst_c
            sc = _nt(qs16, kst_c)
            sp = _nt(qs16, kst_p)
            s_buf[...] = jnp.where(tri2, sc, jnp.where(has_prev, sp, -jnp.inf))
            sd = _nn((qs * unstack(kst_p).astype(f32)).astype(bf16), ones2)
            sd_buf[...] = jnp.where(has_prev, sd, -jnp.inf)

        def softmax(bufs_in, bufs_out):
            s_buf, sd_buf = bufs_in
            p_buf, m_buf, pd_buf = bufs_out
            sc, sd2 = s_buf[...], sd_buf[...]
            m0 = jnp.max(sc[:, :LANES], axis=1, keepdims=True)
            m1 = jnp.max(sc[:, LANES:], axis=1, keepdims=True)
            m2 = jnp.concatenate([jnp.broadcast_to(m0, (BLK, LANES)), jnp.broadcast_to(m1, (BLK, LANES))], axis=1)
            m2 = jnp.maximum(m2, sd2)
            p_buf[...] = jnp.exp(sc - m2).astype(bf16)
            m_pair = jnp.where(head0, m2[:, :LANES], m2[:, LANES:])
            m_buf[...] = m_pair
            pd_buf[...] = jnp.exp(jnp.where(head0, sd2[:, :LANES], sd2[:, LANES:]) - m_pair)

        def output(i, par, d, nb, p):
            rows, _ = block_rows(i, d, nb)
            p_buf, m_buf, pd_buf = prob_bufs[par]
            vst_c = _stack_heads(v_ref[rows, :].astype(bf16), head0)
            vst_p = v_bufs[1 - par][...]
            v_bufs[par][...] = vst_c
            pt16, pd = p_buf[...], pd_buf[...]
            zero = jnp.zeros_like(pt16)
            o = (_nn(jnp.where(tri2, pt16, zero), vst_c) + _nn(jnp.where(tri2, zero, pt16), vst_p)
                 + pd * unstack(vst_p).astype(f32))
            l = _nn(pt16, rmat) + pd
            op_refs[p][rows, :] = o / l
            lp_refs[p][rows, :] = m_buf[...] + jnp.log(l)

        for p, d in enumerate(DILATIONS):
            nb = s // (BLK * d)
            scores(0, 0, d, nb)
            scores(1, 1, d, nb)
            softmax(score_bufs[0], prob_bufs[0])

            def steps(j, carry, d=d, nb=nb, p=p):
                for par in range(2):
                    t = 2 * j + 2 + par
                    scores(t, par, d, nb)
                    output(t - 2, par, d, nb, p)
                    softmax(score_bufs[1 - par], prob_bufs[1 - par])
                return carry

            lax.fori_loop(0, (n_it - 2) // 2, steps, 0)
            output(n_it - 2, 0, d, nb, p)
            softmax(score_bufs[1], prob_bufs[1])
            output(n_it - 1, 1, d, nb, p)

        def merge(i, carry):
            rows = pl.ds(pl.multiple_of(i * 256, 256), 256)
            l0, l1, l2 = lp0[rows, :], lp1[rows, :], lp2[rows, :]
            m = jnp.maximum(jnp.maximum(l0, l1), l2)
            e0, e1, e2 = jnp.exp(l0 - m), jnp.exp(l1 - m), jnp.exp(l2 - m)
            z = e0 + e1 + e2
            o = (e0 * op0[rows, :] + e1 * op1[rows, :] + e2 * op2[rows, :]) / z
            o_ref[rows, :] = o
            l_ref[rows, :] = m + jnp.log(z)
            g = g_ref[rows, :]
            mix_ref[rows, :] = (o * (g * _sigmoid(g))).astype(bf16)
            return carry

        lax.fori_loop(0, s // 256, merge, 0)

    col = lambda base: pl.BlockSpec((s, LANES), lambda h: (0, base + h))
    return pl.pallas_call(
        body, name="attn_fwd", grid=(N_PAIRS,),
        in_specs=[col(0), col(8), col(16), col(24)],
        out_specs=[col(0), col(0), col(0)],
        out_shape=[SDS((s, D_ATTN), f32), SDS((s, D_ATTN), f32), SDS((s, D_ATTN), bf16)],
        scratch_shapes=[pltpu.VMEM((s, LANES), f32)] * 6 + [pltpu.VMEM((BLK, 2 * LANES), f32)] * 4
        + [pltpu.VMEM((BLK, 2 * LANES), bf16)] * 2 + [pltpu.VMEM((BLK, LANES), f32)] * 4
        + [pltpu.VMEM((2 * BLK, LANES), bf16)] * 4,
        compiler_params=pltpu.CompilerParams(dimension_semantics=("parallel",)),
    )(proj, proj, proj, proj)


def _expand_mat():
    colv = _iota((LANES, 2 * D_SSM), 1)
    head = 2 * ((colv % D_SSM) // LANES) + colv // D_SSM
    return (_iota((LANES, 2 * D_SSM), 0) == head).astype(bf16)


def _fold_mat():
    return (_iota((D_SSM, LANES), 0) // HEAD_DIM == _iota((D_SSM, LANES), 1)).astype(bf16)


def _conv(xs_ref, bc_ref, xs_tail, bc_tail, cw_ref, cb_ref, xpad, first):
    keep = jnp.where(first, 0.0, 1.0)
    xpad[0:8, 0:D_SSM] = xs_tail[...] * keep
    xpad[0:8, D_SSM:D_CONV] = bc_tail[...] * keep
    xpad[8:8 + CHUNK, 0:D_SSM] = xs_ref[...]
    xpad[8:8 + CHUNK, D_SSM:D_CONV] = bc_ref[...]
    xp = xpad[...]
    cv = cb_ref[...] + cw_ref[3:4, :] * xp[8:8 + CHUNK]
    for j in range(3):
        cv = cv + cw_ref[j:j + 1, :] * pltpu.roll(xp, 3 - j, 0)[8:8 + CHUNK]
    return cv


def _decay_terms(dt_ref, dtb_ref, alog16_ref, emat_ref):
    pre = dt_ref[...] + dtb_ref[...]
    dt16 = _softplus(pre)
    a16 = -jnp.exp(alog16_ref[...])
    sub, lane = _iota((CHUNK, CHUNK), 0), _iota((CHUNK, CHUNK), 1)
    tri = (sub >= lane).astype(f32)
    al16 = _nn_hi(tri, dt16 * a16)
    al_t = al16.T
    emat = emat_ref[...]
    dt_x = _dot_01(dt16, emat, 3)
    al_x = _dot_01(al16, emat, 3)
    lane_w = _iota((CHUNK, D_SSM), 1)
    even = (lane_w % LANES) < HEAD_DIM
    dt_f = jnp.where(even, dt_x[:, :D_SSM], dt_x[:, D_SSM:])
    al_f = jnp.where(even, al_x[:, :D_SSM], al_x[:, D_SSM:])
    return pre, dt_f, al_f, al_x, al_t


def _decay_mat(al_x, al_t, pair, h):
    sub, lane = _iota((CHUNK, CHUNK), 0), _iota((CHUNK, CHUNK), 1)
    col = al_x[:, h * D_SSM + pair * LANES: h * D_SSM + (pair + 1) * LANES]
    row = al_t[2 * pair + h: 2 * pair + h + 1, :]
    return jnp.exp(jnp.where(sub >= lane, col - row, -jnp.inf))


def _ssd_in_specs(order):
    blk = lambda w, cb: pl.BlockSpec((CHUNK, w), lambda i: (order(i), cb))
    tail = lambda w, cb: pl.BlockSpec((8, w), lambda i: (jnp.maximum(16 * order(i) - 1, 0), cb))
    return [blk(D_SSM, COL_XS // D_SSM), blk(512, COL_BC // 512), tail(D_SSM, COL_XS // D_SSM),
            tail(512, COL_BC // 512), blk(LANES, COL_DT // LANES), blk(D_SSM, COL_Z // D_SSM)]


def _full(shape):
    return pl.BlockSpec(shape, lambda i: (0,) * len(shape))


def _ssd_fwd(proj, conv_w, conv_b, dtb16, alog16, alog_f, d_f, nw):
    s = proj.shape[0]
    nc = s // CHUNK

    def body(xs_ref, bc_ref, xs_tail, bc_tail, dt_ref, z_ref, cw_ref, cb_ref, dtb_ref, alog16_ref, alogf_ref,
             df_ref, nw_ref, mix_ref, y_ref, st_ref, cv_ref, h_scr, xpad, y_scr, emat_ref):
        c = pl.program_id(0)

        @pl.when(c == 0)
        def _():
            h_scr[...] = jnp.zeros_like(h_scr)
            emat_ref[...] = _expand_mat()

        cv = _conv(xs_ref, bc_ref, xs_tail, bc_tail, cw_ref, cb_ref, xpad, c == 0)
        cv_ref[...] = cv
        xbc = cv * _sigmoid(cv)
        _, dt_f, al_f, al_x, al_t = _decay_terms(dt_ref, dtb_ref, alog16_ref, emat_ref)
        head0 = _iota((CHUNK, LANES), 1) < HEAD_DIM
        st_ref[...] = h_scr[...]
        for g in range(N_GROUPS):
            bm = xbc[:, D_SSM + g * D_STATE: D_SSM + (g + 1) * D_STATE].astype(bf16)
            cm = xbc[:, D_SSM + (N_GROUPS + g) * D_STATE: D_SSM + (N_GROUPS + g + 1) * D_STATE].astype(bf16)
            gmat = _nt(cm, bm)
            for pair in range(4 * g, 4 * g + 4):
                sl = slice(pair * LANES, (pair + 1) * LANES)
                xp, dtp, alp = xbc[:, sl], dt_f[:, sl], al_f[:, sl]
                xdt = xp * dtp
                xdt16 = xdt.astype(bf16)
                al_last = alp[CHUNK - 1:CHUNK, :]
                hp = h_scr[:, sl]
                y_off = jnp.exp(alp) * _nn(cm, hp.astype(bf16))
                yd = [_nn((gmat * _decay_mat(al_x, al_t, pair, h)).astype(bf16), xdt16) for h in range(2)]
                y_scr[:, sl] = jnp.where(head0, yd[0], yd[1]) + y_off + df_ref[:, sl] * xp
                st = _tn(bm, (jnp.exp(al_last - alp) * xdt).astype(bf16))
                h_scr[:, sl] = jnp.exp(al_last) * hp + st
        y = y_scr[...]
        y_ref[...] = y
        z = z_ref[...]
        yz = y * (z * _sigmoid(z))
        gw = D_SSM // N_GROUPS
        for g in range(N_GROUPS):
            part = yz[:, g * gw:(g + 1) * gw]
            r = lax.rsqrt(jnp.mean(part * part, axis=-1, keepdims=True) + EPS)
            mix_ref[:, g * gw:(g + 1) * gw] = (part * r * nw_ref[:, g * gw:(g + 1) * gw]).astype(bf16)

    order = lambda i: i
    row = lambda w: pl.BlockSpec((CHUNK, w), lambda i: (i, 0))
    return pl.pallas_call(
        body, name="ssd_fwd", grid=(nc,),
        in_specs=_ssd_in_specs(order) + [_full((4, D_CONV)), _full((1, D_CONV)), _full((1, LANES)), _full((1, LANES)),
                                         _full((1, D_SSM)), _full((1, D_SSM)), _full((1, D_SSM))],
        out_specs=[row(D_SSM), row(D_SSM), pl.BlockSpec((None, D_STATE, D_SSM), lambda i: (i, 0, 0)), row(D_CONV)],
        out_shape=[SDS((s, D_SSM), bf16), SDS((s, D_SSM), f32), SDS((nc, D_STATE, D_SSM), f32),
                   SDS((s, D_CONV), f32)],
        scratch_shapes=[pltpu.VMEM((D_STATE, D_SSM), f32), pltpu.VMEM((8 + CHUNK, D_CONV), f32),
                        pltpu.VMEM((CHUNK, D_SSM), f32), pltpu.VMEM((LANES, 2 * D_SSM), bf16)],
        compiler_params=pltpu.CompilerParams(dimension_semantics=("arbitrary",)),
    )(proj, proj, proj, proj, proj, proj, conv_w, conv_b, dtb16, alog16, alog_f, d_f, nw)


def _outproj_loss(mix_a, mix_s, wo, x, tgt, npw):
    s, d = x.shape
    tm = 512

    def body(ma_ref, ms_ref, wo_ref, x_ref, t_ref, npw_ref, dmix_ref, dout_ref, dres_ref, acc_ref):
        @pl.when(pl.program_id(0) == 0)
        def _():
            acc_ref[...] = jnp.zeros_like(acc_ref)

        out = _nn(ma_ref[...], wo_ref[0:D_ATTN, :]) + _nn(ms_ref[...], wo_ref[D_ATTN:, :])
        r = lax.rsqrt(jnp.mean(out * out, axis=-1, keepdims=True) + EPS)
        on = out * r
        diff = x_ref[...] + on * npw_ref[...] - t_ref[...]
        dres = diff * (1.0 / d)
        dres_ref[...] = dres
        acc_ref[0:1, :] += jnp.sum(diff * diff, axis=0, keepdims=True)
        acc_ref[1:2, :] += jnp.sum(dres * on, axis=0, keepdims=True)
        dn = dres * npw_ref[...]
        dout = (r * (dn - on * jnp.mean(dn * on, axis=-1, keepdims=True))).astype(bf16)
        dout_ref[...] = dout
        dmix_ref[...] = _nt(dout, wo_ref[...])

    row = lambda w: pl.BlockSpec((tm, w), lambda i: (i, 0))
    return pl.pallas_call(
        body, name="outproj_loss", grid=(s // tm,),
        in_specs=[row(D_ATTN), row(D_SSM), _full((D_ATTN + D_SSM, d)), row(d), row(d), _full((1, d))],
        out_specs=[row(D_ATTN + D_SSM), row(d), row(d), _full((8, d))],
        out_shape=[SDS((s, D_ATTN + D_SSM), f32), SDS((s, d), bf16), SDS((s, d), f32), SDS((8, d), f32)],
        compiler_params=pltpu.CompilerParams(dimension_semantics=("arbitrary",)),
    )(mix_a, mix_s, wo, x, tgt, npw)


def _attn_bwd(proj, o, lb, dmix, swap=None):
    s = proj.shape[0]
    n_it = s // BLK

    nsw = 0 if swap is None else 1

    def body(*refs):
        q_ref, k_ref, v_ref, g_ref, o_ref, l_ref, dm_ref = refs[:7]
        swap_in = refs[7:7 + nsw]
        dq_ref, dk_ref, dv_ref, dg_ref = refs[7 + nsw:11 + nsw]
        swap_out = refs[11 + nsw:11 + 2 * nsw]
        dq_acc, dk_acc, dv_acc, do_scr, dl_scr = refs[11 + 2 * nsw:16 + 2 * nsw]
        bufs = refs[16 + 2 * nsw:44 + 2 * nsw]
        swap_sems = refs[44 + 2 * nsw:]
        head0, tri2, _, _, bones = _attn_consts()

        if nsw:
            x, y, c = _my_pos()
            swap_copy = pltpu.make_async_remote_copy(
                src_ref=swap_in[0], dst_ref=swap_out[0], send_sem=swap_sems[0], recv_sem=swap_sems[1],
                device_id=(x, y, 1 - c), device_id_type=MESH)

            @pl.when(pl.program_id(0) == 0)
            def _():
                swap_copy.start()

        def pro(i, carry):
            rows = pl.ds(pl.multiple_of(i * 256, 256), 256)
            g = g_ref[rows, :]
            sg = _sigmoid(g)
            dmx = dm_ref[rows, :]
            ov = o_ref[rows, :]
            dg_ref[rows, :] = (dmx * ov * (sg * (1.0 + g * (1.0 - sg)))).astype(bf16)
            do = dmx * (g * sg)
            do_scr[rows, :] = do
            dl_scr[rows, :] = _split_dot_sum(do * ov, bones)
            z = jnp.zeros((256, LANES), f32)
            dq_acc[rows, :] = z
            dk_acc[rows, :] = z
            dv_acc[rows, :] = z
            return carry

        lax.fori_loop(0, s // 256, pro, 0)

        def per_head(t):
            return jnp.concatenate([t[:, :LANES], t[:, LANES:]], axis=0)

        def both_heads(t):
            tr = pltpu.roll(t, HEAD_DIM, 1)
            return jnp.concatenate([jnp.where(head0, t, tr), jnp.where(head0, tr, t)], axis=1)

        mm_bufs = ((bufs[0], bufs[1], bufs[2], bufs[3]), (bufs[4], bufs[5], bufs[6], bufs[7]))
        ds_bufs = ((bufs[8], bufs[9], bufs[10], bufs[11]), (bufs[12], bufs[13], bufs[14], bufs[15]))
        op_bufs = ((bufs[16], bufs[17], bufs[18], bufs[19]), (bufs[20], bufs[21], bufs[22], bufs[23]))
        vc_bufs, carry_k, carry_v = (bufs[24], bufs[25]), bufs[26], bufs[27]
        for buf in (op_bufs[0][0], op_bufs[1][0]) + vc_bufs:
            buf[...] = jnp.zeros_like(buf)

        def block_rows(i, d, nb):
            r, blk = i // nb, i % nb
            return pl.ds(blk * (BLK * d) + r, BLK, stride=d), blk > 0

        def unstack(st16):
            return st16[:BLK] + st16[BLK:]

        def products(i, par, d, nb):
            rows, has_prev = block_rows(i, d, nb)
            s_buf, dp_buf, sd_buf, dpd_buf = mm_bufs[par]
            kc_buf, kp_buf, q_buf, do_buf = op_bufs[par]
            q = q_ref[rows, :]
            qs = q * 0.125
            do = do_scr[rows, :]
            qs16, do16 = qs.astype(bf16), do.astype(bf16)
            kst_c = _stack_heads(k_ref[rows, :].astype(bf16), head0)
            vst_c = _stack_heads(v_ref[rows, :].astype(bf16), head0)
            kst_p, vst_p = op_bufs[1 - par][0][...], vc_bufs[1 - par][...]
            kc_buf[...] = kst_c
            kp_buf[...] = kst_p
            vc_bufs[par][...] = vst_c
            q_buf[...] = q.astype(bf16)
            do_buf[...] = do16
            s_buf[...] = jnp.where(tri2, _nt(qs16, kst_c), jnp.where(has_prev, _nt(qs16, kst_p), -jnp.inf))
            dp_buf[...] = jnp.where(tri2, _nt(do16, vst_c), jnp.where(has_prev, _nt(do16, vst_p), 0.0))
            sd_buf[...] = _nn((qs * unstack(kst_p).astype(f32)).astype(bf16), bones)
            dpd_buf[...] = jnp.where(has_prev, _nn((do * unstack(vst_p).astype(f32)).astype(bf16), bones), 0.0)

        def softmax_grad(i, par, d, nb):
            rows, has_prev = block_rows(i, d, nb)
            s_buf, dp_buf, sd_buf, dpd_buf = mm_bufs[par]
            p_buf, ds_buf, pd_buf, dsd_buf = ds_bufs[par]
            lse = l_ref[rows, :]
            dl = dl_scr[rows, :]
            pt = jnp.exp(s_buf[...] - both_heads(lse))
            ds_buf[...] = (pt * (dp_buf[...] - both_heads(dl)) * 0.125).astype(bf16)
            p_buf[...] = pt.astype(bf16)
            pd = jnp.where(has_prev, jnp.exp(sd_buf[...] - lse), 0.0)
            pd_buf[...] = pd
            dsd_buf[...] = pd * (dpd_buf[...] - dl) * 0.125

        def accumulate(i, par, d, nb):
            rows, _ = block_rows(i, d, nb)
            before, _ = block_rows(jnp.maximum(i - 1, 0), d, nb)
            p_buf, ds_buf, pd_buf, dsd_buf = ds_bufs[par]
            kc_buf, kp_buf, q_buf, do_buf = op_bufs[par]
            pt16, ds16, pd, dsd = p_buf[...], ds_buf[...], pd_buf[...], dsd_buf[...]
            zero = jnp.zeros_like(pt16)
            dsc, dsp = jnp.where(tri2, ds16, zero), jnp.where(tri2, zero, ds16)
            pc, pp = jnp.where(tri2, pt16, zero), jnp.where(tri2, zero, pt16)
            kst_c, kst_p, q16, do16 = kc_buf[...], kp_buf[...], q_buf[...], do_buf[...]
            qst, dost = _stack_heads(q16, head0), _stack_heads(do16, head0)
            dq_acc[rows, :] += _nn(dsc, kst_c) + _nn(dsp, kst_p) + dsd * unstack(kst_p).astype(f32)
            dk2 = _tn(jnp.concatenate([per_head(dsc), per_head(dsp)], axis=1), qst)
            dv2 = _tn(jnp.concatenate([per_head(pc), per_head(pp)], axis=1), dost)
            dk_acc[before, :] += carry_k[...] + dk2[BLK:] + dsd * q16.astype(f32)
            dv_acc[before, :] += carry_v[...] + dv2[BLK:] + pd * do16.astype(f32)
            carry_k[...] = dk2[:BLK]
            carry_v[...] = dv2[:BLK]

        for d in DILATIONS:
            nb = s // (BLK * d)
            carry_k[...] = jnp.zeros_like(carry_k)
            carry_v[...] = jnp.zeros_like(carry_v)
            products(0, 0, d, nb)
            products(1, 1, d, nb)
            softmax_grad(0, 0, d, nb)

            def steps(j, carry, d=d, nb=nb):
                for par in range(2):
                    t = 2 * j + 2 + par
                    accumulate(t - 2, par, d, nb)
                    products(t, par, d, nb)
                    softmax_grad(t - 1, 1 - par, d, nb)
                return carry

            lax.fori_loop(0, (n_it - 2) // 2, steps, 0)
            accumulate(n_it - 2, 0, d, nb)
            softmax_grad(n_it - 1, 1, d, nb)
            accumulate(n_it - 1, 1, d, nb)
            last, _ = block_rows(n_it - 1, d, nb)
            dk_acc[last, :] += carry_k[...]
            dv_acc[last, :] += carry_v[...]

        def epi(i, carry):
            rows = pl.ds(pl.multiple_of(i * 256, 256), 256)
            dq_ref[rows, :] = dq_acc[rows, :].astype(bf16)
            dk_ref[rows, :] = dk_acc[rows, :].astype(bf16)
            dv_ref[rows, :] = dv_acc[rows, :].astype(bf16)
            return carry

        lax.fori_loop(0, s // 256, epi, 0)

        if nsw:
            @pl.when(pl.program_id(0) == N_PAIRS - 1)
            def _():
                swap_copy.wait_send()
                swap_copy.wait_recv()

    col = lambda base: pl.BlockSpec((s, LANES), lambda h: (0, base + h))
    anyspec = pl.BlockSpec(memory_space=pl.ANY)
    swaps = [] if swap is None else [swap]
    outs = pl.pallas_call(
        body, name="attn_bwd", grid=(N_PAIRS,),
        in_specs=[col(0), col(8), col(16), col(24), col(0), col(0), col(0)] + [anyspec] * nsw,
        out_specs=[col(0)] * 4 + [anyspec] * nsw,
        out_shape=[SDS((s, D_ATTN), bf16)] * 4 + [SDS(a.shape, a.dtype) for a in swaps],
        scratch_shapes=[pltpu.VMEM((s, LANES), f32)] * 5
        + [pltpu.VMEM((BLK, 2 * LANES), f32)] * 2 + [pltpu.VMEM((BLK, LANES), f32)] * 2
        + [pltpu.VMEM((BLK, 2 * LANES), f32)] * 2 + [pltpu.VMEM((BLK, LANES), f32)] * 2
        + [pltpu.VMEM((BLK, 2 * LANES), bf16)] * 2 + [pltpu.VMEM((BLK, LANES), f32)] * 2
        + [pltpu.VMEM((BLK, 2 * LANES), bf16)] * 2 + [pltpu.VMEM((BLK, LANES), f32)] * 2
        + [pltpu.VMEM((2 * BLK, LANES), bf16)] * 2 + [pltpu.VMEM((BLK, LANES), bf16)] * 2
        + [pltpu.VMEM((2 * BLK, LANES), bf16)] * 2 + [pltpu.VMEM((BLK, LANES), bf16)] * 2
        + [pltpu.VMEM((2 * BLK, LANES), bf16)] * 2 + [pltpu.VMEM((BLK, LANES), f32)] * 2
        + [pltpu.SemaphoreType.DMA(())] * (2 * nsw),
        compiler_params=pltpu.CompilerParams(dimension_semantics=("arbitrary",)),
    )(proj, proj, proj, proj, o, lb, dmix, *swaps)
    return outs


def _ssd_bwd(proj, y, states, cv, dmix, conv_w, conv_b, dtb16, alog16, alog_f, d_f, nw, chip_sums=()):
    s = proj.shape[0]
    nc = s // CHUNK
    gw = D_SSM // N_GROUPS
    nx = len(chip_sums)

    def body(*refs):
        (xs_ref, bc_ref, _, _, dt_ref, z_ref, y_ref, st_ref, dm_ref, cw_ref, cb_ref, dtb_ref,
         alog16_ref, alogf_ref, df_ref, nw_ref, cv_ref) = refs[:17]
        cs_in = refs[17:17 + nx]
        out_ref, gconv_ref, gvec_ref, gdt_ref = refs[17 + nx:21 + nx]
        cs_out = refs[21 + nx:21 + 2 * nx]
        (dh_scr, head_scr, dcpad, da_scr, dxdt_scr, dbc_scr, emat_ref, fold_ref) = refs[21 + 2 * nx:29 + 2 * nx]
        cs_sems = refs[29 + 2 * nx:]
        i = pl.program_id(0)
        c = nc - 1 - i

        if nx:
            @pl.when(i == 0)
            def _():
                mine, sends, _ = _chip_exchange_copies(cs_in, cs_out, *cs_sems)
                for cp in mine + sends:
                    cp.start()

            @pl.when(i == nc - 1)
            def _():
                mine, sends, recvs = _chip_exchange_copies(cs_in, cs_out, *cs_sems)
                for cp in recvs:
                    cp.wait_recv()
                for cp in sends:
                    cp.wait_send()
                for cp in mine:
                    cp.wait()

        @pl.when(i == 0)
        def _():
            emat_ref[...] = _expand_mat()
            fold_ref[...] = _fold_mat()
            dh_scr[...] = jnp.zeros_like(dh_scr)
            head_scr[...] = jnp.zeros_like(head_scr)
            gconv_ref[...] = jnp.zeros_like(gconv_ref)
            gvec_ref[...] = jnp.zeros_like(gvec_ref)
            gdt_ref[...] = jnp.zeros_like(gdt_ref)

        cv = cv_ref[...]
        sig = _sigmoid(cv)
        xbc = cv * sig
        pre, dt_f, al_f, al_x, al_t = _decay_terms(dt_ref, dtb_ref, alog16_ref, emat_ref)
        head0 = _iota((CHUNK, LANES), 1) < HEAD_DIM
        sub = _iota((CHUNK, LANES), 0)
        last_row = sub == CHUNK - 1

        yv, z, dmx = y_ref[...], z_ref[...], dm_ref[...]
        sz = _sigmoid(z)
        silu = z * sz
        yz = yv * silu
        dyz_parts = []
        for g in range(N_GROUPS):
            gs = slice(g * gw, (g + 1) * gw)
            part = yz[:, gs]
            r = lax.rsqrt(jnp.mean(part * part, axis=-1, keepdims=True) + EPS)
            nh = part * r
            gvec_ref[0:1, gs] += jnp.sum(dmx[:, gs] * nh, axis=0, keepdims=True)
            dn = dmx[:, gs] * nw_ref[:, gs]
            dyz_parts.append(r * (dn - nh * jnp.mean(dn * nh, axis=-1, keepdims=True)))
        dyz = jnp.concatenate(dyz_parts, axis=1)
        dy = dyz * silu
        out_ref[:, 0:D_SSM] = (dyz * yv * (sz * (1.0 + z * (1.0 - sz)))).astype(bf16)

        x_all = xbc[:, 0:D_SSM]
        gvec_ref[2:3, :] += jnp.sum(dy * x_all, axis=0, keepdims=True)

        for g in range(N_GROUPS):
            bm = xbc[:, D_SSM + g * D_STATE: D_SSM + (g + 1) * D_STATE].astype(bf16)
            cm = xbc[:, D_SSM + (N_GROUPS + g) * D_STATE: D_SSM + (N_GROUPS + g + 1) * D_STATE].astype(bf16)
            gmat = _nt(cm, bm)
            dgm = jnp.zeros((CHUNK, CHUNK), f32)
            db = jnp.zeros((CHUNK, D_STATE), f32)
            dc = jnp.zeros((CHUNK, D_STATE), f32)
            for pair in range(4 * g, 4 * g + 4):
                sl = slice(pair * LANES, (pair + 1) * LANES)
                xp, dtp, alp, dyp = x_all[:, sl], dt_f[:, sl], al_f[:, sl], dy[:, sl]
                xdt = xp * dtp
                xdt16 = xdt.astype(bf16)
                al_last = alp[CHUNK - 1:CHUNK, :]
                e_l = jnp.exp(alp)
                wf = jnp.exp(al_last - alp)
                e_last = jnp.exp(al_last)
                hp = st_ref[:, sl]
                hp16 = hp.astype(bf16)
                dhn = dh_scr[:, sl]
                dhn16 = dhn.astype(bf16)
                y_off = e_l * _nn(cm, hp16)
                dch16 = (dyp * e_l).astype(bf16)
                dc = dc + _nt(dch16, hp16)
                dh_out = _tn(cm, dch16)
                dal = dyp * y_off
                xw16 = (wf * xdt).astype(bf16)
                db = db + _nt(xw16, dhn16)
                dxw = _nn(bm, dhn16)
                dxdt = dxw * wf
                dwf = dxw * xdt * wf
                dal = dal - dwf
                dal_last = jnp.sum(dwf, axis=0, keepdims=True) + jnp.sum(dhn * hp, axis=0, keepdims=True) * e_last
                dh_scr[:, sl] = e_last * dhn + dh_out
                for h in range(2):
                    mh = head0 if h == 0 else jnp.logical_not(head0)
                    dyh16 = jnp.where(mh, dyp, 0.0).astype(bf16)
                    lmat = _decay_mat(al_x, al_t, pair, h)
                    mm = gmat * lmat
                    dmm = _nt(dyh16, xdt16)
                    dxdt = dxdt + _tn(mm.astype(bf16), dyh16)
                    n16 = (dmm * mm).astype(bf16)
                    jh = jnp.where(mh, 1.0 / HEAD_DIM, 0.0).astype(bf16)
                    dal = dal + _nn(n16, jh) - _tn(n16, jh)
                    dgm = dgm + dmm * lmat
                da_scr[:, sl] = dal + jnp.where(last_row, dal_last, 0.0)
                dxdt_scr[:, sl] = dxdt
            dgm16 = dgm.astype(bf16)
            dbc_scr[:, g * D_STATE:(g + 1) * D_STATE] = db + _tn(dgm16, cm)
            dbc_scr[:, (N_GROUPS + g) * D_STATE:(N_GROUPS + g + 1) * D_STATE] = dc + _nn(dgm16, bm)

        sub_c, lane_c = _iota((CHUNK, CHUNK), 0), _iota((CHUNK, CHUNK), 1)
        tri_t = (lane_c >= sub_c).astype(bf16)
        dadt = _dot_01_left(tri_t, da_scr[...], 2)
        a_f = -jnp.exp(alogf_ref[...])
        dxdt_all = dxdt_scr[...]
        ddt_f = dxdt_all * x_all + a_f * dadt
        gvec_ref[1:2, :] += jnp.sum(dt_f * dadt, axis=0, keepdims=True) * a_f
        dx = df_ref[...] * dy + dxdt_all * dt_f
        ddt_raw = _dot_01(ddt_f, fold_ref[...], 2) * _sigmoid(pre)
        gdt_ref[0:1, :] += jnp.sum(ddt_raw, axis=0, keepdims=True)
        out_ref[:, D_SSM + D_CONV:D_SSM + D_CONV + LANES] = ddt_raw.astype(bf16)
        out_ref[:, D_SSM + D_CONV + LANES:] = jnp.zeros((CHUNK, 3 * LANES), bf16)

        dsil = sig * (1.0 + cv * (1.0 - sig))
        dcv_x = dx * dsil[:, 0:D_SSM]
        dcv_bc = dbc_scr[...] * dsil[:, D_SSM:]
        dcpad[0:CHUNK, 0:D_SSM] = dcv_x
        dcpad[0:CHUNK, D_SSM:] = dcv_bc
        dcpad[CHUNK:, :] = head_scr[...]
        dcp = dcpad[...]
        dcv = dcp[0:CHUNK]
        gconv_ref[4:5, :] += jnp.sum(dcv, axis=0, keepdims=True)
        x_raw = jnp.concatenate([xs_ref[...], bc_ref[...]], axis=1)
        draw = cw_ref[3:4, :] * dcv
        gconv_ref[3:4, :] += jnp.sum(dcv * x_raw, axis=0, keepdims=True)
        for j in range(3):
            ahead = pltpu.roll(dcp, CHUNK + 8 - (3 - j), 0)[0:CHUNK]
            draw = draw + cw_ref[j:j + 1, :] * ahead
            gconv_ref[j:j + 1, :] += jnp.sum(ahead * x_raw, axis=0, keepdims=True)
        head_scr[...] = dcv[0:8]
        out_ref[:, D_SSM:D_SSM + D_CONV] = draw.astype(bf16)

    order = lambda i: nc - 1 - i
    row = lambda w, cb=0: pl.BlockSpec((CHUNK, w), lambda i: (nc - 1 - i, cb))
    anyspec = pl.BlockSpec(memory_space=pl.ANY)
    outs = pl.pallas_call(
        body, name="ssd_bwd", grid=(nc,),
        in_specs=_ssd_in_specs(order) + [row(D_SSM), pl.BlockSpec((None, D_STATE, D_SSM), lambda i: (nc - 1 - i, 0, 0)),
                                         row(D_SSM, 1), _full((4, D_CONV)), _full((1, D_CONV)), _full((1, LANES)),
                                         _full((1, LANES)), _full((1, D_SSM)), _full((1, D_SSM)), _full((1, D_SSM)),
                                         row(D_CONV)]
        + [anyspec] * nx,
        out_specs=[row(3072), _full((8, D_CONV)), _full((8, D_SSM)), _full((8, LANES))] + [anyspec] * nx,
        out_shape=[SDS((s, 3072), bf16), SDS((8, D_CONV), f32), SDS((8, D_SSM), f32), SDS((8, LANES), f32)]
        + [SDS(a.shape, a.dtype) for a in chip_sums],
        scratch_shapes=[pltpu.VMEM((D_STATE, D_SSM), f32), pltpu.VMEM((8, D_CONV), f32),
                        pltpu.VMEM((8 + CHUNK, D_CONV), f32),
                        pltpu.VMEM((CHUNK, D_SSM), f32), pltpu.VMEM((CHUNK, D_SSM), f32),
                        pltpu.VMEM((CHUNK, 2 * N_GROUPS * D_STATE), f32),
                        pltpu.VMEM((LANES, 2 * D_SSM), bf16), pltpu.VMEM((D_SSM, LANES), bf16)]
        + (_chip_exchange_scratch(nx) if nx else []),
        compiler_params=pltpu.CompilerParams(dimension_semantics=("arbitrary",)),
    )(proj, proj, proj, proj, proj, proj, y, states, dmix, conv_w, conv_b, dtb16, alog16, alog_f, d_f, nw, cv,
      *chip_sums)
    return outs[0], outs[1], outs[2], outs[3], outs[4:]


def _col_blocks(parts, tile):
    counts = [p.shape[1] // tile for p in parts]
    offs = [sum(counts[:t]) for t in range(len(parts))]
    return offs, counts, sum(counts)


def _bcast_copies(src_ref, out_ref, send_sems, recv_sems, local_sem):
    x, y, c = _my_pos()
    me = 4 * x + 2 * y + c
    mine = pltpu.make_async_copy(src_ref, out_ref.at[me], local_sem)
    sends, recvs = [], []
    for k in range(1, N_DEV):
        to, frm = (me + k) % N_DEV, (me + N_DEV - k) % N_DEV
        sems = dict(send_sem=send_sems.at[k - 1], recv_sem=recv_sems.at[k - 1], device_id_type=MESH)
        sends.append(pltpu.make_async_remote_copy(
            src_ref=src_ref, dst_ref=out_ref.at[me], device_id=(to // 4, (to // 2) % 2, to % 2), **sems))
        recvs.append(pltpu.make_async_remote_copy(
            src_ref=src_ref, dst_ref=out_ref.at[frm], device_id=(x, y, c), **sems))
    return mine, sends, recvs


def _bcast_scratch():
    return [pltpu.SemaphoreType.DMA((N_DEV - 1,)), pltpu.SemaphoreType.DMA((N_DEV - 1,)), pltpu.SemaphoreType.DMA(())]


def _inproj_bwd(dparts, wt, x, nw, dres, chip_sums=(), pack=None):
    s, d = x.shape
    tm, tk = 1024, 1024
    offs, counts, nk = _col_blocks(dparts, tk)
    npart, nx = len(dparts), len(chip_sums)
    npk = 0 if pack is None else 1
    ni = s // tm

    def body(*refs):
        dp_refs = refs[:npart]
        w_ref, x_ref, nw_ref, dres_ref = refs[npart:npart + 4]
        pos = npart + 4
        cs_in, pos = refs[pos:pos + nx], pos + nx
        pack_in, pos = refs[pos:pos + npk], pos + npk
        (gx_ref, gnw_ref), pos = refs[pos:pos + 2], pos + 2
        cs_out, pos = refs[pos:pos + nx], pos + nx
        pack_out, pos = refs[pos:pos + 2 * npk], pos + 2 * npk
        acc, pos = refs[pos], pos + 1
        cs_sems, pos = refs[pos:pos + 3 * min(nx, 1)], pos + 3 * min(nx, 1)
        pk_refs = refs[pos:]
        i, k = pl.program_id(0), pl.program_id(1)

        def exchange():
            return _chip_exchange_copies(cs_in, cs_out, *cs_sems)

        def pack_copies():
            return _bcast_copies(pack_in[0], pack_out[0], *pk_refs[1:4])

        def gnw_copies():
            return _bcast_copies(pk_refs[0], pack_out[1], *pk_refs[4:7])

        @pl.when(jnp.logical_and(i == 0, k == 0))
        def _():
            gnw_ref[...] = jnp.zeros_like(gnw_ref)
            if nx:
                mine, sends, _ = exchange()
                for cp in mine + sends:
                    cp.start()
            if npk:
                mine, sends, _ = pack_copies()
                for cp in [mine] + sends:
                    cp.start()

        @pl.when(k == 0)
        def _():
            acc[...] = _nn(dp_refs[0][...], w_ref[...])

        for t in range(npart):
            @pl.when(jnp.logical_and(k >= max(offs[t], 1), k < offs[t] + counts[t]))
            def _(t=t):
                acc[...] += _nn(dp_refs[t][...], w_ref[...])

        @pl.when(k == nk - 1)
        def _():
            xv = x_ref[...]
            r = lax.rsqrt(jnp.mean(xv * xv, axis=-1, keepdims=True) + EPS)
            xn = xv * r
            du = acc[...]
            gnw_ref[0:1, :] += jnp.sum(du * xn, axis=0, keepdims=True)
            dn = du * nw_ref[...]
            gx_ref[...] = dres_ref[...] + r * (dn - xn * jnp.mean(dn * xn, axis=-1, keepdims=True))

        @pl.when(jnp.logical_and(i == ni - 1, k == nk - 1))
        def _():
            if npk:
                pk_refs[0][...] = gnw_ref[...]
                mine, sends, _ = gnw_copies()
                for cp in [mine] + sends:
                    cp.start()
            if nx:
                mine, sends, recvs = exchange()
                for cp in recvs:
                    cp.wait_recv()
                for cp in sends:
                    cp.wait_send()
                for cp in mine:
                    cp.wait()
            if npk:
                for copies in (pack_copies(), gnw_copies()):
                    mine, sends, recvs = copies
                    for cp in recvs:
                        cp.wait_recv()
                    for cp in sends:
                        cp.wait_send()
                    mine.wait()

    def piece(t):
        return pl.BlockSpec((tm, tk), lambda i, k: (i, jnp.clip(k - offs[t], 0, counts[t] - 1)))

    anyspec = pl.BlockSpec(memory_space=pl.ANY)
    packs = [] if pack is None else [pack]
    pack_shapes = [] if pack is None else [SDS((N_DEV,) + pack.shape, f32), SDS((N_DEV, 8, d), f32)]
    scratch = [pltpu.VMEM((tm, d), f32)] + (_chip_exchange_scratch(nx) if nx else [])
    if npk:
        scratch += [pltpu.VMEM((8, d), f32)] + _bcast_scratch() + _bcast_scratch()
    outs = pl.pallas_call(
        body, name="inproj_bwd", grid=(ni, nk),
        in_specs=[piece(t) for t in range(npart)] + [
            pl.BlockSpec((tk, d), lambda i, k: (k, 0)),
            pl.BlockSpec((tm, d), lambda i, k: (i, 0)), pl.BlockSpec((1, d), lambda i, k: (0, 0)),
            pl.BlockSpec((tm, d), lambda i, k: (i, 0))] + [anyspec] * (nx + npk),
        out_specs=[pl.BlockSpec((tm, d), lambda i, k: (i, 0)), pl.BlockSpec((8, d), lambda i, k: (0, 0))]
        + [anyspec] * (nx + 2 * npk),
        out_shape=[SDS((s, d), f32), SDS((8, d), f32)] + [SDS(a.shape, a.dtype) for a in chip_sums] + pack_shapes,
        scratch_shapes=scratch,
        compiler_params=pltpu.CompilerParams(dimension_semantics=("arbitrary", "arbitrary")),
    )(*dparts, wt, x, nw, dres, *chip_sums, *packs)
    return outs[0], outs[1], outs[2:2 + nx], outs[2 + nx:]


def _matmul_tn(a_parts, b_parts, name):
    tile, tk = 1024, 1024
    s = a_parts[0].shape[0]
    nk = s // tk
    na, nb = len(a_parts), len(b_parts)
    offs_a, counts_a, ni = _col_blocks(a_parts, tile)
    offs_b, counts_b, nj = _col_blocks(b_parts, tile)

    def body(*refs):
        a_refs, b_refs, o_ref = refs[:na], refs[na:na + nb], refs[na + nb]
        i, j = pl.program_id(0), pl.program_id(1)

        @pl.when(pl.program_id(2) == 0)
        def _():
            o_ref[...] = jnp.zeros_like(o_ref)

        for ta in range(na):
            for tb in range(nb):
                in_a = jnp.logical_and(i >= offs_a[ta], i < offs_a[ta] + counts_a[ta])
                in_b = jnp.logical_and(j >= offs_b[tb], j < offs_b[tb] + counts_b[tb])

                @pl.when(jnp.logical_and(in_a, in_b))
                def _(ta=ta, tb=tb):
                    o_ref[...] += _tn(a_refs[ta][...], b_refs[tb][...])

    def spec(offs, counts, t, axis):
        def index(i, j, k):
            pos = (i, j)[axis]
            mine = jnp.logical_and(pos >= offs[t], pos < offs[t] + counts[t])
            return jnp.where(mine, k, 0), jnp.clip(pos - offs[t], 0, counts[t] - 1)
        return pl.BlockSpec((tk, tile), index)

    return pl.pallas_call(
        body, name=name, grid=(ni, nj, nk),
        in_specs=[spec(offs_a, counts_a, t, 0) for t in range(na)] + [spec(offs_b, counts_b, t, 1) for t in range(nb)],
        out_specs=pl.BlockSpec((tile, tile), lambda i, j, k: (i, j)),
        out_shape=SDS((ni * tile, nj * tile), f32),
        compiler_params=pltpu.CompilerParams(dimension_semantics=("parallel", "parallel", "arbitrary")),
    )(*a_parts, *b_parts)


def _adamw(w, g, m, v):
    m = ADAM_B1 * m + (1.0 - ADAM_B1) * g
    v = ADAM_B2 * v + (1.0 - ADAM_B2) * (g * g)
    m_hat = m / (1.0 - ADAM_B1 ** ADAM_STEP)
    v_hat = v / (1.0 - ADAM_B2 ** ADAM_STEP)
    delta = -ADAM_LR * (m_hat / (jnp.sqrt(v_hat) + ADAM_EPS) + ADAM_WD * w)
    return delta, m, v


def _sum_adamw(parts, w, m, v, name):
    r, c = w.shape
    tc = 256

    def body(p_ref, w_ref, m_ref, v_ref, g_ref, d_ref, nm_ref, nv_ref):
        g = p_ref[0].astype(f32)
        for q in range(1, 4):
            g = g + p_ref[q].astype(f32)
        g_ref[...] = g
        d_ref[...], nm_ref[...], nv_ref[...] = _adamw(w_ref[...], g, m_ref[...], v_ref[...])

    blk = pl.BlockSpec((r, tc), lambda i: (0, i))
    return pl.pallas_call(
        body, name=name, grid=(c // tc,),
        in_specs=[pl.BlockSpec((4, r, tc), lambda i: (0, 0, i)), blk, blk, blk],
        out_specs=[blk] * 4, out_shape=[SDS((r, c), f32)] * 4,
        compiler_params=pltpu.CompilerParams(dimension_semantics=("parallel",)),
    )(parts, w, m, v)


def _sum_small(parts, pre_blocks):
    def body(p_ref, b_ref, o_ref):
        t = p_ref[0]
        pre = b_ref[0]
        for j in range(1, N_DEV):
            t = t + p_ref[j]
            pre = pre + b_ref[j]
        o_ref[...] = t
        o_ref[5:6, 0:D_MODEL] = pre[0:1, :]
        row_h = _iota((D_SSM, LANES), 0) // HEAD_DIM
        fold = (row_h == _iota((D_SSM, LANES), 1)).astype(f32)
        lower = t[8:16, 0:LANES]
        folded = _nn_hi(t[8:16, 0:D_SSM], fold)
        loss = jnp.sum(t[11:12, 0:D_MODEL], axis=1, keepdims=True) * (0.5 / D_MODEL)
        row = _iota((8, LANES), 0)
        o_ref[8:16, 0:LANES] = jnp.where(row < 2, folded, jnp.where(row == 4, loss, lower))

    return pl.pallas_call(body, name="sum_small", out_shape=SDS((PACK_ROWS, PACK_W), f32),
                          in_specs=[pl.BlockSpec(memory_space=pltpu.VMEM)] * 2,
                          out_specs=pl.BlockSpec(memory_space=pltpu.VMEM))(parts, pre_blocks)


def _adamw_small(w, g, m, v):
    def body(w_ref, g_ref, m_ref, v_ref, d_ref, nm_ref, nv_ref):
        d_ref[...], nm_ref[...], nv_ref[...] = _adamw(w_ref[...], g_ref[...], m_ref[...], v_ref[...])

    vm = pl.BlockSpec(memory_space=pltpu.VMEM)
    return pl.pallas_call(body, name="adamw_small", out_shape=[SDS(w.shape, f32)] * 3,
                          in_specs=[vm] * 4, out_specs=[vm] * 3)(w, g, m, v)


def _pad_lanes(v, width):
    return jnp.pad(v, ((0, 0), (0, width - v.shape[1])))


def _local_step(x, tgt, norm_pre_w, wt, conv_w, conv_b, dt_bias, a_log, d_skip, ssm_norm_w, wo, norm_post_w, sharded):
    dtb16 = _pad_lanes(dt_bias, LANES)
    alog16 = _pad_lanes(a_log, LANES)
    alog_f = jnp.repeat(a_log, HEAD_DIM, axis=1)
    d_f = jnp.repeat(d_skip, HEAD_DIM, axis=1)

    shard_out = wo.shape[0]
    if sharded:
        proj, u, (g_out, g_cw) = _prenorm_inproj(x, norm_pre_w, wt, gather=(wo, conv_w))
        wo = g_out.reshape(N_DEV * shard_out, D_MODEL)
        conv_w = g_cw.transpose(1, 0, 2).reshape(4, D_CONV)
    else:
        proj, u, _ = _prenorm_inproj(x, norm_pre_w, wt)
    o, lb, mix_a = _attn_fwd(proj)
    mix_s, y, states, cv = _ssd_fwd(proj, conv_w, conv_b, dtb16, alog16, alog_f, d_f, ssm_norm_w)
    dmix, dout, dres, acc_post = _outproj_loss(mix_a, mix_s, wo, x, tgt, norm_post_w)
    dw_out = _matmul_tn([mix_a, mix_s], [dout], "dw_out")
    ssd_args = (proj, y, states, cv, dmix, conv_w, conv_b, dtb16, alog16, alog_f, d_f, ssm_norm_w)
    if sharded:
        dq, dk, dv, dg, got_out = _attn_bwd(proj, o, lb, dmix, swap=dw_out)
        chip_out = _chip_sum(dw_out, got_out, shard_out, "chip_sum_w_out")
        dzxd, g_conv, g_vec, g_dt, (parts_out,) = _ssd_bwd(*ssd_args, chip_sums=[chip_out])
    else:
        dq, dk, dv, dg = _attn_bwd(proj, o, lb, dmix)
        dzxd, g_conv, g_vec, g_dt, _ = _ssd_bwd(*ssd_args)
    dparts = [dq, dk, dv, dg, dzxd]

    def pack(g_pre_row):
        return jnp.concatenate(
            [g_conv[0:5], g_pre_row, _pad_lanes(g_vec[0:1], PACK_W), _pad_lanes(acc_post[1:2], PACK_W),
             _pad_lanes(g_vec[1:3], PACK_W), _pad_lanes(g_dt[0:1], PACK_W), _pad_lanes(acc_post[0:1], PACK_W),
             jnp.zeros((4, PACK_W), f32)], axis=0)

    if sharded:
        dw_in, got_in = _dw_in_swap(dparts, u)
        chip_in = _chip_sum(dw_in, got_in, D_IN_PROJ // N_DEV, "chip_sum_w_in")
        grad_x, _, (parts_in,), small = _inproj_bwd(dparts, wt, x, norm_pre_w, dres, [chip_in],
                                                    pack(jnp.zeros((1, PACK_W), f32)))
        return grad_x, (parts_in, parts_out), small
    dw_in = _matmul_tn(dparts, [u], "dw_in")
    grad_x, g_pre, _, _ = _inproj_bwd(dparts, wt, x, norm_pre_w, dres)
    return grad_x, (dw_in, dw_out), pack(_pad_lanes(g_pre[0:1], PACK_W))


def kernel(x, norm_pre_w, w_in, conv_w, conv_b, dt_bias, a_log, d_skip, ssm_norm_w, w_out, norm_post_w, loss_target, m_norm_pre_w, m_w_in, m_conv_w, m_conv_b, m_dt_bias, m_a_log, m_d_skip, m_ssm_norm_w, m_w_out, m_norm_post_w, v_norm_pre_w, v_w_in, v_conv_w, v_conv_b, v_dt_bias, v_a_log, v_d_skip, v_ssm_norm_w, v_w_out, v_norm_post_w):
    shard_in = w_in.shape[2]
    shard_cv = conv_w.shape[2]
    me = 4 * lax.axis_index("x") + 2 * lax.axis_index("y") + lax.axis_index("c")

    g_in, = _all_gather([w_in[0].T.astype(bf16)])
    wt = _assemble_wt(g_in)

    grad_x, (parts_in, parts_out), (parts_small, pre_blocks) = _local_step(
        x[0], loss_target[0], norm_pre_w, wt, conv_w[0], conv_b, dt_bias, a_log, d_skip, ssm_norm_w,
        w_out[0].astype(bf16), norm_post_w, sharded=True)

    g_w_in, d_w_in, nm_w_in, nv_w_in = (a.T for a in _sum_adamw(
        parts_in, w_in[0].T, m_w_in[0].T, v_w_in[0].T, "sum_adamw_w_in"))
    g_w_out, d_w_out, nm_w_out, nv_w_out = _sum_adamw(parts_out, w_out[0], m_w_out[0], v_w_out[0], "sum_adamw_w_out")
    tot = _sum_small(parts_small, pre_blocks)

    g_cw_all = tot[0:4]
    small_g = {
        "conv_w": lax.dynamic_slice(g_cw_all, (0, me * shard_cv), (4, shard_cv)),
        "conv_b": tot[4:5], "norm_pre_w": tot[5:6, :D_MODEL], "ssm_norm_w": tot[6:7, :D_SSM],
        "norm_post_w": tot[7:8, :D_MODEL], "a_log": tot[8:9, :16], "d_skip": tot[9:10, :16], "dt_bias": tot[10:11, :16],
    }
    loss = tot[12, 0]
    small_w = {"conv_w": (conv_w[0], m_conv_w[0], v_conv_w[0]), "conv_b": (conv_b, m_conv_b, v_conv_b),
               "norm_pre_w": (norm_pre_w, m_norm_pre_w, v_norm_pre_w), "ssm_norm_w": (ssm_norm_w, m_ssm_norm_w, v_ssm_norm_w),
               "norm_post_w": (norm_post_w, m_norm_post_w, v_norm_post_w), "a_log": (a_log, m_a_log, v_a_log),
               "d_skip": (d_skip, m_d_skip, v_d_skip), "dt_bias": (dt_bias, m_dt_bias, v_dt_bias)}
    names = list(small_w)
    sizes = [small_g[k].size for k in names]
    tot_size = sum(sizes)
    pad_to = -(-tot_size // 1024) * 1024

    def flat(arrs):
        v = jnp.concatenate([a.reshape(-1) for a in arrs])
        return jnp.pad(v, (0, pad_to - tot_size)).reshape(pad_to // LANES, LANES)

    fw = flat([small_w[k][0] for k in names])
    fg = flat([small_g[k] for k in names])
    fm = flat([small_w[k][1] for k in names])
    fv = jnp.pad(jnp.concatenate([small_w[k][2].reshape(-1) for k in names]), (0, pad_to - tot_size),
                 constant_values=1.0).reshape(pad_to // LANES, LANES)
    fd, fnm, fnv = _adamw_small(fw, fg, fm, fv)

    def unflat(f):
        out, off = {}, 0
        v = f.reshape(-1)
        for k, n in zip(names, sizes):
            out[k] = v[off:off + n].reshape(small_g[k].shape)
            off += n
        return out

    sd, snm, snv = unflat(fd), unflat(fnm), unflat(fnv)
    lead = lambda a: a[None]
    order = ["norm_pre_w", "w_in", "conv_w", "conv_b", "dt_bias", "a_log", "d_skip", "ssm_norm_w", "w_out", "norm_post_w"]
    grads = dict(small_g, w_in=g_w_in, w_out=g_w_out)
    deltas = dict(sd, w_in=d_w_in, w_out=d_w_out)
    new_m = dict(snm, w_in=nm_w_in, w_out=nm_w_out)
    new_v = dict(snv, w_in=nv_w_in, w_out=nv_w_out)

    def shaped(dct, k):
        a = dct[k]
        return lead(a) if k in ("w_in", "w_out", "conv_w") else a

    return (loss, grad_x[None], *[shaped(grads, k) for k in order], *[shaped(deltas, k) for k in order],
            *[shaped(new_m, k) for k in order], *[shaped(new_v, k) for k in order])
```

```python
import jax
import jax.numpy as jnp
from jax import lax
from jax.experimental import pallas as pl
from jax.experimental.pallas import tpu as pltpu

f32, bf16 = jnp.float32, jnp.bfloat16
SDS = jax.ShapeDtypeStruct
HIGHEST = lax.Precision.HIGHEST
MESH = pl.DeviceIdType.MESH

N_DEV = 8
D_MODEL = 1024
D_ATTN = 1024
D_SSM = 1024
HEAD_DIM = 64
N_PAIRS = 8
D_STATE = 128
N_GROUPS = 2
D_CONV = D_SSM + 2 * N_GROUPS * D_STATE
D_IN_PROJ = 4 * D_ATTN + D_SSM + D_CONV + 16
NP = 7168
CHUNK = 128
BLK = 128
DILATIONS = (1, 4, 16)
EPS = 1e-6
LANES = 128
COL_Z, COL_XS, COL_BC, COL_DT = 4096, 5120, 6144, 6656

ADAM_LR, ADAM_B1, ADAM_B2, ADAM_EPS, ADAM_WD, ADAM_STEP = 0.001, 0.9, 0.999, 1e-08, 0.01, 10

PACK_ROWS, PACK_W = 16, 1536


def _nt(a, b):
    return lax.dot_general(a, b, (((1,), (1,)), ((), ())), preferred_element_type=f32)


def _tn(a, b):
    return lax.dot_general(a, b, (((0,), (0,)), ((), ())), preferred_element_type=f32)


def _nn(a, b):
    return jnp.dot(a, b, preferred_element_type=f32)


def _nn_hi(a, b):
    return jnp.dot(a, b, precision=HIGHEST, preferred_element_type=f32)


def _sigmoid(x):
    return 1.0 / (1.0 + jnp.exp(-x))


def _softplus(x):
    return jnp.maximum(x, 0.0) + jnp.log1p(jnp.exp(-jnp.abs(x)))


def _iota(shape, dim):
    return lax.broadcasted_iota(jnp.int32, shape, dim)


def _my_pos():
    return lax.axis_index("x"), lax.axis_index("y"), lax.axis_index("c")


GATHER_SEMS = 9


def _gather_phases(ins, outs, send_sems, recv_sems, local_sems):
    n, ns = len(ins), GATHER_SEMS
    x, y, c = _my_pos()
    me, sibling = (x, y, c), (x, y, 1 - c)
    xn, yn, diag = (1 - x, y), (x, 1 - y), (1 - x, 1 - y)

    def slot(a, px, py, pc):
        return outs[a].at[4 * px + 2 * py + pc]

    def part(a, ref, h):
        width = ins[a].shape[-1]
        if width % (2 * LANES):
            return ref if h == 1 else None
        return ref.at[:, pl.ds(h * (width // 2), width // 2)]

    def copy(a, k, block, to, src=None, h=None):
        src_ref = slot(a, *block) if src is None else src
        dst_ref = slot(a, *block)
        if h is not None:
            src_ref, dst_ref = part(a, src_ref, h), part(a, dst_ref, h)
            if src_ref is None:
                return None
        return pltpu.make_async_remote_copy(
            src_ref=src_ref, dst_ref=dst_ref, send_sem=send_sems.at[ns * a + k], recv_sem=recv_sems.at[ns * a + k],
            device_id=to, device_id_type=MESH)

    def mine():
        return [pltpu.make_async_copy(ins[a], slot(a, *me), local_sems.at[a]) for a in range(n)]

    def own_sends(a):
        return [copy(a, 0, me, sibling, src=ins[a]), copy(a, 1, me, (*xn, c), src=ins[a]),
                copy(a, 2, me, (*yn, c), src=ins[a])]

    def neighbour_relays(a):
        return [copy(a, 4, (*xn, c), sibling), copy(a, 7, (*xn, c), (*yn, c), h=1),
                copy(a, 5, (*yn, c), sibling), copy(a, 8, (*yn, c), (*xn, c), h=0)]

    def diagonal_halves(a):
        return [copy(a, k, (*diag, c), me, h=h) for k, h in ((8, 0), (7, 1))]

    def start_all(cps):
        for cp in cps:
            if cp is not None:
                cp.start()

    def phase0():
        start_all(mine())
        for a in range(n):
            start_all(own_sends(a))

    def phase1():
        for a in range(n):
            copy(a, 1, (*xn, c), me).wait_recv()
            copy(a, 2, (*yn, c), me).wait_recv()
            start_all(neighbour_relays(a))

    def phase2():
        for a in range(n):
            for cp in diagonal_halves(a):
                if cp is not None:
                    cp.wait_recv()
            copy(a, 6, (*diag, c), sibling).start()

    def finish():
        for a in range(n):
            copy(a, 0, sibling, me).wait_recv()
            for j, chip in enumerate((xn, yn, diag)):
                copy(a, 4 + j, (*chip, 1 - c), me).wait_recv()
        for a in range(n):
            for cp in own_sends(a) + neighbour_relays(a) + [copy(a, 6, (*diag, c), sibling)]:
                if cp is not None:
                    cp.wait_send()
        for cp in mine():
            cp.wait()

    return phase0, phase1, phase2, finish


def _gather_scratch(n):
    return [pltpu.SemaphoreType.DMA((GATHER_SEMS * n,)), pltpu.SemaphoreType.DMA((GATHER_SEMS * n,)),
            pltpu.SemaphoreType.DMA((n,))]


def _all_gather(arrs):
    n = len(arrs)

    def body(*refs):
        for phase in _gather_phases(refs[:n], refs[n:2 * n], *refs[2 * n:]):
            phase()

    anyspec = pl.BlockSpec(memory_space=pl.ANY)
    return pl.pallas_call(
        body, name="weights_all_gather",
        out_shape=[SDS((N_DEV,) + a.shape, a.dtype) for a in arrs],
        in_specs=[anyspec] * n, out_specs=[anyspec] * n, scratch_shapes=_gather_scratch(n),
    )(*arrs)


def _dw_in_swap(a_parts, u):
    tile, tk = 1024, 1024
    s = u.shape[0]
    nk = s // tk
    na = len(a_parts)
    offs, counts, ni = _col_blocks(a_parts, tile)

    def body(*refs):
        a_refs, u_ref = refs[:na], refs[na]
        dw_ref, got_ref = refs[na + 1:na + 3]
        acc, stage, local_sems, send_sems, recv_sem = refs[na + 3:]
        i, k = pl.program_id(0), pl.program_id(1)
        x, y, c = _my_pos()
        par = i % 2

        def tile_copies(t, p):
            rows = pl.ds(pl.multiple_of(t * tile, tile), tile)
            loc = pltpu.make_async_copy(stage.at[p], dw_ref.at[rows], local_sems.at[p])
            rem = pltpu.make_async_remote_copy(
                src_ref=stage.at[p], dst_ref=got_ref.at[rows], send_sem=send_sems.at[p], recv_sem=recv_sem,
                device_id=(x, y, 1 - c), device_id_type=MESH)
            return loc, rem

        @pl.when(k == 0)
        def _():
            acc[...] = jnp.zeros((tile, tile), f32)

        for t in range(na):
            @pl.when(jnp.logical_and(i >= offs[t], i < offs[t] + counts[t]))
            def _(t=t):
                acc[...] += _tn(a_refs[t][...], u_ref[pl.ds(pl.multiple_of(k * tk, tk), tk), :])

        @pl.when(k == nk - 1)
        def _():
            @pl.when(i >= 2)
            def _():
                loc, rem = tile_copies(i - 2, par)
                loc.wait()
                rem.wait_send()
            stage[par] = acc[...]
            loc, rem = tile_copies(i, par)
            loc.start()
            rem.start()

        @pl.when(jnp.logical_and(i == ni - 1, k == nk - 1))
        def _():
            for t in (ni - 2, ni - 1):
                loc, rem = tile_copies(t, t % 2)
                loc.wait()
                rem.wait_send()
            pltpu.make_async_remote_copy(src_ref=dw_ref, dst_ref=got_ref, send_sem=send_sems.at[0], recv_sem=recv_sem,
                                         device_id=(x, y, c), device_id_type=MESH).wait_recv()

    def a_spec(t):
        def index(i, k):
            mine = jnp.logical_and(i >= offs[t], i < offs[t] + counts[t])
            return jnp.where(mine, k, 0), jnp.clip(i - offs[t], 0, counts[t] - 1)
        return pl.BlockSpec((tk, tile), index)

    anyspec = pl.BlockSpec(memory_space=pl.ANY)
    return pl.pallas_call(
        body, name="dw_in_swap", grid=(ni, nk),
        in_specs=[a_spec(t) for t in range(na)] + [pl.BlockSpec((s, tile), lambda i, k: (0, 0))],
        out_specs=[anyspec] * 2,
        out_shape=[SDS((ni * tile, tile), f32), SDS((ni * tile, tile), f32)],
        scratch_shapes=[pltpu.VMEM((tile, tile), f32), pltpu.VMEM((2, tile, tile), f32), pltpu.SemaphoreType.DMA((2,)),
                        pltpu.SemaphoreType.DMA((2,)), pltpu.SemaphoreType.DMA(())],
        compiler_params=pltpu.CompilerParams(dimension_semantics=("arbitrary", "arbitrary")),
    )(*a_parts, u)


def _chip_sum(mine, got, rows, name):
    r, cdim = mine.shape
    tc = LANES

    def body(m_ref, g_ref, s16_ref):
        c = lax.axis_index("c")
        for q in range(4):
            blk = pl.ds(rows * (2 * q + c), rows)
            s16_ref[q] = (m_ref[blk, :] + g_ref[blk, :]).astype(bf16)

    col = pl.BlockSpec((r, tc), lambda i: (0, i))
    return pl.pallas_call(
        body, name=name, grid=(cdim // tc,), in_specs=[col, col],
        out_specs=pl.BlockSpec((4, rows, tc), lambda i: (0, 0, i)), out_shape=SDS((4, rows, cdim), bf16),
        compiler_params=pltpu.CompilerParams(dimension_semantics=("parallel",)),
    )(mine, got)


def _assemble_wt(shards):
    nd, rows, cdim = shards.shape
    tc = 256

    def body(g_ref, o_ref):
        for j in range(nd):
            o_ref[pl.ds(rows * j, rows), :] = g_ref[j]
        o_ref[pl.ds(nd * rows, NP - nd * rows), :] = jnp.zeros((NP - nd * rows, tc), shards.dtype)

    return pl.pallas_call(
        body, name="assemble_w_in", grid=(cdim // tc,),
        in_specs=[pl.BlockSpec((nd, rows, tc), lambda i: (0, 0, i))],
        out_specs=pl.BlockSpec((NP, tc), lambda i: (0, i)), out_shape=SDS((NP, cdim), shards.dtype),
        compiler_params=pltpu.CompilerParams(dimension_semantics=("parallel",)),
    )(shards)


def _chip_exchange_copies(ins, outs, send_sems, recv_sems, local_sems):
    nb = len(ins)
    x, y, c = _my_pos()
    my_q = 2 * x + y
    mine = [pltpu.make_async_copy(ins[a].at[my_q], outs[a].at[my_q], local_sems.at[a]) for a in range(nb)]
    sends, recvs = [], []
    for k in range(1, 4):
        to, frm = (my_q + k) % 4, (my_q + 4 - k) % 4
        for a in range(nb):
            sems = dict(send_sem=send_sems.at[3 * a + k - 1], recv_sem=recv_sems.at[3 * a + k - 1], device_id_type=MESH)
            sends.append(pltpu.make_async_remote_copy(
                src_ref=ins[a].at[to], dst_ref=outs[a].at[my_q], device_id=(to // 2, to % 2, c), **sems))
            recvs.append(pltpu.make_async_remote_copy(
                src_ref=ins[a].at[frm], dst_ref=outs[a].at[frm], device_id=(x, y, c), **sems))
    return mine, sends, recvs


def _chip_exchange_scratch(nb):
    return [pltpu.SemaphoreType.DMA((3 * nb,)), pltpu.SemaphoreType.DMA((3 * nb,)), pltpu.SemaphoreType.DMA((nb,))]


def _prenorm_inproj(x, nw, wt, gather=()):
    s, d = x.shape
    npad = wt.shape[0]
    tm, tn = 1024, 1024
    ng = len(gather)
    ni, nj = s // tm, npad // tn

    def body(x_ref, nw_ref, w_ref, *refs):
        g_in, (proj_ref, u_ref), g_out, sems = refs[:ng], refs[ng:ng + 2], refs[ng + 2:2 * ng + 2], refs[2 * ng + 2:]
        i, j = pl.program_id(0), pl.program_id(1)
        if ng:
            phases = _gather_phases(g_in, g_out, *sems)
            for step, phase in enumerate(phases[:3]):
                @pl.when(jnp.logical_and(i == step, j == 0))
                def _(phase=phase):
                    phase()

        @pl.when(j == 0)
        def _():
            xv = x_ref[...]
            r = lax.rsqrt(jnp.mean(xv * xv, axis=-1, keepdims=True) + EPS)
            u_ref[...] = (xv * r * nw_ref[...]).astype(bf16)
        proj_ref[...] = _nt(u_ref[...], w_ref[pl.ds(pl.multiple_of(j * tn, tn), tn), :])

        if ng:
            @pl.when(jnp.logical_and(i == ni - 1, j == nj - 1))
            def _():
                phases[3]()

    anyspec = pl.BlockSpec(memory_space=pl.ANY)
    outs = pl.pallas_call(
        body, name="prenorm_inproj", grid=(ni, nj),
        in_specs=[pl.BlockSpec((tm, d), lambda i, j: (i, 0)), pl.BlockSpec((1, d), lambda i, j: (0, 0)),
                  pl.BlockSpec((npad, d), lambda i, j: (0, 0))] + [anyspec] * ng,
        out_specs=[pl.BlockSpec((tm, tn), lambda i, j: (i, j)), pl.BlockSpec((tm, d), lambda i, j: (i, 0))]
        + [anyspec] * ng,
        out_shape=[SDS((s, npad), f32), SDS((s, d), bf16)] + [SDS((N_DEV,) + a.shape, a.dtype) for a in gather],
        scratch_shapes=_gather_scratch(ng) if ng else [],
        compiler_params=pltpu.CompilerParams(dimension_semantics=("arbitrary", "arbitrary")),
    )(x, nw, wt, *gather)
    return outs[0], outs[1], outs[2:]


def _attn_consts():
    head0 = _iota((BLK, LANES), 1) < HEAD_DIM
    tri2 = (_iota((BLK, 2 * LANES), 1) % LANES) <= _iota((BLK, 2 * LANES), 0)
    ones2 = ((_iota((LANES, 2 * LANES), 0) < HEAD_DIM) == (_iota((LANES, 2 * LANES), 1) < LANES)).astype(bf16)
    rmat = ((_iota((2 * LANES, LANES), 0) < LANES) == (_iota((2 * LANES, LANES), 1) < HEAD_DIM)).astype(bf16)
    bones = ((_iota((LANES, LANES), 0) < HEAD_DIM) == (_iota((LANES, LANES), 1) < HEAD_DIM)).astype(bf16)
    return head0, tri2, ones2, rmat, bones


def _stack_heads(x16, head0):
    zero = jnp.zeros_like(x16)
    return jnp.concatenate([jnp.where(head0, x16, zero), jnp.where(head0, zero, x16)], axis=0)


def _bf16_terms(x, terms):
    out = []
    for _ in range(terms):
        t = x.astype(bf16)
        out.append(t)
        x = x - t.astype(f32)
    return out


def _dot_01(x, w16, terms):
    return _nn(jnp.concatenate(_bf16_terms(x, terms), axis=1), jnp.concatenate([w16] * terms, axis=0))


def _split_dot_sum(x, w16):
    hi, lo = _bf16_terms(x, 2)
    return _nn(hi, w16) + _nn(lo, w16)


def _dot_01_left(w16, x, terms):
    return _nn(jnp.concatenate([w16] * terms, axis=1), jnp.concatenate(_bf16_terms(x, terms), axis=0))


def _attn_fwd(proj):
    s = proj.shape[0]
    n_it = s // BLK

    def body(q_ref, k_ref, v_ref, g_ref, o_ref, l_ref, mix_ref, op0, op1, op2, lp0, lp1, lp2,
             s_a, s_b, sd_a, sd_b, p_a, p_b, m_a, m_b, pd_a, pd_b, k_a, k_b, v_a, v_b):
        op_refs, lp_refs = (op0, op1, op2), (lp0, lp1, lp2)
        head0, tri2, ones2, rmat, _ = _attn_consts()
        score_bufs, prob_bufs = ((s_a, sd_a), (s_b, sd_b)), ((p_a, m_a, pd_a), (p_b, m_b, pd_b))
        k_bufs, v_bufs = (k_a, k_b), (v_a, v_b)
        for buf in k_bufs + v_bufs:
            buf[...] = jnp.zeros_like(buf)

        def block_rows(i, d, nb):
            r, blk = i // nb, i % nb
            return pl.ds(blk * (BLK * d) + r, BLK, stride=d), blk > 0

        def unstack(st16):
            return st16[:BLK] + st16[BLK:]

        def scores(i, par, d, nb):
            rows, has_prev = block_rows(i, d, nb)
            s_buf, sd_buf = score_bufs[par]
            qs = q_ref[rows, :] * 0.125
            qs16 = qs.astype(bf16)
            kst_c = _stack_heads(k_ref[rows, :].astype(bf16), head0)
            kst_p = k_bufs[1 - par][...]
            k_bufs[par][...] = kst_c
            sc = _nt(qs16, kst_c)
            sp = _nt(qs16, kst_p)
            s_buf[...] = jnp.where(tri2, sc, jnp.where(has_prev, sp, -jnp.inf))
            sd = _nn((qs * unstack(kst_p).astype(f32)).astype(bf16), ones2)
            sd_buf[...] = jnp.where(has_prev, sd, -jnp.inf)

        def softmax(bufs_in, bufs_out):
            s_buf, sd_buf = bufs_in
            p_buf, m_buf, pd_buf = bufs_out
            sc, sd2 = s_buf[...], sd_buf[...]
            m0 = jnp.max(sc[:, :LANES], axis=1, keepdims=True)
            m1 = jnp.max(sc[:, LANES:], axis=1, keepdims=True)
            m2 = jnp.concatenate([jnp.broadcast_to(m0, (BLK, LANES)), jnp.broadcast_to(m1, (BLK, LANES))], axis=1)
            m2 = jnp.maximum(m2, sd2)
            p_buf[...] = jnp.exp(sc - m2).astype(bf16)
            m_pair = jnp.where(head0, m2[:, :LANES], m2[:, LANES:])
            m_buf[...] = m_pair
            pd_buf[...] = jnp.exp(jnp.where(head0, sd2[:, :LANES], sd2[:, LANES:]) - m_pair)

        def output(i, par, d, nb, p):
            rows, _ = block_rows(i, d, nb)
            p_buf, m_buf, pd_buf = prob_bufs[par]
            vst_c = _stack_heads(v_ref[rows, :].astype(bf16), head0)
            vst_p = v_bufs[1 - par][...]
            v_bufs[par][...] = vst_c
            pt16, pd = p_buf[...], pd_buf[...]
            zero = jnp.zeros_like(pt16)
            o = (_nn(jnp.where(tri2, pt16, zero), vst_c) + _nn(jnp.where(tri2, zero, pt16), vst_p)
                 + pd * unstack(vst_p).astype(f32))
            l = _nn(pt16, rmat) + pd
            op_refs[p][rows, :] = o / l
            lp_refs[p][rows, :] = m_buf[...] + jnp.log(l)

        for p, d in enumerate(DILATIONS):
            nb = s // (BLK * d)
            scores(0, 0, d, nb)
            scores(1, 1, d, nb)
            softmax(score_bufs[0], prob_bufs[0])

            def steps(j, carry, d=d, nb=nb, p=p):
                for par in range(2):
                    t = 2 * j + 2 + par
                    scores(t, par, d, nb)
                    output(t - 2, par, d, nb, p)
                    softmax(score_bufs[1 - par], prob_bufs[1 - par])
                return carry

            lax.fori_loop(0, (n_it - 2) // 2, steps, 0)
            output(n_it - 2, 0, d, nb, p)
            softmax(score_bufs[1], prob_bufs[1])
            output(n_it - 1, 1, d, nb, p)

        def merge(i, carry):
            rows = pl.ds(pl.multiple_of(i * 256, 256), 256)
            l0, l1, l2 = lp0[rows, :], lp1[rows, :], lp2[rows, :]
            m = jnp.maximum(jnp.maximum(l0, l1), l2)
            e0, e1, e2 = jnp.exp(l0 - m), jnp.exp(l1 - m), jnp.exp(l2 - m)
            z = e0 + e1 + e2
            o = (e0 * op0[rows, :] + e1 * op1[rows, :] + e2 * op2[rows, :]) / z
            o_ref[rows, :] = o
            l_ref[rows, :] = m + jnp.log(z)
            g = g_ref[rows, :]
            mix_ref[rows, :] = (o * (g * _sigmoid(g))).astype(bf16)
            return carry

        lax.fori_loop(0, s // 256, merge, 0)

    col = lambda base: pl.BlockSpec((s, LANES), lambda h: (0, base + h))
    return pl.pallas_call(
        body, name="attn_fwd", grid=(N_PAIRS,),
        in_specs=[col(0), col(8), col(16), col(24)],
        out_specs=[col(0), col(0), col(0)],
        out_shape=[SDS((s, D_ATTN), f32), SDS((s, D_ATTN), f32), SDS((s, D_ATTN), bf16)],
        scratch_shapes=[pltpu.VMEM((s, LANES), f32)] * 6 + [pltpu.VMEM((BLK, 2 * LANES), f32)] * 4
        + [pltpu.VMEM((BLK, 2 * LANES), bf16)] * 2 + [pltpu.VMEM((BLK, LANES), f32)] * 4
        + [pltpu.VMEM((2 * BLK, LANES), bf16)] * 4,
        compiler_params=pltpu.CompilerParams(dimension_semantics=("parallel",)),
    )(proj, proj, proj, proj)


def _expand_mat():
    colv = _iota((LANES, 2 * D_SSM), 1)
    head = 2 * ((colv % D_SSM) // LANES) + colv // D_SSM
    return (_iota((LANES, 2 * D_SSM), 0) == head).astype(bf16)


def _fold_mat():
    return (_iota((D_SSM, LANES), 0) // HEAD_DIM == _iota((D_SSM, LANES), 1)).astype(bf16)


def _conv(xs_ref, bc_ref, xs_tail, bc_tail, cw_ref, cb_ref, xpad, first):
    keep = jnp.where(first, 0.0, 1.0)
    xpad[0:8, 0:D_SSM] = xs_tail[...] * keep
    xpad[0:8, D_SSM:D_CONV] = bc_tail[...] * keep
    xpad[8:8 + CHUNK, 0:D_SSM] = xs_ref[...]
    xpad[8:8 + CHUNK, D_SSM:D_CONV] = bc_ref[...]
    xp = xpad[...]
    cv = cb_ref[...] + cw_ref[3:4, :] * xp[8:8 + CHUNK]
    for j in range(3):
        cv = cv + cw_ref[j:j + 1, :] * pltpu.roll(xp, 3 - j, 0)[8:8 + CHUNK]
    return cv


def _decay_terms(dt_ref, dtb_ref, alog16_ref, emat_ref):
    pre = dt_ref[...] + dtb_ref[...]
    dt16 = _softplus(pre)
    a16 = -jnp.exp(alog16_ref[...])
    sub, lane = _iota((CHUNK, CHUNK), 0), _iota((CHUNK, CHUNK), 1)
    tri = (sub >= lane).astype(f32)
    al16 = _nn_hi(tri, dt16 * a16)
    al_t = al16.T
    emat = emat_ref[...]
    dt_x = _dot_01(dt16, emat, 3)
    al_x = _dot_01(al16, emat, 3)
    lane_w = _iota((CHUNK, D_SSM), 1)
    even = (lane_w % LANES) < HEAD_DIM
    dt_f = jnp.where(even, dt_x[:, :D_SSM], dt_x[:, D_SSM:])
    al_f = jnp.where(even, al_x[:, :D_SSM], al_x[:, D_SSM:])
    return pre, dt_f, al_f, al_x, al_t


def _decay_mat(al_x, al_t, pair, h):
    sub, lane = _iota((CHUNK, CHUNK), 0), _iota((CHUNK, CHUNK), 1)
    col = al_x[:, h * D_SSM + pair * LANES: h * D_SSM + (pair + 1) * LANES]
    row = al_t[2 * pair + h: 2 * pair + h + 1, :]
    return jnp.exp(jnp.where(sub >= lane, col - row, -jnp.inf))


def _ssd_in_specs(order):
    blk = lambda w, cb: pl.BlockSpec((CHUNK, w), lambda i: (order(i), cb))
    tail = lambda w, cb: pl.BlockSpec((8, w), lambda i: (jnp.maximum(16 * order(i) - 1, 0), cb))
    return [blk(D_SSM, COL_XS // D_SSM), blk(512, COL_BC // 512), tail(D_SSM, COL_XS // D_SSM),
            tail(512, COL_BC // 512), blk(LANES, COL_DT // LANES), blk(D_SSM, COL_Z // D_SSM)]


def _full(shape):
    return pl.BlockSpec(shape, lambda i: (0,) * len(shape))


def _ssd_fwd(proj, conv_w, conv_b, dtb16, alog16, alog_f, d_f, nw):
    s = proj.shape[0]
    nc = s // CHUNK

    def body(xs_ref, bc_ref, xs_tail, bc_tail, dt_ref, z_ref, cw_ref, cb_ref, dtb_ref, alog16_ref, alogf_ref,
             df_ref, nw_ref, mix_ref, y_ref, st_ref, cv_ref, h_scr, xpad, y_scr, emat_ref):
        c = pl.program_id(0)

        @pl.when(c == 0)
        def _():
            h_scr[...] = jnp.zeros_like(h_scr)
            emat_ref[...] = _expand_mat()

        cv = _conv(xs_ref, bc_ref, xs_tail, bc_tail, cw_ref, cb_ref, xpad, c == 0)
        cv_ref[...] = cv
        xbc = cv * _sigmoid(cv)
        _, dt_f, al_f, al_x, al_t = _decay_terms(dt_ref, dtb_ref, alog16_ref, emat_ref)
        head0 = _iota((CHUNK, LANES), 1) < HEAD_DIM
        st_ref[...] = h_scr[...]
        for g in range(N_GROUPS):
            bm = xbc[:, D_SSM + g * D_STATE: D_SSM + (g + 1) * D_STATE].astype(bf16)
            cm = xbc[:, D_SSM + (N_GROUPS + g) * D_STATE: D_SSM + (N_GROUPS + g + 1) * D_STATE].astype(bf16)
            gmat = _nt(cm, bm)
            for pair in range(4 * g, 4 * g + 4):
                sl = slice(pair * LANES, (pair + 1) * LANES)
                xp, dtp, alp = xbc[:, sl], dt_f[:, sl], al_f[:, sl]
                xdt = xp * dtp
                xdt16 = xdt.astype(bf16)
                al_last = alp[CHUNK - 1:CHUNK, :]
                hp = h_scr[:, sl]
                y_off = jnp.exp(alp) * _nn(cm, hp.astype(bf16))
                yd = [_nn((gmat * _decay_mat(al_x, al_t, pair, h)).astype(bf16), xdt16) for h in range(2)]
                y_scr[:, sl] = jnp.where(head0, yd[0], yd[1]) + y_off + df_ref[:, sl] * xp
                st = _tn(bm, (jnp.exp(al_last - alp) * xdt).astype(bf16))
                h_scr[:, sl] = jnp.exp(al_last) * hp + st
        y = y_scr[...]
        y_ref[...] = y
        z = z_ref[...]
        yz = y * (z * _sigmoid(z))
        gw = D_SSM // N_GROUPS
        for g in range(N_GROUPS):
            part = yz[:, g * gw:(g + 1) * gw]
            r = lax.rsqrt(jnp.mean(part * part, axis=-1, keepdims=True) + EPS)
            mix_ref[:, g * gw:(g + 1) * gw] = (part * r * nw_ref[:, g * gw:(g + 1) * gw]).astype(bf16)

    order = lambda i: i
    row = lambda w: pl.BlockSpec((CHUNK, w), lambda i: (i, 0))
    return pl.pallas_call(
        body, name="ssd_fwd", grid=(nc,),
        in_specs=_ssd_in_specs(order) + [_full((4, D_CONV)), _full((1, D_CONV)), _full((1, LANES)), _full((1, LANES)),
                                         _full((1, D_SSM)), _full((1, D_SSM)), _full((1, D_SSM))],
        out_specs=[row(D_SSM), row(D_SSM), pl.BlockSpec((None, D_STATE, D_SSM), lambda i: (i, 0, 0)), row(D_CONV)],
        out_shape=[SDS((s, D_SSM), bf16), SDS((s, D_SSM), f32), SDS((nc, D_STATE, D_SSM), f32),
                   SDS((s, D_CONV), f32)],
        scratch_shapes=[pltpu.VMEM((D_STATE, D_SSM), f32), pltpu.VMEM((8 + CHUNK, D_CONV), f32),
                        pltpu.VMEM((CHUNK, D_SSM), f32), pltpu.VMEM((LANES, 2 * D_SSM), bf16)],
        compiler_params=pltpu.CompilerParams(dimension_semantics=("arbitrary",)),
    )(proj, proj, proj, proj, proj, proj, conv_w, conv_b, dtb16, alog16, alog_f, d_f, nw)


def _outproj_loss(mix_a, mix_s, wo, x, tgt, npw):
    s, d = x.shape
    tm = 512

    def body(ma_ref, ms_ref, wo_ref, x_ref, t_ref, npw_ref, dmix_ref, dres_ref, acc_ref, dwo_ref):
        @pl.when(pl.program_id(0) == 0)
        def _():
            acc_ref[...] = jnp.zeros_like(acc_ref)
            dwo_ref[...] = jnp.zeros_like(dwo_ref)

        out = _nn(ma_ref[...], wo_ref[0:D_ATTN, :]) + _nn(ms_ref[...], wo_ref[D_ATTN:, :])
        r = lax.rsqrt(jnp.mean(out * out, axis=-1, keepdims=True) + EPS)
        on = out * r
        diff = x_ref[...] + on * npw_ref[...] - t_ref[...]
        dres = diff * (1.0 / d)
        dres_ref[...] = dres
        acc_ref[0:1, :] += jnp.sum(diff * diff, axis=0, keepdims=True)
        acc_ref[1:2, :] += jnp.sum(dres * on, axis=0, keepdims=True)
        dn = dres * npw_ref[...]
        dout = (r * (dn - on * jnp.mean(dn * on, axis=-1, keepdims=True))).astype(bf16)
        dmix_ref[...] = _nt(dout, wo_ref[...])
        dwo_ref[0:D_ATTN, :] += _tn(ma_ref[...], dout)
        dwo_ref[D_ATTN:, :] += _tn(ms_ref[...], dout)

    row = lambda w: pl.BlockSpec((tm, w), lambda i: (i, 0))
    return pl.pallas_call(
        body, name="outproj_loss", grid=(s // tm,),
        in_specs=[row(D_ATTN), row(D_SSM), _full((D_ATTN + D_SSM, d)), row(d), row(d), _full((1, d))],
        out_specs=[row(D_ATTN + D_SSM), row(d), _full((8, d)), _full((D_ATTN + D_SSM, d))],
        out_shape=[SDS((s, D_ATTN + D_SSM), f32), SDS((s, d), f32), SDS((8, d), f32), SDS((D_ATTN + D_SSM, d), f32)],
        compiler_params=pltpu.CompilerParams(dimension_semantics=("arbitrary",)),
    )(mix_a, mix_s, wo, x, tgt, npw)


def _attn_bwd(proj, o, lb, dmix, swap=None):
    s = proj.shape[0]
    n_it = s // BLK

    nsw = 0 if swap is None else 1

    def body(*refs):
        q_ref, k_ref, v_ref, g_ref, o_ref, l_ref, dm_ref = refs[:7]
        swap_in = refs[7:7 + nsw]
        dq_ref, dk_ref, dv_ref, dg_ref = refs[7 + nsw:11 + nsw]
        swap_out = refs[11 + nsw:11 + 2 * nsw]
        dq_acc, dk_acc, dv_acc, do_scr, dl_scr = refs[11 + 2 * nsw:16 + 2 * nsw]
        bufs = refs[16 + 2 * nsw:44 + 2 * nsw]
        swap_sems = refs[44 + 2 * nsw:]
        head0, tri2, _, _, bones = _attn_consts()

        if nsw:
            x, y, c = _my_pos()
            swap_copy = pltpu.make_async_remote_copy(
                src_ref=swap_in[0], dst_ref=swap_out[0], send_sem=swap_sems[0], recv_sem=swap_sems[1],
                device_id=(x, y, 1 - c), device_id_type=MESH)

            @pl.when(pl.program_id(0) == 0)
            def _():
                swap_copy.start()

        def pro(i, carry):
            rows = pl.ds(pl.multiple_of(i * 256, 256), 256)
            g = g_ref[rows, :]
            sg = _sigmoid(g)
            dmx = dm_ref[rows, :]
            ov = o_ref[rows, :]
            dg_ref[rows, :] = (dmx * ov * (sg * (1.0 + g * (1.0 - sg)))).astype(bf16)
            do = dmx * (g * sg)
            do_scr[rows, :] = do
            dl_scr[rows, :] = _split_dot_sum(do * ov, bones)
            z = jnp.zeros((256, LANES), f32)
            dq_acc[rows, :] = z
            dk_acc[rows, :] = z
            dv_acc[rows, :] = z
            return carry

        lax.fori_loop(0, s // 256, pro, 0)

        def per_head(t):
            return jnp.concatenate([t[:, :LANES], t[:, LANES:]], axis=0)

        def both_heads(t):
            tr = pltpu.roll(t, HEAD_DIM, 1)
            return jnp.concatenate([jnp.where(head0, t, tr), jnp.where(head0, tr, t)], axis=1)

        mm_bufs = ((bufs[0], bufs[1], bufs[2], bufs[3]), (bufs[4], bufs[5], bufs[6], bufs[7]))
        ds_bufs = ((bufs[8], bufs[9], bufs[10], bufs[11]), (bufs[12], bufs[13], bufs[14], bufs[15]))
        op_bufs = ((bufs[16], bufs[17], bufs[18], bufs[19]), (bufs[20], bufs[21], bufs[22], bufs[23]))
        vc_bufs, carry_k, carry_v = (bufs[24], bufs[25]), bufs[26], bufs[27]
        for buf in (op_bufs[0][0], op_bufs[1][0]) + vc_bufs:
            buf[...] = jnp.zeros_like(buf)

        def block_rows(i, d, nb):
            r, blk = i // nb, i % nb
            return pl.ds(blk * (BLK * d) + r, BLK, stride=d), blk > 0

        def unstack(st16):
            return st16[:BLK] + st16[BLK:]

        def products(i, par, d, nb):
            rows, has_prev = block_rows(i, d, nb)
            s_buf, dp_buf, sd_buf, dpd_buf = mm_bufs[par]
            kc_buf, kp_buf, q_buf, do_buf = op_bufs[par]
            q = q_ref[rows, :]
            qs = q * 0.125
            do = do_scr[rows, :]
            qs16, do16 = qs.astype(bf16), do.astype(bf16)
            kst_c = _stack_heads(k_ref[rows, :].astype(bf16), head0)
            vst_c = _stack_heads(v_ref[rows, :].astype(bf16), head0)
            kst_p, vst_p = op_bufs[1 - par][0][...], vc_bufs[1 - par][...]
            kc_buf[...] = kst_c
            kp_buf[...] = kst_p
            vc_bufs[par][...] = vst_c
            q_buf[...] = q.astype(bf16)
            do_buf[...] = do16
            s_buf[...] = jnp.where(tri2, _nt(qs16, kst_c), jnp.where(has_prev, _nt(qs16, kst_p), -jnp.inf))
            dp_buf[...] = jnp.where(tri2, _nt(do16, vst_c), jnp.where(has_prev, _nt(do16, vst_p), 0.0))
            sd_buf[...] = _nn((qs * unstack(kst_p).astype(f32)).astype(bf16), bones)
            dpd_buf[...] = jnp.where(has_prev, _nn((do * unstack(vst_p).astype(f32)).astype(bf16), bones), 0.0)

        def softmax_grad(i, par, d, nb):
            rows, has_prev = block_rows(i, d, nb)
            s_buf, dp_buf, sd_buf, dpd_buf = mm_bufs[par]
            p_buf, ds_buf, pd_buf, dsd_buf = ds_bufs[par]
            lse = l_ref[rows, :]
            dl = dl_scr[rows, :]
            pt = jnp.exp(s_buf[...] - both_heads(lse))
            ds_buf[...] = (pt * (dp_buf[...] - both_heads(dl)) * 0.125).astype(bf16)
            p_buf[...] = pt.astype(bf16)
            pd = jnp.where(has_prev, jnp.exp(sd_buf[...] - lse), 0.0)
            pd_buf[...] = pd
            dsd_buf[...] = pd * (dpd_buf[...] - dl) * 0.125

        def accumulate(i, par, d, nb):
            rows, _ = block_rows(i, d, nb)
            before, _ = block_rows(jnp.maximum(i - 1, 0), d, nb)
            p_buf, ds_buf, pd_buf, dsd_buf = ds_bufs[par]
            kc_buf, kp_buf, q_buf, do_buf = op_bufs[par]
            pt16, ds16, pd, dsd = p_buf[...], ds_buf[...], pd_buf[...], dsd_buf[...]
            zero = jnp.zeros_like(pt16)
            dsc, dsp = jnp.where(tri2, ds16, zero), jnp.where(tri2, zero, ds16)
            pc, pp = jnp.where(tri2, pt16, zero), jnp.where(tri2, zero, pt16)
            kst_c, kst_p, q16, do16 = kc_buf[...], kp_buf[...], q_buf[...], do_buf[...]
            qst, dost = _stack_heads(q16, head0), _stack_heads(do16, head0)
            dq_acc[rows, :] += _nn(dsc, kst_c) + _nn(dsp, kst_p) + dsd * unstack(kst_p).astype(f32)
            dk2 = _tn(jnp.concatenate([per_head(dsc), per_head(dsp)], axis=1), qst)
            dv2 = _tn(jnp.concatenate([per_head(pc), per_head(pp)], axis=1), dost)
            dk_acc[before, :] += carry_k[...] + dk2[BLK:] + dsd * q16.astype(f32)
            dv_acc[before, :] += carry_v[...] + dv2[BLK:] + pd * do16.astype(f32)
            carry_k[...] = dk2[:BLK]
            carry_v[...] = dv2[:BLK]

        for d in DILATIONS:
            nb = s // (BLK * d)
            carry_k[...] = jnp.zeros_like(carry_k)
            carry_v[...] = jnp.zeros_like(carry_v)
            products(0, 0, d, nb)
            products(1, 1, d, nb)
            softmax_grad(0, 0, d, nb)

            def steps(j, carry, d=d, nb=nb):
                for par in range(2):
                    t = 2 * j + 2 + par
                    accumulate(t - 2, par, d, nb)
                    products(t, par, d, nb)
                    softmax_grad(t - 1, 1 - par, d, nb)
                return carry

            lax.fori_loop(0, (n_it - 2) // 2, steps, 0)
            accumulate(n_it - 2, 0, d, nb)
            softmax_grad(n_it - 1, 1, d, nb)
            accumulate(n_it - 1, 1, d, nb)
            last, _ = block_rows(n_it - 1, d, nb)
            dk_acc[last, :] += carry_k[...]
            dv_acc[last, :] += carry_v[...]

        def epi(i, carry):
            rows = pl.ds(pl.multiple_of(i * 256, 256), 256)
            dq_ref[rows, :] = dq_acc[rows, :].astype(bf16)
            dk_ref[rows, :] = dk_acc[rows, :].astype(bf16)
            dv_ref[rows, :] = dv_acc[rows, :].astype(bf16)
            return carry

        lax.fori_loop(0, s // 256, epi, 0)

        if nsw:
            @pl.when(pl.program_id(0) == N_PAIRS - 1)
            def _():
                swap_copy.wait_send()
                swap_copy.wait_recv()

    col = lambda base: pl.BlockSpec((s, LANES), lambda h: (0, base + h))
    anyspec = pl.BlockSpec(memory_space=pl.ANY)
    swaps = [] if swap is None else [swap]
    outs = pl.pallas_call(
        body, name="attn_bwd", grid=(N_PAIRS,),
        in_specs=[col(0), col(8), col(16), col(24), col(0), col(0), col(0)] + [anyspec] * nsw,
        out_specs=[col(0)] * 4 + [anyspec] * nsw,
        out_shape=[SDS((s, D_ATTN), bf16)] * 4 + [SDS(a.shape, a.dtype) for a in swaps],
        scratch_shapes=[pltpu.VMEM((s, LANES), f32)] * 5
        + [pltpu.VMEM((BLK, 2 * LANES), f32)] * 2 + [pltpu.VMEM((BLK, LANES), f32)] * 2
        + [pltpu.VMEM((BLK, 2 * LANES), f32)] * 2 + [pltpu.VMEM((BLK, LANES), f32)] * 2
        + [pltpu.VMEM((BLK, 2 * LANES), bf16)] * 2 + [pltpu.VMEM((BLK, LANES), f32)] * 2
        + [pltpu.VMEM((BLK, 2 * LANES), bf16)] * 2 + [pltpu.VMEM((BLK, LANES), f32)] * 2
        + [pltpu.VMEM((2 * BLK, LANES), bf16)] * 2 + [pltpu.VMEM((BLK, LANES), bf16)] * 2
        + [pltpu.VMEM((2 * BLK, LANES), bf16)] * 2 + [pltpu.VMEM((BLK, LANES), bf16)] * 2
        + [pltpu.VMEM((2 * BLK, LANES), bf16)] * 2 + [pltpu.VMEM((BLK, LANES), f32)] * 2
        + [pltpu.SemaphoreType.DMA(())] * (2 * nsw),
        compiler_params=pltpu.CompilerParams(dimension_semantics=("arbitrary",)),
    )(proj, proj, proj, proj, o, lb, dmix, *swaps)
    return outs


def _ssd_bwd(proj, y, states, cv, dmix, conv_w, conv_b, dtb16, alog16, alog_f, d_f, nw, chip_sums=()):
    s = proj.shape[0]
    nc = s // CHUNK
    gw = D_SSM // N_GROUPS
    nx = len(chip_sums)

    def body(*refs):
        (xs_ref, bc_ref, _, _, dt_ref, z_ref, y_ref, st_ref, dm_ref, cw_ref, cb_ref, dtb_ref,
         alog16_ref, alogf_ref, df_ref, nw_ref, cv_ref) = refs[:17]
        cs_in = refs[17:17 + nx]
        out_ref, gconv_ref, gvec_ref, gdt_ref = refs[17 + nx:21 + nx]
        cs_out = refs[21 + nx:21 + 2 * nx]
        (dh_scr, head_scr, dcpad, da_scr, dxdt_scr, dbc_scr, emat_ref, fold_ref) = refs[21 + 2 * nx:29 + 2 * nx]
        cs_sems = refs[29 + 2 * nx:]
        i = pl.program_id(0)
        c = nc - 1 - i

        if nx:
            @pl.when(i == 0)
            def _():
                mine, sends, _ = _chip_exchange_copies(cs_in, cs_out, *cs_sems)
                for cp in mine + sends:
                    cp.start()

            @pl.when(i == nc - 1)
            def _():
                mine, sends, recvs = _chip_exchange_copies(cs_in, cs_out, *cs_sems)
                for cp in recvs:
                    cp.wait_recv()
                for cp in sends:
                    cp.wait_send()
                for cp in mine:
                    cp.wait()

        @pl.when(i == 0)
        def _():
            emat_ref[...] = _expand_mat()
            fold_ref[...] = _fold_mat()
            dh_scr[...] = jnp.zeros_like(dh_scr)
            head_scr[...] = jnp.zeros_like(head_scr)
            gconv_ref[...] = jnp.zeros_like(gconv_ref)
            gvec_ref[...] = jnp.zeros_like(gvec_ref)
            gdt_ref[...] = jnp.zeros_like(gdt_ref)

        cv = cv_ref[...]
        sig = _sigmoid(cv)
        xbc = cv * sig
        pre, dt_f, al_f, al_x, al_t = _decay_terms(dt_ref, dtb_ref, alog16_ref, emat_ref)
        head0 = _iota((CHUNK, LANES), 1) < HEAD_DIM
        sub = _iota((CHUNK, LANES), 0)
        last_row = sub == CHUNK - 1

        yv, z, dmx = y_ref[...], z_ref[...], dm_ref[...]
        sz = _sigmoid(z)
        silu = z * sz
        yz = yv * silu
        dyz_parts = []
        for g in range(N_GROUPS):
            gs = slice(g * gw, (g + 1) * gw)
            part = yz[:, gs]
            r = lax.rsqrt(jnp.mean(part * part, axis=-1, keepdims=True) + EPS)
            nh = part * r
            gvec_ref[0:1, gs] += jnp.sum(dmx[:, gs] * nh, axis=0, keepdims=True)
            dn = dmx[:, gs] * nw_ref[:, gs]
            dyz_parts.append(r * (dn - nh * jnp.mean(dn * nh, axis=-1, keepdims=True)))
        dyz = jnp.concatenate(dyz_parts, axis=1)
        dy = dyz * silu
        out_ref[:, 0:D_SSM] = (dyz * yv * (sz * (1.0 + z * (1.0 - sz)))).astype(bf16)

        x_all = xbc[:, 0:D_SSM]
        gvec_ref[2:3, :] += jnp.sum(dy * x_all, axis=0, keepdims=True)

        for g in range(N_GROUPS):
            bm = xbc[:, D_SSM + g * D_STATE: D_SSM + (g + 1) * D_STATE].astype(bf16)
            cm = xbc[:, D_SSM + (N_GROUPS + g) * D_STATE: D_SSM + (N_GROUPS + g + 1) * D_STATE].astype(bf16)
            gmat = _nt(cm, bm)
            dgm = jnp.zeros((CHUNK, CHUNK), f32)
            db = jnp.zeros((CHUNK, D_STATE), f32)
            dc = jnp.zeros((CHUNK, D_STATE), f32)
            for pair in range(4 * g, 4 * g + 4):
                sl = slice(pair * LANES, (pair + 1) * LANES)
                xp, dtp, alp, dyp = x_all[:, sl], dt_f[:, sl], al_f[:, sl], dy[:, sl]
                xdt = xp * dtp
                xdt16 = xdt.astype(bf16)
                al_last = alp[CHUNK - 1:CHUNK, :]
                e_l = jnp.exp(alp)
                wf = jnp.exp(al_last - alp)
                e_last = jnp.exp(al_last)
                hp = st_ref[:, sl]
                hp16 = hp.astype(bf16)
                dhn = dh_scr[:, sl]
                dhn16 = dhn.astype(bf16)
                y_off = e_l * _nn(cm, hp16)
                dch16 = (dyp * e_l).astype(bf16)
                dc = dc + _nt(dch16, hp16)
                dh_out = _tn(cm, dch16)
                dal = dyp * y_off
                xw16 = (wf * xdt).astype(bf16)
                db = db + _nt(xw16, dhn16)
                dxw = _nn(bm, dhn16)
                dxdt = dxw * wf
                dwf = dxw * xdt * wf
                dal = dal - dwf
                dal_last = jnp.sum(dwf, axis=0, keepdims=True) + jnp.sum(dhn * hp, axis=0, keepdims=True) * e_last
                dh_scr[:, sl] = e_last * dhn + dh_out
                for h in range(2):
                    mh = head0 if h == 0 else jnp.logical_not(head0)
                    dyh16 = jnp.where(mh, dyp, 0.0).astype(bf16)
                    lmat = _decay_mat(al_x, al_t, pair, h)
                    mm = gmat * lmat
                    dmm = _nt(dyh16, xdt16)
                    dxdt = dxdt + _tn(mm.astype(bf16), dyh16)
                    n16 = (dmm * mm).astype(bf16)
                    jh = jnp.where(mh, 1.0 / HEAD_DIM, 0.0).astype(bf16)
                    dal = dal + _nn(n16, jh) - _tn(n16, jh)
                    dgm = dgm + dmm * lmat
                da_scr[:, sl] = dal + jnp.where(last_row, dal_last, 0.0)
                dxdt_scr[:, sl] = dxdt
            dgm16 = dgm.astype(bf16)
            dbc_scr[:, g * D_STATE:(g + 1) * D_STATE] = db + _tn(dgm16, cm)
            dbc_scr[:, (N_GROUPS + g) * D_STATE:(N_GROUPS + g + 1) * D_STATE] = dc + _nn(dgm16, bm)

        sub_c, lane_c = _iota((CHUNK, CHUNK), 0), _iota((CHUNK, CHUNK), 1)
        tri_t = (lane_c >= sub_c).astype(bf16)
        dadt = _dot_01_left(tri_t, da_scr[...], 2)
        a_f = -jnp.exp(alogf_ref[...])
        dxdt_all = dxdt_scr[...]
        ddt_f = dxdt_all * x_all + a_f * dadt
        gvec_ref[1:2, :] += jnp.sum(dt_f * dadt, axis=0, keepdims=True) * a_f
        dx = df_ref[...] * dy + dxdt_all * dt_f
        ddt_raw = _dot_01(ddt_f, fold_ref[...], 2) * _sigmoid(pre)
        gdt_ref[0:1, :] += jnp.sum(ddt_raw, axis=0, keepdims=True)
        out_ref[:, D_SSM + D_CONV:D_SSM + D_CONV + LANES] = ddt_raw.astype(bf16)
        out_ref[:, D_SSM + D_CONV + LANES:] = jnp.zeros((CHUNK, 3 * LANES), bf16)

        dsil = sig * (1.0 + cv * (1.0 - sig))
        dcv_x = dx * dsil[:, 0:D_SSM]
        dcv_bc = dbc_scr[...] * dsil[:, D_SSM:]
        dcpad[0:CHUNK, 0:D_SSM] = dcv_x
        dcpad[0:CHUNK, D_SSM:] = dcv_bc
        dcpad[CHUNK:, :] = head_scr[...]
        dcp = dcpad[...]
        dcv = dcp[0:CHUNK]
        gconv_ref[4:5, :] += jnp.sum(dcv, axis=0, keepdims=True)
        x_raw = jnp.concatenate([xs_ref[...], bc_ref[...]], axis=1)
        draw = cw_ref[3:4, :] * dcv
        gconv_ref[3:4, :] += jnp.sum(dcv * x_raw, axis=0, keepdims=True)
        for j in range(3):
            ahead = pltpu.roll(dcp, CHUNK + 8 - (3 - j), 0)[0:CHUNK]
            draw = draw + cw_ref[j:j + 1, :] * ahead
            gconv_ref[j:j + 1, :] += jnp.sum(ahead * x_raw, axis=0, keepdims=True)
        head_scr[...] = dcv[0:8]
        out_ref[:, D_SSM:D_SSM + D_CONV] = draw.astype(bf16)

    order = lambda i: nc - 1 - i
    row = lambda w, cb=0: pl.BlockSpec((CHUNK, w), lambda i: (nc - 1 - i, cb))
    anyspec = pl.BlockSpec(memory_space=pl.ANY)
    outs = pl.pallas_call(
        body, name="ssd_bwd", grid=(nc,),
        in_specs=_ssd_in_specs(order) + [row(D_SSM), pl.BlockSpec((None, D_STATE, D_SSM), lambda i: (nc - 1 - i, 0, 0)),
                                         row(D_SSM, 1), _full((4, D_CONV)), _full((1, D_CONV)), _full((1, LANES)),
                                         _full((1, LANES)), _full((1, D_SSM)), _full((1, D_SSM)), _full((1, D_SSM)),
                                         row(D_CONV)]
        + [anyspec] * nx,
        out_specs=[row(3072), _full((8, D_CONV)), _full((8, D_SSM)), _full((8, LANES))] + [anyspec] * nx,
        out_shape=[SDS((s, 3072), bf16), SDS((8, D_CONV), f32), SDS((8, D_SSM), f32), SDS((8, LANES), f32)]
        + [SDS(a.shape, a.dtype) for a in chip_sums],
        scratch_shapes=[pltpu.VMEM((D_STATE, D_SSM), f32), pltpu.VMEM((8, D_CONV), f32),
                        pltpu.VMEM((8 + CHUNK, D_CONV), f32),
                        pltpu.VMEM((CHUNK, D_SSM), f32), pltpu.VMEM((CHUNK, D_SSM), f32),
                        pltpu.VMEM((CHUNK, 2 * N_GROUPS * D_STATE), f32),
                        pltpu.VMEM((LANES, 2 * D_SSM), bf16), pltpu.VMEM((D_SSM, LANES), bf16)]
        + (_chip_exchange_scratch(nx) if nx else []),
        compiler_params=pltpu.CompilerParams(dimension_semantics=("arbitrary",)),
    )(proj, proj, proj, proj, proj, proj, y, states, dmix, conv_w, conv_b, dtb16, alog16, alog_f, d_f, nw, cv,
      *chip_sums)
    return outs[0], outs[1], outs[2], outs[3], outs[4:]


def _col_blocks(parts, tile):
    counts = [p.shape[1] // tile for p in parts]
    offs = [sum(counts[:t]) for t in range(len(parts))]
    return offs, counts, sum(counts)


def _bcast_copies(src_ref, out_ref, send_sems, recv_sems, local_sem):
    x, y, c = _my_pos()
    me = 4 * x + 2 * y + c
    mine = pltpu.make_async_copy(src_ref, out_ref.at[me], local_sem)
    sends, recvs = [], []
    for k in range(1, N_DEV):
        to, frm = (me + k) % N_DEV, (me + N_DEV - k) % N_DEV
        sems = dict(send_sem=send_sems.at[k - 1], recv_sem=recv_sems.at[k - 1], device_id_type=MESH)
        sends.append(pltpu.make_async_remote_copy(
            src_ref=src_ref, dst_ref=out_ref.at[me], device_id=(to // 4, (to // 2) % 2, to % 2), **sems))
        recvs.append(pltpu.make_async_remote_copy(
            src_ref=src_ref, dst_ref=out_ref.at[frm], device_id=(x, y, c), **sems))
    return mine, sends, recvs


def _bcast_scratch():
    return [pltpu.SemaphoreType.DMA((N_DEV - 1,)), pltpu.SemaphoreType.DMA((N_DEV - 1,)), pltpu.SemaphoreType.DMA(())]


def _inproj_bwd(dparts, wt, x, nw, dres, chip_sums=(), pack=None):
    s, d = x.shape
    tm, tk = 1024, 1024
    offs, counts, nk = _col_blocks(dparts, tk)
    npart, nx = len(dparts), len(chip_sums)
    npk = 0 if pack is None else 1
    ni = s // tm

    def body(*refs):
        dp_refs = refs[:npart]
        w_ref, x_ref, nw_ref, dres_ref = refs[npart:npart + 4]
        pos = npart + 4
        cs_in, pos = refs[pos:pos + nx], pos + nx
        pack_in, pos = refs[pos:pos + npk], pos + npk
        (gx_ref, gnw_ref), pos = refs[pos:pos + 2], pos + 2
        cs_out, pos = refs[pos:pos + nx], pos + nx
        pack_out, pos = refs[pos:pos + 2 * npk], pos + 2 * npk
        acc, pos = refs[pos], pos + 1
        cs_sems, pos = refs[pos:pos + 3 * min(nx, 1)], pos + 3 * min(nx, 1)
        pk_refs = refs[pos:]
        i, k = pl.program_id(0), pl.program_id(1)

        def exchange():
            return _chip_exchange_copies(cs_in, cs_out, *cs_sems)

        def pack_copies():
            return _bcast_copies(pack_in[0], pack_out[0], *pk_refs[1:4])

        def gnw_copies():
            return _bcast_copies(pk_refs[0], pack_out[1], *pk_refs[4:7])

        @pl.when(jnp.logical_and(i == 0, k == 0))
        def _():
            gnw_ref[...] = jnp.zeros_like(gnw_ref)
            if nx:
                mine, sends, _ = exchange()
                for cp in mine + sends:
                    cp.start()
            if npk:
                mine, sends, _ = pack_copies()
                for cp in [mine] + sends:
                    cp.start()

        @pl.when(k == 0)
        def _():
            acc[...] = _nn(dp_refs[0][...], w_ref[...])

        for t in range(npart):
            @pl.when(jnp.logical_and(k >= max(offs[t], 1), k < offs[t] + counts[t]))
            def _(t=t):
                acc[...] += _nn(dp_refs[t][...], w_ref[...])

        @pl.when(k == nk - 1)
        def _():
            xv = x_ref[...]
            r = lax.rsqrt(jnp.mean(xv * xv, axis=-1, keepdims=True) + EPS)
            xn = xv * r
            du = acc[...]
            gnw_ref[0:1, :] += jnp.sum(du * xn, axis=0, keepdims=True)
            dn = du * nw_ref[...]
            gx_ref[...] = dres_ref[...] + r * (dn - xn * jnp.mean(dn * xn, axis=-1, keepdims=True))

        @pl.when(jnp.logical_and(i == ni - 1, k == nk - 1))
        def _():
            if npk:
                pk_refs[0][...] = gnw_ref[...]
                mine, sends, _ = gnw_copies()
                for cp in [mine] + sends:
                    cp.start()
            if nx:
                mine, sends, recvs = exchange()
                for cp in recvs:
                    cp.wait_recv()
                for cp in sends:
                    cp.wait_send()
                for cp in mine:
                    cp.wait()
            if npk:
                for copies in (pack_copies(), gnw_copies()):
                    mine, sends, recvs = copies
                    for cp in recvs:
                        cp.wait_recv()
                    for cp in sends:
                        cp.wait_send()
                    mine.wait()

    def piece(t):
        return pl.BlockSpec((tm, tk), lambda i, k: (i, jnp.clip(k - offs[t], 0, counts[t] - 1)))

    anyspec = pl.BlockSpec(memory_space=pl.ANY)
    packs = [] if pack is None else [pack]
    pack_shapes = [] if pack is None else [SDS((N_DEV,) + pack.shape, f32), SDS((N_DEV, 8, d), f32)]
    scratch = [pltpu.VMEM((tm, d), f32)] + (_chip_exchange_scratch(nx) if nx else [])
    if npk:
        scratch += [pltpu.VMEM((8, d), f32)] + _bcast_scratch() + _bcast_scratch()
    outs = pl.pallas_call(
        body, name="inproj_bwd", grid=(ni, nk),
        in_specs=[piece(t) for t in range(npart)] + [
            pl.BlockSpec((tk, d), lambda i, k: (k, 0)),
            pl.BlockSpec((tm, d), lambda i, k: (i, 0)), pl.BlockSpec((1, d), lambda i, k: (0, 0)),
            pl.BlockSpec((tm, d), lambda i, k: (i, 0))] + [anyspec] * (nx + npk),
        out_specs=[pl.BlockSpec((tm, d), lambda i, k: (i, 0)), pl.BlockSpec((8, d), lambda i, k: (0, 0))]
        + [anyspec] * (nx + 2 * npk),
        out_shape=[SDS((s, d), f32), SDS((8, d), f32)] + [SDS(a.shape, a.dtype) for a in chip_sums] + pack_shapes,
        scratch_shapes=scratch,
        compiler_params=pltpu.CompilerParams(dimension_semantics=("arbitrary", "arbitrary")),
    )(*dparts, wt, x, nw, dres, *chip_sums, *packs)
    return outs[0], outs[1], outs[2:2 + nx], outs[2 + nx:]


def _matmul_tn(a_parts, b_parts, name):
    tile, tk = 1024, 1024
    s = a_parts[0].shape[0]
    nk = s // tk
    na, nb = len(a_parts), len(b_parts)
    offs_a, counts_a, ni = _col_blocks(a_parts, tile)
    offs_b, counts_b, nj = _col_blocks(b_parts, tile)

    def body(*refs):
        a_refs, b_refs, o_ref = refs[:na], refs[na:na + nb], refs[na + nb]
        i, j = pl.program_id(0), pl.program_id(1)

        @pl.when(pl.program_id(2) == 0)
        def _():
            o_ref[...] = jnp.zeros_like(o_ref)

        for ta in range(na):
            for tb in range(nb):
                in_a = jnp.logical_and(i >= offs_a[ta], i < offs_a[ta] + counts_a[ta])
                in_b = jnp.logical_and(j >= offs_b[tb], j < offs_b[tb] + counts_b[tb])

                @pl.when(jnp.logical_and(in_a, in_b))
                def _(ta=ta, tb=tb):
                    o_ref[...] += _tn(a_refs[ta][...], b_refs[tb][...])

    def spec(offs, counts, t, axis):
        def index(i, j, k):
            pos = (i, j)[axis]
            mine = jnp.logical_and(pos >= offs[t], pos < offs[t] + counts[t])
            return jnp.where(mine, k, 0), jnp.clip(pos - offs[t], 0, counts[t] - 1)
        return pl.BlockSpec((tk, tile), index)

    return pl.pallas_call(
        body, name=name, grid=(ni, nj, nk),
        in_specs=[spec(offs_a, counts_a, t, 0) for t in range(na)] + [spec(offs_b, counts_b, t, 1) for t in range(nb)],
        out_specs=pl.BlockSpec((tile, tile), lambda i, j, k: (i, j)),
        out_shape=SDS((ni * tile, nj * tile), f32),
        compiler_params=pltpu.CompilerParams(dimension_semantics=("parallel", "parallel", "arbitrary")),
    )(*a_parts, *b_parts)


def _adamw(w, g, m, v):
    m = ADAM_B1 * m + (1.0 - ADAM_B1) * g
    v = ADAM_B2 * v + (1.0 - ADAM_B2) * (g * g)
    m_hat = m / (1.0 - ADAM_B1 ** ADAM_STEP)
    v_hat = v / (1.0 - ADAM_B2 ** ADAM_STEP)
    delta = -ADAM_LR * (m_hat / (jnp.sqrt(v_hat) + ADAM_EPS) + ADAM_WD * w)
    return delta, m, v


def _sum_adamw(parts, w, m, v, name):
    r, c = w.shape
    tc = 256

    def body(p_ref, w_ref, m_ref, v_ref, g_ref, d_ref, nm_ref, nv_ref):
        g = p_ref[0].astype(f32)
        for q in range(1, 4):
            g = g + p_ref[q].astype(f32)
        g_ref[...] = g
        d_ref[...], nm_ref[...], nv_ref[...] = _adamw(w_ref[...], g, m_ref[...], v_ref[...])

    blk = pl.BlockSpec((r, tc), lambda i: (0, i))
    return pl.pallas_call(
        body, name=name, grid=(c // tc,),
        in_specs=[pl.BlockSpec((4, r, tc), lambda i: (0, 0, i)), blk, blk, blk],
        out_specs=[blk] * 4, out_shape=[SDS((r, c), f32)] * 4,
        compiler_params=pltpu.CompilerParams(dimension_semantics=("parallel",)),
    )(parts, w, m, v)


def _sum_small(parts, pre_blocks):
    def body(p_ref, b_ref, o_ref):
        t = p_ref[0]
        pre = b_ref[0]
        for j in range(1, N_DEV):
            t = t + p_ref[j]
            pre = pre + b_ref[j]
        o_ref[...] = t
        o_ref[5:6, 0:D_MODEL] = pre[0:1, :]
        row_h = _iota((D_SSM, LANES), 0) // HEAD_DIM
        fold = (row_h == _iota((D_SSM, LANES), 1)).astype(f32)
        lower = t[8:16, 0:LANES]
        folded = _nn_hi(t[8:16, 0:D_SSM], fold)
        loss = jnp.sum(t[11:12, 0:D_MODEL], axis=1, keepdims=True) * (0.5 / D_MODEL)
        row = _iota((8, LANES), 0)
        o_ref[8:16, 0:LANES] = jnp.where(row < 2, folded, jnp.where(row == 4, loss, lower))

    return pl.pallas_call(body, name="sum_small", out_shape=SDS((PACK_ROWS, PACK_W), f32),
                          in_specs=[pl.BlockSpec(memory_space=pltpu.VMEM)] * 2,
                          out_specs=pl.BlockSpec(memory_space=pltpu.VMEM))(parts, pre_blocks)


def _adamw_small(w, g, m, v):
    def body(w_ref, g_ref, m_ref, v_ref, d_ref, nm_ref, nv_ref):
        d_ref[...], nm_ref[...], nv_ref[...] = _adamw(w_ref[...], g_ref[...], m_ref[...], v_ref[...])

    vm = pl.BlockSpec(memory_space=pltpu.VMEM)
    return pl.pallas_call(body, name="adamw_small", out_shape=[SDS(w.shape, f32)] * 3,
                          in_specs=[vm] * 4, out_specs=[vm] * 3)(w, g, m, v)


def _pad_lanes(v, width):
    return jnp.pad(v, ((0, 0), (0, width - v.shape[1])))


def _local_step(x, tgt, norm_pre_w, wt, conv_w, conv_b, dt_bias, a_log, d_skip, ssm_norm_w, wo, norm_post_w, sharded):
    dtb16 = _pad_lanes(dt_bias, LANES)
    alog16 = _pad_lanes(a_log, LANES)
    alog_f = jnp.repeat(a_log, HEAD_DIM, axis=1)
    d_f = jnp.repeat(d_skip, HEAD_DIM, axis=1)

    shard_out = wo.shape[0]
    if sharded:
        proj, u, (g_out, g_cw) = _prenorm_inproj(x, norm_pre_w, wt, gather=(wo, conv_w))
        wo = g_out.reshape(N_DEV * shard_out, D_MODEL)
        conv_w = g_cw.transpose(1, 0, 2).reshape(4, D_CONV)
    else:
        proj, u, _ = _prenorm_inproj(x, norm_pre_w, wt)
    o, lb, mix_a = _attn_fwd(proj)
    mix_s, y, states, cv = _ssd_fwd(proj, conv_w, conv_b, dtb16, alog16, alog_f, d_f, ssm_norm_w)
    dmix, dres, acc_post, dw_out = _outproj_loss(mix_a, mix_s, wo, x, tgt, norm_post_w)
    ssd_args = (proj, y, states, cv, dmix, conv_w, conv_b, dtb16, alog16, alog_f, d_f, ssm_norm_w)
    if sharded:
        dq, dk, dv, dg, got_out = _attn_bwd(proj, o, lb, dmix, swap=dw_out)
        chip_out = _chip_sum(dw_out, got_out, shard_out, "chip_sum_w_out")
        dzxd, g_conv, g_vec, g_dt, (parts_out,) = _ssd_bwd(*ssd_args, chip_sums=[chip_out])
    else:
        dq, dk, dv, dg = _attn_bwd(proj, o, lb, dmix)
        dzxd, g_conv, g_vec, g_dt, _ = _ssd_bwd(*ssd_args)
    dparts = [dq, dk, dv, dg, dzxd]

    def pack(g_pre_row):
        return jnp.concatenate(
            [g_conv[0:5], g_pre_row, _pad_lanes(g_vec[0:1], PACK_W), _pad_lanes(acc_post[1:2], PACK_W),
             _pad_lanes(g_vec[1:3], PACK_W), _pad_lanes(g_dt[0:1], PACK_W), _pad_lanes(acc_post[0:1], PACK_W),
             jnp.zeros((4, PACK_W), f32)], axis=0)

    if sharded:
        dw_in, got_in = _dw_in_swap(dparts, u)
        chip_in = _chip_sum(dw_in, got_in, D_IN_PROJ // N_DEV, "chip_sum_w_in")
        grad_x, _, (parts_in,), small = _inproj_bwd(dparts, wt, x, norm_pre_w, dres, [chip_in],
                                                    pack(jnp.zeros((1, PACK_W), f32)))
        return grad_x, (parts_in, parts_out), small
    dw_in = _matmul_tn(dparts, [u], "dw_in")
    grad_x, g_pre, _, _ = _inproj_bwd(dparts, wt, x, norm_pre_w, dres)
    return grad_x, (dw_in, dw_out), pack(_pad_lanes(g_pre[0:1], PACK_W))


def kernel(x, norm_pre_w, w_in, conv_w, conv_b, dt_bias, a_log, d_skip, ssm_norm_w, w_out, norm_post_w, loss_target, m_norm_pre_w, m_w_in, m_conv_w, m_conv_b, m_dt_bias, m_a_log, m_d_skip, m_ssm_norm_w, m_w_out, m_norm_post_w, v_norm_pre_w, v_w_in, v_conv_w, v_conv_b, v_dt_bias, v_a_log, v_d_skip, v_ssm_norm_w, v_w_out, v_norm_post_w):
    shard_in = w_in.shape[2]
    shard_cv = conv_w.shape[2]
    me = 4 * lax.axis_index("x") + 2 * lax.axis_index("y") + lax.axis_index("c")

    g_in, = _all_gather([w_in[0].T.astype(bf16)])
    wt = _assemble_wt(g_in)

    grad_x, (parts_in, parts_out), (parts_small, pre_blocks) = _local_step(
        x[0], loss_target[0], norm_pre_w, wt, conv_w[0], conv_b, dt_bias, a_log, d_skip, ssm_norm_w,
        w_out[0].astype(bf16), norm_post_w, sharded=True)

    g_w_in, d_w_in, nm_w_in, nv_w_in = (a.T for a in _sum_adamw(
        parts_in, w_in[0].T, m_w_in[0].T, v_w_in[0].T, "sum_adamw_w_in"))
    g_w_out, d_w_out, nm_w_out, nv_w_out = _sum_adamw(parts_out, w_out[0], m_w_out[0], v_w_out[0], "sum_adamw_w_out")
    tot = _sum_small(parts_small, pre_blocks)

    g_cw_all = tot[0:4]
    small_g = {
        "conv_w": lax.dynamic_slice(g_cw_all, (0, me * shard_cv), (4, shard_cv)),
        "conv_b": tot[4:5], "norm_pre_w": tot[5:6, :D_MODEL], "ssm_norm_w": tot[6:7, :D_SSM],
        "norm_post_w": tot[7:8, :D_MODEL], "a_log": tot[8:9, :16], "d_skip": tot[9:10, :16], "dt_bias": tot[10:11, :16],
    }
    loss = tot[12, 0]
    small_w = {"conv_w": (conv_w[0], m_conv_w[0], v_conv_w[0]), "conv_b": (conv_b, m_conv_b, v_conv_b),
               "norm_pre_w": (norm_pre_w, m_norm_pre_w, v_norm_pre_w), "ssm_norm_w": (ssm_norm_w, m_ssm_norm_w, v_ssm_norm_w),
               "norm_post_w": (norm_post_w, m_norm_post_w, v_norm_post_w), "a_log": (a_log, m_a_log, v_a_log),
               "d_skip": (d_skip, m_d_skip, v_d_skip), "dt_bias": (dt_bias, m_dt_bias, v_dt_bias)}
    names = list(small_w)
    sizes = [small_g[k].size for k in names]
    tot_size = sum(sizes)
    pad_to = -(-tot_size // 1024) * 1024

    def flat(arrs):
        v = jnp.concatenate([a.reshape(-1) for a in arrs])
        return jnp.pad(v, (0, pad_to - tot_size)).reshape(pad_to // LANES, LANES)

    fw = flat([small_w[k][0] for k in names])
    fg = flat([small_g[k] for k in names])
    fm = flat([small_w[k][1] for k in names])
    fv = jnp.pad(jnp.concatenate([small_w[k][2].reshape(-1) for k in names]), (0, pad_to - tot_size),
                 constant_values=1.0).reshape(pad_to // LANES, LANES)
    fd, fnm, fnv = _adamw_small(fw, fg, fm, fv)

    def unflat(f):
        out, off = {}, 0
        v = f.reshape(-1)
        for k, n in zip(names, sizes):
            out[k] = v[off:off + n].reshape(small_g[k].shape)
            off += n
        return out

    sd, snm, snv = unflat(fd), unflat(fnm), unflat(fnv)
    lead = lambda a: a[None]
    order = ["norm_pre_w", "w_in", "conv_w", "conv_b", "dt_bias", "a_log", "d_skip", "ssm_norm_w", "w_out", "norm_post_w"]
    grads = dict(small_g, w_in=g_w_in, w_out=g_w_out)
    deltas = dict(sd, w_in=d_w_in, w_out=d_w_out)
    new_m = dict(snm, w_in=nm_w_in, w_out=nm_w_out)
    new_v = dict(snv, w_in=nv_w_in, w_out=nv_w_out)

    def shaped(dct, k):
        a = dct[k]
        return lead(a) if k in ("w_in", "w_out", "conv_w") else a

    return (loss, grad_x[None], *[shaped(grads, k) for k in order], *[shaped(deltas, k) for k in order],
            *[shaped(new_m, k) for k in order], *[shaped(new_v, k) for k in order])
```

```python
import jax
import jax.numpy as jnp
from jax import lax
from jax.experimental import pallas as pl
from jax.experimental.pallas import tpu as pltpu

f32, bf16 = jnp.float32, jnp.bfloat16
SDS = jax.ShapeDtypeStruct
HIGHEST = lax.Precision.HIGHEST
MESH = pl.DeviceIdType.MESH

N_DEV = 8
D_MODEL = 1024
D_ATTN = 1024
D_SSM = 1024
HEAD_DIM = 64
N_PAIRS = 8
D_STATE = 128
N_GROUPS = 2
D_CONV = D_SSM + 2 * N_GROUPS * D_STATE
D_IN_PROJ = 4 * D_ATTN + D_SSM + D_CONV + 16
NP = 7168
CHUNK = 128
BLK = 128
DILATIONS = (1, 4, 16)
EPS = 1e-6
LANES = 128
COL_Z, COL_XS, COL_BC, COL_DT = 4096, 5120, 6144, 6656

ADAM_LR, ADAM_B1, ADAM_B2, ADAM_EPS, ADAM_WD, ADAM_STEP = 0.001, 0.9, 0.999, 1e-08, 0.01, 10

PACK_ROWS, PACK_W = 16, 1536


def _nt(a, b):
    return lax.dot_general(a, b, (((1,), (1,)), ((), ())), preferred_element_type=f32)


def _tn(a, b):
    return lax.dot_general(a, b, (((0,), (0,)), ((), ())), preferred_element_type=f32)


def _nn(a, b):
    return jnp.dot(a, b, preferred_element_type=f32)


def _nn_hi(a, b):
    return jnp.dot(a, b, precision=HIGHEST, preferred_element_type=f32)


def _sigmoid(x):
    return 1.0 / (1.0 + jnp.exp(-x))


def _softplus(x):
    return jnp.maximum(x, 0.0) + jnp.log1p(jnp.exp(-jnp.abs(x)))


def _iota(shape, dim):
    return lax.broadcasted_iota(jnp.int32, shape, dim)


def _my_pos():
    return lax.axis_index("x"), lax.axis_index("y"), lax.axis_index("c")


GATHER_SEMS = 9


def _gather_phases(ins, outs, send_sems, recv_sems, local_sems):
    n, ns = len(ins), GATHER_SEMS
    x, y, c = _my_pos()
    me, sibling = (x, y, c), (x, y, 1 - c)
    xn, yn, diag = (1 - x, y), (x, 1 - y), (1 - x, 1 - y)

    def slot(a, px, py, pc):
        return outs[a].at[4 * px + 2 * py + pc]

    def part(a, ref, h):
        width = ins[a].shape[-1]
        if width % (2 * LANES):
            return ref if h == 1 else None
        return ref.at[:, pl.ds(h * (width // 2), width // 2)]

    def copy(a, k, block, to, src=None, h=None):
        src_ref = slot(a, *block) if src is None else src
        dst_ref = slot(a, *block)
        if h is not None:
            src_ref, dst_ref = part(a, src_ref, h), part(a, dst_ref, h)
            if src_ref is None:
                return None
        return pltpu.make_async_remote_copy(
            src_ref=src_ref, dst_ref=dst_ref, send_sem=send_sems.at[ns * a + k], recv_sem=recv_sems.at[ns * a + k],
            device_id=to, device_id_type=MESH)

    def mine():
        return [pltpu.make_async_copy(ins[a], slot(a, *me), local_sems.at[a]) for a in range(n)]

    def own_sends(a):
        return [copy(a, 0, me, sibling, src=ins[a]), copy(a, 1, me, (*xn, c), src=ins[a]),
                copy(a, 2, me, (*yn, c), src=ins[a])]

    def neighbour_relays(a):
        return [copy(a, 4, (*xn, c), sibling), copy(a, 7, (*xn, c), (*yn, c), h=1),
                copy(a, 5, (*yn, c), sibling), copy(a, 8, (*yn, c), (*xn, c), h=0)]

    def diagonal_halves(a):
        return [copy(a, k, (*diag, c), me, h=h) for k, h in ((8, 0), (7, 1))]

    def start_all(cps):
        for cp in cps:
            if cp is not None:
                cp.start()

    def phase0():
        start_all(mine())
        for a in range(n):
            start_all(own_sends(a))

    def phase1():
        for a in range(n):
            copy(a, 1, (*xn, c), me).wait_recv()
            copy(a, 2, (*yn, c), me).wait_recv()
            start_all(neighbour_relays(a))

    def phase2():
        for a in range(n):
            for cp in diagonal_halves(a):
                if cp is not None:
                    cp.wait_recv()
            copy(a, 6, (*diag, c), sibling).start()

    def finish():
        for a in range(n):
            copy(a, 0, sibling, me).wait_recv()
            for j, chip in enumerate((xn, yn, diag)):
                copy(a, 4 + j, (*chip, 1 - c), me).wait_recv()
        for a in range(n):
            for cp in own_sends(a) + neighbour_relays(a) + [copy(a, 6, (*diag, c), sibling)]:
                if cp is not None:
                    cp.wait_send()
        for cp in mine():
            cp.wait()

    return phase0, phase1, phase2, finish


def _gather_scratch(n):
    return [pltpu.SemaphoreType.DMA((GATHER_SEMS * n,)), pltpu.SemaphoreType.DMA((GATHER_SEMS * n,)),
            pltpu.SemaphoreType.DMA((n,))]


def _all_gather(arrs):
    n = len(arrs)

    def body(*refs):
        for phase in _gather_phases(refs[:n], refs[n:2 * n], *refs[2 * n:]):
            phase()

    anyspec = pl.BlockSpec(memory_space=pl.ANY)
    return pl.pallas_call(
        body, name="weights_all_gather",
        out_shape=[SDS((N_DEV,) + a.shape, a.dtype) for a in arrs],
        in_specs=[anyspec] * n, out_specs=[anyspec] * n, scratch_shapes=_gather_scratch(n),
    )(*arrs)


def _dw_in_swap(a_parts, u):
    tile, tk = 1024, 1024
    s = u.shape[0]
    nk = s // tk
    na = len(a_parts)
    offs, counts, ni = _col_blocks(a_parts, tile)

    def body(*refs):
        a_refs, u_ref = refs[:na], refs[na]
        dw_ref, got_ref = refs[na + 1:na + 3]
        acc, stage, local_sems, send_sems, recv_sem = refs[na + 3:]
        i, k = pl.program_id(0), pl.program_id(1)
        x, y, c = _my_pos()
        par = i % 2

        def tile_copies(t, p):
            rows = pl.ds(pl.multiple_of(t * tile, tile), tile)
            loc = pltpu.make_async_copy(stage.at[p], dw_ref.at[rows], local_sems.at[p])
            rem = pltpu.make_async_remote_copy(
                src_ref=stage.at[p], dst_ref=got_ref.at[rows], send_sem=send_sems.at[p], recv_sem=recv_sem,
                device_id=(x, y, 1 - c), device_id_type=MESH)
            return loc, rem

        @pl.when(k == 0)
        def _():
            acc[...] = jnp.zeros((tile, tile), f32)

        for t in range(na):
            @pl.when(jnp.logical_and(i >= offs[t], i < offs[t] + counts[t]))
            def _(t=t):
                acc[...] += _tn(a_refs[t][...], u_ref[pl.ds(pl.multiple_of(k * tk, tk), tk), :])

        @pl.when(k == nk - 1)
        def _():
            @pl.when(i >= 2)
            def _():
                loc, rem = tile_copies(i - 2, par)
                loc.wait()
                rem.wait_send()
            stage[par] = acc[...]
            loc, rem = tile_copies(i, par)
            loc.start()
            rem.start()

        @pl.when(jnp.logical_and(i == ni - 1, k == nk - 1))
        def _():
            for t in (ni - 2, ni - 1):
                loc, rem = tile_copies(t, t % 2)
                loc.wait()
                rem.wait_send()
            pltpu.make_async_remote_copy(src_ref=dw_ref, dst_ref=got_ref, send_sem=send_sems.at[0], recv_sem=recv_sem,
                                         device_id=(x, y, c), device_id_type=MESH).wait_recv()

    def a_spec(t):
        def index(i, k):
            mine = jnp.logical_and(i >= offs[t], i < offs[t] + counts[t])
            return jnp.where(mine, k, 0), jnp.clip(i - offs[t], 0, counts[t] - 1)
        return pl.BlockSpec((tk, tile), index)

    anyspec = pl.BlockSpec(memory_space=pl.ANY)
    return pl.pallas_call(
        body, name="dw_in_swap", grid=(ni, nk),
        in_specs=[a_spec(t) for t in range(na)] + [pl.BlockSpec((s, tile), lambda i, k: (0, 0))],
        out_specs=[anyspec] * 2,
        out_shape=[SDS((ni * tile, tile), f32), SDS((ni * tile, tile), f32)],
        scratch_shapes=[pltpu.VMEM((tile, tile), f32), pltpu.VMEM((2, tile, tile), f32), pltpu.SemaphoreType.DMA((2,)),
                        pltpu.SemaphoreType.DMA((2,)), pltpu.SemaphoreType.DMA(())],
        compiler_params=pltpu.CompilerParams(dimension_semantics=("arbitrary", "arbitrary")),
    )(*a_parts, u)


def _chip_sum(mine, got, rows, name):
    r, cdim = mine.shape
    tc = LANES

    def body(m_ref, g_ref, s16_ref):
        c = lax.axis_index("c")
        for q in range(4):
            blk = pl.ds(rows * (2 * q + c), rows)
            s16_ref[q] = (m_ref[blk, :] + g_ref[blk, :]).astype(bf16)

    col = pl.BlockSpec((r, tc), lambda i: (0, i))
    return pl.pallas_call(
        body, name=name, grid=(cdim // tc,), in_specs=[col, col],
        out_specs=pl.BlockSpec((4, rows, tc), lambda i: (0, 0, i)), out_shape=SDS((4, rows, cdim), bf16),
        compiler_params=pltpu.CompilerParams(dimension_semantics=("parallel",)),
    )(mine, got)


def _assemble_wt(shards):
    nd, rows, cdim = shards.shape
    tc = 256

    def body(g_ref, o_ref):
        for j in range(nd):
            o_ref[pl.ds(rows * j, rows), :] = g_ref[j]
        o_ref[pl.ds(nd * rows, NP - nd * rows), :] = jnp.zeros((NP - nd * rows, tc), shards.dtype)

    return pl.pallas_call(
        body, name="assemble_w_in", grid=(cdim // tc,),
        in_specs=[pl.BlockSpec((nd, rows, tc), lambda i: (0, 0, i))],
        out_specs=pl.BlockSpec((NP, tc), lambda i: (0, i)), out_shape=SDS((NP, cdim), shards.dtype),
        compiler_params=pltpu.CompilerParams(dimension_semantics=("parallel",)),
    )(shards)


def _chip_exchange_copies(ins, outs, send_sems, recv_sems, local_sems):
    nb = len(ins)
    x, y, c = _my_pos()
    my_q = 2 * x + y
    mine = [pltpu.make_async_copy(ins[a].at[my_q], outs[a].at[my_q], local_sems.at[a]) for a in range(nb)]
    sends, recvs = [], []
    for k in range(1, 4):
        to, frm = (my_q + k) % 4, (my_q + 4 - k) % 4
        for a in range(nb):
            sems = dict(send_sem=send_sems.at[3 * a + k - 1], recv_sem=recv_sems.at[3 * a + k - 1], device_id_type=MESH)
            sends.append(pltpu.make_async_remote_copy(
                src_ref=ins[a].at[to], dst_ref=outs[a].at[my_q], device_id=(to // 2, to % 2, c), **sems))
            recvs.append(pltpu.make_async_remote_copy(
                src_ref=ins[a].at[frm], dst_ref=outs[a].at[frm], device_id=(x, y, c), **sems))
    return mine, sends, recvs


def _chip_exchange_scratch(nb):
    return [pltpu.SemaphoreType.DMA((3 * nb,)), pltpu.SemaphoreType.DMA((3 * nb,)), pltpu.SemaphoreType.DMA((nb,))]


def _prenorm_inproj(x, nw, wt, gather=()):
    s, d = x.shape
    npad = wt.shape[0]
    tm, tn = 1024, 1024
    ng = len(gather)
    ni, nj = s // tm, npad // tn

    def body(x_ref, nw_ref, w_ref, *refs):
        g_in, (proj_ref, u_ref), g_out, sems = refs[:ng], refs[ng:ng + 2], refs[ng + 2:2 * ng + 2], refs[2 * ng + 2:]
        i, j = pl.program_id(0), pl.program_id(1)
        if ng:
            phases = _gather_phases(g_in, g_out, *sems)
            for step, phase in enumerate(phases[:3]):
                @pl.when(jnp.logical_and(i == step, j == 0))
                def _(phase=phase):
                    phase()

        @pl.when(j == 0)
        def _():
            xv = x_ref[...]
            r = lax.rsqrt(jnp.mean(xv * xv, axis=-1, keepdims=True) + EPS)
            u_ref[...] = (xv * r * nw_ref[...]).astype(bf16)
        proj_ref[...] = _nt(u_ref[...], w_ref[pl.ds(pl.multiple_of(j * tn, tn), tn), :])

        if ng:
            @pl.when(jnp.logical_and(i == ni - 1, j == nj - 1))
            def _():
                phases[3]()

    anyspec = pl.BlockSpec(memory_space=pl.ANY)
    outs = pl.pallas_call(
        body, name="prenorm_inproj", grid=(ni, nj),
        in_specs=[pl.BlockSpec((tm, d), lambda i, j: (i, 0)), pl.BlockSpec((1, d), lambda i, j: (0, 0)),
                  pl.BlockSpec((npad, d), lambda i, j: (0, 0))] + [anyspec] * ng,
        out_specs=[pl.BlockSpec((tm, tn), lambda i, j: (i, j)), pl.BlockSpec((tm, d), lambda i, j: (i, 0))]
        + [anyspec] * ng,
        out_shape=[SDS((s, npad), f32), SDS((s, d), bf16)] + [SDS((N_DEV,) + a.shape, a.dtype) for a in gather],
        scratch_shapes=_gather_scratch(ng) if ng else [],
        compiler_params=pltpu.CompilerParams(dimension_semantics=("arbitrary", "arbitrary")),
    )(x, nw, wt, *gather)
    return outs[0], outs[1], outs[2:]


def _attn_consts():
    head0 = _iota((BLK, LANES), 1) < HEAD_DIM
    tri2 = (_iota((BLK, 2 * LANES), 1) % LANES) <= _iota((BLK, 2 * LANES), 0)
    ones2 = ((_iota((LANES, 2 * LANES), 0) < HEAD_DIM) == (_iota((LANES, 2 * LANES), 1) < LANES)).astype(bf16)
    rmat = ((_iota((2 * LANES, LANES), 0) < LANES) == (_iota((2 * LANES, LANES), 1) < HEAD_DIM)).astype(bf16)
    bones = ((_iota((LANES, LANES), 0) < HEAD_DIM) == (_iota((LANES, LANES), 1) < HEAD_DIM)).astype(bf16)
    return head0, tri2, ones2, rmat, bones


def _stack_heads(x16, head0):
    zero = jnp.zeros_like(x16)
    return jnp.concatenate([jnp.where(head0, x16, zero), jnp.where(head0, zero, x16)], axis=0)


def _bf16_terms(x, terms):
    out = []
    for _ in range(terms):
        t = x.astype(bf16)
        out.append(t)
        x = x - t.astype(f32)
    return out


def _dot_01(x, w16, terms):
    return _nn(jnp.concatenate(_bf16_terms(x, terms), axis=1), jnp.concatenate([w16] * terms, axis=0))


def _split_dot_sum(x, w16):
    hi, lo = _bf16_terms(x, 2)
    return _nn(hi, w16) + _nn(lo, w16)


def _dot_01_left(w16, x, terms):
    return _nn(jnp.concatenate([w16] * terms, axis=1), jnp.concatenate(_bf16_terms(x, terms), axis=0))


def _attn_fwd(proj):
    s = proj.shape[0]
    n_it = s // BLK

    def body(q_ref, k_ref, v_ref, g_ref, o_ref, l_ref, mix_ref, op0, op1, op2, lp0, lp1, lp2,
             s_a, s_b, sd_a, sd_b, p_a, p_b, m_a, m_b, pd_a, pd_b, k_a, k_b, v_a, v_b):
        op_refs, lp_refs = (op0, op1, op2), (lp0, lp1, lp2)
        head0, tri2, ones2, rmat, _ = _attn_consts()
        score_bufs, prob_bufs = ((s_a, sd_a), (s_b, sd_b)), ((p_a, m_a, pd_a), (p_b, m_b, pd_b))
        k_bufs, v_bufs = (k_a, k_b), (v_a, v_b)
        for buf in k_bufs + v_bufs:
            buf[...] = jnp.zeros_like(buf)

        def block_rows(i, d, nb):
            r, blk = i // nb, i % nb
            return pl.ds(blk * (BLK * d) + r, BLK, stride=d), blk > 0

        def unstack(st16):
            return st16[:BLK] + st16[BLK:]

        def scores(i, par, d, nb):
            rows, has_prev = block_rows(i, d, nb)
            s_buf, sd_buf = score_bufs[par]
            qs = q_ref[rows, :] * 0.125
            qs16 = qs.astype(bf16)
            kst_c = _stack_heads(k_ref[rows, :].astype(bf16), head0)
            kst_p = k_bufs[1 - par][...]
            k_bufs[par][...] = kst_c
            sc = _nt(qs16, kst_c)
            sp = _nt(qs16, kst_p)
            s_buf[...] = jnp.where(tri2, sc, jnp.where(has_prev, sp, -jnp.inf))
            sd = _nn((qs * unstack(kst_p).astype(f32)).astype(bf16), ones2)
            sd_buf[...] = jnp.where(has_prev, sd, -jnp.inf)

        def softmax(bufs_in, bufs_out):
            s_buf, sd_buf = bufs_in
            p_buf, m_buf, pd_buf = bufs_out
            sc, sd2 = s_buf[...], sd_buf[...]
            m0 = jnp.max(sc[:, :LANES], axis=1, keepdims=True)
            m1 = jnp.max(sc[:, LANES:], axis=1, keepdims=True)
            m2 = jnp.concatenate([jnp.broadcast_to(m0, (BLK, LANES)), jnp.broadcast_to(m1, (BLK, LANES))], axis=1)
            m2 = jnp.maximum(m2, sd2)
            p_buf[...] = jnp.exp(sc - m2).astype(bf16)
            m_pair = jnp.where(head0, m2[:, :LANES], m2[:, LANES:])
            m_buf[...] = m_pair
            pd_buf[...] = jnp.exp(jnp.where(head0, sd2[:, :LANES], sd2[:, LANES:]) - m_pair)

        def output(i, par, d, nb, p):
            rows, _ = block_rows(i, d, nb)
            p_buf, m_buf, pd_buf = prob_bufs[par]
            vst_c = _stack_heads(v_ref[rows, :].astype(bf16), head0)
            vst_p = v_bufs[1 - par][...]
            v_bufs[par][...] = vst_c
            pt16, pd = p_buf[...], pd_buf[...]
            zero = jnp.zeros_like(pt16)
            o = (_nn(jnp.where(tri2, pt16, zero), vst_c) + _nn(jnp.where(tri2, zero, pt16), vst_p)
                 + pd * unstack(vst_p).astype(f32))
            l = _nn(pt16, rmat) + pd
            op_refs[p][rows, :] = o / l
            lp_refs[p][rows, :] = m_buf[...] + jnp.log(l)

        for p, d in enumerate(DILATIONS):
            nb = s // (BLK * d)
            scores(0, 0, d, nb)
            scores(1, 1, d, nb)
            softmax(score_bufs[0], prob_bufs[0])

            def steps(j, carry, d=d, nb=nb, p=p):
                for par in range(2):
                    t = 2 * j + 2 + par
                    scores(t, par, d, nb)
                    output(t - 2, par, d, nb, p)
                    softmax(score_bufs[1 - par], prob_bufs[1 - par])
                return carry

            lax.fori_loop(0, (n_it - 2) // 2, steps, 0, unroll=5)
            output(n_it - 2, 0, d, nb, p)
            softmax(score_bufs[1], prob_bufs[1])
            output(n_it - 1, 1, d, nb, p)

        def merge(i, carry):
            rows = pl.ds(pl.multiple_of(i * 256, 256), 256)
            l0, l1, l2 = lp0[rows, :], lp1[rows, :], lp2[rows, :]
            m = jnp.maximum(jnp.maximum(l0, l1), l2)
            e0, e1, e2 = jnp.exp(l0 - m), jnp.exp(l1 - m), jnp.exp(l2 - m)
            z = e0 + e1 + e2
            o = (e0 * op0[rows, :] + e1 * op1[rows, :] + e2 * op2[rows, :]) / z
            o_ref[rows, :] = o
            l_ref[rows, :] = m + jnp.log(z)
            g = g_ref[rows, :]
            mix_ref[rows, :] = (o * (g * _sigmoid(g))).astype(bf16)
            return carry

        lax.fori_loop(0, s // 256, merge, 0)

    col = lambda base: pl.BlockSpec((s, LANES), lambda h: (0, base + h))
    return pl.pallas_call(
        body, name="attn_fwd", grid=(N_PAIRS,),
        in_specs=[col(0), col(8), col(16), col(24)],
        out_specs=[col(0), col(0), col(0)],
        out_shape=[SDS((s, D_ATTN), f32), SDS((s, D_ATTN), f32), SDS((s, D_ATTN), bf16)],
        scratch_shapes=[pltpu.VMEM((s, LANES), f32)] * 6 + [pltpu.VMEM((BLK, 2 * LANES), f32)] * 4
        + [pltpu.VMEM((BLK, 2 * LANES), bf16)] * 2 + [pltpu.VMEM((BLK, LANES), f32)] * 4
        + [pltpu.VMEM((2 * BLK, LANES), bf16)] * 4,
        compiler_params=pltpu.CompilerParams(dimension_semantics=("parallel",)),
    )(proj, proj, proj, proj)


def _expand_mat():
    colv = _iota((LANES, 2 * D_SSM), 1)
    head = 2 * ((colv % D_SSM) // LANES) + colv // D_SSM
    return (_iota((LANES, 2 * D_SSM), 0) == head).astype(bf16)


def _fold_mat():
    return (_iota((D_SSM, LANES), 0) // HEAD_DIM == _iota((D_SSM, LANES), 1)).astype(bf16)


def _conv(xs_ref, bc_ref, xs_tail, bc_tail, cw_ref, cb_ref, xpad, first):
    keep = jnp.where(first, 0.0, 1.0)
    xpad[0:8, 0:D_SSM] = xs_tail[...] * keep
    xpad[0:8, D_SSM:D_CONV] = bc_tail[...] * keep
    xpad[8:8 + CHUNK, 0:D_SSM] = xs_ref[...]
    xpad[8:8 + CHUNK, D_SSM:D_CONV] = bc_ref[...]
    xp = xpad[...]
    cv = cb_ref[...] + cw_ref[3:4, :] * xp[8:8 + CHUNK]
    for j in range(3):
        cv = cv + cw_ref[j:j + 1, :] * pltpu.roll(xp, 3 - j, 0)[8:8 + CHUNK]
    return cv


def _decay_terms(dt_ref, dtb_ref, alog16_ref, emat_ref):
    pre = dt_ref[...] + dtb_ref[...]
    dt16 = _softplus(pre)
    a16 = -jnp.exp(alog16_ref[...])
    sub, lane = _iota((CHUNK, CHUNK), 0), _iota((CHUNK, CHUNK), 1)
    tri = (sub >= lane).astype(f32)
    al16 = _nn_hi(tri, dt16 * a16)
    al_t = al16.T
    emat = emat_ref[...]
    dt_x = _dot_01(dt16, emat, 3)
    al_x = _dot_01(al16, emat, 3)
    lane_w = _iota((CHUNK, D_SSM), 1)
    even = (lane_w % LANES) < HEAD_DIM
    dt_f = jnp.where(even, dt_x[:, :D_SSM], dt_x[:, D_SSM:])
    al_f = jnp.where(even, al_x[:, :D_SSM], al_x[:, D_SSM:])
    return pre, dt_f, al_f, al_x, al_t


def _decay_mat(al_x, al_t, pair, h):
    sub, lane = _iota((CHUNK, CHUNK), 0), _iota((CHUNK, CHUNK), 1)
    col = al_x[:, h * D_SSM + pair * LANES: h * D_SSM + (pair + 1) * LANES]
    row = al_t[2 * pair + h: 2 * pair + h + 1, :]
    return jnp.exp(jnp.where(sub >= lane, col - row, -jnp.inf))


def _ssd_in_specs(order):
    blk = lambda w, cb: pl.BlockSpec((CHUNK, w), lambda i: (order(i), cb))
    tail = lambda w, cb: pl.BlockSpec((8, w), lambda i: (jnp.maximum(16 * order(i) - 1, 0), cb))
    return [blk(D_SSM, COL_XS // D_SSM), blk(512, COL_BC // 512), tail(D_SSM, COL_XS // D_SSM),
            tail(512, COL_BC // 512), blk(LANES, COL_DT // LANES), blk(D_SSM, COL_Z // D_SSM)]


def _full(shape):
    return pl.BlockSpec(shape, lambda i: (0,) * len(shape))


def _ssd_fwd(proj, conv_w, conv_b, dtb16, alog16, alog_f, d_f, nw):
    s = proj.shape[0]
    nc = s // CHUNK

    def body(xs_ref, bc_ref, xs_tail, bc_tail, dt_ref, z_ref, cw_ref, cb_ref, dtb_ref, alog16_ref, alogf_ref,
             df_ref, nw_ref, mix_ref, y_ref, st_ref, cv_ref, h_scr, xpad, y_scr, emat_ref):
        c = pl.program_id(0)

        @pl.when(c == 0)
        def _():
            h_scr[...] = jnp.zeros_like(h_scr)
            emat_ref[...] = _expand_mat()

        cv = _conv(xs_ref, bc_ref, xs_tail, bc_tail, cw_ref, cb_ref, xpad, c == 0)
        cv_ref[...] = cv
        xbc = cv * _sigmoid(cv)
        _, dt_f, al_f, al_x, al_t = _decay_terms(dt_ref, dtb_ref, alog16_ref, emat_ref)
        head0 = _iota((CHUNK, LANES), 1) < HEAD_DIM
        st_ref[...] = h_scr[...]
        for g in range(N_GROUPS):
            bm = xbc[:, D_SSM + g * D_STATE: D_SSM + (g + 1) * D_STATE].astype(bf16)
            cm = xbc[:, D_SSM + (N_GROUPS + g) * D_STATE: D_SSM + (N_GROUPS + g + 1) * D_STATE].astype(bf16)
            gmat = _nt(cm, bm)
            for pair in range(4 * g, 4 * g + 4):
                sl = slice(pair * LANES, (pair + 1) * LANES)
                xp, dtp, alp = xbc[:, sl], dt_f[:, sl], al_f[:, sl]
                xdt = xp * dtp
                xdt16 = xdt.astype(bf16)
                al_last = alp[CHUNK - 1:CHUNK, :]
                hp = h_scr[:, sl]
                y_off = jnp.exp(alp) * _nn(cm, hp.astype(bf16))
                yd = [_nn((gmat * _decay_mat(al_x, al_t, pair, h)).astype(bf16), xdt16) for h in range(2)]
                y_scr[:, sl] = jnp.where(head0, yd[0], yd[1]) + y_off + df_ref[:, sl] * xp
                st = _tn(bm, (jnp.exp(al_last - alp) * xdt).astype(bf16))
                h_scr[:, sl] = jnp.exp(al_last) * hp + st
        y = y_scr[...]
        y_ref[...] = y
        z = z_ref[...]
        yz = y * (z * _sigmoid(z))
        gw = D_SSM // N_GROUPS
        for g in range(N_GROUPS):
            part = yz[:, g * gw:(g + 1) * gw]
            r = lax.rsqrt(jnp.mean(part * part, axis=-1, keepdims=True) + EPS)
            mix_ref[:, g * gw:(g + 1) * gw] = (part * r * nw_ref[:, g * gw:(g + 1) * gw]).astype(bf16)

    order = lambda i: i
    row = lambda w: pl.BlockSpec((CHUNK, w), lambda i: (i, 0))
    return pl.pallas_call(
        body, name="ssd_fwd", grid=(nc,),
        in_specs=_ssd_in_specs(order) + [_full((4, D_CONV)), _full((1, D_CONV)), _full((1, LANES)), _full((1, LANES)),
                                         _full((1, D_SSM)), _full((1, D_SSM)), _full((1, D_SSM))],
        out_specs=[row(D_SSM), row(D_SSM), pl.BlockSpec((None, D_STATE, D_SSM), lambda i: (i, 0, 0)), row(D_CONV)],
        out_shape=[SDS((s, D_SSM), bf16), SDS((s, D_SSM), f32), SDS((nc, D_STATE, D_SSM), f32),
                   SDS((s, D_CONV), f32)],
        scratch_shapes=[pltpu.VMEM((D_STATE, D_SSM), f32), pltpu.VMEM((8 + CHUNK, D_CONV), f32),
                        pltpu.VMEM((CHUNK, D_SSM), f32), pltpu.VMEM((LANES, 2 * D_SSM), bf16)],
        compiler_params=pltpu.CompilerParams(dimension_semantics=("arbitrary",)),
    )(proj, proj, proj, proj, proj, proj, conv_w, conv_b, dtb16, alog16, alog_f, d_f, nw)


def _outproj_loss(mix_a, mix_s, wo, x, tgt, npw):
    s, d = x.shape
    tm = 512

    def body(ma_ref, ms_ref, wo_ref, x_ref, t_ref, npw_ref, dmix_ref, dres_ref, acc_ref, dwo_ref):
        @pl.when(pl.program_id(0) == 0)
        def _():
            acc_ref[...] = jnp.zeros_like(acc_ref)
            dwo_ref[...] = jnp.zeros_like(dwo_ref)

        out = _nn(ma_ref[...], wo_ref[0:D_ATTN, :]) + _nn(ms_ref[...], wo_ref[D_ATTN:, :])
        r = lax.rsqrt(jnp.mean(out * out, axis=-1, keepdims=True) + EPS)
        on = out * r
        diff = x_ref[...] + on * npw_ref[...] - t_ref[...]
        dres = diff * (1.0 / d)
        dres_ref[...] = dres
        acc_ref[0:1, :] += jnp.sum(diff * diff, axis=0, keepdims=True)
        acc_ref[1:2, :] += jnp.sum(dres * on, axis=0, keepdims=True)
        dn = dres * npw_ref[...]
        dout = (r * (dn - on * jnp.mean(dn * on, axis=-1, keepdims=True))).astype(bf16)
        dmix_ref[...] = _nt(dout, wo_ref[...])
        dwo_ref[0:D_ATTN, :] += _tn(ma_ref[...], dout)
        dwo_ref[D_ATTN:, :] += _tn(ms_ref[...], dout)

    row = lambda w: pl.BlockSpec((tm, w), lambda i: (i, 0))
    return pl.pallas_call(
        body, name="outproj_loss", grid=(s // tm,),
        in_specs=[row(D_ATTN), row(D_SSM), _full((D_ATTN + D_SSM, d)), row(d), row(d), _full((1, d))],
        out_specs=[row(D_ATTN + D_SSM), row(d), _full((8, d)), _full((D_ATTN + D_SSM, d))],
        out_shape=[SDS((s, D_ATTN + D_SSM), f32), SDS((s, d), f32), SDS((8, d), f32), SDS((D_ATTN + D_SSM, d), f32)],
        compiler_params=pltpu.CompilerParams(dimension_semantics=("arbitrary",)),
    )(mix_a, mix_s, wo, x, tgt, npw)


def _attn_bwd(proj, o, lb, dmix, swap=None):
    s = proj.shape[0]
    n_it = s // BLK

    nsw = 0 if swap is None else 1

    def body(*refs):
        q_ref, k_ref, v_ref, g_ref, o_ref, l_ref, dm_ref = refs[:7]
        swap_in = refs[7:7 + nsw]
        dq_ref, dk_ref, dv_ref, dg_ref = refs[7 + nsw:11 + nsw]
        swap_out = refs[11 + nsw:11 + 2 * nsw]
        dq_acc, dk_acc, dv_acc, do_scr, dl_scr = refs[11 + 2 * nsw:16 + 2 * nsw]
        bufs = refs[16 + 2 * nsw:44 + 2 * nsw]
        swap_sems = refs[44 + 2 * nsw:]
        head0, tri2, _, _, bones = _attn_consts()

        if nsw:
            x, y, c = _my_pos()
            swap_copy = pltpu.make_async_remote_copy(
                src_ref=swap_in[0], dst_ref=swap_out[0], send_sem=swap_sems[0], recv_sem=swap_sems[1],
                device_id=(x, y, 1 - c), device_id_type=MESH)

            @pl.when(pl.program_id(0) == 0)
            def _():
                swap_copy.start()

        def pro(i, carry):
            rows = pl.ds(pl.multiple_of(i * 256, 256), 256)
            g = g_ref[rows, :]
            sg = _sigmoid(g)
            dmx = dm_ref[rows, :]
            ov = o_ref[rows, :]
            dg_ref[rows, :] = (dmx * ov * (sg * (1.0 + g * (1.0 - sg)))).astype(bf16)
            do = dmx * (g * sg)
            do_scr[rows, :] = do
            dl_scr[rows, :] = _split_dot_sum(do * ov, bones)
            z = jnp.zeros((256, LANES), f32)
            dq_acc[rows, :] = z
            dk_acc[rows, :] = z
            dv_acc[rows, :] = z
            return carry

        lax.fori_loop(0, s // 256, pro, 0)

        def per_head(t):
            return jnp.concatenate([t[:, :LANES], t[:, LANES:]], axis=0)

        def both_heads(t):
            tr = pltpu.roll(t, HEAD_DIM, 1)
            return jnp.concatenate([jnp.where(head0, t, tr), jnp.where(head0, tr, t)], axis=1)

        mm_bufs = ((bufs[0], bufs[1], bufs[2], bufs[3]), (bufs[4], bufs[5], bufs[6], bufs[7]))
        ds_bufs = ((bufs[8], bufs[9], bufs[10], bufs[11]), (bufs[12], bufs[13], bufs[14], bufs[15]))
        op_bufs = ((bufs[16], bufs[17], bufs[18], bufs[19]), (bufs[20], bufs[21], bufs[22], bufs[23]))
        vc_bufs, carry_k, carry_v = (bufs[24], bufs[25]), bufs[26], bufs[27]
        for buf in (op_bufs[0][0], op_bufs[1][0]) + vc_bufs:
            buf[...] = jnp.zeros_like(buf)

        def block_rows(i, d, nb):
            r, blk = i // nb, i % nb
            return pl.ds(blk * (BLK * d) + r, BLK, stride=d), blk > 0

        def unstack(st16):
            return st16[:BLK] + st16[BLK:]

        def products(i, par, d, nb):
            rows, has_prev = block_rows(i, d, nb)
            s_buf, dp_buf, sd_buf, dpd_buf = mm_bufs[par]
            kc_buf, kp_buf, q_buf, do_buf = op_bufs[par]
            q = q_ref[rows, :]
            qs = q * 0.125
            do = do_scr[rows, :]
            qs16, do16 = qs.astype(bf16), do.astype(bf16)
            kst_c = _stack_heads(k_ref[rows, :].astype(bf16), head0)
            vst_c = _stack_heads(v_ref[rows, :].astype(bf16), head0)
            kst_p, vst_p = op_bufs[1 - par][0][...], vc_bufs[1 - par][...]
            kc_buf[...] = kst_c
            kp_buf[...] = kst_p
            vc_bufs[par][...] = vst_c
            q_buf[...] = q.astype(bf16)
            do_buf[...] = do16
            s_buf[...] = jnp.where(tri2, _nt(qs16, kst_c), jnp.where(has_prev, _nt(qs16, kst_p), -jnp.inf))
            dp_buf[...] = jnp.where(tri2, _nt(do16, vst_c), jnp.where(has_prev, _nt(do16, vst_p), 0.0))
            sd_buf[...] = _nn((qs * unstack(kst_p).astype(f32)).astype(bf16), bones)
            dpd_buf[...] = jnp.where(has_prev, _nn((do * unstack(vst_p).astype(f32)).astype(bf16), bones), 0.0)

        def softmax_grad(i, par, d, nb):
            rows, has_prev = block_rows(i, d, nb)
            s_buf, dp_buf, sd_buf, dpd_buf = mm_bufs[par]
            p_buf, ds_buf, pd_buf, dsd_buf = ds_bufs[par]
            lse = l_ref[rows, :]
            dl = dl_scr[rows, :]
            pt = jnp.exp(s_buf[...] - both_heads(lse))
            ds_buf[...] = (pt * (dp_buf[...] - both_heads(dl)) * 0.125).astype(bf16)
            p_buf[...] = pt.astype(bf16)
            pd = jnp.where(has_prev, jnp.exp(sd_buf[...] - lse), 0.0)
            pd_buf[...] = pd
            dsd_buf[...] = pd * (dpd_buf[...] - dl) * 0.125

        def accumulate(i, par, d, nb):
            rows, _ = block_rows(i, d, nb)
            before, _ = block_rows(jnp.maximum(i - 1, 0), d, nb)
            p_buf, ds_buf, pd_buf, dsd_buf = ds_bufs[par]
            kc_buf, kp_buf, q_buf, do_buf = op_bufs[par]
            pt16, ds16, pd, dsd = p_buf[...], ds_buf[...], pd_buf[...], dsd_buf[...]
            zero = jnp.zeros_like(pt16)
            dsc, dsp = jnp.where(tri2, ds16, zero), jnp.where(tri2, zero, ds16)
            pc, pp = jnp.where(tri2, pt16, zero), jnp.where(tri2, zero, pt16)
            kst_c, kst_p, q16, do16 = kc_buf[...], kp_buf[...], q_buf[...], do_buf[...]
            qst, dost = _stack_heads(q16, head0), _stack_heads(do16, head0)
            dq_acc[rows, :] += _nn(dsc, kst_c) + _nn(dsp, kst_p) + dsd * unstack(kst_p).astype(f32)
            dk2 = _tn(jnp.concatenate([per_head(dsc), per_head(dsp)], axis=1), qst)
            dv2 = _tn(jnp.concatenate([per_head(pc), per_head(pp)], axis=1), dost)
            dk_acc[before, :] += carry_k[...] + dk2[BLK:] + dsd * q16.astype(f32)
            dv_acc[before, :] += carry_v[...] + dv2[BLK:] + pd * do16.astype(f32)
            carry_k[...] = dk2[:BLK]
            carry_v[...] = dv2[:BLK]

        for d in DILATIONS:
            nb = s // (BLK * d)
            carry_k[...] = jnp.zeros_like(carry_k)
            carry_v[...] = jnp.zeros_like(carry_v)
            products(0, 0, d, nb)
            products(1, 1, d, nb)
            softmax_grad(0, 0, d, nb)

            def steps(j, carry, d=d, nb=nb):
                for par in range(2):
                    t = 2 * j + 2 + par
                    accumulate(t - 2, par, d, nb)
                    products(t, par, d, nb)
                    softmax_grad(t - 1, 1 - par, d, nb)
                return carry

            lax.fori_loop(0, (n_it - 2) // 2, steps, 0, unroll=5)
            accumulate(n_it - 2, 0, d, nb)
            softmax_grad(n_it - 1, 1, d, nb)
            accumulate(n_it - 1, 1, d, nb)
            last, _ = block_rows(n_it - 1, d, nb)
            dk_acc[last, :] += carry_k[...]
            dv_acc[last, :] += carry_v[...]

        def epi(i, carry):
            rows = pl.ds(pl.multiple_of(i * 256, 256), 256)
            dq_ref[rows, :] = dq_acc[rows, :].astype(bf16)
            dk_ref[rows, :] = dk_acc[rows, :].astype(bf16)
            dv_ref[rows, :] = dv_acc[rows, :].astype(bf16)
            return carry

        lax.fori_loop(0, s // 256, epi, 0)

        if nsw:
            @pl.when(pl.program_id(0) == N_PAIRS - 1)
            def _():
                swap_copy.wait_send()
                swap_copy.wait_recv()

    col = lambda base: pl.BlockSpec((s, LANES), lambda h: (0, base + h))
    anyspec = pl.BlockSpec(memory_space=pl.ANY)
    swaps = [] if swap is None else [swap]
    outs = pl.pallas_call(
        body, name="attn_bwd", grid=(N_PAIRS,),
        in_specs=[col(0), col(8), col(16), col(24), col(0), col(0), col(0)] + [anyspec] * nsw,
        out_specs=[col(0)] * 4 + [anyspec] * nsw,
        out_shape=[SDS((s, D_ATTN), bf16)] * 4 + [SDS(a.shape, a.dtype) for a in swaps],
        scratch_shapes=[pltpu.VMEM((s, LANES), f32)] * 5
        + [pltpu.VMEM((BLK, 2 * LANES), f32)] * 2 + [pltpu.VMEM((BLK, LANES), f32)] * 2
        + [pltpu.VMEM((BLK, 2 * LANES), f32)] * 2 + [pltpu.VMEM((BLK, LANES), f32)] * 2
        + [pltpu.VMEM((BLK, 2 * LANES), bf16)] * 2 + [pltpu.VMEM((BLK, LANES), f32)] * 2
        + [pltpu.VMEM((BLK, 2 * LANES), bf16)] * 2 + [pltpu.VMEM((BLK, LANES), f32)] * 2
        + [pltpu.VMEM((2 * BLK, LANES), bf16)] * 2 + [pltpu.VMEM((BLK, LANES), bf16)] * 2
        + [pltpu.VMEM((2 * BLK, LANES), bf16)] * 2 + [pltpu.VMEM((BLK, LANES), bf16)] * 2
        + [pltpu.VMEM((2 * BLK, LANES), bf16)] * 2 + [pltpu.VMEM((BLK, LANES), f32)] * 2
        + [pltpu.SemaphoreType.DMA(())] * (2 * nsw),
        compiler_params=pltpu.CompilerParams(dimension_semantics=("arbitrary",)),
    )(proj, proj, proj, proj, o, lb, dmix, *swaps)
    return outs


def _ssd_bwd(proj, y, states, cv, dmix, conv_w, conv_b, dtb16, alog16, alog_f, d_f, nw, chip_sums=()):
    s = proj.shape[0]
    nc = s // CHUNK
    gw = D_SSM // N_GROUPS
    nx = len(chip_sums)

    def body(*refs):
        (xs_ref, bc_ref, _, _, dt_ref, z_ref, y_ref, st_ref, dm_ref, cw_ref, cb_ref, dtb_ref,
         alog16_ref, alogf_ref, df_ref, nw_ref, cv_ref) = refs[:17]
        cs_in = refs[17:17 + nx]
        out_ref, gconv_ref, gvec_ref, gdt_ref = refs[17 + nx:21 + nx]
        cs_out = refs[21 + nx:21 + 2 * nx]
        (dh_scr, head_scr, dcpad, da_scr, dxdt_scr, dbc_scr, emat_ref, fold_ref) = refs[21 + 2 * nx:29 + 2 * nx]
        cs_sems = refs[29 + 2 * nx:]
        i = pl.program_id(0)
        c = nc - 1 - i

        if nx:
            @pl.when(i == 0)
            def _():
                mine, sends, _ = _chip_exchange_copies(cs_in, cs_out, *cs_sems)
                for cp in mine + sends:
                    cp.start()

            @pl.when(i == nc - 1)
            def _():
                mine, sends, recvs = _chip_exchange_copies(cs_in, cs_out, *cs_sems)
                for cp in recvs:
                    cp.wait_recv()
                for cp in sends:
                    cp.wait_send()
                for cp in mine:
                    cp.wait()

        @pl.when(i == 0)
        def _():
            emat_ref[...] = _expand_mat()
            fold_ref[...] = _fold_mat()
            dh_scr[...] = jnp.zeros_like(dh_scr)
            head_scr[...] = jnp.zeros_like(head_scr)
            gconv_ref[...] = jnp.zeros_like(gconv_ref)
            gvec_ref[...] = jnp.zeros_like(gvec_ref)
            gdt_ref[...] = jnp.zeros_like(gdt_ref)

        cv = cv_ref[...]
        sig = _sigmoid(cv)
        xbc = cv * sig
        pre, dt_f, al_f, al_x, al_t = _decay_terms(dt_ref, dtb_ref, alog16_ref, emat_ref)
        head0 = _iota((CHUNK, LANES), 1) < HEAD_DIM
        sub = _iota((CHUNK, LANES), 0)
        last_row = sub == CHUNK - 1

        yv, z, dmx = y_ref[...], z_ref[...], dm_ref[...]
        sz = _sigmoid(z)
        silu = z * sz
        yz = yv * silu
        dyz_parts = []
        for g in range(N_GROUPS):
            gs = slice(g * gw, (g + 1) * gw)
            part = yz[:, gs]
            r = lax.rsqrt(jnp.mean(part * part, axis=-1, keepdims=True) + EPS)
            nh = part * r
            gvec_ref[0:1, gs] += jnp.sum(dmx[:, gs] * nh, axis=0, keepdims=True)
            dn = dmx[:, gs] * nw_ref[:, gs]
            dyz_parts.append(r * (dn - nh * jnp.mean(dn * nh, axis=-1, keepdims=True)))
        dyz = jnp.concatenate(dyz_parts, axis=1)
        dy = dyz * silu
        out_ref[:, 0:D_SSM] = (dyz * yv * (sz * (1.0 + z * (1.0 - sz)))).astype(bf16)

        x_all = xbc[:, 0:D_SSM]
        gvec_ref[2:3, :] += jnp.sum(dy * x_all, axis=0, keepdims=True)

        for g in range(N_GROUPS):
            bm = xbc[:, D_SSM + g * D_STATE: D_SSM + (g + 1) * D_STATE].astype(bf16)
            cm = xbc[:, D_SSM + (N_GROUPS + g) * D_STATE: D_SSM + (N_GROUPS + g + 1) * D_STATE].astype(bf16)
            gmat = _nt(cm, bm)
            dgm = jnp.zeros((CHUNK, CHUNK), f32)
            db = jnp.zeros((CHUNK, D_STATE), f32)
            dc = jnp.zeros((CHUNK, D_STATE), f32)
            for pair in range(4 * g, 4 * g + 4):
                sl = slice(pair * LANES, (pair + 1) * LANES)
                xp, dtp, alp, dyp = x_all[:, sl], dt_f[:, sl], al_f[:, sl], dy[:, sl]
                xdt = xp * dtp
                xdt16 = xdt.astype(bf16)
                al_last = alp[CHUNK - 1:CHUNK, :]
                e_l = jnp.exp(alp)
                wf = jnp.exp(al_last - alp)
                e_last = jnp.exp(al_last)
                hp = st_ref[:, sl]
                hp16 = hp.astype(bf16)
                dhn = dh_scr[:, sl]
                dhn16 = dhn.astype(bf16)
                y_off = e_l * _nn(cm, hp16)
                dch16 = (dyp * e_l).astype(bf16)
                dc = dc + _nt(dch16, hp16)
                dh_out = _tn(cm, dch16)
                dal = dyp * y_off
                xw16 = (wf * xdt).astype(bf16)
                db = db + _nt(xw16, dhn16)
                dxw = _nn(bm, dhn16)
                dxdt = dxw * wf
                dwf = dxw * xdt * wf
                dal = dal - dwf
                dal_last = jnp.sum(dwf, axis=0, keepdims=True) + jnp.sum(dhn * hp, axis=0, keepdims=True) * e_last
                dh_scr[:, sl] = e_last * dhn + dh_out
                for h in range(2):
                    mh = head0 if h == 0 else jnp.logical_not(head0)
                    dyh16 = jnp.where(mh, dyp, 0.0).astype(bf16)
                    lmat = _decay_mat(al_x, al_t, pair, h)
                    mm = gmat * lmat
                    dmm = _nt(dyh16, xdt16)
                    dxdt = dxdt + _tn(mm.astype(bf16), dyh16)
                    n16 = (dmm * mm).astype(bf16)
                    jh = jnp.where(mh, 1.0 / HEAD_DIM, 0.0).astype(bf16)
                    dal = dal + _nn(n16, jh) - _tn(n16, jh)
                    dgm = dgm + dmm * lmat
                da_scr[:, sl] = dal + jnp.where(last_row, dal_last, 0.0)
                dxdt_scr[:, sl] = dxdt
            dgm16 = dgm.astype(bf16)
            dbc_scr[:, g * D_STATE:(g + 1) * D_STATE] = db + _tn(dgm16, cm)
            dbc_scr[:, (N_GROUPS + g) * D_STATE:(N_GROUPS + g + 1) * D_STATE] = dc + _nn(dgm16, bm)

        sub_c, lane_c = _iota((CHUNK, CHUNK), 0), _iota((CHUNK, CHUNK), 1)
        tri_t = (lane_c >= sub_c).astype(bf16)
        dadt = _dot_01_left(tri_t, da_scr[...], 2)
        a_f = -jnp.exp(alogf_ref[...])
        dxdt_all = dxdt_scr[...]
        ddt_f = dxdt_all * x_all + a_f * dadt
        gvec_ref[1:2, :] += jnp.sum(dt_f * dadt, axis=0, keepdims=True) * a_f
        dx = df_ref[...] * dy + dxdt_all * dt_f
        ddt_raw = _dot_01(ddt_f, fold_ref[...], 2) * _sigmoid(pre)
        gdt_ref[0:1, :] += jnp.sum(ddt_raw, axis=0, keepdims=True)
        out_ref[:, D_SSM + D_CONV:D_SSM + D_CONV + LANES] = ddt_raw.astype(bf16)
        out_ref[:, D_SSM + D_CONV + LANES:] = jnp.zeros((CHUNK, 3 * LANES), bf16)

        dsil = sig * (1.0 + cv * (1.0 - sig))
        dcv_x = dx * dsil[:, 0:D_SSM]
        dcv_bc = dbc_scr[...] * dsil[:, D_SSM:]
        dcpad[0:CHUNK, 0:D_SSM] = dcv_x
        dcpad[0:CHUNK, D_SSM:] = dcv_bc
        dcpad[CHUNK:, :] = head_scr[...]
        dcp = dcpad[...]
        dcv = dcp[0:CHUNK]
        gconv_ref[4:5, :] += jnp.sum(dcv, axis=0, keepdims=True)
        x_raw = jnp.concatenate([xs_ref[...], bc_ref[...]], axis=1)
        draw = cw_ref[3:4, :] * dcv
        gconv_ref[3:4, :] += jnp.sum(dcv * x_raw, axis=0, keepdims=True)
        for j in range(3):
            ahead = pltpu.roll(dcp, CHUNK + 8 - (3 - j), 0)[0:CHUNK]
            draw = draw + cw_ref[j:j + 1, :] * ahead
            gconv_ref[j:j + 1, :] += jnp.sum(ahead * x_raw, axis=0, keepdims=True)
        head_scr[...] = dcv[0:8]
        out_ref[:, D_SSM:D_SSM + D_CONV] = draw.astype(bf16)

    order = lambda i: nc - 1 - i
    row = lambda w, cb=0: pl.BlockSpec((CHUNK, w), lambda i: (nc - 1 - i, cb))
    anyspec = pl.BlockSpec(memory_space=pl.ANY)
    outs = pl.pallas_call(
        body, name="ssd_bwd", grid=(nc,),
        in_specs=_ssd_in_specs(order) + [row(D_SSM), pl.BlockSpec((None, D_STATE, D_SSM), lambda i: (nc - 1 - i, 0, 0)),
                                         row(D_SSM, 1), _full((4, D_CONV)), _full((1, D_CONV)), _full((1, LANES)),
                                         _full((1, LANES)), _full((1, D_SSM)), _full((1, D_SSM)), _full((1, D_SSM)),
                                         row(D_CONV)]
        + [anyspec] * nx,
        out_specs=[row(3072), _full((8, D_CONV)), _full((8, D_SSM)), _full((8, LANES))] + [anyspec] * nx,
        out_shape=[SDS((s, 3072), bf16), SDS((8, D_CONV), f32), SDS((8, D_SSM), f32), SDS((8, LANES), f32)]
        + [SDS(a.shape, a.dtype) for a in chip_sums],
        scratch_shapes=[pltpu.VMEM((D_STATE, D_SSM), f32), pltpu.VMEM((8, D_CONV), f32),
                        pltpu.VMEM((8 + CHUNK, D_CONV), f32),
                        pltpu.VMEM((CHUNK, D_SSM), f32), pltpu.VMEM((CHUNK, D_SSM), f32),
                        pltpu.VMEM((CHUNK, 2 * N_GROUPS * D_STATE), f32),
                        pltpu.VMEM((LANES, 2 * D_SSM), bf16), pltpu.VMEM((D_SSM, LANES), bf16)]
        + (_chip_exchange_scratch(nx) if nx else []),
        compiler_params=pltpu.CompilerParams(dimension_semantics=("arbitrary",)),
    )(proj, proj, proj, proj, proj, proj, y, states, dmix, conv_w, conv_b, dtb16, alog16, alog_f, d_f, nw, cv,
      *chip_sums)
    return outs[0], outs[1], outs[2], outs[3], outs[4:]


def _col_blocks(parts, tile):
    counts = [p.shape[1] // tile for p in parts]
    offs = [sum(counts[:t]) for t in range(len(parts))]
    return offs, counts, sum(counts)


def _bcast_copies(src_ref, out_ref, send_sems, recv_sems, local_sem):
    x, y, c = _my_pos()
    me = 4 * x + 2 * y + c
    mine = pltpu.make_async_copy(src_ref, out_ref.at[me], local_sem)
    sends, recvs = [], []
    for k in range(1, N_DEV):
        to, frm = (me + k) % N_DEV, (me + N_DEV - k) % N_DEV
        sems = dict(send_sem=send_sems.at[k - 1], recv_sem=recv_sems.at[k - 1], device_id_type=MESH)
        sends.append(pltpu.make_async_remote_copy(
            src_ref=src_ref, dst_ref=out_ref.at[me], device_id=(to // 4, (to // 2) % 2, to % 2), **sems))
        recvs.append(pltpu.make_async_remote_copy(
            src_ref=src_ref, dst_ref=out_ref.at[frm], device_id=(x, y, c), **sems))
    return mine, sends, recvs


def _bcast_scratch():
    return [pltpu.SemaphoreType.DMA((N_DEV - 1,)), pltpu.SemaphoreType.DMA((N_DEV - 1,)), pltpu.SemaphoreType.DMA(())]


def _inproj_bwd(dparts, wt, x, nw, dres, chip_sums=(), pack=None):
    s, d = x.shape
    tm, tk = 1024, 1024
    offs, counts, nk = _col_blocks(dparts, tk)
    npart, nx = len(dparts), len(chip_sums)
    npk = 0 if pack is None else 1
    ni = s // tm

    def body(*refs):
        dp_refs = refs[:npart]
        w_ref, x_ref, nw_ref, dres_ref = refs[npart:npart + 4]
        pos = npart + 4
        cs_in, pos = refs[pos:pos + nx], pos + nx
        pack_in, pos = refs[pos:pos + npk], pos + npk
        (gx_ref, gnw_ref), pos = refs[pos:pos + 2], pos + 2
        cs_out, pos = refs[pos:pos + nx], pos + nx
        pack_out, pos = refs[pos:pos + 2 * npk], pos + 2 * npk
        acc, pos = refs[pos], pos + 1
        cs_sems, pos = refs[pos:pos + 3 * min(nx, 1)], pos + 3 * min(nx, 1)
        pk_refs = refs[pos:]
        i, k = pl.program_id(0), pl.program_id(1)

        def exchange():
            return _chip_exchange_copies(cs_in, cs_out, *cs_sems)

        def pack_copies():
            return _bcast_copies(pack_in[0], pack_out[0], *pk_refs[1:4])

        def gnw_copies():
            return _bcast_copies(pk_refs[0], pack_out[1], *pk_refs[4:7])

        @pl.when(jnp.logical_and(i == 0, k == 0))
        def _():
            gnw_ref[...] = jnp.zeros_like(gnw_ref)
            if nx:
                mine, sends, _ = exchange()
                for cp in mine + sends:
                    cp.start()
            if npk:
                mine, sends, _ = pack_copies()
                for cp in [mine] + sends:
                    cp.start()

        @pl.when(k == 0)
        def _():
            acc[...] = _nn(dp_refs[0][...], w_ref[...])

        for t in range(npart):
            @pl.when(jnp.logical_and(k >= max(offs[t], 1), k < offs[t] + counts[t]))
            def _(t=t):
                acc[...] += _nn(dp_refs[t][...], w_ref[...])

        @pl.when(k == nk - 1)
        def _():
            xv = x_ref[...]
            r = lax.rsqrt(jnp.mean(xv * xv, axis=-1, keepdims=True) + EPS)
            xn = xv * r
            du = acc[...]
            gnw_ref[0:1, :] += jnp.sum(du * xn, axis=0, keepdims=True)
            dn = du * nw_ref[...]
            gx_ref[...] = dres_ref[...] + r * (dn - xn * jnp.mean(dn * xn, axis=-1, keepdims=True))

        @pl.when(jnp.logical_and(i == ni - 1, k == nk - 1))
        def _():
            if npk:
                pk_refs[0][...] = gnw_ref[...]
                mine, sends, _ = gnw_copies()
                for cp in [mine] + sends:
                    cp.start()
            if nx:
                mine, sends, recvs = exchange()
                for cp in recvs:
                    cp.wait_recv()
                for cp in sends:
                    cp.wait_send()
                for cp in mine:
                    cp.wait()
            if npk:
                for copies in (pack_copies(), gnw_copies()):
                    mine, sends, recvs = copies
                    for cp in recvs:
                        cp.wait_recv()
                    for cp in sends:
                        cp.wait_send()
                    mine.wait()

    def piece(t):
        return pl.BlockSpec((tm, tk), lambda i, k: (i, jnp.clip(k - offs[t], 0, counts[t] - 1)))

    anyspec = pl.BlockSpec(memory_space=pl.ANY)
    packs = [] if pack is None else [pack]
    pack_shapes = [] if pack is None else [SDS((N_DEV,) + pack.shape, f32), SDS((N_DEV, 8, d), f32)]
    scratch = [pltpu.VMEM((tm, d), f32)] + (_chip_exchange_scratch(nx) if nx else [])
    if npk:
        scratch += [pltpu.VMEM((8, d), f32)] + _bcast_scratch() + _bcast_scratch()
    outs = pl.pallas_call(
        body, name="inproj_bwd", grid=(ni, nk),
        in_specs=[piece(t) for t in range(npart)] + [
            pl.BlockSpec((tk, d), lambda i, k: (k, 0)),
            pl.BlockSpec((tm, d), lambda i, k: (i, 0)), pl.BlockSpec((1, d), lambda i, k: (0, 0)),
            pl.BlockSpec((tm, d), lambda i, k: (i, 0))] + [anyspec] * (nx + npk),
        out_specs=[pl.BlockSpec((tm, d), lambda i, k: (i, 0)), pl.BlockSpec((8, d), lambda i, k: (0, 0))]
        + [anyspec] * (nx + 2 * npk),
        out_shape=[SDS((s, d), f32), SDS((8, d), f32)] + [SDS(a.shape, a.dtype) for a in chip_sums] + pack_shapes,
        scratch_shapes=scratch,
        compiler_params=pltpu.CompilerParams(dimension_semantics=("arbitrary", "arbitrary")),
    )(*dparts, wt, x, nw, dres, *chip_sums, *packs)
    return outs[0], outs[1], outs[2:2 + nx], outs[2 + nx:]


def _matmul_tn(a_parts, b_parts, name):
    tile, tk = 1024, 1024
    s = a_parts[0].shape[0]
    nk = s // tk
    na, nb = len(a_parts), len(b_parts)
    offs_a, counts_a, ni = _col_blocks(a_parts, tile)
    offs_b, counts_b, nj = _col_blocks(b_parts, tile)

    def body(*refs):
        a_refs, b_refs, o_ref = refs[:na], refs[na:na + nb], refs[na + nb]
        i, j = pl.program_id(0), pl.program_id(1)

        @pl.when(pl.program_id(2) == 0)
        def _():
            o_ref[...] = jnp.zeros_like(o_ref)

        for ta in range(na):
            for tb in range(nb):
                in_a = jnp.logical_and(i >= offs_a[ta], i < offs_a[ta] + counts_a[ta])
                in_b = jnp.logical_and(j >= offs_b[tb], j < offs_b[tb] + counts_b[tb])

                @pl.when(jnp.logical_and(in_a, in_b))
                def _(ta=ta, tb=tb):
                    o_ref[...] += _tn(a_refs[ta][...], b_refs[tb][...])

    def spec(offs, counts, t, axis):
        def index(i, j, k):
            pos = (i, j)[axis]
            mine = jnp.logical_and(pos >= offs[t], pos < offs[t] + counts[t])
            return jnp.where(mine, k, 0), jnp.clip(pos - offs[t], 0, counts[t] - 1)
        return pl.BlockSpec((tk, tile), index)

    return pl.pallas_call(
        body, name=name, grid=(ni, nj, nk),
        in_specs=[spec(offs_a, counts_a, t, 0) for t in range(na)] + [spec(offs_b, counts_b, t, 1) for t in range(nb)],
        out_specs=pl.BlockSpec((tile, tile), lambda i, j, k: (i, j)),
        out_shape=SDS((ni * tile, nj * tile), f32),
        compiler_params=pltpu.CompilerParams(dimension_semantics=("parallel", "parallel", "arbitrary")),
    )(*a_parts, *b_parts)


def _adamw(w, g, m, v):
    m = ADAM_B1 * m + (1.0 - ADAM_B1) * g
    v = ADAM_B2 * v + (1.0 - ADAM_B2) * (g * g)
    m_hat = m / (1.0 - ADAM_B1 ** ADAM_STEP)
    v_hat = v / (1.0 - ADAM_B2 ** ADAM_STEP)
    delta = -ADAM_LR * (m_hat / (jnp.sqrt(v_hat) + ADAM_EPS) + ADAM_WD * w)
    return delta, m, v


def _sum_adamw(parts, w, m, v, name):
    r, c = w.shape
    tc = 256

    def body(p_ref, w_ref, m_ref, v_ref, g_ref, d_ref, nm_ref, nv_ref):
        g = p_ref[0].astype(f32)
        for q in range(1, 4):
            g = g + p_ref[q].astype(f32)
        g_ref[...] = g
        d_ref[...], nm_ref[...], nv_ref[...] = _adamw(w_ref[...], g, m_ref[...], v_ref[...])

    blk = pl.BlockSpec((r, tc), lambda i: (0, i))
    return pl.pallas_call(
        body, name=name, grid=(c // tc,),
        in_specs=[pl.BlockSpec((4, r, tc), lambda i: (0, 0, i)), blk, blk, blk],
        out_specs=[blk] * 4, out_shape=[SDS((r, c), f32)] * 4,
        compiler_params=pltpu.CompilerParams(dimension_semantics=("parallel",)),
    )(parts, w, m, v)


def _sum_small(parts, pre_blocks):
    def body(p_ref, b_ref, o_ref):
        t = p_ref[0]
        pre = b_ref[0]
        for j in range(1, N_DEV):
            t = t + p_ref[j]
            pre = pre + b_ref[j]
        o_ref[...] = t
        o_ref[5:6, 0:D_MODEL] = pre[0:1, :]
        row_h = _iota((D_SSM, LANES), 0) // HEAD_DIM
        fold = (row_h == _iota((D_SSM, LANES), 1)).astype(f32)
        lower = t[8:16, 0:LANES]
        folded = _nn_hi(t[8:16, 0:D_SSM], fold)
        loss = jnp.sum(t[11:12, 0:D_MODEL], axis=1, keepdims=True) * (0.5 / D_MODEL)
        row = _iota((8, LANES), 0)
        o_ref[8:16, 0:LANES] = jnp.where(row < 2, folded, jnp.where(row == 4, loss, lower))

    return pl.pallas_call(body, name="sum_small", out_shape=SDS((PACK_ROWS, PACK_W), f32),
                          in_specs=[pl.BlockSpec(memory_space=pltpu.VMEM)] * 2,
                          out_specs=pl.BlockSpec(memory_space=pltpu.VMEM))(parts, pre_blocks)


def _adamw_small(w, g, m, v):
    def body(w_ref, g_ref, m_ref, v_ref, d_ref, nm_ref, nv_ref):
        d_ref[...], nm_ref[...], nv_ref[...] = _adamw(w_ref[...], g_ref[...], m_ref[...], v_ref[...])

    vm = pl.BlockSpec(memory_space=pltpu.VMEM)
    return pl.pallas_call(body, name="adamw_small", out_shape=[SDS(w.shape, f32)] * 3,
                          in_specs=[vm] * 4, out_specs=[vm] * 3)(w, g, m, v)


def _pad_lanes(v, width):
    return jnp.pad(v, ((0, 0), (0, width - v.shape[1])))


def _local_step(x, tgt, norm_pre_w, wt, conv_w, conv_b, dt_bias, a_log, d_skip, ssm_norm_w, wo, norm_post_w, sharded):
    dtb16 = _pad_lanes(dt_bias, LANES)
    alog16 = _pad_lanes(a_log, LANES)
    alog_f = jnp.repeat(a_log, HEAD_DIM, axis=1)
    d_f = jnp.repeat(d_skip, HEAD_DIM, axis=1)

    shard_out = wo.shape[0]
    if sharded:
        proj, u, (g_out, g_cw) = _prenorm_inproj(x, norm_pre_w, wt, gather=(wo, conv_w))
        wo = g_out.reshape(N_DEV * shard_out, D_MODEL)
        conv_w = g_cw.transpose(1, 0, 2).reshape(4, D_CONV)
    else:
        proj, u, _ = _prenorm_inproj(x, norm_pre_w, wt)
    o, lb, mix_a = _attn_fwd(proj)
    mix_s, y, states, cv = _ssd_fwd(proj, conv_w, conv_b, dtb16, alog16, alog_f, d_f, ssm_norm_w)
    dmix, dres, acc_post, dw_out = _outproj_loss(mix_a, mix_s, wo, x, tgt, norm_post_w)
    ssd_args = (proj, y, states, cv, dmix, conv_w, conv_b, dtb16, alog16, alog_f, d_f, ssm_norm_w)
    if sharded:
        dq, dk, dv, dg, got_out = _attn_bwd(proj, o, lb, dmix, swap=dw_out)
        chip_out = _chip_sum(dw_out, got_out, shard_out, "chip_sum_w_out")
        dzxd, g_conv, g_vec, g_dt, (parts_out,) = _ssd_bwd(*ssd_args, chip_sums=[chip_out])
    else:
        dq, dk, dv, dg = _attn_bwd(proj, o, lb, dmix)
        dzxd, g_conv, g_vec, g_dt, _ = _ssd_bwd(*ssd_args)
    dparts = [dq, dk, dv, dg, dzxd]

    def pack(g_pre_row):
        return jnp.concatenate(
            [g_conv[0:5], g_pre_row, _pad_lanes(g_vec[0:1], PACK_W), _pad_lanes(acc_post[1:2], PACK_W),
             _pad_lanes(g_vec[1:3], PACK_W), _pad_lanes(g_dt[0:1], PACK_W), _pad_lanes(acc_post[0:1], PACK_W),
             jnp.zeros((4, PACK_W), f32)], axis=0)

    if sharded:
        dw_in, got_in = _dw_in_swap(dparts, u)
        chip_in = _chip_sum(dw_in, got_in, D_IN_PROJ // N_DEV, "chip_sum_w_in")
        grad_x, _, (parts_in,), small = _inproj_bwd(dparts, wt, x, norm_pre_w, dres, [chip_in],
                                                    pack(jnp.zeros((1, PACK_W), f32)))
        return grad_x, (parts_in, parts_out), small
    dw_in = _matmul_tn(dparts, [u], "dw_in")
    grad_x, g_pre, _, _ = _inproj_bwd(dparts, wt, x, norm_pre_w, dres)
    return grad_x, (dw_in, dw_out), pack(_pad_lanes(g_pre[0:1], PACK_W))


def kernel(x, norm_pre_w, w_in, conv_w, conv_b, dt_bias, a_log, d_skip, ssm_norm_w, w_out, norm_post_w, loss_target, m_norm_pre_w, m_w_in, m_conv_w, m_conv_b, m_dt_bias, m_a_log, m_d_skip, m_ssm_norm_w, m_w_out, m_norm_post_w, v_norm_pre_w, v_w_in, v_conv_w, v_conv_b, v_dt_bias, v_a_log, v_d_skip, v_ssm_norm_w, v_w_out, v_norm_post_w):
    shard_cv = conv_w.shape[2]
    me = 4 * lax.axis_index("x") + 2 * lax.axis_index("y") + lax.axis_index("c")

    g_in, = _all_gather([w_in[0].T.astype(bf16)])
    wt = _assemble_wt(g_in)

    grad_x, (parts_in, parts_out), (parts_small, pre_blocks) = _local_step(
        x[0], loss_target[0], norm_pre_w, wt, conv_w[0], conv_b, dt_bias, a_log, d_skip, ssm_norm_w,
        w_out[0].astype(bf16), norm_post_w, sharded=True)

    g_w_in, d_w_in, nm_w_in, nv_w_in = (a.T for a in _sum_adamw(
        parts_in, w_in[0].T, m_w_in[0].T, v_w_in[0].T, "sum_adamw_w_in"))
    g_w_out, d_w_out, nm_w_out, nv_w_out = _sum_adamw(parts_out, w_out[0], m_w_out[0], v_w_out[0], "sum_adamw_w_out")
    tot = _sum_small(parts_small, pre_blocks)

    g_cw_all = tot[0:4]
    small_g = {
        "conv_w": lax.dynamic_slice(g_cw_all, (0, me * shard_cv), (4, shard_cv)),
        "conv_b": tot[4:5], "norm_pre_w": tot[5:6, :D_MODEL], "ssm_norm_w": tot[6:7, :D_SSM],
        "norm_post_w": tot[7:8, :D_MODEL], "a_log": tot[8:9, :16], "d_skip": tot[9:10, :16], "dt_bias": tot[10:11, :16],
    }
    loss = tot[12, 0]
    small_w = {"conv_w": (conv_w[0], m_conv_w[0], v_conv_w[0]), "conv_b": (conv_b, m_conv_b, v_conv_b),
               "norm_pre_w": (norm_pre_w, m_norm_pre_w, v_norm_pre_w), "ssm_norm_w": (ssm_norm_w, m_ssm_norm_w, v_ssm_norm_w),
               "norm_post_w": (norm_post_w, m_norm_post_w, v_norm_post_w), "a_log": (a_log, m_a_log, v_a_log),
               "d_skip": (d_skip, m_d_skip, v_d_skip), "dt_bias": (dt_bias, m_dt_bias, v_dt_bias)}
    names = list(small_w)
    sizes = [small_g[k].size for k in names]
    tot_size = sum(sizes)
    pad_to = -(-tot_size // 1024) * 1024

    def flat(arrs):
        v = jnp.concatenate([a.reshape(-1) for a in arrs])
        return jnp.pad(v, (0, pad_to - tot_size)).reshape(pad_to // LANES, LANES)

    fw = flat([small_w[k][0] for k in names])
    fg = flat([small_g[k] for k in names])
    fm = flat([small_w[k][1] for k in names])
    fv = jnp.pad(jnp.concatenate([small_w[k][2].reshape(-1) for k in names]), (0, pad_to - tot_size),
                 constant_values=1.0).reshape(pad_to // LANES, LANES)
    fd, fnm, fnv = _adamw_small(fw, fg, fm, fv)

    def unflat(f):
        out, off = {}, 0
        v = f.reshape(-1)
        for k, n in zip(names, sizes):
            out[k] = v[off:off + n].reshape(small_g[k].shape)
            off += n
        return out

    sd, snm, snv = unflat(fd), unflat(fnm), unflat(fnv)
    lead = lambda a: a[None]
    order = ["norm_pre_w", "w_in", "conv_w", "conv_b", "dt_bias", "a_log", "d_skip", "ssm_norm_w", "w_out", "norm_post_w"]
    grads = dict(small_g, w_in=g_w_in, w_out=g_w_out)
    deltas = dict(sd, w_in=d_w_in, w_out=d_w_out)
    new_m = dict(snm, w_in=nm_w_in, w_out=nm_w_out)
    new_v = dict(snv, w_in=nv_w_in, w_out=nv_w_out)

    def shaped(dct, k):
        a = dct[k]
        return lead(a) if k in ("w_in", "w_out", "conv_w") else a

    return (loss, grad_x[None], *[shaped(grads, k) for k in order], *[shaped(deltas, k) for k in order],
            *[shaped(new_m, k) for k in order], *[shaped(new_v, k) for k in order])
```

```python
import jax
import jax.numpy as jnp
from jax import lax
from jax.experimental import pallas as pl
from jax.experimental.pallas import tpu as pltpu

f32, bf16 = jnp.float32, jnp.bfloat16
SDS = jax.ShapeDtypeStruct
HIGHEST = lax.Precision.HIGHEST
MESH = pl.DeviceIdType.MESH

N_DEV = 8
D_MODEL = 1024
D_ATTN = 1024
D_SSM = 1024
HEAD_DIM = 64
N_PAIRS = 8
D_STATE = 128
N_GROUPS = 2
D_CONV = D_SSM + 2 * N_GROUPS * D_STATE
D_IN_PROJ = 4 * D_ATTN + D_SSM + D_CONV + 16
NP = 7168
CHUNK = 128
BLK = 128
DILATIONS = (1, 4, 16)
EPS = 1e-6
LANES = 128
COL_Z, COL_XS, COL_BC, COL_DT = 4096, 5120, 6144, 6656

ADAM_LR, ADAM_B1, ADAM_B2, ADAM_EPS, ADAM_WD, ADAM_STEP = 0.001, 0.9, 0.999, 1e-08, 0.01, 10

PACK_ROWS, PACK_W = 16, 1536


def _nt(a, b):
    return lax.dot_general(a, b, (((1,), (1,)), ((), ())), preferred_element_type=f32)


def _tn(a, b):
    return lax.dot_general(a, b, (((0,), (0,)), ((), ())), preferred_element_type=f32)


def _nn(a, b):
    return jnp.dot(a, b, preferred_element_type=f32)


def _nn_hi(a, b):
    return jnp.dot(a, b, precision=HIGHEST, preferred_element_type=f32)


def _sigmoid(x):
    return 1.0 / (1.0 + jnp.exp(-x))


def _softplus(x):
    return jnp.maximum(x, 0.0) + jnp.log1p(jnp.exp(-jnp.abs(x)))


def _iota(shape, dim):
    return lax.broadcasted_iota(jnp.int32, shape, dim)


def _my_pos():
    return lax.axis_index("x"), lax.axis_index("y"), lax.axis_index("c")


GATHER_SEMS = 9


def _gather_phases(ins, outs, send_sems, recv_sems, local_sems):
    n, ns = len(ins), GATHER_SEMS
    x, y, c = _my_pos()
    me, sibling = (x, y, c), (x, y, 1 - c)
    xn, yn, diag = (1 - x, y), (x, 1 - y), (1 - x, 1 - y)

    def slot(a, px, py, pc):
        return outs[a].at[4 * px + 2 * py + pc]

    def part(a, ref, h):
        width = ins[a].shape[-1]
        if width % (2 * LANES):
            return ref if h == 1 else None
        return ref.at[:, pl.ds(h * (width // 2), width // 2)]

    def copy(a, k, block, to, src=None, h=None):
        src_ref = slot(a, *block) if src is None else src
        dst_ref = slot(a, *block)
        if h is not None:
            src_ref, dst_ref = part(a, src_ref, h), part(a, dst_ref, h)
            if src_ref is None:
                return None
        return pltpu.make_async_remote_copy(
            src_ref=src_ref, dst_ref=dst_ref, send_sem=send_sems.at[ns * a + k], recv_sem=recv_sems.at[ns * a + k],
            device_id=to, device_id_type=MESH)

    def mine():
        return [pltpu.make_async_copy(ins[a], slot(a, *me), local_sems.at[a]) for a in range(n)]

    def own_sends(a):
        return [copy(a, 0, me, sibling, src=ins[a]), copy(a, 1, me, (*xn, c), src=ins[a]),
                copy(a, 2, me, (*yn, c), src=ins[a])]

    def neighbour_relays(a):
        return [copy(a, 4, (*xn, c), sibling), copy(a, 7, (*xn, c), (*yn, c), h=1),
                copy(a, 5, (*yn, c), sibling), copy(a, 8, (*yn, c), (*xn, c), h=0)]

    def diagonal_halves(a):
        return [copy(a, k, (*diag, c), me, h=h) for k, h in ((8, 0), (7, 1))]

    def start_all(cps):
        for cp in cps:
            if cp is not None:
                cp.start()

    def phase0():
        start_all(mine())
        for a in range(n):
            start_all(own_sends(a))

    def phase1():
        for a in range(n):
            copy(a, 1, (*xn, c), me).wait_recv()
            copy(a, 2, (*yn, c), me).wait_recv()
            start_all(neighbour_relays(a))

    def phase2():
        for a in range(n):
            for cp in diagonal_halves(a):
                if cp is not None:
                    cp.wait_recv()
            copy(a, 6, (*diag, c), sibling).start()

    def finish():
        for a in range(n):
            copy(a, 0, sibling, me).wait_recv()
            for j, chip in enumerate((xn, yn, diag)):
                copy(a, 4 + j, (*chip, 1 - c), me).wait_recv()
        for a in range(n):
            for cp in own_sends(a) + neighbour_relays(a) + [copy(a, 6, (*diag, c), sibling)]:
                if cp is not None:
                    cp.wait_send()
        for cp in mine():
            cp.wait()

    return phase0, phase1, phase2, finish


def _gather_scratch(n):
    return [pltpu.SemaphoreType.DMA((GATHER_SEMS * n,)), pltpu.SemaphoreType.DMA((GATHER_SEMS * n,)),
            pltpu.SemaphoreType.DMA((n,))]


def _all_gather(arrs):
    n = len(arrs)

    def body(*refs):
        for phase in _gather_phases(refs[:n], refs[n:2 * n], *refs[2 * n:]):
            phase()

    anyspec = pl.BlockSpec(memory_space=pl.ANY)
    return pl.pallas_call(
        body, name="weights_all_gather",
        out_shape=[SDS((N_DEV,) + a.shape, a.dtype) for a in arrs],
        in_specs=[anyspec] * n, out_specs=[anyspec] * n, scratch_shapes=_gather_scratch(n),
    )(*arrs)


def _dw_in_swap(a_parts, u):
    tile, tk = 1024, 1024
    s = u.shape[0]
    nk = s // tk
    na = len(a_parts)
    offs, counts, ni = _col_blocks(a_parts, tile)

    def body(*refs):
        a_refs, u_ref = refs[:na], refs[na]
        dw_ref, got_ref = refs[na + 1:na + 3]
        acc, stage, local_sems, send_sems, recv_sem = refs[na + 3:]
        i, k = pl.program_id(0), pl.program_id(1)
        x, y, c = _my_pos()
        par = i % 2

        def tile_copies(t, p):
            rows = pl.ds(pl.multiple_of(t * tile, tile), tile)
            loc = pltpu.make_async_copy(stage.at[p], dw_ref.at[rows], local_sems.at[p])
            rem = pltpu.make_async_remote_copy(
                src_ref=stage.at[p], dst_ref=got_ref.at[rows], send_sem=send_sems.at[p], recv_sem=recv_sem,
                device_id=(x, y, 1 - c), device_id_type=MESH)
            return loc, rem

        @pl.when(k == 0)
        def _():
            acc[...] = jnp.zeros((tile, tile), f32)

        for t in range(na):
            @pl.when(jnp.logical_and(i >= offs[t], i < offs[t] + counts[t]))
            def _(t=t):
                acc[...] += _tn(a_refs[t][...], u_ref[pl.ds(pl.multiple_of(k * tk, tk), tk), :])

        @pl.when(k == nk - 1)
        def _():
            @pl.when(i >= 2)
            def _():
                loc, rem = tile_copies(i - 2, par)
                loc.wait()
                rem.wait_send()
            stage[par] = acc[...]
            loc, rem = tile_copies(i, par)
            loc.start()
            rem.start()

        @pl.when(jnp.logical_and(i == ni - 1, k == nk - 1))
        def _():
            for t in (ni - 2, ni - 1):
                loc, rem = tile_copies(t, t % 2)
                loc.wait()
                rem.wait_send()
            pltpu.make_async_remote_copy(src_ref=dw_ref, dst_ref=got_ref, send_sem=send_sems.at[0], recv_sem=recv_sem,
                                         device_id=(x, y, c), device_id_type=MESH).wait_recv()

    def a_spec(t):
        def index(i, k):
            mine = jnp.logical_and(i >= offs[t], i < offs[t] + counts[t])
            return jnp.where(mine, k, 0), jnp.clip(i - offs[t], 0, counts[t] - 1)
        return pl.BlockSpec((tk, tile), index)

    anyspec = pl.BlockSpec(memory_space=pl.ANY)
    return pl.pallas_call(
        body, name="dw_in_swap", grid=(ni, nk),
        in_specs=[a_spec(t) for t in range(na)] + [pl.BlockSpec((s, tile), lambda i, k: (0, 0))],
        out_specs=[anyspec] * 2,
        out_shape=[SDS((ni * tile, tile), f32), SDS((ni * tile, tile), f32)],
        scratch_shapes=[pltpu.VMEM((tile, tile), f32), pltpu.VMEM((2, tile, tile), f32), pltpu.SemaphoreType.DMA((2,)),
                        pltpu.SemaphoreType.DMA((2,)), pltpu.SemaphoreType.DMA(())],
        compiler_params=pltpu.CompilerParams(dimension_semantics=("arbitrary", "arbitrary")),
    )(*a_parts, u)


def _chip_sum(mine, got, rows, name):
    r, cdim = mine.shape
    tc = LANES

    def body(m_ref, g_ref, s16_ref):
        c = lax.axis_index("c")
        for q in range(4):
            blk = pl.ds(rows * (2 * q + c), rows)
            s16_ref[q] = (m_ref[blk, :] + g_ref[blk, :]).astype(bf16)

    col = pl.BlockSpec((r, tc), lambda i: (0, i))
    return pl.pallas_call(
        body, name=name, grid=(cdim // tc,), in_specs=[col, col],
        out_specs=pl.BlockSpec((4, rows, tc), lambda i: (0, 0, i)), out_shape=SDS((4, rows, cdim), bf16),
        compiler_params=pltpu.CompilerParams(dimension_semantics=("parallel",)),
    )(mine, got)


def _assemble_wt(shards):
    nd, rows, cdim = shards.shape
    tc = 256

    def body(g_ref, o_ref):
        for j in range(nd):
            o_ref[pl.ds(rows * j, rows), :] = g_ref[j]
        o_ref[pl.ds(nd * rows, NP - nd * rows), :] = jnp.zeros((NP - nd * rows, tc), shards.dtype)

    return pl.pallas_call(
        body, name="assemble_w_in", grid=(cdim // tc,),
        in_specs=[pl.BlockSpec((nd, rows, tc), lambda i: (0, 0, i))],
        out_specs=pl.BlockSpec((NP, tc), lambda i: (0, i)), out_shape=SDS((NP, cdim), shards.dtype),
        compiler_params=pltpu.CompilerParams(dimension_semantics=("parallel",)),
    )(shards)


def _chip_exchange_copies(ins, outs, send_sems, recv_sems, local_sems):
    nb = len(ins)
    x, y, c = _my_pos()
    my_q = 2 * x + y
    mine = [pltpu.make_async_copy(ins[a].at[my_q], outs[a].at[my_q], local_sems.at[a]) for a in range(nb)]
    sends, recvs = [], []
    for k in range(1, 4):
        to, frm = (my_q + k) % 4, (my_q + 4 - k) % 4
        for a in range(nb):
            sems = dict(send_sem=send_sems.at[3 * a + k - 1], recv_sem=recv_sems.at[3 * a + k - 1], device_id_type=MESH)
            sends.append(pltpu.make_async_remote_copy(
                src_ref=ins[a].at[to], dst_ref=outs[a].at[my_q], device_id=(to // 2, to % 2, c), **sems))
            recvs.append(pltpu.make_async_remote_copy(
                src_ref=ins[a].at[frm], dst_ref=outs[a].at[frm], device_id=(x, y, c), **sems))
    return mine, sends, recvs


def _chip_exchange_scratch(nb):
    return [pltpu.SemaphoreType.DMA((3 * nb,)), pltpu.SemaphoreType.DMA((3 * nb,)), pltpu.SemaphoreType.DMA((nb,))]


def _prenorm_inproj(x, nw, wt, gather=()):
    s, d = x.shape
    npad = wt.shape[0]
    tm, tn = 1024, 1024
    ng = len(gather)
    ni, nj = s // tm, npad // tn

    def body(x_ref, nw_ref, w_ref, *refs):
        g_in, (proj_ref, u_ref), g_out, sems = refs[:ng], refs[ng:ng + 2], refs[ng + 2:2 * ng + 2], refs[2 * ng + 2:]
        i, j = pl.program_id(0), pl.program_id(1)
        if ng:
            phases = _gather_phases(g_in, g_out, *sems)
            for step, phase in enumerate(phases[:3]):
                @pl.when(jnp.logical_and(i == step, j == 0))
                def _(phase=phase):
                    phase()

        @pl.when(j == 0)
        def _():
            xv = x_ref[...]
            r = lax.rsqrt(jnp.mean(xv * xv, axis=-1, keepdims=True) + EPS)
            u_ref[...] = (xv * r * nw_ref[...]).astype(bf16)
        proj_ref[...] = _nt(u_ref[...], w_ref[pl.ds(pl.multiple_of(j * tn, tn), tn), :])

        if ng:
            @pl.when(jnp.logical_and(i == ni - 1, j == nj - 1))
            def _():
                phases[3]()

    anyspec = pl.BlockSpec(memory_space=pl.ANY)
    outs = pl.pallas_call(
        body, name="prenorm_inproj", grid=(ni, nj),
        in_specs=[pl.BlockSpec((tm, d), lambda i, j: (i, 0)), pl.BlockSpec((1, d), lambda i, j: (0, 0)),
                  pl.BlockSpec((npad, d), lambda i, j: (0, 0))] + [anyspec] * ng,
        out_specs=[pl.BlockSpec((tm, tn), lambda i, j: (i, j)), pl.BlockSpec((tm, d), lambda i, j: (i, 0))]
        + [anyspec] * ng,
        out_shape=[SDS((s, npad), f32), SDS((s, d), bf16)] + [SDS((N_DEV,) + a.shape, a.dtype) for a in gather],
        scratch_shapes=_gather_scratch(ng) if ng else [],
        compiler_params=pltpu.CompilerParams(dimension_semantics=("arbitrary", "arbitrary")),
    )(x, nw, wt, *gather)
    return outs[0], outs[1], outs[2:]


def _attn_consts():
    head0 = _iota((BLK, LANES), 1) < HEAD_DIM
    tri2 = (_iota((BLK, 2 * LANES), 1) % LANES) <= _iota((BLK, 2 * LANES), 0)
    ones2 = ((_iota((LANES, 2 * LANES), 0) < HEAD_DIM) == (_iota((LANES, 2 * LANES), 1) < LANES)).astype(bf16)
    rmat = ((_iota((2 * LANES, LANES), 0) < LANES) == (_iota((2 * LANES, LANES), 1) < HEAD_DIM)).astype(bf16)
    bones = ((_iota((LANES, LANES), 0) < HEAD_DIM) == (_iota((LANES, LANES), 1) < HEAD_DIM)).astype(bf16)
    return head0, tri2, ones2, rmat, bones


def _stack_heads(x16, head0):
    zero = jnp.zeros_like(x16)
    return jnp.concatenate([jnp.where(head0, x16, zero), jnp.where(head0, zero, x16)], axis=0)


def _bf16_terms(x, terms):
    out = []
    for _ in range(terms):
        t = x.astype(bf16)
        out.append(t)
        x = x - t.astype(f32)
    return out


def _dot_01(x, w16, terms):
    return _nn(jnp.concatenate(_bf16_terms(x, terms), axis=1), jnp.concatenate([w16] * terms, axis=0))


def _split_dot_sum(x, w16):
    hi, lo = _bf16_terms(x, 2)
    return _nn(hi, w16) + _nn(lo, w16)


def _dot_01_left(w16, x, terms):
    return _nn(jnp.concatenate([w16] * terms, axis=1), jnp.concatenate(_bf16_terms(x, terms), axis=0))


def _attn_fwd(proj):
    s = proj.shape[0]
    n_it = s // BLK

    def body(q_ref, k_ref, v_ref, g_ref, o_ref, l_ref, mix_ref, op0, op1, op2, lp0, lp1, lp2,
             s_a, s_b, sd_a, sd_b, p_a, p_b, m_a, m_b, pd_a, pd_b, k_a, k_b, v_a, v_b):
        op_refs, lp_refs = (op0, op1, op2), (lp0, lp1, lp2)
        head0, tri2, ones2, rmat, _ = _attn_consts()
        score_bufs, prob_bufs = ((s_a, sd_a), (s_b, sd_b)), ((p_a, m_a, pd_a), (p_b, m_b, pd_b))
        k_bufs, v_bufs = (k_a, k_b), (v_a, v_b)
        for buf in k_bufs + v_bufs:
            buf[...] = jnp.zeros_like(buf)

        def block_rows(i, d, nb):
            r, blk = i // nb, i % nb
            return pl.ds(blk * (BLK * d) + r, BLK, stride=d), blk > 0

        def unstack(st16):
            return st16[:BLK] + st16[BLK:]

        def scores(i, par, d, nb):
            rows, has_prev = block_rows(i, d, nb)
            s_buf, sd_buf = score_bufs[par]
            qs = q_ref[rows, :] * 0.125
            qs16 = qs.astype(bf16)
            kst_c = _stack_heads(k_ref[rows, :].astype(bf16), head0)
            kst_p = k_bufs[1 - par][...]
            k_bufs[par][...] = kst_c
            sc = _nt(qs16, kst_c)
            sp = _nt(qs16, kst_p)
            s_buf[...] = jnp.where(tri2, sc, jnp.where(has_prev, sp, -jnp.inf))
            sd = _nn((qs * unstack(kst_p).astype(f32)).astype(bf16), ones2)
            sd_buf[...] = jnp.where(has_prev, sd, -jnp.inf)

        def softmax(bufs_in, bufs_out):
            s_buf, sd_buf = bufs_in
            p_buf, m_buf, pd_buf = bufs_out
            sc, sd2 = s_buf[...], sd_buf[...]
            m0 = jnp.max(sc[:, :LANES], axis=1, keepdims=True)
            m1 = jnp.max(sc[:, LANES:], axis=1, keepdims=True)
            m2 = jnp.concatenate([jnp.broadcast_to(m0, (BLK, LANES)), jnp.broadcast_to(m1, (BLK, LANES))], axis=1)
            m2 = jnp.maximum(m2, sd2)
            p_buf[...] = jnp.exp(sc - m2).astype(bf16)
            m_pair = jnp.where(head0, m2[:, :LANES], m2[:, LANES:])
            m_buf[...] = m_pair
            pd_buf[...] = jnp.exp(jnp.where(head0, sd2[:, :LANES], sd2[:, LANES:]) - m_pair)

        def output(i, par, d, nb, p):
            rows, _ = block_rows(i, d, nb)
            p_buf, m_buf, pd_buf = prob_bufs[par]
            vst_c = _stack_heads(v_ref[rows, :].astype(bf16), head0)
            vst_p = v_bufs[1 - par][...]
            v_bufs[par][...] = vst_c
            pt16, pd = p_buf[...], pd_buf[...]
            zero = jnp.zeros_like(pt16)
            o = (_nn(jnp.where(tri2, pt16, zero), vst_c) + _nn(jnp.where(tri2, zero, pt16), vst_p)
                 + pd * unstack(vst_p).astype(f32))
            l = _nn(pt16, rmat) + pd
            op_refs[p][rows, :] = o / l
            lp_refs[p][rows, :] = m_buf[...] + jnp.log(l)

        for p, d in enumerate(DILATIONS):
            nb = s // (BLK * d)
            scores(0, 0, d, nb)
            scores(1, 1, d, nb)
            softmax(score_bufs[0], prob_bufs[0])

            def steps(j, carry, d=d, nb=nb, p=p):
                for par in range(2):
                    t = 2 * j + 2 + par
                    scores(t, par, d, nb)
                    output(t - 2, par, d, nb, p)
                    softmax(score_bufs[1 - par], prob_bufs[1 - par])
                return carry

            lax.fori_loop(0, (n_it - 2) // 2, steps, 0, unroll=True)
            output(n_it - 2, 0, d, nb, p)
            softmax(score_bufs[1], prob_bufs[1])
            output(n_it - 1, 1, d, nb, p)

        def merge(i, carry):
            rows = pl.ds(pl.multiple_of(i * 256, 256), 256)
            l0, l1, l2 = lp0[rows, :], lp1[rows, :], lp2[rows, :]
            m = jnp.maximum(jnp.maximum(l0, l1), l2)
            e0, e1, e2 = jnp.exp(l0 - m), jnp.exp(l1 - m), jnp.exp(l2 - m)
            z = e0 + e1 + e2
            o = (e0 * op0[rows, :] + e1 * op1[rows, :] + e2 * op2[rows, :]) / z
            o_ref[rows, :] = o
            l_ref[rows, :] = m + jnp.log(z)
            g = g_ref[rows, :]
            mix_ref[rows, :] = (o * (g * _sigmoid(g))).astype(bf16)
            return carry

        lax.fori_loop(0, s // 256, merge, 0)

    col = lambda base: pl.BlockSpec((s, LANES), lambda h: (0, base + h))
    return pl.pallas_call(
        body, name="attn_fwd", grid=(N_PAIRS,),
        in_specs=[col(0), col(8), col(16), col(24)],
        out_specs=[col(0), col(0), col(0)],
        out_shape=[SDS((s, D_ATTN), f32), SDS((s, D_ATTN), f32), SDS((s, D_ATTN), bf16)],
        scratch_shapes=[pltpu.VMEM((s, LANES), f32)] * 6 + [pltpu.VMEM((BLK, 2 * LANES), f32)] * 4
        + [pltpu.VMEM((BLK, 2 * LANES), bf16)] * 2 + [pltpu.VMEM((BLK, LANES), f32)] * 4
        + [pltpu.VMEM((2 * BLK, LANES), bf16)] * 4,
        compiler_params=pltpu.CompilerParams(dimension_semantics=("parallel",)),
    )(proj, proj, proj, proj)


def _expand_mat():
    colv = _iota((LANES, 2 * D_SSM), 1)
    head = 2 * ((colv % D_SSM) // LANES) + colv // D_SSM
    return (_iota((LANES, 2 * D_SSM), 0) == head).astype(bf16)


def _fold_mat():
    return (_iota((D_SSM, LANES), 0) // HEAD_DIM == _iota((D_SSM, LANES), 1)).astype(bf16)


def _conv(xs_ref, bc_ref, xs_tail, bc_tail, cw_ref, cb_ref, xpad, first):
    keep = jnp.where(first, 0.0, 1.0)
    xpad[0:8, 0:D_SSM] = xs_tail[...] * keep
    xpad[0:8, D_SSM:D_CONV] = bc_tail[...] * keep
    xpad[8:8 + CHUNK, 0:D_SSM] = xs_ref[...]
    xpad[8:8 + CHUNK, D_SSM:D_CONV] = bc_ref[...]
    xp = xpad[...]
    cv = cb_ref[...] + cw_ref[3:4, :] * xp[8:8 + CHUNK]
    for j in range(3):
        cv = cv + cw_ref[j:j + 1, :] * pltpu.roll(xp, 3 - j, 0)[8:8 + CHUNK]
    return cv


def _decay_terms(dt_ref, dtb_ref, alog16_ref, emat_ref):
    pre = dt_ref[...] + dtb_ref[...]
    dt16 = _softplus(pre)
    a16 = -jnp.exp(alog16_ref[...])
    sub, lane = _iota((CHUNK, CHUNK), 0), _iota((CHUNK, CHUNK), 1)
    tri = (sub >= lane).astype(f32)
    al16 = _nn_hi(tri, dt16 * a16)
    al_t = al16.T
    emat = emat_ref[...]
    dt_x = _dot_01(dt16, emat, 3)
    al_x = _dot_01(al16, emat, 3)
    lane_w = _iota((CHUNK, D_SSM), 1)
    even = (lane_w % LANES) < HEAD_DIM
    dt_f = jnp.where(even, dt_x[:, :D_SSM], dt_x[:, D_SSM:])
    al_f = jnp.where(even, al_x[:, :D_SSM], al_x[:, D_SSM:])
    return pre, dt_f, al_f, al_x, al_t


def _decay_mat(al_x, al_t, pair, h):
    sub, lane = _iota((CHUNK, CHUNK), 0), _iota((CHUNK, CHUNK), 1)
    col = al_x[:, h * D_SSM + pair * LANES: h * D_SSM + (pair + 1) * LANES]
    row = al_t[2 * pair + h: 2 * pair + h + 1, :]
    return jnp.exp(jnp.where(sub >= lane, col - row, -jnp.inf))


def _ssd_in_specs(order):
    blk = lambda w, cb: pl.BlockSpec((CHUNK, w), lambda i: (order(i), cb))
    tail = lambda w, cb: pl.BlockSpec((8, w), lambda i: (jnp.maximum(16 * order(i) - 1, 0), cb))
    return [blk(D_SSM, COL_XS // D_SSM), blk(512, COL_BC // 512), tail(D_SSM, COL_XS // D_SSM),
            tail(512, COL_BC // 512), blk(LANES, COL_DT // LANES), blk(D_SSM, COL_Z // D_SSM)]


def _full(shape):
    return pl.BlockSpec(shape, lambda i: (0,) * len(shape))


def _ssd_fwd(proj, conv_w, conv_b, dtb16, alog16, alog_f, d_f, nw):
    s = proj.shape[0]
    nc = s // CHUNK

    def body(xs_ref, bc_ref, xs_tail, bc_tail, dt_ref, z_ref, cw_ref, cb_ref, dtb_ref, alog16_ref, alogf_ref,
             df_ref, nw_ref, mix_ref, y_ref, st_ref, cv_ref, h_scr, xpad, y_scr, emat_ref):
        c = pl.program_id(0)

        @pl.when(c == 0)
        def _():
            h_scr[...] = jnp.zeros_like(h_scr)
            emat_ref[...] = _expand_mat()

        cv = _conv(xs_ref, bc_ref, xs_tail, bc_tail, cw_ref, cb_ref, xpad, c == 0)
        cv_ref[...] = cv
        xbc = cv * _sigmoid(cv)
        _, dt_f, al_f, al_x, al_t = _decay_terms(dt_ref, dtb_ref, alog16_ref, emat_ref)
        head0 = _iota((CHUNK, LANES), 1) < HEAD_DIM
        st_ref[...] = h_scr[...]
        for g in range(N_GROUPS):
            bm = xbc[:, D_SSM + g * D_STATE: D_SSM + (g + 1) * D_STATE].astype(bf16)
            cm = xbc[:, D_SSM + (N_GROUPS + g) * D_STATE: D_SSM + (N_GROUPS + g + 1) * D_STATE].astype(bf16)
            gmat = _nt(cm, bm)
            for pair in range(4 * g, 4 * g + 4):
                sl = slice(pair * LANES, (pair + 1) * LANES)
                xp, dtp, alp = xbc[:, sl], dt_f[:, sl], al_f[:, sl]
                xdt = xp * dtp
                xdt16 = xdt.astype(bf16)
                al_last = alp[CHUNK - 1:CHUNK, :]
                hp = h_scr[:, sl]
                y_off = jnp.exp(alp) * _nn(cm, hp.astype(bf16))
                yd = [_nn((gmat * _decay_mat(al_x, al_t, pair, h)).astype(bf16), xdt16) for h in range(2)]
                y_scr[:, sl] = jnp.where(head0, yd[0], yd[1]) + y_off + df_ref[:, sl] * xp
                st = _tn(bm, (jnp.exp(al_last - alp) * xdt).astype(bf16))
                h_scr[:, sl] = jnp.exp(al_last) * hp + st
        y = y_scr[...]
        y_ref[...] = y
        z = z_ref[...]
        yz = y * (z * _sigmoid(z))
        gw = D_SSM // N_GROUPS
        for g in range(N_GROUPS):
            part = yz[:, g * gw:(g + 1) * gw]
            r = lax.rsqrt(jnp.mean(part * part, axis=-1, keepdims=True) + EPS)
            mix_ref[:, g * gw:(g + 1) * gw] = (part * r * nw_ref[:, g * gw:(g + 1) * gw]).astype(bf16)

    order = lambda i: i
    row = lambda w: pl.BlockSpec((CHUNK, w), lambda i: (i, 0))
    return pl.pallas_call(
        body, name="ssd_fwd", grid=(nc,),
        in_specs=_ssd_in_specs(order) + [_full((4, D_CONV)), _full((1, D_CONV)), _full((1, LANES)), _full((1, LANES)),
                                         _full((1, D_SSM)), _full((1, D_SSM)), _full((1, D_SSM))],
        out_specs=[row(D_SSM), row(D_SSM), pl.BlockSpec((None, D_STATE, D_SSM), lambda i: (i, 0, 0)), row(D_CONV)],
        out_shape=[SDS((s, D_SSM), bf16), SDS((s, D_SSM), f32), SDS((nc, D_STATE, D_SSM), f32),
                   SDS((s, D_CONV), f32)],
        scratch_shapes=[pltpu.VMEM((D_STATE, D_SSM), f32), pltpu.VMEM((8 + CHUNK, D_CONV), f32),
                        pltpu.VMEM((CHUNK, D_SSM), f32), pltpu.VMEM((LANES, 2 * D_SSM), bf16)],
        compiler_params=pltpu.CompilerParams(dimension_semantics=("arbitrary",)),
    )(proj, proj, proj, proj, proj, proj, conv_w, conv_b, dtb16, alog16, alog_f, d_f, nw)


def _outproj_loss(mix_a, mix_s, wo, x, tgt, npw):
    s, d = x.shape
    tm = 512

    def body(ma_ref, ms_ref, wo_ref, x_ref, t_ref, npw_ref, dmix_ref, dres_ref, acc_ref, dwo_ref):
        @pl.when(pl.program_id(0) == 0)
        def _():
            acc_ref[...] = jnp.zeros_like(acc_ref)
            dwo_ref[...] = jnp.zeros_like(dwo_ref)

        out = _nn(ma_ref[...], wo_ref[0:D_ATTN, :]) + _nn(ms_ref[...], wo_ref[D_ATTN:, :])
        r = lax.rsqrt(jnp.mean(out * out, axis=-1, keepdims=True) + EPS)
        on = out * r
        diff = x_ref[...] + on * npw_ref[...] - t_ref[...]
        dres = diff * (1.0 / d)
        dres_ref[...] = dres
        acc_ref[0:1, :] += jnp.sum(diff * diff, axis=0, keepdims=True)
        acc_ref[1:2, :] += jnp.sum(dres * on, axis=0, keepdims=True)
        dn = dres * npw_ref[...]
        dout = (r * (dn - on * jnp.mean(dn * on, axis=-1, keepdims=True))).astype(bf16)
        dmix_ref[...] = _nt(dout, wo_ref[...])
        dwo_ref[0:D_ATTN, :] += _tn(ma_ref[...], dout)
        dwo_ref[D_ATTN:, :] += _tn(ms_ref[...], dout)

    row = lambda w: pl.BlockSpec((tm, w), lambda i: (i, 0))
    return pl.pallas_call(
        body, name="outproj_loss", grid=(s // tm,),
        in_specs=[row(D_ATTN), row(D_SSM), _full((D_ATTN + D_SSM, d)), row(d), row(d), _full((1, d))],
        out_specs=[row(D_ATTN + D_SSM), row(d), _full((8, d)), _full((D_ATTN + D_SSM, d))],
        out_shape=[SDS((s, D_ATTN + D_SSM), f32), SDS((s, d), f32), SDS((8, d), f32), SDS((D_ATTN + D_SSM, d), f32)],
        compiler_params=pltpu.CompilerParams(dimension_semantics=("arbitrary",)),
    )(mix_a, mix_s, wo, x, tgt, npw)


def _attn_bwd(proj, o, lb, dmix, swap=None):
    s = proj.shape[0]
    n_it = s // BLK

    nsw = 0 if swap is None else 1

    def body(*refs):
        q_ref, k_ref, v_ref, g_ref, o_ref, l_ref, dm_ref = refs[:7]
        swap_in = refs[7:7 + nsw]
        dq_ref, dk_ref, dv_ref, dg_ref = refs[7 + nsw:11 + nsw]
        swap_out = refs[11 + nsw:11 + 2 * nsw]
        dq_acc, dk_acc, dv_acc, do_scr, dl_scr = refs[11 + 2 * nsw:16 + 2 * nsw]
        bufs = refs[16 + 2 * nsw:44 + 2 * nsw]
        swap_sems = refs[44 + 2 * nsw:]
        head0, tri2, _, _, bones = _attn_consts()

        if nsw:
            x, y, c = _my_pos()
            swap_copy = pltpu.make_async_remote_copy(
                src_ref=swap_in[0], dst_ref=swap_out[0], send_sem=swap_sems[0], recv_sem=swap_sems[1],
                device_id=(x, y, 1 - c), device_id_type=MESH)

            @pl.when(pl.program_id(0) == 0)
            def _():
                swap_copy.start()

        def pro(i, carry):
            rows = pl.ds(pl.multiple_of(i * 256, 256), 256)
            g = g_ref[rows, :]
            sg = _sigmoid(g)
            dmx = dm_ref[rows, :]
            ov = o_ref[rows, :]
            dg_ref[rows, :] = (dmx * ov * (sg * (1.0 + g * (1.0 - sg)))).astype(bf16)
            do = dmx * (g * sg)
            do_scr[rows, :] = do
            dl_scr[rows, :] = _split_dot_sum(do * ov, bones)
            z = jnp.zeros((256, LANES), f32)
            dq_acc[rows, :] = z
            dk_acc[rows, :] = z
            dv_acc[rows, :] = z
            return carry

        lax.fori_loop(0, s // 256, pro, 0, unroll=2)

        def per_head(t):
            return jnp.concatenate([t[:, :LANES], t[:, LANES:]], axis=0)

        def both_heads(t):
            tr = pltpu.roll(t, HEAD_DIM, 1)
            return jnp.concatenate([jnp.where(head0, t, tr), jnp.where(head0, tr, t)], axis=1)

        mm_bufs = ((bufs[0], bufs[1], bufs[2], bufs[3]), (bufs[4], bufs[5], bufs[6], bufs[7]))
        ds_bufs = ((bufs[8], bufs[9], bufs[10], bufs[11]), (bufs[12], bufs[13], bufs[14], bufs[15]))
        op_bufs = ((bufs[16], bufs[17], bufs[18], bufs[19]), (bufs[20], bufs[21], bufs[22], bufs[23]))
        vc_bufs, carry_k, carry_v = (bufs[24], bufs[25]), bufs[26], bufs[27]
        for buf in (op_bufs[0][0], op_bufs[1][0]) + vc_bufs:
            buf[...] = jnp.zeros_like(buf)

        def block_rows(i, d, nb):
            r, blk = i // nb, i % nb
            return pl.ds(blk * (BLK * d) + r, BLK, stride=d), blk > 0

        def unstack(st16):
            return st16[:BLK] + st16[BLK:]

        def products(i, par, d, nb):
            rows, has_prev = block_rows(i, d, nb)
            s_buf, dp_buf, sd_buf, dpd_buf = mm_bufs[par]
            kc_buf, kp_buf, q_buf, do_buf = op_bufs[par]
            q = q_ref[rows, :]
            qs = q * 0.125
            do = do_scr[rows, :]
            qs16, do16 = qs.astype(bf16), do.astype(bf16)
            kst_c = _stack_heads(k_ref[rows, :].astype(bf16), head0)
            vst_c = _stack_heads(v_ref[rows, :].astype(bf16), head0)
            kst_p, vst_p = op_bufs[1 - par][0][...], vc_bufs[1 - par][...]
            kc_buf[...] = kst_c
            kp_buf[...] = kst_p
            vc_bufs[par][...] = vst_c
            q_buf[...] = q.astype(bf16)
            do_buf[...] = do16
            s_buf[...] = jnp.where(tri2, _nt(qs16, kst_c), jnp.where(has_prev, _nt(qs16, kst_p), -jnp.inf))
            dp_buf[...] = jnp.where(tri2, _nt(do16, vst_c), jnp.where(has_prev, _nt(do16, vst_p), 0.0))
            sd_buf[...] = _nn((qs * unstack(kst_p).astype(f32)).astype(bf16), bones)
            dpd_buf[...] = jnp.where(has_prev, _nn((do * unstack(vst_p).astype(f32)).astype(bf16), bones), 0.0)

        def softmax_grad(i, par, d, nb):
            rows, has_prev = block_rows(i, d, nb)
            s_buf, dp_buf, sd_buf, dpd_buf = mm_bufs[par]
            p_buf, ds_buf, pd_buf, dsd_buf = ds_bufs[par]
            lse = l_ref[rows, :]
            dl = dl_scr[rows, :]
            pt = jnp.exp(s_buf[...] - both_heads(lse))
            ds_buf[...] = (pt * (dp_buf[...] - both_heads(dl)) * 0.125).astype(bf16)
            p_buf[...] = pt.astype(bf16)
            pd = jnp.where(has_prev, jnp.exp(sd_buf[...] - lse), 0.0)
            pd_buf[...] = pd
            dsd_buf[...] = pd * (dpd_buf[...] - dl) * 0.125

        def accumulate(i, par, d, nb):
            rows, _ = block_rows(i, d, nb)
            before, _ = block_rows(jnp.maximum(i - 1, 0), d, nb)
            p_buf, ds_buf, pd_buf, dsd_buf = ds_bufs[par]
            kc_buf, kp_buf, q_buf, do_buf = op_bufs[par]
            pt16, ds16, pd, dsd = p_buf[...], ds_buf[...], pd_buf[...], dsd_buf[...]
            zero = jnp.zeros_like(pt16)
            dsc, dsp = jnp.where(tri2, ds16, zero), jnp.where(tri2, zero, ds16)
            pc, pp = jnp.where(tri2, pt16, zero), jnp.where(tri2, zero, pt16)
            kst_c, kst_p, q16, do16 = kc_buf[...], kp_buf[...], q_buf[...], do_buf[...]
            qst, dost = _stack_heads(q16, head0), _stack_heads(do16, head0)
            dq_acc[rows, :] += _nn(dsc, kst_c) + _nn(dsp, kst_p) + dsd * unstack(kst_p).astype(f32)
            dk2 = _tn(jnp.concatenate([per_head(dsc), per_head(dsp)], axis=1), qst)
            dv2 = _tn(jnp.concatenate([per_head(pc), per_head(pp)], axis=1), dost)
            dk_acc[before, :] += carry_k[...] + dk2[BLK:] + dsd * q16.astype(f32)
            dv_acc[before, :] += carry_v[...] + dv2[BLK:] + pd * do16.astype(f32)
            carry_k[...] = dk2[:BLK]
            carry_v[...] = dv2[:BLK]

        for d in DILATIONS:
            nb = s // (BLK * d)
            carry_k[...] = jnp.zeros_like(carry_k)
            carry_v[...] = jnp.zeros_like(carry_v)
            products(0, 0, d, nb)
            products(1, 1, d, nb)
            softmax_grad(0, 0, d, nb)

            def steps(j, carry, d=d, nb=nb):
                for par in range(2):
                    t = 2 * j + 2 + par
                    accumulate(t - 2, par, d, nb)
                    products(t, par, d, nb)
                    softmax_grad(t - 1, 1 - par, d, nb)
                return carry

            lax.fori_loop(0, (n_it - 2) // 2, steps, 0, unroll=True)
            accumulate(n_it - 2, 0, d, nb)
            softmax_grad(n_it - 1, 1, d, nb)
            accumulate(n_it - 1, 1, d, nb)
            last, _ = block_rows(n_it - 1, d, nb)
            dk_acc[last, :] += carry_k[...]
            dv_acc[last, :] += carry_v[...]

        def epi(i, carry):
            rows = pl.ds(pl.multiple_of(i * 256, 256), 256)
            dq_ref[rows, :] = dq_acc[rows, :].astype(bf16)
            dk_ref[rows, :] = dk_acc[rows, :].astype(bf16)
            dv_ref[rows, :] = dv_acc[rows, :].astype(bf16)
            return carry

        lax.fori_loop(0, s // 256, epi, 0)

        if nsw:
            @pl.when(pl.program_id(0) == N_PAIRS - 1)
            def _():
                swap_copy.wait_send()
                swap_copy.wait_recv()

    col = lambda base: pl.BlockSpec((s, LANES), lambda h: (0, base + h))
    anyspec = pl.BlockSpec(memory_space=pl.ANY)
    swaps = [] if swap is None else [swap]
    outs = pl.pallas_call(
        body, name="attn_bwd", grid=(N_PAIRS,),
        in_specs=[col(0), col(8), col(16), col(24), col(0), col(0), col(0)] + [anyspec] * nsw,
        out_specs=[col(0)] * 4 + [anyspec] * nsw,
        out_shape=[SDS((s, D_ATTN), bf16)] * 4 + [SDS(a.shape, a.dtype) for a in swaps],
        scratch_shapes=[pltpu.VMEM((s, LANES), f32)] * 5
        + [pltpu.VMEM((BLK, 2 * LANES), f32)] * 2 + [pltpu.VMEM((BLK, LANES), f32)] * 2
        + [pltpu.VMEM((BLK, 2 * LANES), f32)] * 2 + [pltpu.VMEM((BLK, LANES), f32)] * 2
        + [pltpu.VMEM((BLK, 2 * LANES), bf16)] * 2 + [pltpu.VMEM((BLK, LANES), f32)] * 2
        + [pltpu.VMEM((BLK, 2 * LANES), bf16)] * 2 + [pltpu.VMEM((BLK, LANES), f32)] * 2
        + [pltpu.VMEM((2 * BLK, LANES), bf16)] * 2 + [pltpu.VMEM((BLK, LANES), bf16)] * 2
        + [pltpu.VMEM((2 * BLK, LANES), bf16)] * 2 + [pltpu.VMEM((BLK, LANES), bf16)] * 2
        + [pltpu.VMEM((2 * BLK, LANES), bf16)] * 2 + [pltpu.VMEM((BLK, LANES), f32)] * 2
        + [pltpu.SemaphoreType.DMA(())] * (2 * nsw),
        compiler_params=pltpu.CompilerParams(dimension_semantics=("arbitrary",)),
    )(proj, proj, proj, proj, o, lb, dmix, *swaps)
    return outs


def _ssd_bwd(proj, y, states, cv, dmix, conv_w, conv_b, dtb16, alog16, alog_f, d_f, nw, chip_sums=()):
    s = proj.shape[0]
    nc = s // CHUNK
    gw = D_SSM // N_GROUPS
    nx = len(chip_sums)

    def body(*refs):
        (xs_ref, bc_ref, _, _, dt_ref, z_ref, y_ref, st_ref, dm_ref, cw_ref, cb_ref, dtb_ref,
         alog16_ref, alogf_ref, df_ref, nw_ref, cv_ref) = refs[:17]
        cs_in = refs[17:17 + nx]
        out_ref, gconv_ref, gvec_ref, gdt_ref = refs[17 + nx:21 + nx]
        cs_out = refs[21 + nx:21 + 2 * nx]
        (dh_scr, head_scr, dcpad, da_scr, dxdt_scr, dbc_scr, emat_ref, fold_ref) = refs[21 + 2 * nx:29 + 2 * nx]
        cs_sems = refs[29 + 2 * nx:]
        i = pl.program_id(0)
        c = nc - 1 - i

        if nx:
            @pl.when(i == 0)
            def _():
                mine, sends, _ = _chip_exchange_copies(cs_in, cs_out, *cs_sems)
                for cp in mine + sends:
                    cp.start()

            @pl.when(i == nc - 1)
            def _():
                mine, sends, recvs = _chip_exchange_copies(cs_in, cs_out, *cs_sems)
                for cp in recvs:
                    cp.wait_recv()
                for cp in sends:
                    cp.wait_send()
                for cp in mine:
                    cp.wait()

        @pl.when(i == 0)
        def _():
            emat_ref[...] = _expand_mat()
            fold_ref[...] = _fold_mat()
            dh_scr[...] = jnp.zeros_like(dh_scr)
            head_scr[...] = jnp.zeros_like(head_scr)
            gconv_ref[...] = jnp.zeros_like(gconv_ref)
            gvec_ref[...] = jnp.zeros_like(gvec_ref)
            gdt_ref[...] = jnp.zeros_like(gdt_ref)

        cv = cv_ref[...]
        sig = _sigmoid(cv)
        xbc = cv * sig
        pre, dt_f, al_f, al_x, al_t = _decay_terms(dt_ref, dtb_ref, alog16_ref, emat_ref)
        head0 = _iota((CHUNK, LANES), 1) < HEAD_DIM
        sub = _iota((CHUNK, LANES), 0)
        last_row = sub == CHUNK - 1

        yv, z, dmx = y_ref[...], z_ref[...], dm_ref[...]
        sz = _sigmoid(z)
        silu = z * sz
        yz = yv * silu
        dyz_parts = []
        for g in range(N_GROUPS):
            gs = slice(g * gw, (g + 1) * gw)
            part = yz[:, gs]
            r = lax.rsqrt(jnp.mean(part * part, axis=-1, keepdims=True) + EPS)
            nh = part * r
            gvec_ref[0:1, gs] += jnp.sum(dmx[:, gs] * nh, axis=0, keepdims=True)
            dn = dmx[:, gs] * nw_ref[:, gs]
            dyz_parts.append(r * (dn - nh * jnp.mean(dn * nh, axis=-1, keepdims=True)))
        dyz = jnp.concatenate(dyz_parts, axis=1)
        dy = dyz * silu
        out_ref[:, 0:D_SSM] = (dyz * yv * (sz * (1.0 + z * (1.0 - sz)))).astype(bf16)

        x_all = xbc[:, 0:D_SSM]
        gvec_ref[2:3, :] += jnp.sum(dy * x_all, axis=0, keepdims=True)

        for g in range(N_GROUPS):
            bm = xbc[:, D_SSM + g * D_STATE: D_SSM + (g + 1) * D_STATE].astype(bf16)
            cm = xbc[:, D_SSM + (N_GROUPS + g) * D_STATE: D_SSM + (N_GROUPS + g + 1) * D_STATE].astype(bf16)
            gmat = _nt(cm, bm)
            dgm = jnp.zeros((CHUNK, CHUNK), f32)
            db = jnp.zeros((CHUNK, D_STATE), f32)
            dc = jnp.zeros((CHUNK, D_STATE), f32)
            for pair in range(4 * g, 4 * g + 4):
                sl = slice(pair * LANES, (pair + 1) * LANES)
                xp, dtp, alp, dyp = x_all[:, sl], dt_f[:, sl], al_f[:, sl], dy[:, sl]
                xdt = xp * dtp
                xdt16 = xdt.astype(bf16)
                al_last = alp[CHUNK - 1:CHUNK, :]
                e_l = jnp.exp(alp)
                wf = jnp.exp(al_last - alp)
                e_last = jnp.exp(al_last)
                hp = st_ref[:, sl]
                hp16 = hp.astype(bf16)
                dhn = dh_scr[:, sl]
                dhn16 = dhn.astype(bf16)
                y_off = e_l * _nn(cm, hp16)
                dch16 = (dyp * e_l).astype(bf16)
                dc = dc + _nt(dch16, hp16)
                dh_out = _tn(cm, dch16)
                dal = dyp * y_off
                xw16 = (wf * xdt).astype(bf16)
                db = db + _nt(xw16, dhn16)
                dxw = _nn(bm, dhn16)
                dxdt = dxw * wf
                dwf = dxw * xdt * wf
                dal = dal - dwf
                dal_last = jnp.sum(dwf, axis=0, keepdims=True) + jnp.sum(dhn * hp, axis=0, keepdims=True) * e_last
                dh_scr[:, sl] = e_last * dhn + dh_out
                for h in range(2):
                    mh = head0 if h == 0 else jnp.logical_not(head0)
                    dyh16 = jnp.where(mh, dyp, 0.0).astype(bf16)
                    lmat = _decay_mat(al_x, al_t, pair, h)
                    mm = gmat * lmat
                    dmm = _nt(dyh16, xdt16)
                    dxdt = dxdt + _tn(mm.astype(bf16), dyh16)
                    n16 = (dmm * mm).astype(bf16)
                    jh = jnp.where(mh, 1.0 / HEAD_DIM, 0.0).astype(bf16)
                    dal = dal + _nn(n16, jh) - _tn(n16, jh)
                    dgm = dgm + dmm * lmat
                da_scr[:, sl] = dal + jnp.where(last_row, dal_last, 0.0)
                dxdt_scr[:, sl] = dxdt
            dgm16 = dgm.astype(bf16)
            dbc_scr[:, g * D_STATE:(g + 1) * D_STATE] = db + _tn(dgm16, cm)
            dbc_scr[:, (N_GROUPS + g) * D_STATE:(N_GROUPS + g + 1) * D_STATE] = dc + _nn(dgm16, bm)

        sub_c, lane_c = _iota((CHUNK, CHUNK), 0), _iota((CHUNK, CHUNK), 1)
        tri_t = (lane_c >= sub_c).astype(bf16)
        dadt = _dot_01_left(tri_t, da_scr[...], 2)
        a_f = -jnp.exp(alogf_ref[...])
        dxdt_all = dxdt_scr[...]
        ddt_f = dxdt_all * x_all + a_f * dadt
        gvec_ref[1:2, :] += jnp.sum(dt_f * dadt, axis=0, keepdims=True) * a_f
        dx = df_ref[...] * dy + dxdt_all * dt_f
        ddt_raw = _dot_01(ddt_f, fold_ref[...], 2) * _sigmoid(pre)
        gdt_ref[0:1, :] += jnp.sum(ddt_raw, axis=0, keepdims=True)
        out_ref[:, D_SSM + D_CONV:D_SSM + D_CONV + LANES] = ddt_raw.astype(bf16)
        out_ref[:, D_SSM + D_CONV + LANES:] = jnp.zeros((CHUNK, 3 * LANES), bf16)

        dsil = sig * (1.0 + cv * (1.0 - sig))
        dcv_x = dx * dsil[:, 0:D_SSM]
        dcv_bc = dbc_scr[...] * dsil[:, D_SSM:]
        dcpad[0:CHUNK, 0:D_SSM] = dcv_x
        dcpad[0:CHUNK, D_SSM:] = dcv_bc
        dcpad[CHUNK:, :] = head_scr[...]
        dcp = dcpad[...]
        dcv = dcp[0:CHUNK]
        gconv_ref[4:5, :] += jnp.sum(dcv, axis=0, keepdims=True)
        x_raw = jnp.concatenate([xs_ref[...], bc_ref[...]], axis=1)
        draw = cw_ref[3:4, :] * dcv
        gconv_ref[3:4, :] += jnp.sum(dcv * x_raw, axis=0, keepdims=True)
        for j in range(3):
            ahead = pltpu.roll(dcp, CHUNK + 8 - (3 - j), 0)[0:CHUNK]
            draw = draw + cw_ref[j:j + 1, :] * ahead
            gconv_ref[j:j + 1, :] += jnp.sum(ahead * x_raw, axis=0, keepdims=True)
        head_scr[...] = dcv[0:8]
        out_ref[:, D_SSM:D_SSM + D_CONV] = draw.astype(bf16)

    order = lambda i: nc - 1 - i
    row = lambda w, cb=0: pl.BlockSpec((CHUNK, w), lambda i: (nc - 1 - i, cb))
    anyspec = pl.BlockSpec(memory_space=pl.ANY)
    outs = pl.pallas_call(
        body, name="ssd_bwd", grid=(nc,),
        in_specs=_ssd_in_specs(order) + [row(D_SSM), pl.BlockSpec((None, D_STATE, D_SSM), lambda i: (nc - 1 - i, 0, 0)),
                                         row(D_SSM, 1), _full((4, D_CONV)), _full((1, D_CONV)), _full((1, LANES)),
                                         _full((1, LANES)), _full((1, D_SSM)), _full((1, D_SSM)), _full((1, D_SSM)),
                                         row(D_CONV)]
        + [anyspec] * nx,
        out_specs=[row(3072), _full((8, D_CONV)), _full((8, D_SSM)), _full((8, LANES))] + [anyspec] * nx,
        out_shape=[SDS((s, 3072), bf16), SDS((8, D_CONV), f32), SDS((8, D_SSM), f32), SDS((8, LANES), f32)]
        + [SDS(a.shape, a.dtype) for a in chip_sums],
        scratch_shapes=[pltpu.VMEM((D_STATE, D_SSM), f32), pltpu.VMEM((8, D_CONV), f32),
                        pltpu.VMEM((8 + CHUNK, D_CONV), f32),
                        pltpu.VMEM((CHUNK, D_SSM), f32), pltpu.VMEM((CHUNK, D_SSM), f32),
                        pltpu.VMEM((CHUNK, 2 * N_GROUPS * D_STATE), f32),
                        pltpu.VMEM((LANES, 2 * D_SSM), bf16), pltpu.VMEM((D_SSM, LANES), bf16)]
        + (_chip_exchange_scratch(nx) if nx else []),
        compiler_params=pltpu.CompilerParams(dimension_semantics=("arbitrary",)),
    )(proj, proj, proj, proj, proj, proj, y, states, dmix, conv_w, conv_b, dtb16, alog16, alog_f, d_f, nw, cv,
      *chip_sums)
    return outs[0], outs[1], outs[2], outs[3], outs[4:]


def _col_blocks(parts, tile):
    counts = [p.shape[1] // tile for p in parts]
    offs = [sum(counts[:t]) for t in range(len(parts))]
    return offs, counts, sum(counts)


def _bcast_copies(src_ref, out_ref, send_sems, recv_sems, local_sem):
    x, y, c = _my_pos()
    me = 4 * x + 2 * y + c
    mine = pltpu.make_async_copy(src_ref, out_ref.at[me], local_sem)
    sends, recvs = [], []
    for k in range(1, N_DEV):
        to, frm = (me + k) % N_DEV, (me + N_DEV - k) % N_DEV
        sems = dict(send_sem=send_sems.at[k - 1], recv_sem=recv_sems.at[k - 1], device_id_type=MESH)
        sends.append(pltpu.make_async_remote_copy(
            src_ref=src_ref, dst_ref=out_ref.at[me], device_id=(to // 4, (to // 2) % 2, to % 2), **sems))
        recvs.append(pltpu.make_async_remote_copy(
            src_ref=src_ref, dst_ref=out_ref.at[frm], device_id=(x, y, c), **sems))
    return mine, sends, recvs


def _bcast_scratch():
    return [pltpu.SemaphoreType.DMA((N_DEV - 1,)), pltpu.SemaphoreType.DMA((N_DEV - 1,)), pltpu.SemaphoreType.DMA(())]


def _inproj_bwd(dparts, wt, x, nw, dres, chip_sums=(), pack=None):
    s, d = x.shape
    tm, tk = 1024, 1024
    offs, counts, nk = _col_blocks(dparts, tk)
    npart, nx = len(dparts), len(chip_sums)
    npk = 0 if pack is None else 1
    ni = s // tm

    def body(*refs):
        dp_refs = refs[:npart]
        w_ref, x_ref, nw_ref, dres_ref = refs[npart:npart + 4]
        pos = npart + 4
        cs_in, pos = refs[pos:pos + nx], pos + nx
        pack_in, pos = refs[pos:pos + npk], pos + npk
        (gx_ref, gnw_ref), pos = refs[pos:pos + 2], pos + 2
        cs_out, pos = refs[pos:pos + nx], pos + nx
        pack_out, pos = refs[pos:pos + 2 * npk], pos + 2 * npk
        acc, pos = refs[pos], pos + 1
        cs_sems, pos = refs[pos:pos + 3 * min(nx, 1)], pos + 3 * min(nx, 1)
        pk_refs = refs[pos:]
        i, k = pl.program_id(0), pl.program_id(1)

        def exchange():
            return _chip_exchange_copies(cs_in, cs_out, *cs_sems)

        def pack_copies():
            return _bcast_copies(pack_in[0], pack_out[0], *pk_refs[1:4])

        def gnw_copies():
            return _bcast_copies(pk_refs[0], pack_out[1], *pk_refs[4:7])

        @pl.when(jnp.logical_and(i == 0, k == 0))
        def _():
            gnw_ref[...] = jnp.zeros_like(gnw_ref)
            if nx:
                mine, sends, _ = exchange()
                for cp in mine + sends:
                    cp.start()
            if npk:
                mine, sends, _ = pack_copies()
                for cp in [mine] + sends:
                    cp.start()

        @pl.when(k == 0)
        def _():
            acc[...] = _nn(dp_refs[0][...], w_ref[...])

        for t in range(npart):
            @pl.when(jnp.logical_and(k >= max(offs[t], 1), k < offs[t] + counts[t]))
            def _(t=t):
                acc[...] += _nn(dp_refs[t][...], w_ref[...])

        @pl.when(k == nk - 1)
        def _():
            xv = x_ref[...]
            r = lax.rsqrt(jnp.mean(xv * xv, axis=-1, keepdims=True) + EPS)
            xn = xv * r
            du = acc[...]
            gnw_ref[0:1, :] += jnp.sum(du * xn, axis=0, keepdims=True)
            dn = du * nw_ref[...]
            gx_ref[...] = dres_ref[...] + r * (dn - xn * jnp.mean(dn * xn, axis=-1, keepdims=True))

        @pl.when(jnp.logical_and(i == ni - 1, k == nk - 1))
        def _():
            if npk:
                pk_refs[0][...] = gnw_ref[...]
                mine, sends, _ = gnw_copies()
                for cp in [mine] + sends:
                    cp.start()
            if nx:
                mine, sends, recvs = exchange()
                for cp in recvs:
                    cp.wait_recv()
                for cp in sends:
                    cp.wait_send()
                for cp in mine:
                    cp.wait()
            if npk:
                for copies in (pack_copies(), gnw_copies()):
                    mine, sends, recvs = copies
                    for cp in recvs:
                        cp.wait_recv()
                    for cp in sends:
                        cp.wait_send()
                    mine.wait()

    def piece(t):
        return pl.BlockSpec((tm, tk), lambda i, k: (i, jnp.clip(k - offs[t], 0, counts[t] - 1)))

    anyspec = pl.BlockSpec(memory_space=pl.ANY)
    packs = [] if pack is None else [pack]
    pack_shapes = [] if pack is None else [SDS((N_DEV,) + pack.shape, f32), SDS((N_DEV, 8, d), f32)]
    scratch = [pltpu.VMEM((tm, d), f32)] + (_chip_exchange_scratch(nx) if nx else [])
    if npk:
        scratch += [pltpu.VMEM((8, d), f32)] + _bcast_scratch() + _bcast_scratch()
    outs = pl.pallas_call(
        body, name="inproj_bwd", grid=(ni, nk),
        in_specs=[piece(t) for t in range(npart)] + [
            pl.BlockSpec((tk, d), lambda i, k: (k, 0)),
            pl.BlockSpec((tm, d), lambda i, k: (i, 0)), pl.BlockSpec((1, d), lambda i, k: (0, 0)),
            pl.BlockSpec((tm, d), lambda i, k: (i, 0))] + [anyspec] * (nx + npk),
        out_specs=[pl.BlockSpec((tm, d), lambda i, k: (i, 0)), pl.BlockSpec((8, d), lambda i, k: (0, 0))]
        + [anyspec] * (nx + 2 * npk),
        out_shape=[SDS((s, d), f32), SDS((8, d), f32)] + [SDS(a.shape, a.dtype) for a in chip_sums] + pack_shapes,
        scratch_shapes=scratch,
        compiler_params=pltpu.CompilerParams(dimension_semantics=("arbitrary", "arbitrary")),
    )(*dparts, wt, x, nw, dres, *chip_sums, *packs)
    return outs[0], outs[1], outs[2:2 + nx], outs[2 + nx:]


def _matmul_tn(a_parts, b_parts, name):
    tile, tk = 1024, 1024
    s = a_parts[0].shape[0]
    nk = s // tk
    na, nb = len(a_parts), len(b_parts)
    offs_a, counts_a, ni = _col_blocks(a_parts, tile)
    offs_b, counts_b, nj = _col_blocks(b_parts, tile)

    def body(*refs):
        a_refs, b_refs, o_ref = refs[:na], refs[na:na + nb], refs[na + nb]
        i, j = pl.program_id(0), pl.program_id(1)

        @pl.when(pl.program_id(2) == 0)
        def _():
            o_ref[...] = jnp.zeros_like(o_ref)

        for ta in range(na):
            for tb in range(nb):
                in_a = jnp.logical_and(i >= offs_a[ta], i < offs_a[ta] + counts_a[ta])
                in_b = jnp.logical_and(j >= offs_b[tb], j < offs_b[tb] + counts_b[tb])

                @pl.when(jnp.logical_and(in_a, in_b))
                def _(ta=ta, tb=tb):
                    o_ref[...] += _tn(a_refs[ta][...], b_refs[tb][...])

    def spec(offs, counts, t, axis):
        def index(i, j, k):
            pos = (i, j)[axis]
            mine = jnp.logical_and(pos >= offs[t], pos < offs[t] + counts[t])
            return jnp.where(mine, k, 0), jnp.clip(pos - offs[t], 0, counts[t] - 1)
        return pl.BlockSpec((tk, tile), index)

    return pl.pallas_call(
        body, name=name, grid=(ni, nj, nk),
        in_specs=[spec(offs_a, counts_a, t, 0) for t in range(na)] + [spec(offs_b, counts_b, t, 1) for t in range(nb)],
        out_specs=pl.BlockSpec((tile, tile), lambda i, j, k: (i, j)),
        out_shape=SDS((ni * tile, nj * tile), f32),
        compiler_params=pltpu.CompilerParams(dimension_semantics=("parallel", "parallel", "arbitrary")),
    )(*a_parts, *b_parts)


def _adamw(w, g, m, v):
    m = ADAM_B1 * m + (1.0 - ADAM_B1) * g
    v = ADAM_B2 * v + (1.0 - ADAM_B2) * (g * g)
    m_hat = m / (1.0 - ADAM_B1 ** ADAM_STEP)
    v_hat = v / (1.0 - ADAM_B2 ** ADAM_STEP)
    delta = -ADAM_LR * (m_hat / (jnp.sqrt(v_hat) + ADAM_EPS) + ADAM_WD * w)
    return delta, m, v


def _sum_adamw(parts, w, m, v, name):
    r, c = w.shape
    tc = 256

    def body(p_ref, w_ref, m_ref, v_ref, g_ref, d_ref, nm_ref, nv_ref):
        g = p_ref[0].astype(f32)
        for q in range(1, 4):
            g = g + p_ref[q].astype(f32)
        g_ref[...] = g
        d_ref[...], nm_ref[...], nv_ref[...] = _adamw(w_ref[...], g, m_ref[...], v_ref[...])

    blk = pl.BlockSpec((r, tc), lambda i: (0, i))
    return pl.pallas_call(
        body, name=name, grid=(c // tc,),
        in_specs=[pl.BlockSpec((4, r, tc), lambda i: (0, 0, i)), blk, blk, blk],
        out_specs=[blk] * 4, out_shape=[SDS((r, c), f32)] * 4,
        compiler_params=pltpu.CompilerParams(dimension_semantics=("parallel",)),
    )(parts, w, m, v)


def _sum_small(parts, pre_blocks):
    def body(p_ref, b_ref, o_ref):
        t = p_ref[0]
        pre = b_ref[0]
        for j in range(1, N_DEV):
            t = t + p_ref[j]
            pre = pre + b_ref[j]
        o_ref[...] = t
        o_ref[5:6, 0:D_MODEL] = pre[0:1, :]
        row_h = _iota((D_SSM, LANES), 0) // HEAD_DIM
        fold = (row_h == _iota((D_SSM, LANES), 1)).astype(f32)
        lower = t[8:16, 0:LANES]
        folded = _nn_hi(t[8:16, 0:D_SSM], fold)
        loss = jnp.sum(t[11:12, 0:D_MODEL], axis=1, keepdims=True) * (0.5 / D_MODEL)
        row = _iota((8, LANES), 0)
        o_ref[8:16, 0:LANES] = jnp.where(row < 2, folded, jnp.where(row == 4, loss, lower))

    return pl.pallas_call(body, name="sum_small", out_shape=SDS((PACK_ROWS, PACK_W), f32),
                          in_specs=[pl.BlockSpec(memory_space=pltpu.VMEM)] * 2,
                          out_specs=pl.BlockSpec(memory_space=pltpu.VMEM))(parts, pre_blocks)


def _adamw_small(w, g, m, v):
    def body(w_ref, g_ref, m_ref, v_ref, d_ref, nm_ref, nv_ref):
        d_ref[...], nm_ref[...], nv_ref[...] = _adamw(w_ref[...], g_ref[...], m_ref[...], v_ref[...])

    vm = pl.BlockSpec(memory_space=pltpu.VMEM)
    return pl.pallas_call(body, name="adamw_small", out_shape=[SDS(w.shape, f32)] * 3,
                          in_specs=[vm] * 4, out_specs=[vm] * 3)(w, g, m, v)


def _pad_lanes(v, width):
    return jnp.pad(v, ((0, 0), (0, width - v.shape[1])))


def _local_step(x, tgt, norm_pre_w, wt, conv_w, conv_b, dt_bias, a_log, d_skip, ssm_norm_w, wo, norm_post_w, sharded):
    dtb16 = _pad_lanes(dt_bias, LANES)
    alog16 = _pad_lanes(a_log, LANES)
    alog_f = jnp.repeat(a_log, HEAD_DIM, axis=1)
    d_f = jnp.repeat(d_skip, HEAD_DIM, axis=1)

    shard_out = wo.shape[0]
    if sharded:
        proj, u, (g_out, g_cw) = _prenorm_inproj(x, norm_pre_w, wt, gather=(wo, conv_w))
        wo = g_out.reshape(N_DEV * shard_out, D_MODEL)
        conv_w = g_cw.transpose(1, 0, 2).reshape(4, D_CONV)
    else:
        proj, u, _ = _prenorm_inproj(x, norm_pre_w, wt)
    o, lb, mix_a = _attn_fwd(proj)
    mix_s, y, states, cv = _ssd_fwd(proj, conv_w, conv_b, dtb16, alog16, alog_f, d_f, ssm_norm_w)
    dmix, dres, acc_post, dw_out = _outproj_loss(mix_a, mix_s, wo, x, tgt, norm_post_w)
    ssd_args = (proj, y, states, cv, dmix, conv_w, conv_b, dtb16, alog16, alog_f, d_f, ssm_norm_w)
    if sharded:
        dq, dk, dv, dg, got_out = _attn_bwd(proj, o, lb, dmix, swap=dw_out)
        chip_out = _chip_sum(dw_out, got_out, shard_out, "chip_sum_w_out")
        dzxd, g_conv, g_vec, g_dt, (parts_out,) = _ssd_bwd(*ssd_args, chip_sums=[chip_out])
    else:
        dq, dk, dv, dg = _attn_bwd(proj, o, lb, dmix)
        dzxd, g_conv, g_vec, g_dt, _ = _ssd_bwd(*ssd_args)
    dparts = [dq, dk, dv, dg, dzxd]

    def pack(g_pre_row):
        return jnp.concatenate(
            [g_conv[0:5], g_pre_row, _pad_lanes(g_vec[0:1], PACK_W), _pad_lanes(acc_post[1:2], PACK_W),
             _pad_lanes(g_vec[1:3], PACK_W), _pad_lanes(g_dt[0:1], PACK_W), _pad_lanes(acc_post[0:1], PACK_W),
             jnp.zeros((4, PACK_W), f32)], axis=0)

    if sharded:
        dw_in, got_in = _dw_in_swap(dparts, u)
        chip_in = _chip_sum(dw_in, got_in, D_IN_PROJ // N_DEV, "chip_sum_w_in")
        grad_x, _, (parts_in,), small = _inproj_bwd(dparts, wt, x, norm_pre_w, dres, [chip_in],
                                                    pack(jnp.zeros((1, PACK_W), f32)))
        return grad_x, (parts_in, parts_out), small
    dw_in = _matmul_tn(dparts, [u], "dw_in")
    grad_x, g_pre, _, _ = _inproj_bwd(dparts, wt, x, norm_pre_w, dres)
    return grad_x, (dw_in, dw_out), pack(_pad_lanes(g_pre[0:1], PACK_W))


def kernel(x, norm_pre_w, w_in, conv_w, conv_b, dt_bias, a_log, d_skip, ssm_norm_w, w_out, norm_post_w, loss_target, m_norm_pre_w, m_w_in, m_conv_w, m_conv_b, m_dt_bias, m_a_log, m_d_skip, m_ssm_norm_w, m_w_out, m_norm_post_w, v_norm_pre_w, v_w_in, v_conv_w, v_conv_b, v_dt_bias, v_a_log, v_d_skip, v_ssm_norm_w, v_w_out, v_norm_post_w):
    shard_cv = conv_w.shape[2]
    me = 4 * lax.axis_index("x") + 2 * lax.axis_index("y") + lax.axis_index("c")

    g_in, = _all_gather([w_in[0].T.astype(bf16)])
    wt = _assemble_wt(g_in)

    grad_x, (parts_in, parts_out), (parts_small, pre_blocks) = _local_step(
        x[0], loss_target[0], norm_pre_w, wt, conv_w[0], conv_b, dt_bias, a_log, d_skip, ssm_norm_w,
        w_out[0].astype(bf16), norm_post_w, sharded=True)

    g_w_in, d_w_in, nm_w_in, nv_w_in = (a.T for a in _sum_adamw(
        parts_in, w_in[0].T, m_w_in[0].T, v_w_in[0].T, "sum_adamw_w_in"))
    g_w_out, d_w_out, nm_w_out, nv_w_out = _sum_adamw(parts_out, w_out[0], m_w_out[0], v_w_out[0], "sum_adamw_w_out")
    tot = _sum_small(parts_small, pre_blocks)

    g_cw_all = tot[0:4]
    small_g = {
        "conv_w": lax.dynamic_slice(g_cw_all, (0, me * shard_cv), (4, shard_cv)),
        "conv_b": tot[4:5], "norm_pre_w": tot[5:6, :D_MODEL], "ssm_norm_w": tot[6:7, :D_SSM],
        "norm_post_w": tot[7:8, :D_MODEL], "a_log": tot[8:9, :16], "d_skip": tot[9:10, :16], "dt_bias": tot[10:11, :16],
    }
    loss = tot[12, 0]
    small_w = {"conv_w": (conv_w[0], m_conv_w[0], v_conv_w[0]), "conv_b": (conv_b, m_conv_b, v_conv_b),
               "norm_pre_w": (norm_pre_w, m_norm_pre_w, v_norm_pre_w), "ssm_norm_w": (ssm_norm_w, m_ssm_norm_w, v_ssm_norm_w),
               "norm_post_w": (norm_post_w, m_norm_post_w, v_norm_post_w), "a_log": (a_log, m_a_log, v_a_log),
               "d_skip": (d_skip, m_d_skip, v_d_skip), "dt_bias": (dt_bias, m_dt_bias, v_dt_bias)}
    names = list(small_w)
    sizes = [small_g[k].size for k in names]
    tot_size = sum(sizes)
    pad_to = -(-tot_size // 1024) * 1024

    def flat(arrs):
        v = jnp.concatenate([a.reshape(-1) for a in arrs])
        return jnp.pad(v, (0, pad_to - tot_size)).reshape(pad_to // LANES, LANES)

    fw = flat([small_w[k][0] for k in names])
    fg = flat([small_g[k] for k in names])
    fm = flat([small_w[k][1] for k in names])
    fv = jnp.pad(jnp.concatenate([small_w[k][2].reshape(-1) for k in names]), (0, pad_to - tot_size),
                 constant_values=1.0).reshape(pad_to // LANES, LANES)
    fd, fnm, fnv = _adamw_small(fw, fg, fm, fv)

    def unflat(f):
        out, off = {}, 0
        v = f.reshape(-1)
        for k, n in zip(names, sizes):
            out[k] = v[off:off + n].reshape(small_g[k].shape)
            off += n
        return out

    sd, snm, snv = unflat(fd), unflat(fnm), unflat(fnv)
    lead = lambda a: a[None]
    order = ["norm_pre_w", "w_in", "conv_w", "conv_b", "dt_bias", "a_log", "d_skip", "ssm_norm_w", "w_out", "norm_post_w"]
    grads = dict(small_g, w_in=g_w_in, w_out=g_w_out)
    deltas = dict(sd, w_in=d_w_in, w_out=d_w_out)
    new_m = dict(snm, w_in=nm_w_in, w_out=nm_w_out)
    new_v = dict(snv, w_in=nv_w_in, w_out=nv_w_out)

    def shaped(dct, k):
        a = dct[k]
        return lead(a) if k in ("w_in", "w_out", "conv_w") else a

    return (loss, grad_x[None], *[shaped(grads, k) for k in order], *[shaped(deltas, k) for k in order],
            *[shaped(new_m, k) for k in order], *[shaped(new_v, k) for k in order])
```

```python
import jax
import jax.numpy as jnp
from jax import lax
from jax.experimental import pallas as pl
from jax.experimental.pallas import tpu as pltpu

f32, bf16 = jnp.float32, jnp.bfloat16
SDS = jax.ShapeDtypeStruct
HIGHEST = lax.Precision.HIGHEST
MESH = pl.DeviceIdType.MESH

N_DEV = 8
D_MODEL = 1024
D_ATTN = 1024
D_SSM = 1024
HEAD_DIM = 64
N_PAIRS = 8
D_STATE = 128
N_GROUPS = 2
D_CONV = D_SSM + 2 * N_GROUPS * D_STATE
D_IN_PROJ = 4 * D_ATTN + D_SSM + D_CONV + 16
NP = 7168
CHUNK = 128
BLK = 128
DILATIONS = (1, 4, 16)
EPS = 1e-6
LANES = 128
COL_Z, COL_XS, COL_BC, COL_DT = 4096, 5120, 6144, 6656

ADAM_LR, ADAM_B1, ADAM_B2, ADAM_EPS, ADAM_WD, ADAM_STEP = 0.001, 0.9, 0.999, 1e-08, 0.01, 10

PACK_ROWS, PACK_W = 16, 1536


def _nt(a, b):
    return lax.dot_general(a, b, (((1,), (1,)), ((), ())), preferred_element_type=f32)


def _tn(a, b):
    return lax.dot_general(a, b, (((0,), (0,)), ((), ())), preferred_element_type=f32)


def _nn(a, b):
    return jnp.dot(a, b, preferred_element_type=f32)


def _nn_hi(a, b):
    return jnp.dot(a, b, precision=HIGHEST, preferred_element_type=f32)


def _sigmoid(x):
    return 1.0 / (1.0 + jnp.exp(-x))


def _softplus(x):
    return jnp.maximum(x, 0.0) + jnp.log1p(jnp.exp(-jnp.abs(x)))


def _iota(shape, dim):
    return lax.broadcasted_iota(jnp.int32, shape, dim)


def _my_pos():
    return lax.axis_index("x"), lax.axis_index("y"), lax.axis_index("c")


GATHER_SEMS = 9


def _gather_phases(ins, outs, send_sems, recv_sems, local_sems):
    n, ns = len(ins), GATHER_SEMS
    x, y, c = _my_pos()
    me, sibling = (x, y, c), (x, y, 1 - c)
    xn, yn, diag = (1 - x, y), (x, 1 - y), (1 - x, 1 - y)

    def slot(a, px, py, pc):
        return outs[a].at[4 * px + 2 * py + pc]

    def part(a, ref, h):
        width = ins[a].shape[-1]
        if width % (2 * LANES):
            return ref if h == 1 else None
        return ref.at[:, pl.ds(h * (width // 2), width // 2)]

    def copy(a, k, block, to, src=None, h=None):
        src_ref = slot(a, *block) if src is None else src
        dst_ref = slot(a, *block)
        if h is not None:
            src_ref, dst_ref = part(a, src_ref, h), part(a, dst_ref, h)
            if src_ref is None:
                return None
        return pltpu.make_async_remote_copy(
            src_ref=src_ref, dst_ref=dst_ref, send_sem=send_sems.at[ns * a + k], recv_sem=recv_sems.at[ns * a + k],
            device_id=to, device_id_type=MESH)

    def mine():
        return [pltpu.make_async_copy(ins[a], slot(a, *me), local_sems.at[a]) for a in range(n)]

    def own_sends(a):
        return [copy(a, 0, me, sibling, src=ins[a]), copy(a, 1, me, (*xn, c), src=ins[a]),
                copy(a, 2, me, (*yn, c), src=ins[a])]

    def neighbour_relays(a):
        return [copy(a, 4, (*xn, c), sibling), copy(a, 7, (*xn, c), (*yn, c), h=1),
                copy(a, 5, (*yn, c), sibling), copy(a, 8, (*yn, c), (*xn, c), h=0)]

    def diagonal_halves(a):
        return [copy(a, k, (*diag, c), me, h=h) for k, h in ((8, 0), (7, 1))]

    def start_all(cps):
        for cp in cps:
            if cp is not None:
                cp.start()

    def phase0():
        start_all(mine())
        for a in range(n):
            start_all(own_sends(a))

    def phase1():
        for a in range(n):
            copy(a, 1, (*xn, c), me).wait_recv()
            copy(a, 2, (*yn, c), me).wait_recv()
            start_all(neighbour_relays(a))

    def phase2():
        for a in range(n):
            for cp in diagonal_halves(a):
                if cp is not None:
                    cp.wait_recv()
            copy(a, 6, (*diag, c), sibling).start()

    def finish():
        for a in range(n):
            copy(a, 0, sibling, me).wait_recv()
            for j, chip in enumerate((xn, yn, diag)):
                copy(a, 4 + j, (*chip, 1 - c), me).wait_recv()
        for a in range(n):
            for cp in own_sends(a) + neighbour_relays(a) + [copy(a, 6, (*diag, c), sibling)]:
                if cp is not None:
                    cp.wait_send()
        for cp in mine():
            cp.wait()

    return phase0, phase1, phase2, finish


def _gather_scratch(n):
    return [pltpu.SemaphoreType.DMA((GATHER_SEMS * n,)), pltpu.SemaphoreType.DMA((GATHER_SEMS * n,)),
            pltpu.SemaphoreType.DMA((n,))]


def _all_gather(arrs):
    n = len(arrs)

    def body(*refs):
        for phase in _gather_phases(refs[:n], refs[n:2 * n], *refs[2 * n:]):
            phase()

    anyspec = pl.BlockSpec(memory_space=pl.ANY)
    return pl.pallas_call(
        body, name="weights_all_gather",
        out_shape=[SDS((N_DEV,) + a.shape, a.dtype) for a in arrs],
        in_specs=[anyspec] * n, out_specs=[anyspec] * n, scratch_shapes=_gather_scratch(n),
    )(*arrs)


def _dw_in_swap(a_parts, u):
    tile, tk = 1024, 1024
    s = u.shape[0]
    nk = s // tk
    na = len(a_parts)
    offs, counts, ni = _col_blocks(a_parts, tile)

    def body(*refs):
        a_refs, u_ref = refs[:na], refs[na]
        dw_ref, got_ref = refs[na + 1:na + 3]
        acc, stage, local_sems, send_sems, recv_sem = refs[na + 3:]
        i, k = pl.program_id(0), pl.program_id(1)
        x, y, c = _my_pos()
        par = i % 2

        def tile_copies(t, p):
            rows = pl.ds(pl.multiple_of(t * tile, tile), tile)
            loc = pltpu.make_async_copy(stage.at[p], dw_ref.at[rows], local_sems.at[p])
            rem = pltpu.make_async_remote_copy(
                src_ref=stage.at[p], dst_ref=got_ref.at[rows], send_sem=send_sems.at[p], recv_sem=recv_sem,
                device_id=(x, y, 1 - c), device_id_type=MESH)
            return loc, rem

        @pl.when(k == 0)
        def _():
            acc[...] = jnp.zeros((tile, tile), f32)

        for t in range(na):
            @pl.when(jnp.logical_and(i >= offs[t], i < offs[t] + counts[t]))
            def _(t=t):
                acc[...] += _tn(a_refs[t][...], u_ref[pl.ds(pl.multiple_of(k * tk, tk), tk), :])

        @pl.when(k == nk - 1)
        def _():
            @pl.when(i >= 2)
            def _():
                loc, rem = tile_copies(i - 2, par)
                loc.wait()
                rem.wait_send()
            stage[par] = acc[...]
            loc, rem = tile_copies(i, par)
            loc.start()
            rem.start()

        @pl.when(jnp.logical_and(i == ni - 1, k == nk - 1))
        def _():
            for t in (ni - 2, ni - 1):
                loc, rem = tile_copies(t, t % 2)
                loc.wait()
                rem.wait_send()
            pltpu.make_async_remote_copy(src_ref=dw_ref, dst_ref=got_ref, send_sem=send_sems.at[0], recv_sem=recv_sem,
                                         device_id=(x, y, c), device_id_type=MESH).wait_recv()

    def a_spec(t):
        def index(i, k):
            mine = jnp.logical_and(i >= offs[t], i < offs[t] + counts[t])
            return jnp.where(mine, k, 0), jnp.clip(i - offs[t], 0, counts[t] - 1)
        return pl.BlockSpec((tk, tile), index)

    anyspec = pl.BlockSpec(memory_space=pl.ANY)
    return pl.pallas_call(
        body, name="dw_in_swap", grid=(ni, nk),
        in_specs=[a_spec(t) for t in range(na)] + [pl.BlockSpec((s, tile), lambda i, k: (0, 0))],
        out_specs=[anyspec] * 2,
        out_shape=[SDS((ni * tile, tile), f32), SDS((ni * tile, tile), f32)],
        scratch_shapes=[pltpu.VMEM((tile, tile), f32), pltpu.VMEM((2, tile, tile), f32), pltpu.SemaphoreType.DMA((2,)),
                        pltpu.SemaphoreType.DMA((2,)), pltpu.SemaphoreType.DMA(())],
        compiler_params=pltpu.CompilerParams(dimension_semantics=("arbitrary", "arbitrary")),
    )(*a_parts, u)


def _chip_sum(mine, got, rows, name):
    r, cdim = mine.shape
    tc = LANES

    def body(m_ref, g_ref, s16_ref):
        c = lax.axis_index("c")
        for q in range(4):
            blk = pl.ds(rows * (2 * q + c), rows)
            s16_ref[q] = (m_ref[blk, :] + g_ref[blk, :]).astype(bf16)

    col = pl.BlockSpec((r, tc), lambda i: (0, i))
    return pl.pallas_call(
        body, name=name, grid=(cdim // tc,), in_specs=[col, col],
        out_specs=pl.BlockSpec((4, rows, tc), lambda i: (0, 0, i)), out_shape=SDS((4, rows, cdim), bf16),
        compiler_params=pltpu.CompilerParams(dimension_semantics=("parallel",)),
    )(mine, got)


def _assemble_wt(shards):
    nd, rows, cdim = shards.shape
    tc = 256

    def body(g_ref, o_ref):
        for j in range(nd):
            o_ref[pl.ds(rows * j, rows), :] = g_ref[j]
        o_ref[pl.ds(nd * rows, NP - nd * rows), :] = jnp.zeros((NP - nd * rows, tc), shards.dtype)

    return pl.pallas_call(
        body, name="assemble_w_in", grid=(cdim // tc,),
        in_specs=[pl.BlockSpec((nd, rows, tc), lambda i: (0, 0, i))],
        out_specs=pl.BlockSpec((NP, tc), lambda i: (0, i)), out_shape=SDS((NP, cdim), shards.dtype),
        compiler_params=pltpu.CompilerParams(dimension_semantics=("parallel",)),
    )(shards)


def _chip_exchange_copies(ins, outs, send_sems, recv_sems, local_sems):
    nb = len(ins)
    x, y, c = _my_pos()
    my_q = 2 * x + y
    mine = [pltpu.make_async_copy(ins[a].at[my_q], outs[a].at[my_q], local_sems.at[a]) for a in range(nb)]
    sends, recvs = [], []
    for k in range(1, 4):
        to, frm = (my_q + k) % 4, (my_q + 4 - k) % 4
        for a in range(nb):
            sems = dict(send_sem=send_sems.at[3 * a + k - 1], recv_sem=recv_sems.at[3 * a + k - 1], device_id_type=MESH)
            sends.append(pltpu.make_async_remote_copy(
                src_ref=ins[a].at[to], dst_ref=outs[a].at[my_q], device_id=(to // 2, to % 2, c), **sems))
            recvs.append(pltpu.make_async_remote_copy(
                src_ref=ins[a].at[frm], dst_ref=outs[a].at[frm], device_id=(x, y, c), **sems))
    return mine, sends, recvs


def _chip_exchange_scratch(nb):
    return [pltpu.SemaphoreType.DMA((3 * nb,)), pltpu.SemaphoreType.DMA((3 * nb,)), pltpu.SemaphoreType.DMA((nb,))]


def _prenorm_inproj(x, nw, wt, gather=()):
    s, d = x.shape
    npad = wt.shape[0]
    tm, tn = 1024, 1024
    ng = len(gather)
    ni, nj = s // tm, npad // tn

    def body(x_ref, nw_ref, w_ref, *refs):
        g_in, (proj_ref, u_ref), g_out, sems = refs[:ng], refs[ng:ng + 2], refs[ng + 2:2 * ng + 2], refs[2 * ng + 2:]
        i, j = pl.program_id(0), pl.program_id(1)
        if ng:
            phases = _gather_phases(g_in, g_out, *sems)
            for step, phase in enumerate(phases[:3]):
                @pl.when(jnp.logical_and(i == step, j == 0))
                def _(phase=phase):
                    phase()

        @pl.when(j == 0)
        def _():
            xv = x_ref[...]
            r = lax.rsqrt(jnp.mean(xv * xv, axis=-1, keepdims=True) + EPS)
            u_ref[...] = (xv * r * nw_ref[...]).astype(bf16)
        proj_ref[...] = _nt(u_ref[...], w_ref[pl.ds(pl.multiple_of(j * tn, tn), tn), :])

        if ng:
            @pl.when(jnp.logical_and(i == ni - 1, j == nj - 1))
            def _():
                phases[3]()

    anyspec = pl.BlockSpec(memory_space=pl.ANY)
    outs = pl.pallas_call(
        body, name="prenorm_inproj", grid=(ni, nj),
        in_specs=[pl.BlockSpec((tm, d), lambda i, j: (i, 0)), pl.BlockSpec((1, d), lambda i, j: (0, 0)),
                  pl.BlockSpec((npad, d), lambda i, j: (0, 0))] + [anyspec] * ng,
        out_specs=[pl.BlockSpec((tm, tn), lambda i, j: (i, j)), pl.BlockSpec((tm, d), lambda i, j: (i, 0))]
        + [anyspec] * ng,
        out_shape=[SDS((s, npad), f32), SDS((s, d), bf16)] + [SDS((N_DEV,) + a.shape, a.dtype) for a in gather],
        scratch_shapes=_gather_scratch(ng) if ng else [],
        compiler_params=pltpu.CompilerParams(dimension_semantics=("arbitrary", "arbitrary")),
    )(x, nw, wt, *gather)
    return outs[0], outs[1], outs[2:]


def _attn_consts():
    head0 = _iota((BLK, LANES), 1) < HEAD_DIM
    tri2 = (_iota((BLK, 2 * LANES), 1) % LANES) <= _iota((BLK, 2 * LANES), 0)
    ones2 = ((_iota((LANES, 2 * LANES), 0) < HEAD_DIM) == (_iota((LANES, 2 * LANES), 1) < LANES)).astype(bf16)
    rmat = ((_iota((2 * LANES, LANES), 0) < LANES) == (_iota((2 * LANES, LANES), 1) < HEAD_DIM)).astype(bf16)
    bones = ((_iota((LANES, LANES), 0) < HEAD_DIM) == (_iota((LANES, LANES), 1) < HEAD_DIM)).astype(bf16)
    return head0, tri2, ones2, rmat, bones


def _stack_heads(x16, head0):
    zero = jnp.zeros_like(x16)
    return jnp.concatenate([jnp.where(head0, x16, zero), jnp.where(head0, zero, x16)], axis=0)


def _bf16_terms(x, terms):
    out = []
    for _ in range(terms):
        t = x.astype(bf16)
        out.append(t)
        x = x - t.astype(f32)
    return out


def _dot_01(x, w16, terms):
    return _nn(jnp.concatenate(_bf16_terms(x, terms), axis=1), jnp.concatenate([w16] * terms, axis=0))


def _split_dot_sum(x, w16):
    hi, lo = _bf16_terms(x, 2)
    return _nn(hi, w16) + _nn(lo, w16)


def _dot_01_left(w16, x, terms):
    return _nn(jnp.concatenate([w16] * terms, axis=1), jnp.concatenate(_bf16_terms(x, terms), axis=0))


def _attn_fwd(proj):
    s = proj.shape[0]
    n_it = s // BLK

    def body(q_ref, k_ref, v_ref, g_ref, o_ref, l_ref, mix_ref, op0, op1, op2, lp0, lp1, lp2,
             s_a, s_b, sd_a, sd_b, p_a, p_b, m_a, m_b, pd_a, pd_b, k_a, k_b, v_a, v_b):
        op_refs, lp_refs = (op0, op1, op2), (lp0, lp1, lp2)
        head0, tri2, ones2, rmat, _ = _attn_consts()
        score_bufs, prob_bufs = ((s_a, sd_a), (s_b, sd_b)), ((p_a, m_a, pd_a), (p_b, m_b, pd_b))
        k_bufs, v_bufs = (k_a, k_b), (v_a, v_b)
        for buf in k_bufs + v_bufs:
            buf[...] = jnp.zeros_like(buf)

        def block_rows(i, d, nb):
            r, blk = i // nb, i % nb
            return pl.ds(blk * (BLK * d) + r, BLK, stride=d), blk > 0

        def unstack(st16):
            return st16[:BLK] + st16[BLK:]

        def scores(i, par, d, nb):
            rows, has_prev = block_rows(i, d, nb)
            s_buf, sd_buf = score_bufs[par]
            qs = q_ref[rows, :] * 0.125
            qs16 = qs.astype(bf16)
            kst_c = _stack_heads(k_ref[rows, :].astype(bf16), head0)
            kst_p = k_bufs[1 - par][...]
            k_bufs[par][...] = kst_c
            sc = _nt(qs16, kst_c)
            sp = _nt(qs16, kst_p)
            s_buf[...] = jnp.where(tri2, sc, jnp.where(has_prev, sp, -jnp.inf))
            sd = _nn((qs * unstack(kst_p).astype(f32)).astype(bf16), ones2)
            sd_buf[...] = jnp.where(has_prev, sd, -jnp.inf)

        def softmax(bufs_in, bufs_out):
            s_buf, sd_buf = bufs_in
            p_buf, m_buf, pd_buf = bufs_out
            sc, sd2 = s_buf[...], sd_buf[...]
            m0 = jnp.max(sc[:, :LANES], axis=1, keepdims=True)
            m1 = jnp.max(sc[:, LANES:], axis=1, keepdims=True)
            m2 = jnp.concatenate([jnp.broadcast_to(m0, (BLK, LANES)), jnp.broadcast_to(m1, (BLK, LANES))], axis=1)
            m2 = jnp.maximum(m2, sd2)
            p_buf[...] = jnp.exp(sc - m2).astype(bf16)
            m_pair = jnp.where(head0, m2[:, :LANES], m2[:, LANES:])
            m_buf[...] = m_pair
            pd_buf[...] = jnp.exp(jnp.where(head0, sd2[:, :LANES], sd2[:, LANES:]) - m_pair)

        def output(i, par, d, nb, p):
            rows, _ = block_rows(i, d, nb)
            p_buf, m_buf, pd_buf = prob_bufs[par]
            vst_c = _stack_heads(v_ref[rows, :].astype(bf16), head0)
            vst_p = v_bufs[1 - par][...]
            v_bufs[par][...] = vst_c
            pt16, pd = p_buf[...], pd_buf[...]
            zero = jnp.zeros_like(pt16)
            o = (_nn(jnp.where(tri2, pt16, zero), vst_c) + _nn(jnp.where(tri2, zero, pt16), vst_p)
                 + pd * unstack(vst_p).astype(f32))
            l = _nn(pt16, rmat) + pd
            op_refs[p][rows, :] = o / l
            lp_refs[p][rows, :] = m_buf[...] + jnp.log(l)

        for p, d in enumerate(DILATIONS):
            nb = s // (BLK * d)
            scores(0, 0, d, nb)
            scores(1, 1, d, nb)
            softmax(score_bufs[0], prob_bufs[0])

            def steps(j, carry, d=d, nb=nb, p=p):
                for par in range(2):
                    t = 2 * j + 2 + par
                    scores(t, par, d, nb)
                    output(t - 2, par, d, nb, p)
                    softmax(score_bufs[1 - par], prob_bufs[1 - par])
                return carry

            lax.fori_loop(0, (n_it - 2) // 2, steps, 0, unroll=True)
            output(n_it - 2, 0, d, nb, p)
            softmax(score_bufs[1], prob_bufs[1])
            output(n_it - 1, 1, d, nb, p)

        def merge(i, carry):
            rows = pl.ds(pl.multiple_of(i * 256, 256), 256)
            l0, l1, l2 = lp0[rows, :], lp1[rows, :], lp2[rows, :]
            m = jnp.maximum(jnp.maximum(l0, l1), l2)
            e0, e1, e2 = jnp.exp(l0 - m), jnp.exp(l1 - m), jnp.exp(l2 - m)
            z = e0 + e1 + e2
            o = (e0 * op0[rows, :] + e1 * op1[rows, :] + e2 * op2[rows, :]) / z
            o_ref[rows, :] = o
            l_ref[rows, :] = m + jnp.log(z)
            g = g_ref[rows, :]
            mix_ref[rows, :] = (o * (g * _sigmoid(g))).astype(bf16)
            return carry

        lax.fori_loop(0, s // 256, merge, 0)

    col = lambda base: pl.BlockSpec((s, LANES), lambda h: (0, base + h))
    return pl.pallas_call(
        body, name="attn_fwd", grid=(N_PAIRS,),
        in_specs=[col(0), col(8), col(16), col(24)],
        out_specs=[col(0), col(0), col(0)],
        out_shape=[SDS((s, D_ATTN), f32), SDS((s, D_ATTN), f32), SDS((s, D_ATTN), bf16)],
        scratch_shapes=[pltpu.VMEM((s, LANES), f32)] * 6 + [pltpu.VMEM((BLK, 2 * LANES), f32)] * 4
        + [pltpu.VMEM((BLK, 2 * LANES), bf16)] * 2 + [pltpu.VMEM((BLK, LANES), f32)] * 4
        + [pltpu.VMEM((2 * BLK, LANES), bf16)] * 4,
        compiler_params=pltpu.CompilerParams(dimension_semantics=("parallel",)),
    )(proj, proj, proj, proj)


def _expand_mat():
    colv = _iota((LANES, 2 * D_SSM), 1)
    head = 2 * ((colv % D_SSM) // LANES) + colv // D_SSM
    return (_iota((LANES, 2 * D_SSM), 0) == head).astype(bf16)


def _fold_mat():
    return (_iota((D_SSM, LANES), 0) // HEAD_DIM == _iota((D_SSM, LANES), 1)).astype(bf16)


def _conv(xs_ref, bc_ref, xs_tail, bc_tail, cw_ref, cb_ref, xpad, first):
    keep = jnp.where(first, 0.0, 1.0)
    xpad[0:8, 0:D_SSM] = xs_tail[...] * keep
    xpad[0:8, D_SSM:D_CONV] = bc_tail[...] * keep
    xpad[8:8 + CHUNK, 0:D_SSM] = xs_ref[...]
    xpad[8:8 + CHUNK, D_SSM:D_CONV] = bc_ref[...]
    xp = xpad[...]
    cv = cb_ref[...] + cw_ref[3:4, :] * xp[8:8 + CHUNK]
    for j in range(3):
        cv = cv + cw_ref[j:j + 1, :] * pltpu.roll(xp, 3 - j, 0)[8:8 + CHUNK]
    return cv


def _decay_terms(dt_ref, dtb_ref, alog16_ref, emat_ref):
    pre = dt_ref[...] + dtb_ref[...]
    dt16 = _softplus(pre)
    a16 = -jnp.exp(alog16_ref[...])
    sub, lane = _iota((CHUNK, CHUNK), 0), _iota((CHUNK, CHUNK), 1)
    tri = (sub >= lane).astype(f32)
    al16 = _nn_hi(tri, dt16 * a16)
    al_t = al16.T
    emat = emat_ref[...]
    dt_x = _dot_01(dt16, emat, 3)
    al_x = _dot_01(al16, emat, 3)
    lane_w = _iota((CHUNK, D_SSM), 1)
    even = (lane_w % LANES) < HEAD_DIM
    dt_f = jnp.where(even, dt_x[:, :D_SSM], dt_x[:, D_SSM:])
    al_f = jnp.where(even, al_x[:, :D_SSM], al_x[:, D_SSM:])
    return pre, dt_f, al_f, al_x, al_t


def _decay_mat(al_x, al_t, pair, h):
    sub, lane = _iota((CHUNK, CHUNK), 0), _iota((CHUNK, CHUNK), 1)
    col = al_x[:, h * D_SSM + pair * LANES: h * D_SSM + (pair + 1) * LANES]
    row = al_t[2 * pair + h: 2 * pair + h + 1, :]
    return jnp.exp(jnp.where(sub >= lane, col - row, -jnp.inf))


def _ssd_in_specs(order):
    blk = lambda w, cb: pl.BlockSpec((CHUNK, w), lambda i: (order(i), cb))
    tail = lambda w, cb: pl.BlockSpec((8, w), lambda i: (jnp.maximum(16 * order(i) - 1, 0), cb))
    return [blk(D_SSM, COL_XS // D_SSM), blk(512, COL_BC // 512), tail(D_SSM, COL_XS // D_SSM),
            tail(512, COL_BC // 512), blk(LANES, COL_DT // LANES), blk(D_SSM, COL_Z // D_SSM)]


def _full(shape):
    return pl.BlockSpec(shape, lambda i: (0,) * len(shape))


def _ssd_fwd(proj, conv_w, conv_b, dtb16, alog16, alog_f, d_f, nw):
    s = proj.shape[0]
    nc = s // CHUNK

    def body(xs_ref, bc_ref, xs_tail, bc_tail, dt_ref, z_ref, cw_ref, cb_ref, dtb_ref, alog16_ref, alogf_ref,
             df_ref, nw_ref, mix_ref, y_ref, st_ref, cv_ref, h_scr, xpad, y_scr, emat_ref):
        c = pl.program_id(0)

        @pl.when(c == 0)
        def _():
            h_scr[...] = jnp.zeros_like(h_scr)
            emat_ref[...] = _expand_mat()

        cv = _conv(xs_ref, bc_ref, xs_tail, bc_tail, cw_ref, cb_ref, xpad, c == 0)
        cv_ref[...] = cv
        xbc = cv * _sigmoid(cv)
        _, dt_f, al_f, al_x, al_t = _decay_terms(dt_ref, dtb_ref, alog16_ref, emat_ref)
        head0 = _iota((CHUNK, LANES), 1) < HEAD_DIM
        st_ref[...] = h_scr[...]
        for g in range(N_GROUPS):
            bm = xbc[:, D_SSM + g * D_STATE: D_SSM + (g + 1) * D_STATE].astype(bf16)
            cm = xbc[:, D_SSM + (N_GROUPS + g) * D_STATE: D_SSM + (N_GROUPS + g + 1) * D_STATE].astype(bf16)
            gmat = _nt(cm, bm)
            for pair in range(4 * g, 4 * g + 4):
                sl = slice(pair * LANES, (pair + 1) * LANES)
                xp, dtp, alp = xbc[:, sl], dt_f[:, sl], al_f[:, sl]
                xdt = xp * dtp
                xdt16 = xdt.astype(bf16)
                al_last = alp[CHUNK - 1:CHUNK, :]
                hp = h_scr[:, sl]
                y_off = jnp.exp(alp) * _nn(cm, hp.astype(bf16))
                yd = [_nn((gmat * _decay_mat(al_x, al_t, pair, h)).astype(bf16), xdt16) for h in range(2)]
                y_scr[:, sl] = jnp.where(head0, yd[0], yd[1]) + y_off + df_ref[:, sl] * xp
                st = _tn(bm, (jnp.exp(al_last - alp) * xdt).astype(bf16))
                h_scr[:, sl] = jnp.exp(al_last) * hp + st
        y = y_scr[...]
        y_ref[...] = y
        z = z_ref[...]
        yz = y * (z * _sigmoid(z))
        gw = D_SSM // N_GROUPS
        for g in range(N_GROUPS):
            part = yz[:, g * gw:(g + 1) * gw]
            r = lax.rsqrt(jnp.mean(part * part, axis=-1, keepdims=True) + EPS)
            mix_ref[:, g * gw:(g + 1) * gw] = (part * r * nw_ref[:, g * gw:(g + 1) * gw]).astype(bf16)

    order = lambda i: i
    row = lambda w: pl.BlockSpec((CHUNK, w), lambda i: (i, 0))
    return pl.pallas_call(
        body, name="ssd_fwd", grid=(nc,),
        in_specs=_ssd_in_specs(order) + [_full((4, D_CONV)), _full((1, D_CONV)), _full((1, LANES)), _full((1, LANES)),
                                         _full((1, D_SSM)), _full((1, D_SSM)), _full((1, D_SSM))],
        out_specs=[row(D_SSM), row(D_SSM), pl.BlockSpec((None, D_STATE, D_SSM), lambda i: (i, 0, 0)), row(D_CONV)],
        out_shape=[SDS((s, D_SSM), bf16), SDS((s, D_SSM), f32), SDS((nc, D_STATE, D_SSM), f32),
                   SDS((s, D_CONV), f32)],
        scratch_shapes=[pltpu.VMEM((D_STATE, D_SSM), f32), pltpu.VMEM((8 + CHUNK, D_CONV), f32),
                        pltpu.VMEM((CHUNK, D_SSM), f32), pltpu.VMEM((LANES, 2 * D_SSM), bf16)],
        compiler_params=pltpu.CompilerParams(dimension_semantics=("arbitrary",)),
    )(proj, proj, proj, proj, proj, proj, conv_w, conv_b, dtb16, alog16, alog_f, d_f, nw)


def _outproj_loss(mix_a, mix_s, wo, x, tgt, npw):
    s, d = x.shape
    tm = 512

    def body(ma_ref, ms_ref, wo_ref, x_ref, t_ref, npw_ref, dmix_ref, dres_ref, acc_ref, dwo_ref):
        @pl.when(pl.program_id(0) == 0)
        def _():
            acc_ref[...] = jnp.zeros_like(acc_ref)
            dwo_ref[...] = jnp.zeros_like(dwo_ref)

        out = _nn(ma_ref[...], wo_ref[0:D_ATTN, :]) + _nn(ms_ref[...], wo_ref[D_ATTN:, :])
        r = lax.rsqrt(jnp.mean(out * out, axis=-1, keepdims=True) + EPS)
        on = out * r
        diff = x_ref[...] + on * npw_ref[...] - t_ref[...]
        dres = diff * (1.0 / d)
        dres_ref[...] = dres
        acc_ref[0:1, :] += jnp.sum(diff * diff, axis=0, keepdims=True)
        acc_ref[1:2, :] += jnp.sum(dres * on, axis=0, keepdims=True)
        dn = dres * npw_ref[...]
        dout = (r * (dn - on * jnp.mean(dn * on, axis=-1, keepdims=True))).astype(bf16)
        dmix_ref[...] = _nt(dout, wo_ref[...])
        dwo_ref[0:D_ATTN, :] += _tn(ma_ref[...], dout)
        dwo_ref[D_ATTN:, :] += _tn(ms_ref[...], dout)

    row = lambda w: pl.BlockSpec((tm, w), lambda i: (i, 0))
    return pl.pallas_call(
        body, name="outproj_loss", grid=(s // tm,),
        in_specs=[row(D_ATTN), row(D_SSM), _full((D_ATTN + D_SSM, d)), row(d), row(d), _full((1, d))],
        out_specs=[row(D_ATTN + D_SSM), row(d), _full((8, d)), _full((D_ATTN + D_SSM, d))],
        out_shape=[SDS((s, D_ATTN + D_SSM), f32), SDS((s, d), f32), SDS((8, d), f32), SDS((D_ATTN + D_SSM, d), f32)],
        compiler_params=pltpu.CompilerParams(dimension_semantics=("arbitrary",)),
    )(mix_a, mix_s, wo, x, tgt, npw)


def _attn_bwd(proj, o, lb, dmix, swap=None):
    s = proj.shape[0]
    n_it = s // BLK

    nsw = 0 if swap is None else 1

    def body(*refs):
        q_ref, k_ref, v_ref, g_ref, o_ref, l_ref, dm_ref = refs[:7]
        swap_in = refs[7:7 + nsw]
        dq_ref, dk_ref, dv_ref, dg_ref = refs[7 + nsw:11 + nsw]
        swap_out = refs[11 + nsw:11 + 2 * nsw]
        dq_acc, dk_acc, dv_acc, do_scr, dl_scr = refs[11 + 2 * nsw:16 + 2 * nsw]
        bufs = refs[16 + 2 * nsw:44 + 2 * nsw]
        stage_a, stage_b = refs[44 + 2 * nsw:46 + 2 * nsw]
        swap_sems = refs[46 + 2 * nsw:]
        head0, tri2_t, _, _, bones = _attn_consts()
        quarter_order = lambda n: 4 * (n % 32) + n // 32
        tri2_q = (quarter_order(_iota((BLK, 2 * LANES), 1) % LANES)
                  <= quarter_order(_iota((BLK, 2 * LANES), 0)))

        if nsw:
            x, y, c = _my_pos()
            swap_copy = pltpu.make_async_remote_copy(
                src_ref=swap_in[0], dst_ref=swap_out[0], send_sem=swap_sems[0], recv_sem=swap_sems[1],
                device_id=(x, y, 1 - c), device_id_type=MESH)

            @pl.when(pl.program_id(0) == 0)
            def _():
                swap_copy.start()

        def quarter_rows(i, q):
            return pl.ds(pl.multiple_of((i // 2) * 2048 + q * 512 + (i % 2) * 256, 256), 256)

        def pro(i, carry):
            for t in range(4):
                rows = pl.ds(pl.multiple_of(i * 1024 + t * 256, 256), 256)
                g = g_ref[rows, :]
                sg = _sigmoid(g)
                dmx = dm_ref[rows, :]
                ov = o_ref[rows, :]
                dg_ref[rows, :] = (dmx * ov * (sg * (1.0 + g * (1.0 - sg)))).astype(bf16)
                do = dmx * (g * sg)
                stage_a[t * 256:(t + 1) * 256, :] = do
                stage_b[t * 256:(t + 1) * 256, :] = _split_dot_sum(do * ov, bones)
            z = jnp.zeros((256, LANES), f32)
            for q in range(4):
                rows = quarter_rows(i, q)
                do_scr[rows, :] = stage_a[pl.ds(q, 256, stride=4), :]
                dl_scr[rows, :] = stage_b[pl.ds(q, 256, stride=4), :]
                dq_acc[rows, :] = z
                dk_acc[rows, :] = z
                dv_acc[rows, :] = z
            return carry

        lax.fori_loop(0, s // 1024, pro, 0)

        def per_head(t):
            return jnp.concatenate([t[:, :LANES], t[:, LANES:]], axis=0)

        def both_heads(t):
            tr = pltpu.roll(t, HEAD_DIM, 1)
            return jnp.concatenate([jnp.where(head0, t, tr), jnp.where(head0, tr, t)], axis=1)

        mm_bufs = ((bufs[0], bufs[1], bufs[2], bufs[3]), (bufs[4], bufs[5], bufs[6], bufs[7]))
        ds_bufs = ((bufs[8], bufs[9], bufs[10], bufs[11]), (bufs[12], bufs[13], bufs[14], bufs[15]))
        op_bufs = ((bufs[16], bufs[17], bufs[18], bufs[19]), (bufs[20], bufs[21], bufs[22], bufs[23]))
        vc_bufs, carry_k, carry_v = (bufs[24], bufs[25]), bufs[26], bufs[27]
        for buf in (op_bufs[0][0], op_bufs[1][0]) + vc_bufs:
            buf[...] = jnp.zeros_like(buf)

        def block_rows(i, d, nb):
            r, blk = i // nb, i % nb
            if d == 1:
                src = [pl.ds(blk * BLK + q, 32, stride=4) for q in range(4)]
                scr = [pl.ds(pl.multiple_of((blk // 16) * 2048 + q * 512 + (blk % 16) * 32, 32), 32) for q in range(4)]
            elif d == 4:
                src = [pl.ds(blk * (BLK * d) + r, BLK, stride=d)]
                scr = [pl.ds(pl.multiple_of((blk // 4) * 2048 + r * 512 + (blk % 4) * BLK, BLK), BLK)]
            else:
                src = [pl.ds(blk * (BLK * d) + r, BLK, stride=d)]
                scr = [pl.ds(blk * 2048 + (r % 4) * 512 + r // 4, BLK, stride=4)]
            return src, scr, blk > 0

        def load(ref, runs):
            parts = [ref[run, :] for run in runs]
            return parts[0] if len(parts) == 1 else jnp.concatenate(parts, axis=0)

        def add(ref, runs, val):
            n = BLK // len(runs)
            for t, run in enumerate(runs):
                ref[run, :] += val[t * n:(t + 1) * n]

        def unstack(st16):
            return st16[:BLK] + st16[BLK:]

        def products(i, par, d, nb):
            src, scr, has_prev = block_rows(i, d, nb)
            tri2 = tri2_q if d == 1 else tri2_t
            s_buf, dp_buf, sd_buf, dpd_buf = mm_bufs[par]
            kc_buf, kp_buf, q_buf, do_buf = op_bufs[par]
            q = load(q_ref, src)
            qs = q * 0.125
            do = load(do_scr, scr)
            qs16, do16 = qs.astype(bf16), do.astype(bf16)
            kst_c = _stack_heads(load(k_ref, src).astype(bf16), head0)
            vst_c = _stack_heads(load(v_ref, src).astype(bf16), head0)
            kst_p, vst_p = op_bufs[1 - par][0][...], vc_bufs[1 - par][...]
            kc_buf[...] = kst_c
            kp_buf[...] = kst_p
            vc_bufs[par][...] = vst_c
            q_buf[...] = q.astype(bf16)
            do_buf[...] = do16
            s_buf[...] = jnp.where(tri2, _nt(qs16, kst_c), jnp.where(has_prev, _nt(qs16, kst_p), -jnp.inf))
            dp_buf[...] = jnp.where(tri2, _nt(do16, vst_c), jnp.where(has_prev, _nt(do16, vst_p), 0.0))
            sd_buf[...] = _nn((qs * unstack(kst_p).astype(f32)).astype(bf16), bones)
            dpd_buf[...] = jnp.where(has_prev, _nn((do * unstack(vst_p).astype(f32)).astype(bf16), bones), 0.0)

        def softmax_grad(i, par, d, nb):
            src, scr, has_prev = block_rows(i, d, nb)
            s_buf, dp_buf, sd_buf, dpd_buf = mm_bufs[par]
            p_buf, ds_buf, pd_buf, dsd_buf = ds_bufs[par]
            lse = load(l_ref, src)
            dl = load(dl_scr, scr)
            pt = jnp.exp(s_buf[...] - both_heads(lse))
            ds_buf[...] = (pt * (dp_buf[...] - both_heads(dl)) * 0.125).astype(bf16)
            p_buf[...] = pt.astype(bf16)
            pd = jnp.where(has_prev, jnp.exp(sd_buf[...] - lse), 0.0)
            pd_buf[...] = pd
            dsd_buf[...] = pd * (dpd_buf[...] - dl) * 0.125

        def accumulate(i, par, d, nb):
            _, rows, _ = block_rows(i, d, nb)
            _, before, _ = block_rows(jnp.maximum(i - 1, 0), d, nb)
            tri2 = tri2_q if d == 1 else tri2_t
            p_buf, ds_buf, pd_buf, dsd_buf = ds_bufs[par]
            kc_buf, kp_buf, q_buf, do_buf = op_bufs[par]
            pt16, ds16, pd, dsd = p_buf[...], ds_buf[...], pd_buf[...], dsd_buf[...]
            zero = jnp.zeros_like(pt16)
            dsc, dsp = jnp.where(tri2, ds16, zero), jnp.where(tri2, zero, ds16)
            pc, pp = jnp.where(tri2, pt16, zero), jnp.where(tri2, zero, pt16)
            kst_c, kst_p, q16, do16 = kc_buf[...], kp_buf[...], q_buf[...], do_buf[...]
            qst, dost = _stack_heads(q16, head0), _stack_heads(do16, head0)
            add(dq_acc, rows, _nn(dsc, kst_c) + _nn(dsp, kst_p) + dsd * unstack(kst_p).astype(f32))
            dk2 = _tn(jnp.concatenate([per_head(dsc), per_head(dsp)], axis=1), qst)
            dv2 = _tn(jnp.concatenate([per_head(pc), per_head(pp)], axis=1), dost)
            add(dk_acc, before, carry_k[...] + dk2[BLK:] + dsd * q16.astype(f32))
            add(dv_acc, before, carry_v[...] + dv2[BLK:] + pd * do16.astype(f32))
            carry_k[...] = dk2[:BLK]
            carry_v[...] = dv2[:BLK]

        for d in DILATIONS:
            nb = s // (BLK * d)
            carry_k[...] = jnp.zeros_like(carry_k)
            carry_v[...] = jnp.zeros_like(carry_v)
            products(0, 0, d, nb)
            products(1, 1, d, nb)
            softmax_grad(0, 0, d, nb)

            def steps(j, carry, d=d, nb=nb):
                for par in range(2):
                    t = 2 * j + 2 + par
                    accumulate(t - 2, par, d, nb)
                    products(t, par, d, nb)
                    softmax_grad(t - 1, 1 - par, d, nb)
                return carry

            lax.fori_loop(0, (n_it - 2) // 2, steps, 0, unroll=True)
            accumulate(n_it - 2, 0, d, nb)
            softmax_grad(n_it - 1, 1, d, nb)
            accumulate(n_it - 1, 1, d, nb)
            _, last, _ = block_rows(n_it - 1, d, nb)
            add(dk_acc, last, carry_k[...])
            add(dv_acc, last, carry_v[...])

        def epi(i, carry):
            rows = pl.ds(pl.multiple_of(i * 1024, 1024), 1024)
            for acc, out, stage in ((dq_acc, dq_ref, stage_a), (dk_acc, dk_ref, stage_b), (dv_acc, dv_ref, stage_a)):
                for q in range(4):
                    stage[pl.ds(q, 256, stride=4), :] = acc[quarter_rows(i, q), :]
                out[rows, :] = stage[...].astype(bf16)
            return carry

        lax.fori_loop(0, s // 1024, epi, 0)

        if nsw:
            @pl.when(pl.program_id(0) == N_PAIRS - 1)
            def _():
                swap_copy.wait_send()
                swap_copy.wait_recv()

    col = lambda base: pl.BlockSpec((s, LANES), lambda h: (0, base + h))
    anyspec = pl.BlockSpec(memory_space=pl.ANY)
    swaps = [] if swap is None else [swap]
    outs = pl.pallas_call(
        body, name="attn_bwd", grid=(N_PAIRS,),
        in_specs=[col(0), col(8), col(16), col(24), col(0), col(0), col(0)] + [anyspec] * nsw,
        out_specs=[col(0)] * 4 + [anyspec] * nsw,
        out_shape=[SDS((s, D_ATTN), bf16)] * 4 + [SDS(a.shape, a.dtype) for a in swaps],
        scratch_shapes=[pltpu.VMEM((s, LANES), f32)] * 5
        + [pltpu.VMEM((BLK, 2 * LANES), f32)] * 2 + [pltpu.VMEM((BLK, LANES), f32)] * 2
        + [pltpu.VMEM((BLK, 2 * LANES), f32)] * 2 + [pltpu.VMEM((BLK, LANES), f32)] * 2
        + [pltpu.VMEM((BLK, 2 * LANES), bf16)] * 2 + [pltpu.VMEM((BLK, LANES), f32)] * 2
        + [pltpu.VMEM((BLK, 2 * LANES), bf16)] * 2 + [pltpu.VMEM((BLK, LANES), f32)] * 2
        + [pltpu.VMEM((2 * BLK, LANES), bf16)] * 2 + [pltpu.VMEM((BLK, LANES), bf16)] * 2
        + [pltpu.VMEM((2 * BLK, LANES), bf16)] * 2 + [pltpu.VMEM((BLK, LANES), bf16)] * 2
        + [pltpu.VMEM((2 * BLK, LANES), bf16)] * 2 + [pltpu.VMEM((BLK, LANES), f32)] * 2
        + [pltpu.VMEM((1024, LANES), f32)] * 2
        + [pltpu.SemaphoreType.DMA(())] * (2 * nsw),
        compiler_params=pltpu.CompilerParams(dimension_semantics=("arbitrary",)),
    )(proj, proj, proj, proj, o, lb, dmix, *swaps)
    return outs


def _ssd_bwd(proj, y, states, cv, dmix, conv_w, conv_b, dtb16, alog16, alog_f, d_f, nw, chip_sums=()):
    s = proj.shape[0]
    nc = s // CHUNK
    gw = D_SSM // N_GROUPS
    nx = len(chip_sums)

    def body(*refs):
        (xs_ref, bc_ref, _, _, dt_ref, z_ref, y_ref, st_ref, dm_ref, cw_ref, cb_ref, dtb_ref,
         alog16_ref, alogf_ref, df_ref, nw_ref, cv_ref) = refs[:17]
        cs_in = refs[17:17 + nx]
        out_ref, gconv_ref, gvec_ref, gdt_ref = refs[17 + nx:21 + nx]
        cs_out = refs[21 + nx:21 + 2 * nx]
        (dh_scr, head_scr, dcpad, da_scr, dxdt_scr, dbc_scr, emat_ref, fold_ref) = refs[21 + 2 * nx:29 + 2 * nx]
        cs_sems = refs[29 + 2 * nx:]
        i = pl.program_id(0)
        c = nc - 1 - i

        if nx:
            @pl.when(i == 0)
            def _():
                mine, sends, _ = _chip_exchange_copies(cs_in, cs_out, *cs_sems)
                for cp in mine + sends:
                    cp.start()

            @pl.when(i == nc - 1)
            def _():
                mine, sends, recvs = _chip_exchange_copies(cs_in, cs_out, *cs_sems)
                for cp in recvs:
                    cp.wait_recv()
                for cp in sends:
                    cp.wait_send()
                for cp in mine:
                    cp.wait()

        @pl.when(i == 0)
        def _():
            emat_ref[...] = _expand_mat()
            fold_ref[...] = _fold_mat()
            dh_scr[...] = jnp.zeros_like(dh_scr)
            head_scr[...] = jnp.zeros_like(head_scr)
            gconv_ref[...] = jnp.zeros_like(gconv_ref)
            gvec_ref[...] = jnp.zeros_like(gvec_ref)
            gdt_ref[...] = jnp.zeros_like(gdt_ref)

        cv = cv_ref[...]
        sig = _sigmoid(cv)
        xbc = cv * sig
        pre, dt_f, al_f, al_x, al_t = _decay_terms(dt_ref, dtb_ref, alog16_ref, emat_ref)
        head0 = _iota((CHUNK, LANES), 1) < HEAD_DIM
        sub = _iota((CHUNK, LANES), 0)
        last_row = sub == CHUNK - 1

        yv, z, dmx = y_ref[...], z_ref[...], dm_ref[...]
        sz = _sigmoid(z)
        silu = z * sz
        yz = yv * silu
        dyz_parts = []
        for g in range(N_GROUPS):
            gs = slice(g * gw, (g + 1) * gw)
            part = yz[:, gs]
            r = lax.rsqrt(jnp.mean(part * part, axis=-1, keepdims=True) + EPS)
            nh = part * r
            gvec_ref[0:1, gs] += jnp.sum(dmx[:, gs] * nh, axis=0, keepdims=True)
            dn = dmx[:, gs] * nw_ref[:, gs]
            dyz_parts.append(r * (dn - nh * jnp.mean(dn * nh, axis=-1, keepdims=True)))
        dyz = jnp.concatenate(dyz_parts, axis=1)
        dy = dyz * silu
        out_ref[:, 0:D_SSM] = (dyz * yv * (sz * (1.0 + z * (1.0 - sz)))).astype(bf16)

        x_all = xbc[:, 0:D_SSM]
        gvec_ref[2:3, :] += jnp.sum(dy * x_all, axis=0, keepdims=True)

        for g in range(N_GROUPS):
            bm = xbc[:, D_SSM + g * D_STATE: D_SSM + (g + 1) * D_STATE].astype(bf16)
            cm = xbc[:, D_SSM + (N_GROUPS + g) * D_STATE: D_SSM + (N_GROUPS + g + 1) * D_STATE].astype(bf16)
            gmat = _nt(cm, bm)
            dgm = jnp.zeros((CHUNK, CHUNK), f32)
            db = jnp.zeros((CHUNK, D_STATE), f32)
            dc = jnp.zeros((CHUNK, D_STATE), f32)
            for pair in range(4 * g, 4 * g + 4):
                sl = slice(pair * LANES, (pair + 1) * LANES)
                xp, dtp, alp, dyp = x_all[:, sl], dt_f[:, sl], al_f[:, sl], dy[:, sl]
                xdt = xp * dtp
                xdt16 = xdt.astype(bf16)
                al_last = alp[CHUNK - 1:CHUNK, :]
                e_l = jnp.exp(alp)
                wf = jnp.exp(al_last - alp)
                e_last = jnp.exp(al_last)
                hp = st_ref[:, sl]
                hp16 = hp.astype(bf16)
                dhn = dh_scr[:, sl]
                dhn16 = dhn.astype(bf16)
                y_off = e_l * _nn(cm, hp16)
                dch16 = (dyp * e_l).astype(bf16)
                dc = dc + _nt(dch16, hp16)
                dh_out = _tn(cm, dch16)
                dal = dyp * y_off
                xw16 = (wf * xdt).astype(bf16)
                db = db + _nt(xw16, dhn16)
                dxw = _nn(bm, dhn16)
                dxdt = dxw * wf
                dwf = dxw * xdt * wf
                dal = dal - dwf
                dal_last = jnp.sum(dwf, axis=0, keepdims=True) + jnp.sum(dhn * hp, axis=0, keepdims=True) * e_last
                dh_scr[:, sl] = e_last * dhn + dh_out
                for h in range(2):
                    mh = head0 if h == 0 else jnp.logical_not(head0)
                    dyh16 = jnp.where(mh, dyp, 0.0).astype(bf16)
                    lmat = _decay_mat(al_x, al_t, pair, h)
                    mm = gmat * lmat
                    dmm = _nt(dyh16, xdt16)
                    dxdt = dxdt + _tn(mm.astype(bf16), dyh16)
                    n16 = (dmm * mm).astype(bf16)
                    jh = jnp.where(mh, 1.0 / HEAD_DIM, 0.0).astype(bf16)
                    dal = dal + _nn(n16, jh) - _tn(n16, jh)
                    dgm = dgm + dmm * lmat
                da_scr[:, sl] = dal + jnp.where(last_row, dal_last, 0.0)
                dxdt_scr[:, sl] = dxdt
            dgm16 = dgm.astype(bf16)
            dbc_scr[:, g * D_STATE:(g + 1) * D_STATE] = db + _tn(dgm16, cm)
            dbc_scr[:, (N_GROUPS + g) * D_STATE:(N_GROUPS + g + 1) * D_STATE] = dc + _nn(dgm16, bm)

        sub_c, lane_c = _iota((CHUNK, CHUNK), 0), _iota((CHUNK, CHUNK), 1)
        tri_t = (lane_c >= sub_c).astype(bf16)
        dadt = _dot_01_left(tri_t, da_scr[...], 2)
        a_f = -jnp.exp(alogf_ref[...])
        dxdt_all = dxdt_scr[...]
        ddt_f = dxdt_all * x_all + a_f * dadt
        gvec_ref[1:2, :] += jnp.sum(dt_f * dadt, axis=0, keepdims=True) * a_f
        dx = df_ref[...] * dy + dxdt_all * dt_f
        ddt_raw = _dot_01(ddt_f, fold_ref[...], 2) * _sigmoid(pre)
        gdt_ref[0:1, :] += jnp.sum(ddt_raw, axis=0, keepdims=True)
        out_ref[:, D_SSM + D_CONV:D_SSM + D_CONV + LANES] = ddt_raw.astype(bf16)
        out_ref[:, D_SSM + D_CONV + LANES:] = jnp.zeros((CHUNK, 3 * LANES), bf16)

        dsil = sig * (1.0 + cv * (1.0 - sig))
        dcv_x = dx * dsil[:, 0:D_SSM]
        dcv_bc = dbc_scr[...] * dsil[:, D_SSM:]
        dcpad[0:CHUNK, 0:D_SSM] = dcv_x
        dcpad[0:CHUNK, D_SSM:] = dcv_bc
        dcpad[CHUNK:, :] = head_scr[...]
        dcp = dcpad[...]
        dcv = dcp[0:CHUNK]
        gconv_ref[4:5, :] += jnp.sum(dcv, axis=0, keepdims=True)
        x_raw = jnp.concatenate([xs_ref[...], bc_ref[...]], axis=1)
        draw = cw_ref[3:4, :] * dcv
        gconv_ref[3:4, :] += jnp.sum(dcv * x_raw, axis=0, keepdims=True)
        for j in range(3):
            ahead = pltpu.roll(dcp, CHUNK + 8 - (3 - j), 0)[0:CHUNK]
            draw = draw + cw_ref[j:j + 1, :] * ahead
            gconv_ref[j:j + 1, :] += jnp.sum(ahead * x_raw, axis=0, keepdims=True)
        head_scr[...] = dcv[0:8]
        out_ref[:, D_SSM:D_SSM + D_CONV] = draw.astype(bf16)

    order = lambda i: nc - 1 - i
    row = lambda w, cb=0: pl.BlockSpec((CHUNK, w), lambda i: (nc - 1 - i, cb))
    anyspec = pl.BlockSpec(memory_space=pl.ANY)
    outs = pl.pallas_call(
        body, name="ssd_bwd", grid=(nc,),
        in_specs=_ssd_in_specs(order) + [row(D_SSM), pl.BlockSpec((None, D_STATE, D_SSM), lambda i: (nc - 1 - i, 0, 0)),
                                         row(D_SSM, 1), _full((4, D_CONV)), _full((1, D_CONV)), _full((1, LANES)),
                                         _full((1, LANES)), _full((1, D_SSM)), _full((1, D_SSM)), _full((1, D_SSM)),
                                         row(D_CONV)]
        + [anyspec] * nx,
        out_specs=[row(3072), _full((8, D_CONV)), _full((8, D_SSM)), _full((8, LANES))] + [anyspec] * nx,
        out_shape=[SDS((s, 3072), bf16), SDS((8, D_CONV), f32), SDS((8, D_SSM), f32), SDS((8, LANES), f32)]
        + [SDS(a.shape, a.dtype) for a in chip_sums],
        scratch_shapes=[pltpu.VMEM((D_STATE, D_SSM), f32), pltpu.VMEM((8, D_CONV), f32),
                        pltpu.VMEM((8 + CHUNK, D_CONV), f32),
                        pltpu.VMEM((CHUNK, D_SSM), f32), pltpu.VMEM((CHUNK, D_SSM), f32),
                        pltpu.VMEM((CHUNK, 2 * N_GROUPS * D_STATE), f32),
                        pltpu.VMEM((LANES, 2 * D_SSM), bf16), pltpu.VMEM((D_SSM, LANES), bf16)]
        + (_chip_exchange_scratch(nx) if nx else []),
        compiler_params=pltpu.CompilerParams(dimension_semantics=("arbitrary",)),
    )(proj, proj, proj, proj, proj, proj, y, states, dmix, conv_w, conv_b, dtb16, alog16, alog_f, d_f, nw, cv,
      *chip_sums)
    return outs[0], outs[1], outs[2], outs[3], outs[4:]


def _col_blocks(parts, tile):
    counts = [p.shape[1] // tile for p in parts]
    offs = [sum(counts[:t]) for t in range(len(parts))]
    return offs, counts, sum(counts)


def _bcast_copies(src_ref, out_ref, send_sems, recv_sems, local_sem):
    x, y, c = _my_pos()
    me = 4 * x + 2 * y + c
    mine = pltpu.make_async_copy(src_ref, out_ref.at[me], local_sem)
    sends, recvs = [], []
    for k in range(1, N_DEV):
        to, frm = (me + k) % N_DEV, (me + N_DEV - k) % N_DEV
        sems = dict(send_sem=send_sems.at[k - 1], recv_sem=recv_sems.at[k - 1], device_id_type=MESH)
        sends.append(pltpu.make_async_remote_copy(
            src_ref=src_ref, dst_ref=out_ref.at[me], device_id=(to // 4, (to // 2) % 2, to % 2), **sems))
        recvs.append(pltpu.make_async_remote_copy(
            src_ref=src_ref, dst_ref=out_ref.at[frm], device_id=(x, y, c), **sems))
    return mine, sends, recvs


def _bcast_scratch():
    return [pltpu.SemaphoreType.DMA((N_DEV - 1,)), pltpu.SemaphoreType.DMA((N_DEV - 1,)), pltpu.SemaphoreType.DMA(())]


def _inproj_bwd(dparts, wt, x, nw, dres, chip_sums=(), pack=None):
    s, d = x.shape
    tm, tk = 1024, 1024
    offs, counts, nk = _col_blocks(dparts, tk)
    npart, nx = len(dparts), len(chip_sums)
    npk = 0 if pack is None else 1
    ni = s // tm

    def body(*refs):
        dp_refs = refs[:npart]
        w_ref, x_ref, nw_ref, dres_ref = refs[npart:npart + 4]
        pos = npart + 4
        cs_in, pos = refs[pos:pos + nx], pos + nx
        pack_in, pos = refs[pos:pos + npk], pos + npk
        (gx_ref, gnw_ref), pos = refs[pos:pos + 2], pos + 2
        cs_out, pos = refs[pos:pos + nx], pos + nx
        pack_out, pos = refs[pos:pos + 2 * npk], pos + 2 * npk
        acc, pos = refs[pos], pos + 1
        cs_sems, pos = refs[pos:pos + 3 * min(nx, 1)], pos + 3 * min(nx, 1)
        pk_refs = refs[pos:]
        i, k = pl.program_id(0), pl.program_id(1)

        def exchange():
            return _chip_exchange_copies(cs_in, cs_out, *cs_sems)

        def pack_copies():
            return _bcast_copies(pack_in[0], pack_out[0], *pk_refs[1:4])

        def gnw_copies():
            return _bcast_copies(pk_refs[0], pack_out[1], *pk_refs[4:7])

        @pl.when(jnp.logical_and(i == 0, k == 0))
        def _():
            gnw_ref[...] = jnp.zeros_like(gnw_ref)
            if nx:
                mine, sends, _ = exchange()
                for cp in mine + sends:
                    cp.start()
            if npk:
                mine, sends, _ = pack_copies()
                for cp in [mine] + sends:
                    cp.start()

        @pl.when(k == 0)
        def _():
            acc[...] = _nn(dp_refs[0][...], w_ref[...])

        for t in range(npart):
            @pl.when(jnp.logical_and(k >= max(offs[t], 1), k < offs[t] + counts[t]))
            def _(t=t):
                acc[...] += _nn(dp_refs[t][...], w_ref[...])

        @pl.when(k == nk - 1)
        def _():
            xv = x_ref[...]
            r = lax.rsqrt(jnp.mean(xv * xv, axis=-1, keepdims=True) + EPS)
            xn = xv * r
            du = acc[...]
            gnw_ref[0:1, :] += jnp.sum(du * xn, axis=0, keepdims=True)
            dn = du * nw_ref[...]
            gx_ref[...] = dres_ref[...] + r * (dn - xn * jnp.mean(dn * xn, axis=-1, keepdims=True))

        @pl.when(jnp.logical_and(i == ni - 1, k == nk - 1))
        def _():
            if npk:
                pk_refs[0][...] = gnw_ref[...]
                mine, sends, _ = gnw_copies()
                for cp in [mine] + sends:
                    cp.start()
            if nx:
                mine, sends, recvs = exchange()
                for cp in recvs:
                    cp.wait_recv()
                for cp in sends:
                    cp.wait_send()
                for cp in mine:
                    cp.wait()
            if npk:
                for copies in (pack_copies(), gnw_copies()):
                    mine, sends, recvs = copies
                    for cp in recvs:
                        cp.wait_recv()
                    for cp in sends:
                        cp.wait_send()
                    mine.wait()

    def piece(t):
        return pl.BlockSpec((tm, tk), lambda i, k: (i, jnp.clip(k - offs[t], 0, counts[t] - 1)))

    anyspec = pl.BlockSpec(memory_space=pl.ANY)
    packs = [] if pack is None else [pack]
    pack_shapes = [] if pack is None else [SDS((N_DEV,) + pack.shape, f32), SDS((N_DEV, 8, d), f32)]
    scratch = [pltpu.VMEM((tm, d), f32)] + (_chip_exchange_scratch(nx) if nx else [])
    if npk:
        scratch += [pltpu.VMEM((8, d), f32)] + _bcast_scratch() + _bcast_scratch()
    outs = pl.pallas_call(
        body, name="inproj_bwd", grid=(ni, nk),
        in_specs=[piece(t) for t in range(npart)] + [
            pl.BlockSpec((tk, d), lambda i, k: (k, 0)),
            pl.BlockSpec((tm, d), lambda i, k: (i, 0)), pl.BlockSpec((1, d), lambda i, k: (0, 0)),
            pl.BlockSpec((tm, d), lambda i, k: (i, 0))] + [anyspec] * (nx + npk),
        out_specs=[pl.BlockSpec((tm, d), lambda i, k: (i, 0)), pl.BlockSpec((8, d), lambda i, k: (0, 0))]
        + [anyspec] * (nx + 2 * npk),
        out_shape=[SDS((s, d), f32), SDS((8, d), f32)] + [SDS(a.shape, a.dtype) for a in chip_sums] + pack_shapes,
        scratch_shapes=scratch,
        compiler_params=pltpu.CompilerParams(dimension_semantics=("arbitrary", "arbitrary")),
    )(*dparts, wt, x, nw, dres, *chip_sums, *packs)
    return outs[0], outs[1], outs[2:2 + nx], outs[2 + nx:]


def _matmul_tn(a_parts, b_parts, name):
    tile, tk = 1024, 1024
    s = a_parts[0].shape[0]
    nk = s // tk
    na, nb = len(a_parts), len(b_parts)
    offs_a, counts_a, ni = _col_blocks(a_parts, tile)
    offs_b, counts_b, nj = _col_blocks(b_parts, tile)

    def body(*refs):
        a_refs, b_refs, o_ref = refs[:na], refs[na:na + nb], refs[na + nb]
        i, j = pl.program_id(0), pl.program_id(1)

        @pl.when(pl.program_id(2) == 0)
        def _():
            o_ref[...] = jnp.zeros_like(o_ref)

        for ta in range(na):
            for tb in range(nb):
                in_a = jnp.logical_and(i >= offs_a[ta], i < offs_a[ta] + counts_a[ta])
                in_b = jnp.logical_and(j >= offs_b[tb], j < offs_b[tb] + counts_b[tb])

                @pl.when(jnp.logical_and(in_a, in_b))
                def _(ta=ta, tb=tb):
                    o_ref[...] += _tn(a_refs[ta][...], b_refs[tb][...])

    def spec(offs, counts, t, axis):
        def index(i, j, k):
            pos = (i, j)[axis]
            mine = jnp.logical_and(pos >= offs[t], pos < offs[t] + counts[t])
            return jnp.where(mine, k, 0), jnp.clip(pos - offs[t], 0, counts[t] - 1)
        return pl.BlockSpec((tk, tile), index)

    return pl.pallas_call(
        body, name=name, grid=(ni, nj, nk),
        in_specs=[spec(offs_a, counts_a, t, 0) for t in range(na)] + [spec(offs_b, counts_b, t, 1) for t in range(nb)],
        out_specs=pl.BlockSpec((tile, tile), lambda i, j, k: (i, j)),
        out_shape=SDS((ni * tile, nj * tile), f32),
        compiler_params=pltpu.CompilerParams(dimension_semantics=("parallel", "parallel", "arbitrary")),
    )(*a_parts, *b_parts)


def _adamw(w, g, m, v):
    m = ADAM_B1 * m + (1.0 - ADAM_B1) * g
    v = ADAM_B2 * v + (1.0 - ADAM_B2) * (g * g)
    m_hat = m / (1.0 - ADAM_B1 ** ADAM_STEP)
    v_hat = v / (1.0 - ADAM_B2 ** ADAM_STEP)
    delta = -ADAM_LR * (m_hat / (jnp.sqrt(v_hat) + ADAM_EPS) + ADAM_WD * w)
    return delta, m, v


def _sum_adamw(parts, w, m, v, name):
    r, c = w.shape
    tc = 256

    def body(p_ref, w_ref, m_ref, v_ref, g_ref, d_ref, nm_ref, nv_ref):
        g = p_ref[0].astype(f32)
        for q in range(1, 4):
            g = g + p_ref[q].astype(f32)
        g_ref[...] = g
        d_ref[...], nm_ref[...], nv_ref[...] = _adamw(w_ref[...], g, m_ref[...], v_ref[...])

    blk = pl.BlockSpec((r, tc), lambda i: (0, i))
    return pl.pallas_call(
        body, name=name, grid=(c // tc,),
        in_specs=[pl.BlockSpec((4, r, tc), lambda i: (0, 0, i)), blk, blk, blk],
        out_specs=[blk] * 4, out_shape=[SDS((r, c), f32)] * 4,
        compiler_params=pltpu.CompilerParams(dimension_semantics=("parallel",)),
    )(parts, w, m, v)


def _sum_small(parts, pre_blocks):
    def body(p_ref, b_ref, o_ref):
        t = p_ref[0]
        pre = b_ref[0]
        for j in range(1, N_DEV):
            t = t + p_ref[j]
            pre = pre + b_ref[j]
        o_ref[...] = t
        o_ref[5:6, 0:D_MODEL] = pre[0:1, :]
        row_h = _iota((D_SSM, LANES), 0) // HEAD_DIM
        fold = (row_h == _iota((D_SSM, LANES), 1)).astype(f32)
        lower = t[8:16, 0:LANES]
        folded = _nn_hi(t[8:16, 0:D_SSM], fold)
        loss = jnp.sum(t[11:12, 0:D_MODEL], axis=1, keepdims=True) * (0.5 / D_MODEL)
        row = _iota((8, LANES), 0)
        o_ref[8:16, 0:LANES] = jnp.where(row < 2, folded, jnp.where(row == 4, loss, lower))

    return pl.pallas_call(body, name="sum_small", out_shape=SDS((PACK_ROWS, PACK_W), f32),
                          in_specs=[pl.BlockSpec(memory_space=pltpu.VMEM)] * 2,
                          out_specs=pl.BlockSpec(memory_space=pltpu.VMEM))(parts, pre_blocks)


def _adamw_small(w, g, m, v):
    def body(w_ref, g_ref, m_ref, v_ref, d_ref, nm_ref, nv_ref):
        d_ref[...], nm_ref[...], nv_ref[...] = _adamw(w_ref[...], g_ref[...], m_ref[...], v_ref[...])

    vm = pl.BlockSpec(memory_space=pltpu.VMEM)
    return pl.pallas_call(body, name="adamw_small", out_shape=[SDS(w.shape, f32)] * 3,
                          in_specs=[vm] * 4, out_specs=[vm] * 3)(w, g, m, v)


def _pad_lanes(v, width):
    return jnp.pad(v, ((0, 0), (0, width - v.shape[1])))


def _local_step(x, tgt, norm_pre_w, wt, conv_w, conv_b, dt_bias, a_log, d_skip, ssm_norm_w, wo, norm_post_w, sharded):
    dtb16 = _pad_lanes(dt_bias, LANES)
    alog16 = _pad_lanes(a_log, LANES)
    alog_f = jnp.repeat(a_log, HEAD_DIM, axis=1)
    d_f = jnp.repeat(d_skip, HEAD_DIM, axis=1)

    shard_out = wo.shape[0]
    if sharded:
        proj, u, (g_out, g_cw) = _prenorm_inproj(x, norm_pre_w, wt, gather=(wo, conv_w))
        wo = g_out.reshape(N_DEV * shard_out, D_MODEL)
        conv_w = g_cw.transpose(1, 0, 2).reshape(4, D_CONV)
    else:
        proj, u, _ = _prenorm_inproj(x, norm_pre_w, wt)
    o, lb, mix_a = _attn_fwd(proj)
    mix_s, y, states, cv = _ssd_fwd(proj, conv_w, conv_b, dtb16, alog16, alog_f, d_f, ssm_norm_w)
    dmix, dres, acc_post, dw_out = _outproj_loss(mix_a, mix_s, wo, x, tgt, norm_post_w)
    ssd_args = (proj, y, states, cv, dmix, conv_w, conv_b, dtb16, alog16, alog_f, d_f, ssm_norm_w)
    if sharded:
        dq, dk, dv, dg, got_out = _attn_bwd(proj, o, lb, dmix, swap=dw_out)
        chip_out = _chip_sum(dw_out, got_out, shard_out, "chip_sum_w_out")
        dzxd, g_conv, g_vec, g_dt, (parts_out,) = _ssd_bwd(*ssd_args, chip_sums=[chip_out])
    else:
        dq, dk, dv, dg = _attn_bwd(proj, o, lb, dmix)
        dzxd, g_conv, g_vec, g_dt, _ = _ssd_bwd(*ssd_args)
    dparts = [dq, dk, dv, dg, dzxd]

    def pack(g_pre_row):
        return jnp.concatenate(
            [g_conv[0:5], g_pre_row, _pad_lanes(g_vec[0:1], PACK_W), _pad_lanes(acc_post[1:2], PACK_W),
             _pad_lanes(g_vec[1:3], PACK_W), _pad_lanes(g_dt[0:1], PACK_W), _pad_lanes(acc_post[0:1], PACK_W),
             jnp.zeros((4, PACK_W), f32)], axis=0)

    if sharded:
        dw_in, got_in = _dw_in_swap(dparts, u)
        chip_in = _chip_sum(dw_in, got_in, D_IN_PROJ // N_DEV, "chip_sum_w_in")
        grad_x, _, (parts_in,), small = _inproj_bwd(dparts, wt, x, norm_pre_w, dres, [chip_in],
                                                    pack(jnp.zeros((1, PACK_W), f32)))
        return grad_x, (parts_in, parts_out), small
    dw_in = _matmul_tn(dparts, [u], "dw_in")
    grad_x, g_pre, _, _ = _inproj_bwd(dparts, wt, x, norm_pre_w, dres)
    return grad_x, (dw_in, dw_out), pack(_pad_lanes(g_pre[0:1], PACK_W))


def kernel(x, norm_pre_w, w_in, conv_w, conv_b, dt_bias, a_log, d_skip, ssm_norm_w, w_out, norm_post_w, loss_target, m_norm_pre_w, m_w_in, m_conv_w, m_conv_b, m_dt_bias, m_a_log, m_d_skip, m_ssm_norm_w, m_w_out, m_norm_post_w, v_norm_pre_w, v_w_in, v_conv_w, v_conv_b, v_dt_bias, v_a_log, v_d_skip, v_ssm_norm_w, v_w_out, v_norm_post_w):
    shard_cv = conv_w.shape[2]
    me = 4 * lax.axis_index("x") + 2 * lax.axis_index("y") + lax.axis_index("c")

    g_in, = _all_gather([w_in[0].T.astype(bf16)])
    wt = _assemble_wt(g_in)

    grad_x, (parts_in, parts_out), (parts_small, pre_blocks) = _local_step(
        x[0], loss_target[0], norm_pre_w, wt, conv_w[0], conv_b, dt_bias, a_log, d_skip, ssm_norm_w,
        w_out[0].astype(bf16), norm_post_w, sharded=True)

    g_w_in, d_w_in, nm_w_in, nv_w_in = (a.T for a in _sum_adamw(
        parts_in, w_in[0].T, m_w_in[0].T, v_w_in[0].T, "sum_adamw_w_in"))
    g_w_out, d_w_out, nm_w_out, nv_w_out = _sum_adamw(parts_out, w_out[0], m_w_out[0], v_w_out[0], "sum_adamw_w_out")
    tot = _sum_small(parts_small, pre_blocks)

    g_cw_all = tot[0:4]
    small_g = {
        "conv_w": lax.dynamic_slice(g_cw_all, (0, me * shard_cv), (4, shard_cv)),
        "conv_b": tot[4:5], "norm_pre_w": tot[5:6, :D_MODEL], "ssm_norm_w": tot[6:7, :D_SSM],
        "norm_post_w": tot[7:8, :D_MODEL], "a_log": tot[8:9, :16], "d_skip": tot[9:10, :16], "dt_bias": tot[10:11, :16],
    }
    loss = tot[12, 0]
    small_w = {"conv_w": (conv_w[0], m_conv_w[0], v_conv_w[0]), "conv_b": (conv_b, m_conv_b, v_conv_b),
               "norm_pre_w": (norm_pre_w, m_norm_pre_w, v_norm_pre_w), "ssm_norm_w": (ssm_norm_w, m_ssm_norm_w, v_ssm_norm_w),
               "norm_post_w": (norm_post_w, m_norm_post_w, v_norm_post_w), "a_log": (a_log, m_a_log, v_a_log),
               "d_skip": (d_skip, m_d_skip, v_d_skip), "dt_bias": (dt_bias, m_dt_bias, v_dt_bias)}
    names = list(small_w)
    sizes = [small_g[k].size for k in names]
    tot_size = sum(sizes)
    pad_to = -(-tot_size // 1024) * 1024

    def flat(arrs):
        v = jnp.concatenate([a.reshape(-1) for a in arrs])
        return jnp.pad(v, (0, pad_to - tot_size)).reshape(pad_to // LANES, LANES)

    fw = flat([small_w[k][0] for k in names])
    fg = flat([small_g[k] for k in names])
    fm = flat([small_w[k][1] for k in names])
    fv = jnp.pad(jnp.concatenate([small_w[k][2].reshape(-1) for k in names]), (0, pad_to - tot_size),
                 constant_values=1.0).reshape(pad_to // LANES, LANES)
    fd, fnm, fnv = _adamw_small(fw, fg, fm, fv)

    def unflat(f):
        out, off = {}, 0
        v = f.reshape(-1)
        for k, n in zip(names, sizes):
            out[k] = v[off:off + n].reshape(small_g[k].shape)
            off += n
        return out

    sd, snm, snv = unflat(fd), unflat(fnm), unflat(fnv)
    lead = lambda a: a[None]
    order = ["norm_pre_w", "w_in", "conv_w", "conv_b", "dt_bias", "a_log", "d_skip", "ssm_norm_w", "w_out", "norm_post_w"]
    grads = dict(small_g, w_in=g_w_in, w_out=g_w_out)
    deltas = dict(sd, w_in=d_w_in, w_out=d_w_out)
    new_m = dict(snm, w_in=nm_w_in, w_out=nm_w_out)
    new_v = dict(snv, w_in=nv_w_in, w_out=nv_w_out)

    def shaped(dct, k):
        a = dct[k]
        return lead(a) if k in ("w_in", "w_out", "conv_w") else a

    return (loss, grad_x[None], *[shaped(grads, k) for k in order], *[shaped(deltas, k) for k in order],
            *[shaped(new_m, k) for k in order], *[shaped(new_v, k) for k in order])
```

```python
import jax
import jax.numpy as jnp
from jax import lax
from jax.experimental import pallas as pl
from jax.experimental.pallas import tpu as pltpu

f32, bf16 = jnp.float32, jnp.bfloat16
SDS = jax.ShapeDtypeStruct
HIGHEST = lax.Precision.HIGHEST
MESH = pl.DeviceIdType.MESH

N_DEV = 8
D_MODEL = 1024
D_ATTN = 1024
D_SSM = 1024
HEAD_DIM = 64
N_PAIRS = 8
D_STATE = 128
N_GROUPS = 2
D_CONV = D_SSM + 2 * N_GROUPS * D_STATE
D_IN_PROJ = 4 * D_ATTN + D_SSM + D_CONV + 16
NP = 7168
CHUNK = 128
BLK = 128
DILATIONS = (1, 4, 16)
EPS = 1e-6
LANES = 128
COL_Z, COL_XS, COL_BC, COL_DT = 4096, 5120, 6144, 6656

ADAM_LR, ADAM_B1, ADAM_B2, ADAM_EPS, ADAM_WD, ADAM_STEP = 0.001, 0.9, 0.999, 1e-08, 0.01, 10

PACK_ROWS, PACK_W = 16, 1536


def _nt(a, b):
    return lax.dot_general(a, b, (((1,), (1,)), ((), ())), preferred_element_type=f32)


def _tn(a, b):
    return lax.dot_general(a, b, (((0,), (0,)), ((), ())), preferred_element_type=f32)


def _nn(a, b):
    return jnp.dot(a, b, preferred_element_type=f32)


def _nn_hi(a, b):
    return jnp.dot(a, b, precision=HIGHEST, preferred_element_type=f32)


def _sigmoid(x):
    return 1.0 / (1.0 + jnp.exp(-x))


def _softplus(x):
    return jnp.maximum(x, 0.0) + jnp.log1p(jnp.exp(-jnp.abs(x)))


def _iota(shape, dim):
    return lax.broadcasted_iota(jnp.int32, shape, dim)


def _my_pos():
    return lax.axis_index("x"), lax.axis_index("y"), lax.axis_index("c")


GATHER_SEMS = 9


def _gather_phases(ins, outs, send_sems, recv_sems, local_sems):
    n, ns = len(ins), GATHER_SEMS
    x, y, c = _my_pos()
    me, sibling = (x, y, c), (x, y, 1 - c)
    xn, yn, diag = (1 - x, y), (x, 1 - y), (1 - x, 1 - y)

    def slot(a, px, py, pc):
        return outs[a].at[4 * px + 2 * py + pc]

    def part(a, ref, h):
        width = ins[a].shape[-1]
        if width % (2 * LANES):
            return ref if h == 1 else None
        return ref.at[:, pl.ds(h * (width // 2), width // 2)]

    def copy(a, k, block, to, src=None, h=None):
        src_ref = slot(a, *block) if src is None else src
        dst_ref = slot(a, *block)
        if h is not None:
            src_ref, dst_ref = part(a, src_ref, h), part(a, dst_ref, h)
            if src_ref is None:
                return None
        return pltpu.make_async_remote_copy(
            src_ref=src_ref, dst_ref=dst_ref, send_sem=send_sems.at[ns * a + k], recv_sem=recv_sems.at[ns * a + k],
            device_id=to, device_id_type=MESH)

    def mine():
        return [pltpu.make_async_copy(ins[a], slot(a, *me), local_sems.at[a]) for a in range(n)]

    def own_sends(a):
        return [copy(a, 0, me, sibling, src=ins[a]), copy(a, 1, me, (*xn, c), src=ins[a]),
                copy(a, 2, me, (*yn, c), src=ins[a])]

    def neighbour_relays(a):
        return [copy(a, 4, (*xn, c), sibling), copy(a, 7, (*xn, c), (*yn, c), h=1),
                copy(a, 5, (*yn, c), sibling), copy(a, 8, (*yn, c), (*xn, c), h=0)]

    def diagonal_halves(a):
        return [copy(a, k, (*diag, c), me, h=h) for k, h in ((8, 0), (7, 1))]

    def start_all(cps):
        for cp in cps:
            if cp is not None:
                cp.start()

    def phase0():
        start_all(mine())
        for a in range(n):
            start_all(own_sends(a))

    def phase1():
        for a in range(n):
            copy(a, 1, (*xn, c), me).wait_recv()
            copy(a, 2, (*yn, c), me).wait_recv()
            start_all(neighbour_relays(a))

    def phase2():
        for a in range(n):
            for cp in diagonal_halves(a):
                if cp is not None:
                    cp.wait_recv()
            copy(a, 6, (*diag, c), sibling).start()

    def finish():
        for a in range(n):
            copy(a, 0, sibling, me).wait_recv()
            for j, chip in enumerate((xn, yn, diag)):
                copy(a, 4 + j, (*chip, 1 - c), me).wait_recv()
        for a in range(n):
            for cp in own_sends(a) + neighbour_relays(a) + [copy(a, 6, (*diag, c), sibling)]:
                if cp is not None:
                    cp.wait_send()
        for cp in mine():
            cp.wait()

    return phase0, phase1, phase2, finish


def _gather_scratch(n):
    return [pltpu.SemaphoreType.DMA((GATHER_SEMS * n,)), pltpu.SemaphoreType.DMA((GATHER_SEMS * n,)),
            pltpu.SemaphoreType.DMA((n,))]


def _all_gather(arrs):
    n = len(arrs)

    def body(*refs):
        for phase in _gather_phases(refs[:n], refs[n:2 * n], *refs[2 * n:]):
            phase()

    anyspec = pl.BlockSpec(memory_space=pl.ANY)
    return pl.pallas_call(
        body, name="weights_all_gather",
        out_shape=[SDS((N_DEV,) + a.shape, a.dtype) for a in arrs],
        in_specs=[anyspec] * n, out_specs=[anyspec] * n, scratch_shapes=_gather_scratch(n),
    )(*arrs)


def _dw_in_swap(a_parts, u):
    tile, tk = 1024, 1024
    s = u.shape[0]
    nk = s // tk
    na = len(a_parts)
    offs, counts, ni = _col_blocks(a_parts, tile)

    def body(*refs):
        a_refs, u_ref = refs[:na], refs[na]
        dw_ref, got_ref = refs[na + 1:na + 3]
        acc, stage, local_sems, send_sems, recv_sem = refs[na + 3:]
        i, k = pl.program_id(0), pl.program_id(1)
        x, y, c = _my_pos()
        par = i % 2

        def tile_copies(t, p):
            rows = pl.ds(pl.multiple_of(t * tile, tile), tile)
            loc = pltpu.make_async_copy(stage.at[p], dw_ref.at[rows], local_sems.at[p])
            rem = pltpu.make_async_remote_copy(
                src_ref=stage.at[p], dst_ref=got_ref.at[rows], send_sem=send_sems.at[p], recv_sem=recv_sem,
                device_id=(x, y, 1 - c), device_id_type=MESH)
            return loc, rem

        @pl.when(k == 0)
        def _():
            acc[...] = jnp.zeros((tile, tile), f32)

        for t in range(na):
            @pl.when(jnp.logical_and(i >= offs[t], i < offs[t] + counts[t]))
            def _(t=t):
                acc[...] += _tn(a_refs[t][...], u_ref[pl.ds(pl.multiple_of(k * tk, tk), tk), :])

        @pl.when(k == nk - 1)
        def _():
            @pl.when(i >= 2)
            def _():
                loc, rem = tile_copies(i - 2, par)
                loc.wait()
                rem.wait_send()
            stage[par] = acc[...]
            loc, rem = tile_copies(i, par)
            loc.start()
            rem.start()

        @pl.when(jnp.logical_and(i == ni - 1, k == nk - 1))
        def _():
            for t in (ni - 2, ni - 1):
                loc, rem = tile_copies(t, t % 2)
                loc.wait()
                rem.wait_send()
            pltpu.make_async_remote_copy(src_ref=dw_ref, dst_ref=got_ref, send_sem=send_sems.at[0], recv_sem=recv_sem,
                                         device_id=(x, y, c), device_id_type=MESH).wait_recv()

    def a_spec(t):
        def index(i, k):
            mine = jnp.logical_and(i >= offs[t], i < offs[t] + counts[t])
            return jnp.where(mine, k, 0), jnp.clip(i - offs[t], 0, counts[t] - 1)
        return pl.BlockSpec((tk, tile), index)

    anyspec = pl.BlockSpec(memory_space=pl.ANY)
    return pl.pallas_call(
        body, name="dw_in_swap", grid=(ni, nk),
        in_specs=[a_spec(t) for t in range(na)] + [pl.BlockSpec((s, tile), lambda i, k: (0, 0))],
        out_specs=[anyspec] * 2,
        out_shape=[SDS((ni * tile, tile), f32), SDS((ni * tile, tile), f32)],
        scratch_shapes=[pltpu.VMEM((tile, tile), f32), pltpu.VMEM((2, tile, tile), f32), pltpu.SemaphoreType.DMA((2,)),
                        pltpu.SemaphoreType.DMA((2,)), pltpu.SemaphoreType.DMA(())],
        compiler_params=pltpu.CompilerParams(dimension_semantics=("arbitrary", "arbitrary")),
    )(*a_parts, u)


def _chip_sum(mine, got, rows, name):
    r, cdim = mine.shape
    tc = LANES

    def body(m_ref, g_ref, s16_ref):
        c = lax.axis_index("c")
        for q in range(4):
            blk = pl.ds(rows * (2 * q + c), rows)
            s16_ref[q] = (m_ref[blk, :] + g_ref[blk, :]).astype(bf16)

    col = pl.BlockSpec((r, tc), lambda i: (0, i))
    return pl.pallas_call(
        body, name=name, grid=(cdim // tc,), in_specs=[col, col],
        out_specs=pl.BlockSpec((4, rows, tc), lambda i: (0, 0, i)), out_shape=SDS((4, rows, cdim), bf16),
        compiler_params=pltpu.CompilerParams(dimension_semantics=("parallel",)),
    )(mine, got)


def _assemble_wt(shards):
    nd, rows, cdim = shards.shape
    tc = 256

    def body(g_ref, o_ref):
        for j in range(nd):
            o_ref[pl.ds(rows * j, rows), :] = g_ref[j]
        o_ref[pl.ds(nd * rows, NP - nd * rows), :] = jnp.zeros((NP - nd * rows, tc), shards.dtype)

    return pl.pallas_call(
        body, name="assemble_w_in", grid=(cdim // tc,),
        in_specs=[pl.BlockSpec((nd, rows, tc), lambda i: (0, 0, i))],
        out_specs=pl.BlockSpec((NP, tc), lambda i: (0, i)), out_shape=SDS((NP, cdim), shards.dtype),
        compiler_params=pltpu.CompilerParams(dimension_semantics=("parallel",)),
    )(shards)


def _chip_exchange_copies(ins, outs, send_sems, recv_sems, local_sems):
    nb = len(ins)
    x, y, c = _my_pos()
    my_q = 2 * x + y
    mine = [pltpu.make_async_copy(ins[a].at[my_q], outs[a].at[my_q], local_sems.at[a]) for a in range(nb)]
    sends, recvs = [], []
    for k in range(1, 4):
        to, frm = (my_q + k) % 4, (my_q + 4 - k) % 4
        for a in range(nb):
            sems = dict(send_sem=send_sems.at[3 * a + k - 1], recv_sem=recv_sems.at[3 * a + k - 1], device_id_type=MESH)
            sends.append(pltpu.make_async_remote_copy(
                src_ref=ins[a].at[to], dst_ref=outs[a].at[my_q], device_id=(to // 2, to % 2, c), **sems))
            recvs.append(pltpu.make_async_remote_copy(
                src_ref=ins[a].at[frm], dst_ref=outs[a].at[frm], device_id=(x, y, c), **sems))
    return mine, sends, recvs


def _chip_exchange_scratch(nb):
    return [pltpu.SemaphoreType.DMA((3 * nb,)), pltpu.SemaphoreType.DMA((3 * nb,)), pltpu.SemaphoreType.DMA((nb,))]


def _prenorm_inproj(x, nw, wt, gather=()):
    s, d = x.shape
    npad = wt.shape[0]
    tm, tn = 1024, 1024
    ng = len(gather)
    ni, nj = s // tm, npad // tn

    def body(x_ref, nw_ref, w_ref, *refs):
        g_in, (proj_ref, u_ref), g_out, sems = refs[:ng], refs[ng:ng + 2], refs[ng + 2:2 * ng + 2], refs[2 * ng + 2:]
        i, j = pl.program_id(0), pl.program_id(1)
        if ng:
            phases = _gather_phases(g_in, g_out, *sems)
            for step, phase in enumerate(phases[:3]):
                @pl.when(jnp.logical_and(i == step, j == 0))
                def _(phase=phase):
                    phase()

        @pl.when(j == 0)
        def _():
            xv = x_ref[...]
            r = lax.rsqrt(jnp.mean(xv * xv, axis=-1, keepdims=True) + EPS)
            u_ref[...] = (xv * r * nw_ref[...]).astype(bf16)
        proj_ref[...] = _nt(u_ref[...], w_ref[pl.ds(pl.multiple_of(j * tn, tn), tn), :])

        if ng:
            @pl.when(jnp.logical_and(i == ni - 1, j == nj - 1))
            def _():
                phases[3]()

    anyspec = pl.BlockSpec(memory_space=pl.ANY)
    outs = pl.pallas_call(
        body, name="prenorm_inproj", grid=(ni, nj),
        in_specs=[pl.BlockSpec((tm, d), lambda i, j: (i, 0)), pl.BlockSpec((1, d), lambda i, j: (0, 0)),
                  pl.BlockSpec((npad, d), lambda i, j: (0, 0))] + [anyspec] * ng,
        out_specs=[pl.BlockSpec((tm, tn), lambda i, j: (i, j)), pl.BlockSpec((tm, d), lambda i, j: (i, 0))]
        + [anyspec] * ng,
        out_shape=[SDS((s, npad), f32), SDS((s, d), bf16)] + [SDS((N_DEV,) + a.shape, a.dtype) for a in gather],
        scratch_shapes=_gather_scratch(ng) if ng else [],
        compiler_params=pltpu.CompilerParams(dimension_semantics=("arbitrary", "arbitrary")),
    )(x, nw, wt, *gather)
    return outs[0], outs[1], outs[2:]


def _attn_consts():
    head0 = _iota((BLK, LANES), 1) < HEAD_DIM
    tri2 = (_iota((BLK, 2 * LANES), 1) % LANES) <= _iota((BLK, 2 * LANES), 0)
    ones2 = ((_iota((LANES, 2 * LANES), 0) < HEAD_DIM) == (_iota((LANES, 2 * LANES), 1) < LANES)).astype(bf16)
    rmat = ((_iota((2 * LANES, LANES), 0) < LANES) == (_iota((2 * LANES, LANES), 1) < HEAD_DIM)).astype(bf16)
    bones = ((_iota((LANES, LANES), 0) < HEAD_DIM) == (_iota((LANES, LANES), 1) < HEAD_DIM)).astype(bf16)
    return head0, tri2, ones2, rmat, bones


def _stack_heads(x16, head0):
    zero = jnp.zeros_like(x16)
    return jnp.concatenate([jnp.where(head0, x16, zero), jnp.where(head0, zero, x16)], axis=0)


def _bf16_terms(x, terms):
    out = []
    for _ in range(terms):
        t = x.astype(bf16)
        out.append(t)
        x = x - t.astype(f32)
    return out


def _dot_01(x, w16, terms):
    return _nn(jnp.concatenate(_bf16_terms(x, terms), axis=1), jnp.concatenate([w16] * terms, axis=0))


def _split_dot_sum(x, w16):
    hi, lo = _bf16_terms(x, 2)
    return _nn(hi, w16) + _nn(lo, w16)


def _dot_01_left(w16, x, terms):
    return _nn(jnp.concatenate([w16] * terms, axis=1), jnp.concatenate(_bf16_terms(x, terms), axis=0))


def _quarter_rows(i, q):
    return pl.ds(pl.multiple_of((i // 2) * 2048 + q * 512 + (i % 2) * 256, 256), 256)


def _token_rows(i, q):
    return pl.ds(i * 1024 + q, 256, stride=4)


def _quarter_block(i, d, nb):
    r, blk = i // nb, i % nb
    if d == 1:
        runs = [pl.ds(pl.multiple_of((blk // 16) * 2048 + q * 512 + (blk % 16) * 32, 32), 32) for q in range(4)]
    elif d == 4:
        runs = [pl.ds(pl.multiple_of((blk // 4) * 2048 + r * 512 + (blk % 4) * BLK, BLK), BLK)]
    else:
        runs = [pl.ds(blk * 2048 + (r % 4) * 512 + r // 4, BLK, stride=4)]
    return runs, blk > 0


def _quarter_mask():
    order = lambda n: 4 * (n % 32) + n // 32
    return order(_iota((BLK, 2 * LANES), 1) % LANES) <= order(_iota((BLK, 2 * LANES), 0))


def _load_runs(ref, runs):
    parts = [ref[run, :] for run in runs]
    return parts[0] if len(parts) == 1 else jnp.concatenate(parts, axis=0)


def _store_runs(ref, runs, val):
    n = BLK // len(runs)
    for t, run in enumerate(runs):
        ref[run, :] = val[t * n:(t + 1) * n]


def _add_runs(ref, runs, val):
    n = BLK // len(runs)
    for t, run in enumerate(runs):
        ref[run, :] += val[t * n:(t + 1) * n]


def _attn_fwd(proj):
    s = proj.shape[0]
    n_it = s // BLK

    def body(q_in, k_in, v_in, g_ref, o_ref, l_ref, mix_ref, q_ref, k_ref, v_ref, op0, op1, op2, lp0, lp1, lp2,
             s_a, s_b, sd_a, sd_b, p_a, p_b, m_a, m_b, pd_a, pd_b, k_a, k_b, v_a, v_b, stage):
        op_refs, lp_refs = (op0, op1, op2), (lp0, lp1, lp2)
        head0, tri2_t, ones2, rmat, _ = _attn_consts()
        tri2_q = _quarter_mask()

        def reorder(i, carry):
            for src, dst in ((q_in, q_ref), (k_in, k_ref), (v_in, v_ref)):
                for q in range(4):
                    dst[_quarter_rows(i, q), :] = src[_token_rows(i, q), :]
            return carry

        lax.fori_loop(0, s // 1024, reorder, 0)
        score_bufs, prob_bufs = ((s_a, sd_a), (s_b, sd_b)), ((p_a, m_a, pd_a), (p_b, m_b, pd_b))
        k_bufs, v_bufs = (k_a, k_b), (v_a, v_b)
        for buf in k_bufs + v_bufs:
            buf[...] = jnp.zeros_like(buf)

        def unstack(st16):
            return st16[:BLK] + st16[BLK:]

        def scores(i, par, d, nb):
            rows, has_prev = _quarter_block(i, d, nb)
            tri2 = tri2_q if d == 1 else tri2_t
            s_buf, sd_buf = score_bufs[par]
            qs = _load_runs(q_ref, rows) * 0.125
            qs16 = qs.astype(bf16)
            kst_c = _stack_heads(_load_runs(k_ref, rows).astype(bf16), head0)
            kst_p = k_bufs[1 - par][...]
            k_bufs[par][...] = kst_c
            sc = _nt(qs16, kst_c)
            sp = _nt(qs16, kst_p)
            s_buf[...] = jnp.where(tri2, sc, jnp.where(has_prev, sp, -jnp.inf))
            sd = _nn((qs * unstack(kst_p).astype(f32)).astype(bf16), ones2)
            sd_buf[...] = jnp.where(has_prev, sd, -jnp.inf)

        def softmax(bufs_in, bufs_out):
            s_buf, sd_buf = bufs_in
            p_buf, m_buf, pd_buf = bufs_out
            sc, sd2 = s_buf[...], sd_buf[...]
            m0 = jnp.max(sc[:, :LANES], axis=1, keepdims=True)
            m1 = jnp.max(sc[:, LANES:], axis=1, keepdims=True)
            m2 = jnp.concatenate([jnp.broadcast_to(m0, (BLK, LANES)), jnp.broadcast_to(m1, (BLK, LANES))], axis=1)
            m2 = jnp.maximum(m2, sd2)
            p_buf[...] = jnp.exp(sc - m2).astype(bf16)
            m_pair = jnp.where(head0, m2[:, :LANES], m2[:, LANES:])
            m_buf[...] = m_pair
            pd_buf[...] = jnp.exp(jnp.where(head0, sd2[:, :LANES], sd2[:, LANES:]) - m_pair)

        def output(i, par, d, nb, p):
            rows, _ = _quarter_block(i, d, nb)
            tri2 = tri2_q if d == 1 else tri2_t
            p_buf, m_buf, pd_buf = prob_bufs[par]
            vst_c = _stack_heads(_load_runs(v_ref, rows).astype(bf16), head0)
            vst_p = v_bufs[1 - par][...]
            v_bufs[par][...] = vst_c
            pt16, pd = p_buf[...], pd_buf[...]
            zero = jnp.zeros_like(pt16)
            o = (_nn(jnp.where(tri2, pt16, zero), vst_c) + _nn(jnp.where(tri2, zero, pt16), vst_p)
                 + pd * unstack(vst_p).astype(f32))
            l = _nn(pt16, rmat) + pd
            _store_runs(op_refs[p], rows, o / l)
            _store_runs(lp_refs[p], rows, m_buf[...] + jnp.log(l))

        for p, d in enumerate(DILATIONS):
            nb = s // (BLK * d)
            scores(0, 0, d, nb)
            scores(1, 1, d, nb)
            softmax(score_bufs[0], prob_bufs[0])

            def steps(j, carry, d=d, nb=nb, p=p):
                for par in range(2):
                    t = 2 * j + 2 + par
                    scores(t, par, d, nb)
                    output(t - 2, par, d, nb, p)
                    softmax(score_bufs[1 - par], prob_bufs[1 - par])
                return carry

            lax.fori_loop(0, (n_it - 2) // 2, steps, 0, unroll=True)
            output(n_it - 2, 0, d, nb, p)
            softmax(score_bufs[1], prob_bufs[1])
            output(n_it - 1, 1, d, nb, p)

        def merge(i, carry):
            for q in range(4):
                rows, tokens = _quarter_rows(i, q), _token_rows(i, q)
                l0, l1, l2 = lp0[rows, :], lp1[rows, :], lp2[rows, :]
                m = jnp.maximum(jnp.maximum(l0, l1), l2)
                e0, e1, e2 = jnp.exp(l0 - m), jnp.exp(l1 - m), jnp.exp(l2 - m)
                z = e0 + e1 + e2
                o = (e0 * op0[rows, :] + e1 * op1[rows, :] + e2 * op2[rows, :]) / z
                o_ref[tokens, :] = o
                l_ref[rows, :] = m + jnp.log(z)
                g = g_ref[tokens, :]
                stage[pl.ds(q, 256, stride=4), :] = o * (g * _sigmoid(g))
            mix_ref[pl.ds(pl.multiple_of(i * 1024, 1024), 1024), :] = stage[...].astype(bf16)
            return carry

        lax.fori_loop(0, s // 1024, merge, 0)

    col = lambda base: pl.BlockSpec((s, LANES), lambda h: (0, base + h))
    return pl.pallas_call(
        body, name="attn_fwd", grid=(N_PAIRS,),
        in_specs=[col(0), col(8), col(16), col(24)],
        out_specs=[col(0)] * 6,
        out_shape=[SDS((s, D_ATTN), f32), SDS((s, D_ATTN), f32), SDS((s, D_ATTN), bf16)] + [SDS((s, D_ATTN), f32)] * 3,
        scratch_shapes=[pltpu.VMEM((s, LANES), f32)] * 6 + [pltpu.VMEM((BLK, 2 * LANES), f32)] * 4
        + [pltpu.VMEM((BLK, 2 * LANES), bf16)] * 2 + [pltpu.VMEM((BLK, LANES), f32)] * 4
        + [pltpu.VMEM((2 * BLK, LANES), bf16)] * 4 + [pltpu.VMEM((1024, LANES), f32)],
        compiler_params=pltpu.CompilerParams(dimension_semantics=("parallel",)),
    )(proj, proj, proj, proj)


def _expand_mat():
    colv = _iota((LANES, 2 * D_SSM), 1)
    head = 2 * ((colv % D_SSM) // LANES) + colv // D_SSM
    return (_iota((LANES, 2 * D_SSM), 0) == head).astype(bf16)


def _fold_mat():
    return (_iota((D_SSM, LANES), 0) // HEAD_DIM == _iota((D_SSM, LANES), 1)).astype(bf16)


def _conv(xs_ref, bc_ref, xs_tail, bc_tail, cw_ref, cb_ref, xpad, first):
    keep = jnp.where(first, 0.0, 1.0)
    xpad[0:8, 0:D_SSM] = xs_tail[...] * keep
    xpad[0:8, D_SSM:D_CONV] = bc_tail[...] * keep
    xpad[8:8 + CHUNK, 0:D_SSM] = xs_ref[...]
    xpad[8:8 + CHUNK, D_SSM:D_CONV] = bc_ref[...]
    xp = xpad[...]
    cv = cb_ref[...] + cw_ref[3:4, :] * xp[8:8 + CHUNK]
    for j in range(3):
        cv = cv + cw_ref[j:j + 1, :] * pltpu.roll(xp, 3 - j, 0)[8:8 + CHUNK]
    return cv


def _decay_terms(dt_ref, dtb_ref, alog16_ref, emat_ref):
    pre = dt_ref[...] + dtb_ref[...]
    dt16 = _softplus(pre)
    a16 = -jnp.exp(alog16_ref[...])
    sub, lane = _iota((CHUNK, CHUNK), 0), _iota((CHUNK, CHUNK), 1)
    tri = (sub >= lane).astype(f32)
    al16 = _nn_hi(tri, dt16 * a16)
    al_t = al16.T
    emat = emat_ref[...]
    dt_x = _dot_01(dt16, emat, 3)
    al_x = _dot_01(al16, emat, 3)
    lane_w = _iota((CHUNK, D_SSM), 1)
    even = (lane_w % LANES) < HEAD_DIM
    dt_f = jnp.where(even, dt_x[:, :D_SSM], dt_x[:, D_SSM:])
    al_f = jnp.where(even, al_x[:, :D_SSM], al_x[:, D_SSM:])
    return pre, dt_f, al_f, al_x, al_t


def _decay_mat(al_x, al_t, pair, h):
    sub, lane = _iota((CHUNK, CHUNK), 0), _iota((CHUNK, CHUNK), 1)
    col = al_x[:, h * D_SSM + pair * LANES: h * D_SSM + (pair + 1) * LANES]
    row = al_t[2 * pair + h: 2 * pair + h + 1, :]
    return jnp.exp(jnp.where(sub >= lane, col - row, -jnp.inf))


def _ssd_in_specs(order):
    blk = lambda w, cb: pl.BlockSpec((CHUNK, w), lambda i: (order(i), cb))
    tail = lambda w, cb: pl.BlockSpec((8, w), lambda i: (jnp.maximum(16 * order(i) - 1, 0), cb))
    return [blk(D_SSM, COL_XS // D_SSM), blk(512, COL_BC // 512), tail(D_SSM, COL_XS // D_SSM),
            tail(512, COL_BC // 512), blk(LANES, COL_DT // LANES), blk(D_SSM, COL_Z // D_SSM)]


def _full(shape):
    return pl.BlockSpec(shape, lambda i: (0,) * len(shape))


def _ssd_fwd(proj, conv_w, conv_b, dtb16, alog16, alog_f, d_f, nw):
    s = proj.shape[0]
    nc = s // CHUNK

    def body(xs_ref, bc_ref, xs_tail, bc_tail, dt_ref, z_ref, cw_ref, cb_ref, dtb_ref, alog16_ref, alogf_ref,
             df_ref, nw_ref, mix_ref, y_ref, st_ref, cv_ref, h_scr, xpad, y_scr, emat_ref):
        c = pl.program_id(0)

        @pl.when(c == 0)
        def _():
            h_scr[...] = jnp.zeros_like(h_scr)
            emat_ref[...] = _expand_mat()

        cv = _conv(xs_ref, bc_ref, xs_tail, bc_tail, cw_ref, cb_ref, xpad, c == 0)
        cv_ref[...] = cv
        xbc = cv * _sigmoid(cv)
        _, dt_f, al_f, al_x, al_t = _decay_terms(dt_ref, dtb_ref, alog16_ref, emat_ref)
        head0 = _iota((CHUNK, LANES), 1) < HEAD_DIM
        st_ref[...] = h_scr[...]
        for g in range(N_GROUPS):
            bm = xbc[:, D_SSM + g * D_STATE: D_SSM + (g + 1) * D_STATE].astype(bf16)
            cm = xbc[:, D_SSM + (N_GROUPS + g) * D_STATE: D_SSM + (N_GROUPS + g + 1) * D_STATE].astype(bf16)
            gmat = _nt(cm, bm)
            for pair in range(4 * g, 4 * g + 4):
                sl = slice(pair * LANES, (pair + 1) * LANES)
                xp, dtp, alp = xbc[:, sl], dt_f[:, sl], al_f[:, sl]
                xdt = xp * dtp
                xdt16 = xdt.astype(bf16)
                al_last = alp[CHUNK - 1:CHUNK, :]
                hp = h_scr[:, sl]
                y_off = jnp.exp(alp) * _nn(cm, hp.astype(bf16))
                yd = [_nn((gmat * _decay_mat(al_x, al_t, pair, h)).astype(bf16), xdt16) for h in range(2)]
                y_scr[:, sl] = jnp.where(head0, yd[0], yd[1]) + y_off + df_ref[:, sl] * xp
                st = _tn(bm, (jnp.exp(al_last - alp) * xdt).astype(bf16))
                h_scr[:, sl] = jnp.exp(al_last) * hp + st
        y = y_scr[...]
        y_ref[...] = y
        z = z_ref[...]
        yz = y * (z * _sigmoid(z))
        gw = D_SSM // N_GROUPS
        for g in range(N_GROUPS):
            part = yz[:, g * gw:(g + 1) * gw]
            r = lax.rsqrt(jnp.mean(part * part, axis=-1, keepdims=True) + EPS)
            mix_ref[:, g * gw:(g + 1) * gw] = (part * r * nw_ref[:, g * gw:(g + 1) * gw]).astype(bf16)

    order = lambda i: i
    row = lambda w: pl.BlockSpec((CHUNK, w), lambda i: (i, 0))
    return pl.pallas_call(
        body, name="ssd_fwd", grid=(nc,),
        in_specs=_ssd_in_specs(order) + [_full((4, D_CONV)), _full((1, D_CONV)), _full((1, LANES)), _full((1, LANES)),
                                         _full((1, D_SSM)), _full((1, D_SSM)), _full((1, D_SSM))],
        out_specs=[row(D_SSM), row(D_SSM), pl.BlockSpec((None, D_STATE, D_SSM), lambda i: (i, 0, 0)), row(D_CONV)],
        out_shape=[SDS((s, D_SSM), bf16), SDS((s, D_SSM), f32), SDS((nc, D_STATE, D_SSM), f32),
                   SDS((s, D_CONV), f32)],
        scratch_shapes=[pltpu.VMEM((D_STATE, D_SSM), f32), pltpu.VMEM((8 + CHUNK, D_CONV), f32),
                        pltpu.VMEM((CHUNK, D_SSM), f32), pltpu.VMEM((LANES, 2 * D_SSM), bf16)],
        compiler_params=pltpu.CompilerParams(dimension_semantics=("arbitrary",)),
    )(proj, proj, proj, proj, proj, proj, conv_w, conv_b, dtb16, alog16, alog_f, d_f, nw)


def _outproj_loss(mix_a, mix_s, wo, x, tgt, npw):
    s, d = x.shape
    tm = 512

    def body(ma_ref, ms_ref, wo_ref, x_ref, t_ref, npw_ref, dmix_ref, dres_ref, acc_ref, dwo_ref):
        @pl.when(pl.program_id(0) == 0)
        def _():
            acc_ref[...] = jnp.zeros_like(acc_ref)
            dwo_ref[...] = jnp.zeros_like(dwo_ref)

        out = _nn(ma_ref[...], wo_ref[0:D_ATTN, :]) + _nn(ms_ref[...], wo_ref[D_ATTN:, :])
        r = lax.rsqrt(jnp.mean(out * out, axis=-1, keepdims=True) + EPS)
        on = out * r
        diff = x_ref[...] + on * npw_ref[...] - t_ref[...]
        dres = diff * (1.0 / d)
        dres_ref[...] = dres
        acc_ref[0:1, :] += jnp.sum(diff * diff, axis=0, keepdims=True)
        acc_ref[1:2, :] += jnp.sum(dres * on, axis=0, keepdims=True)
        dn = dres * npw_ref[...]
        dout = (r * (dn - on * jnp.mean(dn * on, axis=-1, keepdims=True))).astype(bf16)
        dmix_ref[...] = _nt(dout, wo_ref[...])
        dwo_ref[0:D_ATTN, :] += _tn(ma_ref[...], dout)
        dwo_ref[D_ATTN:, :] += _tn(ms_ref[...], dout)

    row = lambda w: pl.BlockSpec((tm, w), lambda i: (i, 0))
    return pl.pallas_call(
        body, name="outproj_loss", grid=(s // tm,),
        in_specs=[row(D_ATTN), row(D_SSM), _full((D_ATTN + D_SSM, d)), row(d), row(d), _full((1, d))],
        out_specs=[row(D_ATTN + D_SSM), row(d), _full((8, d)), _full((D_ATTN + D_SSM, d))],
        out_shape=[SDS((s, D_ATTN + D_SSM), f32), SDS((s, d), f32), SDS((8, d), f32), SDS((D_ATTN + D_SSM, d), f32)],
        compiler_params=pltpu.CompilerParams(dimension_semantics=("arbitrary",)),
    )(mix_a, mix_s, wo, x, tgt, npw)


def _attn_bwd(proj, qkv, o, lb, dmix, swap=None):
    s = proj.shape[0]
    n_it = s // BLK

    nsw = 0 if swap is None else 1

    def body(*refs):
        q_ref, k_ref, v_ref, g_ref, o_ref, l_ref, dm_ref = refs[:7]
        swap_in = refs[7:7 + nsw]
        dq_ref, dk_ref, dv_ref, dg_ref = refs[7 + nsw:11 + nsw]
        swap_out = refs[11 + nsw:11 + 2 * nsw]
        dq_acc, dk_acc, dv_acc, do_scr, dl_scr = refs[11 + 2 * nsw:16 + 2 * nsw]
        bufs = refs[16 + 2 * nsw:44 + 2 * nsw]
        stage_a, stage_b = refs[44 + 2 * nsw:46 + 2 * nsw]
        swap_sems = refs[46 + 2 * nsw:]
        head0, tri2_t, _, _, bones = _attn_consts()
        tri2_q = _quarter_mask()

        if nsw:
            x, y, c = _my_pos()
            swap_copy = pltpu.make_async_remote_copy(
                src_ref=swap_in[0], dst_ref=swap_out[0], send_sem=swap_sems[0], recv_sem=swap_sems[1],
                device_id=(x, y, 1 - c), device_id_type=MESH)

            @pl.when(pl.program_id(0) == 0)
            def _():
                swap_copy.start()

        quarter_rows, load, add = _quarter_rows, _load_runs, _add_runs

        def pro(i, carry):
            for t in range(4):
                rows = pl.ds(pl.multiple_of(i * 1024 + t * 256, 256), 256)
                g = g_ref[rows, :]
                sg = _sigmoid(g)
                dmx = dm_ref[rows, :]
                ov = o_ref[rows, :]
                dg_ref[rows, :] = (dmx * ov * (sg * (1.0 + g * (1.0 - sg)))).astype(bf16)
                do = dmx * (g * sg)
                stage_a[t * 256:(t + 1) * 256, :] = do
                stage_b[t * 256:(t + 1) * 256, :] = _split_dot_sum(do * ov, bones)
            z = jnp.zeros((256, LANES), f32)
            for q in range(4):
                rows = quarter_rows(i, q)
                do_scr[rows, :] = stage_a[pl.ds(q, 256, stride=4), :]
                dl_scr[rows, :] = stage_b[pl.ds(q, 256, stride=4), :]
                dq_acc[rows, :] = z
                dk_acc[rows, :] = z
                dv_acc[rows, :] = z
            return carry

        lax.fori_loop(0, s // 1024, pro, 0)

        def per_head(t):
            return jnp.concatenate([t[:, :LANES], t[:, LANES:]], axis=0)

        def both_heads(t):
            tr = pltpu.roll(t, HEAD_DIM, 1)
            return jnp.concatenate([jnp.where(head0, t, tr), jnp.where(head0, tr, t)], axis=1)

        mm_bufs = ((bufs[0], bufs[1], bufs[2], bufs[3]), (bufs[4], bufs[5], bufs[6], bufs[7]))
        ds_bufs = ((bufs[8], bufs[9], bufs[10], bufs[11]), (bufs[12], bufs[13], bufs[14], bufs[15]))
        op_bufs = ((bufs[16], bufs[17], bufs[18], bufs[19]), (bufs[20], bufs[21], bufs[22], bufs[23]))
        vc_bufs, carry_k, carry_v = (bufs[24], bufs[25]), bufs[26], bufs[27]
        for buf in (op_bufs[0][0], op_bufs[1][0]) + vc_bufs:
            buf[...] = jnp.zeros_like(buf)

        def block_rows(i, d, nb):
            rows, has_prev = _quarter_block(i, d, nb)
            return rows, rows, has_prev

        def unstack(st16):
            return st16[:BLK] + st16[BLK:]

        def products(i, par, d, nb):
            src, scr, has_prev = block_rows(i, d, nb)
            tri2 = tri2_q if d == 1 else tri2_t
            s_buf, dp_buf, sd_buf, dpd_buf = mm_bufs[par]
            kc_buf, kp_buf, q_buf, do_buf = op_bufs[par]
            q = load(q_ref, src)
            qs = q * 0.125
            do = load(do_scr, scr)
            qs16, do16 = qs.astype(bf16), do.astype(bf16)
            kst_c = _stack_heads(load(k_ref, src).astype(bf16), head0)
            vst_c = _stack_heads(load(v_ref, src).astype(bf16), head0)
            kst_p, vst_p = op_bufs[1 - par][0][...], vc_bufs[1 - par][...]
            kc_buf[...] = kst_c
            kp_buf[...] = kst_p
            vc_bufs[par][...] = vst_c
            q_buf[...] = q.astype(bf16)
            do_buf[...] = do16
            s_buf[...] = jnp.where(tri2, _nt(qs16, kst_c), jnp.where(has_prev, _nt(qs16, kst_p), -jnp.inf))
            dp_buf[...] = jnp.where(tri2, _nt(do16, vst_c), jnp.where(has_prev, _nt(do16, vst_p), 0.0))
            sd_buf[...] = _nn((qs * unstack(kst_p).astype(f32)).astype(bf16), bones)
            dpd_buf[...] = jnp.where(has_prev, _nn((do * unstack(vst_p).astype(f32)).astype(bf16), bones), 0.0)

        def softmax_grad(i, par, d, nb):
            src, scr, has_prev = block_rows(i, d, nb)
            s_buf, dp_buf, sd_buf, dpd_buf = mm_bufs[par]
            p_buf, ds_buf, pd_buf, dsd_buf = ds_bufs[par]
            lse = load(l_ref, src)
            dl = load(dl_scr, scr)
            pt = jnp.exp(s_buf[...] - both_heads(lse))
            ds_buf[...] = (pt * (dp_buf[...] - both_heads(dl)) * 0.125).astype(bf16)
            p_buf[...] = pt.astype(bf16)
            pd = jnp.where(has_prev, jnp.exp(sd_buf[...] - lse), 0.0)
            pd_buf[...] = pd
            dsd_buf[...] = pd * (dpd_buf[...] - dl) * 0.125

        def accumulate(i, par, d, nb):
            _, rows, _ = block_rows(i, d, nb)
            _, before, _ = block_rows(jnp.maximum(i - 1, 0), d, nb)
            tri2 = tri2_q if d == 1 else tri2_t
            p_buf, ds_buf, pd_buf, dsd_buf = ds_bufs[par]
            kc_buf, kp_buf, q_buf, do_buf = op_bufs[par]
            pt16, ds16, pd, dsd = p_buf[...], ds_buf[...], pd_buf[...], dsd_buf[...]
            zero = jnp.zeros_like(pt16)
            dsc, dsp = jnp.where(tri2, ds16, zero), jnp.where(tri2, zero, ds16)
            pc, pp = jnp.where(tri2, pt16, zero), jnp.where(tri2, zero, pt16)
            kst_c, kst_p, q16, do16 = kc_buf[...], kp_buf[...], q_buf[...], do_buf[...]
            qst, dost = _stack_heads(q16, head0), _stack_heads(do16, head0)
            add(dq_acc, rows, _nn(dsc, kst_c) + _nn(dsp, kst_p) + dsd * unstack(kst_p).astype(f32))
            dk2 = _tn(jnp.concatenate([per_head(dsc), per_head(dsp)], axis=1), qst)
            dv2 = _tn(jnp.concatenate([per_head(pc), per_head(pp)], axis=1), dost)
            add(dk_acc, before, carry_k[...] + dk2[BLK:] + dsd * q16.astype(f32))
            add(dv_acc, before, carry_v[...] + dv2[BLK:] + pd * do16.astype(f32))
            carry_k[...] = dk2[:BLK]
            carry_v[...] = dv2[:BLK]

        for d in DILATIONS:
            nb = s // (BLK * d)
            carry_k[...] = jnp.zeros_like(carry_k)
            carry_v[...] = jnp.zeros_like(carry_v)
            products(0, 0, d, nb)
            products(1, 1, d, nb)
            softmax_grad(0, 0, d, nb)

            def steps(j, carry, d=d, nb=nb):
                for par in range(2):
                    t = 2 * j + 2 + par
                    accumulate(t - 2, par, d, nb)
                    products(t, par, d, nb)
                    softmax_grad(t - 1, 1 - par, d, nb)
                return carry

            lax.fori_loop(0, (n_it - 2) // 2, steps, 0, unroll=True)
            accumulate(n_it - 2, 0, d, nb)
            softmax_grad(n_it - 1, 1, d, nb)
            accumulate(n_it - 1, 1, d, nb)
            _, last, _ = block_rows(n_it - 1, d, nb)
            add(dk_acc, last, carry_k[...])
            add(dv_acc, last, carry_v[...])

        def epi(i, carry):
            rows = pl.ds(pl.multiple_of(i * 1024, 1024), 1024)
            for acc, out, stage in ((dq_acc, dq_ref, stage_a), (dk_acc, dk_ref, stage_b), (dv_acc, dv_ref, stage_a)):
                for q in range(4):
                    stage[pl.ds(q, 256, stride=4), :] = acc[quarter_rows(i, q), :]
                out[rows, :] = stage[...].astype(bf16)
            return carry

        lax.fori_loop(0, s // 1024, epi, 0)

        if nsw:
            @pl.when(pl.program_id(0) == N_PAIRS - 1)
            def _():
                swap_copy.wait_send()
                swap_copy.wait_recv()

    col = lambda base: pl.BlockSpec((s, LANES), lambda h: (0, base + h))
    anyspec = pl.BlockSpec(memory_space=pl.ANY)
    swaps = [] if swap is None else [swap]
    outs = pl.pallas_call(
        body, name="attn_bwd", grid=(N_PAIRS,),
        in_specs=[col(0), col(0), col(0), col(24), col(0), col(0), col(0)] + [anyspec] * nsw,
        out_specs=[col(0)] * 4 + [anyspec] * nsw,
        out_shape=[SDS((s, D_ATTN), bf16)] * 4 + [SDS(a.shape, a.dtype) for a in swaps],
        scratch_shapes=[pltpu.VMEM((s, LANES), f32)] * 5
        + [pltpu.VMEM((BLK, 2 * LANES), f32)] * 2 + [pltpu.VMEM((BLK, LANES), f32)] * 2
        + [pltpu.VMEM((BLK, 2 * LANES), f32)] * 2 + [pltpu.VMEM((BLK, LANES), f32)] * 2
        + [pltpu.VMEM((BLK, 2 * LANES), bf16)] * 2 + [pltpu.VMEM((BLK, LANES), f32)] * 2
        + [pltpu.VMEM((BLK, 2 * LANES), bf16)] * 2 + [pltpu.VMEM((BLK, LANES), f32)] * 2
        + [pltpu.VMEM((2 * BLK, LANES), bf16)] * 2 + [pltpu.VMEM((BLK, LANES), bf16)] * 2
        + [pltpu.VMEM((2 * BLK, LANES), bf16)] * 2 + [pltpu.VMEM((BLK, LANES), bf16)] * 2
        + [pltpu.VMEM((2 * BLK, LANES), bf16)] * 2 + [pltpu.VMEM((BLK, LANES), f32)] * 2
        + [pltpu.VMEM((1024, LANES), f32)] * 2
        + [pltpu.SemaphoreType.DMA(())] * (2 * nsw),
        compiler_params=pltpu.CompilerParams(dimension_semantics=("arbitrary",)),
    )(*qkv, proj, o, lb, dmix, *swaps)
    return outs


def _ssd_bwd(proj, y, states, cv, dmix, conv_w, conv_b, dtb16, alog16, alog_f, d_f, nw, chip_sums=()):
    s = proj.shape[0]
    nc = s // CHUNK
    gw = D_SSM // N_GROUPS
    nx = len(chip_sums)

    def body(*refs):
        (xs_ref, bc_ref, _, _, dt_ref, z_ref, y_ref, st_ref, dm_ref, cw_ref, cb_ref, dtb_ref,
         alog16_ref, alogf_ref, df_ref, nw_ref, cv_ref) = refs[:17]
        cs_in = refs[17:17 + nx]
        out_ref, gconv_ref, gvec_ref, gdt_ref = refs[17 + nx:21 + nx]
        cs_out = refs[21 + nx:21 + 2 * nx]
        (dh_scr, head_scr, dcpad, da_scr, dxdt_scr, dbc_scr, emat_ref, fold_ref) = refs[21 + 2 * nx:29 + 2 * nx]
        cs_sems = refs[29 + 2 * nx:]
        i = pl.program_id(0)
        c = nc - 1 - i

        if nx:
            @pl.when(i == 0)
            def _():
                mine, sends, _ = _chip_exchange_copies(cs_in, cs_out, *cs_sems)
                for cp in mine + sends:
                    cp.start()

            @pl.when(i == nc - 1)
            def _():
                mine, sends, recvs = _chip_exchange_copies(cs_in, cs_out, *cs_sems)
                for cp in recvs:
                    cp.wait_recv()
                for cp in sends:
                    cp.wait_send()
                for cp in mine:
                    cp.wait()

        @pl.when(i == 0)
        def _():
            emat_ref[...] = _expand_mat()
            fold_ref[...] = _fold_mat()
            dh_scr[...] = jnp.zeros_like(dh_scr)
            head_scr[...] = jnp.zeros_like(head_scr)
            gconv_ref[...] = jnp.zeros_like(gconv_ref)
            gvec_ref[...] = jnp.zeros_like(gvec_ref)
            gdt_ref[...] = jnp.zeros_like(gdt_ref)

        cv = cv_ref[...]
        sig = _sigmoid(cv)
        xbc = cv * sig
        pre, dt_f, al_f, al_x, al_t = _decay_terms(dt_ref, dtb_ref, alog16_ref, emat_ref)
        head0 = _iota((CHUNK, LANES), 1) < HEAD_DIM
        sub = _iota((CHUNK, LANES), 0)
        last_row = sub == CHUNK - 1

        yv, z, dmx = y_ref[...], z_ref[...], dm_ref[...]
        sz = _sigmoid(z)
        silu = z * sz
        yz = yv * silu
        dyz_parts = []
        for g in range(N_GROUPS):
            gs = slice(g * gw, (g + 1) * gw)
            part = yz[:, gs]
            r = lax.rsqrt(jnp.mean(part * part, axis=-1, keepdims=True) + EPS)
            nh = part * r
            gvec_ref[0:1, gs] += jnp.sum(dmx[:, gs] * nh, axis=0, keepdims=True)
            dn = dmx[:, gs] * nw_ref[:, gs]
            dyz_parts.append(r * (dn - nh * jnp.mean(dn * nh, axis=-1, keepdims=True)))
        dyz = jnp.concatenate(dyz_parts, axis=1)
        dy = dyz * silu
        out_ref[:, 0:D_SSM] = (dyz * yv * (sz * (1.0 + z * (1.0 - sz)))).astype(bf16)

        x_all = xbc[:, 0:D_SSM]
        gvec_ref[2:3, :] += jnp.sum(dy * x_all, axis=0, keepdims=True)

        for g in range(N_GROUPS):
            bm = xbc[:, D_SSM + g * D_STATE: D_SSM + (g + 1) * D_STATE].astype(bf16)
            cm = xbc[:, D_SSM + (N_GROUPS + g) * D_STATE: D_SSM + (N_GROUPS + g + 1) * D_STATE].astype(bf16)
            gmat = _nt(cm, bm)
            dgm = jnp.zeros((CHUNK, CHUNK), f32)
            db = jnp.zeros((CHUNK, D_STATE), f32)
            dc = jnp.zeros((CHUNK, D_STATE), f32)
            for pair in range(4 * g, 4 * g + 4):
                sl = slice(pair * LANES, (pair + 1) * LANES)
                xp, dtp, alp, dyp = x_all[:, sl], dt_f[:, sl], al_f[:, sl], dy[:, sl]
                xdt = xp * dtp
                xdt16 = xdt.astype(bf16)
                al_last = alp[CHUNK - 1:CHUNK, :]
                e_l = jnp.exp(alp)
                wf = jnp.exp(al_last - alp)
                e_last = jnp.exp(al_last)
                hp = st_ref[:, sl]
                hp16 = hp.astype(bf16)
                dhn = dh_scr[:, sl]
                dhn16 = dhn.astype(bf16)
                y_off = e_l * _nn(cm, hp16)
                dch16 = (dyp * e_l).astype(bf16)
                dc = dc + _nt(dch16, hp16)
                dh_out = _tn(cm, dch16)
                dal = dyp * y_off
                xw16 = (wf * xdt).astype(bf16)
                db = db + _nt(xw16, dhn16)
                dxw = _nn(bm, dhn16)
                dxdt = dxw * wf
                dwf = dxw * xdt * wf
                dal = dal - dwf
                dal_last = jnp.sum(dwf, axis=0, keepdims=True) + jnp.sum(dhn * hp, axis=0, keepdims=True) * e_last
                dh_scr[:, sl] = e_last * dhn + dh_out
                for h in range(2):
                    mh = head0 if h == 0 else jnp.logical_not(head0)
                    dyh16 = jnp.where(mh, dyp, 0.0).astype(bf16)
                    lmat = _decay_mat(al_x, al_t, pair, h)
                    mm = gmat * lmat
                    dmm = _nt(dyh16, xdt16)
                    dxdt = dxdt + _tn(mm.astype(bf16), dyh16)
                    n16 = (dmm * mm).astype(bf16)
                    jh = jnp.where(mh, 1.0 / HEAD_DIM, 0.0).astype(bf16)
                    dal = dal + _nn(n16, jh) - _tn(n16, jh)
                    dgm = dgm + dmm * lmat
                da_scr[:, sl] = dal + jnp.where(last_row, dal_last, 0.0)
                dxdt_scr[:, sl] = dxdt
            dgm16 = dgm.astype(bf16)
            dbc_scr[:, g * D_STATE:(g + 1) * D_STATE] = db + _tn(dgm16, cm)
            dbc_scr[:, (N_GROUPS + g) * D_STATE:(N_GROUPS + g + 1) * D_STATE] = dc + _nn(dgm16, bm)

        sub_c, lane_c = _iota((CHUNK, CHUNK), 0), _iota((CHUNK, CHUNK), 1)
        tri_t = (lane_c >= sub_c).astype(bf16)
        dadt = _dot_01_left(tri_t, da_scr[...], 2)
        a_f = -jnp.exp(alogf_ref[...])
        dxdt_all = dxdt_scr[...]
        ddt_f = dxdt_all * x_all + a_f * dadt
        gvec_ref[1:2, :] += jnp.sum(dt_f * dadt, axis=0, keepdims=True) * a_f
        dx = df_ref[...] * dy + dxdt_all * dt_f
        ddt_raw = _dot_01(ddt_f, fold_ref[...], 2) * _sigmoid(pre)
        gdt_ref[0:1, :] += jnp.sum(ddt_raw, axis=0, keepdims=True)
        out_ref[:, D_SSM + D_CONV:D_SSM + D_CONV + LANES] = ddt_raw.astype(bf16)
        out_ref[:, D_SSM + D_CONV + LANES:] = jnp.zeros((CHUNK, 3 * LANES), bf16)

        dsil = sig * (1.0 + cv * (1.0 - sig))
        dcv_x = dx * dsil[:, 0:D_SSM]
        dcv_bc = dbc_scr[...] * dsil[:, D_SSM:]
        dcpad[0:CHUNK, 0:D_SSM] = dcv_x
        dcpad[0:CHUNK, D_SSM:] = dcv_bc
        dcpad[CHUNK:, :] = head_scr[...]
        dcp = dcpad[...]
        dcv = dcp[0:CHUNK]
        gconv_ref[4:5, :] += jnp.sum(dcv, axis=0, keepdims=True)
        x_raw = jnp.concatenate([xs_ref[...], bc_ref[...]], axis=1)
        draw = cw_ref[3:4, :] * dcv
        gconv_ref[3:4, :] += jnp.sum(dcv * x_raw, axis=0, keepdims=True)
        for j in range(3):
            ahead = pltpu.roll(dcp, CHUNK + 8 - (3 - j), 0)[0:CHUNK]
            draw = draw + cw_ref[j:j + 1, :] * ahead
            gconv_ref[j:j + 1, :] += jnp.sum(ahead * x_raw, axis=0, keepdims=True)
        head_scr[...] = dcv[0:8]
        out_ref[:, D_SSM:D_SSM + D_CONV] = draw.astype(bf16)

    order = lambda i: nc - 1 - i
    row = lambda w, cb=0: pl.BlockSpec((CHUNK, w), lambda i: (nc - 1 - i, cb))
    anyspec = pl.BlockSpec(memory_space=pl.ANY)
    outs = pl.pallas_call(
        body, name="ssd_bwd", grid=(nc,),
        in_specs=_ssd_in_specs(order) + [row(D_SSM), pl.BlockSpec((None, D_STATE, D_SSM), lambda i: (nc - 1 - i, 0, 0)),
                                         row(D_SSM, 1), _full((4, D_CONV)), _full((1, D_CONV)), _full((1, LANES)),
                                         _full((1, LANES)), _full((1, D_SSM)), _full((1, D_SSM)), _full((1, D_SSM)),
                                         row(D_CONV)]
        + [anyspec] * nx,
        out_specs=[row(3072), _full((8, D_CONV)), _full((8, D_SSM)), _full((8, LANES))] + [anyspec] * nx,
        out_shape=[SDS((s, 3072), bf16), SDS((8, D_CONV), f32), SDS((8, D_SSM), f32), SDS((8, LANES), f32)]
        + [SDS(a.shape, a.dtype) for a in chip_sums],
        scratch_shapes=[pltpu.VMEM((D_STATE, D_SSM), f32), pltpu.VMEM((8, D_CONV), f32),
                        pltpu.VMEM((8 + CHUNK, D_CONV), f32),
                        pltpu.VMEM((CHUNK, D_SSM), f32), pltpu.VMEM((CHUNK, D_SSM), f32),
                        pltpu.VMEM((CHUNK, 2 * N_GROUPS * D_STATE), f32),
                        pltpu.VMEM((LANES, 2 * D_SSM), bf16), pltpu.VMEM((D_SSM, LANES), bf16)]
        + (_chip_exchange_scratch(nx) if nx else []),
        compiler_params=pltpu.CompilerParams(dimension_semantics=("arbitrary",)),
    )(proj, proj, proj, proj, proj, proj, y, states, dmix, conv_w, conv_b, dtb16, alog16, alog_f, d_f, nw, cv,
      *chip_sums)
    return outs[0], outs[1], outs[2], outs[3], outs[4:]


def _col_blocks(parts, tile):
    counts = [p.shape[1] // tile for p in parts]
    offs = [sum(counts[:t]) for t in range(len(parts))]
    return offs, counts, sum(counts)


def _bcast_copies(src_ref, out_ref, send_sems, recv_sems, local_sem):
    x, y, c = _my_pos()
    me = 4 * x + 2 * y + c
    mine = pltpu.make_async_copy(src_ref, out_ref.at[me], local_sem)
    sends, recvs = [], []
    for k in range(1, N_DEV):
        to, frm = (me + k) % N_DEV, (me + N_DEV - k) % N_DEV
        sems = dict(send_sem=send_sems.at[k - 1], recv_sem=recv_sems.at[k - 1], device_id_type=MESH)
        sends.append(pltpu.make_async_remote_copy(
            src_ref=src_ref, dst_ref=out_ref.at[me], device_id=(to // 4, (to // 2) % 2, to % 2), **sems))
        recvs.append(pltpu.make_async_remote_copy(
            src_ref=src_ref, dst_ref=out_ref.at[frm], device_id=(x, y, c), **sems))
    return mine, sends, recvs


def _bcast_scratch():
    return [pltpu.SemaphoreType.DMA((N_DEV - 1,)), pltpu.SemaphoreType.DMA((N_DEV - 1,)), pltpu.SemaphoreType.DMA(())]


def _inproj_bwd(dparts, wt, x, nw, dres, chip_sums=(), pack=None):
    s, d = x.shape
    tm, tk = 1024, 1024
    offs, counts, nk = _col_blocks(dparts, tk)
    npart, nx = len(dparts), len(chip_sums)
    npk = 0 if pack is None else 1
    ni = s // tm

    def body(*refs):
        dp_refs = refs[:npart]
        w_ref, x_ref, nw_ref, dres_ref = refs[npart:npart + 4]
        pos = npart + 4
        cs_in, pos = refs[pos:pos + nx], pos + nx
        pack_in, pos = refs[pos:pos + npk], pos + npk
        (gx_ref, gnw_ref), pos = refs[pos:pos + 2], pos + 2
        cs_out, pos = refs[pos:pos + nx], pos + nx
        pack_out, pos = refs[pos:pos + 2 * npk], pos + 2 * npk
        acc, pos = refs[pos], pos + 1
        cs_sems, pos = refs[pos:pos + 3 * min(nx, 1)], pos + 3 * min(nx, 1)
        pk_refs = refs[pos:]
        i, k = pl.program_id(0), pl.program_id(1)

        def exchange():
            return _chip_exchange_copies(cs_in, cs_out, *cs_sems)

        def pack_copies():
            return _bcast_copies(pack_in[0], pack_out[0], *pk_refs[1:4])

        def gnw_copies():
            return _bcast_copies(pk_refs[0], pack_out[1], *pk_refs[4:7])

        @pl.when(jnp.logical_and(i == 0, k == 0))
        def _():
            gnw_ref[...] = jnp.zeros_like(gnw_ref)
            if nx:
                mine, sends, _ = exchange()
                for cp in mine + sends:
                    cp.start()
            if npk:
                mine, sends, _ = pack_copies()
                for cp in [mine] + sends:
                    cp.start()

        @pl.when(k == 0)
        def _():
            acc[...] = _nn(dp_refs[0][...], w_ref[...])

        for t in range(npart):
            @pl.when(jnp.logical_and(k >= max(offs[t], 1), k < offs[t] + counts[t]))
            def _(t=t):
                acc[...] += _nn(dp_refs[t][...], w_ref[...])

        @pl.when(k == nk - 1)
        def _():
            xv = x_ref[...]
            r = lax.rsqrt(jnp.mean(xv * xv, axis=-1, keepdims=True) + EPS)
            xn = xv * r
            du = acc[...]
            gnw_ref[0:1, :] += jnp.sum(du * xn, axis=0, keepdims=True)
            dn = du * nw_ref[...]
            gx_ref[...] = dres_ref[...] + r * (dn - xn * jnp.mean(dn * xn, axis=-1, keepdims=True))

        @pl.when(jnp.logical_and(i == ni - 1, k == nk - 1))
        def _():
            if npk:
                pk_refs[0][...] = gnw_ref[...]
                mine, sends, _ = gnw_copies()
                for cp in [mine] + sends:
                    cp.start()
            if nx:
                mine, sends, recvs = exchange()
                for cp in recvs:
                    cp.wait_recv()
                for cp in sends:
                    cp.wait_send()
                for cp in mine:
                    cp.wait()
            if npk:
                for copies in (pack_copies(), gnw_copies()):
                    mine, sends, recvs = copies
                    for cp in recvs:
                        cp.wait_recv()
                    for cp in sends:
                        cp.wait_send()
                    mine.wait()

    def piece(t):
        return pl.BlockSpec((tm, tk), lambda i, k: (i, jnp.clip(k - offs[t], 0, counts[t] - 1)))

    anyspec = pl.BlockSpec(memory_space=pl.ANY)
    packs = [] if pack is None else [pack]
    pack_shapes = [] if pack is None else [SDS((N_DEV,) + pack.shape, f32), SDS((N_DEV, 8, d), f32)]
    scratch = [pltpu.VMEM((tm, d), f32)] + (_chip_exchange_scratch(nx) if nx else [])
    if npk:
        scratch += [pltpu.VMEM((8, d), f32)] + _bcast_scratch() + _bcast_scratch()
    outs = pl.pallas_call(
        body, name="inproj_bwd", grid=(ni, nk),
        in_specs=[piece(t) for t in range(npart)] + [
            pl.BlockSpec((tk, d), lambda i, k: (k, 0)),
            pl.BlockSpec((tm, d), lambda i, k: (i, 0)), pl.BlockSpec((1, d), lambda i, k: (0, 0)),
            pl.BlockSpec((tm, d), lambda i, k: (i, 0))] + [anyspec] * (nx + npk),
        out_specs=[pl.BlockSpec((tm, d), lambda i, k: (i, 0)), pl.BlockSpec((8, d), lambda i, k: (0, 0))]
        + [anyspec] * (nx + 2 * npk),
        out_shape=[SDS((s, d), f32), SDS((8, d), f32)] + [SDS(a.shape, a.dtype) for a in chip_sums] + pack_shapes,
        scratch_shapes=scratch,
        compiler_params=pltpu.CompilerParams(dimension_semantics=("arbitrary", "arbitrary")),
    )(*dparts, wt, x, nw, dres, *chip_sums, *packs)
    return outs[0], outs[1], outs[2:2 + nx], outs[2 + nx:]


def _matmul_tn(a_parts, b_parts, name):
    tile, tk = 1024, 1024
    s = a_parts[0].shape[0]
    nk = s // tk
    na, nb = len(a_parts), len(b_parts)
    offs_a, counts_a, ni = _col_blocks(a_parts, tile)
    offs_b, counts_b, nj = _col_blocks(b_parts, tile)

    def body(*refs):
        a_refs, b_refs, o_ref = refs[:na], refs[na:na + nb], refs[na + nb]
        i, j = pl.program_id(0), pl.program_id(1)

        @pl.when(pl.program_id(2) == 0)
        def _():
            o_ref[...] = jnp.zeros_like(o_ref)

        for ta in range(na):
            for tb in range(nb):
                in_a = jnp.logical_and(i >= offs_a[ta], i < offs_a[ta] + counts_a[ta])
                in_b = jnp.logical_and(j >= offs_b[tb], j < offs_b[tb] + counts_b[tb])

                @pl.when(jnp.logical_and(in_a, in_b))
                def _(ta=ta, tb=tb):
                    o_ref[...] += _tn(a_refs[ta][...], b_refs[tb][...])

    def spec(offs, counts, t, axis):
        def index(i, j, k):
            pos = (i, j)[axis]
            mine = jnp.logical_and(pos >= offs[t], pos < offs[t] + counts[t])
            return jnp.where(mine, k, 0), jnp.clip(pos - offs[t], 0, counts[t] - 1)
        return pl.BlockSpec((tk, tile), index)

    return pl.pallas_call(
        body, name=name, grid=(ni, nj, nk),
        in_specs=[spec(offs_a, counts_a, t, 0) for t in range(na)] + [spec(offs_b, counts_b, t, 1) for t in range(nb)],
        out_specs=pl.BlockSpec((tile, tile), lambda i, j, k: (i, j)),
        out_shape=SDS((ni * tile, nj * tile), f32),
        compiler_params=pltpu.CompilerParams(dimension_semantics=("parallel", "parallel", "arbitrary")),
    )(*a_parts, *b_parts)


def _adamw(w, g, m, v):
    m = ADAM_B1 * m + (1.0 - ADAM_B1) * g
    v = ADAM_B2 * v + (1.0 - ADAM_B2) * (g * g)
    m_hat = m / (1.0 - ADAM_B1 ** ADAM_STEP)
    v_hat = v / (1.0 - ADAM_B2 ** ADAM_STEP)
    delta = -ADAM_LR * (m_hat / (jnp.sqrt(v_hat) + ADAM_EPS) + ADAM_WD * w)
    return delta, m, v


def _sum_adamw(parts, w, m, v, name):
    r, c = w.shape
    tc = 256

    def body(p_ref, w_ref, m_ref, v_ref, g_ref, d_ref, nm_ref, nv_ref):
        g = p_ref[0].astype(f32)
        for q in range(1, 4):
            g = g + p_ref[q].astype(f32)
        g_ref[...] = g
        d_ref[...], nm_ref[...], nv_ref[...] = _adamw(w_ref[...], g, m_ref[...], v_ref[...])

    blk = pl.BlockSpec((r, tc), lambda i: (0, i))
    return pl.pallas_call(
        body, name=name, grid=(c // tc,),
        in_specs=[pl.BlockSpec((4, r, tc), lambda i: (0, 0, i)), blk, blk, blk],
        out_specs=[blk] * 4, out_shape=[SDS((r, c), f32)] * 4,
        compiler_params=pltpu.CompilerParams(dimension_semantics=("parallel",)),
    )(parts, w, m, v)


def _sum_small(parts, pre_blocks):
    def body(p_ref, b_ref, o_ref):
        t = p_ref[0]
        pre = b_ref[0]
        for j in range(1, N_DEV):
            t = t + p_ref[j]
            pre = pre + b_ref[j]
        o_ref[...] = t
        o_ref[5:6, 0:D_MODEL] = pre[0:1, :]
        row_h = _iota((D_SSM, LANES), 0) // HEAD_DIM
        fold = (row_h == _iota((D_SSM, LANES), 1)).astype(f32)
        lower = t[8:16, 0:LANES]
        folded = _nn_hi(t[8:16, 0:D_SSM], fold)
        loss = jnp.sum(t[11:12, 0:D_MODEL], axis=1, keepdims=True) * (0.5 / D_MODEL)
        row = _iota((8, LANES), 0)
        o_ref[8:16, 0:LANES] = jnp.where(row < 2, folded, jnp.where(row == 4, loss, lower))

    return pl.pallas_call(body, name="sum_small", out_shape=SDS((PACK_ROWS, PACK_W), f32),
                          in_specs=[pl.BlockSpec(memory_space=pltpu.VMEM)] * 2,
                          out_specs=pl.BlockSpec(memory_space=pltpu.VMEM))(parts, pre_blocks)


def _adamw_small(w, g, m, v):
    def body(w_ref, g_ref, m_ref, v_ref, d_ref, nm_ref, nv_ref):
        d_ref[...], nm_ref[...], nv_ref[...] = _adamw(w_ref[...], g_ref[...], m_ref[...], v_ref[...])

    vm = pl.BlockSpec(memory_space=pltpu.VMEM)
    return pl.pallas_call(body, name="adamw_small", out_shape=[SDS(w.shape, f32)] * 3,
                          in_specs=[vm] * 4, out_specs=[vm] * 3)(w, g, m, v)


def _pad_lanes(v, width):
    return jnp.pad(v, ((0, 0), (0, width - v.shape[1])))


def _local_step(x, tgt, norm_pre_w, wt, conv_w, conv_b, dt_bias, a_log, d_skip, ssm_norm_w, wo, norm_post_w, sharded):
    dtb16 = _pad_lanes(dt_bias, LANES)
    alog16 = _pad_lanes(a_log, LANES)
    alog_f = jnp.repeat(a_log, HEAD_DIM, axis=1)
    d_f = jnp.repeat(d_skip, HEAD_DIM, axis=1)

    shard_out = wo.shape[0]
    if sharded:
        proj, u, (g_out, g_cw) = _prenorm_inproj(x, norm_pre_w, wt, gather=(wo, conv_w))
        wo = g_out.reshape(N_DEV * shard_out, D_MODEL)
        conv_w = g_cw.transpose(1, 0, 2).reshape(4, D_CONV)
    else:
        proj, u, _ = _prenorm_inproj(x, norm_pre_w, wt)
    o, lb, mix_a, *qkv = _attn_fwd(proj)
    mix_s, y, states, cv = _ssd_fwd(proj, conv_w, conv_b, dtb16, alog16, alog_f, d_f, ssm_norm_w)
    dmix, dres, acc_post, dw_out = _outproj_loss(mix_a, mix_s, wo, x, tgt, norm_post_w)
    ssd_args = (proj, y, states, cv, dmix, conv_w, conv_b, dtb16, alog16, alog_f, d_f, ssm_norm_w)
    if sharded:
        dq, dk, dv, dg, got_out = _attn_bwd(proj, qkv, o, lb, dmix, swap=dw_out)
        chip_out = _chip_sum(dw_out, got_out, shard_out, "chip_sum_w_out")
        dzxd, g_conv, g_vec, g_dt, (parts_out,) = _ssd_bwd(*ssd_args, chip_sums=[chip_out])
    else:
        dq, dk, dv, dg = _attn_bwd(proj, qkv, o, lb, dmix)
        dzxd, g_conv, g_vec, g_dt, _ = _ssd_bwd(*ssd_args)
    dparts = [dq, dk, dv, dg, dzxd]

    def pack(g_pre_row):
        return jnp.concatenate(
            [g_conv[0:5], g_pre_row, _pad_lanes(g_vec[0:1], PACK_W), _pad_lanes(acc_post[1:2], PACK_W),
             _pad_lanes(g_vec[1:3], PACK_W), _pad_lanes(g_dt[0:1], PACK_W), _pad_lanes(acc_post[0:1], PACK_W),
             jnp.zeros((4, PACK_W), f32)], axis=0)

    if sharded:
        dw_in, got_in = _dw_in_swap(dparts, u)
        chip_in = _chip_sum(dw_in, got_in, D_IN_PROJ // N_DEV, "chip_sum_w_in")
        grad_x, _, (parts_in,), small = _inproj_bwd(dparts, wt, x, norm_pre_w, dres, [chip_in],
                                                    pack(jnp.zeros((1, PACK_W), f32)))
        return grad_x, (parts_in, parts_out), small
    dw_in = _matmul_tn(dparts, [u], "dw_in")
    grad_x, g_pre, _, _ = _inproj_bwd(dparts, wt, x, norm_pre_w, dres)
    return grad_x, (dw_in, dw_out), pack(_pad_lanes(g_pre[0:1], PACK_W))


def kernel(x, norm_pre_w, w_in, conv_w, conv_b, dt_bias, a_log, d_skip, ssm_norm_w, w_out, norm_post_w, loss_target, m_norm_pre_w, m_w_in, m_conv_w, m_conv_b, m_dt_bias, m_a_log, m_d_skip, m_ssm_norm_w, m_w_out, m_norm_post_w, v_norm_pre_w, v_w_in, v_conv_w, v_conv_b, v_dt_bias, v_a_log, v_d_skip, v_ssm_norm_w, v_w_out, v_norm_post_w):
    shard_cv = conv_w.shape[2]
    me = 4 * lax.axis_index("x") + 2 * lax.axis_index("y") + lax.axis_index("c")

    g_in, = _all_gather([w_in[0].T.astype(bf16)])
    wt = _assemble_wt(g_in)

    grad_x, (parts_in, parts_out), (parts_small, pre_blocks) = _local_step(
        x[0], loss_target[0], norm_pre_w, wt, conv_w[0], conv_b, dt_bias, a_log, d_skip, ssm_norm_w,
        w_out[0].astype(bf16), norm_post_w, sharded=True)

    g_w_in, d_w_in, nm_w_in, nv_w_in = (a.T for a in _sum_adamw(
        parts_in, w_in[0].T, m_w_in[0].T, v_w_in[0].T, "sum_adamw_w_in"))
    g_w_out, d_w_out, nm_w_out, nv_w_out = _sum_adamw(parts_out, w_out[0], m_w_out[0], v_w_out[0], "sum_adamw_w_out")
    tot = _sum_small(parts_small, pre_blocks)

    g_cw_all = tot[0:4]
    small_g = {
        "conv_w": lax.dynamic_slice(g_cw_all, (0, me * shard_cv), (4, shard_cv)),
        "conv_b": tot[4:5], "norm_pre_w": tot[5:6, :D_MODEL], "ssm_norm_w": tot[6:7, :D_SSM],
        "norm_post_w": tot[7:8, :D_MODEL], "a_log": tot[8:9, :16], "d_skip": tot[9:10, :16], "dt_bias": tot[10:11, :16],
    }
    loss = tot[12, 0]
    small_w = {"conv_w": (conv_w[0], m_conv_w[0], v_conv_w[0]), "conv_b": (conv_b, m_conv_b, v_conv_b),
               "norm_pre_w": (norm_pre_w, m_norm_pre_w, v_norm_pre_w), "ssm_norm_w": (ssm_norm_w, m_ssm_norm_w, v_ssm_norm_w),
               "norm_post_w": (norm_post_w, m_norm_post_w, v_norm_post_w), "a_log": (a_log, m_a_log, v_a_log),
               "d_skip": (d_skip, m_d_skip, v_d_skip), "dt_bias": (dt_bias, m_dt_bias, v_dt_bias)}
    names = list(small_w)
    sizes = [small_g[k].size for k in names]
    tot_size = sum(sizes)
    pad_to = -(-tot_size // 1024) * 1024

    def flat(arrs):
        v = jnp.concatenate([a.reshape(-1) for a in arrs])
        return jnp.pad(v, (0, pad_to - tot_size)).reshape(pad_to // LANES, LANES)

    fw = flat([small_w[k][0] for k in names])
    fg = flat([small_g[k] for k in names])
    fm = flat([small_w[k][1] for k in names])
    fv = jnp.pad(jnp.concatenate([small_w[k][2].reshape(-1) for k in names]), (0, pad_to - tot_size),
                 constant_values=1.0).reshape(pad_to // LANES, LANES)
    fd, fnm, fnv = _adamw_small(fw, fg, fm, fv)

    def unflat(f):
        out, off = {}, 0
        v = f.reshape(-1)
        for k, n in zip(names, sizes):
            out[k] = v[off:off + n].reshape(small_g[k].shape)
            off += n
        return out

    sd, snm, snv = unflat(fd), unflat(fnm), unflat(fnv)
    lead = lambda a: a[None]
    order = ["norm_pre_w", "w_in", "conv_w", "conv_b", "dt_bias", "a_log", "d_skip", "ssm_norm_w", "w_out", "norm_post_w"]
    grads = dict(small_g, w_in=g_w_in, w_out=g_w_out)
    deltas = dict(sd, w_in=d_w_in, w_out=d_w_out)
    new_m = dict(snm, w_in=nm_w_in, w_out=nm_w_out)
    new_v = dict(snv, w_in=nv_w_in, w_out=nv_w_out)

    def shaped(dct, k):
        a = dct[k]
        return lead(a) if k in ("w_in", "w_out", "conv_w") else a

    return (loss, grad_x[None], *[shaped(grads, k) for k in order], *[shaped(deltas, k) for k in order],
            *[shaped(new_m, k) for k in order], *[shaped(new_v, k) for k in order])
```

```python
import jax
import jax.numpy as jnp
from jax import lax
from jax.experimental import pallas as pl
from jax.experimental.pallas import tpu as pltpu

f32, bf16 = jnp.float32, jnp.bfloat16
SDS = jax.ShapeDtypeStruct
HIGHEST = lax.Precision.HIGHEST
MESH = pl.DeviceIdType.MESH

N_DEV = 8
D_MODEL = 1024
D_ATTN = 1024
D_SSM = 1024
HEAD_DIM = 64
N_PAIRS = 8
D_STATE = 128
N_GROUPS = 2
D_CONV = D_SSM + 2 * N_GROUPS * D_STATE
D_IN_PROJ = 4 * D_ATTN + D_SSM + D_CONV + 16
NP = 7168
CHUNK = 128
BLK = 128
DILATIONS = (1, 4, 16)
EPS = 1e-6
LANES = 128
COL_Z, COL_XS, COL_BC, COL_DT = 4096, 5120, 6144, 6656

ADAM_LR, ADAM_B1, ADAM_B2, ADAM_EPS, ADAM_WD, ADAM_STEP = 0.001, 0.9, 0.999, 1e-08, 0.01, 10

PACK_ROWS, PACK_W = 16, 1536


def _nt(a, b):
    return lax.dot_general(a, b, (((1,), (1,)), ((), ())), preferred_element_type=f32)


def _tn(a, b):
    return lax.dot_general(a, b, (((0,), (0,)), ((), ())), preferred_element_type=f32)


def _nn(a, b):
    return jnp.dot(a, b, preferred_element_type=f32)


def _nn_hi(a, b):
    return jnp.dot(a, b, precision=HIGHEST, preferred_element_type=f32)


def _sigmoid(x):
    return 1.0 / (1.0 + jnp.exp(-x))


def _softplus(x):
    return jnp.maximum(x, 0.0) + jnp.log1p(jnp.exp(-jnp.abs(x)))


def _iota(shape, dim):
    return lax.broadcasted_iota(jnp.int32, shape, dim)


def _my_pos():
    return lax.axis_index("x"), lax.axis_index("y"), lax.axis_index("c")


GATHER_SEMS = 9


def _gather_phases(ins, outs, send_sems, recv_sems, local_sems):
    n, ns = len(ins), GATHER_SEMS
    x, y, c = _my_pos()
    me, sibling = (x, y, c), (x, y, 1 - c)
    xn, yn, diag = (1 - x, y), (x, 1 - y), (1 - x, 1 - y)

    def slot(a, px, py, pc):
        return outs[a].at[4 * px + 2 * py + pc]

    def part(a, ref, h):
        width = ins[a].shape[-1]
        if width % (2 * LANES):
            return ref if h == 1 else None
        return ref.at[:, pl.ds(h * (width // 2), width // 2)]

    def copy(a, k, block, to, src=None, h=None):
        src_ref = slot(a, *block) if src is None else src
        dst_ref = slot(a, *block)
        if h is not None:
            src_ref, dst_ref = part(a, src_ref, h), part(a, dst_ref, h)
            if src_ref is None:
                return None
        return pltpu.make_async_remote_copy(
            src_ref=src_ref, dst_ref=dst_ref, send_sem=send_sems.at[ns * a + k], recv_sem=recv_sems.at[ns * a + k],
            device_id=to, device_id_type=MESH)

    def mine():
        return [pltpu.make_async_copy(ins[a], slot(a, *me), local_sems.at[a]) for a in range(n)]

    def own_sends(a):
        return [copy(a, 0, me, sibling, src=ins[a]), copy(a, 1, me, (*xn, c), src=ins[a]),
                copy(a, 2, me, (*yn, c), src=ins[a])]

    def neighbour_relays(a):
        return [copy(a, 4, (*xn, c), sibling), copy(a, 7, (*xn, c), (*yn, c), h=1),
                copy(a, 5, (*yn, c), sibling), copy(a, 8, (*yn, c), (*xn, c), h=0)]

    def diagonal_halves(a):
        return [copy(a, k, (*diag, c), me, h=h) for k, h in ((8, 0), (7, 1))]

    def start_all(cps):
        for cp in cps:
            if cp is not None:
                cp.start()

    def phase0():
        start_all(mine())
        for a in range(n):
            start_all(own_sends(a))

    def phase1():
        for a in range(n):
            copy(a, 1, (*xn, c), me).wait_recv()
            copy(a, 2, (*yn, c), me).wait_recv()
            start_all(neighbour_relays(a))

    def phase2():
        for a in range(n):
            for cp in diagonal_halves(a):
                if cp is not None:
                    cp.wait_recv()
            copy(a, 6, (*diag, c), sibling).start()

    def finish():
        for a in range(n):
            copy(a, 0, sibling, me).wait_recv()
            for j, chip in enumerate((xn, yn, diag)):
                copy(a, 4 + j, (*chip, 1 - c), me).wait_recv()
        for a in range(n):
            for cp in own_sends(a) + neighbour_relays(a) + [copy(a, 6, (*diag, c), sibling)]:
                if cp is not None:
                    cp.wait_send()
        for cp in mine():
            cp.wait()

    return phase0, phase1, phase2, finish


def _gather_scratch(n):
    return [pltpu.SemaphoreType.DMA((GATHER_SEMS * n,)), pltpu.SemaphoreType.DMA((GATHER_SEMS * n,)),
            pltpu.SemaphoreType.DMA((n,))]


def _all_gather(arrs):
    n = len(arrs)

    def body(*refs):
        for phase in _gather_phases(refs[:n], refs[n:2 * n], *refs[2 * n:]):
            phase()

    anyspec = pl.BlockSpec(memory_space=pl.ANY)
    return pl.pallas_call(
        body, name="weights_all_gather",
        out_shape=[SDS((N_DEV,) + a.shape, a.dtype) for a in arrs],
        in_specs=[anyspec] * n, out_specs=[anyspec] * n, scratch_shapes=_gather_scratch(n),
    )(*arrs)


def _dw_in_swap(a_parts, u):
    tile, tk = 1024, 1024
    s = u.shape[0]
    nk = s // tk
    na = len(a_parts)
    offs, counts, ni = _col_blocks(a_parts, tile)

    def body(*refs):
        a_refs, u_ref = refs[:na], refs[na]
        dw_ref, got_ref = refs[na + 1:na + 3]
        acc, stage, local_sems, send_sems, recv_sem = refs[na + 3:]
        i, k = pl.program_id(0), pl.program_id(1)
        x, y, c = _my_pos()
        par = i % 2

        def tile_copies(t, p):
            rows = pl.ds(pl.multiple_of(t * tile, tile), tile)
            loc = pltpu.make_async_copy(stage.at[p], dw_ref.at[rows], local_sems.at[p])
            rem = pltpu.make_async_remote_copy(
                src_ref=stage.at[p], dst_ref=got_ref.at[rows], send_sem=send_sems.at[p], recv_sem=recv_sem,
                device_id=(x, y, 1 - c), device_id_type=MESH)
            return loc, rem

        @pl.when(k == 0)
        def _():
            acc[...] = jnp.zeros((tile, tile), f32)

        for t in range(na):
            @pl.when(jnp.logical_and(i >= offs[t], i < offs[t] + counts[t]))
            def _(t=t):
                acc[...] += _tn(a_refs[t][...], u_ref[pl.ds(pl.multiple_of(k * tk, tk), tk), :])

        @pl.when(k == nk - 1)
        def _():
            @pl.when(i >= 2)
            def _():
                loc, rem = tile_copies(i - 2, par)
                loc.wait()
                rem.wait_send()
            stage[par] = acc[...]
            loc, rem = tile_copies(i, par)
            loc.start()
            rem.start()

        @pl.when(jnp.logical_and(i == ni - 1, k == nk - 1))
        def _():
            for t in (ni - 2, ni - 1):
                loc, rem = tile_copies(t, t % 2)
                loc.wait()
                rem.wait_send()
            pltpu.make_async_remote_copy(src_ref=dw_ref, dst_ref=got_ref, send_sem=send_sems.at[0], recv_sem=recv_sem,
                                         device_id=(x, y, c), device_id_type=MESH).wait_recv()

    def a_spec(t):
        def index(i, k):
            mine = jnp.logical_and(i >= offs[t], i < offs[t] + counts[t])
            return jnp.where(mine, k, 0), jnp.clip(i - offs[t], 0, counts[t] - 1)
        return pl.BlockSpec((tk, tile), index)

    anyspec = pl.BlockSpec(memory_space=pl.ANY)
    return pl.pallas_call(
        body, name="dw_in_swap", grid=(ni, nk),
        in_specs=[a_spec(t) for t in range(na)] + [pl.BlockSpec((s, tile), lambda i, k: (0, 0))],
        out_specs=[anyspec] * 2,
        out_shape=[SDS((ni * tile, tile), f32), SDS((ni * tile, tile), f32)],
        scratch_shapes=[pltpu.VMEM((tile, tile), f32), pltpu.VMEM((2, tile, tile), f32), pltpu.SemaphoreType.DMA((2,)),
                        pltpu.SemaphoreType.DMA((2,)), pltpu.SemaphoreType.DMA(())],
        compiler_params=pltpu.CompilerParams(dimension_semantics=("arbitrary", "arbitrary")),
    )(*a_parts, u)


def _chip_sum(mine, got, rows, name):
    r, cdim = mine.shape
    tc = LANES

    def body(m_ref, g_ref, s16_ref):
        c = lax.axis_index("c")
        for q in range(4):
            blk = pl.ds(rows * (2 * q + c), rows)
            s16_ref[q] = (m_ref[blk, :] + g_ref[blk, :]).astype(bf16)

    col = pl.BlockSpec((r, tc), lambda i: (0, i))
    return pl.pallas_call(
        body, name=name, grid=(cdim // tc,), in_specs=[col, col],
        out_specs=pl.BlockSpec((4, rows, tc), lambda i: (0, 0, i)), out_shape=SDS((4, rows, cdim), bf16),
        compiler_params=pltpu.CompilerParams(dimension_semantics=("parallel",)),
    )(mine, got)


def _assemble_wt(shards):
    nd, rows, cdim = shards.shape
    tc = 256

    def body(g_ref, o_ref):
        for j in range(nd):
            o_ref[pl.ds(rows * j, rows), :] = g_ref[j]
        o_ref[pl.ds(nd * rows, NP - nd * rows), :] = jnp.zeros((NP - nd * rows, tc), shards.dtype)

    return pl.pallas_call(
        body, name="assemble_w_in", grid=(cdim // tc,),
        in_specs=[pl.BlockSpec((nd, rows, tc), lambda i: (0, 0, i))],
        out_specs=pl.BlockSpec((NP, tc), lambda i: (0, i)), out_shape=SDS((NP, cdim), shards.dtype),
        compiler_params=pltpu.CompilerParams(dimension_semantics=("parallel",)),
    )(shards)


def _chip_exchange_copies(ins, outs, send_sems, recv_sems, local_sems):
    nb = len(ins)
    x, y, c = _my_pos()
    my_q = 2 * x + y
    mine = [pltpu.make_async_copy(ins[a].at[my_q], outs[a].at[my_q], local_sems.at[a]) for a in range(nb)]
    sends, recvs = [], []
    for k in range(1, 4):
        to, frm = (my_q + k) % 4, (my_q + 4 - k) % 4
        for a in range(nb):
            sems = dict(send_sem=send_sems.at[3 * a + k - 1], recv_sem=recv_sems.at[3 * a + k - 1], device_id_type=MESH)
            sends.append(pltpu.make_async_remote_copy(
                src_ref=ins[a].at[to], dst_ref=outs[a].at[my_q], device_id=(to // 2, to % 2, c), **sems))
            recvs.append(pltpu.make_async_remote_copy(
                src_ref=ins[a].at[frm], dst_ref=outs[a].at[frm], device_id=(x, y, c), **sems))
    return mine, sends, recvs


def _chip_exchange_scratch(nb):
    return [pltpu.SemaphoreType.DMA((3 * nb,)), pltpu.SemaphoreType.DMA((3 * nb,)), pltpu.SemaphoreType.DMA((nb,))]


def _prenorm_inproj(x, nw, wt, gather=()):
    s, d = x.shape
    npad = wt.shape[0]
    tm, tn = 1024, 1024
    ng = len(gather)
    ni, nj = s // tm, npad // tn

    def body(x_ref, nw_ref, w_ref, *refs):
        g_in, (proj_ref, u_ref), g_out, sems = refs[:ng], refs[ng:ng + 2], refs[ng + 2:2 * ng + 2], refs[2 * ng + 2:]
        i, j = pl.program_id(0), pl.program_id(1)
        if ng:
            phases = _gather_phases(g_in, g_out, *sems)
            for step, phase in enumerate(phases[:3]):
                @pl.when(jnp.logical_and(i == step, j == 0))
                def _(phase=phase):
                    phase()

        @pl.when(j == 0)
        def _():
            xv = x_ref[...]
            r = lax.rsqrt(jnp.mean(xv * xv, axis=-1, keepdims=True) + EPS)
            u_ref[...] = (xv * r * nw_ref[...]).astype(bf16)
        proj_ref[...] = _nt(u_ref[...], w_ref[pl.ds(pl.multiple_of(j * tn, tn), tn), :])

        if ng:
            @pl.when(jnp.logical_and(i == ni - 1, j == nj - 1))
            def _():
                phases[3]()

    anyspec = pl.BlockSpec(memory_space=pl.ANY)
    outs = pl.pallas_call(
        body, name="prenorm_inproj", grid=(ni, nj),
        in_specs=[pl.BlockSpec((tm, d), lambda i, j: (i, 0)), pl.BlockSpec((1, d), lambda i, j: (0, 0)),
                  pl.BlockSpec((npad, d), lambda i, j: (0, 0))] + [anyspec] * ng,
        out_specs=[pl.BlockSpec((tm, tn), lambda i, j: (i, j)), pl.BlockSpec((tm, d), lambda i, j: (i, 0))]
        + [anyspec] * ng,
        out_shape=[SDS((s, npad), f32), SDS((s, d), bf16)] + [SDS((N_DEV,) + a.shape, a.dtype) for a in gather],
        scratch_shapes=_gather_scratch(ng) if ng else [],
        compiler_params=pltpu.CompilerParams(dimension_semantics=("arbitrary", "arbitrary")),
    )(x, nw, wt, *gather)
    return outs[0], outs[1], outs[2:]


def _attn_consts():
    head0 = _iota((BLK, LANES), 1) < HEAD_DIM
    tri2 = (_iota((BLK, 2 * LANES), 1) % LANES) <= _iota((BLK, 2 * LANES), 0)
    ones2 = ((_iota((LANES, 2 * LANES), 0) < HEAD_DIM) == (_iota((LANES, 2 * LANES), 1) < LANES)).astype(bf16)
    rmat = ((_iota((2 * LANES, LANES), 0) < LANES) == (_iota((2 * LANES, LANES), 1) < HEAD_DIM)).astype(bf16)
    bones = ((_iota((LANES, LANES), 0) < HEAD_DIM) == (_iota((LANES, LANES), 1) < HEAD_DIM)).astype(bf16)
    return head0, tri2, ones2, rmat, bones


def _stack_heads(x16, head0):
    zero = jnp.zeros_like(x16)
    return jnp.concatenate([jnp.where(head0, x16, zero), jnp.where(head0, zero, x16)], axis=0)


def _bf16_terms(x, terms):
    out = []
    for _ in range(terms):
        t = x.astype(bf16)
        out.append(t)
        x = x - t.astype(f32)
    return out


def _dot_01(x, w16, terms):
    return _nn(jnp.concatenate(_bf16_terms(x, terms), axis=1), jnp.concatenate([w16] * terms, axis=0))


def _split_dot_sum(x, w16):
    hi, lo = _bf16_terms(x, 2)
    return _nn(hi, w16) + _nn(lo, w16)


def _dot_01_left(w16, x, terms):
    return _nn(jnp.concatenate([w16] * terms, axis=1), jnp.concatenate(_bf16_terms(x, terms), axis=0))


def _quarter_rows(i, q):
    return pl.ds(pl.multiple_of((i // 2) * 2048 + q * 512 + (i % 2) * 256, 256), 256)


def _token_rows(i, q):
    return pl.ds(i * 1024 + q, 256, stride=4)


def _quarter_block(i, d, nb):
    r, blk = i // nb, i % nb
    if d == 1:
        runs = [pl.ds(pl.multiple_of((blk // 16) * 2048 + q * 512 + (blk % 16) * 32, 32), 32) for q in range(4)]
    elif d == 4:
        runs = [pl.ds(pl.multiple_of((blk // 4) * 2048 + r * 512 + (blk % 4) * BLK, BLK), BLK)]
    else:
        runs = [pl.ds(blk * 2048 + (r % 4) * 512 + r // 4, BLK, stride=4)]
    return runs, blk > 0


def _quarter_mask():
    order = lambda n: 4 * (n % 32) + n // 32
    return order(_iota((BLK, 2 * LANES), 1) % LANES) <= order(_iota((BLK, 2 * LANES), 0))


def _load_runs(ref, runs):
    parts = [ref[run, :] for run in runs]
    return parts[0] if len(parts) == 1 else jnp.concatenate(parts, axis=0)


def _store_runs(ref, runs, val):
    n = BLK // len(runs)
    for t, run in enumerate(runs):
        ref[run, :] = val[t * n:(t + 1) * n]


def _add_runs(ref, runs, val):
    n = BLK // len(runs)
    for t, run in enumerate(runs):
        ref[run, :] += val[t * n:(t + 1) * n]


def _attn_fwd(proj):
    s = proj.shape[0]
    n_it = s // BLK

    def body(q_in, k_in, v_in, g_ref, o_ref, l_ref, mix_ref, q_ref, k_ref, v_ref, op0, op1, op2, lp0, lp1, lp2,
             s_a, s_b, sd_a, sd_b, p_a, p_b, m_a, m_b, pd_a, pd_b, k_a, k_b, v_a, v_b, stage):
        op_refs, lp_refs = (op0, op1, op2), (lp0, lp1, lp2)
        head0, tri2_t, ones2, rmat, _ = _attn_consts()
        tri2_q = _quarter_mask()

        def reorder(i, carry):
            for src, dst in ((q_in, q_ref), (k_in, k_ref), (v_in, v_ref)):
                for q in range(4):
                    dst[_quarter_rows(i, q), :] = src[_token_rows(i, q), :]
            return carry

        lax.fori_loop(0, s // 1024, reorder, 0, unroll=True)
        score_bufs, prob_bufs = ((s_a, sd_a), (s_b, sd_b)), ((p_a, m_a, pd_a), (p_b, m_b, pd_b))
        k_bufs, v_bufs = (k_a, k_b), (v_a, v_b)
        for buf in k_bufs + v_bufs:
            buf[...] = jnp.zeros_like(buf)

        def unstack(st16):
            return st16[:BLK] + st16[BLK:]

        def scores(i, par, d, nb):
            rows, has_prev = _quarter_block(i, d, nb)
            tri2 = tri2_q if d == 1 else tri2_t
            s_buf, sd_buf = score_bufs[par]
            qs = _load_runs(q_ref, rows) * 0.125
            qs16 = qs.astype(bf16)
            kst_c = _stack_heads(_load_runs(k_ref, rows).astype(bf16), head0)
            kst_p = k_bufs[1 - par][...]
            k_bufs[par][...] = kst_c
            sc = _nt(qs16, kst_c)
            sp = _nt(qs16, kst_p)
            s_buf[...] = jnp.where(tri2, sc, jnp.where(has_prev, sp, -jnp.inf))
            sd = _nn((qs * unstack(kst_p).astype(f32)).astype(bf16), ones2)
            sd_buf[...] = jnp.where(has_prev, sd, -jnp.inf)

        def softmax(bufs_in, bufs_out):
            s_buf, sd_buf = bufs_in
            p_buf, m_buf, pd_buf = bufs_out
            sc, sd2 = s_buf[...], sd_buf[...]
            m0 = jnp.max(sc[:, :LANES], axis=1, keepdims=True)
            m1 = jnp.max(sc[:, LANES:], axis=1, keepdims=True)
            m2 = jnp.concatenate([jnp.broadcast_to(m0, (BLK, LANES)), jnp.broadcast_to(m1, (BLK, LANES))], axis=1)
            m2 = jnp.maximum(m2, sd2)
            p_buf[...] = jnp.exp(sc - m2).astype(bf16)
            m_pair = jnp.where(head0, m2[:, :LANES], m2[:, LANES:])
            m_buf[...] = m_pair
            pd_buf[...] = jnp.exp(jnp.where(head0, sd2[:, :LANES], sd2[:, LANES:]) - m_pair)

        def output(i, par, d, nb, p):
            rows, _ = _quarter_block(i, d, nb)
            tri2 = tri2_q if d == 1 else tri2_t
            p_buf, m_buf, pd_buf = prob_bufs[par]
            vst_c = _stack_heads(_load_runs(v_ref, rows).astype(bf16), head0)
            vst_p = v_bufs[1 - par][...]
            v_bufs[par][...] = vst_c
            pt16, pd = p_buf[...], pd_buf[...]
            zero = jnp.zeros_like(pt16)
            o = (_nn(jnp.where(tri2, pt16, zero), vst_c) + _nn(jnp.where(tri2, zero, pt16), vst_p)
                 + pd * unstack(vst_p).astype(f32))
            l = _nn(pt16, rmat) + pd
            _store_runs(op_refs[p], rows, o / l)
            _store_runs(lp_refs[p], rows, m_buf[...] + jnp.log(l))

        for p, d in enumerate(DILATIONS):
            nb = s // (BLK * d)
            scores(0, 0, d, nb)
            scores(1, 1, d, nb)
            softmax(score_bufs[0], prob_bufs[0])

            def steps(j, carry, d=d, nb=nb, p=p):
                for par in range(2):
                    t = 2 * j + 2 + par
                    scores(t, par, d, nb)
                    output(t - 2, par, d, nb, p)
                    softmax(score_bufs[1 - par], prob_bufs[1 - par])
                return carry

            lax.fori_loop(0, (n_it - 2) // 2, steps, 0, unroll=True)
            output(n_it - 2, 0, d, nb, p)
            softmax(score_bufs[1], prob_bufs[1])
            output(n_it - 1, 1, d, nb, p)

        def merge(i, carry):
            for q in range(4):
                rows, tokens = _quarter_rows(i, q), _token_rows(i, q)
                l0, l1, l2 = lp0[rows, :], lp1[rows, :], lp2[rows, :]
                m = jnp.maximum(jnp.maximum(l0, l1), l2)
                e0, e1, e2 = jnp.exp(l0 - m), jnp.exp(l1 - m), jnp.exp(l2 - m)
                z = e0 + e1 + e2
                o = (e0 * op0[rows, :] + e1 * op1[rows, :] + e2 * op2[rows, :]) / z
                o_ref[tokens, :] = o
                l_ref[rows, :] = m + jnp.log(z)
                g = g_ref[tokens, :]
                stage[pl.ds(q, 256, stride=4), :] = o * (g * _sigmoid(g))
            mix_ref[pl.ds(pl.multiple_of(i * 1024, 1024), 1024), :] = stage[...].astype(bf16)
            return carry

        lax.fori_loop(0, s // 1024, merge, 0, unroll=True)

    col = lambda base: pl.BlockSpec((s, LANES), lambda h: (0, base + h))
    return pl.pallas_call(
        body, name="attn_fwd", grid=(N_PAIRS,),
        in_specs=[col(0), col(8), col(16), col(24)],
        out_specs=[col(0)] * 6,
        out_shape=[SDS((s, D_ATTN), f32), SDS((s, D_ATTN), f32), SDS((s, D_ATTN), bf16)] + [SDS((s, D_ATTN), f32)] * 3,
        scratch_shapes=[pltpu.VMEM((s, LANES), f32)] * 6 + [pltpu.VMEM((BLK, 2 * LANES), f32)] * 4
        + [pltpu.VMEM((BLK, 2 * LANES), bf16)] * 2 + [pltpu.VMEM((BLK, LANES), f32)] * 4
        + [pltpu.VMEM((2 * BLK, LANES), bf16)] * 4 + [pltpu.VMEM((1024, LANES), f32)],
        compiler_params=pltpu.CompilerParams(dimension_semantics=("parallel",)),
    )(proj, proj, proj, proj)


def _expand_mat():
    colv = _iota((LANES, 2 * D_SSM), 1)
    head = 2 * ((colv % D_SSM) // LANES) + colv // D_SSM
    return (_iota((LANES, 2 * D_SSM), 0) == head).astype(bf16)


def _fold_mat():
    return (_iota((D_SSM, LANES), 0) // HEAD_DIM == _iota((D_SSM, LANES), 1)).astype(bf16)


def _conv(xs_ref, bc_ref, xs_tail, bc_tail, cw_ref, cb_ref, xpad, first):
    keep = jnp.where(first, 0.0, 1.0)
    xpad[0:8, 0:D_SSM] = xs_tail[...] * keep
    xpad[0:8, D_SSM:D_CONV] = bc_tail[...] * keep
    xpad[8:8 + CHUNK, 0:D_SSM] = xs_ref[...]
    xpad[8:8 + CHUNK, D_SSM:D_CONV] = bc_ref[...]
    xp = xpad[...]
    cv = cb_ref[...] + cw_ref[3:4, :] * xp[8:8 + CHUNK]
    for j in range(3):
        cv = cv + cw_ref[j:j + 1, :] * pltpu.roll(xp, 3 - j, 0)[8:8 + CHUNK]
    return cv


def _decay_terms(dt_ref, dtb_ref, alog16_ref, emat_ref):
    pre = dt_ref[...] + dtb_ref[...]
    dt16 = _softplus(pre)
    a16 = -jnp.exp(alog16_ref[...])
    sub, lane = _iota((CHUNK, CHUNK), 0), _iota((CHUNK, CHUNK), 1)
    tri = (sub >= lane).astype(f32)
    al16 = _nn_hi(tri, dt16 * a16)
    al_t = al16.T
    emat = emat_ref[...]
    dt_x = _dot_01(dt16, emat, 3)
    al_x = _dot_01(al16, emat, 3)
    lane_w = _iota((CHUNK, D_SSM), 1)
    even = (lane_w % LANES) < HEAD_DIM
    dt_f = jnp.where(even, dt_x[:, :D_SSM], dt_x[:, D_SSM:])
    al_f = jnp.where(even, al_x[:, :D_SSM], al_x[:, D_SSM:])
    return pre, dt_f, al_f, al_x, al_t


def _decay_mat(al_x, al_t, pair, h):
    sub, lane = _iota((CHUNK, CHUNK), 0), _iota((CHUNK, CHUNK), 1)
    col = al_x[:, h * D_SSM + pair * LANES: h * D_SSM + (pair + 1) * LANES]
    row = al_t[2 * pair + h: 2 * pair + h + 1, :]
    return jnp.exp(jnp.where(sub >= lane, col - row, -jnp.inf))


def _ssd_in_specs(order):
    blk = lambda w, cb: pl.BlockSpec((CHUNK, w), lambda i: (order(i), cb))
    tail = lambda w, cb: pl.BlockSpec((8, w), lambda i: (jnp.maximum(16 * order(i) - 1, 0), cb))
    return [blk(D_SSM, COL_XS // D_SSM), blk(512, COL_BC // 512), tail(D_SSM, COL_XS // D_SSM),
            tail(512, COL_BC // 512), blk(LANES, COL_DT // LANES), blk(D_SSM, COL_Z // D_SSM)]


def _full(shape):
    return pl.BlockSpec(shape, lambda i: (0,) * len(shape))


def _ssd_fwd(proj, conv_w, conv_b, dtb16, alog16, alog_f, d_f, nw):
    s = proj.shape[0]
    nc = s // CHUNK

    def body(xs_ref, bc_ref, xs_tail, bc_tail, dt_ref, z_ref, cw_ref, cb_ref, dtb_ref, alog16_ref, alogf_ref,
             df_ref, nw_ref, mix_ref, y_ref, st_ref, cv_ref, h_scr, xpad, y_scr, emat_ref):
        c = pl.program_id(0)

        @pl.when(c == 0)
        def _():
            h_scr[...] = jnp.zeros_like(h_scr)
            emat_ref[...] = _expand_mat()

        cv = _conv(xs_ref, bc_ref, xs_tail, bc_tail, cw_ref, cb_ref, xpad, c == 0)
        cv_ref[...] = cv
        xbc = cv * _sigmoid(cv)
        _, dt_f, al_f, al_x, al_t = _decay_terms(dt_ref, dtb_ref, alog16_ref, emat_ref)
        head0 = _iota((CHUNK, LANES), 1) < HEAD_DIM
        st_ref[...] = h_scr[...]
        for g in range(N_GROUPS):
            bm = xbc[:, D_SSM + g * D_STATE: D_SSM + (g + 1) * D_STATE].astype(bf16)
            cm = xbc[:, D_SSM + (N_GROUPS + g) * D_STATE: D_SSM + (N_GROUPS + g + 1) * D_STATE].astype(bf16)
            gmat = _nt(cm, bm)
            for pair in range(4 * g, 4 * g + 4):
                sl = slice(pair * LANES, (pair + 1) * LANES)
                xp, dtp, alp = xbc[:, sl], dt_f[:, sl], al_f[:, sl]
                xdt = xp * dtp
                xdt16 = xdt.astype(bf16)
                al_last = alp[CHUNK - 1:CHUNK, :]
                hp = h_scr[:, sl]
                y_off = jnp.exp(alp) * _nn(cm, hp.astype(bf16))
                yd = [_nn((gmat * _decay_mat(al_x, al_t, pair, h)).astype(bf16), xdt16) for h in range(2)]
                y_scr[:, sl] = jnp.where(head0, yd[0], yd[1]) + y_off + df_ref[:, sl] * xp
                st = _tn(bm, (jnp.exp(al_last - alp) * xdt).astype(bf16))
                h_scr[:, sl] = jnp.exp(al_last) * hp + st
        y = y_scr[...]
        y_ref[...] = y
        z = z_ref[...]
        yz = y * (z * _sigmoid(z))
        gw = D_SSM // N_GROUPS
        for g in range(N_GROUPS):
            part = yz[:, g * gw:(g + 1) * gw]
            r = lax.rsqrt(jnp.mean(part * part, axis=-1, keepdims=True) + EPS)
            mix_ref[:, g * gw:(g + 1) * gw] = (part * r * nw_ref[:, g * gw:(g + 1) * gw]).astype(bf16)

    order = lambda i: i
    row = lambda w: pl.BlockSpec((CHUNK, w), lambda i: (i, 0))
    return pl.pallas_call(
        body, name="ssd_fwd", grid=(nc,),
        in_specs=_ssd_in_specs(order) + [_full((4, D_CONV)), _full((1, D_CONV)), _full((1, LANES)), _full((1, LANES)),
                                         _full((1, D_SSM)), _full((1, D_SSM)), _full((1, D_SSM))],
        out_specs=[row(D_SSM), row(D_SSM), pl.BlockSpec((None, D_STATE, D_SSM), lambda i: (i, 0, 0)), row(D_CONV)],
        out_shape=[SDS((s, D_SSM), bf16), SDS((s, D_SSM), f32), SDS((nc, D_STATE, D_SSM), f32),
                   SDS((s, D_CONV), f32)],
        scratch_shapes=[pltpu.VMEM((D_STATE, D_SSM), f32), pltpu.VMEM((8 + CHUNK, D_CONV), f32),
                        pltpu.VMEM((CHUNK, D_SSM), f32), pltpu.VMEM((LANES, 2 * D_SSM), bf16)],
        compiler_params=pltpu.CompilerParams(dimension_semantics=("arbitrary",)),
    )(proj, proj, proj, proj, proj, proj, conv_w, conv_b, dtb16, alog16, alog_f, d_f, nw)


def _outproj_loss(mix_a, mix_s, wo, x, tgt, npw):
    s, d = x.shape
    tm = 512

    def body(ma_ref, ms_ref, wo_ref, x_ref, t_ref, npw_ref, dmix_ref, dres_ref, acc_ref, dwo_ref):
        @pl.when(pl.program_id(0) == 0)
        def _():
            acc_ref[...] = jnp.zeros_like(acc_ref)
            dwo_ref[...] = jnp.zeros_like(dwo_ref)

        out = _nn(ma_ref[...], wo_ref[0:D_ATTN, :]) + _nn(ms_ref[...], wo_ref[D_ATTN:, :])
        r = lax.rsqrt(jnp.mean(out * out, axis=-1, keepdims=True) + EPS)
        on = out * r
        diff = x_ref[...] + on * npw_ref[...] - t_ref[...]
        dres = diff * (1.0 / d)
        dres_ref[...] = dres
        acc_ref[0:1, :] += jnp.sum(diff * diff, axis=0, keepdims=True)
        acc_ref[1:2, :] += jnp.sum(dres * on, axis=0, keepdims=True)
        dn = dres * npw_ref[...]
        dout = (r * (dn - on * jnp.mean(dn * on, axis=-1, keepdims=True))).astype(bf16)
        dmix_ref[...] = _nt(dout, wo_ref[...])
        dwo_ref[0:D_ATTN, :] += _tn(ma_ref[...], dout)
        dwo_ref[D_ATTN:, :] += _tn(ms_ref[...], dout)

    row = lambda w: pl.BlockSpec((tm, w), lambda i: (i, 0))
    return pl.pallas_call(
        body, name="outproj_loss", grid=(s // tm,),
        in_specs=[row(D_ATTN), row(D_SSM), _full((D_ATTN + D_SSM, d)), row(d), row(d), _full((1, d))],
        out_specs=[row(D_ATTN + D_SSM), row(d), _full((8, d)), _full((D_ATTN + D_SSM, d))],
        out_shape=[SDS((s, D_ATTN + D_SSM), f32), SDS((s, d), f32), SDS((8, d), f32), SDS((D_ATTN + D_SSM, d), f32)],
        compiler_params=pltpu.CompilerParams(dimension_semantics=("arbitrary",)),
    )(mix_a, mix_s, wo, x, tgt, npw)


def _attn_bwd(proj, qkv, o, lb, dmix, swap=None):
    s = proj.shape[0]
    n_it = s // BLK

    nsw = 0 if swap is None else 1

    def body(*refs):
        q_ref, k_ref, v_ref, g_ref, o_ref, l_ref, dm_ref = refs[:7]
        swap_in = refs[7:7 + nsw]
        dq_ref, dk_ref, dv_ref, dg_ref = refs[7 + nsw:11 + nsw]
        swap_out = refs[11 + nsw:11 + 2 * nsw]
        dq_acc, dk_acc, dv_acc, do_scr, dl_scr = refs[11 + 2 * nsw:16 + 2 * nsw]
        bufs = refs[16 + 2 * nsw:44 + 2 * nsw]
        stage_a, stage_b = refs[44 + 2 * nsw:46 + 2 * nsw]
        swap_sems = refs[46 + 2 * nsw:]
        head0, tri2_t, _, _, bones = _attn_consts()
        tri2_q = _quarter_mask()

        if nsw:
            x, y, c = _my_pos()
            swap_copy = pltpu.make_async_remote_copy(
                src_ref=swap_in[0], dst_ref=swap_out[0], send_sem=swap_sems[0], recv_sem=swap_sems[1],
                device_id=(x, y, 1 - c), device_id_type=MESH)

            @pl.when(pl.program_id(0) == 0)
            def _():
                swap_copy.start()

        quarter_rows, load, add = _quarter_rows, _load_runs, _add_runs

        def pro(i, carry):
            for t in range(4):
                rows = pl.ds(pl.multiple_of(i * 1024 + t * 256, 256), 256)
                g = g_ref[rows, :]
                sg = _sigmoid(g)
                dmx = dm_ref[rows, :]
                ov = o_ref[rows, :]
                dg_ref[rows, :] = (dmx * ov * (sg * (1.0 + g * (1.0 - sg)))).astype(bf16)
                do = dmx * (g * sg)
                stage_a[t * 256:(t + 1) * 256, :] = do
                stage_b[t * 256:(t + 1) * 256, :] = _split_dot_sum(do * ov, bones)
            z = jnp.zeros((256, LANES), f32)
            for q in range(4):
                rows = quarter_rows(i, q)
                do_scr[rows, :] = stage_a[pl.ds(q, 256, stride=4), :]
                dl_scr[rows, :] = stage_b[pl.ds(q, 256, stride=4), :]
                dq_acc[rows, :] = z
                dk_acc[rows, :] = z
                dv_acc[rows, :] = z
            return carry

        lax.fori_loop(0, s // 1024, pro, 0, unroll=True)

        def per_head(t):
            return jnp.concatenate([t[:, :LANES], t[:, LANES:]], axis=0)

        def both_heads(t):
            tr = pltpu.roll(t, HEAD_DIM, 1)
            return jnp.concatenate([jnp.where(head0, t, tr), jnp.where(head0, tr, t)], axis=1)

        mm_bufs = ((bufs[0], bufs[1], bufs[2], bufs[3]), (bufs[4], bufs[5], bufs[6], bufs[7]))
        ds_bufs = ((bufs[8], bufs[9], bufs[10], bufs[11]), (bufs[12], bufs[13], bufs[14], bufs[15]))
        op_bufs = ((bufs[16], bufs[17], bufs[18], bufs[19]), (bufs[20], bufs[21], bufs[22], bufs[23]))
        vc_bufs, carry_k, carry_v = (bufs[24], bufs[25]), bufs[26], bufs[27]
        for buf in (op_bufs[0][0], op_bufs[1][0]) + vc_bufs:
            buf[...] = jnp.zeros_like(buf)

        def block_rows(i, d, nb):
            rows, has_prev = _quarter_block(i, d, nb)
            return rows, rows, has_prev

        def unstack(st16):
            return st16[:BLK] + st16[BLK:]

        def products(i, par, d, nb):
            src, scr, has_prev = block_rows(i, d, nb)
            tri2 = tri2_q if d == 1 else tri2_t
            s_buf, dp_buf, sd_buf, dpd_buf = mm_bufs[par]
            kc_buf, kp_buf, q_buf, do_buf = op_bufs[par]
            q = load(q_ref, src)
            qs = q * 0.125
            do = load(do_scr, scr)
            qs16, do16 = qs.astype(bf16), do.astype(bf16)
            kst_c = _stack_heads(load(k_ref, src).astype(bf16), head0)
            vst_c = _stack_heads(load(v_ref, src).astype(bf16), head0)
            kst_p, vst_p = op_bufs[1 - par][0][...], vc_bufs[1 - par][...]
            kc_buf[...] = kst_c
            kp_buf[...] = kst_p
            vc_bufs[par][...] = vst_c
            q_buf[...] = q.astype(bf16)
            do_buf[...] = do16
            s_buf[...] = jnp.where(tri2, _nt(qs16, kst_c), jnp.where(has_prev, _nt(qs16, kst_p), -jnp.inf))
            dp_buf[...] = jnp.where(tri2, _nt(do16, vst_c), jnp.where(has_prev, _nt(do16, vst_p), 0.0))
            sd_buf[...] = _nn((qs * unstack(kst_p).astype(f32)).astype(bf16), bones)
            dpd_buf[...] = jnp.where(has_prev, _nn((do * unstack(vst_p).astype(f32)).astype(bf16), bones), 0.0)

        def softmax_grad(i, par, d, nb):
            src, scr, has_prev = block_rows(i, d, nb)
            s_buf, dp_buf, sd_buf, dpd_buf = mm_bufs[par]
            p_buf, ds_buf, pd_buf, dsd_buf = ds_bufs[par]
            lse = load(l_ref, src)
            dl = load(dl_scr, scr)
            pt = jnp.exp(s_buf[...] - both_heads(lse))
            ds_buf[...] = (pt * (dp_buf[...] - both_heads(dl)) * 0.125).astype(bf16)
            p_buf[...] = pt.astype(bf16)
            pd = jnp.where(has_prev, jnp.exp(sd_buf[...] - lse), 0.0)
            pd_buf[...] = pd
            dsd_buf[...] = pd * (dpd_buf[...] - dl) * 0.125

        def accumulate(i, par, d, nb):
            _, rows, _ = block_rows(i, d, nb)
            _, before, _ = block_rows(jnp.maximum(i - 1, 0), d, nb)
            tri2 = tri2_q if d == 1 else tri2_t
            p_buf, ds_buf, pd_buf, dsd_buf = ds_bufs[par]
            kc_buf, kp_buf, q_buf, do_buf = op_bufs[par]
            pt16, ds16, pd, dsd = p_buf[...], ds_buf[...], pd_buf[...], dsd_buf[...]
            zero = jnp.zeros_like(pt16)
            dsc, dsp = jnp.where(tri2, ds16, zero), jnp.where(tri2, zero, ds16)
            pc, pp = jnp.where(tri2, pt16, zero), jnp.where(tri2, zero, pt16)
            kst_c, kst_p, q16, do16 = kc_buf[...], kp_buf[...], q_buf[...], do_buf[...]
            qst, dost = _stack_heads(q16, head0), _stack_heads(do16, head0)
            add(dq_acc, rows, _nn(dsc, kst_c) + _nn(dsp, kst_p) + dsd * unstack(kst_p).astype(f32))
            dk2 = _tn(jnp.concatenate([per_head(dsc), per_head(dsp)], axis=1), qst)
            dv2 = _tn(jnp.concatenate([per_head(pc), per_head(pp)], axis=1), dost)
            add(dk_acc, before, carry_k[...] + dk2[BLK:] + dsd * q16.astype(f32))
            add(dv_acc, before, carry_v[...] + dv2[BLK:] + pd * do16.astype(f32))
            carry_k[...] = dk2[:BLK]
            carry_v[...] = dv2[:BLK]

        for d in DILATIONS:
            nb = s // (BLK * d)
            carry_k[...] = jnp.zeros_like(carry_k)
            carry_v[...] = jnp.zeros_like(carry_v)
            products(0, 0, d, nb)
            products(1, 1, d, nb)
            softmax_grad(0, 0, d, nb)

            def steps(j, carry, d=d, nb=nb):
                for par in range(2):
                    t = 2 * j + 2 + par
                    accumulate(t - 2, par, d, nb)
                    products(t, par, d, nb)
                    softmax_grad(t - 1, 1 - par, d, nb)
                return carry

            lax.fori_loop(0, (n_it - 2) // 2, steps, 0, unroll=True)
            accumulate(n_it - 2, 0, d, nb)
            softmax_grad(n_it - 1, 1, d, nb)
            accumulate(n_it - 1, 1, d, nb)
            _, last, _ = block_rows(n_it - 1, d, nb)
            add(dk_acc, last, carry_k[...])
            add(dv_acc, last, carry_v[...])

        def epi(i, carry):
            rows = pl.ds(pl.multiple_of(i * 1024, 1024), 1024)
            for acc, out, stage in ((dq_acc, dq_ref, stage_a), (dk_acc, dk_ref, stage_b), (dv_acc, dv_ref, stage_a)):
                for q in range(4):
                    stage[pl.ds(q, 256, stride=4), :] = acc[quarter_rows(i, q), :]
                out[rows, :] = stage[...].astype(bf16)
            return carry

        lax.fori_loop(0, s // 1024, epi, 0, unroll=True)

        if nsw:
            @pl.when(pl.program_id(0) == N_PAIRS - 1)
            def _():
                swap_copy.wait_send()
                swap_copy.wait_recv()

    col = lambda base: pl.BlockSpec((s, LANES), lambda h: (0, base + h))
    anyspec = pl.BlockSpec(memory_space=pl.ANY)
    swaps = [] if swap is None else [swap]
    outs = pl.pallas_call(
        body, name="attn_bwd", grid=(N_PAIRS,),
        in_specs=[col(0), col(0), col(0), col(24), col(0), col(0), col(0)] + [anyspec] * nsw,
        out_specs=[col(0)] * 4 + [anyspec] * nsw,
        out_shape=[SDS((s, D_ATTN), bf16)] * 4 + [SDS(a.shape, a.dtype) for a in swaps],
        scratch_shapes=[pltpu.VMEM((s, LANES), f32)] * 5
        + [pltpu.VMEM((BLK, 2 * LANES), f32)] * 2 + [pltpu.VMEM((BLK, LANES), f32)] * 2
        + [pltpu.VMEM((BLK, 2 * LANES), f32)] * 2 + [pltpu.VMEM((BLK, LANES), f32)] * 2
        + [pltpu.VMEM((BLK, 2 * LANES), bf16)] * 2 + [pltpu.VMEM((BLK, LANES), f32)] * 2
        + [pltpu.VMEM((BLK, 2 * LANES), bf16)] * 2 + [pltpu.VMEM((BLK, LANES), f32)] * 2
        + [pltpu.VMEM((2 * BLK, LANES), bf16)] * 2 + [pltpu.VMEM((BLK, LANES), bf16)] * 2
        + [pltpu.VMEM((2 * BLK, LANES), bf16)] * 2 + [pltpu.VMEM((BLK, LANES), bf16)] * 2
        + [pltpu.VMEM((2 * BLK, LANES), bf16)] * 2 + [pltpu.VMEM((BLK, LANES), f32)] * 2
        + [pltpu.VMEM((1024, LANES), f32)] * 2
        + [pltpu.SemaphoreType.DMA(())] * (2 * nsw),
        compiler_params=pltpu.CompilerParams(dimension_semantics=("arbitrary",)),
    )(*qkv, proj, o, lb, dmix, *swaps)
    return outs


def _ssd_bwd(proj, y, states, cv, dmix, conv_w, conv_b, dtb16, alog16, alog_f, d_f, nw, chip_sums=()):
    s = proj.shape[0]
    nc = s // CHUNK
    gw = D_SSM // N_GROUPS
    nx = len(chip_sums)

    def body(*refs):
        (xs_ref, bc_ref, _, _, dt_ref, z_ref, y_ref, st_ref, dm_ref, cw_ref, cb_ref, dtb_ref,
         alog16_ref, alogf_ref, df_ref, nw_ref, cv_ref) = refs[:17]
        cs_in = refs[17:17 + nx]
        out_ref, gconv_ref, gvec_ref, gdt_ref = refs[17 + nx:21 + nx]
        cs_out = refs[21 + nx:21 + 2 * nx]
        (dh_scr, head_scr, dcpad, da_scr, dxdt_scr, dbc_scr, emat_ref, fold_ref) = refs[21 + 2 * nx:29 + 2 * nx]
        cs_sems = refs[29 + 2 * nx:]
        i = pl.program_id(0)
        c = nc - 1 - i

        if nx:
            @pl.when(i == 0)
            def _():
                mine, sends, _ = _chip_exchange_copies(cs_in, cs_out, *cs_sems)
                for cp in mine + sends:
                    cp.start()

            @pl.when(i == nc - 1)
            def _():
                mine, sends, recvs = _chip_exchange_copies(cs_in, cs_out, *cs_sems)
                for cp in recvs:
                    cp.wait_recv()
                for cp in sends:
                    cp.wait_send()
                for cp in mine:
                    cp.wait()

        @pl.when(i == 0)
        def _():
            emat_ref[...] = _expand_mat()
            fold_ref[...] = _fold_mat()
            dh_scr[...] = jnp.zeros_like(dh_scr)
            head_scr[...] = jnp.zeros_like(head_scr)
            gconv_ref[...] = jnp.zeros_like(gconv_ref)
            gvec_ref[...] = jnp.zeros_like(gvec_ref)
            gdt_ref[...] = jnp.zeros_like(gdt_ref)

        cv = cv_ref[...]
        sig = _sigmoid(cv)
        xbc = cv * sig
        pre, dt_f, al_f, al_x, al_t = _decay_terms(dt_ref, dtb_ref, alog16_ref, emat_ref)
        head0 = _iota((CHUNK, LANES), 1) < HEAD_DIM
        sub = _iota((CHUNK, LANES), 0)
        last_row = sub == CHUNK - 1

        yv, z, dmx = y_ref[...], z_ref[...], dm_ref[...]
        sz = _sigmoid(z)
        silu = z * sz
        yz = yv * silu
        dyz_parts = []
        for g in range(N_GROUPS):
            gs = slice(g * gw, (g + 1) * gw)
            part = yz[:, gs]
            r = lax.rsqrt(jnp.mean(part * part, axis=-1, keepdims=True) + EPS)
            nh = part * r
            gvec_ref[0:1, gs] += jnp.sum(dmx[:, gs] * nh, axis=0, keepdims=True)
            dn = dmx[:, gs] * nw_ref[:, gs]
            dyz_parts.append(r * (dn - nh * jnp.mean(dn * nh, axis=-1, keepdims=True)))
        dyz = jnp.concatenate(dyz_parts, axis=1)
        dy = dyz * silu
        out_ref[:, 0:D_SSM] = (dyz * yv * (sz * (1.0 + z * (1.0 - sz)))).astype(bf16)

        x_all = xbc[:, 0:D_SSM]
        gvec_ref[2:3, :] += jnp.sum(dy * x_all, axis=0, keepdims=True)

        for g in range(N_GROUPS):
            bm = xbc[:, D_SSM + g * D_STATE: D_SSM + (g + 1) * D_STATE].astype(bf16)
            cm = xbc[:, D_SSM + (N_GROUPS + g) * D_STATE: D_SSM + (N_GROUPS + g + 1) * D_STATE].astype(bf16)
            gmat = _nt(cm, bm)
            dgm = jnp.zeros((CHUNK, CHUNK), f32)
            db = jnp.zeros((CHUNK, D_STATE), f32)
            dc = jnp.zeros((CHUNK, D_STATE), f32)
            for pair in range(4 * g, 4 * g + 4):
                sl = slice(pair * LANES, (pair + 1) * LANES)
                xp, dtp, alp, dyp = x_all[:, sl], dt_f[:, sl], al_f[:, sl], dy[:, sl]
                xdt = xp * dtp
                xdt16 = xdt.astype(bf16)
                al_last = alp[CHUNK - 1:CHUNK, :]
                e_l = jnp.exp(alp)
                wf = jnp.exp(al_last - alp)
                e_last = jnp.exp(al_last)
                hp = st_ref[:, sl]
                hp16 = hp.astype(bf16)
                dhn = dh_scr[:, sl]
                dhn16 = dhn.astype(bf16)
                y_off = e_l * _nn(cm, hp16)
                dch16 = (dyp * e_l).astype(bf16)
                dc = dc + _nt(dch16, hp16)
                dh_out = _tn(cm, dch16)
                dal = dyp * y_off
                xw16 = (wf * xdt).astype(bf16)
                db = db + _nt(xw16, dhn16)
                dxw = _nn(bm, dhn16)
                dxdt = dxw * wf
                dwf = dxw * xdt * wf
                dal = dal - dwf
                dal_last = jnp.sum(dwf, axis=0, keepdims=True) + jnp.sum(dhn * hp, axis=0, keepdims=True) * e_last
                dh_scr[:, sl] = e_last * dhn + dh_out
                for h in range(2):
                    mh = head0 if h == 0 else jnp.logical_not(head0)
                    dyh16 = jnp.where(mh, dyp, 0.0).astype(bf16)
                    lmat = _decay_mat(al_x, al_t, pair, h)
                    mm = gmat * lmat
                    dmm = _nt(dyh16, xdt16)
                    dxdt = dxdt + _tn(mm.astype(bf16), dyh16)
                    n16 = (dmm * mm).astype(bf16)
                    jh = jnp.where(mh, 1.0 / HEAD_DIM, 0.0).astype(bf16)
                    dal = dal + _nn(n16, jh) - _tn(n16, jh)
                    dgm = dgm + dmm * lmat
                da_scr[:, sl] = dal + jnp.where(last_row, dal_last, 0.0)
                dxdt_scr[:, sl] = dxdt
            dgm16 = dgm.astype(bf16)
            dbc_scr[:, g * D_STATE:(g + 1) * D_STATE] = db + _tn(dgm16, cm)
            dbc_scr[:, (N_GROUPS + g) * D_STATE:(N_GROUPS + g + 1) * D_STATE] = dc + _nn(dgm16, bm)

        sub_c, lane_c = _iota((CHUNK, CHUNK), 0), _iota((CHUNK, CHUNK), 1)
        tri_t = (lane_c >= sub_c).astype(bf16)
        dadt = _dot_01_left(tri_t, da_scr[...], 2)
        a_f = -jnp.exp(alogf_ref[...])
        dxdt_all = dxdt_scr[...]
        ddt_f = dxdt_all * x_all + a_f * dadt
        gvec_ref[1:2, :] += jnp.sum(dt_f * dadt, axis=0, keepdims=True) * a_f
        dx = df_ref[...] * dy + dxdt_all * dt_f
        ddt_raw = _dot_01(ddt_f, fold_ref[...], 2) * _sigmoid(pre)
        gdt_ref[0:1, :] += jnp.sum(ddt_raw, axis=0, keepdims=True)
        out_ref[:, D_SSM + D_CONV:D_SSM + D_CONV + LANES] = ddt_raw.astype(bf16)
        out_ref[:, D_SSM + D_CONV + LANES:] = jnp.zeros((CHUNK, 3 * LANES), bf16)

        dsil = sig * (1.0 + cv * (1.0 - sig))
        dcv_x = dx * dsil[:, 0:D_SSM]
        dcv_bc = dbc_scr[...] * dsil[:, D_SSM:]
        dcpad[0:CHUNK, 0:D_SSM] = dcv_x
        dcpad[0:CHUNK, D_SSM:] = dcv_bc
        dcpad[CHUNK:, :] = head_scr[...]
        dcp = dcpad[...]
        dcv = dcp[0:CHUNK]
        gconv_ref[4:5, :] += jnp.sum(dcv, axis=0, keepdims=True)
        x_raw = jnp.concatenate([xs_ref[...], bc_ref[...]], axis=1)
        draw = cw_ref[3:4, :] * dcv
        gconv_ref[3:4, :] += jnp.sum(dcv * x_raw, axis=0, keepdims=True)
        for j in range(3):
            ahead = pltpu.roll(dcp, CHUNK + 8 - (3 - j), 0)[0:CHUNK]
            draw = draw + cw_ref[j:j + 1, :] * ahead
            gconv_ref[j:j + 1, :] += jnp.sum(ahead * x_raw, axis=0, keepdims=True)
        head_scr[...] = dcv[0:8]
        out_ref[:, D_SSM:D_SSM + D_CONV] = draw.astype(bf16)

    order = lambda i: nc - 1 - i
    row = lambda w, cb=0: pl.BlockSpec((CHUNK, w), lambda i: (nc - 1 - i, cb))
    anyspec = pl.BlockSpec(memory_space=pl.ANY)
    outs = pl.pallas_call(
        body, name="ssd_bwd", grid=(nc,),
        in_specs=_ssd_in_specs(order) + [row(D_SSM), pl.BlockSpec((None, D_STATE, D_SSM), lambda i: (nc - 1 - i, 0, 0)),
                                         row(D_SSM, 1), _full((4, D_CONV)), _full((1, D_CONV)), _full((1, LANES)),
                                         _full((1, LANES)), _full((1, D_SSM)), _full((1, D_SSM)), _full((1, D_SSM)),
                                         row(D_CONV)]
        + [anyspec] * nx,
        out_specs=[row(3072), _full((8, D_CONV)), _full((8, D_SSM)), _full((8, LANES))] + [anyspec] * nx,
        out_shape=[SDS((s, 3072), bf16), SDS((8, D_CONV), f32), SDS((8, D_SSM), f32), SDS((8, LANES), f32)]
        + [SDS(a.shape, a.dtype) for a in chip_sums],
        scratch_shapes=[pltpu.VMEM((D_STATE, D_SSM), f32), pltpu.VMEM((8, D_CONV), f32),
                        pltpu.VMEM((8 + CHUNK, D_CONV), f32),
                        pltpu.VMEM((CHUNK, D_SSM), f32), pltpu.VMEM((CHUNK, D_SSM), f32),
                        pltpu.VMEM((CHUNK, 2 * N_GROUPS * D_STATE), f32),
                        pltpu.VMEM((LANES, 2 * D_SSM), bf16), pltpu.VMEM((D_SSM, LANES), bf16)]
        + (_chip_exchange_scratch(nx) if nx else []),
        compiler_params=pltpu.CompilerParams(dimension_semantics=("arbitrary",)),
    )(proj, proj, proj, proj, proj, proj, y, states, dmix, conv_w, conv_b, dtb16, alog16, alog_f, d_f, nw, cv,
      *chip_sums)
    return outs[0], outs[1], outs[2], outs[3], outs[4:]


def _col_blocks(parts, tile):
    counts = [p.shape[1] // tile for p in parts]
    offs = [sum(counts[:t]) for t in range(len(parts))]
    return offs, counts, sum(counts)


def _bcast_copies(src_ref, out_ref, send_sems, recv_sems, local_sem):
    x, y, c = _my_pos()
    me = 4 * x + 2 * y + c
    mine = pltpu.make_async_copy(src_ref, out_ref.at[me], local_sem)
    sends, recvs = [], []
    for k in range(1, N_DEV):
        to, frm = (me + k) % N_DEV, (me + N_DEV - k) % N_DEV
        sems = dict(send_sem=send_sems.at[k - 1], recv_sem=recv_sems.at[k - 1], device_id_type=MESH)
        sends.append(pltpu.make_async_remote_copy(
            src_ref=src_ref, dst_ref=out_ref.at[me], device_id=(to // 4, (to // 2) % 2, to % 2), **sems))
        recvs.append(pltpu.make_async_remote_copy(
            src_ref=src_ref, dst_ref=out_ref.at[frm], device_id=(x, y, c), **sems))
    return mine, sends, recvs


def _bcast_scratch():
    return [pltpu.SemaphoreType.DMA((N_DEV - 1,)), pltpu.SemaphoreType.DMA((N_DEV - 1,)), pltpu.SemaphoreType.DMA(())]


def _inproj_bwd(dparts, wt, x, nw, dres, chip_sums=(), pack=None):
    s, d = x.shape
    tm, tk = 1024, 1024
    offs, counts, nk = _col_blocks(dparts, tk)
    npart, nx = len(dparts), len(chip_sums)
    npk = 0 if pack is None else 1
    ni = s // tm

    def body(*refs):
        dp_refs = refs[:npart]
        w_ref, x_ref, nw_ref, dres_ref = refs[npart:npart + 4]
        pos = npart + 4
        cs_in, pos = refs[pos:pos + nx], pos + nx
        pack_in, pos = refs[pos:pos + npk], pos + npk
        (gx_ref, gnw_ref), pos = refs[pos:pos + 2], pos + 2
        cs_out, pos = refs[pos:pos + nx], pos + nx
        pack_out, pos = refs[pos:pos + 2 * npk], pos + 2 * npk
        acc, pos = refs[pos], pos + 1
        cs_sems, pos = refs[pos:pos + 3 * min(nx, 1)], pos + 3 * min(nx, 1)
        pk_refs = refs[pos:]
        i, k = pl.program_id(0), pl.program_id(1)

        def exchange():
            return _chip_exchange_copies(cs_in, cs_out, *cs_sems)

        def pack_copies():
            return _bcast_copies(pack_in[0], pack_out[0], *pk_refs[1:4])

        def gnw_copies():
            return _bcast_copies(pk_refs[0], pack_out[1], *pk_refs[4:7])

        @pl.when(jnp.logical_and(i == 0, k == 0))
        def _():
            gnw_ref[...] = jnp.zeros_like(gnw_ref)
            if nx:
                mine, sends, _ = exchange()
                for cp in mine + sends:
                    cp.start()
            if npk:
                mine, sends, _ = pack_copies()
                for cp in [mine] + sends:
                    cp.start()

        @pl.when(k == 0)
        def _():
            acc[...] = _nn(dp_refs[0][...], w_ref[...])

        for t in range(npart):
            @pl.when(jnp.logical_and(k >= max(offs[t], 1), k < offs[t] + counts[t]))
            def _(t=t):
                acc[...] += _nn(dp_refs[t][...], w_ref[...])

        @pl.when(k == nk - 1)
        def _():
            xv = x_ref[...]
            r = lax.rsqrt(jnp.mean(xv * xv, axis=-1, keepdims=True) + EPS)
            xn = xv * r
            du = acc[...]
            gnw_ref[0:1, :] += jnp.sum(du * xn, axis=0, keepdims=True)
            dn = du * nw_ref[...]
            gx_ref[...] = dres_ref[...] + r * (dn - xn * jnp.mean(dn * xn, axis=-1, keepdims=True))

        @pl.when(jnp.logical_and(i == ni - 1, k == nk - 1))
        def _():
            if npk:
                pk_refs[0][...] = gnw_ref[...]
                mine, sends, _ = gnw_copies()
                for cp in [mine] + sends:
                    cp.start()
            if nx:
                mine, sends, recvs = exchange()
                for cp in recvs:
                    cp.wait_recv()
                for cp in sends:
                    cp.wait_send()
                for cp in mine:
                    cp.wait()
            if npk:
                for copies in (pack_copies(), gnw_copies()):
                    mine, sends, recvs = copies
                    for cp in recvs:
                        cp.wait_recv()
                    for cp in sends:
                        cp.wait_send()
                    mine.wait()

    def piece(t):
        return pl.BlockSpec((tm, tk), lambda i, k: (i, jnp.clip(k - offs[t], 0, counts[t] - 1)))

    anyspec = pl.BlockSpec(memory_space=pl.ANY)
    packs = [] if pack is None else [pack]
    pack_shapes = [] if pack is None else [SDS((N_DEV,) + pack.shape, f32), SDS((N_DEV, 8, d), f32)]
    scratch = [pltpu.VMEM((tm, d), f32)] + (_chip_exchange_scratch(nx) if nx else [])
    if npk:
        scratch += [pltpu.VMEM((8, d), f32)] + _bcast_scratch() + _bcast_scratch()
    outs = pl.pallas_call(
        body, name="inproj_bwd", grid=(ni, nk),
        in_specs=[piece(t) for t in range(npart)] + [
            pl.BlockSpec((tk, d), lambda i, k: (k, 0)),
            pl.BlockSpec((tm, d), lambda i, k: (i, 0)), pl.BlockSpec((1, d), lambda i, k: (0, 0)),
            pl.BlockSpec((tm, d), lambda i, k: (i, 0))] + [anyspec] * (nx + npk),
        out_specs=[pl.BlockSpec((tm, d), lambda i, k: (i, 0)), pl.BlockSpec((8, d), lambda i, k: (0, 0))]
        + [anyspec] * (nx + 2 * npk),
        out_shape=[SDS((s, d), f32), SDS((8, d), f32)] + [SDS(a.shape, a.dtype) for a in chip_sums] + pack_shapes,
        scratch_shapes=scratch,
        compiler_params=pltpu.CompilerParams(dimension_semantics=("arbitrary", "arbitrary")),
    )(*dparts, wt, x, nw, dres, *chip_sums, *packs)
    return outs[0], outs[1], outs[2:2 + nx], outs[2 + nx:]


def _matmul_tn(a_parts, b_parts, name):
    tile, tk = 1024, 1024
    s = a_parts[0].shape[0]
    nk = s // tk
    na, nb = len(a_parts), len(b_parts)
    offs_a, counts_a, ni = _col_blocks(a_parts, tile)
    offs_b, counts_b, nj = _col_blocks(b_parts, tile)

    def body(*refs):
        a_refs, b_refs, o_ref = refs[:na], refs[na:na + nb], refs[na + nb]
        i, j = pl.program_id(0), pl.program_id(1)

        @pl.when(pl.program_id(2) == 0)
        def _():
            o_ref[...] = jnp.zeros_like(o_ref)

        for ta in range(na):
            for tb in range(nb):
                in_a = jnp.logical_and(i >= offs_a[ta], i < offs_a[ta] + counts_a[ta])
                in_b = jnp.logical_and(j >= offs_b[tb], j < offs_b[tb] + counts_b[tb])

                @pl.when(jnp.logical_and(in_a, in_b))
                def _(ta=ta, tb=tb):
                    o_ref[...] += _tn(a_refs[ta][...], b_refs[tb][...])

    def spec(offs, counts, t, axis):
        def index(i, j, k):
            pos = (i, j)[axis]
            mine = jnp.logical_and(pos >= offs[t], pos < offs[t] + counts[t])
            return jnp.where(mine, k, 0), jnp.clip(pos - offs[t], 0, counts[t] - 1)
        return pl.BlockSpec((tk, tile), index)

    return pl.pallas_call(
        body, name=name, grid=(ni, nj, nk),
        in_specs=[spec(offs_a, counts_a, t, 0) for t in range(na)] + [spec(offs_b, counts_b, t, 1) for t in range(nb)],
        out_specs=pl.BlockSpec((tile, tile), lambda i, j, k: (i, j)),
        out_shape=SDS((ni * tile, nj * tile), f32),
        compiler_params=pltpu.CompilerParams(dimension_semantics=("parallel", "parallel", "arbitrary")),
    )(*a_parts, *b_parts)


def _adamw(w, g, m, v):
    m = ADAM_B1 * m + (1.0 - ADAM_B1) * g
    v = ADAM_B2 * v + (1.0 - ADAM_B2) * (g * g)
    m_hat = m / (1.0 - ADAM_B1 ** ADAM_STEP)
    v_hat = v / (1.0 - ADAM_B2 ** ADAM_STEP)
    delta = -ADAM_LR * (m_hat / (jnp.sqrt(v_hat) + ADAM_EPS) + ADAM_WD * w)
    return delta, m, v


def _sum_adamw(parts, w, m, v, name):
    r, c = w.shape
    tc = 256

    def body(p_ref, w_ref, m_ref, v_ref, g_ref, d_ref, nm_ref, nv_ref):
        g = p_ref[0].astype(f32)
        for q in range(1, 4):
            g = g + p_ref[q].astype(f32)
        g_ref[...] = g
        d_ref[...], nm_ref[...], nv_ref[...] = _adamw(w_ref[...], g, m_ref[...], v_ref[...])

    blk = pl.BlockSpec((r, tc), lambda i: (0, i))
    return pl.pallas_call(
        body, name=name, grid=(c // tc,),
        in_specs=[pl.BlockSpec((4, r, tc), lambda i: (0, 0, i)), blk, blk, blk],
        out_specs=[blk] * 4, out_shape=[SDS((r, c), f32)] * 4,
        compiler_params=pltpu.CompilerParams(dimension_semantics=("parallel",)),
    )(parts, w, m, v)


def _sum_small(parts, pre_blocks):
    def body(p_ref, b_ref, o_ref):
        t = p_ref[0]
        pre = b_ref[0]
        for j in range(1, N_DEV):
            t = t + p_ref[j]
            pre = pre + b_ref[j]
        o_ref[...] = t
        o_ref[5:6, 0:D_MODEL] = pre[0:1, :]
        row_h = _iota((D_SSM, LANES), 0) // HEAD_DIM
        fold = (row_h == _iota((D_SSM, LANES), 1)).astype(f32)
        lower = t[8:16, 0:LANES]
        folded = _nn_hi(t[8:16, 0:D_SSM], fold)
        loss = jnp.sum(t[11:12, 0:D_MODEL], axis=1, keepdims=True) * (0.5 / D_MODEL)
        row = _iota((8, LANES), 0)
        o_ref[8:16, 0:LANES] = jnp.where(row < 2, folded, jnp.where(row == 4, loss, lower))

    return pl.pallas_call(body, name="sum_small", out_shape=SDS((PACK_ROWS, PACK_W), f32),
                          in_specs=[pl.BlockSpec(memory_space=pltpu.VMEM)] * 2,
                          out_specs=pl.BlockSpec(memory_space=pltpu.VMEM))(parts, pre_blocks)


def _adamw_small(w, g, m, v):
    def body(w_ref, g_ref, m_ref, v_ref, d_ref, nm_ref, nv_ref):
        d_ref[...], nm_ref[...], nv_ref[...] = _adamw(w_ref[...], g_ref[...], m_ref[...], v_ref[...])

    vm = pl.BlockSpec(memory_space=pltpu.VMEM)
    return pl.pallas_call(body, name="adamw_small", out_shape=[SDS(w.shape, f32)] * 3,
                          in_specs=[vm] * 4, out_specs=[vm] * 3)(w, g, m, v)


def _pad_lanes(v, width):
    return jnp.pad(v, ((0, 0), (0, width - v.shape[1])))


def _local_step(x, tgt, norm_pre_w, wt, conv_w, conv_b, dt_bias, a_log, d_skip, ssm_norm_w, wo, norm_post_w, sharded):
    dtb16 = _pad_lanes(dt_bias, LANES)
    alog16 = _pad_lanes(a_log, LANES)
    alog_f = jnp.repeat(a_log, HEAD_DIM, axis=1)
    d_f = jnp.repeat(d_skip, HEAD_DIM, axis=1)

    shard_out = wo.shape[0]
    if sharded:
        proj, u, (g_out, g_cw) = _prenorm_inproj(x, norm_pre_w, wt, gather=(wo, conv_w))
        wo = g_out.reshape(N_DEV * shard_out, D_MODEL)
        conv_w = g_cw.transpose(1, 0, 2).reshape(4, D_CONV)
    else:
        proj, u, _ = _prenorm_inproj(x, norm_pre_w, wt)
    o, lb, mix_a, *qkv = _attn_fwd(proj)
    mix_s, y, states, cv = _ssd_fwd(proj, conv_w, conv_b, dtb16, alog16, alog_f, d_f, ssm_norm_w)
    dmix, dres, acc_post, dw_out = _outproj_loss(mix_a, mix_s, wo, x, tgt, norm_post_w)
    ssd_args = (proj, y, states, cv, dmix, conv_w, conv_b, dtb16, alog16, alog_f, d_f, ssm_norm_w)
    if sharded:
        dq, dk, dv, dg, got_out = _attn_bwd(proj, qkv, o, lb, dmix, swap=dw_out)
        chip_out = _chip_sum(dw_out, got_out, shard_out, "chip_sum_w_out")
        dzxd, g_conv, g_vec, g_dt, (parts_out,) = _ssd_bwd(*ssd_args, chip_sums=[chip_out])
    else:
        dq, dk, dv, dg = _attn_bwd(proj, qkv, o, lb, dmix)
        dzxd, g_conv, g_vec, g_dt, _ = _ssd_bwd(*ssd_args)
    dparts = [dq, dk, dv, dg, dzxd]

    def pack(g_pre_row):
        return jnp.concatenate(
            [g_conv[0:5], g_pre_row, _pad_lanes(g_vec[0:1], PACK_W), _pad_lanes(acc_post[1:2], PACK_W),
             _pad_lanes(g_vec[1:3], PACK_W), _pad_lanes(g_dt[0:1], PACK_W), _pad_lanes(acc_post[0:1], PACK_W),
             jnp.zeros((4, PACK_W), f32)], axis=0)

    if sharded:
        dw_in, got_in = _dw_in_swap(dparts, u)
        chip_in = _chip_sum(dw_in, got_in, D_IN_PROJ // N_DEV, "chip_sum_w_in")
        grad_x, _, (parts_in,), small = _inproj_bwd(dparts, wt, x, norm_pre_w, dres, [chip_in],
                                                    pack(jnp.zeros((1, PACK_W), f32)))
        return grad_x, (parts_in, parts_out), small
    dw_in = _matmul_tn(dparts, [u], "dw_in")
    grad_x, g_pre, _, _ = _inproj_bwd(dparts, wt, x, norm_pre_w, dres)
    return grad_x, (dw_in, dw_out), pack(_pad_lanes(g_pre[0:1], PACK_W))


def kernel(x, norm_pre_w, w_in, conv_w, conv_b, dt_bias, a_log, d_skip, ssm_norm_w, w_out, norm_post_w, loss_target, m_norm_pre_w, m_w_in, m_conv_w, m_conv_b, m_dt_bias, m_a_log, m_d_skip, m_ssm_norm_w, m_w_out, m_norm_post_w, v_norm_pre_w, v_w_in, v_conv_w, v_conv_b, v_dt_bias, v_a_log, v_d_skip, v_ssm_norm_w, v_w_out, v_norm_post_w):
    shard_cv = conv_w.shape[2]
    me = 4 * lax.axis_index("x") + 2 * lax.axis_index("y") + lax.axis_index("c")

    g_in, = _all_gather([w_in[0].T.astype(bf16)])
    wt = _assemble_wt(g_in)

    grad_x, (parts_in, parts_out), (parts_small, pre_blocks) = _local_step(
        x[0], loss_target[0], norm_pre_w, wt, conv_w[0], conv_b, dt_bias, a_log, d_skip, ssm_norm_w,
        w_out[0].astype(bf16), norm_post_w, sharded=True)

    g_w_in, d_w_in, nm_w_in, nv_w_in = (a.T for a in _sum_adamw(
        parts_in, w_in[0].T, m_w_in[0].T, v_w_in[0].T, "sum_adamw_w_in"))
    g_w_out, d_w_out, nm_w_out, nv_w_out = _sum_adamw(parts_out, w_out[0], m_w_out[0], v_w_out[0], "sum_adamw_w_out")
    tot = _sum_small(parts_small, pre_blocks)

    g_cw_all = tot[0:4]
    small_g = {
        "conv_w": lax.dynamic_slice(g_cw_all, (0, me * shard_cv), (4, shard_cv)),
        "conv_b": tot[4:5], "norm_pre_w": tot[5:6, :D_MODEL], "ssm_norm_w": tot[6:7, :D_SSM],
        "norm_post_w": tot[7:8, :D_MODEL], "a_log": tot[8:9, :16], "d_skip": tot[9:10, :16], "dt_bias": tot[10:11, :16],
    }
    loss = tot[12, 0]
    small_w = {"conv_w": (conv_w[0], m_conv_w[0], v_conv_w[0]), "conv_b": (conv_b, m_conv_b, v_conv_b),
               "norm_pre_w": (norm_pre_w, m_norm_pre_w, v_norm_pre_w), "ssm_norm_w": (ssm_norm_w, m_ssm_norm_w, v_ssm_norm_w),
               "norm_post_w": (norm_post_w, m_norm_post_w, v_norm_post_w), "a_log": (a_log, m_a_log, v_a_log),
               "d_skip": (d_skip, m_d_skip, v_d_skip), "dt_bias": (dt_bias, m_dt_bias, v_dt_bias)}
    names = list(small_w)
    sizes = [small_g[k].size for k in names]
    tot_size = sum(sizes)
    pad_to = -(-tot_size // 1024) * 1024

    def flat(arrs):
        v = jnp.concatenate([a.reshape(-1) for a in arrs])
        return jnp.pad(v, (0, pad_to - tot_size)).reshape(pad_to // LANES, LANES)

    fw = flat([small_w[k][0] for k in names])
    fg = flat([small_g[k] for k in names])
    fm = flat([small_w[k][1] for k in names])
    fv = jnp.pad(jnp.concatenate([small_w[k][2].reshape(-1) for k in names]), (0, pad_to - tot_size),
                 constant_values=1.0).reshape(pad_to // LANES, LANES)
    fd, fnm, fnv = _adamw_small(fw, fg, fm, fv)

    def unflat(f):
        out, off = {}, 0
        v = f.reshape(-1)
        for k, n in zip(names, sizes):
            out[k] = v[off:off + n].reshape(small_g[k].shape)
            off += n
        return out

    sd, snm, snv = unflat(fd), unflat(fnm), unflat(fnv)
    lead = lambda a: a[None]
    order = ["norm_pre_w", "w_in", "conv_w", "conv_b", "dt_bias", "a_log", "d_skip", "ssm_norm_w", "w_out", "norm_post_w"]
    grads = dict(small_g, w_in=g_w_in, w_out=g_w_out)
    deltas = dict(sd, w_in=d_w_in, w_out=d_w_out)
    new_m = dict(snm, w_in=nm_w_in, w_out=nm_w_out)
    new_v = dict(snv, w_in=nv_w_in, w_out=nv_w_out)

    def shaped(dct, k):
        a = dct[k]
        return lead(a) if k in ("w_in", "w_out", "conv_w") else a

    return (loss, grad_x[None], *[shaped(grads, k) for k in order], *[shaped(deltas, k) for k in order],
            *[shaped(new_m, k) for k in order], *[shaped(new_v, k) for k in order])
```

```python
import jax
import jax.numpy as jnp
from jax import lax
from jax.experimental import pallas as pl
from jax.experimental.pallas import tpu as pltpu

f32, bf16 = jnp.float32, jnp.bfloat16
SDS = jax.ShapeDtypeStruct
HIGHEST = lax.Precision.HIGHEST
MESH = pl.DeviceIdType.MESH

N_DEV = 8
D_MODEL = 1024
D_ATTN = 1024
D_SSM = 1024
HEAD_DIM = 64
N_PAIRS = 8
D_STATE = 128
N_GROUPS = 2
D_CONV = D_SSM + 2 * N_GROUPS * D_STATE
D_IN_PROJ = 4 * D_ATTN + D_SSM + D_CONV + 16
NP = 7168
CHUNK = 128
BLK = 128
DILATIONS = (1, 4, 16)
EPS = 1e-6
LANES = 128
COL_Z, COL_XS, COL_BC, COL_DT = 4096, 5120, 6144, 6656

ADAM_LR, ADAM_B1, ADAM_B2, ADAM_EPS, ADAM_WD, ADAM_STEP = 0.001, 0.9, 0.999, 1e-08, 0.01, 10

PACK_ROWS, PACK_W = 16, 1536


def _nt(a, b):
    return lax.dot_general(a, b, (((1,), (1,)), ((), ())), preferred_element_type=f32)


def _tn(a, b):
    return lax.dot_general(a, b, (((0,), (0,)), ((), ())), preferred_element_type=f32)


def _nn(a, b):
    return jnp.dot(a, b, preferred_element_type=f32)


def _nn_hi(a, b):
    return jnp.dot(a, b, precision=HIGHEST, preferred_element_type=f32)


def _sigmoid(x):
    return 1.0 / (1.0 + jnp.exp(-x))


def _softplus(x):
    return jnp.maximum(x, 0.0) + jnp.log1p(jnp.exp(-jnp.abs(x)))


def _iota(shape, dim):
    return lax.broadcasted_iota(jnp.int32, shape, dim)


def _my_pos():
    return lax.axis_index("x"), lax.axis_index("y"), lax.axis_index("c")


GATHER_SEMS = 9


def _gather_phases(ins, outs, send_sems, recv_sems, local_sems):
    n, ns = len(ins), GATHER_SEMS
    x, y, c = _my_pos()
    me, sibling = (x, y, c), (x, y, 1 - c)
    xn, yn, diag = (1 - x, y), (x, 1 - y), (1 - x, 1 - y)

    def slot(a, px, py, pc):
        return outs[a].at[4 * px + 2 * py + pc]

    def part(a, ref, h):
        width = ins[a].shape[-1]
        if width % (2 * LANES):
            return ref if h == 1 else None
        return ref.at[:, pl.ds(h * (width // 2), width // 2)]

    def copy(a, k, block, to, src=None, h=None):
        src_ref = slot(a, *block) if src is None else src
        dst_ref = slot(a, *block)
        if h is not None:
            src_ref, dst_ref = part(a, src_ref, h), part(a, dst_ref, h)
            if src_ref is None:
                return None
        return pltpu.make_async_remote_copy(
            src_ref=src_ref, dst_ref=dst_ref, send_sem=send_sems.at[ns * a + k], recv_sem=recv_sems.at[ns * a + k],
            device_id=to, device_id_type=MESH)

    def mine():
        return [pltpu.make_async_copy(ins[a], slot(a, *me), local_sems.at[a]) for a in range(n)]

    def own_sends(a):
        return [copy(a, 0, me, sibling, src=ins[a]), copy(a, 1, me, (*xn, c), src=ins[a]),
                copy(a, 2, me, (*yn, c), src=ins[a])]

    def neighbour_relays(a):
        return [copy(a, 4, (*xn, c), sibling), copy(a, 7, (*xn, c), (*yn, c), h=1),
                copy(a, 5, (*yn, c), sibling), copy(a, 8, (*yn, c), (*xn, c), h=0)]

    def diagonal_halves(a):
        return [copy(a, k, (*diag, c), me, h=h) for k, h in ((8, 0), (7, 1))]

    def start_all(cps):
        for cp in cps:
            if cp is not None:
                cp.start()

    def phase0():
        start_all(mine())
        for a in range(n):
            start_all(own_sends(a))

    def phase1():
        for a in range(n):
            copy(a, 1, (*xn, c), me).wait_recv()
            copy(a, 2, (*yn, c), me).wait_recv()
            start_all(neighbour_relays(a))

    def phase2():
        for a in range(n):
            for cp in diagonal_halves(a):
                if cp is not None:
                    cp.wait_recv()
            copy(a, 6, (*diag, c), sibling).start()

    def finish():
        for a in range(n):
            copy(a, 0, sibling, me).wait_recv()
            for j, chip in enumerate((xn, yn, diag)):
                copy(a, 4 + j, (*chip, 1 - c), me).wait_recv()
        for a in range(n):
            for cp in own_sends(a) + neighbour_relays(a) + [copy(a, 6, (*diag, c), sibling)]:
                if cp is not None:
                    cp.wait_send()
        for cp in mine():
            cp.wait()

    return phase0, phase1, phase2, finish


def _gather_scratch(n):
    return [pltpu.SemaphoreType.DMA((GATHER_SEMS * n,)), pltpu.SemaphoreType.DMA((GATHER_SEMS * n,)),
            pltpu.SemaphoreType.DMA((n,))]


def _all_gather(arrs):
    n = len(arrs)

    def body(*refs):
        for phase in _gather_phases(refs[:n], refs[n:2 * n], *refs[2 * n:]):
            phase()

    anyspec = pl.BlockSpec(memory_space=pl.ANY)
    return pl.pallas_call(
        body, name="weights_all_gather",
        out_shape=[SDS((N_DEV,) + a.shape, a.dtype) for a in arrs],
        in_specs=[anyspec] * n, out_specs=[anyspec] * n, scratch_shapes=_gather_scratch(n),
    )(*arrs)


def _dw_in_swap(a_parts, u):
    tile, tk = 1024, 1024
    s = u.shape[0]
    nk = s // tk
    na = len(a_parts)
    offs, counts, ni = _col_blocks(a_parts, tile)

    def body(*refs):
        a_refs, u_ref = refs[:na], refs[na]
        dw_ref, got_ref = refs[na + 1:na + 3]
        acc, stage, local_sems, send_sems, recv_sem = refs[na + 3:]
        i, k = pl.program_id(0), pl.program_id(1)
        x, y, c = _my_pos()
        par = i % 2

        def tile_copies(t, p):
            rows = pl.ds(pl.multiple_of(t * tile, tile), tile)
            loc = pltpu.make_async_copy(stage.at[p], dw_ref.at[rows], local_sems.at[p])
            rem = pltpu.make_async_remote_copy(
                src_ref=stage.at[p], dst_ref=got_ref.at[rows], send_sem=send_sems.at[p], recv_sem=recv_sem,
                device_id=(x, y, 1 - c), device_id_type=MESH)
            return loc, rem

        @pl.when(k == 0)
        def _():
            acc[...] = jnp.zeros((tile, tile), f32)

        for t in range(na):
            @pl.when(jnp.logical_and(i >= offs[t], i < offs[t] + counts[t]))
            def _(t=t):
                acc[...] += _tn(a_refs[t][...], u_ref[pl.ds(pl.multiple_of(k * tk, tk), tk), :])

        @pl.when(k == nk - 1)
        def _():
            @pl.when(i >= 2)
            def _():
                loc, rem = tile_copies(i - 2, par)
                loc.wait()
                rem.wait_send()
            stage[par] = acc[...]
            loc, rem = tile_copies(i, par)
            loc.start()
            rem.start()

        @pl.when(jnp.logical_and(i == ni - 1, k == nk - 1))
        def _():
            for t in (ni - 2, ni - 1):
                loc, rem = tile_copies(t, t % 2)
                loc.wait()
                rem.wait_send()
            pltpu.make_async_remote_copy(src_ref=dw_ref, dst_ref=got_ref, send_sem=send_sems.at[0], recv_sem=recv_sem,
                                         device_id=(x, y, c), device_id_type=MESH).wait_recv()

    def a_spec(t):
        def index(i, k):
            mine = jnp.logical_and(i >= offs[t], i < offs[t] + counts[t])
            return jnp.where(mine, k, 0), jnp.clip(i - offs[t], 0, counts[t] - 1)
        return pl.BlockSpec((tk, tile), index)

    anyspec = pl.BlockSpec(memory_space=pl.ANY)
    return pl.pallas_call(
        body, name="dw_in_swap", grid=(ni, nk),
        in_specs=[a_spec(t) for t in range(na)] + [pl.BlockSpec((s, tile), lambda i, k: (0, 0))],
        out_specs=[anyspec] * 2,
        out_shape=[SDS((ni * tile, tile), f32), SDS((ni * tile, tile), f32)],
        scratch_shapes=[pltpu.VMEM((tile, tile), f32), pltpu.VMEM((2, tile, tile), f32), pltpu.SemaphoreType.DMA((2,)),
                        pltpu.SemaphoreType.DMA((2,)), pltpu.SemaphoreType.DMA(())],
        compiler_params=pltpu.CompilerParams(dimension_semantics=("arbitrary", "arbitrary")),
    )(*a_parts, u)


def _chip_sum(mine, got, rows, name):
    r, cdim = mine.shape
    tc = LANES

    def body(m_ref, g_ref, s16_ref):
        c = lax.axis_index("c")
        for q in range(4):
            blk = pl.ds(rows * (2 * q + c), rows)
            s16_ref[q] = (m_ref[blk, :] + g_ref[blk, :]).astype(bf16)

    col = pl.BlockSpec((r, tc), lambda i: (0, i))
    return pl.pallas_call(
        body, name=name, grid=(cdim // tc,), in_specs=[col, col],
        out_specs=pl.BlockSpec((4, rows, tc), lambda i: (0, 0, i)), out_shape=SDS((4, rows, cdim), bf16),
        compiler_params=pltpu.CompilerParams(dimension_semantics=("parallel",)),
    )(mine, got)


def _assemble_wt(shards):
    nd, rows, cdim = shards.shape
    tc = 256

    def body(g_ref, o_ref):
        for j in range(nd):
            o_ref[pl.ds(rows * j, rows), :] = g_ref[j]
        o_ref[pl.ds(nd * rows, NP - nd * rows), :] = jnp.zeros((NP - nd * rows, tc), shards.dtype)

    return pl.pallas_call(
        body, name="assemble_w_in", grid=(cdim // tc,),
        in_specs=[pl.BlockSpec((nd, rows, tc), lambda i: (0, 0, i))],
        out_specs=pl.BlockSpec((NP, tc), lambda i: (0, i)), out_shape=SDS((NP, cdim), shards.dtype),
        compiler_params=pltpu.CompilerParams(dimension_semantics=("parallel",)),
    )(shards)


def _chip_exchange_copies(ins, outs, send_sems, recv_sems, local_sems):
    nb = len(ins)
    x, y, c = _my_pos()
    my_q = 2 * x + y
    mine = [pltpu.make_async_copy(ins[a].at[my_q], outs[a].at[my_q], local_sems.at[a]) for a in range(nb)]
    sends, recvs = [], []
    for k in range(1, 4):
        to, frm = (my_q + k) % 4, (my_q + 4 - k) % 4
        for a in range(nb):
            sems = dict(send_sem=send_sems.at[3 * a + k - 1], recv_sem=recv_sems.at[3 * a + k - 1], device_id_type=MESH)
            sends.append(pltpu.make_async_remote_copy(
                src_ref=ins[a].at[to], dst_ref=outs[a].at[my_q], device_id=(to // 2, to % 2, c), **sems))
            recvs.append(pltpu.make_async_remote_copy(
                src_ref=ins[a].at[frm], dst_ref=outs[a].at[frm], device_id=(x, y, c), **sems))
    return mine, sends, recvs


def _chip_exchange_scratch(nb):
    return [pltpu.SemaphoreType.DMA((3 * nb,)), pltpu.SemaphoreType.DMA((3 * nb,)), pltpu.SemaphoreType.DMA((nb,))]


def _prenorm_inproj(x, nw, wt, gather=()):
    s, d = x.shape
    npad = wt.shape[0]
    tm, tn = 1024, 1024
    ng = len(gather)
    ni, nj = s // tm, npad // tn

    def body(x_ref, nw_ref, w_ref, *refs):
        g_in, (proj_ref, u_ref), g_out, sems = refs[:ng], refs[ng:ng + 2], refs[ng + 2:2 * ng + 2], refs[2 * ng + 2:]
        i, j = pl.program_id(0), pl.program_id(1)
        if ng:
            phases = _gather_phases(g_in, g_out, *sems)
            for step, phase in enumerate(phases[:3]):
                @pl.when(jnp.logical_and(i == step, j == 0))
                def _(phase=phase):
                    phase()

        @pl.when(j == 0)
        def _():
            xv = x_ref[...]
            r = lax.rsqrt(jnp.mean(xv * xv, axis=-1, keepdims=True) + EPS)
            u_ref[...] = (xv * r * nw_ref[...]).astype(bf16)
        proj_ref[...] = _nt(u_ref[...], w_ref[pl.ds(pl.multiple_of(j * tn, tn), tn), :])

        if ng:
            @pl.when(jnp.logical_and(i == ni - 1, j == nj - 1))
            def _():
                phases[3]()

    anyspec = pl.BlockSpec(memory_space=pl.ANY)
    outs = pl.pallas_call(
        body, name="prenorm_inproj", grid=(ni, nj),
        in_specs=[pl.BlockSpec((tm, d), lambda i, j: (i, 0)), pl.BlockSpec((1, d), lambda i, j: (0, 0)),
                  pl.BlockSpec((npad, d), lambda i, j: (0, 0))] + [anyspec] * ng,
        out_specs=[pl.BlockSpec((tm, tn), lambda i, j: (i, j)), pl.BlockSpec((tm, d), lambda i, j: (i, 0))]
        + [anyspec] * ng,
        out_shape=[SDS((s, npad), f32), SDS((s, d), bf16)] + [SDS((N_DEV,) + a.shape, a.dtype) for a in gather],
        scratch_shapes=_gather_scratch(ng) if ng else [],
        compiler_params=pltpu.CompilerParams(dimension_semantics=("arbitrary", "arbitrary")),
    )(x, nw, wt, *gather)
    return outs[0], outs[1], outs[2:]


def _attn_consts():
    head0 = _iota((BLK, LANES), 1) < HEAD_DIM
    tri2 = (_iota((BLK, 2 * LANES), 1) % LANES) <= _iota((BLK, 2 * LANES), 0)
    ones2 = ((_iota((LANES, 2 * LANES), 0) < HEAD_DIM) == (_iota((LANES, 2 * LANES), 1) < LANES)).astype(bf16)
    rmat = ((_iota((2 * LANES, LANES), 0) < LANES) == (_iota((2 * LANES, LANES), 1) < HEAD_DIM)).astype(bf16)
    bones = ((_iota((LANES, LANES), 0) < HEAD_DIM) == (_iota((LANES, LANES), 1) < HEAD_DIM)).astype(bf16)
    return head0, tri2, ones2, rmat, bones


def _stack_heads(x16, head0):
    zero = jnp.zeros_like(x16)
    return jnp.concatenate([jnp.where(head0, x16, zero), jnp.where(head0, zero, x16)], axis=0)


def _bf16_terms(x, terms):
    out = []
    for _ in range(terms):
        t = x.astype(bf16)
        out.append(t)
        x = x - t.astype(f32)
    return out


def _dot_01(x, w16, terms):
    return _nn(jnp.concatenate(_bf16_terms(x, terms), axis=1), jnp.concatenate([w16] * terms, axis=0))


def _split_dot_sum(x, w16):
    hi, lo = _bf16_terms(x, 2)
    return _nn(hi, w16) + _nn(lo, w16)


def _dot_01_left(w16, x, terms):
    return _nn(jnp.concatenate([w16] * terms, axis=1), jnp.concatenate(_bf16_terms(x, terms), axis=0))


def _quarter_rows(i, q):
    return pl.ds(pl.multiple_of((i // 2) * 2048 + q * 512 + (i % 2) * 256, 256), 256)


def _token_rows(i, q):
    return pl.ds(i * 1024 + q, 256, stride=4)


def _quarter_block(i, d, nb):
    r, blk = i // nb, i % nb
    if d == 1:
        runs = [pl.ds(pl.multiple_of((blk // 16) * 2048 + q * 512 + (blk % 16) * 32, 32), 32) for q in range(4)]
    elif d == 4:
        runs = [pl.ds(pl.multiple_of((blk // 4) * 2048 + r * 512 + (blk % 4) * BLK, BLK), BLK)]
    else:
        runs = [pl.ds(blk * 2048 + (r % 4) * 512 + r // 4, BLK, stride=4)]
    return runs, blk > 0


def _quarter_mask():
    order = lambda n: 4 * (n % 32) + n // 32
    return order(_iota((BLK, 2 * LANES), 1) % LANES) <= order(_iota((BLK, 2 * LANES), 0))


def _load_runs(ref, runs):
    parts = [ref[run, :] for run in runs]
    return parts[0] if len(parts) == 1 else jnp.concatenate(parts, axis=0)


def _store_runs(ref, runs, val):
    n = BLK // len(runs)
    for t, run in enumerate(runs):
        ref[run, :] = val[t * n:(t + 1) * n]


def _add_runs(ref, runs, val):
    n = BLK // len(runs)
    for t, run in enumerate(runs):
        ref[run, :] += val[t * n:(t + 1) * n]


def _attn_fwd(proj):
    s = proj.shape[0]
    n_it = s // BLK

    def body(q_in, k_in, v_in, g_ref, o_ref, l_ref, mix_ref, q_ref, k_ref, v_ref, op0, op1, op2, lp0, lp1, lp2,
             s_a, s_b, sd_a, sd_b, p_a, p_b, m_a, m_b, pd_a, pd_b, k_a, k_b, v_a, v_b, stage):
        op_refs, lp_refs = (op0, op1, op2), (lp0, lp1, lp2)
        head0, tri2_t, ones2, rmat, _ = _attn_consts()
        tri2_q = _quarter_mask()

        def reorder(i, carry):
            for src, dst, scale in ((q_in, q_ref, 0.125), (k_in, k_ref, 1.0), (v_in, v_ref, 1.0)):
                for q in range(4):
                    t = src[_token_rows(i, q), :]
                    dst[_quarter_rows(i, q), :] = t if scale == 1.0 else t * scale
            return carry

        lax.fori_loop(0, s // 1024, reorder, 0)
        score_bufs, prob_bufs = ((s_a, sd_a), (s_b, sd_b)), ((p_a, m_a, pd_a), (p_b, m_b, pd_b))
        k_bufs, v_bufs = (k_a, k_b), (v_a, v_b)
        for buf in k_bufs + v_bufs:
            buf[...] = jnp.zeros_like(buf)

        def unstack(st16):
            return st16[:BLK] + st16[BLK:]

        def scores(i, par, d, nb):
            rows, has_prev = _quarter_block(i, d, nb)
            tri2 = tri2_q if d == 1 else tri2_t
            s_buf, sd_buf = score_bufs[par]
            qs = _load_runs(q_ref, rows)
            qs16 = qs.astype(bf16)
            kst_c = _stack_heads(_load_runs(k_ref, rows).astype(bf16), head0)
            kst_p = k_bufs[1 - par][...]
            k_bufs[par][...] = kst_c
            sc = _nt(qs16, kst_c)
            sp = _nt(qs16, kst_p)
            s_buf[...] = jnp.where(tri2, sc, jnp.where(has_prev, sp, -jnp.inf))
            sd = _nn((qs * unstack(kst_p).astype(f32)).astype(bf16), ones2)
            sd_buf[...] = jnp.where(has_prev, sd, -jnp.inf)

        def softmax(bufs_in, bufs_out):
            s_buf, sd_buf = bufs_in
            p_buf, m_buf, pd_buf = bufs_out
            sc, sd2 = s_buf[...], sd_buf[...]
            m0 = jnp.max(sc[:, :LANES], axis=1, keepdims=True)
            m1 = jnp.max(sc[:, LANES:], axis=1, keepdims=True)
            m2 = jnp.concatenate([jnp.broadcast_to(m0, (BLK, LANES)), jnp.broadcast_to(m1, (BLK, LANES))], axis=1)
            m2 = jnp.maximum(m2, sd2)
            p_buf[...] = jnp.exp(sc - m2).astype(bf16)
            m_pair = jnp.where(head0, m2[:, :LANES], m2[:, LANES:])
            m_buf[...] = m_pair
            pd_buf[...] = jnp.exp(jnp.where(head0, sd2[:, :LANES], sd2[:, LANES:]) - m_pair)

        def output(i, par, d, nb, p):
            rows, _ = _quarter_block(i, d, nb)
            tri2 = tri2_q if d == 1 else tri2_t
            p_buf, m_buf, pd_buf = prob_bufs[par]
            vst_c = _stack_heads(_load_runs(v_ref, rows).astype(bf16), head0)
            vst_p = v_bufs[1 - par][...]
            v_bufs[par][...] = vst_c
            pt16, pd = p_buf[...], pd_buf[...]
            zero = jnp.zeros_like(pt16)
            o = (_nn(jnp.where(tri2, pt16, zero), vst_c) + _nn(jnp.where(tri2, zero, pt16), vst_p)
                 + pd * unstack(vst_p).astype(f32))
            l = _nn(pt16, rmat) + pd
            _store_runs(op_refs[p], rows, o / l)
            _store_runs(lp_refs[p], rows, m_buf[...] + jnp.log(l))

        for p, d in enumerate(DILATIONS):
            nb = s // (BLK * d)
            scores(0, 0, d, nb)
            scores(1, 1, d, nb)
            softmax(score_bufs[0], prob_bufs[0])

            def steps(j, carry, d=d, nb=nb, p=p):
                for par in range(2):
                    t = 2 * j + 2 + par
                    scores(t, par, d, nb)
                    output(t - 2, par, d, nb, p)
                    softmax(score_bufs[1 - par], prob_bufs[1 - par])
                return carry

            lax.fori_loop(0, (n_it - 2) // 2, steps, 0, unroll=True)
            output(n_it - 2, 0, d, nb, p)
            softmax(score_bufs[1], prob_bufs[1])
            output(n_it - 1, 1, d, nb, p)

        def merge(i, carry):
            for q in range(4):
                rows, tokens = _quarter_rows(i, q), _token_rows(i, q)
                l0, l1, l2 = lp0[rows, :], lp1[rows, :], lp2[rows, :]
                m = jnp.maximum(jnp.maximum(l0, l1), l2)
                e0, e1, e2 = jnp.exp(l0 - m), jnp.exp(l1 - m), jnp.exp(l2 - m)
                z = e0 + e1 + e2
                o = (e0 * op0[rows, :] + e1 * op1[rows, :] + e2 * op2[rows, :]) / z
                o_ref[tokens, :] = o
                l_ref[rows, :] = m + jnp.log(z)
                g = g_ref[tokens, :]
                stage[pl.ds(q, 256, stride=4), :] = o * (g * _sigmoid(g))
            mix_ref[pl.ds(pl.multiple_of(i * 1024, 1024), 1024), :] = stage[...].astype(bf16)
            return carry

        lax.fori_loop(0, s // 1024, merge, 0)

    col = lambda base: pl.BlockSpec((s, LANES), lambda h: (0, base + h))
    return pl.pallas_call(
        body, name="attn_fwd", grid=(N_PAIRS,),
        in_specs=[col(0), col(8), col(16), col(24)],
        out_specs=[col(0)] * 6,
        out_shape=[SDS((s, D_ATTN), f32), SDS((s, D_ATTN), f32), SDS((s, D_ATTN), bf16)] + [SDS((s, D_ATTN), f32)] * 3,
        scratch_shapes=[pltpu.VMEM((s, LANES), f32)] * 6 + [pltpu.VMEM((BLK, 2 * LANES), f32)] * 4
        + [pltpu.VMEM((BLK, 2 * LANES), bf16)] * 2 + [pltpu.VMEM((BLK, LANES), f32)] * 4
        + [pltpu.VMEM((2 * BLK, LANES), bf16)] * 4 + [pltpu.VMEM((1024, LANES), f32)],
        compiler_params=pltpu.CompilerParams(dimension_semantics=("parallel",)),
    )(proj, proj, proj, proj)


def _expand_mat():
    colv = _iota((LANES, 2 * D_SSM), 1)
    head = 2 * ((colv % D_SSM) // LANES) + colv // D_SSM
    return (_iota((LANES, 2 * D_SSM), 0) == head).astype(bf16)


def _fold_mat():
    return (_iota((D_SSM, LANES), 0) // HEAD_DIM == _iota((D_SSM, LANES), 1)).astype(bf16)


def _conv(xs_ref, bc_ref, xs_tail, bc_tail, cw_ref, cb_ref, xpad, first):
    keep = jnp.where(first, 0.0, 1.0)
    xpad[0:8, 0:D_SSM] = xs_tail[...] * keep
    xpad[0:8, D_SSM:D_CONV] = bc_tail[...] * keep
    xpad[8:8 + CHUNK, 0:D_SSM] = xs_ref[...]
    xpad[8:8 + CHUNK, D_SSM:D_CONV] = bc_ref[...]
    xp = xpad[...]
    cv = cb_ref[...] + cw_ref[3:4, :] * xp[8:8 + CHUNK]
    for j in range(3):
        cv = cv + cw_ref[j:j + 1, :] * pltpu.roll(xp, 3 - j, 0)[8:8 + CHUNK]
    return cv


def _decay_terms(dt_ref, dtb_ref, alog16_ref, emat_ref):
    pre = dt_ref[...] + dtb_ref[...]
    dt16 = _softplus(pre)
    a16 = -jnp.exp(alog16_ref[...])
    sub, lane = _iota((CHUNK, CHUNK), 0), _iota((CHUNK, CHUNK), 1)
    tri = (sub >= lane).astype(f32)
    al16 = _nn_hi(tri, dt16 * a16)
    al_t = al16.T
    emat = emat_ref[...]
    dt_x = _dot_01(dt16, emat, 3)
    al_x = _dot_01(al16, emat, 3)
    lane_w = _iota((CHUNK, D_SSM), 1)
    even = (lane_w % LANES) < HEAD_DIM
    dt_f = jnp.where(even, dt_x[:, :D_SSM], dt_x[:, D_SSM:])
    al_f = jnp.where(even, al_x[:, :D_SSM], al_x[:, D_SSM:])
    return pre, dt_f, al_f, al_x, al_t


def _decay_mat(al_x, al_t, pair, h):
    sub, lane = _iota((CHUNK, CHUNK), 0), _iota((CHUNK, CHUNK), 1)
    col = al_x[:, h * D_SSM + pair * LANES: h * D_SSM + (pair + 1) * LANES]
    row = al_t[2 * pair + h: 2 * pair + h + 1, :]
    return jnp.exp(jnp.where(sub >= lane, col - row, -jnp.inf))


def _ssd_in_specs(order):
    blk = lambda w, cb: pl.BlockSpec((CHUNK, w), lambda i: (order(i), cb))
    tail = lambda w, cb: pl.BlockSpec((8, w), lambda i: (jnp.maximum(16 * order(i) - 1, 0), cb))
    return [blk(D_SSM, COL_XS // D_SSM), blk(512, COL_BC // 512), tail(D_SSM, COL_XS // D_SSM),
            tail(512, COL_BC // 512), blk(LANES, COL_DT // LANES), blk(D_SSM, COL_Z // D_SSM)]


def _full(shape):
    return pl.BlockSpec(shape, lambda i: (0,) * len(shape))


def _ssd_fwd(proj, conv_w, conv_b, dtb16, alog16, alog_f, d_f, nw):
    s = proj.shape[0]
    nc = s // CHUNK

    def body(xs_ref, bc_ref, xs_tail, bc_tail, dt_ref, z_ref, cw_ref, cb_ref, dtb_ref, alog16_ref, alogf_ref,
             df_ref, nw_ref, mix_ref, y_ref, st_ref, cv_ref, h_scr, xpad, y_scr, emat_ref):
        c = pl.program_id(0)

        @pl.when(c == 0)
        def _():
            h_scr[...] = jnp.zeros_like(h_scr)
            emat_ref[...] = _expand_mat()

        cv = _conv(xs_ref, bc_ref, xs_tail, bc_tail, cw_ref, cb_ref, xpad, c == 0)
        cv_ref[...] = cv
        xbc = cv * _sigmoid(cv)
        _, dt_f, al_f, al_x, al_t = _decay_terms(dt_ref, dtb_ref, alog16_ref, emat_ref)
        head0 = _iota((CHUNK, LANES), 1) < HEAD_DIM
        st_ref[...] = h_scr[...]
        for g in range(N_GROUPS):
            bm = xbc[:, D_SSM + g * D_STATE: D_SSM + (g + 1) * D_STATE].astype(bf16)
            cm = xbc[:, D_SSM + (N_GROUPS + g) * D_STATE: D_SSM + (N_GROUPS + g + 1) * D_STATE].astype(bf16)
            gmat = _nt(cm, bm)
            for pair in range(4 * g, 4 * g + 4):
                sl = slice(pair * LANES, (pair + 1) * LANES)
                xp, dtp, alp = xbc[:, sl], dt_f[:, sl], al_f[:, sl]
                xdt = xp * dtp
                xdt16 = xdt.astype(bf16)
                al_last = alp[CHUNK - 1:CHUNK, :]
                hp = h_scr[:, sl]
                y_off = jnp.exp(alp) * _nn(cm, hp.astype(bf16))
                yd = [_nn((gmat * _decay_mat(al_x, al_t, pair, h)).astype(bf16), xdt16) for h in range(2)]
                y_scr[:, sl] = jnp.where(head0, yd[0], yd[1]) + y_off + df_ref[:, sl] * xp
                st = _tn(bm, (jnp.exp(al_last - alp) * xdt).astype(bf16))
                h_scr[:, sl] = jnp.exp(al_last) * hp + st
        y = y_scr[...]
        y_ref[...] = y
        z = z_ref[...]
        yz = y * (z * _sigmoid(z))
        gw = D_SSM // N_GROUPS
        for g in range(N_GROUPS):
            part = yz[:, g * gw:(g + 1) * gw]
            r = lax.rsqrt(jnp.mean(part * part, axis=-1, keepdims=True) + EPS)
            mix_ref[:, g * gw:(g + 1) * gw] = (part * r * nw_ref[:, g * gw:(g + 1) * gw]).astype(bf16)

    order = lambda i: i
    row = lambda w: pl.BlockSpec((CHUNK, w), lambda i: (i, 0))
    return pl.pallas_call(
        body, name="ssd_fwd", grid=(nc,),
        in_specs=_ssd_in_specs(order) + [_full((4, D_CONV)), _full((1, D_CONV)), _full((1, LANES)), _full((1, LANES)),
                                         _full((1, D_SSM)), _full((1, D_SSM)), _full((1, D_SSM))],
        out_specs=[row(D_SSM), row(D_SSM), pl.BlockSpec((None, D_STATE, D_SSM), lambda i: (i, 0, 0)), row(D_CONV)],
        out_shape=[SDS((s, D_SSM), bf16), SDS((s, D_SSM), f32), SDS((nc, D_STATE, D_SSM), f32),
                   SDS((s, D_CONV), f32)],
        scratch_shapes=[pltpu.VMEM((D_STATE, D_SSM), f32), pltpu.VMEM((8 + CHUNK, D_CONV), f32),
                        pltpu.VMEM((CHUNK, D_SSM), f32), pltpu.VMEM((LANES, 2 * D_SSM), bf16)],
        compiler_params=pltpu.CompilerParams(dimension_semantics=("arbitrary",)),
    )(proj, proj, proj, proj, proj, proj, conv_w, conv_b, dtb16, alog16, alog_f, d_f, nw)


def _outproj_loss(mix_a, mix_s, wo, x, tgt, npw):
    s, d = x.shape
    tm = 512

    def body(ma_ref, ms_ref, wo_ref, x_ref, t_ref, npw_ref, dmix_ref, dres_ref, acc_ref, dwo_ref):
        @pl.when(pl.program_id(0) == 0)
        def _():
            acc_ref[...] = jnp.zeros_like(acc_ref)
            dwo_ref[...] = jnp.zeros_like(dwo_ref)

        out = _nn(ma_ref[...], wo_ref[0:D_ATTN, :]) + _nn(ms_ref[...], wo_ref[D_ATTN:, :])
        r = lax.rsqrt(jnp.mean(out * out, axis=-1, keepdims=True) + EPS)
        on = out * r
        diff = x_ref[...] + on * npw_ref[...] - t_ref[...]
        dres = diff * (1.0 / d)
        dres_ref[...] = dres
        acc_ref[0:1, :] += jnp.sum(diff * diff, axis=0, keepdims=True)
        acc_ref[1:2, :] += jnp.sum(dres * on, axis=0, keepdims=True)
        dn = dres * npw_ref[...]
        dout = (r * (dn - on * jnp.mean(dn * on, axis=-1, keepdims=True))).astype(bf16)
        dmix_ref[...] = _nt(dout, wo_ref[...])
        dwo_ref[0:D_ATTN, :] += _tn(ma_ref[...], dout)
        dwo_ref[D_ATTN:, :] += _tn(ms_ref[...], dout)

    row = lambda w: pl.BlockSpec((tm, w), lambda i: (i, 0))
    return pl.pallas_call(
        body, name="outproj_loss", grid=(s // tm,),
        in_specs=[row(D_ATTN), row(D_SSM), _full((D_ATTN + D_SSM, d)), row(d), row(d), _full((1, d))],
        out_specs=[row(D_ATTN + D_SSM), row(d), _full((8, d)), _full((D_ATTN + D_SSM, d))],
        out_shape=[SDS((s, D_ATTN + D_SSM), f32), SDS((s, d), f32), SDS((8, d), f32), SDS((D_ATTN + D_SSM, d), f32)],
        compiler_params=pltpu.CompilerParams(dimension_semantics=("arbitrary",)),
    )(mix_a, mix_s, wo, x, tgt, npw)


def _attn_bwd(proj, qkv, o, lb, dmix, swap=None):
    s = proj.shape[0]
    n_it = s // BLK

    nsw = 0 if swap is None else 1

    def body(*refs):
        q_ref, k_ref, v_ref, g_ref, o_ref, l_ref, dm_ref = refs[:7]
        swap_in = refs[7:7 + nsw]
        dq_ref, dk_ref, dv_ref, dg_ref = refs[7 + nsw:11 + nsw]
        swap_out = refs[11 + nsw:11 + 2 * nsw]
        dq_acc, dk_acc, dv_acc, do_scr, dl_scr = refs[11 + 2 * nsw:16 + 2 * nsw]
        bufs = refs[16 + 2 * nsw:44 + 2 * nsw]
        stage_a, stage_b = refs[44 + 2 * nsw:46 + 2 * nsw]
        swap_sems = refs[46 + 2 * nsw:]
        head0, tri2_t, _, _, bones = _attn_consts()
        tri2_q = _quarter_mask()

        if nsw:
            x, y, c = _my_pos()
            swap_copy = pltpu.make_async_remote_copy(
                src_ref=swap_in[0], dst_ref=swap_out[0], send_sem=swap_sems[0], recv_sem=swap_sems[1],
                device_id=(x, y, 1 - c), device_id_type=MESH)

            @pl.when(pl.program_id(0) == 0)
            def _():
                swap_copy.start()

        quarter_rows, load, add = _quarter_rows, _load_runs, _add_runs

        def pro(i, carry):
            for t in range(4):
                rows = pl.ds(pl.multiple_of(i * 1024 + t * 256, 256), 256)
                g = g_ref[rows, :]
                sg = _sigmoid(g)
                dmx = dm_ref[rows, :]
                ov = o_ref[rows, :]
                dg_ref[rows, :] = (dmx * ov * (sg * (1.0 + g * (1.0 - sg)))).astype(bf16)
                do = dmx * (g * sg)
                stage_a[t * 256:(t + 1) * 256, :] = do
                stage_b[t * 256:(t + 1) * 256, :] = _split_dot_sum(do * ov, bones)
            z = jnp.zeros((256, LANES), f32)
            for q in range(4):
                rows = quarter_rows(i, q)
                do_scr[rows, :] = stage_a[pl.ds(q, 256, stride=4), :]
                dl_scr[rows, :] = stage_b[pl.ds(q, 256, stride=4), :]
                dq_acc[rows, :] = z
                dk_acc[rows, :] = z
                dv_acc[rows, :] = z
            return carry

        lax.fori_loop(0, s // 1024, pro, 0)

        def per_head(t):
            return jnp.concatenate([t[:, :LANES], t[:, LANES:]], axis=0)

        def both_heads(t):
            tr = pltpu.roll(t, HEAD_DIM, 1)
            return jnp.concatenate([jnp.where(head0, t, tr), jnp.where(head0, tr, t)], axis=1)

        mm_bufs = ((bufs[0], bufs[1], bufs[2], bufs[3]), (bufs[4], bufs[5], bufs[6], bufs[7]))
        ds_bufs = ((bufs[8], bufs[9], bufs[10], bufs[11]), (bufs[12], bufs[13], bufs[14], bufs[15]))
        op_bufs = ((bufs[16], bufs[17], bufs[18], bufs[19]), (bufs[20], bufs[21], bufs[22], bufs[23]))
        vc_bufs, carry_k, carry_v = (bufs[24], bufs[25]), bufs[26], bufs[27]
        for buf in (op_bufs[0][0], op_bufs[1][0]) + vc_bufs:
            buf[...] = jnp.zeros_like(buf)

        def block_rows(i, d, nb):
            rows, has_prev = _quarter_block(i, d, nb)
            return rows, rows, has_prev

        def unstack(st16):
            return st16[:BLK] + st16[BLK:]

        def products(i, par, d, nb):
            src, scr, has_prev = block_rows(i, d, nb)
            tri2 = tri2_q if d == 1 else tri2_t
            s_buf, dp_buf, sd_buf, dpd_buf = mm_bufs[par]
            kc_buf, kp_buf, q_buf, do_buf = op_bufs[par]
            qs = load(q_ref, src)
            do = load(do_scr, scr)
            qs16, do16 = qs.astype(bf16), do.astype(bf16)
            kst_c = _stack_heads(load(k_ref, src).astype(bf16), head0)
            vst_c = _stack_heads(load(v_ref, src).astype(bf16), head0)
            kst_p, vst_p = op_bufs[1 - par][0][...], vc_bufs[1 - par][...]
            kc_buf[...] = kst_c
            kp_buf[...] = kst_p
            vc_bufs[par][...] = vst_c
            q_buf[...] = qs16
            do_buf[...] = do16
            s_buf[...] = jnp.where(tri2, _nt(qs16, kst_c), jnp.where(has_prev, _nt(qs16, kst_p), -jnp.inf))
            dp_buf[...] = jnp.where(tri2, _nt(do16, vst_c), jnp.where(has_prev, _nt(do16, vst_p), 0.0))
            sd_buf[...] = _nn((qs * unstack(kst_p).astype(f32)).astype(bf16), bones)
            dpd_buf[...] = jnp.where(has_prev, _nn((do * unstack(vst_p).astype(f32)).astype(bf16), bones), 0.0)

        def softmax_grad(i, par, d, nb):
            src, scr, has_prev = block_rows(i, d, nb)
            s_buf, dp_buf, sd_buf, dpd_buf = mm_bufs[par]
            p_buf, ds_buf, pd_buf, dsd_buf = ds_bufs[par]
            lse = load(l_ref, src)
            dl = load(dl_scr, scr)
            pt = jnp.exp(s_buf[...] - both_heads(lse))
            ds_buf[...] = (pt * (dp_buf[...] - both_heads(dl))).astype(bf16)
            p_buf[...] = pt.astype(bf16)
            pd = jnp.where(has_prev, jnp.exp(sd_buf[...] - lse), 0.0)
            pd_buf[...] = pd
            dsd_buf[...] = pd * (dpd_buf[...] - dl)

        def accumulate(i, par, d, nb):
            _, rows, _ = block_rows(i, d, nb)
            _, before, _ = block_rows(jnp.maximum(i - 1, 0), d, nb)
            tri2 = tri2_q if d == 1 else tri2_t
            p_buf, ds_buf, pd_buf, dsd_buf = ds_bufs[par]
            kc_buf, kp_buf, q_buf, do_buf = op_bufs[par]
            pt16, ds16, pd, dsd = p_buf[...], ds_buf[...], pd_buf[...], dsd_buf[...]
            zero = jnp.zeros_like(pt16)
            dsc, dsp = jnp.where(tri2, ds16, zero), jnp.where(tri2, zero, ds16)
            pc, pp = jnp.where(tri2, pt16, zero), jnp.where(tri2, zero, pt16)
            kst_c, kst_p, q16, do16 = kc_buf[...], kp_buf[...], q_buf[...], do_buf[...]
            qst, dost = _stack_heads(q16, head0), _stack_heads(do16, head0)
            add(dq_acc, rows, _nn(dsc, kst_c) + _nn(dsp, kst_p) + dsd * unstack(kst_p).astype(f32))
            dk2 = _tn(jnp.concatenate([per_head(dsc), per_head(dsp)], axis=1), qst)
            dv2 = _tn(jnp.concatenate([per_head(pc), per_head(pp)], axis=1), dost)
            add(dk_acc, before, carry_k[...] + dk2[BLK:] + dsd * q16.astype(f32))
            add(dv_acc, before, carry_v[...] + dv2[BLK:] + pd * do16.astype(f32))
            carry_k[...] = dk2[:BLK]
            carry_v[...] = dv2[:BLK]

        for d in DILATIONS:
            nb = s // (BLK * d)
            carry_k[...] = jnp.zeros_like(carry_k)
            carry_v[...] = jnp.zeros_like(carry_v)
            products(0, 0, d, nb)
            products(1, 1, d, nb)
            softmax_grad(0, 0, d, nb)

            def steps(j, carry, d=d, nb=nb):
                for par in range(2):
                    t = 2 * j + 2 + par
                    accumulate(t - 2, par, d, nb)
                    products(t, par, d, nb)
                    softmax_grad(t - 1, 1 - par, d, nb)
                return carry

            lax.fori_loop(0, (n_it - 2) // 2, steps, 0, unroll=True)
            accumulate(n_it - 2, 0, d, nb)
            softmax_grad(n_it - 1, 1, d, nb)
            accumulate(n_it - 1, 1, d, nb)
            _, last, _ = block_rows(n_it - 1, d, nb)
            add(dk_acc, last, carry_k[...])
            add(dv_acc, last, carry_v[...])

        def epi(i, carry):
            rows = pl.ds(pl.multiple_of(i * 1024, 1024), 1024)
            for acc, out, stage, scale in ((dq_acc, dq_ref, stage_a, 0.125), (dk_acc, dk_ref, stage_b, 1.0),
                                           (dv_acc, dv_ref, stage_a, 1.0)):
                for q in range(4):
                    stage[pl.ds(q, 256, stride=4), :] = acc[quarter_rows(i, q), :]
                out[rows, :] = (stage[...] if scale == 1.0 else stage[...] * scale).astype(bf16)
            return carry

        lax.fori_loop(0, s // 1024, epi, 0)

        if nsw:
            @pl.when(pl.program_id(0) == N_PAIRS - 1)
            def _():
                swap_copy.wait_send()
                swap_copy.wait_recv()

    col = lambda base: pl.BlockSpec((s, LANES), lambda h: (0, base + h))
    anyspec = pl.BlockSpec(memory_space=pl.ANY)
    swaps = [] if swap is None else [swap]
    outs = pl.pallas_call(
        body, name="attn_bwd", grid=(N_PAIRS,),
        in_specs=[col(0), col(0), col(0), col(24), col(0), col(0), col(0)] + [anyspec] * nsw,
        out_specs=[col(0)] * 4 + [anyspec] * nsw,
        out_shape=[SDS((s, D_ATTN), bf16)] * 4 + [SDS(a.shape, a.dtype) for a in swaps],
        scratch_shapes=[pltpu.VMEM((s, LANES), f32)] * 5
        + [pltpu.VMEM((BLK, 2 * LANES), f32)] * 2 + [pltpu.VMEM((BLK, LANES), f32)] * 2
        + [pltpu.VMEM((BLK, 2 * LANES), f32)] * 2 + [pltpu.VMEM((BLK, LANES), f32)] * 2
        + [pltpu.VMEM((BLK, 2 * LANES), bf16)] * 2 + [pltpu.VMEM((BLK, LANES), f32)] * 2
        + [pltpu.VMEM((BLK, 2 * LANES), bf16)] * 2 + [pltpu.VMEM((BLK, LANES), f32)] * 2
        + [pltpu.VMEM((2 * BLK, LANES), bf16)] * 2 + [pltpu.VMEM((BLK, LANES), bf16)] * 2
        + [pltpu.VMEM((2 * BLK, LANES), bf16)] * 2 + [pltpu.VMEM((BLK, LANES), bf16)] * 2
        + [pltpu.VMEM((2 * BLK, LANES), bf16)] * 2 + [pltpu.VMEM((BLK, LANES), f32)] * 2
        + [pltpu.VMEM((1024, LANES), f32)] * 2
        + [pltpu.SemaphoreType.DMA(())] * (2 * nsw),
        compiler_params=pltpu.CompilerParams(dimension_semantics=("arbitrary",)),
    )(*qkv, proj, o, lb, dmix, *swaps)
    return outs


def _ssd_bwd(proj, y, states, cv, dmix, conv_w, conv_b, dtb16, alog16, alog_f, d_f, nw, chip_sums=()):
    s = proj.shape[0]
    nc = s // CHUNK
    gw = D_SSM // N_GROUPS
    nx = len(chip_sums)

    def body(*refs):
        (xs_ref, bc_ref, _, _, dt_ref, z_ref, y_ref, st_ref, dm_ref, cw_ref, cb_ref, dtb_ref,
         alog16_ref, alogf_ref, df_ref, nw_ref, cv_ref) = refs[:17]
        cs_in = refs[17:17 + nx]
        out_ref, gconv_ref, gvec_ref, gdt_ref = refs[17 + nx:21 + nx]
        cs_out = refs[21 + nx:21 + 2 * nx]
        (dh_scr, head_scr, dcpad, da_scr, dxdt_scr, dbc_scr, emat_ref, fold_ref) = refs[21 + 2 * nx:29 + 2 * nx]
        cs_sems = refs[29 + 2 * nx:]
        i = pl.program_id(0)
        c = nc - 1 - i

        if nx:
            @pl.when(i == 0)
            def _():
                mine, sends, _ = _chip_exchange_copies(cs_in, cs_out, *cs_sems)
                for cp in mine + sends:
                    cp.start()

            @pl.when(i == nc - 1)
            def _():
                mine, sends, recvs = _chip_exchange_copies(cs_in, cs_out, *cs_sems)
                for cp in recvs:
                    cp.wait_recv()
                for cp in sends:
                    cp.wait_send()
                for cp in mine:
                    cp.wait()

        @pl.when(i == 0)
        def _():
            emat_ref[...] = _expand_mat()
            fold_ref[...] = _fold_mat()
            dh_scr[...] = jnp.zeros_like(dh_scr)
            head_scr[...] = jnp.zeros_like(head_scr)
            gconv_ref[...] = jnp.zeros_like(gconv_ref)
            gvec_ref[...] = jnp.zeros_like(gvec_ref)
            gdt_ref[...] = jnp.zeros_like(gdt_ref)

        cv = cv_ref[...]
        sig = _sigmoid(cv)
        xbc = cv * sig
        pre, dt_f, al_f, al_x, al_t = _decay_terms(dt_ref, dtb_ref, alog16_ref, emat_ref)
        head0 = _iota((CHUNK, LANES), 1) < HEAD_DIM
        sub = _iota((CHUNK, LANES), 0)
        last_row = sub == CHUNK - 1

        yv, z, dmx = y_ref[...], z_ref[...], dm_ref[...]
        sz = _sigmoid(z)
        silu = z * sz
        yz = yv * silu
        dyz_parts = []
        for g in range(N_GROUPS):
            gs = slice(g * gw, (g + 1) * gw)
            part = yz[:, gs]
            r = lax.rsqrt(jnp.mean(part * part, axis=-1, keepdims=True) + EPS)
            nh = part * r
            gvec_ref[0:1, gs] += jnp.sum(dmx[:, gs] * nh, axis=0, keepdims=True)
            dn = dmx[:, gs] * nw_ref[:, gs]
            dyz_parts.append(r * (dn - nh * jnp.mean(dn * nh, axis=-1, keepdims=True)))
        dyz = jnp.concatenate(dyz_parts, axis=1)
        dy = dyz * silu
        out_ref[:, 0:D_SSM] = (dyz * yv * (sz * (1.0 + z * (1.0 - sz)))).astype(bf16)

        x_all = xbc[:, 0:D_SSM]
        gvec_ref[2:3, :] += jnp.sum(dy * x_all, axis=0, keepdims=True)

        for g in range(N_GROUPS):
            bm = xbc[:, D_SSM + g * D_STATE: D_SSM + (g + 1) * D_STATE].astype(bf16)
            cm = xbc[:, D_SSM + (N_GROUPS + g) * D_STATE: D_SSM + (N_GROUPS + g + 1) * D_STATE].astype(bf16)
            gmat = _nt(cm, bm)
            dgm = jnp.zeros((CHUNK, CHUNK), f32)
            db = jnp.zeros((CHUNK, D_STATE), f32)
            dc = jnp.zeros((CHUNK, D_STATE), f32)
            for pair in range(4 * g, 4 * g + 4):
                sl = slice(pair * LANES, (pair + 1) * LANES)
                xp, dtp, alp, dyp = x_all[:, sl], dt_f[:, sl], al_f[:, sl], dy[:, sl]
                xdt = xp * dtp
                xdt16 = xdt.astype(bf16)
                al_last = alp[CHUNK - 1:CHUNK, :]
                e_l = jnp.exp(alp)
                wf = jnp.exp(al_last - alp)
                e_last = jnp.exp(al_last)
                hp = st_ref[:, sl]
                hp16 = hp.astype(bf16)
                dhn = dh_scr[:, sl]
                dhn16 = dhn.astype(bf16)
                y_off = e_l * _nn(cm, hp16)
                dch16 = (dyp * e_l).astype(bf16)
                dc = dc + _nt(dch16, hp16)
                dh_out = _tn(cm, dch16)
                dal = dyp * y_off
                xw16 = (wf * xdt).astype(bf16)
                db = db + _nt(xw16, dhn16)
                dxw = _nn(bm, dhn16)
                dxdt = dxw * wf
                dwf = dxw * xdt * wf
                dal = dal - dwf
                dal_last = jnp.sum(dwf, axis=0, keepdims=True) + jnp.sum(dhn * hp, axis=0, keepdims=True) * e_last
                dh_scr[:, sl] = e_last * dhn + dh_out
                for h in range(2):
                    mh = head0 if h == 0 else jnp.logical_not(head0)
                    dyh16 = jnp.where(mh, dyp, 0.0).astype(bf16)
                    lmat = _decay_mat(al_x, al_t, pair, h)
                    mm = gmat * lmat
                    dmm = _nt(dyh16, xdt16)
                    dxdt = dxdt + _tn(mm.astype(bf16), dyh16)
                    n16 = (dmm * mm).astype(bf16)
                    jh = jnp.where(mh, 1.0 / HEAD_DIM, 0.0).astype(bf16)
                    dal = dal + _nn(n16, jh) - _tn(n16, jh)
                    dgm = dgm + dmm * lmat
                da_scr[:, sl] = dal + jnp.where(last_row, dal_last, 0.0)
                dxdt_scr[:, sl] = dxdt
            dgm16 = dgm.astype(bf16)
            dbc_scr[:, g * D_STATE:(g + 1) * D_STATE] = db + _tn(dgm16, cm)
            dbc_scr[:, (N_GROUPS + g) * D_STATE:(N_GROUPS + g + 1) * D_STATE] = dc + _nn(dgm16, bm)

        sub_c, lane_c = _iota((CHUNK, CHUNK), 0), _iota((CHUNK, CHUNK), 1)
        tri_t = (lane_c >= sub_c).astype(bf16)
        dadt = _dot_01_left(tri_t, da_scr[...], 2)
        a_f = -jnp.exp(alogf_ref[...])
        dxdt_all = dxdt_scr[...]
        ddt_f = dxdt_all * x_all + a_f * dadt
        gvec_ref[1:2, :] += jnp.sum(dt_f * dadt, axis=0, keepdims=True) * a_f
        dx = df_ref[...] * dy + dxdt_all * dt_f
        ddt_raw = _dot_01(ddt_f, fold_ref[...], 2) * _sigmoid(pre)
        gdt_ref[0:1, :] += jnp.sum(ddt_raw, axis=0, keepdims=True)
        out_ref[:, D_SSM + D_CONV:D_SSM + D_CONV + LANES] = ddt_raw.astype(bf16)
        out_ref[:, D_SSM + D_CONV + LANES:] = jnp.zeros((CHUNK, 3 * LANES), bf16)

        dsil = sig * (1.0 + cv * (1.0 - sig))
        dcv_x = dx * dsil[:, 0:D_SSM]
        dcv_bc = dbc_scr[...] * dsil[:, D_SSM:]
        dcpad[0:CHUNK, 0:D_SSM] = dcv_x
        dcpad[0:CHUNK, D_SSM:] = dcv_bc
        dcpad[CHUNK:, :] = head_scr[...]
        dcp = dcpad[...]
        dcv = dcp[0:CHUNK]
        gconv_ref[4:5, :] += jnp.sum(dcv, axis=0, keepdims=True)
        x_raw = jnp.concatenate([xs_ref[...], bc_ref[...]], axis=1)
        draw = cw_ref[3:4, :] * dcv
        gconv_ref[3:4, :] += jnp.sum(dcv * x_raw, axis=0, keepdims=True)
        for j in range(3):
            ahead = pltpu.roll(dcp, CHUNK + 8 - (3 - j), 0)[0:CHUNK]
            draw = draw + cw_ref[j:j + 1, :] * ahead
            gconv_ref[j:j + 1, :] += jnp.sum(ahead * x_raw, axis=0, keepdims=True)
        head_scr[...] = dcv[0:8]
        out_ref[:, D_SSM:D_SSM + D_CONV] = draw.astype(bf16)

    order = lambda i: nc - 1 - i
    row = lambda w, cb=0: pl.BlockSpec((CHUNK, w), lambda i: (nc - 1 - i, cb))
    anyspec = pl.BlockSpec(memory_space=pl.ANY)
    outs = pl.pallas_call(
        body, name="ssd_bwd", grid=(nc,),
        in_specs=_ssd_in_specs(order) + [row(D_SSM), pl.BlockSpec((None, D_STATE, D_SSM), lambda i: (nc - 1 - i, 0, 0)),
                                         row(D_SSM, 1), _full((4, D_CONV)), _full((1, D_CONV)), _full((1, LANES)),
                                         _full((1, LANES)), _full((1, D_SSM)), _full((1, D_SSM)), _full((1, D_SSM)),
                                         row(D_CONV)]
        + [anyspec] * nx,
        out_specs=[row(3072), _full((8, D_CONV)), _full((8, D_SSM)), _full((8, LANES))] + [anyspec] * nx,
        out_shape=[SDS((s, 3072), bf16), SDS((8, D_CONV), f32), SDS((8, D_SSM), f32), SDS((8, LANES), f32)]
        + [SDS(a.shape, a.dtype) for a in chip_sums],
        scratch_shapes=[pltpu.VMEM((D_STATE, D_SSM), f32), pltpu.VMEM((8, D_CONV), f32),
                        pltpu.VMEM((8 + CHUNK, D_CONV), f32),
                        pltpu.VMEM((CHUNK, D_SSM), f32), pltpu.VMEM((CHUNK, D_SSM), f32),
                        pltpu.VMEM((CHUNK, 2 * N_GROUPS * D_STATE), f32),
                        pltpu.VMEM((LANES, 2 * D_SSM), bf16), pltpu.VMEM((D_SSM, LANES), bf16)]
        + (_chip_exchange_scratch(nx) if nx else []),
        compiler_params=pltpu.CompilerParams(dimension_semantics=("arbitrary",)),
    )(proj, proj, proj, proj, proj, proj, y, states, dmix, conv_w, conv_b, dtb16, alog16, alog_f, d_f, nw, cv,
      *chip_sums)
    return outs[0], outs[1], outs[2], outs[3], outs[4:]


def _col_blocks(parts, tile):
    counts = [p.shape[1] // tile for p in parts]
    offs = [sum(counts[:t]) for t in range(len(parts))]
    return offs, counts, sum(counts)


def _bcast_copies(src_ref, out_ref, send_sems, recv_sems, local_sem):
    x, y, c = _my_pos()
    me = 4 * x + 2 * y + c
    mine = pltpu.make_async_copy(src_ref, out_ref.at[me], local_sem)
    sends, recvs = [], []
    for k in range(1, N_DEV):
        to, frm = (me + k) % N_DEV, (me + N_DEV - k) % N_DEV
        sems = dict(send_sem=send_sems.at[k - 1], recv_sem=recv_sems.at[k - 1], device_id_type=MESH)
        sends.append(pltpu.make_async_remote_copy(
            src_ref=src_ref, dst_ref=out_ref.at[me], device_id=(to // 4, (to // 2) % 2, to % 2), **sems))
        recvs.append(pltpu.make_async_remote_copy(
            src_ref=src_ref, dst_ref=out_ref.at[frm], device_id=(x, y, c), **sems))
    return mine, sends, recvs


def _bcast_scratch():
    return [pltpu.SemaphoreType.DMA((N_DEV - 1,)), pltpu.SemaphoreType.DMA((N_DEV - 1,)), pltpu.SemaphoreType.DMA(())]


def _inproj_bwd(dparts, wt, x, nw, dres, chip_sums=(), pack=None):
    s, d = x.shape
    tm, tk = 1024, 1024
    offs, counts, nk = _col_blocks(dparts, tk)
    npart, nx = len(dparts), len(chip_sums)
    npk = 0 if pack is None else 1
    ni = s // tm

    def body(*refs):
        dp_refs = refs[:npart]
        w_ref, x_ref, nw_ref, dres_ref = refs[npart:npart + 4]
        pos = npart + 4
        cs_in, pos = refs[pos:pos + nx], pos + nx
        pack_in, pos = refs[pos:pos + npk], pos + npk
        (gx_ref, gnw_ref), pos = refs[pos:pos + 2], pos + 2
        cs_out, pos = refs[pos:pos + nx], pos + nx
        pack_out, pos = refs[pos:pos + 2 * npk], pos + 2 * npk
        acc, pos = refs[pos], pos + 1
        cs_sems, pos = refs[pos:pos + 3 * min(nx, 1)], pos + 3 * min(nx, 1)
        pk_refs = refs[pos:]
        i, k = pl.program_id(0), pl.program_id(1)

        def exchange():
            return _chip_exchange_copies(cs_in, cs_out, *cs_sems)

        def pack_copies():
            return _bcast_copies(pack_in[0], pack_out[0], *pk_refs[1:4])

        def gnw_copies():
            return _bcast_copies(pk_refs[0], pack_out[1], *pk_refs[4:7])

        @pl.when(jnp.logical_and(i == 0, k == 0))
        def _():
            gnw_ref[...] = jnp.zeros_like(gnw_ref)
            if nx:
                mine, sends, _ = exchange()
                for cp in mine + sends:
                    cp.start()
            if npk:
                mine, sends, _ = pack_copies()
                for cp in [mine] + sends:
                    cp.start()

        @pl.when(k == 0)
        def _():
            acc[...] = _nn(dp_refs[0][...], w_ref[...])

        for t in range(npart):
            @pl.when(jnp.logical_and(k >= max(offs[t], 1), k < offs[t] + counts[t]))
            def _(t=t):
                acc[...] += _nn(dp_refs[t][...], w_ref[...])

        @pl.when(k == nk - 1)
        def _():
            xv = x_ref[...]
            r = lax.rsqrt(jnp.mean(xv * xv, axis=-1, keepdims=True) + EPS)
            xn = xv * r
            du = acc[...]
            gnw_ref[0:1, :] += jnp.sum(du * xn, axis=0, keepdims=True)
            dn = du * nw_ref[...]
            gx_ref[...] = dres_ref[...] + r * (dn - xn * jnp.mean(dn * xn, axis=-1, keepdims=True))

        @pl.when(jnp.logical_and(i == ni - 1, k == nk - 1))
        def _():
            if npk:
                pk_refs[0][...] = gnw_ref[...]
                mine, sends, _ = gnw_copies()
                for cp in [mine] + sends:
                    cp.start()
            if nx:
                mine, sends, recvs = exchange()
                for cp in recvs:
                    cp.wait_recv()
                for cp in sends:
                    cp.wait_send()
                for cp in mine:
                    cp.wait()
            if npk:
                for copies in (pack_copies(), gnw_copies()):
                    mine, sends, recvs = copies
                    for cp in recvs:
                        cp.wait_recv()
                    for cp in sends:
                        cp.wait_send()
                    mine.wait()

    def piece(t):
        return pl.BlockSpec((tm, tk), lambda i, k: (i, jnp.clip(k - offs[t], 0, counts[t] - 1)))

    anyspec = pl.BlockSpec(memory_space=pl.ANY)
    packs = [] if pack is None else [pack]
    pack_shapes = [] if pack is None else [SDS((N_DEV,) + pack.shape, f32), SDS((N_DEV, 8, d), f32)]
    scratch = [pltpu.VMEM((tm, d), f32)] + (_chip_exchange_scratch(nx) if nx else [])
    if npk:
        scratch += [pltpu.VMEM((8, d), f32)] + _bcast_scratch() + _bcast_scratch()
    outs = pl.pallas_call(
        body, name="inproj_bwd", grid=(ni, nk),
        in_specs=[piece(t) for t in range(npart)] + [
            pl.BlockSpec((tk, d), lambda i, k: (k, 0)),
            pl.BlockSpec((tm, d), lambda i, k: (i, 0)), pl.BlockSpec((1, d), lambda i, k: (0, 0)),
            pl.BlockSpec((tm, d), lambda i, k: (i, 0))] + [anyspec] * (nx + npk),
        out_specs=[pl.BlockSpec((tm, d), lambda i, k: (i, 0)), pl.BlockSpec((8, d), lambda i, k: (0, 0))]
        + [anyspec] * (nx + 2 * npk),
        out_shape=[SDS((s, d), f32), SDS((8, d), f32)] + [SDS(a.shape, a.dtype) for a in chip_sums] + pack_shapes,
        scratch_shapes=scratch,
        compiler_params=pltpu.CompilerParams(dimension_semantics=("arbitrary", "arbitrary")),
    )(*dparts, wt, x, nw, dres, *chip_sums, *packs)
    return outs[0], outs[1], outs[2:2 + nx], outs[2 + nx:]


def _matmul_tn(a_parts, b_parts, name):
    tile, tk = 1024, 1024
    s = a_parts[0].shape[0]
    nk = s // tk
    na, nb = len(a_parts), len(b_parts)
    offs_a, counts_a, ni = _col_blocks(a_parts, tile)
    offs_b, counts_b, nj = _col_blocks(b_parts, tile)

    def body(*refs):
        a_refs, b_refs, o_ref = refs[:na], refs[na:na + nb], refs[na + nb]
        i, j = pl.program_id(0), pl.program_id(1)

        @pl.when(pl.program_id(2) == 0)
        def _():
            o_ref[...] = jnp.zeros_like(o_ref)

        for ta in range(na):
            for tb in range(nb):
                in_a = jnp.logical_and(i >= offs_a[ta], i < offs_a[ta] + counts_a[ta])
                in_b = jnp.logical_and(j >= offs_b[tb], j < offs_b[tb] + counts_b[tb])

                @pl.when(jnp.logical_and(in_a, in_b))
                def _(ta=ta, tb=tb):
                    o_ref[...] += _tn(a_refs[ta][...], b_refs[tb][...])

    def spec(offs, counts, t, axis):
        def index(i, j, k):
            pos = (i, j)[axis]
            mine = jnp.logical_and(pos >= offs[t], pos < offs[t] + counts[t])
            return jnp.where(mine, k, 0), jnp.clip(pos - offs[t], 0, counts[t] - 1)
        return pl.BlockSpec((tk, tile), index)

    return pl.pallas_call(
        body, name=name, grid=(ni, nj, nk),
        in_specs=[spec(offs_a, counts_a, t, 0) for t in range(na)] + [spec(offs_b, counts_b, t, 1) for t in range(nb)],
        out_specs=pl.BlockSpec((tile, tile), lambda i, j, k: (i, j)),
        out_shape=SDS((ni * tile, nj * tile), f32),
        compiler_params=pltpu.CompilerParams(dimension_semantics=("parallel", "parallel", "arbitrary")),
    )(*a_parts, *b_parts)


def _adamw(w, g, m, v):
    m = ADAM_B1 * m + (1.0 - ADAM_B1) * g
    v = ADAM_B2 * v + (1.0 - ADAM_B2) * (g * g)
    m_hat = m / (1.0 - ADAM_B1 ** ADAM_STEP)
    v_hat = v / (1.0 - ADAM_B2 ** ADAM_STEP)
    delta = -ADAM_LR * (m_hat / (jnp.sqrt(v_hat) + ADAM_EPS) + ADAM_WD * w)
    return delta, m, v


def _sum_adamw(parts, w, m, v, name):
    r, c = w.shape
    tc = 256

    def body(p_ref, w_ref, m_ref, v_ref, g_ref, d_ref, nm_ref, nv_ref):
        g = p_ref[0].astype(f32)
        for q in range(1, 4):
            g = g + p_ref[q].astype(f32)
        g_ref[...] = g
        d_ref[...], nm_ref[...], nv_ref[...] = _adamw(w_ref[...], g, m_ref[...], v_ref[...])

    blk = pl.BlockSpec((r, tc), lambda i: (0, i))
    return pl.pallas_call(
        body, name=name, grid=(c // tc,),
        in_specs=[pl.BlockSpec((4, r, tc), lambda i: (0, 0, i)), blk, blk, blk],
        out_specs=[blk] * 4, out_shape=[SDS((r, c), f32)] * 4,
        compiler_params=pltpu.CompilerParams(dimension_semantics=("parallel",)),
    )(parts, w, m, v)


def _sum_small(parts, pre_blocks):
    def body(p_ref, b_ref, o_ref):
        t = p_ref[0]
        pre = b_ref[0]
        for j in range(1, N_DEV):
            t = t + p_ref[j]
            pre = pre + b_ref[j]
        o_ref[...] = t
        o_ref[5:6, 0:D_MODEL] = pre[0:1, :]
        row_h = _iota((D_SSM, LANES), 0) // HEAD_DIM
        fold = (row_h == _iota((D_SSM, LANES), 1)).astype(f32)
        lower = t[8:16, 0:LANES]
        folded = _nn_hi(t[8:16, 0:D_SSM], fold)
        loss = jnp.sum(t[11:12, 0:D_MODEL], axis=1, keepdims=True) * (0.5 / D_MODEL)
        row = _iota((8, LANES), 0)
        o_ref[8:16, 0:LANES] = jnp.where(row < 2, folded, jnp.where(row == 4, loss, lower))

    return pl.pallas_call(body, name="sum_small", out_shape=SDS((PACK_ROWS, PACK_W), f32),
                          in_specs=[pl.BlockSpec(memory_space=pltpu.VMEM)] * 2,
                          out_specs=pl.BlockSpec(memory_space=pltpu.VMEM))(parts, pre_blocks)


def _adamw_small(w, g, m, v):
    def body(w_ref, g_ref, m_ref, v_ref, d_ref, nm_ref, nv_ref):
        d_ref[...], nm_ref[...], nv_ref[...] = _adamw(w_ref[...], g_ref[...], m_ref[...], v_ref[...])

    vm = pl.BlockSpec(memory_space=pltpu.VMEM)
    return pl.pallas_call(body, name="adamw_small", out_shape=[SDS(w.shape, f32)] * 3,
                          in_specs=[vm] * 4, out_specs=[vm] * 3)(w, g, m, v)


def _pad_lanes(v, width):
    return jnp.pad(v, ((0, 0), (0, width - v.shape[1])))


def _local_step(x, tgt, norm_pre_w, wt, conv_w, conv_b, dt_bias, a_log, d_skip, ssm_norm_w, wo, norm_post_w, sharded):
    dtb16 = _pad_lanes(dt_bias, LANES)
    alog16 = _pad_lanes(a_log, LANES)
    alog_f = jnp.repeat(a_log, HEAD_DIM, axis=1)
    d_f = jnp.repeat(d_skip, HEAD_DIM, axis=1)

    shard_out = wo.shape[0]
    if sharded:
        proj, u, (g_out, g_cw) = _prenorm_inproj(x, norm_pre_w, wt, gather=(wo, conv_w))
        wo = g_out.reshape(N_DEV * shard_out, D_MODEL)
        conv_w = g_cw.transpose(1, 0, 2).reshape(4, D_CONV)
    else:
        proj, u, _ = _prenorm_inproj(x, norm_pre_w, wt)
    o, lb, mix_a, *qkv = _attn_fwd(proj)
    mix_s, y, states, cv = _ssd_fwd(proj, conv_w, conv_b, dtb16, alog16, alog_f, d_f, ssm_norm_w)
    dmix, dres, acc_post, dw_out = _outproj_loss(mix_a, mix_s, wo, x, tgt, norm_post_w)
    ssd_args = (proj, y, states, cv, dmix, conv_w, conv_b, dtb16, alog16, alog_f, d_f, ssm_norm_w)
    if sharded:
        dq, dk, dv, dg, got_out = _attn_bwd(proj, qkv, o, lb, dmix, swap=dw_out)
        chip_out = _chip_sum(dw_out, got_out, shard_out, "chip_sum_w_out")
        dzxd, g_conv, g_vec, g_dt, (parts_out,) = _ssd_bwd(*ssd_args, chip_sums=[chip_out])
    else:
        dq, dk, dv, dg = _attn_bwd(proj, qkv, o, lb, dmix)
        dzxd, g_conv, g_vec, g_dt, _ = _ssd_bwd(*ssd_args)
    dparts = [dq, dk, dv, dg, dzxd]

    def pack(g_pre_row):
        return jnp.concatenate(
            [g_conv[0:5], g_pre_row, _pad_lanes(g_vec[0:1], PACK_W), _pad_lanes(acc_post[1:2], PACK_W),
             _pad_lanes(g_vec[1:3], PACK_W), _pad_lanes(g_dt[0:1], PACK_W), _pad_lanes(acc_post[0:1], PACK_W),
             jnp.zeros((4, PACK_W), f32)], axis=0)

    if sharded:
        dw_in, got_in = _dw_in_swap(dparts, u)
        chip_in = _chip_sum(dw_in, got_in, D_IN_PROJ // N_DEV, "chip_sum_w_in")
        grad_x, _, (parts_in,), small = _inproj_bwd(dparts, wt, x, norm_pre_w, dres, [chip_in],
                                                    pack(jnp.zeros((1, PACK_W), f32)))
        return grad_x, (parts_in, parts_out), small
    dw_in = _matmul_tn(dparts, [u], "dw_in")
    grad_x, g_pre, _, _ = _inproj_bwd(dparts, wt, x, norm_pre_w, dres)
    return grad_x, (dw_in, dw_out), pack(_pad_lanes(g_pre[0:1], PACK_W))


def kernel(x, norm_pre_w, w_in, conv_w, conv_b, dt_bias, a_log, d_skip, ssm_norm_w, w_out, norm_post_w, loss_target, m_norm_pre_w, m_w_in, m_conv_w, m_conv_b, m_dt_bias, m_a_log, m_d_skip, m_ssm_norm_w, m_w_out, m_norm_post_w, v_norm_pre_w, v_w_in, v_conv_w, v_conv_b, v_dt_bias, v_a_log, v_d_skip, v_ssm_norm_w, v_w_out, v_norm_post_w):
    shard_cv = conv_w.shape[2]
    me = 4 * lax.axis_index("x") + 2 * lax.axis_index("y") + lax.axis_index("c")

    g_in, = _all_gather([w_in[0].T.astype(bf16)])
    wt = _assemble_wt(g_in)

    grad_x, (parts_in, parts_out), (parts_small, pre_blocks) = _local_step(
        x[0], loss_target[0], norm_pre_w, wt, conv_w[0], conv_b, dt_bias, a_log, d_skip, ssm_norm_w,
        w_out[0].astype(bf16), norm_post_w, sharded=True)

    g_w_in, d_w_in, nm_w_in, nv_w_in = (a.T for a in _sum_adamw(
        parts_in, w_in[0].T, m_w_in[0].T, v_w_in[0].T, "sum_adamw_w_in"))
    g_w_out, d_w_out, nm_w_out, nv_w_out = _sum_adamw(parts_out, w_out[0], m_w_out[0], v_w_out[0], "sum_adamw_w_out")
    tot = _sum_small(parts_small, pre_blocks)

    g_cw_all = tot[0:4]
    small_g = {
        "conv_w": lax.dynamic_slice(g_cw_all, (0, me * shard_cv), (4, shard_cv)),
        "conv_b": tot[4:5], "norm_pre_w": tot[5:6, :D_MODEL], "ssm_norm_w": tot[6:7, :D_SSM],
        "norm_post_w": tot[7:8, :D_MODEL], "a_log": tot[8:9, :16], "d_skip": tot[9:10, :16], "dt_bias": tot[10:11, :16],
    }
    loss = tot[12, 0]
    small_w = {"conv_w": (conv_w[0], m_conv_w[0], v_conv_w[0]), "conv_b": (conv_b, m_conv_b, v_conv_b),
               "norm_pre_w": (norm_pre_w, m_norm_pre_w, v_norm_pre_w), "ssm_norm_w": (ssm_norm_w, m_ssm_norm_w, v_ssm_norm_w),
               "norm_post_w": (norm_post_w, m_norm_post_w, v_norm_post_w), "a_log": (a_log, m_a_log, v_a_log),
               "d_skip": (d_skip, m_d_skip, v_d_skip), "dt_bias": (dt_bias, m_dt_bias, v_dt_bias)}
    names = list(small_w)
    sizes = [small_g[k].size for k in names]
    tot_size = sum(sizes)
    pad_to = -(-tot_size // 1024) * 1024

    def flat(arrs):
        v = jnp.concatenate([a.reshape(-1) for a in arrs])
        return jnp.pad(v, (0, pad_to - tot_size)).reshape(pad_to // LANES, LANES)

    fw = flat([small_w[k][0] for k in names])
    fg = flat([small_g[k] for k in names])
    fm = flat([small_w[k][1] for k in names])
    fv = jnp.pad(jnp.concatenate([small_w[k][2].reshape(-1) for k in names]), (0, pad_to - tot_size),
                 constant_values=1.0).reshape(pad_to // LANES, LANES)
    fd, fnm, fnv = _adamw_small(fw, fg, fm, fv)

    def unflat(f):
        out, off = {}, 0
        v = f.reshape(-1)
        for k, n in zip(names, sizes):
            out[k] = v[off:off + n].reshape(small_g[k].shape)
            off += n
        return out

    sd, snm, snv = unflat(fd), unflat(fnm), unflat(fnv)
    lead = lambda a: a[None]
    order = ["norm_pre_w", "w_in", "conv_w", "conv_b", "dt_bias", "a_log", "d_skip", "ssm_norm_w", "w_out", "norm_post_w"]
    grads = dict(small_g, w_in=g_w_in, w_out=g_w_out)
    deltas = dict(sd, w_in=d_w_in, w_out=d_w_out)
    new_m = dict(snm, w_in=nm_w_in, w_out=nm_w_out)
    new_v = dict(snv, w_in=nv_w_in, w_out=nv_w_out)

    def shaped(dct, k):
        a = dct[k]
        return lead(a) if k in ("w_in", "w_out", "conv_w") else a

    return (loss, grad_x[None], *[shaped(grads, k) for k in order], *[shaped(deltas, k) for k in order],
            *[shaped(new_m, k) for k in order], *[shaped(new_v, k) for k in order])
```

```python
import jax
import jax.numpy as jnp
from jax import lax
from jax.experimental import pallas as pl
from jax.experimental.pallas import tpu as pltpu

f32, bf16 = jnp.float32, jnp.bfloat16
SDS = jax.ShapeDtypeStruct
HIGHEST = lax.Precision.HIGHEST
MESH = pl.DeviceIdType.MESH

N_DEV = 8
D_MODEL = 1024
D_ATTN = 1024
D_SSM = 1024
HEAD_DIM = 64
N_PAIRS = 8
D_STATE = 128
N_GROUPS = 2
D_CONV = D_SSM + 2 * N_GROUPS * D_STATE
D_IN_PROJ = 4 * D_ATTN + D_SSM + D_CONV + 16
NP = 7168
CHUNK = 128
BLK = 128
DILATIONS = (1, 4, 16)
EPS = 1e-6
LANES = 128
COL_Z, COL_XS, COL_BC, COL_DT = 4096, 5120, 6144, 6656

ADAM_LR, ADAM_B1, ADAM_B2, ADAM_EPS, ADAM_WD, ADAM_STEP = 0.001, 0.9, 0.999, 1e-08, 0.01, 10

PACK_ROWS, PACK_W = 16, 1536


def _nt(a, b):
    return lax.dot_general(a, b, (((1,), (1,)), ((), ())), preferred_element_type=f32)


def _tn(a, b):
    return lax.dot_general(a, b, (((0,), (0,)), ((), ())), preferred_element_type=f32)


def _nn(a, b):
    return jnp.dot(a, b, preferred_element_type=f32)


def _nn_hi(a, b):
    return jnp.dot(a, b, precision=HIGHEST, preferred_element_type=f32)


def _sigmoid(x):
    return 1.0 / (1.0 + jnp.exp(-x))


def _softplus(x):
    return jnp.maximum(x, 0.0) + jnp.log1p(jnp.exp(-jnp.abs(x)))


def _iota(shape, dim):
    return lax.broadcasted_iota(jnp.int32, shape, dim)


def _my_pos():
    return lax.axis_index("x"), lax.axis_index("y"), lax.axis_index("c")


GATHER_SEMS = 9


def _gather_phases(ins, outs, send_sems, recv_sems, local_sems):
    n, ns = len(ins), GATHER_SEMS
    x, y, c = _my_pos()
    me, sibling = (x, y, c), (x, y, 1 - c)
    xn, yn, diag = (1 - x, y), (x, 1 - y), (1 - x, 1 - y)

    def slot(a, px, py, pc):
        return outs[a].at[4 * px + 2 * py + pc]

    def part(a, ref, h):
        width = ins[a].shape[-1]
        if width % (2 * LANES):
            return ref if h == 1 else None
        return ref.at[:, pl.ds(h * (width // 2), width // 2)]

    def copy(a, k, block, to, src=None, h=None):
        src_ref = slot(a, *block) if src is None else src
        dst_ref = slot(a, *block)
        if h is not None:
            src_ref, dst_ref = part(a, src_ref, h), part(a, dst_ref, h)
            if src_ref is None:
                return None
        return pltpu.make_async_remote_copy(
            src_ref=src_ref, dst_ref=dst_ref, send_sem=send_sems.at[ns * a + k], recv_sem=recv_sems.at[ns * a + k],
            device_id=to, device_id_type=MESH)

    def mine():
        return [pltpu.make_async_copy(ins[a], slot(a, *me), local_sems.at[a]) for a in range(n)]

    def own_sends(a):
        return [copy(a, 0, me, sibling, src=ins[a]), copy(a, 1, me, (*xn, c), src=ins[a]),
                copy(a, 2, me, (*yn, c), src=ins[a])]

    def neighbour_relays(a):
        return [copy(a, 4, (*xn, c), sibling), copy(a, 7, (*xn, c), (*yn, c), h=1),
                copy(a, 5, (*yn, c), sibling), copy(a, 8, (*yn, c), (*xn, c), h=0)]

    def diagonal_halves(a):
        return [copy(a, k, (*diag, c), me, h=h) for k, h in ((8, 0), (7, 1))]

    def start_all(cps):
        for cp in cps:
            if cp is not None:
                cp.start()

    def phase0():
        start_all(mine())
        for a in range(n):
            start_all(own_sends(a))

    def phase1():
        for a in range(n):
            copy(a, 1, (*xn, c), me).wait_recv()
            copy(a, 2, (*yn, c), me).wait_recv()
            start_all(neighbour_relays(a))

    def phase2():
        for a in range(n):
            for cp in diagonal_halves(a):
                if cp is not None:
                    cp.wait_recv()
            copy(a, 6, (*diag, c), sibling).start()

    def finish():
        for a in range(n):
            copy(a, 0, sibling, me).wait_recv()
            for j, chip in enumerate((xn, yn, diag)):
                copy(a, 4 + j, (*chip, 1 - c), me).wait_recv()
        for a in range(n):
            for cp in own_sends(a) + neighbour_relays(a) + [copy(a, 6, (*diag, c), sibling)]:
                if cp is not None:
                    cp.wait_send()
        for cp in mine():
            cp.wait()

    return phase0, phase1, phase2, finish


def _gather_scratch(n):
    return [pltpu.SemaphoreType.DMA((GATHER_SEMS * n,)), pltpu.SemaphoreType.DMA((GATHER_SEMS * n,)),
            pltpu.SemaphoreType.DMA((n,))]


def _all_gather(arrs):
    n = len(arrs)

    def body(*refs):
        for phase in _gather_phases(refs[:n], refs[n:2 * n], *refs[2 * n:]):
            phase()

    anyspec = pl.BlockSpec(memory_space=pl.ANY)
    return pl.pallas_call(
        body, name="weights_all_gather",
        out_shape=[SDS((N_DEV,) + a.shape, a.dtype) for a in arrs],
        in_specs=[anyspec] * n, out_specs=[anyspec] * n, scratch_shapes=_gather_scratch(n),
    )(*arrs)


def _dw_in_swap(a_parts, u):
    tile, tk = 1024, 1024
    s = u.shape[0]
    nk = s // tk
    na = len(a_parts)
    offs, counts, ni = _col_blocks(a_parts, tile)

    def body(*refs):
        a_refs, u_ref = refs[:na], refs[na]
        dw_ref, got_ref = refs[na + 1:na + 3]
        acc, stage, local_sems, send_sems, recv_sem = refs[na + 3:]
        i, k = pl.program_id(0), pl.program_id(1)
        x, y, c = _my_pos()
        par = i % 2

        def tile_copies(t, p):
            rows = pl.ds(pl.multiple_of(t * tile, tile), tile)
            loc = pltpu.make_async_copy(stage.at[p], dw_ref.at[rows], local_sems.at[p])
            rem = pltpu.make_async_remote_copy(
                src_ref=stage.at[p], dst_ref=got_ref.at[rows], send_sem=send_sems.at[p], recv_sem=recv_sem,
                device_id=(x, y, 1 - c), device_id_type=MESH)
            return loc, rem

        @pl.when(k == 0)
        def _():
            acc[...] = jnp.zeros((tile, tile), f32)

        for t in range(na):
            @pl.when(jnp.logical_and(i >= offs[t], i < offs[t] + counts[t]))
            def _(t=t):
                acc[...] += _tn(a_refs[t][...], u_ref[pl.ds(pl.multiple_of(k * tk, tk), tk), :])

        @pl.when(k == nk - 1)
        def _():
            @pl.when(i >= 2)
            def _():
                loc, rem = tile_copies(i - 2, par)
                loc.wait()
                rem.wait_send()
            stage[par] = acc[...]
            loc, rem = tile_copies(i, par)
            loc.start()
            rem.start()

        @pl.when(jnp.logical_and(i == ni - 1, k == nk - 1))
        def _():
            for t in (ni - 2, ni - 1):
                loc, rem = tile_copies(t, t % 2)
                loc.wait()
                rem.wait_send()
            pltpu.make_async_remote_copy(src_ref=dw_ref, dst_ref=got_ref, send_sem=send_sems.at[0], recv_sem=recv_sem,
                                         device_id=(x, y, c), device_id_type=MESH).wait_recv()

    def a_spec(t):
        def index(i, k):
            mine = jnp.logical_and(i >= offs[t], i < offs[t] + counts[t])
            return jnp.where(mine, k, 0), jnp.clip(i - offs[t], 0, counts[t] - 1)
        return pl.BlockSpec((tk, tile), index)

    anyspec = pl.BlockSpec(memory_space=pl.ANY)
    return pl.pallas_call(
        body, name="dw_in_swap", grid=(ni, nk),
        in_specs=[a_spec(t) for t in range(na)] + [pl.BlockSpec((s, tile), lambda i, k: (0, 0))],
        out_specs=[anyspec] * 2,
        out_shape=[SDS((ni * tile, tile), f32), SDS((ni * tile, tile), f32)],
        scratch_shapes=[pltpu.VMEM((tile, tile), f32), pltpu.VMEM((2, tile, tile), f32), pltpu.SemaphoreType.DMA((2,)),
                        pltpu.SemaphoreType.DMA((2,)), pltpu.SemaphoreType.DMA(())],
        compiler_params=pltpu.CompilerParams(dimension_semantics=("arbitrary", "arbitrary")),
    )(*a_parts, u)


def _chip_sum(mine, got, rows, name):
    r, cdim = mine.shape
    tc = LANES

    def body(m_ref, g_ref, s16_ref):
        c = lax.axis_index("c")
        for q in range(4):
            blk = pl.ds(rows * (2 * q + c), rows)
            s16_ref[q] = (m_ref[blk, :] + g_ref[blk, :]).astype(bf16)

    col = pl.BlockSpec((r, tc), lambda i: (0, i))
    return pl.pallas_call(
        body, name=name, grid=(cdim // tc,), in_specs=[col, col],
        out_specs=pl.BlockSpec((4, rows, tc), lambda i: (0, 0, i)), out_shape=SDS((4, rows, cdim), bf16),
        compiler_params=pltpu.CompilerParams(dimension_semantics=("parallel",)),
    )(mine, got)


def _assemble_wt(shards):
    nd, rows, cdim = shards.shape
    tc = 256

    def body(g_ref, o_ref):
        for j in range(nd):
            o_ref[pl.ds(rows * j, rows), :] = g_ref[j]
        o_ref[pl.ds(nd * rows, NP - nd * rows), :] = jnp.zeros((NP - nd * rows, tc), shards.dtype)

    return pl.pallas_call(
        body, name="assemble_w_in", grid=(cdim // tc,),
        in_specs=[pl.BlockSpec((nd, rows, tc), lambda i: (0, 0, i))],
        out_specs=pl.BlockSpec((NP, tc), lambda i: (0, i)), out_shape=SDS((NP, cdim), shards.dtype),
        compiler_params=pltpu.CompilerParams(dimension_semantics=("parallel",)),
    )(shards)


def _chip_exchange_copies(ins, outs, send_sems, recv_sems, local_sems):
    nb = len(ins)
    x, y, c = _my_pos()
    my_q = 2 * x + y
    mine = [pltpu.make_async_copy(ins[a].at[my_q], outs[a].at[my_q], local_sems.at[a]) for a in range(nb)]
    sends, recvs = [], []
    for k in range(1, 4):
        to, frm = (my_q + k) % 4, (my_q + 4 - k) % 4
        for a in range(nb):
            sems = dict(send_sem=send_sems.at[3 * a + k - 1], recv_sem=recv_sems.at[3 * a + k - 1], device_id_type=MESH)
            sends.append(pltpu.make_async_remote_copy(
                src_ref=ins[a].at[to], dst_ref=outs[a].at[my_q], device_id=(to // 2, to % 2, c), **sems))
            recvs.append(pltpu.make_async_remote_copy(
                src_ref=ins[a].at[frm], dst_ref=outs[a].at[frm], device_id=(x, y, c), **sems))
    return mine, sends, recvs


def _chip_exchange_scratch(nb):
    return [pltpu.SemaphoreType.DMA((3 * nb,)), pltpu.SemaphoreType.DMA((3 * nb,)), pltpu.SemaphoreType.DMA((nb,))]


def _prenorm_inproj(x, nw, wt, gather=()):
    s, d = x.shape
    npad = wt.shape[0]
    tm, tn = 1024, 1024
    ng = len(gather)
    ni, nj = s // tm, npad // tn

    def body(x_ref, nw_ref, w_ref, *refs):
        g_in, (proj_ref, u_ref), g_out, sems = refs[:ng], refs[ng:ng + 2], refs[ng + 2:2 * ng + 2], refs[2 * ng + 2:]
        i, j = pl.program_id(0), pl.program_id(1)
        if ng:
            phases = _gather_phases(g_in, g_out, *sems)
            for step, phase in enumerate(phases[:3]):
                @pl.when(jnp.logical_and(i == step, j == 0))
                def _(phase=phase):
                    phase()

        @pl.when(j == 0)
        def _():
            xv = x_ref[...]
            r = lax.rsqrt(jnp.mean(xv * xv, axis=-1, keepdims=True) + EPS)
            u_ref[...] = (xv * r * nw_ref[...]).astype(bf16)
        proj_ref[...] = _nt(u_ref[...], w_ref[pl.ds(pl.multiple_of(j * tn, tn), tn), :])

        if ng:
            @pl.when(jnp.logical_and(i == ni - 1, j == nj - 1))
            def _():
                phases[3]()

    anyspec = pl.BlockSpec(memory_space=pl.ANY)
    outs = pl.pallas_call(
        body, name="prenorm_inproj", grid=(ni, nj),
        in_specs=[pl.BlockSpec((tm, d), lambda i, j: (i, 0)), pl.BlockSpec((1, d), lambda i, j: (0, 0)),
                  pl.BlockSpec((npad, d), lambda i, j: (0, 0))] + [anyspec] * ng,
        out_specs=[pl.BlockSpec((tm, tn), lambda i, j: (i, j)), pl.BlockSpec((tm, d), lambda i, j: (i, 0))]
        + [anyspec] * ng,
        out_shape=[SDS((s, npad), f32), SDS((s, d), bf16)] + [SDS((N_DEV,) + a.shape, a.dtype) for a in gather],
        scratch_shapes=_gather_scratch(ng) if ng else [],
        compiler_params=pltpu.CompilerParams(dimension_semantics=("arbitrary", "arbitrary")),
    )(x, nw, wt, *gather)
    return outs[0], outs[1], outs[2:]


def _attn_consts():
    head0 = _iota((BLK, LANES), 1) < HEAD_DIM
    tri2 = (_iota((BLK, 2 * LANES), 1) % LANES) <= _iota((BLK, 2 * LANES), 0)
    ones2 = ((_iota((LANES, 2 * LANES), 0) < HEAD_DIM) == (_iota((LANES, 2 * LANES), 1) < LANES)).astype(bf16)
    rmat = ((_iota((2 * LANES, LANES), 0) < LANES) == (_iota((2 * LANES, LANES), 1) < HEAD_DIM)).astype(bf16)
    bones = ((_iota((LANES, LANES), 0) < HEAD_DIM) == (_iota((LANES, LANES), 1) < HEAD_DIM)).astype(bf16)
    return head0, tri2, ones2, rmat, bones


def _stack_heads(x16, head0):
    zero = jnp.zeros_like(x16)
    return jnp.concatenate([jnp.where(head0, x16, zero), jnp.where(head0, zero, x16)], axis=0)


def _bf16_terms(x, terms):
    out = []
    for _ in range(terms):
        t = x.astype(bf16)
        out.append(t)
        x = x - t.astype(f32)
    return out


def _dot_01(x, w16, terms):
    return _nn(jnp.concatenate(_bf16_terms(x, terms), axis=1), jnp.concatenate([w16] * terms, axis=0))


def _split_dot_sum(x, w16):
    hi, lo = _bf16_terms(x, 2)
    return _nn(hi, w16) + _nn(lo, w16)


def _dot_01_left(w16, x, terms):
    return _nn(jnp.concatenate([w16] * terms, axis=1), jnp.concatenate(_bf16_terms(x, terms), axis=0))


def _quarter_rows(i, q):
    return pl.ds(pl.multiple_of((i // 2) * 2048 + q * 512 + (i % 2) * 256, 256), 256)


def _token_rows(i, q):
    return pl.ds(i * 1024 + q, 256, stride=4)


def _quarter_block(i, d, nb):
    assert isinstance(i, int)
    r, blk = i // nb, i % nb
    if d == 1:
        runs = [pl.ds((blk // 16) * 2048 + q * 512 + (blk % 16) * 32, 32) for q in range(4)]
    elif d == 4:
        runs = [pl.ds((blk // 4) * 2048 + r * 512 + (blk % 4) * BLK, BLK)]
    else:
        runs = [pl.ds(blk * 2048 + (r % 4) * 512 + r // 4, BLK, stride=4)]
    return runs, blk > 0


def _if_prev(has_prev, x, fill):
    return x if has_prev else jnp.full_like(x, fill)


def _quarter_mask():
    order = lambda n: 4 * (n % 32) + n // 32
    return order(_iota((BLK, 2 * LANES), 1) % LANES) <= order(_iota((BLK, 2 * LANES), 0))


def _load_runs(ref, runs):
    parts = [ref[run, :] for run in runs]
    return parts[0] if len(parts) == 1 else jnp.concatenate(parts, axis=0)


def _store_runs(ref, runs, val):
    n = BLK // len(runs)
    for t, run in enumerate(runs):
        ref[run, :] = val[t * n:(t + 1) * n]


def _add_runs(ref, runs, val):
    n = BLK // len(runs)
    for t, run in enumerate(runs):
        ref[run, :] += val[t * n:(t + 1) * n]


def _attn_fwd(proj):
    s = proj.shape[0]
    n_it = s // BLK

    def body(q_in, k_in, v_in, g_ref, o_ref, l_ref, mix_ref, q_ref, k_ref, v_ref, op0, op1, op2, lp0, lp1, lp2,
             s_a, s_b, sd_a, sd_b, p_a, p_b, m_a, m_b, pd_a, pd_b, k_a, k_b, v_a, v_b, stage):
        op_refs, lp_refs = (op0, op1, op2), (lp0, lp1, lp2)
        head0, tri2_t, ones2, rmat, _ = _attn_consts()
        tri2_q = _quarter_mask()

        def reorder(i, carry):
            for src, dst, scale in ((q_in, q_ref, 0.125), (k_in, k_ref, 1.0), (v_in, v_ref, 1.0)):
                for q in range(4):
                    t = src[_token_rows(i, q), :]
                    dst[_quarter_rows(i, q), :] = t if scale == 1.0 else t * scale
            return carry

        lax.fori_loop(0, s // 1024, reorder, 0)
        score_bufs, prob_bufs = ((s_a, sd_a), (s_b, sd_b)), ((p_a, m_a, pd_a), (p_b, m_b, pd_b))
        k_bufs, v_bufs = (k_a, k_b), (v_a, v_b)
        for buf in k_bufs + v_bufs:
            buf[...] = jnp.zeros_like(buf)

        def unstack(st16):
            return st16[:BLK] + st16[BLK:]

        def scores(i, par, d, nb):
            rows, has_prev = _quarter_block(i, d, nb)
            tri2 = tri2_q if d == 1 else tri2_t
            s_buf, sd_buf = score_bufs[par]
            qs = _load_runs(q_ref, rows)
            qs16 = qs.astype(bf16)
            kst_c = _stack_heads(_load_runs(k_ref, rows).astype(bf16), head0)
            kst_p = k_bufs[1 - par][...]
            k_bufs[par][...] = kst_c
            sc = _nt(qs16, kst_c)
            sp = _nt(qs16, kst_p)
            s_buf[...] = jnp.where(tri2, sc, _if_prev(has_prev, sp, -jnp.inf))
            sd = _nn((qs * unstack(kst_p).astype(f32)).astype(bf16), ones2)
            sd_buf[...] = _if_prev(has_prev, sd, -jnp.inf)

        def softmax(bufs_in, bufs_out):
            s_buf, sd_buf = bufs_in
            p_buf, m_buf, pd_buf = bufs_out
            sc, sd2 = s_buf[...], sd_buf[...]
            m0 = jnp.max(sc[:, :LANES], axis=1, keepdims=True)
            m1 = jnp.max(sc[:, LANES:], axis=1, keepdims=True)
            m2 = jnp.concatenate([jnp.broadcast_to(m0, (BLK, LANES)), jnp.broadcast_to(m1, (BLK, LANES))], axis=1)
            m2 = jnp.maximum(m2, sd2)
            p_buf[...] = jnp.exp(sc - m2).astype(bf16)
            m_pair = jnp.where(head0, m2[:, :LANES], m2[:, LANES:])
            m_buf[...] = m_pair
            pd_buf[...] = jnp.exp(jnp.where(head0, sd2[:, :LANES], sd2[:, LANES:]) - m_pair)

        def output(i, par, d, nb, p):
            rows, _ = _quarter_block(i, d, nb)
            tri2 = tri2_q if d == 1 else tri2_t
            p_buf, m_buf, pd_buf = prob_bufs[par]
            vst_c = _stack_heads(_load_runs(v_ref, rows).astype(bf16), head0)
            vst_p = v_bufs[1 - par][...]
            v_bufs[par][...] = vst_c
            pt16, pd = p_buf[...], pd_buf[...]
            zero = jnp.zeros_like(pt16)
            o = (_nn(jnp.where(tri2, pt16, zero), vst_c) + _nn(jnp.where(tri2, zero, pt16), vst_p)
                 + pd * unstack(vst_p).astype(f32))
            l = _nn(pt16, rmat) + pd
            _store_runs(op_refs[p], rows, o / l)
            _store_runs(lp_refs[p], rows, m_buf[...] + jnp.log(l))

        for p, d in enumerate(DILATIONS):
            nb = s // (BLK * d)
            scores(0, 0, d, nb)
            scores(1, 1, d, nb)
            softmax(score_bufs[0], prob_bufs[0])

            for t in range(2, n_it):
                par = t % 2
                scores(t, par, d, nb)
                output(t - 2, par, d, nb, p)
                softmax(score_bufs[1 - par], prob_bufs[1 - par])
            output(n_it - 2, 0, d, nb, p)
            softmax(score_bufs[1], prob_bufs[1])
            output(n_it - 1, 1, d, nb, p)

        def merge(i, carry):
            for q in range(4):
                rows, tokens = _quarter_rows(i, q), _token_rows(i, q)
                l0, l1, l2 = lp0[rows, :], lp1[rows, :], lp2[rows, :]
                m = jnp.maximum(jnp.maximum(l0, l1), l2)
                e0, e1, e2 = jnp.exp(l0 - m), jnp.exp(l1 - m), jnp.exp(l2 - m)
                z = e0 + e1 + e2
                o = (e0 * op0[rows, :] + e1 * op1[rows, :] + e2 * op2[rows, :]) / z
                o_ref[tokens, :] = o
                l_ref[rows, :] = m + jnp.log(z)
                g = g_ref[tokens, :]
                stage[pl.ds(q, 256, stride=4), :] = o * (g * _sigmoid(g))
            mix_ref[pl.ds(pl.multiple_of(i * 1024, 1024), 1024), :] = stage[...].astype(bf16)
            return carry

        lax.fori_loop(0, s // 1024, merge, 0)

    col = lambda base: pl.BlockSpec((s, LANES), lambda h: (0, base + h))
    return pl.pallas_call(
        body, name="attn_fwd", grid=(N_PAIRS,),
        in_specs=[col(0), col(8), col(16), col(24)],
        out_specs=[col(0)] * 6,
        out_shape=[SDS((s, D_ATTN), f32), SDS((s, D_ATTN), f32), SDS((s, D_ATTN), bf16)] + [SDS((s, D_ATTN), f32)] * 3,
        scratch_shapes=[pltpu.VMEM((s, LANES), f32)] * 6 + [pltpu.VMEM((BLK, 2 * LANES), f32)] * 4
        + [pltpu.VMEM((BLK, 2 * LANES), bf16)] * 2 + [pltpu.VMEM((BLK, LANES), f32)] * 4
        + [pltpu.VMEM((2 * BLK, LANES), bf16)] * 4 + [pltpu.VMEM((1024, LANES), f32)],
        compiler_params=pltpu.CompilerParams(dimension_semantics=("parallel",)),
    )(proj, proj, proj, proj)


def _expand_mat():
    colv = _iota((LANES, 2 * D_SSM), 1)
    head = 2 * ((colv % D_SSM) // LANES) + colv // D_SSM
    return (_iota((LANES, 2 * D_SSM), 0) == head).astype(bf16)


def _fold_mat():
    return (_iota((D_SSM, LANES), 0) // HEAD_DIM == _iota((D_SSM, LANES), 1)).astype(bf16)


def _conv(xs_ref, bc_ref, xs_tail, bc_tail, cw_ref, cb_ref, xpad, first):
    keep = jnp.where(first, 0.0, 1.0)
    xpad[0:8, 0:D_SSM] = xs_tail[...] * keep
    xpad[0:8, D_SSM:D_CONV] = bc_tail[...] * keep
    xpad[8:8 + CHUNK, 0:D_SSM] = xs_ref[...]
    xpad[8:8 + CHUNK, D_SSM:D_CONV] = bc_ref[...]
    xp = xpad[...]
    cv = cb_ref[...] + cw_ref[3:4, :] * xp[8:8 + CHUNK]
    for j in range(3):
        cv = cv + cw_ref[j:j + 1, :] * pltpu.roll(xp, 3 - j, 0)[8:8 + CHUNK]
    return cv


def _decay_terms(dt_ref, dtb_ref, alog16_ref, emat_ref):
    pre = dt_ref[...] + dtb_ref[...]
    dt16 = _softplus(pre)
    a16 = -jnp.exp(alog16_ref[...])
    sub, lane = _iota((CHUNK, CHUNK), 0), _iota((CHUNK, CHUNK), 1)
    tri = (sub >= lane).astype(f32)
    al16 = _nn_hi(tri, dt16 * a16)
    al_t = al16.T
    emat = emat_ref[...]
    dt_x = _dot_01(dt16, emat, 3)
    al_x = _dot_01(al16, emat, 3)
    lane_w = _iota((CHUNK, D_SSM), 1)
    even = (lane_w % LANES) < HEAD_DIM
    dt_f = jnp.where(even, dt_x[:, :D_SSM], dt_x[:, D_SSM:])
    al_f = jnp.where(even, al_x[:, :D_SSM], al_x[:, D_SSM:])
    return pre, dt_f, al_f, al_x, al_t


def _decay_mat(al_x, al_t, pair, h):
    sub, lane = _iota((CHUNK, CHUNK), 0), _iota((CHUNK, CHUNK), 1)
    col = al_x[:, h * D_SSM + pair * LANES: h * D_SSM + (pair + 1) * LANES]
    row = al_t[2 * pair + h: 2 * pair + h + 1, :]
    return jnp.exp(jnp.where(sub >= lane, col - row, -jnp.inf))


def _ssd_in_specs(order):
    blk = lambda w, cb: pl.BlockSpec((CHUNK, w), lambda i: (order(i), cb))
    tail = lambda w, cb: pl.BlockSpec((8, w), lambda i: (jnp.maximum(16 * order(i) - 1, 0), cb))
    return [blk(D_SSM, COL_XS // D_SSM), blk(512, COL_BC // 512), tail(D_SSM, COL_XS // D_SSM),
            tail(512, COL_BC // 512), blk(LANES, COL_DT // LANES), blk(D_SSM, COL_Z // D_SSM)]


def _full(shape):
    return pl.BlockSpec(shape, lambda i: (0,) * len(shape))


def _ssd_fwd(proj, conv_w, conv_b, dtb16, alog16, alog_f, d_f, nw):
    s = proj.shape[0]
    nc = s // CHUNK

    def body(xs_ref, bc_ref, xs_tail, bc_tail, dt_ref, z_ref, cw_ref, cb_ref, dtb_ref, alog16_ref, alogf_ref,
             df_ref, nw_ref, mix_ref, y_ref, st_ref, cv_ref, h_scr, xpad, y_scr, emat_ref):
        c = pl.program_id(0)

        @pl.when(c == 0)
        def _():
            h_scr[...] = jnp.zeros_like(h_scr)
            emat_ref[...] = _expand_mat()

        cv = _conv(xs_ref, bc_ref, xs_tail, bc_tail, cw_ref, cb_ref, xpad, c == 0)
        cv_ref[...] = cv
        xbc = cv * _sigmoid(cv)
        _, dt_f, al_f, al_x, al_t = _decay_terms(dt_ref, dtb_ref, alog16_ref, emat_ref)
        head0 = _iota((CHUNK, LANES), 1) < HEAD_DIM
        st_ref[...] = h_scr[...]
        for g in range(N_GROUPS):
            bm = xbc[:, D_SSM + g * D_STATE: D_SSM + (g + 1) * D_STATE].astype(bf16)
            cm = xbc[:, D_SSM + (N_GROUPS + g) * D_STATE: D_SSM + (N_GROUPS + g + 1) * D_STATE].astype(bf16)
            gmat = _nt(cm, bm)
            for pair in range(4 * g, 4 * g + 4):
                sl = slice(pair * LANES, (pair + 1) * LANES)
                xp, dtp, alp = xbc[:, sl], dt_f[:, sl], al_f[:, sl]
                xdt = xp * dtp
                xdt16 = xdt.astype(bf16)
                al_last = alp[CHUNK - 1:CHUNK, :]
                hp = h_scr[:, sl]
                y_off = jnp.exp(alp) * _nn(cm, hp.astype(bf16))
                yd = [_nn((gmat * _decay_mat(al_x, al_t, pair, h)).astype(bf16), xdt16) for h in range(2)]
                y_scr[:, sl] = jnp.where(head0, yd[0], yd[1]) + y_off + df_ref[:, sl] * xp
                st = _tn(bm, (jnp.exp(al_last - alp) * xdt).astype(bf16))
                h_scr[:, sl] = jnp.exp(al_last) * hp + st
        y = y_scr[...]
        y_ref[...] = y
        z = z_ref[...]
        yz = y * (z * _sigmoid(z))
        gw = D_SSM // N_GROUPS
        for g in range(N_GROUPS):
            part = yz[:, g * gw:(g + 1) * gw]
            r = lax.rsqrt(jnp.mean(part * part, axis=-1, keepdims=True) + EPS)
            mix_ref[:, g * gw:(g + 1) * gw] = (part * r * nw_ref[:, g * gw:(g + 1) * gw]).astype(bf16)

    order = lambda i: i
    row = lambda w: pl.BlockSpec((CHUNK, w), lambda i: (i, 0))
    return pl.pallas_call(
        body, name="ssd_fwd", grid=(nc,),
        in_specs=_ssd_in_specs(order) + [_full((4, D_CONV)), _full((1, D_CONV)), _full((1, LANES)), _full((1, LANES)),
                                         _full((1, D_SSM)), _full((1, D_SSM)), _full((1, D_SSM))],
        out_specs=[row(D_SSM), row(D_SSM), pl.BlockSpec((None, D_STATE, D_SSM), lambda i: (i, 0, 0)), row(D_CONV)],
        out_shape=[SDS((s, D_SSM), bf16), SDS((s, D_SSM), f32), SDS((nc, D_STATE, D_SSM), f32),
                   SDS((s, D_CONV), f32)],
        scratch_shapes=[pltpu.VMEM((D_STATE, D_SSM), f32), pltpu.VMEM((8 + CHUNK, D_CONV), f32),
                        pltpu.VMEM((CHUNK, D_SSM), f32), pltpu.VMEM((LANES, 2 * D_SSM), bf16)],
        compiler_params=pltpu.CompilerParams(dimension_semantics=("arbitrary",)),
    )(proj, proj, proj, proj, proj, proj, conv_w, conv_b, dtb16, alog16, alog_f, d_f, nw)


def _outproj_loss(mix_a, mix_s, wo, x, tgt, npw):
    s, d = x.shape
    tm = 512

    def body(ma_ref, ms_ref, wo_ref, x_ref, t_ref, npw_ref, dmix_ref, dres_ref, acc_ref, dwo_ref):
        @pl.when(pl.program_id(0) == 0)
        def _():
            acc_ref[...] = jnp.zeros_like(acc_ref)
            dwo_ref[...] = jnp.zeros_like(dwo_ref)

        out = _nn(ma_ref[...], wo_ref[0:D_ATTN, :]) + _nn(ms_ref[...], wo_ref[D_ATTN:, :])
        r = lax.rsqrt(jnp.mean(out * out, axis=-1, keepdims=True) + EPS)
        on = out * r
        diff = x_ref[...] + on * npw_ref[...] - t_ref[...]
        dres = diff * (1.0 / d)
        dres_ref[...] = dres
        acc_ref[0:1, :] += jnp.sum(diff * diff, axis=0, keepdims=True)
        acc_ref[1:2, :] += jnp.sum(dres * on, axis=0, keepdims=True)
        dn = dres * npw_ref[...]
        dout = (r * (dn - on * jnp.mean(dn * on, axis=-1, keepdims=True))).astype(bf16)
        dmix_ref[...] = _nt(dout, wo_ref[...])
        dwo_ref[0:D_ATTN, :] += _tn(ma_ref[...], dout)
        dwo_ref[D_ATTN:, :] += _tn(ms_ref[...], dout)

    row = lambda w: pl.BlockSpec((tm, w), lambda i: (i, 0))
    return pl.pallas_call(
        body, name="outproj_loss", grid=(s // tm,),
        in_specs=[row(D_ATTN), row(D_SSM), _full((D_ATTN + D_SSM, d)), row(d), row(d), _full((1, d))],
        out_specs=[row(D_ATTN + D_SSM), row(d), _full((8, d)), _full((D_ATTN + D_SSM, d))],
        out_shape=[SDS((s, D_ATTN + D_SSM), f32), SDS((s, d), f32), SDS((8, d), f32), SDS((D_ATTN + D_SSM, d), f32)],
        compiler_params=pltpu.CompilerParams(dimension_semantics=("arbitrary",)),
    )(mix_a, mix_s, wo, x, tgt, npw)


def _attn_bwd(proj, qkv, o, lb, dmix, swap=None):
    s = proj.shape[0]
    n_it = s // BLK

    nsw = 0 if swap is None else 1

    def body(*refs):
        q_ref, k_ref, v_ref, g_ref, o_ref, l_ref, dm_ref = refs[:7]
        swap_in = refs[7:7 + nsw]
        dq_ref, dk_ref, dv_ref, dg_ref = refs[7 + nsw:11 + nsw]
        swap_out = refs[11 + nsw:11 + 2 * nsw]
        dq_acc, dk_acc, dv_acc, do_scr, dl_scr = refs[11 + 2 * nsw:16 + 2 * nsw]
        bufs = refs[16 + 2 * nsw:44 + 2 * nsw]
        stage_a, stage_b = refs[44 + 2 * nsw:46 + 2 * nsw]
        swap_sems = refs[46 + 2 * nsw:]
        head0, tri2_t, _, _, bones = _attn_consts()
        tri2_q = _quarter_mask()

        if nsw:
            x, y, c = _my_pos()
            swap_copy = pltpu.make_async_remote_copy(
                src_ref=swap_in[0], dst_ref=swap_out[0], send_sem=swap_sems[0], recv_sem=swap_sems[1],
                device_id=(x, y, 1 - c), device_id_type=MESH)

            @pl.when(pl.program_id(0) == 0)
            def _():
                swap_copy.start()

        quarter_rows, load, add = _quarter_rows, _load_runs, _add_runs

        def pro(i, carry):
            for t in range(4):
                rows = pl.ds(pl.multiple_of(i * 1024 + t * 256, 256), 256)
                g = g_ref[rows, :]
                sg = _sigmoid(g)
                dmx = dm_ref[rows, :]
                ov = o_ref[rows, :]
                dg_ref[rows, :] = (dmx * ov * (sg * (1.0 + g * (1.0 - sg)))).astype(bf16)
                do = dmx * (g * sg)
                stage_a[t * 256:(t + 1) * 256, :] = do
                stage_b[t * 256:(t + 1) * 256, :] = _split_dot_sum(do * ov, bones)
            z = jnp.zeros((256, LANES), f32)
            for q in range(4):
                rows = quarter_rows(i, q)
                do_scr[rows, :] = stage_a[pl.ds(q, 256, stride=4), :]
                dl_scr[rows, :] = stage_b[pl.ds(q, 256, stride=4), :]
                dq_acc[rows, :] = z
                dk_acc[rows, :] = z
                dv_acc[rows, :] = z
            return carry

        lax.fori_loop(0, s // 1024, pro, 0)

        def per_head(t):
            return jnp.concatenate([t[:, :LANES], t[:, LANES:]], axis=0)

        def both_heads(t):
            tr = pltpu.roll(t, HEAD_DIM, 1)
            return jnp.concatenate([jnp.where(head0, t, tr), jnp.where(head0, tr, t)], axis=1)

        mm_bufs = ((bufs[0], bufs[1], bufs[2], bufs[3]), (bufs[4], bufs[5], bufs[6], bufs[7]))
        ds_bufs = ((bufs[8], bufs[9], bufs[10], bufs[11]), (bufs[12], bufs[13], bufs[14], bufs[15]))
        op_bufs = ((bufs[16], bufs[17], bufs[18], bufs[19]), (bufs[20], bufs[21], bufs[22], bufs[23]))
        vc_bufs, carry_k, carry_v = (bufs[24], bufs[25]), bufs[26], bufs[27]
        for buf in (op_bufs[0][0], op_bufs[1][0]) + vc_bufs:
            buf[...] = jnp.zeros_like(buf)

        def block_rows(i, d, nb):
            rows, has_prev = _quarter_block(i, d, nb)
            return rows, rows, has_prev

        def unstack(st16):
            return st16[:BLK] + st16[BLK:]

        def products(i, par, d, nb):
            src, scr, has_prev = block_rows(i, d, nb)
            tri2 = tri2_q if d == 1 else tri2_t
            s_buf, dp_buf, sd_buf, dpd_buf = mm_bufs[par]
            kc_buf, kp_buf, q_buf, do_buf = op_bufs[par]
            qs = load(q_ref, src)
            do = load(do_scr, scr)
            qs16, do16 = qs.astype(bf16), do.astype(bf16)
            kst_c = _stack_heads(load(k_ref, src).astype(bf16), head0)
            vst_c = _stack_heads(load(v_ref, src).astype(bf16), head0)
            kst_p, vst_p = op_bufs[1 - par][0][...], vc_bufs[1 - par][...]
            kc_buf[...] = kst_c
            kp_buf[...] = kst_p
            vc_bufs[par][...] = vst_c
            q_buf[...] = qs16
            do_buf[...] = do16
            s_buf[...] = jnp.where(tri2, _nt(qs16, kst_c), _if_prev(has_prev, _nt(qs16, kst_p), -jnp.inf))
            dp_buf[...] = jnp.where(tri2, _nt(do16, vst_c), _if_prev(has_prev, _nt(do16, vst_p), 0.0))
            sd_buf[...] = _nn((qs * unstack(kst_p).astype(f32)).astype(bf16), bones)
            dpd_buf[...] = _if_prev(has_prev, _nn((do * unstack(vst_p).astype(f32)).astype(bf16), bones), 0.0)

        def softmax_grad(i, par, d, nb):
            src, scr, has_prev = block_rows(i, d, nb)
            s_buf, dp_buf, sd_buf, dpd_buf = mm_bufs[par]
            p_buf, ds_buf, pd_buf, dsd_buf = ds_bufs[par]
            lse = load(l_ref, src)
            dl = load(dl_scr, scr)
            pt = jnp.exp(s_buf[...] - both_heads(lse))
            ds_buf[...] = (pt * (dp_buf[...] - both_heads(dl))).astype(bf16)
            p_buf[...] = pt.astype(bf16)
            pd = _if_prev(has_prev, jnp.exp(sd_buf[...] - lse), 0.0)
            pd_buf[...] = pd
            dsd_buf[...] = pd * (dpd_buf[...] - dl)

        def accumulate(i, par, d, nb):
            _, rows, _ = block_rows(i, d, nb)
            _, before, _ = block_rows(max(i - 1, 0), d, nb)
            tri2 = tri2_q if d == 1 else tri2_t
            p_buf, ds_buf, pd_buf, dsd_buf = ds_bufs[par]
            kc_buf, kp_buf, q_buf, do_buf = op_bufs[par]
            pt16, ds16, pd, dsd = p_buf[...], ds_buf[...], pd_buf[...], dsd_buf[...]
            zero = jnp.zeros_like(pt16)
            dsc, dsp = jnp.where(tri2, ds16, zero), jnp.where(tri2, zero, ds16)
            pc, pp = jnp.where(tri2, pt16, zero), jnp.where(tri2, zero, pt16)
            kst_c, kst_p, q16, do16 = kc_buf[...], kp_buf[...], q_buf[...], do_buf[...]
            qst, dost = _stack_heads(q16, head0), _stack_heads(do16, head0)
            add(dq_acc, rows, _nn(dsc, kst_c) + _nn(dsp, kst_p) + dsd * unstack(kst_p).astype(f32))
            dk2 = _tn(jnp.concatenate([per_head(dsc), per_head(dsp)], axis=1), qst)
            dv2 = _tn(jnp.concatenate([per_head(pc), per_head(pp)], axis=1), dost)
            add(dk_acc, before, carry_k[...] + dk2[BLK:] + dsd * q16.astype(f32))
            add(dv_acc, before, carry_v[...] + dv2[BLK:] + pd * do16.astype(f32))
            carry_k[...] = dk2[:BLK]
            carry_v[...] = dv2[:BLK]

        for d in DILATIONS:
            nb = s // (BLK * d)
            carry_k[...] = jnp.zeros_like(carry_k)
            carry_v[...] = jnp.zeros_like(carry_v)
            products(0, 0, d, nb)
            products(1, 1, d, nb)
            softmax_grad(0, 0, d, nb)

            for t in range(2, n_it):
                par = t % 2
                accumulate(t - 2, par, d, nb)
                products(t, par, d, nb)
                softmax_grad(t - 1, 1 - par, d, nb)
            accumulate(n_it - 2, 0, d, nb)
            softmax_grad(n_it - 1, 1, d, nb)
            accumulate(n_it - 1, 1, d, nb)
            _, last, _ = block_rows(n_it - 1, d, nb)
            add(dk_acc, last, carry_k[...])
            add(dv_acc, last, carry_v[...])

        def epi(i, carry):
            rows = pl.ds(pl.multiple_of(i * 1024, 1024), 1024)
            for acc, out, stage, scale in ((dq_acc, dq_ref, stage_a, 0.125), (dk_acc, dk_ref, stage_b, 1.0),
                                           (dv_acc, dv_ref, stage_a, 1.0)):
                for q in range(4):
                    stage[pl.ds(q, 256, stride=4), :] = acc[quarter_rows(i, q), :]
                out[rows, :] = (stage[...] if scale == 1.0 else stage[...] * scale).astype(bf16)
            return carry

        lax.fori_loop(0, s // 1024, epi, 0)

        if nsw:
            @pl.when(pl.program_id(0) == N_PAIRS - 1)
            def _():
                swap_copy.wait_send()
                swap_copy.wait_recv()

    col = lambda base: pl.BlockSpec((s, LANES), lambda h: (0, base + h))
    anyspec = pl.BlockSpec(memory_space=pl.ANY)
    swaps = [] if swap is None else [swap]
    outs = pl.pallas_call(
        body, name="attn_bwd", grid=(N_PAIRS,),
        in_specs=[col(0), col(0), col(0), col(24), col(0), col(0), col(0)] + [anyspec] * nsw,
        out_specs=[col(0)] * 4 + [anyspec] * nsw,
        out_shape=[SDS((s, D_ATTN), bf16)] * 4 + [SDS(a.shape, a.dtype) for a in swaps],
        scratch_shapes=[pltpu.VMEM((s, LANES), f32)] * 5
        + [pltpu.VMEM((BLK, 2 * LANES), f32)] * 2 + [pltpu.VMEM((BLK, LANES), f32)] * 2
        + [pltpu.VMEM((BLK, 2 * LANES), f32)] * 2 + [pltpu.VMEM((BLK, LANES), f32)] * 2
        + [pltpu.VMEM((BLK, 2 * LANES), bf16)] * 2 + [pltpu.VMEM((BLK, LANES), f32)] * 2
        + [pltpu.VMEM((BLK, 2 * LANES), bf16)] * 2 + [pltpu.VMEM((BLK, LANES), f32)] * 2
        + [pltpu.VMEM((2 * BLK, LANES), bf16)] * 2 + [pltpu.VMEM((BLK, LANES), bf16)] * 2
        + [pltpu.VMEM((2 * BLK, LANES), bf16)] * 2 + [pltpu.VMEM((BLK, LANES), bf16)] * 2
        + [pltpu.VMEM((2 * BLK, LANES), bf16)] * 2 + [pltpu.VMEM((BLK, LANES), f32)] * 2
        + [pltpu.VMEM((1024, LANES), f32)] * 2
        + [pltpu.SemaphoreType.DMA(())] * (2 * nsw),
        compiler_params=pltpu.CompilerParams(dimension_semantics=("arbitrary",)),
    )(*qkv, proj, o, lb, dmix, *swaps)
    return outs


def _ssd_bwd(proj, y, states, cv, dmix, conv_w, conv_b, dtb16, alog16, alog_f, d_f, nw, chip_sums=()):
    s = proj.shape[0]
    nc = s // CHUNK
    gw = D_SSM // N_GROUPS
    nx = len(chip_sums)

    def body(*refs):
        (xs_ref, bc_ref, _, _, dt_ref, z_ref, y_ref, st_ref, dm_ref, cw_ref, cb_ref, dtb_ref,
         alog16_ref, alogf_ref, df_ref, nw_ref, cv_ref) = refs[:17]
        cs_in = refs[17:17 + nx]
        out_ref, gconv_ref, gvec_ref, gdt_ref = refs[17 + nx:21 + nx]
        cs_out = refs[21 + nx:21 + 2 * nx]
        (dh_scr, head_scr, dcpad, da_scr, dxdt_scr, dbc_scr, emat_ref, fold_ref) = refs[21 + 2 * nx:29 + 2 * nx]
        cs_sems = refs[29 + 2 * nx:]
        i = pl.program_id(0)
        c = nc - 1 - i

        if nx:
            @pl.when(i == 0)
            def _():
                mine, sends, _ = _chip_exchange_copies(cs_in, cs_out, *cs_sems)
                for cp in mine + sends:
                    cp.start()

            @pl.when(i == nc - 1)
            def _():
                mine, sends, recvs = _chip_exchange_copies(cs_in, cs_out, *cs_sems)
                for cp in recvs:
                    cp.wait_recv()
                for cp in sends:
                    cp.wait_send()
                for cp in mine:
                    cp.wait()

        @pl.when(i == 0)
        def _():
            emat_ref[...] = _expand_mat()
            fold_ref[...] = _fold_mat()
            dh_scr[...] = jnp.zeros_like(dh_scr)
            head_scr[...] = jnp.zeros_like(head_scr)
            gconv_ref[...] = jnp.zeros_like(gconv_ref)
            gvec_ref[...] = jnp.zeros_like(gvec_ref)
            gdt_ref[...] = jnp.zeros_like(gdt_ref)

        cv = cv_ref[...]
        sig = _sigmoid(cv)
        xbc = cv * sig
        pre, dt_f, al_f, al_x, al_t = _decay_terms(dt_ref, dtb_ref, alog16_ref, emat_ref)
        head0 = _iota((CHUNK, LANES), 1) < HEAD_DIM
        sub = _iota((CHUNK, LANES), 0)
        last_row = sub == CHUNK - 1

        yv, z, dmx = y_ref[...], z_ref[...], dm_ref[...]
        sz = _sigmoid(z)
        silu = z * sz
        yz = yv * silu
        dyz_parts = []
        for g in range(N_GROUPS):
            gs = slice(g * gw, (g + 1) * gw)
            part = yz[:, gs]
            r = lax.rsqrt(jnp.mean(part * part, axis=-1, keepdims=True) + EPS)
            nh = part * r
            gvec_ref[0:1, gs] += jnp.sum(dmx[:, gs] * nh, axis=0, keepdims=True)
            dn = dmx[:, gs] * nw_ref[:, gs]
            dyz_parts.append(r * (dn - nh * jnp.mean(dn * nh, axis=-1, keepdims=True)))
        dyz = jnp.concatenate(dyz_parts, axis=1)
        dy = dyz * silu
        out_ref[:, 0:D_SSM] = (dyz * yv * (sz * (1.0 + z * (1.0 - sz)))).astype(bf16)

        x_all = xbc[:, 0:D_SSM]
        gvec_ref[2:3, :] += jnp.sum(dy * x_all, axis=0, keepdims=True)

        for g in range(N_GROUPS):
            bm = xbc[:, D_SSM + g * D_STATE: D_SSM + (g + 1) * D_STATE].astype(bf16)
            cm = xbc[:, D_SSM + (N_GROUPS + g) * D_STATE: D_SSM + (N_GROUPS + g + 1) * D_STATE].astype(bf16)
            gmat = _nt(cm, bm)
            dgm = jnp.zeros((CHUNK, CHUNK), f32)
            db = jnp.zeros((CHUNK, D_STATE), f32)
            dc = jnp.zeros((CHUNK, D_STATE), f32)
            for pair in range(4 * g, 4 * g + 4):
                sl = slice(pair * LANES, (pair + 1) * LANES)
                xp, dtp, alp, dyp = x_all[:, sl], dt_f[:, sl], al_f[:, sl], dy[:, sl]
                xdt = xp * dtp
                xdt16 = xdt.astype(bf16)
                al_last = alp[CHUNK - 1:CHUNK, :]
                e_l = jnp.exp(alp)
                wf = jnp.exp(al_last - alp)
                e_last = jnp.exp(al_last)
                hp = st_ref[:, sl]
                hp16 = hp.astype(bf16)
                dhn = dh_scr[:, sl]
                dhn16 = dhn.astype(bf16)
                y_off = e_l * _nn(cm, hp16)
                dch16 = (dyp * e_l).astype(bf16)
                dc = dc + _nt(dch16, hp16)
                dh_out = _tn(cm, dch16)
                dal = dyp * y_off
                xw16 = (wf * xdt).astype(bf16)
                db = db + _nt(xw16, dhn16)
                dxw = _nn(bm, dhn16)
                dxdt = dxw * wf
                dwf = dxw * xdt * wf
                dal = dal - dwf
                dal_last = jnp.sum(dwf, axis=0, keepdims=True) + jnp.sum(dhn * hp, axis=0, keepdims=True) * e_last
                dh_scr[:, sl] = e_last * dhn + dh_out
                for h in range(2):
                    mh = head0 if h == 0 else jnp.logical_not(head0)
                    dyh16 = jnp.where(mh, dyp, 0.0).astype(bf16)
                    lmat = _decay_mat(al_x, al_t, pair, h)
                    mm = gmat * lmat
                    dmm = _nt(dyh16, xdt16)
                    dxdt = dxdt + _tn(mm.astype(bf16), dyh16)
                    n16 = (dmm * mm).astype(bf16)
                    jh = jnp.where(mh, 1.0 / HEAD_DIM, 0.0).astype(bf16)
                    dal = dal + _nn(n16, jh) - _tn(n16, jh)
                    dgm = dgm + dmm * lmat
                da_scr[:, sl] = dal + jnp.where(last_row, dal_last, 0.0)
                dxdt_scr[:, sl] = dxdt
            dgm16 = dgm.astype(bf16)
            dbc_scr[:, g * D_STATE:(g + 1) * D_STATE] = db + _tn(dgm16, cm)
            dbc_scr[:, (N_GROUPS + g) * D_STATE:(N_GROUPS + g + 1) * D_STATE] = dc + _nn(dgm16, bm)

        sub_c, lane_c = _iota((CHUNK, CHUNK), 0), _iota((CHUNK, CHUNK), 1)
        tri_t = (lane_c >= sub_c).astype(bf16)
        dadt = _dot_01_left(tri_t, da_scr[...], 2)
        a_f = -jnp.exp(alogf_ref[...])
        dxdt_all = dxdt_scr[...]
        ddt_f = dxdt_all * x_all + a_f * dadt
        gvec_ref[1:2, :] += jnp.sum(dt_f * dadt, axis=0, keepdims=True) * a_f
        dx = df_ref[...] * dy + dxdt_all * dt_f
        ddt_raw = _dot_01(ddt_f, fold_ref[...], 2) * _sigmoid(pre)
        gdt_ref[0:1, :] += jnp.sum(ddt_raw, axis=0, keepdims=True)
        out_ref[:, D_SSM + D_CONV:D_SSM + D_CONV + LANES] = ddt_raw.astype(bf16)
        out_ref[:, D_SSM + D_CONV + LANES:] = jnp.zeros((CHUNK, 3 * LANES), bf16)

        dsil = sig * (1.0 + cv * (1.0 - sig))
        dcv_x = dx * dsil[:, 0:D_SSM]
        dcv_bc = dbc_scr[...] * dsil[:, D_SSM:]
        dcpad[0:CHUNK, 0:D_SSM] = dcv_x
        dcpad[0:CHUNK, D_SSM:] = dcv_bc
        dcpad[CHUNK:, :] = head_scr[...]
        dcp = dcpad[...]
        dcv = dcp[0:CHUNK]
        gconv_ref[4:5, :] += jnp.sum(dcv, axis=0, keepdims=True)
        x_raw = jnp.concatenate([xs_ref[...], bc_ref[...]], axis=1)
        draw = cw_ref[3:4, :] * dcv
        gconv_ref[3:4, :] += jnp.sum(dcv * x_raw, axis=0, keepdims=True)
        for j in range(3):
            ahead = pltpu.roll(dcp, CHUNK + 8 - (3 - j), 0)[0:CHUNK]
            draw = draw + cw_ref[j:j + 1, :] * ahead
            gconv_ref[j:j + 1, :] += jnp.sum(ahead * x_raw, axis=0, keepdims=True)
        head_scr[...] = dcv[0:8]
        out_ref[:, D_SSM:D_SSM + D_CONV] = draw.astype(bf16)

    order = lambda i: nc - 1 - i
    row = lambda w, cb=0: pl.BlockSpec((CHUNK, w), lambda i: (nc - 1 - i, cb))
    anyspec = pl.BlockSpec(memory_space=pl.ANY)
    outs = pl.pallas_call(
        body, name="ssd_bwd", grid=(nc,),
        in_specs=_ssd_in_specs(order) + [row(D_SSM), pl.BlockSpec((None, D_STATE, D_SSM), lambda i: (nc - 1 - i, 0, 0)),
                                         row(D_SSM, 1), _full((4, D_CONV)), _full((1, D_CONV)), _full((1, LANES)),
                                         _full((1, LANES)), _full((1, D_SSM)), _full((1, D_SSM)), _full((1, D_SSM)),
                                         row(D_CONV)]
        + [anyspec] * nx,
        out_specs=[row(3072), _full((8, D_CONV)), _full((8, D_SSM)), _full((8, LANES))] + [anyspec] * nx,
        out_shape=[SDS((s, 3072), bf16), SDS((8, D_CONV), f32), SDS((8, D_SSM), f32), SDS((8, LANES), f32)]
        + [SDS(a.shape, a.dtype) for a in chip_sums],
        scratch_shapes=[pltpu.VMEM((D_STATE, D_SSM), f32), pltpu.VMEM((8, D_CONV), f32),
                        pltpu.VMEM((8 + CHUNK, D_CONV), f32),
                        pltpu.VMEM((CHUNK, D_SSM), f32), pltpu.VMEM((CHUNK, D_SSM), f32),
                        pltpu.VMEM((CHUNK, 2 * N_GROUPS * D_STATE), f32),
                        pltpu.VMEM((LANES, 2 * D_SSM), bf16), pltpu.VMEM((D_SSM, LANES), bf16)]
        + (_chip_exchange_scratch(nx) if nx else []),
        compiler_params=pltpu.CompilerParams(dimension_semantics=("arbitrary",)),
    )(proj, proj, proj, proj, proj, proj, y, states, dmix, conv_w, conv_b, dtb16, alog16, alog_f, d_f, nw, cv,
      *chip_sums)
    return outs[0], outs[1], outs[2], outs[3], outs[4:]


def _col_blocks(parts, tile):
    counts = [p.shape[1] // tile for p in parts]
    offs = [sum(counts[:t]) for t in range(len(parts))]
    return offs, counts, sum(counts)


def _bcast_copies(src_ref, out_ref, send_sems, recv_sems, local_sem):
    x, y, c = _my_pos()
    me = 4 * x + 2 * y + c
    mine = pltpu.make_async_copy(src_ref, out_ref.at[me], local_sem)
    sends, recvs = [], []
    for k in range(1, N_DEV):
        to, frm = (me + k) % N_DEV, (me + N_DEV - k) % N_DEV
        sems = dict(send_sem=send_sems.at[k - 1], recv_sem=recv_sems.at[k - 1], device_id_type=MESH)
        sends.append(pltpu.make_async_remote_copy(
            src_ref=src_ref, dst_ref=out_ref.at[me], device_id=(to // 4, (to // 2) % 2, to % 2), **sems))
        recvs.append(pltpu.make_async_remote_copy(
            src_ref=src_ref, dst_ref=out_ref.at[frm], device_id=(x, y, c), **sems))
    return mine, sends, recvs


def _bcast_scratch():
    return [pltpu.SemaphoreType.DMA((N_DEV - 1,)), pltpu.SemaphoreType.DMA((N_DEV - 1,)), pltpu.SemaphoreType.DMA(())]


def _inproj_bwd(dparts, wt, x, nw, dres, chip_sums=(), pack=None):
    s, d = x.shape
    tm, tk = 1024, 1024
    offs, counts, nk = _col_blocks(dparts, tk)
    npart, nx = len(dparts), len(chip_sums)
    npk = 0 if pack is None else 1
    ni = s // tm

    def body(*refs):
        dp_refs = refs[:npart]
        w_ref, x_ref, nw_ref, dres_ref = refs[npart:npart + 4]
        pos = npart + 4
        cs_in, pos = refs[pos:pos + nx], pos + nx
        pack_in, pos = refs[pos:pos + npk], pos + npk
        (gx_ref, gnw_ref), pos = refs[pos:pos + 2], pos + 2
        cs_out, pos = refs[pos:pos + nx], pos + nx
        pack_out, pos = refs[pos:pos + 2 * npk], pos + 2 * npk
        acc, pos = refs[pos], pos + 1
        cs_sems, pos = refs[pos:pos + 3 * min(nx, 1)], pos + 3 * min(nx, 1)
        pk_refs = refs[pos:]
        i, k = pl.program_id(0), pl.program_id(1)

        def exchange():
            return _chip_exchange_copies(cs_in, cs_out, *cs_sems)

        def pack_copies():
            return _bcast_copies(pack_in[0], pack_out[0], *pk_refs[1:4])

        def gnw_copies():
            return _bcast_copies(pk_refs[0], pack_out[1], *pk_refs[4:7])

        @pl.when(jnp.logical_and(i == 0, k == 0))
        def _():
            gnw_ref[...] = jnp.zeros_like(gnw_ref)
            if nx:
                mine, sends, _ = exchange()
                for cp in mine + sends:
                    cp.start()
            if npk:
                mine, sends, _ = pack_copies()
                for cp in [mine] + sends:
                    cp.start()

        @pl.when(k == 0)
        def _():
            acc[...] = _nn(dp_refs[0][...], w_ref[...])

        for t in range(npart):
            @pl.when(jnp.logical_and(k >= max(offs[t], 1), k < offs[t] + counts[t]))
            def _(t=t):
                acc[...] += _nn(dp_refs[t][...], w_ref[...])

        @pl.when(k == nk - 1)
        def _():
            xv = x_ref[...]
            r = lax.rsqrt(jnp.mean(xv * xv, axis=-1, keepdims=True) + EPS)
            xn = xv * r
            du = acc[...]
            gnw_ref[0:1, :] += jnp.sum(du * xn, axis=0, keepdims=True)
            dn = du * nw_ref[...]
            gx_ref[...] = dres_ref[...] + r * (dn - xn * jnp.mean(dn * xn, axis=-1, keepdims=True))

        @pl.when(jnp.logical_and(i == ni - 1, k == nk - 1))
        def _():
            if npk:
                pk_refs[0][...] = gnw_ref[...]
                mine, sends, _ = gnw_copies()
                for cp in [mine] + sends:
                    cp.start()
            if nx:
                mine, sends, recvs = exchange()
                for cp in recvs:
                    cp.wait_recv()
                for cp in sends:
                    cp.wait_send()
                for cp in mine:
                    cp.wait()
            if npk:
                for copies in (pack_copies(), gnw_copies()):
                    mine, sends, recvs = copies
                    for cp in recvs:
                        cp.wait_recv()
                    for cp in sends:
                        cp.wait_send()
                    mine.wait()

    def piece(t):
        return pl.BlockSpec((tm, tk), lambda i, k: (i, jnp.clip(k - offs[t], 0, counts[t] - 1)))

    anyspec = pl.BlockSpec(memory_space=pl.ANY)
    packs = [] if pack is None else [pack]
    pack_shapes = [] if pack is None else [SDS((N_DEV,) + pack.shape, f32), SDS((N_DEV, 8, d), f32)]
    scratch = [pltpu.VMEM((tm, d), f32)] + (_chip_exchange_scratch(nx) if nx else [])
    if npk:
        scratch += [pltpu.VMEM((8, d), f32)] + _bcast_scratch() + _bcast_scratch()
    outs = pl.pallas_call(
        body, name="inproj_bwd", grid=(ni, nk),
        in_specs=[piece(t) for t in range(npart)] + [
            pl.BlockSpec((tk, d), lambda i, k: (k, 0)),
            pl.BlockSpec((tm, d), lambda i, k: (i, 0)), pl.BlockSpec((1, d), lambda i, k: (0, 0)),
            pl.BlockSpec((tm, d), lambda i, k: (i, 0))] + [anyspec] * (nx + npk),
        out_specs=[pl.BlockSpec((tm, d), lambda i, k: (i, 0)), pl.BlockSpec((8, d), lambda i, k: (0, 0))]
        + [anyspec] * (nx + 2 * npk),
        out_shape=[SDS((s, d), f32), SDS((8, d), f32)] + [SDS(a.shape, a.dtype) for a in chip_sums] + pack_shapes,
        scratch_shapes=scratch,
        compiler_params=pltpu.CompilerParams(dimension_semantics=("arbitrary", "arbitrary")),
    )(*dparts, wt, x, nw, dres, *chip_sums, *packs)
    return outs[0], outs[1], outs[2:2 + nx], outs[2 + nx:]


def _matmul_tn(a_parts, b_parts, name):
    tile, tk = 1024, 1024
    s = a_parts[0].shape[0]
    nk = s // tk
    na, nb = len(a_parts), len(b_parts)
    offs_a, counts_a, ni = _col_blocks(a_parts, tile)
    offs_b, counts_b, nj = _col_blocks(b_parts, tile)

    def body(*refs):
        a_refs, b_refs, o_ref = refs[:na], refs[na:na + nb], refs[na + nb]
        i, j = pl.program_id(0), pl.program_id(1)

        @pl.when(pl.program_id(2) == 0)
        def _():
            o_ref[...] = jnp.zeros_like(o_ref)

        for ta in range(na):
            for tb in range(nb):
                in_a = jnp.logical_and(i >= offs_a[ta], i < offs_a[ta] + counts_a[ta])
                in_b = jnp.logical_and(j >= offs_b[tb], j < offs_b[tb] + counts_b[tb])

                @pl.when(jnp.logical_and(in_a, in_b))
                def _(ta=ta, tb=tb):
                    o_ref[...] += _tn(a_refs[ta][...], b_refs[tb][...])

    def spec(offs, counts, t, axis):
        def index(i, j, k):
            pos = (i, j)[axis]
            mine = jnp.logical_and(pos >= offs[t], pos < offs[t] + counts[t])
            return jnp.where(mine, k, 0), jnp.clip(pos - offs[t], 0, counts[t] - 1)
        return pl.BlockSpec((tk, tile), index)

    return pl.pallas_call(
        body, name=name, grid=(ni, nj, nk),
        in_specs=[spec(offs_a, counts_a, t, 0) for t in range(na)] + [spec(offs_b, counts_b, t, 1) for t in range(nb)],
        out_specs=pl.BlockSpec((tile, tile), lambda i, j, k: (i, j)),
        out_shape=SDS((ni * tile, nj * tile), f32),
        compiler_params=pltpu.CompilerParams(dimension_semantics=("parallel", "parallel", "arbitrary")),
    )(*a_parts, *b_parts)


def _adamw(w, g, m, v):
    m = ADAM_B1 * m + (1.0 - ADAM_B1) * g
    v = ADAM_B2 * v + (1.0 - ADAM_B2) * (g * g)
    m_hat = m / (1.0 - ADAM_B1 ** ADAM_STEP)
    v_hat = v / (1.0 - ADAM_B2 ** ADAM_STEP)
    delta = -ADAM_LR * (m_hat / (jnp.sqrt(v_hat) + ADAM_EPS) + ADAM_WD * w)
    return delta, m, v


def _sum_adamw(parts, w, m, v, name):
    r, c = w.shape
    tc = 256

    def body(p_ref, w_ref, m_ref, v_ref, g_ref, d_ref, nm_ref, nv_ref):
        g = p_ref[0].astype(f32)
        for q in range(1, 4):
            g = g + p_ref[q].astype(f32)
        g_ref[...] = g
        d_ref[...], nm_ref[...], nv_ref[...] = _adamw(w_ref[...], g, m_ref[...], v_ref[...])

    blk = pl.BlockSpec((r, tc), lambda i: (0, i))
    return pl.pallas_call(
        body, name=name, grid=(c // tc,),
        in_specs=[pl.BlockSpec((4, r, tc), lambda i: (0, 0, i)), blk, blk, blk],
        out_specs=[blk] * 4, out_shape=[SDS((r, c), f32)] * 4,
        compiler_params=pltpu.CompilerParams(dimension_semantics=("parallel",)),
    )(parts, w, m, v)


def _sum_small(parts, pre_blocks):
    def body(p_ref, b_ref, o_ref):
        t = p_ref[0]
        pre = b_ref[0]
        for j in range(1, N_DEV):
            t = t + p_ref[j]
            pre = pre + b_ref[j]
        o_ref[...] = t
        o_ref[5:6, 0:D_MODEL] = pre[0:1, :]
        row_h = _iota((D_SSM, LANES), 0) // HEAD_DIM
        fold = (row_h == _iota((D_SSM, LANES), 1)).astype(f32)
        lower = t[8:16, 0:LANES]
        folded = _nn_hi(t[8:16, 0:D_SSM], fold)
        loss = jnp.sum(t[11:12, 0:D_MODEL], axis=1, keepdims=True) * (0.5 / D_MODEL)
        row = _iota((8, LANES), 0)
        o_ref[8:16, 0:LANES] = jnp.where(row < 2, folded, jnp.where(row == 4, loss, lower))

    return pl.pallas_call(body, name="sum_small", out_shape=SDS((PACK_ROWS, PACK_W), f32),
                          in_specs=[pl.BlockSpec(memory_space=pltpu.VMEM)] * 2,
                          out_specs=pl.BlockSpec(memory_space=pltpu.VMEM))(parts, pre_blocks)


def _adamw_small(w, g, m, v):
    def body(w_ref, g_ref, m_ref, v_ref, d_ref, nm_ref, nv_ref):
        d_ref[...], nm_ref[...], nv_ref[...] = _adamw(w_ref[...], g_ref[...], m_ref[...], v_ref[...])

    vm = pl.BlockSpec(memory_space=pltpu.VMEM)
    return pl.pallas_call(body, name="adamw_small", out_shape=[SDS(w.shape, f32)] * 3,
                          in_specs=[vm] * 4, out_specs=[vm] * 3)(w, g, m, v)


def _pad_lanes(v, width):
    return jnp.pad(v, ((0, 0), (0, width - v.shape[1])))


def _local_step(x, tgt, norm_pre_w, wt, conv_w, conv_b, dt_bias, a_log, d_skip, ssm_norm_w, wo, norm_post_w, sharded):
    dtb16 = _pad_lanes(dt_bias, LANES)
    alog16 = _pad_lanes(a_log, LANES)
    alog_f = jnp.repeat(a_log, HEAD_DIM, axis=1)
    d_f = jnp.repeat(d_skip, HEAD_DIM, axis=1)

    shard_out = wo.shape[0]
    if sharded:
        proj, u, (g_out, g_cw) = _prenorm_inproj(x, norm_pre_w, wt, gather=(wo, conv_w))
        wo = g_out.reshape(N_DEV * shard_out, D_MODEL)
        conv_w = g_cw.transpose(1, 0, 2).reshape(4, D_CONV)
    else:
        proj, u, _ = _prenorm_inproj(x, norm_pre_w, wt)
    o, lb, mix_a, *qkv = _attn_fwd(proj)
    mix_s, y, states, cv = _ssd_fwd(proj, conv_w, conv_b, dtb16, alog16, alog_f, d_f, ssm_norm_w)
    dmix, dres, acc_post, dw_out = _outproj_loss(mix_a, mix_s, wo, x, tgt, norm_post_w)
    ssd_args = (proj, y, states, cv, dmix, conv_w, conv_b, dtb16, alog16, alog_f, d_f, ssm_norm_w)
    if sharded:
        dq, dk, dv, dg, got_out = _attn_bwd(proj, qkv, o, lb, dmix, swap=dw_out)
        chip_out = _chip_sum(dw_out, got_out, shard_out, "chip_sum_w_out")
        dzxd, g_conv, g_vec, g_dt, (parts_out,) = _ssd_bwd(*ssd_args, chip_sums=[chip_out])
    else:
        dq, dk, dv, dg = _attn_bwd(proj, qkv, o, lb, dmix)
        dzxd, g_conv, g_vec, g_dt, _ = _ssd_bwd(*ssd_args)
    dparts = [dq, dk, dv, dg, dzxd]

    def pack(g_pre_row):
        return jnp.concatenate(
            [g_conv[0:5], g_pre_row, _pad_lanes(g_vec[0:1], PACK_W), _pad_lanes(acc_post[1:2], PACK_W),
             _pad_lanes(g_vec[1:3], PACK_W), _pad_lanes(g_dt[0:1], PACK_W), _pad_lanes(acc_post[0:1], PACK_W),
             jnp.zeros((4, PACK_W), f32)], axis=0)

    if sharded:
        dw_in, got_in = _dw_in_swap(dparts, u)
        chip_in = _chip_sum(dw_in, got_in, D_IN_PROJ // N_DEV, "chip_sum_w_in")
        grad_x, _, (parts_in,), small = _inproj_bwd(dparts, wt, x, norm_pre_w, dres, [chip_in],
                                                    pack(jnp.zeros((1, PACK_W), f32)))
        return grad_x, (parts_in, parts_out), small
    dw_in = _matmul_tn(dparts, [u], "dw_in")
    grad_x, g_pre, _, _ = _inproj_bwd(dparts, wt, x, norm_pre_w, dres)
    return grad_x, (dw_in, dw_out), pack(_pad_lanes(g_pre[0:1], PACK_W))


def kernel(x, norm_pre_w, w_in, conv_w, conv_b, dt_bias, a_log, d_skip, ssm_norm_w, w_out, norm_post_w, loss_target, m_norm_pre_w, m_w_in, m_conv_w, m_conv_b, m_dt_bias, m_a_log, m_d_skip, m_ssm_norm_w, m_w_out, m_norm_post_w, v_norm_pre_w, v_w_in, v_conv_w, v_conv_b, v_dt_bias, v_a_log, v_d_skip, v_ssm_norm_w, v_w_out, v_norm_post_w):
    shard_cv = conv_w.shape[2]
    me = 4 * lax.axis_index("x") + 2 * lax.axis_index("y") + lax.axis_index("c")

    g_in, = _all_gather([w_in[0].T.astype(bf16)])
    wt = _assemble_wt(g_in)

    grad_x, (parts_in, parts_out), (parts_small, pre_blocks) = _local_step(
        x[0], loss_target[0], norm_pre_w, wt, conv_w[0], conv_b, dt_bias, a_log, d_skip, ssm_norm_w,
        w_out[0].astype(bf16), norm_post_w, sharded=True)

    g_w_in, d_w_in, nm_w_in, nv_w_in = (a.T for a in _sum_adamw(
        parts_in, w_in[0].T, m_w_in[0].T, v_w_in[0].T, "sum_adamw_w_in"))
    g_w_out, d_w_out, nm_w_out, nv_w_out = _sum_adamw(parts_out, w_out[0], m_w_out[0], v_w_out[0], "sum_adamw_w_out")
    tot = _sum_small(parts_small, pre_blocks)

    g_cw_all = tot[0:4]
    small_g = {
        "conv_w": lax.dynamic_slice(g_cw_all, (0, me * shard_cv), (4, shard_cv)),
        "conv_b": tot[4:5], "norm_pre_w": tot[5:6, :D_MODEL], "ssm_norm_w": tot[6:7, :D_SSM],
        "norm_post_w": tot[7:8, :D_MODEL], "a_log": tot[8:9, :16], "d_skip": tot[9:10, :16], "dt_bias": tot[10:11, :16],
    }
    loss = tot[12, 0]
    small_w = {"conv_w": (conv_w[0], m_conv_w[0], v_conv_w[0]), "conv_b": (conv_b, m_conv_b, v_conv_b),
               "norm_pre_w": (norm_pre_w, m_norm_pre_w, v_norm_pre_w), "ssm_norm_w": (ssm_norm_w, m_ssm_norm_w, v_ssm_norm_w),
               "norm_post_w": (norm_post_w, m_norm_post_w, v_norm_post_w), "a_log": (a_log, m_a_log, v_a_log),
               "d_skip": (d_skip, m_d_skip, v_d_skip), "dt_bias": (dt_bias, m_dt_bias, v_dt_bias)}
    names = list(small_w)
    sizes = [small_g[k].size for k in names]
    tot_size = sum(sizes)
    pad_to = -(-tot_size // 1024) * 1024

    def flat(arrs):
        v = jnp.concatenate([a.reshape(-1) for a in arrs])
        return jnp.pad(v, (0, pad_to - tot_size)).reshape(pad_to // LANES, LANES)

    fw = flat([small_w[k][0] for k in names])
    fg = flat([small_g[k] for k in names])
    fm = flat([small_w[k][1] for k in names])
    fv = jnp.pad(jnp.concatenate([small_w[k][2].reshape(-1) for k in names]), (0, pad_to - tot_size),
                 constant_values=1.0).reshape(pad_to // LANES, LANES)
    fd, fnm, fnv = _adamw_small(fw, fg, fm, fv)

    def unflat(f):
        out, off = {}, 0
        v = f.reshape(-1)
        for k, n in zip(names, sizes):
            out[k] = v[off:off + n].reshape(small_g[k].shape)
            off += n
        return out

    sd, snm, snv = unflat(fd), unflat(fnm), unflat(fnv)
    lead = lambda a: a[None]
    order = ["norm_pre_w", "w_in", "conv_w", "conv_b", "dt_bias", "a_log", "d_skip", "ssm_norm_w", "w_out", "norm_post_w"]
    grads = dict(small_g, w_in=g_w_in, w_out=g_w_out)
    deltas = dict(sd, w_in=d_w_in, w_out=d_w_out)
    new_m = dict(snm, w_in=nm_w_in, w_out=nm_w_out)
    new_v = dict(snv, w_in=nv_w_in, w_out=nv_w_out)

    def shaped(dct, k):
        a = dct[k]
        return lead(a) if k in ("w_in", "w_out", "conv_w") else a

    return (loss, grad_x[None], *[shaped(grads, k) for k in order], *[shaped(deltas, k) for k in order],
            *[shaped(new_m, k) for k in order], *[shaped(new_v, k) for k in order])
```

```python
import jax
import jax.numpy as jnp
from jax import lax
from jax.experimental import pallas as pl
from jax.experimental.pallas import tpu as pltpu

f32, bf16 = jnp.float32, jnp.bfloat16
SDS = jax.ShapeDtypeStruct
HIGHEST = lax.Precision.HIGHEST
MESH = pl.DeviceIdType.MESH

N_DEV = 8
D_MODEL = 1024
D_ATTN = 1024
D_SSM = 1024
HEAD_DIM = 64
N_PAIRS = 8
D_STATE = 128
N_GROUPS = 2
D_CONV = D_SSM + 2 * N_GROUPS * D_STATE
D_IN_PROJ = 4 * D_ATTN + D_SSM + D_CONV + 16
NP = 7168
CHUNK = 128
BLK = 128
DILATIONS = (1, 4, 16)
EPS = 1e-6
LANES = 128
COL_Z, COL_XS, COL_BC, COL_DT = 4096, 5120, 6144, 6656

ADAM_LR, ADAM_B1, ADAM_B2, ADAM_EPS, ADAM_WD, ADAM_STEP = 0.001, 0.9, 0.999, 1e-08, 0.01, 10

PACK_ROWS, PACK_W = 16, 1536


def _nt(a, b):
    return lax.dot_general(a, b, (((1,), (1,)), ((), ())), preferred_element_type=f32)


def _tn(a, b):
    return lax.dot_general(a, b, (((0,), (0,)), ((), ())), preferred_element_type=f32)


def _nn(a, b):
    return jnp.dot(a, b, preferred_element_type=f32)


def _nn_hi(a, b):
    return jnp.dot(a, b, precision=HIGHEST, preferred_element_type=f32)


def _sigmoid(x):
    return 1.0 / (1.0 + jnp.exp(-x))


def _softplus(x):
    return jnp.maximum(x, 0.0) + jnp.log1p(jnp.exp(-jnp.abs(x)))


def _iota(shape, dim):
    return lax.broadcasted_iota(jnp.int32, shape, dim)


def _my_pos():
    return lax.axis_index("x"), lax.axis_index("y"), lax.axis_index("c")


GATHER_SEMS = 9


def _gather_phases(ins, outs, send_sems, recv_sems, local_sems):
    n, ns = len(ins), GATHER_SEMS
    x, y, c = _my_pos()
    me, sibling = (x, y, c), (x, y, 1 - c)
    xn, yn, diag = (1 - x, y), (x, 1 - y), (1 - x, 1 - y)

    def slot(a, px, py, pc):
        return outs[a].at[4 * px + 2 * py + pc]

    def part(a, ref, h):
        width = ins[a].shape[-1]
        if width % (2 * LANES):
            return ref if h == 1 else None
        return ref.at[:, pl.ds(h * (width // 2), width // 2)]

    def copy(a, k, block, to, src=None, h=None):
        src_ref = slot(a, *block) if src is None else src
        dst_ref = slot(a, *block)
        if h is not None:
            src_ref, dst_ref = part(a, src_ref, h), part(a, dst_ref, h)
            if src_ref is None:
                return None
        return pltpu.make_async_remote_copy(
            src_ref=src_ref, dst_ref=dst_ref, send_sem=send_sems.at[ns * a + k], recv_sem=recv_sems.at[ns * a + k],
            device_id=to, device_id_type=MESH)

    def mine():
        return [pltpu.make_async_copy(ins[a], slot(a, *me), local_sems.at[a]) for a in range(n)]

    def own_sends(a):
        return [copy(a, 0, me, sibling, src=ins[a]), copy(a, 1, me, (*xn, c), src=ins[a]),
                copy(a, 2, me, (*yn, c), src=ins[a])]

    def neighbour_relays(a):
        return [copy(a, 4, (*xn, c), sibling), copy(a, 7, (*xn, c), (*yn, c), h=1),
                copy(a, 5, (*yn, c), sibling), copy(a, 8, (*yn, c), (*xn, c), h=0)]

    def diagonal_halves(a):
        return [copy(a, k, (*diag, c), me, h=h) for k, h in ((8, 0), (7, 1))]

    def start_all(cps):
        for cp in cps:
            if cp is not None:
                cp.start()

    def phase0():
        start_all(mine())
        for a in range(n):
            start_all(own_sends(a))

    def phase1():
        for a in range(n):
            copy(a, 1, (*xn, c), me).wait_recv()
            copy(a, 2, (*yn, c), me).wait_recv()
            start_all(neighbour_relays(a))

    def phase2():
        for a in range(n):
            for cp in diagonal_halves(a):
                if cp is not None:
                    cp.wait_recv()
            copy(a, 6, (*diag, c), sibling).start()

    def finish():
        for a in range(n):
            copy(a, 0, sibling, me).wait_recv()
            for j, chip in enumerate((xn, yn, diag)):
                copy(a, 4 + j, (*chip, 1 - c), me).wait_recv()
        for a in range(n):
            for cp in own_sends(a) + neighbour_relays(a) + [copy(a, 6, (*diag, c), sibling)]:
                if cp is not None:
                    cp.wait_send()
        for cp in mine():
            cp.wait()

    return phase0, phase1, phase2, finish


def _gather_scratch(n):
    return [pltpu.SemaphoreType.DMA((GATHER_SEMS * n,)), pltpu.SemaphoreType.DMA((GATHER_SEMS * n,)),
            pltpu.SemaphoreType.DMA((n,))]


def _all_gather(arrs):
    n = len(arrs)

    def body(*refs):
        for phase in _gather_phases(refs[:n], refs[n:2 * n], *refs[2 * n:]):
            phase()

    anyspec = pl.BlockSpec(memory_space=pl.ANY)
    return pl.pallas_call(
        body, name="weights_all_gather",
        out_shape=[SDS((N_DEV,) + a.shape, a.dtype) for a in arrs],
        in_specs=[anyspec] * n, out_specs=[anyspec] * n, scratch_shapes=_gather_scratch(n),
    )(*arrs)


def _dw_in_swap(a_parts, u):
    tile, tk = 1024, 1024
    s = u.shape[0]
    nk = s // tk
    na = len(a_parts)
    offs, counts, ni = _col_blocks(a_parts, tile)

    def body(*refs):
        a_refs, u_ref = refs[:na], refs[na]
        dw_ref, got_ref = refs[na + 1:na + 3]
        acc, stage, local_sems, send_sems, recv_sem = refs[na + 3:]
        i, k = pl.program_id(0), pl.program_id(1)
        x, y, c = _my_pos()
        par = i % 2

        def tile_copies(t, p):
            rows = pl.ds(pl.multiple_of(t * tile, tile), tile)
            loc = pltpu.make_async_copy(stage.at[p], dw_ref.at[rows], local_sems.at[p])
            rem = pltpu.make_async_remote_copy(
                src_ref=stage.at[p], dst_ref=got_ref.at[rows], send_sem=send_sems.at[p], recv_sem=recv_sem,
                device_id=(x, y, 1 - c), device_id_type=MESH)
            return loc, rem

        @pl.when(k == 0)
        def _():
            acc[...] = jnp.zeros((tile, tile), f32)

        for t in range(na):
            @pl.when(jnp.logical_and(i >= offs[t], i < offs[t] + counts[t]))
            def _(t=t):
                acc[...] += _tn(a_refs[t][...], u_ref[pl.ds(pl.multiple_of(k * tk, tk), tk), :])

        @pl.when(k == nk - 1)
        def _():
            @pl.when(i >= 2)
            def _():
                loc, rem = tile_copies(i - 2, par)
                loc.wait()
                rem.wait_send()
            stage[par] = acc[...]
            loc, rem = tile_copies(i, par)
            loc.start()
            rem.start()

        @pl.when(jnp.logical_and(i == ni - 1, k == nk - 1))
        def _():
            for t in (ni - 2, ni - 1):
                loc, rem = tile_copies(t, t % 2)
                loc.wait()
                rem.wait_send()
            pltpu.make_async_remote_copy(src_ref=dw_ref, dst_ref=got_ref, send_sem=send_sems.at[0], recv_sem=recv_sem,
                                         device_id=(x, y, c), device_id_type=MESH).wait_recv()

    def a_spec(t):
        def index(i, k):
            mine = jnp.logical_and(i >= offs[t], i < offs[t] + counts[t])
            return jnp.where(mine, k, 0), jnp.clip(i - offs[t], 0, counts[t] - 1)
        return pl.BlockSpec((tk, tile), index)

    anyspec = pl.BlockSpec(memory_space=pl.ANY)
    return pl.pallas_call(
        body, name="dw_in_swap", grid=(ni, nk),
        in_specs=[a_spec(t) for t in range(na)] + [pl.BlockSpec((s, tile), lambda i, k: (0, 0))],
        out_specs=[anyspec] * 2,
        out_shape=[SDS((ni * tile, tile), f32), SDS((ni * tile, tile), f32)],
        scratch_shapes=[pltpu.VMEM((tile, tile), f32), pltpu.VMEM((2, tile, tile), f32), pltpu.SemaphoreType.DMA((2,)),
                        pltpu.SemaphoreType.DMA((2,)), pltpu.SemaphoreType.DMA(())],
        compiler_params=pltpu.CompilerParams(dimension_semantics=("arbitrary", "arbitrary")),
    )(*a_parts, u)


def _chip_sum(mine, got, rows, name):
    r, cdim = mine.shape
    tc = LANES

    def body(m_ref, g_ref, s16_ref):
        c = lax.axis_index("c")
        for q in range(4):
            blk = pl.ds(rows * (2 * q + c), rows)
            s16_ref[q] = (m_ref[blk, :] + g_ref[blk, :]).astype(bf16)

    col = pl.BlockSpec((r, tc), lambda i: (0, i))
    return pl.pallas_call(
        body, name=name, grid=(cdim // tc,), in_specs=[col, col],
        out_specs=pl.BlockSpec((4, rows, tc), lambda i: (0, 0, i)), out_shape=SDS((4, rows, cdim), bf16),
        compiler_params=pltpu.CompilerParams(dimension_semantics=("parallel",)),
    )(mine, got)


def _assemble_wt(shards):
    nd, rows, cdim = shards.shape
    tc = 256

    def body(g_ref, o_ref):
        for j in range(nd):
            o_ref[pl.ds(rows * j, rows), :] = g_ref[j]
        o_ref[pl.ds(nd * rows, NP - nd * rows), :] = jnp.zeros((NP - nd * rows, tc), shards.dtype)

    return pl.pallas_call(
        body, name="assemble_w_in", grid=(cdim // tc,),
        in_specs=[pl.BlockSpec((nd, rows, tc), lambda i: (0, 0, i))],
        out_specs=pl.BlockSpec((NP, tc), lambda i: (0, i)), out_shape=SDS((NP, cdim), shards.dtype),
        compiler_params=pltpu.CompilerParams(dimension_semantics=("parallel",)),
    )(shards)


def _chip_exchange_copies(ins, outs, send_sems, recv_sems, local_sems):
    nb = len(ins)
    x, y, c = _my_pos()
    my_q = 2 * x + y
    mine = [pltpu.make_async_copy(ins[a].at[my_q], outs[a].at[my_q], local_sems.at[a]) for a in range(nb)]
    sends, recvs = [], []
    for k in range(1, 4):
        to, frm = (my_q + k) % 4, (my_q + 4 - k) % 4
        for a in range(nb):
            sems = dict(send_sem=send_sems.at[3 * a + k - 1], recv_sem=recv_sems.at[3 * a + k - 1], device_id_type=MESH)
            sends.append(pltpu.make_async_remote_copy(
                src_ref=ins[a].at[to], dst_ref=outs[a].at[my_q], device_id=(to // 2, to % 2, c), **sems))
            recvs.append(pltpu.make_async_remote_copy(
                src_ref=ins[a].at[frm], dst_ref=outs[a].at[frm], device_id=(x, y, c), **sems))
    return mine, sends, recvs


def _chip_exchange_scratch(nb):
    return [pltpu.SemaphoreType.DMA((3 * nb,)), pltpu.SemaphoreType.DMA((3 * nb,)), pltpu.SemaphoreType.DMA((nb,))]


def _prenorm_inproj(x, nw, wt, gather=()):
    s, d = x.shape
    npad = wt.shape[0]
    tm, tn = 1024, 1024
    ng = len(gather)
    ni, nj = s // tm, npad // tn

    def body(x_ref, nw_ref, w_ref, *refs):
        g_in, (proj_ref, u_ref), g_out, sems = refs[:ng], refs[ng:ng + 2], refs[ng + 2:2 * ng + 2], refs[2 * ng + 2:]
        i, j = pl.program_id(0), pl.program_id(1)
        if ng:
            phases = _gather_phases(g_in, g_out, *sems)
            for step, phase in enumerate(phases[:3]):
                @pl.when(jnp.logical_and(i == step, j == 0))
                def _(phase=phase):
                    phase()

        @pl.when(j == 0)
        def _():
            xv = x_ref[...]
            r = lax.rsqrt(jnp.mean(xv * xv, axis=-1, keepdims=True) + EPS)
            u_ref[...] = (xv * r * nw_ref[...]).astype(bf16)
        proj_ref[...] = _nt(u_ref[...], w_ref[pl.ds(pl.multiple_of(j * tn, tn), tn), :])

        if ng:
            @pl.when(jnp.logical_and(i == ni - 1, j == nj - 1))
            def _():
                phases[3]()

    anyspec = pl.BlockSpec(memory_space=pl.ANY)
    outs = pl.pallas_call(
        body, name="prenorm_inproj", grid=(ni, nj),
        in_specs=[pl.BlockSpec((tm, d), lambda i, j: (i, 0)), pl.BlockSpec((1, d), lambda i, j: (0, 0)),
                  pl.BlockSpec((npad, d), lambda i, j: (0, 0))] + [anyspec] * ng,
        out_specs=[pl.BlockSpec((tm, tn), lambda i, j: (i, j)), pl.BlockSpec((tm, d), lambda i, j: (i, 0))]
        + [anyspec] * ng,
        out_shape=[SDS((s, npad), f32), SDS((s, d), bf16)] + [SDS((N_DEV,) + a.shape, a.dtype) for a in gather],
        scratch_shapes=_gather_scratch(ng) if ng else [],
        compiler_params=pltpu.CompilerParams(dimension_semantics=("arbitrary", "arbitrary")),
    )(x, nw, wt, *gather)
    return outs[0], outs[1], outs[2:]


def _attn_consts():
    head0 = _iota((BLK, LANES), 1) < HEAD_DIM
    tri2 = (_iota((BLK, 2 * LANES), 1) % LANES) <= _iota((BLK, 2 * LANES), 0)
    ones2 = ((_iota((LANES, 2 * LANES), 0) < HEAD_DIM) == (_iota((LANES, 2 * LANES), 1) < LANES)).astype(bf16)
    rmat = ((_iota((2 * LANES, LANES), 0) < LANES) == (_iota((2 * LANES, LANES), 1) < HEAD_DIM)).astype(bf16)
    bones = ((_iota((LANES, LANES), 0) < HEAD_DIM) == (_iota((LANES, LANES), 1) < HEAD_DIM)).astype(bf16)
    return head0, tri2, ones2, rmat, bones


def _stack_heads(x16, head0):
    zero = jnp.zeros_like(x16)
    return jnp.concatenate([jnp.where(head0, x16, zero), jnp.where(head0, zero, x16)], axis=0)


def _bf16_terms(x, terms):
    out = []
    for _ in range(terms):
        t = x.astype(bf16)
        out.append(t)
        x = x - t.astype(f32)
    return out


def _dot_01(x, w16, terms):
    return _nn(jnp.concatenate(_bf16_terms(x, terms), axis=1), jnp.concatenate([w16] * terms, axis=0))


def _split_dot_sum(x, w16):
    hi, lo = _bf16_terms(x, 2)
    return _nn(hi, w16) + _nn(lo, w16)


def _dot_01_left(w16, x, terms):
    return _nn(jnp.concatenate([w16] * terms, axis=1), jnp.concatenate(_bf16_terms(x, terms), axis=0))


def _quarter_rows(i, q):
    return pl.ds(pl.multiple_of((i // 2) * 2048 + q * 512 + (i % 2) * 256, 256), 256)


def _token_rows(i, q):
    return pl.ds(i * 1024 + q, 256, stride=4)


def _quarter_block(i, d, nb):
    assert isinstance(i, int)
    r, blk = i // nb, i % nb
    if d == 1:
        runs = [pl.ds((blk // 16) * 2048 + q * 512 + (blk % 16) * 32, 32) for q in range(4)]
    elif d == 4:
        runs = [pl.ds((blk // 4) * 2048 + r * 512 + (blk % 4) * BLK, BLK)]
    else:
        runs = [pl.ds(blk * 2048 + (r % 4) * 512 + r // 4, BLK, stride=4)]
    return runs, blk > 0


def _if_prev(has_prev, x, fill):
    return x if has_prev else jnp.full_like(x, fill)


def _quarter_mask():
    order = lambda n: 4 * (n % 32) + n // 32
    return order(_iota((BLK, 2 * LANES), 1) % LANES) <= order(_iota((BLK, 2 * LANES), 0))


def _load_runs(ref, runs):
    parts = [ref[run, :] for run in runs]
    return parts[0] if len(parts) == 1 else jnp.concatenate(parts, axis=0)


def _store_runs(ref, runs, val):
    n = BLK // len(runs)
    for t, run in enumerate(runs):
        ref[run, :] = val[t * n:(t + 1) * n]


def _add_runs(ref, runs, val):
    n = BLK // len(runs)
    for t, run in enumerate(runs):
        ref[run, :] += val[t * n:(t + 1) * n]


def _attn_fwd(proj):
    s = proj.shape[0]
    n_it = s // BLK

    def body(q_in, k_in, v_in, g_ref, o_ref, l_ref, mix_ref, q_ref, k_ref, v_ref, op0, op1, op2, lp0, lp1, lp2,
             s_a, s_b, sd_a, sd_b, p_a, p_b, m_a, m_b, pd_a, pd_b, k_a, k_b, v_a, v_b, stage):
        op_refs, lp_refs = (op0, op1, op2), (lp0, lp1, lp2)
        head0, tri2_t, ones2, rmat, _ = _attn_consts()
        tri2_q = _quarter_mask()

        def reorder(i, carry):
            for src, dst, scale in ((q_in, q_ref, 0.125), (k_in, k_ref, 1.0), (v_in, v_ref, 1.0)):
                for q in range(4):
                    t = src[_token_rows(i, q), :]
                    dst[_quarter_rows(i, q), :] = t if scale == 1.0 else t * scale
            return carry

        lax.fori_loop(0, s // 1024, reorder, 0)
        score_bufs, prob_bufs = ((s_a, sd_a), (s_b, sd_b)), ((p_a, m_a, pd_a), (p_b, m_b, pd_b))
        k_bufs, v_bufs = (k_a, k_b), (v_a, v_b)
        for buf in k_bufs + v_bufs:
            buf[...] = jnp.zeros_like(buf)

        def unstack(st16):
            return st16[:BLK] + st16[BLK:]

        def scores(i, par, d, nb):
            rows, has_prev = _quarter_block(i, d, nb)
            tri2 = tri2_q if d == 1 else tri2_t
            s_buf, sd_buf = score_bufs[par]
            qs = _load_runs(q_ref, rows)
            qs16 = qs.astype(bf16)
            kst_c = _stack_heads(_load_runs(k_ref, rows).astype(bf16), head0)
            kst_p = k_bufs[1 - par][...]
            k_bufs[par][...] = kst_c
            sc = _nt(qs16, kst_c)
            sp = _nt(qs16, kst_p)
            s_buf[...] = jnp.where(tri2, sc, _if_prev(has_prev, sp, -jnp.inf))
            sd = _nn((qs * unstack(kst_p).astype(f32)).astype(bf16), ones2)
            sd_buf[...] = _if_prev(has_prev, sd, -jnp.inf)

        def softmax(bufs_in, bufs_out):
            s_buf, sd_buf = bufs_in
            p_buf, m_buf, pd_buf = bufs_out
            sc, sd2 = s_buf[...], sd_buf[...]
            m0 = jnp.max(sc[:, :LANES], axis=1, keepdims=True)
            m1 = jnp.max(sc[:, LANES:], axis=1, keepdims=True)
            m2 = jnp.concatenate([jnp.broadcast_to(m0, (BLK, LANES)), jnp.broadcast_to(m1, (BLK, LANES))], axis=1)
            m2 = jnp.maximum(m2, sd2)
            p_buf[...] = jnp.exp(sc - m2).astype(bf16)
            m_pair = jnp.where(head0, m2[:, :LANES], m2[:, LANES:])
            m_buf[...] = m_pair
            pd_buf[...] = jnp.exp(jnp.where(head0, sd2[:, :LANES], sd2[:, LANES:]) - m_pair)

        def output(i, par, d, nb, p):
            rows, has_prev = _quarter_block(i, d, nb)
            tri2 = tri2_q if d == 1 else tri2_t
            p_buf, m_buf, pd_buf = prob_bufs[par]
            vst_c = _stack_heads(_load_runs(v_ref, rows).astype(bf16), head0)
            vst_p = v_bufs[1 - par][...]
            v_bufs[par][...] = vst_c
            pt16, pd = p_buf[...], pd_buf[...]
            zero = jnp.zeros_like(pt16)
            o = _nn(jnp.where(tri2, pt16, zero), vst_c)
            if has_prev:
                o = o + _nn(jnp.where(tri2, zero, pt16), vst_p) + pd * unstack(vst_p).astype(f32)
            l = _nn(pt16, rmat) + pd
            _store_runs(op_refs[p], rows, o / l)
            _store_runs(lp_refs[p], rows, m_buf[...] + jnp.log(l))

        for p, d in enumerate(DILATIONS):
            nb = s // (BLK * d)
            scores(0, 0, d, nb)
            scores(1, 1, d, nb)
            softmax(score_bufs[0], prob_bufs[0])

            for t in range(2, n_it):
                par = t % 2
                scores(t, par, d, nb)
                output(t - 2, par, d, nb, p)
                softmax(score_bufs[1 - par], prob_bufs[1 - par])
            output(n_it - 2, 0, d, nb, p)
            softmax(score_bufs[1], prob_bufs[1])
            output(n_it - 1, 1, d, nb, p)

        def merge(i, carry):
            for q in range(4):
                rows, tokens = _quarter_rows(i, q), _token_rows(i, q)
                l0, l1, l2 = lp0[rows, :], lp1[rows, :], lp2[rows, :]
                m = jnp.maximum(jnp.maximum(l0, l1), l2)
                e0, e1, e2 = jnp.exp(l0 - m), jnp.exp(l1 - m), jnp.exp(l2 - m)
                z = e0 + e1 + e2
                o = (e0 * op0[rows, :] + e1 * op1[rows, :] + e2 * op2[rows, :]) / z
                o_ref[tokens, :] = o
                l_ref[rows, :] = m + jnp.log(z)
                g = g_ref[tokens, :]
                stage[pl.ds(q, 256, stride=4), :] = o * (g * _sigmoid(g))
            mix_ref[pl.ds(pl.multiple_of(i * 1024, 1024), 1024), :] = stage[...].astype(bf16)
            return carry

        lax.fori_loop(0, s // 1024, merge, 0)

    col = lambda base: pl.BlockSpec((s, LANES), lambda h: (0, base + h))
    return pl.pallas_call(
        body, name="attn_fwd", grid=(N_PAIRS,),
        in_specs=[col(0), col(8), col(16), col(24)],
        out_specs=[col(0)] * 6,
        out_shape=[SDS((s, D_ATTN), f32), SDS((s, D_ATTN), f32), SDS((s, D_ATTN), bf16)] + [SDS((s, D_ATTN), f32)] * 3,
        scratch_shapes=[pltpu.VMEM((s, LANES), f32)] * 6 + [pltpu.VMEM((BLK, 2 * LANES), f32)] * 4
        + [pltpu.VMEM((BLK, 2 * LANES), bf16)] * 2 + [pltpu.VMEM((BLK, LANES), f32)] * 4
        + [pltpu.VMEM((2 * BLK, LANES), bf16)] * 4 + [pltpu.VMEM((1024, LANES), f32)],
        compiler_params=pltpu.CompilerParams(dimension_semantics=("parallel",)),
    )(proj, proj, proj, proj)


def _expand_mat():
    colv = _iota((LANES, 2 * D_SSM), 1)
    head = 2 * ((colv % D_SSM) // LANES) + colv // D_SSM
    return (_iota((LANES, 2 * D_SSM), 0) == head).astype(bf16)


def _fold_mat():
    return (_iota((D_SSM, LANES), 0) // HEAD_DIM == _iota((D_SSM, LANES), 1)).astype(bf16)


def _conv(xs_ref, bc_ref, xs_tail, bc_tail, cw_ref, cb_ref, xpad, first):
    keep = jnp.where(first, 0.0, 1.0)
    xpad[0:8, 0:D_SSM] = xs_tail[...] * keep
    xpad[0:8, D_SSM:D_CONV] = bc_tail[...] * keep
    xpad[8:8 + CHUNK, 0:D_SSM] = xs_ref[...]
    xpad[8:8 + CHUNK, D_SSM:D_CONV] = bc_ref[...]
    xp = xpad[...]
    cv = cb_ref[...] + cw_ref[3:4, :] * xp[8:8 + CHUNK]
    for j in range(3):
        cv = cv + cw_ref[j:j + 1, :] * pltpu.roll(xp, 3 - j, 0)[8:8 + CHUNK]
    return cv


def _decay_terms(dt_ref, dtb_ref, alog16_ref, emat_ref):
    pre = dt_ref[...] + dtb_ref[...]
    dt16 = _softplus(pre)
    a16 = -jnp.exp(alog16_ref[...])
    sub, lane = _iota((CHUNK, CHUNK), 0), _iota((CHUNK, CHUNK), 1)
    tri = (sub >= lane).astype(f32)
    al16 = _nn_hi(tri, dt16 * a16)
    al_t = al16.T
    emat = emat_ref[...]
    dt_x = _dot_01(dt16, emat, 3)
    al_x = _dot_01(al16, emat, 3)
    lane_w = _iota((CHUNK, D_SSM), 1)
    even = (lane_w % LANES) < HEAD_DIM
    dt_f = jnp.where(even, dt_x[:, :D_SSM], dt_x[:, D_SSM:])
    al_f = jnp.where(even, al_x[:, :D_SSM], al_x[:, D_SSM:])
    return pre, dt_f, al_f, al_x, al_t


def _decay_mat(al_x, al_t, pair, h):
    sub, lane = _iota((CHUNK, CHUNK), 0), _iota((CHUNK, CHUNK), 1)
    col = al_x[:, h * D_SSM + pair * LANES: h * D_SSM + (pair + 1) * LANES]
    row = al_t[2 * pair + h: 2 * pair + h + 1, :]
    return jnp.exp(jnp.where(sub >= lane, col - row, -jnp.inf))


def _ssd_in_specs(order):
    blk = lambda w, cb: pl.BlockSpec((CHUNK, w), lambda i: (order(i), cb))
    tail = lambda w, cb: pl.BlockSpec((8, w), lambda i: (jnp.maximum(16 * order(i) - 1, 0), cb))
    return [blk(D_SSM, COL_XS // D_SSM), blk(512, COL_BC // 512), tail(D_SSM, COL_XS // D_SSM),
            tail(512, COL_BC // 512), blk(LANES, COL_DT // LANES), blk(D_SSM, COL_Z // D_SSM)]


def _full(shape):
    return pl.BlockSpec(shape, lambda i: (0,) * len(shape))


def _ssd_fwd(proj, conv_w, conv_b, dtb16, alog16, alog_f, d_f, nw):
    s = proj.shape[0]
    nc = s // CHUNK

    def body(xs_ref, bc_ref, xs_tail, bc_tail, dt_ref, z_ref, cw_ref, cb_ref, dtb_ref, alog16_ref, alogf_ref,
             df_ref, nw_ref, mix_ref, y_ref, st_ref, cv_ref, h_scr, xpad, y_scr, emat_ref):
        c = pl.program_id(0)

        @pl.when(c == 0)
        def _():
            h_scr[...] = jnp.zeros_like(h_scr)
            emat_ref[...] = _expand_mat()

        cv = _conv(xs_ref, bc_ref, xs_tail, bc_tail, cw_ref, cb_ref, xpad, c == 0)
        cv_ref[...] = cv
        xbc = cv * _sigmoid(cv)
        _, dt_f, al_f, al_x, al_t = _decay_terms(dt_ref, dtb_ref, alog16_ref, emat_ref)
        head0 = _iota((CHUNK, LANES), 1) < HEAD_DIM
        st_ref[...] = h_scr[...]
        for g in range(N_GROUPS):
            bm = xbc[:, D_SSM + g * D_STATE: D_SSM + (g + 1) * D_STATE].astype(bf16)
            cm = xbc[:, D_SSM + (N_GROUPS + g) * D_STATE: D_SSM + (N_GROUPS + g + 1) * D_STATE].astype(bf16)
            gmat = _nt(cm, bm)
            for pair in range(4 * g, 4 * g + 4):
                sl = slice(pair * LANES, (pair + 1) * LANES)
                xp, dtp, alp = xbc[:, sl], dt_f[:, sl], al_f[:, sl]
                xdt = xp * dtp
                xdt16 = xdt.astype(bf16)
                al_last = alp[CHUNK - 1:CHUNK, :]
                hp = h_scr[:, sl]
                y_off = jnp.exp(alp) * _nn(cm, hp.astype(bf16))
                yd = [_nn((gmat * _decay_mat(al_x, al_t, pair, h)).astype(bf16), xdt16) for h in range(2)]
                y_scr[:, sl] = jnp.where(head0, yd[0], yd[1]) + y_off + df_ref[:, sl] * xp
                st = _tn(bm, (jnp.exp(al_last - alp) * xdt).astype(bf16))
                h_scr[:, sl] = jnp.exp(al_last) * hp + st
        y = y_scr[...]
        y_ref[...] = y
        z = z_ref[...]
        yz = y * (z * _sigmoid(z))
        gw = D_SSM // N_GROUPS
        for g in range(N_GROUPS):
            part = yz[:, g * gw:(g + 1) * gw]
            r = lax.rsqrt(jnp.mean(part * part, axis=-1, keepdims=True) + EPS)
            mix_ref[:, g * gw:(g + 1) * gw] = (part * r * nw_ref[:, g * gw:(g + 1) * gw]).astype(bf16)

    order = lambda i: i
    row = lambda w: pl.BlockSpec((CHUNK, w), lambda i: (i, 0))
    return pl.pallas_call(
        body, name="ssd_fwd", grid=(nc,),
        in_specs=_ssd_in_specs(order) + [_full((4, D_CONV)), _full((1, D_CONV)), _full((1, LANES)), _full((1, LANES)),
                                         _full((1, D_SSM)), _full((1, D_SSM)), _full((1, D_SSM))],
        out_specs=[row(D_SSM), row(D_SSM), pl.BlockSpec((None, D_STATE, D_SSM), lambda i: (i, 0, 0)), row(D_CONV)],
        out_shape=[SDS((s, D_SSM), bf16), SDS((s, D_SSM), f32), SDS((nc, D_STATE, D_SSM), f32),
                   SDS((s, D_CONV), f32)],
        scratch_shapes=[pltpu.VMEM((D_STATE, D_SSM), f32), pltpu.VMEM((8 + CHUNK, D_CONV), f32),
                        pltpu.VMEM((CHUNK, D_SSM), f32), pltpu.VMEM((LANES, 2 * D_SSM), bf16)],
        compiler_params=pltpu.CompilerParams(dimension_semantics=("arbitrary",)),
    )(proj, proj, proj, proj, proj, proj, conv_w, conv_b, dtb16, alog16, alog_f, d_f, nw)


def _outproj_loss(mix_a, mix_s, wo, x, tgt, npw):
    s, d = x.shape
    tm = 512

    def body(ma_ref, ms_ref, wo_ref, x_ref, t_ref, npw_ref, dmix_ref, dres_ref, acc_ref, dwo_ref):
        @pl.when(pl.program_id(0) == 0)
        def _():
            acc_ref[...] = jnp.zeros_like(acc_ref)
            dwo_ref[...] = jnp.zeros_like(dwo_ref)

        out = _nn(ma_ref[...], wo_ref[0:D_ATTN, :]) + _nn(ms_ref[...], wo_ref[D_ATTN:, :])
        r = lax.rsqrt(jnp.mean(out * out, axis=-1, keepdims=True) + EPS)
        on = out * r
        diff = x_ref[...] + on * npw_ref[...] - t_ref[...]
        dres = diff * (1.0 / d)
        dres_ref[...] = dres
        acc_ref[0:1, :] += jnp.sum(diff * diff, axis=0, keepdims=True)
        acc_ref[1:2, :] += jnp.sum(dres * on, axis=0, keepdims=True)
        dn = dres * npw_ref[...]
        dout = (r * (dn - on * jnp.mean(dn * on, axis=-1, keepdims=True))).astype(bf16)
        dmix_ref[...] = _nt(dout, wo_ref[...])
        dwo_ref[0:D_ATTN, :] += _tn(ma_ref[...], dout)
        dwo_ref[D_ATTN:, :] += _tn(ms_ref[...], dout)

    row = lambda w: pl.BlockSpec((tm, w), lambda i: (i, 0))
    return pl.pallas_call(
        body, name="outproj_loss", grid=(s // tm,),
        in_specs=[row(D_ATTN), row(D_SSM), _full((D_ATTN + D_SSM, d)), row(d), row(d), _full((1, d))],
        out_specs=[row(D_ATTN + D_SSM), row(d), _full((8, d)), _full((D_ATTN + D_SSM, d))],
        out_shape=[SDS((s, D_ATTN + D_SSM), f32), SDS((s, d), f32), SDS((8, d), f32), SDS((D_ATTN + D_SSM, d), f32)],
        compiler_params=pltpu.CompilerParams(dimension_semantics=("arbitrary",)),
    )(mix_a, mix_s, wo, x, tgt, npw)


def _attn_bwd(proj, qkv, o, lb, dmix, swap=None):
    s = proj.shape[0]
    n_it = s // BLK

    nsw = 0 if swap is None else 1

    def body(*refs):
        q_ref, k_ref, v_ref, g_ref, o_ref, l_ref, dm_ref = refs[:7]
        swap_in = refs[7:7 + nsw]
        dq_ref, dk_ref, dv_ref, dg_ref = refs[7 + nsw:11 + nsw]
        swap_out = refs[11 + nsw:11 + 2 * nsw]
        dq_acc, dk_acc, dv_acc, do_scr, dl_scr = refs[11 + 2 * nsw:16 + 2 * nsw]
        bufs = refs[16 + 2 * nsw:44 + 2 * nsw]
        stage_a, stage_b = refs[44 + 2 * nsw:46 + 2 * nsw]
        swap_sems = refs[46 + 2 * nsw:]
        head0, tri2_t, _, _, bones = _attn_consts()
        tri2_q = _quarter_mask()

        if nsw:
            x, y, c = _my_pos()
            swap_copy = pltpu.make_async_remote_copy(
                src_ref=swap_in[0], dst_ref=swap_out[0], send_sem=swap_sems[0], recv_sem=swap_sems[1],
                device_id=(x, y, 1 - c), device_id_type=MESH)

            @pl.when(pl.program_id(0) == 0)
            def _():
                swap_copy.start()

        quarter_rows, load, add = _quarter_rows, _load_runs, _add_runs

        def pro(i, carry):
            for t in range(4):
                rows = pl.ds(pl.multiple_of(i * 1024 + t * 256, 256), 256)
                g = g_ref[rows, :]
                sg = _sigmoid(g)
                dmx = dm_ref[rows, :]
                ov = o_ref[rows, :]
                dg_ref[rows, :] = (dmx * ov * (sg * (1.0 + g * (1.0 - sg)))).astype(bf16)
                do = dmx * (g * sg)
                stage_a[t * 256:(t + 1) * 256, :] = do
                stage_b[t * 256:(t + 1) * 256, :] = _split_dot_sum(do * ov, bones)
            z = jnp.zeros((256, LANES), f32)
            for q in range(4):
                rows = quarter_rows(i, q)
                do_scr[rows, :] = stage_a[pl.ds(q, 256, stride=4), :]
                dl_scr[rows, :] = stage_b[pl.ds(q, 256, stride=4), :]
                dq_acc[rows, :] = z
                dk_acc[rows, :] = z
                dv_acc[rows, :] = z
            return carry

        lax.fori_loop(0, s // 1024, pro, 0)

        def per_head(t):
            return jnp.concatenate([t[:, :LANES], t[:, LANES:]], axis=0)

        def both_heads(t):
            tr = pltpu.roll(t, HEAD_DIM, 1)
            return jnp.concatenate([jnp.where(head0, t, tr), jnp.where(head0, tr, t)], axis=1)

        mm_bufs = ((bufs[0], bufs[1], bufs[2], bufs[3]), (bufs[4], bufs[5], bufs[6], bufs[7]))
        ds_bufs = ((bufs[8], bufs[9], bufs[10], bufs[11]), (bufs[12], bufs[13], bufs[14], bufs[15]))
        op_bufs = ((bufs[16], bufs[17], bufs[18], bufs[19]), (bufs[20], bufs[21], bufs[22], bufs[23]))
        vc_bufs, carry_k, carry_v = (bufs[24], bufs[25]), bufs[26], bufs[27]
        for buf in (op_bufs[0][0], op_bufs[1][0]) + vc_bufs:
            buf[...] = jnp.zeros_like(buf)

        def block_rows(i, d, nb):
            rows, has_prev = _quarter_block(i, d, nb)
            return rows, rows, has_prev

        def unstack(st16):
            return st16[:BLK] + st16[BLK:]

        def products(i, par, d, nb):
            src, scr, has_prev = block_rows(i, d, nb)
            tri2 = tri2_q if d == 1 else tri2_t
            s_buf, dp_buf, sd_buf, dpd_buf = mm_bufs[par]
            kc_buf, kp_buf, q_buf, do_buf = op_bufs[par]
            qs = load(q_ref, src)
            do = load(do_scr, scr)
            qs16, do16 = qs.astype(bf16), do.astype(bf16)
            kst_c = _stack_heads(load(k_ref, src).astype(bf16), head0)
            vst_c = _stack_heads(load(v_ref, src).astype(bf16), head0)
            kst_p, vst_p = op_bufs[1 - par][0][...], vc_bufs[1 - par][...]
            kc_buf[...] = kst_c
            kp_buf[...] = kst_p
            vc_bufs[par][...] = vst_c
            q_buf[...] = qs16
            do_buf[...] = do16
            s_buf[...] = jnp.where(tri2, _nt(qs16, kst_c), _if_prev(has_prev, _nt(qs16, kst_p), -jnp.inf))
            dp_buf[...] = jnp.where(tri2, _nt(do16, vst_c), _if_prev(has_prev, _nt(do16, vst_p), 0.0))
            sd_buf[...] = _nn((qs * unstack(kst_p).astype(f32)).astype(bf16), bones)
            dpd_buf[...] = _if_prev(has_prev, _nn((do * unstack(vst_p).astype(f32)).astype(bf16), bones), 0.0)

        def softmax_grad(i, par, d, nb):
            src, scr, has_prev = block_rows(i, d, nb)
            s_buf, dp_buf, sd_buf, dpd_buf = mm_bufs[par]
            p_buf, ds_buf, pd_buf, dsd_buf = ds_bufs[par]
            lse = load(l_ref, src)
            dl = load(dl_scr, scr)
            pt = jnp.exp(s_buf[...] - both_heads(lse))
            ds_buf[...] = (pt * (dp_buf[...] - both_heads(dl))).astype(bf16)
            p_buf[...] = pt.astype(bf16)
            pd = _if_prev(has_prev, jnp.exp(sd_buf[...] - lse), 0.0)
            pd_buf[...] = pd
            dsd_buf[...] = pd * (dpd_buf[...] - dl)

        def accumulate(i, par, d, nb):
            _, rows, has_prev = block_rows(i, d, nb)
            _, before, _ = block_rows(max(i - 1, 0), d, nb)
            tri2 = tri2_q if d == 1 else tri2_t
            p_buf, ds_buf, pd_buf, dsd_buf = ds_bufs[par]
            kc_buf, kp_buf, q_buf, do_buf = op_bufs[par]
            pt16, ds16, pd, dsd = p_buf[...], ds_buf[...], pd_buf[...], dsd_buf[...]
            zero = jnp.zeros_like(pt16)
            dsc, dsp = jnp.where(tri2, ds16, zero), jnp.where(tri2, zero, ds16)
            pc, pp = jnp.where(tri2, pt16, zero), jnp.where(tri2, zero, pt16)
            kst_c, kst_p, q16, do16 = kc_buf[...], kp_buf[...], q_buf[...], do_buf[...]
            qst, dost = _stack_heads(q16, head0), _stack_heads(do16, head0)
            if not has_prev:
                add(dq_acc, rows, _nn(dsc, kst_c))
                add(dk_acc, before, carry_k[...])
                add(dv_acc, before, carry_v[...])
                carry_k[...] = _tn(per_head(dsc), qst)
                carry_v[...] = _tn(per_head(pc), dost)
                return
            add(dq_acc, rows, _nn(dsc, kst_c) + _nn(dsp, kst_p) + dsd * unstack(kst_p).astype(f32))
            dk2 = _tn(jnp.concatenate([per_head(dsc), per_head(dsp)], axis=1), qst)
            dv2 = _tn(jnp.concatenate([per_head(pc), per_head(pp)], axis=1), dost)
            add(dk_acc, before, carry_k[...] + dk2[BLK:] + dsd * q16.astype(f32))
            add(dv_acc, before, carry_v[...] + dv2[BLK:] + pd * do16.astype(f32))
            carry_k[...] = dk2[:BLK]
            carry_v[...] = dv2[:BLK]

        for d in DILATIONS:
            nb = s // (BLK * d)
            carry_k[...] = jnp.zeros_like(carry_k)
            carry_v[...] = jnp.zeros_like(carry_v)
            products(0, 0, d, nb)
            products(1, 1, d, nb)
            softmax_grad(0, 0, d, nb)

            for t in range(2, n_it):
                par = t % 2
                accumulate(t - 2, par, d, nb)
                products(t, par, d, nb)
                softmax_grad(t - 1, 1 - par, d, nb)
            accumulate(n_it - 2, 0, d, nb)
            softmax_grad(n_it - 1, 1, d, nb)
            accumulate(n_it - 1, 1, d, nb)
            _, last, _ = block_rows(n_it - 1, d, nb)
            add(dk_acc, last, carry_k[...])
            add(dv_acc, last, carry_v[...])

        def epi(i, carry):
            rows = pl.ds(pl.multiple_of(i * 1024, 1024), 1024)
            for acc, out, stage, scale in ((dq_acc, dq_ref, stage_a, 0.125), (dk_acc, dk_ref, stage_b, 1.0),
                                           (dv_acc, dv_ref, stage_a, 1.0)):
                for q in range(4):
                    stage[pl.ds(q, 256, stride=4), :] = acc[quarter_rows(i, q), :]
                out[rows, :] = (stage[...] if scale == 1.0 else stage[...] * scale).astype(bf16)
            return carry

        lax.fori_loop(0, s // 1024, epi, 0)

        if nsw:
            @pl.when(pl.program_id(0) == N_PAIRS - 1)
            def _():
                swap_copy.wait_send()
                swap_copy.wait_recv()

    col = lambda base: pl.BlockSpec((s, LANES), lambda h: (0, base + h))
    anyspec = pl.BlockSpec(memory_space=pl.ANY)
    swaps = [] if swap is None else [swap]
    outs = pl.pallas_call(
        body, name="attn_bwd", grid=(N_PAIRS,),
        in_specs=[col(0), col(0), col(0), col(24), col(0), col(0), col(0)] + [anyspec] * nsw,
        out_specs=[col(0)] * 4 + [anyspec] * nsw,
        out_shape=[SDS((s, D_ATTN), bf16)] * 4 + [SDS(a.shape, a.dtype) for a in swaps],
        scratch_shapes=[pltpu.VMEM((s, LANES), f32)] * 5
        + [pltpu.VMEM((BLK, 2 * LANES), f32)] * 2 + [pltpu.VMEM((BLK, LANES), f32)] * 2
        + [pltpu.VMEM((BLK, 2 * LANES), f32)] * 2 + [pltpu.VMEM((BLK, LANES), f32)] * 2
        + [pltpu.VMEM((BLK, 2 * LANES), bf16)] * 2 + [pltpu.VMEM((BLK, LANES), f32)] * 2
        + [pltpu.VMEM((BLK, 2 * LANES), bf16)] * 2 + [pltpu.VMEM((BLK, LANES), f32)] * 2
        + [pltpu.VMEM((2 * BLK, LANES), bf16)] * 2 + [pltpu.VMEM((BLK, LANES), bf16)] * 2
        + [pltpu.VMEM((2 * BLK, LANES), bf16)] * 2 + [pltpu.VMEM((BLK, LANES), bf16)] * 2
        + [pltpu.VMEM((2 * BLK, LANES), bf16)] * 2 + [pltpu.VMEM((BLK, LANES), f32)] * 2
        + [pltpu.VMEM((1024, LANES), f32)] * 2
        + [pltpu.SemaphoreType.DMA(())] * (2 * nsw),
        compiler_params=pltpu.CompilerParams(dimension_semantics=("arbitrary",)),
    )(*qkv, proj, o, lb, dmix, *swaps)
    return outs


def _ssd_bwd(proj, y, states, cv, dmix, conv_w, conv_b, dtb16, alog16, alog_f, d_f, nw, chip_sums=()):
    s = proj.shape[0]
    nc = s // CHUNK
    gw = D_SSM // N_GROUPS
    nx = len(chip_sums)

    def body(*refs):
        (xs_ref, bc_ref, _, _, dt_ref, z_ref, y_ref, st_ref, dm_ref, cw_ref, cb_ref, dtb_ref,
         alog16_ref, alogf_ref, df_ref, nw_ref, cv_ref) = refs[:17]
        cs_in = refs[17:17 + nx]
        out_ref, gconv_ref, gvec_ref, gdt_ref = refs[17 + nx:21 + nx]
        cs_out = refs[21 + nx:21 + 2 * nx]
        (dh_scr, head_scr, dcpad, da_scr, dxdt_scr, dbc_scr, emat_ref, fold_ref) = refs[21 + 2 * nx:29 + 2 * nx]
        cs_sems = refs[29 + 2 * nx:]
        i = pl.program_id(0)
        c = nc - 1 - i

        if nx:
            @pl.when(i == 0)
            def _():
                mine, sends, _ = _chip_exchange_copies(cs_in, cs_out, *cs_sems)
                for cp in mine + sends:
                    cp.start()

            @pl.when(i == nc - 1)
            def _():
                mine, sends, recvs = _chip_exchange_copies(cs_in, cs_out, *cs_sems)
                for cp in recvs:
                    cp.wait_recv()
                for cp in sends:
                    cp.wait_send()
                for cp in mine:
                    cp.wait()

        @pl.when(i == 0)
        def _():
            emat_ref[...] = _expand_mat()
            fold_ref[...] = _fold_mat()
            dh_scr[...] = jnp.zeros_like(dh_scr)
            head_scr[...] = jnp.zeros_like(head_scr)
            gconv_ref[...] = jnp.zeros_like(gconv_ref)
            gvec_ref[...] = jnp.zeros_like(gvec_ref)
            gdt_ref[...] = jnp.zeros_like(gdt_ref)

        cv = cv_ref[...]
        sig = _sigmoid(cv)
        xbc = cv * sig
        pre, dt_f, al_f, al_x, al_t = _decay_terms(dt_ref, dtb_ref, alog16_ref, emat_ref)
        head0 = _iota((CHUNK, LANES), 1) < HEAD_DIM
        sub = _iota((CHUNK, LANES), 0)
        last_row = sub == CHUNK - 1

        yv, z, dmx = y_ref[...], z_ref[...], dm_ref[...]
        sz = _sigmoid(z)
        silu = z * sz
        yz = yv * silu
        dyz_parts = []
        for g in range(N_GROUPS):
            gs = slice(g * gw, (g + 1) * gw)
            part = yz[:, gs]
            r = lax.rsqrt(jnp.mean(part * part, axis=-1, keepdims=True) + EPS)
            nh = part * r
            gvec_ref[0:1, gs] += jnp.sum(dmx[:, gs] * nh, axis=0, keepdims=True)
            dn = dmx[:, gs] * nw_ref[:, gs]
            dyz_parts.append(r * (dn - nh * jnp.mean(dn * nh, axis=-1, keepdims=True)))
        dyz = jnp.concatenate(dyz_parts, axis=1)
        dy = dyz * silu
        out_ref[:, 0:D_SSM] = (dyz * yv * (sz * (1.0 + z * (1.0 - sz)))).astype(bf16)

        x_all = xbc[:, 0:D_SSM]
        gvec_ref[2:3, :] += jnp.sum(dy * x_all, axis=0, keepdims=True)

        for g in range(N_GROUPS):
            bm = xbc[:, D_SSM + g * D_STATE: D_SSM + (g + 1) * D_STATE].astype(bf16)
            cm = xbc[:, D_SSM + (N_GROUPS + g) * D_STATE: D_SSM + (N_GROUPS + g + 1) * D_STATE].astype(bf16)
            gmat = _nt(cm, bm)
            dgm = jnp.zeros((CHUNK, CHUNK), f32)
            db = jnp.zeros((CHUNK, D_STATE), f32)
            dc = jnp.zeros((CHUNK, D_STATE), f32)
            for pair in range(4 * g, 4 * g + 4):
                sl = slice(pair * LANES, (pair + 1) * LANES)
                xp, dtp, alp, dyp = x_all[:, sl], dt_f[:, sl], al_f[:, sl], dy[:, sl]
                xdt = xp * dtp
                xdt16 = xdt.astype(bf16)
                al_last = alp[CHUNK - 1:CHUNK, :]
                e_l = jnp.exp(alp)
                wf = jnp.exp(al_last - alp)
                e_last = jnp.exp(al_last)
                hp = st_ref[:, sl]
                hp16 = hp.astype(bf16)
                dhn = dh_scr[:, sl]
                dhn16 = dhn.astype(bf16)
                y_off = e_l * _nn(cm, hp16)
                dch16 = (dyp * e_l).astype(bf16)
                dc = dc + _nt(dch16, hp16)
                dh_out = _tn(cm, dch16)
                dal = dyp * y_off
                xw16 = (wf * xdt).astype(bf16)
                db = db + _nt(xw16, dhn16)
                dxw = _nn(bm, dhn16)
                dxdt = dxw * wf
                dwf = dxw * xdt * wf
                dal = dal - dwf
                dal_last = jnp.sum(dwf, axis=0, keepdims=True) + jnp.sum(dhn * hp, axis=0, keepdims=True) * e_last
                dh_scr[:, sl] = e_last * dhn + dh_out
                for h in range(2):
                    mh = head0 if h == 0 else jnp.logical_not(head0)
                    dyh16 = jnp.where(mh, dyp, 0.0).astype(bf16)
                    lmat = _decay_mat(al_x, al_t, pair, h)
                    mm = gmat * lmat
                    dmm = _nt(dyh16, xdt16)
                    dxdt = dxdt + _tn(mm.astype(bf16), dyh16)
                    n16 = (dmm * mm).astype(bf16)
                    jh = jnp.where(mh, 1.0 / HEAD_DIM, 0.0).astype(bf16)
                    dal = dal + _nn(n16, jh) - _tn(n16, jh)
                    dgm = dgm + dmm * lmat
                da_scr[:, sl] = dal + jnp.where(last_row, dal_last, 0.0)
                dxdt_scr[:, sl] = dxdt
            dgm16 = dgm.astype(bf16)
            dbc_scr[:, g * D_STATE:(g + 1) * D_STATE] = db + _tn(dgm16, cm)
            dbc_scr[:, (N_GROUPS + g) * D_STATE:(N_GROUPS + g + 1) * D_STATE] = dc + _nn(dgm16, bm)

        sub_c, lane_c = _iota((CHUNK, CHUNK), 0), _iota((CHUNK, CHUNK), 1)
        tri_t = (lane_c >= sub_c).astype(bf16)
        dadt = _dot_01_left(tri_t, da_scr[...], 2)
        a_f = -jnp.exp(alogf_ref[...])
        dxdt_all = dxdt_scr[...]
        ddt_f = dxdt_all * x_all + a_f * dadt
        gvec_ref[1:2, :] += jnp.sum(dt_f * dadt, axis=0, keepdims=True) * a_f
        dx = df_ref[...] * dy + dxdt_all * dt_f
        ddt_raw = _dot_01(ddt_f, fold_ref[...], 2) * _sigmoid(pre)
        gdt_ref[0:1, :] += jnp.sum(ddt_raw, axis=0, keepdims=True)
        out_ref[:, D_SSM + D_CONV:D_SSM + D_CONV + LANES] = ddt_raw.astype(bf16)
        out_ref[:, D_SSM + D_CONV + LANES:] = jnp.zeros((CHUNK, 3 * LANES), bf16)

        dsil = sig * (1.0 + cv * (1.0 - sig))
        dcv_x = dx * dsil[:, 0:D_SSM]
        dcv_bc = dbc_scr[...] * dsil[:, D_SSM:]
        dcpad[0:CHUNK, 0:D_SSM] = dcv_x
        dcpad[0:CHUNK, D_SSM:] = dcv_bc
        dcpad[CHUNK:, :] = head_scr[...]
        dcp = dcpad[...]
        dcv = dcp[0:CHUNK]
        gconv_ref[4:5, :] += jnp.sum(dcv, axis=0, keepdims=True)
        x_raw = jnp.concatenate([xs_ref[...], bc_ref[...]], axis=1)
        draw = cw_ref[3:4, :] * dcv
        gconv_ref[3:4, :] += jnp.sum(dcv * x_raw, axis=0, keepdims=True)
        for j in range(3):
            ahead = pltpu.roll(dcp, CHUNK + 8 - (3 - j), 0)[0:CHUNK]
            draw = draw + cw_ref[j:j + 1, :] * ahead
            gconv_ref[j:j + 1, :] += jnp.sum(ahead * x_raw, axis=0, keepdims=True)
        head_scr[...] = dcv[0:8]
        out_ref[:, D_SSM:D_SSM + D_CONV] = draw.astype(bf16)

    order = lambda i: nc - 1 - i
    row = lambda w, cb=0: pl.BlockSpec((CHUNK, w), lambda i: (nc - 1 - i, cb))
    anyspec = pl.BlockSpec(memory_space=pl.ANY)
    outs = pl.pallas_call(
        body, name="ssd_bwd", grid=(nc,),
        in_specs=_ssd_in_specs(order) + [row(D_SSM), pl.BlockSpec((None, D_STATE, D_SSM), lambda i: (nc - 1 - i, 0, 0)),
                                         row(D_SSM, 1), _full((4, D_CONV)), _full((1, D_CONV)), _full((1, LANES)),
                                         _full((1, LANES)), _full((1, D_SSM)), _full((1, D_SSM)), _full((1, D_SSM)),
                                         row(D_CONV)]
        + [anyspec] * nx,
        out_specs=[row(3072), _full((8, D_CONV)), _full((8, D_SSM)), _full((8, LANES))] + [anyspec] * nx,
        out_shape=[SDS((s, 3072), bf16), SDS((8, D_CONV), f32), SDS((8, D_SSM), f32), SDS((8, LANES), f32)]
        + [SDS(a.shape, a.dtype) for a in chip_sums],
        scratch_shapes=[pltpu.VMEM((D_STATE, D_SSM), f32), pltpu.VMEM((8, D_CONV), f32),
                        pltpu.VMEM((8 + CHUNK, D_CONV), f32),
                        pltpu.VMEM((CHUNK, D_SSM), f32), pltpu.VMEM((CHUNK, D_SSM), f32),
                        pltpu.VMEM((CHUNK, 2 * N_GROUPS * D_STATE), f32),
                        pltpu.VMEM((LANES, 2 * D_SSM), bf16), pltpu.VMEM((D_SSM, LANES), bf16)]
        + (_chip_exchange_scratch(nx) if nx else []),
        compiler_params=pltpu.CompilerParams(dimension_semantics=("arbitrary",)),
    )(proj, proj, proj, proj, proj, proj, y, states, dmix, conv_w, conv_b, dtb16, alog16, alog_f, d_f, nw, cv,
      *chip_sums)
    return outs[0], outs[1], outs[2], outs[3], outs[4:]


def _col_blocks(parts, tile):
    counts = [p.shape[1] // tile for p in parts]
    offs = [sum(counts[:t]) for t in range(len(parts))]
    return offs, counts, sum(counts)


def _bcast_copies(src_ref, out_ref, send_sems, recv_sems, local_sem):
    x, y, c = _my_pos()
    me = 4 * x + 2 * y + c
    mine = pltpu.make_async_copy(src_ref, out_ref.at[me], local_sem)
    sends, recvs = [], []
    for k in range(1, N_DEV):
        to, frm = (me + k) % N_DEV, (me + N_DEV - k) % N_DEV
        sems = dict(send_sem=send_sems.at[k - 1], recv_sem=recv_sems.at[k - 1], device_id_type=MESH)
        sends.append(pltpu.make_async_remote_copy(
            src_ref=src_ref, dst_ref=out_ref.at[me], device_id=(to // 4, (to // 2) % 2, to % 2), **sems))
        recvs.append(pltpu.make_async_remote_copy(
            src_ref=src_ref, dst_ref=out_ref.at[frm], device_id=(x, y, c), **sems))
    return mine, sends, recvs


def _bcast_scratch():
    return [pltpu.SemaphoreType.DMA((N_DEV - 1,)), pltpu.SemaphoreType.DMA((N_DEV - 1,)), pltpu.SemaphoreType.DMA(())]


def _inproj_bwd(dparts, wt, x, nw, dres, chip_sums=(), pack=None):
    s, d = x.shape
    tm, tk = 1024, 1024
    offs, counts, nk = _col_blocks(dparts, tk)
    npart, nx = len(dparts), len(chip_sums)
    npk = 0 if pack is None else 1
    ni = s // tm

    def body(*refs):
        dp_refs = refs[:npart]
        w_ref, x_ref, nw_ref, dres_ref = refs[npart:npart + 4]
        pos = npart + 4
        cs_in, pos = refs[pos:pos + nx], pos + nx
        pack_in, pos = refs[pos:pos + npk], pos + npk
        (gx_ref, gnw_ref), pos = refs[pos:pos + 2], pos + 2
        cs_out, pos = refs[pos:pos + nx], pos + nx
        pack_out, pos = refs[pos:pos + 2 * npk], pos + 2 * npk
        acc, pos = refs[pos], pos + 1
        cs_sems, pos = refs[pos:pos + 3 * min(nx, 1)], pos + 3 * min(nx, 1)
        pk_refs = refs[pos:]
        i, k = pl.program_id(0), pl.program_id(1)

        def exchange():
            return _chip_exchange_copies(cs_in, cs_out, *cs_sems)

        def pack_copies():
            return _bcast_copies(pack_in[0], pack_out[0], *pk_refs[1:4])

        def gnw_copies():
            return _bcast_copies(pk_refs[0], pack_out[1], *pk_refs[4:7])

        @pl.when(jnp.logical_and(i == 0, k == 0))
        def _():
            gnw_ref[...] = jnp.zeros_like(gnw_ref)
            if nx:
                mine, sends, _ = exchange()
                for cp in mine + sends:
                    cp.start()
            if npk:
                mine, sends, _ = pack_copies()
                for cp in [mine] + sends:
                    cp.start()

        @pl.when(k == 0)
        def _():
            acc[...] = _nn(dp_refs[0][...], w_ref[...])

        for t in range(npart):
            @pl.when(jnp.logical_and(k >= max(offs[t], 1), k < offs[t] + counts[t]))
            def _(t=t):
                acc[...] += _nn(dp_refs[t][...], w_ref[...])

        @pl.when(k == nk - 1)
        def _():
            xv = x_ref[...]
            r = lax.rsqrt(jnp.mean(xv * xv, axis=-1, keepdims=True) + EPS)
            xn = xv * r
            du = acc[...]
            gnw_ref[0:1, :] += jnp.sum(du * xn, axis=0, keepdims=True)
            dn = du * nw_ref[...]
            gx_ref[...] = dres_ref[...] + r * (dn - xn * jnp.mean(dn * xn, axis=-1, keepdims=True))

        @pl.when(jnp.logical_and(i == ni - 1, k == nk - 1))
        def _():
            if npk:
                pk_refs[0][...] = gnw_ref[...]
                mine, sends, _ = gnw_copies()
                for cp in [mine] + sends:
                    cp.start()
            if nx:
                mine, sends, recvs = exchange()
                for cp in recvs:
                    cp.wait_recv()
                for cp in sends:
                    cp.wait_send()
                for cp in mine:
                    cp.wait()
            if npk:
                for copies in (pack_copies(), gnw_copies()):
                    mine, sends, recvs = copies
                    for cp in recvs:
                        cp.wait_recv()
                    for cp in sends:
                        cp.wait_send()
                    mine.wait()

    def piece(t):
        return pl.BlockSpec((tm, tk), lambda i, k: (i, jnp.clip(k - offs[t], 0, counts[t] - 1)))

    anyspec = pl.BlockSpec(memory_space=pl.ANY)
    packs = [] if pack is None else [pack]
    pack_shapes = [] if pack is None else [SDS((N_DEV,) + pack.shape, f32), SDS((N_DEV, 8, d), f32)]
    scratch = [pltpu.VMEM((tm, d), f32)] + (_chip_exchange_scratch(nx) if nx else [])
    if npk:
        scratch += [pltpu.VMEM((8, d), f32)] + _bcast_scratch() + _bcast_scratch()
    outs = pl.pallas_call(
        body, name="inproj_bwd", grid=(ni, nk),
        in_specs=[piece(t) for t in range(npart)] + [
            pl.BlockSpec((tk, d), lambda i, k: (k, 0)),
            pl.BlockSpec((tm, d), lambda i, k: (i, 0)), pl.BlockSpec((1, d), lambda i, k: (0, 0)),
            pl.BlockSpec((tm, d), lambda i, k: (i, 0))] + [anyspec] * (nx + npk),
        out_specs=[pl.BlockSpec((tm, d), lambda i, k: (i, 0)), pl.BlockSpec((8, d), lambda i, k: (0, 0))]
        + [anyspec] * (nx + 2 * npk),
        out_shape=[SDS((s, d), f32), SDS((8, d), f32)] + [SDS(a.shape, a.dtype) for a in chip_sums] + pack_shapes,
        scratch_shapes=scratch,
        compiler_params=pltpu.CompilerParams(dimension_semantics=("arbitrary", "arbitrary")),
    )(*dparts, wt, x, nw, dres, *chip_sums, *packs)
    return outs[0], outs[1], outs[2:2 + nx], outs[2 + nx:]


def _matmul_tn(a_parts, b_parts, name):
    tile, tk = 1024, 1024
    s = a_parts[0].shape[0]
    nk = s // tk
    na, nb = len(a_parts), len(b_parts)
    offs_a, counts_a, ni = _col_blocks(a_parts, tile)
    offs_b, counts_b, nj = _col_blocks(b_parts, tile)

    def body(*refs):
        a_refs, b_refs, o_ref = refs[:na], refs[na:na + nb], refs[na + nb]
        i, j = pl.program_id(0), pl.program_id(1)

        @pl.when(pl.program_id(2) == 0)
        def _():
            o_ref[...] = jnp.zeros_like(o_ref)

        for ta in range(na):
            for tb in range(nb):
                in_a = jnp.logical_and(i >= offs_a[ta], i < offs_a[ta] + counts_a[ta])
                in_b = jnp.logical_and(j >= offs_b[tb], j < offs_b[tb] + counts_b[tb])

                @pl.when(jnp.logical_and(in_a, in_b))
                def _(ta=ta, tb=tb):
                    o_ref[...] += _tn(a_refs[ta][...], b_refs[tb][...])

    def spec(offs, counts, t, axis):
        def index(i, j, k):
            pos = (i, j)[axis]
            mine = jnp.logical_and(pos >= offs[t], pos < offs[t] + counts[t])
            return jnp.where(mine, k, 0), jnp.clip(pos - offs[t], 0, counts[t] - 1)
        return pl.BlockSpec((tk, tile), index)

    return pl.pallas_call(
        body, name=name, grid=(ni, nj, nk),
        in_specs=[spec(offs_a, counts_a, t, 0) for t in range(na)] + [spec(offs_b, counts_b, t, 1) for t in range(nb)],
        out_specs=pl.BlockSpec((tile, tile), lambda i, j, k: (i, j)),
        out_shape=SDS((ni * tile, nj * tile), f32),
        compiler_params=pltpu.CompilerParams(dimension_semantics=("parallel", "parallel", "arbitrary")),
    )(*a_parts, *b_parts)


def _adamw(w, g, m, v):
    m = ADAM_B1 * m + (1.0 - ADAM_B1) * g
    v = ADAM_B2 * v + (1.0 - ADAM_B2) * (g * g)
    m_hat = m / (1.0 - ADAM_B1 ** ADAM_STEP)
    v_hat = v / (1.0 - ADAM_B2 ** ADAM_STEP)
    delta = -ADAM_LR * (m_hat / (jnp.sqrt(v_hat) + ADAM_EPS) + ADAM_WD * w)
    return delta, m, v


def _sum_adamw(parts, w, m, v, name):
    r, c = w.shape
    tc = 256

    def body(p_ref, w_ref, m_ref, v_ref, g_ref, d_ref, nm_ref, nv_ref):
        g = p_ref[0].astype(f32)
        for q in range(1, 4):
            g = g + p_ref[q].astype(f32)
        g_ref[...] = g
        d_ref[...], nm_ref[...], nv_ref[...] = _adamw(w_ref[...], g, m_ref[...], v_ref[...])

    blk = pl.BlockSpec((r, tc), lambda i: (0, i))
    return pl.pallas_call(
        body, name=name, grid=(c // tc,),
        in_specs=[pl.BlockSpec((4, r, tc), lambda i: (0, 0, i)), blk, blk, blk],
        out_specs=[blk] * 4, out_shape=[SDS((r, c), f32)] * 4,
        compiler_params=pltpu.CompilerParams(dimension_semantics=("parallel",)),
    )(parts, w, m, v)


def _sum_small(parts, pre_blocks):
    def body(p_ref, b_ref, o_ref):
        t = p_ref[0]
        pre = b_ref[0]
        for j in range(1, N_DEV):
            t = t + p_ref[j]
            pre = pre + b_ref[j]
        o_ref[...] = t
        o_ref[5:6, 0:D_MODEL] = pre[0:1, :]
        row_h = _iota((D_SSM, LANES), 0) // HEAD_DIM
        fold = (row_h == _iota((D_SSM, LANES), 1)).astype(f32)
        lower = t[8:16, 0:LANES]
        folded = _nn_hi(t[8:16, 0:D_SSM], fold)
        loss = jnp.sum(t[11:12, 0:D_MODEL], axis=1, keepdims=True) * (0.5 / D_MODEL)
        row = _iota((8, LANES), 0)
        o_ref[8:16, 0:LANES] = jnp.where(row < 2, folded, jnp.where(row == 4, loss, lower))

    return pl.pallas_call(body, name="sum_small", out_shape=SDS((PACK_ROWS, PACK_W), f32),
                          in_specs=[pl.BlockSpec(memory_space=pltpu.VMEM)] * 2,
                          out_specs=pl.BlockSpec(memory_space=pltpu.VMEM))(parts, pre_blocks)


def _adamw_small(w, g, m, v):
    def body(w_ref, g_ref, m_ref, v_ref, d_ref, nm_ref, nv_ref):
        d_ref[...], nm_ref[...], nv_ref[...] = _adamw(w_ref[...], g_ref[...], m_ref[...], v_ref[...])

    vm = pl.BlockSpec(memory_space=pltpu.VMEM)
    return pl.pallas_call(body, name="adamw_small", out_shape=[SDS(w.shape, f32)] * 3,
                          in_specs=[vm] * 4, out_specs=[vm] * 3)(w, g, m, v)


def _pad_lanes(v, width):
    return jnp.pad(v, ((0, 0), (0, width - v.shape[1])))


def _local_step(x, tgt, norm_pre_w, wt, conv_w, conv_b, dt_bias, a_log, d_skip, ssm_norm_w, wo, norm_post_w, sharded):
    dtb16 = _pad_lanes(dt_bias, LANES)
    alog16 = _pad_lanes(a_log, LANES)
    alog_f = jnp.repeat(a_log, HEAD_DIM, axis=1)
    d_f = jnp.repeat(d_skip, HEAD_DIM, axis=1)

    shard_out = wo.shape[0]
    if sharded:
        proj, u, (g_out, g_cw) = _prenorm_inproj(x, norm_pre_w, wt, gather=(wo, conv_w))
        wo = g_out.reshape(N_DEV * shard_out, D_MODEL)
        conv_w = g_cw.transpose(1, 0, 2).reshape(4, D_CONV)
    else:
        proj, u, _ = _prenorm_inproj(x, norm_pre_w, wt)
    o, lb, mix_a, *qkv = _attn_fwd(proj)
    mix_s, y, states, cv = _ssd_fwd(proj, conv_w, conv_b, dtb16, alog16, alog_f, d_f, ssm_norm_w)
    dmix, dres, acc_post, dw_out = _outproj_loss(mix_a, mix_s, wo, x, tgt, norm_post_w)
    ssd_args = (proj, y, states, cv, dmix, conv_w, conv_b, dtb16, alog16, alog_f, d_f, ssm_norm_w)
    if sharded:
        dq, dk, dv, dg, got_out = _attn_bwd(proj, qkv, o, lb, dmix, swap=dw_out)
        chip_out = _chip_sum(dw_out, got_out, shard_out, "chip_sum_w_out")
        dzxd, g_conv, g_vec, g_dt, (parts_out,) = _ssd_bwd(*ssd_args, chip_sums=[chip_out])
    else:
        dq, dk, dv, dg = _attn_bwd(proj, qkv, o, lb, dmix)
        dzxd, g_conv, g_vec, g_dt, _ = _ssd_bwd(*ssd_args)
    dparts = [dq, dk, dv, dg, dzxd]

    def pack(g_pre_row):
        return jnp.concatenate(
            [g_conv[0:5], g_pre_row, _pad_lanes(g_vec[0:1], PACK_W), _pad_lanes(acc_post[1:2], PACK_W),
             _pad_lanes(g_vec[1:3], PACK_W), _pad_lanes(g_dt[0:1], PACK_W), _pad_lanes(acc_post[0:1], PACK_W),
             jnp.zeros((4, PACK_W), f32)], axis=0)

    if sharded:
        dw_in, got_in = _dw_in_swap(dparts, u)
        chip_in = _chip_sum(dw_in, got_in, D_IN_PROJ // N_DEV, "chip_sum_w_in")
        grad_x, _, (parts_in,), small = _inproj_bwd(dparts, wt, x, norm_pre_w, dres, [chip_in],
                                                    pack(jnp.zeros((1, PACK_W), f32)))
        return grad_x, (parts_in, parts_out), small
    dw_in = _matmul_tn(dparts, [u], "dw_in")
    grad_x, g_pre, _, _ = _inproj_bwd(dparts, wt, x, norm_pre_w, dres)
    return grad_x, (dw_in, dw_out), pack(_pad_lanes(g_pre[0:1], PACK_W))


def kernel(x, norm_pre_w, w_in, conv_w, conv_b, dt_bias, a_log, d_skip, ssm_norm_w, w_out, norm_post_w, loss_target, m_norm_pre_w, m_w_in, m_conv_w, m_conv_b, m_dt_bias, m_a_log, m_d_skip, m_ssm_norm_w, m_w_out, m_norm_post_w, v_norm_pre_w, v_w_in, v_conv_w, v_conv_b, v_dt_bias, v_a_log, v_d_skip, v_ssm_norm_w, v_w_out, v_norm_post_w):
    shard_cv = conv_w.shape[2]
    me = 4 * lax.axis_index("x") + 2 * lax.axis_index("y") + lax.axis_index("c")

    g_in, = _all_gather([w_in[0].T.astype(bf16)])
    wt = _assemble_wt(g_in)

    grad_x, (parts_in, parts_out), (parts_small, pre_blocks) = _local_step(
        x[0], loss_target[0], norm_pre_w, wt, conv_w[0], conv_b, dt_bias, a_log, d_skip, ssm_norm_w,
        w_out[0].astype(bf16), norm_post_w, sharded=True)

    g_w_in, d_w_in, nm_w_in, nv_w_in = (a.T for a in _sum_adamw(
        parts_in, w_in[0].T, m_w_in[0].T, v_w_in[0].T, "sum_adamw_w_in"))
    g_w_out, d_w_out, nm_w_out, nv_w_out = _sum_adamw(parts_out, w_out[0], m_w_out[0], v_w_out[0], "sum_adamw_w_out")
    tot = _sum_small(parts_small, pre_blocks)

    g_cw_all = tot[0:4]
    small_g = {
        "conv_w": lax.dynamic_slice(g_cw_all, (0, me * shard_cv), (4, shard_cv)),
        "conv_b": tot[4:5], "norm_pre_w": tot[5:6, :D_MODEL], "ssm_norm_w": tot[6:7, :D_SSM],
        "norm_post_w": tot[7:8, :D_MODEL], "a_log": tot[8:9, :16], "d_skip": tot[9:10, :16], "dt_bias": tot[10:11, :16],
    }
    loss = tot[12, 0]
    small_w = {"conv_w": (conv_w[0], m_conv_w[0], v_conv_w[0]), "conv_b": (conv_b, m_conv_b, v_conv_b),
               "norm_pre_w": (norm_pre_w, m_norm_pre_w, v_norm_pre_w), "ssm_norm_w": (ssm_norm_w, m_ssm_norm_w, v_ssm_norm_w),
               "norm_post_w": (norm_post_w, m_norm_post_w, v_norm_post_w), "a_log": (a_log, m_a_log, v_a_log),
               "d_skip": (d_skip, m_d_skip, v_d_skip), "dt_bias": (dt_bias, m_dt_bias, v_dt_bias)}
    names = list(small_w)
    sizes = [small_g[k].size for k in names]
    tot_size = sum(sizes)
    pad_to = -(-tot_size // 1024) * 1024

    def flat(arrs):
        v = jnp.concatenate([a.reshape(-1) for a in arrs])
        return jnp.pad(v, (0, pad_to - tot_size)).reshape(pad_to // LANES, LANES)

    fw = flat([small_w[k][0] for k in names])
    fg = flat([small_g[k] for k in names])
    fm = flat([small_w[k][1] for k in names])
    fv = jnp.pad(jnp.concatenate([small_w[k][2].reshape(-1) for k in names]), (0, pad_to - tot_size),
                 constant_values=1.0).reshape(pad_to // LANES, LANES)
    fd, fnm, fnv = _adamw_small(fw, fg, fm, fv)

    def unflat(f):
        out, off = {}, 0
        v = f.reshape(-1)
        for k, n in zip(names, sizes):
            out[k] = v[off:off + n].reshape(small_g[k].shape)
            off += n
        return out

    sd, snm, snv = unflat(fd), unflat(fnm), unflat(fnv)
    lead = lambda a: a[None]
    order = ["norm_pre_w", "w_in", "conv_w", "conv_b", "dt_bias", "a_log", "d_skip", "ssm_norm_w", "w_out", "norm_post_w"]
    grads = dict(small_g, w_in=g_w_in, w_out=g_w_out)
    deltas = dict(sd, w_in=d_w_in, w_out=d_w_out)
    new_m = dict(snm, w_in=nm_w_in, w_out=nm_w_out)
    new_v = dict(snv, w_in=nv_w_in, w_out=nv_w_out)

    def shaped(dct, k):
        a = dct[k]
        return lead(a) if k in ("w_in", "w_out", "conv_w") else a

    return (loss, grad_x[None], *[shaped(grads, k) for k in order], *[shaped(deltas, k) for k in order],
            *[shaped(new_m, k) for k in order], *[shaped(new_v, k) for k in order])
```

```python
import jax
import jax.numpy as jnp
from jax import lax
from jax.experimental import pallas as pl
from jax.experimental.pallas import tpu as pltpu

f32, bf16 = jnp.float32, jnp.bfloat16
SDS = jax.ShapeDtypeStruct
HIGHEST = lax.Precision.HIGHEST
MESH = pl.DeviceIdType.MESH

N_DEV = 8
D_MODEL = 1024
D_ATTN = 1024
D_SSM = 1024
HEAD_DIM = 64
N_PAIRS = 8
D_STATE = 128
N_GROUPS = 2
D_CONV = D_SSM + 2 * N_GROUPS * D_STATE
D_IN_PROJ = 4 * D_ATTN + D_SSM + D_CONV + 16
NP = 7168
CHUNK = 128
BLK = 128
DILATIONS = (1, 4, 16)
EPS = 1e-6
LANES = 128
COL_Z, COL_XS, COL_BC, COL_DT = 4096, 5120, 6144, 6656

ADAM_LR, ADAM_B1, ADAM_B2, ADAM_EPS, ADAM_WD, ADAM_STEP = 0.001, 0.9, 0.999, 1e-08, 0.01, 10

PACK_ROWS, PACK_W = 16, 1536


def _nt(a, b):
    return lax.dot_general(a, b, (((1,), (1,)), ((), ())), preferred_element_type=f32)


def _tn(a, b):
    return lax.dot_general(a, b, (((0,), (0,)), ((), ())), preferred_element_type=f32)


def _nn(a, b):
    return jnp.dot(a, b, preferred_element_type=f32)


def _nn_hi(a, b):
    return jnp.dot(a, b, precision=HIGHEST, preferred_element_type=f32)


def _sigmoid(x):
    return 1.0 / (1.0 + jnp.exp(-x))


def _softplus(x):
    return jnp.maximum(x, 0.0) + jnp.log1p(jnp.exp(-jnp.abs(x)))


def _iota(shape, dim):
    return lax.broadcasted_iota(jnp.int32, shape, dim)


def _my_pos():
    return lax.axis_index("x"), lax.axis_index("y"), lax.axis_index("c")


GATHER_SEMS = 9


def _gather_phases(ins, outs, send_sems, recv_sems, local_sems):
    n, ns = len(ins), GATHER_SEMS
    x, y, c = _my_pos()
    me, sibling = (x, y, c), (x, y, 1 - c)
    xn, yn, diag = (1 - x, y), (x, 1 - y), (1 - x, 1 - y)

    def slot(a, px, py, pc):
        return outs[a].at[4 * px + 2 * py + pc]

    def part(a, ref, h):
        width = ins[a].shape[-1]
        if width % (2 * LANES):
            return ref if h == 1 else None
        return ref.at[:, pl.ds(h * (width // 2), width // 2)]

    def copy(a, k, block, to, src=None, h=None):
        src_ref = slot(a, *block) if src is None else src
        dst_ref = slot(a, *block)
        if h is not None:
            src_ref, dst_ref = part(a, src_ref, h), part(a, dst_ref, h)
            if src_ref is None:
                return None
        return pltpu.make_async_remote_copy(
            src_ref=src_ref, dst_ref=dst_ref, send_sem=send_sems.at[ns * a + k], recv_sem=recv_sems.at[ns * a + k],
            device_id=to, device_id_type=MESH)

    def mine():
        return [pltpu.make_async_copy(ins[a], slot(a, *me), local_sems.at[a]) for a in range(n)]

    def own_sends(a):
        return [copy(a, 0, me, sibling, src=ins[a]), copy(a, 1, me, (*xn, c), src=ins[a]),
                copy(a, 2, me, (*yn, c), src=ins[a])]

    def neighbour_relays(a):
        return [copy(a, 4, (*xn, c), sibling), copy(a, 7, (*xn, c), (*yn, c), h=1),
                copy(a, 5, (*yn, c), sibling), copy(a, 8, (*yn, c), (*xn, c), h=0)]

    def diagonal_halves(a):
        return [copy(a, k, (*diag, c), me, h=h) for k, h in ((8, 0), (7, 1))]

    def start_all(cps):
        for cp in cps:
            if cp is not None:
                cp.start()

    def phase0():
        start_all(mine())
        for a in range(n):
            start_all(own_sends(a))

    def phase1():
        for a in range(n):
            copy(a, 1, (*xn, c), me).wait_recv()
            copy(a, 2, (*yn, c), me).wait_recv()
            start_all(neighbour_relays(a))

    def phase2():
        for a in range(n):
            for cp in diagonal_halves(a):
                if cp is not None:
                    cp.wait_recv()
            copy(a, 6, (*diag, c), sibling).start()

    def finish():
        for a in range(n):
            copy(a, 0, sibling, me).wait_recv()
            for j, chip in enumerate((xn, yn, diag)):
                copy(a, 4 + j, (*chip, 1 - c), me).wait_recv()
        for a in range(n):
            for cp in own_sends(a) + neighbour_relays(a) + [copy(a, 6, (*diag, c), sibling)]:
                if cp is not None:
                    cp.wait_send()
        for cp in mine():
            cp.wait()

    return phase0, phase1, phase2, finish


def _gather_scratch(n):
    return [pltpu.SemaphoreType.DMA((GATHER_SEMS * n,)), pltpu.SemaphoreType.DMA((GATHER_SEMS * n,)),
            pltpu.SemaphoreType.DMA((n,))]


def _all_gather(arrs):
    n = len(arrs)

    def body(*refs):
        for phase in _gather_phases(refs[:n], refs[n:2 * n], *refs[2 * n:]):
            phase()

    anyspec = pl.BlockSpec(memory_space=pl.ANY)
    return pl.pallas_call(
        body, name="weights_all_gather",
        out_shape=[SDS((N_DEV,) + a.shape, a.dtype) for a in arrs],
        in_specs=[anyspec] * n, out_specs=[anyspec] * n, scratch_shapes=_gather_scratch(n),
    )(*arrs)


def _dw_in_swap(a_parts, u):
    tile, tk = 1024, 1024
    s = u.shape[0]
    nk = s // tk
    na = len(a_parts)
    offs, counts, ni = _col_blocks(a_parts, tile)

    def body(*refs):
        a_refs, u_ref = refs[:na], refs[na]
        dw_ref, got_ref = refs[na + 1:na + 3]
        acc, stage, local_sems, send_sems, recv_sem = refs[na + 3:]
        i, k = pl.program_id(0), pl.program_id(1)
        x, y, c = _my_pos()
        par = i % 2

        def tile_copies(t, p):
            rows = pl.ds(pl.multiple_of(t * tile, tile), tile)
            loc = pltpu.make_async_copy(stage.at[p], dw_ref.at[rows], local_sems.at[p])
            rem = pltpu.make_async_remote_copy(
                src_ref=stage.at[p], dst_ref=got_ref.at[rows], send_sem=send_sems.at[p], recv_sem=recv_sem,
                device_id=(x, y, 1 - c), device_id_type=MESH)
            return loc, rem

        @pl.when(k == 0)
        def _():
            acc[...] = jnp.zeros((tile, tile), f32)

        for t in range(na):
            @pl.when(jnp.logical_and(i >= offs[t], i < offs[t] + counts[t]))
            def _(t=t):
                acc[...] += _tn(a_refs[t][...], u_ref[pl.ds(pl.multiple_of(k * tk, tk), tk), :])

        @pl.when(k == nk - 1)
        def _():
            @pl.when(i >= 2)
            def _():
                loc, rem = tile_copies(i - 2, par)
                loc.wait()
                rem.wait_send()
            stage[par] = acc[...]
            loc, rem = tile_copies(i, par)
            loc.start()
            rem.start()

        @pl.when(jnp.logical_and(i == ni - 1, k == nk - 1))
        def _():
            for t in (ni - 2, ni - 1):
                loc, rem = tile_copies(t, t % 2)
                loc.wait()
                rem.wait_send()
            pltpu.make_async_remote_copy(src_ref=dw_ref, dst_ref=got_ref, send_sem=send_sems.at[0], recv_sem=recv_sem,
                                         device_id=(x, y, c), device_id_type=MESH).wait_recv()

    def a_spec(t):
        def index(i, k):
            mine = jnp.logical_and(i >= offs[t], i < offs[t] + counts[t])
            return jnp.where(mine, k, 0), jnp.clip(i - offs[t], 0, counts[t] - 1)
        return pl.BlockSpec((tk, tile), index)

    anyspec = pl.BlockSpec(memory_space=pl.ANY)
    return pl.pallas_call(
        body, name="dw_in_swap", grid=(ni, nk),
        in_specs=[a_spec(t) for t in range(na)] + [pl.BlockSpec((s, tile), lambda i, k: (0, 0))],
        out_specs=[anyspec] * 2,
        out_shape=[SDS((ni * tile, tile), f32), SDS((ni * tile, tile), f32)],
        scratch_shapes=[pltpu.VMEM((tile, tile), f32), pltpu.VMEM((2, tile, tile), f32), pltpu.SemaphoreType.DMA((2,)),
                        pltpu.SemaphoreType.DMA((2,)), pltpu.SemaphoreType.DMA(())],
        compiler_params=pltpu.CompilerParams(dimension_semantics=("arbitrary", "arbitrary")),
    )(*a_parts, u)


def _chip_sum(mine, got, rows, name):
    r, cdim = mine.shape
    tc = LANES

    def body(m_ref, g_ref, s16_ref):
        c = lax.axis_index("c")
        for q in range(4):
            blk = pl.ds(rows * (2 * q + c), rows)
            s16_ref[q] = (m_ref[blk, :] + g_ref[blk, :]).astype(bf16)

    col = pl.BlockSpec((r, tc), lambda i: (0, i))
    return pl.pallas_call(
        body, name=name, grid=(cdim // tc,), in_specs=[col, col],
        out_specs=pl.BlockSpec((4, rows, tc), lambda i: (0, 0, i)), out_shape=SDS((4, rows, cdim), bf16),
        compiler_params=pltpu.CompilerParams(dimension_semantics=("parallel",)),
    )(mine, got)


def _assemble_wt(shards):
    nd, rows, cdim = shards.shape
    tc = 256

    def body(g_ref, o_ref):
        for j in range(nd):
            o_ref[pl.ds(rows * j, rows), :] = g_ref[j]
        o_ref[pl.ds(nd * rows, NP - nd * rows), :] = jnp.zeros((NP - nd * rows, tc), shards.dtype)

    return pl.pallas_call(
        body, name="assemble_w_in", grid=(cdim // tc,),
        in_specs=[pl.BlockSpec((nd, rows, tc), lambda i: (0, 0, i))],
        out_specs=pl.BlockSpec((NP, tc), lambda i: (0, i)), out_shape=SDS((NP, cdim), shards.dtype),
        compiler_params=pltpu.CompilerParams(dimension_semantics=("parallel",)),
    )(shards)


def _chip_exchange_copies(ins, outs, send_sems, recv_sems, local_sems):
    nb = len(ins)
    x, y, c = _my_pos()
    my_q = 2 * x + y
    mine = [pltpu.make_async_copy(ins[a].at[my_q], outs[a].at[my_q], local_sems.at[a]) for a in range(nb)]
    sends, recvs = [], []
    for k in range(1, 4):
        to, frm = (my_q + k) % 4, (my_q + 4 - k) % 4
        for a in range(nb):
            sems = dict(send_sem=send_sems.at[3 * a + k - 1], recv_sem=recv_sems.at[3 * a + k - 1], device_id_type=MESH)
            sends.append(pltpu.make_async_remote_copy(
                src_ref=ins[a].at[to], dst_ref=outs[a].at[my_q], device_id=(to // 2, to % 2, c), **sems))
            recvs.append(pltpu.make_async_remote_copy(
                src_ref=ins[a].at[frm], dst_ref=outs[a].at[frm], device_id=(x, y, c), **sems))
    return mine, sends, recvs


def _chip_exchange_scratch(nb):
    return [pltpu.SemaphoreType.DMA((3 * nb,)), pltpu.SemaphoreType.DMA((3 * nb,)), pltpu.SemaphoreType.DMA((nb,))]


def _prenorm_inproj(x, nw, wt, gather=()):
    s, d = x.shape
    npad = wt.shape[0]
    tm, tn = 1024, 1024
    ng = len(gather)
    ni, nj = s // tm, npad // tn

    def body(x_ref, nw_ref, w_ref, *refs):
        g_in, (proj_ref, u_ref), g_out, sems = refs[:ng], refs[ng:ng + 2], refs[ng + 2:2 * ng + 2], refs[2 * ng + 2:]
        i, j = pl.program_id(0), pl.program_id(1)
        if ng:
            phases = _gather_phases(g_in, g_out, *sems)
            for step, phase in enumerate(phases[:3]):
                @pl.when(jnp.logical_and(i == step, j == 0))
                def _(phase=phase):
                    phase()

        @pl.when(j == 0)
        def _():
            xv = x_ref[...]
            r = lax.rsqrt(jnp.mean(xv * xv, axis=-1, keepdims=True) + EPS)
            u_ref[...] = (xv * r * nw_ref[...]).astype(bf16)
        proj_ref[...] = _nt(u_ref[...], w_ref[pl.ds(pl.multiple_of(j * tn, tn), tn), :])

        if ng:
            @pl.when(jnp.logical_and(i == ni - 1, j == nj - 1))
            def _():
                phases[3]()

    anyspec = pl.BlockSpec(memory_space=pl.ANY)
    outs = pl.pallas_call(
        body, name="prenorm_inproj", grid=(ni, nj),
        in_specs=[pl.BlockSpec((tm, d), lambda i, j: (i, 0)), pl.BlockSpec((1, d), lambda i, j: (0, 0)),
                  pl.BlockSpec((npad, d), lambda i, j: (0, 0))] + [anyspec] * ng,
        out_specs=[pl.BlockSpec((tm, tn), lambda i, j: (i, j)), pl.BlockSpec((tm, d), lambda i, j: (i, 0))]
        + [anyspec] * ng,
        out_shape=[SDS((s, npad), f32), SDS((s, d), bf16)] + [SDS((N_DEV,) + a.shape, a.dtype) for a in gather],
        scratch_shapes=_gather_scratch(ng) if ng else [],
        compiler_params=pltpu.CompilerParams(dimension_semantics=("arbitrary", "arbitrary")),
    )(x, nw, wt, *gather)
    return outs[0], outs[1], outs[2:]


def _attn_consts():
    head0 = _iota((BLK, LANES), 1) < HEAD_DIM
    tri2 = (_iota((BLK, 2 * LANES), 1) % LANES) <= _iota((BLK, 2 * LANES), 0)
    ones2 = ((_iota((LANES, 2 * LANES), 0) < HEAD_DIM) == (_iota((LANES, 2 * LANES), 1) < LANES)).astype(bf16)
    rmat = ((_iota((2 * LANES, LANES), 0) < LANES) == (_iota((2 * LANES, LANES), 1) < HEAD_DIM)).astype(bf16)
    bones = ((_iota((LANES, LANES), 0) < HEAD_DIM) == (_iota((LANES, LANES), 1) < HEAD_DIM)).astype(bf16)
    return head0, tri2, ones2, rmat, bones


def _stack_heads(x16, head0):
    zero = jnp.zeros_like(x16)
    return jnp.concatenate([jnp.where(head0, x16, zero), jnp.where(head0, zero, x16)], axis=0)


def _bf16_terms(x, terms):
    out = []
    for _ in range(terms):
        t = x.astype(bf16)
        out.append(t)
        x = x - t.astype(f32)
    return out


def _dot_01(x, w16, terms):
    return _nn(jnp.concatenate(_bf16_terms(x, terms), axis=1), jnp.concatenate([w16] * terms, axis=0))


def _split_dot_sum(x, w16):
    hi, lo = _bf16_terms(x, 2)
    return _nn(hi, w16) + _nn(lo, w16)


def _dot_01_left(w16, x, terms):
    return _nn(jnp.concatenate([w16] * terms, axis=1), jnp.concatenate(_bf16_terms(x, terms), axis=0))


def _quarter_rows(i, q):
    return pl.ds(pl.multiple_of((i // 2) * 2048 + q * 512 + (i % 2) * 256, 256), 256)


def _token_rows(i, q):
    return pl.ds(i * 1024 + q, 256, stride=4)


def _quarter_block(i, d, nb):
    assert isinstance(i, int)
    r, blk = i // nb, i % nb
    if d == 1:
        runs = [pl.ds((blk // 16) * 2048 + q * 512 + (blk % 16) * 32, 32) for q in range(4)]
    elif d == 4:
        runs = [pl.ds((blk // 4) * 2048 + r * 512 + (blk % 4) * BLK, BLK)]
    else:
        runs = [pl.ds(blk * 2048 + (r % 4) * 512 + r // 4, BLK, stride=4)]
    return runs, blk > 0


def _if_prev(has_prev, x, fill):
    return x if has_prev else jnp.full_like(x, fill)


def _quarter_mask():
    order = lambda n: 4 * (n % 32) + n // 32
    return order(_iota((BLK, 2 * LANES), 1) % LANES) <= order(_iota((BLK, 2 * LANES), 0))


def _load_runs(ref, runs):
    parts = [ref[run, :] for run in runs]
    return parts[0] if len(parts) == 1 else jnp.concatenate(parts, axis=0)


def _store_runs(ref, runs, val):
    n = BLK // len(runs)
    for t, run in enumerate(runs):
        ref[run, :] = val[t * n:(t + 1) * n]


def _add_runs(ref, runs, val):
    n = BLK // len(runs)
    for t, run in enumerate(runs):
        ref[run, :] += val[t * n:(t + 1) * n]


def _attn_fwd(proj):
    s = proj.shape[0]
    n_it = s // BLK

    def body(q_in, k_in, v_in, g_ref, o_ref, l_ref, mix_ref, q_ref, k_ref, v_ref, op0, op1, op2, lp0, lp1, lp2,
             s_a, s_b, sd_a, sd_b, p_a, p_b, m_a, m_b, pd_a, pd_b, k_a, k_b, v_a, v_b, stage):
        op_refs, lp_refs = (op0, op1, op2), (lp0, lp1, lp2)
        head0, tri2_t, ones2, rmat, _ = _attn_consts()
        tri2_q = _quarter_mask()

        def reorder(i, carry):
            for src, dst, scale in ((q_in, q_ref, 0.125), (k_in, k_ref, 1.0), (v_in, v_ref, 1.0)):
                for q in range(4):
                    t = src[_token_rows(i, q), :]
                    dst[_quarter_rows(i, q), :] = t if scale == 1.0 else t * scale
            return carry

        lax.fori_loop(0, s // 1024, reorder, 0)
        score_bufs, prob_bufs = ((s_a, sd_a), (s_b, sd_b)), ((p_a, m_a, pd_a), (p_b, m_b, pd_b))
        k_bufs, v_bufs = (k_a, k_b), (v_a, v_b)
        for buf in k_bufs + v_bufs:
            buf[...] = jnp.zeros_like(buf)

        def unstack(st16):
            return st16[:BLK] + st16[BLK:]

        def scores(i, par, d, nb):
            rows, has_prev = _quarter_block(i, d, nb)
            tri2 = tri2_q if d == 1 else tri2_t
            s_buf, sd_buf = score_bufs[par]
            qs = _load_runs(q_ref, rows)
            qs16 = qs.astype(bf16)
            kst_c = _stack_heads(_load_runs(k_ref, rows).astype(bf16), head0)
            kst_p = k_bufs[1 - par][...]
            k_bufs[par][...] = kst_c
            sc = _nt(qs16, kst_c)
            sp = _nt(qs16, kst_p)
            s_buf[...] = jnp.where(tri2, sc, _if_prev(has_prev, sp, -jnp.inf))
            sd = _nn((qs * unstack(kst_p).astype(f32)).astype(bf16), ones2)
            sd_buf[...] = _if_prev(has_prev, sd, -jnp.inf)

        def softmax(bufs_in, bufs_out):
            s_buf, sd_buf = bufs_in
            p_buf, m_buf, pd_buf = bufs_out
            sc, sd2 = s_buf[...], sd_buf[...]
            m0 = jnp.max(sc[:, :LANES], axis=1, keepdims=True)
            m1 = jnp.max(sc[:, LANES:], axis=1, keepdims=True)
            m2 = jnp.concatenate([jnp.broadcast_to(m0, (BLK, LANES)), jnp.broadcast_to(m1, (BLK, LANES))], axis=1)
            m2 = jnp.maximum(m2, sd2)
            p_buf[...] = jnp.exp(sc - m2).astype(bf16)
            m_pair = jnp.where(head0, m2[:, :LANES], m2[:, LANES:])
            m_buf[...] = m_pair
            pd_buf[...] = jnp.exp(jnp.where(head0, sd2[:, :LANES], sd2[:, LANES:]) - m_pair)

        def output(i, par, d, nb, p):
            rows, has_prev = _quarter_block(i, d, nb)
            tri2 = tri2_q if d == 1 else tri2_t
            p_buf, m_buf, pd_buf = prob_bufs[par]
            vst_c = _stack_heads(_load_runs(v_ref, rows).astype(bf16), head0)
            vst_p = v_bufs[1 - par][...]
            v_bufs[par][...] = vst_c
            pt16, pd = p_buf[...], pd_buf[...]
            zero = jnp.zeros_like(pt16)
            o = _nn(jnp.where(tri2, pt16, zero), vst_c)
            if has_prev:
                o = o + _nn(jnp.where(tri2, zero, pt16), vst_p) + pd * unstack(vst_p).astype(f32)
            l = _nn(pt16, rmat) + pd
            _store_runs(op_refs[p], rows, o / l)
            _store_runs(lp_refs[p], rows, m_buf[...] + jnp.log(l))

        for p, d in enumerate(DILATIONS):
            nb = s // (BLK * d)
            scores(0, 0, d, nb)
            scores(1, 1, d, nb)
            softmax(score_bufs[0], prob_bufs[0])

            for t in range(2, n_it):
                par = t % 2
                scores(t, par, d, nb)
                output(t - 2, par, d, nb, p)
                softmax(score_bufs[1 - par], prob_bufs[1 - par])
            output(n_it - 2, 0, d, nb, p)
            softmax(score_bufs[1], prob_bufs[1])
            output(n_it - 1, 1, d, nb, p)

        def merge(i, carry):
            for q in range(4):
                rows, tokens = _quarter_rows(i, q), _token_rows(i, q)
                l0, l1, l2 = lp0[rows, :], lp1[rows, :], lp2[rows, :]
                m = jnp.maximum(jnp.maximum(l0, l1), l2)
                e0, e1, e2 = jnp.exp(l0 - m), jnp.exp(l1 - m), jnp.exp(l2 - m)
                z = e0 + e1 + e2
                o = (e0 * op0[rows, :] + e1 * op1[rows, :] + e2 * op2[rows, :]) / z
                o_ref[tokens, :] = o
                l_ref[rows, :] = m + jnp.log(z)
                g = g_ref[tokens, :]
                stage[pl.ds(q, 256, stride=4), :] = o * (g * _sigmoid(g))
            mix_ref[pl.ds(pl.multiple_of(i * 1024, 1024), 1024), :] = stage[...].astype(bf16)
            return carry

        lax.fori_loop(0, s // 1024, merge, 0)

    col = lambda base: pl.BlockSpec((s, LANES), lambda h: (0, base + h))
    return pl.pallas_call(
        body, name="attn_fwd", grid=(N_PAIRS,),
        in_specs=[col(0), col(8), col(16), col(24)],
        out_specs=[col(0)] * 6,
        out_shape=[SDS((s, D_ATTN), f32), SDS((s, D_ATTN), f32), SDS((s, D_ATTN), bf16)] + [SDS((s, D_ATTN), f32)] * 3,
        scratch_shapes=[pltpu.VMEM((s, LANES), f32)] * 6 + [pltpu.VMEM((BLK, 2 * LANES), f32)] * 4
        + [pltpu.VMEM((BLK, 2 * LANES), bf16)] * 2 + [pltpu.VMEM((BLK, LANES), f32)] * 4
        + [pltpu.VMEM((2 * BLK, LANES), bf16)] * 4 + [pltpu.VMEM((1024, LANES), f32)],
        compiler_params=pltpu.CompilerParams(dimension_semantics=("parallel",)),
    )(proj, proj, proj, proj)


def _expand_mat():
    colv = _iota((LANES, 2 * D_SSM), 1)
    head = 2 * ((colv % D_SSM) // LANES) + colv // D_SSM
    return (_iota((LANES, 2 * D_SSM), 0) == head).astype(bf16)


def _fold_mat():
    return (_iota((D_SSM, LANES), 0) // HEAD_DIM == _iota((D_SSM, LANES), 1)).astype(bf16)


def _conv(xs_ref, bc_ref, xs_tail, bc_tail, cw_ref, cb_ref, xpad, first):
    keep = jnp.where(first, 0.0, 1.0)
    xpad[0:8, 0:D_SSM] = xs_tail[...] * keep
    xpad[0:8, D_SSM:D_CONV] = bc_tail[...] * keep
    xpad[8:8 + CHUNK, 0:D_SSM] = xs_ref[...]
    xpad[8:8 + CHUNK, D_SSM:D_CONV] = bc_ref[...]
    xp = xpad[...]
    cv = cb_ref[...] + cw_ref[3:4, :] * xp[8:8 + CHUNK]
    for j in range(3):
        cv = cv + cw_ref[j:j + 1, :] * pltpu.roll(xp, 3 - j, 0)[8:8 + CHUNK]
    return cv


def _decay_terms(dt_ref, dtb_ref, alog16_ref, emat_ref):
    pre = dt_ref[...] + dtb_ref[...]
    dt16 = _softplus(pre)
    a16 = -jnp.exp(alog16_ref[...])
    sub, lane = _iota((CHUNK, CHUNK), 0), _iota((CHUNK, CHUNK), 1)
    tri = (sub >= lane).astype(f32)
    al16 = _nn_hi(tri, dt16 * a16)
    al_t = al16.T
    emat = emat_ref[...]
    dt_x = _dot_01(dt16, emat, 3)
    al_x = _dot_01(al16, emat, 3)
    lane_w = _iota((CHUNK, D_SSM), 1)
    even = (lane_w % LANES) < HEAD_DIM
    dt_f = jnp.where(even, dt_x[:, :D_SSM], dt_x[:, D_SSM:])
    al_f = jnp.where(even, al_x[:, :D_SSM], al_x[:, D_SSM:])
    return pre, dt_f, al_f, al_x, al_t


def _decay_mat(al_x, al_t, pair, h):
    sub, lane = _iota((CHUNK, CHUNK), 0), _iota((CHUNK, CHUNK), 1)
    col = al_x[:, h * D_SSM + pair * LANES: h * D_SSM + (pair + 1) * LANES]
    row = al_t[2 * pair + h: 2 * pair + h + 1, :]
    return jnp.exp(jnp.where(sub >= lane, col - row, -jnp.inf))


def _ssd_in_specs(order, rows=CHUNK):
    blk = lambda w, cb: pl.BlockSpec((rows, w), lambda i: (order(i), cb))
    tail = lambda w, cb: pl.BlockSpec((8, w), lambda i: (jnp.maximum((rows // 8) * order(i) - 1, 0), cb))
    return [blk(D_SSM, COL_XS // D_SSM), blk(512, COL_BC // 512), tail(D_SSM, COL_XS // D_SSM),
            tail(512, COL_BC // 512), blk(LANES, COL_DT // LANES), blk(D_SSM, COL_Z // D_SSM)]


def _full(shape):
    return pl.BlockSpec(shape, lambda i: (0,) * len(shape))


def _ssd_fwd(proj, conv_w, conv_b, dtb16, alog16, alog_f, d_f, nw):
    s = proj.shape[0]
    nc = s // CHUNK

    per_step = 2

    def body(xs_ref, bc_ref, xs_tail, bc_tail, dt_ref, z_ref, cw_ref, cb_ref, dtb_ref, alog16_ref, alogf_ref,
             df_ref, nw_ref, mix_ref, y_ref, st_ref, cv_ref, h_scr, xpad_a, xpad_b, y_a, y_b, emat_ref):
        c = pl.program_id(0)

        @pl.when(c == 0)
        def _():
            h_scr[...] = jnp.zeros_like(h_scr)
            emat_ref[...] = _expand_mat()

        for t, (xpad, y_scr) in enumerate(((xpad_a, y_a), (xpad_b, y_b))):
            rows, before = pl.ds(t * CHUNK, CHUNK), pl.ds(max(t * CHUNK - 8, 0), 8)
            chunk(xs_ref.at[rows], bc_ref.at[rows], xs_tail if t == 0 else xs_ref.at[before],
                  bc_tail if t == 0 else bc_ref.at[before], dt_ref.at[rows], z_ref.at[rows], cw_ref, cb_ref, dtb_ref,
                  alog16_ref, df_ref, nw_ref, mix_ref.at[rows], y_ref.at[rows], st_ref.at[t], cv_ref.at[rows],
                  h_scr, xpad, y_scr, emat_ref, (c == 0) if t == 0 else False)

    def chunk(xs_ref, bc_ref, xs_tail, bc_tail, dt_ref, z_ref, cw_ref, cb_ref, dtb_ref, alog16_ref,
              df_ref, nw_ref, mix_ref, y_ref, st_ref, cv_ref, h_scr, xpad, y_scr, emat_ref, first):
        cv = _conv(xs_ref, bc_ref, xs_tail, bc_tail, cw_ref, cb_ref, xpad, first)
        cv_ref[...] = cv
        xbc = cv * _sigmoid(cv)
        _, dt_f, al_f, al_x, al_t = _decay_terms(dt_ref, dtb_ref, alog16_ref, emat_ref)
        head0 = _iota((CHUNK, LANES), 1) < HEAD_DIM
        st_ref[...] = h_scr[...]
        for g in range(N_GROUPS):
            bm = xbc[:, D_SSM + g * D_STATE: D_SSM + (g + 1) * D_STATE].astype(bf16)
            cm = xbc[:, D_SSM + (N_GROUPS + g) * D_STATE: D_SSM + (N_GROUPS + g + 1) * D_STATE].astype(bf16)
            gmat = _nt(cm, bm)
            for pair in range(4 * g, 4 * g + 4):
                sl = slice(pair * LANES, (pair + 1) * LANES)
                xp, dtp, alp = xbc[:, sl], dt_f[:, sl], al_f[:, sl]
                xdt = xp * dtp
                xdt16 = xdt.astype(bf16)
                al_last = alp[CHUNK - 1:CHUNK, :]
                hp = h_scr[:, sl]
                y_off = jnp.exp(alp) * _nn(cm, hp.astype(bf16))
                yd = [_nn((gmat * _decay_mat(al_x, al_t, pair, h)).astype(bf16), xdt16) for h in range(2)]
                y_scr[:, sl] = jnp.where(head0, yd[0], yd[1]) + y_off + df_ref[:, sl] * xp
                st = _tn(bm, (jnp.exp(al_last - alp) * xdt).astype(bf16))
                h_scr[:, sl] = jnp.exp(al_last) * hp + st
        y = y_scr[...]
        y_ref[...] = y
        z = z_ref[...]
        yz = y * (z * _sigmoid(z))
        gw = D_SSM // N_GROUPS
        for g in range(N_GROUPS):
            part = yz[:, g * gw:(g + 1) * gw]
            r = lax.rsqrt(jnp.mean(part * part, axis=-1, keepdims=True) + EPS)
            mix_ref[:, g * gw:(g + 1) * gw] = (part * r * nw_ref[:, g * gw:(g + 1) * gw]).astype(bf16)

    order = lambda i: i
    step_rows = per_step * CHUNK
    row = lambda w: pl.BlockSpec((step_rows, w), lambda i: (i, 0))
    return pl.pallas_call(
        body, name="ssd_fwd", grid=(nc // per_step,),
        in_specs=_ssd_in_specs(order, step_rows) + [_full((4, D_CONV)), _full((1, D_CONV)), _full((1, LANES)),
                                                    _full((1, LANES)), _full((1, D_SSM)), _full((1, D_SSM)),
                                                    _full((1, D_SSM))],
        out_specs=[row(D_SSM), row(D_SSM), pl.BlockSpec((per_step, D_STATE, D_SSM), lambda i: (i, 0, 0)), row(D_CONV)],
        out_shape=[SDS((s, D_SSM), bf16), SDS((s, D_SSM), f32), SDS((nc, D_STATE, D_SSM), f32),
                   SDS((s, D_CONV), f32)],
        scratch_shapes=[pltpu.VMEM((D_STATE, D_SSM), f32)] + [pltpu.VMEM((8 + CHUNK, D_CONV), f32)] * 2
        + [pltpu.VMEM((CHUNK, D_SSM), f32)] * 2 + [pltpu.VMEM((LANES, 2 * D_SSM), bf16)],
        compiler_params=pltpu.CompilerParams(dimension_semantics=("arbitrary",)),
    )(proj, proj, proj, proj, proj, proj, conv_w, conv_b, dtb16, alog16, alog_f, d_f, nw)


def _outproj_loss(mix_a, mix_s, wo, x, tgt, npw):
    s, d = x.shape
    tm = 512

    def body(ma_ref, ms_ref, wo_ref, x_ref, t_ref, npw_ref, dmix_ref, dres_ref, acc_ref, dwo_ref):
        @pl.when(pl.program_id(0) == 0)
        def _():
            acc_ref[...] = jnp.zeros_like(acc_ref)
            dwo_ref[...] = jnp.zeros_like(dwo_ref)

        out = _nn(ma_ref[...], wo_ref[0:D_ATTN, :]) + _nn(ms_ref[...], wo_ref[D_ATTN:, :])
        r = lax.rsqrt(jnp.mean(out * out, axis=-1, keepdims=True) + EPS)
        on = out * r
        diff = x_ref[...] + on * npw_ref[...] - t_ref[...]
        dres = diff * (1.0 / d)
        dres_ref[...] = dres
        acc_ref[0:1, :] += jnp.sum(diff * diff, axis=0, keepdims=True)
        acc_ref[1:2, :] += jnp.sum(dres * on, axis=0, keepdims=True)
        dn = dres * npw_ref[...]
        dout = (r * (dn - on * jnp.mean(dn * on, axis=-1, keepdims=True))).astype(bf16)
        dmix_ref[...] = _nt(dout, wo_ref[...])
        dwo_ref[0:D_ATTN, :] += _tn(ma_ref[...], dout)
        dwo_ref[D_ATTN:, :] += _tn(ms_ref[...], dout)

    row = lambda w: pl.BlockSpec((tm, w), lambda i: (i, 0))
    return pl.pallas_call(
        body, name="outproj_loss", grid=(s // tm,),
        in_specs=[row(D_ATTN), row(D_SSM), _full((D_ATTN + D_SSM, d)), row(d), row(d), _full((1, d))],
        out_specs=[row(D_ATTN + D_SSM), row(d), _full((8, d)), _full((D_ATTN + D_SSM, d))],
        out_shape=[SDS((s, D_ATTN + D_SSM), f32), SDS((s, d), f32), SDS((8, d), f32), SDS((D_ATTN + D_SSM, d), f32)],
        compiler_params=pltpu.CompilerParams(dimension_semantics=("arbitrary",)),
    )(mix_a, mix_s, wo, x, tgt, npw)


def _attn_bwd(proj, qkv, o, lb, dmix, swap=None):
    s = proj.shape[0]
    n_it = s // BLK

    nsw = 0 if swap is None else 1

    def body(*refs):
        q_ref, k_ref, v_ref, g_ref, o_ref, l_ref, dm_ref = refs[:7]
        swap_in = refs[7:7 + nsw]
        dq_ref, dk_ref, dv_ref, dg_ref = refs[7 + nsw:11 + nsw]
        swap_out = refs[11 + nsw:11 + 2 * nsw]
        dq_acc, dk_acc, dv_acc, do_scr, dl_scr = refs[11 + 2 * nsw:16 + 2 * nsw]
        bufs = refs[16 + 2 * nsw:44 + 2 * nsw]
        stage_a, stage_b = refs[44 + 2 * nsw:46 + 2 * nsw]
        swap_sems = refs[46 + 2 * nsw:]
        head0, tri2_t, _, _, bones = _attn_consts()
        tri2_q = _quarter_mask()

        if nsw:
            x, y, c = _my_pos()
            swap_copy = pltpu.make_async_remote_copy(
                src_ref=swap_in[0], dst_ref=swap_out[0], send_sem=swap_sems[0], recv_sem=swap_sems[1],
                device_id=(x, y, 1 - c), device_id_type=MESH)

            @pl.when(pl.program_id(0) == 0)
            def _():
                swap_copy.start()

        quarter_rows, load, add = _quarter_rows, _load_runs, _add_runs

        def pro(i, carry):
            for t in range(4):
                rows = pl.ds(pl.multiple_of(i * 1024 + t * 256, 256), 256)
                g = g_ref[rows, :]
                sg = _sigmoid(g)
                dmx = dm_ref[rows, :]
                ov = o_ref[rows, :]
                dg_ref[rows, :] = (dmx * ov * (sg * (1.0 + g * (1.0 - sg)))).astype(bf16)
                do = dmx * (g * sg)
                stage_a[t * 256:(t + 1) * 256, :] = do
                stage_b[t * 256:(t + 1) * 256, :] = _split_dot_sum(do * ov, bones)
            z = jnp.zeros((256, LANES), f32)
            for q in range(4):
                rows = quarter_rows(i, q)
                do_scr[rows, :] = stage_a[pl.ds(q, 256, stride=4), :]
                dl_scr[rows, :] = stage_b[pl.ds(q, 256, stride=4), :]
                dq_acc[rows, :] = z
                dk_acc[rows, :] = z
                dv_acc[rows, :] = z
            return carry

        lax.fori_loop(0, s // 1024, pro, 0)

        def per_head(t):
            return jnp.concatenate([t[:, :LANES], t[:, LANES:]], axis=0)

        def both_heads(t):
            tr = pltpu.roll(t, HEAD_DIM, 1)
            return jnp.concatenate([jnp.where(head0, t, tr), jnp.where(head0, tr, t)], axis=1)

        mm_bufs = ((bufs[0], bufs[1], bufs[2], bufs[3]), (bufs[4], bufs[5], bufs[6], bufs[7]))
        ds_bufs = ((bufs[8], bufs[9], bufs[10], bufs[11]), (bufs[12], bufs[13], bufs[14], bufs[15]))
        op_bufs = ((bufs[16], bufs[17], bufs[18], bufs[19]), (bufs[20], bufs[21], bufs[22], bufs[23]))
        vc_bufs, carry_k, carry_v = (bufs[24], bufs[25]), bufs[26], bufs[27]
        for buf in (op_bufs[0][0], op_bufs[1][0]) + vc_bufs:
            buf[...] = jnp.zeros_like(buf)

        def block_rows(i, d, nb):
            rows, has_prev = _quarter_block(i, d, nb)
            return rows, rows, has_prev

        def unstack(st16):
            return st16[:BLK] + st16[BLK:]

        def products(i, par, d, nb):
            src, scr, has_prev = block_rows(i, d, nb)
            tri2 = tri2_q if d == 1 else tri2_t
            s_buf, dp_buf, sd_buf, dpd_buf = mm_bufs[par]
            kc_buf, kp_buf, q_buf, do_buf = op_bufs[par]
            qs = load(q_ref, src)
            do = load(do_scr, scr)
            qs16, do16 = qs.astype(bf16), do.astype(bf16)
            kst_c = _stack_heads(load(k_ref, src).astype(bf16), head0)
            vst_c = _stack_heads(load(v_ref, src).astype(bf16), head0)
            kst_p, vst_p = op_bufs[1 - par][0][...], vc_bufs[1 - par][...]
            kc_buf[...] = kst_c
            kp_buf[...] = kst_p
            vc_bufs[par][...] = vst_c
            q_buf[...] = qs16
            do_buf[...] = do16
            s_buf[...] = jnp.where(tri2, _nt(qs16, kst_c), _if_prev(has_prev, _nt(qs16, kst_p), -jnp.inf))
            dp_buf[...] = jnp.where(tri2, _nt(do16, vst_c), _if_prev(has_prev, _nt(do16, vst_p), 0.0))
            sd_buf[...] = _nn((qs * unstack(kst_p).astype(f32)).astype(bf16), bones)
            dpd_buf[...] = _if_prev(has_prev, _nn((do * unstack(vst_p).astype(f32)).astype(bf16), bones), 0.0)

        def softmax_grad(i, par, d, nb):
            src, scr, has_prev = block_rows(i, d, nb)
            s_buf, dp_buf, sd_buf, dpd_buf = mm_bufs[par]
            p_buf, ds_buf, pd_buf, dsd_buf = ds_bufs[par]
            lse = load(l_ref, src)
            dl = load(dl_scr, scr)
            pt = jnp.exp(s_buf[...] - both_heads(lse))
            ds_buf[...] = (pt * (dp_buf[...] - both_heads(dl))).astype(bf16)
            p_buf[...] = pt.astype(bf16)
            pd = _if_prev(has_prev, jnp.exp(sd_buf[...] - lse), 0.0)
            pd_buf[...] = pd
            dsd_buf[...] = pd * (dpd_buf[...] - dl)

        def accumulate(i, par, d, nb):
            _, rows, has_prev = block_rows(i, d, nb)
            _, before, _ = block_rows(max(i - 1, 0), d, nb)
            tri2 = tri2_q if d == 1 else tri2_t
            p_buf, ds_buf, pd_buf, dsd_buf = ds_bufs[par]
            kc_buf, kp_buf, q_buf, do_buf = op_bufs[par]
            pt16, ds16, pd, dsd = p_buf[...], ds_buf[...], pd_buf[...], dsd_buf[...]
            zero = jnp.zeros_like(pt16)
            dsc, dsp = jnp.where(tri2, ds16, zero), jnp.where(tri2, zero, ds16)
            pc, pp = jnp.where(tri2, pt16, zero), jnp.where(tri2, zero, pt16)
            kst_c, kst_p, q16, do16 = kc_buf[...], kp_buf[...], q_buf[...], do_buf[...]
            qst, dost = _stack_heads(q16, head0), _stack_heads(do16, head0)
            if not has_prev:
                add(dq_acc, rows, _nn(dsc, kst_c))
                add(dk_acc, before, carry_k[...])
                add(dv_acc, before, carry_v[...])
                carry_k[...] = _tn(per_head(dsc), qst)
                carry_v[...] = _tn(per_head(pc), dost)
                return
            add(dq_acc, rows, _nn(dsc, kst_c) + _nn(dsp, kst_p) + dsd * unstack(kst_p).astype(f32))
            dk2 = _tn(jnp.concatenate([per_head(dsc), per_head(dsp)], axis=1), qst)
            dv2 = _tn(jnp.concatenate([per_head(pc), per_head(pp)], axis=1), dost)
            add(dk_acc, before, carry_k[...] + dk2[BLK:] + dsd * q16.astype(f32))
            add(dv_acc, before, carry_v[...] + dv2[BLK:] + pd * do16.astype(f32))
            carry_k[...] = dk2[:BLK]
            carry_v[...] = dv2[:BLK]

        for d in DILATIONS:
            nb = s // (BLK * d)
            carry_k[...] = jnp.zeros_like(carry_k)
            carry_v[...] = jnp.zeros_like(carry_v)
            products(0, 0, d, nb)
            products(1, 1, d, nb)
            softmax_grad(0, 0, d, nb)

            for t in range(2, n_it):
                par = t % 2
                accumulate(t - 2, par, d, nb)
                products(t, par, d, nb)
                softmax_grad(t - 1, 1 - par, d, nb)
            accumulate(n_it - 2, 0, d, nb)
            softmax_grad(n_it - 1, 1, d, nb)
            accumulate(n_it - 1, 1, d, nb)
            _, last, _ = block_rows(n_it - 1, d, nb)
            add(dk_acc, last, carry_k[...])
            add(dv_acc, last, carry_v[...])

        def epi(i, carry):
            rows = pl.ds(pl.multiple_of(i * 1024, 1024), 1024)
            for acc, out, stage, scale in ((dq_acc, dq_ref, stage_a, 0.125), (dk_acc, dk_ref, stage_b, 1.0),
                                           (dv_acc, dv_ref, stage_a, 1.0)):
                for q in range(4):
                    stage[pl.ds(q, 256, stride=4), :] = acc[quarter_rows(i, q), :]
                out[rows, :] = (stage[...] if scale == 1.0 else stage[...] * scale).astype(bf16)
            return carry

        lax.fori_loop(0, s // 1024, epi, 0)

        if nsw:
            @pl.when(pl.program_id(0) == N_PAIRS - 1)
            def _():
                swap_copy.wait_send()
                swap_copy.wait_recv()

    col = lambda base: pl.BlockSpec((s, LANES), lambda h: (0, base + h))
    anyspec = pl.BlockSpec(memory_space=pl.ANY)
    swaps = [] if swap is None else [swap]
    outs = pl.pallas_call(
        body, name="attn_bwd", grid=(N_PAIRS,),
        in_specs=[col(0), col(0), col(0), col(24), col(0), col(0), col(0)] + [anyspec] * nsw,
        out_specs=[col(0)] * 4 + [anyspec] * nsw,
        out_shape=[SDS((s, D_ATTN), bf16)] * 4 + [SDS(a.shape, a.dtype) for a in swaps],
        scratch_shapes=[pltpu.VMEM((s, LANES), f32)] * 5
        + [pltpu.VMEM((BLK, 2 * LANES), f32)] * 2 + [pltpu.VMEM((BLK, LANES), f32)] * 2
        + [pltpu.VMEM((BLK, 2 * LANES), f32)] * 2 + [pltpu.VMEM((BLK, LANES), f32)] * 2
        + [pltpu.VMEM((BLK, 2 * LANES), bf16)] * 2 + [pltpu.VMEM((BLK, LANES), f32)] * 2
        + [pltpu.VMEM((BLK, 2 * LANES), bf16)] * 2 + [pltpu.VMEM((BLK, LANES), f32)] * 2
        + [pltpu.VMEM((2 * BLK, LANES), bf16)] * 2 + [pltpu.VMEM((BLK, LANES), bf16)] * 2
        + [pltpu.VMEM((2 * BLK, LANES), bf16)] * 2 + [pltpu.VMEM((BLK, LANES), bf16)] * 2
        + [pltpu.VMEM((2 * BLK, LANES), bf16)] * 2 + [pltpu.VMEM((BLK, LANES), f32)] * 2
        + [pltpu.VMEM((1024, LANES), f32)] * 2
        + [pltpu.SemaphoreType.DMA(())] * (2 * nsw),
        compiler_params=pltpu.CompilerParams(dimension_semantics=("arbitrary",)),
    )(*qkv, proj, o, lb, dmix, *swaps)
    return outs


def _ssd_bwd(proj, y, states, cv, dmix, conv_w, conv_b, dtb16, alog16, alog_f, d_f, nw, chip_sums=()):
    s = proj.shape[0]
    nc = s // CHUNK
    gw = D_SSM // N_GROUPS
    nx = len(chip_sums)

    def body(*refs):
        (xs_ref, bc_ref, _, _, dt_ref, z_ref, y_ref, st_ref, dm_ref, cw_ref, cb_ref, dtb_ref,
         alog16_ref, alogf_ref, df_ref, nw_ref, cv_ref) = refs[:17]
        cs_in = refs[17:17 + nx]
        out_ref, gconv_ref, gvec_ref, gdt_ref = refs[17 + nx:21 + nx]
        cs_out = refs[21 + nx:21 + 2 * nx]
        (dh_scr, head_scr, dcpad, da_scr, dxdt_scr, dbc_scr, emat_ref, fold_ref) = refs[21 + 2 * nx:29 + 2 * nx]
        cs_sems = refs[29 + 2 * nx:]
        i = pl.program_id(0)
        c = nc - 1 - i

        if nx:
            @pl.when(i == 0)
            def _():
                mine, sends, _ = _chip_exchange_copies(cs_in, cs_out, *cs_sems)
                for cp in mine + sends:
                    cp.start()

            @pl.when(i == nc - 1)
            def _():
                mine, sends, recvs = _chip_exchange_copies(cs_in, cs_out, *cs_sems)
                for cp in recvs:
                    cp.wait_recv()
                for cp in sends:
                    cp.wait_send()
                for cp in mine:
                    cp.wait()

        @pl.when(i == 0)
        def _():
            emat_ref[...] = _expand_mat()
            fold_ref[...] = _fold_mat()
            dh_scr[...] = jnp.zeros_like(dh_scr)
            head_scr[...] = jnp.zeros_like(head_scr)
            gconv_ref[...] = jnp.zeros_like(gconv_ref)
            gvec_ref[...] = jnp.zeros_like(gvec_ref)
            gdt_ref[...] = jnp.zeros_like(gdt_ref)

        cv = cv_ref[...]
        sig = _sigmoid(cv)
        xbc = cv * sig
        pre, dt_f, al_f, al_x, al_t = _decay_terms(dt_ref, dtb_ref, alog16_ref, emat_ref)
        head0 = _iota((CHUNK, LANES), 1) < HEAD_DIM
        sub = _iota((CHUNK, LANES), 0)
        last_row = sub == CHUNK - 1

        yv, z, dmx = y_ref[...], z_ref[...], dm_ref[...]
        sz = _sigmoid(z)
        silu = z * sz
        yz = yv * silu
        dyz_parts = []
        for g in range(N_GROUPS):
            gs = slice(g * gw, (g + 1) * gw)
            part = yz[:, gs]
            r = lax.rsqrt(jnp.mean(part * part, axis=-1, keepdims=True) + EPS)
            nh = part * r
            gvec_ref[0:1, gs] += jnp.sum(dmx[:, gs] * nh, axis=0, keepdims=True)
            dn = dmx[:, gs] * nw_ref[:, gs]
            dyz_parts.append(r * (dn - nh * jnp.mean(dn * nh, axis=-1, keepdims=True)))
        dyz = jnp.concatenate(dyz_parts, axis=1)
        dy = dyz * silu
        out_ref[:, 0:D_SSM] = (dyz * yv * (sz * (1.0 + z * (1.0 - sz)))).astype(bf16)

        x_all = xbc[:, 0:D_SSM]
        gvec_ref[2:3, :] += jnp.sum(dy * x_all, axis=0, keepdims=True)

        for g in range(N_GROUPS):
            bm = xbc[:, D_SSM + g * D_STATE: D_SSM + (g + 1) * D_STATE].astype(bf16)
            cm = xbc[:, D_SSM + (N_GROUPS + g) * D_STATE: D_SSM + (N_GROUPS + g + 1) * D_STATE].astype(bf16)
            gmat = _nt(cm, bm)
            dgm = jnp.zeros((CHUNK, CHUNK), f32)
            db = jnp.zeros((CHUNK, D_STATE), f32)
            dc = jnp.zeros((CHUNK, D_STATE), f32)
            for pair in range(4 * g, 4 * g + 4):
                sl = slice(pair * LANES, (pair + 1) * LANES)
                xp, dtp, alp, dyp = x_all[:, sl], dt_f[:, sl], al_f[:, sl], dy[:, sl]
                xdt = xp * dtp
                xdt16 = xdt.astype(bf16)
                al_last = alp[CHUNK - 1:CHUNK, :]
                e_l = jnp.exp(alp)
                wf = jnp.exp(al_last - alp)
                e_last = jnp.exp(al_last)
                hp = st_ref[:, sl]
                hp16 = hp.astype(bf16)
                dhn = dh_scr[:, sl]
                dhn16 = dhn.astype(bf16)
                y_off = e_l * _nn(cm, hp16)
                dch16 = (dyp * e_l).astype(bf16)
                dc = dc + _nt(dch16, hp16)
                dh_out = _tn(cm, dch16)
                dal = dyp * y_off
                xw16 = (wf * xdt).astype(bf16)
                db = db + _nt(xw16, dhn16)
                dxw = _nn(bm, dhn16)
                dxdt = dxw * wf
                dwf = dxw * xdt * wf
                dal = dal - dwf
                dal_last = jnp.sum(dwf, axis=0, keepdims=True) + jnp.sum(dhn * hp, axis=0, keepdims=True) * e_last
                dh_scr[:, sl] = e_last * dhn + dh_out
                for h in range(2):
                    mh = head0 if h == 0 else jnp.logical_not(head0)
                    dyh16 = jnp.where(mh, dyp, 0.0).astype(bf16)
                    lmat = _decay_mat(al_x, al_t, pair, h)
                    mm = gmat * lmat
                    dmm = _nt(dyh16, xdt16)
                    dxdt = dxdt + _tn(mm.astype(bf16), dyh16)
                    n16 = (dmm * mm).astype(bf16)
                    jh = jnp.where(mh, 1.0 / HEAD_DIM, 0.0).astype(bf16)
                    dal = dal + _nn(n16, jh) - _tn(n16, jh)
                    dgm = dgm + dmm * lmat
                da_scr[:, sl] = dal + jnp.where(last_row, dal_last, 0.0)
                dxdt_scr[:, sl] = dxdt
            dgm16 = dgm.astype(bf16)
            dbc_scr[:, g * D_STATE:(g + 1) * D_STATE] = db + _tn(dgm16, cm)
            dbc_scr[:, (N_GROUPS + g) * D_STATE:(N_GROUPS + g + 1) * D_STATE] = dc + _nn(dgm16, bm)

        sub_c, lane_c = _iota((CHUNK, CHUNK), 0), _iota((CHUNK, CHUNK), 1)
        tri_t = (lane_c >= sub_c).astype(bf16)
        dadt = _dot_01_left(tri_t, da_scr[...], 2)
        a_f = -jnp.exp(alogf_ref[...])
        dxdt_all = dxdt_scr[...]
        ddt_f = dxdt_all * x_all + a_f * dadt
        gvec_ref[1:2, :] += jnp.sum(dt_f * dadt, axis=0, keepdims=True) * a_f
        dx = df_ref[...] * dy + dxdt_all * dt_f
        ddt_raw = _dot_01(ddt_f, fold_ref[...], 2) * _sigmoid(pre)
        gdt_ref[0:1, :] += jnp.sum(ddt_raw, axis=0, keepdims=True)
        out_ref[:, D_SSM + D_CONV:D_SSM + D_CONV + LANES] = ddt_raw.astype(bf16)
        out_ref[:, D_SSM + D_CONV + LANES:] = jnp.zeros((CHUNK, 3 * LANES), bf16)

        dsil = sig * (1.0 + cv * (1.0 - sig))
        dcv_x = dx * dsil[:, 0:D_SSM]
        dcv_bc = dbc_scr[...] * dsil[:, D_SSM:]
        dcpad[0:CHUNK, 0:D_SSM] = dcv_x
        dcpad[0:CHUNK, D_SSM:] = dcv_bc
        dcpad[CHUNK:, :] = head_scr[...]
        dcp = dcpad[...]
        dcv = dcp[0:CHUNK]
        gconv_ref[4:5, :] += jnp.sum(dcv, axis=0, keepdims=True)
        x_raw = jnp.concatenate([xs_ref[...], bc_ref[...]], axis=1)
        draw = cw_ref[3:4, :] * dcv
        gconv_ref[3:4, :] += jnp.sum(dcv * x_raw, axis=0, keepdims=True)
        for j in range(3):
            ahead = pltpu.roll(dcp, CHUNK + 8 - (3 - j), 0)[0:CHUNK]
            draw = draw + cw_ref[j:j + 1, :] * ahead
            gconv_ref[j:j + 1, :] += jnp.sum(ahead * x_raw, axis=0, keepdims=True)
        head_scr[...] = dcv[0:8]
        out_ref[:, D_SSM:D_SSM + D_CONV] = draw.astype(bf16)

    order = lambda i: nc - 1 - i
    row = lambda w, cb=0: pl.BlockSpec((CHUNK, w), lambda i: (nc - 1 - i, cb))
    anyspec = pl.BlockSpec(memory_space=pl.ANY)
    outs = pl.pallas_call(
        body, name="ssd_bwd", grid=(nc,),
        in_specs=_ssd_in_specs(order) + [row(D_SSM), pl.BlockSpec((None, D_STATE, D_SSM), lambda i: (nc - 1 - i, 0, 0)),
                                         row(D_SSM, 1), _full((4, D_CONV)), _full((1, D_CONV)), _full((1, LANES)),
                                         _full((1, LANES)), _full((1, D_SSM)), _full((1, D_SSM)), _full((1, D_SSM)),
                                         row(D_CONV)]
        + [anyspec] * nx,
        out_specs=[row(3072), _full((8, D_CONV)), _full((8, D_SSM)), _full((8, LANES))] + [anyspec] * nx,
        out_shape=[SDS((s, 3072), bf16), SDS((8, D_CONV), f32), SDS((8, D_SSM), f32), SDS((8, LANES), f32)]
        + [SDS(a.shape, a.dtype) for a in chip_sums],
        scratch_shapes=[pltpu.VMEM((D_STATE, D_SSM), f32), pltpu.VMEM((8, D_CONV), f32),
                        pltpu.VMEM((8 + CHUNK, D_CONV), f32),
                        pltpu.VMEM((CHUNK, D_SSM), f32), pltpu.VMEM((CHUNK, D_SSM), f32),
                        pltpu.VMEM((CHUNK, 2 * N_GROUPS * D_STATE), f32),
                        pltpu.VMEM((LANES, 2 * D_SSM), bf16), pltpu.VMEM((D_SSM, LANES), bf16)]
        + (_chip_exchange_scratch(nx) if nx else []),
        compiler_params=pltpu.CompilerParams(dimension_semantics=("arbitrary",)),
    )(proj, proj, proj, proj, proj, proj, y, states, dmix, conv_w, conv_b, dtb16, alog16, alog_f, d_f, nw, cv,
      *chip_sums)
    return outs[0], outs[1], outs[2], outs[3], outs[4:]


def _col_blocks(parts, tile):
    counts = [p.shape[1] // tile for p in parts]
    offs = [sum(counts[:t]) for t in range(len(parts))]
    return offs, counts, sum(counts)


def _bcast_copies(src_ref, out_ref, send_sems, recv_sems, local_sem):
    x, y, c = _my_pos()
    me = 4 * x + 2 * y + c
    mine = pltpu.make_async_copy(src_ref, out_ref.at[me], local_sem)
    sends, recvs = [], []
    for k in range(1, N_DEV):
        to, frm = (me + k) % N_DEV, (me + N_DEV - k) % N_DEV
        sems = dict(send_sem=send_sems.at[k - 1], recv_sem=recv_sems.at[k - 1], device_id_type=MESH)
        sends.append(pltpu.make_async_remote_copy(
            src_ref=src_ref, dst_ref=out_ref.at[me], device_id=(to // 4, (to // 2) % 2, to % 2), **sems))
        recvs.append(pltpu.make_async_remote_copy(
            src_ref=src_ref, dst_ref=out_ref.at[frm], device_id=(x, y, c), **sems))
    return mine, sends, recvs


def _bcast_scratch():
    return [pltpu.SemaphoreType.DMA((N_DEV - 1,)), pltpu.SemaphoreType.DMA((N_DEV - 1,)), pltpu.SemaphoreType.DMA(())]


def _inproj_bwd(dparts, wt, x, nw, dres, chip_sums=(), pack=None):
    s, d = x.shape
    tm, tk = 1024, 1024
    offs, counts, nk = _col_blocks(dparts, tk)
    npart, nx = len(dparts), len(chip_sums)
    npk = 0 if pack is None else 1
    ni = s // tm

    def body(*refs):
        dp_refs = refs[:npart]
        w_ref, x_ref, nw_ref, dres_ref = refs[npart:npart + 4]
        pos = npart + 4
        cs_in, pos = refs[pos:pos + nx], pos + nx
        pack_in, pos = refs[pos:pos + npk], pos + npk
        (gx_ref, gnw_ref), pos = refs[pos:pos + 2], pos + 2
        cs_out, pos = refs[pos:pos + nx], pos + nx
        pack_out, pos = refs[pos:pos + 2 * npk], pos + 2 * npk
        acc, pos = refs[pos], pos + 1
        cs_sems, pos = refs[pos:pos + 3 * min(nx, 1)], pos + 3 * min(nx, 1)
        pk_refs = refs[pos:]
        i, k = pl.program_id(0), pl.program_id(1)

        def exchange():
            return _chip_exchange_copies(cs_in, cs_out, *cs_sems)

        def pack_copies():
            return _bcast_copies(pack_in[0], pack_out[0], *pk_refs[1:4])

        def gnw_copies():
            return _bcast_copies(pk_refs[0], pack_out[1], *pk_refs[4:7])

        @pl.when(jnp.logical_and(i == 0, k == 0))
        def _():
            gnw_ref[...] = jnp.zeros_like(gnw_ref)
            if nx:
                mine, sends, _ = exchange()
                for cp in mine + sends:
                    cp.start()
            if npk:
                mine, sends, _ = pack_copies()
                for cp in [mine] + sends:
                    cp.start()

        @pl.when(k == 0)
        def _():
            acc[...] = _nn(dp_refs[0][...], w_ref[...])

        for t in range(npart):
            @pl.when(jnp.logical_and(k >= max(offs[t], 1), k < offs[t] + counts[t]))
            def _(t=t):
                acc[...] += _nn(dp_refs[t][...], w_ref[...])

        @pl.when(k == nk - 1)
        def _():
            xv = x_ref[...]
            r = lax.rsqrt(jnp.mean(xv * xv, axis=-1, keepdims=True) + EPS)
            xn = xv * r
            du = acc[...]
            gnw_ref[0:1, :] += jnp.sum(du * xn, axis=0, keepdims=True)
            dn = du * nw_ref[...]
            gx_ref[...] = dres_ref[...] + r * (dn - xn * jnp.mean(dn * xn, axis=-1, keepdims=True))

        @pl.when(jnp.logical_and(i == ni - 1, k == nk - 1))
        def _():
            if npk:
                pk_refs[0][...] = gnw_ref[...]
                mine, sends, _ = gnw_copies()
                for cp in [mine] + sends:
                    cp.start()
            if nx:
                mine, sends, recvs = exchange()
                for cp in recvs:
                    cp.wait_recv()
                for cp in sends:
                    cp.wait_send()
                for cp in mine:
                    cp.wait()
            if npk:
                for copies in (pack_copies(), gnw_copies()):
                    mine, sends, recvs = copies
                    for cp in recvs:
                        cp.wait_recv()
                    for cp in sends:
                        cp.wait_send()
                    mine.wait()

    def piece(t):
        return pl.BlockSpec((tm, tk), lambda i, k: (i, jnp.clip(k - offs[t], 0, counts[t] - 1)))

    anyspec = pl.BlockSpec(memory_space=pl.ANY)
    packs = [] if pack is None else [pack]
    pack_shapes = [] if pack is None else [SDS((N_DEV,) + pack.shape, f32), SDS((N_DEV, 8, d), f32)]
    scratch = [pltpu.VMEM((tm, d), f32)] + (_chip_exchange_scratch(nx) if nx else [])
    if npk:
        scratch += [pltpu.VMEM((8, d), f32)] + _bcast_scratch() + _bcast_scratch()
    outs = pl.pallas_call(
        body, name="inproj_bwd", grid=(ni, nk),
        in_specs=[piece(t) for t in range(npart)] + [
            pl.BlockSpec((tk, d), lambda i, k: (k, 0)),
            pl.BlockSpec((tm, d), lambda i, k: (i, 0)), pl.BlockSpec((1, d), lambda i, k: (0, 0)),
            pl.BlockSpec((tm, d), lambda i, k: (i, 0))] + [anyspec] * (nx + npk),
        out_specs=[pl.BlockSpec((tm, d), lambda i, k: (i, 0)), pl.BlockSpec((8, d), lambda i, k: (0, 0))]
        + [anyspec] * (nx + 2 * npk),
        out_shape=[SDS((s, d), f32), SDS((8, d), f32)] + [SDS(a.shape, a.dtype) for a in chip_sums] + pack_shapes,
        scratch_shapes=scratch,
        compiler_params=pltpu.CompilerParams(dimension_semantics=("arbitrary", "arbitrary")),
    )(*dparts, wt, x, nw, dres, *chip_sums, *packs)
    return outs[0], outs[1], outs[2:2 + nx], outs[2 + nx:]


def _matmul_tn(a_parts, b_parts, name):
    tile, tk = 1024, 1024
    s = a_parts[0].shape[0]
    nk = s // tk
    na, nb = len(a_parts), len(b_parts)
    offs_a, counts_a, ni = _col_blocks(a_parts, tile)
    offs_b, counts_b, nj = _col_blocks(b_parts, tile)

    def body(*refs):
        a_refs, b_refs, o_ref = refs[:na], refs[na:na + nb], refs[na + nb]
        i, j = pl.program_id(0), pl.program_id(1)

        @pl.when(pl.program_id(2) == 0)
        def _():
            o_ref[...] = jnp.zeros_like(o_ref)

        for ta in range(na):
            for tb in range(nb):
                in_a = jnp.logical_and(i >= offs_a[ta], i < offs_a[ta] + counts_a[ta])
                in_b = jnp.logical_and(j >= offs_b[tb], j < offs_b[tb] + counts_b[tb])

                @pl.when(jnp.logical_and(in_a, in_b))
                def _(ta=ta, tb=tb):
                    o_ref[...] += _tn(a_refs[ta][...], b_refs[tb][...])

    def spec(offs, counts, t, axis):
        def index(i, j, k):
            pos = (i, j)[axis]
            mine = jnp.logical_and(pos >= offs[t], pos < offs[t] + counts[t])
            return jnp.where(mine, k, 0), jnp.clip(pos - offs[t], 0, counts[t] - 1)
        return pl.BlockSpec((tk, tile), index)

    return pl.pallas_call(
        body, name=name, grid=(ni, nj, nk),
        in_specs=[spec(offs_a, counts_a, t, 0) for t in range(na)] + [spec(offs_b, counts_b, t, 1) for t in range(nb)],
        out_specs=pl.BlockSpec((tile, tile), lambda i, j, k: (i, j)),
        out_shape=SDS((ni * tile, nj * tile), f32),
        compiler_params=pltpu.CompilerParams(dimension_semantics=("parallel", "parallel", "arbitrary")),
    )(*a_parts, *b_parts)


def _adamw(w, g, m, v):
    m = ADAM_B1 * m + (1.0 - ADAM_B1) * g
    v = ADAM_B2 * v + (1.0 - ADAM_B2) * (g * g)
    m_hat = m / (1.0 - ADAM_B1 ** ADAM_STEP)
    v_hat = v / (1.0 - ADAM_B2 ** ADAM_STEP)
    delta = -ADAM_LR * (m_hat / (jnp.sqrt(v_hat) + ADAM_EPS) + ADAM_WD * w)
    return delta, m, v


def _sum_adamw(parts, w, m, v, name):
    r, c = w.shape
    tc = 256

    def body(p_ref, w_ref, m_ref, v_ref, g_ref, d_ref, nm_ref, nv_ref):
        g = p_ref[0].astype(f32)
        for q in range(1, 4):
            g = g + p_ref[q].astype(f32)
        g_ref[...] = g
        d_ref[...], nm_ref[...], nv_ref[...] = _adamw(w_ref[...], g, m_ref[...], v_ref[...])

    blk = pl.BlockSpec((r, tc), lambda i: (0, i))
    return pl.pallas_call(
        body, name=name, grid=(c // tc,),
        in_specs=[pl.BlockSpec((4, r, tc), lambda i: (0, 0, i)), blk, blk, blk],
        out_specs=[blk] * 4, out_shape=[SDS((r, c), f32)] * 4,
        compiler_params=pltpu.CompilerParams(dimension_semantics=("parallel",)),
    )(parts, w, m, v)


def _sum_small(parts, pre_blocks):
    def body(p_ref, b_ref, o_ref):
        t = p_ref[0]
        pre = b_ref[0]
        for j in range(1, N_DEV):
            t = t + p_ref[j]
            pre = pre + b_ref[j]
        o_ref[...] = t
        o_ref[5:6, 0:D_MODEL] = pre[0:1, :]
        row_h = _iota((D_SSM, LANES), 0) // HEAD_DIM
        fold = (row_h == _iota((D_SSM, LANES), 1)).astype(f32)
        lower = t[8:16, 0:LANES]
        folded = _nn_hi(t[8:16, 0:D_SSM], fold)
        loss = jnp.sum(t[11:12, 0:D_MODEL], axis=1, keepdims=True) * (0.5 / D_MODEL)
        row = _iota((8, LANES), 0)
        o_ref[8:16, 0:LANES] = jnp.where(row < 2, folded, jnp.where(row == 4, loss, lower))

    return pl.pallas_call(body, name="sum_small", out_shape=SDS((PACK_ROWS, PACK_W), f32),
                          in_specs=[pl.BlockSpec(memory_space=pltpu.VMEM)] * 2,
                          out_specs=pl.BlockSpec(memory_space=pltpu.VMEM))(parts, pre_blocks)


def _adamw_small(w, g, m, v):
    def body(w_ref, g_ref, m_ref, v_ref, d_ref, nm_ref, nv_ref):
        d_ref[...], nm_ref[...], nv_ref[...] = _adamw(w_ref[...], g_ref[...], m_ref[...], v_ref[...])

    vm = pl.BlockSpec(memory_space=pltpu.VMEM)
    return pl.pallas_call(body, name="adamw_small", out_shape=[SDS(w.shape, f32)] * 3,
                          in_specs=[vm] * 4, out_specs=[vm] * 3)(w, g, m, v)


def _pad_lanes(v, width):
    return jnp.pad(v, ((0, 0), (0, width - v.shape[1])))


def _local_step(x, tgt, norm_pre_w, wt, conv_w, conv_b, dt_bias, a_log, d_skip, ssm_norm_w, wo, norm_post_w, sharded):
    dtb16 = _pad_lanes(dt_bias, LANES)
    alog16 = _pad_lanes(a_log, LANES)
    alog_f = jnp.repeat(a_log, HEAD_DIM, axis=1)
    d_f = jnp.repeat(d_skip, HEAD_DIM, axis=1)

    shard_out = wo.shape[0]
    if sharded:
        proj, u, (g_out, g_cw) = _prenorm_inproj(x, norm_pre_w, wt, gather=(wo, conv_w))
        wo = g_out.reshape(N_DEV * shard_out, D_MODEL)
        conv_w = g_cw.transpose(1, 0, 2).reshape(4, D_CONV)
    else:
        proj, u, _ = _prenorm_inproj(x, norm_pre_w, wt)
    o, lb, mix_a, *qkv = _attn_fwd(proj)
    mix_s, y, states, cv = _ssd_fwd(proj, conv_w, conv_b, dtb16, alog16, alog_f, d_f, ssm_norm_w)
    dmix, dres, acc_post, dw_out = _outproj_loss(mix_a, mix_s, wo, x, tgt, norm_post_w)
    ssd_args = (proj, y, states, cv, dmix, conv_w, conv_b, dtb16, alog16, alog_f, d_f, ssm_norm_w)
    if sharded:
        dq, dk, dv, dg, got_out = _attn_bwd(proj, qkv, o, lb, dmix, swap=dw_out)
        chip_out = _chip_sum(dw_out, got_out, shard_out, "chip_sum_w_out")
        dzxd, g_conv, g_vec, g_dt, (parts_out,) = _ssd_bwd(*ssd_args, chip_sums=[chip_out])
    else:
        dq, dk, dv, dg = _attn_bwd(proj, qkv, o, lb, dmix)
        dzxd, g_conv, g_vec, g_dt, _ = _ssd_bwd(*ssd_args)
    dparts = [dq, dk, dv, dg, dzxd]

    def pack(g_pre_row):
        return jnp.concatenate(
            [g_conv[0:5], g_pre_row, _pad_lanes(g_vec[0:1], PACK_W), _pad_lanes(acc_post[1:2], PACK_W),
             _pad_lanes(g_vec[1:3], PACK_W), _pad_lanes(g_dt[0:1], PACK_W), _pad_lanes(acc_post[0:1], PACK_W),
             jnp.zeros((4, PACK_W), f32)], axis=0)

    if sharded:
        dw_in, got_in = _dw_in_swap(dparts, u)
        chip_in = _chip_sum(dw_in, got_in, D_IN_PROJ // N_DEV, "chip_sum_w_in")
        grad_x, _, (parts_in,), small = _inproj_bwd(dparts, wt, x, norm_pre_w, dres, [chip_in],
                                                    pack(jnp.zeros((1, PACK_W), f32)))
        return grad_x, (parts_in, parts_out), small
    dw_in = _matmul_tn(dparts, [u], "dw_in")
    grad_x, g_pre, _, _ = _inproj_bwd(dparts, wt, x, norm_pre_w, dres)
    return grad_x, (dw_in, dw_out), pack(_pad_lanes(g_pre[0:1], PACK_W))


def kernel(x, norm_pre_w, w_in, conv_w, conv_b, dt_bias, a_log, d_skip, ssm_norm_w, w_out, norm_post_w, loss_target, m_norm_pre_w, m_w_in, m_conv_w, m_conv_b, m_dt_bias, m_a_log, m_d_skip, m_ssm_norm_w, m_w_out, m_norm_post_w, v_norm_pre_w, v_w_in, v_conv_w, v_conv_b, v_dt_bias, v_a_log, v_d_skip, v_ssm_norm_w, v_w_out, v_norm_post_w):
    shard_cv = conv_w.shape[2]
    me = 4 * lax.axis_index("x") + 2 * lax.axis_index("y") + lax.axis_index("c")

    g_in, = _all_gather([w_in[0].T.astype(bf16)])
    wt = _assemble_wt(g_in)

    grad_x, (parts_in, parts_out), (parts_small, pre_blocks) = _local_step(
        x[0], loss_target[0], norm_pre_w, wt, conv_w[0], conv_b, dt_bias, a_log, d_skip, ssm_norm_w,
        w_out[0].astype(bf16), norm_post_w, sharded=True)

    g_w_in, d_w_in, nm_w_in, nv_w_in = (a.T for a in _sum_adamw(
        parts_in, w_in[0].T, m_w_in[0].T, v_w_in[0].T, "sum_adamw_w_in"))
    g_w_out, d_w_out, nm_w_out, nv_w_out = _sum_adamw(parts_out, w_out[0], m_w_out[0], v_w_out[0], "sum_adamw_w_out")
    tot = _sum_small(parts_small, pre_blocks)

    g_cw_all = tot[0:4]
    small_g = {
        "conv_w": lax.dynamic_slice(g_cw_all, (0, me * shard_cv), (4, shard_cv)),
        "conv_b": tot[4:5], "norm_pre_w": tot[5:6, :D_MODEL], "ssm_norm_w": tot[6:7, :D_SSM],
        "norm_post_w": tot[7:8, :D_MODEL], "a_log": tot[8:9, :16], "d_skip": tot[9:10, :16], "dt_bias": tot[10:11, :16],
    }
    loss = tot[12, 0]
    small_w = {"conv_w": (conv_w[0], m_conv_w[0], v_conv_w[0]), "conv_b": (conv_b, m_conv_b, v_conv_b),
               "norm_pre_w": (norm_pre_w, m_norm_pre_w, v_norm_pre_w), "ssm_norm_w": (ssm_norm_w, m_ssm_norm_w, v_ssm_norm_w),
               "norm_post_w": (norm_post_w, m_norm_post_w, v_norm_post_w), "a_log": (a_log, m_a_log, v_a_log),
               "d_skip": (d_skip, m_d_skip, v_d_skip), "dt_bias": (dt_bias, m_dt_bias, v_dt_bias)}
    names = list(small_w)
    sizes = [small_g[k].size for k in names]
    tot_size = sum(sizes)
    pad_to = -(-tot_size // 1024) * 1024

    def flat(arrs):
        v = jnp.concatenate([a.reshape(-1) for a in arrs])
        return jnp.pad(v, (0, pad_to - tot_size)).reshape(pad_to // LANES, LANES)

    fw = flat([small_w[k][0] for k in names])
    fg = flat([small_g[k] for k in names])
    fm = flat([small_w[k][1] for k in names])
    fv = jnp.pad(jnp.concatenate([small_w[k][2].reshape(-1) for k in names]), (0, pad_to - tot_size),
                 constant_values=1.0).reshape(pad_to // LANES, LANES)
    fd, fnm, fnv = _adamw_small(fw, fg, fm, fv)

    def unflat(f):
        out, off = {}, 0
        v = f.reshape(-1)
        for k, n in zip(names, sizes):
            out[k] = v[off:off + n].reshape(small_g[k].shape)
            off += n
        return out

    sd, snm, snv = unflat(fd), unflat(fnm), unflat(fnv)
    lead = lambda a: a[None]
    order = ["norm_pre_w", "w_in", "conv_w", "conv_b", "dt_bias", "a_log", "d_skip", "ssm_norm_w", "w_out", "norm_post_w"]
    grads = dict(small_g, w_in=g_w_in, w_out=g_w_out)
    deltas = dict(sd, w_in=d_w_in, w_out=d_w_out)
    new_m = dict(snm, w_in=nm_w_in, w_out=nm_w_out)
    new_v = dict(snv, w_in=nv_w_in, w_out=nv_w_out)

    def shaped(dct, k):
        a = dct[k]
        return lead(a) if k in ("w_in", "w_out", "conv_w") else a

    return (loss, grad_x[None], *[shaped(grads, k) for k in order], *[shaped(deltas, k) for k in order],
            *[shaped(new_m, k) for k in order], *[shaped(new_v, k) for k in order])
```

```python
import jax
import jax.numpy as jnp
from jax import lax
from jax.experimental import pallas as pl
from jax.experimental.pallas import tpu as pltpu

f32, bf16 = jnp.float32, jnp.bfloat16
SDS = jax.ShapeDtypeStruct
HIGHEST = lax.Precision.HIGHEST
MESH = pl.DeviceIdType.MESH

N_DEV = 8
D_MODEL = 1024
D_ATTN = 1024
D_SSM = 1024
HEAD_DIM = 64
N_PAIRS = 8
D_STATE = 128
N_GROUPS = 2
D_CONV = D_SSM + 2 * N_GROUPS * D_STATE
D_IN_PROJ = 4 * D_ATTN + D_SSM + D_CONV + 16
NP = 7168
CHUNK = 128
BLK = 128
DILATIONS = (1, 4, 16)
EPS = 1e-6
LANES = 128
COL_Z, COL_XS, COL_BC, COL_DT = 4096, 5120, 6144, 6656

ADAM_LR, ADAM_B1, ADAM_B2, ADAM_EPS, ADAM_WD, ADAM_STEP = 0.001, 0.9, 0.999, 1e-08, 0.01, 10

PACK_ROWS, PACK_W = 16, 1536


def _nt(a, b):
    return lax.dot_general(a, b, (((1,), (1,)), ((), ())), preferred_element_type=f32)


def _tn(a, b):
    return lax.dot_general(a, b, (((0,), (0,)), ((), ())), preferred_element_type=f32)


def _nn(a, b):
    return jnp.dot(a, b, preferred_element_type=f32)


def _nn_hi(a, b):
    return jnp.dot(a, b, precision=HIGHEST, preferred_element_type=f32)


def _sigmoid(x):
    return 1.0 / (1.0 + jnp.exp(-x))


def _softplus(x):
    return jnp.maximum(x, 0.0) + jnp.log1p(jnp.exp(-jnp.abs(x)))


def _iota(shape, dim):
    return lax.broadcasted_iota(jnp.int32, shape, dim)


def _my_pos():
    return lax.axis_index("x"), lax.axis_index("y"), lax.axis_index("c")


GATHER_SEMS = 9


def _gather_phases(ins, outs, send_sems, recv_sems, local_sems):
    n, ns = len(ins), GATHER_SEMS
    x, y, c = _my_pos()
    me, sibling = (x, y, c), (x, y, 1 - c)
    xn, yn, diag = (1 - x, y), (x, 1 - y), (1 - x, 1 - y)

    def slot(a, px, py, pc):
        return outs[a].at[4 * px + 2 * py + pc]

    def part(a, ref, h):
        width = ins[a].shape[-1]
        if width % (2 * LANES):
            return ref if h == 1 else None
        return ref.at[:, pl.ds(h * (width // 2), width // 2)]

    def copy(a, k, block, to, src=None, h=None):
        src_ref = slot(a, *block) if src is None else src
        dst_ref = slot(a, *block)
        if h is not None:
            src_ref, dst_ref = part(a, src_ref, h), part(a, dst_ref, h)
            if src_ref is None:
                return None
        return pltpu.make_async_remote_copy(
            src_ref=src_ref, dst_ref=dst_ref, send_sem=send_sems.at[ns * a + k], recv_sem=recv_sems.at[ns * a + k],
            device_id=to, device_id_type=MESH)

    def mine():
        return [pltpu.make_async_copy(ins[a], slot(a, *me), local_sems.at[a]) for a in range(n)]

    def own_sends(a):
        return [copy(a, 0, me, sibling, src=ins[a]), copy(a, 1, me, (*xn, c), src=ins[a]),
                copy(a, 2, me, (*yn, c), src=ins[a])]

    def neighbour_relays(a):
        return [copy(a, 4, (*xn, c), sibling), copy(a, 7, (*xn, c), (*yn, c), h=1),
                copy(a, 5, (*yn, c), sibling), copy(a, 8, (*yn, c), (*xn, c), h=0)]

    def diagonal_halves(a):
        return [copy(a, k, (*diag, c), me, h=h) for k, h in ((8, 0), (7, 1))]

    def start_all(cps):
        for cp in cps:
            if cp is not None:
                cp.start()

    def phase0():
        start_all(mine())
        for a in range(n):
            start_all(own_sends(a))

    def phase1():
        for a in range(n):
            copy(a, 1, (*xn, c), me).wait_recv()
            copy(a, 2, (*yn, c), me).wait_recv()
            start_all(neighbour_relays(a))

    def phase2():
        for a in range(n):
            for cp in diagonal_halves(a):
                if cp is not None:
                    cp.wait_recv()
            copy(a, 6, (*diag, c), sibling).start()

    def finish():
        for a in range(n):
            copy(a, 0, sibling, me).wait_recv()
            for j, chip in enumerate((xn, yn, diag)):
                copy(a, 4 + j, (*chip, 1 - c), me).wait_recv()
        for a in range(n):
            for cp in own_sends(a) + neighbour_relays(a) + [copy(a, 6, (*diag, c), sibling)]:
                if cp is not None:
                    cp.wait_send()
        for cp in mine():
            cp.wait()

    return phase0, phase1, phase2, finish


def _gather_scratch(n):
    return [pltpu.SemaphoreType.DMA((GATHER_SEMS * n,)), pltpu.SemaphoreType.DMA((GATHER_SEMS * n,)),
            pltpu.SemaphoreType.DMA((n,))]


def _all_gather(arrs):
    n = len(arrs)

    def body(*refs):
        for phase in _gather_phases(refs[:n], refs[n:2 * n], *refs[2 * n:]):
            phase()

    anyspec = pl.BlockSpec(memory_space=pl.ANY)
    return pl.pallas_call(
        body, name="weights_all_gather",
        out_shape=[SDS((N_DEV,) + a.shape, a.dtype) for a in arrs],
        in_specs=[anyspec] * n, out_specs=[anyspec] * n, scratch_shapes=_gather_scratch(n),
    )(*arrs)


def _dw_in_swap(a_parts, u):
    tile, tk = 1024, 1024
    s = u.shape[0]
    nk = s // tk
    na = len(a_parts)
    offs, counts, ni = _col_blocks(a_parts, tile)

    def body(*refs):
        a_refs, u_ref = refs[:na], refs[na]
        dw_ref, got_ref = refs[na + 1:na + 3]
        acc, stage, local_sems, send_sems, recv_sem = refs[na + 3:]
        i, k = pl.program_id(0), pl.program_id(1)
        x, y, c = _my_pos()
        par = i % 2

        def tile_copies(t, p):
            rows = pl.ds(pl.multiple_of(t * tile, tile), tile)
            loc = pltpu.make_async_copy(stage.at[p], dw_ref.at[rows], local_sems.at[p])
            rem = pltpu.make_async_remote_copy(
                src_ref=stage.at[p], dst_ref=got_ref.at[rows], send_sem=send_sems.at[p], recv_sem=recv_sem,
                device_id=(x, y, 1 - c), device_id_type=MESH)
            return loc, rem

        @pl.when(k == 0)
        def _():
            acc[...] = jnp.zeros((tile, tile), f32)

        for t in range(na):
            @pl.when(jnp.logical_and(i >= offs[t], i < offs[t] + counts[t]))
            def _(t=t):
                acc[...] += _tn(a_refs[t][...], u_ref[pl.ds(pl.multiple_of(k * tk, tk), tk), :])

        @pl.when(k == nk - 1)
        def _():
            @pl.when(i >= 2)
            def _():
                loc, rem = tile_copies(i - 2, par)
                loc.wait()
                rem.wait_send()
            stage[par] = acc[...]
            loc, rem = tile_copies(i, par)
            loc.start()
            rem.start()

        @pl.when(jnp.logical_and(i == ni - 1, k == nk - 1))
        def _():
            for t in (ni - 2, ni - 1):
                loc, rem = tile_copies(t, t % 2)
                loc.wait()
                rem.wait_send()
            pltpu.make_async_remote_copy(src_ref=dw_ref, dst_ref=got_ref, send_sem=send_sems.at[0], recv_sem=recv_sem,
                                         device_id=(x, y, c), device_id_type=MESH).wait_recv()

    def a_spec(t):
        def index(i, k):
            mine = jnp.logical_and(i >= offs[t], i < offs[t] + counts[t])
            return jnp.where(mine, k, 0), jnp.clip(i - offs[t], 0, counts[t] - 1)
        return pl.BlockSpec((tk, tile), index)

    anyspec = pl.BlockSpec(memory_space=pl.ANY)
    return pl.pallas_call(
        body, name="dw_in_swap", grid=(ni, nk),
        in_specs=[a_spec(t) for t in range(na)] + [pl.BlockSpec((s, tile), lambda i, k: (0, 0))],
        out_specs=[anyspec] * 2,
        out_shape=[SDS((ni * tile, tile), f32), SDS((ni * tile, tile), f32)],
        scratch_shapes=[pltpu.VMEM((tile, tile), f32), pltpu.VMEM((2, tile, tile), f32), pltpu.SemaphoreType.DMA((2,)),
                        pltpu.SemaphoreType.DMA((2,)), pltpu.SemaphoreType.DMA(())],
        compiler_params=pltpu.CompilerParams(dimension_semantics=("arbitrary", "arbitrary")),
    )(*a_parts, u)


def _chip_sum(mine, got, rows, name):
    r, cdim = mine.shape
    tc = LANES

    def body(m_ref, g_ref, s16_ref):
        c = lax.axis_index("c")
        for q in range(4):
            blk = pl.ds(rows * (2 * q + c), rows)
            s16_ref[q] = (m_ref[blk, :] + g_ref[blk, :]).astype(bf16)

    col = pl.BlockSpec((r, tc), lambda i: (0, i))
    return pl.pallas_call(
        body, name=name, grid=(cdim // tc,), in_specs=[col, col],
        out_specs=pl.BlockSpec((4, rows, tc), lambda i: (0, 0, i)), out_shape=SDS((4, rows, cdim), bf16),
        compiler_params=pltpu.CompilerParams(dimension_semantics=("parallel",)),
    )(mine, got)


def _assemble_wt(shards):
    nd, rows, cdim = shards.shape
    tc = 256

    def body(g_ref, o_ref):
        for j in range(nd):
            o_ref[pl.ds(rows * j, rows), :] = g_ref[j]
        o_ref[pl.ds(nd * rows, NP - nd * rows), :] = jnp.zeros((NP - nd * rows, tc), shards.dtype)

    return pl.pallas_call(
        body, name="assemble_w_in", grid=(cdim // tc,),
        in_specs=[pl.BlockSpec((nd, rows, tc), lambda i: (0, 0, i))],
        out_specs=pl.BlockSpec((NP, tc), lambda i: (0, i)), out_shape=SDS((NP, cdim), shards.dtype),
        compiler_params=pltpu.CompilerParams(dimension_semantics=("parallel",)),
    )(shards)


def _chip_exchange_copies(ins, outs, send_sems, recv_sems, local_sems):
    nb = len(ins)
    x, y, c = _my_pos()
    my_q = 2 * x + y
    mine = [pltpu.make_async_copy(ins[a].at[my_q], outs[a].at[my_q], local_sems.at[a]) for a in range(nb)]
    sends, recvs = [], []
    for k in range(1, 4):
        to, frm = (my_q + k) % 4, (my_q + 4 - k) % 4
        for a in range(nb):
            sems = dict(send_sem=send_sems.at[3 * a + k - 1], recv_sem=recv_sems.at[3 * a + k - 1], device_id_type=MESH)
            sends.append(pltpu.make_async_remote_copy(
                src_ref=ins[a].at[to], dst_ref=outs[a].at[my_q], device_id=(to // 2, to % 2, c), **sems))
            recvs.append(pltpu.make_async_remote_copy(
                src_ref=ins[a].at[frm], dst_ref=outs[a].at[frm], device_id=(x, y, c), **sems))
    return mine, sends, recvs


def _chip_exchange_scratch(nb):
    return [pltpu.SemaphoreType.DMA((3 * nb,)), pltpu.SemaphoreType.DMA((3 * nb,)), pltpu.SemaphoreType.DMA((nb,))]


def _prenorm_inproj(x, nw, wt, gather=()):
    s, d = x.shape
    npad = wt.shape[0]
    tm, tn = 1024, 1024
    ng = len(gather)
    ni, nj = s // tm, npad // tn

    def body(x_ref, nw_ref, w_ref, *refs):
        g_in, (proj_ref, u_ref), g_out, sems = refs[:ng], refs[ng:ng + 2], refs[ng + 2:2 * ng + 2], refs[2 * ng + 2:]
        i, j = pl.program_id(0), pl.program_id(1)
        if ng:
            phases = _gather_phases(g_in, g_out, *sems)
            for step, phase in enumerate(phases[:3]):
                @pl.when(jnp.logical_and(i == step, j == 0))
                def _(phase=phase):
                    phase()

        @pl.when(j == 0)
        def _():
            xv = x_ref[...]
            r = lax.rsqrt(jnp.mean(xv * xv, axis=-1, keepdims=True) + EPS)
            u_ref[...] = (xv * r * nw_ref[...]).astype(bf16)
        proj_ref[...] = _nt(u_ref[...], w_ref[pl.ds(pl.multiple_of(j * tn, tn), tn), :])

        if ng:
            @pl.when(jnp.logical_and(i == ni - 1, j == nj - 1))
            def _():
                phases[3]()

    anyspec = pl.BlockSpec(memory_space=pl.ANY)
    outs = pl.pallas_call(
        body, name="prenorm_inproj", grid=(ni, nj),
        in_specs=[pl.BlockSpec((tm, d), lambda i, j: (i, 0)), pl.BlockSpec((1, d), lambda i, j: (0, 0)),
                  pl.BlockSpec((npad, d), lambda i, j: (0, 0))] + [anyspec] * ng,
        out_specs=[pl.BlockSpec((tm, tn), lambda i, j: (i, j)), pl.BlockSpec((tm, d), lambda i, j: (i, 0))]
        + [anyspec] * ng,
        out_shape=[SDS((s, npad), f32), SDS((s, d), bf16)] + [SDS((N_DEV,) + a.shape, a.dtype) for a in gather],
        scratch_shapes=_gather_scratch(ng) if ng else [],
        compiler_params=pltpu.CompilerParams(dimension_semantics=("arbitrary", "arbitrary")),
    )(x, nw, wt, *gather)
    return outs[0], outs[1], outs[2:]


def _attn_consts():
    head0 = _iota((BLK, LANES), 1) < HEAD_DIM
    tri2 = (_iota((BLK, 2 * LANES), 1) % LANES) <= _iota((BLK, 2 * LANES), 0)
    ones2 = ((_iota((LANES, 2 * LANES), 0) < HEAD_DIM) == (_iota((LANES, 2 * LANES), 1) < LANES)).astype(bf16)
    rmat = ((_iota((2 * LANES, LANES), 0) < LANES) == (_iota((2 * LANES, LANES), 1) < HEAD_DIM)).astype(bf16)
    bones = ((_iota((LANES, LANES), 0) < HEAD_DIM) == (_iota((LANES, LANES), 1) < HEAD_DIM)).astype(bf16)
    return head0, tri2, ones2, rmat, bones


def _stack_heads(x16, head0):
    zero = jnp.zeros_like(x16)
    return jnp.concatenate([jnp.where(head0, x16, zero), jnp.where(head0, zero, x16)], axis=0)


def _bf16_terms(x, terms):
    out = []
    for _ in range(terms):
        t = x.astype(bf16)
        out.append(t)
        x = x - t.astype(f32)
    return out


def _dot_01(x, w16, terms):
    return _nn(jnp.concatenate(_bf16_terms(x, terms), axis=1), jnp.concatenate([w16] * terms, axis=0))


def _split_dot_sum(x, w16):
    hi, lo = _bf16_terms(x, 2)
    return _nn(hi, w16) + _nn(lo, w16)


def _dot_01_left(w16, x, terms):
    return _nn(jnp.concatenate([w16] * terms, axis=1), jnp.concatenate(_bf16_terms(x, terms), axis=0))


def _quarter_rows(i, q):
    return pl.ds(pl.multiple_of((i // 2) * 2048 + q * 512 + (i % 2) * 256, 256), 256)


def _token_rows(i, q):
    return pl.ds(i * 1024 + q, 256, stride=4)


def _quarter_block(i, d, nb):
    assert isinstance(i, int)
    r, blk = i // nb, i % nb
    if d == 1:
        runs = [pl.ds((blk // 16) * 2048 + q * 512 + (blk % 16) * 32, 32) for q in range(4)]
    elif d == 4:
        runs = [pl.ds((blk // 4) * 2048 + r * 512 + (blk % 4) * BLK, BLK)]
    else:
        runs = [pl.ds(blk * 2048 + (r % 4) * 512 + r // 4, BLK, stride=4)]
    return runs, blk > 0


def _if_prev(has_prev, x, fill):
    return x if has_prev else jnp.full_like(x, fill)


def _quarter_mask():
    order = lambda n: 4 * (n % 32) + n // 32
    return order(_iota((BLK, 2 * LANES), 1) % LANES) <= order(_iota((BLK, 2 * LANES), 0))


def _load_runs(ref, runs):
    parts = [ref[run, :] for run in runs]
    return parts[0] if len(parts) == 1 else jnp.concatenate(parts, axis=0)


def _store_runs(ref, runs, val):
    n = BLK // len(runs)
    for t, run in enumerate(runs):
        ref[run, :] = val[t * n:(t + 1) * n]


def _add_runs(ref, runs, val):
    n = BLK // len(runs)
    for t, run in enumerate(runs):
        ref[run, :] += val[t * n:(t + 1) * n]


def _attn_fwd(proj):
    s = proj.shape[0]
    n_it = s // BLK

    def body(q_in, k_in, v_in, g_ref, o_ref, l_ref, mix_ref, q_ref, k_ref, v_ref, op0, op1, op2, lp0, lp1, lp2,
             s_a, s_b, sd_a, sd_b, p_a, p_b, m_a, m_b, pd_a, pd_b, k_a, k_b, v_a, v_b, stage):
        op_refs, lp_refs = (op0, op1, op2), (lp0, lp1, lp2)
        head0, tri2_t, ones2, rmat, _ = _attn_consts()
        tri2_q = _quarter_mask()

        def reorder(i, carry):
            for src, dst, scale in ((q_in, q_ref, 0.125), (k_in, k_ref, 1.0), (v_in, v_ref, 1.0)):
                for q in range(4):
                    t = src[_token_rows(i, q), :]
                    dst[_quarter_rows(i, q), :] = t if scale == 1.0 else t * scale
            return carry

        lax.fori_loop(0, s // 1024, reorder, 0)
        score_bufs, prob_bufs = ((s_a, sd_a), (s_b, sd_b)), ((p_a, m_a, pd_a), (p_b, m_b, pd_b))
        k_bufs, v_bufs = (k_a, k_b), (v_a, v_b)
        for buf in k_bufs + v_bufs:
            buf[...] = jnp.zeros_like(buf)

        def unstack(st16):
            return st16[:BLK] + st16[BLK:]

        def scores(i, par, d, nb):
            rows, has_prev = _quarter_block(i, d, nb)
            tri2 = tri2_q if d == 1 else tri2_t
            s_buf, sd_buf = score_bufs[par]
            qs = _load_runs(q_ref, rows)
            qs16 = qs.astype(bf16)
            kst_c = _stack_heads(_load_runs(k_ref, rows).astype(bf16), head0)
            kst_p = k_bufs[1 - par][...]
            k_bufs[par][...] = kst_c
            sc = _nt(qs16, kst_c)
            sp = _nt(qs16, kst_p)
            s_buf[...] = jnp.where(tri2, sc, _if_prev(has_prev, sp, -jnp.inf))
            sd = _nn((qs * unstack(kst_p).astype(f32)).astype(bf16), ones2)
            sd_buf[...] = _if_prev(has_prev, sd, -jnp.inf)

        def softmax(bufs_in, bufs_out):
            s_buf, sd_buf = bufs_in
            p_buf, m_buf, pd_buf = bufs_out
            sc, sd2 = s_buf[...], sd_buf[...]
            m0 = jnp.max(sc[:, :LANES], axis=1, keepdims=True)
            m1 = jnp.max(sc[:, LANES:], axis=1, keepdims=True)
            m2 = jnp.concatenate([jnp.broadcast_to(m0, (BLK, LANES)), jnp.broadcast_to(m1, (BLK, LANES))], axis=1)
            m2 = jnp.maximum(m2, sd2)
            p_buf[...] = jnp.exp(sc - m2).astype(bf16)
            m_pair = jnp.where(head0, m2[:, :LANES], m2[:, LANES:])
            m_buf[...] = m_pair
            pd_buf[...] = jnp.exp(jnp.where(head0, sd2[:, :LANES], sd2[:, LANES:]) - m_pair)

        def output(i, par, d, nb, p):
            rows, has_prev = _quarter_block(i, d, nb)
            tri2 = tri2_q if d == 1 else tri2_t
            p_buf, m_buf, pd_buf = prob_bufs[par]
            vst_c = _stack_heads(_load_runs(v_ref, rows).astype(bf16), head0)
            vst_p = v_bufs[1 - par][...]
            v_bufs[par][...] = vst_c
            pt16, pd = p_buf[...], pd_buf[...]
            zero = jnp.zeros_like(pt16)
            o = _nn(jnp.where(tri2, pt16, zero), vst_c)
            if has_prev:
                o = o + _nn(jnp.where(tri2, zero, pt16), vst_p) + pd * unstack(vst_p).astype(f32)
            l = _nn(pt16, rmat) + pd
            _store_runs(op_refs[p], rows, o / l)
            _store_runs(lp_refs[p], rows, m_buf[...] + jnp.log(l))

        for p, d in enumerate(DILATIONS):
            nb = s // (BLK * d)
            scores(0, 0, d, nb)
            scores(1, 1, d, nb)
            softmax(score_bufs[0], prob_bufs[0])

            for t in range(2, n_it):
                par = t % 2
                scores(t, par, d, nb)
                output(t - 2, par, d, nb, p)
                softmax(score_bufs[1 - par], prob_bufs[1 - par])
            output(n_it - 2, 0, d, nb, p)
            softmax(score_bufs[1], prob_bufs[1])
            output(n_it - 1, 1, d, nb, p)

        def merge(i, carry):
            for q in range(4):
                rows, tokens = _quarter_rows(i, q), _token_rows(i, q)
                l0, l1, l2 = lp0[rows, :], lp1[rows, :], lp2[rows, :]
                m = jnp.maximum(jnp.maximum(l0, l1), l2)
                e0, e1, e2 = jnp.exp(l0 - m), jnp.exp(l1 - m), jnp.exp(l2 - m)
                z = e0 + e1 + e2
                o = (e0 * op0[rows, :] + e1 * op1[rows, :] + e2 * op2[rows, :]) / z
                o_ref[tokens, :] = o
                l_ref[rows, :] = m + jnp.log(z)
                g = g_ref[tokens, :]
                stage[pl.ds(q, 256, stride=4), :] = o * (g * _sigmoid(g))
            mix_ref[pl.ds(pl.multiple_of(i * 1024, 1024), 1024), :] = stage[...].astype(bf16)
            return carry

        lax.fori_loop(0, s // 1024, merge, 0)

    col = lambda base: pl.BlockSpec((s, LANES), lambda h: (0, base + h))
    return pl.pallas_call(
        body, name="attn_fwd", grid=(N_PAIRS,),
        in_specs=[col(0), col(8), col(16), col(24)],
        out_specs=[col(0)] * 6,
        out_shape=[SDS((s, D_ATTN), f32), SDS((s, D_ATTN), f32), SDS((s, D_ATTN), bf16)] + [SDS((s, D_ATTN), f32)] * 3,
        scratch_shapes=[pltpu.VMEM((s, LANES), f32)] * 6 + [pltpu.VMEM((BLK, 2 * LANES), f32)] * 4
        + [pltpu.VMEM((BLK, 2 * LANES), bf16)] * 2 + [pltpu.VMEM((BLK, LANES), f32)] * 4
        + [pltpu.VMEM((2 * BLK, LANES), bf16)] * 4 + [pltpu.VMEM((1024, LANES), f32)],
        compiler_params=pltpu.CompilerParams(dimension_semantics=("parallel",)),
    )(proj, proj, proj, proj)


def _expand_mat():
    colv = _iota((LANES, 2 * D_SSM), 1)
    head = 2 * ((colv % D_SSM) // LANES) + colv // D_SSM
    return (_iota((LANES, 2 * D_SSM), 0) == head).astype(bf16)


def _fold_mat():
    return (_iota((D_SSM, LANES), 0) // HEAD_DIM == _iota((D_SSM, LANES), 1)).astype(bf16)


def _conv(xs_ref, bc_ref, xs_tail, bc_tail, cw_ref, cb_ref, xpad, first):
    keep = jnp.where(first, 0.0, 1.0)
    xpad[0:8, 0:D_SSM] = xs_tail[...] * keep
    xpad[0:8, D_SSM:D_CONV] = bc_tail[...] * keep
    xpad[8:8 + CHUNK, 0:D_SSM] = xs_ref[...]
    xpad[8:8 + CHUNK, D_SSM:D_CONV] = bc_ref[...]
    xp = xpad[...]
    cv = cb_ref[...] + cw_ref[3:4, :] * xp[8:8 + CHUNK]
    for j in range(3):
        cv = cv + cw_ref[j:j + 1, :] * pltpu.roll(xp, 3 - j, 0)[8:8 + CHUNK]
    return cv


def _decay_terms(dt_ref, dtb_ref, alog16_ref, emat_ref):
    pre = dt_ref[...] + dtb_ref[...]
    dt16 = _softplus(pre)
    a16 = -jnp.exp(alog16_ref[...])
    sub, lane = _iota((CHUNK, CHUNK), 0), _iota((CHUNK, CHUNK), 1)
    tri = (sub >= lane).astype(f32)
    al16 = _nn_hi(tri, dt16 * a16)
    al_t = al16.T
    emat = emat_ref[...]
    dt_x = _dot_01(dt16, emat, 3)
    al_x = _dot_01(al16, emat, 3)
    lane_w = _iota((CHUNK, D_SSM), 1)
    even = (lane_w % LANES) < HEAD_DIM
    dt_f = jnp.where(even, dt_x[:, :D_SSM], dt_x[:, D_SSM:])
    al_f = jnp.where(even, al_x[:, :D_SSM], al_x[:, D_SSM:])
    return pre, dt_f, al_f, al_x, al_t


def _decay_mat(al_x, al_t, pair, h):
    sub, lane = _iota((CHUNK, CHUNK), 0), _iota((CHUNK, CHUNK), 1)
    col = al_x[:, h * D_SSM + pair * LANES: h * D_SSM + (pair + 1) * LANES]
    row = al_t[2 * pair + h: 2 * pair + h + 1, :]
    return jnp.exp(jnp.where(sub >= lane, col - row, -jnp.inf))


def _ssd_in_specs(order, rows=CHUNK):
    blk = lambda w, cb: pl.BlockSpec((rows, w), lambda i: (order(i), cb))
    tail = lambda w, cb: pl.BlockSpec((8, w), lambda i: (jnp.maximum((rows // 8) * order(i) - 1, 0), cb))
    return [blk(D_SSM, COL_XS // D_SSM), blk(512, COL_BC // 512), tail(D_SSM, COL_XS // D_SSM),
            tail(512, COL_BC // 512), blk(LANES, COL_DT // LANES), blk(D_SSM, COL_Z // D_SSM)]


def _full(shape):
    return pl.BlockSpec(shape, lambda i: (0,) * len(shape))


def _ssd_fwd(proj, conv_w, conv_b, dtb16, alog16, alog_f, d_f, nw):
    s = proj.shape[0]
    nc = s // CHUNK

    per_step = 4

    def body(xs_ref, bc_ref, xs_tail, bc_tail, dt_ref, z_ref, cw_ref, cb_ref, dtb_ref, alog16_ref, alogf_ref,
             df_ref, nw_ref, mix_ref, y_ref, st_ref, cv_ref, h_scr, *scratch):
        xpads, ys, emat_ref = scratch[:per_step], scratch[per_step:2 * per_step], scratch[2 * per_step]
        c = pl.program_id(0)

        @pl.when(c == 0)
        def _():
            h_scr[...] = jnp.zeros_like(h_scr)
            emat_ref[...] = _expand_mat()

        for t, (xpad, y_scr) in enumerate(zip(xpads, ys)):
            rows, before = pl.ds(t * CHUNK, CHUNK), pl.ds(max(t * CHUNK - 8, 0), 8)
            chunk(xs_ref.at[rows], bc_ref.at[rows], xs_tail if t == 0 else xs_ref.at[before],
                  bc_tail if t == 0 else bc_ref.at[before], dt_ref.at[rows], z_ref.at[rows], cw_ref, cb_ref, dtb_ref,
                  alog16_ref, df_ref, nw_ref, mix_ref.at[rows], y_ref.at[rows], st_ref.at[t], cv_ref.at[rows],
                  h_scr, xpad, y_scr, emat_ref, (c == 0) if t == 0 else False)

    def chunk(xs_ref, bc_ref, xs_tail, bc_tail, dt_ref, z_ref, cw_ref, cb_ref, dtb_ref, alog16_ref,
              df_ref, nw_ref, mix_ref, y_ref, st_ref, cv_ref, h_scr, xpad, y_scr, emat_ref, first):
        cv = _conv(xs_ref, bc_ref, xs_tail, bc_tail, cw_ref, cb_ref, xpad, first)
        cv_ref[...] = cv
        xbc = cv * _sigmoid(cv)
        _, dt_f, al_f, al_x, al_t = _decay_terms(dt_ref, dtb_ref, alog16_ref, emat_ref)
        head0 = _iota((CHUNK, LANES), 1) < HEAD_DIM
        st_ref[...] = h_scr[...]
        for g in range(N_GROUPS):
            bm = xbc[:, D_SSM + g * D_STATE: D_SSM + (g + 1) * D_STATE].astype(bf16)
            cm = xbc[:, D_SSM + (N_GROUPS + g) * D_STATE: D_SSM + (N_GROUPS + g + 1) * D_STATE].astype(bf16)
            gmat = _nt(cm, bm)
            for pair in range(4 * g, 4 * g + 4):
                sl = slice(pair * LANES, (pair + 1) * LANES)
                xp, dtp, alp = xbc[:, sl], dt_f[:, sl], al_f[:, sl]
                xdt = xp * dtp
                xdt16 = xdt.astype(bf16)
                al_last = alp[CHUNK - 1:CHUNK, :]
                hp = h_scr[:, sl]
                y_off = jnp.exp(alp) * _nn(cm, hp.astype(bf16))
                yd = [_nn((gmat * _decay_mat(al_x, al_t, pair, h)).astype(bf16), xdt16) for h in range(2)]
                y_scr[:, sl] = jnp.where(head0, yd[0], yd[1]) + y_off + df_ref[:, sl] * xp
                st = _tn(bm, (jnp.exp(al_last - alp) * xdt).astype(bf16))
                h_scr[:, sl] = jnp.exp(al_last) * hp + st
        y = y_scr[...]
        y_ref[...] = y
        z = z_ref[...]
        yz = y * (z * _sigmoid(z))
        gw = D_SSM // N_GROUPS
        for g in range(N_GROUPS):
            part = yz[:, g * gw:(g + 1) * gw]
            r = lax.rsqrt(jnp.mean(part * part, axis=-1, keepdims=True) + EPS)
            mix_ref[:, g * gw:(g + 1) * gw] = (part * r * nw_ref[:, g * gw:(g + 1) * gw]).astype(bf16)

    order = lambda i: i
    step_rows = per_step * CHUNK
    row = lambda w: pl.BlockSpec((step_rows, w), lambda i: (i, 0))
    return pl.pallas_call(
        body, name="ssd_fwd", grid=(nc // per_step,),
        in_specs=_ssd_in_specs(order, step_rows) + [_full((4, D_CONV)), _full((1, D_CONV)), _full((1, LANES)),
                                                    _full((1, LANES)), _full((1, D_SSM)), _full((1, D_SSM)),
                                                    _full((1, D_SSM))],
        out_specs=[row(D_SSM), row(D_SSM), pl.BlockSpec((per_step, D_STATE, D_SSM), lambda i: (i, 0, 0)), row(D_CONV)],
        out_shape=[SDS((s, D_SSM), bf16), SDS((s, D_SSM), f32), SDS((nc, D_STATE, D_SSM), f32),
                   SDS((s, D_CONV), f32)],
        scratch_shapes=[pltpu.VMEM((D_STATE, D_SSM), f32)] + [pltpu.VMEM((8 + CHUNK, D_CONV), f32)] * per_step
        + [pltpu.VMEM((CHUNK, D_SSM), f32)] * per_step + [pltpu.VMEM((LANES, 2 * D_SSM), bf16)],
        compiler_params=pltpu.CompilerParams(dimension_semantics=("arbitrary",)),
    )(proj, proj, proj, proj, proj, proj, conv_w, conv_b, dtb16, alog16, alog_f, d_f, nw)


def _outproj_loss(mix_a, mix_s, wo, x, tgt, npw):
    s, d = x.shape
    tm = 512

    def body(ma_ref, ms_ref, wo_ref, x_ref, t_ref, npw_ref, dmix_ref, dres_ref, acc_ref, dwo_ref):
        @pl.when(pl.program_id(0) == 0)
        def _():
            acc_ref[...] = jnp.zeros_like(acc_ref)
            dwo_ref[...] = jnp.zeros_like(dwo_ref)

        out = _nn(ma_ref[...], wo_ref[0:D_ATTN, :]) + _nn(ms_ref[...], wo_ref[D_ATTN:, :])
        r = lax.rsqrt(jnp.mean(out * out, axis=-1, keepdims=True) + EPS)
        on = out * r
        diff = x_ref[...] + on * npw_ref[...] - t_ref[...]
        dres = diff * (1.0 / d)
        dres_ref[...] = dres
        acc_ref[0:1, :] += jnp.sum(diff * diff, axis=0, keepdims=True)
        acc_ref[1:2, :] += jnp.sum(dres * on, axis=0, keepdims=True)
        dn = dres * npw_ref[...]
        dout = (r * (dn - on * jnp.mean(dn * on, axis=-1, keepdims=True))).astype(bf16)
        dmix_ref[...] = _nt(dout, wo_ref[...])
        dwo_ref[0:D_ATTN, :] += _tn(ma_ref[...], dout)
        dwo_ref[D_ATTN:, :] += _tn(ms_ref[...], dout)

    row = lambda w: pl.BlockSpec((tm, w), lambda i: (i, 0))
    return pl.pallas_call(
        body, name="outproj_loss", grid=(s // tm,),
        in_specs=[row(D_ATTN), row(D_SSM), _full((D_ATTN + D_SSM, d)), row(d), row(d), _full((1, d))],
        out_specs=[row(D_ATTN + D_SSM), row(d), _full((8, d)), _full((D_ATTN + D_SSM, d))],
        out_shape=[SDS((s, D_ATTN + D_SSM), f32), SDS((s, d), f32), SDS((8, d), f32), SDS((D_ATTN + D_SSM, d), f32)],
        compiler_params=pltpu.CompilerParams(dimension_semantics=("arbitrary",)),
    )(mix_a, mix_s, wo, x, tgt, npw)


def _attn_bwd(proj, qkv, o, lb, dmix, swap=None):
    s = proj.shape[0]
    n_it = s // BLK

    nsw = 0 if swap is None else 1

    def body(*refs):
        q_ref, k_ref, v_ref, g_ref, o_ref, l_ref, dm_ref = refs[:7]
        swap_in = refs[7:7 + nsw]
        dq_ref, dk_ref, dv_ref, dg_ref = refs[7 + nsw:11 + nsw]
        swap_out = refs[11 + nsw:11 + 2 * nsw]
        dq_acc, dk_acc, dv_acc, do_scr, dl_scr = refs[11 + 2 * nsw:16 + 2 * nsw]
        bufs = refs[16 + 2 * nsw:44 + 2 * nsw]
        stage_a, stage_b = refs[44 + 2 * nsw:46 + 2 * nsw]
        swap_sems = refs[46 + 2 * nsw:]
        head0, tri2_t, _, _, bones = _attn_consts()
        tri2_q = _quarter_mask()

        if nsw:
            x, y, c = _my_pos()
            swap_copy = pltpu.make_async_remote_copy(
                src_ref=swap_in[0], dst_ref=swap_out[0], send_sem=swap_sems[0], recv_sem=swap_sems[1],
                device_id=(x, y, 1 - c), device_id_type=MESH)

            @pl.when(pl.program_id(0) == 0)
            def _():
                swap_copy.start()

        quarter_rows, load, add = _quarter_rows, _load_runs, _add_runs

        def pro(i, carry):
            for t in range(4):
                rows = pl.ds(pl.multiple_of(i * 1024 + t * 256, 256), 256)
                g = g_ref[rows, :]
                sg = _sigmoid(g)
                dmx = dm_ref[rows, :]
                ov = o_ref[rows, :]
                dg_ref[rows, :] = (dmx * ov * (sg * (1.0 + g * (1.0 - sg)))).astype(bf16)
                do = dmx * (g * sg)
                stage_a[t * 256:(t + 1) * 256, :] = do
                stage_b[t * 256:(t + 1) * 256, :] = _split_dot_sum(do * ov, bones)
            z = jnp.zeros((256, LANES), f32)
            for q in range(4):
                rows = quarter_rows(i, q)
                do_scr[rows, :] = stage_a[pl.ds(q, 256, stride=4), :]
                dl_scr[rows, :] = stage_b[pl.ds(q, 256, stride=4), :]
                dq_acc[rows, :] = z
                dk_acc[rows, :] = z
                dv_acc[rows, :] = z
            return carry

        lax.fori_loop(0, s // 1024, pro, 0)

        def per_head(t):
            return jnp.concatenate([t[:, :LANES], t[:, LANES:]], axis=0)

        def both_heads(t):
            tr = pltpu.roll(t, HEAD_DIM, 1)
            return jnp.concatenate([jnp.where(head0, t, tr), jnp.where(head0, tr, t)], axis=1)

        mm_bufs = ((bufs[0], bufs[1], bufs[2], bufs[3]), (bufs[4], bufs[5], bufs[6], bufs[7]))
        ds_bufs = ((bufs[8], bufs[9], bufs[10], bufs[11]), (bufs[12], bufs[13], bufs[14], bufs[15]))
        op_bufs = ((bufs[16], bufs[17], bufs[18], bufs[19]), (bufs[20], bufs[21], bufs[22], bufs[23]))
        vc_bufs, carry_k, carry_v = (bufs[24], bufs[25]), bufs[26], bufs[27]
        for buf in (op_bufs[0][0], op_bufs[1][0]) + vc_bufs:
            buf[...] = jnp.zeros_like(buf)

        def block_rows(i, d, nb):
            rows, has_prev = _quarter_block(i, d, nb)
            return rows, rows, has_prev

        def unstack(st16):
            return st16[:BLK] + st16[BLK:]

        def products(i, par, d, nb):
            src, scr, has_prev = block_rows(i, d, nb)
            tri2 = tri2_q if d == 1 else tri2_t
            s_buf, dp_buf, sd_buf, dpd_buf = mm_bufs[par]
            kc_buf, kp_buf, q_buf, do_buf = op_bufs[par]
            qs = load(q_ref, src)
            do = load(do_scr, scr)
            qs16, do16 = qs.astype(bf16), do.astype(bf16)
            kst_c = _stack_heads(load(k_ref, src).astype(bf16), head0)
            vst_c = _stack_heads(load(v_ref, src).astype(bf16), head0)
            kst_p, vst_p = op_bufs[1 - par][0][...], vc_bufs[1 - par][...]
            kc_buf[...] = kst_c
            kp_buf[...] = kst_p
            vc_bufs[par][...] = vst_c
            q_buf[...] = qs16
            do_buf[...] = do16
            s_buf[...] = jnp.where(tri2, _nt(qs16, kst_c), _if_prev(has_prev, _nt(qs16, kst_p), -jnp.inf))
            dp_buf[...] = jnp.where(tri2, _nt(do16, vst_c), _if_prev(has_prev, _nt(do16, vst_p), 0.0))
            sd_buf[...] = _nn((qs * unstack(kst_p).astype(f32)).astype(bf16), bones)
            dpd_buf[...] = _if_prev(has_prev, _nn((do * unstack(vst_p).astype(f32)).astype(bf16), bones), 0.0)

        def softmax_grad(i, par, d, nb):
            src, scr, has_prev = block_rows(i, d, nb)
            s_buf, dp_buf, sd_buf, dpd_buf = mm_bufs[par]
            p_buf, ds_buf, pd_buf, dsd_buf = ds_bufs[par]
            lse = load(l_ref, src)
            dl = load(dl_scr, scr)
            pt = jnp.exp(s_buf[...] - both_heads(lse))
            ds_buf[...] = (pt * (dp_buf[...] - both_heads(dl))).astype(bf16)
            p_buf[...] = pt.astype(bf16)
            pd = _if_prev(has_prev, jnp.exp(sd_buf[...] - lse), 0.0)
            pd_buf[...] = pd
            dsd_buf[...] = pd * (dpd_buf[...] - dl)

        def accumulate(i, par, d, nb):
            _, rows, has_prev = block_rows(i, d, nb)
            _, before, _ = block_rows(max(i - 1, 0), d, nb)
            tri2 = tri2_q if d == 1 else tri2_t
            p_buf, ds_buf, pd_buf, dsd_buf = ds_bufs[par]
            kc_buf, kp_buf, q_buf, do_buf = op_bufs[par]
            pt16, ds16, pd, dsd = p_buf[...], ds_buf[...], pd_buf[...], dsd_buf[...]
            zero = jnp.zeros_like(pt16)
            dsc, dsp = jnp.where(tri2, ds16, zero), jnp.where(tri2, zero, ds16)
            pc, pp = jnp.where(tri2, pt16, zero), jnp.where(tri2, zero, pt16)
            kst_c, kst_p, q16, do16 = kc_buf[...], kp_buf[...], q_buf[...], do_buf[...]
            qst, dost = _stack_heads(q16, head0), _stack_heads(do16, head0)
            if not has_prev:
                add(dq_acc, rows, _nn(dsc, kst_c))
                add(dk_acc, before, carry_k[...])
                add(dv_acc, before, carry_v[...])
                carry_k[...] = _tn(per_head(dsc), qst)
                carry_v[...] = _tn(per_head(pc), dost)
                return
            add(dq_acc, rows, _nn(dsc, kst_c) + _nn(dsp, kst_p) + dsd * unstack(kst_p).astype(f32))
            dk2 = _tn(jnp.concatenate([per_head(dsc), per_head(dsp)], axis=1), qst)
            dv2 = _tn(jnp.concatenate([per_head(pc), per_head(pp)], axis=1), dost)
            add(dk_acc, before, carry_k[...] + dk2[BLK:] + dsd * q16.astype(f32))
            add(dv_acc, before, carry_v[...] + dv2[BLK:] + pd * do16.astype(f32))
            carry_k[...] = dk2[:BLK]
            carry_v[...] = dv2[:BLK]

        for d in DILATIONS:
            nb = s // (BLK * d)
            carry_k[...] = jnp.zeros_like(carry_k)
            carry_v[...] = jnp.zeros_like(carry_v)
            products(0, 0, d, nb)
            products(1, 1, d, nb)
            softmax_grad(0, 0, d, nb)

            for t in range(2, n_it):
                par = t % 2
                accumulate(t - 2, par, d, nb)
                products(t, par, d, nb)
                softmax_grad(t - 1, 1 - par, d, nb)
            accumulate(n_it - 2, 0, d, nb)
            softmax_grad(n_it - 1, 1, d, nb)
            accumulate(n_it - 1, 1, d, nb)
            _, last, _ = block_rows(n_it - 1, d, nb)
            add(dk_acc, last, carry_k[...])
            add(dv_acc, last, carry_v[...])

        def epi(i, carry):
            rows = pl.ds(pl.multiple_of(i * 1024, 1024), 1024)
            for acc, out, stage, scale in ((dq_acc, dq_ref, stage_a, 0.125), (dk_acc, dk_ref, stage_b, 1.0),
                                           (dv_acc, dv_ref, stage_a, 1.0)):
                for q in range(4):
                    stage[pl.ds(q, 256, stride=4), :] = acc[quarter_rows(i, q), :]
                out[rows, :] = (stage[...] if scale == 1.0 else stage[...] * scale).astype(bf16)
            return carry

        lax.fori_loop(0, s // 1024, epi, 0)

        if nsw:
            @pl.when(pl.program_id(0) == N_PAIRS - 1)
            def _():
                swap_copy.wait_send()
                swap_copy.wait_recv()

    col = lambda base: pl.BlockSpec((s, LANES), lambda h: (0, base + h))
    anyspec = pl.BlockSpec(memory_space=pl.ANY)
    swaps = [] if swap is None else [swap]
    outs = pl.pallas_call(
        body, name="attn_bwd", grid=(N_PAIRS,),
        in_specs=[col(0), col(0), col(0), col(24), col(0), col(0), col(0)] + [anyspec] * nsw,
        out_specs=[col(0)] * 4 + [anyspec] * nsw,
        out_shape=[SDS((s, D_ATTN), bf16)] * 4 + [SDS(a.shape, a.dtype) for a in swaps],
        scratch_shapes=[pltpu.VMEM((s, LANES), f32)] * 5
        + [pltpu.VMEM((BLK, 2 * LANES), f32)] * 2 + [pltpu.VMEM((BLK, LANES), f32)] * 2
        + [pltpu.VMEM((BLK, 2 * LANES), f32)] * 2 + [pltpu.VMEM((BLK, LANES), f32)] * 2
        + [pltpu.VMEM((BLK, 2 * LANES), bf16)] * 2 + [pltpu.VMEM((BLK, LANES), f32)] * 2
        + [pltpu.VMEM((BLK, 2 * LANES), bf16)] * 2 + [pltpu.VMEM((BLK, LANES), f32)] * 2
        + [pltpu.VMEM((2 * BLK, LANES), bf16)] * 2 + [pltpu.VMEM((BLK, LANES), bf16)] * 2
        + [pltpu.VMEM((2 * BLK, LANES), bf16)] * 2 + [pltpu.VMEM((BLK, LANES), bf16)] * 2
        + [pltpu.VMEM((2 * BLK, LANES), bf16)] * 2 + [pltpu.VMEM((BLK, LANES), f32)] * 2
        + [pltpu.VMEM((1024, LANES), f32)] * 2
        + [pltpu.SemaphoreType.DMA(())] * (2 * nsw),
        compiler_params=pltpu.CompilerParams(dimension_semantics=("arbitrary",)),
    )(*qkv, proj, o, lb, dmix, *swaps)
    return outs


def _ssd_bwd(proj, y, states, cv, dmix, conv_w, conv_b, dtb16, alog16, alog_f, d_f, nw, chip_sums=()):
    s = proj.shape[0]
    nc = s // CHUNK
    gw = D_SSM // N_GROUPS
    nx = len(chip_sums)

    def body(*refs):
        (xs_ref, bc_ref, _, _, dt_ref, z_ref, y_ref, st_ref, dm_ref, cw_ref, cb_ref, dtb_ref,
         alog16_ref, alogf_ref, df_ref, nw_ref, cv_ref) = refs[:17]
        cs_in = refs[17:17 + nx]
        out_ref, gconv_ref, gvec_ref, gdt_ref = refs[17 + nx:21 + nx]
        cs_out = refs[21 + nx:21 + 2 * nx]
        (dh_scr, head_scr, dcpad, da_scr, dxdt_scr, dbc_scr, emat_ref, fold_ref) = refs[21 + 2 * nx:29 + 2 * nx]
        cs_sems = refs[29 + 2 * nx:]
        i = pl.program_id(0)
        c = nc - 1 - i

        if nx:
            @pl.when(i == 0)
            def _():
                mine, sends, _ = _chip_exchange_copies(cs_in, cs_out, *cs_sems)
                for cp in mine + sends:
                    cp.start()

            @pl.when(i == nc - 1)
            def _():
                mine, sends, recvs = _chip_exchange_copies(cs_in, cs_out, *cs_sems)
                for cp in recvs:
                    cp.wait_recv()
                for cp in sends:
                    cp.wait_send()
                for cp in mine:
                    cp.wait()

        @pl.when(i == 0)
        def _():
            emat_ref[...] = _expand_mat()
            fold_ref[...] = _fold_mat()
            dh_scr[...] = jnp.zeros_like(dh_scr)
            head_scr[...] = jnp.zeros_like(head_scr)
            gconv_ref[...] = jnp.zeros_like(gconv_ref)
            gvec_ref[...] = jnp.zeros_like(gvec_ref)
            gdt_ref[...] = jnp.zeros_like(gdt_ref)

        cv = cv_ref[...]
        sig = _sigmoid(cv)
        xbc = cv * sig
        pre, dt_f, al_f, al_x, al_t = _decay_terms(dt_ref, dtb_ref, alog16_ref, emat_ref)
        head0 = _iota((CHUNK, LANES), 1) < HEAD_DIM
        sub = _iota((CHUNK, LANES), 0)
        last_row = sub == CHUNK - 1

        yv, z, dmx = y_ref[...], z_ref[...], dm_ref[...]
        sz = _sigmoid(z)
        silu = z * sz
        yz = yv * silu
        dyz_parts = []
        for g in range(N_GROUPS):
            gs = slice(g * gw, (g + 1) * gw)
            part = yz[:, gs]
            r = lax.rsqrt(jnp.mean(part * part, axis=-1, keepdims=True) + EPS)
            nh = part * r
            gvec_ref[0:1, gs] += jnp.sum(dmx[:, gs] * nh, axis=0, keepdims=True)
            dn = dmx[:, gs] * nw_ref[:, gs]
            dyz_parts.append(r * (dn - nh * jnp.mean(dn * nh, axis=-1, keepdims=True)))
        dyz = jnp.concatenate(dyz_parts, axis=1)
        dy = dyz * silu
        out_ref[:, 0:D_SSM] = (dyz * yv * (sz * (1.0 + z * (1.0 - sz)))).astype(bf16)

        x_all = xbc[:, 0:D_SSM]
        gvec_ref[2:3, :] += jnp.sum(dy * x_all, axis=0, keepdims=True)

        for g in range(N_GROUPS):
            bm = xbc[:, D_SSM + g * D_STATE: D_SSM + (g + 1) * D_STATE].astype(bf16)
            cm = xbc[:, D_SSM + (N_GROUPS + g) * D_STATE: D_SSM + (N_GROUPS + g + 1) * D_STATE].astype(bf16)
            gmat = _nt(cm, bm)
            dgm = jnp.zeros((CHUNK, CHUNK), f32)
            db = jnp.zeros((CHUNK, D_STATE), f32)
            dc = jnp.zeros((CHUNK, D_STATE), f32)
            for pair in range(4 * g, 4 * g + 4):
                sl = slice(pair * LANES, (pair + 1) * LANES)
                xp, dtp, alp, dyp = x_all[:, sl], dt_f[:, sl], al_f[:, sl], dy[:, sl]
                xdt = xp * dtp
                xdt16 = xdt.astype(bf16)
                al_last = alp[CHUNK - 1:CHUNK, :]
                e_l = jnp.exp(alp)
                wf = jnp.exp(al_last - alp)
                e_last = jnp.exp(al_last)
                hp = st_ref[:, sl]
                hp16 = hp.astype(bf16)
                dhn = dh_scr[:, sl]
                dhn16 = dhn.astype(bf16)
                y_off = e_l * _nn(cm, hp16)
                dch16 = (dyp * e_l).astype(bf16)
                dc = dc + _nt(dch16, hp16)
                dh_out = _tn(cm, dch16)
                dal = dyp * y_off
                xw16 = (wf * xdt).astype(bf16)
                db = db + _nt(xw16, dhn16)
                dxw = _nn(bm, dhn16)
                dxdt = dxw * wf
                dwf = dxw * xdt * wf
                dal = dal - dwf
                dal_last = jnp.sum(dwf, axis=0, keepdims=True) + jnp.sum(dhn * hp, axis=0, keepdims=True) * e_last
                dh_scr[:, sl] = e_last * dhn + dh_out
                for h in range(2):
                    mh = head0 if h == 0 else jnp.logical_not(head0)
                    dyh16 = jnp.where(mh, dyp, 0.0).astype(bf16)
                    lmat = _decay_mat(al_x, al_t, pair, h)
                    mm = gmat * lmat
                    dmm = _nt(dyh16, xdt16)
                    dxdt = dxdt + _tn(mm.astype(bf16), dyh16)
                    n16 = (dmm * mm).astype(bf16)
                    jh = jnp.where(mh, 1.0 / HEAD_DIM, 0.0).astype(bf16)
                    dal = dal + _nn(n16, jh) - _tn(n16, jh)
                    dgm = dgm + dmm * lmat
                da_scr[:, sl] = dal + jnp.where(last_row, dal_last, 0.0)
                dxdt_scr[:, sl] = dxdt
            dgm16 = dgm.astype(bf16)
            dbc_scr[:, g * D_STATE:(g + 1) * D_STATE] = db + _tn(dgm16, cm)
            dbc_scr[:, (N_GROUPS + g) * D_STATE:(N_GROUPS + g + 1) * D_STATE] = dc + _nn(dgm16, bm)

        sub_c, lane_c = _iota((CHUNK, CHUNK), 0), _iota((CHUNK, CHUNK), 1)
        tri_t = (lane_c >= sub_c).astype(bf16)
        dadt = _dot_01_left(tri_t, da_scr[...], 2)
        a_f = -jnp.exp(alogf_ref[...])
        dxdt_all = dxdt_scr[...]
        ddt_f = dxdt_all * x_all + a_f * dadt
        gvec_ref[1:2, :] += jnp.sum(dt_f * dadt, axis=0, keepdims=True) * a_f
        dx = df_ref[...] * dy + dxdt_all * dt_f
        ddt_raw = _dot_01(ddt_f, fold_ref[...], 2) * _sigmoid(pre)
        gdt_ref[0:1, :] += jnp.sum(ddt_raw, axis=0, keepdims=True)
        out_ref[:, D_SSM + D_CONV:D_SSM + D_CONV + LANES] = ddt_raw.astype(bf16)
        out_ref[:, D_SSM + D_CONV + LANES:] = jnp.zeros((CHUNK, 3 * LANES), bf16)

        dsil = sig * (1.0 + cv * (1.0 - sig))
        dcv_x = dx * dsil[:, 0:D_SSM]
        dcv_bc = dbc_scr[...] * dsil[:, D_SSM:]
        dcpad[0:CHUNK, 0:D_SSM] = dcv_x
        dcpad[0:CHUNK, D_SSM:] = dcv_bc
        dcpad[CHUNK:, :] = head_scr[...]
        dcp = dcpad[...]
        dcv = dcp[0:CHUNK]
        gconv_ref[4:5, :] += jnp.sum(dcv, axis=0, keepdims=True)
        x_raw = jnp.concatenate([xs_ref[...], bc_ref[...]], axis=1)
        draw = cw_ref[3:4, :] * dcv
        gconv_ref[3:4, :] += jnp.sum(dcv * x_raw, axis=0, keepdims=True)
        for j in range(3):
            ahead = pltpu.roll(dcp, CHUNK + 8 - (3 - j), 0)[0:CHUNK]
            draw = draw + cw_ref[j:j + 1, :] * ahead
            gconv_ref[j:j + 1, :] += jnp.sum(ahead * x_raw, axis=0, keepdims=True)
        head_scr[...] = dcv[0:8]
        out_ref[:, D_SSM:D_SSM + D_CONV] = draw.astype(bf16)

    order = lambda i: nc - 1 - i
    row = lambda w, cb=0: pl.BlockSpec((CHUNK, w), lambda i: (nc - 1 - i, cb))
    anyspec = pl.BlockSpec(memory_space=pl.ANY)
    outs = pl.pallas_call(
        body, name="ssd_bwd", grid=(nc,),
        in_specs=_ssd_in_specs(order) + [row(D_SSM), pl.BlockSpec((None, D_STATE, D_SSM), lambda i: (nc - 1 - i, 0, 0)),
                                         row(D_SSM, 1), _full((4, D_CONV)), _full((1, D_CONV)), _full((1, LANES)),
                                         _full((1, LANES)), _full((1, D_SSM)), _full((1, D_SSM)), _full((1, D_SSM)),
                                         row(D_CONV)]
        + [anyspec] * nx,
        out_specs=[row(3072), _full((8, D_CONV)), _full((8, D_SSM)), _full((8, LANES))] + [anyspec] * nx,
        out_shape=[SDS((s, 3072), bf16), SDS((8, D_CONV), f32), SDS((8, D_SSM), f32), SDS((8, LANES), f32)]
        + [SDS(a.shape, a.dtype) for a in chip_sums],
        scratch_shapes=[pltpu.VMEM((D_STATE, D_SSM), f32), pltpu.VMEM((8, D_CONV), f32),
                        pltpu.VMEM((8 + CHUNK, D_CONV), f32),
                        pltpu.VMEM((CHUNK, D_SSM), f32), pltpu.VMEM((CHUNK, D_SSM), f32),
                        pltpu.VMEM((CHUNK, 2 * N_GROUPS * D_STATE), f32),
                        pltpu.VMEM((LANES, 2 * D_SSM), bf16), pltpu.VMEM((D_SSM, LANES), bf16)]
        + (_chip_exchange_scratch(nx) if nx else []),
        compiler_params=pltpu.CompilerParams(dimension_semantics=("arbitrary",)),
    )(proj, proj, proj, proj, proj, proj, y, states, dmix, conv_w, conv_b, dtb16, alog16, alog_f, d_f, nw, cv,
      *chip_sums)
    return outs[0], outs[1], outs[2], outs[3], outs[4:]


def _col_blocks(parts, tile):
    counts = [p.shape[1] // tile for p in parts]
    offs = [sum(counts[:t]) for t in range(len(parts))]
    return offs, counts, sum(counts)


def _bcast_copies(src_ref, out_ref, send_sems, recv_sems, local_sem):
    x, y, c = _my_pos()
    me = 4 * x + 2 * y + c
    mine = pltpu.make_async_copy(src_ref, out_ref.at[me], local_sem)
    sends, recvs = [], []
    for k in range(1, N_DEV):
        to, frm = (me + k) % N_DEV, (me + N_DEV - k) % N_DEV
        sems = dict(send_sem=send_sems.at[k - 1], recv_sem=recv_sems.at[k - 1], device_id_type=MESH)
        sends.append(pltpu.make_async_remote_copy(
            src_ref=src_ref, dst_ref=out_ref.at[me], device_id=(to // 4, (to // 2) % 2, to % 2), **sems))
        recvs.append(pltpu.make_async_remote_copy(
            src_ref=src_ref, dst_ref=out_ref.at[frm], device_id=(x, y, c), **sems))
    return mine, sends, recvs


def _bcast_scratch():
    return [pltpu.SemaphoreType.DMA((N_DEV - 1,)), pltpu.SemaphoreType.DMA((N_DEV - 1,)), pltpu.SemaphoreType.DMA(())]


def _inproj_bwd(dparts, wt, x, nw, dres, chip_sums=(), pack=None):
    s, d = x.shape
    tm, tk = 1024, 1024
    offs, counts, nk = _col_blocks(dparts, tk)
    npart, nx = len(dparts), len(chip_sums)
    npk = 0 if pack is None else 1
    ni = s // tm

    def body(*refs):
        dp_refs = refs[:npart]
        w_ref, x_ref, nw_ref, dres_ref = refs[npart:npart + 4]
        pos = npart + 4
        cs_in, pos = refs[pos:pos + nx], pos + nx
        pack_in, pos = refs[pos:pos + npk], pos + npk
        (gx_ref, gnw_ref), pos = refs[pos:pos + 2], pos + 2
        cs_out, pos = refs[pos:pos + nx], pos + nx
        pack_out, pos = refs[pos:pos + 2 * npk], pos + 2 * npk
        acc, pos = refs[pos], pos + 1
        cs_sems, pos = refs[pos:pos + 3 * min(nx, 1)], pos + 3 * min(nx, 1)
        pk_refs = refs[pos:]
        i, k = pl.program_id(0), pl.program_id(1)

        def exchange():
            return _chip_exchange_copies(cs_in, cs_out, *cs_sems)

        def pack_copies():
            return _bcast_copies(pack_in[0], pack_out[0], *pk_refs[1:4])

        def gnw_copies():
            return _bcast_copies(pk_refs[0], pack_out[1], *pk_refs[4:7])

        @pl.when(jnp.logical_and(i == 0, k == 0))
        def _():
            gnw_ref[...] = jnp.zeros_like(gnw_ref)
            if nx:
                mine, sends, _ = exchange()
                for cp in mine + sends:
                    cp.start()
            if npk:
                mine, sends, _ = pack_copies()
                for cp in [mine] + sends:
                    cp.start()

        @pl.when(k == 0)
        def _():
            acc[...] = _nn(dp_refs[0][...], w_ref[...])

        for t in range(npart):
            @pl.when(jnp.logical_and(k >= max(offs[t], 1), k < offs[t] + counts[t]))
            def _(t=t):
                acc[...] += _nn(dp_refs[t][...], w_ref[...])

        @pl.when(k == nk - 1)
        def _():
            xv = x_ref[...]
            r = lax.rsqrt(jnp.mean(xv * xv, axis=-1, keepdims=True) + EPS)
            xn = xv * r
            du = acc[...]
            gnw_ref[0:1, :] += jnp.sum(du * xn, axis=0, keepdims=True)
            dn = du * nw_ref[...]
            gx_ref[...] = dres_ref[...] + r * (dn - xn * jnp.mean(dn * xn, axis=-1, keepdims=True))

        @pl.when(jnp.logical_and(i == ni - 1, k == nk - 1))
        def _():
            if npk:
                pk_refs[0][...] = gnw_ref[...]
                mine, sends, _ = gnw_copies()
                for cp in [mine] + sends:
                    cp.start()
            if nx:
                mine, sends, recvs = exchange()
                for cp in recvs:
                    cp.wait_recv()
                for cp in sends:
                    cp.wait_send()
                for cp in mine:
                    cp.wait()
            if npk:
                for copies in (pack_copies(), gnw_copies()):
                    mine, sends, recvs = copies
                    for cp in recvs:
                        cp.wait_recv()
                    for cp in sends:
                        cp.wait_send()
                    mine.wait()

    def piece(t):
        return pl.BlockSpec((tm, tk), lambda i, k: (i, jnp.clip(k - offs[t], 0, counts[t] - 1)))

    anyspec = pl.BlockSpec(memory_space=pl.ANY)
    packs = [] if pack is None else [pack]
    pack_shapes = [] if pack is None else [SDS((N_DEV,) + pack.shape, f32), SDS((N_DEV, 8, d), f32)]
    scratch = [pltpu.VMEM((tm, d), f32)] + (_chip_exchange_scratch(nx) if nx else [])
    if npk:
        scratch += [pltpu.VMEM((8, d), f32)] + _bcast_scratch() + _bcast_scratch()
    outs = pl.pallas_call(
        body, name="inproj_bwd", grid=(ni, nk),
        in_specs=[piece(t) for t in range(npart)] + [
            pl.BlockSpec((tk, d), lambda i, k: (k, 0)),
            pl.BlockSpec((tm, d), lambda i, k: (i, 0)), pl.BlockSpec((1, d), lambda i, k: (0, 0)),
            pl.BlockSpec((tm, d), lambda i, k: (i, 0))] + [anyspec] * (nx + npk),
        out_specs=[pl.BlockSpec((tm, d), lambda i, k: (i, 0)), pl.BlockSpec((8, d), lambda i, k: (0, 0))]
        + [anyspec] * (nx + 2 * npk),
        out_shape=[SDS((s, d), f32), SDS((8, d), f32)] + [SDS(a.shape, a.dtype) for a in chip_sums] + pack_shapes,
        scratch_shapes=scratch,
        compiler_params=pltpu.CompilerParams(dimension_semantics=("arbitrary", "arbitrary")),
    )(*dparts, wt, x, nw, dres, *chip_sums, *packs)
    return outs[0], outs[1], outs[2:2 + nx], outs[2 + nx:]


def _matmul_tn(a_parts, b_parts, name):
    tile, tk = 1024, 1024
    s = a_parts[0].shape[0]
    nk = s // tk
    na, nb = len(a_parts), len(b_parts)
    offs_a, counts_a, ni = _col_blocks(a_parts, tile)
    offs_b, counts_b, nj = _col_blocks(b_parts, tile)

    def body(*refs):
        a_refs, b_refs, o_ref = refs[:na], refs[na:na + nb], refs[na + nb]
        i, j = pl.program_id(0), pl.program_id(1)

        @pl.when(pl.program_id(2) == 0)
        def _():
            o_ref[...] = jnp.zeros_like(o_ref)

        for ta in range(na):
            for tb in range(nb):
                in_a = jnp.logical_and(i >= offs_a[ta], i < offs_a[ta] + counts_a[ta])
                in_b = jnp.logical_and(j >= offs_b[tb], j < offs_b[tb] + counts_b[tb])

                @pl.when(jnp.logical_and(in_a, in_b))
                def _(ta=ta, tb=tb):
                    o_ref[...] += _tn(a_refs[ta][...], b_refs[tb][...])

    def spec(offs, counts, t, axis):
        def index(i, j, k):
            pos = (i, j)[axis]
            mine = jnp.logical_and(pos >= offs[t], pos < offs[t] + counts[t])
            return jnp.where(mine, k, 0), jnp.clip(pos - offs[t], 0, counts[t] - 1)
        return pl.BlockSpec((tk, tile), index)

    return pl.pallas_call(
        body, name=name, grid=(ni, nj, nk),
        in_specs=[spec(offs_a, counts_a, t, 0) for t in range(na)] + [spec(offs_b, counts_b, t, 1) for t in range(nb)],
        out_specs=pl.BlockSpec((tile, tile), lambda i, j, k: (i, j)),
        out_shape=SDS((ni * tile, nj * tile), f32),
        compiler_params=pltpu.CompilerParams(dimension_semantics=("parallel", "parallel", "arbitrary")),
    )(*a_parts, *b_parts)


def _adamw(w, g, m, v):
    m = ADAM_B1 * m + (1.0 - ADAM_B1) * g
    v = ADAM_B2 * v + (1.0 - ADAM_B2) * (g * g)
    m_hat = m / (1.0 - ADAM_B1 ** ADAM_STEP)
    v_hat = v / (1.0 - ADAM_B2 ** ADAM_STEP)
    delta = -ADAM_LR * (m_hat / (jnp.sqrt(v_hat) + ADAM_EPS) + ADAM_WD * w)
    return delta, m, v


def _sum_adamw(parts, w, m, v, name):
    r, c = w.shape
    tc = 256

    def body(p_ref, w_ref, m_ref, v_ref, g_ref, d_ref, nm_ref, nv_ref):
        g = p_ref[0].astype(f32)
        for q in range(1, 4):
            g = g + p_ref[q].astype(f32)
        g_ref[...] = g
        d_ref[...], nm_ref[...], nv_ref[...] = _adamw(w_ref[...], g, m_ref[...], v_ref[...])

    blk = pl.BlockSpec((r, tc), lambda i: (0, i))
    return pl.pallas_call(
        body, name=name, grid=(c // tc,),
        in_specs=[pl.BlockSpec((4, r, tc), lambda i: (0, 0, i)), blk, blk, blk],
        out_specs=[blk] * 4, out_shape=[SDS((r, c), f32)] * 4,
        compiler_params=pltpu.CompilerParams(dimension_semantics=("parallel",)),
    )(parts, w, m, v)


def _sum_small(parts, pre_blocks):
    def body(p_ref, b_ref, o_ref):
        t = p_ref[0]
        pre = b_ref[0]
        for j in range(1, N_DEV):
            t = t + p_ref[j]
            pre = pre + b_ref[j]
        o_ref[...] = t
        o_ref[5:6, 0:D_MODEL] = pre[0:1, :]
        row_h = _iota((D_SSM, LANES), 0) // HEAD_DIM
        fold = (row_h == _iota((D_SSM, LANES), 1)).astype(f32)
        lower = t[8:16, 0:LANES]
        folded = _nn_hi(t[8:16, 0:D_SSM], fold)
        loss = jnp.sum(t[11:12, 0:D_MODEL], axis=1, keepdims=True) * (0.5 / D_MODEL)
        row = _iota((8, LANES), 0)
        o_ref[8:16, 0:LANES] = jnp.where(row < 2, folded, jnp.where(row == 4, loss, lower))

    return pl.pallas_call(body, name="sum_small", out_shape=SDS((PACK_ROWS, PACK_W), f32),
                          in_specs=[pl.BlockSpec(memory_space=pltpu.VMEM)] * 2,
                          out_specs=pl.BlockSpec(memory_space=pltpu.VMEM))(parts, pre_blocks)


def _adamw_small(w, g, m, v):
    def body(w_ref, g_ref, m_ref, v_ref, d_ref, nm_ref, nv_ref):
        d_ref[...], nm_ref[...], nv_ref[...] = _adamw(w_ref[...], g_ref[...], m_ref[...], v_ref[...])

    vm = pl.BlockSpec(memory_space=pltpu.VMEM)
    return pl.pallas_call(body, name="adamw_small", out_shape=[SDS(w.shape, f32)] * 3,
                          in_specs=[vm] * 4, out_specs=[vm] * 3)(w, g, m, v)


def _pad_lanes(v, width):
    return jnp.pad(v, ((0, 0), (0, width - v.shape[1])))


def _local_step(x, tgt, norm_pre_w, wt, conv_w, conv_b, dt_bias, a_log, d_skip, ssm_norm_w, wo, norm_post_w, sharded):
    dtb16 = _pad_lanes(dt_bias, LANES)
    alog16 = _pad_lanes(a_log, LANES)
    alog_f = jnp.repeat(a_log, HEAD_DIM, axis=1)
    d_f = jnp.repeat(d_skip, HEAD_DIM, axis=1)

    shard_out = wo.shape[0]
    if sharded:
        proj, u, (g_out, g_cw) = _prenorm_inproj(x, norm_pre_w, wt, gather=(wo, conv_w))
        wo = g_out.reshape(N_DEV * shard_out, D_MODEL)
        conv_w = g_cw.transpose(1, 0, 2).reshape(4, D_CONV)
    else:
        proj, u, _ = _prenorm_inproj(x, norm_pre_w, wt)
    o, lb, mix_a, *qkv = _attn_fwd(proj)
    mix_s, y, states, cv = _ssd_fwd(proj, conv_w, conv_b, dtb16, alog16, alog_f, d_f, ssm_norm_w)
    dmix, dres, acc_post, dw_out = _outproj_loss(mix_a, mix_s, wo, x, tgt, norm_post_w)
    ssd_args = (proj, y, states, cv, dmix, conv_w, conv_b, dtb16, alog16, alog_f, d_f, ssm_norm_w)
    if sharded:
        dq, dk, dv, dg, got_out = _attn_bwd(proj, qkv, o, lb, dmix, swap=dw_out)
        chip_out = _chip_sum(dw_out, got_out, shard_out, "chip_sum_w_out")
        dzxd, g_conv, g_vec, g_dt, (parts_out,) = _ssd_bwd(*ssd_args, chip_sums=[chip_out])
    else:
        dq, dk, dv, dg = _attn_bwd(proj, qkv, o, lb, dmix)
        dzxd, g_conv, g_vec, g_dt, _ = _ssd_bwd(*ssd_args)
    dparts = [dq, dk, dv, dg, dzxd]

    def pack(g_pre_row):
        return jnp.concatenate(
            [g_conv[0:5], g_pre_row, _pad_lanes(g_vec[0:1], PACK_W), _pad_lanes(acc_post[1:2], PACK_W),
             _pad_lanes(g_vec[1:3], PACK_W), _pad_lanes(g_dt[0:1], PACK_W), _pad_lanes(acc_post[0:1], PACK_W),
             jnp.zeros((4, PACK_W), f32)], axis=0)

    if sharded:
        dw_in, got_in = _dw_in_swap(dparts, u)
        chip_in = _chip_sum(dw_in, got_in, D_IN_PROJ // N_DEV, "chip_sum_w_in")
        grad_x, _, (parts_in,), small = _inproj_bwd(dparts, wt, x, norm_pre_w, dres, [chip_in],
                                                    pack(jnp.zeros((1, PACK_W), f32)))
        return grad_x, (parts_in, parts_out), small
    dw_in = _matmul_tn(dparts, [u], "dw_in")
    grad_x, g_pre, _, _ = _inproj_bwd(dparts, wt, x, norm_pre_w, dres)
    return grad_x, (dw_in, dw_out), pack(_pad_lanes(g_pre[0:1], PACK_W))


def kernel(x, norm_pre_w, w_in, conv_w, conv_b, dt_bias, a_log, d_skip, ssm_norm_w, w_out, norm_post_w, loss_target, m_norm_pre_w, m_w_in, m_conv_w, m_conv_b, m_dt_bias, m_a_log, m_d_skip, m_ssm_norm_w, m_w_out, m_norm_post_w, v_norm_pre_w, v_w_in, v_conv_w, v_conv_b, v_dt_bias, v_a_log, v_d_skip, v_ssm_norm_w, v_w_out, v_norm_post_w):
    shard_cv = conv_w.shape[2]
    me = 4 * lax.axis_index("x") + 2 * lax.axis_index("y") + lax.axis_index("c")

    g_in, = _all_gather([w_in[0].T.astype(bf16)])
    wt = _assemble_wt(g_in)

    grad_x, (parts_in, parts_out), (parts_small, pre_blocks) = _local_step(
        x[0], loss_target[0], norm_pre_w, wt, conv_w[0], conv_b, dt_bias, a_log, d_skip, ssm_norm_w,
        w_out[0].astype(bf16), norm_post_w, sharded=True)

    g_w_in, d_w_in, nm_w_in, nv_w_in = (a.T for a in _sum_adamw(
        parts_in, w_in[0].T, m_w_in[0].T, v_w_in[0].T, "sum_adamw_w_in"))
    g_w_out, d_w_out, nm_w_out, nv_w_out = _sum_adamw(parts_out, w_out[0], m_w_out[0], v_w_out[0], "sum_adamw_w_out")
    tot = _sum_small(parts_small, pre_blocks)

    g_cw_all = tot[0:4]
    small_g = {
        "conv_w": lax.dynamic_slice(g_cw_all, (0, me * shard_cv), (4, shard_cv)),
        "conv_b": tot[4:5], "norm_pre_w": tot[5:6, :D_MODEL], "ssm_norm_w": tot[6:7, :D_SSM],
        "norm_post_w": tot[7:8, :D_MODEL], "a_log": tot[8:9, :16], "d_skip": tot[9:10, :16], "dt_bias": tot[10:11, :16],
    }
    loss = tot[12, 0]
    small_w = {"conv_w": (conv_w[0], m_conv_w[0], v_conv_w[0]), "conv_b": (conv_b, m_conv_b, v_conv_b),
               "norm_pre_w": (norm_pre_w, m_norm_pre_w, v_norm_pre_w), "ssm_norm_w": (ssm_norm_w, m_ssm_norm_w, v_ssm_norm_w),
               "norm_post_w": (norm_post_w, m_norm_post_w, v_norm_post_w), "a_log": (a_log, m_a_log, v_a_log),
               "d_skip": (d_skip, m_d_skip, v_d_skip), "dt_bias": (dt_bias, m_dt_bias, v_dt_bias)}
    names = list(small_w)
    sizes = [small_g[k].size for k in names]
    tot_size = sum(sizes)
    pad_to = -(-tot_size // 1024) * 1024

    def flat(arrs):
        v = jnp.concatenate([a.reshape(-1) for a in arrs])
        return jnp.pad(v, (0, pad_to - tot_size)).reshape(pad_to // LANES, LANES)

    fw = flat([small_w[k][0] for k in names])
    fg = flat([small_g[k] for k in names])
    fm = flat([small_w[k][1] for k in names])
    fv = jnp.pad(jnp.concatenate([small_w[k][2].reshape(-1) for k in names]), (0, pad_to - tot_size),
                 constant_values=1.0).reshape(pad_to // LANES, LANES)
    fd, fnm, fnv = _adamw_small(fw, fg, fm, fv)

    def unflat(f):
        out, off = {}, 0
        v = f.reshape(-1)
        for k, n in zip(names, sizes):
            out[k] = v[off:off + n].reshape(small_g[k].shape)
            off += n
        return out

    sd, snm, snv = unflat(fd), unflat(fnm), unflat(fnv)
    lead = lambda a: a[None]
    order = ["norm_pre_w", "w_in", "conv_w", "conv_b", "dt_bias", "a_log", "d_skip", "ssm_norm_w", "w_out", "norm_post_w"]
    grads = dict(small_g, w_in=g_w_in, w_out=g_w_out)
    deltas = dict(sd, w_in=d_w_in, w_out=d_w_out)
    new_m = dict(snm, w_in=nm_w_in, w_out=nm_w_out)
    new_v = dict(snv, w_in=nv_w_in, w_out=nv_w_out)

    def shaped(dct, k):
        a = dct[k]
        return lead(a) if k in ("w_in", "w_out", "conv_w") else a

    return (loss, grad_x[None], *[shaped(grads, k) for k in order], *[shaped(deltas, k) for k in order],
            *[shaped(new_m, k) for k in order], *[shaped(new_v, k) for k in order])
```
